```python
import math
import jax, jax.numpy as jnp
from jax import lax
import numpy as np

D_MODEL = 2048
BATCH = 8
SEQ = 4096
DEPTH = 2

HEAD_DIM = 128
WIDTH_A = D_MODEL // 2
WIDTH_B = D_MODEL // 2
N_HEADS_A = WIDTH_A // HEAD_DIM
N_HEADS_B = WIDTH_B // HEAD_DIM
CHUNK = 128
Q_BLOCK = 128
SSM_WIDTH = D_MODEL // 2
SSM_GROUP = 16
SSM_GROUPS = SSM_WIDTH // SSM_GROUP
SSM_STATE = 64
EPS = 1e-6
DT_MIN = 1e-3
DT_MAX = 1e-1

kernel_name = "hybrid_sgu_stickbreak_s5_adaln"


def rms_norm(x, g):
    xf = x.astype(jnp.float32)
    y = xf * lax.rsqrt(jnp.mean(xf * xf, axis=-1, keepdims=True) + EPS)
    return (y * g.astype(jnp.float32)).astype(x.dtype)


def spatial_gating(u, v, norm_g, w_s, b_s):
    bsz, l, _ = v.shape
    n_chunks = l // CHUNK
    vh = v.reshape(bsz, n_chunks, CHUNK, N_HEADS_A, HEAD_DIM)
    vh = rms_norm(vh, norm_g.reshape(N_HEADS_A, HEAD_DIM))
    causal = jnp.tril(jnp.ones((CHUNK, CHUNK), dtype=bool))
    w = jnp.where(causal[None], w_s, 0.0).astype(vh.dtype)
    s = jnp.einsum('hts,bnshd->bnthd', w, vh) + b_s.T.astype(vh.dtype)[None, None, :, :, None]
    return u * s.reshape(bsz, l, WIDTH_A)


def stick_breaking(q, k, v):
    bsz, l, h, dh = q.shape
    n_blocks = l // Q_BLOCK
    qb = q.reshape(bsz, n_blocks, Q_BLOCK, h, dh).transpose(1, 0, 3, 2, 4)
    kt = k.transpose(0, 2, 1, 3)
    vt = v.transpose(0, 2, 1, 3)
    scale = 1.0 / math.sqrt(dh)
    k_pos = jnp.arange(l)

    def block(args):
        q_blk, blk = args
        q_pos = blk * Q_BLOCK + jnp.arange(Q_BLOCK)
        mask = k_pos[None, :] < q_pos[:, None]
        z = jnp.einsum('bhqd,bhkd->bhqk', q_blk, kt).astype(jnp.float32) * scale
        log_beta = jax.nn.log_sigmoid(z)
        log_keep = jnp.where(mask, jax.nn.log_sigmoid(-z), 0.0)
        later = lax.cumsum(log_keep, axis=3, reverse=True) - log_keep
        w = jnp.where(mask, jnp.exp(log_beta + later), 0.0)
        return jnp.einsum('bhqk,bhkd->bhqd', w.astype(vt.dtype), vt)

    out = lax.map(block, (qb, jnp.arange(n_blocks)))
    return out.transpose(1, 0, 3, 2, 4).reshape(bsz, l, h * dh)


def s5_ssm(u, lam_re, lam_im, b_re, b_im, c_re, c_im, d_skip, log_dt):
    f32 = jnp.float32
    bsz, l, _ = u.shape
    uf = u.astype(f32).reshape(bsz, l, SSM_GROUPS, SSM_GROUP)
    dt = jnp.exp(log_dt.astype(f32))[:, None]
    lr = lam_re.astype(f32)
    li = lam_im.astype(f32)
    mag = jnp.exp(lr * dt)
    a_re = mag * jnp.cos(li * dt)
    a_im = mag * jnp.sin(li * dt)
    den = lr * lr + li * li
    nr = a_re - 1.0
    coef_re = (nr * lr + a_im * li) / den
    coef_im = (a_im * lr - nr * li) / den
    br = b_re.astype(f32)
    bi = b_im.astype(f32)
    bb_re = coef_re[..., None] * br - coef_im[..., None] * bi
    bb_im = coef_re[..., None] * bi + coef_im[..., None] * br
    bu_re = jnp.einsum('gpc,blgc->blgp', bb_re, uf)
    bu_im = jnp.einsum('gpc,blgc->blgp', bb_im, uf)
    a_re_t = jnp.broadcast_to(a_re, (1, l) + a_re.shape)
    a_im_t = jnp.broadcast_to(a_im, (1, l) + a_im.shape)

    def combine(e1, e2):
        a1r, a1i, b1r, b1i = e1
        a2r, a2i, b2r, b2i = e2
        return (a2r * a1r - a2i * a1i,
                a2r * a1i + a2i * a1r,
                a2r * b1r - a2i * b1i + b2r,
                a2r * b1i + a2i * b1r + b2i)

    _, _, h_re, h_im = lax.associative_scan(combine, (a_re_t, a_im_t, bu_re, bu_im), axis=1)
    y = (jnp.einsum('gcp,blgp->blgc', c_re.astype(f32), h_re)
         - jnp.einsum('gcp,blgp->blgc', c_im.astype(f32), h_im))
    y = y.reshape(bsz, l, SSM_WIDTH) + d_skip.astype(f32) * u.astype(f32)
    return y.astype(u.dtype)


def ab_mixer(h, w_in, w_out, sgu_norm_g, sgu_w, sgu_b):
    bsz, l, _ = h.shape
    proj = h @ w_in
    cuts = np.cumsum([WIDTH_A, WIDTH_A, WIDTH_A, WIDTH_B, WIDTH_B, WIDTH_B]).tolist()
    a_u, a_v, a_z, q, k, v, b_z = jnp.split(proj, cuts, axis=-1)
    out_a = spatial_gating(jax.nn.gelu(a_u), jax.nn.gelu(a_v), sgu_norm_g, sgu_w, sgu_b)
    out_a = out_a * jax.nn.silu(a_z)
    shp = (bsz, l, N_HEADS_B, HEAD_DIM)
    out_b = stick_breaking(q.reshape(shp), k.reshape(shp), v.reshape(shp)) * jax.nn.silu(b_z)
    return jnp.concatenate([out_a, out_b], axis=-1) @ w_out


def ssm_mixer(h, w_in, w_out, lam_re, lam_im, b_re, b_im, c_re, c_im, d_skip, log_dt, w_glu, b_glu):
    proj = h @ w_in
    u, z = jnp.split(proj, 2, axis=-1)
    y = s5_ssm(u, lam_re, lam_im, b_re, b_im, c_re, c_im, d_skip, log_dt)
    g = jax.nn.gelu(y)
    y = g * jax.nn.sigmoid(g @ w_glu + b_glu)
    return (y * jax.nn.silu(z)) @ w_out


def _fwd_setup_inputs(seed: int = 0) -> dict:
    key = jax.random.key(seed)
    ks = jax.random.split(key, 24)
    n_even = (DEPTH + 1) // 2
    n_odd = DEPTH // 2
    d = D_MODEL
    nrm = jax.random.normal
    w_in_ab_cols = 3 * WIDTH_A + 4 * WIDTH_B
    log_dt = jax.random.uniform(ks[20], (n_odd, SSM_GROUPS), minval=math.log(DT_MIN), maxval=math.log(DT_MAX))
    n_idx = jnp.arange(SSM_STATE, dtype=jnp.float32)
    return {
        "x": nrm(ks[0], (BATCH, SEQ, d)),
        "c": nrm(ks[1], (BATCH, d)),
        "ln_pre_g": 1.0 + 0.02 * nrm(ks[2], (DEPTH, d)),
        "ln_post_g": 1.0 + 0.02 * nrm(ks[3], (DEPTH, d)),
        "w_mod": nrm(ks[4], (DEPTH, d, 3 * d)) * d ** -0.5,
        "b_mod": 0.02 * nrm(ks[5], (DEPTH, 3 * d)),
        "w_in_ab": nrm(ks[6], (n_even, d, w_in_ab_cols)) * d ** -0.5,
        "w_out_ab": nrm(ks[7], (n_even, WIDTH_A + WIDTH_B, d)) * (WIDTH_A + WIDTH_B) ** -0.5,
        "sgu_norm_g": 1.0 + 0.02 * nrm(ks[8], (n_even, WIDTH_A)),
        "sgu_w": nrm(ks[9], (n_even, N_HEADS_A, CHUNK, CHUNK)) * CHUNK ** -0.5,
        "sgu_b": 1.0 + 0.02 * nrm(ks[10], (n_even, N_HEADS_A, CHUNK)),
        "w_in_ssm": nrm(ks[11], (n_odd, d, 2 * SSM_WIDTH)) * d ** -0.5,
        "w_out_ssm": nrm(ks[12], (n_odd, SSM_WIDTH, d)) * SSM_WIDTH ** -0.5,
        "lam_re": -0.5 + 0.01 * nrm(ks[13], (n_odd, SSM_GROUPS, SSM_STATE)),
        "lam_im": math.pi * n_idx + 0.01 * nrm(ks[14], (n_odd, SSM_GROUPS, SSM_STATE)),
        "b_re": nrm(ks[15], (n_odd, SSM_GROUPS, SSM_STATE, SSM_GROUP)) * (2 * SSM_GROUP) ** -0.5,
        "b_im": nrm(ks[16], (n_odd, SSM_GROUPS, SSM_STATE, SSM_GROUP)) * (2 * SSM_GROUP) ** -0.5,
        "c_re": nrm(ks[17], (n_odd, SSM_GROUPS, SSM_GROUP, SSM_STATE)) * (2 * SSM_STATE) ** -0.5,
        "c_im": nrm(ks[18], (n_odd, SSM_GROUPS, SSM_GROUP, SSM_STATE)) * (2 * SSM_STATE) ** -0.5,
        "d_skip": nrm(ks[19], (n_odd, SSM_WIDTH)),
        "log_dt": log_dt,
        "w_glu": nrm(ks[21], (n_odd, SSM_WIDTH, SSM_WIDTH)) * SSM_WIDTH ** -0.5,
        "b_glu": 0.02 * nrm(ks[22], (n_odd, SSM_WIDTH)),
    }


def _fwd_reference(x, c, ln_pre_g, ln_post_g, w_mod, b_mod, w_in_ab, w_out_ab, sgu_norm_g, sgu_w, sgu_b,
              w_in_ssm, w_out_ssm, lam_re, lam_im, b_re, b_im, c_re, c_im, d_skip, log_dt, w_glu, b_glu):
    cond = jax.nn.silu(c)
    for layer in range(DEPTH):
        mod = cond @ w_mod[layer] + b_mod[layer]
        shift, scale, gate = jnp.split(mod[:, None, :], 3, axis=-1)
        h = rms_norm(x, ln_pre_g[layer]) * (1.0 + scale) + shift
        i = layer // 2
        if layer % 2 == 0:
            y = ab_mixer(h, w_in_ab[i], w_out_ab[i], sgu_norm_g[i], sgu_w[i], sgu_b[i])
        else:
            y = ssm_mixer(h, w_in_ssm[i], w_out_ssm[i], lam_re[i], lam_im[i], b_re[i], b_im[i],
                          c_re[i], c_im[i], d_skip[i], log_dt[i], w_glu[i], b_glu[i])
        x = x + (gate * rms_norm(y, ln_post_g[layer])).astype(x.dtype)
    return x


import jax as _jax
import jax.numpy as _jnp

TWIN_FORMAT = 'train_step'
FWD_PARAMS = ['x', 'c', 'ln_pre_g', 'ln_post_g', 'w_mod', 'b_mod', 'w_in_ab', 'w_out_ab', 'sgu_norm_g', 'sgu_w', 'sgu_b', 'w_in_ssm', 'w_out_ssm', 'lam_re', 'lam_im', 'b_re', 'b_im', 'c_re', 'c_im', 'd_skip', 'log_dt', 'w_glu', 'b_glu']
TWIN_WEIGHTS = ['ln_pre_g', 'ln_post_g', 'w_mod', 'b_mod', 'w_in_ab', 'w_out_ab', 'sgu_norm_g', 'sgu_w', 'sgu_b', 'w_in_ssm', 'w_out_ssm', 'lam_re', 'lam_im', 'b_re', 'b_im', 'c_re', 'c_im', 'd_skip', 'log_dt', 'w_glu', 'b_glu']
TWIN_DIFF_INPUT = 'x'
TWIN_INPUTS = ['x', 'c', 'ln_pre_g', 'ln_post_g', 'w_mod', 'b_mod', 'w_in_ab', 'w_out_ab', 'sgu_norm_g', 'sgu_w', 'sgu_b', 'w_in_ssm', 'w_out_ssm', 'lam_re', 'lam_im', 'b_re', 'b_im', 'c_re', 'c_im', 'd_skip', 'log_dt', 'w_glu', 'b_glu', 'loss_target', 'm_ln_pre_g', 'm_ln_post_g', 'm_w_mod', 'm_b_mod', 'm_w_in_ab', 'm_w_out_ab', 'm_sgu_norm_g', 'm_sgu_w', 'm_sgu_b', 'm_w_in_ssm', 'm_w_out_ssm', 'm_lam_re', 'm_lam_im', 'm_b_re', 'm_b_im', 'm_c_re', 'm_c_im', 'm_d_skip', 'm_log_dt', 'm_w_glu', 'm_b_glu', 'v_ln_pre_g', 'v_ln_post_g', 'v_w_mod', 'v_b_mod', 'v_w_in_ab', 'v_w_out_ab', 'v_sgu_norm_g', 'v_sgu_w', 'v_sgu_b', 'v_w_in_ssm', 'v_w_out_ssm', 'v_lam_re', 'v_lam_im', 'v_b_re', 'v_b_im', 'v_c_re', 'v_c_im', 'v_d_skip', 'v_log_dt', 'v_w_glu', 'v_b_glu']
TWIN_OUTPUTS = ['loss', 'grad_x', 'grad_ln_pre_g', 'grad_ln_post_g', 'grad_w_mod', 'grad_b_mod', 'grad_w_in_ab', 'grad_w_out_ab', 'grad_sgu_norm_g', 'grad_sgu_w', 'grad_sgu_b', 'grad_w_in_ssm', 'grad_w_out_ssm', 'grad_lam_re', 'grad_lam_im', 'grad_b_re', 'grad_b_im', 'grad_c_re', 'grad_c_im', 'grad_d_skip', 'grad_log_dt', 'grad_w_glu', 'grad_b_glu', 'delta_ln_pre_g', 'delta_ln_post_g', 'delta_w_mod', 'delta_b_mod', 'delta_w_in_ab', 'delta_w_out_ab', 'delta_sgu_norm_g', 'delta_sgu_w', 'delta_sgu_b', 'delta_w_in_ssm', 'delta_w_out_ssm', 'delta_lam_re', 'delta_lam_im', 'delta_b_re', 'delta_b_im', 'delta_c_re', 'delta_c_im', 'delta_d_skip', 'delta_log_dt', 'delta_w_glu', 'delta_b_glu', 'new_m_ln_pre_g', 'new_m_ln_post_g', 'new_m_w_mod', 'new_m_b_mod', 'new_m_w_in_ab', 'new_m_w_out_ab', 'new_m_sgu_norm_g', 'new_m_sgu_w', 'new_m_sgu_b', 'new_m_w_in_ssm', 'new_m_w_out_ssm', 'new_m_lam_re', 'new_m_lam_im', 'new_m_b_re', 'new_m_b_im', 'new_m_c_re', 'new_m_c_im', 'new_m_d_skip', 'new_m_log_dt', 'new_m_w_glu', 'new_m_b_glu', 'new_v_ln_pre_g', 'new_v_ln_post_g', 'new_v_w_mod', 'new_v_b_mod', 'new_v_w_in_ab', 'new_v_w_out_ab', 'new_v_sgu_norm_g', 'new_v_sgu_w', 'new_v_sgu_b', 'new_v_w_in_ssm', 'new_v_w_out_ssm', 'new_v_lam_re', 'new_v_lam_im', 'new_v_b_re', 'new_v_b_im', 'new_v_c_re', 'new_v_c_im', 'new_v_d_skip', 'new_v_log_dt', 'new_v_w_glu', 'new_v_b_glu']
TWIN_LEAF_KINDS = {'loss': 'loss', 'grad_x': 'grad_x', 'grad_ln_pre_g': 'grad_w', 'grad_ln_post_g': 'grad_w', 'grad_w_mod': 'grad_w', 'grad_b_mod': 'grad_w', 'grad_w_in_ab': 'grad_w', 'grad_w_out_ab': 'grad_w', 'grad_sgu_norm_g': 'grad_w', 'grad_sgu_w': 'grad_w', 'grad_sgu_b': 'grad_w', 'grad_w_in_ssm': 'grad_w', 'grad_w_out_ssm': 'grad_w', 'grad_lam_re': 'grad_w', 'grad_lam_im': 'grad_w', 'grad_b_re': 'grad_w', 'grad_b_im': 'grad_w', 'grad_c_re': 'grad_w', 'grad_c_im': 'grad_w', 'grad_d_skip': 'grad_w', 'grad_log_dt': 'grad_w', 'grad_w_glu': 'grad_w', 'grad_b_glu': 'grad_w', 'delta_ln_pre_g': 'delta_w', 'delta_ln_post_g': 'delta_w', 'delta_w_mod': 'delta_w', 'delta_b_mod': 'delta_w', 'delta_w_in_ab': 'delta_w', 'delta_w_out_ab': 'delta_w', 'delta_sgu_norm_g': 'delta_w', 'delta_sgu_w': 'delta_w', 'delta_sgu_b': 'delta_w', 'delta_w_in_ssm': 'delta_w', 'delta_w_out_ssm': 'delta_w', 'delta_lam_re': 'delta_w', 'delta_lam_im': 'delta_w', 'delta_b_re': 'delta_w', 'delta_b_im': 'delta_w', 'delta_c_re': 'delta_w', 'delta_c_im': 'delta_w', 'delta_d_skip': 'delta_w', 'delta_log_dt': 'delta_w', 'delta_w_glu': 'delta_w', 'delta_b_glu': 'delta_w', 'new_m_ln_pre_g': 'new_m', 'new_m_ln_post_g': 'new_m', 'new_m_w_mod': 'new_m', 'new_m_b_mod': 'new_m', 'new_m_w_in_ab': 'new_m', 'new_m_w_out_ab': 'new_m', 'new_m_sgu_norm_g': 'new_m', 'new_m_sgu_w': 'new_m', 'new_m_sgu_b': 'new_m', 'new_m_w_in_ssm': 'new_m', 'new_m_w_out_ssm': 'new_m', 'new_m_lam_re': 'new_m', 'new_m_lam_im': 'new_m', 'new_m_b_re': 'new_m', 'new_m_b_im': 'new_m', 'new_m_c_re': 'new_m', 'new_m_c_im': 'new_m', 'new_m_d_skip': 'new_m', 'new_m_log_dt': 'new_m', 'new_m_w_glu': 'new_m', 'new_m_b_glu': 'new_m', 'new_v_ln_pre_g': 'new_v', 'new_v_ln_post_g': 'new_v', 'new_v_w_mod': 'new_v', 'new_v_b_mod': 'new_v', 'new_v_w_in_ab': 'new_v', 'new_v_w_out_ab': 'new_v', 'new_v_sgu_norm_g': 'new_v', 'new_v_sgu_w': 'new_v', 'new_v_sgu_b': 'new_v', 'new_v_w_in_ssm': 'new_v', 'new_v_w_out_ssm': 'new_v', 'new_v_lam_re': 'new_v', 'new_v_lam_im': 'new_v', 'new_v_b_re': 'new_v', 'new_v_b_im': 'new_v', 'new_v_c_re': 'new_v', 'new_v_c_im': 'new_v', 'new_v_d_skip': 'new_v', 'new_v_log_dt': 'new_v', 'new_v_w_glu': 'new_v', 'new_v_b_glu': 'new_v'}


def _forward(args):
    return _fwd_reference(*[args[k] for k in FWD_PARAMS])


def _output_shape():
    def fwd():
        inp = _fwd_setup_inputs(0)
        return _fwd_reference(*[inp[k] for k in FWD_PARAMS])
    out = _jax.eval_shape(fwd)
    return out.shape, out.dtype

N_MICROBATCH = 1
ADAM_LR = 0.001
ADAM_B1 = 0.9
ADAM_B2 = 0.999
ADAM_EPS = 1e-08
ADAM_WD = 0.01
ADAM_STEP = 10
PER_EXAMPLE_BATCH_AXIS = {'x': 0, 'c': 0, 'loss_target': 0}
SHARED_INPUTS = []
_WEIGHT_DTYPES = {'ln_pre_g': _jnp.float32, 'ln_post_g': _jnp.float32, 'w_mod': _jnp.float32, 'b_mod': _jnp.float32, 'w_in_ab': _jnp.float32, 'w_out_ab': _jnp.float32, 'sgu_norm_g': _jnp.float32, 'sgu_w': _jnp.float32, 'sgu_b': _jnp.float32, 'w_in_ssm': _jnp.float32, 'w_out_ssm': _jnp.float32, 'lam_re': _jnp.float32, 'lam_im': _jnp.float32, 'b_re': _jnp.float32, 'b_im': _jnp.float32, 'c_re': _jnp.float32, 'c_im': _jnp.float32, 'd_skip': _jnp.float32, 'log_dt': _jnp.float32, 'w_glu': _jnp.float32, 'b_glu': _jnp.float32}
MOMENT_SCALE = {'ln_pre_g': 1.725285e-01, 'ln_post_g': 6.634693e+00, 'w_mod': 1.255365e+00, 'b_mod': 2.645445e+00, 'w_in_ab': 1.947211e-01, 'w_out_ab': 3.281325e-01, 'sgu_norm_g': 7.779274e-02, 'sgu_w': 6.795647e-02, 'sgu_b': 9.801856e-02, 'w_in_ssm': 3.696513e-01, 'w_out_ssm': 4.200578e-01, 'lam_re': 3.661087e-02, 'lam_im': 4.000199e-02, 'b_re': 2.745007e-02, 'b_im': 3.105615e-02, 'c_re': 5.413131e-02, 'c_im': 5.886685e-02, 'd_skip': 6.265727e-01, 'log_dt': 4.227585e+00, 'w_glu': 1.397315e-01, 'b_glu': 2.654508e-01}


def _to_microbatches(a, axis):
    t = _jnp.moveaxis(a, axis, 0)
    t = t.reshape((N_MICROBATCH, t.shape[0] // N_MICROBATCH) + t.shape[1:])
    return _jnp.moveaxis(t, 1, axis + 1)


def setup_inputs(seed: int = 0) -> dict:
    inp = _fwd_setup_inputs(seed)
    key = _jax.random.fold_in(_jax.random.key(seed), 7919)
    shape, _ = _output_shape()
    out = dict(inp)
    out["loss_target"] = _jax.random.normal(_jax.random.fold_in(key, 0), shape, _jnp.float32)
    for i, name in enumerate(TWIN_WEIGHTS):
        w = inp[name].astype(_jnp.float32)
        if MOMENT_SCALE is None:
            s = _jnp.sqrt(_jnp.mean(_jnp.square(w)) + 1e-30)
        else:
            s = MOMENT_SCALE[name]
        km, kv = _jax.random.split(_jax.random.fold_in(key, i + 1))
        out[name] = w
        out["m_" + name] = s * _jax.random.normal(km, w.shape, _jnp.float32)
        out["v_" + name] = (s * s) * _jax.random.uniform(kv, w.shape, _jnp.float32, 0.5, 1.5)
    if N_MICROBATCH > 1:
        for name, axis in PER_EXAMPLE_BATCH_AXIS.items():
            out[name] = _to_microbatches(out[name], axis)
    return {'x': out['x'], 'c': out['c'], 'ln_pre_g': out['ln_pre_g'], 'ln_post_g': out['ln_post_g'], 'w_mod': out['w_mod'], 'b_mod': out['b_mod'], 'w_in_ab': out['w_in_ab'], 'w_out_ab': out['w_out_ab'], 'sgu_norm_g': out['sgu_norm_g'], 'sgu_w': out['sgu_w'], 'sgu_b': out['sgu_b'], 'w_in_ssm': out['w_in_ssm'], 'w_out_ssm': out['w_out_ssm'], 'lam_re': out['lam_re'], 'lam_im': out['lam_im'], 'b_re': out['b_re'], 'b_im': out['b_im'], 'c_re': out['c_re'], 'c_im': out['c_im'], 'd_skip': out['d_skip'], 'log_dt': out['log_dt'], 'w_glu': out['w_glu'], 'b_glu': out['b_glu'], 'loss_target': out['loss_target'], 'm_ln_pre_g': out['m_ln_pre_g'], 'm_ln_post_g': out['m_ln_post_g'], 'm_w_mod': out['m_w_mod'], 'm_b_mod': out['m_b_mod'], 'm_w_in_ab': out['m_w_in_ab'], 'm_w_out_ab': out['m_w_out_ab'], 'm_sgu_norm_g': out['m_sgu_norm_g'], 'm_sgu_w': out['m_sgu_w'], 'm_sgu_b': out['m_sgu_b'], 'm_w_in_ssm': out['m_w_in_ssm'], 'm_w_out_ssm': out['m_w_out_ssm'], 'm_lam_re': out['m_lam_re'], 'm_lam_im': out['m_lam_im'], 'm_b_re': out['m_b_re'], 'm_b_im': out['m_b_im'], 'm_c_re': out['m_c_re'], 'm_c_im': out['m_c_im'], 'm_d_skip': out['m_d_skip'], 'm_log_dt': out['m_log_dt'], 'm_w_glu': out['m_w_glu'], 'm_b_glu': out['m_b_glu'], 'v_ln_pre_g': out['v_ln_pre_g'], 'v_ln_post_g': out['v_ln_post_g'], 'v_w_mod': out['v_w_mod'], 'v_b_mod': out['v_b_mod'], 'v_w_in_ab': out['v_w_in_ab'], 'v_w_out_ab': out['v_w_out_ab'], 'v_sgu_norm_g': out['v_sgu_norm_g'], 'v_sgu_w': out['v_sgu_w'], 'v_sgu_b': out['v_sgu_b'], 'v_w_in_ssm': out['v_w_in_ssm'], 'v_w_out_ssm': out['v_w_out_ssm'], 'v_lam_re': out['v_lam_re'], 'v_lam_im': out['v_lam_im'], 'v_b_re': out['v_b_re'], 'v_b_im': out['v_b_im'], 'v_c_re': out['v_c_re'], 'v_c_im': out['v_c_im'], 'v_d_skip': out['v_d_skip'], 'v_log_dt': out['v_log_dt'], 'v_w_glu': out['v_w_glu'], 'v_b_glu': out['v_b_glu']}


def _loss(weights, diff, rest, loss_target):
    with _jax.named_scope("forward"):
        args = {**rest, TWIN_DIFF_INPUT: diff, **{k: w.astype(_WEIGHT_DTYPES[k]) for k, w in weights.items()}}
        y = _forward(args)
    with _jax.named_scope("loss_head"):
        err = _jnp.square(y.astype(_jnp.float32) - loss_target)
        return 0.5 * _jnp.sum(_jnp.mean(err, axis=-1)) if err.ndim else 0.5 * err


def _adamw(w, g, m, v):
    m = ADAM_B1 * m + (1.0 - ADAM_B1) * g
    v = ADAM_B2 * v + (1.0 - ADAM_B2) * _jnp.square(g)
    m_hat = m / (1.0 - ADAM_B1 ** ADAM_STEP)
    v_hat = v / (1.0 - ADAM_B2 ** ADAM_STEP)
    delta = -ADAM_LR * (m_hat / (_jnp.sqrt(v_hat) + ADAM_EPS) + ADAM_WD * w)
    return delta, m, v


def reference(x, c, ln_pre_g, ln_post_g, w_mod, b_mod, w_in_ab, w_out_ab, sgu_norm_g, sgu_w, sgu_b, w_in_ssm, w_out_ssm, lam_re, lam_im, b_re, b_im, c_re, c_im, d_skip, log_dt, w_glu, b_glu, loss_target, m_ln_pre_g, m_ln_post_g, m_w_mod, m_b_mod, m_w_in_ab, m_w_out_ab, m_sgu_norm_g, m_sgu_w, m_sgu_b, m_w_in_ssm, m_w_out_ssm, m_lam_re, m_lam_im, m_b_re, m_b_im, m_c_re, m_c_im, m_d_skip, m_log_dt, m_w_glu, m_b_glu, v_ln_pre_g, v_ln_post_g, v_w_mod, v_b_mod, v_w_in_ab, v_w_out_ab, v_sgu_norm_g, v_sgu_w, v_sgu_b, v_w_in_ssm, v_w_out_ssm, v_lam_re, v_lam_im, v_b_re, v_b_im, v_c_re, v_c_im, v_d_skip, v_log_dt, v_w_glu, v_b_glu):
    given = dict(x=x, c=c, ln_pre_g=ln_pre_g, ln_post_g=ln_post_g, w_mod=w_mod, b_mod=b_mod, w_in_ab=w_in_ab, w_out_ab=w_out_ab, sgu_norm_g=sgu_norm_g, sgu_w=sgu_w, sgu_b=sgu_b, w_in_ssm=w_in_ssm, w_out_ssm=w_out_ssm, lam_re=lam_re, lam_im=lam_im, b_re=b_re, b_im=b_im, c_re=c_re, c_im=c_im, d_skip=d_skip, log_dt=log_dt, w_glu=w_glu, b_glu=b_glu, loss_target=loss_target, m_ln_pre_g=m_ln_pre_g, m_ln_post_g=m_ln_post_g, m_w_mod=m_w_mod, m_b_mod=m_b_mod, m_w_in_ab=m_w_in_ab, m_w_out_ab=m_w_out_ab, m_sgu_norm_g=m_sgu_norm_g, m_sgu_w=m_sgu_w, m_sgu_b=m_sgu_b, m_w_in_ssm=m_w_in_ssm, m_w_out_ssm=m_w_out_ssm, m_lam_re=m_lam_re, m_lam_im=m_lam_im, m_b_re=m_b_re, m_b_im=m_b_im, m_c_re=m_c_re, m_c_im=m_c_im, m_d_skip=m_d_skip, m_log_dt=m_log_dt, m_w_glu=m_w_glu, m_b_glu=m_b_glu, v_ln_pre_g=v_ln_pre_g, v_ln_post_g=v_ln_post_g, v_w_mod=v_w_mod, v_b_mod=v_b_mod, v_w_in_ab=v_w_in_ab, v_w_out_ab=v_w_out_ab, v_sgu_norm_g=v_sgu_norm_g, v_sgu_w=v_sgu_w, v_sgu_b=v_sgu_b, v_w_in_ssm=v_w_in_ssm, v_w_out_ssm=v_w_out_ssm, v_lam_re=v_lam_re, v_lam_im=v_lam_im, v_b_re=v_b_re, v_b_im=v_b_im, v_c_re=v_c_re, v_c_im=v_c_im, v_d_skip=v_d_skip, v_log_dt=v_log_dt, v_w_glu=v_w_glu, v_b_glu=v_b_glu)
    weights = {n: given[n] for n in TWIN_WEIGHTS}
    shared = {n: given[n] for n in SHARED_INPUTS}
    per_example = {n: given[n] for n in ['x', 'c']}
    grad_fn = _jax.value_and_grad(_loss, argnums=(0, 1))

    def one_microbatch(ex, loss_target):
        ex = dict(ex)
        diff = ex.pop(TWIN_DIFF_INPUT)
        return grad_fn(weights, diff, {**shared, **ex}, loss_target)

    if N_MICROBATCH == 1:
        loss, (grad_w, grad_x) = one_microbatch(per_example, given["loss_target"])
    else:
        def body(carry, xs):
            loss_sum, grad_sum = carry
            l_k, (gw_k, gx_k) = one_microbatch(xs[0], xs[1])
            with _jax.named_scope("update"):
                return (loss_sum + l_k, _jax.tree.map(_jnp.add, grad_sum, gw_k)), gx_k

        init = (_jnp.zeros((), _jnp.float32), _jax.tree.map(_jnp.zeros_like, weights))
        (loss, grad_w), grad_x = _jax.lax.scan(body, init, (per_example, given["loss_target"]))
    with _jax.named_scope("update"):
        delta_w, new_m, new_v = {}, {}, {}
        for n in TWIN_WEIGHTS:
            delta_w[n], new_m[n], new_v[n] = _adamw(weights[n], grad_w[n], given["m_" + n], given["v_" + n])
    return (loss, grad_x, *[grad_w[n] for n in TWIN_WEIGHTS], *[delta_w[n] for n in TWIN_WEIGHTS],
            *[new_m[n] for n in TWIN_WEIGHTS], *[new_v[n] for n in TWIN_WEIGHTS])
```

```python
import functools
import math

import jax
import jax.numpy as jnp
from jax import lax
from jax.experimental import pallas as pl
from jax.experimental.pallas import tpu as pltpu

f32 = jnp.float32
bf16 = jnp.bfloat16

N_DEV = 8
EPS = 1e-6
HEAD = 128
SUBLANES = 8
SSM_GROUP = 16
SSM_STATE = 64
GROUPS_PER_LANE_BLOCK = HEAD // SSM_GROUP
STATES_PER_LANE_BLOCK = GROUPS_PER_LANE_BLOCK * SSM_STATE
VMEM_LIMIT = 56 * 2 ** 20
ADAM_LR, ADAM_B1, ADAM_B2, ADAM_EPS, ADAM_WD, ADAM_STEP = 0.001, 0.9, 0.999, 1e-08, 0.01, 10
_GELU_C0 = math.sqrt(2.0 / math.pi)
_GELU_C1 = 0.044715
MESH = pl.DeviceIdType.MESH


def _cparams(*sem):
    return pltpu.CompilerParams(dimension_semantics=sem if sem else None, vmem_limit_bytes=VMEM_LIMIT)


def _gelu(x):
    return 0.5 * x * (1.0 + jnp.tanh(_GELU_C0 * (x + _GELU_C1 * x * x * x)))


def _gelu_grad(x):
    t = jnp.tanh(_GELU_C0 * (x + _GELU_C1 * x * x * x))
    return 0.5 * (1.0 + t) + 0.5 * x * (1.0 - t * t) * _GELU_C0 * (1.0 + 3.0 * _GELU_C1 * x * x)


def _silu(x):
    return x * jax.nn.sigmoid(x)


def _silu_grad(x):
    s = jax.nn.sigmoid(x)
    return s * (1.0 + x * (1.0 - s))


def _dot(a, b):
    return jnp.dot(a, b, preferred_element_type=f32)


def _dot_nt(a, b):
    return lax.dot_general(a, b, (((1,), (1,)), ((), ())), preferred_element_type=f32)


def _dot_tn(a, b):
    return lax.dot_general(a, b, (((0,), (0,)), ((), ())), preferred_element_type=f32)


def _split_bf16(v):
    hi = v.astype(bf16)
    lo = (v - hi.astype(f32)).astype(bf16)
    return hi, lo


def _row(d):
    return pl.BlockSpec((1, d), lambda *_: (0, 0))


def _my_index():
    return 4 * lax.axis_index("x") + 2 * lax.axis_index("y") + lax.axis_index("c")


def _peer(k):
    x, y, c = lax.axis_index("x"), lax.axis_index("y"), lax.axis_index("c")
    return (1 - x if k & 4 else x, 1 - y if k & 2 else y, 1 - c if k & 1 else c)


def all_gather(arrs, name):
    n = len(arrs)

    def body(*refs):
        ins, outs = refs[:n], refs[n:2 * n]
        send, recv, local = refs[2 * n:]
        me = _my_index()
        copies = []
        for a in range(n):
            cp = pltpu.make_async_copy(ins[a], outs[a].at[me], local.at[a])
            cp.start()
            copies.append(cp)
            for k in range(1, N_DEV):
                s = a * (N_DEV - 1) + k - 1
                cp = pltpu.make_async_remote_copy(src_ref=ins[a], dst_ref=outs[a].at[me], send_sem=send.at[s],
                                                  recv_sem=recv.at[s], device_id=_peer(k), device_id_type=MESH)
                cp.start()
                copies.append(cp)
        for cp in copies:
            cp.wait()

    any_spec = pl.BlockSpec(memory_space=pl.ANY)
    outs = pl.pallas_call(
        body, name=name,
        out_shape=[jax.ShapeDtypeStruct((N_DEV,) + a.shape, a.dtype) for a in arrs],
        in_specs=[any_spec] * n, out_specs=[any_spec] * n,
        scratch_shapes=[pltpu.SemaphoreType.DMA((n * (N_DEV - 1),)), pltpu.SemaphoreType.DMA((n * (N_DEV - 1),)),
                        pltpu.SemaphoreType.DMA((n,))],
        compiler_params=pltpu.CompilerParams(has_side_effects=True),
    )(*arrs)
    return list(outs)


def all_to_all(arrs, name):
    n = len(arrs)

    def body(*refs):
        ins, outs = refs[:n], refs[n:2 * n]
        send, recv, local = refs[2 * n:]
        me = _my_index()
        copies = []
        for a in range(n):
            cp = pltpu.make_async_copy(ins[a].at[me], outs[a].at[me], local.at[a])
            cp.start()
            copies.append(cp)
            for k in range(1, N_DEV):
                s = a * (N_DEV - 1) + k - 1
                cp = pltpu.make_async_remote_copy(src_ref=ins[a].at[jnp.bitwise_xor(me, k)], dst_ref=outs[a].at[me],
                                                  send_sem=send.at[s], recv_sem=recv.at[s], device_id=_peer(k),
                                                  device_id_type=MESH)
                cp.start()
                copies.append(cp)
        for cp in copies:
            cp.wait()

    any_spec = pl.BlockSpec(memory_space=pl.ANY)
    outs = pl.pallas_call(
        body, name=name,
        out_shape=[jax.ShapeDtypeStruct(a.shape, a.dtype) for a in arrs],
        in_specs=[any_spec] * n, out_specs=[any_spec] * n,
        scratch_shapes=[pltpu.SemaphoreType.DMA((n * (N_DEV - 1),)), pltpu.SemaphoreType.DMA((n * (N_DEV - 1),)),
                        pltpu.SemaphoreType.DMA((n,))],
        compiler_params=pltpu.CompilerParams(has_side_effects=True),
    )(*arrs)
    return list(outs)


def _tile(n, pref):
    for t in pref:
        if n % t == 0:
            return t
    return n


def mm_nn(a, b3, out_dtype, name):
    m, k = a.shape
    nb, _, bn = b3.shape
    tm = _tile(m, (512, 256, 128))
    tn = _tile(bn, (1024, 896, 512, 256, 128))
    per = bn // tn

    def body(a_ref, b_ref, o_ref):
        o_ref[...] = _dot(a_ref[...], b_ref[...]).astype(o_ref.dtype)

    return pl.pallas_call(
        body, name=name, grid=(m // tm, nb, per),
        in_specs=[pl.BlockSpec((tm, k), lambda i, j, jj: (i, 0)),
                  pl.BlockSpec((None, k, tn), lambda i, j, jj: (j, 0, jj))],
        out_specs=pl.BlockSpec((tm, tn), lambda i, j, jj: (i, j * per + jj)),
        out_shape=jax.ShapeDtypeStruct((m, nb * bn), out_dtype),
        compiler_params=_cparams("parallel", "arbitrary", "arbitrary"),
    )(a, b3)


def mm_nt(a, w3, out_dtype, name):
    m, _ = a.shape
    nb, ko, bn = w3.shape
    tm = _tile(m, (512, 256, 128))
    tko = _tile(ko, (1024, 512, 256, 128))

    def body(a_ref, w_ref, o_ref, acc_ref):
        j = pl.program_id(2)

        @pl.when(j == 0)
        def _():
            acc_ref[...] = jnp.zeros_like(acc_ref)

        acc_ref[...] += _dot_nt(a_ref[...], w_ref[...])

        @pl.when(j == nb - 1)
        def _():
            o_ref[...] = acc_ref[...].astype(o_ref.dtype)

    return pl.pallas_call(
        body, name=name, grid=(m // tm, ko // tko, nb),
        in_specs=[pl.BlockSpec((tm, bn), lambda i, o, j: (i, j)),
                  pl.BlockSpec((None, tko, bn), lambda i, o, j: (j, o, 0))],
        out_specs=pl.BlockSpec((tm, tko), lambda i, o, j: (i, o)),
        out_shape=jax.ShapeDtypeStruct((m, ko), out_dtype),
        scratch_shapes=[pltpu.VMEM((tm, tko), f32)],
        compiler_params=_cparams("parallel", "arbitrary", "arbitrary"),
    )(a, w3)


def mm_tn(a, dy, ncb, out_dtype, name):
    l, ka = a.shape
    _, n = dy.shape
    bn = n // ncb
    tl = _tile(l, (512, 256, 128))
    tka = _tile(ka, (512, 256, 128))
    tn = _tile(bn, (1024, 896, 512, 256, 128))
    per = bn // tn
    nl = l // tl

    def body(a_ref, dy_ref, o_ref, acc_ref):
        s = pl.program_id(2)

        @pl.when(s == 0)
        def _():
            acc_ref[...] = jnp.zeros_like(acc_ref)

        acc_ref[...] += _dot_tn(a_ref[...], dy_ref[...])

        @pl.when(s == nl - 1)
        def _():
            o_ref[...] = acc_ref[...].astype(o_ref.dtype)

    return pl.pallas_call(
        body, name=name, grid=(ka // tka, n // tn, nl),
        in_specs=[pl.BlockSpec((tl, tka), lambda i, j, s: (s, i)),
                  pl.BlockSpec((tl, tn), lambda i, j, s: (s, j))],
        out_specs=pl.BlockSpec((None, tka, tn), lambda i, j, s: (j // per, i, j % per)),
        out_shape=jax.ShapeDtypeStruct((ncb, ka, bn), out_dtype),
        scratch_shapes=[pltpu.VMEM((tka, tn), f32)],
        compiler_params=_cparams("parallel", "parallel", "arbitrary"),
    )(a, dy)


def mod_part(c_all, w_mod, b_cols):
    nl, d, cols = w_mod.shape

    def body(c_ref, w_ref, b_ref, o_ref):
        cond = _silu(c_ref[...]).astype(bf16)
        o_ref[...] = _dot(cond, w_ref[...].astype(bf16)) + b_ref[...]

    return pl.pallas_call(
        body, name="mod_part", grid=(nl,),
        in_specs=[pl.BlockSpec((N_DEV, d), lambda l: (0, 0)),
                  pl.BlockSpec((None, d, cols), lambda l: (l, 0, 0)),
                  pl.BlockSpec((None, 1, cols), lambda l: (l, 0, 0))],
        out_specs=pl.BlockSpec((None, N_DEV, cols), lambda l: (l, 0, 0)),
        out_shape=jax.ShapeDtypeStruct((nl, N_DEV, cols), f32),
        compiler_params=_cparams("arbitrary"),
    )(c_all, w_mod, b_cols.reshape(nl, 1, cols))


def _row_tile(l):
    return _tile(l, (256, 128))


def prenorm_fwd(x, g, shift, scale, name):
    l, d = x.shape
    tm = _row_tile(l)

    def body(x_ref, g_ref, sh_ref, sc_ref, h_ref):
        xv = x_ref[...]
        r = lax.rsqrt(jnp.mean(xv * xv, axis=-1, keepdims=True) + EPS)
        h_ref[...] = (xv * r * (g_ref[...] * (1.0 + sc_ref[...])) + sh_ref[...]).astype(h_ref.dtype)

    return pl.pallas_call(
        body, name=name, grid=(l // tm,),
        in_specs=[pl.BlockSpec((tm, d), lambda i: (i, 0)), _row(d), _row(d), _row(d)],
        out_specs=pl.BlockSpec((tm, d), lambda i: (i, 0)),
        out_shape=jax.ShapeDtypeStruct((l, d), bf16),
        compiler_params=_cparams("parallel"),
    )(x, g, shift, scale)


def post_fwd(x, y, gate, g, name):
    l, d = x.shape
    tm = _row_tile(l)

    def body(x_ref, y_ref, gate_ref, g_ref, o_ref):
        yv = y_ref[...]
        r = lax.rsqrt(jnp.mean(yv * yv, axis=-1, keepdims=True) + EPS)
        o_ref[...] = x_ref[...] + gate_ref[...] * (yv * r * g_ref[...])

    blk = pl.BlockSpec((tm, d), lambda i: (i, 0))
    return pl.pallas_call(
        body, name=name, grid=(l // tm,),
        in_specs=[blk, blk, _row(d), _row(d)], out_specs=blk,
        out_shape=jax.ShapeDtypeStruct((l, d), f32),
        compiler_params=_cparams("parallel"),
    )(x, y, gate, g)


def final_loss(x, y, gate, g, target):
    l, d = x.shape
    tm = _row_tile(l)

    def body(x_ref, y_ref, gate_ref, g_ref, t_ref, dx_ref, loss_ref):
        @pl.when(pl.program_id(0) == 0)
        def _():
            loss_ref[...] = jnp.zeros_like(loss_ref)

        yv = y_ref[...]
        r = lax.rsqrt(jnp.mean(yv * yv, axis=-1, keepdims=True) + EPS)
        diff = x_ref[...] + gate_ref[...] * (yv * r * g_ref[...]) - t_ref[...]
        dx_ref[...] = diff * (1.0 / d)
        loss_ref[...] += jnp.sum(diff * diff)

    blk = pl.BlockSpec((tm, d), lambda i: (i, 0))
    return pl.pallas_call(
        body, name="final_loss", grid=(l // tm,),
        in_specs=[blk, blk, _row(d), _row(d), blk],
        out_specs=[blk, pl.BlockSpec((SUBLANES, HEAD), lambda i: (0, 0))],
        out_shape=[jax.ShapeDtypeStruct((l, d), f32), jax.ShapeDtypeStruct((SUBLANES, HEAD), f32)],
        compiler_params=_cparams("arbitrary"),
    )(x, y, gate, g, target)


def post_bwd(dx, y, gate, g, name):
    l, d = dx.shape
    tm = _row_tile(l)

    def body(dx_ref, y_ref, gate_ref, g_ref, dy_ref, dgate_ref, dg_ref):
        @pl.when(pl.program_id(0) == 0)
        def _():
            dgate_ref[...] = jnp.zeros_like(dgate_ref)
            dg_ref[...] = jnp.zeros_like(dg_ref)

        yv, dxv, gv = y_ref[...], dx_ref[...], g_ref[...]
        r = lax.rsqrt(jnp.mean(yv * yv, axis=-1, keepdims=True) + EPS)
        yn = yv * r
        dgate_ref[...] += jnp.sum(dxv * yn * gv, axis=0, keepdims=True)
        dyg = dxv * gate_ref[...]
        dg_ref[...] += jnp.sum(dyg * yn, axis=0, keepdims=True)
        dyn = dyg * gv
        dy_ref[...] = (r * (dyn - yn * jnp.mean(dyn * yn, axis=-1, keepdims=True))).astype(dy_ref.dtype)

    blk = pl.BlockSpec((tm, d), lambda i: (i, 0))
    return pl.pallas_call(
        body, name=name, grid=(l // tm,),
        in_specs=[blk, blk, _row(d), _row(d)], out_specs=[blk, _row(d), _row(d)],
        out_shape=[jax.ShapeDtypeStruct((l, d), bf16), jax.ShapeDtypeStruct((1, d), f32),
                   jax.ShapeDtypeStruct((1, d), f32)],
        compiler_params=_cparams("arbitrary"),
    )(dx, y, gate, g)


def prenorm_bwd(dh, x, dx_next, g, scale, name):
    l, d = x.shape
    tm = _row_tile(l)

    def body(dh_ref, x_ref, dxn_ref, g_ref, sc_ref, dx_ref, dsh_ref, dsc_ref, dg_ref):
        @pl.when(pl.program_id(0) == 0)
        def _():
            dsh_ref[...] = jnp.zeros_like(dsh_ref)
            dsc_ref[...] = jnp.zeros_like(dsc_ref)
            dg_ref[...] = jnp.zeros_like(dg_ref)

        xv, dhv, gv, sc1 = x_ref[...], dh_ref[...], g_ref[...], 1.0 + sc_ref[...]
        r = lax.rsqrt(jnp.mean(xv * xv, axis=-1, keepdims=True) + EPS)
        xn = xv * r
        dhx = dhv * xn
        dsh_ref[...] += jnp.sum(dhv, axis=0, keepdims=True)
        dsc_ref[...] += jnp.sum(dhx * gv, axis=0, keepdims=True)
        dg_ref[...] += jnp.sum(dhx * sc1, axis=0, keepdims=True)
        dxn = dhv * (gv * sc1)
        dx_ref[...] = dxn_ref[...] + r * (dxn - xn * jnp.mean(dxn * xn, axis=-1, keepdims=True))

    blk = pl.BlockSpec((tm, d), lambda i: (i, 0))
    return pl.pallas_call(
        body, name=name, grid=(l // tm,),
        in_specs=[blk, blk, blk, _row(d), _row(d)], out_specs=[blk, _row(d), _row(d), _row(d)],
        out_shape=[jax.ShapeDtypeStruct((l, d), f32)] + [jax.ShapeDtypeStruct((1, d), f32)] * 3,
        compiler_params=_cparams("arbitrary"),
    )(dh, x, dx_next, g, scale)


def _tril_mask():
    r = lax.broadcasted_iota(jnp.int32, (HEAD, HEAD), 0)
    c = lax.broadcasted_iota(jnp.int32, (HEAD, HEAD), 1)
    return r >= c


def sgu_fwd(proj, norm_g, w_s, b_s):
    l = proj.shape[0]
    nh = w_s.shape[0]
    wa = nh * HEAD

    def body(au_ref, av_ref, az_ref, ng_ref, w_ref, b_ref, o_ref):
        tril = _tril_mask()
        for h in range(nh):
            sl = slice(h * HEAD, (h + 1) * HEAD)
            gv = _gelu(av_ref[:, sl].astype(f32))
            r = lax.rsqrt(jnp.mean(gv * gv, axis=-1, keepdims=True) + EPS)
            vh = gv * r * ng_ref[:, sl]
            wm = jnp.where(tril, w_ref[h], 0.0).astype(bf16)
            s = _dot(wm, vh.astype(bf16)) + b_ref[h]
            o_ref[:, sl] = (_gelu(au_ref[:, sl].astype(f32)) * s * _silu(az_ref[:, sl].astype(f32))).astype(o_ref.dtype)

    def col(j):
        return pl.BlockSpec((HEAD, wa), lambda n: (n, j))

    return pl.pallas_call(
        body, name="sgu_fwd", grid=(l // HEAD,),
        in_specs=[col(0), col(1), col(2), _row(wa),
                  pl.BlockSpec((nh, HEAD, HEAD), lambda n: (0, 0, 0)), pl.BlockSpec((nh, HEAD, 1), lambda n: (0, 0, 0))],
        out_specs=pl.BlockSpec((HEAD, wa), lambda n: (n, 0)),
        out_shape=jax.ShapeDtypeStruct((l, wa), bf16),
        compiler_params=_cparams("parallel"),
    )(proj, proj, proj, norm_g, w_s, b_s)


def sgu_bwd(proj, dcat, norm_g, w_s, b_s):
    l = proj.shape[0]
    nh = w_s.shape[0]
    wa = nh * HEAD

    def body(au_ref, av_ref, az_ref, do_ref, ng_ref, w_ref, b_ref, da_ref, dw_ref, db_ref, dng_ref):
        @pl.when(pl.program_id(0) == 0)
        def _():
            dw_ref[...] = jnp.zeros_like(dw_ref)
            db_ref[...] = jnp.zeros_like(db_ref)
            dng_ref[...] = jnp.zeros_like(dng_ref)

        tril = _tril_mask()
        for h in range(nh):
            sl = slice(h * HEAD, (h + 1) * HEAD)
            au, av, az = au_ref[:, sl].astype(f32), av_ref[:, sl].astype(f32), az_ref[:, sl].astype(f32)
            ng = ng_ref[:, sl]
            gv = _gelu(av)
            r = lax.rsqrt(jnp.mean(gv * gv, axis=-1, keepdims=True) + EPS)
            gvn = gv * r
            vh = (gvn * ng).astype(bf16)
            wm = jnp.where(tril, w_ref[h], 0.0).astype(bf16)
            s = _dot(wm, vh) + b_ref[h]
            gu, sz = _gelu(au), _silu(az)
            dov = do_ref[:, sl].astype(f32)
            da_ref[:, sl] = (dov * s * sz * _gelu_grad(au)).astype(da_ref.dtype)
            da_ref[:, 2 * wa + h * HEAD:2 * wa + (h + 1) * HEAD] = (dov * gu * s * _silu_grad(az)).astype(da_ref.dtype)
            ds = dov * gu * sz
            db_ref[h] += jnp.sum(ds, axis=-1, keepdims=True)
            dsb = ds.astype(bf16)
            dw_ref[h] += jnp.where(tril, _dot_nt(dsb, vh), 0.0)
            dvh = _dot_tn(wm, dsb)
            dng_ref[:, sl] += jnp.sum(dvh * gvn, axis=0, keepdims=True)
            dgvn = dvh * ng
            dgv = r * (dgvn - gvn * jnp.mean(dgvn * gvn, axis=-1, keepdims=True))
            da_ref[:, wa + h * HEAD:wa + (h + 1) * HEAD] = (dgv * _gelu_grad(av)).astype(da_ref.dtype)

    def col(j):
        return pl.BlockSpec((HEAD, wa), lambda n: (n, j))

    whole_w = pl.BlockSpec((nh, HEAD, HEAD), lambda n: (0, 0, 0))
    whole_b = pl.BlockSpec((nh, HEAD, 1), lambda n: (0, 0, 0))
    return pl.pallas_call(
        body, name="sgu_bwd", grid=(l // HEAD,),
        in_specs=[col(0), col(1), col(2), col(0), _row(wa), whole_w, whole_b],
        out_specs=[pl.BlockSpec((HEAD, 3 * wa), lambda n: (n, 0)), whole_w, whole_b, _row(wa)],
        out_shape=[jax.ShapeDtypeStruct((l, 3 * wa), bf16), jax.ShapeDtypeStruct((nh, HEAD, HEAD), f32),
                   jax.ShapeDtypeStruct((nh, HEAD, 1), f32), jax.ShapeDtypeStruct((1, wa), f32)],
        compiler_params=_cparams("arbitrary"),
    )(proj, proj, proj, dcat, norm_g, w_s, b_s)


def _sb_block(q, k, scale, diag):
    z = _dot_nt(q, k) * scale
    sp = jnp.maximum(z, 0.0) + jnp.log(1.0 + jnp.exp(-jnp.abs(z)))
    r = lax.broadcasted_iota(jnp.int32, z.shape, 0)
    c = lax.broadcasted_iota(jnp.int32, z.shape, 1)
    mask = jnp.logical_or(c < r, jnp.logical_not(diag))
    return z - sp, jnp.where(mask, -sp, 0.0), mask


def _ones_where(cond):
    return jnp.where(cond, 1.0, 0.0).astype(bf16)


def sb_fwd(proj, nh):
    l = proj.shape[0]
    wb = nh * HEAD
    nq = l // HEAD
    scale = 1.0 / math.sqrt(HEAD)
    qc, kc, vc, zc = 3 * nh, 4 * nh, 5 * nh, 6 * nh

    def body(q_ref, k_ref, v_ref, bz_ref, o_ref, att_ref, tot_ref):
        i = pl.program_id(1)
        q = q_ref[...]
        r = lax.broadcasted_iota(jnp.int32, (HEAD, HEAD), 0)
        c = lax.broadcasted_iota(jnp.int32, (HEAD, HEAD), 1)
        after = _ones_where(r > c)

        def step(t, carry):
            acc, tot = carry
            j = i - t
            rows = pl.ds(pl.multiple_of(j * HEAD, HEAD), HEAD)
            lb, lk, mask = _sb_block(q, k_ref[rows, :], scale, t == 0)
            hi, lo = _split_bf16(lk)
            later = tot + _dot(hi, after) + _dot(lo, after)
            w = jnp.where(mask, jnp.exp(lb + later), 0.0)
            acc = acc + _dot(w.astype(bf16), v_ref[rows, :])
            return acc, tot + jnp.sum(lk, axis=-1, keepdims=True)

        acc, tot = lax.fori_loop(0, i + 1, step, (jnp.zeros((HEAD, HEAD), f32), jnp.zeros((HEAD, 1), f32)))
        att_ref[...] = acc.astype(att_ref.dtype)
        o_ref[...] = (acc * _silu(bz_ref[...].astype(f32))).astype(o_ref.dtype)
        tot_ref[...] = tot

    blk = lambda c0: pl.BlockSpec((HEAD, HEAD), lambda h, i: (i, c0 + h))
    head = lambda c0: pl.BlockSpec((l, HEAD), lambda h, i: (0, c0 + h))
    out = pl.BlockSpec((HEAD, HEAD), lambda h, i: (i, h))
    return pl.pallas_call(
        body, name="sb_fwd", grid=(nh, nq),
        in_specs=[blk(qc), head(kc), head(vc), blk(zc)],
        out_specs=[out, out, pl.BlockSpec((None, HEAD, 1), lambda h, i: (h, i, 0))],
        out_shape=[jax.ShapeDtypeStruct((l, wb), bf16), jax.ShapeDtypeStruct((l, wb), bf16),
                   jax.ShapeDtypeStruct((nh, l, 1), f32)],
        compiler_params=_cparams("parallel", "arbitrary"),
    )(proj, proj, proj, proj)


def sb_bwd(proj, dcat, att, tot, nh):
    l = proj.shape[0]
    wb = nh * HEAD
    nq = l // HEAD
    scale = 1.0 / math.sqrt(HEAD)
    qc, kc, vc, zc = 3 * nh, 4 * nh, 5 * nh, 6 * nh

    def body(q_ref, k_ref, v_ref, bz_ref, do_ref, att_ref, tot_ref, dq_ref, dk_ref, dv_ref, dbz_ref, dk_acc, dv_acc):
        i = pl.program_id(1)

        @pl.when(i == 0)
        def _():
            dk_acc[...] = jnp.zeros_like(dk_acc)
            dv_acc[...] = jnp.zeros_like(dv_acc)

        q = q_ref[...]
        bz = bz_ref[...].astype(f32)
        dov = do_ref[...].astype(f32)
        dbz_ref[...] = (dov * att_ref[...].astype(f32) * _silu_grad(bz)).astype(dbz_ref.dtype)
        dob = (dov * _silu(bz)).astype(bf16)
        total = tot_ref[...]
        r = lax.broadcasted_iota(jnp.int32, (HEAD, HEAD), 0)
        c = lax.broadcasted_iota(jnp.int32, (HEAD, HEAD), 1)
        upto = _ones_where(r <= c)
        before = _ones_where(r < c)

        def step(j, carry):
            dq, lk_seen, e_seen = carry
            rows = pl.ds(pl.multiple_of(j * HEAD, HEAD), HEAD)
            kj, vj = k_ref[rows, :], v_ref[rows, :]
            lb, lk, mask = _sb_block(q, kj, scale, j == i)
            hi, lo = _split_bf16(lk)
            later = total - lk_seen - (_dot(hi, upto) + _dot(lo, upto))
            w = jnp.where(mask, jnp.exp(lb + later), 0.0)
            wb16 = w.astype(bf16)
            dv_acc[rows, :] += _dot_tn(wb16, dob)
            e = _dot_nt(dob, vj) * w
            hi, lo = _split_bf16(e)
            pre = e_seen + _dot(hi, before) + _dot(lo, before)
            beta = jnp.exp(lb)
            dz = (jnp.where(mask, e * (1.0 - beta) - pre * beta, 0.0) * scale).astype(bf16)
            dk_acc[rows, :] += _dot_tn(dz, q)
            return (dq + _dot(dz, kj), lk_seen + jnp.sum(lk, axis=-1, keepdims=True),
                    e_seen + jnp.sum(e, axis=-1, keepdims=True))

        zero = jnp.zeros((HEAD, 1), f32)
        dq, _, _ = lax.fori_loop(0, i + 1, step, (jnp.zeros((HEAD, HEAD), f32), zero, zero))
        dq_ref[...] = dq.astype(dq_ref.dtype)

        @pl.when(i == nq - 1)
        def _():
            dk_ref[...] = dk_acc[...].astype(dk_ref.dtype)
            dv_ref[...] = dv_acc[...].astype(dv_ref.dtype)

    blk = lambda c0: pl.BlockSpec((HEAD, HEAD), lambda h, i: (i, c0 + h))
    head = lambda c0: pl.BlockSpec((l, HEAD), lambda h, i: (0, c0 + h))
    return pl.pallas_call(
        body, name="sb_bwd", grid=(nh, nq),
        in_specs=[blk(qc), head(kc), head(vc), blk(zc), blk(nh), blk(0),
                  pl.BlockSpec((None, HEAD, 1), lambda h, i: (h, i, 0))],
        out_specs=[blk(0), head(0), head(0), blk(0)],
        out_shape=[jax.ShapeDtypeStruct((l, wb), bf16)] * 4,
        scratch_shapes=[pltpu.VMEM((l, HEAD), f32), pltpu.VMEM((l, HEAD), f32)],
        compiler_params=_cparams("parallel", "arbitrary"),
    )(proj, proj, proj, proj, dcat, att, tot)


def _disc(lr, li, ldt):
    dt = jnp.exp(ldt)
    mag = jnp.exp(lr * dt)
    a_re = mag * jnp.cos(li * dt)
    a_im = mag * jnp.sin(li * dt)
    den = lr * lr + li * li
    nr = a_re - 1.0
    return a_re, a_im, (nr * lr + a_im * li) / den, (a_im * lr - nr * li) / den


def s5_params_fwd(lr, li, ldt, bt_re, bt_im):
    g, c, p = bt_re.shape

    def body(lr_ref, li_ref, ldt_ref, br_ref, bi_ref, ar_ref, ai_ref, bbr_ref, bbi_ref):
        a_re, a_im, cr, ci = _disc(lr_ref[...], li_ref[...], ldt_ref[...])
        ar_ref[...] = a_re
        ai_ref[...] = a_im
        for k in range(c):
            br, bi = br_ref[:, k, :], bi_ref[:, k, :]
            bbr_ref[:, k, :] = cr * br - ci * bi
            bbi_ref[:, k, :] = cr * bi + ci * br

    return pl.pallas_call(
        body, name="s5_params_fwd",
        out_shape=[jax.ShapeDtypeStruct((g, p), f32)] * 2 + [jax.ShapeDtypeStruct((g, c, p), f32)] * 2,
    )(lr, li, ldt, bt_re, bt_im)


def s5_params_bwd(lr, li, ldt, bt_re, bt_im, da_re, da_im, dbbt_re, dbbt_im):
    g, c, p = bt_re.shape

    def body(lr_ref, li_ref, ldt_ref, br_ref, bi_ref, dar_ref, dai_ref, dbbr_ref, dbbi_ref,
             dlr_ref, dli_ref, dldt_ref, dbr_ref, dbi_ref):
        (a_re, a_im, cr, ci), vjp = jax.vjp(_disc, lr_ref[...], li_ref[...], ldt_ref[...])
        dcr = jnp.zeros((g, p), f32)
        dci = jnp.zeros((g, p), f32)
        for k in range(c):
            br, bi = br_ref[:, k, :], bi_ref[:, k, :]
            dr, di = dbbr_ref[:, k, :], dbbi_ref[:, k, :]
            dcr += dr * br + di * bi
            dci += di * br - dr * bi
            dbr_ref[:, k, :] = cr * dr + ci * di
            dbi_ref[:, k, :] = cr * di - ci * dr
        dlr, dli, dldt = vjp((dar_ref[...], dai_ref[...], dcr, dci))
        dlr_ref[...] = dlr
        dli_ref[...] = dli
        dldt_ref[...] = dldt

    return pl.pallas_call(
        body, name="s5_params_bwd",
        out_shape=[jax.ShapeDtypeStruct((g, p), f32)] * 2 + [jax.ShapeDtypeStruct((g, 1), f32)]
        + [jax.ShapeDtypeStruct((g, c, p), f32)] * 2,
    )(lr, li, ldt, bt_re, bt_im, da_re, da_im, dbbt_re, dbbt_im)


def _cmul(ar, ai, br, bi):
    return ar * br - ai * bi, ar * bi + ai * br


def _power_tables(ar, ai):
    rows = lax.broadcasted_iota(jnp.int32, (SUBLANES, ar.shape[1]), 0)
    pr = jnp.zeros((SUBLANES, ar.shape[1]), f32)
    pi = jnp.zeros((SUBLANES, ar.shape[1]), f32)
    cr, ci = ar, ai
    pows = {}
    for r in range(SUBLANES):
        pows[r + 1] = (cr, ci)
        pr = jnp.where(rows == r, cr, pr)
        pi = jnp.where(rows == r, ci, pi)
        cr, ci = _cmul(cr, ci, ar, ai)
    return [pows[1], pows[2], pows[4]], pr, pi


def _ssm_time_tile(l):
    return _tile(l, (512, 256, 128))


def ssm_fwd(u, bre3, bim3, cre3, cimn3, a_re, a_im, d_skip):
    l, w = u.shape
    nj = w // HEAD
    ns = STATES_PER_LANE_BLOCK
    tt = _ssm_time_tile(l)

    def body(u_ref, bre_ref, bim_ref, cre_ref, cim_ref, ar_ref, ai_ref, d_ref, y_ref, hr_ref, hi_ref, cr_ref, ci_ref):
        @pl.when(pl.program_id(1) == 0)
        def _():
            cr_ref[...] = jnp.zeros_like(cr_ref)
            ci_ref[...] = jnp.zeros_like(ci_ref)

        uv = u_ref[...]
        hr_ref[...] = _dot(uv, bre_ref[...])
        hi_ref[...] = _dot(uv, bim_ref[...])
        steps, pr, pi = _power_tables(ar_ref[...], ai_ref[...])
        rows = lax.broadcasted_iota(jnp.int32, (SUBLANES, ns), 0)

        def blk(b, carry):
            cr, ci = carry
            sl = pl.ds(pl.multiple_of(b * SUBLANES, SUBLANES), SUBLANES)
            xr, xi = hr_ref[sl, :], hi_ref[sl, :]
            for d, (sr_, si_) in zip((1, 2, 4), steps):
                keep = rows >= d
                qr = jnp.where(keep, pltpu.roll(xr, d, axis=0), 0.0)
                qi = jnp.where(keep, pltpu.roll(xi, d, axis=0), 0.0)
                mr, mi = _cmul(sr_, si_, qr, qi)
                xr, xi = xr + mr, xi + mi
            mr, mi = _cmul(pr, pi, cr, ci)
            xr, xi = xr + mr, xi + mi
            hr_ref[sl, :] = xr
            hi_ref[sl, :] = xi
            return xr[SUBLANES - 1:, :], xi[SUBLANES - 1:, :]

        cr, ci = lax.fori_loop(0, tt // SUBLANES, blk, (cr_ref[...], ci_ref[...]))
        cr_ref[...] = cr
        ci_ref[...] = ci
        y = _dot(hr_ref[...].astype(bf16), cre_ref[...]) + _dot(hi_ref[...].astype(bf16), cim_ref[...])
        y_ref[...] = y + d_ref[...] * uv.astype(f32)

    lane = pl.BlockSpec((tt, HEAD), lambda j, i: (i, j))
    st = pl.BlockSpec((tt, ns), lambda j, i: (i, j))
    b3 = pl.BlockSpec((None, HEAD, ns), lambda j, i: (j, 0, 0))
    c3 = pl.BlockSpec((None, ns, HEAD), lambda j, i: (j, 0, 0))
    arow = pl.BlockSpec((1, ns), lambda j, i: (0, j))
    return pl.pallas_call(
        body, name="ssm_fwd", grid=(nj, l // tt),
        in_specs=[lane, b3, b3, c3, c3, arow, arow, pl.BlockSpec((1, HEAD), lambda j, i: (0, j))],
        out_specs=[lane, st, st],
        out_shape=[jax.ShapeDtypeStruct((l, w), f32), jax.ShapeDtypeStruct((l, nj * ns), f32),
                   jax.ShapeDtypeStruct((l, nj * ns), f32)],
        scratch_shapes=[pltpu.VMEM((1, ns), f32), pltpu.VMEM((1, ns), f32)],
        compiler_params=_cparams("parallel", "arbitrary"),
    )(u, bre3, bim3, cre3, cimn3, a_re, a_im, d_skip)


def ssm_bwd(dy, u, h_re, h_im, bre3, bim3, cre3, cimn3, a_re, a_im, d_skip):
    l, w = u.shape
    nj = w // HEAD
    ns = STATES_PER_LANE_BLOCK
    tt = _ssm_time_tile(l)
    nt = l // tt

    def body(dy_ref, u_ref, hr_ref, hi_ref, bre_ref, bim_ref, cre_ref, cim_ref, ar_ref, ai_ref, d_ref,
             du_ref, dd_ref, dar_ref, dai_ref, dbre_ref, dbim_ref, dcre_ref, dcim_ref, kr_ref, ki_ref, cr_ref, ci_ref,
             accr_ref, acci_ref):
        i = pl.program_id(1)

        @pl.when(i == 0)
        def _():
            for ref in (cr_ref, ci_ref, accr_ref, acci_ref, dd_ref, dbre_ref, dbim_ref, dcre_ref, dcim_ref):
                ref[...] = jnp.zeros_like(ref)

        dyv = dy_ref[...]
        dyb = dyv.astype(bf16)
        uv = u_ref[...]
        kr_ref[...] = _dot_nt(dyb, cre_ref[...])
        ki_ref[...] = _dot_nt(dyb, cim_ref[...])
        steps, pr, pi = _power_tables(ar_ref[...], -ai_ref[...])
        rows = lax.broadcasted_iota(jnp.int32, (SUBLANES, ns), 0)
        qr = jnp.zeros((SUBLANES, ns), f32)
        qi = jnp.zeros((SUBLANES, ns), f32)
        for r in range(SUBLANES):
            qr = jnp.where(rows == r, pr[SUBLANES - 1 - r:SUBLANES - r, :], qr)
            qi = jnp.where(rows == r, pi[SUBLANES - 1 - r:SUBLANES - r, :], qi)
        nb = tt // SUBLANES

        def blk(t, carry):
            cr, ci, accr, acci = carry
            sl = pl.ds(pl.multiple_of((nb - 1 - t) * SUBLANES, SUBLANES), SUBLANES)
            xr, xi = kr_ref[sl, :], ki_ref[sl, :]
            for d, (sr_, si_) in zip((1, 2, 4), steps):
                keep = rows < SUBLANES - d
                zr = jnp.where(keep, pltpu.roll(xr, SUBLANES - d, axis=0), 0.0)
                zi = jnp.where(keep, pltpu.roll(xi, SUBLANES - d, axis=0), 0.0)
                mr, mi = _cmul(sr_, si_, zr, zi)
                xr, xi = xr + mr, xi + mi
            mr, mi = _cmul(qr, qi, cr, ci)
            xr, xi = xr + mr, xi + mi
            kr_ref[sl, :] = xr
            ki_ref[sl, :] = xi
            last = rows == SUBLANES - 1
            nr = jnp.where(last, cr, pltpu.roll(xr, SUBLANES - 1, axis=0))
            ni = jnp.where(last, ci, pltpu.roll(xi, SUBLANES - 1, axis=0))
            hr, hi = hr_ref[sl, :], hi_ref[sl, :]
            accr = accr + nr * hr + ni * hi
            acci = acci + ni * hr - nr * hi
            return xr[:1, :], xi[:1, :], accr, acci

        cr, ci, accr, acci = lax.fori_loop(0, nb, blk, (cr_ref[...], ci_ref[...], accr_ref[...], acci_ref[...]))
        cr_ref[...] = cr
        ci_ref[...] = ci
        accr_ref[...] = accr
        acci_ref[...] = acci
        kr, ki = kr_ref[...].astype(bf16), ki_ref[...].astype(bf16)
        du = _dot_nt(kr, bre_ref[...]) + _dot_nt(ki, bim_ref[...]) + d_ref[...] * dyv
        du_ref[...] = du.astype(du_ref.dtype)
        dd_ref[...] += jnp.sum(dyv * uv.astype(f32), axis=0, keepdims=True)
        dbre_ref[...] += _dot_tn(uv, kr)
        dbim_ref[...] += _dot_tn(uv, ki)
        dcre_ref[...] += _dot_tn(hr_ref[...].astype(bf16), dyb)
        dcim_ref[...] += _dot_tn(hi_ref[...].astype(bf16), dyb)

        @pl.when(i == nt - 1)
        def _():
            dar_ref[...] = jnp.sum(accr_ref[...], axis=0, keepdims=True)
            dai_ref[...] = jnp.sum(acci_ref[...], axis=0, keepdims=True)

    lane = pl.BlockSpec((tt, HEAD), lambda j, i: (nt - 1 - i, j))
    st = pl.BlockSpec((tt, ns), lambda j, i: (nt - 1 - i, j))
    b3 = pl.BlockSpec((None, HEAD, ns), lambda j, i: (j, 0, 0))
    c3 = pl.BlockSpec((None, ns, HEAD), lambda j, i: (j, 0, 0))
    arow = pl.BlockSpec((1, ns), lambda j, i: (0, j))
    drow = pl.BlockSpec((1, HEAD), lambda j, i: (0, j))
    return pl.pallas_call(
        body, name="ssm_bwd", grid=(nj, nt),
        in_specs=[lane, lane, st, st, b3, b3, c3, c3, arow, arow, drow],
        out_specs=[lane, drow, arow, arow, b3, b3, c3, c3],
        out_shape=[jax.ShapeDtypeStruct((l, w), bf16), jax.ShapeDtypeStruct((1, w), f32),
                   jax.ShapeDtypeStruct((1, nj * ns), f32), jax.ShapeDtypeStruct((1, nj * ns), f32),
                   jax.ShapeDtypeStruct((nj, HEAD, ns), f32), jax.ShapeDtypeStruct((nj, HEAD, ns), f32),
                   jax.ShapeDtypeStruct((nj, ns, HEAD), f32), jax.ShapeDtypeStruct((nj, ns, HEAD), f32)],
        scratch_shapes=[pltpu.VMEM((tt, ns), f32), pltpu.VMEM((tt, ns), f32), pltpu.VMEM((1, ns), f32),
                        pltpu.VMEM((1, ns), f32), pltpu.VMEM((SUBLANES, ns), f32), pltpu.VMEM((SUBLANES, ns), f32)],
        compiler_params=_cparams("parallel", "arbitrary"),
    )(dy, u, h_re, h_im, bre3, bim3, cre3, cimn3, a_re, a_im, d_skip)


def glu_fwd(y, z_src, w_glu, b_glu):
    l, w = y.shape
    tm = _row_tile(l)

    def body(y_ref, z_ref, w_ref, b_ref, g_ref, t_ref, o_ref):
        g = _gelu(y_ref[...])
        gb = g.astype(bf16)
        t = _dot(gb, w_ref[...]) + b_ref[...]
        g_ref[...] = gb
        t_ref[...] = t
        o_ref[...] = (g * jax.nn.sigmoid(t) * _silu(z_ref[...].astype(f32))).astype(o_ref.dtype)

    blk = pl.BlockSpec((tm, w), lambda i: (i, 0))
    return pl.pallas_call(
        body, name="glu_fwd", grid=(l // tm,),
        in_specs=[blk, pl.BlockSpec((tm, w), lambda i: (i, 1)), pl.BlockSpec((w, w), lambda i: (0, 0)), _row(w)],
        out_specs=[blk, blk, blk],
        out_shape=[jax.ShapeDtypeStruct((l, w), bf16), jax.ShapeDtypeStruct((l, w), f32),
                   jax.ShapeDtypeStruct((l, w), bf16)],
        compiler_params=_cparams("parallel"),
    )(y, z_src, w_glu, b_glu)


def glu_bwd(dout, y, t, z_src, w_glu):
    l, w = y.shape
    tm = _row_tile(l)

    def body(do_ref, y_ref, t_ref, z_ref, w_ref, dy_ref, dz_ref, dt_ref, db_ref):
        @pl.when(pl.program_id(0) == 0)
        def _():
            db_ref[...] = jnp.zeros_like(db_ref)

        yv, zv, dov = y_ref[...], z_ref[...].astype(f32), do_ref[...]
        g = _gelu(yv)
        sg = jax.nn.sigmoid(t_ref[...])
        dy2 = dov * _silu(zv)
        dz_ref[...] = (dov * g * sg * _silu_grad(zv)).astype(dz_ref.dtype)
        dt = dy2 * g * sg * (1.0 - sg)
        dtb = dt.astype(bf16)
        dt_ref[...] = dtb
        db_ref[...] += jnp.sum(dt, axis=0, keepdims=True)
        dg = dy2 * sg + _dot_nt(dtb, w_ref[...])
        dy_ref[...] = dg * _gelu_grad(yv)

    blk = pl.BlockSpec((tm, w), lambda i: (i, 0))
    return pl.pallas_call(
        body, name="glu_bwd", grid=(l // tm,),
        in_specs=[blk, blk, blk, pl.BlockSpec((tm, w), lambda i: (i, 1)), pl.BlockSpec((w, w), lambda i: (0, 0))],
        out_specs=[blk, blk, blk, _row(w)],
        out_shape=[jax.ShapeDtypeStruct((l, w), f32), jax.ShapeDtypeStruct((l, w), bf16),
                   jax.ShapeDtypeStruct((l, w), bf16), jax.ShapeDtypeStruct((1, w), f32)],
        compiler_params=_cparams("arbitrary"),
    )(dout, y, t, z_src, w_glu)


def _adamw(w, g, m, v):
    m = ADAM_B1 * m + (1.0 - ADAM_B1) * g
    v = ADAM_B2 * v + (1.0 - ADAM_B2) * (g * g)
    m_hat = m / (1.0 - ADAM_B1 ** ADAM_STEP)
    v_hat = v / (1.0 - ADAM_B2 ** ADAM_STEP)
    return -ADAM_LR * (m_hat / (jnp.sqrt(v_hat) + ADAM_EPS) + ADAM_WD * w), m, v


def adam_reduce(pieces, w, m, v, name):
    r, c = w.shape
    tr = _tile(r, (256, 128, 64, 32, 16, 8))

    def body(p_ref, w_ref, m_ref, v_ref, g_ref, d_ref, nm_ref, nv_ref):
        g = p_ref[0].astype(f32)
        for s in range(1, N_DEV):
            g = g + p_ref[s].astype(f32)
        g_ref[...] = g
        d_ref[...], nm_ref[...], nv_ref[...] = _adamw(w_ref[...], g, m_ref[...], v_ref[...])

    blk = pl.BlockSpec((tr, c), lambda i: (i, 0))
    return pl.pallas_call(
        body, name=name, grid=(r // tr,),
        in_specs=[pl.BlockSpec((N_DEV, tr, c), lambda i: (0, i, 0)), blk, blk, blk],
        out_specs=[blk] * 4, out_shape=[jax.ShapeDtypeStruct((r, c), f32)] * 4,
        compiler_params=_cparams("parallel"),
    )(pieces, w, m, v)


def adam_w_mod(cond_t, dm, w, m, v):
    nl, d, cols = w.shape
    tr = _tile(d, (512, 256, 128))

    def body(c_ref, dm_ref, w_ref, m_ref, v_ref, g_ref, d_ref, nm_ref, nv_ref):
        g = jnp.dot(c_ref[...], dm_ref[...], preferred_element_type=f32, precision=lax.Precision.HIGHEST)
        g_ref[...] = g
        d_ref[...], nm_ref[...], nv_ref[...] = _adamw(w_ref[...], g, m_ref[...], v_ref[...])

    blk = pl.BlockSpec((None, tr, cols), lambda l, i: (l, i, 0))
    return pl.pallas_call(
        body, name="adam_w_mod", grid=(nl, d // tr),
        in_specs=[pl.BlockSpec((tr, N_DEV), lambda l, i: (i, 0)), pl.BlockSpec((None, N_DEV, cols), lambda l, i: (l, 0, 0)),
                  blk, blk, blk],
        out_specs=[blk] * 4, out_shape=[jax.ShapeDtypeStruct((nl, d, cols), f32)] * 4,
        compiler_params=_cparams("parallel", "parallel"),
    )(cond_t, dm, w, m, v)


def silu_rows(c_all):
    def body(c_ref, o_ref):
        o_ref[...] = _silu(c_ref[...])

    return pl.pallas_call(body, name="silu_rows", out_shape=jax.ShapeDtypeStruct(c_all.shape, f32))(c_all)


def _block_diag(x):
    g, a, b = x.shape
    nj = g // GROUPS_PER_LANE_BLOCK
    eye = jnp.eye(GROUPS_PER_LANE_BLOCK, dtype=x.dtype)
    x5 = x.reshape(nj, GROUPS_PER_LANE_BLOCK, a, b)
    return jnp.einsum("jgab,gh->jgahb", x5, eye).reshape(nj, GROUPS_PER_LANE_BLOCK * a, GROUPS_PER_LANE_BLOCK * b)


def _diag_blocks(x, a, b):
    nj = x.shape[0]
    x5 = x.reshape(nj, GROUPS_PER_LANE_BLOCK, a, GROUPS_PER_LANE_BLOCK, b)
    eye = jnp.eye(GROUPS_PER_LANE_BLOCK, dtype=x.dtype)
    return jnp.einsum("jgahb,gh->jgab", x5, eye).reshape(nj * GROUPS_PER_LANE_BLOCK, a, b)


PACK_ROW = SUBLANES * HEAD


def _pack(parts):
    rows = []
    for p in parts:
        flat = p.reshape(-1)
        pad = (-flat.shape[0]) % PACK_ROW
        if pad:
            flat = jnp.concatenate([flat, jnp.zeros((pad,), flat.dtype)])
        rows.append(flat.reshape(-1, HEAD))
    return jnp.concatenate(rows, axis=0)


def _unpack(packed, shapes):
    out, r0 = [], 0
    for shp in shapes:
        n = math.prod(shp)
        nr = -(-n // PACK_ROW) * SUBLANES
        out.append(packed[r0:r0 + nr].reshape(-1)[:n].reshape(shp))
        r0 += nr
    return out


def adam_small(gathered, w, m, v):
    r, c = w.shape

    def body(p_ref, w_ref, m_ref, v_ref, g_ref, d_ref, nm_ref, nv_ref):
        g = p_ref[0]
        for s in range(1, N_DEV):
            g = g + p_ref[s]
        g_ref[...] = g
        d_ref[...], nm_ref[...], nv_ref[...] = _adamw(w_ref[...], g, m_ref[...], v_ref[...])

    return pl.pallas_call(
        body, name="adam_small", out_shape=[jax.ShapeDtypeStruct((r, c), f32)] * 4,
        compiler_params=_cparams(),
    )(gathered, w, m, v)


def kernel(x, c, ln_pre_g, ln_post_g, w_mod, b_mod, w_in_ab, w_out_ab, sgu_norm_g, sgu_w, sgu_b, w_in_ssm, w_out_ssm, lam_re, lam_im, b_re, b_im, c_re, c_im, d_skip, log_dt, w_glu, b_glu, loss_target, m_ln_pre_g, m_ln_post_g, m_w_mod, m_b_mod, m_w_in_ab, m_w_out_ab, m_sgu_norm_g, m_sgu_w, m_sgu_b, m_w_in_ssm, m_w_out_ssm, m_lam_re, m_lam_im, m_b_re, m_b_im, m_c_re, m_c_im, m_d_skip, m_log_dt, m_w_glu, m_b_glu, v_ln_pre_g, v_ln_post_g, v_w_mod, v_b_mod, v_w_in_ab, v_w_out_ab, v_sgu_norm_g, v_sgu_w, v_sgu_b, v_w_in_ssm, v_w_out_ssm, v_lam_re, v_lam_im, v_b_re, v_b_im, v_c_re, v_c_im, v_d_skip, v_log_dt, v_w_glu, v_b_glu):
    me = _my_index()
    x0 = x[0]
    l, d = x0.shape
    target = loss_target[0]
    nh = sgu_w.shape[1]
    wa = nh * HEAD
    n_grp, n_st = lam_re.shape[1], lam_re.shape[2]
    mod_cols = w_mod.shape[2]

    gathered = all_gather(
        [c, w_in_ab[0].astype(bf16), w_out_ab[0].astype(bf16), w_in_ssm[0].astype(bf16), w_out_ssm[0].astype(bf16),
         w_glu[0].astype(bf16), d_skip, b_glu], "gather_weights")
    c_all = gathered[0].reshape(N_DEV, d)
    win_ab3 = gathered[1]
    wout_ab3 = gathered[2].reshape(1, d, d)
    win_ssm3 = gathered[3].reshape(1, d, d)
    wout_ssm3 = gathered[4]
    wglu = gathered[5].reshape(w_glu.shape[2], w_glu.shape[2])
    d_skip_all = gathered[6].reshape(1, -1)
    b_glu_all = gathered[7].reshape(1, -1)

    b_cols = lax.dynamic_slice_in_dim(b_mod, me * mod_cols, mod_cols, axis=1)
    (mod_all,) = all_gather([mod_part(c_all, w_mod, b_cols)], "gather_mod")
    mod_mine = lax.dynamic_index_in_dim(mod_all, me, axis=2, keepdims=False)
    mod_rows = jnp.transpose(mod_mine, (1, 0, 2)).reshape(2, 3, 1, d)

    def rows(a, i):
        return a[i].reshape(1, d)

    shift0, scale0, gate0 = mod_rows[0, 0], mod_rows[0, 1], mod_rows[0, 2]
    h0 = prenorm_fwd(x0, rows(ln_pre_g, 0), shift0, scale0, "prenorm0")
    proj0 = mm_nn(h0, win_ab3, bf16, "proj0")
    sgu_b3 = sgu_b[0].reshape(nh, HEAD, 1)
    out_a = sgu_fwd(proj0, sgu_norm_g, sgu_w[0], sgu_b3)
    out_b, att, tot = sb_fwd(proj0, nh)
    cat = jnp.concatenate([out_a, out_b], axis=1)
    y0 = mm_nn(cat, wout_ab3, f32, "out0")
    x1 = post_fwd(x0, y0, gate0, rows(ln_post_g, 0), "post0")

    shift1, scale1, gate1 = mod_rows[1, 0], mod_rows[1, 1], mod_rows[1, 2]
    h1 = prenorm_fwd(x1, rows(ln_pre_g, 1), shift1, scale1, "prenorm1")
    proj1 = mm_nn(h1, win_ssm3, bf16, "proj1")
    w_ssm = proj1.shape[1] // 2
    ldt = log_dt[0].reshape(n_grp, 1)
    bt_re = jnp.transpose(b_re[0], (0, 2, 1))
    bt_im = jnp.transpose(b_im[0], (0, 2, 1))
    a_re, a_im, bbt_re, bbt_im = s5_params_fwd(lam_re[0], lam_im[0], ldt, bt_re, bt_im)
    bre3 = _block_diag(bbt_re).astype(bf16)
    bim3 = _block_diag(bbt_im).astype(bf16)
    cre3 = _block_diag(jnp.transpose(c_re[0], (0, 2, 1))).astype(bf16)
    cimn3 = _block_diag(-jnp.transpose(c_im[0], (0, 2, 1))).astype(bf16)
    a_re_row, a_im_row = a_re.reshape(1, -1), a_im.reshape(1, -1)
    u = proj1[:, :w_ssm]
    y_ssm, hs_re, hs_im = ssm_fwd(u, bre3, bim3, cre3, cimn3, a_re_row, a_im_row, d_skip_all)
    g_act, t_glu, mix1 = glu_fwd(y_ssm, proj1, wglu, b_glu_all)
    y1 = mm_nn(mix1, wout_ssm3, f32, "out1")

    dx2, loss_tile = final_loss(x1, y1, gate1, rows(ln_post_g, 1), target)
    loss = lax.psum(loss_tile[0, 0] * (0.5 / d), ("x", "y", "c"))

    dy1, dgate1, dgpost1 = post_bwd(dx2, y1, gate1, rows(ln_post_g, 1), "post1_bwd")
    dmix1 = mm_nt(dy1, wout_ssm3, f32, "dmix1")
    gw_out_ssm = mm_tn(mix1, dy1, N_DEV, bf16, "gw_out_ssm")
    dy_ssm, dz1, dt_glu, db_glu = glu_bwd(dmix1, y_ssm, t_glu, proj1, wglu)
    gw_glu = mm_tn(g_act, dt_glu, 1, bf16, "gw_glu").reshape(N_DEV, -1, w_ssm)
    du, dd_skip, da_re, da_im, dbre3, dbim3, dcre3, dcimn3 = ssm_bwd(
        dy_ssm, u, hs_re, hs_im, bre3, bim3, cre3, cimn3, a_re_row, a_im_row, d_skip_all)
    dproj1 = jnp.concatenate([du, dz1], axis=1)
    dh1 = mm_nt(dproj1, win_ssm3, f32, "dh1")
    gw_in_ssm = mm_tn(h1, dproj1, 1, bf16, "gw_in_ssm").reshape(N_DEV, -1, proj1.shape[1])
    dx1, dshift1, dscale1, dgpre1 = prenorm_bwd(dh1, x1, dx2, rows(ln_pre_g, 1), scale1, "prenorm1_bwd")
    dlr, dli, dldt, dbt_re, dbt_im = s5_params_bwd(
        lam_re[0], lam_im[0], ldt, bt_re, bt_im, da_re.reshape(n_grp, n_st), da_im.reshape(n_grp, n_st),
        _diag_blocks(dbre3, SSM_GROUP, n_st), _diag_blocks(dbim3, SSM_GROUP, n_st))
    g_b_re = jnp.transpose(dbt_re, (0, 2, 1))
    g_b_im = jnp.transpose(dbt_im, (0, 2, 1))
    g_c_re = jnp.transpose(_diag_blocks(dcre3, n_st, SSM_GROUP), (0, 2, 1))
    g_c_im = -jnp.transpose(_diag_blocks(dcimn3, n_st, SSM_GROUP), (0, 2, 1))

    dy0, dgate0, dgpost0 = post_bwd(dx1, y0, gate0, rows(ln_post_g, 0), "post0_bwd")
    dcat = mm_nt(dy0, wout_ab3, f32, "dcat")
    gw_out_ab = mm_tn(cat, dy0, 1, bf16, "gw_out_ab").reshape(N_DEV, -1, d)
    da, dsgu_w, dsgu_b, dsgu_ng = sgu_bwd(proj0, dcat, sgu_norm_g, sgu_w[0], sgu_b3)
    dq, dk, dv, dbz = sb_bwd(proj0, dcat, att, tot, nh)
    dproj0 = jnp.concatenate([da, dq, dk, dv, dbz], axis=1)
    dh0 = mm_nt(dproj0, win_ab3, f32, "dh0")
    gw_in_ab = mm_tn(h0, dproj0, N_DEV, bf16, "gw_in_ab")
    dx0, dshift0, dscale0, dgpre0 = prenorm_bwd(dh0, x0, dx1, rows(ln_pre_g, 0), scale0, "prenorm0_bwd")

    pieces = all_to_all(
        [gw_in_ab, gw_out_ab, gw_in_ssm, gw_out_ssm, gw_glu,
         dd_skip.reshape(N_DEV, 1, -1), db_glu.reshape(N_DEV, 1, -1)], "scatter_grads")

    def sharded(p, w, m, v, name):
        shp = w.shape
        w2, m2, v2 = (a.reshape(-1, shp[-1]) for a in (w, m, v))
        return [o.reshape(shp) for o in adam_reduce(p.reshape(N_DEV, -1, shp[-1]), w2, m2, v2, name)]

    r_w_in_ab = sharded(pieces[0], w_in_ab, m_w_in_ab, v_w_in_ab, "adam_w_in_ab")
    r_w_out_ab = sharded(pieces[1], w_out_ab, m_w_out_ab, v_w_out_ab, "adam_w_out_ab")
    r_w_in_ssm = sharded(pieces[2], w_in_ssm, m_w_in_ssm, v_w_in_ssm, "adam_w_in_ssm")
    r_w_out_ssm = sharded(pieces[3], w_out_ssm, m_w_out_ssm, v_w_out_ssm, "adam_w_out_ssm")
    r_w_glu = sharded(pieces[4], w_glu, m_w_glu, v_w_glu, "adam_w_glu")
    r_d_skip = sharded(pieces[5], d_skip, m_d_skip, v_d_skip, "adam_d_skip")
    r_b_glu = sharded(pieces[6], b_glu, m_b_glu, v_b_glu, "adam_b_glu")

    small_names = ["ln_pre_g", "ln_post_g", "b_mod", "sgu_norm_g", "sgu_w", "sgu_b", "lam_re", "lam_im", "b_re", "b_im",
                   "c_re", "c_im", "log_dt"]
    small_w = [ln_pre_g, ln_post_g, b_mod, sgu_norm_g, sgu_w, sgu_b, lam_re, lam_im, b_re, b_im, c_re, c_im, log_dt]
    small_m = [m_ln_pre_g, m_ln_post_g, m_b_mod, m_sgu_norm_g, m_sgu_w, m_sgu_b, m_lam_re, m_lam_im, m_b_re, m_b_im,
               m_c_re, m_c_im, m_log_dt]
    small_v = [v_ln_pre_g, v_ln_post_g, v_b_mod, v_sgu_norm_g, v_sgu_w, v_sgu_b, v_lam_re, v_lam_im, v_b_re, v_b_im,
               v_c_re, v_c_im, v_log_dt]
    dmod = jnp.concatenate([dshift0, dscale0, dgate0, dshift1, dscale1, dgate1], axis=1)
    small_g = [jnp.concatenate([dgpre0, dgpre1]), jnp.concatenate([dgpost0, dgpost1]), dmod, dsgu_ng, dsgu_w, dsgu_b,
               dlr, dli, g_b_re, g_b_im, g_c_re, g_c_im, dldt]
    shapes = [w.shape for w in small_w]
    (g_all,) = all_gather([_pack(small_g)], "gather_small_grads")
    r_small = [_unpack(o, shapes) for o in adam_small(g_all, _pack(small_w), _pack(small_m), _pack(small_v))]
    small = {n: [r_small[k][i] for k in range(4)] for i, n in enumerate(small_names)}

    n_rows_before = sum(-(-math.prod(s) // PACK_ROW) * SUBLANES for s in shapes[:2])
    n_rows = math.prod(b_mod.shape) // HEAD
    dmod_all = g_all[:, n_rows_before:n_rows_before + n_rows].reshape(N_DEV, 2, 3 * d)
    dm_cols = jnp.transpose(lax.dynamic_slice_in_dim(dmod_all, me * mod_cols, mod_cols, axis=2), (1, 0, 2))
    cond_t = jnp.transpose(silu_rows(c_all))
    r_w_mod = adam_w_mod(cond_t, dm_cols, w_mod, m_w_mod, v_w_mod)

    res = dict(small)
    res.update(w_mod=r_w_mod, w_in_ab=r_w_in_ab, w_out_ab=r_w_out_ab, w_in_ssm=r_w_in_ssm, w_out_ssm=r_w_out_ssm,
               d_skip=r_d_skip, w_glu=r_w_glu, b_glu=r_b_glu)
    order = ["ln_pre_g", "ln_post_g", "w_mod", "b_mod", "w_in_ab", "w_out_ab", "sgu_norm_g", "sgu_w", "sgu_b", "w_in_ssm",
             "w_out_ssm", "lam_re", "lam_im", "b_re", "b_im", "c_re", "c_im", "d_skip", "log_dt", "w_glu", "b_glu"]
    outs = [loss, dx0.reshape(x.shape)]
    for k in range(4):
        outs += [res[n][k] for n in order]
    return tuple(outs)
```

```python
import functools
import math

import jax
import jax.numpy as jnp
from jax import lax
from jax.experimental import pallas as pl
from jax.experimental.pallas import tpu as pltpu

f32 = jnp.float32
bf16 = jnp.bfloat16

N_DEV = 8
EPS = 1e-6
HEAD = 128
SUBLANES = 8
SSM_GROUP = 16
SSM_STATE = 64
GROUPS_PER_LANE_BLOCK = HEAD // SSM_GROUP
STATES_PER_LANE_BLOCK = GROUPS_PER_LANE_BLOCK * SSM_STATE
VMEM_LIMIT = 56 * 2 ** 20
ADAM_LR, ADAM_B1, ADAM_B2, ADAM_EPS, ADAM_WD, ADAM_STEP = 0.001, 0.9, 0.999, 1e-08, 0.01, 10
_GELU_C0 = math.sqrt(2.0 / math.pi)
_GELU_C1 = 0.044715
MESH = pl.DeviceIdType.MESH


def _cparams(*sem):
    return pltpu.CompilerParams(dimension_semantics=sem if sem else None, vmem_limit_bytes=VMEM_LIMIT)


def _gelu(x):
    return 0.5 * x * (1.0 + jnp.tanh(_GELU_C0 * (x + _GELU_C1 * x * x * x)))


def _gelu_grad(x):
    t = jnp.tanh(_GELU_C0 * (x + _GELU_C1 * x * x * x))
    return 0.5 * (1.0 + t) + 0.5 * x * (1.0 - t * t) * _GELU_C0 * (1.0 + 3.0 * _GELU_C1 * x * x)


def _silu(x):
    return x * jax.nn.sigmoid(x)


def _silu_grad(x):
    s = jax.nn.sigmoid(x)
    return s * (1.0 + x * (1.0 - s))


def _dot(a, b):
    return jnp.dot(a, b, preferred_element_type=f32)


def _dot_nt(a, b):
    return lax.dot_general(a, b, (((1,), (1,)), ((), ())), preferred_element_type=f32)


def _dot_tn(a, b):
    return lax.dot_general(a, b, (((0,), (0,)), ((), ())), preferred_element_type=f32)


def _split_bf16(v):
    hi = v.astype(bf16)
    lo = (v - hi.astype(f32)).astype(bf16)
    return hi, lo


def _row(d):
    return pl.BlockSpec((1, d), lambda *_: (0, 0))


def _my_index():
    return 4 * lax.axis_index("x") + 2 * lax.axis_index("y") + lax.axis_index("c")


def _peer(k):
    x, y, c = lax.axis_index("x"), lax.axis_index("y"), lax.axis_index("c")
    return (1 - x if k & 4 else x, 1 - y if k & 2 else y, 1 - c if k & 1 else c)


def all_gather(arrs, name):
    n = len(arrs)

    def body(*refs):
        ins, outs = refs[:n], refs[n:2 * n]
        send, recv, local = refs[2 * n:]
        me = _my_index()
        copies = []
        for a in range(n):
            cp = pltpu.make_async_copy(ins[a], outs[a].at[me], local.at[a])
            cp.start()
            copies.append(cp)
            for k in range(1, N_DEV):
                s = a * (N_DEV - 1) + k - 1
                cp = pltpu.make_async_remote_copy(src_ref=ins[a], dst_ref=outs[a].at[me], send_sem=send.at[s],
                                                  recv_sem=recv.at[s], device_id=_peer(k), device_id_type=MESH)
                cp.start()
                copies.append(cp)
        for cp in copies:
            cp.wait()

    any_spec = pl.BlockSpec(memory_space=pl.ANY)
    outs = pl.pallas_call(
        body, name=name,
        out_shape=[jax.ShapeDtypeStruct((N_DEV,) + a.shape, a.dtype) for a in arrs],
        in_specs=[any_spec] * n, out_specs=[any_spec] * n,
        scratch_shapes=[pltpu.SemaphoreType.DMA((n * (N_DEV - 1),)), pltpu.SemaphoreType.DMA((n * (N_DEV - 1),)),
                        pltpu.SemaphoreType.DMA((n,))],
        compiler_params=pltpu.CompilerParams(has_side_effects=True),
    )(*arrs)
    return list(outs)


def all_to_all(arrs, name):
    n = len(arrs)

    def body(*refs):
        ins, outs = refs[:n], refs[n:2 * n]
        send, recv, local = refs[2 * n:]
        me = _my_index()
        copies = []
        for a in range(n):
            cp = pltpu.make_async_copy(ins[a].at[me], outs[a].at[me], local.at[a])
            cp.start()
            copies.append(cp)
            for k in range(1, N_DEV):
                s = a * (N_DEV - 1) + k - 1
                cp = pltpu.make_async_remote_copy(src_ref=ins[a].at[jnp.bitwise_xor(me, k)], dst_ref=outs[a].at[me],
                                                  send_sem=send.at[s], recv_sem=recv.at[s], device_id=_peer(k),
                                                  device_id_type=MESH)
                cp.start()
                copies.append(cp)
        for cp in copies:
            cp.wait()

    any_spec = pl.BlockSpec(memory_space=pl.ANY)
    outs = pl.pallas_call(
        body, name=name,
        out_shape=[jax.ShapeDtypeStruct(a.shape, a.dtype) for a in arrs],
        in_specs=[any_spec] * n, out_specs=[any_spec] * n,
        scratch_shapes=[pltpu.SemaphoreType.DMA((n * (N_DEV - 1),)), pltpu.SemaphoreType.DMA((n * (N_DEV - 1),)),
                        pltpu.SemaphoreType.DMA((n,))],
        compiler_params=pltpu.CompilerParams(has_side_effects=True),
    )(*arrs)
    return list(outs)


def _tile(n, pref):
    for t in pref:
        if n % t == 0:
            return t
    return n


def mm_nn(a, b3, out_dtype, name):
    m, k = a.shape
    nb, _, bn = b3.shape
    tm = _tile(m, (512, 256, 128))
    tn = _tile(bn, (1024, 896, 512, 256, 128))
    per = bn // tn

    def body(a_ref, b_ref, o_ref):
        o_ref[...] = _dot(a_ref[...], b_ref[...]).astype(o_ref.dtype)

    return pl.pallas_call(
        body, name=name, grid=(m // tm, nb, per),
        in_specs=[pl.BlockSpec((tm, k), lambda i, j, jj: (i, 0)),
                  pl.BlockSpec((None, k, tn), lambda i, j, jj: (j, 0, jj))],
        out_specs=pl.BlockSpec((tm, tn), lambda i, j, jj: (i, j * per + jj)),
        out_shape=jax.ShapeDtypeStruct((m, nb * bn), out_dtype),
        compiler_params=_cparams("parallel", "arbitrary", "arbitrary"),
    )(a, b3)


def mm_nt(a, w3, out_dtype, name):
    m, _ = a.shape
    nb, ko, bn = w3.shape
    tm = _tile(m, (512, 256, 128))
    tko = _tile(ko, (1024, 512, 256, 128))

    def body(a_ref, w_ref, o_ref, acc_ref):
        j = pl.program_id(2)

        @pl.when(j == 0)
        def _():
            acc_ref[...] = jnp.zeros_like(acc_ref)

        acc_ref[...] += _dot_nt(a_ref[...], w_ref[...])

        @pl.when(j == nb - 1)
        def _():
            o_ref[...] = acc_ref[...].astype(o_ref.dtype)

    return pl.pallas_call(
        body, name=name, grid=(m // tm, ko // tko, nb),
        in_specs=[pl.BlockSpec((tm, bn), lambda i, o, j: (i, j)),
                  pl.BlockSpec((None, tko, bn), lambda i, o, j: (j, o, 0))],
        out_specs=pl.BlockSpec((tm, tko), lambda i, o, j: (i, o)),
        out_shape=jax.ShapeDtypeStruct((m, ko), out_dtype),
        scratch_shapes=[pltpu.VMEM((tm, tko), f32)],
        compiler_params=_cparams("parallel", "arbitrary", "arbitrary"),
    )(a, w3)


def mm_tn(a, dy, ncb, out_dtype, name):
    l, ka = a.shape
    _, n = dy.shape
    bn = n // ncb
    tl = _tile(l, (512, 256, 128))
    tka = _tile(ka, (512, 256, 128))
    tn = _tile(bn, (1024, 896, 512, 256, 128))
    per = bn // tn
    nl = l // tl

    def body(a_ref, dy_ref, o_ref, acc_ref):
        s = pl.program_id(2)

        @pl.when(s == 0)
        def _():
            acc_ref[...] = jnp.zeros_like(acc_ref)

        acc_ref[...] += _dot_tn(a_ref[...], dy_ref[...])

        @pl.when(s == nl - 1)
        def _():
            o_ref[...] = acc_ref[...].astype(o_ref.dtype)

    return pl.pallas_call(
        body, name=name, grid=(ka // tka, n // tn, nl),
        in_specs=[pl.BlockSpec((tl, tka), lambda i, j, s: (s, i)),
                  pl.BlockSpec((tl, tn), lambda i, j, s: (s, j))],
        out_specs=pl.BlockSpec((None, tka, tn), lambda i, j, s: (j // per, i, j % per)),
        out_shape=jax.ShapeDtypeStruct((ncb, ka, bn), out_dtype),
        scratch_shapes=[pltpu.VMEM((tka, tn), f32)],
        compiler_params=_cparams("parallel", "parallel", "arbitrary"),
    )(a, dy)


def mod_part(c_all, w_mod, b_cols):
    nl, d, cols = w_mod.shape

    def body(c_ref, w_ref, b_ref, o_ref):
        cond = _silu(c_ref[...]).astype(bf16)
        o_ref[...] = _dot(cond, w_ref[...].astype(bf16)) + b_ref[...]

    return pl.pallas_call(
        body, name="mod_part", grid=(nl,),
        in_specs=[pl.BlockSpec((N_DEV, d), lambda l: (0, 0)),
                  pl.BlockSpec((None, d, cols), lambda l: (l, 0, 0)),
                  pl.BlockSpec((None, 1, cols), lambda l: (l, 0, 0))],
        out_specs=pl.BlockSpec((None, N_DEV, cols), lambda l: (l, 0, 0)),
        out_shape=jax.ShapeDtypeStruct((nl, N_DEV, cols), f32),
        compiler_params=_cparams("arbitrary"),
    )(c_all, w_mod, b_cols.reshape(nl, 1, cols))


def _row_tile(l):
    return _tile(l, (256, 128))


def prenorm_fwd(x, g, shift, scale, name):
    l, d = x.shape
    tm = _row_tile(l)

    def body(x_ref, g_ref, sh_ref, sc_ref, h_ref):
        xv = x_ref[...]
        r = lax.rsqrt(jnp.mean(xv * xv, axis=-1, keepdims=True) + EPS)
        h_ref[...] = (xv * r * (g_ref[...] * (1.0 + sc_ref[...])) + sh_ref[...]).astype(h_ref.dtype)

    return pl.pallas_call(
        body, name=name, grid=(l // tm,),
        in_specs=[pl.BlockSpec((tm, d), lambda i: (i, 0)), _row(d), _row(d), _row(d)],
        out_specs=pl.BlockSpec((tm, d), lambda i: (i, 0)),
        out_shape=jax.ShapeDtypeStruct((l, d), bf16),
        compiler_params=_cparams("parallel"),
    )(x, g, shift, scale)


def post_fwd(x, y, gate, g, name):
    l, d = x.shape
    tm = _row_tile(l)

    def body(x_ref, y_ref, gate_ref, g_ref, o_ref):
        yv = y_ref[...]
        r = lax.rsqrt(jnp.mean(yv * yv, axis=-1, keepdims=True) + EPS)
        o_ref[...] = x_ref[...] + gate_ref[...] * (yv * r * g_ref[...])

    blk = pl.BlockSpec((tm, d), lambda i: (i, 0))
    return pl.pallas_call(
        body, name=name, grid=(l // tm,),
        in_specs=[blk, blk, _row(d), _row(d)], out_specs=blk,
        out_shape=jax.ShapeDtypeStruct((l, d), f32),
        compiler_params=_cparams("parallel"),
    )(x, y, gate, g)


def final_loss(x, y, gate, g, target):
    l, d = x.shape
    tm = _row_tile(l)

    def body(x_ref, y_ref, gate_ref, g_ref, t_ref, dx_ref, loss_ref):
        @pl.when(pl.program_id(0) == 0)
        def _():
            loss_ref[...] = jnp.zeros_like(loss_ref)

        yv = y_ref[...]
        r = lax.rsqrt(jnp.mean(yv * yv, axis=-1, keepdims=True) + EPS)
        diff = x_ref[...] + gate_ref[...] * (yv * r * g_ref[...]) - t_ref[...]
        dx_ref[...] = diff * (1.0 / d)
        loss_ref[...] += jnp.sum(diff * diff)

    blk = pl.BlockSpec((tm, d), lambda i: (i, 0))
    return pl.pallas_call(
        body, name="final_loss", grid=(l // tm,),
        in_specs=[blk, blk, _row(d), _row(d), blk],
        out_specs=[blk, pl.BlockSpec((SUBLANES, HEAD), lambda i: (0, 0))],
        out_shape=[jax.ShapeDtypeStruct((l, d), f32), jax.ShapeDtypeStruct((SUBLANES, HEAD), f32)],
        compiler_params=_cparams("arbitrary"),
    )(x, y, gate, g, target)


def post_bwd(dx, y, gate, g, name):
    l, d = dx.shape
    tm = _row_tile(l)

    def body(dx_ref, y_ref, gate_ref, g_ref, dy_ref, dgate_ref, dg_ref):
        @pl.when(pl.program_id(0) == 0)
        def _():
            dgate_ref[...] = jnp.zeros_like(dgate_ref)
            dg_ref[...] = jnp.zeros_like(dg_ref)

        yv, dxv, gv = y_ref[...], dx_ref[...], g_ref[...]
        r = lax.rsqrt(jnp.mean(yv * yv, axis=-1, keepdims=True) + EPS)
        yn = yv * r
        dgate_ref[...] += jnp.sum(dxv * yn * gv, axis=0, keepdims=True)
        dyg = dxv * gate_ref[...]
        dg_ref[...] += jnp.sum(dyg * yn, axis=0, keepdims=True)
        dyn = dyg * gv
        dy_ref[...] = (r * (dyn - yn * jnp.mean(dyn * yn, axis=-1, keepdims=True))).astype(dy_ref.dtype)

    blk = pl.BlockSpec((tm, d), lambda i: (i, 0))
    return pl.pallas_call(
        body, name=name, grid=(l // tm,),
        in_specs=[blk, blk, _row(d), _row(d)], out_specs=[blk, _row(d), _row(d)],
        out_shape=[jax.ShapeDtypeStruct((l, d), bf16), jax.ShapeDtypeStruct((1, d), f32),
                   jax.ShapeDtypeStruct((1, d), f32)],
        compiler_params=_cparams("arbitrary"),
    )(dx, y, gate, g)


def prenorm_bwd(dh, x, dx_next, g, scale, name):
    l, d = x.shape
    tm = _row_tile(l)

    def body(dh_ref, x_ref, dxn_ref, g_ref, sc_ref, dx_ref, dsh_ref, dsc_ref, dg_ref):
        @pl.when(pl.program_id(0) == 0)
        def _():
            dsh_ref[...] = jnp.zeros_like(dsh_ref)
            dsc_ref[...] = jnp.zeros_like(dsc_ref)
            dg_ref[...] = jnp.zeros_like(dg_ref)

        xv, dhv, gv, sc1 = x_ref[...], dh_ref[...], g_ref[...], 1.0 + sc_ref[...]
        r = lax.rsqrt(jnp.mean(xv * xv, axis=-1, keepdims=True) + EPS)
        xn = xv * r
        dhx = dhv * xn
        dsh_ref[...] += jnp.sum(dhv, axis=0, keepdims=True)
        dsc_ref[...] += jnp.sum(dhx * gv, axis=0, keepdims=True)
        dg_ref[...] += jnp.sum(dhx * sc1, axis=0, keepdims=True)
        dxn = dhv * (gv * sc1)
        dx_ref[...] = dxn_ref[...] + r * (dxn - xn * jnp.mean(dxn * xn, axis=-1, keepdims=True))

    blk = pl.BlockSpec((tm, d), lambda i: (i, 0))
    return pl.pallas_call(
        body, name=name, grid=(l // tm,),
        in_specs=[blk, blk, blk, _row(d), _row(d)], out_specs=[blk, _row(d), _row(d), _row(d)],
        out_shape=[jax.ShapeDtypeStruct((l, d), f32)] + [jax.ShapeDtypeStruct((1, d), f32)] * 3,
        compiler_params=_cparams("arbitrary"),
    )(dh, x, dx_next, g, scale)


def _tril_mask():
    r = lax.broadcasted_iota(jnp.int32, (HEAD, HEAD), 0)
    c = lax.broadcasted_iota(jnp.int32, (HEAD, HEAD), 1)
    return r >= c


def sgu_fwd(proj, norm_g, w_s, b_s):
    l = proj.shape[0]
    nh = w_s.shape[0]
    wa = nh * HEAD

    def body(au_ref, av_ref, az_ref, ng_ref, w_ref, b_ref, o_ref):
        tril = _tril_mask()
        for h in range(nh):
            sl = slice(h * HEAD, (h + 1) * HEAD)
            gv = _gelu(av_ref[:, sl].astype(f32))
            r = lax.rsqrt(jnp.mean(gv * gv, axis=-1, keepdims=True) + EPS)
            vh = gv * r * ng_ref[:, sl]
            wm = jnp.where(tril, w_ref[h], 0.0).astype(bf16)
            s = _dot(wm, vh.astype(bf16)) + b_ref[h]
            o_ref[:, sl] = (_gelu(au_ref[:, sl].astype(f32)) * s * _silu(az_ref[:, sl].astype(f32))).astype(o_ref.dtype)

    def col(j):
        return pl.BlockSpec((HEAD, wa), lambda n: (n, j))

    return pl.pallas_call(
        body, name="sgu_fwd", grid=(l // HEAD,),
        in_specs=[col(0), col(1), col(2), _row(wa),
                  pl.BlockSpec((nh, HEAD, HEAD), lambda n: (0, 0, 0)), pl.BlockSpec((nh, HEAD, 1), lambda n: (0, 0, 0))],
        out_specs=pl.BlockSpec((HEAD, wa), lambda n: (n, 0)),
        out_shape=jax.ShapeDtypeStruct((l, wa), bf16),
        compiler_params=_cparams("parallel"),
    )(proj, proj, proj, norm_g, w_s, b_s)


def sgu_bwd(proj, dcat, norm_g, w_s, b_s):
    l = proj.shape[0]
    nh = w_s.shape[0]
    wa = nh * HEAD

    def body(au_ref, av_ref, az_ref, do_ref, ng_ref, w_ref, b_ref, da_ref, dw_ref, db_ref, dng_ref):
        @pl.when(pl.program_id(0) == 0)
        def _():
            dw_ref[...] = jnp.zeros_like(dw_ref)
            db_ref[...] = jnp.zeros_like(db_ref)
            dng_ref[...] = jnp.zeros_like(dng_ref)

        tril = _tril_mask()
        for h in range(nh):
            sl = slice(h * HEAD, (h + 1) * HEAD)
            au, av, az = au_ref[:, sl].astype(f32), av_ref[:, sl].astype(f32), az_ref[:, sl].astype(f32)
            ng = ng_ref[:, sl]
            gv = _gelu(av)
            r = lax.rsqrt(jnp.mean(gv * gv, axis=-1, keepdims=True) + EPS)
            gvn = gv * r
            vh = (gvn * ng).astype(bf16)
            wm = jnp.where(tril, w_ref[h], 0.0).astype(bf16)
            s = _dot(wm, vh) + b_ref[h]
            gu, sz = _gelu(au), _silu(az)
            dov = do_ref[:, sl].astype(f32)
            da_ref[:, sl] = (dov * s * sz * _gelu_grad(au)).astype(da_ref.dtype)
            da_ref[:, 2 * wa + h * HEAD:2 * wa + (h + 1) * HEAD] = (dov * gu * s * _silu_grad(az)).astype(da_ref.dtype)
            ds = dov * gu * sz
            db_ref[h] += jnp.sum(ds, axis=-1, keepdims=True)
            dsb = ds.astype(bf16)
            dw_ref[h] += jnp.where(tril, _dot_nt(dsb, vh), 0.0)
            dvh = _dot_tn(wm, dsb)
            dng_ref[:, sl] += jnp.sum(dvh * gvn, axis=0, keepdims=True)
            dgvn = dvh * ng
            dgv = r * (dgvn - gvn * jnp.mean(dgvn * gvn, axis=-1, keepdims=True))
            da_ref[:, wa + h * HEAD:wa + (h + 1) * HEAD] = (dgv * _gelu_grad(av)).astype(da_ref.dtype)

    def col(j):
        return pl.BlockSpec((HEAD, wa), lambda n: (n, j))

    whole_w = pl.BlockSpec((nh, HEAD, HEAD), lambda n: (0, 0, 0))
    whole_b = pl.BlockSpec((nh, HEAD, 1), lambda n: (0, 0, 0))
    return pl.pallas_call(
        body, name="sgu_bwd", grid=(l // HEAD,),
        in_specs=[col(0), col(1), col(2), col(0), _row(wa), whole_w, whole_b],
        out_specs=[pl.BlockSpec((HEAD, 3 * wa), lambda n: (n, 0)), whole_w, whole_b, _row(wa)],
        out_shape=[jax.ShapeDtypeStruct((l, 3 * wa), bf16), jax.ShapeDtypeStruct((nh, HEAD, HEAD), f32),
                   jax.ShapeDtypeStruct((nh, HEAD, 1), f32), jax.ShapeDtypeStruct((1, wa), f32)],
        compiler_params=_cparams("arbitrary"),
    )(proj, proj, proj, dcat, norm_g, w_s, b_s)


_LOG2E = 1.0 / math.log(2.0)


def _sb_scores(q, k, scale):
    z = _dot_nt(q, k) * (scale * _LOG2E)
    return z, jnp.maximum(z, 0.0) + jnp.log2(1.0 + jnp.exp2(-jnp.abs(z)))


def _sb_sum_matrix(tri):
    s = lax.broadcasted_iota(jnp.int32, (2 * HEAD, 2 * HEAD), 0) % HEAD
    j = lax.broadcasted_iota(jnp.int32, (2 * HEAD, 2 * HEAD), 1)
    return jnp.where(jnp.logical_or(j >= HEAD, tri(s, j)), 1.0, 0.0).astype(bf16)


def _sb_sums(x, sums):
    c2 = _dot(jnp.concatenate(_split_bf16(x), axis=1), sums)
    return c2[:, :HEAD], c2[:, HEAD:]


def _sb_q_tile(l):
    return _tile(l, (512, 256, 128))


def _sb_heads_per_step(nh, most):
    return _tile(nh, tuple(h for h in (4, 2) if h <= most))


def sb_fwd(proj, nh):
    l = proj.shape[0]
    wb = nh * HEAD
    tq = _sb_q_tile(l)
    band = tq // HEAD
    hp = _sb_heads_per_step(nh, 4)
    scale = 1.0 / math.sqrt(HEAD)
    qc, kc, vc, zc = 3 * nh, 4 * nh, 5 * nh, 6 * nh

    def body(q_ref, k_ref, v_ref, bz_ref, o_ref, att_ref, tot_ref):
        i = pl.program_id(1)
        sums = _sb_sum_matrix(lambda s, j: s > j)
        t_pos = i * tq + lax.broadcasted_iota(jnp.int32, (tq, HEAD), 0)
        s_off = lax.broadcasted_iota(jnp.int32, (tq, HEAD), 1)

        def step(j, carry, masked):
            rows = pl.ds(pl.multiple_of(j * HEAD, HEAD), HEAD)
            out = []
            for e in range(hp):
                acc, tot = carry[e]
                sl = slice(e * HEAD, (e + 1) * HEAD)
                z, sp = _sb_scores(q_ref[:, sl], k_ref[rows, sl], scale)
                lb = z - sp
                if masked:
                    mask = s_off + j * HEAD < t_pos
                    sp = jnp.where(mask, sp, 0.0)
                later, total = _sb_sums(sp, sums)
                w = jnp.exp2(lb + tot - later)
                if masked:
                    w = jnp.where(mask, w, 0.0)
                out.append((acc + _dot(w.astype(bf16), v_ref[rows, sl]), tot - total))
            return tuple(out)

        zero = jnp.zeros((tq, HEAD), f32)
        carry = lax.fori_loop(0, band, lambda t, c: step(band * i + band - 1 - t, c, True), ((zero, zero),) * hp)
        carry = lax.fori_loop(0, band * i, lambda t, c: step(band * i - 1 - t, c, False), carry)
        for e in range(hp):
            acc, tot = carry[e]
            sl = slice(e * HEAD, (e + 1) * HEAD)
            att_ref[:, sl] = acc.astype(att_ref.dtype)
            o_ref[:, sl] = (acc * _silu(bz_ref[:, sl].astype(f32))).astype(o_ref.dtype)
            tot_ref[e] = tot[:, :1]

    blk = lambda c0: pl.BlockSpec((tq, hp * HEAD), lambda g, i: (i, c0 // hp + g))
    head = lambda c0: pl.BlockSpec((l, hp * HEAD), lambda g, i: (0, c0 // hp + g))
    return pl.pallas_call(
        body, name="sb_fwd", grid=(nh // hp, l // tq),
        in_specs=[blk(qc), head(kc), head(vc), blk(zc)],
        out_specs=[blk(0), blk(0), pl.BlockSpec((hp, tq, 1), lambda g, i: (g, i, 0))],
        out_shape=[jax.ShapeDtypeStruct((l, wb), bf16), jax.ShapeDtypeStruct((l, wb), bf16),
                   jax.ShapeDtypeStruct((nh, l, 1), f32)],
        compiler_params=_cparams("parallel", "arbitrary"),
    )(proj, proj, proj, proj)


def sb_bwd(proj, dcat, att, tot, nh):
    l = proj.shape[0]
    wb = nh * HEAD
    tq = _sb_q_tile(l)
    band = tq // HEAD
    nq = l // tq
    hp = _sb_heads_per_step(nh, 2)
    scale = 1.0 / math.sqrt(HEAD)
    qc, kc, vc, zc = 3 * nh, 4 * nh, 5 * nh, 6 * nh

    def body(q_ref, k_ref, v_ref, bz_ref, do_ref, att_ref, tot_ref, dq_ref, dk_ref, dv_ref, dbz_ref, dk_acc, dv_acc,
             dob_ref):
        i = pl.program_id(1)

        @pl.when(i == 0)
        def _():
            dk_acc[...] = jnp.zeros_like(dk_acc)
            dv_acc[...] = jnp.zeros_like(dv_acc)

        bz = bz_ref[...].astype(f32)
        dov = do_ref[...].astype(f32)
        dbz_ref[...] = (dov * att_ref[...].astype(f32) * _silu_grad(bz)).astype(dbz_ref.dtype)
        dob_ref[...] = (dov * _silu(bz)).astype(bf16)
        upto = _sb_sum_matrix(lambda s, j: s <= j)
        before = _sb_sum_matrix(lambda j, s: j < s)
        t_pos = i * tq + lax.broadcasted_iota(jnp.int32, (tq, HEAD), 0)
        s_off = lax.broadcasted_iota(jnp.int32, (tq, HEAD), 1)

        def step(j, carry, masked):
            rows = pl.ds(pl.multiple_of(j * HEAD, HEAD), HEAD)
            out = []
            for h in range(hp):
                dq, sp_seen, e_seen = carry[h]
                sl = slice(h * HEAD, (h + 1) * HEAD)
                q, kj, vj, dob = q_ref[:, sl], k_ref[rows, sl], v_ref[rows, sl], dob_ref[:, sl]
                z, sp = _sb_scores(q, kj, scale)
                lb = z - sp
                if masked:
                    mask = s_off + j * HEAD < t_pos
                    sp = jnp.where(mask, sp, 0.0)
                sp_upto, sp_total = _sb_sums(sp, upto)
                w = jnp.exp2(lb + sp_seen + sp_upto)
                if masked:
                    w = jnp.where(mask, w, 0.0)
                dv_acc[rows, sl] += _dot_tn(w.astype(bf16), dob)
                e = _dot_nt(dob, vj) * w
                e_before, e_total = _sb_sums(e, before)
                dz = (e - (e + e_seen + e_before) * jnp.exp2(lb)) * scale
                if masked:
                    dz = jnp.where(mask, dz, 0.0)
                dz = dz.astype(bf16)
                dk_acc[rows, sl] += _dot_tn(dz, q)
                out.append((dq + _dot(dz, kj), sp_seen + sp_total, e_seen + e_total))
            return tuple(out)

        zero = jnp.zeros((tq, HEAD), f32)
        init = tuple((zero, jnp.broadcast_to(tot_ref[h], (tq, HEAD)), zero) for h in range(hp))
        carry = lax.fori_loop(0, band * i, lambda j, c: step(j, c, False), init)
        carry = lax.fori_loop(0, band, lambda t, c: step(band * i + t, c, True), carry)
        for h in range(hp):
            dq_ref[:, h * HEAD:(h + 1) * HEAD] = carry[h][0].astype(dq_ref.dtype)

        @pl.when(i == nq - 1)
        def _():
            dk_ref[...] = dk_acc[...].astype(dk_ref.dtype)
            dv_ref[...] = dv_acc[...].astype(dv_ref.dtype)

    blk = lambda c0: pl.BlockSpec((tq, hp * HEAD), lambda g, i: (i, c0 // hp + g))
    head = lambda c0: pl.BlockSpec((l, hp * HEAD), lambda g, i: (0, c0 // hp + g))
    return pl.pallas_call(
        body, name="sb_bwd", grid=(nh // hp, nq),
        in_specs=[blk(qc), head(kc), head(vc), blk(zc), blk(nh), blk(0),
                  pl.BlockSpec((hp, tq, 1), lambda g, i: (g, i, 0))],
        out_specs=[blk(0), head(0), head(0), blk(0)],
        out_shape=[jax.ShapeDtypeStruct((l, wb), bf16)] * 4,
        scratch_shapes=[pltpu.VMEM((l, hp * HEAD), f32), pltpu.VMEM((l, hp * HEAD), f32),
                        pltpu.VMEM((tq, hp * HEAD), bf16)],
        compiler_params=_cparams("parallel", "arbitrary"),
    )(proj, proj, proj, proj, dcat, att, tot)


def _disc(lr, li, ldt):
    dt = jnp.exp(ldt)
    mag = jnp.exp(lr * dt)
    a_re = mag * jnp.cos(li * dt)
    a_im = mag * jnp.sin(li * dt)
    den = lr * lr + li * li
    nr = a_re - 1.0
    return a_re, a_im, (nr * lr + a_im * li) / den, (a_im * lr - nr * li) / den


def s5_params_fwd(lr, li, ldt, bt_re, bt_im):
    g, c, p = bt_re.shape

    def body(lr_ref, li_ref, ldt_ref, br_ref, bi_ref, ar_ref, ai_ref, bbr_ref, bbi_ref):
        a_re, a_im, cr, ci = _disc(lr_ref[...], li_ref[...], ldt_ref[...])
        ar_ref[...] = a_re
        ai_ref[...] = a_im
        for k in range(c):
            br, bi = br_ref[:, k, :], bi_ref[:, k, :]
            bbr_ref[:, k, :] = cr * br - ci * bi
            bbi_ref[:, k, :] = cr * bi + ci * br

    return pl.pallas_call(
        body, name="s5_params_fwd",
        out_shape=[jax.ShapeDtypeStruct((g, p), f32)] * 2 + [jax.ShapeDtypeStruct((g, c, p), f32)] * 2,
    )(lr, li, ldt, bt_re, bt_im)


def s5_params_bwd(lr, li, ldt, bt_re, bt_im, da_re, da_im, dbbt_re, dbbt_im):
    g, c, p = bt_re.shape

    def body(lr_ref, li_ref, ldt_ref, br_ref, bi_ref, dar_ref, dai_ref, dbbr_ref, dbbi_ref,
             dlr_ref, dli_ref, dldt_ref, dbr_ref, dbi_ref):
        (a_re, a_im, cr, ci), vjp = jax.vjp(_disc, lr_ref[...], li_ref[...], ldt_ref[...])
        dcr = jnp.zeros((g, p), f32)
        dci = jnp.zeros((g, p), f32)
        for k in range(c):
            br, bi = br_ref[:, k, :], bi_ref[:, k, :]
            dr, di = dbbr_ref[:, k, :], dbbi_ref[:, k, :]
            dcr += dr * br + di * bi
            dci += di * br - dr * bi
            dbr_ref[:, k, :] = cr * dr + ci * di
            dbi_ref[:, k, :] = cr * di - ci * dr
        dlr, dli, dldt = vjp((dar_ref[...], dai_ref[...], dcr, dci))
        dlr_ref[...] = dlr
        dli_ref[...] = dli
        dldt_ref[...] = dldt

    return pl.pallas_call(
        body, name="s5_params_bwd",
        out_shape=[jax.ShapeDtypeStruct((g, p), f32)] * 2 + [jax.ShapeDtypeStruct((g, 1), f32)]
        + [jax.ShapeDtypeStruct((g, c, p), f32)] * 2,
    )(lr, li, ldt, bt_re, bt_im, da_re, da_im, dbbt_re, dbbt_im)


def _cmul(ar, ai, br, bi):
    return ar * br - ai * bi, ar * bi + ai * br


def _power_tables(ar, ai):
    rows = lax.broadcasted_iota(jnp.int32, (SUBLANES, ar.shape[1]), 0)
    pr = jnp.zeros((SUBLANES, ar.shape[1]), f32)
    pi = jnp.zeros((SUBLANES, ar.shape[1]), f32)
    cr, ci = ar, ai
    pows = {}
    for r in range(SUBLANES):
        pows[r + 1] = (cr, ci)
        pr = jnp.where(rows == r, cr, pr)
        pi = jnp.where(rows == r, ci, pi)
        cr, ci = _cmul(cr, ci, ar, ai)
    return [pows[1], pows[2], pows[4]], pr, pi


def _ssm_time_tile(l):
    return _tile(l, (512, 256, 128))


def ssm_fwd(u, bre3, bim3, cre3, cimn3, a_re, a_im, d_skip):
    l, w = u.shape
    nj = w // HEAD
    ns = STATES_PER_LANE_BLOCK
    tt = _ssm_time_tile(l)

    def body(u_ref, bre_ref, bim_ref, cre_ref, cim_ref, ar_ref, ai_ref, d_ref, y_ref, hr_ref, hi_ref, cr_ref, ci_ref):
        @pl.when(pl.program_id(1) == 0)
        def _():
            cr_ref[...] = jnp.zeros_like(cr_ref)
            ci_ref[...] = jnp.zeros_like(ci_ref)

        uv = u_ref[...]
        hr_ref[...] = _dot(uv, bre_ref[...])
        hi_ref[...] = _dot(uv, bim_ref[...])
        steps, pr, pi = _power_tables(ar_ref[...], ai_ref[...])
        rows = lax.broadcasted_iota(jnp.int32, (SUBLANES, ns), 0)

        def blk(b, carry):
            cr, ci = carry
            sl = pl.ds(pl.multiple_of(b * SUBLANES, SUBLANES), SUBLANES)
            xr, xi = hr_ref[sl, :], hi_ref[sl, :]
            for d, (sr_, si_) in zip((1, 2, 4), steps):
                keep = rows >= d
                qr = jnp.where(keep, pltpu.roll(xr, d, axis=0), 0.0)
                qi = jnp.where(keep, pltpu.roll(xi, d, axis=0), 0.0)
                mr, mi = _cmul(sr_, si_, qr, qi)
                xr, xi = xr + mr, xi + mi
            mr, mi = _cmul(pr, pi, cr, ci)
            xr, xi = xr + mr, xi + mi
            hr_ref[sl, :] = xr
            hi_ref[sl, :] = xi
            return xr[SUBLANES - 1:, :], xi[SUBLANES - 1:, :]

        cr, ci = lax.fori_loop(0, tt // SUBLANES, blk, (cr_ref[...], ci_ref[...]))
        cr_ref[...] = cr
        ci_ref[...] = ci
        y = _dot(hr_ref[...].astype(bf16), cre_ref[...]) + _dot(hi_ref[...].astype(bf16), cim_ref[...])
        y_ref[...] = y + d_ref[...] * uv.astype(f32)

    lane = pl.BlockSpec((tt, HEAD), lambda j, i: (i, j))
    st = pl.BlockSpec((tt, ns), lambda j, i: (i, j))
    b3 = pl.BlockSpec((None, HEAD, ns), lambda j, i: (j, 0, 0))
    c3 = pl.BlockSpec((None, ns, HEAD), lambda j, i: (j, 0, 0))
    arow = pl.BlockSpec((1, ns), lambda j, i: (0, j))
    return pl.pallas_call(
        body, name="ssm_fwd", grid=(nj, l // tt),
        in_specs=[lane, b3, b3, c3, c3, arow, arow, pl.BlockSpec((1, HEAD), lambda j, i: (0, j))],
        out_specs=[lane, st, st],
        out_shape=[jax.ShapeDtypeStruct((l, w), f32), jax.ShapeDtypeStruct((l, nj * ns), f32),
                   jax.ShapeDtypeStruct((l, nj * ns), f32)],
        scratch_shapes=[pltpu.VMEM((1, ns), f32), pltpu.VMEM((1, ns), f32)],
        compiler_params=_cparams("parallel", "arbitrary"),
    )(u, bre3, bim3, cre3, cimn3, a_re, a_im, d_skip)


def ssm_bwd(dy, u, h_re, h_im, bre3, bim3, cre3, cimn3, a_re, a_im, d_skip):
    l, w = u.shape
    nj = w // HEAD
    ns = STATES_PER_LANE_BLOCK
    tt = _ssm_time_tile(l)
    nt = l // tt

    def body(dy_ref, u_ref, hr_ref, hi_ref, bre_ref, bim_ref, cre_ref, cim_ref, ar_ref, ai_ref, d_ref,
             du_ref, dd_ref, dar_ref, dai_ref, dbre_ref, dbim_ref, dcre_ref, dcim_ref, kr_ref, ki_ref, cr_ref, ci_ref,
             accr_ref, acci_ref):
        i = pl.program_id(1)

        @pl.when(i == 0)
        def _():
            for ref in (cr_ref, ci_ref, accr_ref, acci_ref, dd_ref, dbre_ref, dbim_ref, dcre_ref, dcim_ref):
                ref[...] = jnp.zeros_like(ref)

        dyv = dy_ref[...]
        dyb = dyv.astype(bf16)
        uv = u_ref[...]
        kr_ref[...] = _dot_nt(dyb, cre_ref[...])
        ki_ref[...] = _dot_nt(dyb, cim_ref[...])
        steps, pr, pi = _power_tables(ar_ref[...], -ai_ref[...])
        rows = lax.broadcasted_iota(jnp.int32, (SUBLANES, ns), 0)
        qr = jnp.zeros((SUBLANES, ns), f32)
        qi = jnp.zeros((SUBLANES, ns), f32)
        for r in range(SUBLANES):
            qr = jnp.where(rows == r, pr[SUBLANES - 1 - r:SUBLANES - r, :], qr)
            qi = jnp.where(rows == r, pi[SUBLANES - 1 - r:SUBLANES - r, :], qi)
        nb = tt // SUBLANES

        def blk(t, carry):
            cr, ci, accr, acci = carry
            sl = pl.ds(pl.multiple_of((nb - 1 - t) * SUBLANES, SUBLANES), SUBLANES)
            xr, xi = kr_ref[sl, :], ki_ref[sl, :]
            for d, (sr_, si_) in zip((1, 2, 4), steps):
                keep = rows < SUBLANES - d
                zr = jnp.where(keep, pltpu.roll(xr, SUBLANES - d, axis=0), 0.0)
                zi = jnp.where(keep, pltpu.roll(xi, SUBLANES - d, axis=0), 0.0)
                mr, mi = _cmul(sr_, si_, zr, zi)
                xr, xi = xr + mr, xi + mi
            mr, mi = _cmul(qr, qi, cr, ci)
            xr, xi = xr + mr, xi + mi
            kr_ref[sl, :] = xr
            ki_ref[sl, :] = xi
            last = rows == SUBLANES - 1
            nr = jnp.where(last, cr, pltpu.roll(xr, SUBLANES - 1, axis=0))
            ni = jnp.where(last, ci, pltpu.roll(xi, SUBLANES - 1, axis=0))
            hr, hi = hr_ref[sl, :], hi_ref[sl, :]
            accr = accr + nr * hr + ni * hi
            acci = acci + ni * hr - nr * hi
            return xr[:1, :], xi[:1, :], accr, acci

        cr, ci, accr, acci = lax.fori_loop(0, nb, blk, (cr_ref[...], ci_ref[...], accr_ref[...], acci_ref[...]))
        cr_ref[...] = cr
        ci_ref[...] = ci
        accr_ref[...] = accr
        acci_ref[...] = acci
        kr, ki = kr_ref[...].astype(bf16), ki_ref[...].astype(bf16)
        du = _dot_nt(kr, bre_ref[...]) + _dot_nt(ki, bim_ref[...]) + d_ref[...] * dyv
        du_ref[...] = du.astype(du_ref.dtype)
        dd_ref[...] += jnp.sum(dyv * uv.astype(f32), axis=0, keepdims=True)
        dbre_ref[...] += _dot_tn(uv, kr)
        dbim_ref[...] += _dot_tn(uv, ki)
        dcre_ref[...] += _dot_tn(hr_ref[...].astype(bf16), dyb)
        dcim_ref[...] += _dot_tn(hi_ref[...].astype(bf16), dyb)

        @pl.when(i == nt - 1)
        def _():
            dar_ref[...] = jnp.sum(accr_ref[...], axis=0, keepdims=True)
            dai_ref[...] = jnp.sum(acci_ref[...], axis=0, keepdims=True)

    lane = pl.BlockSpec((tt, HEAD), lambda j, i: (nt - 1 - i, j))
    st = pl.BlockSpec((tt, ns), lambda j, i: (nt - 1 - i, j))
    b3 = pl.BlockSpec((None, HEAD, ns), lambda j, i: (j, 0, 0))
    c3 = pl.BlockSpec((None, ns, HEAD), lambda j, i: (j, 0, 0))
    arow = pl.BlockSpec((1, ns), lambda j, i: (0, j))
    drow = pl.BlockSpec((1, HEAD), lambda j, i: (0, j))
    return pl.pallas_call(
        body, name="ssm_bwd", grid=(nj, nt),
        in_specs=[lane, lane, st, st, b3, b3, c3, c3, arow, arow, drow],
        out_specs=[lane, drow, arow, arow, b3, b3, c3, c3],
        out_shape=[jax.ShapeDtypeStruct((l, w), bf16), jax.ShapeDtypeStruct((1, w), f32),
                   jax.ShapeDtypeStruct((1, nj * ns), f32), jax.ShapeDtypeStruct((1, nj * ns), f32),
                   jax.ShapeDtypeStruct((nj, HEAD, ns), f32), jax.ShapeDtypeStruct((nj, HEAD, ns), f32),
                   jax.ShapeDtypeStruct((nj, ns, HEAD), f32), jax.ShapeDtypeStruct((nj, ns, HEAD), f32)],
        scratch_shapes=[pltpu.VMEM((tt, ns), f32), pltpu.VMEM((tt, ns), f32), pltpu.VMEM((1, ns), f32),
                        pltpu.VMEM((1, ns), f32), pltpu.VMEM((SUBLANES, ns), f32), pltpu.VMEM((SUBLANES, ns), f32)],
        compiler_params=_cparams("parallel", "arbitrary"),
    )(dy, u, h_re, h_im, bre3, bim3, cre3, cimn3, a_re, a_im, d_skip)


def glu_fwd(y, z_src, w_glu, b_glu):
    l, w = y.shape
    tm = _row_tile(l)

    def body(y_ref, z_ref, w_ref, b_ref, g_ref, t_ref, o_ref):
        g = _gelu(y_ref[...])
        gb = g.astype(bf16)
        t = _dot(gb, w_ref[...]) + b_ref[...]
        g_ref[...] = gb
        t_ref[...] = t
        o_ref[...] = (g * jax.nn.sigmoid(t) * _silu(z_ref[...].astype(f32))).astype(o_ref.dtype)

    blk = pl.BlockSpec((tm, w), lambda i: (i, 0))
    return pl.pallas_call(
        body, name="glu_fwd", grid=(l // tm,),
        in_specs=[blk, pl.BlockSpec((tm, w), lambda i: (i, 1)), pl.BlockSpec((w, w), lambda i: (0, 0)), _row(w)],
        out_specs=[blk, blk, blk],
        out_shape=[jax.ShapeDtypeStruct((l, w), bf16), jax.ShapeDtypeStruct((l, w), f32),
                   jax.ShapeDtypeStruct((l, w), bf16)],
        compiler_params=_cparams("parallel"),
    )(y, z_src, w_glu, b_glu)


def glu_bwd(dout, y, t, z_src, w_glu):
    l, w = y.shape
    tm = _row_tile(l)

    def body(do_ref, y_ref, t_ref, z_ref, w_ref, dy_ref, dz_ref, dt_ref, db_ref):
        @pl.when(pl.program_id(0) == 0)
        def _():
            db_ref[...] = jnp.zeros_like(db_ref)

        yv, zv, dov = y_ref[...], z_ref[...].astype(f32), do_ref[...]
        g = _gelu(yv)
        sg = jax.nn.sigmoid(t_ref[...])
        dy2 = dov * _silu(zv)
        dz_ref[...] = (dov * g * sg * _silu_grad(zv)).astype(dz_ref.dtype)
        dt = dy2 * g * sg * (1.0 - sg)
        dtb = dt.astype(bf16)
        dt_ref[...] = dtb
        db_ref[...] += jnp.sum(dt, axis=0, keepdims=True)
        dg = dy2 * sg + _dot_nt(dtb, w_ref[...])
        dy_ref[...] = dg * _gelu_grad(yv)

    blk = pl.BlockSpec((tm, w), lambda i: (i, 0))
    return pl.pallas_call(
        body, name="glu_bwd", grid=(l // tm,),
        in_specs=[blk, blk, blk, pl.BlockSpec((tm, w), lambda i: (i, 1)), pl.BlockSpec((w, w), lambda i: (0, 0))],
        out_specs=[blk, blk, blk, _row(w)],
        out_shape=[jax.ShapeDtypeStruct((l, w), f32), jax.ShapeDtypeStruct((l, w), bf16),
                   jax.ShapeDtypeStruct((l, w), bf16), jax.ShapeDtypeStruct((1, w), f32)],
        compiler_params=_cparams("arbitrary"),
    )(dout, y, t, z_src, w_glu)


def _adamw(w, g, m, v):
    m = ADAM_B1 * m + (1.0 - ADAM_B1) * g
    v = ADAM_B2 * v + (1.0 - ADAM_B2) * (g * g)
    m_hat = m / (1.0 - ADAM_B1 ** ADAM_STEP)
    v_hat = v / (1.0 - ADAM_B2 ** ADAM_STEP)
    return -ADAM_LR * (m_hat / (jnp.sqrt(v_hat) + ADAM_EPS) + ADAM_WD * w), m, v


def adam_reduce(pieces, w, m, v, name):
    r, c = w.shape
    tr = _tile(r, (256, 128, 64, 32, 16, 8))

    def body(p_ref, w_ref, m_ref, v_ref, g_ref, d_ref, nm_ref, nv_ref):
        g = p_ref[0].astype(f32)
        for s in range(1, N_DEV):
            g = g + p_ref[s].astype(f32)
        g_ref[...] = g
        d_ref[...], nm_ref[...], nv_ref[...] = _adamw(w_ref[...], g, m_ref[...], v_ref[...])

    blk = pl.BlockSpec((tr, c), lambda i: (i, 0))
    return pl.pallas_call(
        body, name=name, grid=(r // tr,),
        in_specs=[pl.BlockSpec((N_DEV, tr, c), lambda i: (0, i, 0)), blk, blk, blk],
        out_specs=[blk] * 4, out_shape=[jax.ShapeDtypeStruct((r, c), f32)] * 4,
        compiler_params=_cparams("parallel"),
    )(pieces, w, m, v)


def adam_w_mod(cond_t, dm, w, m, v):
    nl, d, cols = w.shape
    tr = _tile(d, (512, 256, 128))

    def body(c_ref, dm_ref, w_ref, m_ref, v_ref, g_ref, d_ref, nm_ref, nv_ref):
        g = jnp.dot(c_ref[...], dm_ref[...], preferred_element_type=f32, precision=lax.Precision.HIGHEST)
        g_ref[...] = g
        d_ref[...], nm_ref[...], nv_ref[...] = _adamw(w_ref[...], g, m_ref[...], v_ref[...])

    blk = pl.BlockSpec((None, tr, cols), lambda l, i: (l, i, 0))
    return pl.pallas_call(
        body, name="adam_w_mod", grid=(nl, d // tr),
        in_specs=[pl.BlockSpec((tr, N_DEV), lambda l, i: (i, 0)), pl.BlockSpec((None, N_DEV, cols), lambda l, i: (l, 0, 0)),
                  blk, blk, blk],
        out_specs=[blk] * 4, out_shape=[jax.ShapeDtypeStruct((nl, d, cols), f32)] * 4,
        compiler_params=_cparams("parallel", "parallel"),
    )(cond_t, dm, w, m, v)


def silu_rows(c_all):
    def body(c_ref, o_ref):
        o_ref[...] = _silu(c_ref[...])

    return pl.pallas_call(body, name="silu_rows", out_shape=jax.ShapeDtypeStruct(c_all.shape, f32))(c_all)


def _block_diag(x):
    g, a, b = x.shape
    nj = g // GROUPS_PER_LANE_BLOCK
    eye = jnp.eye(GROUPS_PER_LANE_BLOCK, dtype=x.dtype)
    x5 = x.reshape(nj, GROUPS_PER_LANE_BLOCK, a, b)
    return jnp.einsum("jgab,gh->jgahb", x5, eye).reshape(nj, GROUPS_PER_LANE_BLOCK * a, GROUPS_PER_LANE_BLOCK * b)


def _diag_blocks(x, a, b):
    nj = x.shape[0]
    x5 = x.reshape(nj, GROUPS_PER_LANE_BLOCK, a, GROUPS_PER_LANE_BLOCK, b)
    eye = jnp.eye(GROUPS_PER_LANE_BLOCK, dtype=x.dtype)
    return jnp.einsum("jgahb,gh->jgab", x5, eye).reshape(nj * GROUPS_PER_LANE_BLOCK, a, b)


PACK_ROW = SUBLANES * HEAD


def _pack(parts):
    rows = []
    for p in parts:
        flat = p.reshape(-1)
        pad = (-flat.shape[0]) % PACK_ROW
        if pad:
            flat = jnp.concatenate([flat, jnp.zeros((pad,), flat.dtype)])
        rows.append(flat.reshape(-1, HEAD))
    return jnp.concatenate(rows, axis=0)


def _unpack(packed, shapes):
    out, r0 = [], 0
    for shp in shapes:
        n = math.prod(shp)
        nr = -(-n // PACK_ROW) * SUBLANES
        out.append(packed[r0:r0 + nr].reshape(-1)[:n].reshape(shp))
        r0 += nr
    return out


def adam_small(gathered, w, m, v):
    r, c = w.shape

    def body(p_ref, w_ref, m_ref, v_ref, g_ref, d_ref, nm_ref, nv_ref):
        g = p_ref[0]
        for s in range(1, N_DEV):
            g = g + p_ref[s]
        g_ref[...] = g
        d_ref[...], nm_ref[...], nv_ref[...] = _adamw(w_ref[...], g, m_ref[...], v_ref[...])

    return pl.pallas_call(
        body, name="adam_small", out_shape=[jax.ShapeDtypeStruct((r, c), f32)] * 4,
        compiler_params=_cparams(),
    )(gathered, w, m, v)


def kernel(x, c, ln_pre_g, ln_post_g, w_mod, b_mod, w_in_ab, w_out_ab, sgu_norm_g, sgu_w, sgu_b, w_in_ssm, w_out_ssm, lam_re, lam_im, b_re, b_im, c_re, c_im, d_skip, log_dt, w_glu, b_glu, loss_target, m_ln_pre_g, m_ln_post_g, m_w_mod, m_b_mod, m_w_in_ab, m_w_out_ab, m_sgu_norm_g, m_sgu_w, m_sgu_b, m_w_in_ssm, m_w_out_ssm, m_lam_re, m_lam_im, m_b_re, m_b_im, m_c_re, m_c_im, m_d_skip, m_log_dt, m_w_glu, m_b_glu, v_ln_pre_g, v_ln_post_g, v_w_mod, v_b_mod, v_w_in_ab, v_w_out_ab, v_sgu_norm_g, v_sgu_w, v_sgu_b, v_w_in_ssm, v_w_out_ssm, v_lam_re, v_lam_im, v_b_re, v_b_im, v_c_re, v_c_im, v_d_skip, v_log_dt, v_w_glu, v_b_glu):
    me = _my_index()
    x0 = x[0]
    l, d = x0.shape
    target = loss_target[0]
    nh = sgu_w.shape[1]
    wa = nh * HEAD
    n_grp, n_st = lam_re.shape[1], lam_re.shape[2]
    mod_cols = w_mod.shape[2]

    gathered = all_gather(
        [c, w_in_ab[0].astype(bf16), w_out_ab[0].astype(bf16), w_in_ssm[0].astype(bf16), w_out_ssm[0].astype(bf16),
         w_glu[0].astype(bf16), d_skip, b_glu], "gather_weights")
    c_all = gathered[0].reshape(N_DEV, d)
    win_ab3 = gathered[1]
    wout_ab3 = gathered[2].reshape(1, d, d)
    win_ssm3 = gathered[3].reshape(1, d, d)
    wout_ssm3 = gathered[4]
    wglu = gathered[5].reshape(w_glu.shape[2], w_glu.shape[2])
    d_skip_all = gathered[6].reshape(1, -1)
    b_glu_all = gathered[7].reshape(1, -1)

    b_cols = lax.dynamic_slice_in_dim(b_mod, me * mod_cols, mod_cols, axis=1)
    (mod_all,) = all_gather([mod_part(c_all, w_mod, b_cols)], "gather_mod")
    mod_mine = lax.dynamic_index_in_dim(mod_all, me, axis=2, keepdims=False)
    mod_rows = jnp.transpose(mod_mine, (1, 0, 2)).reshape(2, 3, 1, d)

    def rows(a, i):
        return a[i].reshape(1, d)

    shift0, scale0, gate0 = mod_rows[0, 0], mod_rows[0, 1], mod_rows[0, 2]
    h0 = prenorm_fwd(x0, rows(ln_pre_g, 0), shift0, scale0, "prenorm0")
    proj0 = mm_nn(h0, win_ab3, bf16, "proj0")
    sgu_b3 = sgu_b[0].reshape(nh, HEAD, 1)
    out_a = sgu_fwd(proj0, sgu_norm_g, sgu_w[0], sgu_b3)
    out_b, att, tot = sb_fwd(proj0, nh)
    cat = jnp.concatenate([out_a, out_b], axis=1)
    y0 = mm_nn(cat, wout_ab3, f32, "out0")
    x1 = post_fwd(x0, y0, gate0, rows(ln_post_g, 0), "post0")

    shift1, scale1, gate1 = mod_rows[1, 0], mod_rows[1, 1], mod_rows[1, 2]
    h1 = prenorm_fwd(x1, rows(ln_pre_g, 1), shift1, scale1, "prenorm1")
    proj1 = mm_nn(h1, win_ssm3, bf16, "proj1")
    w_ssm = proj1.shape[1] // 2
    ldt = log_dt[0].reshape(n_grp, 1)
    bt_re = jnp.transpose(b_re[0], (0, 2, 1))
    bt_im = jnp.transpose(b_im[0], (0, 2, 1))
    a_re, a_im, bbt_re, bbt_im = s5_params_fwd(lam_re[0], lam_im[0], ldt, bt_re, bt_im)
    bre3 = _block_diag(bbt_re).astype(bf16)
    bim3 = _block_diag(bbt_im).astype(bf16)
    cre3 = _block_diag(jnp.transpose(c_re[0], (0, 2, 1))).astype(bf16)
    cimn3 = _block_diag(-jnp.transpose(c_im[0], (0, 2, 1))).astype(bf16)
    a_re_row, a_im_row = a_re.reshape(1, -1), a_im.reshape(1, -1)
    u = proj1[:, :w_ssm]
    y_ssm, hs_re, hs_im = ssm_fwd(u, bre3, bim3, cre3, cimn3, a_re_row, a_im_row, d_skip_all)
    g_act, t_glu, mix1 = glu_fwd(y_ssm, proj1, wglu, b_glu_all)
    y1 = mm_nn(mix1, wout_ssm3, f32, "out1")

    dx2, loss_tile = final_loss(x1, y1, gate1, rows(ln_post_g, 1), target)
    loss = lax.psum(loss_tile[0, 0] * (0.5 / d), ("x", "y", "c"))

    dy1, dgate1, dgpost1 = post_bwd(dx2, y1, gate1, rows(ln_post_g, 1), "post1_bwd")
    dmix1 = mm_nt(dy1, wout_ssm3, f32, "dmix1")
    gw_out_ssm = mm_tn(mix1, dy1, N_DEV, bf16, "gw_out_ssm")
    dy_ssm, dz1, dt_glu, db_glu = glu_bwd(dmix1, y_ssm, t_glu, proj1, wglu)
    gw_glu = mm_tn(g_act, dt_glu, 1, bf16, "gw_glu").reshape(N_DEV, -1, w_ssm)
    du, dd_skip, da_re, da_im, dbre3, dbim3, dcre3, dcimn3 = ssm_bwd(
        dy_ssm, u, hs_re, hs_im, bre3, bim3, cre3, cimn3, a_re_row, a_im_row, d_skip_all)
    dproj1 = jnp.concatenate([du, dz1], axis=1)
    dh1 = mm_nt(dproj1, win_ssm3, f32, "dh1")
    gw_in_ssm = mm_tn(h1, dproj1, 1, bf16, "gw_in_ssm").reshape(N_DEV, -1, proj1.shape[1])
    dx1, dshift1, dscale1, dgpre1 = prenorm_bwd(dh1, x1, dx2, rows(ln_pre_g, 1), scale1, "prenorm1_bwd")
    dlr, dli, dldt, dbt_re, dbt_im = s5_params_bwd(
        lam_re[0], lam_im[0], ldt, bt_re, bt_im, da_re.reshape(n_grp, n_st), da_im.reshape(n_grp, n_st),
        _diag_blocks(dbre3, SSM_GROUP, n_st), _diag_blocks(dbim3, SSM_GROUP, n_st))
    g_b_re = jnp.transpose(dbt_re, (0, 2, 1))
    g_b_im = jnp.transpose(dbt_im, (0, 2, 1))
    g_c_re = jnp.transpose(_diag_blocks(dcre3, n_st, SSM_GROUP), (0, 2, 1))
    g_c_im = -jnp.transpose(_diag_blocks(dcimn3, n_st, SSM_GROUP), (0, 2, 1))

    dy0, dgate0, dgpost0 = post_bwd(dx1, y0, gate0, rows(ln_post_g, 0), "post0_bwd")
    dcat = mm_nt(dy0, wout_ab3, f32, "dcat")
    gw_out_ab = mm_tn(cat, dy0, 1, bf16, "gw_out_ab").reshape(N_DEV, -1, d)
    da, dsgu_w, dsgu_b, dsgu_ng = sgu_bwd(proj0, dcat, sgu_norm_g, sgu_w[0], sgu_b3)
    dq, dk, dv, dbz = sb_bwd(proj0, dcat, att, tot, nh)
    dproj0 = jnp.concatenate([da, dq, dk, dv, dbz], axis=1)
    dh0 = mm_nt(dproj0, win_ab3, f32, "dh0")
    gw_in_ab = mm_tn(h0, dproj0, N_DEV, bf16, "gw_in_ab")
    dx0, dshift0, dscale0, dgpre0 = prenorm_bwd(dh0, x0, dx1, rows(ln_pre_g, 0), scale0, "prenorm0_bwd")

    pieces = all_to_all(
        [gw_in_ab, gw_out_ab, gw_in_ssm, gw_out_ssm, gw_glu,
         dd_skip.reshape(N_DEV, 1, -1), db_glu.reshape(N_DEV, 1, -1)], "scatter_grads")

    def sharded(p, w, m, v, name):
        shp = w.shape
        w2, m2, v2 = (a.reshape(-1, shp[-1]) for a in (w, m, v))
        return [o.reshape(shp) for o in adam_reduce(p.reshape(N_DEV, -1, shp[-1]), w2, m2, v2, name)]

    r_w_in_ab = sharded(pieces[0], w_in_ab, m_w_in_ab, v_w_in_ab, "adam_w_in_ab")
    r_w_out_ab = sharded(pieces[1], w_out_ab, m_w_out_ab, v_w_out_ab, "adam_w_out_ab")
    r_w_in_ssm = sharded(pieces[2], w_in_ssm, m_w_in_ssm, v_w_in_ssm, "adam_w_in_ssm")
    r_w_out_ssm = sharded(pieces[3], w_out_ssm, m_w_out_ssm, v_w_out_ssm, "adam_w_out_ssm")
    r_w_glu = sharded(pieces[4], w_glu, m_w_glu, v_w_glu, "adam_w_glu")
    r_d_skip = sharded(pieces[5], d_skip, m_d_skip, v_d_skip, "adam_d_skip")
    r_b_glu = sharded(pieces[6], b_glu, m_b_glu, v_b_glu, "adam_b_glu")

    small_names = ["ln_pre_g", "ln_post_g", "b_mod", "sgu_norm_g", "sgu_w", "sgu_b", "lam_re", "lam_im", "b_re", "b_im",
                   "c_re", "c_im", "log_dt"]
    small_w = [ln_pre_g, ln_post_g, b_mod, sgu_norm_g, sgu_w, sgu_b, lam_re, lam_im, b_re, b_im, c_re, c_im, log_dt]
    small_m = [m_ln_pre_g, m_ln_post_g, m_b_mod, m_sgu_norm_g, m_sgu_w, m_sgu_b, m_lam_re, m_lam_im, m_b_re, m_b_im,
               m_c_re, m_c_im, m_log_dt]
    small_v = [v_ln_pre_g, v_ln_post_g, v_b_mod, v_sgu_norm_g, v_sgu_w, v_sgu_b, v_lam_re, v_lam_im, v_b_re, v_b_im,
               v_c_re, v_c_im, v_log_dt]
    dmod = jnp.concatenate([dshift0, dscale0, dgate0, dshift1, dscale1, dgate1], axis=1)
    small_g = [jnp.concatenate([dgpre0, dgpre1]), jnp.concatenate([dgpost0, dgpost1]), dmod, dsgu_ng, dsgu_w, dsgu_b,
               dlr, dli, g_b_re, g_b_im, g_c_re, g_c_im, dldt]
    shapes = [w.shape for w in small_w]
    (g_all,) = all_gather([_pack(small_g)], "gather_small_grads")
    r_small = [_unpack(o, shapes) for o in adam_small(g_all, _pack(small_w), _pack(small_m), _pack(small_v))]
    small = {n: [r_small[k][i] for k in range(4)] for i, n in enumerate(small_names)}

    n_rows_before = sum(-(-math.prod(s) // PACK_ROW) * SUBLANES for s in shapes[:2])
    n_rows = math.prod(b_mod.shape) // HEAD
    dmod_all = g_all[:, n_rows_before:n_rows_before + n_rows].reshape(N_DEV, 2, 3 * d)
    dm_cols = jnp.transpose(lax.dynamic_slice_in_dim(dmod_all, me * mod_cols, mod_cols, axis=2), (1, 0, 2))
    cond_t = jnp.transpose(silu_rows(c_all))
    r_w_mod = adam_w_mod(cond_t, dm_cols, w_mod, m_w_mod, v_w_mod)

    res = dict(small)
    res.update(w_mod=r_w_mod, w_in_ab=r_w_in_ab, w_out_ab=r_w_out_ab, w_in_ssm=r_w_in_ssm, w_out_ssm=r_w_out_ssm,
               d_skip=r_d_skip, w_glu=r_w_glu, b_glu=r_b_glu)
    order = ["ln_pre_g", "ln_post_g", "w_mod", "b_mod", "w_in_ab", "w_out_ab", "sgu_norm_g", "sgu_w", "sgu_b", "w_in_ssm",
             "w_out_ssm", "lam_re", "lam_im", "b_re", "b_im", "c_re", "c_im", "d_skip", "log_dt", "w_glu", "b_glu"]
    outs = [loss, dx0.reshape(x.shape)]
    for k in range(4):
        outs += [res[n][k] for n in order]
    return tuple(outs)
```

```python
import functools
import math

import jax
import jax.numpy as jnp
from jax import lax
from jax.experimental import pallas as pl
from jax.experimental.pallas import tpu as pltpu

f32 = jnp.float32
bf16 = jnp.bfloat16

N_DEV = 8
EPS = 1e-6
HEAD = 128
SUBLANES = 8
SSM_GROUP = 16
SSM_STATE = 64
GROUPS_PER_LANE_BLOCK = HEAD // SSM_GROUP
STATES_PER_LANE_BLOCK = GROUPS_PER_LANE_BLOCK * SSM_STATE
VMEM_LIMIT = 56 * 2 ** 20
ADAM_LR, ADAM_B1, ADAM_B2, ADAM_EPS, ADAM_WD, ADAM_STEP = 0.001, 0.9, 0.999, 1e-08, 0.01, 10
_GELU_C0 = math.sqrt(2.0 / math.pi)
_GELU_C1 = 0.044715
MESH = pl.DeviceIdType.MESH


def _cparams(*sem):
    return pltpu.CompilerParams(dimension_semantics=sem if sem else None, vmem_limit_bytes=VMEM_LIMIT)


def _gelu(x):
    return 0.5 * x * (1.0 + jnp.tanh(_GELU_C0 * (x + _GELU_C1 * x * x * x)))


def _gelu_grad(x):
    t = jnp.tanh(_GELU_C0 * (x + _GELU_C1 * x * x * x))
    return 0.5 * (1.0 + t) + 0.5 * x * (1.0 - t * t) * _GELU_C0 * (1.0 + 3.0 * _GELU_C1 * x * x)


def _silu(x):
    return x * jax.nn.sigmoid(x)


def _silu_grad(x):
    s = jax.nn.sigmoid(x)
    return s * (1.0 + x * (1.0 - s))


def _dot(a, b):
    return jnp.dot(a, b, preferred_element_type=f32)


def _dot_nt(a, b):
    return lax.dot_general(a, b, (((1,), (1,)), ((), ())), preferred_element_type=f32)


def _dot_tn(a, b):
    return lax.dot_general(a, b, (((0,), (0,)), ((), ())), preferred_element_type=f32)


def _split_bf16(v):
    hi = v.astype(bf16)
    lo = (v - hi.astype(f32)).astype(bf16)
    return hi, lo


def _row(d):
    return pl.BlockSpec((1, d), lambda *_: (0, 0))


def _my_index():
    return 4 * lax.axis_index("x") + 2 * lax.axis_index("y") + lax.axis_index("c")


def _peer(k):
    x, y, c = lax.axis_index("x"), lax.axis_index("y"), lax.axis_index("c")
    return (1 - x if k & 4 else x, 1 - y if k & 2 else y, 1 - c if k & 1 else c)


def all_gather(arrs, name):
    n = len(arrs)

    def body(*refs):
        ins, outs = refs[:n], refs[n:2 * n]
        send, recv, local = refs[2 * n:]
        me = _my_index()
        copies = []
        for a in range(n):
            cp = pltpu.make_async_copy(ins[a], outs[a].at[me], local.at[a])
            cp.start()
            copies.append(cp)
            for k in range(1, N_DEV):
                s = a * (N_DEV - 1) + k - 1
                cp = pltpu.make_async_remote_copy(src_ref=ins[a], dst_ref=outs[a].at[me], send_sem=send.at[s],
                                                  recv_sem=recv.at[s], device_id=_peer(k), device_id_type=MESH)
                cp.start()
                copies.append(cp)
        for cp in copies:
            cp.wait()

    any_spec = pl.BlockSpec(memory_space=pl.ANY)
    outs = pl.pallas_call(
        body, name=name,
        out_shape=[jax.ShapeDtypeStruct((N_DEV,) + a.shape, a.dtype) for a in arrs],
        in_specs=[any_spec] * n, out_specs=[any_spec] * n,
        scratch_shapes=[pltpu.SemaphoreType.DMA((n * (N_DEV - 1),)), pltpu.SemaphoreType.DMA((n * (N_DEV - 1),)),
                        pltpu.SemaphoreType.DMA((n,))],
        compiler_params=pltpu.CompilerParams(has_side_effects=True),
    )(*arrs)
    return list(outs)


_HBM = pl.BlockSpec(memory_space=pltpu.HBM)
_SEM = pl.BlockSpec(memory_space=pltpu.SEMAPHORE)
_EFFECT = pltpu.SideEffectType.DATAFLOW_SIDE_EFFECTING
GATHER, SCATTER = "gather", "scatter"


def _in_hbm(a):
    return pltpu.with_memory_space_constraint(a, pltpu.HBM)


def _exchange_copies(kind, srcs, lands, send, recv):
    me = _my_index()
    copies = []
    for a, (src, land) in enumerate(zip(srcs, lands)):
        for k in range(1, N_DEV):
            s = a * (N_DEV - 1) + k - 1
            copies.append(pltpu.make_async_remote_copy(
                src_ref=src if kind == GATHER else src.at[jnp.bitwise_xor(me, k)], dst_ref=land.at[me],
                send_sem=send.at[s], recv_sem=recv.at[s], device_id=_peer(k), device_id_type=MESH))
    return copies


def exchange_start(kind, arrs, thru, name):
    n, nt = len(arrs), len(thru)
    n_sem = n * (N_DEV - 1)
    land_shapes = [((N_DEV,) + a.shape if kind == GATHER else a.shape) for a in arrs]

    def body(*refs):
        srcs, lands = refs[:n], refs[n:2 * n]
        send, recv = refs[2 * n + nt], refs[2 * n + nt + 1]
        local = refs[-1]
        me = _my_index()
        mine = [pltpu.make_async_copy(src if kind == GATHER else src.at[me], land.at[me], local.at[a])
                for a, (src, land) in enumerate(zip(srcs, lands))]
        for cp in mine:
            cp.start()
        for cp in _exchange_copies(kind, srcs, lands, send, recv):
            cp.start()
        for cp in mine:
            cp.wait()

    hbm_out = [pltpu.HBM(a.shape, a.dtype) for a in arrs] + [pltpu.HBM(s, a.dtype) for s, a in zip(land_shapes, arrs)] \
        + [pltpu.HBM(t.shape, t.dtype) for t in thru]
    outs = pl.pallas_call(
        body, name=name,
        out_shape=[pltpu.SemaphoreType.DMA((n_sem,)), pltpu.SemaphoreType.DMA((n_sem,))] + hbm_out,
        in_specs=[_HBM] * (2 * n + nt), out_specs=[_SEM, _SEM] + [_HBM] * (2 * n + nt),
        input_output_aliases={i: 2 + i for i in range(2 * n + nt)},
        scratch_shapes=[pltpu.SemaphoreType.DMA((n,))],
        compiler_params=pltpu.CompilerParams(has_side_effects=_EFFECT),
    )(*[_in_hbm(a) for a in arrs], *[_in_hbm(lax.empty(s, a.dtype)) for s, a in zip(land_shapes, arrs)],
      *[_in_hbm(t) for t in thru])
    return (kind, outs[0], outs[1], outs[2:2 + n], outs[2 + n:2 + 2 * n]), list(outs[2 + 2 * n:])


def exchange_wait(handle, after, name):
    kind, send, recv, srcs, lands = handle
    n = len(srcs)

    def body(*refs):
        for cp in _exchange_copies(kind, refs[:n], refs[n:2 * n], refs[2 * n], refs[2 * n + 1]):
            cp.wait_send()
            cp.wait_recv()

    outs = pl.pallas_call(
        body, name=name,
        out_shape=[pltpu.HBM(a.shape, a.dtype) for a in list(srcs) + list(lands)],
        in_specs=[_HBM] * (2 * n) + [_SEM, _SEM, pl.BlockSpec(memory_space=pl.ANY)], out_specs=[_HBM] * (2 * n),
        input_output_aliases={i: i for i in range(2 * n)},
        compiler_params=pltpu.CompilerParams(has_side_effects=_EFFECT),
    )(*srcs, *lands, send, recv, after)
    return list(outs[n:])


def _tile(n, pref):
    for t in pref:
        if n % t == 0:
            return t
    return n


def mm_nn(a, b3, out_dtype, name):
    m, k = a.shape
    nb, _, bn = b3.shape
    tm = _tile(m, (512, 256, 128))
    tn = _tile(bn, (1024, 896, 512, 256, 128))
    per = bn // tn

    def body(a_ref, b_ref, o_ref):
        o_ref[...] = _dot(a_ref[...], b_ref[...]).astype(o_ref.dtype)

    return pl.pallas_call(
        body, name=name, grid=(m // tm, nb, per),
        in_specs=[pl.BlockSpec((tm, k), lambda i, j, jj: (i, 0)),
                  pl.BlockSpec((None, k, tn), lambda i, j, jj: (j, 0, jj))],
        out_specs=pl.BlockSpec((tm, tn), lambda i, j, jj: (i, j * per + jj)),
        out_shape=jax.ShapeDtypeStruct((m, nb * bn), out_dtype),
        compiler_params=_cparams("parallel", "arbitrary", "arbitrary"),
    )(a, b3)


def mm_nt(a, w3, out_dtype, name):
    m, _ = a.shape
    nb, ko, bn = w3.shape
    tm = _tile(m, (512, 256, 128))
    tko = _tile(ko, (1024, 512, 256, 128))

    def body(a_ref, w_ref, o_ref, acc_ref):
        j = pl.program_id(2)

        @pl.when(j == 0)
        def _():
            acc_ref[...] = jnp.zeros_like(acc_ref)

        acc_ref[...] += _dot_nt(a_ref[...], w_ref[...])

        @pl.when(j == nb - 1)
        def _():
            o_ref[...] = acc_ref[...].astype(o_ref.dtype)

    return pl.pallas_call(
        body, name=name, grid=(m // tm, ko // tko, nb),
        in_specs=[pl.BlockSpec((tm, bn), lambda i, o, j: (i, j)),
                  pl.BlockSpec((None, tko, bn), lambda i, o, j: (j, o, 0))],
        out_specs=pl.BlockSpec((tm, tko), lambda i, o, j: (i, o)),
        out_shape=jax.ShapeDtypeStruct((m, ko), out_dtype),
        scratch_shapes=[pltpu.VMEM((tm, tko), f32)],
        compiler_params=_cparams("parallel", "arbitrary", "arbitrary"),
    )(a, w3)


def mm_tn(a, dy, ncb, out_dtype, name):
    l, ka = a.shape
    _, n = dy.shape
    bn = n // ncb
    tl = _tile(l, (512, 256, 128))
    tka = _tile(ka, (512, 256, 128))
    tn = _tile(bn, (1024, 896, 512, 256, 128))
    per = bn // tn
    nl = l // tl

    def body(a_ref, dy_ref, o_ref, acc_ref):
        s = pl.program_id(2)

        @pl.when(s == 0)
        def _():
            acc_ref[...] = jnp.zeros_like(acc_ref)

        acc_ref[...] += _dot_tn(a_ref[...], dy_ref[...])

        @pl.when(s == nl - 1)
        def _():
            o_ref[...] = acc_ref[...].astype(o_ref.dtype)

    return pl.pallas_call(
        body, name=name, grid=(ka // tka, n // tn, nl),
        in_specs=[pl.BlockSpec((tl, tka), lambda i, j, s: (s, i)),
                  pl.BlockSpec((tl, tn), lambda i, j, s: (s, j))],
        out_specs=pl.BlockSpec((None, tka, tn), lambda i, j, s: (j // per, i, j % per)),
        out_shape=jax.ShapeDtypeStruct((ncb, ka, bn), out_dtype),
        scratch_shapes=[pltpu.VMEM((tka, tn), f32)],
        compiler_params=_cparams("parallel", "parallel", "arbitrary"),
    )(a, dy)


def mod_part(c_all, w_mod, b_cols):
    nl, d, cols = w_mod.shape

    def body(c_ref, w_ref, b_ref, o_ref):
        cond = _silu(c_ref[...]).astype(bf16)
        o_ref[...] = _dot(cond, w_ref[...].astype(bf16)) + b_ref[...]

    return pl.pallas_call(
        body, name="mod_part", grid=(nl,),
        in_specs=[pl.BlockSpec((N_DEV, d), lambda l: (0, 0)),
                  pl.BlockSpec((None, d, cols), lambda l: (l, 0, 0)),
                  pl.BlockSpec((None, 1, cols), lambda l: (l, 0, 0))],
        out_specs=pl.BlockSpec((None, N_DEV, cols), lambda l: (l, 0, 0)),
        out_shape=jax.ShapeDtypeStruct((nl, N_DEV, cols), f32),
        compiler_params=_cparams("arbitrary"),
    )(c_all, w_mod, b_cols.reshape(nl, 1, cols))


def _row_tile(l):
    return _tile(l, (256, 128))


def prenorm_fwd(x, g, shift, scale, name):
    l, d = x.shape
    tm = _row_tile(l)

    def body(x_ref, g_ref, sh_ref, sc_ref, h_ref):
        xv = x_ref[...]
        r = lax.rsqrt(jnp.mean(xv * xv, axis=-1, keepdims=True) + EPS)
        h_ref[...] = (xv * r * (g_ref[...] * (1.0 + sc_ref[...])) + sh_ref[...]).astype(h_ref.dtype)

    return pl.pallas_call(
        body, name=name, grid=(l // tm,),
        in_specs=[pl.BlockSpec((tm, d), lambda i: (i, 0)), _row(d), _row(d), _row(d)],
        out_specs=pl.BlockSpec((tm, d), lambda i: (i, 0)),
        out_shape=jax.ShapeDtypeStruct((l, d), bf16),
        compiler_params=_cparams("parallel"),
    )(x, g, shift, scale)


def post_fwd(x, y, gate, g, name):
    l, d = x.shape
    tm = _row_tile(l)

    def body(x_ref, y_ref, gate_ref, g_ref, o_ref):
        yv = y_ref[...]
        r = lax.rsqrt(jnp.mean(yv * yv, axis=-1, keepdims=True) + EPS)
        o_ref[...] = x_ref[...] + gate_ref[...] * (yv * r * g_ref[...])

    blk = pl.BlockSpec((tm, d), lambda i: (i, 0))
    return pl.pallas_call(
        body, name=name, grid=(l // tm,),
        in_specs=[blk, blk, _row(d), _row(d)], out_specs=blk,
        out_shape=jax.ShapeDtypeStruct((l, d), f32),
        compiler_params=_cparams("parallel"),
    )(x, y, gate, g)


def final_loss(x, y, gate, g, target):
    l, d = x.shape
    tm = _row_tile(l)

    def body(x_ref, y_ref, gate_ref, g_ref, t_ref, dx_ref, loss_ref):
        @pl.when(pl.program_id(0) == 0)
        def _():
            loss_ref[...] = jnp.zeros_like(loss_ref)

        yv = y_ref[...]
        r = lax.rsqrt(jnp.mean(yv * yv, axis=-1, keepdims=True) + EPS)
        diff = x_ref[...] + gate_ref[...] * (yv * r * g_ref[...]) - t_ref[...]
        dx_ref[...] = diff * (1.0 / d)
        loss_ref[...] += jnp.sum(diff * diff)

    blk = pl.BlockSpec((tm, d), lambda i: (i, 0))
    return pl.pallas_call(
        body, name="final_loss", grid=(l // tm,),
        in_specs=[blk, blk, _row(d), _row(d), blk],
        out_specs=[blk, pl.BlockSpec((SUBLANES, HEAD), lambda i: (0, 0))],
        out_shape=[jax.ShapeDtypeStruct((l, d), f32), jax.ShapeDtypeStruct((SUBLANES, HEAD), f32)],
        compiler_params=_cparams("arbitrary"),
    )(x, y, gate, g, target)


def post_bwd(dx, y, gate, g, name):
    l, d = dx.shape
    tm = _row_tile(l)

    def body(dx_ref, y_ref, gate_ref, g_ref, dy_ref, dgate_ref, dg_ref):
        @pl.when(pl.program_id(0) == 0)
        def _():
            dgate_ref[...] = jnp.zeros_like(dgate_ref)
            dg_ref[...] = jnp.zeros_like(dg_ref)

        yv, dxv, gv = y_ref[...], dx_ref[...], g_ref[...]
        r = lax.rsqrt(jnp.mean(yv * yv, axis=-1, keepdims=True) + EPS)
        yn = yv * r
        dgate_ref[...] += jnp.sum(dxv * yn * gv, axis=0, keepdims=True)
        dyg = dxv * gate_ref[...]
        dg_ref[...] += jnp.sum(dyg * yn, axis=0, keepdims=True)
        dyn = dyg * gv
        dy_ref[...] = (r * (dyn - yn * jnp.mean(dyn * yn, axis=-1, keepdims=True))).astype(dy_ref.dtype)

    blk = pl.BlockSpec((tm, d), lambda i: (i, 0))
    return pl.pallas_call(
        body, name=name, grid=(l // tm,),
        in_specs=[blk, blk, _row(d), _row(d)], out_specs=[blk, _row(d), _row(d)],
        out_shape=[jax.ShapeDtypeStruct((l, d), bf16), jax.ShapeDtypeStruct((1, d), f32),
                   jax.ShapeDtypeStruct((1, d), f32)],
        compiler_params=_cparams("arbitrary"),
    )(dx, y, gate, g)


def prenorm_bwd(dh, x, dx_next, g, scale, name):
    l, d = x.shape
    tm = _row_tile(l)

    def body(dh_ref, x_ref, dxn_ref, g_ref, sc_ref, dx_ref, dsh_ref, dsc_ref, dg_ref):
        @pl.when(pl.program_id(0) == 0)
        def _():
            dsh_ref[...] = jnp.zeros_like(dsh_ref)
            dsc_ref[...] = jnp.zeros_like(dsc_ref)
            dg_ref[...] = jnp.zeros_like(dg_ref)

        xv, dhv, gv, sc1 = x_ref[...], dh_ref[...], g_ref[...], 1.0 + sc_ref[...]
        r = lax.rsqrt(jnp.mean(xv * xv, axis=-1, keepdims=True) + EPS)
        xn = xv * r
        dhx = dhv * xn
        dsh_ref[...] += jnp.sum(dhv, axis=0, keepdims=True)
        dsc_ref[...] += jnp.sum(dhx * gv, axis=0, keepdims=True)
        dg_ref[...] += jnp.sum(dhx * sc1, axis=0, keepdims=True)
        dxn = dhv * (gv * sc1)
        dx_ref[...] = dxn_ref[...] + r * (dxn - xn * jnp.mean(dxn * xn, axis=-1, keepdims=True))

    blk = pl.BlockSpec((tm, d), lambda i: (i, 0))
    return pl.pallas_call(
        body, name=name, grid=(l // tm,),
        in_specs=[blk, blk, blk, _row(d), _row(d)], out_specs=[blk, _row(d), _row(d), _row(d)],
        out_shape=[jax.ShapeDtypeStruct((l, d), f32)] + [jax.ShapeDtypeStruct((1, d), f32)] * 3,
        compiler_params=_cparams("arbitrary"),
    )(dh, x, dx_next, g, scale)


def _tril_mask():
    r = lax.broadcasted_iota(jnp.int32, (HEAD, HEAD), 0)
    c = lax.broadcasted_iota(jnp.int32, (HEAD, HEAD), 1)
    return r >= c


def sgu_fwd(proj, norm_g, w_s, b_s):
    l = proj.shape[0]
    nh = w_s.shape[0]
    wa = nh * HEAD

    def body(au_ref, av_ref, az_ref, ng_ref, w_ref, b_ref, o_ref):
        tril = _tril_mask()
        for h in range(nh):
            sl = slice(h * HEAD, (h + 1) * HEAD)
            gv = _gelu(av_ref[:, sl].astype(f32))
            r = lax.rsqrt(jnp.mean(gv * gv, axis=-1, keepdims=True) + EPS)
            vh = gv * r * ng_ref[:, sl]
            wm = jnp.where(tril, w_ref[h], 0.0).astype(bf16)
            s = _dot(wm, vh.astype(bf16)) + b_ref[h]
            o_ref[:, sl] = (_gelu(au_ref[:, sl].astype(f32)) * s * _silu(az_ref[:, sl].astype(f32))).astype(o_ref.dtype)

    def col(j):
        return pl.BlockSpec((HEAD, wa), lambda n: (n, j))

    return pl.pallas_call(
        body, name="sgu_fwd", grid=(l // HEAD,),
        in_specs=[col(0), col(1), col(2), _row(wa),
                  pl.BlockSpec((nh, HEAD, HEAD), lambda n: (0, 0, 0)), pl.BlockSpec((nh, HEAD, 1), lambda n: (0, 0, 0))],
        out_specs=pl.BlockSpec((HEAD, wa), lambda n: (n, 0)),
        out_shape=jax.ShapeDtypeStruct((l, wa), bf16),
        compiler_params=_cparams("parallel"),
    )(proj, proj, proj, norm_g, w_s, b_s)


def sgu_bwd(proj, dcat, norm_g, w_s, b_s):
    l = proj.shape[0]
    nh = w_s.shape[0]
    wa = nh * HEAD

    def body(au_ref, av_ref, az_ref, do_ref, ng_ref, w_ref, b_ref, da_ref, dw_ref, db_ref, dng_ref):
        @pl.when(pl.program_id(0) == 0)
        def _():
            dw_ref[...] = jnp.zeros_like(dw_ref)
            db_ref[...] = jnp.zeros_like(db_ref)
            dng_ref[...] = jnp.zeros_like(dng_ref)

        tril = _tril_mask()
        for h in range(nh):
            sl = slice(h * HEAD, (h + 1) * HEAD)
            au, av, az = au_ref[:, sl].astype(f32), av_ref[:, sl].astype(f32), az_ref[:, sl].astype(f32)
            ng = ng_ref[:, sl]
            gv = _gelu(av)
            r = lax.rsqrt(jnp.mean(gv * gv, axis=-1, keepdims=True) + EPS)
            gvn = gv * r
            vh = (gvn * ng).astype(bf16)
            wm = jnp.where(tril, w_ref[h], 0.0).astype(bf16)
            s = _dot(wm, vh) + b_ref[h]
            gu, sz = _gelu(au), _silu(az)
            dov = do_ref[:, sl].astype(f32)
            da_ref[:, sl] = (dov * s * sz * _gelu_grad(au)).astype(da_ref.dtype)
            da_ref[:, 2 * wa + h * HEAD:2 * wa + (h + 1) * HEAD] = (dov * gu * s * _silu_grad(az)).astype(da_ref.dtype)
            ds = dov * gu * sz
            db_ref[h] += jnp.sum(ds, axis=-1, keepdims=True)
            dsb = ds.astype(bf16)
            dw_ref[h] += jnp.where(tril, _dot_nt(dsb, vh), 0.0)
            dvh = _dot_tn(wm, dsb)
            dng_ref[:, sl] += jnp.sum(dvh * gvn, axis=0, keepdims=True)
            dgvn = dvh * ng
            dgv = r * (dgvn - gvn * jnp.mean(dgvn * gvn, axis=-1, keepdims=True))
            da_ref[:, wa + h * HEAD:wa + (h + 1) * HEAD] = (dgv * _gelu_grad(av)).astype(da_ref.dtype)

    def col(j):
        return pl.BlockSpec((HEAD, wa), lambda n: (n, j))

    whole_w = pl.BlockSpec((nh, HEAD, HEAD), lambda n: (0, 0, 0))
    whole_b = pl.BlockSpec((nh, HEAD, 1), lambda n: (0, 0, 0))
    return pl.pallas_call(
        body, name="sgu_bwd", grid=(l // HEAD,),
        in_specs=[col(0), col(1), col(2), col(0), _row(wa), whole_w, whole_b],
        out_specs=[pl.BlockSpec((HEAD, 3 * wa), lambda n: (n, 0)), whole_w, whole_b, _row(wa)],
        out_shape=[jax.ShapeDtypeStruct((l, 3 * wa), bf16), jax.ShapeDtypeStruct((nh, HEAD, HEAD), f32),
                   jax.ShapeDtypeStruct((nh, HEAD, 1), f32), jax.ShapeDtypeStruct((1, wa), f32)],
        compiler_params=_cparams("arbitrary"),
    )(proj, proj, proj, dcat, norm_g, w_s, b_s)


_LOG2E = 1.0 / math.log(2.0)


def _sb_scores(q, k, scale):
    z = _dot_nt(q, k) * (scale * _LOG2E)
    return z, jnp.maximum(z, 0.0) + jnp.log2(1.0 + jnp.exp2(-jnp.abs(z)))


def _sb_sum_matrix(tri):
    s = lax.broadcasted_iota(jnp.int32, (2 * HEAD, 2 * HEAD), 0) % HEAD
    j = lax.broadcasted_iota(jnp.int32, (2 * HEAD, 2 * HEAD), 1)
    return jnp.where(jnp.logical_or(j >= HEAD, tri(s, j)), 1.0, 0.0).astype(bf16)


def _sb_sums(x, sums):
    c2 = _dot(jnp.concatenate(_split_bf16(x), axis=1), sums)
    return c2[:, :HEAD], c2[:, HEAD:]


def _sb_q_tile(l):
    return _tile(l, (512, 256, 128))


def _sb_heads_per_step(nh, most):
    return _tile(nh, tuple(h for h in (4, 2) if h <= most))


def sb_fwd(proj, nh):
    l = proj.shape[0]
    wb = nh * HEAD
    tq = _sb_q_tile(l)
    band = tq // HEAD
    hp = _sb_heads_per_step(nh, 4)
    scale = 1.0 / math.sqrt(HEAD)
    qc, kc, vc, zc = 3 * nh, 4 * nh, 5 * nh, 6 * nh

    def body(q_ref, k_ref, v_ref, bz_ref, o_ref, att_ref, tot_ref):
        i = pl.program_id(1)
        sums = _sb_sum_matrix(lambda s, j: s > j)
        t_pos = i * tq + lax.broadcasted_iota(jnp.int32, (tq, HEAD), 0)
        s_off = lax.broadcasted_iota(jnp.int32, (tq, HEAD), 1)

        def step(j, carry, masked):
            rows = pl.ds(pl.multiple_of(j * HEAD, HEAD), HEAD)
            out = []
            for e in range(hp):
                acc, tot = carry[e]
                sl = slice(e * HEAD, (e + 1) * HEAD)
                z, sp = _sb_scores(q_ref[:, sl], k_ref[rows, sl], scale)
                lb = z - sp
                if masked:
                    mask = s_off + j * HEAD < t_pos
                    sp = jnp.where(mask, sp, 0.0)
                later, total = _sb_sums(sp, sums)
                w = jnp.exp2(lb + tot - later)
                if masked:
                    w = jnp.where(mask, w, 0.0)
                out.append((acc + _dot(w.astype(bf16), v_ref[rows, sl]), tot - total))
            return tuple(out)

        zero = jnp.zeros((tq, HEAD), f32)
        carry = lax.fori_loop(0, band, lambda t, c: step(band * i + band - 1 - t, c, True), ((zero, zero),) * hp)
        carry = lax.fori_loop(0, band * i, lambda t, c: step(band * i - 1 - t, c, False), carry)
        for e in range(hp):
            acc, tot = carry[e]
            sl = slice(e * HEAD, (e + 1) * HEAD)
            att_ref[:, sl] = acc.astype(att_ref.dtype)
            o_ref[:, sl] = (acc * _silu(bz_ref[:, sl].astype(f32))).astype(o_ref.dtype)
            tot_ref[e] = tot[:, :1]

    blk = lambda c0: pl.BlockSpec((tq, hp * HEAD), lambda g, i: (i, c0 // hp + g))
    head = lambda c0: pl.BlockSpec((l, hp * HEAD), lambda g, i: (0, c0 // hp + g))
    return pl.pallas_call(
        body, name="sb_fwd", grid=(nh // hp, l // tq),
        in_specs=[blk(qc), head(kc), head(vc), blk(zc)],
        out_specs=[blk(0), blk(0), pl.BlockSpec((hp, tq, 1), lambda g, i: (g, i, 0))],
        out_shape=[jax.ShapeDtypeStruct((l, wb), bf16), jax.ShapeDtypeStruct((l, wb), bf16),
                   jax.ShapeDtypeStruct((nh, l, 1), f32)],
        compiler_params=_cparams("parallel", "arbitrary"),
    )(proj, proj, proj, proj)


def sb_bwd(proj, dcat, att, tot, nh):
    l = proj.shape[0]
    wb = nh * HEAD
    tq = _sb_q_tile(l)
    band = tq // HEAD
    nq = l // tq
    hp = _sb_heads_per_step(nh, 2)
    scale = 1.0 / math.sqrt(HEAD)
    qc, kc, vc, zc = 3 * nh, 4 * nh, 5 * nh, 6 * nh

    def body(q_ref, k_ref, v_ref, bz_ref, do_ref, att_ref, tot_ref, dq_ref, dk_ref, dv_ref, dbz_ref, dk_acc, dv_acc,
             dob_ref):
        i = pl.program_id(1)

        @pl.when(i == 0)
        def _():
            dk_acc[...] = jnp.zeros_like(dk_acc)
            dv_acc[...] = jnp.zeros_like(dv_acc)

        bz = bz_ref[...].astype(f32)
        dov = do_ref[...].astype(f32)
        dbz_ref[...] = (dov * att_ref[...].astype(f32) * _silu_grad(bz)).astype(dbz_ref.dtype)
        dob_ref[...] = (dov * _silu(bz)).astype(bf16)
        upto = _sb_sum_matrix(lambda s, j: s <= j)
        before = _sb_sum_matrix(lambda j, s: j < s)
        t_pos = i * tq + lax.broadcasted_iota(jnp.int32, (tq, HEAD), 0)
        s_off = lax.broadcasted_iota(jnp.int32, (tq, HEAD), 1)

        def step(j, carry, masked):
            rows = pl.ds(pl.multiple_of(j * HEAD, HEAD), HEAD)
            out = []
            for h in range(hp):
                dq, sp_seen, e_seen = carry[h]
                sl = slice(h * HEAD, (h + 1) * HEAD)
                q, kj, vj, dob = q_ref[:, sl], k_ref[rows, sl], v_ref[rows, sl], dob_ref[:, sl]
                z, sp = _sb_scores(q, kj, scale)
                lb = z - sp
                if masked:
                    mask = s_off + j * HEAD < t_pos
                    sp = jnp.where(mask, sp, 0.0)
                sp_upto, sp_total = _sb_sums(sp, upto)
                w = jnp.exp2(lb + sp_seen + sp_upto)
                if masked:
                    w = jnp.where(mask, w, 0.0)
                dv_acc[rows, sl] += _dot_tn(w.astype(bf16), dob)
                e = _dot_nt(dob, vj) * w
                e_before, e_total = _sb_sums(e, before)
                dz = (e - (e + e_seen + e_before) * jnp.exp2(lb)) * scale
                if masked:
                    dz = jnp.where(mask, dz, 0.0)
                dz = dz.astype(bf16)
                dk_acc[rows, sl] += _dot_tn(dz, q)
                out.append((dq + _dot(dz, kj), sp_seen + sp_total, e_seen + e_total))
            return tuple(out)

        zero = jnp.zeros((tq, HEAD), f32)
        init = tuple((zero, jnp.broadcast_to(tot_ref[h], (tq, HEAD)), zero) for h in range(hp))
        carry = lax.fori_loop(0, band * i, lambda j, c: step(j, c, False), init)
        carry = lax.fori_loop(0, band, lambda t, c: step(band * i + t, c, True), carry)
        for h in range(hp):
            dq_ref[:, h * HEAD:(h + 1) * HEAD] = carry[h][0].astype(dq_ref.dtype)

        @pl.when(i == nq - 1)
        def _():
            dk_ref[...] = dk_acc[...].astype(dk_ref.dtype)
            dv_ref[...] = dv_acc[...].astype(dv_ref.dtype)

    blk = lambda c0: pl.BlockSpec((tq, hp * HEAD), lambda g, i: (i, c0 // hp + g))
    head = lambda c0: pl.BlockSpec((l, hp * HEAD), lambda g, i: (0, c0 // hp + g))
    return pl.pallas_call(
        body, name="sb_bwd", grid=(nh // hp, nq),
        in_specs=[blk(qc), head(kc), head(vc), blk(zc), blk(nh), blk(0),
                  pl.BlockSpec((hp, tq, 1), lambda g, i: (g, i, 0))],
        out_specs=[blk(0), head(0), head(0), blk(0)],
        out_shape=[jax.ShapeDtypeStruct((l, wb), bf16)] * 4,
        scratch_shapes=[pltpu.VMEM((l, hp * HEAD), f32), pltpu.VMEM((l, hp * HEAD), f32),
                        pltpu.VMEM((tq, hp * HEAD), bf16)],
        compiler_params=_cparams("parallel", "arbitrary"),
    )(proj, proj, proj, proj, dcat, att, tot)


def _disc(lr, li, ldt):
    dt = jnp.exp(ldt)
    mag = jnp.exp(lr * dt)
    a_re = mag * jnp.cos(li * dt)
    a_im = mag * jnp.sin(li * dt)
    den = lr * lr + li * li
    nr = a_re - 1.0
    return a_re, a_im, (nr * lr + a_im * li) / den, (a_im * lr - nr * li) / den


def s5_params_fwd(lr, li, ldt, bt_re, bt_im):
    g, c, p = bt_re.shape

    def body(lr_ref, li_ref, ldt_ref, br_ref, bi_ref, ar_ref, ai_ref, bbr_ref, bbi_ref):
        a_re, a_im, cr, ci = _disc(lr_ref[...], li_ref[...], ldt_ref[...])
        ar_ref[...] = a_re
        ai_ref[...] = a_im
        for k in range(c):
            br, bi = br_ref[:, k, :], bi_ref[:, k, :]
            bbr_ref[:, k, :] = cr * br - ci * bi
            bbi_ref[:, k, :] = cr * bi + ci * br

    return pl.pallas_call(
        body, name="s5_params_fwd",
        out_shape=[jax.ShapeDtypeStruct((g, p), f32)] * 2 + [jax.ShapeDtypeStruct((g, c, p), f32)] * 2,
    )(lr, li, ldt, bt_re, bt_im)


def s5_params_bwd(lr, li, ldt, bt_re, bt_im, da_re, da_im, dbbt_re, dbbt_im):
    g, c, p = bt_re.shape

    def body(lr_ref, li_ref, ldt_ref, br_ref, bi_ref, dar_ref, dai_ref, dbbr_ref, dbbi_ref,
             dlr_ref, dli_ref, dldt_ref, dbr_ref, dbi_ref):
        (a_re, a_im, cr, ci), vjp = jax.vjp(_disc, lr_ref[...], li_ref[...], ldt_ref[...])
        dcr = jnp.zeros((g, p), f32)
        dci = jnp.zeros((g, p), f32)
        for k in range(c):
            br, bi = br_ref[:, k, :], bi_ref[:, k, :]
            dr, di = dbbr_ref[:, k, :], dbbi_ref[:, k, :]
            dcr += dr * br + di * bi
            dci += di * br - dr * bi
            dbr_ref[:, k, :] = cr * dr + ci * di
            dbi_ref[:, k, :] = cr * di - ci * dr
        dlr, dli, dldt = vjp((dar_ref[...], dai_ref[...], dcr, dci))
        dlr_ref[...] = dlr
        dli_ref[...] = dli
        dldt_ref[...] = dldt

    return pl.pallas_call(
        body, name="s5_params_bwd",
        out_shape=[jax.ShapeDtypeStruct((g, p), f32)] * 2 + [jax.ShapeDtypeStruct((g, 1), f32)]
        + [jax.ShapeDtypeStruct((g, c, p), f32)] * 2,
    )(lr, li, ldt, bt_re, bt_im, da_re, da_im, dbbt_re, dbbt_im)


def _cmul(ar, ai, br, bi):
    return ar * br - ai * bi, ar * bi + ai * br


def _power_tables(ar, ai):
    rows = lax.broadcasted_iota(jnp.int32, (SUBLANES, ar.shape[1]), 0)
    pr = jnp.zeros((SUBLANES, ar.shape[1]), f32)
    pi = jnp.zeros((SUBLANES, ar.shape[1]), f32)
    cr, ci = ar, ai
    pows = {}
    for r in range(SUBLANES):
        pows[r + 1] = (cr, ci)
        pr = jnp.where(rows == r, cr, pr)
        pi = jnp.where(rows == r, ci, pi)
        cr, ci = _cmul(cr, ci, ar, ai)
    return [pows[1], pows[2], pows[4]], pr, pi


def _ssm_time_tile(l):
    return _tile(l, (512, 256, 128))


def ssm_fwd(u, bre3, bim3, cre3, cimn3, a_re, a_im, d_skip):
    l, w = u.shape
    nj = w // HEAD
    ns = STATES_PER_LANE_BLOCK
    tt = _ssm_time_tile(l)

    def body(u_ref, bre_ref, bim_ref, cre_ref, cim_ref, ar_ref, ai_ref, d_ref, y_ref, hr_ref, hi_ref, cr_ref, ci_ref):
        @pl.when(pl.program_id(1) == 0)
        def _():
            cr_ref[...] = jnp.zeros_like(cr_ref)
            ci_ref[...] = jnp.zeros_like(ci_ref)

        uv = u_ref[...]
        hr_ref[...] = _dot(uv, bre_ref[...])
        hi_ref[...] = _dot(uv, bim_ref[...])
        steps, pr, pi = _power_tables(ar_ref[...], ai_ref[...])
        rows = lax.broadcasted_iota(jnp.int32, (SUBLANES, ns), 0)

        def blk(b, carry):
            cr, ci = carry
            sl = pl.ds(pl.multiple_of(b * SUBLANES, SUBLANES), SUBLANES)
            xr, xi = hr_ref[sl, :], hi_ref[sl, :]
            for d, (sr_, si_) in zip((1, 2, 4), steps):
                keep = rows >= d
                qr = jnp.where(keep, pltpu.roll(xr, d, axis=0), 0.0)
                qi = jnp.where(keep, pltpu.roll(xi, d, axis=0), 0.0)
                mr, mi = _cmul(sr_, si_, qr, qi)
                xr, xi = xr + mr, xi + mi
            mr, mi = _cmul(pr, pi, cr, ci)
            xr, xi = xr + mr, xi + mi
            hr_ref[sl, :] = xr
            hi_ref[sl, :] = xi
            return xr[SUBLANES - 1:, :], xi[SUBLANES - 1:, :]

        cr, ci = lax.fori_loop(0, tt // SUBLANES, blk, (cr_ref[...], ci_ref[...]))
        cr_ref[...] = cr
        ci_ref[...] = ci
        y = _dot(hr_ref[...].astype(bf16), cre_ref[...]) + _dot(hi_ref[...].astype(bf16), cim_ref[...])
        y_ref[...] = y + d_ref[...] * uv.astype(f32)

    lane = pl.BlockSpec((tt, HEAD), lambda j, i: (i, j))
    st = pl.BlockSpec((tt, ns), lambda j, i: (i, j))
    b3 = pl.BlockSpec((None, HEAD, ns), lambda j, i: (j, 0, 0))
    c3 = pl.BlockSpec((None, ns, HEAD), lambda j, i: (j, 0, 0))
    arow = pl.BlockSpec((1, ns), lambda j, i: (0, j))
    return pl.pallas_call(
        body, name="ssm_fwd", grid=(nj, l // tt),
        in_specs=[lane, b3, b3, c3, c3, arow, arow, pl.BlockSpec((1, HEAD), lambda j, i: (0, j))],
        out_specs=[lane, st, st],
        out_shape=[jax.ShapeDtypeStruct((l, w), f32), jax.ShapeDtypeStruct((l, nj * ns), f32),
                   jax.ShapeDtypeStruct((l, nj * ns), f32)],
        scratch_shapes=[pltpu.VMEM((1, ns), f32), pltpu.VMEM((1, ns), f32)],
        compiler_params=_cparams("parallel", "arbitrary"),
    )(u, bre3, bim3, cre3, cimn3, a_re, a_im, d_skip)


def ssm_bwd(dy, u, h_re, h_im, bre3, bim3, cre3, cimn3, a_re, a_im, d_skip):
    l, w = u.shape
    nj = w // HEAD
    ns = STATES_PER_LANE_BLOCK
    tt = _ssm_time_tile(l)
    nt = l // tt

    def body(dy_ref, u_ref, hr_ref, hi_ref, bre_ref, bim_ref, cre_ref, cim_ref, ar_ref, ai_ref, d_ref,
             du_ref, dd_ref, dar_ref, dai_ref, dbre_ref, dbim_ref, dcre_ref, dcim_ref, kr_ref, ki_ref, cr_ref, ci_ref,
             accr_ref, acci_ref):
        i = pl.program_id(1)

        @pl.when(i == 0)
        def _():
            for ref in (cr_ref, ci_ref, accr_ref, acci_ref, dd_ref, dbre_ref, dbim_ref, dcre_ref, dcim_ref):
                ref[...] = jnp.zeros_like(ref)

        dyv = dy_ref[...]
        dyb = dyv.astype(bf16)
        uv = u_ref[...]
        kr_ref[...] = _dot_nt(dyb, cre_ref[...])
        ki_ref[...] = _dot_nt(dyb, cim_ref[...])
        steps, pr, pi = _power_tables(ar_ref[...], -ai_ref[...])
        rows = lax.broadcasted_iota(jnp.int32, (SUBLANES, ns), 0)
        qr = jnp.zeros((SUBLANES, ns), f32)
        qi = jnp.zeros((SUBLANES, ns), f32)
        for r in range(SUBLANES):
            qr = jnp.where(rows == r, pr[SUBLANES - 1 - r:SUBLANES - r, :], qr)
            qi = jnp.where(rows == r, pi[SUBLANES - 1 - r:SUBLANES - r, :], qi)
        nb = tt // SUBLANES

        def blk(t, carry):
            cr, ci, accr, acci = carry
            sl = pl.ds(pl.multiple_of((nb - 1 - t) * SUBLANES, SUBLANES), SUBLANES)
            xr, xi = kr_ref[sl, :], ki_ref[sl, :]
            for d, (sr_, si_) in zip((1, 2, 4), steps):
                keep = rows < SUBLANES - d
                zr = jnp.where(keep, pltpu.roll(xr, SUBLANES - d, axis=0), 0.0)
                zi = jnp.where(keep, pltpu.roll(xi, SUBLANES - d, axis=0), 0.0)
                mr, mi = _cmul(sr_, si_, zr, zi)
                xr, xi = xr + mr, xi + mi
            mr, mi = _cmul(qr, qi, cr, ci)
            xr, xi = xr + mr, xi + mi
            kr_ref[sl, :] = xr
            ki_ref[sl, :] = xi
            last = rows == SUBLANES - 1
            nr = jnp.where(last, cr, pltpu.roll(xr, SUBLANES - 1, axis=0))
            ni = jnp.where(last, ci, pltpu.roll(xi, SUBLANES - 1, axis=0))
            hr, hi = hr_ref[sl, :], hi_ref[sl, :]
            accr = accr + nr * hr + ni * hi
            acci = acci + ni * hr - nr * hi
            return xr[:1, :], xi[:1, :], accr, acci

        cr, ci, accr, acci = lax.fori_loop(0, nb, blk, (cr_ref[...], ci_ref[...], accr_ref[...], acci_ref[...]))
        cr_ref[...] = cr
        ci_ref[...] = ci
        accr_ref[...] = accr
        acci_ref[...] = acci
        kr, ki = kr_ref[...].astype(bf16), ki_ref[...].astype(bf16)
        du = _dot_nt(kr, bre_ref[...]) + _dot_nt(ki, bim_ref[...]) + d_ref[...] * dyv
        du_ref[...] = du.astype(du_ref.dtype)
        dd_ref[...] += jnp.sum(dyv * uv.astype(f32), axis=0, keepdims=True)
        dbre_ref[...] += _dot_tn(uv, kr)
        dbim_ref[...] += _dot_tn(uv, ki)
        dcre_ref[...] += _dot_tn(hr_ref[...].astype(bf16), dyb)
        dcim_ref[...] += _dot_tn(hi_ref[...].astype(bf16), dyb)

        @pl.when(i == nt - 1)
        def _():
            dar_ref[...] = jnp.sum(accr_ref[...], axis=0, keepdims=True)
            dai_ref[...] = jnp.sum(acci_ref[...], axis=0, keepdims=True)

    lane = pl.BlockSpec((tt, HEAD), lambda j, i: (nt - 1 - i, j))
    st = pl.BlockSpec((tt, ns), lambda j, i: (nt - 1 - i, j))
    b3 = pl.BlockSpec((None, HEAD, ns), lambda j, i: (j, 0, 0))
    c3 = pl.BlockSpec((None, ns, HEAD), lambda j, i: (j, 0, 0))
    arow = pl.BlockSpec((1, ns), lambda j, i: (0, j))
    drow = pl.BlockSpec((1, HEAD), lambda j, i: (0, j))
    return pl.pallas_call(
        body, name="ssm_bwd", grid=(nj, nt),
        in_specs=[lane, lane, st, st, b3, b3, c3, c3, arow, arow, drow],
        out_specs=[lane, drow, arow, arow, b3, b3, c3, c3],
        out_shape=[jax.ShapeDtypeStruct((l, w), bf16), jax.ShapeDtypeStruct((1, w), f32),
                   jax.ShapeDtypeStruct((1, nj * ns), f32), jax.ShapeDtypeStruct((1, nj * ns), f32),
                   jax.ShapeDtypeStruct((nj, HEAD, ns), f32), jax.ShapeDtypeStruct((nj, HEAD, ns), f32),
                   jax.ShapeDtypeStruct((nj, ns, HEAD), f32), jax.ShapeDtypeStruct((nj, ns, HEAD), f32)],
        scratch_shapes=[pltpu.VMEM((tt, ns), f32), pltpu.VMEM((tt, ns), f32), pltpu.VMEM((1, ns), f32),
                        pltpu.VMEM((1, ns), f32), pltpu.VMEM((SUBLANES, ns), f32), pltpu.VMEM((SUBLANES, ns), f32)],
        compiler_params=_cparams("parallel", "arbitrary"),
    )(dy, u, h_re, h_im, bre3, bim3, cre3, cimn3, a_re, a_im, d_skip)


def glu_fwd(y, z_src, w_glu, b_glu):
    l, w = y.shape
    tm = _row_tile(l)

    def body(y_ref, z_ref, w_ref, b_ref, g_ref, t_ref, o_ref):
        g = _gelu(y_ref[...])
        gb = g.astype(bf16)
        t = _dot(gb, w_ref[...]) + b_ref[...]
        g_ref[...] = gb
        t_ref[...] = t
        o_ref[...] = (g * jax.nn.sigmoid(t) * _silu(z_ref[...].astype(f32))).astype(o_ref.dtype)

    blk = pl.BlockSpec((tm, w), lambda i: (i, 0))
    return pl.pallas_call(
        body, name="glu_fwd", grid=(l // tm,),
        in_specs=[blk, pl.BlockSpec((tm, w), lambda i: (i, 1)), pl.BlockSpec((w, w), lambda i: (0, 0)), _row(w)],
        out_specs=[blk, blk, blk],
        out_shape=[jax.ShapeDtypeStruct((l, w), bf16), jax.ShapeDtypeStruct((l, w), f32),
                   jax.ShapeDtypeStruct((l, w), bf16)],
        compiler_params=_cparams("parallel"),
    )(y, z_src, w_glu, b_glu)


def glu_bwd(dout, y, t, z_src, w_glu):
    l, w = y.shape
    tm = _row_tile(l)

    def body(do_ref, y_ref, t_ref, z_ref, w_ref, dy_ref, dz_ref, dt_ref, db_ref):
        @pl.when(pl.program_id(0) == 0)
        def _():
            db_ref[...] = jnp.zeros_like(db_ref)

        yv, zv, dov = y_ref[...], z_ref[...].astype(f32), do_ref[...]
        g = _gelu(yv)
        sg = jax.nn.sigmoid(t_ref[...])
        dy2 = dov * _silu(zv)
        dz_ref[...] = (dov * g * sg * _silu_grad(zv)).astype(dz_ref.dtype)
        dt = dy2 * g * sg * (1.0 - sg)
        dtb = dt.astype(bf16)
        dt_ref[...] = dtb
        db_ref[...] += jnp.sum(dt, axis=0, keepdims=True)
        dg = dy2 * sg + _dot_nt(dtb, w_ref[...])
        dy_ref[...] = dg * _gelu_grad(yv)

    blk = pl.BlockSpec((tm, w), lambda i: (i, 0))
    return pl.pallas_call(
        body, name="glu_bwd", grid=(l // tm,),
        in_specs=[blk, blk, blk, pl.BlockSpec((tm, w), lambda i: (i, 1)), pl.BlockSpec((w, w), lambda i: (0, 0))],
        out_specs=[blk, blk, blk, _row(w)],
        out_shape=[jax.ShapeDtypeStruct((l, w), f32), jax.ShapeDtypeStruct((l, w), bf16),
                   jax.ShapeDtypeStruct((l, w), bf16), jax.ShapeDtypeStruct((1, w), f32)],
        compiler_params=_cparams("arbitrary"),
    )(dout, y, t, z_src, w_glu)


def _adamw(w, g, m, v):
    m = ADAM_B1 * m + (1.0 - ADAM_B1) * g
    v = ADAM_B2 * v + (1.0 - ADAM_B2) * (g * g)
    m_hat = m / (1.0 - ADAM_B1 ** ADAM_STEP)
    v_hat = v / (1.0 - ADAM_B2 ** ADAM_STEP)
    return -ADAM_LR * (m_hat / (jnp.sqrt(v_hat) + ADAM_EPS) + ADAM_WD * w), m, v


def adam_reduce(pieces, w, m, v, name):
    r, c = w.shape
    tr = _tile(r, (256, 128, 64, 32, 16, 8))

    def body(p_ref, w_ref, m_ref, v_ref, g_ref, d_ref, nm_ref, nv_ref):
        g = p_ref[0].astype(f32)
        for s in range(1, N_DEV):
            g = g + p_ref[s].astype(f32)
        g_ref[...] = g
        d_ref[...], nm_ref[...], nv_ref[...] = _adamw(w_ref[...], g, m_ref[...], v_ref[...])

    blk = pl.BlockSpec((tr, c), lambda i: (i, 0))
    return pl.pallas_call(
        body, name=name, grid=(r // tr,),
        in_specs=[pl.BlockSpec((N_DEV, tr, c), lambda i: (0, i, 0)), blk, blk, blk],
        out_specs=[blk] * 4, out_shape=[jax.ShapeDtypeStruct((r, c), f32)] * 4,
        compiler_params=_cparams("parallel"),
    )(pieces, w, m, v)


def adam_w_mod(cond_t, dm, w, m, v):
    nl, d, cols = w.shape
    tr = _tile(d, (512, 256, 128))

    def body(c_ref, dm_ref, w_ref, m_ref, v_ref, g_ref, d_ref, nm_ref, nv_ref):
        g = jnp.dot(c_ref[...], dm_ref[...], preferred_element_type=f32, precision=lax.Precision.HIGHEST)
        g_ref[...] = g
        d_ref[...], nm_ref[...], nv_ref[...] = _adamw(w_ref[...], g, m_ref[...], v_ref[...])

    blk = pl.BlockSpec((None, tr, cols), lambda l, i: (l, i, 0))
    return pl.pallas_call(
        body, name="adam_w_mod", grid=(nl, d // tr),
        in_specs=[pl.BlockSpec((tr, N_DEV), lambda l, i: (i, 0)), pl.BlockSpec((None, N_DEV, cols), lambda l, i: (l, 0, 0)),
                  blk, blk, blk],
        out_specs=[blk] * 4, out_shape=[jax.ShapeDtypeStruct((nl, d, cols), f32)] * 4,
        compiler_params=_cparams("parallel", "parallel"),
    )(cond_t, dm, w, m, v)


def silu_rows(c_all):
    def body(c_ref, o_ref):
        o_ref[...] = _silu(c_ref[...])

    return pl.pallas_call(body, name="silu_rows", out_shape=jax.ShapeDtypeStruct(c_all.shape, f32))(c_all)


def _block_diag(x):
    g, a, b = x.shape
    nj = g // GROUPS_PER_LANE_BLOCK
    eye = jnp.eye(GROUPS_PER_LANE_BLOCK, dtype=x.dtype)
    x5 = x.reshape(nj, GROUPS_PER_LANE_BLOCK, a, b)
    return jnp.einsum("jgab,gh->jgahb", x5, eye).reshape(nj, GROUPS_PER_LANE_BLOCK * a, GROUPS_PER_LANE_BLOCK * b)


def _diag_blocks(x, a, b):
    nj = x.shape[0]
    x5 = x.reshape(nj, GROUPS_PER_LANE_BLOCK, a, GROUPS_PER_LANE_BLOCK, b)
    eye = jnp.eye(GROUPS_PER_LANE_BLOCK, dtype=x.dtype)
    return jnp.einsum("jgahb,gh->jgab", x5, eye).reshape(nj * GROUPS_PER_LANE_BLOCK, a, b)


PACK_ROW = SUBLANES * HEAD


def _pack(parts):
    rows = []
    for p in parts:
        flat = p.reshape(-1)
        pad = (-flat.shape[0]) % PACK_ROW
        if pad:
            flat = jnp.concatenate([flat, jnp.zeros((pad,), flat.dtype)])
        rows.append(flat.reshape(-1, HEAD))
    return jnp.concatenate(rows, axis=0)


def _unpack(packed, shapes):
    out, r0 = [], 0
    for shp in shapes:
        n = math.prod(shp)
        nr = -(-n // PACK_ROW) * SUBLANES
        out.append(packed[r0:r0 + nr].reshape(-1)[:n].reshape(shp))
        r0 += nr
    return out


def adam_small(gathered, w, m, v):
    r, c = w.shape

    def body(p_ref, w_ref, m_ref, v_ref, g_ref, d_ref, nm_ref, nv_ref):
        g = p_ref[0]
        for s in range(1, N_DEV):
            g = g + p_ref[s]
        g_ref[...] = g
        d_ref[...], nm_ref[...], nv_ref[...] = _adamw(w_ref[...], g, m_ref[...], v_ref[...])

    tr = max(t for t in range(SUBLANES, 1024 + 1, SUBLANES) if r % t == 0)
    blk = pl.BlockSpec((tr, c), lambda i: (i, 0))
    return pl.pallas_call(
        body, name="adam_small", grid=(r // tr,),
        in_specs=[pl.BlockSpec((N_DEV, tr, c), lambda i: (0, i, 0)), blk, blk, blk],
        out_specs=[blk] * 4, out_shape=[jax.ShapeDtypeStruct((r, c), f32)] * 4,
        compiler_params=_cparams("parallel"),
    )(gathered, w, m, v)


def kernel(x, c, ln_pre_g, ln_post_g, w_mod, b_mod, w_in_ab, w_out_ab, sgu_norm_g, sgu_w, sgu_b, w_in_ssm, w_out_ssm, lam_re, lam_im, b_re, b_im, c_re, c_im, d_skip, log_dt, w_glu, b_glu, loss_target, m_ln_pre_g, m_ln_post_g, m_w_mod, m_b_mod, m_w_in_ab, m_w_out_ab, m_sgu_norm_g, m_sgu_w, m_sgu_b, m_w_in_ssm, m_w_out_ssm, m_lam_re, m_lam_im, m_b_re, m_b_im, m_c_re, m_c_im, m_d_skip, m_log_dt, m_w_glu, m_b_glu, v_ln_pre_g, v_ln_post_g, v_w_mod, v_b_mod, v_w_in_ab, v_w_out_ab, v_sgu_norm_g, v_sgu_w, v_sgu_b, v_w_in_ssm, v_w_out_ssm, v_lam_re, v_lam_im, v_b_re, v_b_im, v_c_re, v_c_im, v_d_skip, v_log_dt, v_w_glu, v_b_glu):
    me = _my_index()
    x0 = x[0]
    l, d = x0.shape
    target = loss_target[0]
    nh = sgu_w.shape[1]
    wa = nh * HEAD
    n_grp, n_st = lam_re.shape[1], lam_re.shape[2]
    mod_cols = w_mod.shape[2]

    w_in_flight, (c_thru,) = exchange_start(GATHER, [w_in_ab[0].astype(bf16)], [c], "gather_w_in_start")
    c_all, d_skip_all, b_glu_all = all_gather([c_thru, d_skip, b_glu], "gather_c")
    c_all = c_all.reshape(N_DEV, d)
    d_skip_all = d_skip_all.reshape(1, -1)
    b_glu_all = b_glu_all.reshape(1, -1)

    b_cols = lax.dynamic_slice_in_dim(b_mod, me * mod_cols, mod_cols, axis=1)
    (mod_all,) = all_gather([mod_part(c_all, w_mod, b_cols)], "gather_mod")
    mod_mine = lax.dynamic_index_in_dim(mod_all, me, axis=2, keepdims=False)
    mod_rows = jnp.transpose(mod_mine, (1, 0, 2)).reshape(2, 3, 1, d)

    def rows(a, i):
        return a[i].reshape(1, d)

    shift0, scale0, gate0 = mod_rows[0, 0], mod_rows[0, 1], mod_rows[0, 2]
    h0 = prenorm_fwd(x0, rows(ln_pre_g, 0), shift0, scale0, "prenorm0")
    (win_ab3,) = exchange_wait(w_in_flight, h0, "gather_w_in_wait")
    w_in_flight, (win_ab3,) = exchange_start(
        GATHER, [w_out_ab[0].astype(bf16), w_in_ssm[0].astype(bf16), w_out_ssm[0].astype(bf16), w_glu[0].astype(bf16)],
        [win_ab3], "gather_w_rest_start")
    proj0 = mm_nn(h0, win_ab3, bf16, "proj0")
    sgu_b3 = sgu_b[0].reshape(nh, HEAD, 1)
    out_a = sgu_fwd(proj0, sgu_norm_g, sgu_w[0], sgu_b3)
    out_b, att, tot = sb_fwd(proj0, nh)
    cat = jnp.concatenate([out_a, out_b], axis=1)
    wout_ab3, win_ssm3, wout_ssm3, wglu = exchange_wait(w_in_flight, cat, "gather_w_rest_wait")
    wout_ab3 = wout_ab3.reshape(1, d, d)
    win_ssm3 = win_ssm3.reshape(1, d, d)
    wglu = wglu.reshape(w_glu.shape[2], w_glu.shape[2])
    y0 = mm_nn(cat, wout_ab3, f32, "out0")
    x1 = post_fwd(x0, y0, gate0, rows(ln_post_g, 0), "post0")

    shift1, scale1, gate1 = mod_rows[1, 0], mod_rows[1, 1], mod_rows[1, 2]
    h1 = prenorm_fwd(x1, rows(ln_pre_g, 1), shift1, scale1, "prenorm1")
    proj1 = mm_nn(h1, win_ssm3, bf16, "proj1")
    w_ssm = proj1.shape[1] // 2
    ldt = log_dt[0].reshape(n_grp, 1)
    bt_re = jnp.transpose(b_re[0], (0, 2, 1))
    bt_im = jnp.transpose(b_im[0], (0, 2, 1))
    a_re, a_im, bbt_re, bbt_im = s5_params_fwd(lam_re[0], lam_im[0], ldt, bt_re, bt_im)
    bre3 = _block_diag(bbt_re).astype(bf16)
    bim3 = _block_diag(bbt_im).astype(bf16)
    cre3 = _block_diag(jnp.transpose(c_re[0], (0, 2, 1))).astype(bf16)
    cimn3 = _block_diag(-jnp.transpose(c_im[0], (0, 2, 1))).astype(bf16)
    a_re_row, a_im_row = a_re.reshape(1, -1), a_im.reshape(1, -1)
    u = proj1[:, :w_ssm]
    y_ssm, hs_re, hs_im = ssm_fwd(u, bre3, bim3, cre3, cimn3, a_re_row, a_im_row, d_skip_all)
    g_act, t_glu, mix1 = glu_fwd(y_ssm, proj1, wglu, b_glu_all)
    y1 = mm_nn(mix1, wout_ssm3, f32, "out1")

    dx2, loss_tile = final_loss(x1, y1, gate1, rows(ln_post_g, 1), target)
    loss = lax.psum(loss_tile[0, 0] * (0.5 / d), ("x", "y", "c"))

    dy1, dgate1, dgpost1 = post_bwd(dx2, y1, gate1, rows(ln_post_g, 1), "post1_bwd")
    dmix1 = mm_nt(dy1, wout_ssm3, f32, "dmix1")
    gw_out_ssm = mm_tn(mix1, dy1, N_DEV, bf16, "gw_out_ssm")
    g_flight1, (dmix1,) = exchange_start(SCATTER, [gw_out_ssm], [dmix1], "scatter_g1_start")
    dy_ssm, dz1, dt_glu, db_glu = glu_bwd(dmix1, y_ssm, t_glu, proj1, wglu)
    gw_glu = mm_tn(g_act, dt_glu, 1, bf16, "gw_glu").reshape(N_DEV, -1, w_ssm)
    du, dd_skip, da_re, da_im, dbre3, dbim3, dcre3, dcimn3 = ssm_bwd(
        dy_ssm, u, hs_re, hs_im, bre3, bim3, cre3, cimn3, a_re_row, a_im_row, d_skip_all)
    dproj1 = jnp.concatenate([du, dz1], axis=1)
    gw_in_ssm = mm_tn(h1, dproj1, 1, bf16, "gw_in_ssm").reshape(N_DEV, -1, proj1.shape[1])
    g_flight2, (dproj1,) = exchange_start(SCATTER, [gw_in_ssm, gw_glu], [dproj1], "scatter_g2_start")
    dh1 = mm_nt(dproj1, win_ssm3, f32, "dh1")
    dx1, dshift1, dscale1, dgpre1 = prenorm_bwd(dh1, x1, dx2, rows(ln_pre_g, 1), scale1, "prenorm1_bwd")
    dlr, dli, dldt, dbt_re, dbt_im = s5_params_bwd(
        lam_re[0], lam_im[0], ldt, bt_re, bt_im, da_re.reshape(n_grp, n_st), da_im.reshape(n_grp, n_st),
        _diag_blocks(dbre3, SSM_GROUP, n_st), _diag_blocks(dbim3, SSM_GROUP, n_st))
    g_b_re = jnp.transpose(dbt_re, (0, 2, 1))
    g_b_im = jnp.transpose(dbt_im, (0, 2, 1))
    g_c_re = jnp.transpose(_diag_blocks(dcre3, n_st, SSM_GROUP), (0, 2, 1))
    g_c_im = -jnp.transpose(_diag_blocks(dcimn3, n_st, SSM_GROUP), (0, 2, 1))

    dy0, dgate0, dgpost0 = post_bwd(dx1, y0, gate0, rows(ln_post_g, 0), "post0_bwd")
    dcat = mm_nt(dy0, wout_ab3, f32, "dcat")
    gw_out_ab = mm_tn(cat, dy0, 1, bf16, "gw_out_ab").reshape(N_DEV, -1, d)
    g_flight3, (dcat,) = exchange_start(SCATTER, [gw_out_ab], [dcat], "scatter_g3_start")
    da, dsgu_w, dsgu_b, dsgu_ng = sgu_bwd(proj0, dcat, sgu_norm_g, sgu_w[0], sgu_b3)
    dq, dk, dv, dbz = sb_bwd(proj0, dcat, att, tot, nh)
    dproj0 = jnp.concatenate([da, dq, dk, dv, dbz], axis=1)
    gw_in_ab = mm_tn(h0, dproj0, N_DEV, bf16, "gw_in_ab")
    g_flight4, (dproj0,) = exchange_start(SCATTER, [gw_in_ab], [dproj0], "scatter_g4_start")
    dh0 = mm_nt(dproj0, win_ab3, f32, "dh0")
    dx0, dshift0, dscale0, dgpre0 = prenorm_bwd(dh0, x0, dx1, rows(ln_pre_g, 0), scale0, "prenorm0_bwd")

    small_names = ["ln_pre_g", "ln_post_g", "b_mod", "sgu_norm_g", "sgu_w", "sgu_b", "lam_re", "lam_im", "b_re", "b_im",
                   "c_re", "c_im", "log_dt"]
    small_w = [ln_pre_g, ln_post_g, b_mod, sgu_norm_g, sgu_w, sgu_b, lam_re, lam_im, b_re, b_im, c_re, c_im, log_dt]
    small_m = [m_ln_pre_g, m_ln_post_g, m_b_mod, m_sgu_norm_g, m_sgu_w, m_sgu_b, m_lam_re, m_lam_im, m_b_re, m_b_im,
               m_c_re, m_c_im, m_log_dt]
    small_v = [v_ln_pre_g, v_ln_post_g, v_b_mod, v_sgu_norm_g, v_sgu_w, v_sgu_b, v_lam_re, v_lam_im, v_b_re, v_b_im,
               v_c_re, v_c_im, v_log_dt]
    dmod = jnp.concatenate([dshift0, dscale0, dgate0, dshift1, dscale1, dgate1], axis=1)
    small_g = [jnp.concatenate([dgpre0, dgpre1]), jnp.concatenate([dgpost0, dgpost1]), dmod, dsgu_ng, dsgu_w, dsgu_b,
               dlr, dli, g_b_re, g_b_im, g_c_re, g_c_im, dldt]
    shapes = [w.shape for w in small_w]
    (g_all,) = all_gather([_pack(small_g + [dd_skip, db_glu])], "gather_small_grads")
    r_small = [_unpack(o, shapes) for o in adam_small(g_all, _pack(small_w), _pack(small_m), _pack(small_v))]
    small = {n: [r_small[k][i] for k in range(4)] for i, n in enumerate(small_names)}
    n_rows_small = sum(-(-math.prod(s) // PACK_ROW) * SUBLANES for s in shapes)
    vec_rows = d_skip_all.shape[1] // HEAD

    def my_columns(r0):
        whole = g_all[:, r0:r0 + vec_rows].reshape(N_DEV, 1, -1)
        return lax.dynamic_slice_in_dim(whole, me * d_skip.shape[1], d_skip.shape[1], axis=2)

    def sharded(p, w, m, v, name):
        shp = w.shape
        w2, m2, v2 = (a.reshape(-1, shp[-1]) for a in (w, m, v))
        return [o.reshape(shp) for o in adam_reduce(p.reshape(N_DEV, -1, shp[-1]), w2, m2, v2, name)]

    r_d_skip = sharded(my_columns(n_rows_small), d_skip, m_d_skip, v_d_skip, "adam_d_skip")
    r_b_glu = sharded(my_columns(n_rows_small + vec_rows), b_glu, m_b_glu, v_b_glu, "adam_b_glu")
    (p_out_ssm,) = exchange_wait(g_flight1, g_all, "scatter_g1_wait")
    p_in_ssm, p_glu = exchange_wait(g_flight2, g_all, "scatter_g2_wait")
    (p_out_ab,) = exchange_wait(g_flight3, g_all, "scatter_g3_wait")
    (p_in_ab,) = exchange_wait(g_flight4, g_all, "scatter_g4_wait")
    r_w_out_ssm = sharded(p_out_ssm, w_out_ssm, m_w_out_ssm, v_w_out_ssm, "adam_w_out_ssm")
    r_w_in_ssm = sharded(p_in_ssm, w_in_ssm, m_w_in_ssm, v_w_in_ssm, "adam_w_in_ssm")
    r_w_glu = sharded(p_glu, w_glu, m_w_glu, v_w_glu, "adam_w_glu")
    r_w_out_ab = sharded(p_out_ab, w_out_ab, m_w_out_ab, v_w_out_ab, "adam_w_out_ab")
    r_w_in_ab = sharded(p_in_ab, w_in_ab, m_w_in_ab, v_w_in_ab, "adam_w_in_ab")

    n_rows_before = sum(-(-math.prod(s) // PACK_ROW) * SUBLANES for s in shapes[:2])
    n_rows = math.prod(b_mod.shape) // HEAD
    dmod_all = g_all[:, n_rows_before:n_rows_before + n_rows].reshape(N_DEV, 2, 3 * d)
    dm_cols = jnp.transpose(lax.dynamic_slice_in_dim(dmod_all, me * mod_cols, mod_cols, axis=2), (1, 0, 2))
    cond_t = jnp.transpose(silu_rows(c_all))
    r_w_mod = adam_w_mod(cond_t, dm_cols, w_mod, m_w_mod, v_w_mod)

    res = dict(small)
    res.update(w_mod=r_w_mod, w_in_ab=r_w_in_ab, w_out_ab=r_w_out_ab, w_in_ssm=r_w_in_ssm, w_out_ssm=r_w_out_ssm,
               d_skip=r_d_skip, w_glu=r_w_glu, b_glu=r_b_glu)
    order = ["ln_pre_g", "ln_post_g", "w_mod", "b_mod", "w_in_ab", "w_out_ab", "sgu_norm_g", "sgu_w", "sgu_b", "w_in_ssm",
             "w_out_ssm", "lam_re", "lam_im", "b_re", "b_im", "c_re", "c_im", "d_skip", "log_dt", "w_glu", "b_glu"]
    outs = [loss, dx0.reshape(x.shape)]
    for k in range(4):
        outs += [res[n][k] for n in order]
    return tuple(outs)
```

```python
import functools
import math

import jax
import jax.numpy as jnp
from jax import lax
from jax.experimental import pallas as pl
from jax.experimental.pallas import tpu as pltpu

f32 = jnp.float32
bf16 = jnp.bfloat16

N_DEV = 8
EPS = 1e-6
HEAD = 128
SUBLANES = 8
SSM_GROUP = 16
SSM_STATE = 64
GROUPS_PER_LANE_BLOCK = HEAD // SSM_GROUP
STATES_PER_LANE_BLOCK = GROUPS_PER_LANE_BLOCK * SSM_STATE
VMEM_LIMIT = 56 * 2 ** 20
ADAM_LR, ADAM_B1, ADAM_B2, ADAM_EPS, ADAM_WD, ADAM_STEP = 0.001, 0.9, 0.999, 1e-08, 0.01, 10
_GELU_C0 = math.sqrt(2.0 / math.pi)
_GELU_C1 = 0.044715
MESH = pl.DeviceIdType.MESH


def _cparams(*sem):
    return pltpu.CompilerParams(dimension_semantics=sem if sem else None, vmem_limit_bytes=VMEM_LIMIT)


def _gelu(x):
    return 0.5 * x * (1.0 + jnp.tanh(_GELU_C0 * (x + _GELU_C1 * x * x * x)))


def _gelu_grad(x):
    t = jnp.tanh(_GELU_C0 * (x + _GELU_C1 * x * x * x))
    return 0.5 * (1.0 + t) + 0.5 * x * (1.0 - t * t) * _GELU_C0 * (1.0 + 3.0 * _GELU_C1 * x * x)


def _silu(x):
    return x * jax.nn.sigmoid(x)


def _silu_grad(x):
    s = jax.nn.sigmoid(x)
    return s * (1.0 + x * (1.0 - s))


def _dot(a, b):
    return jnp.dot(a, b, preferred_element_type=f32)


def _dot_nt(a, b):
    return lax.dot_general(a, b, (((1,), (1,)), ((), ())), preferred_element_type=f32)


def _dot_tn(a, b):
    return lax.dot_general(a, b, (((0,), (0,)), ((), ())), preferred_element_type=f32)


def _split_bf16(v):
    hi = v.astype(bf16)
    lo = (v - hi.astype(f32)).astype(bf16)
    return hi, lo


def _row(d):
    return pl.BlockSpec((1, d), lambda *_: (0, 0))


def _my_index():
    return 4 * lax.axis_index("x") + 2 * lax.axis_index("y") + lax.axis_index("c")


def _peer(k):
    x, y, c = lax.axis_index("x"), lax.axis_index("y"), lax.axis_index("c")
    return (1 - x if k & 4 else x, 1 - y if k & 2 else y, 1 - c if k & 1 else c)


def all_gather(arrs, name):
    n = len(arrs)

    def body(*refs):
        ins, outs = refs[:n], refs[n:2 * n]
        send, recv, local = refs[2 * n:]
        me = _my_index()
        copies = []
        for a in range(n):
            cp = pltpu.make_async_copy(ins[a], outs[a].at[me], local.at[a])
            cp.start()
            copies.append(cp)
            for k in range(1, N_DEV):
                s = a * (N_DEV - 1) + k - 1
                cp = pltpu.make_async_remote_copy(src_ref=ins[a], dst_ref=outs[a].at[me], send_sem=send.at[s],
                                                  recv_sem=recv.at[s], device_id=_peer(k), device_id_type=MESH)
                cp.start()
                copies.append(cp)
        for cp in copies:
            cp.wait()

    any_spec = pl.BlockSpec(memory_space=pl.ANY)
    outs = pl.pallas_call(
        body, name=name,
        out_shape=[jax.ShapeDtypeStruct((N_DEV,) + a.shape, a.dtype) for a in arrs],
        in_specs=[any_spec] * n, out_specs=[any_spec] * n,
        scratch_shapes=[pltpu.SemaphoreType.DMA((n * (N_DEV - 1),)), pltpu.SemaphoreType.DMA((n * (N_DEV - 1),)),
                        pltpu.SemaphoreType.DMA((n,))],
        compiler_params=pltpu.CompilerParams(has_side_effects=True),
    )(*arrs)
    return list(outs)


def all_reduce_rows(pack, extra, name):
    r, c = pack.shape
    rs = r // N_DEV
    n_peer = N_DEV - 1

    def body(p_ref, x_ref, o_ref, xo_ref, land, red, send1, recv1, send2, recv2, sendx, recvx, local):
        me = _my_index()

        def rows(i):
            return pl.ds(pl.multiple_of(i * rs, SUBLANES), rs)

        own = [pltpu.make_async_copy(p_ref.at[rows(me)], land.at[me], local.at[0]),
               pltpu.make_async_copy(x_ref, xo_ref.at[me], local.at[1])]
        first = []
        for k in range(1, N_DEV):
            first.append(pltpu.make_async_remote_copy(
                src_ref=p_ref.at[rows(jnp.bitwise_xor(me, k))], dst_ref=land.at[me], send_sem=send1.at[k - 1],
                recv_sem=recv1.at[k - 1], device_id=_peer(k), device_id_type=MESH))
            first.append(pltpu.make_async_remote_copy(
                src_ref=x_ref, dst_ref=xo_ref.at[me], send_sem=sendx.at[k - 1], recv_sem=recvx.at[k - 1],
                device_id=_peer(k), device_id_type=MESH))
        for cp in own + first:
            cp.start()
        for cp in own + first:
            cp.wait()
        acc = land[0]
        for s in range(1, N_DEV):
            acc = acc + land[s]
        red[...] = acc
        mine = pltpu.make_async_copy(red, o_ref.at[rows(me)], local.at[2])
        second = [pltpu.make_async_remote_copy(
            src_ref=red, dst_ref=o_ref.at[rows(me)], send_sem=send2.at[k - 1], recv_sem=recv2.at[k - 1],
            device_id=_peer(k), device_id_type=MESH) for k in range(1, N_DEV)]
        for cp in [mine] + second:
            cp.start()
        for cp in [mine] + second:
            cp.wait()

    any_spec = pl.BlockSpec(memory_space=pl.ANY)
    return pl.pallas_call(
        body, name=name,
        out_shape=[jax.ShapeDtypeStruct((r, c), pack.dtype), jax.ShapeDtypeStruct((N_DEV,) + extra.shape, extra.dtype)],
        in_specs=[any_spec, any_spec], out_specs=[any_spec, any_spec],
        scratch_shapes=[pltpu.VMEM((N_DEV, rs, c), pack.dtype), pltpu.VMEM((rs, c), pack.dtype)]
        + [pltpu.SemaphoreType.DMA((n_peer,))] * 6 + [pltpu.SemaphoreType.DMA((3,))],
        compiler_params=pltpu.CompilerParams(has_side_effects=True),
    )(pack, extra)


_HBM = pl.BlockSpec(memory_space=pltpu.HBM)
_SEM = pl.BlockSpec(memory_space=pltpu.SEMAPHORE)
_EFFECT = pltpu.SideEffectType.DATAFLOW_SIDE_EFFECTING
GATHER, SCATTER = "gather", "scatter"


def _in_hbm(a):
    return pltpu.with_memory_space_constraint(a, pltpu.HBM)


def _exchange_copies(kind, srcs, lands, send, recv):
    me = _my_index()
    copies = []
    for a, (src, land) in enumerate(zip(srcs, lands)):
        for k in range(1, N_DEV):
            s = a * (N_DEV - 1) + k - 1
            copies.append(pltpu.make_async_remote_copy(
                src_ref=src if kind == GATHER else src.at[jnp.bitwise_xor(me, k)], dst_ref=land.at[me],
                send_sem=send.at[s], recv_sem=recv.at[s], device_id=_peer(k), device_id_type=MESH))
    return copies


def exchange_start(kind, arrs, thru, name):
    n, nt = len(arrs), len(thru)
    n_sem = n * (N_DEV - 1)
    land_shapes = [((N_DEV,) + a.shape if kind == GATHER else a.shape) for a in arrs]

    def body(*refs):
        srcs, lands = refs[:n], refs[n:2 * n]
        send, recv = refs[2 * n + nt], refs[2 * n + nt + 1]
        local = refs[-1]
        me = _my_index()
        mine = [pltpu.make_async_copy(src if kind == GATHER else src.at[me], land.at[me], local.at[a])
                for a, (src, land) in enumerate(zip(srcs, lands))]
        for cp in mine:
            cp.start()
        for cp in mine:
            cp.wait()
        for cp in _exchange_copies(kind, srcs, lands, send, recv):
            cp.start()

    hbm_out = [pltpu.HBM(a.shape, a.dtype) for a in arrs] + [pltpu.HBM(s, a.dtype) for s, a in zip(land_shapes, arrs)] \
        + [pltpu.HBM(t.shape, t.dtype) for t in thru]
    outs = pl.pallas_call(
        body, name=name,
        out_shape=[pltpu.SemaphoreType.DMA((n_sem,)), pltpu.SemaphoreType.DMA((n_sem,))] + hbm_out,
        in_specs=[_HBM] * (2 * n + nt), out_specs=[_SEM, _SEM] + [_HBM] * (2 * n + nt),
        input_output_aliases={i: 2 + i for i in range(2 * n + nt)},
        scratch_shapes=[pltpu.SemaphoreType.DMA((n,))],
        compiler_params=pltpu.CompilerParams(has_side_effects=_EFFECT),
    )(*[_in_hbm(a) for a in arrs], *[_in_hbm(lax.empty(s, a.dtype)) for s, a in zip(land_shapes, arrs)],
      *[_in_hbm(t) for t in thru])
    return (kind, outs[0], outs[1], outs[2:2 + n], outs[2 + n:2 + 2 * n]), list(outs[2 + 2 * n:])


def exchange_wait(handle, after, name):
    kind, send, recv, srcs, lands = handle
    n = len(srcs)

    def body(*refs):
        for cp in _exchange_copies(kind, refs[:n], refs[n:2 * n], refs[2 * n], refs[2 * n + 1]):
            cp.wait_send()
            cp.wait_recv()

    outs = pl.pallas_call(
        body, name=name,
        out_shape=[pltpu.HBM(a.shape, a.dtype) for a in list(srcs) + list(lands)],
        in_specs=[_HBM] * (2 * n) + [_SEM, _SEM, pl.BlockSpec(memory_space=pl.ANY)], out_specs=[_HBM] * (2 * n),
        input_output_aliases={i: i for i in range(2 * n)},
        compiler_params=pltpu.CompilerParams(has_side_effects=_EFFECT),
    )(*srcs, *lands, send, recv, after)
    return list(outs[n:])


def _tile(n, pref):
    for t in pref:
        if n % t == 0:
            return t
    return n


def mm_nn(a, b3, out_dtype, name):
    m, k = a.shape
    nb, _, bn = b3.shape
    tm = _tile(m, (512, 256, 128))
    tn = _tile(bn, (1024, 896, 512, 256, 128))
    per = bn // tn

    def body(a_ref, b_ref, o_ref):
        o_ref[...] = _dot(a_ref[...], b_ref[...]).astype(o_ref.dtype)

    return pl.pallas_call(
        body, name=name, grid=(m // tm, nb, per),
        in_specs=[pl.BlockSpec((tm, k), lambda i, j, jj: (i, 0)),
                  pl.BlockSpec((None, k, tn), lambda i, j, jj: (j, 0, jj))],
        out_specs=pl.BlockSpec((tm, tn), lambda i, j, jj: (i, j * per + jj)),
        out_shape=jax.ShapeDtypeStruct((m, nb * bn), out_dtype),
        compiler_params=_cparams("parallel", "arbitrary", "arbitrary"),
    )(a, b3)


def mm_nt(a, w3, out_dtype, name):
    m, _ = a.shape
    nb, ko, bn = w3.shape
    tm = _tile(m, (512, 256, 128))
    tko = _tile(ko, (1024, 512, 256, 128))

    def body(a_ref, w_ref, o_ref, acc_ref):
        j = pl.program_id(2)

        @pl.when(j == 0)
        def _():
            acc_ref[...] = jnp.zeros_like(acc_ref)

        acc_ref[...] += _dot_nt(a_ref[...], w_ref[...])

        @pl.when(j == nb - 1)
        def _():
            o_ref[...] = acc_ref[...].astype(o_ref.dtype)

    return pl.pallas_call(
        body, name=name, grid=(m // tm, ko // tko, nb),
        in_specs=[pl.BlockSpec((tm, bn), lambda i, o, j: (i, j)),
                  pl.BlockSpec((None, tko, bn), lambda i, o, j: (j, o, 0))],
        out_specs=pl.BlockSpec((tm, tko), lambda i, o, j: (i, o)),
        out_shape=jax.ShapeDtypeStruct((m, ko), out_dtype),
        scratch_shapes=[pltpu.VMEM((tm, tko), f32)],
        compiler_params=_cparams("parallel", "arbitrary", "arbitrary"),
    )(a, w3)


def mm_tn(a, dy, ncb, out_dtype, name):
    l, ka = a.shape
    _, n = dy.shape
    bn = n // ncb
    tl = _tile(l, (512, 256, 128))
    tka = _tile(ka, (512, 256, 128))
    tn = _tile(bn, (1024, 896, 512, 256, 128))
    per = bn // tn
    nl = l // tl

    def body(a_ref, dy_ref, o_ref, acc_ref):
        s = pl.program_id(2)

        @pl.when(s == 0)
        def _():
            acc_ref[...] = jnp.zeros_like(acc_ref)

        acc_ref[...] += _dot_tn(a_ref[...], dy_ref[...])

        @pl.when(s == nl - 1)
        def _():
            o_ref[...] = acc_ref[...].astype(o_ref.dtype)

    return pl.pallas_call(
        body, name=name, grid=(ka // tka, n // tn, nl),
        in_specs=[pl.BlockSpec((tl, tka), lambda i, j, s: (s, i)),
                  pl.BlockSpec((tl, tn), lambda i, j, s: (s, j))],
        out_specs=pl.BlockSpec((None, tka, tn), lambda i, j, s: (j // per, i, j % per)),
        out_shape=jax.ShapeDtypeStruct((ncb, ka, bn), out_dtype),
        scratch_shapes=[pltpu.VMEM((tka, tn), f32)],
        compiler_params=_cparams("parallel", "parallel", "arbitrary"),
    )(a, dy)


def mod_part(c_all, w_mod, b_cols):
    nl, d, cols = w_mod.shape

    def body(c_ref, w_ref, b_ref, o_ref):
        cond = _silu(c_ref[...]).astype(bf16)
        o_ref[...] = _dot(cond, w_ref[...].astype(bf16)) + b_ref[...]

    return pl.pallas_call(
        body, name="mod_part", grid=(nl,),
        in_specs=[pl.BlockSpec((N_DEV, d), lambda l: (0, 0)),
                  pl.BlockSpec((None, d, cols), lambda l: (l, 0, 0)),
                  pl.BlockSpec((None, 1, cols), lambda l: (l, 0, 0))],
        out_specs=pl.BlockSpec((None, N_DEV, cols), lambda l: (l, 0, 0)),
        out_shape=jax.ShapeDtypeStruct((nl, N_DEV, cols), f32),
        compiler_params=_cparams("arbitrary"),
    )(c_all, w_mod, b_cols.reshape(nl, 1, cols))


def _row_tile(l):
    return _tile(l, (256, 128))


def prenorm_fwd(x, g, shift, scale, name):
    l, d = x.shape
    tm = _row_tile(l)

    def body(x_ref, g_ref, sh_ref, sc_ref, h_ref):
        xv = x_ref[...]
        r = lax.rsqrt(jnp.mean(xv * xv, axis=-1, keepdims=True) + EPS)
        h_ref[...] = (xv * r * (g_ref[...] * (1.0 + sc_ref[...])) + sh_ref[...]).astype(h_ref.dtype)

    return pl.pallas_call(
        body, name=name, grid=(l // tm,),
        in_specs=[pl.BlockSpec((tm, d), lambda i: (i, 0)), _row(d), _row(d), _row(d)],
        out_specs=pl.BlockSpec((tm, d), lambda i: (i, 0)),
        out_shape=jax.ShapeDtypeStruct((l, d), bf16),
        compiler_params=_cparams("parallel"),
    )(x, g, shift, scale)


def post_fwd(x, y, gate, g, name):
    l, d = x.shape
    tm = _row_tile(l)

    def body(x_ref, y_ref, gate_ref, g_ref, o_ref):
        yv = y_ref[...]
        r = lax.rsqrt(jnp.mean(yv * yv, axis=-1, keepdims=True) + EPS)
        o_ref[...] = x_ref[...] + gate_ref[...] * (yv * r * g_ref[...])

    blk = pl.BlockSpec((tm, d), lambda i: (i, 0))
    return pl.pallas_call(
        body, name=name, grid=(l // tm,),
        in_specs=[blk, blk, _row(d), _row(d)], out_specs=blk,
        out_shape=jax.ShapeDtypeStruct((l, d), f32),
        compiler_params=_cparams("parallel"),
    )(x, y, gate, g)


def final_loss(x, y, gate, g, target):
    l, d = x.shape
    tm = _row_tile(l)

    def body(x_ref, y_ref, gate_ref, g_ref, t_ref, dx_ref, loss_ref):
        @pl.when(pl.program_id(0) == 0)
        def _():
            loss_ref[...] = jnp.zeros_like(loss_ref)

        yv = y_ref[...]
        r = lax.rsqrt(jnp.mean(yv * yv, axis=-1, keepdims=True) + EPS)
        diff = x_ref[...] + gate_ref[...] * (yv * r * g_ref[...]) - t_ref[...]
        dx_ref[...] = diff * (1.0 / d)
        loss_ref[...] += jnp.sum(diff * diff)

    blk = pl.BlockSpec((tm, d), lambda i: (i, 0))
    return pl.pallas_call(
        body, name="final_loss", grid=(l // tm,),
        in_specs=[blk, blk, _row(d), _row(d), blk],
        out_specs=[blk, pl.BlockSpec((SUBLANES, HEAD), lambda i: (0, 0))],
        out_shape=[jax.ShapeDtypeStruct((l, d), f32), jax.ShapeDtypeStruct((SUBLANES, HEAD), f32)],
        compiler_params=_cparams("arbitrary"),
    )(x, y, gate, g, target)


def post_bwd(dx, y, gate, g, name):
    l, d = dx.shape
    tm = _row_tile(l)

    def body(dx_ref, y_ref, gate_ref, g_ref, dy_ref, dgate_ref, dg_ref):
        @pl.when(pl.program_id(0) == 0)
        def _():
            dgate_ref[...] = jnp.zeros_like(dgate_ref)
            dg_ref[...] = jnp.zeros_like(dg_ref)

        yv, dxv, gv = y_ref[...], dx_ref[...], g_ref[...]
        r = lax.rsqrt(jnp.mean(yv * yv, axis=-1, keepdims=True) + EPS)
        yn = yv * r
        dgate_ref[...] += jnp.sum(dxv * yn * gv, axis=0, keepdims=True)
        dyg = dxv * gate_ref[...]
        dg_ref[...] += jnp.sum(dyg * yn, axis=0, keepdims=True)
        dyn = dyg * gv
        dy_ref[...] = (r * (dyn - yn * jnp.mean(dyn * yn, axis=-1, keepdims=True))).astype(dy_ref.dtype)

    blk = pl.BlockSpec((tm, d), lambda i: (i, 0))
    return pl.pallas_call(
        body, name=name, grid=(l // tm,),
        in_specs=[blk, blk, _row(d), _row(d)], out_specs=[blk, _row(d), _row(d)],
        out_shape=[jax.ShapeDtypeStruct((l, d), bf16), jax.ShapeDtypeStruct((1, d), f32),
                   jax.ShapeDtypeStruct((1, d), f32)],
        compiler_params=_cparams("arbitrary"),
    )(dx, y, gate, g)


def prenorm_bwd(dh, x, dx_next, g, scale, name):
    l, d = x.shape
    tm = _row_tile(l)

    def body(dh_ref, x_ref, dxn_ref, g_ref, sc_ref, dx_ref, dsh_ref, dsc_ref, dg_ref):
        @pl.when(pl.program_id(0) == 0)
        def _():
            dsh_ref[...] = jnp.zeros_like(dsh_ref)
            dsc_ref[...] = jnp.zeros_like(dsc_ref)
            dg_ref[...] = jnp.zeros_like(dg_ref)

        xv, dhv, gv, sc1 = x_ref[...], dh_ref[...], g_ref[...], 1.0 + sc_ref[...]
        r = lax.rsqrt(jnp.mean(xv * xv, axis=-1, keepdims=True) + EPS)
        xn = xv * r
        dhx = dhv * xn
        dsh_ref[...] += jnp.sum(dhv, axis=0, keepdims=True)
        dsc_ref[...] += jnp.sum(dhx * gv, axis=0, keepdims=True)
        dg_ref[...] += jnp.sum(dhx * sc1, axis=0, keepdims=True)
        dxn = dhv * (gv * sc1)
        dx_ref[...] = dxn_ref[...] + r * (dxn - xn * jnp.mean(dxn * xn, axis=-1, keepdims=True))

    blk = pl.BlockSpec((tm, d), lambda i: (i, 0))
    return pl.pallas_call(
        body, name=name, grid=(l // tm,),
        in_specs=[blk, blk, blk, _row(d), _row(d)], out_specs=[blk, _row(d), _row(d), _row(d)],
        out_shape=[jax.ShapeDtypeStruct((l, d), f32)] + [jax.ShapeDtypeStruct((1, d), f32)] * 3,
        compiler_params=_cparams("arbitrary"),
    )(dh, x, dx_next, g, scale)


def _tril_mask():
    r = lax.broadcasted_iota(jnp.int32, (HEAD, HEAD), 0)
    c = lax.broadcasted_iota(jnp.int32, (HEAD, HEAD), 1)
    return r >= c


def sgu_fwd(proj, norm_g, w_s, b_s):
    l = proj.shape[0]
    nh = w_s.shape[0]
    wa = nh * HEAD

    def body(au_ref, av_ref, az_ref, ng_ref, w_ref, b_ref, o_ref):
        tril = _tril_mask()
        for h in range(nh):
            sl = slice(h * HEAD, (h + 1) * HEAD)
            gv = _gelu(av_ref[:, sl].astype(f32))
            r = lax.rsqrt(jnp.mean(gv * gv, axis=-1, keepdims=True) + EPS)
            vh = gv * r * ng_ref[:, sl]
            wm = jnp.where(tril, w_ref[h], 0.0).astype(bf16)
            s = _dot(wm, vh.astype(bf16)) + b_ref[h]
            o_ref[:, sl] = (_gelu(au_ref[:, sl].astype(f32)) * s * _silu(az_ref[:, sl].astype(f32))).astype(o_ref.dtype)

    def col(j):
        return pl.BlockSpec((HEAD, wa), lambda n: (n, j))

    return pl.pallas_call(
        body, name="sgu_fwd", grid=(l // HEAD,),
        in_specs=[col(0), col(1), col(2), _row(wa),
                  pl.BlockSpec((nh, HEAD, HEAD), lambda n: (0, 0, 0)), pl.BlockSpec((nh, HEAD, 1), lambda n: (0, 0, 0))],
        out_specs=pl.BlockSpec((HEAD, wa), lambda n: (n, 0)),
        out_shape=jax.ShapeDtypeStruct((l, wa), bf16),
        compiler_params=_cparams("parallel"),
    )(proj, proj, proj, norm_g, w_s, b_s)


def sgu_bwd(proj, dcat, norm_g, w_s, b_s):
    l = proj.shape[0]
    nh = w_s.shape[0]
    wa = nh * HEAD

    def body(au_ref, av_ref, az_ref, do_ref, ng_ref, w_ref, b_ref, da_ref, dw_ref, db_ref, dng_ref):
        @pl.when(pl.program_id(0) == 0)
        def _():
            dw_ref[...] = jnp.zeros_like(dw_ref)
            db_ref[...] = jnp.zeros_like(db_ref)
            dng_ref[...] = jnp.zeros_like(dng_ref)

        tril = _tril_mask()
        for h in range(nh):
            sl = slice(h * HEAD, (h + 1) * HEAD)
            au, av, az = au_ref[:, sl].astype(f32), av_ref[:, sl].astype(f32), az_ref[:, sl].astype(f32)
            ng = ng_ref[:, sl]
            gv = _gelu(av)
            r = lax.rsqrt(jnp.mean(gv * gv, axis=-1, keepdims=True) + EPS)
            gvn = gv * r
            vh = (gvn * ng).astype(bf16)
            wm = jnp.where(tril, w_ref[h], 0.0).astype(bf16)
            s = _dot(wm, vh) + b_ref[h]
            gu, sz = _gelu(au), _silu(az)
            dov = do_ref[:, sl].astype(f32)
            da_ref[:, sl] = (dov * s * sz * _gelu_grad(au)).astype(da_ref.dtype)
            da_ref[:, 2 * wa + h * HEAD:2 * wa + (h + 1) * HEAD] = (dov * gu * s * _silu_grad(az)).astype(da_ref.dtype)
            ds = dov * gu * sz
            db_ref[h] += jnp.sum(ds, axis=-1, keepdims=True)
            dsb = ds.astype(bf16)
            dw_ref[h] += jnp.where(tril, _dot_nt(dsb, vh), 0.0)
            dvh = _dot_tn(wm, dsb)
            dng_ref[:, sl] += jnp.sum(dvh * gvn, axis=0, keepdims=True)
            dgvn = dvh * ng
            dgv = r * (dgvn - gvn * jnp.mean(dgvn * gvn, axis=-1, keepdims=True))
            da_ref[:, wa + h * HEAD:wa + (h + 1) * HEAD] = (dgv * _gelu_grad(av)).astype(da_ref.dtype)

    def col(j):
        return pl.BlockSpec((HEAD, wa), lambda n: (n, j))

    whole_w = pl.BlockSpec((nh, HEAD, HEAD), lambda n: (0, 0, 0))
    whole_b = pl.BlockSpec((nh, HEAD, 1), lambda n: (0, 0, 0))
    return pl.pallas_call(
        body, name="sgu_bwd", grid=(l // HEAD,),
        in_specs=[col(0), col(1), col(2), col(0), _row(wa), whole_w, whole_b],
        out_specs=[pl.BlockSpec((HEAD, 3 * wa), lambda n: (n, 0)), whole_w, whole_b, _row(wa)],
        out_shape=[jax.ShapeDtypeStruct((l, 3 * wa), bf16), jax.ShapeDtypeStruct((nh, HEAD, HEAD), f32),
                   jax.ShapeDtypeStruct((nh, HEAD, 1), f32), jax.ShapeDtypeStruct((1, wa), f32)],
        compiler_params=_cparams("arbitrary"),
    )(proj, proj, proj, dcat, norm_g, w_s, b_s)


_LOG2E = 1.0 / math.log(2.0)


def _sb_scores(q, k, scale):
    z = _dot_nt(q, k) * (scale * _LOG2E)
    return z, jnp.maximum(z, 0.0) + jnp.log2(1.0 + jnp.exp2(-jnp.abs(z)))


def _sb_sum_matrix(tri):
    s = lax.broadcasted_iota(jnp.int32, (2 * HEAD, 2 * HEAD), 0) % HEAD
    j = lax.broadcasted_iota(jnp.int32, (2 * HEAD, 2 * HEAD), 1)
    return jnp.where(jnp.logical_or(j >= HEAD, tri(s, j)), 1.0, 0.0).astype(bf16)


def _sb_sums(x, sums):
    c2 = _dot(jnp.concatenate(_split_bf16(x), axis=1), sums)
    return c2[:, :HEAD], c2[:, HEAD:]


def _sb_q_tile(l):
    return _tile(l, (512, 256, 128))


def _sb_heads_per_step(nh, most):
    return _tile(nh, tuple(h for h in (4, 2) if h <= most))


def sb_fwd(proj, nh):
    l = proj.shape[0]
    wb = nh * HEAD
    tq = _sb_q_tile(l)
    band = tq // HEAD
    hp = _sb_heads_per_step(nh, 4)
    scale = 1.0 / math.sqrt(HEAD)
    qc, kc, vc, zc = 3 * nh, 4 * nh, 5 * nh, 6 * nh

    def body(q_ref, k_ref, v_ref, bz_ref, o_ref, att_ref, tot_ref):
        i = pl.program_id(1)
        sums = _sb_sum_matrix(lambda s, j: s > j)
        t_pos = i * tq + lax.broadcasted_iota(jnp.int32, (tq, HEAD), 0)
        s_off = lax.broadcasted_iota(jnp.int32, (tq, HEAD), 1)

        def step(j, carry, masked):
            rows = pl.ds(pl.multiple_of(j * HEAD, HEAD), HEAD)
            out = []
            for e in range(hp):
                acc, tot = carry[e]
                sl = slice(e * HEAD, (e + 1) * HEAD)
                z, sp = _sb_scores(q_ref[:, sl], k_ref[rows, sl], scale)
                lb = z - sp
                if masked:
                    mask = s_off + j * HEAD < t_pos
                    sp = jnp.where(mask, sp, 0.0)
                later, total = _sb_sums(sp, sums)
                w = jnp.exp2(lb + tot - later)
                if masked:
                    w = jnp.where(mask, w, 0.0)
                out.append((acc + _dot(w.astype(bf16), v_ref[rows, sl]), tot - total))
            return tuple(out)

        zero = jnp.zeros((tq, HEAD), f32)
        carry = lax.fori_loop(0, band, lambda t, c: step(band * i + band - 1 - t, c, True), ((zero, zero),) * hp)
        carry = lax.fori_loop(0, band * i, lambda t, c: step(band * i - 1 - t, c, False), carry)
        for e in range(hp):
            acc, tot = carry[e]
            sl = slice(e * HEAD, (e + 1) * HEAD)
            att_ref[:, sl] = acc.astype(att_ref.dtype)
            o_ref[:, sl] = (acc * _silu(bz_ref[:, sl].astype(f32))).astype(o_ref.dtype)
            tot_ref[e] = tot[:, :1]

    blk = lambda c0: pl.BlockSpec((tq, hp * HEAD), lambda g, i: (i, c0 // hp + g))
    head = lambda c0: pl.BlockSpec((l, hp * HEAD), lambda g, i: (0, c0 // hp + g))
    return pl.pallas_call(
        body, name="sb_fwd", grid=(nh // hp, l // tq),
        in_specs=[blk(qc), head(kc), head(vc), blk(zc)],
        out_specs=[blk(0), blk(0), pl.BlockSpec((hp, tq, 1), lambda g, i: (g, i, 0))],
        out_shape=[jax.ShapeDtypeStruct((l, wb), bf16), jax.ShapeDtypeStruct((l, wb), bf16),
                   jax.ShapeDtypeStruct((nh, l, 1), f32)],
        compiler_params=_cparams("parallel", "arbitrary"),
    )(proj, proj, proj, proj)


def sb_bwd(proj, dcat, att, tot, nh):
    l = proj.shape[0]
    wb = nh * HEAD
    tq = _sb_q_tile(l)
    band = tq // HEAD
    nq = l // tq
    hp = _sb_heads_per_step(nh, 2)
    scale = 1.0 / math.sqrt(HEAD)
    qc, kc, vc, zc = 3 * nh, 4 * nh, 5 * nh, 6 * nh

    def body(q_ref, k_ref, v_ref, bz_ref, do_ref, att_ref, tot_ref, dq_ref, dk_ref, dv_ref, dbz_ref, dk_acc, dv_acc,
             dob_ref):
        i = pl.program_id(1)

        @pl.when(i == 0)
        def _():
            dk_acc[...] = jnp.zeros_like(dk_acc)
            dv_acc[...] = jnp.zeros_like(dv_acc)

        bz = bz_ref[...].astype(f32)
        dov = do_ref[...].astype(f32)
        dbz_ref[...] = (dov * att_ref[...].astype(f32) * _silu_grad(bz)).astype(dbz_ref.dtype)
        dob_ref[...] = (dov * _silu(bz)).astype(bf16)
        upto = _sb_sum_matrix(lambda s, j: s <= j)
        before = _sb_sum_matrix(lambda j, s: j < s)
        t_pos = i * tq + lax.broadcasted_iota(jnp.int32, (tq, HEAD), 0)
        s_off = lax.broadcasted_iota(jnp.int32, (tq, HEAD), 1)

        def step(j, carry, masked):
            rows = pl.ds(pl.multiple_of(j * HEAD, HEAD), HEAD)
            out = []
            for h in range(hp):
                dq, sp_seen, e_seen = carry[h]
                sl = slice(h * HEAD, (h + 1) * HEAD)
                q, kj, vj, dob = q_ref[:, sl], k_ref[rows, sl], v_ref[rows, sl], dob_ref[:, sl]
                z, sp = _sb_scores(q, kj, scale)
                lb = z - sp
                if masked:
                    mask = s_off + j * HEAD < t_pos
                    sp = jnp.where(mask, sp, 0.0)
                sp_upto, sp_total = _sb_sums(sp, upto)
                w = jnp.exp2(lb + sp_seen + sp_upto)
                if masked:
                    w = jnp.where(mask, w, 0.0)
                dv_acc[rows, sl] += _dot_tn(w.astype(bf16), dob)
                e = _dot_nt(dob, vj) * w
                e_before, e_total = _sb_sums(e, before)
                dz = (e - (e + e_seen + e_before) * jnp.exp2(lb)) * scale
                if masked:
                    dz = jnp.where(mask, dz, 0.0)
                dz = dz.astype(bf16)
                dk_acc[rows, sl] += _dot_tn(dz, q)
                out.append((dq + _dot(dz, kj), sp_seen + sp_total, e_seen + e_total))
            return tuple(out)

        zero = jnp.zeros((tq, HEAD), f32)
        init = tuple((zero, jnp.broadcast_to(tot_ref[h], (tq, HEAD)), zero) for h in range(hp))
        carry = lax.fori_loop(0, band * i, lambda j, c: step(j, c, False), init)
        carry = lax.fori_loop(0, band, lambda t, c: step(band * i + t, c, True), carry)
        for h in range(hp):
            dq_ref[:, h * HEAD:(h + 1) * HEAD] = carry[h][0].astype(dq_ref.dtype)

        @pl.when(i == nq - 1)
        def _():
            dk_ref[...] = dk_acc[...].astype(dk_ref.dtype)
            dv_ref[...] = dv_acc[...].astype(dv_ref.dtype)

    blk = lambda c0: pl.BlockSpec((tq, hp * HEAD), lambda g, i: (i, c0 // hp + g))
    head = lambda c0: pl.BlockSpec((l, hp * HEAD), lambda g, i: (0, c0 // hp + g))
    return pl.pallas_call(
        body, name="sb_bwd", grid=(nh // hp, nq),
        in_specs=[blk(qc), head(kc), head(vc), blk(zc), blk(nh), blk(0),
                  pl.BlockSpec((hp, tq, 1), lambda g, i: (g, i, 0))],
        out_specs=[blk(0), head(0), head(0), blk(0)],
        out_shape=[jax.ShapeDtypeStruct((l, wb), bf16)] * 4,
        scratch_shapes=[pltpu.VMEM((l, hp * HEAD), f32), pltpu.VMEM((l, hp * HEAD), f32),
                        pltpu.VMEM((tq, hp * HEAD), bf16)],
        compiler_params=_cparams("parallel", "arbitrary"),
    )(proj, proj, proj, proj, dcat, att, tot)


def _disc(lr, li, ldt):
    dt = jnp.exp(ldt)
    mag = jnp.exp(lr * dt)
    a_re = mag * jnp.cos(li * dt)
    a_im = mag * jnp.sin(li * dt)
    den = lr * lr + li * li
    nr = a_re - 1.0
    return a_re, a_im, (nr * lr + a_im * li) / den, (a_im * lr - nr * li) / den


def s5_params_fwd(lr, li, ldt, bt_re, bt_im):
    g, c, p = bt_re.shape

    def body(lr_ref, li_ref, ldt_ref, br_ref, bi_ref, ar_ref, ai_ref, bbr_ref, bbi_ref):
        a_re, a_im, cr, ci = _disc(lr_ref[...], li_ref[...], ldt_ref[...])
        ar_ref[...] = a_re
        ai_ref[...] = a_im
        for k in range(c):
            br, bi = br_ref[:, k, :], bi_ref[:, k, :]
            bbr_ref[:, k, :] = cr * br - ci * bi
            bbi_ref[:, k, :] = cr * bi + ci * br

    return pl.pallas_call(
        body, name="s5_params_fwd",
        out_shape=[jax.ShapeDtypeStruct((g, p), f32)] * 2 + [jax.ShapeDtypeStruct((g, c, p), f32)] * 2,
    )(lr, li, ldt, bt_re, bt_im)


def s5_params_bwd(lr, li, ldt, bt_re, bt_im, da_re, da_im, dbbt_re, dbbt_im):
    g, c, p = bt_re.shape

    def body(lr_ref, li_ref, ldt_ref, br_ref, bi_ref, dar_ref, dai_ref, dbbr_ref, dbbi_ref,
             dlr_ref, dli_ref, dldt_ref, dbr_ref, dbi_ref):
        (a_re, a_im, cr, ci), vjp = jax.vjp(_disc, lr_ref[...], li_ref[...], ldt_ref[...])
        dcr = jnp.zeros((g, p), f32)
        dci = jnp.zeros((g, p), f32)
        for k in range(c):
            br, bi = br_ref[:, k, :], bi_ref[:, k, :]
            dr, di = dbbr_ref[:, k, :], dbbi_ref[:, k, :]
            dcr += dr * br + di * bi
            dci += di * br - dr * bi
            dbr_ref[:, k, :] = cr * dr + ci * di
            dbi_ref[:, k, :] = cr * di - ci * dr
        dlr, dli, dldt = vjp((dar_ref[...], dai_ref[...], dcr, dci))
        dlr_ref[...] = dlr
        dli_ref[...] = dli
        dldt_ref[...] = dldt

    return pl.pallas_call(
        body, name="s5_params_bwd",
        out_shape=[jax.ShapeDtypeStruct((g, p), f32)] * 2 + [jax.ShapeDtypeStruct((g, 1), f32)]
        + [jax.ShapeDtypeStruct((g, c, p), f32)] * 2,
    )(lr, li, ldt, bt_re, bt_im, da_re, da_im, dbbt_re, dbbt_im)


def _cmul(ar, ai, br, bi):
    return ar * br - ai * bi, ar * bi + ai * br


def _power_tables(ar, ai):
    rows = lax.broadcasted_iota(jnp.int32, (SUBLANES, ar.shape[1]), 0)
    pr = jnp.zeros((SUBLANES, ar.shape[1]), f32)
    pi = jnp.zeros((SUBLANES, ar.shape[1]), f32)
    cr, ci = ar, ai
    pows = {}
    for r in range(SUBLANES):
        pows[r + 1] = (cr, ci)
        pr = jnp.where(rows == r, cr, pr)
        pi = jnp.where(rows == r, ci, pi)
        cr, ci = _cmul(cr, ci, ar, ai)
    return [pows[1], pows[2], pows[4]], pr, pi


def _ssm_time_tile(l):
    return _tile(l, (512, 256, 128))


def ssm_fwd(u, bre3, bim3, cre3, cimn3, a_re, a_im, d_skip):
    l, w = u.shape
    nj = w // HEAD
    ns = STATES_PER_LANE_BLOCK
    tt = _ssm_time_tile(l)

    def body(u_ref, bre_ref, bim_ref, cre_ref, cim_ref, ar_ref, ai_ref, d_ref, y_ref, hr_ref, hi_ref, cr_ref, ci_ref):
        @pl.when(pl.program_id(1) == 0)
        def _():
            cr_ref[...] = jnp.zeros_like(cr_ref)
            ci_ref[...] = jnp.zeros_like(ci_ref)

        uv = u_ref[...]
        hr_ref[...] = _dot(uv, bre_ref[...])
        hi_ref[...] = _dot(uv, bim_ref[...])
        steps, pr, pi = _power_tables(ar_ref[...], ai_ref[...])
        rows = lax.broadcasted_iota(jnp.int32, (SUBLANES, ns), 0)

        def blk(b, carry):
            cr, ci = carry
            sl = pl.ds(pl.multiple_of(b * SUBLANES, SUBLANES), SUBLANES)
            xr, xi = hr_ref[sl, :], hi_ref[sl, :]
            for d, (sr_, si_) in zip((1, 2, 4), steps):
                keep = rows >= d
                qr = jnp.where(keep, pltpu.roll(xr, d, axis=0), 0.0)
                qi = jnp.where(keep, pltpu.roll(xi, d, axis=0), 0.0)
                mr, mi = _cmul(sr_, si_, qr, qi)
                xr, xi = xr + mr, xi + mi
            mr, mi = _cmul(pr, pi, cr, ci)
            xr, xi = xr + mr, xi + mi
            hr_ref[sl, :] = xr
            hi_ref[sl, :] = xi
            return xr[SUBLANES - 1:, :], xi[SUBLANES - 1:, :]

        cr, ci = lax.fori_loop(0, tt // SUBLANES, blk, (cr_ref[...], ci_ref[...]))
        cr_ref[...] = cr
        ci_ref[...] = ci
        y = _dot(hr_ref[...].astype(bf16), cre_ref[...]) + _dot(hi_ref[...].astype(bf16), cim_ref[...])
        y_ref[...] = y + d_ref[...] * uv.astype(f32)

    lane = pl.BlockSpec((tt, HEAD), lambda j, i: (i, j))
    st = pl.BlockSpec((tt, ns), lambda j, i: (i, j))
    b3 = pl.BlockSpec((None, HEAD, ns), lambda j, i: (j, 0, 0))
    c3 = pl.BlockSpec((None, ns, HEAD), lambda j, i: (j, 0, 0))
    arow = pl.BlockSpec((1, ns), lambda j, i: (0, j))
    return pl.pallas_call(
        body, name="ssm_fwd", grid=(nj, l // tt),
        in_specs=[lane, b3, b3, c3, c3, arow, arow, pl.BlockSpec((1, HEAD), lambda j, i: (0, j))],
        out_specs=[lane, st, st],
        out_shape=[jax.ShapeDtypeStruct((l, w), f32), jax.ShapeDtypeStruct((l, nj * ns), f32),
                   jax.ShapeDtypeStruct((l, nj * ns), f32)],
        scratch_shapes=[pltpu.VMEM((1, ns), f32), pltpu.VMEM((1, ns), f32)],
        compiler_params=_cparams("parallel", "arbitrary"),
    )(u, bre3, bim3, cre3, cimn3, a_re, a_im, d_skip)


def ssm_bwd(dy, u, h_re, h_im, bre3, bim3, cre3, cimn3, a_re, a_im, d_skip):
    l, w = u.shape
    nj = w // HEAD
    ns = STATES_PER_LANE_BLOCK
    tt = _ssm_time_tile(l)
    nt = l // tt

    def body(dy_ref, u_ref, hr_ref, hi_ref, bre_ref, bim_ref, cre_ref, cim_ref, ar_ref, ai_ref, d_ref,
             du_ref, dd_ref, dar_ref, dai_ref, dbre_ref, dbim_ref, dcre_ref, dcim_ref, kr_ref, ki_ref, cr_ref, ci_ref,
             accr_ref, acci_ref):
        i = pl.program_id(1)

        @pl.when(i == 0)
        def _():
            for ref in (cr_ref, ci_ref, accr_ref, acci_ref, dd_ref, dbre_ref, dbim_ref, dcre_ref, dcim_ref):
                ref[...] = jnp.zeros_like(ref)

        dyv = dy_ref[...]
        dyb = dyv.astype(bf16)
        uv = u_ref[...]
        kr_ref[...] = _dot_nt(dyb, cre_ref[...])
        ki_ref[...] = _dot_nt(dyb, cim_ref[...])
        steps, pr, pi = _power_tables(ar_ref[...], -ai_ref[...])
        rows = lax.broadcasted_iota(jnp.int32, (SUBLANES, ns), 0)
        qr = jnp.zeros((SUBLANES, ns), f32)
        qi = jnp.zeros((SUBLANES, ns), f32)
        for r in range(SUBLANES):
            qr = jnp.where(rows == r, pr[SUBLANES - 1 - r:SUBLANES - r, :], qr)
            qi = jnp.where(rows == r, pi[SUBLANES - 1 - r:SUBLANES - r, :], qi)
        nb = tt // SUBLANES

        def blk(t, carry):
            cr, ci, accr, acci = carry
            sl = pl.ds(pl.multiple_of((nb - 1 - t) * SUBLANES, SUBLANES), SUBLANES)
            xr, xi = kr_ref[sl, :], ki_ref[sl, :]
            for d, (sr_, si_) in zip((1, 2, 4), steps):
                keep = rows < SUBLANES - d
                zr = jnp.where(keep, pltpu.roll(xr, SUBLANES - d, axis=0), 0.0)
                zi = jnp.where(keep, pltpu.roll(xi, SUBLANES - d, axis=0), 0.0)
                mr, mi = _cmul(sr_, si_, zr, zi)
                xr, xi = xr + mr, xi + mi
            mr, mi = _cmul(qr, qi, cr, ci)
            xr, xi = xr + mr, xi + mi
            kr_ref[sl, :] = xr
            ki_ref[sl, :] = xi
            last = rows == SUBLANES - 1
            nr = jnp.where(last, cr, pltpu.roll(xr, SUBLANES - 1, axis=0))
            ni = jnp.where(last, ci, pltpu.roll(xi, SUBLANES - 1, axis=0))
            hr, hi = hr_ref[sl, :], hi_ref[sl, :]
            accr = accr + nr * hr + ni * hi
            acci = acci + ni * hr - nr * hi
            return xr[:1, :], xi[:1, :], accr, acci

        cr, ci, accr, acci = lax.fori_loop(0, nb, blk, (cr_ref[...], ci_ref[...], accr_ref[...], acci_ref[...]))
        cr_ref[...] = cr
        ci_ref[...] = ci
        accr_ref[...] = accr
        acci_ref[...] = acci
        kr, ki = kr_ref[...].astype(bf16), ki_ref[...].astype(bf16)
        du = _dot_nt(kr, bre_ref[...]) + _dot_nt(ki, bim_ref[...]) + d_ref[...] * dyv
        du_ref[...] = du.astype(du_ref.dtype)
        dd_ref[...] += jnp.sum(dyv * uv.astype(f32), axis=0, keepdims=True)
        dbre_ref[...] += _dot_tn(uv, kr)
        dbim_ref[...] += _dot_tn(uv, ki)
        dcre_ref[...] += _dot_tn(hr_ref[...].astype(bf16), dyb)
        dcim_ref[...] += _dot_tn(hi_ref[...].astype(bf16), dyb)

        @pl.when(i == nt - 1)
        def _():
            dar_ref[...] = jnp.sum(accr_ref[...], axis=0, keepdims=True)
            dai_ref[...] = jnp.sum(acci_ref[...], axis=0, keepdims=True)

    lane = pl.BlockSpec((tt, HEAD), lambda j, i: (nt - 1 - i, j))
    st = pl.BlockSpec((tt, ns), lambda j, i: (nt - 1 - i, j))
    b3 = pl.BlockSpec((None, HEAD, ns), lambda j, i: (j, 0, 0))
    c3 = pl.BlockSpec((None, ns, HEAD), lambda j, i: (j, 0, 0))
    arow = pl.BlockSpec((1, ns), lambda j, i: (0, j))
    drow = pl.BlockSpec((1, HEAD), lambda j, i: (0, j))
    return pl.pallas_call(
        body, name="ssm_bwd", grid=(nj, nt),
        in_specs=[lane, lane, st, st, b3, b3, c3, c3, arow, arow, drow],
        out_specs=[lane, drow, arow, arow, b3, b3, c3, c3],
        out_shape=[jax.ShapeDtypeStruct((l, w), bf16), jax.ShapeDtypeStruct((1, w), f32),
                   jax.ShapeDtypeStruct((1, nj * ns), f32), jax.ShapeDtypeStruct((1, nj * ns), f32),
                   jax.ShapeDtypeStruct((nj, HEAD, ns), f32), jax.ShapeDtypeStruct((nj, HEAD, ns), f32),
                   jax.ShapeDtypeStruct((nj, ns, HEAD), f32), jax.ShapeDtypeStruct((nj, ns, HEAD), f32)],
        scratch_shapes=[pltpu.VMEM((tt, ns), f32), pltpu.VMEM((tt, ns), f32), pltpu.VMEM((1, ns), f32),
                        pltpu.VMEM((1, ns), f32), pltpu.VMEM((SUBLANES, ns), f32), pltpu.VMEM((SUBLANES, ns), f32)],
        compiler_params=_cparams("parallel", "arbitrary"),
    )(dy, u, h_re, h_im, bre3, bim3, cre3, cimn3, a_re, a_im, d_skip)


def glu_fwd(y, z_src, w_glu, b_glu):
    l, w = y.shape
    tm = _row_tile(l)

    def body(y_ref, z_ref, w_ref, b_ref, g_ref, t_ref, o_ref):
        g = _gelu(y_ref[...])
        gb = g.astype(bf16)
        t = _dot(gb, w_ref[...]) + b_ref[...]
        g_ref[...] = gb
        t_ref[...] = t
        o_ref[...] = (g * jax.nn.sigmoid(t) * _silu(z_ref[...].astype(f32))).astype(o_ref.dtype)

    blk = pl.BlockSpec((tm, w), lambda i: (i, 0))
    return pl.pallas_call(
        body, name="glu_fwd", grid=(l // tm,),
        in_specs=[blk, pl.BlockSpec((tm, w), lambda i: (i, 1)), pl.BlockSpec((w, w), lambda i: (0, 0)), _row(w)],
        out_specs=[blk, blk, blk],
        out_shape=[jax.ShapeDtypeStruct((l, w), bf16), jax.ShapeDtypeStruct((l, w), f32),
                   jax.ShapeDtypeStruct((l, w), bf16)],
        compiler_params=_cparams("parallel"),
    )(y, z_src, w_glu, b_glu)


def glu_bwd(dout, y, t, z_src, w_glu):
    l, w = y.shape
    tm = _row_tile(l)

    def body(do_ref, y_ref, t_ref, z_ref, w_ref, dy_ref, dz_ref, dt_ref, db_ref):
        @pl.when(pl.program_id(0) == 0)
        def _():
            db_ref[...] = jnp.zeros_like(db_ref)

        yv, zv, dov = y_ref[...], z_ref[...].astype(f32), do_ref[...]
        g = _gelu(yv)
        sg = jax.nn.sigmoid(t_ref[...])
        dy2 = dov * _silu(zv)
        dz_ref[...] = (dov * g * sg * _silu_grad(zv)).astype(dz_ref.dtype)
        dt = dy2 * g * sg * (1.0 - sg)
        dtb = dt.astype(bf16)
        dt_ref[...] = dtb
        db_ref[...] += jnp.sum(dt, axis=0, keepdims=True)
        dg = dy2 * sg + _dot_nt(dtb, w_ref[...])
        dy_ref[...] = dg * _gelu_grad(yv)

    blk = pl.BlockSpec((tm, w), lambda i: (i, 0))
    return pl.pallas_call(
        body, name="glu_bwd", grid=(l // tm,),
        in_specs=[blk, blk, blk, pl.BlockSpec((tm, w), lambda i: (i, 1)), pl.BlockSpec((w, w), lambda i: (0, 0))],
        out_specs=[blk, blk, blk, _row(w)],
        out_shape=[jax.ShapeDtypeStruct((l, w), f32), jax.ShapeDtypeStruct((l, w), bf16),
                   jax.ShapeDtypeStruct((l, w), bf16), jax.ShapeDtypeStruct((1, w), f32)],
        compiler_params=_cparams("arbitrary"),
    )(dout, y, t, z_src, w_glu)


def _adamw(w, g, m, v):
    m = ADAM_B1 * m + (1.0 - ADAM_B1) * g
    v = ADAM_B2 * v + (1.0 - ADAM_B2) * (g * g)
    m_hat = m / (1.0 - ADAM_B1 ** ADAM_STEP)
    v_hat = v / (1.0 - ADAM_B2 ** ADAM_STEP)
    return -ADAM_LR * (m_hat / (jnp.sqrt(v_hat) + ADAM_EPS) + ADAM_WD * w), m, v


def adam_reduce(pieces, w, m, v, name):
    r, c = w.shape
    n = pieces.shape[0]
    tr = _tile(r, (256, 128, 64, 32, 16, 8))

    def body(p_ref, w_ref, m_ref, v_ref, g_ref, d_ref, nm_ref, nv_ref):
        g = p_ref[0].astype(f32)
        for s in range(1, n):
            g = g + p_ref[s].astype(f32)
        g_ref[...] = g
        d_ref[...], nm_ref[...], nv_ref[...] = _adamw(w_ref[...], g, m_ref[...], v_ref[...])

    blk = pl.BlockSpec((tr, c), lambda i: (i, 0))
    return pl.pallas_call(
        body, name=name, grid=(r // tr,),
        in_specs=[pl.BlockSpec((n, tr, c), lambda i: (0, i, 0)), blk, blk, blk],
        out_specs=[blk] * 4, out_shape=[jax.ShapeDtypeStruct((r, c), f32)] * 4,
        compiler_params=_cparams("parallel"),
    )(pieces, w, m, v)


def adam_w_mod(cond_t, dm, w, m, v):
    nl, d, cols = w.shape
    tr = _tile(d, (512, 256, 128))

    def body(c_ref, dm_ref, w_ref, m_ref, v_ref, g_ref, d_ref, nm_ref, nv_ref):
        g = jnp.dot(c_ref[...], dm_ref[...], preferred_element_type=f32, precision=lax.Precision.HIGHEST)
        g_ref[...] = g
        d_ref[...], nm_ref[...], nv_ref[...] = _adamw(w_ref[...], g, m_ref[...], v_ref[...])

    blk = pl.BlockSpec((None, tr, cols), lambda l, i: (l, i, 0))
    return pl.pallas_call(
        body, name="adam_w_mod", grid=(nl, d // tr),
        in_specs=[pl.BlockSpec((tr, N_DEV), lambda l, i: (i, 0)), pl.BlockSpec((None, N_DEV, cols), lambda l, i: (l, 0, 0)),
                  blk, blk, blk],
        out_specs=[blk] * 4, out_shape=[jax.ShapeDtypeStruct((nl, d, cols), f32)] * 4,
        compiler_params=_cparams("parallel", "parallel"),
    )(cond_t, dm, w, m, v)


def silu_rows(c_all):
    def body(c_ref, o_ref):
        o_ref[...] = _silu(c_ref[...])

    return pl.pallas_call(body, name="silu_rows", out_shape=jax.ShapeDtypeStruct(c_all.shape, f32))(c_all)


def _block_diag(x):
    g, a, b = x.shape
    nj = g // GROUPS_PER_LANE_BLOCK
    eye = jnp.eye(GROUPS_PER_LANE_BLOCK, dtype=x.dtype)
    x5 = x.reshape(nj, GROUPS_PER_LANE_BLOCK, a, b)
    return jnp.einsum("jgab,gh->jgahb", x5, eye).reshape(nj, GROUPS_PER_LANE_BLOCK * a, GROUPS_PER_LANE_BLOCK * b)


def _diag_blocks(x, a, b):
    nj = x.shape[0]
    x5 = x.reshape(nj, GROUPS_PER_LANE_BLOCK, a, GROUPS_PER_LANE_BLOCK, b)
    eye = jnp.eye(GROUPS_PER_LANE_BLOCK, dtype=x.dtype)
    return jnp.einsum("jgahb,gh->jgab", x5, eye).reshape(nj * GROUPS_PER_LANE_BLOCK, a, b)


PACK_ROW = SUBLANES * HEAD


def _pack(parts, row_multiple=SUBLANES):
    rows = []
    for p in parts:
        flat = p.reshape(-1)
        pad = (-flat.shape[0]) % PACK_ROW
        if pad:
            flat = jnp.concatenate([flat, jnp.zeros((pad,), flat.dtype)])
        rows.append(flat.reshape(-1, HEAD))
    pad = (-sum(r.shape[0] for r in rows)) % row_multiple
    if pad:
        rows.append(jnp.zeros((pad, HEAD), rows[0].dtype))
    return jnp.concatenate(rows, axis=0)


def _unpack(packed, shapes):
    out, r0 = [], 0
    for shp in shapes:
        n = math.prod(shp)
        nr = -(-n // PACK_ROW) * SUBLANES
        out.append(packed[r0:r0 + nr].reshape(-1)[:n].reshape(shp))
        r0 += nr
    return out


def adam_small(g, w, m, v):
    r, c = w.shape

    def body(g_ref, w_ref, m_ref, v_ref, d_ref, nm_ref, nv_ref):
        d_ref[...], nm_ref[...], nv_ref[...] = _adamw(w_ref[...], g_ref[...], m_ref[...], v_ref[...])

    tr = max(t for t in range(SUBLANES, 1024 + 1, SUBLANES) if r % t == 0)
    blk = pl.BlockSpec((tr, c), lambda i: (i, 0))
    return pl.pallas_call(
        body, name="adam_small", grid=(r // tr,),
        in_specs=[blk] * 4, out_specs=[blk] * 3, out_shape=[jax.ShapeDtypeStruct((r, c), f32)] * 3,
        compiler_params=_cparams("parallel"),
    )(g, w, m, v)


def kernel(x, c, ln_pre_g, ln_post_g, w_mod, b_mod, w_in_ab, w_out_ab, sgu_norm_g, sgu_w, sgu_b, w_in_ssm, w_out_ssm, lam_re, lam_im, b_re, b_im, c_re, c_im, d_skip, log_dt, w_glu, b_glu, loss_target, m_ln_pre_g, m_ln_post_g, m_w_mod, m_b_mod, m_w_in_ab, m_w_out_ab, m_sgu_norm_g, m_sgu_w, m_sgu_b, m_w_in_ssm, m_w_out_ssm, m_lam_re, m_lam_im, m_b_re, m_b_im, m_c_re, m_c_im, m_d_skip, m_log_dt, m_w_glu, m_b_glu, v_ln_pre_g, v_ln_post_g, v_w_mod, v_b_mod, v_w_in_ab, v_w_out_ab, v_sgu_norm_g, v_sgu_w, v_sgu_b, v_w_in_ssm, v_w_out_ssm, v_lam_re, v_lam_im, v_b_re, v_b_im, v_c_re, v_c_im, v_d_skip, v_log_dt, v_w_glu, v_b_glu):
    me = _my_index()
    x0 = x[0]
    l, d = x0.shape
    target = loss_target[0]
    nh = sgu_w.shape[1]
    wa = nh * HEAD
    n_grp, n_st = lam_re.shape[1], lam_re.shape[2]
    mod_cols = w_mod.shape[2]

    c_all, d_skip_all, b_glu_all = all_gather([c, d_skip, b_glu], "gather_c")
    c_all = c_all.reshape(N_DEV, d)
    d_skip_all = d_skip_all.reshape(1, -1)
    b_glu_all = b_glu_all.reshape(1, -1)

    b_cols = lax.dynamic_slice_in_dim(b_mod, me * mod_cols, mod_cols, axis=1)
    (mod_all,) = all_gather([mod_part(c_all, w_mod, b_cols)], "gather_mod")
    w_in_flight, (mod_all,) = exchange_start(GATHER, [w_in_ab[0].astype(bf16)], [mod_all], "gather_w_in_start")
    mod_mine = lax.dynamic_index_in_dim(mod_all, me, axis=2, keepdims=False)
    mod_rows = jnp.transpose(mod_mine, (1, 0, 2)).reshape(2, 3, 1, d)

    def rows(a, i):
        return a[i].reshape(1, d)

    shift0, scale0, gate0 = mod_rows[0, 0], mod_rows[0, 1], mod_rows[0, 2]
    h0 = prenorm_fwd(x0, rows(ln_pre_g, 0), shift0, scale0, "prenorm0")
    (win_ab3,) = exchange_wait(w_in_flight, h0, "gather_w_in_wait")
    w_in_flight, (win_ab3,) = exchange_start(
        GATHER, [w_out_ab[0].astype(bf16), w_in_ssm[0].astype(bf16), w_out_ssm[0].astype(bf16), w_glu[0].astype(bf16)],
        [win_ab3], "gather_w_rest_start")
    proj0 = mm_nn(h0, win_ab3, bf16, "proj0")
    sgu_b3 = sgu_b[0].reshape(nh, HEAD, 1)
    out_a = sgu_fwd(proj0, sgu_norm_g, sgu_w[0], sgu_b3)
    out_b, att, tot = sb_fwd(proj0, nh)
    cat = jnp.concatenate([out_a, out_b], axis=1)
    wout_ab3, win_ssm3, wout_ssm3, wglu = exchange_wait(w_in_flight, cat, "gather_w_rest_wait")
    wout_ab3 = wout_ab3.reshape(1, d, d)
    win_ssm3 = win_ssm3.reshape(1, d, d)
    wglu = wglu.reshape(w_glu.shape[2], w_glu.shape[2])
    y0 = mm_nn(cat, wout_ab3, f32, "out0")
    x1 = post_fwd(x0, y0, gate0, rows(ln_post_g, 0), "post0")

    shift1, scale1, gate1 = mod_rows[1, 0], mod_rows[1, 1], mod_rows[1, 2]
    h1 = prenorm_fwd(x1, rows(ln_pre_g, 1), shift1, scale1, "prenorm1")
    proj1 = mm_nn(h1, win_ssm3, bf16, "proj1")
    w_ssm = proj1.shape[1] // 2
    ldt = log_dt[0].reshape(n_grp, 1)
    bt_re = jnp.transpose(b_re[0], (0, 2, 1))
    bt_im = jnp.transpose(b_im[0], (0, 2, 1))
    a_re, a_im, bbt_re, bbt_im = s5_params_fwd(lam_re[0], lam_im[0], ldt, bt_re, bt_im)
    bre3 = _block_diag(bbt_re).astype(bf16)
    bim3 = _block_diag(bbt_im).astype(bf16)
    cre3 = _block_diag(jnp.transpose(c_re[0], (0, 2, 1))).astype(bf16)
    cimn3 = _block_diag(-jnp.transpose(c_im[0], (0, 2, 1))).astype(bf16)
    a_re_row, a_im_row = a_re.reshape(1, -1), a_im.reshape(1, -1)
    u = proj1[:, :w_ssm]
    y_ssm, hs_re, hs_im = ssm_fwd(u, bre3, bim3, cre3, cimn3, a_re_row, a_im_row, d_skip_all)
    g_act, t_glu, mix1 = glu_fwd(y_ssm, proj1, wglu, b_glu_all)
    y1 = mm_nn(mix1, wout_ssm3, f32, "out1")

    dx2, loss_tile = final_loss(x1, y1, gate1, rows(ln_post_g, 1), target)
    loss = lax.psum(loss_tile[0, 0] * (0.5 / d), ("x", "y", "c"))

    dy1, dgate1, dgpost1 = post_bwd(dx2, y1, gate1, rows(ln_post_g, 1), "post1_bwd")
    dmix1 = mm_nt(dy1, wout_ssm3, f32, "dmix1")
    gw_out_ssm = mm_tn(mix1, dy1, N_DEV, bf16, "gw_out_ssm")
    g_flight1, (dmix1,) = exchange_start(SCATTER, [gw_out_ssm], [dmix1], "scatter_g1_start")
    dy_ssm, dz1, dt_glu, db_glu = glu_bwd(dmix1, y_ssm, t_glu, proj1, wglu)
    gw_glu = mm_tn(g_act, dt_glu, 1, bf16, "gw_glu").reshape(N_DEV, -1, w_ssm)
    du, dd_skip, da_re, da_im, dbre3, dbim3, dcre3, dcimn3 = ssm_bwd(
        dy_ssm, u, hs_re, hs_im, bre3, bim3, cre3, cimn3, a_re_row, a_im_row, d_skip_all)
    dproj1 = jnp.concatenate([du, dz1], axis=1)
    gw_in_ssm = mm_tn(h1, dproj1, 1, bf16, "gw_in_ssm").reshape(N_DEV, -1, proj1.shape[1])
    g_flight2, (dproj1,) = exchange_start(SCATTER, [gw_in_ssm, gw_glu], [dproj1], "scatter_g2_start")
    dh1 = mm_nt(dproj1, win_ssm3, f32, "dh1")
    dx1, dshift1, dscale1, dgpre1 = prenorm_bwd(dh1, x1, dx2, rows(ln_pre_g, 1), scale1, "prenorm1_bwd")
    dlr, dli, dldt, dbt_re, dbt_im = s5_params_bwd(
        lam_re[0], lam_im[0], ldt, bt_re, bt_im, da_re.reshape(n_grp, n_st), da_im.reshape(n_grp, n_st),
        _diag_blocks(dbre3, SSM_GROUP, n_st), _diag_blocks(dbim3, SSM_GROUP, n_st))
    g_b_re = jnp.transpose(dbt_re, (0, 2, 1))
    g_b_im = jnp.transpose(dbt_im, (0, 2, 1))
    g_c_re = jnp.transpose(_diag_blocks(dcre3, n_st, SSM_GROUP), (0, 2, 1))
    g_c_im = -jnp.transpose(_diag_blocks(dcimn3, n_st, SSM_GROUP), (0, 2, 1))

    dy0, dgate0, dgpost0 = post_bwd(dx1, y0, gate0, rows(ln_post_g, 0), "post0_bwd")
    dcat = mm_nt(dy0, wout_ab3, f32, "dcat")
    gw_out_ab = mm_tn(cat, dy0, 1, bf16, "gw_out_ab").reshape(N_DEV, -1, d)
    g_flight3, (dcat,) = exchange_start(SCATTER, [gw_out_ab], [dcat], "scatter_g3_start")
    da, dsgu_w, dsgu_b, dsgu_ng = sgu_bwd(proj0, dcat, sgu_norm_g, sgu_w[0], sgu_b3)
    dq, dk, dv, dbz = sb_bwd(proj0, dcat, att, tot, nh)
    dproj0 = jnp.concatenate([da, dq, dk, dv, dbz], axis=1)
    gw_in_ab = mm_tn(h0, dproj0, N_DEV, bf16, "gw_in_ab")
    g_flight4, (dproj0,) = exchange_start(SCATTER, [gw_in_ab], [dproj0], "scatter_g4_start")
    dh0 = mm_nt(dproj0, win_ab3, f32, "dh0")
    dx0, dshift0, dscale0, dgpre0 = prenorm_bwd(dh0, x0, dx1, rows(ln_pre_g, 0), scale0, "prenorm0_bwd")

    small_names = ["ln_pre_g", "ln_post_g", "b_mod", "sgu_norm_g", "sgu_w", "sgu_b", "lam_re", "lam_im", "b_re", "b_im",
                   "c_re", "c_im", "log_dt"]
    small_w = [ln_pre_g, ln_post_g, b_mod, sgu_norm_g, sgu_w, sgu_b, lam_re, lam_im, b_re, b_im, c_re, c_im, log_dt]
    small_m = [m_ln_pre_g, m_ln_post_g, m_b_mod, m_sgu_norm_g, m_sgu_w, m_sgu_b, m_lam_re, m_lam_im, m_b_re, m_b_im,
               m_c_re, m_c_im, m_log_dt]
    small_v = [v_ln_pre_g, v_ln_post_g, v_b_mod, v_sgu_norm_g, v_sgu_w, v_sgu_b, v_lam_re, v_lam_im, v_b_re, v_b_im,
               v_c_re, v_c_im, v_log_dt]
    dmod = jnp.concatenate([dshift0, dscale0, dgate0, dshift1, dscale1, dgate1], axis=1)
    small_g = [jnp.concatenate([dgpre0, dgpre1]), jnp.concatenate([dgpost0, dgpost1]), dmod, dsgu_ng, dsgu_w, dsgu_b,
               dlr, dli, g_b_re, g_b_im, g_c_re, g_c_im, dldt]
    shapes = [w.shape for w in small_w]
    g_sum, dmod_all = all_reduce_rows(_pack(small_g + [dd_skip, db_glu], SUBLANES * N_DEV), dmod, "reduce_small_grads")
    n_rows_small = sum(-(-math.prod(s) // PACK_ROW) * SUBLANES for s in shapes)
    new_small = adam_small(g_sum, _pack(small_w), _pack(small_m), _pack(small_v))
    r_small = [_unpack(o, shapes) for o in [g_sum[:n_rows_small]] + list(new_small)]
    small = {n: [r_small[k][i] for k in range(4)] for i, n in enumerate(small_names)}
    vec_rows = d_skip_all.shape[1] // HEAD

    def my_columns(r0):
        whole = g_sum[r0:r0 + vec_rows].reshape(1, 1, -1)
        return lax.dynamic_slice_in_dim(whole, me * d_skip.shape[1], d_skip.shape[1], axis=2)

    def sharded(p, w, m, v, name):
        shp = w.shape
        w2, m2, v2 = (a.reshape(-1, shp[-1]) for a in (w, m, v))
        return [o.reshape(shp) for o in adam_reduce(p.reshape(p.shape[0], -1, shp[-1]), w2, m2, v2, name)]

    r_d_skip = sharded(my_columns(n_rows_small), d_skip, m_d_skip, v_d_skip, "adam_d_skip")
    r_b_glu = sharded(my_columns(n_rows_small + vec_rows), b_glu, m_b_glu, v_b_glu, "adam_b_glu")
    (p_out_ssm,) = exchange_wait(g_flight1, g_sum, "scatter_g1_wait")
    p_in_ssm, p_glu = exchange_wait(g_flight2, g_sum, "scatter_g2_wait")
    (p_out_ab,) = exchange_wait(g_flight3, g_sum, "scatter_g3_wait")
    (p_in_ab,) = exchange_wait(g_flight4, g_sum, "scatter_g4_wait")
    r_w_out_ssm = sharded(p_out_ssm, w_out_ssm, m_w_out_ssm, v_w_out_ssm, "adam_w_out_ssm")
    r_w_in_ssm = sharded(p_in_ssm, w_in_ssm, m_w_in_ssm, v_w_in_ssm, "adam_w_in_ssm")
    r_w_glu = sharded(p_glu, w_glu, m_w_glu, v_w_glu, "adam_w_glu")
    r_w_out_ab = sharded(p_out_ab, w_out_ab, m_w_out_ab, v_w_out_ab, "adam_w_out_ab")
    r_w_in_ab = sharded(p_in_ab, w_in_ab, m_w_in_ab, v_w_in_ab, "adam_w_in_ab")

    dm_cols = jnp.transpose(
        lax.dynamic_slice_in_dim(dmod_all.reshape(N_DEV, 2, 3 * d), me * mod_cols, mod_cols, axis=2), (1, 0, 2))
    cond_t = jnp.transpose(silu_rows(c_all))
    r_w_mod = adam_w_mod(cond_t, dm_cols, w_mod, m_w_mod, v_w_mod)

    res = dict(small)
    res.update(w_mod=r_w_mod, w_in_ab=r_w_in_ab, w_out_ab=r_w_out_ab, w_in_ssm=r_w_in_ssm, w_out_ssm=r_w_out_ssm,
               d_skip=r_d_skip, w_glu=r_w_glu, b_glu=r_b_glu)
    order = ["ln_pre_g", "ln_post_g", "w_mod", "b_mod", "w_in_ab", "w_out_ab", "sgu_norm_g", "sgu_w", "sgu_b", "w_in_ssm",
             "w_out_ssm", "lam_re", "lam_im", "b_re", "b_im", "c_re", "c_im", "d_skip", "log_dt", "w_glu", "b_glu"]
    outs = [loss, dx0.reshape(x.shape)]
    for k in range(4):
        outs += [res[n][k] for n in order]
    return tuple(outs)
```

```python
import functools
import math

import jax
import jax.numpy as jnp
from jax import lax
from jax.experimental import pallas as pl
from jax.experimental.pallas import tpu as pltpu
from jax.experimental.pallas import tpu_sc as plsc

f32 = jnp.float32
bf16 = jnp.bfloat16

N_DEV = 8
EPS = 1e-6
HEAD = 128
SUBLANES = 8
SSM_GROUP = 16
SSM_STATE = 64
GROUPS_PER_LANE_BLOCK = HEAD // SSM_GROUP
STATES_PER_LANE_BLOCK = GROUPS_PER_LANE_BLOCK * SSM_STATE
VMEM_LIMIT = 56 * 2 ** 20
ADAM_LR, ADAM_B1, ADAM_B2, ADAM_EPS, ADAM_WD, ADAM_STEP = 0.001, 0.9, 0.999, 1e-08, 0.01, 10
_GELU_C0 = math.sqrt(2.0 / math.pi)
_GELU_C1 = 0.044715
MESH = pl.DeviceIdType.MESH


def _cparams(*sem):
    return pltpu.CompilerParams(dimension_semantics=sem if sem else None, vmem_limit_bytes=VMEM_LIMIT)


def _gelu(x):
    return 0.5 * x * (1.0 + jnp.tanh(_GELU_C0 * (x + _GELU_C1 * x * x * x)))


def _gelu_grad(x):
    t = jnp.tanh(_GELU_C0 * (x + _GELU_C1 * x * x * x))
    return 0.5 * (1.0 + t) + 0.5 * x * (1.0 - t * t) * _GELU_C0 * (1.0 + 3.0 * _GELU_C1 * x * x)


def _silu(x):
    return x * jax.nn.sigmoid(x)


def _silu_grad(x):
    s = jax.nn.sigmoid(x)
    return s * (1.0 + x * (1.0 - s))


def _dot(a, b):
    return jnp.dot(a, b, preferred_element_type=f32)


def _dot_nt(a, b):
    return lax.dot_general(a, b, (((1,), (1,)), ((), ())), preferred_element_type=f32)


def _dot_tn(a, b):
    return lax.dot_general(a, b, (((0,), (0,)), ((), ())), preferred_element_type=f32)


def _split_bf16(v):
    hi = v.astype(bf16)
    lo = (v - hi.astype(f32)).astype(bf16)
    return hi, lo


def _row(d):
    return pl.BlockSpec((1, d), lambda *_: (0, 0))


def _my_index():
    return 4 * lax.axis_index("x") + 2 * lax.axis_index("y") + lax.axis_index("c")


def _peer(k):
    x, y, c = lax.axis_index("x"), lax.axis_index("y"), lax.axis_index("c")
    return (1 - x if k & 4 else x, 1 - y if k & 2 else y, 1 - c if k & 1 else c)


def all_gather(arrs, name):
    n = len(arrs)

    def body(*refs):
        ins, outs = refs[:n], refs[n:2 * n]
        send, recv, local = refs[2 * n:]
        me = _my_index()
        copies = []
        for a in range(n):
            cp = pltpu.make_async_copy(ins[a], outs[a].at[me], local.at[a])
            cp.start()
            copies.append(cp)
            for k in range(1, N_DEV):
                s = a * (N_DEV - 1) + k - 1
                cp = pltpu.make_async_remote_copy(src_ref=ins[a], dst_ref=outs[a].at[me], send_sem=send.at[s],
                                                  recv_sem=recv.at[s], device_id=_peer(k), device_id_type=MESH)
                cp.start()
                copies.append(cp)
        for cp in copies:
            cp.wait()

    any_spec = pl.BlockSpec(memory_space=pl.ANY)
    outs = pl.pallas_call(
        body, name=name,
        out_shape=[jax.ShapeDtypeStruct((N_DEV,) + a.shape, a.dtype) for a in arrs],
        in_specs=[any_spec] * n, out_specs=[any_spec] * n,
        scratch_shapes=[pltpu.SemaphoreType.DMA((n * (N_DEV - 1),)), pltpu.SemaphoreType.DMA((n * (N_DEV - 1),)),
                        pltpu.SemaphoreType.DMA((n,))],
        compiler_params=pltpu.CompilerParams(has_side_effects=True),
    )(*arrs)
    return list(outs)


def all_reduce_rows(pack, extra, name):
    r, c = pack.shape
    rs = r // N_DEV
    n_peer = N_DEV - 1

    def body(p_ref, x_ref, o_ref, xo_ref, land, red, send1, recv1, send2, recv2, sendx, recvx, local):
        me = _my_index()

        def rows(i):
            return pl.ds(pl.multiple_of(i * rs, SUBLANES), rs)

        own = [pltpu.make_async_copy(p_ref.at[rows(me)], land.at[me], local.at[0]),
               pltpu.make_async_copy(x_ref, xo_ref.at[me], local.at[1])]
        first = []
        for k in range(1, N_DEV):
            first.append(pltpu.make_async_remote_copy(
                src_ref=p_ref.at[rows(jnp.bitwise_xor(me, k))], dst_ref=land.at[me], send_sem=send1.at[k - 1],
                recv_sem=recv1.at[k - 1], device_id=_peer(k), device_id_type=MESH))
            first.append(pltpu.make_async_remote_copy(
                src_ref=x_ref, dst_ref=xo_ref.at[me], send_sem=sendx.at[k - 1], recv_sem=recvx.at[k - 1],
                device_id=_peer(k), device_id_type=MESH))
        for cp in own + first:
            cp.start()
        for cp in own + first:
            cp.wait()
        acc = land[0]
        for s in range(1, N_DEV):
            acc = acc + land[s]
        red[...] = acc
        mine = pltpu.make_async_copy(red, o_ref.at[rows(me)], local.at[2])
        second = [pltpu.make_async_remote_copy(
            src_ref=red, dst_ref=o_ref.at[rows(me)], send_sem=send2.at[k - 1], recv_sem=recv2.at[k - 1],
            device_id=_peer(k), device_id_type=MESH) for k in range(1, N_DEV)]
        for cp in [mine] + second:
            cp.start()
        for cp in [mine] + second:
            cp.wait()

    any_spec = pl.BlockSpec(memory_space=pl.ANY)
    return pl.pallas_call(
        body, name=name,
        out_shape=[jax.ShapeDtypeStruct((r, c), pack.dtype), jax.ShapeDtypeStruct((N_DEV,) + extra.shape, extra.dtype)],
        in_specs=[any_spec, any_spec], out_specs=[any_spec, any_spec],
        scratch_shapes=[pltpu.VMEM((N_DEV, rs, c), pack.dtype), pltpu.VMEM((rs, c), pack.dtype)]
        + [pltpu.SemaphoreType.DMA((n_peer,))] * 6 + [pltpu.SemaphoreType.DMA((3,))],
        compiler_params=pltpu.CompilerParams(has_side_effects=True),
    )(pack, extra)


_HBM = pl.BlockSpec(memory_space=pltpu.HBM)
_SEM = pl.BlockSpec(memory_space=pltpu.SEMAPHORE)
_EFFECT = pltpu.SideEffectType.DATAFLOW_SIDE_EFFECTING
GATHER, SCATTER = "gather", "scatter"


def _in_hbm(a):
    return pltpu.with_memory_space_constraint(a, pltpu.HBM)


def _exchange_copies(kind, srcs, lands, send, recv):
    me = _my_index()
    copies = []
    for a, (src, land) in enumerate(zip(srcs, lands)):
        for k in range(1, N_DEV):
            s = a * (N_DEV - 1) + k - 1
            copies.append(pltpu.make_async_remote_copy(
                src_ref=src if kind == GATHER else src.at[jnp.bitwise_xor(me, k)], dst_ref=land.at[me],
                send_sem=send.at[s], recv_sem=recv.at[s], device_id=_peer(k), device_id_type=MESH))
    return copies


def sequencer_exchange(kind, arrs, name, collective_id):
    n = len(arrs)
    n_sem = n * (N_DEV - 1)
    land_shapes = [((N_DEV,) + a.shape if kind == GATHER else a.shape) for a in arrs]
    srcs = [jax.new_ref(a, memory_space=pltpu.MemorySpace.HBM) for a in arrs]
    lands = [jax.empty_ref(jax.ShapeDtypeStruct(s, a.dtype), memory_space=pltpu.MemorySpace.HBM)
             for s, a in zip(land_shapes, arrs)]

    @pl.kernel(mesh=plsc.ScalarSubcoreMesh(axis_name="sequencer", num_cores=1), name=name,
               scratch_types=(pltpu.SemaphoreType.DMA((n_sem,)), pltpu.SemaphoreType.DMA((n_sem,)),
                              pltpu.SemaphoreType.DMA((n,))),
               compiler_params=pltpu.CompilerParams(collective_id=collective_id))
    def launch(send, recv, local):
        barrier = pltpu.get_barrier_semaphore()
        for k in range(1, N_DEV):
            pl.semaphore_signal(barrier, inc=1, device_id=_peer(k), device_id_type=MESH)
        pl.semaphore_wait(barrier, N_DEV - 1)
        me = _my_index()
        copies = [pltpu.make_async_copy(src if kind == GATHER else src.at[me], land.at[me], local.at[a])
                  for a, (src, land) in enumerate(zip(srcs, lands))]
        copies += _exchange_copies(kind, srcs, lands, send, recv)
        for cp in copies:
            cp.start()
        for cp in copies:
            cp.wait()

    launch()
    return [land[...] for land in lands]


def exchange_start(kind, arrs, thru, name):
    n, nt = len(arrs), len(thru)
    n_sem = n * (N_DEV - 1)
    land_shapes = [((N_DEV,) + a.shape if kind == GATHER else a.shape) for a in arrs]

    def body(*refs):
        srcs, lands = refs[:n], refs[n:2 * n]
        send, recv = refs[2 * n + nt], refs[2 * n + nt + 1]
        local = refs[-1]
        me = _my_index()
        mine = [pltpu.make_async_copy(src if kind == GATHER else src.at[me], land.at[me], local.at[a])
                for a, (src, land) in enumerate(zip(srcs, lands))]
        for cp in mine:
            cp.start()
        for cp in mine:
            cp.wait()
        for cp in _exchange_copies(kind, srcs, lands, send, recv):
            cp.start()

    hbm_out = [pltpu.HBM(a.shape, a.dtype) for a in arrs] + [pltpu.HBM(s, a.dtype) for s, a in zip(land_shapes, arrs)] \
        + [pltpu.HBM(t.shape, t.dtype) for t in thru]
    outs = pl.pallas_call(
        body, name=name,
        out_shape=[pltpu.SemaphoreType.DMA((n_sem,)), pltpu.SemaphoreType.DMA((n_sem,))] + hbm_out,
        in_specs=[_HBM] * (2 * n + nt), out_specs=[_SEM, _SEM] + [_HBM] * (2 * n + nt),
        input_output_aliases={i: 2 + i for i in range(2 * n + nt)},
        scratch_shapes=[pltpu.SemaphoreType.DMA((n,))],
        compiler_params=pltpu.CompilerParams(has_side_effects=_EFFECT),
    )(*[_in_hbm(a) for a in arrs], *[_in_hbm(lax.empty(s, a.dtype)) for s, a in zip(land_shapes, arrs)],
      *[_in_hbm(t) for t in thru])
    return (kind, outs[0], outs[1], outs[2:2 + n], outs[2 + n:2 + 2 * n]), list(outs[2 + 2 * n:])


def exchange_wait(handle, after, name):
    kind, send, recv, srcs, lands = handle
    n = len(srcs)

    def body(*refs):
        for cp in _exchange_copies(kind, refs[:n], refs[n:2 * n], refs[2 * n], refs[2 * n + 1]):
            cp.wait_send()
            cp.wait_recv()

    outs = pl.pallas_call(
        body, name=name,
        out_shape=[pltpu.HBM(a.shape, a.dtype) for a in list(srcs) + list(lands)],
        in_specs=[_HBM] * (2 * n) + [_SEM, _SEM, pl.BlockSpec(memory_space=pl.ANY)], out_specs=[_HBM] * (2 * n),
        input_output_aliases={i: i for i in range(2 * n)},
        compiler_params=pltpu.CompilerParams(has_side_effects=_EFFECT),
    )(*srcs, *lands, send, recv, after)
    return list(outs[n:])


def _tile(n, pref):
    for t in pref:
        if n % t == 0:
            return t
    return n


def mm_nn(a, b3, out_dtype, name):
    m, k = a.shape
    nb, _, bn = b3.shape
    tm = _tile(m, (512, 256, 128))
    tn = _tile(bn, (1024, 896, 512, 256, 128))
    per = bn // tn

    def body(a_ref, b_ref, o_ref):
        o_ref[...] = _dot(a_ref[...], b_ref[...]).astype(o_ref.dtype)

    return pl.pallas_call(
        body, name=name, grid=(m // tm, nb, per),
        in_specs=[pl.BlockSpec((tm, k), lambda i, j, jj: (i, 0)),
                  pl.BlockSpec((None, k, tn), lambda i, j, jj: (j, 0, jj))],
        out_specs=pl.BlockSpec((tm, tn), lambda i, j, jj: (i, j * per + jj)),
        out_shape=jax.ShapeDtypeStruct((m, nb * bn), out_dtype),
        compiler_params=_cparams("parallel", "arbitrary", "arbitrary"),
    )(a, b3)


def mm_nt(a, w3, out_dtype, name):
    m, _ = a.shape
    nb, ko, bn = w3.shape
    tm = _tile(m, (512, 256, 128))
    tko = _tile(ko, (1024, 512, 256, 128))

    def body(a_ref, w_ref, o_ref, acc_ref):
        j = pl.program_id(2)

        @pl.when(j == 0)
        def _():
            acc_ref[...] = jnp.zeros_like(acc_ref)

        acc_ref[...] += _dot_nt(a_ref[...], w_ref[...])

        @pl.when(j == nb - 1)
        def _():
            o_ref[...] = acc_ref[...].astype(o_ref.dtype)

    return pl.pallas_call(
        body, name=name, grid=(m // tm, ko // tko, nb),
        in_specs=[pl.BlockSpec((tm, bn), lambda i, o, j: (i, j)),
                  pl.BlockSpec((None, tko, bn), lambda i, o, j: (j, o, 0))],
        out_specs=pl.BlockSpec((tm, tko), lambda i, o, j: (i, o)),
        out_shape=jax.ShapeDtypeStruct((m, ko), out_dtype),
        scratch_shapes=[pltpu.VMEM((tm, tko), f32)],
        compiler_params=_cparams("parallel", "arbitrary", "arbitrary"),
    )(a, w3)


def mm_tn(a, dy, ncb, out_dtype, name):
    l, ka = a.shape
    _, n = dy.shape
    bn = n // ncb
    tl = _tile(l, (1024, 512, 256, 128))
    tka = _tile(ka, (512, 256, 128))
    tn = _tile(bn, (1024, 896, 512, 256, 128))
    per = bn // tn
    nl = l // tl

    def body(a_ref, dy_ref, o_ref, acc_ref):
        s = pl.program_id(2)

        @pl.when(s == 0)
        def _():
            acc_ref[...] = jnp.zeros_like(acc_ref)

        acc_ref[...] += _dot_tn(a_ref[...], dy_ref[...])

        @pl.when(s == nl - 1)
        def _():
            o_ref[...] = acc_ref[...].astype(o_ref.dtype)

    return pl.pallas_call(
        body, name=name, grid=(ka // tka, n // tn, nl),
        in_specs=[pl.BlockSpec((tl, tka), lambda i, j, s: (s, i)),
                  pl.BlockSpec((tl, tn), lambda i, j, s: (s, j))],
        out_specs=pl.BlockSpec((None, tka, tn), lambda i, j, s: (j // per, i, j % per)),
        out_shape=jax.ShapeDtypeStruct((ncb, ka, bn), out_dtype),
        scratch_shapes=[pltpu.VMEM((tka, tn), f32)],
        compiler_params=_cparams("parallel", "parallel", "arbitrary"),
    )(a, dy)


def mod_part(c_all, w_mod, b_cols):
    nl, d, cols = w_mod.shape

    def body(c_ref, w_ref, b_ref, o_ref):
        cond = _silu(c_ref[...]).astype(bf16)
        o_ref[...] = _dot(cond, w_ref[...].astype(bf16)) + b_ref[...]

    return pl.pallas_call(
        body, name="mod_part", grid=(nl,),
        in_specs=[pl.BlockSpec((N_DEV, d), lambda l: (0, 0)),
                  pl.BlockSpec((None, d, cols), lambda l: (l, 0, 0)),
                  pl.BlockSpec((None, 1, cols), lambda l: (l, 0, 0))],
        out_specs=pl.BlockSpec((None, N_DEV, cols), lambda l: (l, 0, 0)),
        out_shape=jax.ShapeDtypeStruct((nl, N_DEV, cols), f32),
        compiler_params=_cparams("arbitrary"),
    )(c_all, w_mod, b_cols.reshape(nl, 1, cols))


def _row_tile(l):
    return _tile(l, (256, 128))


def prenorm_fwd(x, g, shift, scale, name):
    l, d = x.shape
    tm = _row_tile(l)

    def body(x_ref, g_ref, sh_ref, sc_ref, h_ref):
        xv = x_ref[...]
        r = lax.rsqrt(jnp.mean(xv * xv, axis=-1, keepdims=True) + EPS)
        h_ref[...] = (xv * r * (g_ref[...] * (1.0 + sc_ref[...])) + sh_ref[...]).astype(h_ref.dtype)

    return pl.pallas_call(
        body, name=name, grid=(l // tm,),
        in_specs=[pl.BlockSpec((tm, d), lambda i: (i, 0)), _row(d), _row(d), _row(d)],
        out_specs=pl.BlockSpec((tm, d), lambda i: (i, 0)),
        out_shape=jax.ShapeDtypeStruct((l, d), bf16),
        compiler_params=_cparams("parallel"),
    )(x, g, shift, scale)


def post_fwd(x, y, gate, g, name):
    l, d = x.shape
    tm = _row_tile(l)

    def body(x_ref, y_ref, gate_ref, g_ref, o_ref):
        yv = y_ref[...]
        r = lax.rsqrt(jnp.mean(yv * yv, axis=-1, keepdims=True) + EPS)
        o_ref[...] = x_ref[...] + gate_ref[...] * (yv * r * g_ref[...])

    blk = pl.BlockSpec((tm, d), lambda i: (i, 0))
    return pl.pallas_call(
        body, name=name, grid=(l // tm,),
        in_specs=[blk, blk, _row(d), _row(d)], out_specs=blk,
        out_shape=jax.ShapeDtypeStruct((l, d), f32),
        compiler_params=_cparams("parallel"),
    )(x, y, gate, g)


def final_loss(x, y, gate, g, target):
    l, d = x.shape
    tm = _row_tile(l)

    def body(x_ref, y_ref, gate_ref, g_ref, t_ref, dx_ref, loss_ref):
        @pl.when(pl.program_id(0) == 0)
        def _():
            loss_ref[...] = jnp.zeros_like(loss_ref)

        yv = y_ref[...]
        r = lax.rsqrt(jnp.mean(yv * yv, axis=-1, keepdims=True) + EPS)
        diff = x_ref[...] + gate_ref[...] * (yv * r * g_ref[...]) - t_ref[...]
        dx_ref[...] = diff * (1.0 / d)
        loss_ref[...] += jnp.sum(diff * diff)

    blk = pl.BlockSpec((tm, d), lambda i: (i, 0))
    return pl.pallas_call(
        body, name="final_loss", grid=(l // tm,),
        in_specs=[blk, blk, _row(d), _row(d), blk],
        out_specs=[blk, pl.BlockSpec((SUBLANES, HEAD), lambda i: (0, 0))],
        out_shape=[jax.ShapeDtypeStruct((l, d), f32), jax.ShapeDtypeStruct((SUBLANES, HEAD), f32)],
        compiler_params=_cparams("arbitrary"),
    )(x, y, gate, g, target)


def post_bwd(dx, y, gate, g, name):
    l, d = dx.shape
    tm = _row_tile(l)

    def body(dx_ref, y_ref, gate_ref, g_ref, dy_ref, dgate_ref, dg_ref):
        @pl.when(pl.program_id(0) == 0)
        def _():
            dgate_ref[...] = jnp.zeros_like(dgate_ref)
            dg_ref[...] = jnp.zeros_like(dg_ref)

        yv, dxv, gv = y_ref[...], dx_ref[...], g_ref[...]
        r = lax.rsqrt(jnp.mean(yv * yv, axis=-1, keepdims=True) + EPS)
        yn = yv * r
        dgate_ref[...] += jnp.sum(dxv * yn * gv, axis=0, keepdims=True)
        dyg = dxv * gate_ref[...]
        dg_ref[...] += jnp.sum(dyg * yn, axis=0, keepdims=True)
        dyn = dyg * gv
        dy_ref[...] = (r * (dyn - yn * jnp.mean(dyn * yn, axis=-1, keepdims=True))).astype(dy_ref.dtype)

    blk = pl.BlockSpec((tm, d), lambda i: (i, 0))
    return pl.pallas_call(
        body, name=name, grid=(l // tm,),
        in_specs=[blk, blk, _row(d), _row(d)], out_specs=[blk, _row(d), _row(d)],
        out_shape=[jax.ShapeDtypeStruct((l, d), bf16), jax.ShapeDtypeStruct((1, d), f32),
                   jax.ShapeDtypeStruct((1, d), f32)],
        compiler_params=_cparams("arbitrary"),
    )(dx, y, gate, g)


def prenorm_bwd(dh, x, dx_next, g, scale, name):
    l, d = x.shape
    tm = _row_tile(l)

    def body(dh_ref, x_ref, dxn_ref, g_ref, sc_ref, dx_ref, dsh_ref, dsc_ref, dg_ref):
        @pl.when(pl.program_id(0) == 0)
        def _():
            dsh_ref[...] = jnp.zeros_like(dsh_ref)
            dsc_ref[...] = jnp.zeros_like(dsc_ref)
            dg_ref[...] = jnp.zeros_like(dg_ref)

        xv, dhv, gv, sc1 = x_ref[...], dh_ref[...], g_ref[...], 1.0 + sc_ref[...]
        r = lax.rsqrt(jnp.mean(xv * xv, axis=-1, keepdims=True) + EPS)
        xn = xv * r
        dhx = dhv * xn
        dsh_ref[...] += jnp.sum(dhv, axis=0, keepdims=True)
        dsc_ref[...] += jnp.sum(dhx * gv, axis=0, keepdims=True)
        dg_ref[...] += jnp.sum(dhx * sc1, axis=0, keepdims=True)
        dxn = dhv * (gv * sc1)
        dx_ref[...] = dxn_ref[...] + r * (dxn - xn * jnp.mean(dxn * xn, axis=-1, keepdims=True))

    blk = pl.BlockSpec((tm, d), lambda i: (i, 0))
    return pl.pallas_call(
        body, name=name, grid=(l // tm,),
        in_specs=[blk, blk, blk, _row(d), _row(d)], out_specs=[blk, _row(d), _row(d), _row(d)],
        out_shape=[jax.ShapeDtypeStruct((l, d), f32)] + [jax.ShapeDtypeStruct((1, d), f32)] * 3,
        compiler_params=_cparams("arbitrary"),
    )(dh, x, dx_next, g, scale)


def _tril_mask():
    r = lax.broadcasted_iota(jnp.int32, (HEAD, HEAD), 0)
    c = lax.broadcasted_iota(jnp.int32, (HEAD, HEAD), 1)
    return r >= c


def sgu_fwd(proj, norm_g, w_s, b_s):
    l = proj.shape[0]
    nh = w_s.shape[0]
    wa = nh * HEAD

    def body(au_ref, av_ref, az_ref, ng_ref, w_ref, b_ref, o_ref):
        tril = _tril_mask()
        for h in range(nh):
            sl = slice(h * HEAD, (h + 1) * HEAD)
            gv = _gelu(av_ref[:, sl].astype(f32))
            r = lax.rsqrt(jnp.mean(gv * gv, axis=-1, keepdims=True) + EPS)
            vh = gv * r * ng_ref[:, sl]
            wm = jnp.where(tril, w_ref[h], 0.0).astype(bf16)
            s = _dot(wm, vh.astype(bf16)) + b_ref[h]
            o_ref[:, sl] = (_gelu(au_ref[:, sl].astype(f32)) * s * _silu(az_ref[:, sl].astype(f32))).astype(o_ref.dtype)

    def col(j):
        return pl.BlockSpec((HEAD, wa), lambda n: (n, j))

    return pl.pallas_call(
        body, name="sgu_fwd", grid=(l // HEAD,),
        in_specs=[col(0), col(1), col(2), _row(wa),
                  pl.BlockSpec((nh, HEAD, HEAD), lambda n: (0, 0, 0)), pl.BlockSpec((nh, HEAD, 1), lambda n: (0, 0, 0))],
        out_specs=pl.BlockSpec((HEAD, wa), lambda n: (n, 0)),
        out_shape=jax.ShapeDtypeStruct((l, wa), bf16),
        compiler_params=_cparams("parallel"),
    )(proj, proj, proj, norm_g, w_s, b_s)


def sgu_bwd(proj, dcat, norm_g, w_s, b_s):
    l = proj.shape[0]
    nh = w_s.shape[0]
    wa = nh * HEAD

    def body(au_ref, av_ref, az_ref, do_ref, ng_ref, w_ref, b_ref, da_ref, dw_ref, db_ref, dng_ref):
        @pl.when(pl.program_id(0) == 0)
        def _():
            dw_ref[...] = jnp.zeros_like(dw_ref)
            db_ref[...] = jnp.zeros_like(db_ref)
            dng_ref[...] = jnp.zeros_like(dng_ref)

        tril = _tril_mask()
        for h in range(nh):
            sl = slice(h * HEAD, (h + 1) * HEAD)
            au, av, az = au_ref[:, sl].astype(f32), av_ref[:, sl].astype(f32), az_ref[:, sl].astype(f32)
            ng = ng_ref[:, sl]
            gv = _gelu(av)
            r = lax.rsqrt(jnp.mean(gv * gv, axis=-1, keepdims=True) + EPS)
            gvn = gv * r
            vh = (gvn * ng).astype(bf16)
            wm = jnp.where(tril, w_ref[h], 0.0).astype(bf16)
            s = _dot(wm, vh) + b_ref[h]
            gu, sz = _gelu(au), _silu(az)
            dov = do_ref[:, sl].astype(f32)
            da_ref[:, sl] = (dov * s * sz * _gelu_grad(au)).astype(da_ref.dtype)
            da_ref[:, 2 * wa + h * HEAD:2 * wa + (h + 1) * HEAD] = (dov * gu * s * _silu_grad(az)).astype(da_ref.dtype)
            ds = dov * gu * sz
            db_ref[h] += jnp.sum(ds, axis=-1, keepdims=True)
            dsb = ds.astype(bf16)
            dw_ref[h] += jnp.where(tril, _dot_nt(dsb, vh), 0.0)
            dvh = _dot_tn(wm, dsb)
            dng_ref[:, sl] += jnp.sum(dvh * gvn, axis=0, keepdims=True)
            dgvn = dvh * ng
            dgv = r * (dgvn - gvn * jnp.mean(dgvn * gvn, axis=-1, keepdims=True))
            da_ref[:, wa + h * HEAD:wa + (h + 1) * HEAD] = (dgv * _gelu_grad(av)).astype(da_ref.dtype)

    def col(j):
        return pl.BlockSpec((HEAD, wa), lambda n: (n, j))

    whole_w = pl.BlockSpec((nh, HEAD, HEAD), lambda n: (0, 0, 0))
    whole_b = pl.BlockSpec((nh, HEAD, 1), lambda n: (0, 0, 0))
    return pl.pallas_call(
        body, name="sgu_bwd", grid=(l // HEAD,),
        in_specs=[col(0), col(1), col(2), col(0), _row(wa), whole_w, whole_b],
        out_specs=[pl.BlockSpec((HEAD, 3 * wa), lambda n: (n, 0)), whole_w, whole_b, _row(wa)],
        out_shape=[jax.ShapeDtypeStruct((l, 3 * wa), bf16), jax.ShapeDtypeStruct((nh, HEAD, HEAD), f32),
                   jax.ShapeDtypeStruct((nh, HEAD, 1), f32), jax.ShapeDtypeStruct((1, wa), f32)],
        compiler_params=_cparams("arbitrary"),
    )(proj, proj, proj, dcat, norm_g, w_s, b_s)


_LOG2E = 1.0 / math.log(2.0)


def _sb_scores(q, k, scale):
    z = _dot_nt(q, k) * (scale * _LOG2E)
    return z, jnp.maximum(z, 0.0) + jnp.log2(1.0 + jnp.exp2(-jnp.abs(z)))


def _sb_sum_matrix(tri):
    s = lax.broadcasted_iota(jnp.int32, (2 * HEAD, 2 * HEAD), 0) % HEAD
    j = lax.broadcasted_iota(jnp.int32, (2 * HEAD, 2 * HEAD), 1)
    return jnp.where(jnp.logical_or(j >= HEAD, tri(s, j)), 1.0, 0.0).astype(bf16)


def _sb_sums(x, sums):
    c2 = _dot(jnp.concatenate(_split_bf16(x), axis=1), sums)
    return c2[:, :HEAD], c2[:, HEAD:]


def _sb_q_tile(l, most=512):
    return _tile(l, tuple(t for t in (1024, 512, 256, 128) if t <= most))


def _sb_heads_per_step(nh, most):
    return _tile(nh, tuple(h for h in (4, 2) if h <= most))


def sb_fwd(proj, nh):
    l = proj.shape[0]
    wb = nh * HEAD
    tq = _sb_q_tile(l, 1024)
    band = tq // HEAD
    hp = _sb_heads_per_step(nh, 2)
    scale = 1.0 / math.sqrt(HEAD)
    qc, kc, vc, zc = 3 * nh, 4 * nh, 5 * nh, 6 * nh

    def body(q_ref, k_ref, v_ref, bz_ref, o_ref, att_ref, tot_ref):
        i = pl.program_id(1)
        sums = _sb_sum_matrix(lambda s, j: s > j)
        t_pos = i * tq + lax.broadcasted_iota(jnp.int32, (tq, HEAD), 0)
        s_off = lax.broadcasted_iota(jnp.int32, (tq, HEAD), 1)

        def step(j, carry, masked):
            rows = pl.ds(pl.multiple_of(j * HEAD, HEAD), HEAD)
            out = []
            for e in range(hp):
                acc, tot = carry[e]
                sl = slice(e * HEAD, (e + 1) * HEAD)
                z, sp = _sb_scores(q_ref[:, sl], k_ref[rows, sl], scale)
                lb = z - sp
                if masked:
                    mask = s_off + j * HEAD < t_pos
                    sp = jnp.where(mask, sp, 0.0)
                later, total = _sb_sums(sp, sums)
                w = jnp.exp2(lb + tot - later)
                if masked:
                    w = jnp.where(mask, w, 0.0)
                out.append((acc + _dot(w.astype(bf16), v_ref[rows, sl]), tot - total))
            return tuple(out)

        zero = jnp.zeros((tq, HEAD), f32)
        carry = lax.fori_loop(0, band, lambda t, c: step(band * i + band - 1 - t, c, True), ((zero, zero),) * hp)
        carry = lax.fori_loop(0, band * i, lambda t, c: step(band * i - 1 - t, c, False), carry)
        for e in range(hp):
            acc, tot = carry[e]
            sl = slice(e * HEAD, (e + 1) * HEAD)
            att_ref[:, sl] = acc.astype(att_ref.dtype)
            o_ref[:, sl] = (acc * _silu(bz_ref[:, sl].astype(f32))).astype(o_ref.dtype)
            tot_ref[e] = tot[:, :1]

    blk = lambda c0: pl.BlockSpec((tq, hp * HEAD), lambda g, i: (i, c0 // hp + g))
    head = lambda c0: pl.BlockSpec((l, hp * HEAD), lambda g, i: (0, c0 // hp + g))
    return pl.pallas_call(
        body, name="sb_fwd", grid=(nh // hp, l // tq),
        in_specs=[blk(qc), head(kc), head(vc), blk(zc)],
        out_specs=[blk(0), blk(0), pl.BlockSpec((hp, tq, 1), lambda g, i: (g, i, 0))],
        out_shape=[jax.ShapeDtypeStruct((l, wb), bf16), jax.ShapeDtypeStruct((l, wb), bf16),
                   jax.ShapeDtypeStruct((nh, l, 1), f32)],
        compiler_params=_cparams("parallel", "arbitrary"),
    )(proj, proj, proj, proj)


def sb_bwd(proj, dcat, att, tot, nh):
    l = proj.shape[0]
    wb = nh * HEAD
    tq = _sb_q_tile(l, 1024)
    band = tq // HEAD
    nq = l // tq
    hp = _sb_heads_per_step(nh, 2)
    scale = 1.0 / math.sqrt(HEAD)
    qc, kc, vc, zc = 3 * nh, 4 * nh, 5 * nh, 6 * nh

    def body(q_ref, k_ref, v_ref, bz_ref, do_ref, att_ref, tot_ref, dq_ref, dk_ref, dv_ref, dbz_ref, dk_acc, dv_acc,
             dob_ref):
        i = pl.program_id(1)

        @pl.when(i == 0)
        def _():
            dk_acc[...] = jnp.zeros_like(dk_acc)
            dv_acc[...] = jnp.zeros_like(dv_acc)

        bz = bz_ref[...].astype(f32)
        dov = do_ref[...].astype(f32)
        dbz_ref[...] = (dov * att_ref[...].astype(f32) * _silu_grad(bz)).astype(dbz_ref.dtype)
        dob_ref[...] = (dov * _silu(bz)).astype(bf16)
        upto = _sb_sum_matrix(lambda s, j: s <= j)
        before = _sb_sum_matrix(lambda j, s: j < s)
        t_pos = i * tq + lax.broadcasted_iota(jnp.int32, (tq, HEAD), 0)
        s_off = lax.broadcasted_iota(jnp.int32, (tq, HEAD), 1)

        def step(j, carry, masked):
            rows = pl.ds(pl.multiple_of(j * HEAD, HEAD), HEAD)
            out = []
            for h in range(hp):
                dq, sp_seen, e_seen = carry[h]
                sl = slice(h * HEAD, (h + 1) * HEAD)
                q, kj, vj, dob = q_ref[:, sl], k_ref[rows, sl], v_ref[rows, sl], dob_ref[:, sl]
                z, sp = _sb_scores(q, kj, scale)
                lb = z - sp
                if masked:
                    mask = s_off + j * HEAD < t_pos
                    sp = jnp.where(mask, sp, 0.0)
                sp_upto, sp_total = _sb_sums(sp, upto)
                w = jnp.exp2(lb + sp_seen + sp_upto)
                if masked:
                    w = jnp.where(mask, w, 0.0)
                dv_acc[rows, sl] += _dot_tn(w.astype(bf16), dob)
                e = _dot_nt(dob, vj) * w
                e_before, e_total = _sb_sums(e, before)
                dz = (e - (e + e_seen + e_before) * jnp.exp2(lb)) * scale
                if masked:
                    dz = jnp.where(mask, dz, 0.0)
                dz = dz.astype(bf16)
                dk_acc[rows, sl] += _dot_tn(dz, q)
                out.append((dq + _dot(dz, kj), sp_seen + sp_total, e_seen + e_total))
            return tuple(out)

        zero = jnp.zeros((tq, HEAD), f32)
        init = tuple((zero, jnp.broadcast_to(tot_ref[h], (tq, HEAD)), zero) for h in range(hp))
        carry = lax.fori_loop(0, band * i, lambda j, c: step(j, c, False), init)
        carry = lax.fori_loop(0, band, lambda t, c: step(band * i + t, c, True), carry)
        for h in range(hp):
            dq_ref[:, h * HEAD:(h + 1) * HEAD] = carry[h][0].astype(dq_ref.dtype)

        @pl.when(i == nq - 1)
        def _():
            dk_ref[...] = dk_acc[...].astype(dk_ref.dtype)
            dv_ref[...] = dv_acc[...].astype(dv_ref.dtype)

    blk = lambda c0: pl.BlockSpec((tq, hp * HEAD), lambda g, i: (i, c0 // hp + g))
    head = lambda c0: pl.BlockSpec((l, hp * HEAD), lambda g, i: (0, c0 // hp + g))
    return pl.pallas_call(
        body, name="sb_bwd", grid=(nh // hp, nq),
        in_specs=[blk(qc), head(kc), head(vc), blk(zc), blk(nh), blk(0),
                  pl.BlockSpec((hp, tq, 1), lambda g, i: (g, i, 0))],
        out_specs=[blk(0), head(0), head(0), blk(0)],
        out_shape=[jax.ShapeDtypeStruct((l, wb), bf16)] * 4,
        scratch_shapes=[pltpu.VMEM((l, hp * HEAD), f32), pltpu.VMEM((l, hp * HEAD), f32),
                        pltpu.VMEM((tq, hp * HEAD), bf16)],
        compiler_params=_cparams("parallel", "arbitrary"),
    )(proj, proj, proj, proj, dcat, att, tot)


def _disc(lr, li, ldt):
    dt = jnp.exp(ldt)
    mag = jnp.exp(lr * dt)
    a_re = mag * jnp.cos(li * dt)
    a_im = mag * jnp.sin(li * dt)
    den = lr * lr + li * li
    nr = a_re - 1.0
    return a_re, a_im, (nr * lr + a_im * li) / den, (a_im * lr - nr * li) / den


def s5_params_fwd(lr, li, ldt, bt_re, bt_im):
    g, c, p = bt_re.shape

    def body(lr_ref, li_ref, ldt_ref, br_ref, bi_ref, ar_ref, ai_ref, bbr_ref, bbi_ref):
        a_re, a_im, cr, ci = _disc(lr_ref[...], li_ref[...], ldt_ref[...])
        ar_ref[...] = a_re
        ai_ref[...] = a_im
        for k in range(c):
            br, bi = br_ref[:, k, :], bi_ref[:, k, :]
            bbr_ref[:, k, :] = cr * br - ci * bi
            bbi_ref[:, k, :] = cr * bi + ci * br

    return pl.pallas_call(
        body, name="s5_params_fwd",
        out_shape=[jax.ShapeDtypeStruct((g, p), f32)] * 2 + [jax.ShapeDtypeStruct((g, c, p), f32)] * 2,
    )(lr, li, ldt, bt_re, bt_im)


def s5_params_bwd(lr, li, ldt, bt_re, bt_im, da_re, da_im, dbbt_re, dbbt_im):
    g, c, p = bt_re.shape

    def body(lr_ref, li_ref, ldt_ref, br_ref, bi_ref, dar_ref, dai_ref, dbbr_ref, dbbi_ref,
             dlr_ref, dli_ref, dldt_ref, dbr_ref, dbi_ref):
        (a_re, a_im, cr, ci), vjp = jax.vjp(_disc, lr_ref[...], li_ref[...], ldt_ref[...])
        dcr = jnp.zeros((g, p), f32)
        dci = jnp.zeros((g, p), f32)
        for k in range(c):
            br, bi = br_ref[:, k, :], bi_ref[:, k, :]
            dr, di = dbbr_ref[:, k, :], dbbi_ref[:, k, :]
            dcr += dr * br + di * bi
            dci += di * br - dr * bi
            dbr_ref[:, k, :] = cr * dr + ci * di
            dbi_ref[:, k, :] = cr * di - ci * dr
        dlr, dli, dldt = vjp((dar_ref[...], dai_ref[...], dcr, dci))
        dlr_ref[...] = dlr
        dli_ref[...] = dli
        dldt_ref[...] = dldt

    return pl.pallas_call(
        body, name="s5_params_bwd",
        out_shape=[jax.ShapeDtypeStruct((g, p), f32)] * 2 + [jax.ShapeDtypeStruct((g, 1), f32)]
        + [jax.ShapeDtypeStruct((g, c, p), f32)] * 2,
    )(lr, li, ldt, bt_re, bt_im, da_re, da_im, dbbt_re, dbbt_im)


def _cmul(ar, ai, br, bi):
    return ar * br - ai * bi, ar * bi + ai * br


def _power_tables(ar, ai):
    rows = lax.broadcasted_iota(jnp.int32, (SUBLANES, ar.shape[1]), 0)
    pr = jnp.zeros((SUBLANES, ar.shape[1]), f32)
    pi = jnp.zeros((SUBLANES, ar.shape[1]), f32)
    cr, ci = ar, ai
    pows = {}
    for r in range(SUBLANES):
        pows[r + 1] = (cr, ci)
        pr = jnp.where(rows == r, cr, pr)
        pi = jnp.where(rows == r, ci, pi)
        cr, ci = _cmul(cr, ci, ar, ai)
    return [pows[1], pows[2], pows[4]], pr, pi


def _ssm_time_tile(l):
    return _tile(l, (512, 256, 128))


def ssm_fwd(u, bre3, bim3, cre3, cimn3, a_re, a_im, d_skip):
    l, w = u.shape
    nj = w // HEAD
    ns = STATES_PER_LANE_BLOCK
    tt = _ssm_time_tile(l)

    def body(u_ref, bre_ref, bim_ref, cre_ref, cim_ref, ar_ref, ai_ref, d_ref, y_ref, hr_ref, hi_ref, cr_ref, ci_ref):
        @pl.when(pl.program_id(1) == 0)
        def _():
            cr_ref[...] = jnp.zeros_like(cr_ref)
            ci_ref[...] = jnp.zeros_like(ci_ref)

        uv = u_ref[...]
        hr_ref[...] = _dot(uv, bre_ref[...])
        hi_ref[...] = _dot(uv, bim_ref[...])
        steps, pr, pi = _power_tables(ar_ref[...], ai_ref[...])
        rows = lax.broadcasted_iota(jnp.int32, (SUBLANES, ns), 0)

        def blk(b, carry):
            cr, ci = carry
            sl = pl.ds(pl.multiple_of(b * SUBLANES, SUBLANES), SUBLANES)
            xr, xi = hr_ref[sl, :], hi_ref[sl, :]
            for d, (sr_, si_) in zip((1, 2, 4), steps):
                keep = rows >= d
                qr = jnp.where(keep, pltpu.roll(xr, d, axis=0), 0.0)
                qi = jnp.where(keep, pltpu.roll(xi, d, axis=0), 0.0)
                mr, mi = _cmul(sr_, si_, qr, qi)
                xr, xi = xr + mr, xi + mi
            mr, mi = _cmul(pr, pi, cr, ci)
            xr, xi = xr + mr, xi + mi
            hr_ref[sl, :] = xr
            hi_ref[sl, :] = xi
            return xr[SUBLANES - 1:, :], xi[SUBLANES - 1:, :]

        cr, ci = lax.fori_loop(0, tt // SUBLANES, blk, (cr_ref[...], ci_ref[...]))
        cr_ref[...] = cr
        ci_ref[...] = ci
        y = _dot(hr_ref[...].astype(bf16), cre_ref[...]) + _dot(hi_ref[...].astype(bf16), cim_ref[...])
        y_ref[...] = y + d_ref[...] * uv.astype(f32)

    lane = pl.BlockSpec((tt, HEAD), lambda j, i: (i, j))
    st = pl.BlockSpec((tt, ns), lambda j, i: (i, j))
    b3 = pl.BlockSpec((None, HEAD, ns), lambda j, i: (j, 0, 0))
    c3 = pl.BlockSpec((None, ns, HEAD), lambda j, i: (j, 0, 0))
    arow = pl.BlockSpec((1, ns), lambda j, i: (0, j))
    return pl.pallas_call(
        body, name="ssm_fwd", grid=(nj, l // tt),
        in_specs=[lane, b3, b3, c3, c3, arow, arow, pl.BlockSpec((1, HEAD), lambda j, i: (0, j))],
        out_specs=[lane, st, st],
        out_shape=[jax.ShapeDtypeStruct((l, w), f32), jax.ShapeDtypeStruct((l, nj * ns), f32),
                   jax.ShapeDtypeStruct((l, nj * ns), f32)],
        scratch_shapes=[pltpu.VMEM((1, ns), f32), pltpu.VMEM((1, ns), f32)],
        compiler_params=_cparams("parallel", "arbitrary"),
    )(u, bre3, bim3, cre3, cimn3, a_re, a_im, d_skip)


def ssm_bwd(dy, u, h_re, h_im, bre3, bim3, cre3, cimn3, a_re, a_im, d_skip):
    l, w = u.shape
    nj = w // HEAD
    ns = STATES_PER_LANE_BLOCK
    tt = _ssm_time_tile(l)
    nt = l // tt

    def body(dy_ref, u_ref, hr_ref, hi_ref, bre_ref, bim_ref, cre_ref, cim_ref, ar_ref, ai_ref, d_ref,
             du_ref, dd_ref, dar_ref, dai_ref, dbre_ref, dbim_ref, dcre_ref, dcim_ref, kr_ref, ki_ref, cr_ref, ci_ref,
             accr_ref, acci_ref):
        i = pl.program_id(1)

        @pl.when(i == 0)
        def _():
            for ref in (cr_ref, ci_ref, accr_ref, acci_ref, dd_ref, dbre_ref, dbim_ref, dcre_ref, dcim_ref):
                ref[...] = jnp.zeros_like(ref)

        dyv = dy_ref[...]
        dyb = dyv.astype(bf16)
        uv = u_ref[...]
        kr_ref[...] = _dot_nt(dyb, cre_ref[...])
        ki_ref[...] = _dot_nt(dyb, cim_ref[...])
        steps, pr, pi = _power_tables(ar_ref[...], -ai_ref[...])
        rows = lax.broadcasted_iota(jnp.int32, (SUBLANES, ns), 0)
        qr = jnp.zeros((SUBLANES, ns), f32)
        qi = jnp.zeros((SUBLANES, ns), f32)
        for r in range(SUBLANES):
            qr = jnp.where(rows == r, pr[SUBLANES - 1 - r:SUBLANES - r, :], qr)
            qi = jnp.where(rows == r, pi[SUBLANES - 1 - r:SUBLANES - r, :], qi)
        nb = tt // SUBLANES

        def blk(t, carry):
            cr, ci, accr, acci = carry
            sl = pl.ds(pl.multiple_of((nb - 1 - t) * SUBLANES, SUBLANES), SUBLANES)
            xr, xi = kr_ref[sl, :], ki_ref[sl, :]
            for d, (sr_, si_) in zip((1, 2, 4), steps):
                keep = rows < SUBLANES - d
                zr = jnp.where(keep, pltpu.roll(xr, SUBLANES - d, axis=0), 0.0)
                zi = jnp.where(keep, pltpu.roll(xi, SUBLANES - d, axis=0), 0.0)
                mr, mi = _cmul(sr_, si_, zr, zi)
                xr, xi = xr + mr, xi + mi
            mr, mi = _cmul(qr, qi, cr, ci)
            xr, xi = xr + mr, xi + mi
            kr_ref[sl, :] = xr
            ki_ref[sl, :] = xi
            last = rows == SUBLANES - 1
            nr = jnp.where(last, cr, pltpu.roll(xr, SUBLANES - 1, axis=0))
            ni = jnp.where(last, ci, pltpu.roll(xi, SUBLANES - 1, axis=0))
            hr, hi = hr_ref[sl, :], hi_ref[sl, :]
            accr = accr + nr * hr + ni * hi
            acci = acci + ni * hr - nr * hi
            return xr[:1, :], xi[:1, :], accr, acci

        cr, ci, accr, acci = lax.fori_loop(0, nb, blk, (cr_ref[...], ci_ref[...], accr_ref[...], acci_ref[...]))
        cr_ref[...] = cr
        ci_ref[...] = ci
        accr_ref[...] = accr
        acci_ref[...] = acci
        kr, ki = kr_ref[...].astype(bf16), ki_ref[...].astype(bf16)
        du = _dot_nt(kr, bre_ref[...]) + _dot_nt(ki, bim_ref[...]) + d_ref[...] * dyv
        du_ref[...] = du.astype(du_ref.dtype)
        dd_ref[...] += jnp.sum(dyv * uv.astype(f32), axis=0, keepdims=True)
        dbre_ref[...] += _dot_tn(uv, kr)
        dbim_ref[...] += _dot_tn(uv, ki)
        dcre_ref[...] += _dot_tn(hr_ref[...].astype(bf16), dyb)
        dcim_ref[...] += _dot_tn(hi_ref[...].astype(bf16), dyb)

        @pl.when(i == nt - 1)
        def _():
            dar_ref[...] = jnp.sum(accr_ref[...], axis=0, keepdims=True)
            dai_ref[...] = jnp.sum(acci_ref[...], axis=0, keepdims=True)

    lane = pl.BlockSpec((tt, HEAD), lambda j, i: (nt - 1 - i, j))
    st = pl.BlockSpec((tt, ns), lambda j, i: (nt - 1 - i, j))
    b3 = pl.BlockSpec((None, HEAD, ns), lambda j, i: (j, 0, 0))
    c3 = pl.BlockSpec((None, ns, HEAD), lambda j, i: (j, 0, 0))
    arow = pl.BlockSpec((1, ns), lambda j, i: (0, j))
    drow = pl.BlockSpec((1, HEAD), lambda j, i: (0, j))
    return pl.pallas_call(
        body, name="ssm_bwd", grid=(nj, nt),
        in_specs=[lane, lane, st, st, b3, b3, c3, c3, arow, arow, drow],
        out_specs=[lane, drow, arow, arow, b3, b3, c3, c3],
        out_shape=[jax.ShapeDtypeStruct((l, w), bf16), jax.ShapeDtypeStruct((1, w), f32),
                   jax.ShapeDtypeStruct((1, nj * ns), f32), jax.ShapeDtypeStruct((1, nj * ns), f32),
                   jax.ShapeDtypeStruct((nj, HEAD, ns), f32), jax.ShapeDtypeStruct((nj, HEAD, ns), f32),
                   jax.ShapeDtypeStruct((nj, ns, HEAD), f32), jax.ShapeDtypeStruct((nj, ns, HEAD), f32)],
        scratch_shapes=[pltpu.VMEM((tt, ns), f32), pltpu.VMEM((tt, ns), f32), pltpu.VMEM((1, ns), f32),
                        pltpu.VMEM((1, ns), f32), pltpu.VMEM((SUBLANES, ns), f32), pltpu.VMEM((SUBLANES, ns), f32)],
        compiler_params=_cparams("parallel", "arbitrary"),
    )(dy, u, h_re, h_im, bre3, bim3, cre3, cimn3, a_re, a_im, d_skip)


def glu_fwd(y, z_src, w_glu, b_glu):
    l, w = y.shape
    tm = _row_tile(l)

    def body(y_ref, z_ref, w_ref, b_ref, g_ref, t_ref, o_ref):
        g = _gelu(y_ref[...])
        gb = g.astype(bf16)
        t = _dot(gb, w_ref[...]) + b_ref[...]
        g_ref[...] = gb
        t_ref[...] = t
        o_ref[...] = (g * jax.nn.sigmoid(t) * _silu(z_ref[...].astype(f32))).astype(o_ref.dtype)

    blk = pl.BlockSpec((tm, w), lambda i: (i, 0))
    return pl.pallas_call(
        body, name="glu_fwd", grid=(l // tm,),
        in_specs=[blk, pl.BlockSpec((tm, w), lambda i: (i, 1)), pl.BlockSpec((w, w), lambda i: (0, 0)), _row(w)],
        out_specs=[blk, blk, blk],
        out_shape=[jax.ShapeDtypeStruct((l, w), bf16), jax.ShapeDtypeStruct((l, w), f32),
                   jax.ShapeDtypeStruct((l, w), bf16)],
        compiler_params=_cparams("parallel"),
    )(y, z_src, w_glu, b_glu)


def glu_bwd(dout, y, t, z_src, w_glu):
    l, w = y.shape
    tm = _row_tile(l)

    def body(do_ref, y_ref, t_ref, z_ref, w_ref, dy_ref, dz_ref, dt_ref, db_ref):
        @pl.when(pl.program_id(0) == 0)
        def _():
            db_ref[...] = jnp.zeros_like(db_ref)

        yv, zv, dov = y_ref[...], z_ref[...].astype(f32), do_ref[...]
        g = _gelu(yv)
        sg = jax.nn.sigmoid(t_ref[...])
        dy2 = dov * _silu(zv)
        dz_ref[...] = (dov * g * sg * _silu_grad(zv)).astype(dz_ref.dtype)
        dt = dy2 * g * sg * (1.0 - sg)
        dtb = dt.astype(bf16)
        dt_ref[...] = dtb
        db_ref[...] += jnp.sum(dt, axis=0, keepdims=True)
        dg = dy2 * sg + _dot_nt(dtb, w_ref[...])
        dy_ref[...] = dg * _gelu_grad(yv)

    blk = pl.BlockSpec((tm, w), lambda i: (i, 0))
    return pl.pallas_call(
        body, name="glu_bwd", grid=(l // tm,),
        in_specs=[blk, blk, blk, pl.BlockSpec((tm, w), lambda i: (i, 1)), pl.BlockSpec((w, w), lambda i: (0, 0))],
        out_specs=[blk, blk, blk, _row(w)],
        out_shape=[jax.ShapeDtypeStruct((l, w), f32), jax.ShapeDtypeStruct((l, w), bf16),
                   jax.ShapeDtypeStruct((l, w), bf16), jax.ShapeDtypeStruct((1, w), f32)],
        compiler_params=_cparams("arbitrary"),
    )(dout, y, t, z_src, w_glu)


def _adamw(w, g, m, v):
    m = ADAM_B1 * m + (1.0 - ADAM_B1) * g
    v = ADAM_B2 * v + (1.0 - ADAM_B2) * (g * g)
    m_hat = m / (1.0 - ADAM_B1 ** ADAM_STEP)
    v_hat = v / (1.0 - ADAM_B2 ** ADAM_STEP)
    return -ADAM_LR * (m_hat / (jnp.sqrt(v_hat) + ADAM_EPS) + ADAM_WD * w), m, v


def adam_reduce(pieces, w, m, v, name):
    r, c = w.shape
    n = pieces.shape[0]
    tr = _tile(r, (256, 128, 64, 32, 16, 8))

    def body(p_ref, w_ref, m_ref, v_ref, g_ref, d_ref, nm_ref, nv_ref):
        g = p_ref[0].astype(f32)
        for s in range(1, n):
            g = g + p_ref[s].astype(f32)
        g_ref[...] = g
        d_ref[...], nm_ref[...], nv_ref[...] = _adamw(w_ref[...], g, m_ref[...], v_ref[...])

    blk = pl.BlockSpec((tr, c), lambda i: (i, 0))
    return pl.pallas_call(
        body, name=name, grid=(r // tr,),
        in_specs=[pl.BlockSpec((n, tr, c), lambda i: (0, i, 0)), blk, blk, blk],
        out_specs=[blk] * 4, out_shape=[jax.ShapeDtypeStruct((r, c), f32)] * 4,
        compiler_params=_cparams("parallel"),
    )(pieces, w, m, v)


def adam_w_mod(cond_t, dm, w, m, v):
    nl, d, cols = w.shape
    tr = _tile(d, (512, 256, 128))

    def body(c_ref, dm_ref, w_ref, m_ref, v_ref, g_ref, d_ref, nm_ref, nv_ref):
        g = jnp.dot(c_ref[...], dm_ref[...], preferred_element_type=f32, precision=lax.Precision.HIGHEST)
        g_ref[...] = g
        d_ref[...], nm_ref[...], nv_ref[...] = _adamw(w_ref[...], g, m_ref[...], v_ref[...])

    blk = pl.BlockSpec((None, tr, cols), lambda l, i: (l, i, 0))
    return pl.pallas_call(
        body, name="adam_w_mod", grid=(nl, d // tr),
        in_specs=[pl.BlockSpec((tr, N_DEV), lambda l, i: (i, 0)), pl.BlockSpec((None, N_DEV, cols), lambda l, i: (l, 0, 0)),
                  blk, blk, blk],
        out_specs=[blk] * 4, out_shape=[jax.ShapeDtypeStruct((nl, d, cols), f32)] * 4,
        compiler_params=_cparams("parallel", "parallel"),
    )(cond_t, dm, w, m, v)


def silu_rows(c_all):
    def body(c_ref, o_ref):
        o_ref[...] = _silu(c_ref[...])

    return pl.pallas_call(body, name="silu_rows", out_shape=jax.ShapeDtypeStruct(c_all.shape, f32))(c_all)


def _block_diag(x):
    g, a, b = x.shape
    nj = g // GROUPS_PER_LANE_BLOCK
    eye = jnp.eye(GROUPS_PER_LANE_BLOCK, dtype=x.dtype)
    x5 = x.reshape(nj, GROUPS_PER_LANE_BLOCK, a, b)
    return jnp.einsum("jgab,gh->jgahb", x5, eye).reshape(nj, GROUPS_PER_LANE_BLOCK * a, GROUPS_PER_LANE_BLOCK * b)


def _diag_blocks(x, a, b):
    nj = x.shape[0]
    x5 = x.reshape(nj, GROUPS_PER_LANE_BLOCK, a, GROUPS_PER_LANE_BLOCK, b)
    eye = jnp.eye(GROUPS_PER_LANE_BLOCK, dtype=x.dtype)
    return jnp.einsum("jgahb,gh->jgab", x5, eye).reshape(nj * GROUPS_PER_LANE_BLOCK, a, b)


PACK_ROW = SUBLANES * HEAD


def _pack(parts, row_multiple=SUBLANES):
    rows = []
    for p in parts:
        flat = p.reshape(-1)
        pad = (-flat.shape[0]) % PACK_ROW
        if pad:
            flat = jnp.concatenate([flat, jnp.zeros((pad,), flat.dtype)])
        rows.append(flat.reshape(-1, HEAD))
    pad = (-sum(r.shape[0] for r in rows)) % row_multiple
    if pad:
        rows.append(jnp.zeros((pad, HEAD), rows[0].dtype))
    return jnp.concatenate(rows, axis=0)


def _unpack(packed, shapes):
    out, r0 = [], 0
    for shp in shapes:
        n = math.prod(shp)
        nr = -(-n // PACK_ROW) * SUBLANES
        out.append(packed[r0:r0 + nr].reshape(-1)[:n].reshape(shp))
        r0 += nr
    return out


def adam_small(g, w, m, v):
    r, c = w.shape

    def body(g_ref, w_ref, m_ref, v_ref, d_ref, nm_ref, nv_ref):
        d_ref[...], nm_ref[...], nv_ref[...] = _adamw(w_ref[...], g_ref[...], m_ref[...], v_ref[...])

    tr = max(t for t in range(SUBLANES, 1024 + 1, SUBLANES) if r % t == 0)
    blk = pl.BlockSpec((tr, c), lambda i: (i, 0))
    return pl.pallas_call(
        body, name="adam_small", grid=(r // tr,),
        in_specs=[blk] * 4, out_specs=[blk] * 3, out_shape=[jax.ShapeDtypeStruct((r, c), f32)] * 3,
        compiler_params=_cparams("parallel"),
    )(g, w, m, v)


def kernel(x, c, ln_pre_g, ln_post_g, w_mod, b_mod, w_in_ab, w_out_ab, sgu_norm_g, sgu_w, sgu_b, w_in_ssm, w_out_ssm, lam_re, lam_im, b_re, b_im, c_re, c_im, d_skip, log_dt, w_glu, b_glu, loss_target, m_ln_pre_g, m_ln_post_g, m_w_mod, m_b_mod, m_w_in_ab, m_w_out_ab, m_sgu_norm_g, m_sgu_w, m_sgu_b, m_w_in_ssm, m_w_out_ssm, m_lam_re, m_lam_im, m_b_re, m_b_im, m_c_re, m_c_im, m_d_skip, m_log_dt, m_w_glu, m_b_glu, v_ln_pre_g, v_ln_post_g, v_w_mod, v_b_mod, v_w_in_ab, v_w_out_ab, v_sgu_norm_g, v_sgu_w, v_sgu_b, v_w_in_ssm, v_w_out_ssm, v_lam_re, v_lam_im, v_b_re, v_b_im, v_c_re, v_c_im, v_d_skip, v_log_dt, v_w_glu, v_b_glu):
    me = _my_index()
    x0 = x[0]
    l, d = x0.shape
    target = loss_target[0]
    nh = sgu_w.shape[1]
    wa = nh * HEAD
    n_grp, n_st = lam_re.shape[1], lam_re.shape[2]
    mod_cols = w_mod.shape[2]

    c_all, d_skip_all, b_glu_all = all_gather([c, d_skip, b_glu], "gather_c")
    c_all = c_all.reshape(N_DEV, d)
    d_skip_all = d_skip_all.reshape(1, -1)
    b_glu_all = b_glu_all.reshape(1, -1)

    b_cols = lax.dynamic_slice_in_dim(b_mod, me * mod_cols, mod_cols, axis=1)
    (mod_all,) = all_gather([mod_part(c_all, w_mod, b_cols)], "gather_mod")
    def after(a, first):
        return a + jnp.minimum(jnp.abs(first.reshape(-1)[0].astype(f32)), 0.0).astype(a.dtype)

    (win_ab3,) = sequencer_exchange(GATHER, [after(w_in_ab[0], mod_all).astype(bf16)], "gather_w_in", 1)
    mod_mine = lax.dynamic_index_in_dim(mod_all, me, axis=2, keepdims=False)
    mod_rows = jnp.transpose(mod_mine, (1, 0, 2)).reshape(2, 3, 1, d)

    def rows(a, i):
        return a[i].reshape(1, d)

    shift0, scale0, gate0 = mod_rows[0, 0], mod_rows[0, 1], mod_rows[0, 2]
    h0 = prenorm_fwd(x0, rows(ln_pre_g, 0), shift0, scale0, "prenorm0")
    wout_ab3, win_ssm3, wout_ssm3, wglu = sequencer_exchange(
        GATHER, [after(w, win_ab3).astype(bf16) for w in (w_out_ab[0], w_in_ssm[0], w_out_ssm[0], w_glu[0])],
        "gather_w_rest", 2)
    proj0 = mm_nn(h0, win_ab3, bf16, "proj0")
    sgu_b3 = sgu_b[0].reshape(nh, HEAD, 1)
    out_a = sgu_fwd(proj0, sgu_norm_g, sgu_w[0], sgu_b3)
    out_b, att, tot = sb_fwd(proj0, nh)
    cat = jnp.concatenate([out_a, out_b], axis=1)
    wout_ab3 = wout_ab3.reshape(1, d, d)
    win_ssm3 = win_ssm3.reshape(1, d, d)
    wglu = wglu.reshape(w_glu.shape[2], w_glu.shape[2])
    y0 = mm_nn(cat, wout_ab3, f32, "out0")
    x1 = post_fwd(x0, y0, gate0, rows(ln_post_g, 0), "post0")

    shift1, scale1, gate1 = mod_rows[1, 0], mod_rows[1, 1], mod_rows[1, 2]
    h1 = prenorm_fwd(x1, rows(ln_pre_g, 1), shift1, scale1, "prenorm1")
    proj1 = mm_nn(h1, win_ssm3, bf16, "proj1")
    w_ssm = proj1.shape[1] // 2
    ldt = log_dt[0].reshape(n_grp, 1)
    bt_re = jnp.transpose(b_re[0], (0, 2, 1))
    bt_im = jnp.transpose(b_im[0], (0, 2, 1))
    a_re, a_im, bbt_re, bbt_im = s5_params_fwd(lam_re[0], lam_im[0], ldt, bt_re, bt_im)
    bre3 = _block_diag(bbt_re).astype(bf16)
    bim3 = _block_diag(bbt_im).astype(bf16)
    cre3 = _block_diag(jnp.transpose(c_re[0], (0, 2, 1))).astype(bf16)
    cimn3 = _block_diag(-jnp.transpose(c_im[0], (0, 2, 1))).astype(bf16)
    a_re_row, a_im_row = a_re.reshape(1, -1), a_im.reshape(1, -1)
    u = proj1[:, :w_ssm]
    y_ssm, hs_re, hs_im = ssm_fwd(u, bre3, bim3, cre3, cimn3, a_re_row, a_im_row, d_skip_all)
    g_act, t_glu, mix1 = glu_fwd(y_ssm, proj1, wglu, b_glu_all)
    y1 = mm_nn(mix1, wout_ssm3, f32, "out1")

    dx2, loss_tile = final_loss(x1, y1, gate1, rows(ln_post_g, 1), target)
    loss = lax.psum(loss_tile[0, 0] * (0.5 / d), ("x", "y", "c"))

    dy1, dgate1, dgpost1 = post_bwd(dx2, y1, gate1, rows(ln_post_g, 1), "post1_bwd")
    dmix1 = mm_nt(dy1, wout_ssm3, f32, "dmix1")
    gw_out_ssm = mm_tn(mix1, dy1, N_DEV, bf16, "gw_out_ssm")
    (p_out_ssm,) = sequencer_exchange(SCATTER, [gw_out_ssm], "scatter_g1", 3)
    dy_ssm, dz1, dt_glu, db_glu = glu_bwd(dmix1, y_ssm, t_glu, proj1, wglu)
    gw_glu = mm_tn(g_act, dt_glu, 1, bf16, "gw_glu").reshape(N_DEV, -1, w_ssm)
    du, dd_skip, da_re, da_im, dbre3, dbim3, dcre3, dcimn3 = ssm_bwd(
        dy_ssm, u, hs_re, hs_im, bre3, bim3, cre3, cimn3, a_re_row, a_im_row, d_skip_all)
    dproj1 = jnp.concatenate([du, dz1], axis=1)
    gw_in_ssm = mm_tn(h1, dproj1, 1, bf16, "gw_in_ssm").reshape(N_DEV, -1, proj1.shape[1])
    p_in_ssm, p_glu = sequencer_exchange(SCATTER, [gw_in_ssm, gw_glu], "scatter_g2", 4)
    dh1 = mm_nt(dproj1, win_ssm3, f32, "dh1")
    dx1, dshift1, dscale1, dgpre1 = prenorm_bwd(dh1, x1, dx2, rows(ln_pre_g, 1), scale1, "prenorm1_bwd")
    dlr, dli, dldt, dbt_re, dbt_im = s5_params_bwd(
        lam_re[0], lam_im[0], ldt, bt_re, bt_im, da_re.reshape(n_grp, n_st), da_im.reshape(n_grp, n_st),
        _diag_blocks(dbre3, SSM_GROUP, n_st), _diag_blocks(dbim3, SSM_GROUP, n_st))
    g_b_re = jnp.transpose(dbt_re, (0, 2, 1))
    g_b_im = jnp.transpose(dbt_im, (0, 2, 1))
    g_c_re = jnp.transpose(_diag_blocks(dcre3, n_st, SSM_GROUP), (0, 2, 1))
    g_c_im = -jnp.transpose(_diag_blocks(dcimn3, n_st, SSM_GROUP), (0, 2, 1))

    dy0, dgate0, dgpost0 = post_bwd(dx1, y0, gate0, rows(ln_post_g, 0), "post0_bwd")
    dcat = mm_nt(dy0, wout_ab3, f32, "dcat")
    gw_out_ab = mm_tn(cat, dy0, 1, bf16, "gw_out_ab").reshape(N_DEV, -1, d)
    (p_out_ab,) = sequencer_exchange(SCATTER, [gw_out_ab], "scatter_g3", 5)
    da, dsgu_w, dsgu_b, dsgu_ng = sgu_bwd(proj0, dcat, sgu_norm_g, sgu_w[0], sgu_b3)
    dq, dk, dv, dbz = sb_bwd(proj0, dcat, att, tot, nh)
    dproj0 = jnp.concatenate([da, dq, dk, dv, dbz], axis=1)
    gw_in_ab = mm_tn(h0, dproj0, N_DEV, bf16, "gw_in_ab")
    (p_in_ab,) = sequencer_exchange(SCATTER, [gw_in_ab], "scatter_g4", 6)
    dh0 = mm_nt(dproj0, win_ab3, f32, "dh0")
    dx0, dshift0, dscale0, dgpre0 = prenorm_bwd(dh0, x0, dx1, rows(ln_pre_g, 0), scale0, "prenorm0_bwd")

    small_names = ["ln_pre_g", "ln_post_g", "b_mod", "sgu_norm_g", "sgu_w", "sgu_b", "lam_re", "lam_im", "b_re", "b_im",
                   "c_re", "c_im", "log_dt"]
    small_w = [ln_pre_g, ln_post_g, b_mod, sgu_norm_g, sgu_w, sgu_b, lam_re, lam_im, b_re, b_im, c_re, c_im, log_dt]
    small_m = [m_ln_pre_g, m_ln_post_g, m_b_mod, m_sgu_norm_g, m_sgu_w, m_sgu_b, m_lam_re, m_lam_im, m_b_re, m_b_im,
               m_c_re, m_c_im, m_log_dt]
    small_v = [v_ln_pre_g, v_ln_post_g, v_b_mod, v_sgu_norm_g, v_sgu_w, v_sgu_b, v_lam_re, v_lam_im, v_b_re, v_b_im,
               v_c_re, v_c_im, v_log_dt]
    dmod = jnp.concatenate([dshift0, dscale0, dgate0, dshift1, dscale1, dgate1], axis=1)
    small_g = [jnp.concatenate([dgpre0, dgpre1]), jnp.concatenate([dgpost0, dgpost1]), dmod, dsgu_ng, dsgu_w, dsgu_b,
               dlr, dli, g_b_re, g_b_im, g_c_re, g_c_im, dldt]
    shapes = [w.shape for w in small_w]
    g_sum, dmod_all = all_reduce_rows(_pack(small_g + [dd_skip, db_glu], SUBLANES * N_DEV), dmod, "reduce_small_grads")
    n_rows_small = sum(-(-math.prod(s) // PACK_ROW) * SUBLANES for s in shapes)
    new_small = adam_small(g_sum, _pack(small_w), _pack(small_m), _pack(small_v))
    r_small = [_unpack(o, shapes) for o in [g_sum[:n_rows_small]] + list(new_small)]
    small = {n: [r_small[k][i] for k in range(4)] for i, n in enumerate(small_names)}
    vec_rows = d_skip_all.shape[1] // HEAD

    def my_columns(r0):
        whole = g_sum[r0:r0 + vec_rows].reshape(1, 1, -1)
        return lax.dynamic_slice_in_dim(whole, me * d_skip.shape[1], d_skip.shape[1], axis=2)

    def sharded(p, w, m, v, name):
        shp = w.shape
        w2, m2, v2 = (a.reshape(-1, shp[-1]) for a in (w, m, v))
        return [o.reshape(shp) for o in adam_reduce(p.reshape(p.shape[0], -1, shp[-1]), w2, m2, v2, name)]

    r_d_skip = sharded(my_columns(n_rows_small), d_skip, m_d_skip, v_d_skip, "adam_d_skip")
    r_b_glu = sharded(my_columns(n_rows_small + vec_rows), b_glu, m_b_glu, v_b_glu, "adam_b_glu")
    r_w_out_ssm = sharded(p_out_ssm, w_out_ssm, m_w_out_ssm, v_w_out_ssm, "adam_w_out_ssm")
    r_w_in_ssm = sharded(p_in_ssm, w_in_ssm, m_w_in_ssm, v_w_in_ssm, "adam_w_in_ssm")
    r_w_glu = sharded(p_glu, w_glu, m_w_glu, v_w_glu, "adam_w_glu")
    r_w_out_ab = sharded(p_out_ab, w_out_ab, m_w_out_ab, v_w_out_ab, "adam_w_out_ab")
    r_w_in_ab = sharded(p_in_ab, w_in_ab, m_w_in_ab, v_w_in_ab, "adam_w_in_ab")

    dm_cols = jnp.transpose(
        lax.dynamic_slice_in_dim(dmod_all.reshape(N_DEV, 2, 3 * d), me * mod_cols, mod_cols, axis=2), (1, 0, 2))
    cond_t = jnp.transpose(silu_rows(c_all))
    r_w_mod = adam_w_mod(cond_t, dm_cols, w_mod, m_w_mod, v_w_mod)

    res = dict(small)
    res.update(w_mod=r_w_mod, w_in_ab=r_w_in_ab, w_out_ab=r_w_out_ab, w_in_ssm=r_w_in_ssm, w_out_ssm=r_w_out_ssm,
               d_skip=r_d_skip, w_glu=r_w_glu, b_glu=r_b_glu)
    order = ["ln_pre_g", "ln_post_g", "w_mod", "b_mod", "w_in_ab", "w_out_ab", "sgu_norm_g", "sgu_w", "sgu_b", "w_in_ssm",
             "w_out_ssm", "lam_re", "lam_im", "b_re", "b_im", "c_re", "c_im", "d_skip", "log_dt", "w_glu", "b_glu"]
    outs = [loss, dx0.reshape(x.shape)]
    for k in range(4):
        outs += [res[n][k] for n in order]
    return tuple(outs)
```

```python
import functools
import math

import jax
import jax.numpy as jnp
from jax import lax
from jax.experimental import pallas as pl
from jax.experimental.pallas import tpu as pltpu
from jax.experimental.pallas import tpu_sc as plsc

f32 = jnp.float32
bf16 = jnp.bfloat16

N_DEV = 8
EPS = 1e-6
HEAD = 128
SUBLANES = 8
SSM_GROUP = 16
SSM_STATE = 64
GROUPS_PER_LANE_BLOCK = HEAD // SSM_GROUP
STATES_PER_LANE_BLOCK = GROUPS_PER_LANE_BLOCK * SSM_STATE
VMEM_LIMIT = 56 * 2 ** 20
ADAM_LR, ADAM_B1, ADAM_B2, ADAM_EPS, ADAM_WD, ADAM_STEP = 0.001, 0.9, 0.999, 1e-08, 0.01, 10
_GELU_C0 = math.sqrt(2.0 / math.pi)
_GELU_C1 = 0.044715
MESH = pl.DeviceIdType.MESH


def _cparams(*sem):
    return pltpu.CompilerParams(dimension_semantics=sem if sem else None, vmem_limit_bytes=VMEM_LIMIT)


def _gelu(x):
    return 0.5 * x * (1.0 + jnp.tanh(_GELU_C0 * (x + _GELU_C1 * x * x * x)))


def _gelu_grad(x):
    t = jnp.tanh(_GELU_C0 * (x + _GELU_C1 * x * x * x))
    return 0.5 * (1.0 + t) + 0.5 * x * (1.0 - t * t) * _GELU_C0 * (1.0 + 3.0 * _GELU_C1 * x * x)


def _silu(x):
    return x * jax.nn.sigmoid(x)


def _silu_grad(x):
    s = jax.nn.sigmoid(x)
    return s * (1.0 + x * (1.0 - s))


def _dot(a, b):
    return jnp.dot(a, b, preferred_element_type=f32)


def _dot_nt(a, b):
    return lax.dot_general(a, b, (((1,), (1,)), ((), ())), preferred_element_type=f32)


def _dot_tn(a, b):
    return lax.dot_general(a, b, (((0,), (0,)), ((), ())), preferred_element_type=f32)


def _split_bf16(v):
    hi = v.astype(bf16)
    lo = (v - hi.astype(f32)).astype(bf16)
    return hi, lo


def _row(d):
    return pl.BlockSpec((1, d), lambda *_: (0, 0))


def _my_index():
    return 4 * lax.axis_index("x") + 2 * lax.axis_index("y") + lax.axis_index("c")


def _peer(k):
    x, y, c = lax.axis_index("x"), lax.axis_index("y"), lax.axis_index("c")
    return (1 - x if k & 4 else x, 1 - y if k & 2 else y, 1 - c if k & 1 else c)


def all_gather(arrs, name):
    n = len(arrs)

    def body(*refs):
        ins, outs = refs[:n], refs[n:2 * n]
        send, recv, local = refs[2 * n:]
        me = _my_index()
        copies = []
        for a in range(n):
            cp = pltpu.make_async_copy(ins[a], outs[a].at[me], local.at[a])
            cp.start()
            copies.append(cp)
            for k in range(1, N_DEV):
                s = a * (N_DEV - 1) + k - 1
                cp = pltpu.make_async_remote_copy(src_ref=ins[a], dst_ref=outs[a].at[me], send_sem=send.at[s],
                                                  recv_sem=recv.at[s], device_id=_peer(k), device_id_type=MESH)
                cp.start()
                copies.append(cp)
        for cp in copies:
            cp.wait()

    any_spec = pl.BlockSpec(memory_space=pl.ANY)
    outs = pl.pallas_call(
        body, name=name,
        out_shape=[jax.ShapeDtypeStruct((N_DEV,) + a.shape, a.dtype) for a in arrs],
        in_specs=[any_spec] * n, out_specs=[any_spec] * n,
        scratch_shapes=[pltpu.SemaphoreType.DMA((n * (N_DEV - 1),)), pltpu.SemaphoreType.DMA((n * (N_DEV - 1),)),
                        pltpu.SemaphoreType.DMA((n,))],
        compiler_params=pltpu.CompilerParams(has_side_effects=True),
    )(*arrs)
    return list(outs)


def all_reduce_rows(pack, extra, name):
    r, c = pack.shape
    rs = r // N_DEV
    n_peer = N_DEV - 1

    def body(p_ref, x_ref, o_ref, xo_ref, land, red, send1, recv1, send2, recv2, sendx, recvx, local):
        me = _my_index()

        def rows(i):
            return pl.ds(pl.multiple_of(i * rs, SUBLANES), rs)

        own = [pltpu.make_async_copy(p_ref.at[rows(me)], land.at[me], local.at[0]),
               pltpu.make_async_copy(x_ref, xo_ref.at[me], local.at[1])]
        first = []
        for k in range(1, N_DEV):
            first.append(pltpu.make_async_remote_copy(
                src_ref=p_ref.at[rows(jnp.bitwise_xor(me, k))], dst_ref=land.at[me], send_sem=send1.at[k - 1],
                recv_sem=recv1.at[k - 1], device_id=_peer(k), device_id_type=MESH))
            first.append(pltpu.make_async_remote_copy(
                src_ref=x_ref, dst_ref=xo_ref.at[me], send_sem=sendx.at[k - 1], recv_sem=recvx.at[k - 1],
                device_id=_peer(k), device_id_type=MESH))
        for cp in own + first:
            cp.start()
        for cp in own + first:
            cp.wait()
        acc = land[0]
        for s in range(1, N_DEV):
            acc = acc + land[s]
        red[...] = acc
        mine = pltpu.make_async_copy(red, o_ref.at[rows(me)], local.at[2])
        second = [pltpu.make_async_remote_copy(
            src_ref=red, dst_ref=o_ref.at[rows(me)], send_sem=send2.at[k - 1], recv_sem=recv2.at[k - 1],
            device_id=_peer(k), device_id_type=MESH) for k in range(1, N_DEV)]
        for cp in [mine] + second:
            cp.start()
        for cp in [mine] + second:
            cp.wait()

    any_spec = pl.BlockSpec(memory_space=pl.ANY)
    return pl.pallas_call(
        body, name=name,
        out_shape=[jax.ShapeDtypeStruct((r, c), pack.dtype), jax.ShapeDtypeStruct((N_DEV,) + extra.shape, extra.dtype)],
        in_specs=[any_spec, any_spec], out_specs=[any_spec, any_spec],
        scratch_shapes=[pltpu.VMEM((N_DEV, rs, c), pack.dtype), pltpu.VMEM((rs, c), pack.dtype)]
        + [pltpu.SemaphoreType.DMA((n_peer,))] * 6 + [pltpu.SemaphoreType.DMA((3,))],
        compiler_params=pltpu.CompilerParams(has_side_effects=True),
    )(pack, extra)


GATHER, SCATTER = "gather", "scatter"


def _exchange_copies(srcs, lands, send, recv):
    me = _my_index()
    copies = []
    for a, (src, land) in enumerate(zip(srcs, lands)):
        for k in range(1, N_DEV):
            s = a * (N_DEV - 1) + k - 1
            copies.append(pltpu.make_async_remote_copy(
                src_ref=src.at[jnp.bitwise_xor(me, k)], dst_ref=land.at[me],
                send_sem=send.at[s], recv_sem=recv.at[s], device_id=_peer(k), device_id_type=MESH))
    return copies


def sequencer_exchange(kind, arrs, name, collective_id):
    n = len(arrs)
    n_sem = n * (N_DEV - 1)
    land_shapes = [((N_DEV,) + a.shape if kind == GATHER else a.shape) for a in arrs]
    srcs = [jax.new_ref(a, memory_space=pltpu.MemorySpace.HBM) for a in arrs]
    lands = [jax.empty_ref(jax.ShapeDtypeStruct(s, a.dtype), memory_space=pltpu.MemorySpace.HBM)
             for s, a in zip(land_shapes, arrs)]

    @pl.kernel(mesh=plsc.ScalarSubcoreMesh(axis_name="sequencer", num_cores=1), name=name,
               scratch_types=(pltpu.SemaphoreType.DMA((n_sem,)), pltpu.SemaphoreType.DMA((n_sem,)),
                              pltpu.SemaphoreType.DMA((n,))),
               compiler_params=pltpu.CompilerParams(collective_id=collective_id))
    def launch(send, recv, local):
        barrier = pltpu.get_barrier_semaphore()
        for k in range(1, N_DEV):
            pl.semaphore_signal(barrier, inc=1, device_id=_peer(k), device_id_type=MESH)
        pl.semaphore_wait(barrier, N_DEV - 1)
        me = _my_index()
        mine = [pltpu.make_async_copy(src if kind == GATHER else src.at[me], land.at[me], local.at[a])
                for a, (src, land) in enumerate(zip(srcs, lands))]
        if kind == SCATTER:
            copies = mine + _exchange_copies(srcs, lands, send, recv)
            for cp in copies:
                cp.start()
            for cp in copies:
                cp.wait()
            return

        def block_copy(a, slot, block, k, src=None):
            s = a * (N_DEV - 1) + slot
            return pltpu.make_async_remote_copy(
                src_ref=lands[a].at[block] if src is None else src, dst_ref=lands[a].at[block],
                send_sem=send.at[s], recv_sem=recv.at[s], device_id=_peer(k), device_id_type=MESH)

        chips = (2, 4, 6)
        sibling = jnp.bitwise_xor(me, 1)
        first = [block_copy(a, slot, me, k, src=srcs[a]) for a in range(n) for slot, k in enumerate((1,) + chips)]
        for cp in mine + first:
            cp.start()
        passed = []
        for a in range(n):
            for i, k in enumerate(chips):
                block = jnp.bitwise_xor(me, k)
                block_copy(a, 1 + i, block, k).wait_recv()
                passed.append(block_copy(a, 4 + i, block, 1))
                passed[-1].start()
        for a in range(n):
            block_copy(a, 0, sibling, 1).wait_recv()
            for i, k in enumerate(chips):
                block_copy(a, 4 + i, jnp.bitwise_xor(sibling, k), 1).wait_recv()
        for cp in mine:
            cp.wait()
        for cp in first + passed:
            cp.wait_send()

    launch()
    return [land[...] for land in lands]


def _tile(n, pref):
    for t in pref:
        if n % t == 0:
            return t
    return n


def mm_nn(a, b3, out_dtype, name):
    m, k = a.shape
    nb, _, bn = b3.shape
    tm = _tile(m, (512, 256, 128))
    tn = _tile(bn, (1024, 896, 512, 256, 128))
    per = bn // tn

    def body(a_ref, b_ref, o_ref):
        o_ref[...] = _dot(a_ref[...], b_ref[...]).astype(o_ref.dtype)

    return pl.pallas_call(
        body, name=name, grid=(m // tm, nb, per),
        in_specs=[pl.BlockSpec((tm, k), lambda i, j, jj: (i, 0)),
                  pl.BlockSpec((None, k, tn), lambda i, j, jj: (j, 0, jj))],
        out_specs=pl.BlockSpec((tm, tn), lambda i, j, jj: (i, j * per + jj)),
        out_shape=jax.ShapeDtypeStruct((m, nb * bn), out_dtype),
        compiler_params=_cparams("parallel", "arbitrary", "arbitrary"),
    )(a, b3)


def mm_nt(a, w3, out_dtype, name):
    m, _ = a.shape
    nb, ko, bn = w3.shape
    tm = _tile(m, (512, 256, 128))
    tko = _tile(ko, (1024, 512, 256, 128))

    def body(a_ref, w_ref, o_ref, acc_ref):
        j = pl.program_id(2)

        @pl.when(j == 0)
        def _():
            acc_ref[...] = jnp.zeros_like(acc_ref)

        acc_ref[...] += _dot_nt(a_ref[...], w_ref[...])

        @pl.when(j == nb - 1)
        def _():
            o_ref[...] = acc_ref[...].astype(o_ref.dtype)

    return pl.pallas_call(
        body, name=name, grid=(m // tm, ko // tko, nb),
        in_specs=[pl.BlockSpec((tm, bn), lambda i, o, j: (i, j)),
                  pl.BlockSpec((None, tko, bn), lambda i, o, j: (j, o, 0))],
        out_specs=pl.BlockSpec((tm, tko), lambda i, o, j: (i, o)),
        out_shape=jax.ShapeDtypeStruct((m, ko), out_dtype),
        scratch_shapes=[pltpu.VMEM((tm, tko), f32)],
        compiler_params=_cparams("parallel", "arbitrary", "arbitrary"),
    )(a, w3)


def mm_tn(a, dy, ncb, out_dtype, name):
    l, ka = a.shape
    _, n = dy.shape
    bn = n // ncb
    tl = _tile(l, (1024, 512, 256, 128))
    tka = _tile(ka, (512, 256, 128))
    tn = _tile(bn, (1024, 896, 512, 256, 128))
    per = bn // tn
    nl = l // tl

    def body(a_ref, dy_ref, o_ref, acc_ref):
        s = pl.program_id(2)

        @pl.when(s == 0)
        def _():
            acc_ref[...] = jnp.zeros_like(acc_ref)

        acc_ref[...] += _dot_tn(a_ref[...], dy_ref[...])

        @pl.when(s == nl - 1)
        def _():
            o_ref[...] = acc_ref[...].astype(o_ref.dtype)

    return pl.pallas_call(
        body, name=name, grid=(ka // tka, n // tn, nl),
        in_specs=[pl.BlockSpec((tl, tka), lambda i, j, s: (s, i)),
                  pl.BlockSpec((tl, tn), lambda i, j, s: (s, j))],
        out_specs=pl.BlockSpec((None, tka, tn), lambda i, j, s: (j // per, i, j % per)),
        out_shape=jax.ShapeDtypeStruct((ncb, ka, bn), out_dtype),
        scratch_shapes=[pltpu.VMEM((tka, tn), f32)],
        compiler_params=_cparams("parallel", "parallel", "arbitrary"),
    )(a, dy)


def mod_part(c_all, w_mod, b_cols):
    nl, d, cols = w_mod.shape

    def body(c_ref, w_ref, b_ref, o_ref):
        cond = _silu(c_ref[...]).astype(bf16)
        o_ref[...] = _dot(cond, w_ref[...].astype(bf16)) + b_ref[...]

    return pl.pallas_call(
        body, name="mod_part", grid=(nl,),
        in_specs=[pl.BlockSpec((N_DEV, d), lambda l: (0, 0)),
                  pl.BlockSpec((None, d, cols), lambda l: (l, 0, 0)),
                  pl.BlockSpec((None, 1, cols), lambda l: (l, 0, 0))],
        out_specs=pl.BlockSpec((None, N_DEV, cols), lambda l: (l, 0, 0)),
        out_shape=jax.ShapeDtypeStruct((nl, N_DEV, cols), f32),
        compiler_params=_cparams("arbitrary"),
    )(c_all, w_mod, b_cols.reshape(nl, 1, cols))


def _row_tile(l):
    return _tile(l, (256, 128))


def prenorm_fwd(x, g, shift, scale, name):
    l, d = x.shape
    tm = _row_tile(l)

    def body(x_ref, g_ref, sh_ref, sc_ref, h_ref):
        xv = x_ref[...]
        r = lax.rsqrt(jnp.mean(xv * xv, axis=-1, keepdims=True) + EPS)
        h_ref[...] = (xv * r * (g_ref[...] * (1.0 + sc_ref[...])) + sh_ref[...]).astype(h_ref.dtype)

    return pl.pallas_call(
        body, name=name, grid=(l // tm,),
        in_specs=[pl.BlockSpec((tm, d), lambda i: (i, 0)), _row(d), _row(d), _row(d)],
        out_specs=pl.BlockSpec((tm, d), lambda i: (i, 0)),
        out_shape=jax.ShapeDtypeStruct((l, d), bf16),
        compiler_params=_cparams("parallel"),
    )(x, g, shift, scale)


def post_fwd(x, y, gate, g, name):
    l, d = x.shape
    tm = _row_tile(l)

    def body(x_ref, y_ref, gate_ref, g_ref, o_ref):
        yv = y_ref[...]
        r = lax.rsqrt(jnp.mean(yv * yv, axis=-1, keepdims=True) + EPS)
        o_ref[...] = x_ref[...] + gate_ref[...] * (yv * r * g_ref[...])

    blk = pl.BlockSpec((tm, d), lambda i: (i, 0))
    return pl.pallas_call(
        body, name=name, grid=(l // tm,),
        in_specs=[blk, blk, _row(d), _row(d)], out_specs=blk,
        out_shape=jax.ShapeDtypeStruct((l, d), f32),
        compiler_params=_cparams("parallel"),
    )(x, y, gate, g)


def final_loss(x, y, gate, g, target):
    l, d = x.shape
    tm = _row_tile(l)

    def body(x_ref, y_ref, gate_ref, g_ref, t_ref, dx_ref, loss_ref):
        @pl.when(pl.program_id(0) == 0)
        def _():
            loss_ref[...] = jnp.zeros_like(loss_ref)

        yv = y_ref[...]
        r = lax.rsqrt(jnp.mean(yv * yv, axis=-1, keepdims=True) + EPS)
        diff = x_ref[...] + gate_ref[...] * (yv * r * g_ref[...]) - t_ref[...]
        dx_ref[...] = diff * (1.0 / d)
        loss_ref[...] += jnp.sum(diff * diff)

    blk = pl.BlockSpec((tm, d), lambda i: (i, 0))
    return pl.pallas_call(
        body, name="final_loss", grid=(l // tm,),
        in_specs=[blk, blk, _row(d), _row(d), blk],
        out_specs=[blk, pl.BlockSpec((SUBLANES, HEAD), lambda i: (0, 0))],
        out_shape=[jax.ShapeDtypeStruct((l, d), f32), jax.ShapeDtypeStruct((SUBLANES, HEAD), f32)],
        compiler_params=_cparams("arbitrary"),
    )(x, y, gate, g, target)


def post_bwd(dx, y, gate, g, name):
    l, d = dx.shape
    tm = _row_tile(l)

    def body(dx_ref, y_ref, gate_ref, g_ref, dy_ref, dgate_ref, dg_ref):
        @pl.when(pl.program_id(0) == 0)
        def _():
            dgate_ref[...] = jnp.zeros_like(dgate_ref)
            dg_ref[...] = jnp.zeros_like(dg_ref)

        yv, dxv, gv = y_ref[...], dx_ref[...], g_ref[...]
        r = lax.rsqrt(jnp.mean(yv * yv, axis=-1, keepdims=True) + EPS)
        yn = yv * r
        dgate_ref[...] += jnp.sum(dxv * yn * gv, axis=0, keepdims=True)
        dyg = dxv * gate_ref[...]
        dg_ref[...] += jnp.sum(dyg * yn, axis=0, keepdims=True)
        dyn = dyg * gv
        dy_ref[...] = (r * (dyn - yn * jnp.mean(dyn * yn, axis=-1, keepdims=True))).astype(dy_ref.dtype)

    blk = pl.BlockSpec((tm, d), lambda i: (i, 0))
    return pl.pallas_call(
        body, name=name, grid=(l // tm,),
        in_specs=[blk, blk, _row(d), _row(d)], out_specs=[blk, _row(d), _row(d)],
        out_shape=[jax.ShapeDtypeStruct((l, d), bf16), jax.ShapeDtypeStruct((1, d), f32),
                   jax.ShapeDtypeStruct((1, d), f32)],
        compiler_params=_cparams("arbitrary"),
    )(dx, y, gate, g)


def prenorm_bwd(dh, x, dx_next, g, scale, name):
    l, d = x.shape
    tm = _row_tile(l)

    def body(dh_ref, x_ref, dxn_ref, g_ref, sc_ref, dx_ref, dsh_ref, dsc_ref, dg_ref):
        @pl.when(pl.program_id(0) == 0)
        def _():
            dsh_ref[...] = jnp.zeros_like(dsh_ref)
            dsc_ref[...] = jnp.zeros_like(dsc_ref)
            dg_ref[...] = jnp.zeros_like(dg_ref)

        xv, dhv, gv, sc1 = x_ref[...], dh_ref[...], g_ref[...], 1.0 + sc_ref[...]
        r = lax.rsqrt(jnp.mean(xv * xv, axis=-1, keepdims=True) + EPS)
        xn = xv * r
        dhx = dhv * xn
        dsh_ref[...] += jnp.sum(dhv, axis=0, keepdims=True)
        dsc_ref[...] += jnp.sum(dhx * gv, axis=0, keepdims=True)
        dg_ref[...] += jnp.sum(dhx * sc1, axis=0, keepdims=True)
        dxn = dhv * (gv * sc1)
        dx_ref[...] = dxn_ref[...] + r * (dxn - xn * jnp.mean(dxn * xn, axis=-1, keepdims=True))

    blk = pl.BlockSpec((tm, d), lambda i: (i, 0))
    return pl.pallas_call(
        body, name=name, grid=(l // tm,),
        in_specs=[blk, blk, blk, _row(d), _row(d)], out_specs=[blk, _row(d), _row(d), _row(d)],
        out_shape=[jax.ShapeDtypeStruct((l, d), f32)] + [jax.ShapeDtypeStruct((1, d), f32)] * 3,
        compiler_params=_cparams("arbitrary"),
    )(dh, x, dx_next, g, scale)


def _tril_mask():
    r = lax.broadcasted_iota(jnp.int32, (HEAD, HEAD), 0)
    c = lax.broadcasted_iota(jnp.int32, (HEAD, HEAD), 1)
    return r >= c


def sgu_fwd(proj, norm_g, w_s, b_s):
    l = proj.shape[0]
    nh = w_s.shape[0]
    wa = nh * HEAD

    def body(au_ref, av_ref, az_ref, ng_ref, w_ref, b_ref, o_ref):
        tril = _tril_mask()
        for h in range(nh):
            sl = slice(h * HEAD, (h + 1) * HEAD)
            gv = _gelu(av_ref[:, sl].astype(f32))
            r = lax.rsqrt(jnp.mean(gv * gv, axis=-1, keepdims=True) + EPS)
            vh = gv * r * ng_ref[:, sl]
            wm = jnp.where(tril, w_ref[h], 0.0).astype(bf16)
            s = _dot(wm, vh.astype(bf16)) + b_ref[h]
            o_ref[:, sl] = (_gelu(au_ref[:, sl].astype(f32)) * s * _silu(az_ref[:, sl].astype(f32))).astype(o_ref.dtype)

    def col(j):
        return pl.BlockSpec((HEAD, wa), lambda n: (n, j))

    return pl.pallas_call(
        body, name="sgu_fwd", grid=(l // HEAD,),
        in_specs=[col(0), col(1), col(2), _row(wa),
                  pl.BlockSpec((nh, HEAD, HEAD), lambda n: (0, 0, 0)), pl.BlockSpec((nh, HEAD, 1), lambda n: (0, 0, 0))],
        out_specs=pl.BlockSpec((HEAD, wa), lambda n: (n, 0)),
        out_shape=jax.ShapeDtypeStruct((l, wa), bf16),
        compiler_params=_cparams("parallel"),
    )(proj, proj, proj, norm_g, w_s, b_s)


def sgu_bwd(proj, dcat, norm_g, w_s, b_s):
    l = proj.shape[0]
    nh = w_s.shape[0]
    wa = nh * HEAD

    def body(au_ref, av_ref, az_ref, do_ref, ng_ref, w_ref, b_ref, da_ref, dw_ref, db_ref, dng_ref):
        @pl.when(pl.program_id(0) == 0)
        def _():
            dw_ref[...] = jnp.zeros_like(dw_ref)
            db_ref[...] = jnp.zeros_like(db_ref)
            dng_ref[...] = jnp.zeros_like(dng_ref)

        tril = _tril_mask()
        for h in range(nh):
            sl = slice(h * HEAD, (h + 1) * HEAD)
            au, av, az = au_ref[:, sl].astype(f32), av_ref[:, sl].astype(f32), az_ref[:, sl].astype(f32)
            ng = ng_ref[:, sl]
            gv = _gelu(av)
            r = lax.rsqrt(jnp.mean(gv * gv, axis=-1, keepdims=True) + EPS)
            gvn = gv * r
            vh = (gvn * ng).astype(bf16)
            wm = jnp.where(tril, w_ref[h], 0.0).astype(bf16)
            s = _dot(wm, vh) + b_ref[h]
            gu, sz = _gelu(au), _silu(az)
            dov = do_ref[:, sl].astype(f32)
            da_ref[:, sl] = (dov * s * sz * _gelu_grad(au)).astype(da_ref.dtype)
            da_ref[:, 2 * wa + h * HEAD:2 * wa + (h + 1) * HEAD] = (dov * gu * s * _silu_grad(az)).astype(da_ref.dtype)
            ds = dov * gu * sz
            db_ref[h] += jnp.sum(ds, axis=-1, keepdims=True)
            dsb = ds.astype(bf16)
            dw_ref[h] += jnp.where(tril, _dot_nt(dsb, vh), 0.0)
            dvh = _dot_tn(wm, dsb)
            dng_ref[:, sl] += jnp.sum(dvh * gvn, axis=0, keepdims=True)
            dgvn = dvh * ng
            dgv = r * (dgvn - gvn * jnp.mean(dgvn * gvn, axis=-1, keepdims=True))
            da_ref[:, wa + h * HEAD:wa + (h + 1) * HEAD] = (dgv * _gelu_grad(av)).astype(da_ref.dtype)

    def col(j):
        return pl.BlockSpec((HEAD, wa), lambda n: (n, j))

    whole_w = pl.BlockSpec((nh, HEAD, HEAD), lambda n: (0, 0, 0))
    whole_b = pl.BlockSpec((nh, HEAD, 1), lambda n: (0, 0, 0))
    return pl.pallas_call(
        body, name="sgu_bwd", grid=(l // HEAD,),
        in_specs=[col(0), col(1), col(2), col(0), _row(wa), whole_w, whole_b],
        out_specs=[pl.BlockSpec((HEAD, 3 * wa), lambda n: (n, 0)), whole_w, whole_b, _row(wa)],
        out_shape=[jax.ShapeDtypeStruct((l, 3 * wa), bf16), jax.ShapeDtypeStruct((nh, HEAD, HEAD), f32),
                   jax.ShapeDtypeStruct((nh, HEAD, 1), f32), jax.ShapeDtypeStruct((1, wa), f32)],
        compiler_params=_cparams("arbitrary"),
    )(proj, proj, proj, dcat, norm_g, w_s, b_s)


_LOG2E = 1.0 / math.log(2.0)


def _sb_scores(q, k, scale):
    z = _dot_nt(q, k) * (scale * _LOG2E)
    return z, jnp.maximum(z, 0.0) + jnp.log2(1.0 + jnp.exp2(-jnp.abs(z)))


def _sb_sum_matrix(tri):
    s = lax.broadcasted_iota(jnp.int32, (2 * HEAD, 2 * HEAD), 0) % HEAD
    j = lax.broadcasted_iota(jnp.int32, (2 * HEAD, 2 * HEAD), 1)
    return jnp.where(jnp.logical_or(j >= HEAD, tri(s, j)), 1.0, 0.0).astype(bf16)


def _sb_sums(x, sums):
    c2 = _dot(jnp.concatenate(_split_bf16(x), axis=1), sums)
    return c2[:, :HEAD], c2[:, HEAD:]


def _sb_q_tile(l, most=512):
    return _tile(l, tuple(t for t in (1024, 512, 256, 128) if t <= most))


def _sb_heads_per_step(nh, most):
    return _tile(nh, tuple(h for h in (4, 2) if h <= most))


def sb_fwd(proj, nh):
    l = proj.shape[0]
    wb = nh * HEAD
    tq = _sb_q_tile(l, 1024)
    band = tq // HEAD
    hp = _sb_heads_per_step(nh, 2)
    scale = 1.0 / math.sqrt(HEAD)
    qc, kc, vc, zc = 3 * nh, 4 * nh, 5 * nh, 6 * nh

    def body(q_ref, k_ref, v_ref, bz_ref, o_ref, att_ref, tot_ref):
        i = pl.program_id(1)
        sums = _sb_sum_matrix(lambda s, j: s > j)
        t_pos = i * tq + lax.broadcasted_iota(jnp.int32, (tq, HEAD), 0)
        s_off = lax.broadcasted_iota(jnp.int32, (tq, HEAD), 1)

        def step(j, carry, masked):
            rows = pl.ds(pl.multiple_of(j * HEAD, HEAD), HEAD)
            out = []
            for e in range(hp):
                acc, tot = carry[e]
                sl = slice(e * HEAD, (e + 1) * HEAD)
                z, sp = _sb_scores(q_ref[:, sl], k_ref[rows, sl], scale)
                lb = z - sp
                if masked:
                    mask = s_off + j * HEAD < t_pos
                    sp = jnp.where(mask, sp, 0.0)
                later, total = _sb_sums(sp, sums)
                w = jnp.exp2(lb + tot - later)
                if masked:
                    w = jnp.where(mask, w, 0.0)
                out.append((acc + _dot(w.astype(bf16), v_ref[rows, sl]), tot - total))
            return tuple(out)

        zero = jnp.zeros((tq, HEAD), f32)
        carry = lax.fori_loop(0, band, lambda t, c: step(band * i + band - 1 - t, c, True), ((zero, zero),) * hp)
        carry = lax.fori_loop(0, band * i, lambda t, c: step(band * i - 1 - t, c, False), carry)
        for e in range(hp):
            acc, tot = carry[e]
            sl = slice(e * HEAD, (e + 1) * HEAD)
            att_ref[:, sl] = acc.astype(att_ref.dtype)
            o_ref[:, sl] = (acc * _silu(bz_ref[:, sl].astype(f32))).astype(o_ref.dtype)
            tot_ref[e] = tot[:, :1]

    blk = lambda c0: pl.BlockSpec((tq, hp * HEAD), lambda g, i: (i, c0 // hp + g))
    head = lambda c0: pl.BlockSpec((l, hp * HEAD), lambda g, i: (0, c0 // hp + g))
    return pl.pallas_call(
        body, name="sb_fwd", grid=(nh // hp, l // tq),
        in_specs=[blk(qc), head(kc), head(vc), blk(zc)],
        out_specs=[blk(0), blk(0), pl.BlockSpec((hp, tq, 1), lambda g, i: (g, i, 0))],
        out_shape=[jax.ShapeDtypeStruct((l, wb), bf16), jax.ShapeDtypeStruct((l, wb), bf16),
                   jax.ShapeDtypeStruct((nh, l, 1), f32)],
        compiler_params=_cparams("parallel", "arbitrary"),
    )(proj, proj, proj, proj)


def sb_bwd(proj, dcat, att, tot, nh):
    l = proj.shape[0]
    wb = nh * HEAD
    tq = _sb_q_tile(l, 1024)
    band = tq // HEAD
    nq = l // tq
    hp = _sb_heads_per_step(nh, 2)
    scale = 1.0 / math.sqrt(HEAD)
    qc, kc, vc, zc = 3 * nh, 4 * nh, 5 * nh, 6 * nh

    def body(q_ref, k_ref, v_ref, bz_ref, do_ref, att_ref, tot_ref, dq_ref, dk_ref, dv_ref, dbz_ref, dk_acc, dv_acc,
             dob_ref):
        i = pl.program_id(1)

        @pl.when(i == 0)
        def _():
            dk_acc[...] = jnp.zeros_like(dk_acc)
            dv_acc[...] = jnp.zeros_like(dv_acc)

        bz = bz_ref[...].astype(f32)
        dov = do_ref[...].astype(f32)
        dbz_ref[...] = (dov * att_ref[...].astype(f32) * _silu_grad(bz)).astype(dbz_ref.dtype)
        dob_ref[...] = (dov * _silu(bz)).astype(bf16)
        upto = _sb_sum_matrix(lambda s, j: s <= j)
        before = _sb_sum_matrix(lambda j, s: j < s)
        t_pos = i * tq + lax.broadcasted_iota(jnp.int32, (tq, HEAD), 0)
        s_off = lax.broadcasted_iota(jnp.int32, (tq, HEAD), 1)

        def step(j, carry, masked):
            rows = pl.ds(pl.multiple_of(j * HEAD, HEAD), HEAD)
            out = []
            for h in range(hp):
                dq, sp_seen, e_seen = carry[h]
                sl = slice(h * HEAD, (h + 1) * HEAD)
                q, kj, vj, dob = q_ref[:, sl], k_ref[rows, sl], v_ref[rows, sl], dob_ref[:, sl]
                z, sp = _sb_scores(q, kj, scale)
                lb = z - sp
                if masked:
                    mask = s_off + j * HEAD < t_pos
                    sp = jnp.where(mask, sp, 0.0)
                sp_upto, sp_total = _sb_sums(sp, upto)
                w = jnp.exp2(lb + sp_seen + sp_upto)
                if masked:
                    w = jnp.where(mask, w, 0.0)
                dv_acc[rows, sl] += _dot_tn(w.astype(bf16), dob)
                e = _dot_nt(dob, vj) * w
                e_before, e_total = _sb_sums(e, before)
                dz = (e - (e + e_seen + e_before) * jnp.exp2(lb)) * scale
                if masked:
                    dz = jnp.where(mask, dz, 0.0)
                dz = dz.astype(bf16)
                dk_acc[rows, sl] += _dot_tn(dz, q)
                out.append((dq + _dot(dz, kj), sp_seen + sp_total, e_seen + e_total))
            return tuple(out)

        zero = jnp.zeros((tq, HEAD), f32)
        init = tuple((zero, jnp.broadcast_to(tot_ref[h], (tq, HEAD)), zero) for h in range(hp))
        carry = lax.fori_loop(0, band * i, lambda j, c: step(j, c, False), init)
        carry = lax.fori_loop(0, band, lambda t, c: step(band * i + t, c, True), carry)
        for h in range(hp):
            dq_ref[:, h * HEAD:(h + 1) * HEAD] = carry[h][0].astype(dq_ref.dtype)

        @pl.when(i == nq - 1)
        def _():
            dk_ref[...] = dk_acc[...].astype(dk_ref.dtype)
            dv_ref[...] = dv_acc[...].astype(dv_ref.dtype)

    blk = lambda c0: pl.BlockSpec((tq, hp * HEAD), lambda g, i: (i, c0 // hp + g))
    head = lambda c0: pl.BlockSpec((l, hp * HEAD), lambda g, i: (0, c0 // hp + g))
    return pl.pallas_call(
        body, name="sb_bwd", grid=(nh // hp, nq),
        in_specs=[blk(qc), head(kc), head(vc), blk(zc), blk(nh), blk(0),
                  pl.BlockSpec((hp, tq, 1), lambda g, i: (g, i, 0))],
        out_specs=[blk(0), head(0), head(0), blk(0)],
        out_shape=[jax.ShapeDtypeStruct((l, wb), bf16)] * 4,
        scratch_shapes=[pltpu.VMEM((l, hp * HEAD), f32), pltpu.VMEM((l, hp * HEAD), f32),
                        pltpu.VMEM((tq, hp * HEAD), bf16)],
        compiler_params=_cparams("parallel", "arbitrary"),
    )(proj, proj, proj, proj, dcat, att, tot)


def _disc(lr, li, ldt):
    dt = jnp.exp(ldt)
    mag = jnp.exp(lr * dt)
    a_re = mag * jnp.cos(li * dt)
    a_im = mag * jnp.sin(li * dt)
    den = lr * lr + li * li
    nr = a_re - 1.0
    return a_re, a_im, (nr * lr + a_im * li) / den, (a_im * lr - nr * li) / den


def s5_params_fwd(lr, li, ldt, bt_re, bt_im):
    g, c, p = bt_re.shape

    def body(lr_ref, li_ref, ldt_ref, br_ref, bi_ref, ar_ref, ai_ref, bbr_ref, bbi_ref):
        a_re, a_im, cr, ci = _disc(lr_ref[...], li_ref[...], ldt_ref[...])
        ar_ref[...] = a_re
        ai_ref[...] = a_im
        for k in range(c):
            br, bi = br_ref[:, k, :], bi_ref[:, k, :]
            bbr_ref[:, k, :] = cr * br - ci * bi
            bbi_ref[:, k, :] = cr * bi + ci * br

    return pl.pallas_call(
        body, name="s5_params_fwd",
        out_shape=[jax.ShapeDtypeStruct((g, p), f32)] * 2 + [jax.ShapeDtypeStruct((g, c, p), f32)] * 2,
    )(lr, li, ldt, bt_re, bt_im)


def s5_params_bwd(lr, li, ldt, bt_re, bt_im, da_re, da_im, dbbt_re, dbbt_im):
    g, c, p = bt_re.shape

    def body(lr_ref, li_ref, ldt_ref, br_ref, bi_ref, dar_ref, dai_ref, dbbr_ref, dbbi_ref,
             dlr_ref, dli_ref, dldt_ref, dbr_ref, dbi_ref):
        (a_re, a_im, cr, ci), vjp = jax.vjp(_disc, lr_ref[...], li_ref[...], ldt_ref[...])
        dcr = jnp.zeros((g, p), f32)
        dci = jnp.zeros((g, p), f32)
        for k in range(c):
            br, bi = br_ref[:, k, :], bi_ref[:, k, :]
            dr, di = dbbr_ref[:, k, :], dbbi_ref[:, k, :]
            dcr += dr * br + di * bi
            dci += di * br - dr * bi
            dbr_ref[:, k, :] = cr * dr + ci * di
            dbi_ref[:, k, :] = cr * di - ci * dr
        dlr, dli, dldt = vjp((dar_ref[...], dai_ref[...], dcr, dci))
        dlr_ref[...] = dlr
        dli_ref[...] = dli
        dldt_ref[...] = dldt

    return pl.pallas_call(
        body, name="s5_params_bwd",
        out_shape=[jax.ShapeDtypeStruct((g, p), f32)] * 2 + [jax.ShapeDtypeStruct((g, 1), f32)]
        + [jax.ShapeDtypeStruct((g, c, p), f32)] * 2,
    )(lr, li, ldt, bt_re, bt_im, da_re, da_im, dbbt_re, dbbt_im)


def _cmul(ar, ai, br, bi):
    return ar * br - ai * bi, ar * bi + ai * br


def _power_tables(ar, ai):
    rows = lax.broadcasted_iota(jnp.int32, (SUBLANES, ar.shape[1]), 0)
    pr = jnp.zeros((SUBLANES, ar.shape[1]), f32)
    pi = jnp.zeros((SUBLANES, ar.shape[1]), f32)
    cr, ci = ar, ai
    pows = {}
    for r in range(SUBLANES):
        pows[r + 1] = (cr, ci)
        pr = jnp.where(rows == r, cr, pr)
        pi = jnp.where(rows == r, ci, pi)
        cr, ci = _cmul(cr, ci, ar, ai)
    return [pows[1], pows[2], pows[4]], pr, pi


def _ssm_time_tile(l):
    return _tile(l, (512, 256, 128))


def ssm_fwd(u, bre3, bim3, cre3, cimn3, a_re, a_im, d_skip):
    l, w = u.shape
    nj = w // HEAD
    ns = STATES_PER_LANE_BLOCK
    tt = _ssm_time_tile(l)

    def body(u_ref, bre_ref, bim_ref, cre_ref, cim_ref, ar_ref, ai_ref, d_ref, y_ref, hr_ref, hi_ref, cr_ref, ci_ref):
        @pl.when(pl.program_id(1) == 0)
        def _():
            cr_ref[...] = jnp.zeros_like(cr_ref)
            ci_ref[...] = jnp.zeros_like(ci_ref)

        uv = u_ref[...]
        hr_ref[...] = _dot(uv, bre_ref[...])
        hi_ref[...] = _dot(uv, bim_ref[...])
        steps, pr, pi = _power_tables(ar_ref[...], ai_ref[...])
        rows = lax.broadcasted_iota(jnp.int32, (SUBLANES, ns), 0)

        def blk(b, carry):
            cr, ci = carry
            sl = pl.ds(pl.multiple_of(b * SUBLANES, SUBLANES), SUBLANES)
            xr, xi = hr_ref[sl, :], hi_ref[sl, :]
            for d, (sr_, si_) in zip((1, 2, 4), steps):
                keep = rows >= d
                qr = jnp.where(keep, pltpu.roll(xr, d, axis=0), 0.0)
                qi = jnp.where(keep, pltpu.roll(xi, d, axis=0), 0.0)
                mr, mi = _cmul(sr_, si_, qr, qi)
                xr, xi = xr + mr, xi + mi
            mr, mi = _cmul(pr, pi, cr, ci)
            xr, xi = xr + mr, xi + mi
            hr_ref[sl, :] = xr
            hi_ref[sl, :] = xi
            return xr[SUBLANES - 1:, :], xi[SUBLANES - 1:, :]

        cr, ci = lax.fori_loop(0, tt // SUBLANES, blk, (cr_ref[...], ci_ref[...]))
        cr_ref[...] = cr
        ci_ref[...] = ci
        y = _dot(hr_ref[...].astype(bf16), cre_ref[...]) + _dot(hi_ref[...].astype(bf16), cim_ref[...])
        y_ref[...] = y + d_ref[...] * uv.astype(f32)

    lane = pl.BlockSpec((tt, HEAD), lambda j, i: (i, j))
    st = pl.BlockSpec((tt, ns), lambda j, i: (i, j))
    b3 = pl.BlockSpec((None, HEAD, ns), lambda j, i: (j, 0, 0))
    c3 = pl.BlockSpec((None, ns, HEAD), lambda j, i: (j, 0, 0))
    arow = pl.BlockSpec((1, ns), lambda j, i: (0, j))
    return pl.pallas_call(
        body, name="ssm_fwd", grid=(nj, l // tt),
        in_specs=[lane, b3, b3, c3, c3, arow, arow, pl.BlockSpec((1, HEAD), lambda j, i: (0, j))],
        out_specs=[lane, st, st],
        out_shape=[jax.ShapeDtypeStruct((l, w), f32), jax.ShapeDtypeStruct((l, nj * ns), f32),
                   jax.ShapeDtypeStruct((l, nj * ns), f32)],
        scratch_shapes=[pltpu.VMEM((1, ns), f32), pltpu.VMEM((1, ns), f32)],
        compiler_params=_cparams("parallel", "arbitrary"),
    )(u, bre3, bim3, cre3, cimn3, a_re, a_im, d_skip)


def ssm_bwd(dy, u, h_re, h_im, bre3, bim3, cre3, cimn3, a_re, a_im, d_skip):
    l, w = u.shape
    nj = w // HEAD
    ns = STATES_PER_LANE_BLOCK
    tt = _ssm_time_tile(l)
    nt = l // tt

    def body(dy_ref, u_ref, hr_ref, hi_ref, bre_ref, bim_ref, cre_ref, cim_ref, ar_ref, ai_ref, d_ref,
             du_ref, dd_ref, dar_ref, dai_ref, dbre_ref, dbim_ref, dcre_ref, dcim_ref, kr_ref, ki_ref, cr_ref, ci_ref,
             accr_ref, acci_ref):
        i = pl.program_id(1)

        @pl.when(i == 0)
        def _():
            for ref in (cr_ref, ci_ref, accr_ref, acci_ref, dd_ref, dbre_ref, dbim_ref, dcre_ref, dcim_ref):
                ref[...] = jnp.zeros_like(ref)

        dyv = dy_ref[...]
        dyb = dyv.astype(bf16)
        uv = u_ref[...]
        kr_ref[...] = _dot_nt(dyb, cre_ref[...])
        ki_ref[...] = _dot_nt(dyb, cim_ref[...])
        steps, pr, pi = _power_tables(ar_ref[...], -ai_ref[...])
        rows = lax.broadcasted_iota(jnp.int32, (SUBLANES, ns), 0)
        qr = jnp.zeros((SUBLANES, ns), f32)
        qi = jnp.zeros((SUBLANES, ns), f32)
        for r in range(SUBLANES):
            qr = jnp.where(rows == r, pr[SUBLANES - 1 - r:SUBLANES - r, :], qr)
            qi = jnp.where(rows == r, pi[SUBLANES - 1 - r:SUBLANES - r, :], qi)
        nb = tt // SUBLANES

        def blk(t, carry):
            cr, ci, accr, acci = carry
            sl = pl.ds(pl.multiple_of((nb - 1 - t) * SUBLANES, SUBLANES), SUBLANES)
            xr, xi = kr_ref[sl, :], ki_ref[sl, :]
            for d, (sr_, si_) in zip((1, 2, 4), steps):
                keep = rows < SUBLANES - d
                zr = jnp.where(keep, pltpu.roll(xr, SUBLANES - d, axis=0), 0.0)
                zi = jnp.where(keep, pltpu.roll(xi, SUBLANES - d, axis=0), 0.0)
                mr, mi = _cmul(sr_, si_, zr, zi)
                xr, xi = xr + mr, xi + mi
            mr, mi = _cmul(qr, qi, cr, ci)
            xr, xi = xr + mr, xi + mi
            kr_ref[sl, :] = xr
            ki_ref[sl, :] = xi
            last = rows == SUBLANES - 1
            nr = jnp.where(last, cr, pltpu.roll(xr, SUBLANES - 1, axis=0))
            ni = jnp.where(last, ci, pltpu.roll(xi, SUBLANES - 1, axis=0))
            hr, hi = hr_ref[sl, :], hi_ref[sl, :]
            accr = accr + nr * hr + ni * hi
            acci = acci + ni * hr - nr * hi
            return xr[:1, :], xi[:1, :], accr, acci

        cr, ci, accr, acci = lax.fori_loop(0, nb, blk, (cr_ref[...], ci_ref[...], accr_ref[...], acci_ref[...]))
        cr_ref[...] = cr
        ci_ref[...] = ci
        accr_ref[...] = accr
        acci_ref[...] = acci
        kr, ki = kr_ref[...].astype(bf16), ki_ref[...].astype(bf16)
        du = _dot_nt(kr, bre_ref[...]) + _dot_nt(ki, bim_ref[...]) + d_ref[...] * dyv
        du_ref[...] = du.astype(du_ref.dtype)
        dd_ref[...] += jnp.sum(dyv * uv.astype(f32), axis=0, keepdims=True)
        dbre_ref[...] += _dot_tn(uv, kr)
        dbim_ref[...] += _dot_tn(uv, ki)
        dcre_ref[...] += _dot_tn(hr_ref[...].astype(bf16), dyb)
        dcim_ref[...] += _dot_tn(hi_ref[...].astype(bf16), dyb)

        @pl.when(i == nt - 1)
        def _():
            dar_ref[...] = jnp.sum(accr_ref[...], axis=0, keepdims=True)
            dai_ref[...] = jnp.sum(acci_ref[...], axis=0, keepdims=True)

    lane = pl.BlockSpec((tt, HEAD), lambda j, i: (nt - 1 - i, j))
    st = pl.BlockSpec((tt, ns), lambda j, i: (nt - 1 - i, j))
    b3 = pl.BlockSpec((None, HEAD, ns), lambda j, i: (j, 0, 0))
    c3 = pl.BlockSpec((None, ns, HEAD), lambda j, i: (j, 0, 0))
    arow = pl.BlockSpec((1, ns), lambda j, i: (0, j))
    drow = pl.BlockSpec((1, HEAD), lambda j, i: (0, j))
    return pl.pallas_call(
        body, name="ssm_bwd", grid=(nj, nt),
        in_specs=[lane, lane, st, st, b3, b3, c3, c3, arow, arow, drow],
        out_specs=[lane, drow, arow, arow, b3, b3, c3, c3],
        out_shape=[jax.ShapeDtypeStruct((l, w), bf16), jax.ShapeDtypeStruct((1, w), f32),
                   jax.ShapeDtypeStruct((1, nj * ns), f32), jax.ShapeDtypeStruct((1, nj * ns), f32),
                   jax.ShapeDtypeStruct((nj, HEAD, ns), f32), jax.ShapeDtypeStruct((nj, HEAD, ns), f32),
                   jax.ShapeDtypeStruct((nj, ns, HEAD), f32), jax.ShapeDtypeStruct((nj, ns, HEAD), f32)],
        scratch_shapes=[pltpu.VMEM((tt, ns), f32), pltpu.VMEM((tt, ns), f32), pltpu.VMEM((1, ns), f32),
                        pltpu.VMEM((1, ns), f32), pltpu.VMEM((SUBLANES, ns), f32), pltpu.VMEM((SUBLANES, ns), f32)],
        compiler_params=_cparams("parallel", "arbitrary"),
    )(dy, u, h_re, h_im, bre3, bim3, cre3, cimn3, a_re, a_im, d_skip)


def glu_fwd(y, z_src, w_glu, b_glu):
    l, w = y.shape
    tm = _row_tile(l)

    def body(y_ref, z_ref, w_ref, b_ref, g_ref, t_ref, o_ref):
        g = _gelu(y_ref[...])
        gb = g.astype(bf16)
        t = _dot(gb, w_ref[...]) + b_ref[...]
        g_ref[...] = gb
        t_ref[...] = t
        o_ref[...] = (g * jax.nn.sigmoid(t) * _silu(z_ref[...].astype(f32))).astype(o_ref.dtype)

    blk = pl.BlockSpec((tm, w), lambda i: (i, 0))
    return pl.pallas_call(
        body, name="glu_fwd", grid=(l // tm,),
        in_specs=[blk, pl.BlockSpec((tm, w), lambda i: (i, 1)), pl.BlockSpec((w, w), lambda i: (0, 0)), _row(w)],
        out_specs=[blk, blk, blk],
        out_shape=[jax.ShapeDtypeStruct((l, w), bf16), jax.ShapeDtypeStruct((l, w), f32),
                   jax.ShapeDtypeStruct((l, w), bf16)],
        compiler_params=_cparams("parallel"),
    )(y, z_src, w_glu, b_glu)


def glu_bwd(dout, y, t, z_src, w_glu):
    l, w = y.shape
    tm = _row_tile(l)

    def body(do_ref, y_ref, t_ref, z_ref, w_ref, dy_ref, dz_ref, dt_ref, db_ref):
        @pl.when(pl.program_id(0) == 0)
        def _():
            db_ref[...] = jnp.zeros_like(db_ref)

        yv, zv, dov = y_ref[...], z_ref[...].astype(f32), do_ref[...]
        g = _gelu(yv)
        sg = jax.nn.sigmoid(t_ref[...])
        dy2 = dov * _silu(zv)
        dz_ref[...] = (dov * g * sg * _silu_grad(zv)).astype(dz_ref.dtype)
        dt = dy2 * g * sg * (1.0 - sg)
        dtb = dt.astype(bf16)
        dt_ref[...] = dtb
        db_ref[...] += jnp.sum(dt, axis=0, keepdims=True)
        dg = dy2 * sg + _dot_nt(dtb, w_ref[...])
        dy_ref[...] = dg * _gelu_grad(yv)

    blk = pl.BlockSpec((tm, w), lambda i: (i, 0))
    return pl.pallas_call(
        body, name="glu_bwd", grid=(l // tm,),
        in_specs=[blk, blk, blk, pl.BlockSpec((tm, w), lambda i: (i, 1)), pl.BlockSpec((w, w), lambda i: (0, 0))],
        out_specs=[blk, blk, blk, _row(w)],
        out_shape=[jax.ShapeDtypeStruct((l, w), f32), jax.ShapeDtypeStruct((l, w), bf16),
                   jax.ShapeDtypeStruct((l, w), bf16), jax.ShapeDtypeStruct((1, w), f32)],
        compiler_params=_cparams("arbitrary"),
    )(dout, y, t, z_src, w_glu)


def _adamw(w, g, m, v):
    m = ADAM_B1 * m + (1.0 - ADAM_B1) * g
    v = ADAM_B2 * v + (1.0 - ADAM_B2) * (g * g)
    m_hat = m / (1.0 - ADAM_B1 ** ADAM_STEP)
    v_hat = v / (1.0 - ADAM_B2 ** ADAM_STEP)
    return -ADAM_LR * (m_hat / (jnp.sqrt(v_hat) + ADAM_EPS) + ADAM_WD * w), m, v


def adam_reduce(pieces, w, m, v, name):
    r, c = w.shape
    n = pieces.shape[0]
    tr = _tile(r, (256, 128, 64, 32, 16, 8))

    def body(p_ref, w_ref, m_ref, v_ref, g_ref, d_ref, nm_ref, nv_ref):
        g = p_ref[0].astype(f32)
        for s in range(1, n):
            g = g + p_ref[s].astype(f32)
        g_ref[...] = g
        d_ref[...], nm_ref[...], nv_ref[...] = _adamw(w_ref[...], g, m_ref[...], v_ref[...])

    blk = pl.BlockSpec((tr, c), lambda i: (i, 0))
    return pl.pallas_call(
        body, name=name, grid=(r // tr,),
        in_specs=[pl.BlockSpec((n, tr, c), lambda i: (0, i, 0)), blk, blk, blk],
        out_specs=[blk] * 4, out_shape=[jax.ShapeDtypeStruct((r, c), f32)] * 4,
        compiler_params=_cparams("parallel"),
    )(pieces, w, m, v)


def adam_w_mod(cond_t, dm, w, m, v):
    nl, d, cols = w.shape
    tr = _tile(d, (512, 256, 128))

    def body(c_ref, dm_ref, w_ref, m_ref, v_ref, g_ref, d_ref, nm_ref, nv_ref):
        g = jnp.dot(c_ref[...], dm_ref[...], preferred_element_type=f32, precision=lax.Precision.HIGHEST)
        g_ref[...] = g
        d_ref[...], nm_ref[...], nv_ref[...] = _adamw(w_ref[...], g, m_ref[...], v_ref[...])

    blk = pl.BlockSpec((None, tr, cols), lambda l, i: (l, i, 0))
    return pl.pallas_call(
        body, name="adam_w_mod", grid=(nl, d // tr),
        in_specs=[pl.BlockSpec((tr, N_DEV), lambda l, i: (i, 0)), pl.BlockSpec((None, N_DEV, cols), lambda l, i: (l, 0, 0)),
                  blk, blk, blk],
        out_specs=[blk] * 4, out_shape=[jax.ShapeDtypeStruct((nl, d, cols), f32)] * 4,
        compiler_params=_cparams("parallel", "parallel"),
    )(cond_t, dm, w, m, v)


def silu_rows(c_all):
    def body(c_ref, o_ref):
        o_ref[...] = _silu(c_ref[...])

    return pl.pallas_call(body, name="silu_rows", out_shape=jax.ShapeDtypeStruct(c_all.shape, f32))(c_all)


def _block_diag(x):
    g, a, b = x.shape
    nj = g // GROUPS_PER_LANE_BLOCK
    eye = jnp.eye(GROUPS_PER_LANE_BLOCK, dtype=x.dtype)
    x5 = x.reshape(nj, GROUPS_PER_LANE_BLOCK, a, b)
    return jnp.einsum("jgab,gh->jgahb", x5, eye).reshape(nj, GROUPS_PER_LANE_BLOCK * a, GROUPS_PER_LANE_BLOCK * b)


def _diag_blocks(x, a, b):
    nj = x.shape[0]
    x5 = x.reshape(nj, GROUPS_PER_LANE_BLOCK, a, GROUPS_PER_LANE_BLOCK, b)
    eye = jnp.eye(GROUPS_PER_LANE_BLOCK, dtype=x.dtype)
    return jnp.einsum("jgahb,gh->jgab", x5, eye).reshape(nj * GROUPS_PER_LANE_BLOCK, a, b)


PACK_ROW = SUBLANES * HEAD


def _pack(parts, row_multiple=SUBLANES):
    rows = []
    for p in parts:
        flat = p.reshape(-1)
        pad = (-flat.shape[0]) % PACK_ROW
        if pad:
            flat = jnp.concatenate([flat, jnp.zeros((pad,), flat.dtype)])
        rows.append(flat.reshape(-1, HEAD))
    pad = (-sum(r.shape[0] for r in rows)) % row_multiple
    if pad:
        rows.append(jnp.zeros((pad, HEAD), rows[0].dtype))
    return jnp.concatenate(rows, axis=0)


def _unpack(packed, shapes):
    out, r0 = [], 0
    for shp in shapes:
        n = math.prod(shp)
        nr = -(-n // PACK_ROW) * SUBLANES
        out.append(packed[r0:r0 + nr].reshape(-1)[:n].reshape(shp))
        r0 += nr
    return out


def adam_small(g, w, m, v):
    r, c = w.shape

    def body(g_ref, w_ref, m_ref, v_ref, d_ref, nm_ref, nv_ref):
        d_ref[...], nm_ref[...], nv_ref[...] = _adamw(w_ref[...], g_ref[...], m_ref[...], v_ref[...])

    tr = max(t for t in range(SUBLANES, 1024 + 1, SUBLANES) if r % t == 0)
    blk = pl.BlockSpec((tr, c), lambda i: (i, 0))
    return pl.pallas_call(
        body, name="adam_small", grid=(r // tr,),
        in_specs=[blk] * 4, out_specs=[blk] * 3, out_shape=[jax.ShapeDtypeStruct((r, c), f32)] * 3,
        compiler_params=_cparams("parallel"),
    )(g, w, m, v)


def kernel(x, c, ln_pre_g, ln_post_g, w_mod, b_mod, w_in_ab, w_out_ab, sgu_norm_g, sgu_w, sgu_b, w_in_ssm, w_out_ssm, lam_re, lam_im, b_re, b_im, c_re, c_im, d_skip, log_dt, w_glu, b_glu, loss_target, m_ln_pre_g, m_ln_post_g, m_w_mod, m_b_mod, m_w_in_ab, m_w_out_ab, m_sgu_norm_g, m_sgu_w, m_sgu_b, m_w_in_ssm, m_w_out_ssm, m_lam_re, m_lam_im, m_b_re, m_b_im, m_c_re, m_c_im, m_d_skip, m_log_dt, m_w_glu, m_b_glu, v_ln_pre_g, v_ln_post_g, v_w_mod, v_b_mod, v_w_in_ab, v_w_out_ab, v_sgu_norm_g, v_sgu_w, v_sgu_b, v_w_in_ssm, v_w_out_ssm, v_lam_re, v_lam_im, v_b_re, v_b_im, v_c_re, v_c_im, v_d_skip, v_log_dt, v_w_glu, v_b_glu):
    me = _my_index()
    x0 = x[0]
    l, d = x0.shape
    target = loss_target[0]
    nh = sgu_w.shape[1]
    wa = nh * HEAD
    n_grp, n_st = lam_re.shape[1], lam_re.shape[2]
    mod_cols = w_mod.shape[2]

    c_all, d_skip_all, b_glu_all = all_gather([c, d_skip, b_glu], "gather_c")
    c_all = c_all.reshape(N_DEV, d)
    d_skip_all = d_skip_all.reshape(1, -1)
    b_glu_all = b_glu_all.reshape(1, -1)

    b_cols = lax.dynamic_slice_in_dim(b_mod, me * mod_cols, mod_cols, axis=1)
    (mod_all,) = all_gather([mod_part(c_all, w_mod, b_cols)], "gather_mod")
    def after(a, first):
        return a + jnp.minimum(jnp.abs(first.reshape(-1)[0].astype(f32)), 0.0).astype(a.dtype)

    (win_ab3,) = sequencer_exchange(GATHER, [after(w_in_ab[0], mod_all).astype(bf16)], "gather_w_in", 1)
    mod_mine = lax.dynamic_index_in_dim(mod_all, me, axis=2, keepdims=False)
    mod_rows = jnp.transpose(mod_mine, (1, 0, 2)).reshape(2, 3, 1, d)

    def rows(a, i):
        return a[i].reshape(1, d)

    shift0, scale0, gate0 = mod_rows[0, 0], mod_rows[0, 1], mod_rows[0, 2]
    h0 = prenorm_fwd(x0, rows(ln_pre_g, 0), shift0, scale0, "prenorm0")
    wout_ab3, win_ssm3, wout_ssm3, wglu = sequencer_exchange(
        GATHER, [after(w, win_ab3).astype(bf16) for w in (w_out_ab[0], w_in_ssm[0], w_out_ssm[0], w_glu[0])],
        "gather_w_rest", 2)
    proj0 = mm_nn(h0, win_ab3, bf16, "proj0")
    sgu_b3 = sgu_b[0].reshape(nh, HEAD, 1)
    out_a = sgu_fwd(proj0, sgu_norm_g, sgu_w[0], sgu_b3)
    out_b, att, tot = sb_fwd(proj0, nh)
    cat = jnp.concatenate([out_a, out_b], axis=1)
    wout_ab3 = wout_ab3.reshape(1, d, d)
    win_ssm3 = win_ssm3.reshape(1, d, d)
    wglu = wglu.reshape(w_glu.shape[2], w_glu.shape[2])
    y0 = mm_nn(cat, wout_ab3, f32, "out0")
    x1 = post_fwd(x0, y0, gate0, rows(ln_post_g, 0), "post0")

    shift1, scale1, gate1 = mod_rows[1, 0], mod_rows[1, 1], mod_rows[1, 2]
    h1 = prenorm_fwd(x1, rows(ln_pre_g, 1), shift1, scale1, "prenorm1")
    proj1 = mm_nn(h1, win_ssm3, bf16, "proj1")
    w_ssm = proj1.shape[1] // 2
    ldt = log_dt[0].reshape(n_grp, 1)
    bt_re = jnp.transpose(b_re[0], (0, 2, 1))
    bt_im = jnp.transpose(b_im[0], (0, 2, 1))
    a_re, a_im, bbt_re, bbt_im = s5_params_fwd(lam_re[0], lam_im[0], ldt, bt_re, bt_im)
    bre3 = _block_diag(bbt_re).astype(bf16)
    bim3 = _block_diag(bbt_im).astype(bf16)
    cre3 = _block_diag(jnp.transpose(c_re[0], (0, 2, 1))).astype(bf16)
    cimn3 = _block_diag(-jnp.transpose(c_im[0], (0, 2, 1))).astype(bf16)
    a_re_row, a_im_row = a_re.reshape(1, -1), a_im.reshape(1, -1)
    u = proj1[:, :w_ssm]
    y_ssm, hs_re, hs_im = ssm_fwd(u, bre3, bim3, cre3, cimn3, a_re_row, a_im_row, d_skip_all)
    g_act, t_glu, mix1 = glu_fwd(y_ssm, proj1, wglu, b_glu_all)
    y1 = mm_nn(mix1, wout_ssm3, f32, "out1")

    dx2, loss_tile = final_loss(x1, y1, gate1, rows(ln_post_g, 1), target)

    dy1, dgate1, dgpost1 = post_bwd(dx2, y1, gate1, rows(ln_post_g, 1), "post1_bwd")
    dmix1 = mm_nt(dy1, wout_ssm3, f32, "dmix1")
    gw_out_ssm = mm_tn(mix1, dy1, N_DEV, bf16, "gw_out_ssm")
    (p_out_ssm,) = sequencer_exchange(SCATTER, [gw_out_ssm], "scatter_g1", 3)
    dy_ssm, dz1, dt_glu, db_glu = glu_bwd(dmix1, y_ssm, t_glu, proj1, wglu)
    gw_glu = mm_tn(g_act, dt_glu, 1, bf16, "gw_glu").reshape(N_DEV, -1, w_ssm)
    du, dd_skip, da_re, da_im, dbre3, dbim3, dcre3, dcimn3 = ssm_bwd(
        dy_ssm, u, hs_re, hs_im, bre3, bim3, cre3, cimn3, a_re_row, a_im_row, d_skip_all)
    dproj1 = jnp.concatenate([du, dz1], axis=1)
    gw_in_ssm = mm_tn(h1, dproj1, 1, bf16, "gw_in_ssm").reshape(N_DEV, -1, proj1.shape[1])
    p_in_ssm, p_glu = sequencer_exchange(SCATTER, [gw_in_ssm, gw_glu], "scatter_g2", 4)
    dh1 = mm_nt(dproj1, win_ssm3, f32, "dh1")
    dx1, dshift1, dscale1, dgpre1 = prenorm_bwd(dh1, x1, dx2, rows(ln_pre_g, 1), scale1, "prenorm1_bwd")
    dlr, dli, dldt, dbt_re, dbt_im = s5_params_bwd(
        lam_re[0], lam_im[0], ldt, bt_re, bt_im, da_re.reshape(n_grp, n_st), da_im.reshape(n_grp, n_st),
        _diag_blocks(dbre3, SSM_GROUP, n_st), _diag_blocks(dbim3, SSM_GROUP, n_st))
    g_b_re = jnp.transpose(dbt_re, (0, 2, 1))
    g_b_im = jnp.transpose(dbt_im, (0, 2, 1))
    g_c_re = jnp.transpose(_diag_blocks(dcre3, n_st, SSM_GROUP), (0, 2, 1))
    g_c_im = -jnp.transpose(_diag_blocks(dcimn3, n_st, SSM_GROUP), (0, 2, 1))

    dy0, dgate0, dgpost0 = post_bwd(dx1, y0, gate0, rows(ln_post_g, 0), "post0_bwd")
    dcat = mm_nt(dy0, wout_ab3, f32, "dcat")
    gw_out_ab = mm_tn(cat, dy0, 1, bf16, "gw_out_ab").reshape(N_DEV, -1, d)
    (p_out_ab,) = sequencer_exchange(SCATTER, [gw_out_ab], "scatter_g3", 5)
    da, dsgu_w, dsgu_b, dsgu_ng = sgu_bwd(proj0, dcat, sgu_norm_g, sgu_w[0], sgu_b3)
    dq, dk, dv, dbz = sb_bwd(proj0, dcat, att, tot, nh)
    dproj0 = jnp.concatenate([da, dq, dk, dv, dbz], axis=1)
    gw_in_ab = mm_tn(h0, dproj0, N_DEV, bf16, "gw_in_ab")
    (p_in_ab,) = sequencer_exchange(SCATTER, [gw_in_ab], "scatter_g4", 6)
    dh0 = mm_nt(dproj0, win_ab3, f32, "dh0")
    dx0, dshift0, dscale0, dgpre0 = prenorm_bwd(dh0, x0, dx1, rows(ln_pre_g, 0), scale0, "prenorm0_bwd")

    small_names = ["ln_pre_g", "ln_post_g", "b_mod", "sgu_norm_g", "sgu_w", "sgu_b", "lam_re", "lam_im", "b_re", "b_im",
                   "c_re", "c_im", "log_dt"]
    small_w = [ln_pre_g, ln_post_g, b_mod, sgu_norm_g, sgu_w, sgu_b, lam_re, lam_im, b_re, b_im, c_re, c_im, log_dt]
    small_m = [m_ln_pre_g, m_ln_post_g, m_b_mod, m_sgu_norm_g, m_sgu_w, m_sgu_b, m_lam_re, m_lam_im, m_b_re, m_b_im,
               m_c_re, m_c_im, m_log_dt]
    small_v = [v_ln_pre_g, v_ln_post_g, v_b_mod, v_sgu_norm_g, v_sgu_w, v_sgu_b, v_lam_re, v_lam_im, v_b_re, v_b_im,
               v_c_re, v_c_im, v_log_dt]
    dmod = jnp.concatenate([dshift0, dscale0, dgate0, dshift1, dscale1, dgate1], axis=1)
    small_g = [jnp.concatenate([dgpre0, dgpre1]), jnp.concatenate([dgpost0, dgpost1]), dmod, dsgu_ng, dsgu_w, dsgu_b,
               dlr, dli, g_b_re, g_b_im, g_c_re, g_c_im, dldt]
    shapes = [w.shape for w in small_w]
    g_sum, dmod_all = all_reduce_rows(_pack(small_g + [dd_skip, db_glu, loss_tile], SUBLANES * N_DEV), dmod,
                                      "reduce_small_grads")
    n_rows_small = sum(-(-math.prod(s) // PACK_ROW) * SUBLANES for s in shapes)
    loss = g_sum[n_rows_small + 2 * (d_skip_all.shape[1] // HEAD), 0] * (0.5 / d)
    new_small = adam_small(g_sum, _pack(small_w), _pack(small_m), _pack(small_v))
    r_small = [_unpack(o, shapes) for o in [g_sum[:n_rows_small]] + list(new_small)]
    small = {n: [r_small[k][i] for k in range(4)] for i, n in enumerate(small_names)}
    vec_rows = d_skip_all.shape[1] // HEAD

    def my_columns(r0):
        whole = g_sum[r0:r0 + vec_rows].reshape(1, 1, -1)
        return lax.dynamic_slice_in_dim(whole, me * d_skip.shape[1], d_skip.shape[1], axis=2)

    def sharded(p, w, m, v, name):
        shp = w.shape
        w2, m2, v2 = (a.reshape(-1, shp[-1]) for a in (w, m, v))
        return [o.reshape(shp) for o in adam_reduce(p.reshape(p.shape[0], -1, shp[-1]), w2, m2, v2, name)]

    r_d_skip = sharded(my_columns(n_rows_small), d_skip, m_d_skip, v_d_skip, "adam_d_skip")
    r_b_glu = sharded(my_columns(n_rows_small + vec_rows), b_glu, m_b_glu, v_b_glu, "adam_b_glu")
    r_w_out_ssm = sharded(p_out_ssm, w_out_ssm, m_w_out_ssm, v_w_out_ssm, "adam_w_out_ssm")
    r_w_in_ssm = sharded(p_in_ssm, w_in_ssm, m_w_in_ssm, v_w_in_ssm, "adam_w_in_ssm")
    r_w_glu = sharded(p_glu, w_glu, m_w_glu, v_w_glu, "adam_w_glu")
    r_w_out_ab = sharded(p_out_ab, w_out_ab, m_w_out_ab, v_w_out_ab, "adam_w_out_ab")
    r_w_in_ab = sharded(p_in_ab, w_in_ab, m_w_in_ab, v_w_in_ab, "adam_w_in_ab")

    dm_cols = jnp.transpose(
        lax.dynamic_slice_in_dim(dmod_all.reshape(N_DEV, 2, 3 * d), me * mod_cols, mod_cols, axis=2), (1, 0, 2))
    cond_t = jnp.transpose(silu_rows(c_all))
    r_w_mod = adam_w_mod(cond_t, dm_cols, w_mod, m_w_mod, v_w_mod)

    res = dict(small)
    res.update(w_mod=r_w_mod, w_in_ab=r_w_in_ab, w_out_ab=r_w_out_ab, w_in_ssm=r_w_in_ssm, w_out_ssm=r_w_out_ssm,
               d_skip=r_d_skip, w_glu=r_w_glu, b_glu=r_b_glu)
    order = ["ln_pre_g", "ln_post_g", "w_mod", "b_mod", "w_in_ab", "w_out_ab", "sgu_norm_g", "sgu_w", "sgu_b", "w_in_ssm",
             "w_out_ssm", "lam_re", "lam_im", "b_re", "b_im", "c_re", "c_im", "d_skip", "log_dt", "w_glu", "b_glu"]
    outs = [loss, dx0.reshape(x.shape)]
    for k in range(4):
        outs += [res[n][k] for n in order]
    return tuple(outs)
```

```python
import functools
import math

import jax
import jax.numpy as jnp
from jax import lax
from jax.experimental import pallas as pl
from jax.experimental.pallas import tpu as pltpu
from jax.experimental.pallas import tpu_sc as plsc

f32 = jnp.float32
bf16 = jnp.bfloat16

N_DEV = 8
EPS = 1e-6
HEAD = 128
SUBLANES = 8
SSM_GROUP = 16
SSM_STATE = 64
GROUPS_PER_LANE_BLOCK = HEAD // SSM_GROUP
STATES_PER_LANE_BLOCK = GROUPS_PER_LANE_BLOCK * SSM_STATE
VMEM_LIMIT = 56 * 2 ** 20
ADAM_LR, ADAM_B1, ADAM_B2, ADAM_EPS, ADAM_WD, ADAM_STEP = 0.001, 0.9, 0.999, 1e-08, 0.01, 10
_GELU_C0 = math.sqrt(2.0 / math.pi)
_GELU_C1 = 0.044715
MESH = pl.DeviceIdType.MESH


def _cparams(*sem):
    return pltpu.CompilerParams(dimension_semantics=sem if sem else None, vmem_limit_bytes=VMEM_LIMIT)


def _gelu(x):
    return 0.5 * x * (1.0 + jnp.tanh(_GELU_C0 * (x + _GELU_C1 * x * x * x)))


def _gelu_grad(x):
    t = jnp.tanh(_GELU_C0 * (x + _GELU_C1 * x * x * x))
    return 0.5 * (1.0 + t) + 0.5 * x * (1.0 - t * t) * _GELU_C0 * (1.0 + 3.0 * _GELU_C1 * x * x)


def _silu(x):
    return x * jax.nn.sigmoid(x)


def _silu_grad(x):
    s = jax.nn.sigmoid(x)
    return s * (1.0 + x * (1.0 - s))


def _dot(a, b):
    return jnp.dot(a, b, preferred_element_type=f32)


def _dot_nt(a, b):
    return lax.dot_general(a, b, (((1,), (1,)), ((), ())), preferred_element_type=f32)


def _dot_tn(a, b):
    return lax.dot_general(a, b, (((0,), (0,)), ((), ())), preferred_element_type=f32)


def _split_bf16(v):
    hi = v.astype(bf16)
    lo = (v - hi.astype(f32)).astype(bf16)
    return hi, lo


def _row(d):
    return pl.BlockSpec((1, d), lambda *_: (0, 0))


def _my_index():
    return 4 * lax.axis_index("x") + 2 * lax.axis_index("y") + lax.axis_index("c")


def _peer(k):
    x, y, c = lax.axis_index("x"), lax.axis_index("y"), lax.axis_index("c")
    return (1 - x if k & 4 else x, 1 - y if k & 2 else y, 1 - c if k & 1 else c)


def all_gather(arrs, name):
    n = len(arrs)

    def body(*refs):
        ins, outs = refs[:n], refs[n:2 * n]
        send, recv, local = refs[2 * n:]
        me = _my_index()
        copies = []
        for a in range(n):
            cp = pltpu.make_async_copy(ins[a], outs[a].at[me], local.at[a])
            cp.start()
            copies.append(cp)
            for k in range(1, N_DEV):
                s = a * (N_DEV - 1) + k - 1
                cp = pltpu.make_async_remote_copy(src_ref=ins[a], dst_ref=outs[a].at[me], send_sem=send.at[s],
                                                  recv_sem=recv.at[s], device_id=_peer(k), device_id_type=MESH)
                cp.start()
                copies.append(cp)
        for cp in copies:
            cp.wait()

    any_spec = pl.BlockSpec(memory_space=pl.ANY)
    outs = pl.pallas_call(
        body, name=name,
        out_shape=[jax.ShapeDtypeStruct((N_DEV,) + a.shape, a.dtype) for a in arrs],
        in_specs=[any_spec] * n, out_specs=[any_spec] * n,
        scratch_shapes=[pltpu.SemaphoreType.DMA((n * (N_DEV - 1),)), pltpu.SemaphoreType.DMA((n * (N_DEV - 1),)),
                        pltpu.SemaphoreType.DMA((n,))],
        compiler_params=pltpu.CompilerParams(has_side_effects=True),
    )(*arrs)
    return list(outs)


def all_reduce_rows(pack, extra, name):
    r, c = pack.shape
    rs = r // N_DEV
    n_peer = N_DEV - 1

    def body(p_ref, x_ref, o_ref, xo_ref, land, red, send1, recv1, send2, recv2, sendx, recvx, local):
        me = _my_index()

        def rows(i):
            return pl.ds(pl.multiple_of(i * rs, SUBLANES), rs)

        own = [pltpu.make_async_copy(p_ref.at[rows(me)], land.at[me], local.at[0]),
               pltpu.make_async_copy(x_ref, xo_ref.at[me], local.at[1])]
        first = []
        for k in range(1, N_DEV):
            first.append(pltpu.make_async_remote_copy(
                src_ref=p_ref.at[rows(jnp.bitwise_xor(me, k))], dst_ref=land.at[me], send_sem=send1.at[k - 1],
                recv_sem=recv1.at[k - 1], device_id=_peer(k), device_id_type=MESH))
            first.append(pltpu.make_async_remote_copy(
                src_ref=x_ref, dst_ref=xo_ref.at[me], send_sem=sendx.at[k - 1], recv_sem=recvx.at[k - 1],
                device_id=_peer(k), device_id_type=MESH))
        for cp in own + first:
            cp.start()
        for cp in own + first:
            cp.wait()
        acc = land[0]
        for s in range(1, N_DEV):
            acc = acc + land[s]
        red[...] = acc
        mine = pltpu.make_async_copy(red, o_ref.at[rows(me)], local.at[2])
        second = [pltpu.make_async_remote_copy(
            src_ref=red, dst_ref=o_ref.at[rows(me)], send_sem=send2.at[k - 1], recv_sem=recv2.at[k - 1],
            device_id=_peer(k), device_id_type=MESH) for k in range(1, N_DEV)]
        for cp in [mine] + second:
            cp.start()
        for cp in [mine] + second:
            cp.wait()

    any_spec = pl.BlockSpec(memory_space=pl.ANY)
    return pl.pallas_call(
        body, name=name,
        out_shape=[jax.ShapeDtypeStruct((r, c), pack.dtype), jax.ShapeDtypeStruct((N_DEV,) + extra.shape, extra.dtype)],
        in_specs=[any_spec, any_spec], out_specs=[any_spec, any_spec],
        scratch_shapes=[pltpu.VMEM((N_DEV, rs, c), pack.dtype), pltpu.VMEM((rs, c), pack.dtype)]
        + [pltpu.SemaphoreType.DMA((n_peer,))] * 6 + [pltpu.SemaphoreType.DMA((3,))],
        compiler_params=pltpu.CompilerParams(has_side_effects=True),
    )(pack, extra)


GATHER, SCATTER = "gather", "scatter"


def _exchange_copies(srcs, lands, send, recv):
    me = _my_index()
    copies = []
    for a, (src, land) in enumerate(zip(srcs, lands)):
        for k in range(1, N_DEV):
            s = a * (N_DEV - 1) + k - 1
            copies.append(pltpu.make_async_remote_copy(
                src_ref=src.at[jnp.bitwise_xor(me, k)], dst_ref=land.at[me],
                send_sem=send.at[s], recv_sem=recv.at[s], device_id=_peer(k), device_id_type=MESH))
    return copies


def sequencer_exchange(kind, arrs, name, collective_id):
    n = len(arrs)
    n_sem = n * (N_DEV - 1)
    land_shapes = [((N_DEV,) + a.shape if kind == GATHER else a.shape) for a in arrs]
    srcs = [jax.new_ref(a, memory_space=pltpu.MemorySpace.HBM) for a in arrs]
    lands = [jax.empty_ref(jax.ShapeDtypeStruct(s, a.dtype), memory_space=pltpu.MemorySpace.HBM)
             for s, a in zip(land_shapes, arrs)]

    @pl.kernel(mesh=plsc.ScalarSubcoreMesh(axis_name="sequencer", num_cores=1), name=name,
               scratch_types=(pltpu.SemaphoreType.DMA((n_sem,)), pltpu.SemaphoreType.DMA((n_sem,)),
                              pltpu.SemaphoreType.DMA((n,))),
               compiler_params=pltpu.CompilerParams(collective_id=collective_id))
    def launch(send, recv, local):
        barrier = pltpu.get_barrier_semaphore()
        for k in range(1, N_DEV):
            pl.semaphore_signal(barrier, inc=1, device_id=_peer(k), device_id_type=MESH)
        pl.semaphore_wait(barrier, N_DEV - 1)
        me = _my_index()
        mine = [pltpu.make_async_copy(src if kind == GATHER else src.at[me], land.at[me], local.at[a])
                for a, (src, land) in enumerate(zip(srcs, lands))]
        if kind == SCATTER:
            copies = mine + _exchange_copies(srcs, lands, send, recv)
            for cp in copies:
                cp.start()
            for cp in copies:
                cp.wait()
            return

        def block_copy(a, slot, block, k, src=None):
            s = a * (N_DEV - 1) + slot
            return pltpu.make_async_remote_copy(
                src_ref=lands[a].at[block] if src is None else src, dst_ref=lands[a].at[block],
                send_sem=send.at[s], recv_sem=recv.at[s], device_id=_peer(k), device_id_type=MESH)

        chips = (2, 4, 6)
        sibling = jnp.bitwise_xor(me, 1)
        first = [block_copy(a, slot, me, k, src=srcs[a]) for a in range(n) for slot, k in enumerate((1,) + chips)]
        for cp in mine + first:
            cp.start()
        passed = []
        for a in range(n):
            for i, k in enumerate(chips):
                block = jnp.bitwise_xor(me, k)
                block_copy(a, 1 + i, block, k).wait_recv()
                passed.append(block_copy(a, 4 + i, block, 1))
                passed[-1].start()
        for a in range(n):
            block_copy(a, 0, sibling, 1).wait_recv()
            for i, k in enumerate(chips):
                block_copy(a, 4 + i, jnp.bitwise_xor(sibling, k), 1).wait_recv()
        for cp in mine:
            cp.wait()
        for cp in first + passed:
            cp.wait_send()

    launch()
    return [land[...] for land in lands]


def _tile(n, pref):
    for t in pref:
        if n % t == 0:
            return t
    return n


def mm_nn(a, b3, out_dtype, name):
    m, k = a.shape
    nb, _, bn = b3.shape
    tm = _tile(m, (512, 256, 128))
    tn = _tile(bn, (1024, 896, 512, 256, 128))
    per = bn // tn

    def body(a_ref, b_ref, o_ref):
        o_ref[...] = _dot(a_ref[...], b_ref[...]).astype(o_ref.dtype)

    return pl.pallas_call(
        body, name=name, grid=(m // tm, nb, per),
        in_specs=[pl.BlockSpec((tm, k), lambda i, j, jj: (i, 0)),
                  pl.BlockSpec((None, k, tn), lambda i, j, jj: (j, 0, jj))],
        out_specs=pl.BlockSpec((tm, tn), lambda i, j, jj: (i, j * per + jj)),
        out_shape=jax.ShapeDtypeStruct((m, nb * bn), out_dtype),
        compiler_params=_cparams("parallel", "arbitrary", "arbitrary"),
    )(a, b3)


def mm_nt(a, w3, out_dtype, name):
    m, _ = a.shape
    nb, ko, bn = w3.shape
    tm = _tile(m, (512, 256, 128))
    tko = _tile(ko, (1024, 512, 256, 128))

    def body(a_ref, w_ref, o_ref, acc_ref):
        j = pl.program_id(2)

        @pl.when(j == 0)
        def _():
            acc_ref[...] = jnp.zeros_like(acc_ref)

        acc_ref[...] += _dot_nt(a_ref[...], w_ref[...])

        @pl.when(j == nb - 1)
        def _():
            o_ref[...] = acc_ref[...].astype(o_ref.dtype)

    return pl.pallas_call(
        body, name=name, grid=(m // tm, ko // tko, nb),
        in_specs=[pl.BlockSpec((tm, bn), lambda i, o, j: (i, j)),
                  pl.BlockSpec((None, tko, bn), lambda i, o, j: (j, o, 0))],
        out_specs=pl.BlockSpec((tm, tko), lambda i, o, j: (i, o)),
        out_shape=jax.ShapeDtypeStruct((m, ko), out_dtype),
        scratch_shapes=[pltpu.VMEM((tm, tko), f32)],
        compiler_params=_cparams("parallel", "arbitrary", "arbitrary"),
    )(a, w3)


def mm_tn(a, dy, ncb, out_dtype, name):
    l, ka = a.shape
    _, n = dy.shape
    bn = n // ncb
    tl = _tile(l, (1024, 512, 256, 128))
    tka = _tile(ka, (512, 256, 128))
    tn = _tile(bn, (1024, 896, 512, 256, 128))
    per = bn // tn
    nl = l // tl

    def body(a_ref, dy_ref, o_ref, acc_ref):
        s = pl.program_id(2)

        @pl.when(s == 0)
        def _():
            acc_ref[...] = jnp.zeros_like(acc_ref)

        acc_ref[...] += _dot_tn(a_ref[...], dy_ref[...])

        @pl.when(s == nl - 1)
        def _():
            o_ref[...] = acc_ref[...].astype(o_ref.dtype)

    return pl.pallas_call(
        body, name=name, grid=(ka // tka, n // tn, nl),
        in_specs=[pl.BlockSpec((tl, tka), lambda i, j, s: (s, i)),
                  pl.BlockSpec((tl, tn), lambda i, j, s: (s, j))],
        out_specs=pl.BlockSpec((None, tka, tn), lambda i, j, s: (j // per, i, j % per)),
        out_shape=jax.ShapeDtypeStruct((ncb, ka, bn), out_dtype),
        scratch_shapes=[pltpu.VMEM((tka, tn), f32)],
        compiler_params=_cparams("parallel", "parallel", "arbitrary"),
    )(a, dy)


def mod_part(c_all, w_mod, b_cols):
    nl, d, cols = w_mod.shape

    def body(c_ref, w_ref, b_ref, o_ref):
        cond = _silu(c_ref[...]).astype(bf16)
        o_ref[...] = _dot(cond, w_ref[...].astype(bf16)) + b_ref[...]

    return pl.pallas_call(
        body, name="mod_part", grid=(nl,),
        in_specs=[pl.BlockSpec((N_DEV, d), lambda l: (0, 0)),
                  pl.BlockSpec((None, d, cols), lambda l: (l, 0, 0)),
                  pl.BlockSpec((None, 1, cols), lambda l: (l, 0, 0))],
        out_specs=pl.BlockSpec((None, N_DEV, cols), lambda l: (l, 0, 0)),
        out_shape=jax.ShapeDtypeStruct((nl, N_DEV, cols), f32),
        compiler_params=_cparams("arbitrary"),
    )(c_all, w_mod, b_cols.reshape(nl, 1, cols))


def _row_tile(l):
    return _tile(l, (256, 128))


def prenorm_fwd(x, g, shift, scale, name):
    l, d = x.shape
    tm = _row_tile(l)

    def body(x_ref, g_ref, sh_ref, sc_ref, h_ref):
        xv = x_ref[...]
        r = lax.rsqrt(jnp.mean(xv * xv, axis=-1, keepdims=True) + EPS)
        h_ref[...] = (xv * r * (g_ref[...] * (1.0 + sc_ref[...])) + sh_ref[...]).astype(h_ref.dtype)

    return pl.pallas_call(
        body, name=name, grid=(l // tm,),
        in_specs=[pl.BlockSpec((tm, d), lambda i: (i, 0)), _row(d), _row(d), _row(d)],
        out_specs=pl.BlockSpec((tm, d), lambda i: (i, 0)),
        out_shape=jax.ShapeDtypeStruct((l, d), bf16),
        compiler_params=_cparams("parallel"),
    )(x, g, shift, scale)


def post_prenorm_fwd(x, y, gate, g_post, g_pre, shift, scale, name):
    l, d = x.shape
    tm = _row_tile(l)

    def body(x_ref, y_ref, gate_ref, gp_ref, g_ref, sh_ref, sc_ref, o_ref, h_ref):
        yv = y_ref[...]
        r = lax.rsqrt(jnp.mean(yv * yv, axis=-1, keepdims=True) + EPS)
        xv = x_ref[...] + gate_ref[...] * (yv * r * gp_ref[...])
        o_ref[...] = xv
        r = lax.rsqrt(jnp.mean(xv * xv, axis=-1, keepdims=True) + EPS)
        h_ref[...] = (xv * r * (g_ref[...] * (1.0 + sc_ref[...])) + sh_ref[...]).astype(h_ref.dtype)

    blk = pl.BlockSpec((tm, d), lambda i: (i, 0))
    return pl.pallas_call(
        body, name=name, grid=(l // tm,),
        in_specs=[blk, blk] + [_row(d)] * 5, out_specs=[blk, blk],
        out_shape=[jax.ShapeDtypeStruct((l, d), f32), jax.ShapeDtypeStruct((l, d), bf16)],
        compiler_params=_cparams("parallel"),
    )(x, y, gate, g_post, g_pre, shift, scale)


def _post_bwd_rows(dxv, yv, r, gate, gv, dy_ref, dgate_ref, dg_ref):
    yn = yv * r
    dgate_ref[...] += jnp.sum(dxv * yn * gv, axis=0, keepdims=True)
    dyg = dxv * gate
    dg_ref[...] += jnp.sum(dyg * yn, axis=0, keepdims=True)
    dyn = dyg * gv
    dy_ref[...] = (r * (dyn - yn * jnp.mean(dyn * yn, axis=-1, keepdims=True))).astype(dy_ref.dtype)


def final_loss(x, y, gate, g, target):
    l, d = x.shape
    tm = _row_tile(l)

    def body(x_ref, y_ref, gate_ref, g_ref, t_ref, dx_ref, loss_ref, dy_ref, dgate_ref, dg_ref):
        @pl.when(pl.program_id(0) == 0)
        def _():
            loss_ref[...] = jnp.zeros_like(loss_ref)
            dgate_ref[...] = jnp.zeros_like(dgate_ref)
            dg_ref[...] = jnp.zeros_like(dg_ref)

        yv, gate, gv = y_ref[...], gate_ref[...], g_ref[...]
        r = lax.rsqrt(jnp.mean(yv * yv, axis=-1, keepdims=True) + EPS)
        diff = x_ref[...] + gate * (yv * r * gv) - t_ref[...]
        dxv = diff * (1.0 / d)
        dx_ref[...] = dxv
        loss_ref[...] += jnp.sum(diff * diff)
        _post_bwd_rows(dxv, yv, r, gate, gv, dy_ref, dgate_ref, dg_ref)

    blk = pl.BlockSpec((tm, d), lambda i: (i, 0))
    return pl.pallas_call(
        body, name="final_loss", grid=(l // tm,),
        in_specs=[blk, blk, _row(d), _row(d), blk],
        out_specs=[blk, pl.BlockSpec((SUBLANES, HEAD), lambda i: (0, 0)), blk, _row(d), _row(d)],
        out_shape=[jax.ShapeDtypeStruct((l, d), f32), jax.ShapeDtypeStruct((SUBLANES, HEAD), f32),
                   jax.ShapeDtypeStruct((l, d), bf16), jax.ShapeDtypeStruct((1, d), f32), jax.ShapeDtypeStruct((1, d), f32)],
        compiler_params=_cparams("arbitrary"),
    )(x, y, gate, g, target)


def post_bwd(dx, y, gate, g, name):
    l, d = dx.shape
    tm = _row_tile(l)

    def body(dx_ref, y_ref, gate_ref, g_ref, dy_ref, dgate_ref, dg_ref):
        @pl.when(pl.program_id(0) == 0)
        def _():
            dgate_ref[...] = jnp.zeros_like(dgate_ref)
            dg_ref[...] = jnp.zeros_like(dg_ref)

        yv = y_ref[...]
        r = lax.rsqrt(jnp.mean(yv * yv, axis=-1, keepdims=True) + EPS)
        _post_bwd_rows(dx_ref[...], yv, r, gate_ref[...], g_ref[...], dy_ref, dgate_ref, dg_ref)

    blk = pl.BlockSpec((tm, d), lambda i: (i, 0))
    return pl.pallas_call(
        body, name=name, grid=(l // tm,),
        in_specs=[blk, blk, _row(d), _row(d)], out_specs=[blk, _row(d), _row(d)],
        out_shape=[jax.ShapeDtypeStruct((l, d), bf16), jax.ShapeDtypeStruct((1, d), f32),
                   jax.ShapeDtypeStruct((1, d), f32)],
        compiler_params=_cparams("arbitrary"),
    )(dx, y, gate, g)


def prenorm_bwd(dh, x, dx_next, g, scale, name):
    l, d = x.shape
    tm = _row_tile(l)

    def body(dh_ref, x_ref, dxn_ref, g_ref, sc_ref, dx_ref, dsh_ref, dsc_ref, dg_ref):
        @pl.when(pl.program_id(0) == 0)
        def _():
            dsh_ref[...] = jnp.zeros_like(dsh_ref)
            dsc_ref[...] = jnp.zeros_like(dsc_ref)
            dg_ref[...] = jnp.zeros_like(dg_ref)

        xv, dhv, gv, sc1 = x_ref[...], dh_ref[...], g_ref[...], 1.0 + sc_ref[...]
        r = lax.rsqrt(jnp.mean(xv * xv, axis=-1, keepdims=True) + EPS)
        xn = xv * r
        dhx = dhv * xn
        dsh_ref[...] += jnp.sum(dhv, axis=0, keepdims=True)
        dsc_ref[...] += jnp.sum(dhx * gv, axis=0, keepdims=True)
        dg_ref[...] += jnp.sum(dhx * sc1, axis=0, keepdims=True)
        dxn = dhv * (gv * sc1)
        dx_ref[...] = dxn_ref[...] + r * (dxn - xn * jnp.mean(dxn * xn, axis=-1, keepdims=True))

    blk = pl.BlockSpec((tm, d), lambda i: (i, 0))
    return pl.pallas_call(
        body, name=name, grid=(l // tm,),
        in_specs=[blk, blk, blk, _row(d), _row(d)], out_specs=[blk, _row(d), _row(d), _row(d)],
        out_shape=[jax.ShapeDtypeStruct((l, d), f32)] + [jax.ShapeDtypeStruct((1, d), f32)] * 3,
        compiler_params=_cparams("arbitrary"),
    )(dh, x, dx_next, g, scale)


def _tril_mask():
    r = lax.broadcasted_iota(jnp.int32, (HEAD, HEAD), 0)
    c = lax.broadcasted_iota(jnp.int32, (HEAD, HEAD), 1)
    return r >= c


def sgu_fwd(proj, norm_g, w_s, b_s):
    l = proj.shape[0]
    nh = w_s.shape[0]
    wa = nh * HEAD

    def body(au_ref, av_ref, az_ref, ng_ref, w_ref, b_ref, o_ref):
        tril = _tril_mask()
        for h in range(nh):
            sl = slice(h * HEAD, (h + 1) * HEAD)
            gv = _gelu(av_ref[:, sl].astype(f32))
            r = lax.rsqrt(jnp.mean(gv * gv, axis=-1, keepdims=True) + EPS)
            vh = gv * r * ng_ref[:, sl]
            wm = jnp.where(tril, w_ref[h], 0.0).astype(bf16)
            s = _dot(wm, vh.astype(bf16)) + b_ref[h]
            o_ref[:, sl] = (_gelu(au_ref[:, sl].astype(f32)) * s * _silu(az_ref[:, sl].astype(f32))).astype(o_ref.dtype)

    def col(j):
        return pl.BlockSpec((HEAD, wa), lambda n: (n, j))

    return pl.pallas_call(
        body, name="sgu_fwd", grid=(l // HEAD,),
        in_specs=[col(0), col(1), col(2), _row(wa),
                  pl.BlockSpec((nh, HEAD, HEAD), lambda n: (0, 0, 0)), pl.BlockSpec((nh, HEAD, 1), lambda n: (0, 0, 0))],
        out_specs=pl.BlockSpec((HEAD, wa), lambda n: (n, 0)),
        out_shape=jax.ShapeDtypeStruct((l, wa), bf16),
        compiler_params=_cparams("parallel"),
    )(proj, proj, proj, norm_g, w_s, b_s)


def sgu_bwd(proj, dcat, norm_g, w_s, b_s):
    l = proj.shape[0]
    nh = w_s.shape[0]
    wa = nh * HEAD

    def body(au_ref, av_ref, az_ref, do_ref, ng_ref, w_ref, b_ref, da_ref, dw_ref, db_ref, dng_ref):
        @pl.when(pl.program_id(0) == 0)
        def _():
            dw_ref[...] = jnp.zeros_like(dw_ref)
            db_ref[...] = jnp.zeros_like(db_ref)
            dng_ref[...] = jnp.zeros_like(dng_ref)

        tril = _tril_mask()
        for h in range(nh):
            sl = slice(h * HEAD, (h + 1) * HEAD)
            au, av, az = au_ref[:, sl].astype(f32), av_ref[:, sl].astype(f32), az_ref[:, sl].astype(f32)
            ng = ng_ref[:, sl]
            gv = _gelu(av)
            r = lax.rsqrt(jnp.mean(gv * gv, axis=-1, keepdims=True) + EPS)
            gvn = gv * r
            vh = (gvn * ng).astype(bf16)
            wm = jnp.where(tril, w_ref[h], 0.0).astype(bf16)
            s = _dot(wm, vh) + b_ref[h]
            gu, sz = _gelu(au), _silu(az)
            dov = do_ref[:, sl].astype(f32)
            da_ref[:, sl] = (dov * s * sz * _gelu_grad(au)).astype(da_ref.dtype)
            da_ref[:, 2 * wa + h * HEAD:2 * wa + (h + 1) * HEAD] = (dov * gu * s * _silu_grad(az)).astype(da_ref.dtype)
            ds = dov * gu * sz
            db_ref[h] += jnp.sum(ds, axis=-1, keepdims=True)
            dsb = ds.astype(bf16)
            dw_ref[h] += jnp.where(tril, _dot_nt(dsb, vh), 0.0)
            dvh = _dot_tn(wm, dsb)
            dng_ref[:, sl] += jnp.sum(dvh * gvn, axis=0, keepdims=True)
            dgvn = dvh * ng
            dgv = r * (dgvn - gvn * jnp.mean(dgvn * gvn, axis=-1, keepdims=True))
            da_ref[:, wa + h * HEAD:wa + (h + 1) * HEAD] = (dgv * _gelu_grad(av)).astype(da_ref.dtype)

    def col(j):
        return pl.BlockSpec((HEAD, wa), lambda n: (n, j))

    whole_w = pl.BlockSpec((nh, HEAD, HEAD), lambda n: (0, 0, 0))
    whole_b = pl.BlockSpec((nh, HEAD, 1), lambda n: (0, 0, 0))
    return pl.pallas_call(
        body, name="sgu_bwd", grid=(l // HEAD,),
        in_specs=[col(0), col(1), col(2), col(0), _row(wa), whole_w, whole_b],
        out_specs=[pl.BlockSpec((HEAD, 3 * wa), lambda n: (n, 0)), whole_w, whole_b, _row(wa)],
        out_shape=[jax.ShapeDtypeStruct((l, 3 * wa), bf16), jax.ShapeDtypeStruct((nh, HEAD, HEAD), f32),
                   jax.ShapeDtypeStruct((nh, HEAD, 1), f32), jax.ShapeDtypeStruct((1, wa), f32)],
        compiler_params=_cparams("arbitrary"),
    )(proj, proj, proj, dcat, norm_g, w_s, b_s)


_LOG2E = 1.0 / math.log(2.0)


def _sb_scores(q, k, scale):
    z = _dot_nt(q, k) * (scale * _LOG2E)
    return z, jnp.maximum(z, 0.0) + jnp.log2(1.0 + jnp.exp2(-jnp.abs(z)))


def _sb_sum_matrix(tri):
    s = lax.broadcasted_iota(jnp.int32, (2 * HEAD, 2 * HEAD), 0) % HEAD
    j = lax.broadcasted_iota(jnp.int32, (2 * HEAD, 2 * HEAD), 1)
    return jnp.where(jnp.logical_or(j >= HEAD, tri(s, j)), 1.0, 0.0).astype(bf16)


def _sb_sums(x, sums):
    c2 = _dot(jnp.concatenate(_split_bf16(x), axis=1), sums)
    return c2[:, :HEAD], c2[:, HEAD:]


def _sb_q_tile(l, most=512):
    return _tile(l, tuple(t for t in (1024, 512, 256, 128) if t <= most))


def _sb_heads_per_step(nh, most):
    return _tile(nh, tuple(h for h in (4, 2) if h <= most))


def sb_fwd(proj, nh):
    l = proj.shape[0]
    wb = nh * HEAD
    tq = _sb_q_tile(l, 1024)
    band = tq // HEAD
    hp = _sb_heads_per_step(nh, 2)
    scale = 1.0 / math.sqrt(HEAD)
    qc, kc, vc, zc = 3 * nh, 4 * nh, 5 * nh, 6 * nh

    def body(q_ref, k_ref, v_ref, bz_ref, o_ref, att_ref, tot_ref):
        i = pl.program_id(1)
        sums = _sb_sum_matrix(lambda s, j: s > j)
        t_pos = i * tq + lax.broadcasted_iota(jnp.int32, (tq, HEAD), 0)
        s_off = lax.broadcasted_iota(jnp.int32, (tq, HEAD), 1)

        def step(j, carry, masked):
            rows = pl.ds(pl.multiple_of(j * HEAD, HEAD), HEAD)
            out = []
            for e in range(hp):
                acc, tot = carry[e]
                sl = slice(e * HEAD, (e + 1) * HEAD)
                z, sp = _sb_scores(q_ref[:, sl], k_ref[rows, sl], scale)
                lb = z - sp
                if masked:
                    mask = s_off + j * HEAD < t_pos
                    sp = jnp.where(mask, sp, 0.0)
                later, total = _sb_sums(sp, sums)
                w = jnp.exp2(lb + tot - later)
                if masked:
                    w = jnp.where(mask, w, 0.0)
                out.append((acc + _dot(w.astype(bf16), v_ref[rows, sl]), tot - total))
            return tuple(out)

        zero = jnp.zeros((tq, HEAD), f32)
        carry = lax.fori_loop(0, band, lambda t, c: step(band * i + band - 1 - t, c, True), ((zero, zero),) * hp)
        carry = lax.fori_loop(0, band * i, lambda t, c: step(band * i - 1 - t, c, False), carry)
        for e in range(hp):
            acc, tot = carry[e]
            sl = slice(e * HEAD, (e + 1) * HEAD)
            att_ref[:, sl] = acc.astype(att_ref.dtype)
            o_ref[:, sl] = (acc * _silu(bz_ref[:, sl].astype(f32))).astype(o_ref.dtype)
            tot_ref[e] = tot[:, :1]

    blk = lambda c0: pl.BlockSpec((tq, hp * HEAD), lambda g, i: (i, c0 // hp + g))
    head = lambda c0: pl.BlockSpec((l, hp * HEAD), lambda g, i: (0, c0 // hp + g))
    return pl.pallas_call(
        body, name="sb_fwd", grid=(nh // hp, l // tq),
        in_specs=[blk(qc), head(kc), head(vc), blk(zc)],
        out_specs=[blk(0), blk(0), pl.BlockSpec((hp, tq, 1), lambda g, i: (g, i, 0))],
        out_shape=[jax.ShapeDtypeStruct((l, wb), bf16), jax.ShapeDtypeStruct((l, wb), bf16),
                   jax.ShapeDtypeStruct((nh, l, 1), f32)],
        compiler_params=_cparams("parallel", "arbitrary"),
    )(proj, proj, proj, proj)


def sb_bwd(proj, dcat, att, tot, nh):
    l = proj.shape[0]
    wb = nh * HEAD
    tq = _sb_q_tile(l, 1024)
    band = tq // HEAD
    nq = l // tq
    hp = _sb_heads_per_step(nh, 2)
    scale = 1.0 / math.sqrt(HEAD)
    qc, kc, vc, zc = 3 * nh, 4 * nh, 5 * nh, 6 * nh

    def body(q_ref, k_ref, v_ref, bz_ref, do_ref, att_ref, tot_ref, dq_ref, dk_ref, dv_ref, dbz_ref, dk_acc, dv_acc,
             dob_ref):
        i = pl.program_id(1)

        @pl.when(i == 0)
        def _():
            dk_acc[...] = jnp.zeros_like(dk_acc)
            dv_acc[...] = jnp.zeros_like(dv_acc)

        bz = bz_ref[...].astype(f32)
        dov = do_ref[...].astype(f32)
        dbz_ref[...] = (dov * att_ref[...].astype(f32) * _silu_grad(bz)).astype(dbz_ref.dtype)
        dob_ref[...] = (dov * _silu(bz)).astype(bf16)
        upto = _sb_sum_matrix(lambda s, j: s <= j)
        before = _sb_sum_matrix(lambda j, s: j < s)
        t_pos = i * tq + lax.broadcasted_iota(jnp.int32, (tq, HEAD), 0)
        s_off = lax.broadcasted_iota(jnp.int32, (tq, HEAD), 1)

        def step(j, carry, masked):
            rows = pl.ds(pl.multiple_of(j * HEAD, HEAD), HEAD)
            out = []
            for h in range(hp):
                dq, sp_seen, e_seen = carry[h]
                sl = slice(h * HEAD, (h + 1) * HEAD)
                q, kj, vj, dob = q_ref[:, sl], k_ref[rows, sl], v_ref[rows, sl], dob_ref[:, sl]
                z, sp = _sb_scores(q, kj, scale)
                lb = z - sp
                if masked:
                    mask = s_off + j * HEAD < t_pos
                    sp = jnp.where(mask, sp, 0.0)
                sp_upto, sp_total = _sb_sums(sp, upto)
                w = jnp.exp2(lb + sp_seen + sp_upto)
                if masked:
                    w = jnp.where(mask, w, 0.0)
                dv_acc[rows, sl] += _dot_tn(w.astype(bf16), dob)
                e = _dot_nt(dob, vj) * w
                e_before, e_total = _sb_sums(e, before)
                dz = (e - (e + e_seen + e_before) * jnp.exp2(lb)) * scale
                if masked:
                    dz = jnp.where(mask, dz, 0.0)
                dz = dz.astype(bf16)
                dk_acc[rows, sl] += _dot_tn(dz, q)
                out.append((dq + _dot(dz, kj), sp_seen + sp_total, e_seen + e_total))
            return tuple(out)

        zero = jnp.zeros((tq, HEAD), f32)
        init = tuple((zero, jnp.broadcast_to(tot_ref[h], (tq, HEAD)), zero) for h in range(hp))
        carry = lax.fori_loop(0, band * i, lambda j, c: step(j, c, False), init)
        carry = lax.fori_loop(0, band, lambda t, c: step(band * i + t, c, True), carry)
        for h in range(hp):
            dq_ref[:, h * HEAD:(h + 1) * HEAD] = carry[h][0].astype(dq_ref.dtype)

        @pl.when(i == nq - 1)
        def _():
            dk_ref[...] = dk_acc[...].astype(dk_ref.dtype)
            dv_ref[...] = dv_acc[...].astype(dv_ref.dtype)

    blk = lambda c0: pl.BlockSpec((tq, hp * HEAD), lambda g, i: (i, c0 // hp + g))
    head = lambda c0: pl.BlockSpec((l, hp * HEAD), lambda g, i: (0, c0 // hp + g))
    return pl.pallas_call(
        body, name="sb_bwd", grid=(nh // hp, nq),
        in_specs=[blk(qc), head(kc), head(vc), blk(zc), blk(nh), blk(0),
                  pl.BlockSpec((hp, tq, 1), lambda g, i: (g, i, 0))],
        out_specs=[blk(0), head(0), head(0), blk(0)],
        out_shape=[jax.ShapeDtypeStruct((l, wb), bf16)] * 4,
        scratch_shapes=[pltpu.VMEM((l, hp * HEAD), f32), pltpu.VMEM((l, hp * HEAD), f32),
                        pltpu.VMEM((tq, hp * HEAD), bf16)],
        compiler_params=_cparams("parallel", "arbitrary"),
    )(proj, proj, proj, proj, dcat, att, tot)


def _disc(lr, li, ldt):
    dt = jnp.exp(ldt)
    mag = jnp.exp(lr * dt)
    a_re = mag * jnp.cos(li * dt)
    a_im = mag * jnp.sin(li * dt)
    den = lr * lr + li * li
    nr = a_re - 1.0
    return a_re, a_im, (nr * lr + a_im * li) / den, (a_im * lr - nr * li) / den


def s5_params_fwd(lr, li, ldt, bt_re, bt_im):
    g, c, p = bt_re.shape

    def body(lr_ref, li_ref, ldt_ref, br_ref, bi_ref, ar_ref, ai_ref, bbr_ref, bbi_ref):
        a_re, a_im, cr, ci = _disc(lr_ref[...], li_ref[...], ldt_ref[...])
        ar_ref[...] = a_re
        ai_ref[...] = a_im
        for k in range(c):
            br, bi = br_ref[:, k, :], bi_ref[:, k, :]
            bbr_ref[:, k, :] = cr * br - ci * bi
            bbi_ref[:, k, :] = cr * bi + ci * br

    return pl.pallas_call(
        body, name="s5_params_fwd",
        out_shape=[jax.ShapeDtypeStruct((g, p), f32)] * 2 + [jax.ShapeDtypeStruct((g, c, p), f32)] * 2,
    )(lr, li, ldt, bt_re, bt_im)


def s5_params_bwd(lr, li, ldt, bt_re, bt_im, da_re, da_im, dbbt_re, dbbt_im):
    g, c, p = bt_re.shape

    def body(lr_ref, li_ref, ldt_ref, br_ref, bi_ref, dar_ref, dai_ref, dbbr_ref, dbbi_ref,
             dlr_ref, dli_ref, dldt_ref, dbr_ref, dbi_ref):
        (a_re, a_im, cr, ci), vjp = jax.vjp(_disc, lr_ref[...], li_ref[...], ldt_ref[...])
        dcr = jnp.zeros((g, p), f32)
        dci = jnp.zeros((g, p), f32)
        for k in range(c):
            br, bi = br_ref[:, k, :], bi_ref[:, k, :]
            dr, di = dbbr_ref[:, k, :], dbbi_ref[:, k, :]
            dcr += dr * br + di * bi
            dci += di * br - dr * bi
            dbr_ref[:, k, :] = cr * dr + ci * di
            dbi_ref[:, k, :] = cr * di - ci * dr
        dlr, dli, dldt = vjp((dar_ref[...], dai_ref[...], dcr, dci))
        dlr_ref[...] = dlr
        dli_ref[...] = dli
        dldt_ref[...] = dldt

    return pl.pallas_call(
        body, name="s5_params_bwd",
        out_shape=[jax.ShapeDtypeStruct((g, p), f32)] * 2 + [jax.ShapeDtypeStruct((g, 1), f32)]
        + [jax.ShapeDtypeStruct((g, c, p), f32)] * 2,
    )(lr, li, ldt, bt_re, bt_im, da_re, da_im, dbbt_re, dbbt_im)


def _cmul(ar, ai, br, bi):
    return ar * br - ai * bi, ar * bi + ai * br


def _power_tables(ar, ai):
    rows = lax.broadcasted_iota(jnp.int32, (SUBLANES, ar.shape[1]), 0)
    pr = jnp.zeros((SUBLANES, ar.shape[1]), f32)
    pi = jnp.zeros((SUBLANES, ar.shape[1]), f32)
    cr, ci = ar, ai
    pows = {}
    for r in range(SUBLANES):
        pows[r + 1] = (cr, ci)
        pr = jnp.where(rows == r, cr, pr)
        pi = jnp.where(rows == r, ci, pi)
        cr, ci = _cmul(cr, ci, ar, ai)
    return [pows[1], pows[2], pows[4]], pr, pi


def _ssm_time_tile(l):
    return _tile(l, (512, 256, 128))


def ssm_fwd(u, bre3, bim3, cre3, cimn3, a_re, a_im, d_skip):
    l, w = u.shape
    nj = w // HEAD
    ns = STATES_PER_LANE_BLOCK
    tt = _ssm_time_tile(l)

    def body(u_ref, bre_ref, bim_ref, cre_ref, cim_ref, ar_ref, ai_ref, d_ref, y_ref, hr_ref, hi_ref, cr_ref, ci_ref):
        @pl.when(pl.program_id(1) == 0)
        def _():
            cr_ref[...] = jnp.zeros_like(cr_ref)
            ci_ref[...] = jnp.zeros_like(ci_ref)

        uv = u_ref[...]
        hr_ref[...] = _dot(uv, bre_ref[...])
        hi_ref[...] = _dot(uv, bim_ref[...])
        steps, pr, pi = _power_tables(ar_ref[...], ai_ref[...])
        rows = lax.broadcasted_iota(jnp.int32, (SUBLANES, ns), 0)

        def blk(b, carry):
            cr, ci = carry
            sl = pl.ds(pl.multiple_of(b * SUBLANES, SUBLANES), SUBLANES)
            xr, xi = hr_ref[sl, :], hi_ref[sl, :]
            for d, (sr_, si_) in zip((1, 2, 4), steps):
                keep = rows >= d
                qr = jnp.where(keep, pltpu.roll(xr, d, axis=0), 0.0)
                qi = jnp.where(keep, pltpu.roll(xi, d, axis=0), 0.0)
                mr, mi = _cmul(sr_, si_, qr, qi)
                xr, xi = xr + mr, xi + mi
            mr, mi = _cmul(pr, pi, cr, ci)
            xr, xi = xr + mr, xi + mi
            hr_ref[sl, :] = xr
            hi_ref[sl, :] = xi
            return xr[SUBLANES - 1:, :], xi[SUBLANES - 1:, :]

        cr, ci = lax.fori_loop(0, tt // SUBLANES, blk, (cr_ref[...], ci_ref[...]))
        cr_ref[...] = cr
        ci_ref[...] = ci
        y = _dot(hr_ref[...].astype(bf16), cre_ref[...]) + _dot(hi_ref[...].astype(bf16), cim_ref[...])
        y_ref[...] = y + d_ref[...] * uv.astype(f32)

    lane = pl.BlockSpec((tt, HEAD), lambda j, i: (i, j))
    st = pl.BlockSpec((tt, ns), lambda j, i: (i, j))
    b3 = pl.BlockSpec((None, HEAD, ns), lambda j, i: (j, 0, 0))
    c3 = pl.BlockSpec((None, ns, HEAD), lambda j, i: (j, 0, 0))
    arow = pl.BlockSpec((1, ns), lambda j, i: (0, j))
    return pl.pallas_call(
        body, name="ssm_fwd", grid=(nj, l // tt),
        in_specs=[lane, b3, b3, c3, c3, arow, arow, pl.BlockSpec((1, HEAD), lambda j, i: (0, j))],
        out_specs=[lane, st, st],
        out_shape=[jax.ShapeDtypeStruct((l, w), f32), jax.ShapeDtypeStruct((l, nj * ns), f32),
                   jax.ShapeDtypeStruct((l, nj * ns), f32)],
        scratch_shapes=[pltpu.VMEM((1, ns), f32), pltpu.VMEM((1, ns), f32)],
        compiler_params=_cparams("parallel", "arbitrary"),
    )(u, bre3, bim3, cre3, cimn3, a_re, a_im, d_skip)


def ssm_bwd(dy, u, h_re, h_im, bre3, bim3, cre3, cimn3, a_re, a_im, d_skip):
    l, w = u.shape
    nj = w // HEAD
    ns = STATES_PER_LANE_BLOCK
    tt = _ssm_time_tile(l)
    nt = l // tt

    def body(dy_ref, u_ref, hr_ref, hi_ref, bre_ref, bim_ref, cre_ref, cim_ref, ar_ref, ai_ref, d_ref,
             du_ref, dd_ref, dar_ref, dai_ref, dbre_ref, dbim_ref, dcre_ref, dcim_ref, kr_ref, ki_ref, cr_ref, ci_ref,
             accr_ref, acci_ref):
        i = pl.program_id(1)

        @pl.when(i == 0)
        def _():
            for ref in (cr_ref, ci_ref, accr_ref, acci_ref, dd_ref, dbre_ref, dbim_ref, dcre_ref, dcim_ref):
                ref[...] = jnp.zeros_like(ref)

        dyv = dy_ref[...]
        dyb = dyv.astype(bf16)
        uv = u_ref[...]
        kr_ref[...] = _dot_nt(dyb, cre_ref[...])
        ki_ref[...] = _dot_nt(dyb, cim_ref[...])
        steps, pr, pi = _power_tables(ar_ref[...], -ai_ref[...])
        rows = lax.broadcasted_iota(jnp.int32, (SUBLANES, ns), 0)
        qr = jnp.zeros((SUBLANES, ns), f32)
        qi = jnp.zeros((SUBLANES, ns), f32)
        for r in range(SUBLANES):
            qr = jnp.where(rows == r, pr[SUBLANES - 1 - r:SUBLANES - r, :], qr)
            qi = jnp.where(rows == r, pi[SUBLANES - 1 - r:SUBLANES - r, :], qi)
        nb = tt // SUBLANES

        def blk(t, carry):
            cr, ci, accr, acci = carry
            sl = pl.ds(pl.multiple_of((nb - 1 - t) * SUBLANES, SUBLANES), SUBLANES)
            xr, xi = kr_ref[sl, :], ki_ref[sl, :]
            for d, (sr_, si_) in zip((1, 2, 4), steps):
                keep = rows < SUBLANES - d
                zr = jnp.where(keep, pltpu.roll(xr, SUBLANES - d, axis=0), 0.0)
                zi = jnp.where(keep, pltpu.roll(xi, SUBLANES - d, axis=0), 0.0)
                mr, mi = _cmul(sr_, si_, zr, zi)
                xr, xi = xr + mr, xi + mi
            mr, mi = _cmul(qr, qi, cr, ci)
            xr, xi = xr + mr, xi + mi
            kr_ref[sl, :] = xr
            ki_ref[sl, :] = xi
            last = rows == SUBLANES - 1
            nr = jnp.where(last, cr, pltpu.roll(xr, SUBLANES - 1, axis=0))
            ni = jnp.where(last, ci, pltpu.roll(xi, SUBLANES - 1, axis=0))
            hr, hi = hr_ref[sl, :], hi_ref[sl, :]
            accr = accr + nr * hr + ni * hi
            acci = acci + ni * hr - nr * hi
            return xr[:1, :], xi[:1, :], accr, acci

        cr, ci, accr, acci = lax.fori_loop(0, nb, blk, (cr_ref[...], ci_ref[...], accr_ref[...], acci_ref[...]))
        cr_ref[...] = cr
        ci_ref[...] = ci
        accr_ref[...] = accr
        acci_ref[...] = acci
        kr, ki = kr_ref[...].astype(bf16), ki_ref[...].astype(bf16)
        du = _dot_nt(kr, bre_ref[...]) + _dot_nt(ki, bim_ref[...]) + d_ref[...] * dyv
        du_ref[...] = du.astype(du_ref.dtype)
        dd_ref[...] += jnp.sum(dyv * uv.astype(f32), axis=0, keepdims=True)
        dbre_ref[...] += _dot_tn(uv, kr)
        dbim_ref[...] += _dot_tn(uv, ki)
        dcre_ref[...] += _dot_tn(hr_ref[...].astype(bf16), dyb)
        dcim_ref[...] += _dot_tn(hi_ref[...].astype(bf16), dyb)

        @pl.when(i == nt - 1)
        def _():
            dar_ref[...] = jnp.sum(accr_ref[...], axis=0, keepdims=True)
            dai_ref[...] = jnp.sum(acci_ref[...], axis=0, keepdims=True)

    lane = pl.BlockSpec((tt, HEAD), lambda j, i: (nt - 1 - i, j))
    st = pl.BlockSpec((tt, ns), lambda j, i: (nt - 1 - i, j))
    b3 = pl.BlockSpec((None, HEAD, ns), lambda j, i: (j, 0, 0))
    c3 = pl.BlockSpec((None, ns, HEAD), lambda j, i: (j, 0, 0))
    arow = pl.BlockSpec((1, ns), lambda j, i: (0, j))
    drow = pl.BlockSpec((1, HEAD), lambda j, i: (0, j))
    return pl.pallas_call(
        body, name="ssm_bwd", grid=(nj, nt),
        in_specs=[lane, lane, st, st, b3, b3, c3, c3, arow, arow, drow],
        out_specs=[lane, drow, arow, arow, b3, b3, c3, c3],
        out_shape=[jax.ShapeDtypeStruct((l, w), bf16), jax.ShapeDtypeStruct((1, w), f32),
                   jax.ShapeDtypeStruct((1, nj * ns), f32), jax.ShapeDtypeStruct((1, nj * ns), f32),
                   jax.ShapeDtypeStruct((nj, HEAD, ns), f32), jax.ShapeDtypeStruct((nj, HEAD, ns), f32),
                   jax.ShapeDtypeStruct((nj, ns, HEAD), f32), jax.ShapeDtypeStruct((nj, ns, HEAD), f32)],
        scratch_shapes=[pltpu.VMEM((tt, ns), f32), pltpu.VMEM((tt, ns), f32), pltpu.VMEM((1, ns), f32),
                        pltpu.VMEM((1, ns), f32), pltpu.VMEM((SUBLANES, ns), f32), pltpu.VMEM((SUBLANES, ns), f32)],
        compiler_params=_cparams("parallel", "arbitrary"),
    )(dy, u, h_re, h_im, bre3, bim3, cre3, cimn3, a_re, a_im, d_skip)


def glu_fwd(y, z_src, w_glu, b_glu):
    l, w = y.shape
    tm = _row_tile(l)

    def body(y_ref, z_ref, w_ref, b_ref, g_ref, t_ref, o_ref):
        g = _gelu(y_ref[...])
        gb = g.astype(bf16)
        t = _dot(gb, w_ref[...]) + b_ref[...]
        g_ref[...] = gb
        t_ref[...] = t
        o_ref[...] = (g * jax.nn.sigmoid(t) * _silu(z_ref[...].astype(f32))).astype(o_ref.dtype)

    blk = pl.BlockSpec((tm, w), lambda i: (i, 0))
    return pl.pallas_call(
        body, name="glu_fwd", grid=(l // tm,),
        in_specs=[blk, pl.BlockSpec((tm, w), lambda i: (i, 1)), pl.BlockSpec((w, w), lambda i: (0, 0)), _row(w)],
        out_specs=[blk, blk, blk],
        out_shape=[jax.ShapeDtypeStruct((l, w), bf16), jax.ShapeDtypeStruct((l, w), f32),
                   jax.ShapeDtypeStruct((l, w), bf16)],
        compiler_params=_cparams("parallel"),
    )(y, z_src, w_glu, b_glu)


def glu_bwd(dout, y, t, z_src, w_glu):
    l, w = y.shape
    tm = _row_tile(l)

    def body(do_ref, y_ref, t_ref, z_ref, w_ref, dy_ref, dz_ref, dt_ref, db_ref):
        @pl.when(pl.program_id(0) == 0)
        def _():
            db_ref[...] = jnp.zeros_like(db_ref)

        yv, zv, dov = y_ref[...], z_ref[...].astype(f32), do_ref[...]
        g = _gelu(yv)
        sg = jax.nn.sigmoid(t_ref[...])
        dy2 = dov * _silu(zv)
        dz_ref[...] = (dov * g * sg * _silu_grad(zv)).astype(dz_ref.dtype)
        dt = dy2 * g * sg * (1.0 - sg)
        dtb = dt.astype(bf16)
        dt_ref[...] = dtb
        db_ref[...] += jnp.sum(dt, axis=0, keepdims=True)
        dg = dy2 * sg + _dot_nt(dtb, w_ref[...])
        dy_ref[...] = dg * _gelu_grad(yv)

    blk = pl.BlockSpec((tm, w), lambda i: (i, 0))
    return pl.pallas_call(
        body, name="glu_bwd", grid=(l // tm,),
        in_specs=[blk, blk, blk, pl.BlockSpec((tm, w), lambda i: (i, 1)), pl.BlockSpec((w, w), lambda i: (0, 0))],
        out_specs=[blk, blk, blk, _row(w)],
        out_shape=[jax.ShapeDtypeStruct((l, w), f32), jax.ShapeDtypeStruct((l, w), bf16),
                   jax.ShapeDtypeStruct((l, w), bf16), jax.ShapeDtypeStruct((1, w), f32)],
        compiler_params=_cparams("arbitrary"),
    )(dout, y, t, z_src, w_glu)


def _adamw(w, g, m, v):
    m = ADAM_B1 * m + (1.0 - ADAM_B1) * g
    v = ADAM_B2 * v + (1.0 - ADAM_B2) * (g * g)
    m_hat = m / (1.0 - ADAM_B1 ** ADAM_STEP)
    v_hat = v / (1.0 - ADAM_B2 ** ADAM_STEP)
    return -ADAM_LR * (m_hat / (jnp.sqrt(v_hat) + ADAM_EPS) + ADAM_WD * w), m, v


def adam_reduce(pieces, w, m, v, name):
    r, c = w.shape
    n = pieces.shape[0]
    tr = _tile(r, (256, 128, 64, 32, 16, 8))

    def body(p_ref, w_ref, m_ref, v_ref, g_ref, d_ref, nm_ref, nv_ref):
        g = p_ref[0].astype(f32)
        for s in range(1, n):
            g = g + p_ref[s].astype(f32)
        g_ref[...] = g
        d_ref[...], nm_ref[...], nv_ref[...] = _adamw(w_ref[...], g, m_ref[...], v_ref[...])

    blk = pl.BlockSpec((tr, c), lambda i: (i, 0))
    return pl.pallas_call(
        body, name=name, grid=(r // tr,),
        in_specs=[pl.BlockSpec((n, tr, c), lambda i: (0, i, 0)), blk, blk, blk],
        out_specs=[blk] * 4, out_shape=[jax.ShapeDtypeStruct((r, c), f32)] * 4,
        compiler_params=_cparams("parallel"),
    )(pieces, w, m, v)


def adam_w_mod(cond_t, dm, w, m, v):
    nl, d, cols = w.shape
    tr = _tile(d, (512, 256, 128))

    def body(c_ref, dm_ref, w_ref, m_ref, v_ref, g_ref, d_ref, nm_ref, nv_ref):
        g = jnp.dot(c_ref[...], dm_ref[...], preferred_element_type=f32, precision=lax.Precision.HIGHEST)
        g_ref[...] = g
        d_ref[...], nm_ref[...], nv_ref[...] = _adamw(w_ref[...], g, m_ref[...], v_ref[...])

    blk = pl.BlockSpec((None, tr, cols), lambda l, i: (l, i, 0))
    return pl.pallas_call(
        body, name="adam_w_mod", grid=(nl, d // tr),
        in_specs=[pl.BlockSpec((tr, N_DEV), lambda l, i: (i, 0)), pl.BlockSpec((None, N_DEV, cols), lambda l, i: (l, 0, 0)),
                  blk, blk, blk],
        out_specs=[blk] * 4, out_shape=[jax.ShapeDtypeStruct((nl, d, cols), f32)] * 4,
        compiler_params=_cparams("parallel", "parallel"),
    )(cond_t, dm, w, m, v)


def silu_rows(c_all):
    def body(c_ref, o_ref):
        o_ref[...] = _silu(c_ref[...])

    return pl.pallas_call(body, name="silu_rows", out_shape=jax.ShapeDtypeStruct(c_all.shape, f32))(c_all)


def _block_diag(x):
    g, a, b = x.shape
    nj = g // GROUPS_PER_LANE_BLOCK
    eye = jnp.eye(GROUPS_PER_LANE_BLOCK, dtype=x.dtype)
    x5 = x.reshape(nj, GROUPS_PER_LANE_BLOCK, a, b)
    return jnp.einsum("jgab,gh->jgahb", x5, eye).reshape(nj, GROUPS_PER_LANE_BLOCK * a, GROUPS_PER_LANE_BLOCK * b)


def _diag_blocks(x, a, b):
    nj = x.shape[0]
    x5 = x.reshape(nj, GROUPS_PER_LANE_BLOCK, a, GROUPS_PER_LANE_BLOCK, b)
    eye = jnp.eye(GROUPS_PER_LANE_BLOCK, dtype=x.dtype)
    return jnp.einsum("jgahb,gh->jgab", x5, eye).reshape(nj * GROUPS_PER_LANE_BLOCK, a, b)


PACK_ROW = SUBLANES * HEAD


def _pack(parts, row_multiple=SUBLANES):
    rows = []
    for p in parts:
        flat = p.reshape(-1)
        pad = (-flat.shape[0]) % PACK_ROW
        if pad:
            flat = jnp.concatenate([flat, jnp.zeros((pad,), flat.dtype)])
        rows.append(flat.reshape(-1, HEAD))
    pad = (-sum(r.shape[0] for r in rows)) % row_multiple
    if pad:
        rows.append(jnp.zeros((pad, HEAD), rows[0].dtype))
    return jnp.concatenate(rows, axis=0)


def _unpack(packed, shapes):
    out, r0 = [], 0
    for shp in shapes:
        n = math.prod(shp)
        nr = -(-n // PACK_ROW) * SUBLANES
        out.append(packed[r0:r0 + nr].reshape(-1)[:n].reshape(shp))
        r0 += nr
    return out


def adam_small(g, w, m, v):
    r, c = w.shape

    def body(g_ref, w_ref, m_ref, v_ref, d_ref, nm_ref, nv_ref):
        d_ref[...], nm_ref[...], nv_ref[...] = _adamw(w_ref[...], g_ref[...], m_ref[...], v_ref[...])

    tr = max(t for t in range(SUBLANES, 1024 + 1, SUBLANES) if r % t == 0)
    blk = pl.BlockSpec((tr, c), lambda i: (i, 0))
    return pl.pallas_call(
        body, name="adam_small", grid=(r // tr,),
        in_specs=[blk] * 4, out_specs=[blk] * 3, out_shape=[jax.ShapeDtypeStruct((r, c), f32)] * 3,
        compiler_params=_cparams("parallel"),
    )(g, w, m, v)


def kernel(x, c, ln_pre_g, ln_post_g, w_mod, b_mod, w_in_ab, w_out_ab, sgu_norm_g, sgu_w, sgu_b, w_in_ssm, w_out_ssm, lam_re, lam_im, b_re, b_im, c_re, c_im, d_skip, log_dt, w_glu, b_glu, loss_target, m_ln_pre_g, m_ln_post_g, m_w_mod, m_b_mod, m_w_in_ab, m_w_out_ab, m_sgu_norm_g, m_sgu_w, m_sgu_b, m_w_in_ssm, m_w_out_ssm, m_lam_re, m_lam_im, m_b_re, m_b_im, m_c_re, m_c_im, m_d_skip, m_log_dt, m_w_glu, m_b_glu, v_ln_pre_g, v_ln_post_g, v_w_mod, v_b_mod, v_w_in_ab, v_w_out_ab, v_sgu_norm_g, v_sgu_w, v_sgu_b, v_w_in_ssm, v_w_out_ssm, v_lam_re, v_lam_im, v_b_re, v_b_im, v_c_re, v_c_im, v_d_skip, v_log_dt, v_w_glu, v_b_glu):
    me = _my_index()
    x0 = x[0]
    l, d = x0.shape
    target = loss_target[0]
    nh = sgu_w.shape[1]
    wa = nh * HEAD
    n_grp, n_st = lam_re.shape[1], lam_re.shape[2]
    mod_cols = w_mod.shape[2]

    c_all, d_skip_all, b_glu_all = all_gather([c, d_skip, b_glu], "gather_c")
    c_all = c_all.reshape(N_DEV, d)
    d_skip_all = d_skip_all.reshape(1, -1)
    b_glu_all = b_glu_all.reshape(1, -1)

    b_cols = lax.dynamic_slice_in_dim(b_mod, me * mod_cols, mod_cols, axis=1)
    (mod_all,) = all_gather([mod_part(c_all, w_mod, b_cols)], "gather_mod")
    def after(a, first):
        return a + jnp.minimum(jnp.abs(first[(0,) * first.ndim].astype(f32)), 0.0).astype(a.dtype)

    (win_ab3,) = sequencer_exchange(GATHER, [after(w_in_ab[0], mod_all).astype(bf16)], "gather_w_in", 1)
    mod_mine = lax.dynamic_index_in_dim(mod_all, me, axis=2, keepdims=False)
    mod_rows = jnp.transpose(mod_mine, (1, 0, 2)).reshape(2, 3, 1, d)

    def rows(a, i):
        return a[i].reshape(1, d)

    shift0, scale0, gate0 = mod_rows[0, 0], mod_rows[0, 1], mod_rows[0, 2]
    h0 = prenorm_fwd(x0, rows(ln_pre_g, 0), shift0, scale0, "prenorm0")
    wout_ab3, win_ssm3, wout_ssm3, wglu = sequencer_exchange(
        GATHER, [after(w, win_ab3).astype(bf16) for w in (w_out_ab[0], w_in_ssm[0], w_out_ssm[0], w_glu[0])],
        "gather_w_rest", 2)
    proj0 = mm_nn(h0, win_ab3, bf16, "proj0")
    sgu_b3 = sgu_b[0].reshape(nh, HEAD, 1)
    out_a = sgu_fwd(proj0, sgu_norm_g, sgu_w[0], sgu_b3)
    out_b, att, tot = sb_fwd(proj0, nh)
    cat = jnp.concatenate([out_a, out_b], axis=1)
    wout_ab3 = wout_ab3.reshape(1, d, d)
    win_ssm3 = win_ssm3.reshape(1, d, d)
    wglu = wglu.reshape(w_glu.shape[2], w_glu.shape[2])
    y0 = mm_nn(cat, wout_ab3, f32, "out0")

    shift1, scale1, gate1 = mod_rows[1, 0], mod_rows[1, 1], mod_rows[1, 2]
    x1, h1 = post_prenorm_fwd(x0, y0, gate0, rows(ln_post_g, 0), rows(ln_pre_g, 1), shift1, scale1, "post0_prenorm1")
    proj1 = mm_nn(h1, win_ssm3, bf16, "proj1")
    w_ssm = proj1.shape[1] // 2
    ldt = log_dt[0].reshape(n_grp, 1)
    bt_re = jnp.transpose(b_re[0], (0, 2, 1))
    bt_im = jnp.transpose(b_im[0], (0, 2, 1))
    a_re, a_im, bbt_re, bbt_im = s5_params_fwd(lam_re[0], lam_im[0], ldt, bt_re, bt_im)
    bre3 = _block_diag(bbt_re).astype(bf16)
    bim3 = _block_diag(bbt_im).astype(bf16)
    cre3 = _block_diag(jnp.transpose(c_re[0], (0, 2, 1))).astype(bf16)
    cimn3 = _block_diag(-jnp.transpose(c_im[0], (0, 2, 1))).astype(bf16)
    a_re_row, a_im_row = a_re.reshape(1, -1), a_im.reshape(1, -1)
    u = proj1[:, :w_ssm]
    y_ssm, hs_re, hs_im = ssm_fwd(u, bre3, bim3, cre3, cimn3, a_re_row, a_im_row, d_skip_all)
    g_act, t_glu, mix1 = glu_fwd(y_ssm, proj1, wglu, b_glu_all)
    y1 = mm_nn(mix1, wout_ssm3, f32, "out1")

    dx2, loss_tile, dy1, dgate1, dgpost1 = final_loss(x1, y1, gate1, rows(ln_post_g, 1), target)

    dmix1 = mm_nt(dy1, wout_ssm3, f32, "dmix1")
    gw_out_ssm = mm_tn(mix1, dy1, N_DEV, bf16, "gw_out_ssm")
    (p_out_ssm,) = sequencer_exchange(SCATTER, [gw_out_ssm], "scatter_g1", 3)
    dy_ssm, dz1, dt_glu, db_glu = glu_bwd(dmix1, y_ssm, t_glu, proj1, wglu)
    gw_glu = mm_tn(g_act, dt_glu, 1, bf16, "gw_glu").reshape(N_DEV, -1, w_ssm)
    du, dd_skip, da_re, da_im, dbre3, dbim3, dcre3, dcimn3 = ssm_bwd(
        dy_ssm, u, hs_re, hs_im, bre3, bim3, cre3, cimn3, a_re_row, a_im_row, d_skip_all)
    dproj1 = jnp.concatenate([du, dz1], axis=1)
    gw_in_ssm = mm_tn(h1, dproj1, 1, bf16, "gw_in_ssm").reshape(N_DEV, -1, proj1.shape[1])
    p_in_ssm, p_glu = sequencer_exchange(SCATTER, [gw_in_ssm, gw_glu], "scatter_g2", 4)
    dh1 = mm_nt(dproj1, win_ssm3, f32, "dh1")
    dx1, dshift1, dscale1, dgpre1 = prenorm_bwd(dh1, x1, dx2, rows(ln_pre_g, 1), scale1, "prenorm1_bwd")
    dlr, dli, dldt, dbt_re, dbt_im = s5_params_bwd(
        lam_re[0], lam_im[0], ldt, bt_re, bt_im, da_re.reshape(n_grp, n_st), da_im.reshape(n_grp, n_st),
        _diag_blocks(dbre3, SSM_GROUP, n_st), _diag_blocks(dbim3, SSM_GROUP, n_st))
    g_b_re = jnp.transpose(dbt_re, (0, 2, 1))
    g_b_im = jnp.transpose(dbt_im, (0, 2, 1))
    g_c_re = jnp.transpose(_diag_blocks(dcre3, n_st, SSM_GROUP), (0, 2, 1))
    g_c_im = -jnp.transpose(_diag_blocks(dcimn3, n_st, SSM_GROUP), (0, 2, 1))

    dy0, dgate0, dgpost0 = post_bwd(dx1, y0, gate0, rows(ln_post_g, 0), "post0_bwd")
    dcat = mm_nt(dy0, wout_ab3, f32, "dcat")
    gw_out_ab = mm_tn(cat, dy0, 1, bf16, "gw_out_ab").reshape(N_DEV, -1, d)
    (p_out_ab,) = sequencer_exchange(SCATTER, [gw_out_ab], "scatter_g3", 5)
    da, dsgu_w, dsgu_b, dsgu_ng = sgu_bwd(proj0, dcat, sgu_norm_g, sgu_w[0], sgu_b3)
    dq, dk, dv, dbz = sb_bwd(proj0, dcat, att, tot, nh)
    dproj0 = jnp.concatenate([da, dq, dk, dv, dbz], axis=1)
    gw_in_ab = mm_tn(h0, dproj0, N_DEV, bf16, "gw_in_ab")
    (p_in_ab,) = sequencer_exchange(SCATTER, [gw_in_ab], "scatter_g4", 6)
    dh0 = mm_nt(dproj0, win_ab3, f32, "dh0")
    dx0, dshift0, dscale0, dgpre0 = prenorm_bwd(dh0, x0, dx1, rows(ln_pre_g, 0), scale0, "prenorm0_bwd")

    small_names = ["ln_pre_g", "ln_post_g", "b_mod", "sgu_norm_g", "sgu_w", "sgu_b", "lam_re", "lam_im", "b_re", "b_im",
                   "c_re", "c_im", "log_dt"]
    small_w = [ln_pre_g, ln_post_g, b_mod, sgu_norm_g, sgu_w, sgu_b, lam_re, lam_im, b_re, b_im, c_re, c_im, log_dt]
    small_m = [m_ln_pre_g, m_ln_post_g, m_b_mod, m_sgu_norm_g, m_sgu_w, m_sgu_b, m_lam_re, m_lam_im, m_b_re, m_b_im,
               m_c_re, m_c_im, m_log_dt]
    small_v = [v_ln_pre_g, v_ln_post_g, v_b_mod, v_sgu_norm_g, v_sgu_w, v_sgu_b, v_lam_re, v_lam_im, v_b_re, v_b_im,
               v_c_re, v_c_im, v_log_dt]
    dmod = jnp.concatenate([dshift0, dscale0, dgate0, dshift1, dscale1, dgate1], axis=1)
    small_g = [jnp.concatenate([dgpre0, dgpre1]), jnp.concatenate([dgpost0, dgpost1]), dmod, dsgu_ng, dsgu_w, dsgu_b,
               dlr, dli, g_b_re, g_b_im, g_c_re, g_c_im, dldt]
    shapes = [w.shape for w in small_w]
    g_sum, dmod_all = all_reduce_rows(_pack(small_g + [dd_skip, db_glu, loss_tile], SUBLANES * N_DEV), dmod,
                                      "reduce_small_grads")
    n_rows_small = sum(-(-math.prod(s) // PACK_ROW) * SUBLANES for s in shapes)
    loss = g_sum[n_rows_small + 2 * (d_skip_all.shape[1] // HEAD), 0] * (0.5 / d)
    new_small = adam_small(g_sum, _pack(small_w), _pack(small_m), _pack(small_v))
    r_small = [_unpack(o, shapes) for o in [g_sum[:n_rows_small]] + list(new_small)]
    small = {n: [r_small[k][i] for k in range(4)] for i, n in enumerate(small_names)}
    vec_rows = d_skip_all.shape[1] // HEAD

    def my_columns(r0):
        whole = g_sum[r0:r0 + vec_rows].reshape(1, 1, -1)
        return lax.dynamic_slice_in_dim(whole, me * d_skip.shape[1], d_skip.shape[1], axis=2)

    def sharded(p, w, m, v, name):
        shp = w.shape
        w2, m2, v2 = (a.reshape(-1, shp[-1]) for a in (w, m, v))
        return [o.reshape(shp) for o in adam_reduce(p.reshape(p.shape[0], -1, shp[-1]), w2, m2, v2, name)]

    r_d_skip = sharded(my_columns(n_rows_small), d_skip, m_d_skip, v_d_skip, "adam_d_skip")
    r_b_glu = sharded(my_columns(n_rows_small + vec_rows), b_glu, m_b_glu, v_b_glu, "adam_b_glu")
    r_w_out_ssm = sharded(p_out_ssm, w_out_ssm, m_w_out_ssm, v_w_out_ssm, "adam_w_out_ssm")
    r_w_in_ssm = sharded(p_in_ssm, w_in_ssm, m_w_in_ssm, v_w_in_ssm, "adam_w_in_ssm")
    r_w_glu = sharded(p_glu, w_glu, m_w_glu, v_w_glu, "adam_w_glu")
    r_w_out_ab = sharded(p_out_ab, w_out_ab, m_w_out_ab, v_w_out_ab, "adam_w_out_ab")
    r_w_in_ab = sharded(p_in_ab, w_in_ab, m_w_in_ab, v_w_in_ab, "adam_w_in_ab")

    dm_cols = jnp.transpose(
        lax.dynamic_slice_in_dim(dmod_all.reshape(N_DEV, 2, 3 * d), me * mod_cols, mod_cols, axis=2), (1, 0, 2))
    cond_t = jnp.transpose(silu_rows(c_all))
    r_w_mod = adam_w_mod(cond_t, dm_cols, w_mod, m_w_mod, v_w_mod)

    res = dict(small)
    res.update(w_mod=r_w_mod, w_in_ab=r_w_in_ab, w_out_ab=r_w_out_ab, w_in_ssm=r_w_in_ssm, w_out_ssm=r_w_out_ssm,
               d_skip=r_d_skip, w_glu=r_w_glu, b_glu=r_b_glu)
    order = ["ln_pre_g", "ln_post_g", "w_mod", "b_mod", "w_in_ab", "w_out_ab", "sgu_norm_g", "sgu_w", "sgu_b", "w_in_ssm",
             "w_out_ssm", "lam_re", "lam_im", "b_re", "b_im", "c_re", "c_im", "d_skip", "log_dt", "w_glu", "b_glu"]
    outs = [loss, dx0.reshape(x.shape)]
    for k in range(4):
        outs += [res[n][k] for n in order]
    return tuple(outs)
```

```python
import functools
import math

import jax
import jax.numpy as jnp
from jax import lax
from jax.experimental import pallas as pl
from jax.experimental.pallas import tpu as pltpu
from jax.experimental.pallas import tpu_sc as plsc

f32 = jnp.float32
bf16 = jnp.bfloat16

N_DEV = 8
EPS = 1e-6
HEAD = 128
SUBLANES = 8
SSM_GROUP = 16
SSM_STATE = 64
GROUPS_PER_LANE_BLOCK = HEAD // SSM_GROUP
STATES_PER_LANE_BLOCK = GROUPS_PER_LANE_BLOCK * SSM_STATE
VMEM_LIMIT = 56 * 2 ** 20
ADAM_LR, ADAM_B1, ADAM_B2, ADAM_EPS, ADAM_WD, ADAM_STEP = 0.001, 0.9, 0.999, 1e-08, 0.01, 10
_GELU_C0 = math.sqrt(2.0 / math.pi)
_GELU_C1 = 0.044715
MESH = pl.DeviceIdType.MESH


def _cparams(*sem):
    return pltpu.CompilerParams(dimension_semantics=sem if sem else None, vmem_limit_bytes=VMEM_LIMIT)


def _gelu(x):
    return 0.5 * x * (1.0 + jnp.tanh(_GELU_C0 * (x + _GELU_C1 * x * x * x)))


def _gelu_grad(x):
    t = jnp.tanh(_GELU_C0 * (x + _GELU_C1 * x * x * x))
    return 0.5 * (1.0 + t) + 0.5 * x * (1.0 - t * t) * _GELU_C0 * (1.0 + 3.0 * _GELU_C1 * x * x)


def _silu(x):
    return x * jax.nn.sigmoid(x)


def _silu_grad(x):
    s = jax.nn.sigmoid(x)
    return s * (1.0 + x * (1.0 - s))


def _dot(a, b):
    return jnp.dot(a, b, preferred_element_type=f32)


def _dot_nt(a, b):
    return lax.dot_general(a, b, (((1,), (1,)), ((), ())), preferred_element_type=f32)


def _dot_tn(a, b):
    return lax.dot_general(a, b, (((0,), (0,)), ((), ())), preferred_element_type=f32)


def _split_bf16(v):
    hi = v.astype(bf16)
    lo = (v - hi.astype(f32)).astype(bf16)
    return hi, lo


def _row(d):
    return pl.BlockSpec((1, d), lambda *_: (0, 0))


def _my_index():
    return 4 * lax.axis_index("x") + 2 * lax.axis_index("y") + lax.axis_index("c")


def _peer(k):
    x, y, c = lax.axis_index("x"), lax.axis_index("y"), lax.axis_index("c")
    return (1 - x if k & 4 else x, 1 - y if k & 2 else y, 1 - c if k & 1 else c)


def all_gather(arrs, name):
    n = len(arrs)

    def body(*refs):
        ins, outs = refs[:n], refs[n:2 * n]
        send, recv, local = refs[2 * n:]
        me = _my_index()
        copies = []
        for a in range(n):
            cp = pltpu.make_async_copy(ins[a], outs[a].at[me], local.at[a])
            cp.start()
            copies.append(cp)
            for k in range(1, N_DEV):
                s = a * (N_DEV - 1) + k - 1
                cp = pltpu.make_async_remote_copy(src_ref=ins[a], dst_ref=outs[a].at[me], send_sem=send.at[s],
                                                  recv_sem=recv.at[s], device_id=_peer(k), device_id_type=MESH)
                cp.start()
                copies.append(cp)
        for cp in copies:
            cp.wait()

    any_spec = pl.BlockSpec(memory_space=pl.ANY)
    outs = pl.pallas_call(
        body, name=name,
        out_shape=[jax.ShapeDtypeStruct((N_DEV,) + a.shape, a.dtype) for a in arrs],
        in_specs=[any_spec] * n, out_specs=[any_spec] * n,
        scratch_shapes=[pltpu.SemaphoreType.DMA((n * (N_DEV - 1),)), pltpu.SemaphoreType.DMA((n * (N_DEV - 1),)),
                        pltpu.SemaphoreType.DMA((n,))],
        compiler_params=pltpu.CompilerParams(has_side_effects=True),
    )(*arrs)
    return list(outs)


def all_reduce_rows(pack, extra, name):
    r, c = pack.shape
    rs = r // N_DEV
    n_peer = N_DEV - 1

    def body(p_ref, x_ref, o_ref, xo_ref, land, red, send1, recv1, send2, recv2, sendx, recvx, local):
        me = _my_index()

        def rows(i):
            return pl.ds(pl.multiple_of(i * rs, SUBLANES), rs)

        own = [pltpu.make_async_copy(p_ref.at[rows(me)], land.at[me], local.at[0]),
               pltpu.make_async_copy(x_ref, xo_ref.at[me], local.at[1])]
        first = []
        for k in range(1, N_DEV):
            first.append(pltpu.make_async_remote_copy(
                src_ref=p_ref.at[rows(jnp.bitwise_xor(me, k))], dst_ref=land.at[me], send_sem=send1.at[k - 1],
                recv_sem=recv1.at[k - 1], device_id=_peer(k), device_id_type=MESH))
            first.append(pltpu.make_async_remote_copy(
                src_ref=x_ref, dst_ref=xo_ref.at[me], send_sem=sendx.at[k - 1], recv_sem=recvx.at[k - 1],
                device_id=_peer(k), device_id_type=MESH))
        for cp in own + first:
            cp.start()
        for cp in own + first:
            cp.wait()
        acc = land[0]
        for s in range(1, N_DEV):
            acc = acc + land[s]
        red[...] = acc
        mine = pltpu.make_async_copy(red, o_ref.at[rows(me)], local.at[2])
        second = [pltpu.make_async_remote_copy(
            src_ref=red, dst_ref=o_ref.at[rows(me)], send_sem=send2.at[k - 1], recv_sem=recv2.at[k - 1],
            device_id=_peer(k), device_id_type=MESH) for k in range(1, N_DEV)]
        for cp in [mine] + second:
            cp.start()
        for cp in [mine] + second:
            cp.wait()

    any_spec = pl.BlockSpec(memory_space=pl.ANY)
    return pl.pallas_call(
        body, name=name,
        out_shape=[jax.ShapeDtypeStruct((r, c), pack.dtype), jax.ShapeDtypeStruct((N_DEV,) + extra.shape, extra.dtype)],
        in_specs=[any_spec, any_spec], out_specs=[any_spec, any_spec],
        scratch_shapes=[pltpu.VMEM((N_DEV, rs, c), pack.dtype), pltpu.VMEM((rs, c), pack.dtype)]
        + [pltpu.SemaphoreType.DMA((n_peer,))] * 6 + [pltpu.SemaphoreType.DMA((3,))],
        compiler_params=pltpu.CompilerParams(has_side_effects=True),
    )(pack, extra)


GATHER, SCATTER = "gather", "scatter"


def _exchange_copies(srcs, lands, send, recv):
    me = _my_index()
    copies = []
    for a, (src, land) in enumerate(zip(srcs, lands)):
        for k in range(1, N_DEV):
            s = a * (N_DEV - 1) + k - 1
            copies.append(pltpu.make_async_remote_copy(
                src_ref=src.at[jnp.bitwise_xor(me, k)], dst_ref=land.at[me],
                send_sem=send.at[s], recv_sem=recv.at[s], device_id=_peer(k), device_id_type=MESH))
    return copies


def sequencer_exchange(kind, arrs, name, collective_id):
    n = len(arrs)
    n_sem = n * (N_DEV - 1)
    land_shapes = [((N_DEV,) + a.shape if kind == GATHER else a.shape) for a in arrs]
    srcs = [jax.new_ref(a, memory_space=pltpu.MemorySpace.HBM) for a in arrs]
    lands = [jax.empty_ref(jax.ShapeDtypeStruct(s, a.dtype), memory_space=pltpu.MemorySpace.HBM)
             for s, a in zip(land_shapes, arrs)]

    @pl.kernel(mesh=plsc.ScalarSubcoreMesh(axis_name="sequencer", num_cores=1), name=name,
               scratch_types=(pltpu.SemaphoreType.DMA((n_sem,)), pltpu.SemaphoreType.DMA((n_sem,)),
                              pltpu.SemaphoreType.DMA((n,))),
               compiler_params=pltpu.CompilerParams(collective_id=collective_id))
    def launch(send, recv, local):
        barrier = pltpu.get_barrier_semaphore()
        for k in range(1, N_DEV):
            pl.semaphore_signal(barrier, inc=1, device_id=_peer(k), device_id_type=MESH)
        pl.semaphore_wait(barrier, N_DEV - 1)
        me = _my_index()
        mine = [pltpu.make_async_copy(src if kind == GATHER else src.at[me], land.at[me], local.at[a])
                for a, (src, land) in enumerate(zip(srcs, lands))]
        if kind == SCATTER:
            copies = mine + _exchange_copies(srcs, lands, send, recv)
            for cp in copies:
                cp.start()
            for cp in copies:
                cp.wait()
            return

        def block_copy(a, slot, block, k, src=None):
            s = a * (N_DEV - 1) + slot
            return pltpu.make_async_remote_copy(
                src_ref=lands[a].at[block] if src is None else src, dst_ref=lands[a].at[block],
                send_sem=send.at[s], recv_sem=recv.at[s], device_id=_peer(k), device_id_type=MESH)

        chips = (2, 4, 6)
        sibling = jnp.bitwise_xor(me, 1)
        first = [block_copy(a, slot, me, k, src=srcs[a]) for a in range(n) for slot, k in enumerate((1,) + chips)]
        for cp in mine + first:
            cp.start()
        passed = []
        for a in range(n):
            for i, k in enumerate(chips):
                block = jnp.bitwise_xor(me, k)
                block_copy(a, 1 + i, block, k).wait_recv()
                passed.append(block_copy(a, 4 + i, block, 1))
                passed[-1].start()
        for a in range(n):
            block_copy(a, 0, sibling, 1).wait_recv()
            for i, k in enumerate(chips):
                block_copy(a, 4 + i, jnp.bitwise_xor(sibling, k), 1).wait_recv()
        for cp in mine:
            cp.wait()
        for cp in first + passed:
            cp.wait_send()

    launch()
    return [land[...] for land in lands]


def _tile(n, pref):
    for t in pref:
        if n % t == 0:
            return t
    return n


def mm_nn(a, b3, out_dtype, name):
    m, k = a.shape
    nb, _, bn = b3.shape
    tm = _tile(m, (512, 256, 128))
    tn = _tile(bn, (1024, 896, 512, 256, 128))
    per = bn // tn

    def body(a_ref, b_ref, o_ref):
        o_ref[...] = _dot(a_ref[...], b_ref[...]).astype(o_ref.dtype)

    return pl.pallas_call(
        body, name=name, grid=(m // tm, nb, per),
        in_specs=[pl.BlockSpec((tm, k), lambda i, j, jj: (i, 0)),
                  pl.BlockSpec((None, k, tn), lambda i, j, jj: (j, 0, jj))],
        out_specs=pl.BlockSpec((tm, tn), lambda i, j, jj: (i, j * per + jj)),
        out_shape=jax.ShapeDtypeStruct((m, nb * bn), out_dtype),
        compiler_params=_cparams("parallel", "arbitrary", "arbitrary"),
    )(a, b3)


def mm_nt(a, w3, out_dtype, name):
    m, _ = a.shape
    nb, ko, bn = w3.shape
    tm = _tile(m, (512, 256, 128))
    tko = _tile(ko, (1024, 512, 256, 128))

    def body(a_ref, w_ref, o_ref, acc_ref):
        j = pl.program_id(2)

        @pl.when(j == 0)
        def _():
            acc_ref[...] = jnp.zeros_like(acc_ref)

        acc_ref[...] += _dot_nt(a_ref[...], w_ref[...])

        @pl.when(j == nb - 1)
        def _():
            o_ref[...] = acc_ref[...].astype(o_ref.dtype)

    return pl.pallas_call(
        body, name=name, grid=(m // tm, ko // tko, nb),
        in_specs=[pl.BlockSpec((tm, bn), lambda i, o, j: (i, j)),
                  pl.BlockSpec((None, tko, bn), lambda i, o, j: (j, o, 0))],
        out_specs=pl.BlockSpec((tm, tko), lambda i, o, j: (i, o)),
        out_shape=jax.ShapeDtypeStruct((m, ko), out_dtype),
        scratch_shapes=[pltpu.VMEM((tm, tko), f32)],
        compiler_params=_cparams("parallel", "arbitrary", "arbitrary"),
    )(a, w3)


def mm_tn(a, dy, ncb, out_dtype, name):
    l, ka = a.shape
    _, n = dy.shape
    bn = n // ncb
    tl = _tile(l, (1024, 512, 256, 128))
    tka = _tile(ka, (512, 256, 128))
    tn = _tile(bn, (1024, 896, 512, 256, 128))
    per = bn // tn
    nl = l // tl

    def body(a_ref, dy_ref, o_ref, acc_ref):
        s = pl.program_id(2)

        @pl.when(s == 0)
        def _():
            acc_ref[...] = jnp.zeros_like(acc_ref)

        acc_ref[...] += _dot_tn(a_ref[...], dy_ref[...])

        @pl.when(s == nl - 1)
        def _():
            o_ref[...] = acc_ref[...].astype(o_ref.dtype)

    return pl.pallas_call(
        body, name=name, grid=(ka // tka, n // tn, nl),
        in_specs=[pl.BlockSpec((tl, tka), lambda i, j, s: (s, i)),
                  pl.BlockSpec((tl, tn), lambda i, j, s: (s, j))],
        out_specs=pl.BlockSpec((None, tka, tn), lambda i, j, s: (j // per, i, j % per)),
        out_shape=jax.ShapeDtypeStruct((ncb, ka, bn), out_dtype),
        scratch_shapes=[pltpu.VMEM((tka, tn), f32)],
        compiler_params=_cparams("parallel", "parallel", "arbitrary"),
    )(a, dy)


def mod_part(c_all, w_mod, b_cols):
    nl, d, cols = w_mod.shape

    def body(c_ref, w_ref, b_ref, o_ref):
        cond = _silu(c_ref[...]).astype(bf16)
        o_ref[...] = _dot(cond, w_ref[...].astype(bf16)) + b_ref[...]

    return pl.pallas_call(
        body, name="mod_part", grid=(nl,),
        in_specs=[pl.BlockSpec((N_DEV, d), lambda l: (0, 0)),
                  pl.BlockSpec((None, d, cols), lambda l: (l, 0, 0)),
                  pl.BlockSpec((None, 1, cols), lambda l: (l, 0, 0))],
        out_specs=pl.BlockSpec((None, N_DEV, cols), lambda l: (l, 0, 0)),
        out_shape=jax.ShapeDtypeStruct((nl, N_DEV, cols), f32),
        compiler_params=_cparams("arbitrary"),
    )(c_all, w_mod, b_cols.reshape(nl, 1, cols))


def _row_tile(l):
    return _tile(l, (256, 128))


def prenorm_fwd(x, g, shift, scale, name):
    l, d = x.shape
    tm = _row_tile(l)

    def body(x_ref, g_ref, sh_ref, sc_ref, h_ref):
        xv = x_ref[...]
        r = lax.rsqrt(jnp.mean(xv * xv, axis=-1, keepdims=True) + EPS)
        h_ref[...] = (xv * r * (g_ref[...] * (1.0 + sc_ref[...])) + sh_ref[...]).astype(h_ref.dtype)

    return pl.pallas_call(
        body, name=name, grid=(l // tm,),
        in_specs=[pl.BlockSpec((tm, d), lambda i: (i, 0)), _row(d), _row(d), _row(d)],
        out_specs=pl.BlockSpec((tm, d), lambda i: (i, 0)),
        out_shape=jax.ShapeDtypeStruct((l, d), bf16),
        compiler_params=_cparams("parallel"),
    )(x, g, shift, scale)


def post_prenorm_fwd(x, y, gate, g_post, g_pre, shift, scale, name):
    l, d = x.shape
    tm = _row_tile(l)

    def body(x_ref, y_ref, gate_ref, gp_ref, g_ref, sh_ref, sc_ref, o_ref, h_ref):
        yv = y_ref[...]
        r = lax.rsqrt(jnp.mean(yv * yv, axis=-1, keepdims=True) + EPS)
        xv = x_ref[...] + gate_ref[...] * (yv * r * gp_ref[...])
        o_ref[...] = xv
        r = lax.rsqrt(jnp.mean(xv * xv, axis=-1, keepdims=True) + EPS)
        h_ref[...] = (xv * r * (g_ref[...] * (1.0 + sc_ref[...])) + sh_ref[...]).astype(h_ref.dtype)

    blk = pl.BlockSpec((tm, d), lambda i: (i, 0))
    return pl.pallas_call(
        body, name=name, grid=(l // tm,),
        in_specs=[blk, blk] + [_row(d)] * 5, out_specs=[blk, blk],
        out_shape=[jax.ShapeDtypeStruct((l, d), f32), jax.ShapeDtypeStruct((l, d), bf16)],
        compiler_params=_cparams("parallel"),
    )(x, y, gate, g_post, g_pre, shift, scale)


def _post_bwd_rows(dxv, yv, r, gate, gv, dy_ref, dgate_ref, dg_ref):
    yn = yv * r
    dgate_ref[...] += jnp.sum(dxv * yn * gv, axis=0, keepdims=True)
    dyg = dxv * gate
    dg_ref[...] += jnp.sum(dyg * yn, axis=0, keepdims=True)
    dyn = dyg * gv
    dy_ref[...] = (r * (dyn - yn * jnp.mean(dyn * yn, axis=-1, keepdims=True))).astype(dy_ref.dtype)


def final_loss(x, y, gate, g, target):
    l, d = x.shape
    tm = _row_tile(l)

    def body(x_ref, y_ref, gate_ref, g_ref, t_ref, dx_ref, loss_ref, dy_ref, dgate_ref, dg_ref):
        @pl.when(pl.program_id(0) == 0)
        def _():
            loss_ref[...] = jnp.zeros_like(loss_ref)
            dgate_ref[...] = jnp.zeros_like(dgate_ref)
            dg_ref[...] = jnp.zeros_like(dg_ref)

        yv, gate, gv = y_ref[...], gate_ref[...], g_ref[...]
        r = lax.rsqrt(jnp.mean(yv * yv, axis=-1, keepdims=True) + EPS)
        diff = x_ref[...] + gate * (yv * r * gv) - t_ref[...]
        dxv = diff * (1.0 / d)
        dx_ref[...] = dxv
        loss_ref[...] += jnp.sum(diff * diff)
        _post_bwd_rows(dxv, yv, r, gate, gv, dy_ref, dgate_ref, dg_ref)

    blk = pl.BlockSpec((tm, d), lambda i: (i, 0))
    return pl.pallas_call(
        body, name="final_loss", grid=(l // tm,),
        in_specs=[blk, blk, _row(d), _row(d), blk],
        out_specs=[blk, pl.BlockSpec((SUBLANES, HEAD), lambda i: (0, 0)), blk, _row(d), _row(d)],
        out_shape=[jax.ShapeDtypeStruct((l, d), f32), jax.ShapeDtypeStruct((SUBLANES, HEAD), f32),
                   jax.ShapeDtypeStruct((l, d), bf16), jax.ShapeDtypeStruct((1, d), f32), jax.ShapeDtypeStruct((1, d), f32)],
        compiler_params=_cparams("arbitrary"),
    )(x, y, gate, g, target)


def post_bwd(dx, y, gate, g, name):
    l, d = dx.shape
    tm = _row_tile(l)

    def body(dx_ref, y_ref, gate_ref, g_ref, dy_ref, dgate_ref, dg_ref):
        @pl.when(pl.program_id(0) == 0)
        def _():
            dgate_ref[...] = jnp.zeros_like(dgate_ref)
            dg_ref[...] = jnp.zeros_like(dg_ref)

        yv = y_ref[...]
        r = lax.rsqrt(jnp.mean(yv * yv, axis=-1, keepdims=True) + EPS)
        _post_bwd_rows(dx_ref[...], yv, r, gate_ref[...], g_ref[...], dy_ref, dgate_ref, dg_ref)

    blk = pl.BlockSpec((tm, d), lambda i: (i, 0))
    return pl.pallas_call(
        body, name=name, grid=(l // tm,),
        in_specs=[blk, blk, _row(d), _row(d)], out_specs=[blk, _row(d), _row(d)],
        out_shape=[jax.ShapeDtypeStruct((l, d), bf16), jax.ShapeDtypeStruct((1, d), f32),
                   jax.ShapeDtypeStruct((1, d), f32)],
        compiler_params=_cparams("arbitrary"),
    )(dx, y, gate, g)


def prenorm_bwd(dh, x, dx_next, g, scale, name):
    l, d = x.shape
    tm = _row_tile(l)

    def body(dh_ref, x_ref, dxn_ref, g_ref, sc_ref, dx_ref, dsh_ref, dsc_ref, dg_ref):
        @pl.when(pl.program_id(0) == 0)
        def _():
            dsh_ref[...] = jnp.zeros_like(dsh_ref)
            dsc_ref[...] = jnp.zeros_like(dsc_ref)
            dg_ref[...] = jnp.zeros_like(dg_ref)

        xv, dhv, gv, sc1 = x_ref[...], dh_ref[...], g_ref[...], 1.0 + sc_ref[...]
        r = lax.rsqrt(jnp.mean(xv * xv, axis=-1, keepdims=True) + EPS)
        xn = xv * r
        dhx = dhv * xn
        dsh_ref[...] += jnp.sum(dhv, axis=0, keepdims=True)
        dsc_ref[...] += jnp.sum(dhx * gv, axis=0, keepdims=True)
        dg_ref[...] += jnp.sum(dhx * sc1, axis=0, keepdims=True)
        dxn = dhv * (gv * sc1)
        dx_ref[...] = dxn_ref[...] + r * (dxn - xn * jnp.mean(dxn * xn, axis=-1, keepdims=True))

    blk = pl.BlockSpec((tm, d), lambda i: (i, 0))
    return pl.pallas_call(
        body, name=name, grid=(l // tm,),
        in_specs=[blk, blk, blk, _row(d), _row(d)], out_specs=[blk, _row(d), _row(d), _row(d)],
        out_shape=[jax.ShapeDtypeStruct((l, d), f32)] + [jax.ShapeDtypeStruct((1, d), f32)] * 3,
        compiler_params=_cparams("arbitrary"),
    )(dh, x, dx_next, g, scale)


def _tril_mask():
    r = lax.broadcasted_iota(jnp.int32, (HEAD, HEAD), 0)
    c = lax.broadcasted_iota(jnp.int32, (HEAD, HEAD), 1)
    return r >= c


def sgu_fwd(proj, norm_g, w_s, b_s):
    l = proj.shape[0]
    nh = w_s.shape[0]
    wa = nh * HEAD

    def body(au_ref, av_ref, az_ref, ng_ref, w_ref, b_ref, o_ref):
        tril = _tril_mask()
        for h in range(nh):
            sl = slice(h * HEAD, (h + 1) * HEAD)
            gv = _gelu(av_ref[:, sl].astype(f32))
            r = lax.rsqrt(jnp.mean(gv * gv, axis=-1, keepdims=True) + EPS)
            vh = gv * r * ng_ref[:, sl]
            wm = jnp.where(tril, w_ref[h], 0.0).astype(bf16)
            s = _dot(wm, vh.astype(bf16)) + b_ref[h]
            o_ref[:, sl] = (_gelu(au_ref[:, sl].astype(f32)) * s * _silu(az_ref[:, sl].astype(f32))).astype(o_ref.dtype)

    def col(j):
        return pl.BlockSpec((HEAD, wa), lambda n: (n, j))

    return pl.pallas_call(
        body, name="sgu_fwd", grid=(l // HEAD,),
        in_specs=[col(0), col(1), col(2), _row(wa),
                  pl.BlockSpec((nh, HEAD, HEAD), lambda n: (0, 0, 0)), pl.BlockSpec((nh, HEAD, 1), lambda n: (0, 0, 0))],
        out_specs=pl.BlockSpec((HEAD, wa), lambda n: (n, 0)),
        out_shape=jax.ShapeDtypeStruct((l, wa), bf16),
        compiler_params=_cparams("parallel"),
    )(proj, proj, proj, norm_g, w_s, b_s)


def sgu_bwd(proj, dcat, norm_g, w_s, b_s):
    l = proj.shape[0]
    nh = w_s.shape[0]
    wa = nh * HEAD

    def body(au_ref, av_ref, az_ref, do_ref, ng_ref, w_ref, b_ref, da_ref, dw_ref, db_ref, dng_ref):
        @pl.when(pl.program_id(0) == 0)
        def _():
            dw_ref[...] = jnp.zeros_like(dw_ref)
            db_ref[...] = jnp.zeros_like(db_ref)
            dng_ref[...] = jnp.zeros_like(dng_ref)

        tril = _tril_mask()
        for h in range(nh):
            sl = slice(h * HEAD, (h + 1) * HEAD)
            au, av, az = au_ref[:, sl].astype(f32), av_ref[:, sl].astype(f32), az_ref[:, sl].astype(f32)
            ng = ng_ref[:, sl]
            gv = _gelu(av)
            r = lax.rsqrt(jnp.mean(gv * gv, axis=-1, keepdims=True) + EPS)
            gvn = gv * r
            vh = (gvn * ng).astype(bf16)
            wm = jnp.where(tril, w_ref[h], 0.0).astype(bf16)
            s = _dot(wm, vh) + b_ref[h]
            gu, sz = _gelu(au), _silu(az)
            dov = do_ref[:, sl].astype(f32)
            da_ref[:, sl] = (dov * s * sz * _gelu_grad(au)).astype(da_ref.dtype)
            da_ref[:, 2 * wa + h * HEAD:2 * wa + (h + 1) * HEAD] = (dov * gu * s * _silu_grad(az)).astype(da_ref.dtype)
            ds = dov * gu * sz
            db_ref[h] += jnp.sum(ds, axis=-1, keepdims=True)
            dsb = ds.astype(bf16)
            dw_ref[h] += jnp.where(tril, _dot_nt(dsb, vh), 0.0)
            dvh = _dot_tn(wm, dsb)
            dng_ref[:, sl] += jnp.sum(dvh * gvn, axis=0, keepdims=True)
            dgvn = dvh * ng
            dgv = r * (dgvn - gvn * jnp.mean(dgvn * gvn, axis=-1, keepdims=True))
            da_ref[:, wa + h * HEAD:wa + (h + 1) * HEAD] = (dgv * _gelu_grad(av)).astype(da_ref.dtype)

    def col(j):
        return pl.BlockSpec((HEAD, wa), lambda n: (n, j))

    whole_w = pl.BlockSpec((nh, HEAD, HEAD), lambda n: (0, 0, 0))
    whole_b = pl.BlockSpec((nh, HEAD, 1), lambda n: (0, 0, 0))
    return pl.pallas_call(
        body, name="sgu_bwd", grid=(l // HEAD,),
        in_specs=[col(0), col(1), col(2), col(0), _row(wa), whole_w, whole_b],
        out_specs=[pl.BlockSpec((HEAD, 3 * wa), lambda n: (n, 0)), whole_w, whole_b, _row(wa)],
        out_shape=[jax.ShapeDtypeStruct((l, 3 * wa), bf16), jax.ShapeDtypeStruct((nh, HEAD, HEAD), f32),
                   jax.ShapeDtypeStruct((nh, HEAD, 1), f32), jax.ShapeDtypeStruct((1, wa), f32)],
        compiler_params=_cparams("arbitrary"),
    )(proj, proj, proj, dcat, norm_g, w_s, b_s)


_LOG2E = 1.0 / math.log(2.0)


def _sb_scores(q, k, scale):
    z = _dot_nt(q, k) * (scale * _LOG2E)
    return z, jnp.maximum(z, 0.0) + jnp.log2(1.0 + jnp.exp2(-jnp.abs(z)))


def _sb_sum_matrix(tri):
    s = lax.broadcasted_iota(jnp.int32, (2 * HEAD, 2 * HEAD), 0) % HEAD
    j = lax.broadcasted_iota(jnp.int32, (2 * HEAD, 2 * HEAD), 1)
    return jnp.where(jnp.logical_or(j >= HEAD, tri(s, j)), 1.0, 0.0).astype(bf16)


def _sb_sums(x, sums):
    c2 = _dot(jnp.concatenate(_split_bf16(x), axis=1), sums)
    return c2[:, :HEAD], c2[:, HEAD:]


def _sb_q_tile(l, most=512):
    return _tile(l, tuple(t for t in (1024, 512, 256, 128) if t <= most))


def _sb_band_levels(band):
    return _tile(band, (4, 2, 1))


def _sb_heads_per_step(nh, most):
    return _tile(nh, tuple(h for h in (4, 2) if h <= most))


def sb_fwd(proj, nh):
    l = proj.shape[0]
    wb = nh * HEAD
    tq = _sb_q_tile(l, 1024)
    band = tq // HEAD
    hp = _sb_heads_per_step(nh, 2)
    levels = _sb_band_levels(band)
    scale = 1.0 / math.sqrt(HEAD)
    qc, kc, vc, zc = 3 * nh, 4 * nh, 5 * nh, 6 * nh

    def body(q_ref, k_ref, v_ref, bz_ref, o_ref, att_ref, tot_ref):
        i = pl.program_id(1)
        sums = _sb_sum_matrix(lambda s, j: s > j)
        t_pos = i * tq + lax.broadcasted_iota(jnp.int32, (tq, HEAD), 0)
        s_off = lax.broadcasted_iota(jnp.int32, (tq, HEAD), 1)

        def step(j, carry, masked, row0=0):
            rows = pl.ds(pl.multiple_of(j * HEAD, HEAD), HEAD)
            out = []
            for e in range(hp):
                acc, tot = carry[e]
                sl = slice(e * HEAD, (e + 1) * HEAD)
                z, sp = _sb_scores(q_ref[row0:, sl], k_ref[rows, sl], scale)
                lb = z - sp
                if masked:
                    mask = s_off[row0:] + j * HEAD < t_pos[row0:]
                    sp = jnp.where(mask, sp, 0.0)
                later, total = _sb_sums(sp, sums)
                w = jnp.exp2(lb + tot[row0:] - later)
                if masked:
                    w = jnp.where(mask, w, 0.0)
                new = (acc[row0:] + _dot(w.astype(bf16), v_ref[rows, sl]), tot[row0:] - total)
                out.append(tuple(jnp.concatenate([old[:row0], upd]) if row0 else upd for old, upd in zip(carry[e], new)))
            return tuple(out)

        zero = jnp.zeros((tq, HEAD), f32)
        carry = ((zero, zero),) * hp
        for lv in reversed(range(levels)):
            carry = lax.fori_loop(
                0, band // levels,
                lambda t, c, lv=lv: step(band * i + (lv + 1) * (band // levels) - 1 - t, c, True, lv * (tq // levels)), carry)
        carry = lax.fori_loop(0, band * i, lambda t, c: step(band * i - 1 - t, c, False), carry)
        for e in range(hp):
            acc, tot = carry[e]
            sl = slice(e * HEAD, (e + 1) * HEAD)
            att_ref[:, sl] = acc.astype(att_ref.dtype)
            o_ref[:, sl] = (acc * _silu(bz_ref[:, sl].astype(f32))).astype(o_ref.dtype)
            tot_ref[e] = tot[:, :1]

    blk = lambda c0: pl.BlockSpec((tq, hp * HEAD), lambda g, i: (i, c0 // hp + g))
    head = lambda c0: pl.BlockSpec((l, hp * HEAD), lambda g, i: (0, c0 // hp + g))
    return pl.pallas_call(
        body, name="sb_fwd", grid=(nh // hp, l // tq),
        in_specs=[blk(qc), head(kc), head(vc), blk(zc)],
        out_specs=[blk(0), blk(0), pl.BlockSpec((hp, tq, 1), lambda g, i: (g, i, 0))],
        out_shape=[jax.ShapeDtypeStruct((l, wb), bf16), jax.ShapeDtypeStruct((l, wb), bf16),
                   jax.ShapeDtypeStruct((nh, l, 1), f32)],
        compiler_params=_cparams("parallel", "arbitrary"),
    )(proj, proj, proj, proj)


def sb_bwd(proj, dcat, att, tot, nh):
    l = proj.shape[0]
    wb = nh * HEAD
    tq = _sb_q_tile(l, 1024)
    band = tq // HEAD
    nq = l // tq
    hp = _sb_heads_per_step(nh, 2)
    levels = _sb_band_levels(band)
    scale = 1.0 / math.sqrt(HEAD)
    qc, kc, vc, zc = 3 * nh, 4 * nh, 5 * nh, 6 * nh

    def body(q_ref, k_ref, v_ref, bz_ref, do_ref, att_ref, tot_ref, dq_ref, dk_ref, dv_ref, dbz_ref, dk_acc, dv_acc,
             dob_ref):
        i = pl.program_id(1)

        @pl.when(i == 0)
        def _():
            dk_acc[...] = jnp.zeros_like(dk_acc)
            dv_acc[...] = jnp.zeros_like(dv_acc)

        bz = bz_ref[...].astype(f32)
        dov = do_ref[...].astype(f32)
        dbz_ref[...] = (dov * att_ref[...].astype(f32) * _silu_grad(bz)).astype(dbz_ref.dtype)
        dob_ref[...] = (dov * _silu(bz)).astype(bf16)
        upto = _sb_sum_matrix(lambda s, j: s <= j)
        before = _sb_sum_matrix(lambda j, s: j < s)
        t_pos = i * tq + lax.broadcasted_iota(jnp.int32, (tq, HEAD), 0)
        s_off = lax.broadcasted_iota(jnp.int32, (tq, HEAD), 1)

        def step(j, carry, masked, row0=0):
            rows = pl.ds(pl.multiple_of(j * HEAD, HEAD), HEAD)
            out = []
            for h in range(hp):
                dq, sp_seen, e_seen = (c[row0:] for c in carry[h])
                sl = slice(h * HEAD, (h + 1) * HEAD)
                q, kj, vj, dob = q_ref[row0:, sl], k_ref[rows, sl], v_ref[rows, sl], dob_ref[row0:, sl]
                z, sp = _sb_scores(q, kj, scale)
                lb = z - sp
                if masked:
                    mask = s_off[row0:] + j * HEAD < t_pos[row0:]
                    sp = jnp.where(mask, sp, 0.0)
                sp_upto, sp_total = _sb_sums(sp, upto)
                w = jnp.exp2(lb + sp_seen + sp_upto)
                if masked:
                    w = jnp.where(mask, w, 0.0)
                dv_acc[rows, sl] += _dot_tn(w.astype(bf16), dob)
                e = _dot_nt(dob, vj) * w
                e_before, e_total = _sb_sums(e, before)
                dz = (e - (e + e_seen + e_before) * jnp.exp2(lb)) * scale
                if masked:
                    dz = jnp.where(mask, dz, 0.0)
                dz = dz.astype(bf16)
                dk_acc[rows, sl] += _dot_tn(dz, q)
                new = (dq + _dot(dz, kj), sp_seen + sp_total, e_seen + e_total)
                out.append(tuple(jnp.concatenate([old[:row0], upd]) if row0 else upd for old, upd in zip(carry[h], new)))
            return tuple(out)

        zero = jnp.zeros((tq, HEAD), f32)
        init = tuple((zero, jnp.broadcast_to(tot_ref[h], (tq, HEAD)), zero) for h in range(hp))
        carry = lax.fori_loop(0, band * i, lambda j, c: step(j, c, False), init)
        for lv in range(levels):
            carry = lax.fori_loop(
                0, band // levels,
                lambda t, c, lv=lv: step(band * i + lv * (band // levels) + t, c, True, lv * (tq // levels)), carry)
        for h in range(hp):
            dq_ref[:, h * HEAD:(h + 1) * HEAD] = carry[h][0].astype(dq_ref.dtype)

        @pl.when(i == nq - 1)
        def _():
            dk_ref[...] = dk_acc[...].astype(dk_ref.dtype)
            dv_ref[...] = dv_acc[...].astype(dv_ref.dtype)

    blk = lambda c0: pl.BlockSpec((tq, hp * HEAD), lambda g, i: (i, c0 // hp + g))
    head = lambda c0: pl.BlockSpec((l, hp * HEAD), lambda g, i: (0, c0 // hp + g))
    return pl.pallas_call(
        body, name="sb_bwd", grid=(nh // hp, nq),
        in_specs=[blk(qc), head(kc), head(vc), blk(zc), blk(nh), blk(0),
                  pl.BlockSpec((hp, tq, 1), lambda g, i: (g, i, 0))],
        out_specs=[blk(0), head(0), head(0), blk(0)],
        out_shape=[jax.ShapeDtypeStruct((l, wb), bf16)] * 4,
        scratch_shapes=[pltpu.VMEM((l, hp * HEAD), f32), pltpu.VMEM((l, hp * HEAD), f32),
                        pltpu.VMEM((tq, hp * HEAD), bf16)],
        compiler_params=_cparams("parallel", "arbitrary"),
    )(proj, proj, proj, proj, dcat, att, tot)


def _disc(lr, li, ldt):
    dt = jnp.exp(ldt)
    mag = jnp.exp(lr * dt)
    a_re = mag * jnp.cos(li * dt)
    a_im = mag * jnp.sin(li * dt)
    den = lr * lr + li * li
    nr = a_re - 1.0
    return a_re, a_im, (nr * lr + a_im * li) / den, (a_im * lr - nr * li) / den


def s5_params_fwd(lr, li, ldt, bt_re, bt_im):
    g, c, p = bt_re.shape

    def body(lr_ref, li_ref, ldt_ref, br_ref, bi_ref, ar_ref, ai_ref, bbr_ref, bbi_ref):
        a_re, a_im, cr, ci = _disc(lr_ref[...], li_ref[...], ldt_ref[...])
        ar_ref[...] = a_re
        ai_ref[...] = a_im
        for k in range(c):
            br, bi = br_ref[:, k, :], bi_ref[:, k, :]
            bbr_ref[:, k, :] = cr * br - ci * bi
            bbi_ref[:, k, :] = cr * bi + ci * br

    return pl.pallas_call(
        body, name="s5_params_fwd",
        out_shape=[jax.ShapeDtypeStruct((g, p), f32)] * 2 + [jax.ShapeDtypeStruct((g, c, p), f32)] * 2,
    )(lr, li, ldt, bt_re, bt_im)


def s5_params_bwd(lr, li, ldt, bt_re, bt_im, da_re, da_im, dbbt_re, dbbt_im):
    g, c, p = bt_re.shape

    def body(lr_ref, li_ref, ldt_ref, br_ref, bi_ref, dar_ref, dai_ref, dbbr_ref, dbbi_ref,
             dlr_ref, dli_ref, dldt_ref, dbr_ref, dbi_ref):
        (a_re, a_im, cr, ci), vjp = jax.vjp(_disc, lr_ref[...], li_ref[...], ldt_ref[...])
        dcr = jnp.zeros((g, p), f32)
        dci = jnp.zeros((g, p), f32)
        for k in range(c):
            br, bi = br_ref[:, k, :], bi_ref[:, k, :]
            dr, di = dbbr_ref[:, k, :], dbbi_ref[:, k, :]
            dcr += dr * br + di * bi
            dci += di * br - dr * bi
            dbr_ref[:, k, :] = cr * dr + ci * di
            dbi_ref[:, k, :] = cr * di - ci * dr
        dlr, dli, dldt = vjp((dar_ref[...], dai_ref[...], dcr, dci))
        dlr_ref[...] = dlr
        dli_ref[...] = dli
        dldt_ref[...] = dldt

    return pl.pallas_call(
        body, name="s5_params_bwd",
        out_shape=[jax.ShapeDtypeStruct((g, p), f32)] * 2 + [jax.ShapeDtypeStruct((g, 1), f32)]
        + [jax.ShapeDtypeStruct((g, c, p), f32)] * 2,
    )(lr, li, ldt, bt_re, bt_im, da_re, da_im, dbbt_re, dbbt_im)


def _cmul(ar, ai, br, bi):
    return ar * br - ai * bi, ar * bi + ai * br


def _power_tables(ar, ai):
    rows = lax.broadcasted_iota(jnp.int32, (SUBLANES, ar.shape[1]), 0)
    pr = jnp.zeros((SUBLANES, ar.shape[1]), f32)
    pi = jnp.zeros((SUBLANES, ar.shape[1]), f32)
    cr, ci = ar, ai
    pows = {}
    for r in range(SUBLANES):
        pows[r + 1] = (cr, ci)
        pr = jnp.where(rows == r, cr, pr)
        pi = jnp.where(rows == r, ci, pi)
        cr, ci = _cmul(cr, ci, ar, ai)
    return [pows[1], pows[2], pows[4]], pr, pi


def _ssm_time_tile(l):
    return _tile(l, (512, 256, 128))


def ssm_fwd(u, bre3, bim3, cre3, cimn3, a_re, a_im, d_skip):
    l, w = u.shape
    nj = w // HEAD
    ns = STATES_PER_LANE_BLOCK
    tt = _ssm_time_tile(l)

    def body(u_ref, bre_ref, bim_ref, cre_ref, cim_ref, ar_ref, ai_ref, d_ref, y_ref, hr_ref, hi_ref, cr_ref, ci_ref):
        @pl.when(pl.program_id(1) == 0)
        def _():
            cr_ref[...] = jnp.zeros_like(cr_ref)
            ci_ref[...] = jnp.zeros_like(ci_ref)

        uv = u_ref[...]
        hr_ref[...] = _dot(uv, bre_ref[...])
        hi_ref[...] = _dot(uv, bim_ref[...])
        steps, pr, pi = _power_tables(ar_ref[...], ai_ref[...])
        rows = lax.broadcasted_iota(jnp.int32, (SUBLANES, ns), 0)

        def blk(b, carry):
            cr, ci = carry
            sl = pl.ds(pl.multiple_of(b * SUBLANES, SUBLANES), SUBLANES)
            xr, xi = hr_ref[sl, :], hi_ref[sl, :]
            for d, (sr_, si_) in zip((1, 2, 4), steps):
                keep = rows >= d
                qr = jnp.where(keep, pltpu.roll(xr, d, axis=0), 0.0)
                qi = jnp.where(keep, pltpu.roll(xi, d, axis=0), 0.0)
                mr, mi = _cmul(sr_, si_, qr, qi)
                xr, xi = xr + mr, xi + mi
            mr, mi = _cmul(pr, pi, cr, ci)
            xr, xi = xr + mr, xi + mi
            hr_ref[sl, :] = xr
            hi_ref[sl, :] = xi
            return xr[SUBLANES - 1:, :], xi[SUBLANES - 1:, :]

        cr, ci = lax.fori_loop(0, tt // SUBLANES, blk, (cr_ref[...], ci_ref[...]))
        cr_ref[...] = cr
        ci_ref[...] = ci
        y = _dot(hr_ref[...].astype(bf16), cre_ref[...]) + _dot(hi_ref[...].astype(bf16), cim_ref[...])
        y_ref[...] = y + d_ref[...] * uv.astype(f32)

    lane = pl.BlockSpec((tt, HEAD), lambda j, i: (i, j))
    st = pl.BlockSpec((tt, ns), lambda j, i: (i, j))
    b3 = pl.BlockSpec((None, HEAD, ns), lambda j, i: (j, 0, 0))
    c3 = pl.BlockSpec((None, ns, HEAD), lambda j, i: (j, 0, 0))
    arow = pl.BlockSpec((1, ns), lambda j, i: (0, j))
    return pl.pallas_call(
        body, name="ssm_fwd", grid=(nj, l // tt),
        in_specs=[lane, b3, b3, c3, c3, arow, arow, pl.BlockSpec((1, HEAD), lambda j, i: (0, j))],
        out_specs=[lane, st, st],
        out_shape=[jax.ShapeDtypeStruct((l, w), f32), jax.ShapeDtypeStruct((l, nj * ns), f32),
                   jax.ShapeDtypeStruct((l, nj * ns), f32)],
        scratch_shapes=[pltpu.VMEM((1, ns), f32), pltpu.VMEM((1, ns), f32)],
        compiler_params=_cparams("parallel", "arbitrary"),
    )(u, bre3, bim3, cre3, cimn3, a_re, a_im, d_skip)


def ssm_bwd(dy, u, h_re, h_im, bre3, bim3, cre3, cimn3, a_re, a_im, d_skip):
    l, w = u.shape
    nj = w // HEAD
    ns = STATES_PER_LANE_BLOCK
    tt = _ssm_time_tile(l)
    nt = l // tt

    def body(dy_ref, u_ref, hr_ref, hi_ref, bre_ref, bim_ref, cre_ref, cim_ref, ar_ref, ai_ref, d_ref,
             du_ref, dd_ref, dar_ref, dai_ref, dbre_ref, dbim_ref, dcre_ref, dcim_ref, kr_ref, ki_ref, cr_ref, ci_ref,
             accr_ref, acci_ref):
        i = pl.program_id(1)

        @pl.when(i == 0)
        def _():
            for ref in (cr_ref, ci_ref, accr_ref, acci_ref, dd_ref, dbre_ref, dbim_ref, dcre_ref, dcim_ref):
                ref[...] = jnp.zeros_like(ref)

        dyv = dy_ref[...]
        dyb = dyv.astype(bf16)
        uv = u_ref[...]
        kr_ref[...] = _dot_nt(dyb, cre_ref[...])
        ki_ref[...] = _dot_nt(dyb, cim_ref[...])
        steps, pr, pi = _power_tables(ar_ref[...], -ai_ref[...])
        rows = lax.broadcasted_iota(jnp.int32, (SUBLANES, ns), 0)
        qr = jnp.zeros((SUBLANES, ns), f32)
        qi = jnp.zeros((SUBLANES, ns), f32)
        for r in range(SUBLANES):
            qr = jnp.where(rows == r, pr[SUBLANES - 1 - r:SUBLANES - r, :], qr)
            qi = jnp.where(rows == r, pi[SUBLANES - 1 - r:SUBLANES - r, :], qi)
        nb = tt // SUBLANES

        def blk(t, carry):
            cr, ci, accr, acci = carry
            sl = pl.ds(pl.multiple_of((nb - 1 - t) * SUBLANES, SUBLANES), SUBLANES)
            xr, xi = kr_ref[sl, :], ki_ref[sl, :]
            for d, (sr_, si_) in zip((1, 2, 4), steps):
                keep = rows < SUBLANES - d
                zr = jnp.where(keep, pltpu.roll(xr, SUBLANES - d, axis=0), 0.0)
                zi = jnp.where(keep, pltpu.roll(xi, SUBLANES - d, axis=0), 0.0)
                mr, mi = _cmul(sr_, si_, zr, zi)
                xr, xi = xr + mr, xi + mi
            mr, mi = _cmul(qr, qi, cr, ci)
            xr, xi = xr + mr, xi + mi
            kr_ref[sl, :] = xr
            ki_ref[sl, :] = xi
            last = rows == SUBLANES - 1
            nr = jnp.where(last, cr, pltpu.roll(xr, SUBLANES - 1, axis=0))
            ni = jnp.where(last, ci, pltpu.roll(xi, SUBLANES - 1, axis=0))
            hr, hi = hr_ref[sl, :], hi_ref[sl, :]
            accr = accr + nr * hr + ni * hi
            acci = acci + ni * hr - nr * hi
            return xr[:1, :], xi[:1, :], accr, acci

        cr, ci, accr, acci = lax.fori_loop(0, nb, blk, (cr_ref[...], ci_ref[...], accr_ref[...], acci_ref[...]))
        cr_ref[...] = cr
        ci_ref[...] = ci
        accr_ref[...] = accr
        acci_ref[...] = acci
        kr, ki = kr_ref[...].astype(bf16), ki_ref[...].astype(bf16)
        du = _dot_nt(kr, bre_ref[...]) + _dot_nt(ki, bim_ref[...]) + d_ref[...] * dyv
        du_ref[...] = du.astype(du_ref.dtype)
        dd_ref[...] += jnp.sum(dyv * uv.astype(f32), axis=0, keepdims=True)
        dbre_ref[...] += _dot_tn(uv, kr)
        dbim_ref[...] += _dot_tn(uv, ki)
        dcre_ref[...] += _dot_tn(hr_ref[...].astype(bf16), dyb)
        dcim_ref[...] += _dot_tn(hi_ref[...].astype(bf16), dyb)

        @pl.when(i == nt - 1)
        def _():
            dar_ref[...] = jnp.sum(accr_ref[...], axis=0, keepdims=True)
            dai_ref[...] = jnp.sum(acci_ref[...], axis=0, keepdims=True)

    lane = pl.BlockSpec((tt, HEAD), lambda j, i: (nt - 1 - i, j))
    st = pl.BlockSpec((tt, ns), lambda j, i: (nt - 1 - i, j))
    b3 = pl.BlockSpec((None, HEAD, ns), lambda j, i: (j, 0, 0))
    c3 = pl.BlockSpec((None, ns, HEAD), lambda j, i: (j, 0, 0))
    arow = pl.BlockSpec((1, ns), lambda j, i: (0, j))
    drow = pl.BlockSpec((1, HEAD), lambda j, i: (0, j))
    return pl.pallas_call(
        body, name="ssm_bwd", grid=(nj, nt),
        in_specs=[lane, lane, st, st, b3, b3, c3, c3, arow, arow, drow],
        out_specs=[lane, drow, arow, arow, b3, b3, c3, c3],
        out_shape=[jax.ShapeDtypeStruct((l, w), bf16), jax.ShapeDtypeStruct((1, w), f32),
                   jax.ShapeDtypeStruct((1, nj * ns), f32), jax.ShapeDtypeStruct((1, nj * ns), f32),
                   jax.ShapeDtypeStruct((nj, HEAD, ns), f32), jax.ShapeDtypeStruct((nj, HEAD, ns), f32),
                   jax.ShapeDtypeStruct((nj, ns, HEAD), f32), jax.ShapeDtypeStruct((nj, ns, HEAD), f32)],
        scratch_shapes=[pltpu.VMEM((tt, ns), f32), pltpu.VMEM((tt, ns), f32), pltpu.VMEM((1, ns), f32),
                        pltpu.VMEM((1, ns), f32), pltpu.VMEM((SUBLANES, ns), f32), pltpu.VMEM((SUBLANES, ns), f32)],
        compiler_params=_cparams("parallel", "arbitrary"),
    )(dy, u, h_re, h_im, bre3, bim3, cre3, cimn3, a_re, a_im, d_skip)


def glu_fwd(y, z_src, w_glu, b_glu):
    l, w = y.shape
    tm = _row_tile(l)

    def body(y_ref, z_ref, w_ref, b_ref, g_ref, t_ref, o_ref):
        g = _gelu(y_ref[...])
        gb = g.astype(bf16)
        t = _dot(gb, w_ref[...]) + b_ref[...]
        g_ref[...] = gb
        t_ref[...] = t
        o_ref[...] = (g * jax.nn.sigmoid(t) * _silu(z_ref[...].astype(f32))).astype(o_ref.dtype)

    blk = pl.BlockSpec((tm, w), lambda i: (i, 0))
    return pl.pallas_call(
        body, name="glu_fwd", grid=(l // tm,),
        in_specs=[blk, pl.BlockSpec((tm, w), lambda i: (i, 1)), pl.BlockSpec((w, w), lambda i: (0, 0)), _row(w)],
        out_specs=[blk, blk, blk],
        out_shape=[jax.ShapeDtypeStruct((l, w), bf16), jax.ShapeDtypeStruct((l, w), f32),
                   jax.ShapeDtypeStruct((l, w), bf16)],
        compiler_params=_cparams("parallel"),
    )(y, z_src, w_glu, b_glu)


def glu_bwd(dout, y, t, z_src, w_glu):
    l, w = y.shape
    tm = _row_tile(l)

    def body(do_ref, y_ref, t_ref, z_ref, w_ref, dy_ref, dz_ref, dt_ref, db_ref):
        @pl.when(pl.program_id(0) == 0)
        def _():
            db_ref[...] = jnp.zeros_like(db_ref)

        yv, zv, dov = y_ref[...], z_ref[...].astype(f32), do_ref[...]
        g = _gelu(yv)
        sg = jax.nn.sigmoid(t_ref[...])
        dy2 = dov * _silu(zv)
        dz_ref[...] = (dov * g * sg * _silu_grad(zv)).astype(dz_ref.dtype)
        dt = dy2 * g * sg * (1.0 - sg)
        dtb = dt.astype(bf16)
        dt_ref[...] = dtb
        db_ref[...] += jnp.sum(dt, axis=0, keepdims=True)
        dg = dy2 * sg + _dot_nt(dtb, w_ref[...])
        dy_ref[...] = dg * _gelu_grad(yv)

    blk = pl.BlockSpec((tm, w), lambda i: (i, 0))
    return pl.pallas_call(
        body, name="glu_bwd", grid=(l // tm,),
        in_specs=[blk, blk, blk, pl.BlockSpec((tm, w), lambda i: (i, 1)), pl.BlockSpec((w, w), lambda i: (0, 0))],
        out_specs=[blk, blk, blk, _row(w)],
        out_shape=[jax.ShapeDtypeStruct((l, w), f32), jax.ShapeDtypeStruct((l, w), bf16),
                   jax.ShapeDtypeStruct((l, w), bf16), jax.ShapeDtypeStruct((1, w), f32)],
        compiler_params=_cparams("arbitrary"),
    )(dout, y, t, z_src, w_glu)


def _adamw(w, g, m, v):
    m = ADAM_B1 * m + (1.0 - ADAM_B1) * g
    v = ADAM_B2 * v + (1.0 - ADAM_B2) * (g * g)
    m_hat = m / (1.0 - ADAM_B1 ** ADAM_STEP)
    v_hat = v / (1.0 - ADAM_B2 ** ADAM_STEP)
    return -ADAM_LR * (m_hat / (jnp.sqrt(v_hat) + ADAM_EPS) + ADAM_WD * w), m, v


def adam_reduce(pieces, w, m, v, name):
    r, c = w.shape
    n = pieces.shape[0]
    tr = _tile(r, (256, 128, 64, 32, 16, 8))

    def body(p_ref, w_ref, m_ref, v_ref, g_ref, d_ref, nm_ref, nv_ref):
        g = p_ref[0].astype(f32)
        for s in range(1, n):
            g = g + p_ref[s].astype(f32)
        g_ref[...] = g
        d_ref[...], nm_ref[...], nv_ref[...] = _adamw(w_ref[...], g, m_ref[...], v_ref[...])

    blk = pl.BlockSpec((tr, c), lambda i: (i, 0))
    return pl.pallas_call(
        body, name=name, grid=(r // tr,),
        in_specs=[pl.BlockSpec((n, tr, c), lambda i: (0, i, 0)), blk, blk, blk],
        out_specs=[blk] * 4, out_shape=[jax.ShapeDtypeStruct((r, c), f32)] * 4,
        compiler_params=_cparams("parallel"),
    )(pieces, w, m, v)


def adam_w_mod(cond_t, dm, w, m, v):
    nl, d, cols = w.shape
    tr = _tile(d, (512, 256, 128))

    def body(c_ref, dm_ref, w_ref, m_ref, v_ref, g_ref, d_ref, nm_ref, nv_ref):
        g = jnp.dot(c_ref[...], dm_ref[...], preferred_element_type=f32, precision=lax.Precision.HIGHEST)
        g_ref[...] = g
        d_ref[...], nm_ref[...], nv_ref[...] = _adamw(w_ref[...], g, m_ref[...], v_ref[...])

    blk = pl.BlockSpec((None, tr, cols), lambda l, i: (l, i, 0))
    return pl.pallas_call(
        body, name="adam_w_mod", grid=(nl, d // tr),
        in_specs=[pl.BlockSpec((tr, N_DEV), lambda l, i: (i, 0)), pl.BlockSpec((None, N_DEV, cols), lambda l, i: (l, 0, 0)),
                  blk, blk, blk],
        out_specs=[blk] * 4, out_shape=[jax.ShapeDtypeStruct((nl, d, cols), f32)] * 4,
        compiler_params=_cparams("parallel", "parallel"),
    )(cond_t, dm, w, m, v)


def silu_rows(c_all):
    def body(c_ref, o_ref):
        o_ref[...] = _silu(c_ref[...])

    return pl.pallas_call(body, name="silu_rows", out_shape=jax.ShapeDtypeStruct(c_all.shape, f32))(c_all)


def _block_diag(x):
    g, a, b = x.shape
    nj = g // GROUPS_PER_LANE_BLOCK
    eye = jnp.eye(GROUPS_PER_LANE_BLOCK, dtype=x.dtype)
    x5 = x.reshape(nj, GROUPS_PER_LANE_BLOCK, a, b)
    return jnp.einsum("jgab,gh->jgahb", x5, eye).reshape(nj, GROUPS_PER_LANE_BLOCK * a, GROUPS_PER_LANE_BLOCK * b)


def _diag_blocks(x, a, b):
    nj = x.shape[0]
    x5 = x.reshape(nj, GROUPS_PER_LANE_BLOCK, a, GROUPS_PER_LANE_BLOCK, b)
    eye = jnp.eye(GROUPS_PER_LANE_BLOCK, dtype=x.dtype)
    return jnp.einsum("jgahb,gh->jgab", x5, eye).reshape(nj * GROUPS_PER_LANE_BLOCK, a, b)


PACK_ROW = SUBLANES * HEAD


def _pack(parts, row_multiple=SUBLANES):
    rows = []
    for p in parts:
        flat = p.reshape(-1)
        pad = (-flat.shape[0]) % PACK_ROW
        if pad:
            flat = jnp.concatenate([flat, jnp.zeros((pad,), flat.dtype)])
        rows.append(flat.reshape(-1, HEAD))
    pad = (-sum(r.shape[0] for r in rows)) % row_multiple
    if pad:
        rows.append(jnp.zeros((pad, HEAD), rows[0].dtype))
    return jnp.concatenate(rows, axis=0)


def _unpack(packed, shapes):
    out, r0 = [], 0
    for shp in shapes:
        n = math.prod(shp)
        nr = -(-n // PACK_ROW) * SUBLANES
        out.append(packed[r0:r0 + nr].reshape(-1)[:n].reshape(shp))
        r0 += nr
    return out


def adam_small(g, w, m, v):
    r, c = w.shape

    def body(g_ref, w_ref, m_ref, v_ref, d_ref, nm_ref, nv_ref):
        d_ref[...], nm_ref[...], nv_ref[...] = _adamw(w_ref[...], g_ref[...], m_ref[...], v_ref[...])

    tr = max(t for t in range(SUBLANES, 1024 + 1, SUBLANES) if r % t == 0)
    blk = pl.BlockSpec((tr, c), lambda i: (i, 0))
    return pl.pallas_call(
        body, name="adam_small", grid=(r // tr,),
        in_specs=[blk] * 4, out_specs=[blk] * 3, out_shape=[jax.ShapeDtypeStruct((r, c), f32)] * 3,
        compiler_params=_cparams("parallel"),
    )(g, w, m, v)


def kernel(x, c, ln_pre_g, ln_post_g, w_mod, b_mod, w_in_ab, w_out_ab, sgu_norm_g, sgu_w, sgu_b, w_in_ssm, w_out_ssm, lam_re, lam_im, b_re, b_im, c_re, c_im, d_skip, log_dt, w_glu, b_glu, loss_target, m_ln_pre_g, m_ln_post_g, m_w_mod, m_b_mod, m_w_in_ab, m_w_out_ab, m_sgu_norm_g, m_sgu_w, m_sgu_b, m_w_in_ssm, m_w_out_ssm, m_lam_re, m_lam_im, m_b_re, m_b_im, m_c_re, m_c_im, m_d_skip, m_log_dt, m_w_glu, m_b_glu, v_ln_pre_g, v_ln_post_g, v_w_mod, v_b_mod, v_w_in_ab, v_w_out_ab, v_sgu_norm_g, v_sgu_w, v_sgu_b, v_w_in_ssm, v_w_out_ssm, v_lam_re, v_lam_im, v_b_re, v_b_im, v_c_re, v_c_im, v_d_skip, v_log_dt, v_w_glu, v_b_glu):
    me = _my_index()
    x0 = x[0]
    l, d = x0.shape
    target = loss_target[0]
    nh = sgu_w.shape[1]
    wa = nh * HEAD
    n_grp, n_st = lam_re.shape[1], lam_re.shape[2]
    mod_cols = w_mod.shape[2]

    c_all, d_skip_all, b_glu_all = all_gather([c, d_skip, b_glu], "gather_c")
    c_all = c_all.reshape(N_DEV, d)
    d_skip_all = d_skip_all.reshape(1, -1)
    b_glu_all = b_glu_all.reshape(1, -1)

    b_cols = lax.dynamic_slice_in_dim(b_mod, me * mod_cols, mod_cols, axis=1)
    (mod_all,) = all_gather([mod_part(c_all, w_mod, b_cols)], "gather_mod")
    def after(a, first):
        return a + jnp.minimum(jnp.abs(first[(0,) * first.ndim].astype(f32)), 0.0).astype(a.dtype)

    (win_ab3,) = sequencer_exchange(GATHER, [after(w_in_ab[0], mod_all).astype(bf16)], "gather_w_in", 1)
    mod_mine = lax.dynamic_index_in_dim(mod_all, me, axis=2, keepdims=False)
    mod_rows = jnp.transpose(mod_mine, (1, 0, 2)).reshape(2, 3, 1, d)

    def rows(a, i):
        return a[i].reshape(1, d)

    shift0, scale0, gate0 = mod_rows[0, 0], mod_rows[0, 1], mod_rows[0, 2]
    h0 = prenorm_fwd(x0, rows(ln_pre_g, 0), shift0, scale0, "prenorm0")
    wout_ab3, win_ssm3, wout_ssm3, wglu = sequencer_exchange(
        GATHER, [after(w, win_ab3).astype(bf16) for w in (w_out_ab[0], w_in_ssm[0], w_out_ssm[0], w_glu[0])],
        "gather_w_rest", 2)
    proj0 = mm_nn(h0, win_ab3, bf16, "proj0")
    sgu_b3 = sgu_b[0].reshape(nh, HEAD, 1)
    out_a = sgu_fwd(proj0, sgu_norm_g, sgu_w[0], sgu_b3)
    out_b, att, tot = sb_fwd(proj0, nh)
    cat = jnp.concatenate([out_a, out_b], axis=1)
    wout_ab3 = wout_ab3.reshape(1, d, d)
    win_ssm3 = win_ssm3.reshape(1, d, d)
    wglu = wglu.reshape(w_glu.shape[2], w_glu.shape[2])
    y0 = mm_nn(cat, wout_ab3, f32, "out0")

    shift1, scale1, gate1 = mod_rows[1, 0], mod_rows[1, 1], mod_rows[1, 2]
    x1, h1 = post_prenorm_fwd(x0, y0, gate0, rows(ln_post_g, 0), rows(ln_pre_g, 1), shift1, scale1, "post0_prenorm1")
    proj1 = mm_nn(h1, win_ssm3, bf16, "proj1")
    w_ssm = proj1.shape[1] // 2
    ldt = log_dt[0].reshape(n_grp, 1)
    bt_re = jnp.transpose(b_re[0], (0, 2, 1))
    bt_im = jnp.transpose(b_im[0], (0, 2, 1))
    a_re, a_im, bbt_re, bbt_im = s5_params_fwd(lam_re[0], lam_im[0], ldt, bt_re, bt_im)
    bre3 = _block_diag(bbt_re).astype(bf16)
    bim3 = _block_diag(bbt_im).astype(bf16)
    cre3 = _block_diag(jnp.transpose(c_re[0], (0, 2, 1))).astype(bf16)
    cimn3 = _block_diag(-jnp.transpose(c_im[0], (0, 2, 1))).astype(bf16)
    a_re_row, a_im_row = a_re.reshape(1, -1), a_im.reshape(1, -1)
    u = proj1[:, :w_ssm]
    y_ssm, hs_re, hs_im = ssm_fwd(u, bre3, bim3, cre3, cimn3, a_re_row, a_im_row, d_skip_all)
    g_act, t_glu, mix1 = glu_fwd(y_ssm, proj1, wglu, b_glu_all)
    y1 = mm_nn(mix1, wout_ssm3, f32, "out1")

    dx2, loss_tile, dy1, dgate1, dgpost1 = final_loss(x1, y1, gate1, rows(ln_post_g, 1), target)

    dmix1 = mm_nt(dy1, wout_ssm3, f32, "dmix1")
    gw_out_ssm = mm_tn(mix1, dy1, N_DEV, bf16, "gw_out_ssm")
    (p_out_ssm,) = sequencer_exchange(SCATTER, [gw_out_ssm], "scatter_g1", 3)
    dy_ssm, dz1, dt_glu, db_glu = glu_bwd(dmix1, y_ssm, t_glu, proj1, wglu)
    gw_glu = mm_tn(g_act, dt_glu, 1, bf16, "gw_glu").reshape(N_DEV, -1, w_ssm)
    du, dd_skip, da_re, da_im, dbre3, dbim3, dcre3, dcimn3 = ssm_bwd(
        dy_ssm, u, hs_re, hs_im, bre3, bim3, cre3, cimn3, a_re_row, a_im_row, d_skip_all)
    dproj1 = jnp.concatenate([du, dz1], axis=1)
    gw_in_ssm = mm_tn(h1, dproj1, 1, bf16, "gw_in_ssm").reshape(N_DEV, -1, proj1.shape[1])
    p_in_ssm, p_glu = sequencer_exchange(SCATTER, [gw_in_ssm, gw_glu], "scatter_g2", 4)
    dh1 = mm_nt(dproj1, win_ssm3, f32, "dh1")
    dx1, dshift1, dscale1, dgpre1 = prenorm_bwd(dh1, x1, dx2, rows(ln_pre_g, 1), scale1, "prenorm1_bwd")
    dlr, dli, dldt, dbt_re, dbt_im = s5_params_bwd(
        lam_re[0], lam_im[0], ldt, bt_re, bt_im, da_re.reshape(n_grp, n_st), da_im.reshape(n_grp, n_st),
        _diag_blocks(dbre3, SSM_GROUP, n_st), _diag_blocks(dbim3, SSM_GROUP, n_st))
    g_b_re = jnp.transpose(dbt_re, (0, 2, 1))
    g_b_im = jnp.transpose(dbt_im, (0, 2, 1))
    g_c_re = jnp.transpose(_diag_blocks(dcre3, n_st, SSM_GROUP), (0, 2, 1))
    g_c_im = -jnp.transpose(_diag_blocks(dcimn3, n_st, SSM_GROUP), (0, 2, 1))

    dy0, dgate0, dgpost0 = post_bwd(dx1, y0, gate0, rows(ln_post_g, 0), "post0_bwd")
    dcat = mm_nt(dy0, wout_ab3, f32, "dcat")
    gw_out_ab = mm_tn(cat, dy0, 1, bf16, "gw_out_ab").reshape(N_DEV, -1, d)
    (p_out_ab,) = sequencer_exchange(SCATTER, [gw_out_ab], "scatter_g3", 5)
    da, dsgu_w, dsgu_b, dsgu_ng = sgu_bwd(proj0, dcat, sgu_norm_g, sgu_w[0], sgu_b3)
    dq, dk, dv, dbz = sb_bwd(proj0, dcat, att, tot, nh)
    dproj0 = jnp.concatenate([da, dq, dk, dv, dbz], axis=1)
    gw_in_ab = mm_tn(h0, dproj0, N_DEV, bf16, "gw_in_ab")
    (p_in_ab,) = sequencer_exchange(SCATTER, [gw_in_ab], "scatter_g4", 6)
    dh0 = mm_nt(dproj0, win_ab3, f32, "dh0")
    dx0, dshift0, dscale0, dgpre0 = prenorm_bwd(dh0, x0, dx1, rows(ln_pre_g, 0), scale0, "prenorm0_bwd")

    small_names = ["ln_pre_g", "ln_post_g", "b_mod", "sgu_norm_g", "sgu_w", "sgu_b", "lam_re", "lam_im", "b_re", "b_im",
                   "c_re", "c_im", "log_dt"]
    small_w = [ln_pre_g, ln_post_g, b_mod, sgu_norm_g, sgu_w, sgu_b, lam_re, lam_im, b_re, b_im, c_re, c_im, log_dt]
    small_m = [m_ln_pre_g, m_ln_post_g, m_b_mod, m_sgu_norm_g, m_sgu_w, m_sgu_b, m_lam_re, m_lam_im, m_b_re, m_b_im,
               m_c_re, m_c_im, m_log_dt]
    small_v = [v_ln_pre_g, v_ln_post_g, v_b_mod, v_sgu_norm_g, v_sgu_w, v_sgu_b, v_lam_re, v_lam_im, v_b_re, v_b_im,
               v_c_re, v_c_im, v_log_dt]
    dmod = jnp.concatenate([dshift0, dscale0, dgate0, dshift1, dscale1, dgate1], axis=1)
    small_g = [jnp.concatenate([dgpre0, dgpre1]), jnp.concatenate([dgpost0, dgpost1]), dmod, dsgu_ng, dsgu_w, dsgu_b,
               dlr, dli, g_b_re, g_b_im, g_c_re, g_c_im, dldt]
    shapes = [w.shape for w in small_w]
    g_sum, dmod_all = all_reduce_rows(_pack(small_g + [dd_skip, db_glu, loss_tile], SUBLANES * N_DEV), dmod,
                                      "reduce_small_grads")
    n_rows_small = sum(-(-math.prod(s) // PACK_ROW) * SUBLANES for s in shapes)
    loss = g_sum[n_rows_small + 2 * (d_skip_all.shape[1] // HEAD), 0] * (0.5 / d)
    new_small = adam_small(g_sum, _pack(small_w), _pack(small_m), _pack(small_v))
    r_small = [_unpack(o, shapes) for o in [g_sum[:n_rows_small]] + list(new_small)]
    small = {n: [r_small[k][i] for k in range(4)] for i, n in enumerate(small_names)}
    vec_rows = d_skip_all.shape[1] // HEAD

    def my_columns(r0):
        whole = g_sum[r0:r0 + vec_rows].reshape(1, 1, -1)
        return lax.dynamic_slice_in_dim(whole, me * d_skip.shape[1], d_skip.shape[1], axis=2)

    def sharded(p, w, m, v, name):
        shp = w.shape
        w2, m2, v2 = (a.reshape(-1, shp[-1]) for a in (w, m, v))
        return [o.reshape(shp) for o in adam_reduce(p.reshape(p.shape[0], -1, shp[-1]), w2, m2, v2, name)]

    r_d_skip = sharded(my_columns(n_rows_small), d_skip, m_d_skip, v_d_skip, "adam_d_skip")
    r_b_glu = sharded(my_columns(n_rows_small + vec_rows), b_glu, m_b_glu, v_b_glu, "adam_b_glu")
    r_w_out_ssm = sharded(p_out_ssm, w_out_ssm, m_w_out_ssm, v_w_out_ssm, "adam_w_out_ssm")
    r_w_in_ssm = sharded(p_in_ssm, w_in_ssm, m_w_in_ssm, v_w_in_ssm, "adam_w_in_ssm")
    r_w_glu = sharded(p_glu, w_glu, m_w_glu, v_w_glu, "adam_w_glu")
    r_w_out_ab = sharded(p_out_ab, w_out_ab, m_w_out_ab, v_w_out_ab, "adam_w_out_ab")
    r_w_in_ab = sharded(p_in_ab, w_in_ab, m_w_in_ab, v_w_in_ab, "adam_w_in_ab")

    dm_cols = jnp.transpose(
        lax.dynamic_slice_in_dim(dmod_all.reshape(N_DEV, 2, 3 * d), me * mod_cols, mod_cols, axis=2), (1, 0, 2))
    cond_t = jnp.transpose(silu_rows(c_all))
    r_w_mod = adam_w_mod(cond_t, dm_cols, w_mod, m_w_mod, v_w_mod)

    res = dict(small)
    res.update(w_mod=r_w_mod, w_in_ab=r_w_in_ab, w_out_ab=r_w_out_ab, w_in_ssm=r_w_in_ssm, w_out_ssm=r_w_out_ssm,
               d_skip=r_d_skip, w_glu=r_w_glu, b_glu=r_b_glu)
    order = ["ln_pre_g", "ln_post_g", "w_mod", "b_mod", "w_in_ab", "w_out_ab", "sgu_norm_g", "sgu_w", "sgu_b", "w_in_ssm",
             "w_out_ssm", "lam_re", "lam_im", "b_re", "b_im", "c_re", "c_im", "d_skip", "log_dt", "w_glu", "b_glu"]
    outs = [loss, dx0.reshape(x.shape)]
    for k in range(4):
        outs += [res[n][k] for n in order]
    return tuple(outs)
```

```python
import functools
import math

import jax
import jax.numpy as jnp
from jax import lax
from jax.experimental import pallas as pl
from jax.experimental.pallas import tpu as pltpu
from jax.experimental.pallas import tpu_sc as plsc

f32 = jnp.float32
bf16 = jnp.bfloat16

N_DEV = 8
EPS = 1e-6
HEAD = 128
SUBLANES = 8
SSM_GROUP = 16
SSM_STATE = 64
GROUPS_PER_LANE_BLOCK = HEAD // SSM_GROUP
STATES_PER_LANE_BLOCK = GROUPS_PER_LANE_BLOCK * SSM_STATE
VMEM_LIMIT = 56 * 2 ** 20
ADAM_LR, ADAM_B1, ADAM_B2, ADAM_EPS, ADAM_WD, ADAM_STEP = 0.001, 0.9, 0.999, 1e-08, 0.01, 10
_GELU_C0 = math.sqrt(2.0 / math.pi)
_GELU_C1 = 0.044715
MESH = pl.DeviceIdType.MESH


def _cparams(*sem):
    return pltpu.CompilerParams(dimension_semantics=sem if sem else None, vmem_limit_bytes=VMEM_LIMIT)


def _gelu(x):
    return 0.5 * x * (1.0 + jnp.tanh(_GELU_C0 * (x + _GELU_C1 * x * x * x)))


def _gelu_grad(x):
    t = jnp.tanh(_GELU_C0 * (x + _GELU_C1 * x * x * x))
    return 0.5 * (1.0 + t) + 0.5 * x * (1.0 - t * t) * _GELU_C0 * (1.0 + 3.0 * _GELU_C1 * x * x)


def _silu(x):
    return x * jax.nn.sigmoid(x)


def _silu_grad(x):
    s = jax.nn.sigmoid(x)
    return s * (1.0 + x * (1.0 - s))


def _dot(a, b):
    return jnp.dot(a, b, preferred_element_type=f32)


def _dot_nt(a, b):
    return lax.dot_general(a, b, (((1,), (1,)), ((), ())), preferred_element_type=f32)


def _dot_tn(a, b):
    return lax.dot_general(a, b, (((0,), (0,)), ((), ())), preferred_element_type=f32)


def _split_bf16(v):
    hi = v.astype(bf16)
    lo = (v - hi.astype(f32)).astype(bf16)
    return hi, lo


def _row(d):
    return pl.BlockSpec((1, d), lambda *_: (0, 0))


def _my_index():
    return 4 * lax.axis_index("x") + 2 * lax.axis_index("y") + lax.axis_index("c")


def _peer(k):
    x, y, c = lax.axis_index("x"), lax.axis_index("y"), lax.axis_index("c")
    return (1 - x if k & 4 else x, 1 - y if k & 2 else y, 1 - c if k & 1 else c)


def all_gather(arrs, name):
    n = len(arrs)

    def body(*refs):
        ins, outs = refs[:n], refs[n:2 * n]
        send, recv, local = refs[2 * n:]
        me = _my_index()
        copies = []
        for a in range(n):
            cp = pltpu.make_async_copy(ins[a], outs[a].at[me], local.at[a])
            cp.start()
            copies.append(cp)
            for k in range(1, N_DEV):
                s = a * (N_DEV - 1) + k - 1
                cp = pltpu.make_async_remote_copy(src_ref=ins[a], dst_ref=outs[a].at[me], send_sem=send.at[s],
                                                  recv_sem=recv.at[s], device_id=_peer(k), device_id_type=MESH)
                cp.start()
                copies.append(cp)
        for cp in copies:
            cp.wait()

    any_spec = pl.BlockSpec(memory_space=pl.ANY)
    outs = pl.pallas_call(
        body, name=name,
        out_shape=[jax.ShapeDtypeStruct((N_DEV,) + a.shape, a.dtype) for a in arrs],
        in_specs=[any_spec] * n, out_specs=[any_spec] * n,
        scratch_shapes=[pltpu.SemaphoreType.DMA((n * (N_DEV - 1),)), pltpu.SemaphoreType.DMA((n * (N_DEV - 1),)),
                        pltpu.SemaphoreType.DMA((n,))],
        compiler_params=pltpu.CompilerParams(has_side_effects=True),
    )(*arrs)
    return list(outs)


def all_reduce_rows(pack, extra, name):
    r, c = pack.shape
    rs = r // N_DEV
    n_peer = N_DEV - 1

    def body(p_ref, x_ref, o_ref, xo_ref, land, red, send1, recv1, send2, recv2, sendx, recvx, local):
        me = _my_index()

        def rows(i):
            return pl.ds(pl.multiple_of(i * rs, SUBLANES), rs)

        own = [pltpu.make_async_copy(p_ref.at[rows(me)], land.at[me], local.at[0]),
               pltpu.make_async_copy(x_ref, xo_ref.at[me], local.at[1])]
        first = []
        for k in range(1, N_DEV):
            first.append(pltpu.make_async_remote_copy(
                src_ref=p_ref.at[rows(jnp.bitwise_xor(me, k))], dst_ref=land.at[me], send_sem=send1.at[k - 1],
                recv_sem=recv1.at[k - 1], device_id=_peer(k), device_id_type=MESH))
            first.append(pltpu.make_async_remote_copy(
                src_ref=x_ref, dst_ref=xo_ref.at[me], send_sem=sendx.at[k - 1], recv_sem=recvx.at[k - 1],
                device_id=_peer(k), device_id_type=MESH))
        for cp in own + first:
            cp.start()
        for cp in own + first:
            cp.wait()
        acc = land[0]
        for s in range(1, N_DEV):
            acc = acc + land[s]
        red[...] = acc
        mine = pltpu.make_async_copy(red, o_ref.at[rows(me)], local.at[2])
        second = [pltpu.make_async_remote_copy(
            src_ref=red, dst_ref=o_ref.at[rows(me)], send_sem=send2.at[k - 1], recv_sem=recv2.at[k - 1],
            device_id=_peer(k), device_id_type=MESH) for k in range(1, N_DEV)]
        for cp in [mine] + second:
            cp.start()
        for cp in [mine] + second:
            cp.wait()

    any_spec = pl.BlockSpec(memory_space=pl.ANY)
    return pl.pallas_call(
        body, name=name,
        out_shape=[jax.ShapeDtypeStruct((r, c), pack.dtype), jax.ShapeDtypeStruct((N_DEV,) + extra.shape, extra.dtype)],
        in_specs=[any_spec, any_spec], out_specs=[any_spec, any_spec],
        scratch_shapes=[pltpu.VMEM((N_DEV, rs, c), pack.dtype), pltpu.VMEM((rs, c), pack.dtype)]
        + [pltpu.SemaphoreType.DMA((n_peer,))] * 6 + [pltpu.SemaphoreType.DMA((3,))],
        compiler_params=pltpu.CompilerParams(has_side_effects=True),
    )(pack, extra)


GATHER, SCATTER = "gather", "scatter"


def _exchange_copies(srcs, lands, send, recv):
    me = _my_index()
    copies = []
    for a, (src, land) in enumerate(zip(srcs, lands)):
        for k in range(1, N_DEV):
            s = a * (N_DEV - 1) + k - 1
            copies.append(pltpu.make_async_remote_copy(
                src_ref=src.at[jnp.bitwise_xor(me, k)], dst_ref=land.at[me],
                send_sem=send.at[s], recv_sem=recv.at[s], device_id=_peer(k), device_id_type=MESH))
    return copies


def sequencer_exchange(kind, arrs, name, collective_id):
    n = len(arrs)
    n_sem = n * (N_DEV - 1)
    land_shapes = [((N_DEV,) + a.shape if kind == GATHER else a.shape) for a in arrs]
    srcs = [jax.new_ref(a, memory_space=pltpu.MemorySpace.HBM) for a in arrs]
    lands = [jax.empty_ref(jax.ShapeDtypeStruct(s, a.dtype), memory_space=pltpu.MemorySpace.HBM)
             for s, a in zip(land_shapes, arrs)]

    @pl.kernel(mesh=plsc.ScalarSubcoreMesh(axis_name="sequencer", num_cores=1), name=name,
               scratch_types=(pltpu.SemaphoreType.DMA((n_sem,)), pltpu.SemaphoreType.DMA((n_sem,)),
                              pltpu.SemaphoreType.DMA((n,))),
               compiler_params=pltpu.CompilerParams(collective_id=collective_id))
    def launch(send, recv, local):
        barrier = pltpu.get_barrier_semaphore()
        for k in range(1, N_DEV):
            pl.semaphore_signal(barrier, inc=1, device_id=_peer(k), device_id_type=MESH)
        pl.semaphore_wait(barrier, N_DEV - 1)
        me = _my_index()
        mine = [pltpu.make_async_copy(src if kind == GATHER else src.at[me], land.at[me], local.at[a])
                for a, (src, land) in enumerate(zip(srcs, lands))]
        if kind == SCATTER:
            copies = mine + _exchange_copies(srcs, lands, send, recv)
            for cp in copies:
                cp.start()
            for cp in copies:
                cp.wait()
            return

        def block_copy(a, slot, block, k, src=None):
            s = a * (N_DEV - 1) + slot
            return pltpu.make_async_remote_copy(
                src_ref=lands[a].at[block] if src is None else src, dst_ref=lands[a].at[block],
                send_sem=send.at[s], recv_sem=recv.at[s], device_id=_peer(k), device_id_type=MESH)

        chips = (2, 4, 6)
        sibling = jnp.bitwise_xor(me, 1)
        first = [block_copy(a, slot, me, k, src=srcs[a]) for a in range(n) for slot, k in enumerate((1,) + chips)]
        for cp in mine + first:
            cp.start()
        passed = []
        for a in range(n):
            for i, k in enumerate(chips):
                block = jnp.bitwise_xor(me, k)
                block_copy(a, 1 + i, block, k).wait_recv()
                passed.append(block_copy(a, 4 + i, block, 1))
                passed[-1].start()
        for a in range(n):
            block_copy(a, 0, sibling, 1).wait_recv()
            for i, k in enumerate(chips):
                block_copy(a, 4 + i, jnp.bitwise_xor(sibling, k), 1).wait_recv()
        for cp in mine:
            cp.wait()
        for cp in first + passed:
            cp.wait_send()

    launch()
    return [land[...] for land in lands]


def _tile(n, pref):
    for t in pref:
        if n % t == 0:
            return t
    return n


MM_WIDE = 1024
MM_WEIGHT_BLOCK = 8 * 2 ** 20


def _blocks_per_step(nb, fits):
    return max(g for g in range(1, nb + 1) if nb % g == 0 and fits(g))


def mm_nn(a, b3, out_dtype, name):
    m, k = a.shape
    nb, _, bn = b3.shape
    tm = _tile(m, (512, 256, 128))
    tn = _tile(bn, (1024, 896, 512, 256, 128))
    per = bn // tn
    gb = _blocks_per_step(nb, lambda g: g == 1 or (per == 1 and g * bn <= MM_WIDE))

    def body(a_ref, b_ref, o_ref):
        for g in range(gb):
            o_ref[:, g * tn:(g + 1) * tn] = _dot(a_ref[...], b_ref[g]).astype(o_ref.dtype)

    return pl.pallas_call(
        body, name=name, grid=(m // tm, nb // gb, per),
        in_specs=[pl.BlockSpec((tm, k), lambda i, j, jj: (i, 0)),
                  pl.BlockSpec((gb, k, tn), lambda i, j, jj: (j, 0, jj))],
        out_specs=pl.BlockSpec((tm, gb * tn), lambda i, j, jj: (i, j * per + jj)),
        out_shape=jax.ShapeDtypeStruct((m, nb * bn), out_dtype),
        compiler_params=_cparams("parallel", "arbitrary", "arbitrary"),
    )(a, b3)


def mm_nt(a, w3, out_dtype, name):
    m, _ = a.shape
    nb, ko, bn = w3.shape
    tm = _tile(m, (512, 256, 128))
    tko = _tile(ko, (1024, 512, 256, 128))
    gb = _blocks_per_step(nb, lambda g: g * tko * bn * w3.dtype.itemsize <= MM_WEIGHT_BLOCK)
    ns = nb // gb

    def body(a_ref, w_ref, o_ref, acc_ref):
        j = pl.program_id(2)

        @pl.when(j == 0)
        def _():
            acc_ref[...] = jnp.zeros_like(acc_ref)

        part = _dot_nt(a_ref[:, :bn], w_ref[0])
        for g in range(1, gb):
            part += _dot_nt(a_ref[:, g * bn:(g + 1) * bn], w_ref[g])
        acc_ref[...] += part

        @pl.when(j == ns - 1)
        def _():
            o_ref[...] = acc_ref[...].astype(o_ref.dtype)

    return pl.pallas_call(
        body, name=name, grid=(m // tm, ko // tko, ns),
        in_specs=[pl.BlockSpec((tm, gb * bn), lambda i, o, j: (i, j)),
                  pl.BlockSpec((gb, tko, bn), lambda i, o, j: (j, o, 0))],
        out_specs=pl.BlockSpec((tm, tko), lambda i, o, j: (i, o)),
        out_shape=jax.ShapeDtypeStruct((m, ko), out_dtype),
        scratch_shapes=[pltpu.VMEM((tm, tko), f32)],
        compiler_params=_cparams("parallel", "arbitrary", "arbitrary"),
    )(a, w3)


def mm_tn(a, dy, ncb, out_dtype, name):
    l, ka = a.shape
    _, n = dy.shape
    bn = n // ncb
    tl = _tile(l, (1024, 512, 256, 128))
    tka = _tile(ka, (512, 256, 128))
    tn = _tile(bn, (1024, 896, 512, 256, 128))
    per = bn // tn
    gb = _blocks_per_step(ncb, lambda g: g == 1 or (per == 1 and g * bn <= MM_WIDE))
    nl = l // tl

    def body(a_ref, dy_ref, o_ref, acc_ref):
        s = pl.program_id(2)

        @pl.when(s == 0)
        def _():
            acc_ref[...] = jnp.zeros_like(acc_ref)

        acc_ref[...] += _dot_tn(a_ref[...], dy_ref[...])

        @pl.when(s == nl - 1)
        def _():
            for g in range(gb):
                o_ref[g] = acc_ref[:, g * tn:(g + 1) * tn].astype(o_ref.dtype)

    return pl.pallas_call(
        body, name=name, grid=(ka // tka, n // (gb * tn), nl),
        in_specs=[pl.BlockSpec((tl, tka), lambda i, j, s: (s, i)),
                  pl.BlockSpec((tl, gb * tn), lambda i, j, s: (s, j))],
        out_specs=pl.BlockSpec((gb, tka, tn), lambda i, j, s: (j // per, i, j % per)),
        out_shape=jax.ShapeDtypeStruct((ncb, ka, bn), out_dtype),
        scratch_shapes=[pltpu.VMEM((tka, gb * tn), f32)],
        compiler_params=_cparams("parallel", "parallel", "arbitrary"),
    )(a, dy)


def mod_part(c_all, w_mod, b_cols):
    nl, d, cols = w_mod.shape

    def body(c_ref, w_ref, b_ref, o_ref):
        cond = _silu(c_ref[...]).astype(bf16)
        o_ref[...] = _dot(cond, w_ref[...].astype(bf16)) + b_ref[...]

    return pl.pallas_call(
        body, name="mod_part", grid=(nl,),
        in_specs=[pl.BlockSpec((N_DEV, d), lambda l: (0, 0)),
                  pl.BlockSpec((None, d, cols), lambda l: (l, 0, 0)),
                  pl.BlockSpec((None, 1, cols), lambda l: (l, 0, 0))],
        out_specs=pl.BlockSpec((None, N_DEV, cols), lambda l: (l, 0, 0)),
        out_shape=jax.ShapeDtypeStruct((nl, N_DEV, cols), f32),
        compiler_params=_cparams("arbitrary"),
    )(c_all, w_mod, b_cols.reshape(nl, 1, cols))


def _row_tile(l):
    return _tile(l, (256, 128))


def prenorm_fwd(x, g, shift, scale, name):
    l, d = x.shape
    tm = _row_tile(l)

    def body(x_ref, g_ref, sh_ref, sc_ref, h_ref):
        xv = x_ref[...]
        r = lax.rsqrt(jnp.mean(xv * xv, axis=-1, keepdims=True) + EPS)
        h_ref[...] = (xv * r * (g_ref[...] * (1.0 + sc_ref[...])) + sh_ref[...]).astype(h_ref.dtype)

    return pl.pallas_call(
        body, name=name, grid=(l // tm,),
        in_specs=[pl.BlockSpec((tm, d), lambda i: (i, 0)), _row(d), _row(d), _row(d)],
        out_specs=pl.BlockSpec((tm, d), lambda i: (i, 0)),
        out_shape=jax.ShapeDtypeStruct((l, d), bf16),
        compiler_params=_cparams("parallel"),
    )(x, g, shift, scale)


def post_prenorm_fwd(x, y, gate, g_post, g_pre, shift, scale, name):
    l, d = x.shape
    tm = _row_tile(l)

    def body(x_ref, y_ref, gate_ref, gp_ref, g_ref, sh_ref, sc_ref, o_ref, h_ref):
        yv = y_ref[...]
        r = lax.rsqrt(jnp.mean(yv * yv, axis=-1, keepdims=True) + EPS)
        xv = x_ref[...] + gate_ref[...] * (yv * r * gp_ref[...])
        o_ref[...] = xv
        r = lax.rsqrt(jnp.mean(xv * xv, axis=-1, keepdims=True) + EPS)
        h_ref[...] = (xv * r * (g_ref[...] * (1.0 + sc_ref[...])) + sh_ref[...]).astype(h_ref.dtype)

    blk = pl.BlockSpec((tm, d), lambda i: (i, 0))
    return pl.pallas_call(
        body, name=name, grid=(l // tm,),
        in_specs=[blk, blk] + [_row(d)] * 5, out_specs=[blk, blk],
        out_shape=[jax.ShapeDtypeStruct((l, d), f32), jax.ShapeDtypeStruct((l, d), bf16)],
        compiler_params=_cparams("parallel"),
    )(x, y, gate, g_post, g_pre, shift, scale)


def _post_bwd_rows(dxv, yv, r, gate, gv, dy_ref, dgate_ref, dg_ref):
    yn = yv * r
    dgate_ref[...] += jnp.sum(dxv * yn * gv, axis=0, keepdims=True)
    dyg = dxv * gate
    dg_ref[...] += jnp.sum(dyg * yn, axis=0, keepdims=True)
    dyn = dyg * gv
    dy_ref[...] = (r * (dyn - yn * jnp.mean(dyn * yn, axis=-1, keepdims=True))).astype(dy_ref.dtype)


def final_loss(x, y, gate, g, target):
    l, d = x.shape
    tm = _row_tile(l)

    def body(x_ref, y_ref, gate_ref, g_ref, t_ref, dx_ref, loss_ref, dy_ref, dgate_ref, dg_ref):
        @pl.when(pl.program_id(0) == 0)
        def _():
            loss_ref[...] = jnp.zeros_like(loss_ref)
            dgate_ref[...] = jnp.zeros_like(dgate_ref)
            dg_ref[...] = jnp.zeros_like(dg_ref)

        yv, gate, gv = y_ref[...], gate_ref[...], g_ref[...]
        r = lax.rsqrt(jnp.mean(yv * yv, axis=-1, keepdims=True) + EPS)
        diff = x_ref[...] + gate * (yv * r * gv) - t_ref[...]
        dxv = diff * (1.0 / d)
        dx_ref[...] = dxv
        loss_ref[...] += jnp.sum(diff * diff)
        _post_bwd_rows(dxv, yv, r, gate, gv, dy_ref, dgate_ref, dg_ref)

    blk = pl.BlockSpec((tm, d), lambda i: (i, 0))
    return pl.pallas_call(
        body, name="final_loss", grid=(l // tm,),
        in_specs=[blk, blk, _row(d), _row(d), blk],
        out_specs=[blk, pl.BlockSpec((SUBLANES, HEAD), lambda i: (0, 0)), blk, _row(d), _row(d)],
        out_shape=[jax.ShapeDtypeStruct((l, d), f32), jax.ShapeDtypeStruct((SUBLANES, HEAD), f32),
                   jax.ShapeDtypeStruct((l, d), bf16), jax.ShapeDtypeStruct((1, d), f32), jax.ShapeDtypeStruct((1, d), f32)],
        compiler_params=_cparams("arbitrary"),
    )(x, y, gate, g, target)


def post_bwd(dx, y, gate, g, name):
    l, d = dx.shape
    tm = _row_tile(l)

    def body(dx_ref, y_ref, gate_ref, g_ref, dy_ref, dgate_ref, dg_ref):
        @pl.when(pl.program_id(0) == 0)
        def _():
            dgate_ref[...] = jnp.zeros_like(dgate_ref)
            dg_ref[...] = jnp.zeros_like(dg_ref)

        yv = y_ref[...]
        r = lax.rsqrt(jnp.mean(yv * yv, axis=-1, keepdims=True) + EPS)
        _post_bwd_rows(dx_ref[...], yv, r, gate_ref[...], g_ref[...], dy_ref, dgate_ref, dg_ref)

    blk = pl.BlockSpec((tm, d), lambda i: (i, 0))
    return pl.pallas_call(
        body, name=name, grid=(l // tm,),
        in_specs=[blk, blk, _row(d), _row(d)], out_specs=[blk, _row(d), _row(d)],
        out_shape=[jax.ShapeDtypeStruct((l, d), bf16), jax.ShapeDtypeStruct((1, d), f32),
                   jax.ShapeDtypeStruct((1, d), f32)],
        compiler_params=_cparams("arbitrary"),
    )(dx, y, gate, g)


def prenorm_bwd(dh, x, dx_next, g, scale, name):
    l, d = x.shape
    tm = _row_tile(l)

    def body(dh_ref, x_ref, dxn_ref, g_ref, sc_ref, dx_ref, dsh_ref, dsc_ref, dg_ref):
        @pl.when(pl.program_id(0) == 0)
        def _():
            dsh_ref[...] = jnp.zeros_like(dsh_ref)
            dsc_ref[...] = jnp.zeros_like(dsc_ref)
            dg_ref[...] = jnp.zeros_like(dg_ref)

        xv, dhv, gv, sc1 = x_ref[...], dh_ref[...], g_ref[...], 1.0 + sc_ref[...]
        r = lax.rsqrt(jnp.mean(xv * xv, axis=-1, keepdims=True) + EPS)
        xn = xv * r
        dhx = dhv * xn
        dsh_ref[...] += jnp.sum(dhv, axis=0, keepdims=True)
        dsc_ref[...] += jnp.sum(dhx * gv, axis=0, keepdims=True)
        dg_ref[...] += jnp.sum(dhx * sc1, axis=0, keepdims=True)
        dxn = dhv * (gv * sc1)
        dx_ref[...] = dxn_ref[...] + r * (dxn - xn * jnp.mean(dxn * xn, axis=-1, keepdims=True))

    blk = pl.BlockSpec((tm, d), lambda i: (i, 0))
    return pl.pallas_call(
        body, name=name, grid=(l // tm,),
        in_specs=[blk, blk, blk, _row(d), _row(d)], out_specs=[blk, _row(d), _row(d), _row(d)],
        out_shape=[jax.ShapeDtypeStruct((l, d), f32)] + [jax.ShapeDtypeStruct((1, d), f32)] * 3,
        compiler_params=_cparams("arbitrary"),
    )(dh, x, dx_next, g, scale)


def _tril_mask():
    r = lax.broadcasted_iota(jnp.int32, (HEAD, HEAD), 0)
    c = lax.broadcasted_iota(jnp.int32, (HEAD, HEAD), 1)
    return r >= c


def sgu_fwd(proj, norm_g, w_s, b_s):
    l = proj.shape[0]
    nh = w_s.shape[0]
    wa = nh * HEAD

    def body(au_ref, av_ref, az_ref, ng_ref, w_ref, b_ref, o_ref):
        tril = _tril_mask()
        for h in range(nh):
            sl = slice(h * HEAD, (h + 1) * HEAD)
            gv = _gelu(av_ref[:, sl].astype(f32))
            r = lax.rsqrt(jnp.mean(gv * gv, axis=-1, keepdims=True) + EPS)
            vh = gv * r * ng_ref[:, sl]
            wm = jnp.where(tril, w_ref[h], 0.0).astype(bf16)
            s = _dot(wm, vh.astype(bf16)) + b_ref[h]
            o_ref[:, sl] = (_gelu(au_ref[:, sl].astype(f32)) * s * _silu(az_ref[:, sl].astype(f32))).astype(o_ref.dtype)

    def col(j):
        return pl.BlockSpec((HEAD, wa), lambda n: (n, j))

    return pl.pallas_call(
        body, name="sgu_fwd", grid=(l // HEAD,),
        in_specs=[col(0), col(1), col(2), _row(wa),
                  pl.BlockSpec((nh, HEAD, HEAD), lambda n: (0, 0, 0)), pl.BlockSpec((nh, HEAD, 1), lambda n: (0, 0, 0))],
        out_specs=pl.BlockSpec((HEAD, wa), lambda n: (n, 0)),
        out_shape=jax.ShapeDtypeStruct((l, wa), bf16),
        compiler_params=_cparams("parallel"),
    )(proj, proj, proj, norm_g, w_s, b_s)


def sgu_bwd(proj, dcat, norm_g, w_s, b_s):
    l = proj.shape[0]
    nh = w_s.shape[0]
    wa = nh * HEAD

    def body(au_ref, av_ref, az_ref, do_ref, ng_ref, w_ref, b_ref, da_ref, dw_ref, db_ref, dng_ref):
        @pl.when(pl.program_id(0) == 0)
        def _():
            dw_ref[...] = jnp.zeros_like(dw_ref)
            db_ref[...] = jnp.zeros_like(db_ref)
            dng_ref[...] = jnp.zeros_like(dng_ref)

        tril = _tril_mask()
        for h in range(nh):
            sl = slice(h * HEAD, (h + 1) * HEAD)
            au, av, az = au_ref[:, sl].astype(f32), av_ref[:, sl].astype(f32), az_ref[:, sl].astype(f32)
            ng = ng_ref[:, sl]
            gv = _gelu(av)
            r = lax.rsqrt(jnp.mean(gv * gv, axis=-1, keepdims=True) + EPS)
            gvn = gv * r
            vh = (gvn * ng).astype(bf16)
            wm = jnp.where(tril, w_ref[h], 0.0).astype(bf16)
            s = _dot(wm, vh) + b_ref[h]
            gu, sz = _gelu(au), _silu(az)
            dov = do_ref[:, sl].astype(f32)
            da_ref[:, sl] = (dov * s * sz * _gelu_grad(au)).astype(da_ref.dtype)
            da_ref[:, 2 * wa + h * HEAD:2 * wa + (h + 1) * HEAD] = (dov * gu * s * _silu_grad(az)).astype(da_ref.dtype)
            ds = dov * gu * sz
            db_ref[h] += jnp.sum(ds, axis=-1, keepdims=True)
            dsb = ds.astype(bf16)
            dw_ref[h] += jnp.where(tril, _dot_nt(dsb, vh), 0.0)
            dvh = _dot_tn(wm, dsb)
            dng_ref[:, sl] += jnp.sum(dvh * gvn, axis=0, keepdims=True)
            dgvn = dvh * ng
            dgv = r * (dgvn - gvn * jnp.mean(dgvn * gvn, axis=-1, keepdims=True))
            da_ref[:, wa + h * HEAD:wa + (h + 1) * HEAD] = (dgv * _gelu_grad(av)).astype(da_ref.dtype)

    def col(j):
        return pl.BlockSpec((HEAD, wa), lambda n: (n, j))

    whole_w = pl.BlockSpec((nh, HEAD, HEAD), lambda n: (0, 0, 0))
    whole_b = pl.BlockSpec((nh, HEAD, 1), lambda n: (0, 0, 0))
    return pl.pallas_call(
        body, name="sgu_bwd", grid=(l // HEAD,),
        in_specs=[col(0), col(1), col(2), col(0), _row(wa), whole_w, whole_b],
        out_specs=[pl.BlockSpec((HEAD, 3 * wa), lambda n: (n, 0)), whole_w, whole_b, _row(wa)],
        out_shape=[jax.ShapeDtypeStruct((l, 3 * wa), bf16), jax.ShapeDtypeStruct((nh, HEAD, HEAD), f32),
                   jax.ShapeDtypeStruct((nh, HEAD, 1), f32), jax.ShapeDtypeStruct((1, wa), f32)],
        compiler_params=_cparams("arbitrary"),
    )(proj, proj, proj, dcat, norm_g, w_s, b_s)


_LOG2E = 1.0 / math.log(2.0)


def _sb_scores(q, k, scale):
    z = _dot_nt(q, k) * (scale * _LOG2E)
    return z, jnp.maximum(z, 0.0) + jnp.log2(1.0 + jnp.exp2(-jnp.abs(z)))


def _sb_sum_matrix(tri):
    s = lax.broadcasted_iota(jnp.int32, (2 * HEAD, 2 * HEAD), 0) % HEAD
    j = lax.broadcasted_iota(jnp.int32, (2 * HEAD, 2 * HEAD), 1)
    return jnp.where(jnp.logical_or(j >= HEAD, tri(s, j)), 1.0, 0.0).astype(bf16)


def _sb_sums(x, sums):
    c2 = _dot(jnp.concatenate(_split_bf16(x), axis=1), sums)
    return c2[:, :HEAD], c2[:, HEAD:]


def _sb_q_tile(l, most=512):
    return _tile(l, tuple(t for t in (1024, 512, 256, 128) if t <= most))


def _sb_band_levels(band):
    return _tile(band, (4, 2, 1))


def _sb_heads_per_step(nh, most):
    return _tile(nh, tuple(h for h in (4, 2) if h <= most))


def sb_fwd(proj, nh):
    l = proj.shape[0]
    wb = nh * HEAD
    tq = _sb_q_tile(l, 1024)
    band = tq // HEAD
    hp = _sb_heads_per_step(nh, 2)
    levels = _sb_band_levels(band)
    scale = 1.0 / math.sqrt(HEAD)
    qc, kc, vc, zc = 3 * nh, 4 * nh, 5 * nh, 6 * nh

    def body(q_ref, k_ref, v_ref, bz_ref, o_ref, att_ref, tot_ref):
        i = pl.program_id(1)
        sums = _sb_sum_matrix(lambda s, j: s > j)
        t_pos = i * tq + lax.broadcasted_iota(jnp.int32, (tq, HEAD), 0)
        s_off = lax.broadcasted_iota(jnp.int32, (tq, HEAD), 1)

        def step(j, carry, masked, row0=0):
            rows = pl.ds(pl.multiple_of(j * HEAD, HEAD), HEAD)
            out = []
            for e in range(hp):
                acc, tot = carry[e]
                sl = slice(e * HEAD, (e + 1) * HEAD)
                z, sp = _sb_scores(q_ref[row0:, sl], k_ref[rows, sl], scale)
                lb = z - sp
                if masked:
                    mask = s_off[row0:] + j * HEAD < t_pos[row0:]
                    sp = jnp.where(mask, sp, 0.0)
                later, total = _sb_sums(sp, sums)
                w = jnp.exp2(lb + tot[row0:] - later)
                if masked:
                    w = jnp.where(mask, w, 0.0)
                new = (acc[row0:] + _dot(w.astype(bf16), v_ref[rows, sl]), tot[row0:] - total)
                out.append(tuple(jnp.concatenate([old[:row0], upd]) if row0 else upd for old, upd in zip(carry[e], new)))
            return tuple(out)

        zero = jnp.zeros((tq, HEAD), f32)
        carry = ((zero, zero),) * hp
        for lv in reversed(range(levels)):
            carry = lax.fori_loop(
                0, band // levels,
                lambda t, c, lv=lv: step(band * i + (lv + 1) * (band // levels) - 1 - t, c, True, lv * (tq // levels)), carry)
        carry = lax.fori_loop(0, band * i, lambda t, c: step(band * i - 1 - t, c, False), carry)
        for e in range(hp):
            acc, tot = carry[e]
            sl = slice(e * HEAD, (e + 1) * HEAD)
            att_ref[:, sl] = acc.astype(att_ref.dtype)
            o_ref[:, sl] = (acc * _silu(bz_ref[:, sl].astype(f32))).astype(o_ref.dtype)
            tot_ref[e] = tot[:, :1]

    blk = lambda c0: pl.BlockSpec((tq, hp * HEAD), lambda g, i: (i, c0 // hp + g))
    head = lambda c0: pl.BlockSpec((l, hp * HEAD), lambda g, i: (0, c0 // hp + g))
    return pl.pallas_call(
        body, name="sb_fwd", grid=(nh // hp, l // tq),
        in_specs=[blk(qc), head(kc), head(vc), blk(zc)],
        out_specs=[blk(0), blk(0), pl.BlockSpec((hp, tq, 1), lambda g, i: (g, i, 0))],
        out_shape=[jax.ShapeDtypeStruct((l, wb), bf16), jax.ShapeDtypeStruct((l, wb), bf16),
                   jax.ShapeDtypeStruct((nh, l, 1), f32)],
        compiler_params=_cparams("parallel", "arbitrary"),
    )(proj, proj, proj, proj)


def sb_bwd(proj, dcat, att, tot, nh):
    l = proj.shape[0]
    wb = nh * HEAD
    tq = _sb_q_tile(l, 1024)
    band = tq // HEAD
    nq = l // tq
    hp = _sb_heads_per_step(nh, 2)
    levels = _sb_band_levels(band)
    scale = 1.0 / math.sqrt(HEAD)
    qc, kc, vc, zc = 3 * nh, 4 * nh, 5 * nh, 6 * nh

    def body(q_ref, k_ref, v_ref, bz_ref, do_ref, att_ref, tot_ref, dq_ref, dk_ref, dv_ref, dbz_ref, dk_acc, dv_acc,
             dob_ref):
        i = pl.program_id(1)

        @pl.when(i == 0)
        def _():
            dk_acc[...] = jnp.zeros_like(dk_acc)
            dv_acc[...] = jnp.zeros_like(dv_acc)

        bz = bz_ref[...].astype(f32)
        dov = do_ref[...].astype(f32)
        dbz_ref[...] = (dov * att_ref[...].astype(f32) * _silu_grad(bz)).astype(dbz_ref.dtype)
        dob_ref[...] = (dov * _silu(bz)).astype(bf16)
        upto = _sb_sum_matrix(lambda s, j: s <= j)
        before = _sb_sum_matrix(lambda j, s: j < s)
        t_pos = i * tq + lax.broadcasted_iota(jnp.int32, (tq, HEAD), 0)
        s_off = lax.broadcasted_iota(jnp.int32, (tq, HEAD), 1)

        def step(j, carry, masked, row0=0):
            rows = pl.ds(pl.multiple_of(j * HEAD, HEAD), HEAD)
            out = []
            for h in range(hp):
                dq, sp_seen, e_seen = (c[row0:] for c in carry[h])
                sl = slice(h * HEAD, (h + 1) * HEAD)
                q, kj, vj, dob = q_ref[row0:, sl], k_ref[rows, sl], v_ref[rows, sl], dob_ref[row0:, sl]
                z, sp = _sb_scores(q, kj, scale)
                lb = z - sp
                if masked:
                    mask = s_off[row0:] + j * HEAD < t_pos[row0:]
                    sp = jnp.where(mask, sp, 0.0)
                sp_upto, sp_total = _sb_sums(sp, upto)
                w = jnp.exp2(lb + sp_seen + sp_upto)
                if masked:
                    w = jnp.where(mask, w, 0.0)
                dv_acc[rows, sl] += _dot_tn(w.astype(bf16), dob)
                e = _dot_nt(dob, vj) * w
                e_before, e_total = _sb_sums(e, before)
                dz = (e - (e + e_seen + e_before) * jnp.exp2(lb)) * scale
                if masked:
                    dz = jnp.where(mask, dz, 0.0)
                dz = dz.astype(bf16)
                dk_acc[rows, sl] += _dot_tn(dz, q)
                new = (dq + _dot(dz, kj), sp_seen + sp_total, e_seen + e_total)
                out.append(tuple(jnp.concatenate([old[:row0], upd]) if row0 else upd for old, upd in zip(carry[h], new)))
            return tuple(out)

        zero = jnp.zeros((tq, HEAD), f32)
        init = tuple((zero, jnp.broadcast_to(tot_ref[h], (tq, HEAD)), zero) for h in range(hp))
        carry = lax.fori_loop(0, band * i, lambda j, c: step(j, c, False), init)
        for lv in range(levels):
            carry = lax.fori_loop(
                0, band // levels,
                lambda t, c, lv=lv: step(band * i + lv * (band // levels) + t, c, True, lv * (tq // levels)), carry)
        for h in range(hp):
            dq_ref[:, h * HEAD:(h + 1) * HEAD] = carry[h][0].astype(dq_ref.dtype)

        @pl.when(i == nq - 1)
        def _():
            dk_ref[...] = dk_acc[...].astype(dk_ref.dtype)
            dv_ref[...] = dv_acc[...].astype(dv_ref.dtype)

    blk = lambda c0: pl.BlockSpec((tq, hp * HEAD), lambda g, i: (i, c0 // hp + g))
    head = lambda c0: pl.BlockSpec((l, hp * HEAD), lambda g, i: (0, c0 // hp + g))
    return pl.pallas_call(
        body, name="sb_bwd", grid=(nh // hp, nq),
        in_specs=[blk(qc), head(kc), head(vc), blk(zc), blk(nh), blk(0),
                  pl.BlockSpec((hp, tq, 1), lambda g, i: (g, i, 0))],
        out_specs=[blk(0), head(0), head(0), blk(0)],
        out_shape=[jax.ShapeDtypeStruct((l, wb), bf16)] * 4,
        scratch_shapes=[pltpu.VMEM((l, hp * HEAD), f32), pltpu.VMEM((l, hp * HEAD), f32),
                        pltpu.VMEM((tq, hp * HEAD), bf16)],
        compiler_params=_cparams("parallel", "arbitrary"),
    )(proj, proj, proj, proj, dcat, att, tot)


def _disc(lr, li, ldt):
    dt = jnp.exp(ldt)
    mag = jnp.exp(lr * dt)
    a_re = mag * jnp.cos(li * dt)
    a_im = mag * jnp.sin(li * dt)
    den = lr * lr + li * li
    nr = a_re - 1.0
    return a_re, a_im, (nr * lr + a_im * li) / den, (a_im * lr - nr * li) / den


def s5_params_fwd(lr, li, ldt, bt_re, bt_im):
    g, c, p = bt_re.shape

    def body(lr_ref, li_ref, ldt_ref, br_ref, bi_ref, ar_ref, ai_ref, bbr_ref, bbi_ref):
        a_re, a_im, cr, ci = _disc(lr_ref[...], li_ref[...], ldt_ref[...])
        ar_ref[...] = a_re
        ai_ref[...] = a_im
        for k in range(c):
            br, bi = br_ref[:, k, :], bi_ref[:, k, :]
            bbr_ref[:, k, :] = cr * br - ci * bi
            bbi_ref[:, k, :] = cr * bi + ci * br

    return pl.pallas_call(
        body, name="s5_params_fwd",
        out_shape=[jax.ShapeDtypeStruct((g, p), f32)] * 2 + [jax.ShapeDtypeStruct((g, c, p), f32)] * 2,
    )(lr, li, ldt, bt_re, bt_im)


def s5_params_bwd(lr, li, ldt, bt_re, bt_im, da_re, da_im, dbbt_re, dbbt_im):
    g, c, p = bt_re.shape

    def body(lr_ref, li_ref, ldt_ref, br_ref, bi_ref, dar_ref, dai_ref, dbbr_ref, dbbi_ref,
             dlr_ref, dli_ref, dldt_ref, dbr_ref, dbi_ref):
        (a_re, a_im, cr, ci), vjp = jax.vjp(_disc, lr_ref[...], li_ref[...], ldt_ref[...])
        dcr = jnp.zeros((g, p), f32)
        dci = jnp.zeros((g, p), f32)
        for k in range(c):
            br, bi = br_ref[:, k, :], bi_ref[:, k, :]
            dr, di = dbbr_ref[:, k, :], dbbi_ref[:, k, :]
            dcr += dr * br + di * bi
            dci += di * br - dr * bi
            dbr_ref[:, k, :] = cr * dr + ci * di
            dbi_ref[:, k, :] = cr * di - ci * dr
        dlr, dli, dldt = vjp((dar_ref[...], dai_ref[...], dcr, dci))
        dlr_ref[...] = dlr
        dli_ref[...] = dli
        dldt_ref[...] = dldt

    return pl.pallas_call(
        body, name="s5_params_bwd",
        out_shape=[jax.ShapeDtypeStruct((g, p), f32)] * 2 + [jax.ShapeDtypeStruct((g, 1), f32)]
        + [jax.ShapeDtypeStruct((g, c, p), f32)] * 2,
    )(lr, li, ldt, bt_re, bt_im, da_re, da_im, dbbt_re, dbbt_im)


def _cmul(ar, ai, br, bi):
    return ar * br - ai * bi, ar * bi + ai * br


def _power_tables(ar, ai):
    rows = lax.broadcasted_iota(jnp.int32, (SUBLANES, ar.shape[1]), 0)
    pr = jnp.zeros((SUBLANES, ar.shape[1]), f32)
    pi = jnp.zeros((SUBLANES, ar.shape[1]), f32)
    cr, ci = ar, ai
    pows = {}
    for r in range(SUBLANES):
        pows[r + 1] = (cr, ci)
        pr = jnp.where(rows == r, cr, pr)
        pi = jnp.where(rows == r, ci, pi)
        cr, ci = _cmul(cr, ci, ar, ai)
    return [pows[1], pows[2], pows[4]], pr, pi


def _ssm_time_tile(l):
    return _tile(l, (512, 256, 128))


def ssm_fwd(u, bre3, bim3, cre3, cimn3, a_re, a_im, d_skip):
    l, w = u.shape
    nj = w // HEAD
    ns = STATES_PER_LANE_BLOCK
    tt = _ssm_time_tile(l)

    def body(u_ref, bre_ref, bim_ref, cre_ref, cim_ref, ar_ref, ai_ref, d_ref, y_ref, hr_ref, hi_ref, cr_ref, ci_ref):
        @pl.when(pl.program_id(1) == 0)
        def _():
            cr_ref[...] = jnp.zeros_like(cr_ref)
            ci_ref[...] = jnp.zeros_like(ci_ref)

        uv = u_ref[...]
        hr_ref[...] = _dot(uv, bre_ref[...])
        hi_ref[...] = _dot(uv, bim_ref[...])
        steps, pr, pi = _power_tables(ar_ref[...], ai_ref[...])
        rows = lax.broadcasted_iota(jnp.int32, (SUBLANES, ns), 0)

        def blk(b, carry):
            cr, ci = carry
            sl = pl.ds(pl.multiple_of(b * SUBLANES, SUBLANES), SUBLANES)
            xr, xi = hr_ref[sl, :], hi_ref[sl, :]
            for d, (sr_, si_) in zip((1, 2, 4), steps):
                keep = rows >= d
                qr = jnp.where(keep, pltpu.roll(xr, d, axis=0), 0.0)
                qi = jnp.where(keep, pltpu.roll(xi, d, axis=0), 0.0)
                mr, mi = _cmul(sr_, si_, qr, qi)
                xr, xi = xr + mr, xi + mi
            mr, mi = _cmul(pr, pi, cr, ci)
            xr, xi = xr + mr, xi + mi
            hr_ref[sl, :] = xr
            hi_ref[sl, :] = xi
            return xr[SUBLANES - 1:, :], xi[SUBLANES - 1:, :]

        cr, ci = lax.fori_loop(0, tt // SUBLANES, blk, (cr_ref[...], ci_ref[...]))
        cr_ref[...] = cr
        ci_ref[...] = ci
        y = _dot(hr_ref[...].astype(bf16), cre_ref[...]) + _dot(hi_ref[...].astype(bf16), cim_ref[...])
        y_ref[...] = y + d_ref[...] * uv.astype(f32)

    lane = pl.BlockSpec((tt, HEAD), lambda j, i: (i, j))
    st = pl.BlockSpec((tt, ns), lambda j, i: (i, j))
    b3 = pl.BlockSpec((None, HEAD, ns), lambda j, i: (j, 0, 0))
    c3 = pl.BlockSpec((None, ns, HEAD), lambda j, i: (j, 0, 0))
    arow = pl.BlockSpec((1, ns), lambda j, i: (0, j))
    return pl.pallas_call(
        body, name="ssm_fwd", grid=(nj, l // tt),
        in_specs=[lane, b3, b3, c3, c3, arow, arow, pl.BlockSpec((1, HEAD), lambda j, i: (0, j))],
        out_specs=[lane, st, st],
        out_shape=[jax.ShapeDtypeStruct((l, w), f32), jax.ShapeDtypeStruct((l, nj * ns), f32),
                   jax.ShapeDtypeStruct((l, nj * ns), f32)],
        scratch_shapes=[pltpu.VMEM((1, ns), f32), pltpu.VMEM((1, ns), f32)],
        compiler_params=_cparams("parallel", "arbitrary"),
    )(u, bre3, bim3, cre3, cimn3, a_re, a_im, d_skip)


def ssm_bwd(dy, u, h_re, h_im, bre3, bim3, cre3, cimn3, a_re, a_im, d_skip):
    l, w = u.shape
    nj = w // HEAD
    ns = STATES_PER_LANE_BLOCK
    tt = _ssm_time_tile(l)
    nt = l // tt

    def body(dy_ref, u_ref, hr_ref, hi_ref, bre_ref, bim_ref, cre_ref, cim_ref, ar_ref, ai_ref, d_ref,
             du_ref, dd_ref, dar_ref, dai_ref, dbre_ref, dbim_ref, dcre_ref, dcim_ref, kr_ref, ki_ref, cr_ref, ci_ref,
             accr_ref, acci_ref):
        i = pl.program_id(1)

        @pl.when(i == 0)
        def _():
            for ref in (cr_ref, ci_ref, accr_ref, acci_ref, dd_ref, dbre_ref, dbim_ref, dcre_ref, dcim_ref):
                ref[...] = jnp.zeros_like(ref)

        dyv = dy_ref[...]
        dyb = dyv.astype(bf16)
        uv = u_ref[...]
        kr_ref[...] = _dot_nt(dyb, cre_ref[...])
        ki_ref[...] = _dot_nt(dyb, cim_ref[...])
        steps, pr, pi = _power_tables(ar_ref[...], -ai_ref[...])
        rows = lax.broadcasted_iota(jnp.int32, (SUBLANES, ns), 0)
        qr = jnp.zeros((SUBLANES, ns), f32)
        qi = jnp.zeros((SUBLANES, ns), f32)
        for r in range(SUBLANES):
            qr = jnp.where(rows == r, pr[SUBLANES - 1 - r:SUBLANES - r, :], qr)
            qi = jnp.where(rows == r, pi[SUBLANES - 1 - r:SUBLANES - r, :], qi)
        nb = tt // SUBLANES

        def blk(t, carry):
            cr, ci, accr, acci = carry
            sl = pl.ds(pl.multiple_of((nb - 1 - t) * SUBLANES, SUBLANES), SUBLANES)
            xr, xi = kr_ref[sl, :], ki_ref[sl, :]
            for d, (sr_, si_) in zip((1, 2, 4), steps):
                keep = rows < SUBLANES - d
                zr = jnp.where(keep, pltpu.roll(xr, SUBLANES - d, axis=0), 0.0)
                zi = jnp.where(keep, pltpu.roll(xi, SUBLANES - d, axis=0), 0.0)
                mr, mi = _cmul(sr_, si_, zr, zi)
                xr, xi = xr + mr, xi + mi
            mr, mi = _cmul(qr, qi, cr, ci)
            xr, xi = xr + mr, xi + mi
            kr_ref[sl, :] = xr
            ki_ref[sl, :] = xi
            last = rows == SUBLANES - 1
            nr = jnp.where(last, cr, pltpu.roll(xr, SUBLANES - 1, axis=0))
            ni = jnp.where(last, ci, pltpu.roll(xi, SUBLANES - 1, axis=0))
            hr, hi = hr_ref[sl, :], hi_ref[sl, :]
            accr = accr + nr * hr + ni * hi
            acci = acci + ni * hr - nr * hi
            return xr[:1, :], xi[:1, :], accr, acci

        cr, ci, accr, acci = lax.fori_loop(0, nb, blk, (cr_ref[...], ci_ref[...], accr_ref[...], acci_ref[...]))
        cr_ref[...] = cr
        ci_ref[...] = ci
        accr_ref[...] = accr
        acci_ref[...] = acci
        kr, ki = kr_ref[...].astype(bf16), ki_ref[...].astype(bf16)
        du = _dot_nt(kr, bre_ref[...]) + _dot_nt(ki, bim_ref[...]) + d_ref[...] * dyv
        du_ref[...] = du.astype(du_ref.dtype)
        dd_ref[...] += jnp.sum(dyv * uv.astype(f32), axis=0, keepdims=True)
        dbre_ref[...] += _dot_tn(uv, kr)
        dbim_ref[...] += _dot_tn(uv, ki)
        dcre_ref[...] += _dot_tn(hr_ref[...].astype(bf16), dyb)
        dcim_ref[...] += _dot_tn(hi_ref[...].astype(bf16), dyb)

        @pl.when(i == nt - 1)
        def _():
            dar_ref[...] = jnp.sum(accr_ref[...], axis=0, keepdims=True)
            dai_ref[...] = jnp.sum(acci_ref[...], axis=0, keepdims=True)

    lane = pl.BlockSpec((tt, HEAD), lambda j, i: (nt - 1 - i, j))
    st = pl.BlockSpec((tt, ns), lambda j, i: (nt - 1 - i, j))
    b3 = pl.BlockSpec((None, HEAD, ns), lambda j, i: (j, 0, 0))
    c3 = pl.BlockSpec((None, ns, HEAD), lambda j, i: (j, 0, 0))
    arow = pl.BlockSpec((1, ns), lambda j, i: (0, j))
    drow = pl.BlockSpec((1, HEAD), lambda j, i: (0, j))
    return pl.pallas_call(
        body, name="ssm_bwd", grid=(nj, nt),
        in_specs=[lane, lane, st, st, b3, b3, c3, c3, arow, arow, drow],
        out_specs=[lane, drow, arow, arow, b3, b3, c3, c3],
        out_shape=[jax.ShapeDtypeStruct((l, w), bf16), jax.ShapeDtypeStruct((1, w), f32),
                   jax.ShapeDtypeStruct((1, nj * ns), f32), jax.ShapeDtypeStruct((1, nj * ns), f32),
                   jax.ShapeDtypeStruct((nj, HEAD, ns), f32), jax.ShapeDtypeStruct((nj, HEAD, ns), f32),
                   jax.ShapeDtypeStruct((nj, ns, HEAD), f32), jax.ShapeDtypeStruct((nj, ns, HEAD), f32)],
        scratch_shapes=[pltpu.VMEM((tt, ns), f32), pltpu.VMEM((tt, ns), f32), pltpu.VMEM((1, ns), f32),
                        pltpu.VMEM((1, ns), f32), pltpu.VMEM((SUBLANES, ns), f32), pltpu.VMEM((SUBLANES, ns), f32)],
        compiler_params=_cparams("parallel", "arbitrary"),
    )(dy, u, h_re, h_im, bre3, bim3, cre3, cimn3, a_re, a_im, d_skip)


def glu_fwd(y, z_src, w_glu, b_glu):
    l, w = y.shape
    tm = _row_tile(l)

    def body(y_ref, z_ref, w_ref, b_ref, g_ref, t_ref, o_ref):
        g = _gelu(y_ref[...])
        gb = g.astype(bf16)
        t = _dot(gb, w_ref[...]) + b_ref[...]
        g_ref[...] = gb
        t_ref[...] = t
        o_ref[...] = (g * jax.nn.sigmoid(t) * _silu(z_ref[...].astype(f32))).astype(o_ref.dtype)

    blk = pl.BlockSpec((tm, w), lambda i: (i, 0))
    return pl.pallas_call(
        body, name="glu_fwd", grid=(l // tm,),
        in_specs=[blk, pl.BlockSpec((tm, w), lambda i: (i, 1)), pl.BlockSpec((w, w), lambda i: (0, 0)), _row(w)],
        out_specs=[blk, blk, blk],
        out_shape=[jax.ShapeDtypeStruct((l, w), bf16), jax.ShapeDtypeStruct((l, w), f32),
                   jax.ShapeDtypeStruct((l, w), bf16)],
        compiler_params=_cparams("parallel"),
    )(y, z_src, w_glu, b_glu)


def glu_bwd(dout, y, t, z_src, w_glu):
    l, w = y.shape
    tm = _row_tile(l)

    def body(do_ref, y_ref, t_ref, z_ref, w_ref, dy_ref, dz_ref, dt_ref, db_ref):
        @pl.when(pl.program_id(0) == 0)
        def _():
            db_ref[...] = jnp.zeros_like(db_ref)

        yv, zv, dov = y_ref[...], z_ref[...].astype(f32), do_ref[...]
        g = _gelu(yv)
        sg = jax.nn.sigmoid(t_ref[...])
        dy2 = dov * _silu(zv)
        dz_ref[...] = (dov * g * sg * _silu_grad(zv)).astype(dz_ref.dtype)
        dt = dy2 * g * sg * (1.0 - sg)
        dtb = dt.astype(bf16)
        dt_ref[...] = dtb
        db_ref[...] += jnp.sum(dt, axis=0, keepdims=True)
        dg = dy2 * sg + _dot_nt(dtb, w_ref[...])
        dy_ref[...] = dg * _gelu_grad(yv)

    blk = pl.BlockSpec((tm, w), lambda i: (i, 0))
    return pl.pallas_call(
        body, name="glu_bwd", grid=(l // tm,),
        in_specs=[blk, blk, blk, pl.BlockSpec((tm, w), lambda i: (i, 1)), pl.BlockSpec((w, w), lambda i: (0, 0))],
        out_specs=[blk, blk, blk, _row(w)],
        out_shape=[jax.ShapeDtypeStruct((l, w), f32), jax.ShapeDtypeStruct((l, w), bf16),
                   jax.ShapeDtypeStruct((l, w), bf16), jax.ShapeDtypeStruct((1, w), f32)],
        compiler_params=_cparams("arbitrary"),
    )(dout, y, t, z_src, w_glu)


def _adamw(w, g, m, v):
    m = ADAM_B1 * m + (1.0 - ADAM_B1) * g
    v = ADAM_B2 * v + (1.0 - ADAM_B2) * (g * g)
    m_hat = m / (1.0 - ADAM_B1 ** ADAM_STEP)
    v_hat = v / (1.0 - ADAM_B2 ** ADAM_STEP)
    return -ADAM_LR * (m_hat / (jnp.sqrt(v_hat) + ADAM_EPS) + ADAM_WD * w), m, v


def adam_reduce(pieces, w, m, v, name):
    r, c = w.shape
    n = pieces.shape[0]
    tr = _tile(r, (256, 128, 64, 32, 16, 8))

    def body(p_ref, w_ref, m_ref, v_ref, g_ref, d_ref, nm_ref, nv_ref):
        g = p_ref[0].astype(f32)
        for s in range(1, n):
            g = g + p_ref[s].astype(f32)
        g_ref[...] = g
        d_ref[...], nm_ref[...], nv_ref[...] = _adamw(w_ref[...], g, m_ref[...], v_ref[...])

    blk = pl.BlockSpec((tr, c), lambda i: (i, 0))
    return pl.pallas_call(
        body, name=name, grid=(r // tr,),
        in_specs=[pl.BlockSpec((n, tr, c), lambda i: (0, i, 0)), blk, blk, blk],
        out_specs=[blk] * 4, out_shape=[jax.ShapeDtypeStruct((r, c), f32)] * 4,
        compiler_params=_cparams("parallel"),
    )(pieces, w, m, v)


def adam_w_mod(cond_t, dm, w, m, v):
    nl, d, cols = w.shape
    tr = _tile(d, (512, 256, 128))

    def body(c_ref, dm_ref, w_ref, m_ref, v_ref, g_ref, d_ref, nm_ref, nv_ref):
        g = jnp.dot(c_ref[...], dm_ref[...], preferred_element_type=f32, precision=lax.Precision.HIGHEST)
        g_ref[...] = g
        d_ref[...], nm_ref[...], nv_ref[...] = _adamw(w_ref[...], g, m_ref[...], v_ref[...])

    blk = pl.BlockSpec((None, tr, cols), lambda l, i: (l, i, 0))
    return pl.pallas_call(
        body, name="adam_w_mod", grid=(nl, d // tr),
        in_specs=[pl.BlockSpec((tr, N_DEV), lambda l, i: (i, 0)), pl.BlockSpec((None, N_DEV, cols), lambda l, i: (l, 0, 0)),
                  blk, blk, blk],
        out_specs=[blk] * 4, out_shape=[jax.ShapeDtypeStruct((nl, d, cols), f32)] * 4,
        compiler_params=_cparams("parallel", "parallel"),
    )(cond_t, dm, w, m, v)


def silu_rows(c_all):
    def body(c_ref, o_ref):
        o_ref[...] = _silu(c_ref[...])

    return pl.pallas_call(body, name="silu_rows", out_shape=jax.ShapeDtypeStruct(c_all.shape, f32))(c_all)


def _block_diag(x):
    g, a, b = x.shape
    nj = g // GROUPS_PER_LANE_BLOCK
    eye = jnp.eye(GROUPS_PER_LANE_BLOCK, dtype=x.dtype)
    x5 = x.reshape(nj, GROUPS_PER_LANE_BLOCK, a, b)
    return jnp.einsum("jgab,gh->jgahb", x5, eye).reshape(nj, GROUPS_PER_LANE_BLOCK * a, GROUPS_PER_LANE_BLOCK * b)


def _diag_blocks(x, a, b):
    nj = x.shape[0]
    x5 = x.reshape(nj, GROUPS_PER_LANE_BLOCK, a, GROUPS_PER_LANE_BLOCK, b)
    eye = jnp.eye(GROUPS_PER_LANE_BLOCK, dtype=x.dtype)
    return jnp.einsum("jgahb,gh->jgab", x5, eye).reshape(nj * GROUPS_PER_LANE_BLOCK, a, b)


PACK_ROW = SUBLANES * HEAD


def _pack(parts, row_multiple=SUBLANES):
    rows = []
    for p in parts:
        flat = p.reshape(-1)
        pad = (-flat.shape[0]) % PACK_ROW
        if pad:
            flat = jnp.concatenate([flat, jnp.zeros((pad,), flat.dtype)])
        rows.append(flat.reshape(-1, HEAD))
    pad = (-sum(r.shape[0] for r in rows)) % row_multiple
    if pad:
        rows.append(jnp.zeros((pad, HEAD), rows[0].dtype))
    return jnp.concatenate(rows, axis=0)


def _unpack(packed, shapes):
    out, r0 = [], 0
    for shp in shapes:
        n = math.prod(shp)
        nr = -(-n // PACK_ROW) * SUBLANES
        out.append(packed[r0:r0 + nr].reshape(-1)[:n].reshape(shp))
        r0 += nr
    return out


def adam_small(g, w, m, v):
    r, c = w.shape

    def body(g_ref, w_ref, m_ref, v_ref, d_ref, nm_ref, nv_ref):
        d_ref[...], nm_ref[...], nv_ref[...] = _adamw(w_ref[...], g_ref[...], m_ref[...], v_ref[...])

    tr = max(t for t in range(SUBLANES, 1024 + 1, SUBLANES) if r % t == 0)
    blk = pl.BlockSpec((tr, c), lambda i: (i, 0))
    return pl.pallas_call(
        body, name="adam_small", grid=(r // tr,),
        in_specs=[blk] * 4, out_specs=[blk] * 3, out_shape=[jax.ShapeDtypeStruct((r, c), f32)] * 3,
        compiler_params=_cparams("parallel"),
    )(g, w, m, v)


def kernel(x, c, ln_pre_g, ln_post_g, w_mod, b_mod, w_in_ab, w_out_ab, sgu_norm_g, sgu_w, sgu_b, w_in_ssm, w_out_ssm, lam_re, lam_im, b_re, b_im, c_re, c_im, d_skip, log_dt, w_glu, b_glu, loss_target, m_ln_pre_g, m_ln_post_g, m_w_mod, m_b_mod, m_w_in_ab, m_w_out_ab, m_sgu_norm_g, m_sgu_w, m_sgu_b, m_w_in_ssm, m_w_out_ssm, m_lam_re, m_lam_im, m_b_re, m_b_im, m_c_re, m_c_im, m_d_skip, m_log_dt, m_w_glu, m_b_glu, v_ln_pre_g, v_ln_post_g, v_w_mod, v_b_mod, v_w_in_ab, v_w_out_ab, v_sgu_norm_g, v_sgu_w, v_sgu_b, v_w_in_ssm, v_w_out_ssm, v_lam_re, v_lam_im, v_b_re, v_b_im, v_c_re, v_c_im, v_d_skip, v_log_dt, v_w_glu, v_b_glu):
    me = _my_index()
    x0 = x[0]
    l, d = x0.shape
    target = loss_target[0]
    nh = sgu_w.shape[1]
    wa = nh * HEAD
    n_grp, n_st = lam_re.shape[1], lam_re.shape[2]
    mod_cols = w_mod.shape[2]

    c_all, d_skip_all, b_glu_all = all_gather([c, d_skip, b_glu], "gather_c")
    c_all = c_all.reshape(N_DEV, d)
    d_skip_all = d_skip_all.reshape(1, -1)
    b_glu_all = b_glu_all.reshape(1, -1)

    b_cols = lax.dynamic_slice_in_dim(b_mod, me * mod_cols, mod_cols, axis=1)
    (mod_all,) = all_gather([mod_part(c_all, w_mod, b_cols)], "gather_mod")
    def after(a, first):
        return a + jnp.minimum(jnp.abs(first[(0,) * first.ndim].astype(f32)), 0.0).astype(a.dtype)

    (win_ab3,) = sequencer_exchange(GATHER, [after(w_in_ab[0], mod_all).astype(bf16)], "gather_w_in", 1)
    mod_mine = lax.dynamic_index_in_dim(mod_all, me, axis=2, keepdims=False)
    mod_rows = jnp.transpose(mod_mine, (1, 0, 2)).reshape(2, 3, 1, d)

    def rows(a, i):
        return a[i].reshape(1, d)

    shift0, scale0, gate0 = mod_rows[0, 0], mod_rows[0, 1], mod_rows[0, 2]
    h0 = prenorm_fwd(x0, rows(ln_pre_g, 0), shift0, scale0, "prenorm0")
    wout_ab3, win_ssm3, wout_ssm3, wglu = sequencer_exchange(
        GATHER, [after(w, win_ab3).astype(bf16) for w in (w_out_ab[0], w_in_ssm[0], w_out_ssm[0], w_glu[0])],
        "gather_w_rest", 2)
    proj0 = mm_nn(h0, win_ab3, bf16, "proj0")
    sgu_b3 = sgu_b[0].reshape(nh, HEAD, 1)
    out_a = sgu_fwd(proj0, sgu_norm_g, sgu_w[0], sgu_b3)
    out_b, att, tot = sb_fwd(proj0, nh)
    cat = jnp.concatenate([out_a, out_b], axis=1)
    wout_ab3 = wout_ab3.reshape(1, d, d)
    win_ssm3 = win_ssm3.reshape(1, d, d)
    wglu = wglu.reshape(w_glu.shape[2], w_glu.shape[2])
    y0 = mm_nn(cat, wout_ab3, f32, "out0")

    shift1, scale1, gate1 = mod_rows[1, 0], mod_rows[1, 1], mod_rows[1, 2]
    x1, h1 = post_prenorm_fwd(x0, y0, gate0, rows(ln_post_g, 0), rows(ln_pre_g, 1), shift1, scale1, "post0_prenorm1")
    proj1 = mm_nn(h1, win_ssm3, bf16, "proj1")
    w_ssm = proj1.shape[1] // 2
    ldt = log_dt[0].reshape(n_grp, 1)
    bt_re = jnp.transpose(b_re[0], (0, 2, 1))
    bt_im = jnp.transpose(b_im[0], (0, 2, 1))
    a_re, a_im, bbt_re, bbt_im = s5_params_fwd(lam_re[0], lam_im[0], ldt, bt_re, bt_im)
    bre3 = _block_diag(bbt_re).astype(bf16)
    bim3 = _block_diag(bbt_im).astype(bf16)
    cre3 = _block_diag(jnp.transpose(c_re[0], (0, 2, 1))).astype(bf16)
    cimn3 = _block_diag(-jnp.transpose(c_im[0], (0, 2, 1))).astype(bf16)
    a_re_row, a_im_row = a_re.reshape(1, -1), a_im.reshape(1, -1)
    u = proj1[:, :w_ssm]
    y_ssm, hs_re, hs_im = ssm_fwd(u, bre3, bim3, cre3, cimn3, a_re_row, a_im_row, d_skip_all)
    g_act, t_glu, mix1 = glu_fwd(y_ssm, proj1, wglu, b_glu_all)
    y1 = mm_nn(mix1, wout_ssm3, f32, "out1")

    dx2, loss_tile, dy1, dgate1, dgpost1 = final_loss(x1, y1, gate1, rows(ln_post_g, 1), target)

    dmix1 = mm_nt(dy1, wout_ssm3, f32, "dmix1")
    gw_out_ssm = mm_tn(mix1, dy1, N_DEV, bf16, "gw_out_ssm")
    (p_out_ssm,) = sequencer_exchange(SCATTER, [gw_out_ssm], "scatter_g1", 3)
    dy_ssm, dz1, dt_glu, db_glu = glu_bwd(dmix1, y_ssm, t_glu, proj1, wglu)
    gw_glu = mm_tn(g_act, dt_glu, 1, bf16, "gw_glu").reshape(N_DEV, -1, w_ssm)
    du, dd_skip, da_re, da_im, dbre3, dbim3, dcre3, dcimn3 = ssm_bwd(
        dy_ssm, u, hs_re, hs_im, bre3, bim3, cre3, cimn3, a_re_row, a_im_row, d_skip_all)
    dproj1 = jnp.concatenate([du, dz1], axis=1)
    gw_in_ssm = mm_tn(h1, dproj1, 1, bf16, "gw_in_ssm").reshape(N_DEV, -1, proj1.shape[1])
    p_in_ssm, p_glu = sequencer_exchange(SCATTER, [gw_in_ssm, gw_glu], "scatter_g2", 4)
    dh1 = mm_nt(dproj1, win_ssm3, f32, "dh1")
    dx1, dshift1, dscale1, dgpre1 = prenorm_bwd(dh1, x1, dx2, rows(ln_pre_g, 1), scale1, "prenorm1_bwd")
    dlr, dli, dldt, dbt_re, dbt_im = s5_params_bwd(
        lam_re[0], lam_im[0], ldt, bt_re, bt_im, da_re.reshape(n_grp, n_st), da_im.reshape(n_grp, n_st),
        _diag_blocks(dbre3, SSM_GROUP, n_st), _diag_blocks(dbim3, SSM_GROUP, n_st))
    g_b_re = jnp.transpose(dbt_re, (0, 2, 1))
    g_b_im = jnp.transpose(dbt_im, (0, 2, 1))
    g_c_re = jnp.transpose(_diag_blocks(dcre3, n_st, SSM_GROUP), (0, 2, 1))
    g_c_im = -jnp.transpose(_diag_blocks(dcimn3, n_st, SSM_GROUP), (0, 2, 1))

    dy0, dgate0, dgpost0 = post_bwd(dx1, y0, gate0, rows(ln_post_g, 0), "post0_bwd")
    dcat = mm_nt(dy0, wout_ab3, f32, "dcat")
    gw_out_ab = mm_tn(cat, dy0, 1, bf16, "gw_out_ab").reshape(N_DEV, -1, d)
    (p_out_ab,) = sequencer_exchange(SCATTER, [gw_out_ab], "scatter_g3", 5)
    da, dsgu_w, dsgu_b, dsgu_ng = sgu_bwd(proj0, dcat, sgu_norm_g, sgu_w[0], sgu_b3)
    dq, dk, dv, dbz = sb_bwd(proj0, dcat, att, tot, nh)
    dproj0 = jnp.concatenate([da, dq, dk, dv, dbz], axis=1)
    gw_in_ab = mm_tn(h0, dproj0, N_DEV, bf16, "gw_in_ab")
    (p_in_ab,) = sequencer_exchange(SCATTER, [gw_in_ab], "scatter_g4", 6)
    dh0 = mm_nt(dproj0, win_ab3, f32, "dh0")
    dx0, dshift0, dscale0, dgpre0 = prenorm_bwd(dh0, x0, dx1, rows(ln_pre_g, 0), scale0, "prenorm0_bwd")

    small_names = ["ln_pre_g", "ln_post_g", "b_mod", "sgu_norm_g", "sgu_w", "sgu_b", "lam_re", "lam_im", "b_re", "b_im",
                   "c_re", "c_im", "log_dt"]
    small_w = [ln_pre_g, ln_post_g, b_mod, sgu_norm_g, sgu_w, sgu_b, lam_re, lam_im, b_re, b_im, c_re, c_im, log_dt]
    small_m = [m_ln_pre_g, m_ln_post_g, m_b_mod, m_sgu_norm_g, m_sgu_w, m_sgu_b, m_lam_re, m_lam_im, m_b_re, m_b_im,
               m_c_re, m_c_im, m_log_dt]
    small_v = [v_ln_pre_g, v_ln_post_g, v_b_mod, v_sgu_norm_g, v_sgu_w, v_sgu_b, v_lam_re, v_lam_im, v_b_re, v_b_im,
               v_c_re, v_c_im, v_log_dt]
    dmod = jnp.concatenate([dshift0, dscale0, dgate0, dshift1, dscale1, dgate1], axis=1)
    small_g = [jnp.concatenate([dgpre0, dgpre1]), jnp.concatenate([dgpost0, dgpost1]), dmod, dsgu_ng, dsgu_w, dsgu_b,
               dlr, dli, g_b_re, g_b_im, g_c_re, g_c_im, dldt]
    shapes = [w.shape for w in small_w]
    g_sum, dmod_all = all_reduce_rows(_pack(small_g + [dd_skip, db_glu, loss_tile], SUBLANES * N_DEV), dmod,
                                      "reduce_small_grads")
    n_rows_small = sum(-(-math.prod(s) // PACK_ROW) * SUBLANES for s in shapes)
    loss = g_sum[n_rows_small + 2 * (d_skip_all.shape[1] // HEAD), 0] * (0.5 / d)
    new_small = adam_small(g_sum, _pack(small_w), _pack(small_m), _pack(small_v))
    r_small = [_unpack(o, shapes) for o in [g_sum[:n_rows_small]] + list(new_small)]
    small = {n: [r_small[k][i] for k in range(4)] for i, n in enumerate(small_names)}
    vec_rows = d_skip_all.shape[1] // HEAD

    def my_columns(r0):
        whole = g_sum[r0:r0 + vec_rows].reshape(1, 1, -1)
        return lax.dynamic_slice_in_dim(whole, me * d_skip.shape[1], d_skip.shape[1], axis=2)

    def sharded(p, w, m, v, name):
        shp = w.shape
        w2, m2, v2 = (a.reshape(-1, shp[-1]) for a in (w, m, v))
        return [o.reshape(shp) for o in adam_reduce(p.reshape(p.shape[0], -1, shp[-1]), w2, m2, v2, name)]

    r_d_skip = sharded(my_columns(n_rows_small), d_skip, m_d_skip, v_d_skip, "adam_d_skip")
    r_b_glu = sharded(my_columns(n_rows_small + vec_rows), b_glu, m_b_glu, v_b_glu, "adam_b_glu")
    r_w_out_ssm = sharded(p_out_ssm, w_out_ssm, m_w_out_ssm, v_w_out_ssm, "adam_w_out_ssm")
    r_w_in_ssm = sharded(p_in_ssm, w_in_ssm, m_w_in_ssm, v_w_in_ssm, "adam_w_in_ssm")
    r_w_glu = sharded(p_glu, w_glu, m_w_glu, v_w_glu, "adam_w_glu")
    r_w_out_ab = sharded(p_out_ab, w_out_ab, m_w_out_ab, v_w_out_ab, "adam_w_out_ab")
    r_w_in_ab = sharded(p_in_ab, w_in_ab, m_w_in_ab, v_w_in_ab, "adam_w_in_ab")

    dm_cols = jnp.transpose(
        lax.dynamic_slice_in_dim(dmod_all.reshape(N_DEV, 2, 3 * d), me * mod_cols, mod_cols, axis=2), (1, 0, 2))
    cond_t = jnp.transpose(silu_rows(c_all))
    r_w_mod = adam_w_mod(cond_t, dm_cols, w_mod, m_w_mod, v_w_mod)

    res = dict(small)
    res.update(w_mod=r_w_mod, w_in_ab=r_w_in_ab, w_out_ab=r_w_out_ab, w_in_ssm=r_w_in_ssm, w_out_ssm=r_w_out_ssm,
               d_skip=r_d_skip, w_glu=r_w_glu, b_glu=r_b_glu)
    order = ["ln_pre_g", "ln_post_g", "w_mod", "b_mod", "w_in_ab", "w_out_ab", "sgu_norm_g", "sgu_w", "sgu_b", "w_in_ssm",
             "w_out_ssm", "lam_re", "lam_im", "b_re", "b_im", "c_re", "c_im", "d_skip", "log_dt", "w_glu", "b_glu"]
    outs = [loss, dx0.reshape(x.shape)]
    for k in range(4):
        outs += [res[n][k] for n in order]
    return tuple(outs)
```

```python
import functools
import math

import jax
import jax.numpy as jnp
from jax import lax
from jax.experimental import pallas as pl
from jax.experimental.pallas import tpu as pltpu
from jax.experimental.pallas import tpu_sc as plsc

f32 = jnp.float32
bf16 = jnp.bfloat16

N_DEV = 8
EPS = 1e-6
HEAD = 128
SUBLANES = 8
SSM_GROUP = 16
SSM_STATE = 64
GROUPS_PER_LANE_BLOCK = HEAD // SSM_GROUP
STATES_PER_LANE_BLOCK = GROUPS_PER_LANE_BLOCK * SSM_STATE
VMEM_LIMIT = 56 * 2 ** 20
ADAM_LR, ADAM_B1, ADAM_B2, ADAM_EPS, ADAM_WD, ADAM_STEP = 0.001, 0.9, 0.999, 1e-08, 0.01, 10
_GELU_C0 = math.sqrt(2.0 / math.pi)
_GELU_C1 = 0.044715
MESH = pl.DeviceIdType.MESH


def _cparams(*sem):
    return pltpu.CompilerParams(dimension_semantics=sem if sem else None, vmem_limit_bytes=VMEM_LIMIT)


def _gelu(x):
    return 0.5 * x * (1.0 + jnp.tanh(_GELU_C0 * (x + _GELU_C1 * x * x * x)))


def _gelu_grad(x):
    t = jnp.tanh(_GELU_C0 * (x + _GELU_C1 * x * x * x))
    return 0.5 * (1.0 + t) + 0.5 * x * (1.0 - t * t) * _GELU_C0 * (1.0 + 3.0 * _GELU_C1 * x * x)


def _silu(x):
    return x * jax.nn.sigmoid(x)


def _silu_grad(x):
    s = jax.nn.sigmoid(x)
    return s * (1.0 + x * (1.0 - s))


def _dot(a, b):
    return jnp.dot(a, b, preferred_element_type=f32)


def _dot_nt(a, b):
    return lax.dot_general(a, b, (((1,), (1,)), ((), ())), preferred_element_type=f32)


def _dot_tn(a, b):
    return lax.dot_general(a, b, (((0,), (0,)), ((), ())), preferred_element_type=f32)


def _split_bf16(v):
    hi = v.astype(bf16)
    lo = (v - hi.astype(f32)).astype(bf16)
    return hi, lo


def _row(d):
    return pl.BlockSpec((1, d), lambda *_: (0, 0))


def _my_index():
    return 4 * lax.axis_index("x") + 2 * lax.axis_index("y") + lax.axis_index("c")


def _peer(k):
    x, y, c = lax.axis_index("x"), lax.axis_index("y"), lax.axis_index("c")
    return (1 - x if k & 4 else x, 1 - y if k & 2 else y, 1 - c if k & 1 else c)


def all_gather(arrs, name):
    n = len(arrs)

    def body(*refs):
        ins, outs = refs[:n], refs[n:2 * n]
        send, recv, local = refs[2 * n:]
        me = _my_index()
        copies = []
        for a in range(n):
            cp = pltpu.make_async_copy(ins[a], outs[a].at[me], local.at[a])
            cp.start()
            copies.append(cp)
            for k in range(1, N_DEV):
                s = a * (N_DEV - 1) + k - 1
                cp = pltpu.make_async_remote_copy(src_ref=ins[a], dst_ref=outs[a].at[me], send_sem=send.at[s],
                                                  recv_sem=recv.at[s], device_id=_peer(k), device_id_type=MESH)
                cp.start()
                copies.append(cp)
        for cp in copies:
            cp.wait()

    any_spec = pl.BlockSpec(memory_space=pl.ANY)
    outs = pl.pallas_call(
        body, name=name,
        out_shape=[jax.ShapeDtypeStruct((N_DEV,) + a.shape, a.dtype) for a in arrs],
        in_specs=[any_spec] * n, out_specs=[any_spec] * n,
        scratch_shapes=[pltpu.SemaphoreType.DMA((n * (N_DEV - 1),)), pltpu.SemaphoreType.DMA((n * (N_DEV - 1),)),
                        pltpu.SemaphoreType.DMA((n,))],
        compiler_params=pltpu.CompilerParams(has_side_effects=True),
    )(*arrs)
    return list(outs)


def all_reduce_rows(pack, extra, name):
    r, c = pack.shape
    rs = r // N_DEV
    n_peer = N_DEV - 1

    def body(p_ref, x_ref, o_ref, xo_ref, land, red, send1, recv1, send2, recv2, sendx, recvx, local):
        me = _my_index()

        def rows(i):
            return pl.ds(pl.multiple_of(i * rs, SUBLANES), rs)

        own = [pltpu.make_async_copy(p_ref.at[rows(me)], land.at[me], local.at[0]),
               pltpu.make_async_copy(x_ref, xo_ref.at[me], local.at[1])]
        first = []
        for k in range(1, N_DEV):
            first.append(pltpu.make_async_remote_copy(
                src_ref=p_ref.at[rows(jnp.bitwise_xor(me, k))], dst_ref=land.at[me], send_sem=send1.at[k - 1],
                recv_sem=recv1.at[k - 1], device_id=_peer(k), device_id_type=MESH))
            first.append(pltpu.make_async_remote_copy(
                src_ref=x_ref, dst_ref=xo_ref.at[me], send_sem=sendx.at[k - 1], recv_sem=recvx.at[k - 1],
                device_id=_peer(k), device_id_type=MESH))
        for cp in own + first:
            cp.start()
        for cp in own + first:
            cp.wait()
        acc = land[0]
        for s in range(1, N_DEV):
            acc = acc + land[s]
        red[...] = acc
        mine = pltpu.make_async_copy(red, o_ref.at[rows(me)], local.at[2])
        second = [pltpu.make_async_remote_copy(
            src_ref=red, dst_ref=o_ref.at[rows(me)], send_sem=send2.at[k - 1], recv_sem=recv2.at[k - 1],
            device_id=_peer(k), device_id_type=MESH) for k in range(1, N_DEV)]
        for cp in [mine] + second:
            cp.start()
        for cp in [mine] + second:
            cp.wait()

    any_spec = pl.BlockSpec(memory_space=pl.ANY)
    return pl.pallas_call(
        body, name=name,
        out_shape=[jax.ShapeDtypeStruct((r, c), pack.dtype), jax.ShapeDtypeStruct((N_DEV,) + extra.shape, extra.dtype)],
        in_specs=[any_spec, any_spec], out_specs=[any_spec, any_spec],
        scratch_shapes=[pltpu.VMEM((N_DEV, rs, c), pack.dtype), pltpu.VMEM((rs, c), pack.dtype)]
        + [pltpu.SemaphoreType.DMA((n_peer,))] * 6 + [pltpu.SemaphoreType.DMA((3,))],
        compiler_params=pltpu.CompilerParams(has_side_effects=True),
    )(pack, extra)


GATHER, SCATTER = "gather", "scatter"


def _exchange_copies(srcs, lands, send, recv):
    me = _my_index()
    copies = []
    for a, (src, land) in enumerate(zip(srcs, lands)):
        for k in range(1, N_DEV):
            s = a * (N_DEV - 1) + k - 1
            copies.append(pltpu.make_async_remote_copy(
                src_ref=src.at[jnp.bitwise_xor(me, k)], dst_ref=land.at[me],
                send_sem=send.at[s], recv_sem=recv.at[s], device_id=_peer(k), device_id_type=MESH))
    return copies


def sequencer_exchange(kind, arrs, name, collective_id):
    n = len(arrs)
    n_sem = n * (N_DEV - 1)
    land_shapes = [((N_DEV,) + a.shape if kind == GATHER else a.shape) for a in arrs]
    srcs = [jax.new_ref(a, memory_space=pltpu.MemorySpace.HBM) for a in arrs]
    lands = [jax.empty_ref(jax.ShapeDtypeStruct(s, a.dtype), memory_space=pltpu.MemorySpace.HBM)
             for s, a in zip(land_shapes, arrs)]

    @pl.kernel(mesh=plsc.ScalarSubcoreMesh(axis_name="sequencer", num_cores=1), name=name,
               scratch_types=(pltpu.SemaphoreType.DMA((n_sem,)), pltpu.SemaphoreType.DMA((n_sem,)),
                              pltpu.SemaphoreType.DMA((n,))),
               compiler_params=pltpu.CompilerParams(collective_id=collective_id))
    def launch(send, recv, local):
        barrier = pltpu.get_barrier_semaphore()
        for k in range(1, N_DEV):
            pl.semaphore_signal(barrier, inc=1, device_id=_peer(k), device_id_type=MESH)
        pl.semaphore_wait(barrier, N_DEV - 1)
        me = _my_index()
        mine = [pltpu.make_async_copy(src if kind == GATHER else src.at[me], land.at[me], local.at[a])
                for a, (src, land) in enumerate(zip(srcs, lands))]
        if kind == SCATTER:
            copies = mine + _exchange_copies(srcs, lands, send, recv)
            for cp in copies:
                cp.start()
            for cp in copies:
                cp.wait()
            return

        def block_copy(a, slot, block, k, src=None):
            s = a * (N_DEV - 1) + slot
            return pltpu.make_async_remote_copy(
                src_ref=lands[a].at[block] if src is None else src, dst_ref=lands[a].at[block],
                send_sem=send.at[s], recv_sem=recv.at[s], device_id=_peer(k), device_id_type=MESH)

        chips = (2, 4, 6)
        sibling = jnp.bitwise_xor(me, 1)
        first = [block_copy(a, slot, me, k, src=srcs[a]) for a in range(n) for slot, k in enumerate((1,) + chips)]
        for cp in mine + first:
            cp.start()
        passed = []
        for a in range(n):
            for i, k in enumerate(chips):
                block = jnp.bitwise_xor(me, k)
                block_copy(a, 1 + i, block, k).wait_recv()
                passed.append(block_copy(a, 4 + i, block, 1))
                passed[-1].start()
        for a in range(n):
            block_copy(a, 0, sibling, 1).wait_recv()
            for i, k in enumerate(chips):
                block_copy(a, 4 + i, jnp.bitwise_xor(sibling, k), 1).wait_recv()
        for cp in mine:
            cp.wait()
        for cp in first + passed:
            cp.wait_send()

    launch()
    return [land[...] for land in lands]


def _tile(n, pref):
    for t in pref:
        if n % t == 0:
            return t
    return n


MM_WIDE = 1024
MM_WEIGHT_BLOCK = 8 * 2 ** 20


def _blocks_per_step(nb, fits):
    return max(g for g in range(1, nb + 1) if nb % g == 0 and fits(g))


def mm_nn(a, b3, out_dtype, name):
    m, k = a.shape
    nb, _, bn = b3.shape
    tm = _tile(m, (512, 256, 128))
    tn = _tile(bn, (1024, 896, 512, 256, 128))
    per = bn // tn
    gb = _blocks_per_step(nb, lambda g: g == 1 or (per == 1 and g * bn <= MM_WIDE))

    def body(a_ref, b_ref, o_ref):
        for g in range(gb):
            o_ref[:, g * tn:(g + 1) * tn] = _dot(a_ref[...], b_ref[g]).astype(o_ref.dtype)

    return pl.pallas_call(
        body, name=name, grid=(m // tm, nb // gb, per),
        in_specs=[pl.BlockSpec((tm, k), lambda i, j, jj: (i, 0)),
                  pl.BlockSpec((gb, k, tn), lambda i, j, jj: (j, 0, jj))],
        out_specs=pl.BlockSpec((tm, gb * tn), lambda i, j, jj: (i, j * per + jj)),
        out_shape=jax.ShapeDtypeStruct((m, nb * bn), out_dtype),
        compiler_params=_cparams("parallel", "arbitrary", "arbitrary"),
    )(a, b3)


def mm_nt(a, w3, out_dtype, name):
    m, _ = a.shape
    nb, ko, bn = w3.shape
    tm = _tile(m, (512, 256, 128))
    tko = _tile(ko, (1024, 512, 256, 128))
    gb = _blocks_per_step(nb, lambda g: g * tko * bn * w3.dtype.itemsize <= MM_WEIGHT_BLOCK)
    ns = nb // gb

    def body(a_ref, w_ref, o_ref, acc_ref):
        j = pl.program_id(2)

        @pl.when(j == 0)
        def _():
            acc_ref[...] = jnp.zeros_like(acc_ref)

        part = _dot_nt(a_ref[:, :bn], w_ref[0])
        for g in range(1, gb):
            part += _dot_nt(a_ref[:, g * bn:(g + 1) * bn], w_ref[g])
        acc_ref[...] += part

        @pl.when(j == ns - 1)
        def _():
            o_ref[...] = acc_ref[...].astype(o_ref.dtype)

    return pl.pallas_call(
        body, name=name, grid=(m // tm, ko // tko, ns),
        in_specs=[pl.BlockSpec((tm, gb * bn), lambda i, o, j: (i, j)),
                  pl.BlockSpec((gb, tko, bn), lambda i, o, j: (j, o, 0))],
        out_specs=pl.BlockSpec((tm, tko), lambda i, o, j: (i, o)),
        out_shape=jax.ShapeDtypeStruct((m, ko), out_dtype),
        scratch_shapes=[pltpu.VMEM((tm, tko), f32)],
        compiler_params=_cparams("parallel", "arbitrary", "arbitrary"),
    )(a, w3)


def mm_tn(a, dy, ncb, out_dtype, name):
    l, ka = a.shape
    _, n = dy.shape
    bn = n // ncb
    tl = _tile(l, (1024, 512, 256, 128))
    tka = _tile(ka, (512, 256, 128))
    tn = _tile(bn, (1024, 896, 512, 256, 128))
    per = bn // tn
    gb = _blocks_per_step(ncb, lambda g: g == 1 or (per == 1 and g * bn <= MM_WIDE))
    nl = l // tl

    def body(a_ref, dy_ref, o_ref, acc_ref):
        s = pl.program_id(2)

        @pl.when(s == 0)
        def _():
            acc_ref[...] = jnp.zeros_like(acc_ref)

        acc_ref[...] += _dot_tn(a_ref[...], dy_ref[...])

        @pl.when(s == nl - 1)
        def _():
            for g in range(gb):
                o_ref[g] = acc_ref[:, g * tn:(g + 1) * tn].astype(o_ref.dtype)

    return pl.pallas_call(
        body, name=name, grid=(ka // tka, n // (gb * tn), nl),
        in_specs=[pl.BlockSpec((tl, tka), lambda i, j, s: (s, i)),
                  pl.BlockSpec((tl, gb * tn), lambda i, j, s: (s, j))],
        out_specs=pl.BlockSpec((gb, tka, tn), lambda i, j, s: (j // per, i, j % per)),
        out_shape=jax.ShapeDtypeStruct((ncb, ka, bn), out_dtype),
        scratch_shapes=[pltpu.VMEM((tka, gb * tn), f32)],
        compiler_params=_cparams("parallel", "parallel", "arbitrary"),
    )(a, dy)


def mod_part(c_all, w_mod, b_cols):
    nl, d, cols = w_mod.shape

    def body(c_ref, w_ref, b_ref, o_ref):
        cond = _silu(c_ref[...]).astype(bf16)
        o_ref[...] = _dot(cond, w_ref[...].astype(bf16)) + b_ref[...]

    return pl.pallas_call(
        body, name="mod_part", grid=(nl,),
        in_specs=[pl.BlockSpec((N_DEV, d), lambda l: (0, 0)),
                  pl.BlockSpec((None, d, cols), lambda l: (l, 0, 0)),
                  pl.BlockSpec((None, 1, cols), lambda l: (l, 0, 0))],
        out_specs=pl.BlockSpec((None, N_DEV, cols), lambda l: (l, 0, 0)),
        out_shape=jax.ShapeDtypeStruct((nl, N_DEV, cols), f32),
        compiler_params=_cparams("arbitrary"),
    )(c_all, w_mod, b_cols.reshape(nl, 1, cols))


def _row_tile(l):
    return _tile(l, (256, 128))


def prenorm_fwd(x, g, shift, scale, name):
    l, d = x.shape
    tm = _row_tile(l)

    def body(x_ref, g_ref, sh_ref, sc_ref, h_ref):
        xv = x_ref[...]
        r = lax.rsqrt(jnp.mean(xv * xv, axis=-1, keepdims=True) + EPS)
        h_ref[...] = (xv * r * (g_ref[...] * (1.0 + sc_ref[...])) + sh_ref[...]).astype(h_ref.dtype)

    return pl.pallas_call(
        body, name=name, grid=(l // tm,),
        in_specs=[pl.BlockSpec((tm, d), lambda i: (i, 0)), _row(d), _row(d), _row(d)],
        out_specs=pl.BlockSpec((tm, d), lambda i: (i, 0)),
        out_shape=jax.ShapeDtypeStruct((l, d), bf16),
        compiler_params=_cparams("parallel"),
    )(x, g, shift, scale)


def post_prenorm_fwd(x, y, gate, g_post, g_pre, shift, scale, name):
    l, d = x.shape
    tm = _row_tile(l)

    def body(x_ref, y_ref, gate_ref, gp_ref, g_ref, sh_ref, sc_ref, o_ref, h_ref):
        yv = y_ref[...]
        r = lax.rsqrt(jnp.mean(yv * yv, axis=-1, keepdims=True) + EPS)
        xv = x_ref[...] + gate_ref[...] * (yv * r * gp_ref[...])
        o_ref[...] = xv
        r = lax.rsqrt(jnp.mean(xv * xv, axis=-1, keepdims=True) + EPS)
        h_ref[...] = (xv * r * (g_ref[...] * (1.0 + sc_ref[...])) + sh_ref[...]).astype(h_ref.dtype)

    blk = pl.BlockSpec((tm, d), lambda i: (i, 0))
    return pl.pallas_call(
        body, name=name, grid=(l // tm,),
        in_specs=[blk, blk] + [_row(d)] * 5, out_specs=[blk, blk],
        out_shape=[jax.ShapeDtypeStruct((l, d), f32), jax.ShapeDtypeStruct((l, d), bf16)],
        compiler_params=_cparams("parallel"),
    )(x, y, gate, g_post, g_pre, shift, scale)


def _post_bwd_rows(dxv, yv, r, gate, gv, dy_ref, dgate_ref, dg_ref):
    yn = yv * r
    dgate_ref[...] += jnp.sum(dxv * yn * gv, axis=0, keepdims=True)
    dyg = dxv * gate
    dg_ref[...] += jnp.sum(dyg * yn, axis=0, keepdims=True)
    dyn = dyg * gv
    dy_ref[...] = (r * (dyn - yn * jnp.mean(dyn * yn, axis=-1, keepdims=True))).astype(dy_ref.dtype)


def final_loss(x, y, gate, g, target):
    l, d = x.shape
    tm = _row_tile(l)

    def body(x_ref, y_ref, gate_ref, g_ref, t_ref, dx_ref, loss_ref, dy_ref, dgate_ref, dg_ref):
        @pl.when(pl.program_id(0) == 0)
        def _():
            loss_ref[...] = jnp.zeros_like(loss_ref)
            dgate_ref[...] = jnp.zeros_like(dgate_ref)
            dg_ref[...] = jnp.zeros_like(dg_ref)

        yv, gate, gv = y_ref[...], gate_ref[...], g_ref[...]
        r = lax.rsqrt(jnp.mean(yv * yv, axis=-1, keepdims=True) + EPS)
        diff = x_ref[...] + gate * (yv * r * gv) - t_ref[...]
        dxv = diff * (1.0 / d)
        dx_ref[...] = dxv
        loss_ref[...] += jnp.sum(diff * diff)
        _post_bwd_rows(dxv, yv, r, gate, gv, dy_ref, dgate_ref, dg_ref)

    blk = pl.BlockSpec((tm, d), lambda i: (i, 0))
    return pl.pallas_call(
        body, name="final_loss", grid=(l // tm,),
        in_specs=[blk, blk, _row(d), _row(d), blk],
        out_specs=[blk, pl.BlockSpec((SUBLANES, HEAD), lambda i: (0, 0)), blk, _row(d), _row(d)],
        out_shape=[jax.ShapeDtypeStruct((l, d), f32), jax.ShapeDtypeStruct((SUBLANES, HEAD), f32),
                   jax.ShapeDtypeStruct((l, d), bf16), jax.ShapeDtypeStruct((1, d), f32), jax.ShapeDtypeStruct((1, d), f32)],
        compiler_params=_cparams("arbitrary"),
    )(x, y, gate, g, target)


def post_bwd(dx, y, gate, g, name):
    l, d = dx.shape
    tm = _row_tile(l)

    def body(dx_ref, y_ref, gate_ref, g_ref, dy_ref, dgate_ref, dg_ref):
        @pl.when(pl.program_id(0) == 0)
        def _():
            dgate_ref[...] = jnp.zeros_like(dgate_ref)
            dg_ref[...] = jnp.zeros_like(dg_ref)

        yv = y_ref[...]
        r = lax.rsqrt(jnp.mean(yv * yv, axis=-1, keepdims=True) + EPS)
        _post_bwd_rows(dx_ref[...], yv, r, gate_ref[...], g_ref[...], dy_ref, dgate_ref, dg_ref)

    blk = pl.BlockSpec((tm, d), lambda i: (i, 0))
    return pl.pallas_call(
        body, name=name, grid=(l // tm,),
        in_specs=[blk, blk, _row(d), _row(d)], out_specs=[blk, _row(d), _row(d)],
        out_shape=[jax.ShapeDtypeStruct((l, d), bf16), jax.ShapeDtypeStruct((1, d), f32),
                   jax.ShapeDtypeStruct((1, d), f32)],
        compiler_params=_cparams("arbitrary"),
    )(dx, y, gate, g)


def prenorm_bwd(dh, x, dx_next, g, scale, name):
    l, d = x.shape
    tm = _row_tile(l)

    def body(dh_ref, x_ref, dxn_ref, g_ref, sc_ref, dx_ref, dsh_ref, dsc_ref, dg_ref):
        @pl.when(pl.program_id(0) == 0)
        def _():
            dsh_ref[...] = jnp.zeros_like(dsh_ref)
            dsc_ref[...] = jnp.zeros_like(dsc_ref)
            dg_ref[...] = jnp.zeros_like(dg_ref)

        xv, dhv, gv, sc1 = x_ref[...], dh_ref[...], g_ref[...], 1.0 + sc_ref[...]
        r = lax.rsqrt(jnp.mean(xv * xv, axis=-1, keepdims=True) + EPS)
        xn = xv * r
        dhx = dhv * xn
        dsh_ref[...] += jnp.sum(dhv, axis=0, keepdims=True)
        dsc_ref[...] += jnp.sum(dhx * gv, axis=0, keepdims=True)
        dg_ref[...] += jnp.sum(dhx * sc1, axis=0, keepdims=True)
        dxn = dhv * (gv * sc1)
        dx_ref[...] = dxn_ref[...] + r * (dxn - xn * jnp.mean(dxn * xn, axis=-1, keepdims=True))

    blk = pl.BlockSpec((tm, d), lambda i: (i, 0))
    return pl.pallas_call(
        body, name=name, grid=(l // tm,),
        in_specs=[blk, blk, blk, _row(d), _row(d)], out_specs=[blk, _row(d), _row(d), _row(d)],
        out_shape=[jax.ShapeDtypeStruct((l, d), f32)] + [jax.ShapeDtypeStruct((1, d), f32)] * 3,
        compiler_params=_cparams("arbitrary"),
    )(dh, x, dx_next, g, scale)


def _tril_mask():
    r = lax.broadcasted_iota(jnp.int32, (HEAD, HEAD), 0)
    c = lax.broadcasted_iota(jnp.int32, (HEAD, HEAD), 1)
    return r >= c


def sgu_fwd(proj, norm_g, w_s, b_s):
    l = proj.shape[0]
    nh = w_s.shape[0]
    wa = nh * HEAD

    def body(au_ref, av_ref, az_ref, ng_ref, w_ref, b_ref, o_ref):
        tril = _tril_mask()
        for h in range(nh):
            sl = slice(h * HEAD, (h + 1) * HEAD)
            gv = _gelu(av_ref[:, sl].astype(f32))
            r = lax.rsqrt(jnp.mean(gv * gv, axis=-1, keepdims=True) + EPS)
            vh = gv * r * ng_ref[:, sl]
            wm = jnp.where(tril, w_ref[h], 0.0).astype(bf16)
            s = _dot(wm, vh.astype(bf16)) + b_ref[h]
            o_ref[:, sl] = (_gelu(au_ref[:, sl].astype(f32)) * s * _silu(az_ref[:, sl].astype(f32))).astype(o_ref.dtype)

    def col(j):
        return pl.BlockSpec((HEAD, wa), lambda n: (n, j))

    return pl.pallas_call(
        body, name="sgu_fwd", grid=(l // HEAD,),
        in_specs=[col(0), col(1), col(2), _row(wa),
                  pl.BlockSpec((nh, HEAD, HEAD), lambda n: (0, 0, 0)), pl.BlockSpec((nh, HEAD, 1), lambda n: (0, 0, 0))],
        out_specs=pl.BlockSpec((HEAD, wa), lambda n: (n, 0)),
        out_shape=jax.ShapeDtypeStruct((l, wa), bf16),
        compiler_params=_cparams("parallel"),
    )(proj, proj, proj, norm_g, w_s, b_s)


def sgu_bwd(proj, dcat, norm_g, w_s, b_s):
    l = proj.shape[0]
    nh = w_s.shape[0]
    wa = nh * HEAD

    def body(au_ref, av_ref, az_ref, do_ref, ng_ref, w_ref, b_ref, da_ref, dw_ref, db_ref, dng_ref):
        @pl.when(pl.program_id(0) == 0)
        def _():
            dw_ref[...] = jnp.zeros_like(dw_ref)
            db_ref[...] = jnp.zeros_like(db_ref)
            dng_ref[...] = jnp.zeros_like(dng_ref)

        tril = _tril_mask()
        for h in range(nh):
            sl = slice(h * HEAD, (h + 1) * HEAD)
            au, av, az = au_ref[:, sl].astype(f32), av_ref[:, sl].astype(f32), az_ref[:, sl].astype(f32)
            ng = ng_ref[:, sl]
            gv = _gelu(av)
            r = lax.rsqrt(jnp.mean(gv * gv, axis=-1, keepdims=True) + EPS)
            gvn = gv * r
            vh = (gvn * ng).astype(bf16)
            wm = jnp.where(tril, w_ref[h], 0.0).astype(bf16)
            s = _dot(wm, vh) + b_ref[h]
            gu, sz = _gelu(au), _silu(az)
            dov = do_ref[:, sl].astype(f32)
            da_ref[:, sl] = (dov * s * sz * _gelu_grad(au)).astype(da_ref.dtype)
            da_ref[:, 2 * wa + h * HEAD:2 * wa + (h + 1) * HEAD] = (dov * gu * s * _silu_grad(az)).astype(da_ref.dtype)
            ds = dov * gu * sz
            db_ref[h] += jnp.sum(ds, axis=-1, keepdims=True)
            dsb = ds.astype(bf16)
            dw_ref[h] += jnp.where(tril, _dot_nt(dsb, vh), 0.0)
            dvh = _dot_tn(wm, dsb)
            dng_ref[:, sl] += jnp.sum(dvh * gvn, axis=0, keepdims=True)
            dgvn = dvh * ng
            dgv = r * (dgvn - gvn * jnp.mean(dgvn * gvn, axis=-1, keepdims=True))
            da_ref[:, wa + h * HEAD:wa + (h + 1) * HEAD] = (dgv * _gelu_grad(av)).astype(da_ref.dtype)

    def col(j):
        return pl.BlockSpec((HEAD, wa), lambda n: (n, j))

    whole_w = pl.BlockSpec((nh, HEAD, HEAD), lambda n: (0, 0, 0))
    whole_b = pl.BlockSpec((nh, HEAD, 1), lambda n: (0, 0, 0))
    return pl.pallas_call(
        body, name="sgu_bwd", grid=(l // HEAD,),
        in_specs=[col(0), col(1), col(2), col(0), _row(wa), whole_w, whole_b],
        out_specs=[pl.BlockSpec((HEAD, 3 * wa), lambda n: (n, 0)), whole_w, whole_b, _row(wa)],
        out_shape=[jax.ShapeDtypeStruct((l, 3 * wa), bf16), jax.ShapeDtypeStruct((nh, HEAD, HEAD), f32),
                   jax.ShapeDtypeStruct((nh, HEAD, 1), f32), jax.ShapeDtypeStruct((1, wa), f32)],
        compiler_params=_cparams("arbitrary"),
    )(proj, proj, proj, dcat, norm_g, w_s, b_s)


_LOG2E = 1.0 / math.log(2.0)


def _sb_scores(q, k, scale):
    z = _dot_nt(q, k) * (scale * _LOG2E)
    return z, jnp.maximum(z, 0.0) + jnp.log2(1.0 + jnp.exp2(-jnp.abs(z)))


SB_KEYS = 256


def _sb_sum_matrix(tri, kb):
    s = lax.broadcasted_iota(jnp.int32, (2 * kb, kb + HEAD), 0) % kb
    j = lax.broadcasted_iota(jnp.int32, (2 * kb, kb + HEAD), 1)
    return jnp.where(jnp.logical_or(j >= kb, tri(s, j)), 1.0, 0.0).astype(bf16)


def _sb_sums(x, sums):
    kb = x.shape[1]
    c2 = _dot(jnp.concatenate(_split_bf16(x), axis=1), sums)
    return c2[:, :kb], c2[:, kb:]


def _sb_wide(v, kb):
    return jnp.concatenate([v] * (kb // HEAD), axis=1) if kb > HEAD else v


def _sb_q_tile(l, most=512):
    return _tile(l, tuple(t for t in (1024, 512, 256, 128) if t <= most))


def _sb_band_levels(band):
    return _tile(band, (4, 2, 1))


def _sb_heads_per_step(nh, most):
    return _tile(nh, tuple(h for h in (4, 2) if h <= most))


def sb_fwd(proj, nh):
    l = proj.shape[0]
    wb = nh * HEAD
    tq = _sb_q_tile(l, 1024)
    kb = min(SB_KEYS, tq)
    band = tq // kb
    hp = _sb_heads_per_step(nh, 2)
    levels = _sb_band_levels(band)
    scale = 1.0 / math.sqrt(HEAD)
    qc, kc, vc, zc = 3 * nh, 4 * nh, 5 * nh, 6 * nh

    def body(q_ref, k_ref, v_ref, bz_ref, o_ref, att_ref, tot_ref):
        i = pl.program_id(1)
        sums = _sb_sum_matrix(lambda s, j: s > j, kb)
        t_pos = i * tq + lax.broadcasted_iota(jnp.int32, (tq, kb), 0)
        s_off = lax.broadcasted_iota(jnp.int32, (tq, kb), 1)

        def step(j, carry, masked, row0=0):
            rows = pl.ds(pl.multiple_of(j * kb, kb), kb)
            out = []
            for e in range(hp):
                acc, tot = carry[e]
                sl = slice(e * HEAD, (e + 1) * HEAD)
                z, sp = _sb_scores(q_ref[row0:, sl], k_ref[rows, sl], scale)
                lb = z - sp
                if masked:
                    mask = s_off[row0:] + j * kb < t_pos[row0:]
                    sp = jnp.where(mask, sp, 0.0)
                later, total = _sb_sums(sp, sums)
                w = jnp.exp2(lb + _sb_wide(tot[row0:], kb) - later)
                if masked:
                    w = jnp.where(mask, w, 0.0)
                new = (acc[row0:] + _dot(w.astype(bf16), v_ref[rows, sl]), tot[row0:] - total)
                out.append(tuple(jnp.concatenate([old[:row0], upd]) if row0 else upd for old, upd in zip(carry[e], new)))
            return tuple(out)

        zero = jnp.zeros((tq, HEAD), f32)
        carry = ((zero, zero),) * hp
        for lv in reversed(range(levels)):
            carry = lax.fori_loop(
                0, band // levels,
                lambda t, c, lv=lv: step(band * i + (lv + 1) * (band // levels) - 1 - t, c, True, lv * (tq // levels)), carry)
        carry = lax.fori_loop(0, band * i, lambda t, c: step(band * i - 1 - t, c, False), carry)
        for e in range(hp):
            acc, tot = carry[e]
            sl = slice(e * HEAD, (e + 1) * HEAD)
            att_ref[:, sl] = acc.astype(att_ref.dtype)
            o_ref[:, sl] = (acc * _silu(bz_ref[:, sl].astype(f32))).astype(o_ref.dtype)
            tot_ref[e] = tot[:, :1]

    blk = lambda c0: pl.BlockSpec((tq, hp * HEAD), lambda g, i: (i, c0 // hp + g))
    head = lambda c0: pl.BlockSpec((l, hp * HEAD), lambda g, i: (0, c0 // hp + g))
    return pl.pallas_call(
        body, name="sb_fwd", grid=(nh // hp, l // tq),
        in_specs=[blk(qc), head(kc), head(vc), blk(zc)],
        out_specs=[blk(0), blk(0), pl.BlockSpec((hp, tq, 1), lambda g, i: (g, i, 0))],
        out_shape=[jax.ShapeDtypeStruct((l, wb), bf16), jax.ShapeDtypeStruct((l, wb), bf16),
                   jax.ShapeDtypeStruct((nh, l, 1), f32)],
        compiler_params=_cparams("parallel", "arbitrary"),
    )(proj, proj, proj, proj)


def sb_bwd(proj, dcat, att, tot, nh):
    l = proj.shape[0]
    wb = nh * HEAD
    tq = _sb_q_tile(l, 1024)
    kb = min(SB_KEYS, tq)
    band = tq // kb
    nq = l // tq
    hp = _sb_heads_per_step(nh, 2)
    levels = _sb_band_levels(band)
    scale = 1.0 / math.sqrt(HEAD)
    qc, kc, vc, zc = 3 * nh, 4 * nh, 5 * nh, 6 * nh

    def body(q_ref, k_ref, v_ref, bz_ref, do_ref, att_ref, tot_ref, dq_ref, dk_ref, dv_ref, dbz_ref, dk_acc, dv_acc,
             dob_ref):
        i = pl.program_id(1)

        @pl.when(i == 0)
        def _():
            dk_acc[...] = jnp.zeros_like(dk_acc)
            dv_acc[...] = jnp.zeros_like(dv_acc)

        bz = bz_ref[...].astype(f32)
        dov = do_ref[...].astype(f32)
        dbz_ref[...] = (dov * att_ref[...].astype(f32) * _silu_grad(bz)).astype(dbz_ref.dtype)
        dob_ref[...] = (dov * _silu(bz)).astype(bf16)
        upto = _sb_sum_matrix(lambda s, j: s <= j, kb)
        before = _sb_sum_matrix(lambda j, s: j < s, kb)
        t_pos = i * tq + lax.broadcasted_iota(jnp.int32, (tq, kb), 0)
        s_off = lax.broadcasted_iota(jnp.int32, (tq, kb), 1)

        def step(j, carry, masked, row0=0):
            rows = pl.ds(pl.multiple_of(j * kb, kb), kb)
            out = []
            for h in range(hp):
                dq, sp_seen, e_seen = (c[row0:] for c in carry[h])
                sl = slice(h * HEAD, (h + 1) * HEAD)
                q, kj, vj, dob = q_ref[row0:, sl], k_ref[rows, sl], v_ref[rows, sl], dob_ref[row0:, sl]
                z, sp = _sb_scores(q, kj, scale)
                lb = z - sp
                if masked:
                    mask = s_off[row0:] + j * kb < t_pos[row0:]
                    sp = jnp.where(mask, sp, 0.0)
                sp_upto, sp_total = _sb_sums(sp, upto)
                w = jnp.exp2(lb + _sb_wide(sp_seen, kb) + sp_upto)
                if masked:
                    w = jnp.where(mask, w, 0.0)
                dv_acc[rows, sl] += _dot_tn(w.astype(bf16), dob)
                e = _dot_nt(dob, vj) * w
                e_before, e_total = _sb_sums(e, before)
                dz = (e - (e + _sb_wide(e_seen, kb) + e_before) * jnp.exp2(lb)) * scale
                if masked:
                    dz = jnp.where(mask, dz, 0.0)
                dz = dz.astype(bf16)
                dk_acc[rows, sl] += _dot_tn(dz, q)
                new = (dq + _dot(dz, kj), sp_seen + sp_total, e_seen + e_total)
                out.append(tuple(jnp.concatenate([old[:row0], upd]) if row0 else upd for old, upd in zip(carry[h], new)))
            return tuple(out)

        zero = jnp.zeros((tq, HEAD), f32)
        init = tuple((zero, jnp.broadcast_to(tot_ref[h], (tq, HEAD)), zero) for h in range(hp))
        carry = lax.fori_loop(0, band * i, lambda j, c: step(j, c, False), init)
        for lv in range(levels):
            carry = lax.fori_loop(
                0, band // levels,
                lambda t, c, lv=lv: step(band * i + lv * (band // levels) + t, c, True, lv * (tq // levels)), carry)
        for h in range(hp):
            dq_ref[:, h * HEAD:(h + 1) * HEAD] = carry[h][0].astype(dq_ref.dtype)

        @pl.when(i == nq - 1)
        def _():
            dk_ref[...] = dk_acc[...].astype(dk_ref.dtype)
            dv_ref[...] = dv_acc[...].astype(dv_ref.dtype)

    blk = lambda c0: pl.BlockSpec((tq, hp * HEAD), lambda g, i: (i, c0 // hp + g))
    head = lambda c0: pl.BlockSpec((l, hp * HEAD), lambda g, i: (0, c0 // hp + g))
    return pl.pallas_call(
        body, name="sb_bwd", grid=(nh // hp, nq),
        in_specs=[blk(qc), head(kc), head(vc), blk(zc), blk(nh), blk(0),
                  pl.BlockSpec((hp, tq, 1), lambda g, i: (g, i, 0))],
        out_specs=[blk(0), head(0), head(0), blk(0)],
        out_shape=[jax.ShapeDtypeStruct((l, wb), bf16)] * 4,
        scratch_shapes=[pltpu.VMEM((l, hp * HEAD), f32), pltpu.VMEM((l, hp * HEAD), f32),
                        pltpu.VMEM((tq, hp * HEAD), bf16)],
        compiler_params=_cparams("parallel", "arbitrary"),
    )(proj, proj, proj, proj, dcat, att, tot)


def _disc(lr, li, ldt):
    dt = jnp.exp(ldt)
    mag = jnp.exp(lr * dt)
    a_re = mag * jnp.cos(li * dt)
    a_im = mag * jnp.sin(li * dt)
    den = lr * lr + li * li
    nr = a_re - 1.0
    return a_re, a_im, (nr * lr + a_im * li) / den, (a_im * lr - nr * li) / den


def s5_params_fwd(lr, li, ldt, bt_re, bt_im):
    g, c, p = bt_re.shape

    def body(lr_ref, li_ref, ldt_ref, br_ref, bi_ref, ar_ref, ai_ref, bbr_ref, bbi_ref):
        a_re, a_im, cr, ci = _disc(lr_ref[...], li_ref[...], ldt_ref[...])
        ar_ref[...] = a_re
        ai_ref[...] = a_im
        for k in range(c):
            br, bi = br_ref[:, k, :], bi_ref[:, k, :]
            bbr_ref[:, k, :] = cr * br - ci * bi
            bbi_ref[:, k, :] = cr * bi + ci * br

    return pl.pallas_call(
        body, name="s5_params_fwd",
        out_shape=[jax.ShapeDtypeStruct((g, p), f32)] * 2 + [jax.ShapeDtypeStruct((g, c, p), f32)] * 2,
    )(lr, li, ldt, bt_re, bt_im)


def s5_params_bwd(lr, li, ldt, bt_re, bt_im, da_re, da_im, dbbt_re, dbbt_im):
    g, c, p = bt_re.shape

    def body(lr_ref, li_ref, ldt_ref, br_ref, bi_ref, dar_ref, dai_ref, dbbr_ref, dbbi_ref,
             dlr_ref, dli_ref, dldt_ref, dbr_ref, dbi_ref):
        (a_re, a_im, cr, ci), vjp = jax.vjp(_disc, lr_ref[...], li_ref[...], ldt_ref[...])
        dcr = jnp.zeros((g, p), f32)
        dci = jnp.zeros((g, p), f32)
        for k in range(c):
            br, bi = br_ref[:, k, :], bi_ref[:, k, :]
            dr, di = dbbr_ref[:, k, :], dbbi_ref[:, k, :]
            dcr += dr * br + di * bi
            dci += di * br - dr * bi
            dbr_ref[:, k, :] = cr * dr + ci * di
            dbi_ref[:, k, :] = cr * di - ci * dr
        dlr, dli, dldt = vjp((dar_ref[...], dai_ref[...], dcr, dci))
        dlr_ref[...] = dlr
        dli_ref[...] = dli
        dldt_ref[...] = dldt

    return pl.pallas_call(
        body, name="s5_params_bwd",
        out_shape=[jax.ShapeDtypeStruct((g, p), f32)] * 2 + [jax.ShapeDtypeStruct((g, 1), f32)]
        + [jax.ShapeDtypeStruct((g, c, p), f32)] * 2,
    )(lr, li, ldt, bt_re, bt_im, da_re, da_im, dbbt_re, dbbt_im)


def _cmul(ar, ai, br, bi):
    return ar * br - ai * bi, ar * bi + ai * br


def _power_tables(ar, ai):
    rows = lax.broadcasted_iota(jnp.int32, (SUBLANES, ar.shape[1]), 0)
    pr = jnp.zeros((SUBLANES, ar.shape[1]), f32)
    pi = jnp.zeros((SUBLANES, ar.shape[1]), f32)
    cr, ci = ar, ai
    pows = {}
    for r in range(SUBLANES):
        pows[r + 1] = (cr, ci)
        pr = jnp.where(rows == r, cr, pr)
        pi = jnp.where(rows == r, ci, pi)
        cr, ci = _cmul(cr, ci, ar, ai)
    return [pows[1], pows[2], pows[4]], pr, pi


def _ssm_time_tile(l):
    return _tile(l, (512, 256, 128))


def ssm_fwd(u, bre3, bim3, cre3, cimn3, a_re, a_im, d_skip):
    l, w = u.shape
    nj = w // HEAD
    ns = STATES_PER_LANE_BLOCK
    tt = _ssm_time_tile(l)

    def body(u_ref, bre_ref, bim_ref, cre_ref, cim_ref, ar_ref, ai_ref, d_ref, y_ref, hr_ref, hi_ref, cr_ref, ci_ref):
        @pl.when(pl.program_id(1) == 0)
        def _():
            cr_ref[...] = jnp.zeros_like(cr_ref)
            ci_ref[...] = jnp.zeros_like(ci_ref)

        uv = u_ref[...]
        hr_ref[...] = _dot(uv, bre_ref[...])
        hi_ref[...] = _dot(uv, bim_ref[...])
        steps, pr, pi = _power_tables(ar_ref[...], ai_ref[...])
        rows = lax.broadcasted_iota(jnp.int32, (SUBLANES, ns), 0)

        def blk(b, carry):
            cr, ci = carry
            sl = pl.ds(pl.multiple_of(b * SUBLANES, SUBLANES), SUBLANES)
            xr, xi = hr_ref[sl, :], hi_ref[sl, :]
            for d, (sr_, si_) in zip((1, 2, 4), steps):
                keep = rows >= d
                qr = jnp.where(keep, pltpu.roll(xr, d, axis=0), 0.0)
                qi = jnp.where(keep, pltpu.roll(xi, d, axis=0), 0.0)
                mr, mi = _cmul(sr_, si_, qr, qi)
                xr, xi = xr + mr, xi + mi
            mr, mi = _cmul(pr, pi, cr, ci)
            xr, xi = xr + mr, xi + mi
            hr_ref[sl, :] = xr
            hi_ref[sl, :] = xi
            return xr[SUBLANES - 1:, :], xi[SUBLANES - 1:, :]

        cr, ci = lax.fori_loop(0, tt // SUBLANES, blk, (cr_ref[...], ci_ref[...]))
        cr_ref[...] = cr
        ci_ref[...] = ci
        y = _dot(hr_ref[...].astype(bf16), cre_ref[...]) + _dot(hi_ref[...].astype(bf16), cim_ref[...])
        y_ref[...] = y + d_ref[...] * uv.astype(f32)

    lane = pl.BlockSpec((tt, HEAD), lambda j, i: (i, j))
    st = pl.BlockSpec((tt, ns), lambda j, i: (i, j))
    b3 = pl.BlockSpec((None, HEAD, ns), lambda j, i: (j, 0, 0))
    c3 = pl.BlockSpec((None, ns, HEAD), lambda j, i: (j, 0, 0))
    arow = pl.BlockSpec((1, ns), lambda j, i: (0, j))
    return pl.pallas_call(
        body, name="ssm_fwd", grid=(nj, l // tt),
        in_specs=[lane, b3, b3, c3, c3, arow, arow, pl.BlockSpec((1, HEAD), lambda j, i: (0, j))],
        out_specs=[lane, st, st],
        out_shape=[jax.ShapeDtypeStruct((l, w), f32), jax.ShapeDtypeStruct((l, nj * ns), f32),
                   jax.ShapeDtypeStruct((l, nj * ns), f32)],
        scratch_shapes=[pltpu.VMEM((1, ns), f32), pltpu.VMEM((1, ns), f32)],
        compiler_params=_cparams("parallel", "arbitrary"),
    )(u, bre3, bim3, cre3, cimn3, a_re, a_im, d_skip)


def ssm_bwd(dy, u, h_re, h_im, bre3, bim3, cre3, cimn3, a_re, a_im, d_skip):
    l, w = u.shape
    nj = w // HEAD
    ns = STATES_PER_LANE_BLOCK
    tt = _ssm_time_tile(l)
    nt = l // tt

    def body(dy_ref, u_ref, hr_ref, hi_ref, bre_ref, bim_ref, cre_ref, cim_ref, ar_ref, ai_ref, d_ref,
             du_ref, dd_ref, dar_ref, dai_ref, dbre_ref, dbim_ref, dcre_ref, dcim_ref, kr_ref, ki_ref, cr_ref, ci_ref,
             accr_ref, acci_ref):
        i = pl.program_id(1)

        @pl.when(i == 0)
        def _():
            for ref in (cr_ref, ci_ref, accr_ref, acci_ref, dd_ref, dbre_ref, dbim_ref, dcre_ref, dcim_ref):
                ref[...] = jnp.zeros_like(ref)

        dyv = dy_ref[...]
        dyb = dyv.astype(bf16)
        uv = u_ref[...]
        kr_ref[...] = _dot_nt(dyb, cre_ref[...])
        ki_ref[...] = _dot_nt(dyb, cim_ref[...])
        steps, pr, pi = _power_tables(ar_ref[...], -ai_ref[...])
        rows = lax.broadcasted_iota(jnp.int32, (SUBLANES, ns), 0)
        qr = jnp.zeros((SUBLANES, ns), f32)
        qi = jnp.zeros((SUBLANES, ns), f32)
        for r in range(SUBLANES):
            qr = jnp.where(rows == r, pr[SUBLANES - 1 - r:SUBLANES - r, :], qr)
            qi = jnp.where(rows == r, pi[SUBLANES - 1 - r:SUBLANES - r, :], qi)
        nb = tt // SUBLANES

        def blk(t, carry):
            cr, ci, accr, acci = carry
            sl = pl.ds(pl.multiple_of((nb - 1 - t) * SUBLANES, SUBLANES), SUBLANES)
            xr, xi = kr_ref[sl, :], ki_ref[sl, :]
            for d, (sr_, si_) in zip((1, 2, 4), steps):
                keep = rows < SUBLANES - d
                zr = jnp.where(keep, pltpu.roll(xr, SUBLANES - d, axis=0), 0.0)
                zi = jnp.where(keep, pltpu.roll(xi, SUBLANES - d, axis=0), 0.0)
                mr, mi = _cmul(sr_, si_, zr, zi)
                xr, xi = xr + mr, xi + mi
            mr, mi = _cmul(qr, qi, cr, ci)
            xr, xi = xr + mr, xi + mi
            kr_ref[sl, :] = xr
            ki_ref[sl, :] = xi
            last = rows == SUBLANES - 1
            nr = jnp.where(last, cr, pltpu.roll(xr, SUBLANES - 1, axis=0))
            ni = jnp.where(last, ci, pltpu.roll(xi, SUBLANES - 1, axis=0))
            hr, hi = hr_ref[sl, :], hi_ref[sl, :]
            accr = accr + nr * hr + ni * hi
            acci = acci + ni * hr - nr * hi
            return xr[:1, :], xi[:1, :], accr, acci

        cr, ci, accr, acci = lax.fori_loop(0, nb, blk, (cr_ref[...], ci_ref[...], accr_ref[...], acci_ref[...]))
        cr_ref[...] = cr
        ci_ref[...] = ci
        accr_ref[...] = accr
        acci_ref[...] = acci
        kr, ki = kr_ref[...].astype(bf16), ki_ref[...].astype(bf16)
        du = _dot_nt(kr, bre_ref[...]) + _dot_nt(ki, bim_ref[...]) + d_ref[...] * dyv
        du_ref[...] = du.astype(du_ref.dtype)
        dd_ref[...] += jnp.sum(dyv * uv.astype(f32), axis=0, keepdims=True)
        dbre_ref[...] += _dot_tn(uv, kr)
        dbim_ref[...] += _dot_tn(uv, ki)
        dcre_ref[...] += _dot_tn(hr_ref[...].astype(bf16), dyb)
        dcim_ref[...] += _dot_tn(hi_ref[...].astype(bf16), dyb)

        @pl.when(i == nt - 1)
        def _():
            dar_ref[...] = jnp.sum(accr_ref[...], axis=0, keepdims=True)
            dai_ref[...] = jnp.sum(acci_ref[...], axis=0, keepdims=True)

    lane = pl.BlockSpec((tt, HEAD), lambda j, i: (nt - 1 - i, j))
    st = pl.BlockSpec((tt, ns), lambda j, i: (nt - 1 - i, j))
    b3 = pl.BlockSpec((None, HEAD, ns), lambda j, i: (j, 0, 0))
    c3 = pl.BlockSpec((None, ns, HEAD), lambda j, i: (j, 0, 0))
    arow = pl.BlockSpec((1, ns), lambda j, i: (0, j))
    drow = pl.BlockSpec((1, HEAD), lambda j, i: (0, j))
    return pl.pallas_call(
        body, name="ssm_bwd", grid=(nj, nt),
        in_specs=[lane, lane, st, st, b3, b3, c3, c3, arow, arow, drow],
        out_specs=[lane, drow, arow, arow, b3, b3, c3, c3],
        out_shape=[jax.ShapeDtypeStruct((l, w), bf16), jax.ShapeDtypeStruct((1, w), f32),
                   jax.ShapeDtypeStruct((1, nj * ns), f32), jax.ShapeDtypeStruct((1, nj * ns), f32),
                   jax.ShapeDtypeStruct((nj, HEAD, ns), f32), jax.ShapeDtypeStruct((nj, HEAD, ns), f32),
                   jax.ShapeDtypeStruct((nj, ns, HEAD), f32), jax.ShapeDtypeStruct((nj, ns, HEAD), f32)],
        scratch_shapes=[pltpu.VMEM((tt, ns), f32), pltpu.VMEM((tt, ns), f32), pltpu.VMEM((1, ns), f32),
                        pltpu.VMEM((1, ns), f32), pltpu.VMEM((SUBLANES, ns), f32), pltpu.VMEM((SUBLANES, ns), f32)],
        compiler_params=_cparams("parallel", "arbitrary"),
    )(dy, u, h_re, h_im, bre3, bim3, cre3, cimn3, a_re, a_im, d_skip)


def glu_fwd(y, z_src, w_glu, b_glu):
    l, w = y.shape
    tm = _row_tile(l)

    def body(y_ref, z_ref, w_ref, b_ref, g_ref, t_ref, o_ref):
        g = _gelu(y_ref[...])
        gb = g.astype(bf16)
        t = _dot(gb, w_ref[...]) + b_ref[...]
        g_ref[...] = gb
        t_ref[...] = t
        o_ref[...] = (g * jax.nn.sigmoid(t) * _silu(z_ref[...].astype(f32))).astype(o_ref.dtype)

    blk = pl.BlockSpec((tm, w), lambda i: (i, 0))
    return pl.pallas_call(
        body, name="glu_fwd", grid=(l // tm,),
        in_specs=[blk, pl.BlockSpec((tm, w), lambda i: (i, 1)), pl.BlockSpec((w, w), lambda i: (0, 0)), _row(w)],
        out_specs=[blk, blk, blk],
        out_shape=[jax.ShapeDtypeStruct((l, w), bf16), jax.ShapeDtypeStruct((l, w), f32),
                   jax.ShapeDtypeStruct((l, w), bf16)],
        compiler_params=_cparams("parallel"),
    )(y, z_src, w_glu, b_glu)


def glu_bwd(dout, y, t, z_src, w_glu):
    l, w = y.shape
    tm = _row_tile(l)

    def body(do_ref, y_ref, t_ref, z_ref, w_ref, dy_ref, dz_ref, dt_ref, db_ref):
        @pl.when(pl.program_id(0) == 0)
        def _():
            db_ref[...] = jnp.zeros_like(db_ref)

        yv, zv, dov = y_ref[...], z_ref[...].astype(f32), do_ref[...]
        g = _gelu(yv)
        sg = jax.nn.sigmoid(t_ref[...])
        dy2 = dov * _silu(zv)
        dz_ref[...] = (dov * g * sg * _silu_grad(zv)).astype(dz_ref.dtype)
        dt = dy2 * g * sg * (1.0 - sg)
        dtb = dt.astype(bf16)
        dt_ref[...] = dtb
        db_ref[...] += jnp.sum(dt, axis=0, keepdims=True)
        dg = dy2 * sg + _dot_nt(dtb, w_ref[...])
        dy_ref[...] = dg * _gelu_grad(yv)

    blk = pl.BlockSpec((tm, w), lambda i: (i, 0))
    return pl.pallas_call(
        body, name="glu_bwd", grid=(l // tm,),
        in_specs=[blk, blk, blk, pl.BlockSpec((tm, w), lambda i: (i, 1)), pl.BlockSpec((w, w), lambda i: (0, 0))],
        out_specs=[blk, blk, blk, _row(w)],
        out_shape=[jax.ShapeDtypeStruct((l, w), f32), jax.ShapeDtypeStruct((l, w), bf16),
                   jax.ShapeDtypeStruct((l, w), bf16), jax.ShapeDtypeStruct((1, w), f32)],
        compiler_params=_cparams("arbitrary"),
    )(dout, y, t, z_src, w_glu)


def _adamw(w, g, m, v):
    m = ADAM_B1 * m + (1.0 - ADAM_B1) * g
    v = ADAM_B2 * v + (1.0 - ADAM_B2) * (g * g)
    m_hat = m / (1.0 - ADAM_B1 ** ADAM_STEP)
    v_hat = v / (1.0 - ADAM_B2 ** ADAM_STEP)
    return -ADAM_LR * (m_hat / (jnp.sqrt(v_hat) + ADAM_EPS) + ADAM_WD * w), m, v


def adam_reduce(pieces, w, m, v, name):
    r, c = w.shape
    n = pieces.shape[0]
    tr = _tile(r, (256, 128, 64, 32, 16, 8))

    def body(p_ref, w_ref, m_ref, v_ref, g_ref, d_ref, nm_ref, nv_ref):
        g = p_ref[0].astype(f32)
        for s in range(1, n):
            g = g + p_ref[s].astype(f32)
        g_ref[...] = g
        d_ref[...], nm_ref[...], nv_ref[...] = _adamw(w_ref[...], g, m_ref[...], v_ref[...])

    blk = pl.BlockSpec((tr, c), lambda i: (i, 0))
    return pl.pallas_call(
        body, name=name, grid=(r // tr,),
        in_specs=[pl.BlockSpec((n, tr, c), lambda i: (0, i, 0)), blk, blk, blk],
        out_specs=[blk] * 4, out_shape=[jax.ShapeDtypeStruct((r, c), f32)] * 4,
        compiler_params=_cparams("parallel"),
    )(pieces, w, m, v)


def adam_w_mod(cond_t, dm, w, m, v):
    nl, d, cols = w.shape
    tr = _tile(d, (512, 256, 128))

    def body(c_ref, dm_ref, w_ref, m_ref, v_ref, g_ref, d_ref, nm_ref, nv_ref):
        g = jnp.dot(c_ref[...], dm_ref[...], preferred_element_type=f32, precision=lax.Precision.HIGHEST)
        g_ref[...] = g
        d_ref[...], nm_ref[...], nv_ref[...] = _adamw(w_ref[...], g, m_ref[...], v_ref[...])

    blk = pl.BlockSpec((None, tr, cols), lambda l, i: (l, i, 0))
    return pl.pallas_call(
        body, name="adam_w_mod", grid=(nl, d // tr),
        in_specs=[pl.BlockSpec((tr, N_DEV), lambda l, i: (i, 0)), pl.BlockSpec((None, N_DEV, cols), lambda l, i: (l, 0, 0)),
                  blk, blk, blk],
        out_specs=[blk] * 4, out_shape=[jax.ShapeDtypeStruct((nl, d, cols), f32)] * 4,
        compiler_params=_cparams("parallel", "parallel"),
    )(cond_t, dm, w, m, v)


def silu_rows(c_all):
    def body(c_ref, o_ref):
        o_ref[...] = _silu(c_ref[...])

    return pl.pallas_call(body, name="silu_rows", out_shape=jax.ShapeDtypeStruct(c_all.shape, f32))(c_all)


def _block_diag(x):
    g, a, b = x.shape
    nj = g // GROUPS_PER_LANE_BLOCK
    eye = jnp.eye(GROUPS_PER_LANE_BLOCK, dtype=x.dtype)
    x5 = x.reshape(nj, GROUPS_PER_LANE_BLOCK, a, b)
    return jnp.einsum("jgab,gh->jgahb", x5, eye).reshape(nj, GROUPS_PER_LANE_BLOCK * a, GROUPS_PER_LANE_BLOCK * b)


def _diag_blocks(x, a, b):
    nj = x.shape[0]
    x5 = x.reshape(nj, GROUPS_PER_LANE_BLOCK, a, GROUPS_PER_LANE_BLOCK, b)
    eye = jnp.eye(GROUPS_PER_LANE_BLOCK, dtype=x.dtype)
    return jnp.einsum("jgahb,gh->jgab", x5, eye).reshape(nj * GROUPS_PER_LANE_BLOCK, a, b)


PACK_ROW = SUBLANES * HEAD


def _pack(parts, row_multiple=SUBLANES):
    rows = []
    for p in parts:
        flat = p.reshape(-1)
        pad = (-flat.shape[0]) % PACK_ROW
        if pad:
            flat = jnp.concatenate([flat, jnp.zeros((pad,), flat.dtype)])
        rows.append(flat.reshape(-1, HEAD))
    pad = (-sum(r.shape[0] for r in rows)) % row_multiple
    if pad:
        rows.append(jnp.zeros((pad, HEAD), rows[0].dtype))
    return jnp.concatenate(rows, axis=0)


def _unpack(packed, shapes):
    out, r0 = [], 0
    for shp in shapes:
        n = math.prod(shp)
        nr = -(-n // PACK_ROW) * SUBLANES
        out.append(packed[r0:r0 + nr].reshape(-1)[:n].reshape(shp))
        r0 += nr
    return out


def adam_small(g, w, m, v):
    r, c = w.shape

    def body(g_ref, w_ref, m_ref, v_ref, d_ref, nm_ref, nv_ref):
        d_ref[...], nm_ref[...], nv_ref[...] = _adamw(w_ref[...], g_ref[...], m_ref[...], v_ref[...])

    tr = max(t for t in range(SUBLANES, 1024 + 1, SUBLANES) if r % t == 0)
    blk = pl.BlockSpec((tr, c), lambda i: (i, 0))
    return pl.pallas_call(
        body, name="adam_small", grid=(r // tr,),
        in_specs=[blk] * 4, out_specs=[blk] * 3, out_shape=[jax.ShapeDtypeStruct((r, c), f32)] * 3,
        compiler_params=_cparams("parallel"),
    )(g, w, m, v)


def kernel(x, c, ln_pre_g, ln_post_g, w_mod, b_mod, w_in_ab, w_out_ab, sgu_norm_g, sgu_w, sgu_b, w_in_ssm, w_out_ssm, lam_re, lam_im, b_re, b_im, c_re, c_im, d_skip, log_dt, w_glu, b_glu, loss_target, m_ln_pre_g, m_ln_post_g, m_w_mod, m_b_mod, m_w_in_ab, m_w_out_ab, m_sgu_norm_g, m_sgu_w, m_sgu_b, m_w_in_ssm, m_w_out_ssm, m_lam_re, m_lam_im, m_b_re, m_b_im, m_c_re, m_c_im, m_d_skip, m_log_dt, m_w_glu, m_b_glu, v_ln_pre_g, v_ln_post_g, v_w_mod, v_b_mod, v_w_in_ab, v_w_out_ab, v_sgu_norm_g, v_sgu_w, v_sgu_b, v_w_in_ssm, v_w_out_ssm, v_lam_re, v_lam_im, v_b_re, v_b_im, v_c_re, v_c_im, v_d_skip, v_log_dt, v_w_glu, v_b_glu):
    me = _my_index()
    x0 = x[0]
    l, d = x0.shape
    target = loss_target[0]
    nh = sgu_w.shape[1]
    wa = nh * HEAD
    n_grp, n_st = lam_re.shape[1], lam_re.shape[2]
    mod_cols = w_mod.shape[2]

    c_all, d_skip_all, b_glu_all = all_gather([c, d_skip, b_glu], "gather_c")
    c_all = c_all.reshape(N_DEV, d)
    d_skip_all = d_skip_all.reshape(1, -1)
    b_glu_all = b_glu_all.reshape(1, -1)

    b_cols = lax.dynamic_slice_in_dim(b_mod, me * mod_cols, mod_cols, axis=1)
    (mod_all,) = all_gather([mod_part(c_all, w_mod, b_cols)], "gather_mod")
    def after(a, first):
        return a + jnp.minimum(jnp.abs(first[(0,) * first.ndim].astype(f32)), 0.0).astype(a.dtype)

    (win_ab3,) = sequencer_exchange(GATHER, [after(w_in_ab[0], mod_all).astype(bf16)], "gather_w_in", 1)
    mod_mine = lax.dynamic_index_in_dim(mod_all, me, axis=2, keepdims=False)
    mod_rows = jnp.transpose(mod_mine, (1, 0, 2)).reshape(2, 3, 1, d)

    def rows(a, i):
        return a[i].reshape(1, d)

    shift0, scale0, gate0 = mod_rows[0, 0], mod_rows[0, 1], mod_rows[0, 2]
    h0 = prenorm_fwd(x0, rows(ln_pre_g, 0), shift0, scale0, "prenorm0")
    wout_ab3, win_ssm3, wout_ssm3, wglu = sequencer_exchange(
        GATHER, [after(w, win_ab3).astype(bf16) for w in (w_out_ab[0], w_in_ssm[0], w_out_ssm[0], w_glu[0])],
        "gather_w_rest", 2)
    proj0 = mm_nn(h0, win_ab3, bf16, "proj0")
    sgu_b3 = sgu_b[0].reshape(nh, HEAD, 1)
    out_a = sgu_fwd(proj0, sgu_norm_g, sgu_w[0], sgu_b3)
    out_b, att, tot = sb_fwd(proj0, nh)
    cat = jnp.concatenate([out_a, out_b], axis=1)
    wout_ab3 = wout_ab3.reshape(1, d, d)
    win_ssm3 = win_ssm3.reshape(1, d, d)
    wglu = wglu.reshape(w_glu.shape[2], w_glu.shape[2])
    y0 = mm_nn(cat, wout_ab3, f32, "out0")

    shift1, scale1, gate1 = mod_rows[1, 0], mod_rows[1, 1], mod_rows[1, 2]
    x1, h1 = post_prenorm_fwd(x0, y0, gate0, rows(ln_post_g, 0), rows(ln_pre_g, 1), shift1, scale1, "post0_prenorm1")
    proj1 = mm_nn(h1, win_ssm3, bf16, "proj1")
    w_ssm = proj1.shape[1] // 2
    ldt = log_dt[0].reshape(n_grp, 1)
    bt_re = jnp.transpose(b_re[0], (0, 2, 1))
    bt_im = jnp.transpose(b_im[0], (0, 2, 1))
    a_re, a_im, bbt_re, bbt_im = s5_params_fwd(lam_re[0], lam_im[0], ldt, bt_re, bt_im)
    bre3 = _block_diag(bbt_re).astype(bf16)
    bim3 = _block_diag(bbt_im).astype(bf16)
    cre3 = _block_diag(jnp.transpose(c_re[0], (0, 2, 1))).astype(bf16)
    cimn3 = _block_diag(-jnp.transpose(c_im[0], (0, 2, 1))).astype(bf16)
    a_re_row, a_im_row = a_re.reshape(1, -1), a_im.reshape(1, -1)
    u = proj1[:, :w_ssm]
    y_ssm, hs_re, hs_im = ssm_fwd(u, bre3, bim3, cre3, cimn3, a_re_row, a_im_row, d_skip_all)
    g_act, t_glu, mix1 = glu_fwd(y_ssm, proj1, wglu, b_glu_all)
    y1 = mm_nn(mix1, wout_ssm3, f32, "out1")

    dx2, loss_tile, dy1, dgate1, dgpost1 = final_loss(x1, y1, gate1, rows(ln_post_g, 1), target)

    dmix1 = mm_nt(dy1, wout_ssm3, f32, "dmix1")
    gw_out_ssm = mm_tn(mix1, dy1, N_DEV, bf16, "gw_out_ssm")
    (p_out_ssm,) = sequencer_exchange(SCATTER, [gw_out_ssm], "scatter_g1", 3)
    dy_ssm, dz1, dt_glu, db_glu = glu_bwd(dmix1, y_ssm, t_glu, proj1, wglu)
    gw_glu = mm_tn(g_act, dt_glu, 1, bf16, "gw_glu").reshape(N_DEV, -1, w_ssm)
    du, dd_skip, da_re, da_im, dbre3, dbim3, dcre3, dcimn3 = ssm_bwd(
        dy_ssm, u, hs_re, hs_im, bre3, bim3, cre3, cimn3, a_re_row, a_im_row, d_skip_all)
    dproj1 = jnp.concatenate([du, dz1], axis=1)
    gw_in_ssm = mm_tn(h1, dproj1, 1, bf16, "gw_in_ssm").reshape(N_DEV, -1, proj1.shape[1])
    p_in_ssm, p_glu = sequencer_exchange(SCATTER, [gw_in_ssm, gw_glu], "scatter_g2", 4)
    dh1 = mm_nt(dproj1, win_ssm3, f32, "dh1")
    dx1, dshift1, dscale1, dgpre1 = prenorm_bwd(dh1, x1, dx2, rows(ln_pre_g, 1), scale1, "prenorm1_bwd")
    dlr, dli, dldt, dbt_re, dbt_im = s5_params_bwd(
        lam_re[0], lam_im[0], ldt, bt_re, bt_im, da_re.reshape(n_grp, n_st), da_im.reshape(n_grp, n_st),
        _diag_blocks(dbre3, SSM_GROUP, n_st), _diag_blocks(dbim3, SSM_GROUP, n_st))
    g_b_re = jnp.transpose(dbt_re, (0, 2, 1))
    g_b_im = jnp.transpose(dbt_im, (0, 2, 1))
    g_c_re = jnp.transpose(_diag_blocks(dcre3, n_st, SSM_GROUP), (0, 2, 1))
    g_c_im = -jnp.transpose(_diag_blocks(dcimn3, n_st, SSM_GROUP), (0, 2, 1))

    dy0, dgate0, dgpost0 = post_bwd(dx1, y0, gate0, rows(ln_post_g, 0), "post0_bwd")
    dcat = mm_nt(dy0, wout_ab3, f32, "dcat")
    gw_out_ab = mm_tn(cat, dy0, 1, bf16, "gw_out_ab").reshape(N_DEV, -1, d)
    (p_out_ab,) = sequencer_exchange(SCATTER, [gw_out_ab], "scatter_g3", 5)
    da, dsgu_w, dsgu_b, dsgu_ng = sgu_bwd(proj0, dcat, sgu_norm_g, sgu_w[0], sgu_b3)
    dq, dk, dv, dbz = sb_bwd(proj0, dcat, att, tot, nh)
    dproj0 = jnp.concatenate([da, dq, dk, dv, dbz], axis=1)
    gw_in_ab = mm_tn(h0, dproj0, N_DEV, bf16, "gw_in_ab")
    (p_in_ab,) = sequencer_exchange(SCATTER, [gw_in_ab], "scatter_g4", 6)
    dh0 = mm_nt(dproj0, win_ab3, f32, "dh0")
    dx0, dshift0, dscale0, dgpre0 = prenorm_bwd(dh0, x0, dx1, rows(ln_pre_g, 0), scale0, "prenorm0_bwd")

    small_names = ["ln_pre_g", "ln_post_g", "b_mod", "sgu_norm_g", "sgu_w", "sgu_b", "lam_re", "lam_im", "b_re", "b_im",
                   "c_re", "c_im", "log_dt"]
    small_w = [ln_pre_g, ln_post_g, b_mod, sgu_norm_g, sgu_w, sgu_b, lam_re, lam_im, b_re, b_im, c_re, c_im, log_dt]
    small_m = [m_ln_pre_g, m_ln_post_g, m_b_mod, m_sgu_norm_g, m_sgu_w, m_sgu_b, m_lam_re, m_lam_im, m_b_re, m_b_im,
               m_c_re, m_c_im, m_log_dt]
    small_v = [v_ln_pre_g, v_ln_post_g, v_b_mod, v_sgu_norm_g, v_sgu_w, v_sgu_b, v_lam_re, v_lam_im, v_b_re, v_b_im,
               v_c_re, v_c_im, v_log_dt]
    dmod = jnp.concatenate([dshift0, dscale0, dgate0, dshift1, dscale1, dgate1], axis=1)
    small_g = [jnp.concatenate([dgpre0, dgpre1]), jnp.concatenate([dgpost0, dgpost1]), dmod, dsgu_ng, dsgu_w, dsgu_b,
               dlr, dli, g_b_re, g_b_im, g_c_re, g_c_im, dldt]
    shapes = [w.shape for w in small_w]
    g_sum, dmod_all = all_reduce_rows(_pack(small_g + [dd_skip, db_glu, loss_tile], SUBLANES * N_DEV), dmod,
                                      "reduce_small_grads")
    n_rows_small = sum(-(-math.prod(s) // PACK_ROW) * SUBLANES for s in shapes)
    loss = g_sum[n_rows_small + 2 * (d_skip_all.shape[1] // HEAD), 0] * (0.5 / d)
    new_small = adam_small(g_sum, _pack(small_w), _pack(small_m), _pack(small_v))
    r_small = [_unpack(o, shapes) for o in [g_sum[:n_rows_small]] + list(new_small)]
    small = {n: [r_small[k][i] for k in range(4)] for i, n in enumerate(small_names)}
    vec_rows = d_skip_all.shape[1] // HEAD

    def my_columns(r0):
        whole = g_sum[r0:r0 + vec_rows].reshape(1, 1, -1)
        return lax.dynamic_slice_in_dim(whole, me * d_skip.shape[1], d_skip.shape[1], axis=2)

    def sharded(p, w, m, v, name):
        shp = w.shape
        w2, m2, v2 = (a.reshape(-1, shp[-1]) for a in (w, m, v))
        return [o.reshape(shp) for o in adam_reduce(p.reshape(p.shape[0], -1, shp[-1]), w2, m2, v2, name)]

    r_d_skip = sharded(my_columns(n_rows_small), d_skip, m_d_skip, v_d_skip, "adam_d_skip")
    r_b_glu = sharded(my_columns(n_rows_small + vec_rows), b_glu, m_b_glu, v_b_glu, "adam_b_glu")
    r_w_out_ssm = sharded(p_out_ssm, w_out_ssm, m_w_out_ssm, v_w_out_ssm, "adam_w_out_ssm")
    r_w_in_ssm = sharded(p_in_ssm, w_in_ssm, m_w_in_ssm, v_w_in_ssm, "adam_w_in_ssm")
    r_w_glu = sharded(p_glu, w_glu, m_w_glu, v_w_glu, "adam_w_glu")
    r_w_out_ab = sharded(p_out_ab, w_out_ab, m_w_out_ab, v_w_out_ab, "adam_w_out_ab")
    r_w_in_ab = sharded(p_in_ab, w_in_ab, m_w_in_ab, v_w_in_ab, "adam_w_in_ab")

    dm_cols = jnp.transpose(
        lax.dynamic_slice_in_dim(dmod_all.reshape(N_DEV, 2, 3 * d), me * mod_cols, mod_cols, axis=2), (1, 0, 2))
    cond_t = jnp.transpose(silu_rows(c_all))
    r_w_mod = adam_w_mod(cond_t, dm_cols, w_mod, m_w_mod, v_w_mod)

    res = dict(small)
    res.update(w_mod=r_w_mod, w_in_ab=r_w_in_ab, w_out_ab=r_w_out_ab, w_in_ssm=r_w_in_ssm, w_out_ssm=r_w_out_ssm,
               d_skip=r_d_skip, w_glu=r_w_glu, b_glu=r_b_glu)
    order = ["ln_pre_g", "ln_post_g", "w_mod", "b_mod", "w_in_ab", "w_out_ab", "sgu_norm_g", "sgu_w", "sgu_b", "w_in_ssm",
             "w_out_ssm", "lam_re", "lam_im", "b_re", "b_im", "c_re", "c_im", "d_skip", "log_dt", "w_glu", "b_glu"]
    outs = [loss, dx0.reshape(x.shape)]
    for k in range(4):
        outs += [res[n][k] for n in order]
    return tuple(outs)
```

```python
import functools
import math

import jax
import jax.numpy as jnp
from jax import lax
from jax.experimental import pallas as pl
from jax.experimental.pallas import tpu as pltpu
from jax.experimental.pallas import tpu_sc as plsc

f32 = jnp.float32
bf16 = jnp.bfloat16

N_DEV = 8
EPS = 1e-6
HEAD = 128
SUBLANES = 8
SSM_GROUP = 16
SSM_STATE = 64
GROUPS_PER_LANE_BLOCK = HEAD // SSM_GROUP
STATES_PER_LANE_BLOCK = GROUPS_PER_LANE_BLOCK * SSM_STATE
VMEM_LIMIT = 56 * 2 ** 20
ADAM_LR, ADAM_B1, ADAM_B2, ADAM_EPS, ADAM_WD, ADAM_STEP = 0.001, 0.9, 0.999, 1e-08, 0.01, 10
_GELU_C0 = math.sqrt(2.0 / math.pi)
_GELU_C1 = 0.044715
MESH = pl.DeviceIdType.MESH


def _cparams(*sem):
    return pltpu.CompilerParams(dimension_semantics=sem if sem else None, vmem_limit_bytes=VMEM_LIMIT)


def _gelu(x):
    return 0.5 * x * (1.0 + jnp.tanh(_GELU_C0 * (x + _GELU_C1 * x * x * x)))


def _gelu_grad(x):
    t = jnp.tanh(_GELU_C0 * (x + _GELU_C1 * x * x * x))
    return 0.5 * (1.0 + t) + 0.5 * x * (1.0 - t * t) * _GELU_C0 * (1.0 + 3.0 * _GELU_C1 * x * x)


def _silu(x):
    return x * jax.nn.sigmoid(x)


def _silu_grad(x):
    s = jax.nn.sigmoid(x)
    return s * (1.0 + x * (1.0 - s))


def _dot(a, b):
    return jnp.dot(a, b, preferred_element_type=f32)


def _dot_nt(a, b):
    return lax.dot_general(a, b, (((1,), (1,)), ((), ())), preferred_element_type=f32)


def _dot_tn(a, b):
    return lax.dot_general(a, b, (((0,), (0,)), ((), ())), preferred_element_type=f32)


def _split_bf16(v):
    hi = v.astype(bf16)
    lo = (v - hi.astype(f32)).astype(bf16)
    return hi, lo


def _row(d):
    return pl.BlockSpec((1, d), lambda *_: (0, 0))


def _my_index():
    return 4 * lax.axis_index("x") + 2 * lax.axis_index("y") + lax.axis_index("c")


def _peer(k):
    x, y, c = lax.axis_index("x"), lax.axis_index("y"), lax.axis_index("c")
    return (1 - x if k & 4 else x, 1 - y if k & 2 else y, 1 - c if k & 1 else c)


def all_gather(arrs, name):
    n = len(arrs)

    def body(*refs):
        ins, outs = refs[:n], refs[n:2 * n]
        send, recv, local = refs[2 * n:]
        me = _my_index()
        copies = []
        for a in range(n):
            cp = pltpu.make_async_copy(ins[a], outs[a].at[me], local.at[a])
            cp.start()
            copies.append(cp)
            for k in range(1, N_DEV):
                s = a * (N_DEV - 1) + k - 1
                cp = pltpu.make_async_remote_copy(src_ref=ins[a], dst_ref=outs[a].at[me], send_sem=send.at[s],
                                                  recv_sem=recv.at[s], device_id=_peer(k), device_id_type=MESH)
                cp.start()
                copies.append(cp)
        for cp in copies:
            cp.wait()

    any_spec = pl.BlockSpec(memory_space=pl.ANY)
    outs = pl.pallas_call(
        body, name=name,
        out_shape=[jax.ShapeDtypeStruct((N_DEV,) + a.shape, a.dtype) for a in arrs],
        in_specs=[any_spec] * n, out_specs=[any_spec] * n,
        scratch_shapes=[pltpu.SemaphoreType.DMA((n * (N_DEV - 1),)), pltpu.SemaphoreType.DMA((n * (N_DEV - 1),)),
                        pltpu.SemaphoreType.DMA((n,))],
        compiler_params=pltpu.CompilerParams(has_side_effects=True),
    )(*arrs)
    return list(outs)


def all_reduce_rows(pack, extra, name):
    r, c = pack.shape
    rs = r // N_DEV
    n_peer = N_DEV - 1

    def body(p_ref, x_ref, o_ref, xo_ref, land, red, send1, recv1, send2, recv2, sendx, recvx, local):
        me = _my_index()

        def rows(i):
            return pl.ds(pl.multiple_of(i * rs, SUBLANES), rs)

        own = [pltpu.make_async_copy(p_ref.at[rows(me)], land.at[me], local.at[0]),
               pltpu.make_async_copy(x_ref, xo_ref.at[me], local.at[1])]
        first = []
        for k in range(1, N_DEV):
            first.append(pltpu.make_async_remote_copy(
                src_ref=p_ref.at[rows(jnp.bitwise_xor(me, k))], dst_ref=land.at[me], send_sem=send1.at[k - 1],
                recv_sem=recv1.at[k - 1], device_id=_peer(k), device_id_type=MESH))
            first.append(pltpu.make_async_remote_copy(
                src_ref=x_ref, dst_ref=xo_ref.at[me], send_sem=sendx.at[k - 1], recv_sem=recvx.at[k - 1],
                device_id=_peer(k), device_id_type=MESH))
        for cp in own + first:
            cp.start()
        for cp in own + first:
            cp.wait()
        acc = land[0]
        for s in range(1, N_DEV):
            acc = acc + land[s]
        red[...] = acc
        mine = pltpu.make_async_copy(red, o_ref.at[rows(me)], local.at[2])
        second = [pltpu.make_async_remote_copy(
            src_ref=red, dst_ref=o_ref.at[rows(me)], send_sem=send2.at[k - 1], recv_sem=recv2.at[k - 1],
            device_id=_peer(k), device_id_type=MESH) for k in range(1, N_DEV)]
        for cp in [mine] + second:
            cp.start()
        for cp in [mine] + second:
            cp.wait()

    any_spec = pl.BlockSpec(memory_space=pl.ANY)
    return pl.pallas_call(
        body, name=name,
        out_shape=[jax.ShapeDtypeStruct((r, c), pack.dtype), jax.ShapeDtypeStruct((N_DEV,) + extra.shape, extra.dtype)],
        in_specs=[any_spec, any_spec], out_specs=[any_spec, any_spec],
        scratch_shapes=[pltpu.VMEM((N_DEV, rs, c), pack.dtype), pltpu.VMEM((rs, c), pack.dtype)]
        + [pltpu.SemaphoreType.DMA((n_peer,))] * 6 + [pltpu.SemaphoreType.DMA((3,))],
        compiler_params=pltpu.CompilerParams(has_side_effects=True),
    )(pack, extra)


GATHER, SCATTER = "gather", "scatter"


def _exchange_copies(srcs, lands, send, recv):
    me = _my_index()
    copies = []
    for a, (src, land) in enumerate(zip(srcs, lands)):
        for k in range(1, N_DEV):
            s = a * (N_DEV - 1) + k - 1
            copies.append(pltpu.make_async_remote_copy(
                src_ref=src.at[jnp.bitwise_xor(me, k)], dst_ref=land.at[me],
                send_sem=send.at[s], recv_sem=recv.at[s], device_id=_peer(k), device_id_type=MESH))
    return copies


def sequencer_exchange(kind, arrs, name, collective_id):
    n = len(arrs)
    n_sem = n * (N_DEV - 1)
    land_shapes = [((N_DEV,) + a.shape if kind == GATHER else a.shape) for a in arrs]
    srcs = [jax.new_ref(a, memory_space=pltpu.MemorySpace.HBM) for a in arrs]
    lands = [jax.empty_ref(jax.ShapeDtypeStruct(s, a.dtype), memory_space=pltpu.MemorySpace.HBM)
             for s, a in zip(land_shapes, arrs)]

    @pl.kernel(mesh=plsc.ScalarSubcoreMesh(axis_name="sequencer", num_cores=1), name=name,
               scratch_types=(pltpu.SemaphoreType.DMA((n_sem,)), pltpu.SemaphoreType.DMA((n_sem,)),
                              pltpu.SemaphoreType.DMA((n,))),
               compiler_params=pltpu.CompilerParams(collective_id=collective_id))
    def launch(send, recv, local):
        barrier = pltpu.get_barrier_semaphore()
        for k in range(1, N_DEV):
            pl.semaphore_signal(barrier, inc=1, device_id=_peer(k), device_id_type=MESH)
        pl.semaphore_wait(barrier, N_DEV - 1)
        me = _my_index()
        mine = [pltpu.make_async_copy(src if kind == GATHER else src.at[me], land.at[me], local.at[a])
                for a, (src, land) in enumerate(zip(srcs, lands))]
        if kind == SCATTER:
            copies = mine + _exchange_copies(srcs, lands, send, recv)
            for cp in copies:
                cp.start()
            for cp in copies:
                cp.wait()
            return

        def block_copy(a, slot, block, k, src=None):
            s = a * (N_DEV - 1) + slot
            return pltpu.make_async_remote_copy(
                src_ref=lands[a].at[block] if src is None else src, dst_ref=lands[a].at[block],
                send_sem=send.at[s], recv_sem=recv.at[s], device_id=_peer(k), device_id_type=MESH)

        chips = (2, 4, 6)
        sibling = jnp.bitwise_xor(me, 1)
        first = [block_copy(a, slot, me, k, src=srcs[a]) for a in range(n) for slot, k in enumerate((1,) + chips)]
        for cp in mine + first:
            cp.start()
        passed = []
        for a in range(n):
            for i, k in enumerate(chips):
                block = jnp.bitwise_xor(me, k)
                block_copy(a, 1 + i, block, k).wait_recv()
                passed.append(block_copy(a, 4 + i, block, 1))
                passed[-1].start()
        for a in range(n):
            block_copy(a, 0, sibling, 1).wait_recv()
            for i, k in enumerate(chips):
                block_copy(a, 4 + i, jnp.bitwise_xor(sibling, k), 1).wait_recv()
        for cp in mine:
            cp.wait()
        for cp in first + passed:
            cp.wait_send()

    launch()
    return [land[...] for land in lands]


def _tile(n, pref):
    for t in pref:
        if n % t == 0:
            return t
    return n


MM_WIDE = 1024
MM_WEIGHT_BLOCK = 8 * 2 ** 20


def _blocks_per_step(nb, fits):
    return max(g for g in range(1, nb + 1) if nb % g == 0 and fits(g))


def mm_nn(a, b3, out_dtype, name, split_cols=None):
    m, k = a.shape
    nb, _, bn = b3.shape
    tm = _tile(m, (512, 256, 128))
    tn = bn // split_cols if split_cols else _tile(bn, (1024, 896, 512, 256, 128))
    per = bn // tn
    gb = _blocks_per_step(nb, lambda g: g == 1 or (per == 1 and g * bn <= MM_WIDE))

    def body(a_ref, b_ref, o_ref):
        for g in range(gb):
            o_ref[:, g * tn:(g + 1) * tn] = _dot(a_ref[...], b_ref[g]).astype(o_ref.dtype)

    if split_cols:
        out_spec = pl.BlockSpec((None, tm, tn), lambda i, j, jj: (jj, i, 0))
        out_shape = jax.ShapeDtypeStruct((split_cols, m, tn), out_dtype)
    else:
        out_spec = pl.BlockSpec((tm, gb * tn), lambda i, j, jj: (i, j * per + jj))
        out_shape = jax.ShapeDtypeStruct((m, nb * bn), out_dtype)
    return pl.pallas_call(
        body, name=name, grid=(m // tm, nb // gb, per),
        in_specs=[pl.BlockSpec((tm, k), lambda i, j, jj: (i, 0)),
                  pl.BlockSpec((gb, k, tn), lambda i, j, jj: (j, 0, jj))],
        out_specs=out_spec, out_shape=out_shape,
        compiler_params=_cparams("parallel", "arbitrary", "arbitrary"),
    )(a, b3)


def mm_nt(a, w3, out_dtype, name):
    m, _ = a.shape
    nb, ko, bn = w3.shape
    tm = _tile(m, (512, 256, 128))
    tko = _tile(ko, (1024, 512, 256, 128))
    gb = _blocks_per_step(nb, lambda g: g * tko * bn * w3.dtype.itemsize <= MM_WEIGHT_BLOCK)
    ns = nb // gb

    def body(a_ref, w_ref, o_ref, acc_ref):
        j = pl.program_id(2)

        @pl.when(j == 0)
        def _():
            acc_ref[...] = jnp.zeros_like(acc_ref)

        part = _dot_nt(a_ref[:, :bn], w_ref[0])
        for g in range(1, gb):
            part += _dot_nt(a_ref[:, g * bn:(g + 1) * bn], w_ref[g])
        acc_ref[...] += part

        @pl.when(j == ns - 1)
        def _():
            o_ref[...] = acc_ref[...].astype(o_ref.dtype)

    return pl.pallas_call(
        body, name=name, grid=(m // tm, ko // tko, ns),
        in_specs=[pl.BlockSpec((tm, gb * bn), lambda i, o, j: (i, j)),
                  pl.BlockSpec((gb, tko, bn), lambda i, o, j: (j, o, 0))],
        out_specs=pl.BlockSpec((tm, tko), lambda i, o, j: (i, o)),
        out_shape=jax.ShapeDtypeStruct((m, ko), out_dtype),
        scratch_shapes=[pltpu.VMEM((tm, tko), f32)],
        compiler_params=_cparams("parallel", "arbitrary", "arbitrary"),
    )(a, w3)


def mm_tn(a, dy, ncb, out_dtype, name):
    l, ka = a.shape
    _, n = dy.shape
    bn = n // ncb
    tl = _tile(l, (1024, 512, 256, 128))
    tka = _tile(ka, (512, 256, 128))
    tn = _tile(bn, (1024, 896, 512, 256, 128))
    per = bn // tn
    gb = _blocks_per_step(ncb, lambda g: g == 1 or (per == 1 and g * bn <= MM_WIDE))
    nl = l // tl

    def body(a_ref, dy_ref, o_ref, acc_ref):
        s = pl.program_id(2)

        @pl.when(s == 0)
        def _():
            acc_ref[...] = jnp.zeros_like(acc_ref)

        acc_ref[...] += _dot_tn(a_ref[...], dy_ref[...])

        @pl.when(s == nl - 1)
        def _():
            for g in range(gb):
                o_ref[g] = acc_ref[:, g * tn:(g + 1) * tn].astype(o_ref.dtype)

    return pl.pallas_call(
        body, name=name, grid=(ka // tka, n // (gb * tn), nl),
        in_specs=[pl.BlockSpec((tl, tka), lambda i, j, s: (s, i)),
                  pl.BlockSpec((tl, gb * tn), lambda i, j, s: (s, j))],
        out_specs=pl.BlockSpec((gb, tka, tn), lambda i, j, s: (j // per, i, j % per)),
        out_shape=jax.ShapeDtypeStruct((ncb, ka, bn), out_dtype),
        scratch_shapes=[pltpu.VMEM((tka, gb * tn), f32)],
        compiler_params=_cparams("parallel", "parallel", "arbitrary"),
    )(a, dy)


def mod_part(c_all, w_mod, b_cols):
    nl, d, cols = w_mod.shape

    def body(c_ref, w_ref, b_ref, o_ref):
        cond = _silu(c_ref[...]).astype(bf16)
        o_ref[...] = _dot(cond, w_ref[...].astype(bf16)) + b_ref[...]

    return pl.pallas_call(
        body, name="mod_part", grid=(nl,),
        in_specs=[pl.BlockSpec((N_DEV, d), lambda l: (0, 0)),
                  pl.BlockSpec((None, d, cols), lambda l: (l, 0, 0)),
                  pl.BlockSpec((None, 1, cols), lambda l: (l, 0, 0))],
        out_specs=pl.BlockSpec((None, N_DEV, cols), lambda l: (l, 0, 0)),
        out_shape=jax.ShapeDtypeStruct((nl, N_DEV, cols), f32),
        compiler_params=_cparams("arbitrary"),
    )(c_all, w_mod, b_cols.reshape(nl, 1, cols))


def _row_tile(l):
    return _tile(l, (256, 128))


def prenorm_fwd(x, g, shift, scale, name):
    l, d = x.shape
    tm = _row_tile(l)

    def body(x_ref, g_ref, sh_ref, sc_ref, h_ref):
        xv = x_ref[...]
        r = lax.rsqrt(jnp.mean(xv * xv, axis=-1, keepdims=True) + EPS)
        h_ref[...] = (xv * r * (g_ref[...] * (1.0 + sc_ref[...])) + sh_ref[...]).astype(h_ref.dtype)

    return pl.pallas_call(
        body, name=name, grid=(l // tm,),
        in_specs=[pl.BlockSpec((tm, d), lambda i: (i, 0)), _row(d), _row(d), _row(d)],
        out_specs=pl.BlockSpec((tm, d), lambda i: (i, 0)),
        out_shape=jax.ShapeDtypeStruct((l, d), bf16),
        compiler_params=_cparams("parallel"),
    )(x, g, shift, scale)


def post_prenorm_fwd(x, y, gate, g_post, g_pre, shift, scale, name):
    l, d = x.shape
    tm = _row_tile(l)

    def body(x_ref, y_ref, gate_ref, gp_ref, g_ref, sh_ref, sc_ref, o_ref, h_ref):
        yv = y_ref[...]
        r = lax.rsqrt(jnp.mean(yv * yv, axis=-1, keepdims=True) + EPS)
        xv = x_ref[...] + gate_ref[...] * (yv * r * gp_ref[...])
        o_ref[...] = xv
        r = lax.rsqrt(jnp.mean(xv * xv, axis=-1, keepdims=True) + EPS)
        h_ref[...] = (xv * r * (g_ref[...] * (1.0 + sc_ref[...])) + sh_ref[...]).astype(h_ref.dtype)

    blk = pl.BlockSpec((tm, d), lambda i: (i, 0))
    return pl.pallas_call(
        body, name=name, grid=(l // tm,),
        in_specs=[blk, blk] + [_row(d)] * 5, out_specs=[blk, blk],
        out_shape=[jax.ShapeDtypeStruct((l, d), f32), jax.ShapeDtypeStruct((l, d), bf16)],
        compiler_params=_cparams("parallel"),
    )(x, y, gate, g_post, g_pre, shift, scale)


def _post_bwd_rows(dxv, yv, r, gate, gv, dy_ref, dgate_ref, dg_ref):
    yn = yv * r
    dgate_ref[...] += jnp.sum(dxv * yn * gv, axis=0, keepdims=True)
    dyg = dxv * gate
    dg_ref[...] += jnp.sum(dyg * yn, axis=0, keepdims=True)
    dyn = dyg * gv
    dy_ref[...] = (r * (dyn - yn * jnp.mean(dyn * yn, axis=-1, keepdims=True))).astype(dy_ref.dtype)


def final_loss(x, y, gate, g, target):
    l, d = x.shape
    tm = _row_tile(l)

    def body(x_ref, y_ref, gate_ref, g_ref, t_ref, dx_ref, loss_ref, dy_ref, dgate_ref, dg_ref):
        @pl.when(pl.program_id(0) == 0)
        def _():
            loss_ref[...] = jnp.zeros_like(loss_ref)
            dgate_ref[...] = jnp.zeros_like(dgate_ref)
            dg_ref[...] = jnp.zeros_like(dg_ref)

        yv, gate, gv = y_ref[...], gate_ref[...], g_ref[...]
        r = lax.rsqrt(jnp.mean(yv * yv, axis=-1, keepdims=True) + EPS)
        diff = x_ref[...] + gate * (yv * r * gv) - t_ref[...]
        dxv = diff * (1.0 / d)
        dx_ref[...] = dxv
        loss_ref[...] += jnp.sum(diff * diff)
        _post_bwd_rows(dxv, yv, r, gate, gv, dy_ref, dgate_ref, dg_ref)

    blk = pl.BlockSpec((tm, d), lambda i: (i, 0))
    return pl.pallas_call(
        body, name="final_loss", grid=(l // tm,),
        in_specs=[blk, blk, _row(d), _row(d), blk],
        out_specs=[blk, pl.BlockSpec((SUBLANES, HEAD), lambda i: (0, 0)), blk, _row(d), _row(d)],
        out_shape=[jax.ShapeDtypeStruct((l, d), f32), jax.ShapeDtypeStruct((SUBLANES, HEAD), f32),
                   jax.ShapeDtypeStruct((l, d), bf16), jax.ShapeDtypeStruct((1, d), f32), jax.ShapeDtypeStruct((1, d), f32)],
        compiler_params=_cparams("arbitrary"),
    )(x, y, gate, g, target)


def post_bwd(dx, y, gate, g, name):
    l, d = dx.shape
    tm = _row_tile(l)

    def body(dx_ref, y_ref, gate_ref, g_ref, dy_ref, dgate_ref, dg_ref):
        @pl.when(pl.program_id(0) == 0)
        def _():
            dgate_ref[...] = jnp.zeros_like(dgate_ref)
            dg_ref[...] = jnp.zeros_like(dg_ref)

        yv = y_ref[...]
        r = lax.rsqrt(jnp.mean(yv * yv, axis=-1, keepdims=True) + EPS)
        _post_bwd_rows(dx_ref[...], yv, r, gate_ref[...], g_ref[...], dy_ref, dgate_ref, dg_ref)

    blk = pl.BlockSpec((tm, d), lambda i: (i, 0))
    return pl.pallas_call(
        body, name=name, grid=(l // tm,),
        in_specs=[blk, blk, _row(d), _row(d)], out_specs=[blk, _row(d), _row(d)],
        out_shape=[jax.ShapeDtypeStruct((l, d), bf16), jax.ShapeDtypeStruct((1, d), f32),
                   jax.ShapeDtypeStruct((1, d), f32)],
        compiler_params=_cparams("arbitrary"),
    )(dx, y, gate, g)


def prenorm_bwd(dh, x, dx_next, g, scale, name):
    l, d = x.shape
    tm = _row_tile(l)

    def body(dh_ref, x_ref, dxn_ref, g_ref, sc_ref, dx_ref, dsh_ref, dsc_ref, dg_ref):
        @pl.when(pl.program_id(0) == 0)
        def _():
            dsh_ref[...] = jnp.zeros_like(dsh_ref)
            dsc_ref[...] = jnp.zeros_like(dsc_ref)
            dg_ref[...] = jnp.zeros_like(dg_ref)

        xv, dhv, gv, sc1 = x_ref[...], dh_ref[...], g_ref[...], 1.0 + sc_ref[...]
        r = lax.rsqrt(jnp.mean(xv * xv, axis=-1, keepdims=True) + EPS)
        xn = xv * r
        dhx = dhv * xn
        dsh_ref[...] += jnp.sum(dhv, axis=0, keepdims=True)
        dsc_ref[...] += jnp.sum(dhx * gv, axis=0, keepdims=True)
        dg_ref[...] += jnp.sum(dhx * sc1, axis=0, keepdims=True)
        dxn = dhv * (gv * sc1)
        dx_ref[...] = dxn_ref[...] + r * (dxn - xn * jnp.mean(dxn * xn, axis=-1, keepdims=True))

    blk = pl.BlockSpec((tm, d), lambda i: (i, 0))
    return pl.pallas_call(
        body, name=name, grid=(l // tm,),
        in_specs=[blk, blk, blk, _row(d), _row(d)], out_specs=[blk, _row(d), _row(d), _row(d)],
        out_shape=[jax.ShapeDtypeStruct((l, d), f32)] + [jax.ShapeDtypeStruct((1, d), f32)] * 3,
        compiler_params=_cparams("arbitrary"),
    )(dh, x, dx_next, g, scale)


def _tril_mask():
    r = lax.broadcasted_iota(jnp.int32, (HEAD, HEAD), 0)
    c = lax.broadcasted_iota(jnp.int32, (HEAD, HEAD), 1)
    return r >= c


def sgu_fwd(proj, norm_g, w_s, b_s):
    l = proj.shape[0]
    nh = w_s.shape[0]
    wa = nh * HEAD

    def body(au_ref, av_ref, az_ref, ng_ref, w_ref, b_ref, o_ref):
        tril = _tril_mask()
        for h in range(nh):
            sl = slice(h * HEAD, (h + 1) * HEAD)
            gv = _gelu(av_ref[:, sl].astype(f32))
            r = lax.rsqrt(jnp.mean(gv * gv, axis=-1, keepdims=True) + EPS)
            vh = gv * r * ng_ref[:, sl]
            wm = jnp.where(tril, w_ref[h], 0.0).astype(bf16)
            s = _dot(wm, vh.astype(bf16)) + b_ref[h]
            o_ref[:, sl] = (_gelu(au_ref[:, sl].astype(f32)) * s * _silu(az_ref[:, sl].astype(f32))).astype(o_ref.dtype)

    def col(j):
        return pl.BlockSpec((HEAD, wa), lambda n: (n, j))

    return pl.pallas_call(
        body, name="sgu_fwd", grid=(l // HEAD,),
        in_specs=[col(0), col(1), col(2), _row(wa),
                  pl.BlockSpec((nh, HEAD, HEAD), lambda n: (0, 0, 0)), pl.BlockSpec((nh, HEAD, 1), lambda n: (0, 0, 0))],
        out_specs=pl.BlockSpec((HEAD, wa), lambda n: (n, 0)),
        out_shape=jax.ShapeDtypeStruct((l, wa), bf16),
        compiler_params=_cparams("parallel"),
    )(proj, proj, proj, norm_g, w_s, b_s)


def sgu_bwd(proj, dcat, norm_g, w_s, b_s):
    l = proj.shape[0]
    nh = w_s.shape[0]
    wa = nh * HEAD

    def body(au_ref, av_ref, az_ref, do_ref, ng_ref, w_ref, b_ref, da_ref, dw_ref, db_ref, dng_ref):
        @pl.when(pl.program_id(0) == 0)
        def _():
            dw_ref[...] = jnp.zeros_like(dw_ref)
            db_ref[...] = jnp.zeros_like(db_ref)
            dng_ref[...] = jnp.zeros_like(dng_ref)

        tril = _tril_mask()
        for h in range(nh):
            sl = slice(h * HEAD, (h + 1) * HEAD)
            au, av, az = au_ref[:, sl].astype(f32), av_ref[:, sl].astype(f32), az_ref[:, sl].astype(f32)
            ng = ng_ref[:, sl]
            gv = _gelu(av)
            r = lax.rsqrt(jnp.mean(gv * gv, axis=-1, keepdims=True) + EPS)
            gvn = gv * r
            vh = (gvn * ng).astype(bf16)
            wm = jnp.where(tril, w_ref[h], 0.0).astype(bf16)
            s = _dot(wm, vh) + b_ref[h]
            gu, sz = _gelu(au), _silu(az)
            dov = do_ref[:, sl].astype(f32)
            da_ref[:, sl] = (dov * s * sz * _gelu_grad(au)).astype(da_ref.dtype)
            da_ref[:, 2 * wa + h * HEAD:2 * wa + (h + 1) * HEAD] = (dov * gu * s * _silu_grad(az)).astype(da_ref.dtype)
            ds = dov * gu * sz
            db_ref[h] += jnp.sum(ds, axis=-1, keepdims=True)
            dsb = ds.astype(bf16)
            dw_ref[h] += jnp.where(tril, _dot_nt(dsb, vh), 0.0)
            dvh = _dot_tn(wm, dsb)
            dng_ref[:, sl] += jnp.sum(dvh * gvn, axis=0, keepdims=True)
            dgvn = dvh * ng
            dgv = r * (dgvn - gvn * jnp.mean(dgvn * gvn, axis=-1, keepdims=True))
            da_ref[:, wa + h * HEAD:wa + (h + 1) * HEAD] = (dgv * _gelu_grad(av)).astype(da_ref.dtype)

    def col(j):
        return pl.BlockSpec((HEAD, wa), lambda n: (n, j))

    whole_w = pl.BlockSpec((nh, HEAD, HEAD), lambda n: (0, 0, 0))
    whole_b = pl.BlockSpec((nh, HEAD, 1), lambda n: (0, 0, 0))
    return pl.pallas_call(
        body, name="sgu_bwd", grid=(l // HEAD,),
        in_specs=[col(0), col(1), col(2), col(0), _row(wa), whole_w, whole_b],
        out_specs=[pl.BlockSpec((HEAD, 3 * wa), lambda n: (n, 0)), whole_w, whole_b, _row(wa)],
        out_shape=[jax.ShapeDtypeStruct((l, 3 * wa), bf16), jax.ShapeDtypeStruct((nh, HEAD, HEAD), f32),
                   jax.ShapeDtypeStruct((nh, HEAD, 1), f32), jax.ShapeDtypeStruct((1, wa), f32)],
        compiler_params=_cparams("arbitrary"),
    )(proj, proj, proj, dcat, norm_g, w_s, b_s)


_LOG2E = 1.0 / math.log(2.0)


def _sb_scores(q, k, scale):
    z = _dot_nt(q, k) * (scale * _LOG2E)
    return z, jnp.maximum(z, 0.0) + jnp.log2(1.0 + jnp.exp2(-jnp.abs(z)))


SB_KEYS = 256


def _sb_sum_matrix(tri, kb):
    s = lax.broadcasted_iota(jnp.int32, (2 * kb, kb + HEAD), 0) % kb
    j = lax.broadcasted_iota(jnp.int32, (2 * kb, kb + HEAD), 1)
    return jnp.where(jnp.logical_or(j >= kb, tri(s, j)), 1.0, 0.0).astype(bf16)


def _sb_sums(x, sums):
    kb = x.shape[1]
    c2 = _dot(jnp.concatenate(_split_bf16(x), axis=1), sums)
    return c2[:, :kb], c2[:, kb:]


def _sb_wide(v, kb):
    return jnp.concatenate([v] * (kb // HEAD), axis=1) if kb > HEAD else v


def _sb_q_tile(l, most=512):
    return _tile(l, tuple(t for t in (1024, 512, 256, 128) if t <= most))


def _sb_band_levels(band):
    return _tile(band, (4, 2, 1))


def _sb_heads_per_step(nh, most):
    return _tile(nh, tuple(h for h in (4, 2) if h <= most))


def sb_fwd(proj, nh):
    l = proj.shape[0]
    wb = nh * HEAD
    tq = _sb_q_tile(l, 1024)
    kb = min(SB_KEYS, tq)
    band = tq // kb
    hp = _sb_heads_per_step(nh, 2)
    levels = _sb_band_levels(band)
    scale = 1.0 / math.sqrt(HEAD)
    qc, kc, vc, zc = 3 * nh, 4 * nh, 5 * nh, 6 * nh

    def body(q_ref, k_ref, v_ref, bz_ref, o_ref, att_ref, tot_ref):
        i = pl.program_id(1)
        sums = _sb_sum_matrix(lambda s, j: s > j, kb)
        t_pos = i * tq + lax.broadcasted_iota(jnp.int32, (tq, kb), 0)
        s_off = lax.broadcasted_iota(jnp.int32, (tq, kb), 1)

        def step(j, carry, masked, row0=0):
            rows = pl.ds(pl.multiple_of(j * kb, kb), kb)
            out = []
            for e in range(hp):
                acc, tot = carry[e]
                sl = slice(e * HEAD, (e + 1) * HEAD)
                z, sp = _sb_scores(q_ref[row0:, sl], k_ref[rows, sl], scale)
                lb = z - sp
                if masked:
                    mask = s_off[row0:] + j * kb < t_pos[row0:]
                    sp = jnp.where(mask, sp, 0.0)
                later, total = _sb_sums(sp, sums)
                w = jnp.exp2(lb + _sb_wide(tot[row0:], kb) - later)
                if masked:
                    w = jnp.where(mask, w, 0.0)
                new = (acc[row0:] + _dot(w.astype(bf16), v_ref[rows, sl]), tot[row0:] - total)
                out.append(tuple(jnp.concatenate([old[:row0], upd]) if row0 else upd for old, upd in zip(carry[e], new)))
            return tuple(out)

        zero = jnp.zeros((tq, HEAD), f32)
        carry = ((zero, zero),) * hp
        for lv in reversed(range(levels)):
            carry = lax.fori_loop(
                0, band // levels,
                lambda t, c, lv=lv: step(band * i + (lv + 1) * (band // levels) - 1 - t, c, True, lv * (tq // levels)), carry)
        carry = lax.fori_loop(0, band * i, lambda t, c: step(band * i - 1 - t, c, False), carry)
        for e in range(hp):
            acc, tot = carry[e]
            sl = slice(e * HEAD, (e + 1) * HEAD)
            att_ref[:, sl] = acc.astype(att_ref.dtype)
            o_ref[:, sl] = (acc * _silu(bz_ref[:, sl].astype(f32))).astype(o_ref.dtype)
            tot_ref[e] = tot[:, :1]

    blk = lambda c0: pl.BlockSpec((tq, hp * HEAD), lambda g, i: (i, c0 // hp + g))
    head = lambda c0: pl.BlockSpec((l, hp * HEAD), lambda g, i: (0, c0 // hp + g))
    return pl.pallas_call(
        body, name="sb_fwd", grid=(nh // hp, l // tq),
        in_specs=[blk(qc), head(kc), head(vc), blk(zc)],
        out_specs=[blk(0), blk(0), pl.BlockSpec((hp, tq, 1), lambda g, i: (g, i, 0))],
        out_shape=[jax.ShapeDtypeStruct((l, wb), bf16), jax.ShapeDtypeStruct((l, wb), bf16),
                   jax.ShapeDtypeStruct((nh, l, 1), f32)],
        compiler_params=_cparams("parallel", "arbitrary"),
    )(proj, proj, proj, proj)


def sb_bwd(proj, dcat, att, tot, nh):
    l = proj.shape[0]
    wb = nh * HEAD
    tq = _sb_q_tile(l, 1024)
    kb = min(SB_KEYS, tq)
    band = tq // kb
    nq = l // tq
    hp = _sb_heads_per_step(nh, 2)
    levels = _sb_band_levels(band)
    scale = 1.0 / math.sqrt(HEAD)
    qc, kc, vc, zc = 3 * nh, 4 * nh, 5 * nh, 6 * nh

    def body(q_ref, k_ref, v_ref, bz_ref, do_ref, att_ref, tot_ref, dq_ref, dk_ref, dv_ref, dbz_ref, dk_acc, dv_acc,
             dob_ref):
        i = pl.program_id(1)

        @pl.when(i == 0)
        def _():
            dk_acc[...] = jnp.zeros_like(dk_acc)
            dv_acc[...] = jnp.zeros_like(dv_acc)

        bz = bz_ref[...].astype(f32)
        dov = do_ref[...].astype(f32)
        dbz_ref[...] = (dov * att_ref[...].astype(f32) * _silu_grad(bz)).astype(dbz_ref.dtype)
        dob_ref[...] = (dov * _silu(bz)).astype(bf16)
        upto = _sb_sum_matrix(lambda s, j: s <= j, kb)
        before = _sb_sum_matrix(lambda j, s: j < s, kb)
        t_pos = i * tq + lax.broadcasted_iota(jnp.int32, (tq, kb), 0)
        s_off = lax.broadcasted_iota(jnp.int32, (tq, kb), 1)

        def step(j, carry, masked, row0=0):
            rows = pl.ds(pl.multiple_of(j * kb, kb), kb)
            out = []
            for h in range(hp):
                dq, sp_seen, e_seen = (c[row0:] for c in carry[h])
                sl = slice(h * HEAD, (h + 1) * HEAD)
                q, kj, vj, dob = q_ref[row0:, sl], k_ref[rows, sl], v_ref[rows, sl], dob_ref[row0:, sl]
                z, sp = _sb_scores(q, kj, scale)
                lb = z - sp
                if masked:
                    mask = s_off[row0:] + j * kb < t_pos[row0:]
                    sp = jnp.where(mask, sp, 0.0)
                sp_upto, sp_total = _sb_sums(sp, upto)
                w = jnp.exp2(lb + _sb_wide(sp_seen, kb) + sp_upto)
                if masked:
                    w = jnp.where(mask, w, 0.0)
                dv_acc[rows, sl] += _dot_tn(w.astype(bf16), dob)
                e = _dot_nt(dob, vj) * w
                e_before, e_total = _sb_sums(e, before)
                dz = (e - (e + _sb_wide(e_seen, kb) + e_before) * jnp.exp2(lb)) * scale
                if masked:
                    dz = jnp.where(mask, dz, 0.0)
                dz = dz.astype(bf16)
                dk_acc[rows, sl] += _dot_tn(dz, q)
                new = (dq + _dot(dz, kj), sp_seen + sp_total, e_seen + e_total)
                out.append(tuple(jnp.concatenate([old[:row0], upd]) if row0 else upd for old, upd in zip(carry[h], new)))
            return tuple(out)

        zero = jnp.zeros((tq, HEAD), f32)
        init = tuple((zero, jnp.broadcast_to(tot_ref[h], (tq, HEAD)), zero) for h in range(hp))
        carry = lax.fori_loop(0, band * i, lambda j, c: step(j, c, False), init)
        for lv in range(levels):
            carry = lax.fori_loop(
                0, band // levels,
                lambda t, c, lv=lv: step(band * i + lv * (band // levels) + t, c, True, lv * (tq // levels)), carry)
        for h in range(hp):
            dq_ref[:, h * HEAD:(h + 1) * HEAD] = carry[h][0].astype(dq_ref.dtype)

        @pl.when(i == nq - 1)
        def _():
            dk_ref[...] = dk_acc[...].astype(dk_ref.dtype)
            dv_ref[...] = dv_acc[...].astype(dv_ref.dtype)

    blk = lambda c0: pl.BlockSpec((tq, hp * HEAD), lambda g, i: (i, c0 // hp + g))
    head = lambda c0: pl.BlockSpec((l, hp * HEAD), lambda g, i: (0, c0 // hp + g))
    return pl.pallas_call(
        body, name="sb_bwd", grid=(nh // hp, nq),
        in_specs=[blk(qc), head(kc), head(vc), blk(zc), blk(nh), blk(0),
                  pl.BlockSpec((hp, tq, 1), lambda g, i: (g, i, 0))],
        out_specs=[blk(0), head(0), head(0), blk(0)],
        out_shape=[jax.ShapeDtypeStruct((l, wb), bf16)] * 4,
        scratch_shapes=[pltpu.VMEM((l, hp * HEAD), f32), pltpu.VMEM((l, hp * HEAD), f32),
                        pltpu.VMEM((tq, hp * HEAD), bf16)],
        compiler_params=_cparams("parallel", "arbitrary"),
    )(proj, proj, proj, proj, dcat, att, tot)


def _disc(lr, li, ldt):
    dt = jnp.exp(ldt)
    mag = jnp.exp(lr * dt)
    a_re = mag * jnp.cos(li * dt)
    a_im = mag * jnp.sin(li * dt)
    den = lr * lr + li * li
    nr = a_re - 1.0
    return a_re, a_im, (nr * lr + a_im * li) / den, (a_im * lr - nr * li) / den


def s5_params_fwd(lr, li, ldt, bt_re, bt_im):
    g, c, p = bt_re.shape

    def body(lr_ref, li_ref, ldt_ref, br_ref, bi_ref, ar_ref, ai_ref, bbr_ref, bbi_ref):
        a_re, a_im, cr, ci = _disc(lr_ref[...], li_ref[...], ldt_ref[...])
        ar_ref[...] = a_re
        ai_ref[...] = a_im
        for k in range(c):
            br, bi = br_ref[:, k, :], bi_ref[:, k, :]
            bbr_ref[:, k, :] = cr * br - ci * bi
            bbi_ref[:, k, :] = cr * bi + ci * br

    return pl.pallas_call(
        body, name="s5_params_fwd",
        out_shape=[jax.ShapeDtypeStruct((g, p), f32)] * 2 + [jax.ShapeDtypeStruct((g, c, p), f32)] * 2,
    )(lr, li, ldt, bt_re, bt_im)


def s5_params_bwd(lr, li, ldt, bt_re, bt_im, da_re, da_im, dbbt_re, dbbt_im):
    g, c, p = bt_re.shape

    def body(lr_ref, li_ref, ldt_ref, br_ref, bi_ref, dar_ref, dai_ref, dbbr_ref, dbbi_ref,
             dlr_ref, dli_ref, dldt_ref, dbr_ref, dbi_ref):
        (a_re, a_im, cr, ci), vjp = jax.vjp(_disc, lr_ref[...], li_ref[...], ldt_ref[...])
        dcr = jnp.zeros((g, p), f32)
        dci = jnp.zeros((g, p), f32)
        for k in range(c):
            br, bi = br_ref[:, k, :], bi_ref[:, k, :]
            dr, di = dbbr_ref[:, k, :], dbbi_ref[:, k, :]
            dcr += dr * br + di * bi
            dci += di * br - dr * bi
            dbr_ref[:, k, :] = cr * dr + ci * di
            dbi_ref[:, k, :] = cr * di - ci * dr
        dlr, dli, dldt = vjp((dar_ref[...], dai_ref[...], dcr, dci))
        dlr_ref[...] = dlr
        dli_ref[...] = dli
        dldt_ref[...] = dldt

    return pl.pallas_call(
        body, name="s5_params_bwd",
        out_shape=[jax.ShapeDtypeStruct((g, p), f32)] * 2 + [jax.ShapeDtypeStruct((g, 1), f32)]
        + [jax.ShapeDtypeStruct((g, c, p), f32)] * 2,
    )(lr, li, ldt, bt_re, bt_im, da_re, da_im, dbbt_re, dbbt_im)


def _cmul(ar, ai, br, bi):
    return ar * br - ai * bi, ar * bi + ai * br


def _power_tables(ar, ai):
    rows = lax.broadcasted_iota(jnp.int32, (SUBLANES, ar.shape[1]), 0)
    pr = jnp.zeros((SUBLANES, ar.shape[1]), f32)
    pi = jnp.zeros((SUBLANES, ar.shape[1]), f32)
    cr, ci = ar, ai
    pows = {}
    for r in range(SUBLANES):
        pows[r + 1] = (cr, ci)
        pr = jnp.where(rows == r, cr, pr)
        pi = jnp.where(rows == r, ci, pi)
        cr, ci = _cmul(cr, ci, ar, ai)
    return [pows[1], pows[2], pows[4]], pr, pi


def _ssm_time_tile(l):
    return _tile(l, (512, 256, 128))


def ssm_fwd(u, bre3, bim3, cre3, cimn3, a_re, a_im, d_skip):
    l, w = u.shape
    nj = w // HEAD
    ns = STATES_PER_LANE_BLOCK
    tt = _ssm_time_tile(l)

    def body(u_ref, bre_ref, bim_ref, cre_ref, cim_ref, ar_ref, ai_ref, d_ref, y_ref, hr_ref, hi_ref, cr_ref, ci_ref):
        @pl.when(pl.program_id(1) == 0)
        def _():
            cr_ref[...] = jnp.zeros_like(cr_ref)
            ci_ref[...] = jnp.zeros_like(ci_ref)

        uv = u_ref[...]
        hr_ref[...] = _dot(uv, bre_ref[...])
        hi_ref[...] = _dot(uv, bim_ref[...])
        steps, pr, pi = _power_tables(ar_ref[...], ai_ref[...])
        rows = lax.broadcasted_iota(jnp.int32, (SUBLANES, ns), 0)
        steps = [(jnp.where(rows >= d, sr_, 0.0), jnp.where(rows >= d, si_, 0.0)) for d, (sr_, si_) in zip((1, 2, 4), steps)]

        def blk(b, carry):
            cr, ci = carry
            sl = pl.ds(pl.multiple_of(b * SUBLANES, SUBLANES), SUBLANES)
            xr, xi = hr_ref[sl, :], hi_ref[sl, :]
            for d, (sr_, si_) in zip((1, 2, 4), steps):
                mr, mi = _cmul(sr_, si_, pltpu.roll(xr, d, axis=0), pltpu.roll(xi, d, axis=0))
                xr, xi = xr + mr, xi + mi
            mr, mi = _cmul(pr, pi, cr, ci)
            xr, xi = xr + mr, xi + mi
            hr_ref[sl, :] = xr
            hi_ref[sl, :] = xi
            return xr[SUBLANES - 1:, :], xi[SUBLANES - 1:, :]

        cr, ci = lax.fori_loop(0, tt // SUBLANES, blk, (cr_ref[...], ci_ref[...]))
        cr_ref[...] = cr
        ci_ref[...] = ci
        y = _dot(hr_ref[...].astype(bf16), cre_ref[...]) + _dot(hi_ref[...].astype(bf16), cim_ref[...])
        y_ref[...] = y + d_ref[...] * uv.astype(f32)

    lane = pl.BlockSpec((tt, HEAD), lambda j, i: (i, j))
    st = pl.BlockSpec((tt, ns), lambda j, i: (i, j))
    b3 = pl.BlockSpec((None, HEAD, ns), lambda j, i: (j, 0, 0))
    c3 = pl.BlockSpec((None, ns, HEAD), lambda j, i: (j, 0, 0))
    arow = pl.BlockSpec((1, ns), lambda j, i: (0, j))
    return pl.pallas_call(
        body, name="ssm_fwd", grid=(nj, l // tt),
        in_specs=[lane, b3, b3, c3, c3, arow, arow, pl.BlockSpec((1, HEAD), lambda j, i: (0, j))],
        out_specs=[lane, st, st],
        out_shape=[jax.ShapeDtypeStruct((l, w), f32), jax.ShapeDtypeStruct((l, nj * ns), f32),
                   jax.ShapeDtypeStruct((l, nj * ns), f32)],
        scratch_shapes=[pltpu.VMEM((1, ns), f32), pltpu.VMEM((1, ns), f32)],
        compiler_params=_cparams("parallel", "arbitrary"),
    )(u, bre3, bim3, cre3, cimn3, a_re, a_im, d_skip)


def ssm_bwd(dy, u, h_re, h_im, bre3, bim3, cre3, cimn3, a_re, a_im, d_skip):
    l, w = u.shape
    nj = w // HEAD
    ns = STATES_PER_LANE_BLOCK
    tt = _ssm_time_tile(l)
    nt = l // tt

    def body(dy_ref, u_ref, hr_ref, hi_ref, bre_ref, bim_ref, cre_ref, cim_ref, ar_ref, ai_ref, d_ref,
             du_ref, dd_ref, dar_ref, dai_ref, dbre_ref, dbim_ref, dcre_ref, dcim_ref, kr_ref, ki_ref, cr_ref, ci_ref,
             accr_ref, acci_ref):
        i = pl.program_id(1)

        @pl.when(i == 0)
        def _():
            for ref in (cr_ref, ci_ref, accr_ref, acci_ref, dd_ref, dbre_ref, dbim_ref, dcre_ref, dcim_ref):
                ref[...] = jnp.zeros_like(ref)

        dyv = dy_ref[...]
        dyb = dyv.astype(bf16)
        uv = u_ref[...]
        kr_ref[...] = _dot_nt(dyb, cre_ref[...])
        ki_ref[...] = _dot_nt(dyb, cim_ref[...])
        steps, pr, pi = _power_tables(ar_ref[...], -ai_ref[...])
        rows = lax.broadcasted_iota(jnp.int32, (SUBLANES, ns), 0)
        qr = jnp.zeros((SUBLANES, ns), f32)
        qi = jnp.zeros((SUBLANES, ns), f32)
        for r in range(SUBLANES):
            qr = jnp.where(rows == r, pr[SUBLANES - 1 - r:SUBLANES - r, :], qr)
            qi = jnp.where(rows == r, pi[SUBLANES - 1 - r:SUBLANES - r, :], qi)
        nb = tt // SUBLANES
        steps = [(jnp.where(rows < SUBLANES - d, sr_, 0.0), jnp.where(rows < SUBLANES - d, si_, 0.0))
                 for d, (sr_, si_) in zip((1, 2, 4), steps)]

        def blk(t, carry):
            cr, ci, accr, acci = carry
            sl = pl.ds(pl.multiple_of((nb - 1 - t) * SUBLANES, SUBLANES), SUBLANES)
            xr, xi = kr_ref[sl, :], ki_ref[sl, :]
            for d, (sr_, si_) in zip((1, 2, 4), steps):
                mr, mi = _cmul(sr_, si_, pltpu.roll(xr, SUBLANES - d, axis=0), pltpu.roll(xi, SUBLANES - d, axis=0))
                xr, xi = xr + mr, xi + mi
            mr, mi = _cmul(qr, qi, cr, ci)
            xr, xi = xr + mr, xi + mi
            kr_ref[sl, :] = xr
            ki_ref[sl, :] = xi
            last = rows == SUBLANES - 1
            nr = jnp.where(last, cr, pltpu.roll(xr, SUBLANES - 1, axis=0))
            ni = jnp.where(last, ci, pltpu.roll(xi, SUBLANES - 1, axis=0))
            hr, hi = hr_ref[sl, :], hi_ref[sl, :]
            accr = accr + nr * hr + ni * hi
            acci = acci + ni * hr - nr * hi
            return xr[:1, :], xi[:1, :], accr, acci

        cr, ci, accr, acci = lax.fori_loop(0, nb, blk, (cr_ref[...], ci_ref[...], accr_ref[...], acci_ref[...]))
        cr_ref[...] = cr
        ci_ref[...] = ci
        accr_ref[...] = accr
        acci_ref[...] = acci
        kr, ki = kr_ref[...].astype(bf16), ki_ref[...].astype(bf16)
        du = _dot_nt(kr, bre_ref[...]) + _dot_nt(ki, bim_ref[...]) + d_ref[...] * dyv
        du_ref[...] = du.astype(du_ref.dtype)
        dd_ref[...] += jnp.sum(dyv * uv.astype(f32), axis=0, keepdims=True)
        dbre_ref[...] += _dot_tn(uv, kr)
        dbim_ref[...] += _dot_tn(uv, ki)
        dcre_ref[...] += _dot_tn(hr_ref[...].astype(bf16), dyb)
        dcim_ref[...] += _dot_tn(hi_ref[...].astype(bf16), dyb)

        @pl.when(i == nt - 1)
        def _():
            dar_ref[...] = jnp.sum(accr_ref[...], axis=0, keepdims=True)
            dai_ref[...] = jnp.sum(acci_ref[...], axis=0, keepdims=True)

    lane = pl.BlockSpec((tt, HEAD), lambda j, i: (nt - 1 - i, j))
    st = pl.BlockSpec((tt, ns), lambda j, i: (nt - 1 - i, j))
    b3 = pl.BlockSpec((None, HEAD, ns), lambda j, i: (j, 0, 0))
    c3 = pl.BlockSpec((None, ns, HEAD), lambda j, i: (j, 0, 0))
    arow = pl.BlockSpec((1, ns), lambda j, i: (0, j))
    drow = pl.BlockSpec((1, HEAD), lambda j, i: (0, j))
    return pl.pallas_call(
        body, name="ssm_bwd", grid=(nj, nt),
        in_specs=[lane, lane, st, st, b3, b3, c3, c3, arow, arow, drow],
        out_specs=[lane, drow, arow, arow, b3, b3, c3, c3],
        out_shape=[jax.ShapeDtypeStruct((l, w), bf16), jax.ShapeDtypeStruct((1, w), f32),
                   jax.ShapeDtypeStruct((1, nj * ns), f32), jax.ShapeDtypeStruct((1, nj * ns), f32),
                   jax.ShapeDtypeStruct((nj, HEAD, ns), f32), jax.ShapeDtypeStruct((nj, HEAD, ns), f32),
                   jax.ShapeDtypeStruct((nj, ns, HEAD), f32), jax.ShapeDtypeStruct((nj, ns, HEAD), f32)],
        scratch_shapes=[pltpu.VMEM((tt, ns), f32), pltpu.VMEM((tt, ns), f32), pltpu.VMEM((1, ns), f32),
                        pltpu.VMEM((1, ns), f32), pltpu.VMEM((SUBLANES, ns), f32), pltpu.VMEM((SUBLANES, ns), f32)],
        compiler_params=_cparams("parallel", "arbitrary"),
    )(dy, u, h_re, h_im, bre3, bim3, cre3, cimn3, a_re, a_im, d_skip)


def glu_fwd(y, z_src, w_glu, b_glu):
    l, w = y.shape
    tm = _row_tile(l)

    def body(y_ref, z_ref, w_ref, b_ref, g_ref, t_ref, o_ref):
        g = _gelu(y_ref[...])
        gb = g.astype(bf16)
        t = _dot(gb, w_ref[...]) + b_ref[...]
        g_ref[...] = gb
        t_ref[...] = t
        o_ref[...] = (g * jax.nn.sigmoid(t) * _silu(z_ref[...].astype(f32))).astype(o_ref.dtype)

    blk = pl.BlockSpec((tm, w), lambda i: (i, 0))
    return pl.pallas_call(
        body, name="glu_fwd", grid=(l // tm,),
        in_specs=[blk, pl.BlockSpec((tm, w), lambda i: (i, 1)), pl.BlockSpec((w, w), lambda i: (0, 0)), _row(w)],
        out_specs=[blk, blk, blk],
        out_shape=[jax.ShapeDtypeStruct((l, w), bf16), jax.ShapeDtypeStruct((l, w), f32),
                   jax.ShapeDtypeStruct((l, w), bf16)],
        compiler_params=_cparams("parallel"),
    )(y, z_src, w_glu, b_glu)


def glu_bwd(dout, y, t, z_src, w_glu):
    l, w = y.shape
    tm = _row_tile(l)

    def body(do_ref, y_ref, t_ref, z_ref, w_ref, dy_ref, dz_ref, dt_ref, db_ref):
        @pl.when(pl.program_id(0) == 0)
        def _():
            db_ref[...] = jnp.zeros_like(db_ref)

        yv, zv, dov = y_ref[...], z_ref[...].astype(f32), do_ref[...]
        g = _gelu(yv)
        sg = jax.nn.sigmoid(t_ref[...])
        dy2 = dov * _silu(zv)
        dz_ref[...] = (dov * g * sg * _silu_grad(zv)).astype(dz_ref.dtype)
        dt = dy2 * g * sg * (1.0 - sg)
        dtb = dt.astype(bf16)
        dt_ref[...] = dtb
        db_ref[...] += jnp.sum(dt, axis=0, keepdims=True)
        dg = dy2 * sg + _dot_nt(dtb, w_ref[...])
        dy_ref[...] = dg * _gelu_grad(yv)

    blk = pl.BlockSpec((tm, w), lambda i: (i, 0))
    return pl.pallas_call(
        body, name="glu_bwd", grid=(l // tm,),
        in_specs=[blk, blk, blk, pl.BlockSpec((tm, w), lambda i: (i, 1)), pl.BlockSpec((w, w), lambda i: (0, 0))],
        out_specs=[blk, blk, blk, _row(w)],
        out_shape=[jax.ShapeDtypeStruct((l, w), f32), jax.ShapeDtypeStruct((l, w), bf16),
                   jax.ShapeDtypeStruct((l, w), bf16), jax.ShapeDtypeStruct((1, w), f32)],
        compiler_params=_cparams("arbitrary"),
    )(dout, y, t, z_src, w_glu)


def _adamw(w, g, m, v):
    m = ADAM_B1 * m + (1.0 - ADAM_B1) * g
    v = ADAM_B2 * v + (1.0 - ADAM_B2) * (g * g)
    m_hat = m / (1.0 - ADAM_B1 ** ADAM_STEP)
    v_hat = v / (1.0 - ADAM_B2 ** ADAM_STEP)
    return -ADAM_LR * (m_hat / (jnp.sqrt(v_hat) + ADAM_EPS) + ADAM_WD * w), m, v


def adam_reduce(pieces, w, m, v, name):
    r, c = w.shape
    n = pieces.shape[0]
    tr = _tile(r, (256, 128, 64, 32, 16, 8))

    def body(p_ref, w_ref, m_ref, v_ref, g_ref, d_ref, nm_ref, nv_ref):
        g = p_ref[0].astype(f32)
        for s in range(1, n):
            g = g + p_ref[s].astype(f32)
        g_ref[...] = g
        d_ref[...], nm_ref[...], nv_ref[...] = _adamw(w_ref[...], g, m_ref[...], v_ref[...])

    blk = pl.BlockSpec((tr, c), lambda i: (i, 0))
    return pl.pallas_call(
        body, name=name, grid=(r // tr,),
        in_specs=[pl.BlockSpec((n, tr, c), lambda i: (0, i, 0)), blk, blk, blk],
        out_specs=[blk] * 4, out_shape=[jax.ShapeDtypeStruct((r, c), f32)] * 4,
        compiler_params=_cparams("parallel"),
    )(pieces, w, m, v)


def adam_w_mod(cond_t, dm, w, m, v):
    nl, d, cols = w.shape
    tr = _tile(d, (512, 256, 128))

    def body(c_ref, dm_ref, w_ref, m_ref, v_ref, g_ref, d_ref, nm_ref, nv_ref):
        g = jnp.dot(c_ref[...], dm_ref[...], preferred_element_type=f32, precision=lax.Precision.HIGHEST)
        g_ref[...] = g
        d_ref[...], nm_ref[...], nv_ref[...] = _adamw(w_ref[...], g, m_ref[...], v_ref[...])

    blk = pl.BlockSpec((None, tr, cols), lambda l, i: (l, i, 0))
    return pl.pallas_call(
        body, name="adam_w_mod", grid=(nl, d // tr),
        in_specs=[pl.BlockSpec((tr, N_DEV), lambda l, i: (i, 0)), pl.BlockSpec((None, N_DEV, cols), lambda l, i: (l, 0, 0)),
                  blk, blk, blk],
        out_specs=[blk] * 4, out_shape=[jax.ShapeDtypeStruct((nl, d, cols), f32)] * 4,
        compiler_params=_cparams("parallel", "parallel"),
    )(cond_t, dm, w, m, v)


def silu_rows(c_all):
    def body(c_ref, o_ref):
        o_ref[...] = _silu(c_ref[...])

    return pl.pallas_call(body, name="silu_rows", out_shape=jax.ShapeDtypeStruct(c_all.shape, f32))(c_all)


def _block_diag(x):
    g, a, b = x.shape
    nj = g // GROUPS_PER_LANE_BLOCK
    eye = jnp.eye(GROUPS_PER_LANE_BLOCK, dtype=x.dtype)
    x5 = x.reshape(nj, GROUPS_PER_LANE_BLOCK, a, b)
    return jnp.einsum("jgab,gh->jgahb", x5, eye).reshape(nj, GROUPS_PER_LANE_BLOCK * a, GROUPS_PER_LANE_BLOCK * b)


def _diag_blocks(x, a, b):
    nj = x.shape[0]
    x5 = x.reshape(nj, GROUPS_PER_LANE_BLOCK, a, GROUPS_PER_LANE_BLOCK, b)
    eye = jnp.eye(GROUPS_PER_LANE_BLOCK, dtype=x.dtype)
    return jnp.einsum("jgahb,gh->jgab", x5, eye).reshape(nj * GROUPS_PER_LANE_BLOCK, a, b)


PACK_ROW = SUBLANES * HEAD


def _pack(parts, row_multiple=SUBLANES):
    rows = []
    for p in parts:
        flat = p.reshape(-1)
        pad = (-flat.shape[0]) % PACK_ROW
        if pad:
            flat = jnp.concatenate([flat, jnp.zeros((pad,), flat.dtype)])
        rows.append(flat.reshape(-1, HEAD))
    pad = (-sum(r.shape[0] for r in rows)) % row_multiple
    if pad:
        rows.append(jnp.zeros((pad, HEAD), rows[0].dtype))
    return jnp.concatenate(rows, axis=0)


def _unpack(packed, shapes):
    out, r0 = [], 0
    for shp in shapes:
        n = math.prod(shp)
        nr = -(-n // PACK_ROW) * SUBLANES
        out.append(packed[r0:r0 + nr].reshape(-1)[:n].reshape(shp))
        r0 += nr
    return out


def adam_small(g, w, m, v):
    r, c = w.shape

    def body(g_ref, w_ref, m_ref, v_ref, d_ref, nm_ref, nv_ref):
        d_ref[...], nm_ref[...], nv_ref[...] = _adamw(w_ref[...], g_ref[...], m_ref[...], v_ref[...])

    tr = max(t for t in range(SUBLANES, 1024 + 1, SUBLANES) if r % t == 0)
    blk = pl.BlockSpec((tr, c), lambda i: (i, 0))
    return pl.pallas_call(
        body, name="adam_small", grid=(r // tr,),
        in_specs=[blk] * 4, out_specs=[blk] * 3, out_shape=[jax.ShapeDtypeStruct((r, c), f32)] * 3,
        compiler_params=_cparams("parallel"),
    )(g, w, m, v)


def kernel(x, c, ln_pre_g, ln_post_g, w_mod, b_mod, w_in_ab, w_out_ab, sgu_norm_g, sgu_w, sgu_b, w_in_ssm, w_out_ssm, lam_re, lam_im, b_re, b_im, c_re, c_im, d_skip, log_dt, w_glu, b_glu, loss_target, m_ln_pre_g, m_ln_post_g, m_w_mod, m_b_mod, m_w_in_ab, m_w_out_ab, m_sgu_norm_g, m_sgu_w, m_sgu_b, m_w_in_ssm, m_w_out_ssm, m_lam_re, m_lam_im, m_b_re, m_b_im, m_c_re, m_c_im, m_d_skip, m_log_dt, m_w_glu, m_b_glu, v_ln_pre_g, v_ln_post_g, v_w_mod, v_b_mod, v_w_in_ab, v_w_out_ab, v_sgu_norm_g, v_sgu_w, v_sgu_b, v_w_in_ssm, v_w_out_ssm, v_lam_re, v_lam_im, v_b_re, v_b_im, v_c_re, v_c_im, v_d_skip, v_log_dt, v_w_glu, v_b_glu):
    me = _my_index()
    x0 = x[0]
    l, d = x0.shape
    target = loss_target[0]
    nh = sgu_w.shape[1]
    wa = nh * HEAD
    n_grp, n_st = lam_re.shape[1], lam_re.shape[2]
    mod_cols = w_mod.shape[2]

    c_all, d_skip_all, b_glu_all = all_gather([c, d_skip, b_glu], "gather_c")
    c_all = c_all.reshape(N_DEV, d)
    d_skip_all = d_skip_all.reshape(1, -1)
    b_glu_all = b_glu_all.reshape(1, -1)

    b_cols = lax.dynamic_slice_in_dim(b_mod, me * mod_cols, mod_cols, axis=1)
    (mod_all,) = all_gather([mod_part(c_all, w_mod, b_cols)], "gather_mod")
    def after(a, first):
        return a + jnp.minimum(jnp.abs(first[(0,) * first.ndim].astype(f32)), 0.0).astype(a.dtype)

    (win_ab3,) = sequencer_exchange(GATHER, [after(w_in_ab[0], mod_all).astype(bf16)], "gather_w_in", 1)
    mod_mine = lax.dynamic_index_in_dim(mod_all, me, axis=2, keepdims=False)
    mod_rows = jnp.transpose(mod_mine, (1, 0, 2)).reshape(2, 3, 1, d)

    def rows(a, i):
        return a[i].reshape(1, d)

    shift0, scale0, gate0 = mod_rows[0, 0], mod_rows[0, 1], mod_rows[0, 2]
    h0 = prenorm_fwd(x0, rows(ln_pre_g, 0), shift0, scale0, "prenorm0")
    wout_ab3, win_ssm3, wout_ssm3, wglu = sequencer_exchange(
        GATHER, [after(w, win_ab3).astype(bf16) for w in (w_out_ab[0], w_in_ssm[0], w_out_ssm[0], w_glu[0])],
        "gather_w_rest", 2)
    proj0 = mm_nn(h0, win_ab3, bf16, "proj0")
    sgu_b3 = sgu_b[0].reshape(nh, HEAD, 1)
    out_a = sgu_fwd(proj0, sgu_norm_g, sgu_w[0], sgu_b3)
    out_b, att, tot = sb_fwd(proj0, nh)
    cat = jnp.concatenate([out_a, out_b], axis=1)
    wout_ab3 = wout_ab3.reshape(1, d, d)
    win_ssm3 = win_ssm3.reshape(1, d, d)
    wglu = wglu.reshape(w_glu.shape[2], w_glu.shape[2])
    y0 = mm_nn(cat, wout_ab3, f32, "out0")

    shift1, scale1, gate1 = mod_rows[1, 0], mod_rows[1, 1], mod_rows[1, 2]
    x1, h1 = post_prenorm_fwd(x0, y0, gate0, rows(ln_post_g, 0), rows(ln_pre_g, 1), shift1, scale1, "post0_prenorm1")
    proj1 = mm_nn(h1, win_ssm3, bf16, "proj1")
    w_ssm = proj1.shape[1] // 2
    ldt = log_dt[0].reshape(n_grp, 1)
    bt_re = jnp.transpose(b_re[0], (0, 2, 1))
    bt_im = jnp.transpose(b_im[0], (0, 2, 1))
    a_re, a_im, bbt_re, bbt_im = s5_params_fwd(lam_re[0], lam_im[0], ldt, bt_re, bt_im)
    bre3 = _block_diag(bbt_re).astype(bf16)
    bim3 = _block_diag(bbt_im).astype(bf16)
    cre3 = _block_diag(jnp.transpose(c_re[0], (0, 2, 1))).astype(bf16)
    cimn3 = _block_diag(-jnp.transpose(c_im[0], (0, 2, 1))).astype(bf16)
    a_re_row, a_im_row = a_re.reshape(1, -1), a_im.reshape(1, -1)
    u = proj1[:, :w_ssm]
    y_ssm, hs_re, hs_im = ssm_fwd(u, bre3, bim3, cre3, cimn3, a_re_row, a_im_row, d_skip_all)
    g_act, t_glu, mix1 = glu_fwd(y_ssm, proj1, wglu, b_glu_all)
    y1 = mm_nn(mix1, wout_ssm3, f32, "out1")

    dx2, loss_tile, dy1, dgate1, dgpost1 = final_loss(x1, y1, gate1, rows(ln_post_g, 1), target)

    dmix1 = mm_nt(dy1, wout_ssm3, f32, "dmix1")
    gw_out_ssm = mm_tn(mix1, dy1, N_DEV, bf16, "gw_out_ssm")
    (p_out_ssm,) = sequencer_exchange(SCATTER, [gw_out_ssm], "scatter_g1", 3)
    dy_ssm, dz1, dt_glu, db_glu = glu_bwd(dmix1, y_ssm, t_glu, proj1, wglu)
    gw_glu = mm_tn(g_act, dt_glu, 1, bf16, "gw_glu").reshape(N_DEV, -1, w_ssm)
    du, dd_skip, da_re, da_im, dbre3, dbim3, dcre3, dcimn3 = ssm_bwd(
        dy_ssm, u, hs_re, hs_im, bre3, bim3, cre3, cimn3, a_re_row, a_im_row, d_skip_all)
    dproj1 = jnp.concatenate([du, dz1], axis=1)
    gw_in_ssm = mm_tn(h1, dproj1, 1, bf16, "gw_in_ssm").reshape(N_DEV, -1, proj1.shape[1])
    p_in_ssm, p_glu = sequencer_exchange(SCATTER, [gw_in_ssm, gw_glu], "scatter_g2", 4)
    dh1 = mm_nt(dproj1, win_ssm3, f32, "dh1")
    dx1, dshift1, dscale1, dgpre1 = prenorm_bwd(dh1, x1, dx2, rows(ln_pre_g, 1), scale1, "prenorm1_bwd")
    dlr, dli, dldt, dbt_re, dbt_im = s5_params_bwd(
        lam_re[0], lam_im[0], ldt, bt_re, bt_im, da_re.reshape(n_grp, n_st), da_im.reshape(n_grp, n_st),
        _diag_blocks(dbre3, SSM_GROUP, n_st), _diag_blocks(dbim3, SSM_GROUP, n_st))
    g_b_re = jnp.transpose(dbt_re, (0, 2, 1))
    g_b_im = jnp.transpose(dbt_im, (0, 2, 1))
    g_c_re = jnp.transpose(_diag_blocks(dcre3, n_st, SSM_GROUP), (0, 2, 1))
    g_c_im = -jnp.transpose(_diag_blocks(dcimn3, n_st, SSM_GROUP), (0, 2, 1))

    dy0, dgate0, dgpost0 = post_bwd(dx1, y0, gate0, rows(ln_post_g, 0), "post0_bwd")
    dcat = mm_nt(dy0, wout_ab3, f32, "dcat")
    gw_out_ab = mm_tn(cat, dy0, 1, bf16, "gw_out_ab").reshape(N_DEV, -1, d)
    (p_out_ab,) = sequencer_exchange(SCATTER, [gw_out_ab], "scatter_g3", 5)
    da, dsgu_w, dsgu_b, dsgu_ng = sgu_bwd(proj0, dcat, sgu_norm_g, sgu_w[0], sgu_b3)
    dq, dk, dv, dbz = sb_bwd(proj0, dcat, att, tot, nh)
    dproj0 = jnp.concatenate([da, dq, dk, dv, dbz], axis=1)
    gw_in_ab = mm_nn(jnp.transpose(h0), dproj0[None], bf16, "gw_in_ab", split_cols=N_DEV)
    (p_in_ab,) = sequencer_exchange(SCATTER, [gw_in_ab], "scatter_g4", 6)
    dh0 = mm_nt(dproj0, win_ab3, f32, "dh0")
    dx0, dshift0, dscale0, dgpre0 = prenorm_bwd(dh0, x0, dx1, rows(ln_pre_g, 0), scale0, "prenorm0_bwd")

    small_names = ["ln_pre_g", "ln_post_g", "b_mod", "sgu_norm_g", "sgu_w", "sgu_b", "lam_re", "lam_im", "b_re", "b_im",
                   "c_re", "c_im", "log_dt"]
    small_w = [ln_pre_g, ln_post_g, b_mod, sgu_norm_g, sgu_w, sgu_b, lam_re, lam_im, b_re, b_im, c_re, c_im, log_dt]
    small_m = [m_ln_pre_g, m_ln_post_g, m_b_mod, m_sgu_norm_g, m_sgu_w, m_sgu_b, m_lam_re, m_lam_im, m_b_re, m_b_im,
               m_c_re, m_c_im, m_log_dt]
    small_v = [v_ln_pre_g, v_ln_post_g, v_b_mod, v_sgu_norm_g, v_sgu_w, v_sgu_b, v_lam_re, v_lam_im, v_b_re, v_b_im,
               v_c_re, v_c_im, v_log_dt]
    dmod = jnp.concatenate([dshift0, dscale0, dgate0, dshift1, dscale1, dgate1], axis=1)
    small_g = [jnp.concatenate([dgpre0, dgpre1]), jnp.concatenate([dgpost0, dgpost1]), dmod, dsgu_ng, dsgu_w, dsgu_b,
               dlr, dli, g_b_re, g_b_im, g_c_re, g_c_im, dldt]
    shapes = [w.shape for w in small_w]
    g_sum, dmod_all = all_reduce_rows(_pack(small_g + [dd_skip, db_glu, loss_tile], SUBLANES * N_DEV), dmod,
                                      "reduce_small_grads")
    n_rows_small = sum(-(-math.prod(s) // PACK_ROW) * SUBLANES for s in shapes)
    loss = g_sum[n_rows_small + 2 * (d_skip_all.shape[1] // HEAD), 0] * (0.5 / d)
    new_small = adam_small(g_sum, _pack(small_w), _pack(small_m), _pack(small_v))
    r_small = [_unpack(o, shapes) for o in [g_sum[:n_rows_small]] + list(new_small)]
    small = {n: [r_small[k][i] for k in range(4)] for i, n in enumerate(small_names)}
    vec_rows = d_skip_all.shape[1] // HEAD

    def my_columns(r0):
        whole = g_sum[r0:r0 + vec_rows].reshape(1, 1, -1)
        return lax.dynamic_slice_in_dim(whole, me * d_skip.shape[1], d_skip.shape[1], axis=2)

    def sharded(p, w, m, v, name):
        shp = w.shape
        w2, m2, v2 = (a.reshape(-1, shp[-1]) for a in (w, m, v))
        return [o.reshape(shp) for o in adam_reduce(p.reshape(p.shape[0], -1, shp[-1]), w2, m2, v2, name)]

    r_d_skip = sharded(my_columns(n_rows_small), d_skip, m_d_skip, v_d_skip, "adam_d_skip")
    r_b_glu = sharded(my_columns(n_rows_small + vec_rows), b_glu, m_b_glu, v_b_glu, "adam_b_glu")
    r_w_out_ssm = sharded(p_out_ssm, w_out_ssm, m_w_out_ssm, v_w_out_ssm, "adam_w_out_ssm")
    r_w_in_ssm = sharded(p_in_ssm, w_in_ssm, m_w_in_ssm, v_w_in_ssm, "adam_w_in_ssm")
    r_w_glu = sharded(p_glu, w_glu, m_w_glu, v_w_glu, "adam_w_glu")
    r_w_out_ab = sharded(p_out_ab, w_out_ab, m_w_out_ab, v_w_out_ab, "adam_w_out_ab")
    r_w_in_ab = sharded(p_in_ab, w_in_ab, m_w_in_ab, v_w_in_ab, "adam_w_in_ab")

    dm_cols = jnp.transpose(
        lax.dynamic_slice_in_dim(dmod_all.reshape(N_DEV, 2, 3 * d), me * mod_cols, mod_cols, axis=2), (1, 0, 2))
    cond_t = jnp.transpose(silu_rows(c_all))
    r_w_mod = adam_w_mod(cond_t, dm_cols, w_mod, m_w_mod, v_w_mod)

    res = dict(small)
    res.update(w_mod=r_w_mod, w_in_ab=r_w_in_ab, w_out_ab=r_w_out_ab, w_in_ssm=r_w_in_ssm, w_out_ssm=r_w_out_ssm,
               d_skip=r_d_skip, w_glu=r_w_glu, b_glu=r_b_glu)
    order = ["ln_pre_g", "ln_post_g", "w_mod", "b_mod", "w_in_ab", "w_out_ab", "sgu_norm_g", "sgu_w", "sgu_b", "w_in_ssm",
             "w_out_ssm", "lam_re", "lam_im", "b_re", "b_im", "c_re", "c_im", "d_skip", "log_dt", "w_glu", "b_glu"]
    outs = [loss, dx0.reshape(x.shape)]
    for k in range(4):
        outs += [res[n][k] for n in order]
    return tuple(outs)
```

```python
import functools
import math

import jax
import jax.numpy as jnp
from jax import lax
from jax.experimental import pallas as pl
from jax.experimental.pallas import tpu as pltpu
from jax.experimental.pallas import tpu_sc as plsc

f32 = jnp.float32
bf16 = jnp.bfloat16

N_DEV = 8
EPS = 1e-6
HEAD = 128
SUBLANES = 8
SSM_GROUP = 16
SSM_STATE = 64
GROUPS_PER_LANE_BLOCK = HEAD // SSM_GROUP
STATES_PER_LANE_BLOCK = GROUPS_PER_LANE_BLOCK * SSM_STATE
VMEM_LIMIT = 56 * 2 ** 20
ADAM_LR, ADAM_B1, ADAM_B2, ADAM_EPS, ADAM_WD, ADAM_STEP = 0.001, 0.9, 0.999, 1e-08, 0.01, 10
_GELU_C0 = math.sqrt(2.0 / math.pi)
_GELU_C1 = 0.044715
MESH = pl.DeviceIdType.MESH


def _cparams(*sem):
    return pltpu.CompilerParams(dimension_semantics=sem if sem else None, vmem_limit_bytes=VMEM_LIMIT)


def _gelu(x):
    return 0.5 * x * (1.0 + jnp.tanh(_GELU_C0 * (x + _GELU_C1 * x * x * x)))


def _gelu_grad(x):
    t = jnp.tanh(_GELU_C0 * (x + _GELU_C1 * x * x * x))
    return 0.5 * (1.0 + t) + 0.5 * x * (1.0 - t * t) * _GELU_C0 * (1.0 + 3.0 * _GELU_C1 * x * x)


def _silu(x):
    return x * jax.nn.sigmoid(x)


def _silu_grad(x):
    s = jax.nn.sigmoid(x)
    return s * (1.0 + x * (1.0 - s))


def _dot(a, b):
    return jnp.dot(a, b, preferred_element_type=f32)


def _dot_nt(a, b):
    return lax.dot_general(a, b, (((1,), (1,)), ((), ())), preferred_element_type=f32)


def _dot_tn(a, b):
    return lax.dot_general(a, b, (((0,), (0,)), ((), ())), preferred_element_type=f32)


def _split_bf16(v):
    hi = v.astype(bf16)
    lo = (v - hi.astype(f32)).astype(bf16)
    return hi, lo


def _row(d):
    return pl.BlockSpec((1, d), lambda *_: (0, 0))


def _my_index():
    return 4 * lax.axis_index("x") + 2 * lax.axis_index("y") + lax.axis_index("c")


def _peer(k):
    x, y, c = lax.axis_index("x"), lax.axis_index("y"), lax.axis_index("c")
    return (1 - x if k & 4 else x, 1 - y if k & 2 else y, 1 - c if k & 1 else c)


def all_gather(arrs, name):
    n = len(arrs)

    def body(*refs):
        ins, outs = refs[:n], refs[n:2 * n]
        send, recv, local = refs[2 * n:]
        me = _my_index()
        copies = []
        for a in range(n):
            cp = pltpu.make_async_copy(ins[a], outs[a].at[me], local.at[a])
            cp.start()
            copies.append(cp)
            for k in range(1, N_DEV):
                s = a * (N_DEV - 1) + k - 1
                cp = pltpu.make_async_remote_copy(src_ref=ins[a], dst_ref=outs[a].at[me], send_sem=send.at[s],
                                                  recv_sem=recv.at[s], device_id=_peer(k), device_id_type=MESH)
                cp.start()
                copies.append(cp)
        for cp in copies:
            cp.wait()

    any_spec = pl.BlockSpec(memory_space=pl.ANY)
    outs = pl.pallas_call(
        body, name=name,
        out_shape=[jax.ShapeDtypeStruct((N_DEV,) + a.shape, a.dtype) for a in arrs],
        in_specs=[any_spec] * n, out_specs=[any_spec] * n,
        scratch_shapes=[pltpu.SemaphoreType.DMA((n * (N_DEV - 1),)), pltpu.SemaphoreType.DMA((n * (N_DEV - 1),)),
                        pltpu.SemaphoreType.DMA((n,))],
        compiler_params=pltpu.CompilerParams(has_side_effects=True),
    )(*arrs)
    return list(outs)


def all_reduce_rows(pack, extra, name):
    r, c = pack.shape
    rs = r // N_DEV
    n_peer = N_DEV - 1

    def body(p_ref, x_ref, o_ref, xo_ref, land, red, send1, recv1, send2, recv2, sendx, recvx, local):
        me = _my_index()

        def rows(i):
            return pl.ds(pl.multiple_of(i * rs, SUBLANES), rs)

        own = [pltpu.make_async_copy(p_ref.at[rows(me)], land.at[me], local.at[0]),
               pltpu.make_async_copy(x_ref, xo_ref.at[me], local.at[1])]
        first = []
        for k in range(1, N_DEV):
            first.append(pltpu.make_async_remote_copy(
                src_ref=p_ref.at[rows(jnp.bitwise_xor(me, k))], dst_ref=land.at[me], send_sem=send1.at[k - 1],
                recv_sem=recv1.at[k - 1], device_id=_peer(k), device_id_type=MESH))
            first.append(pltpu.make_async_remote_copy(
                src_ref=x_ref, dst_ref=xo_ref.at[me], send_sem=sendx.at[k - 1], recv_sem=recvx.at[k - 1],
                device_id=_peer(k), device_id_type=MESH))
        for cp in own + first:
            cp.start()
        for cp in own + first:
            cp.wait()
        acc = land[0]
        for s in range(1, N_DEV):
            acc = acc + land[s]
        red[...] = acc
        mine = pltpu.make_async_copy(red, o_ref.at[rows(me)], local.at[2])
        second = [pltpu.make_async_remote_copy(
            src_ref=red, dst_ref=o_ref.at[rows(me)], send_sem=send2.at[k - 1], recv_sem=recv2.at[k - 1],
            device_id=_peer(k), device_id_type=MESH) for k in range(1, N_DEV)]
        for cp in [mine] + second:
            cp.start()
        for cp in [mine] + second:
            cp.wait()

    any_spec = pl.BlockSpec(memory_space=pl.ANY)
    return pl.pallas_call(
        body, name=name,
        out_shape=[jax.ShapeDtypeStruct((r, c), pack.dtype), jax.ShapeDtypeStruct((N_DEV,) + extra.shape, extra.dtype)],
        in_specs=[any_spec, any_spec], out_specs=[any_spec, any_spec],
        scratch_shapes=[pltpu.VMEM((N_DEV, rs, c), pack.dtype), pltpu.VMEM((rs, c), pack.dtype)]
        + [pltpu.SemaphoreType.DMA((n_peer,))] * 6 + [pltpu.SemaphoreType.DMA((3,))],
        compiler_params=pltpu.CompilerParams(has_side_effects=True),
    )(pack, extra)


GATHER, SCATTER = "gather", "scatter"


def _exchange_copies(srcs, lands, send, recv):
    me = _my_index()
    copies = []
    for a, (src, land) in enumerate(zip(srcs, lands)):
        for k in range(1, N_DEV):
            s = a * (N_DEV - 1) + k - 1
            copies.append(pltpu.make_async_remote_copy(
                src_ref=src.at[jnp.bitwise_xor(me, k)], dst_ref=land.at[me],
                send_sem=send.at[s], recv_sem=recv.at[s], device_id=_peer(k), device_id_type=MESH))
    return copies


def sequencer_exchange(kind, arrs, name, collective_id):
    n = len(arrs)
    n_sem = n * (N_DEV - 1)
    land_shapes = [((N_DEV,) + a.shape if kind == GATHER else a.shape) for a in arrs]
    srcs = [jax.new_ref(a, memory_space=pltpu.MemorySpace.HBM) for a in arrs]
    lands = [jax.empty_ref(jax.ShapeDtypeStruct(s, a.dtype), memory_space=pltpu.MemorySpace.HBM)
             for s, a in zip(land_shapes, arrs)]

    @pl.kernel(mesh=plsc.ScalarSubcoreMesh(axis_name="sequencer", num_cores=1), name=name,
               scratch_types=(pltpu.SemaphoreType.DMA((n_sem,)), pltpu.SemaphoreType.DMA((n_sem,)),
                              pltpu.SemaphoreType.DMA((n,))),
               compiler_params=pltpu.CompilerParams(collective_id=collective_id))
    def launch(send, recv, local):
        barrier = pltpu.get_barrier_semaphore()
        for k in range(1, N_DEV):
            pl.semaphore_signal(barrier, inc=1, device_id=_peer(k), device_id_type=MESH)
        pl.semaphore_wait(barrier, N_DEV - 1)
        me = _my_index()
        mine = [pltpu.make_async_copy(src if kind == GATHER else src.at[me], land.at[me], local.at[a])
                for a, (src, land) in enumerate(zip(srcs, lands))]
        if kind == SCATTER:
            copies = mine + _exchange_copies(srcs, lands, send, recv)
            for cp in copies:
                cp.start()
            for cp in copies:
                cp.wait()
            return

        def block_copy(a, slot, block, k, src=None):
            s = a * (N_DEV - 1) + slot
            return pltpu.make_async_remote_copy(
                src_ref=lands[a].at[block] if src is None else src, dst_ref=lands[a].at[block],
                send_sem=send.at[s], recv_sem=recv.at[s], device_id=_peer(k), device_id_type=MESH)

        chips = (2, 4, 6)
        sibling = jnp.bitwise_xor(me, 1)
        first = [block_copy(a, slot, me, k, src=srcs[a]) for a in range(n) for slot, k in enumerate((1,) + chips)]
        for cp in mine + first:
            cp.start()
        passed = []
        for a in range(n):
            for i, k in enumerate(chips):
                block = jnp.bitwise_xor(me, k)
                block_copy(a, 1 + i, block, k).wait_recv()
                passed.append(block_copy(a, 4 + i, block, 1))
                passed[-1].start()
        for a in range(n):
            block_copy(a, 0, sibling, 1).wait_recv()
            for i, k in enumerate(chips):
                block_copy(a, 4 + i, jnp.bitwise_xor(sibling, k), 1).wait_recv()
        for cp in mine:
            cp.wait()
        for cp in first + passed:
            cp.wait_send()

    launch()
    return [land[...] for land in lands]


def _tile(n, pref):
    for t in pref:
        if n % t == 0:
            return t
    return n


MM_WIDE = 1024
MM_WEIGHT_BLOCK = 8 * 2 ** 20


def _blocks_per_step(nb, fits):
    return max(g for g in range(1, nb + 1) if nb % g == 0 and fits(g))


def mm_nn(a, b3, out_dtype, name, split_cols=None):
    m, k = a.shape
    nb, _, bn = b3.shape
    tm = _tile(m, (512, 256, 128))
    tn = bn // split_cols if split_cols else _tile(bn, (1024, 896, 512, 256, 128))
    per = bn // tn
    gb = _blocks_per_step(nb, lambda g: g == 1 or (per == 1 and g * bn <= MM_WIDE))

    def body(a_ref, b_ref, o_ref):
        for g in range(gb):
            o_ref[:, g * tn:(g + 1) * tn] = _dot(a_ref[...], b_ref[g]).astype(o_ref.dtype)

    if split_cols:
        out_spec = pl.BlockSpec((None, tm, tn), lambda i, j, jj: (jj, i, 0))
        out_shape = jax.ShapeDtypeStruct((split_cols, m, tn), out_dtype)
    else:
        out_spec = pl.BlockSpec((tm, gb * tn), lambda i, j, jj: (i, j * per + jj))
        out_shape = jax.ShapeDtypeStruct((m, nb * bn), out_dtype)
    return pl.pallas_call(
        body, name=name, grid=(m // tm, nb // gb, per),
        in_specs=[pl.BlockSpec((tm, k), lambda i, j, jj: (i, 0)),
                  pl.BlockSpec((gb, k, tn), lambda i, j, jj: (j, 0, jj))],
        out_specs=out_spec, out_shape=out_shape,
        compiler_params=_cparams("parallel", "arbitrary", "arbitrary"),
    )(a, b3)


def mm_nt(a, w3, out_dtype, name):
    m, _ = a.shape
    nb, ko, bn = w3.shape
    tm = _tile(m, (512, 256, 128))
    tko = _tile(ko, (1024, 512, 256, 128))
    gb = _blocks_per_step(nb, lambda g: g * tko * bn * w3.dtype.itemsize <= MM_WEIGHT_BLOCK)
    ns = nb // gb

    def body(a_ref, w_ref, o_ref, acc_ref):
        j = pl.program_id(2)

        @pl.when(j == 0)
        def _():
            acc_ref[...] = jnp.zeros_like(acc_ref)

        part = _dot_nt(a_ref[:, :bn], w_ref[0])
        for g in range(1, gb):
            part += _dot_nt(a_ref[:, g * bn:(g + 1) * bn], w_ref[g])
        acc_ref[...] += part

        @pl.when(j == ns - 1)
        def _():
            o_ref[...] = acc_ref[...].astype(o_ref.dtype)

    return pl.pallas_call(
        body, name=name, grid=(m // tm, ko // tko, ns),
        in_specs=[pl.BlockSpec((tm, gb * bn), lambda i, o, j: (i, j)),
                  pl.BlockSpec((gb, tko, bn), lambda i, o, j: (j, o, 0))],
        out_specs=pl.BlockSpec((tm, tko), lambda i, o, j: (i, o)),
        out_shape=jax.ShapeDtypeStruct((m, ko), out_dtype),
        scratch_shapes=[pltpu.VMEM((tm, tko), f32)],
        compiler_params=_cparams("parallel", "arbitrary", "arbitrary"),
    )(a, w3)


def mm_tn(a, dy, ncb, out_dtype, name):
    l, ka = a.shape
    _, n = dy.shape
    bn = n // ncb
    tl = _tile(l, (1024, 512, 256, 128))
    tka = _tile(ka, (512, 256, 128))
    tn = _tile(bn, (1024, 896, 512, 256, 128))
    per = bn // tn
    gb = _blocks_per_step(ncb, lambda g: g == 1 or (per == 1 and g * bn <= MM_WIDE))
    nl = l // tl

    def body(a_ref, dy_ref, o_ref, acc_ref):
        s = pl.program_id(2)

        @pl.when(s == 0)
        def _():
            acc_ref[...] = jnp.zeros_like(acc_ref)

        acc_ref[...] += _dot_tn(a_ref[...], dy_ref[...])

        @pl.when(s == nl - 1)
        def _():
            for g in range(gb):
                o_ref[g] = acc_ref[:, g * tn:(g + 1) * tn].astype(o_ref.dtype)

    return pl.pallas_call(
        body, name=name, grid=(ka // tka, n // (gb * tn), nl),
        in_specs=[pl.BlockSpec((tl, tka), lambda i, j, s: (s, i)),
                  pl.BlockSpec((tl, gb * tn), lambda i, j, s: (s, j))],
        out_specs=pl.BlockSpec((gb, tka, tn), lambda i, j, s: (j // per, i, j % per)),
        out_shape=jax.ShapeDtypeStruct((ncb, ka, bn), out_dtype),
        scratch_shapes=[pltpu.VMEM((tka, gb * tn), f32)],
        compiler_params=_cparams("parallel", "parallel", "arbitrary"),
    )(a, dy)


def mod_part(c_all, w_mod, b_cols):
    nl, d, cols = w_mod.shape

    def body(c_ref, w_ref, b_ref, o_ref):
        cond = _silu(c_ref[...]).astype(bf16)
        o_ref[...] = _dot(cond, w_ref[...].astype(bf16)) + b_ref[...]

    return pl.pallas_call(
        body, name="mod_part", grid=(nl,),
        in_specs=[pl.BlockSpec((N_DEV, d), lambda l: (0, 0)),
                  pl.BlockSpec((None, d, cols), lambda l: (l, 0, 0)),
                  pl.BlockSpec((None, 1, cols), lambda l: (l, 0, 0))],
        out_specs=pl.BlockSpec((None, N_DEV, cols), lambda l: (l, 0, 0)),
        out_shape=jax.ShapeDtypeStruct((nl, N_DEV, cols), f32),
        compiler_params=_cparams("arbitrary"),
    )(c_all, w_mod, b_cols.reshape(nl, 1, cols))


def _row_tile(l):
    return _tile(l, (256, 128))


def prenorm_fwd(x, g, shift, scale, name):
    l, d = x.shape
    tm = _row_tile(l)

    def body(x_ref, g_ref, sh_ref, sc_ref, h_ref):
        xv = x_ref[...]
        r = lax.rsqrt(jnp.mean(xv * xv, axis=-1, keepdims=True) + EPS)
        h_ref[...] = (xv * r * (g_ref[...] * (1.0 + sc_ref[...])) + sh_ref[...]).astype(h_ref.dtype)

    return pl.pallas_call(
        body, name=name, grid=(l // tm,),
        in_specs=[pl.BlockSpec((tm, d), lambda i: (i, 0)), _row(d), _row(d), _row(d)],
        out_specs=pl.BlockSpec((tm, d), lambda i: (i, 0)),
        out_shape=jax.ShapeDtypeStruct((l, d), bf16),
        compiler_params=_cparams("parallel"),
    )(x, g, shift, scale)


def post_prenorm_fwd(x, y, gate, g_post, g_pre, shift, scale, name):
    l, d = x.shape
    tm = _row_tile(l)

    def body(x_ref, y_ref, gate_ref, gp_ref, g_ref, sh_ref, sc_ref, o_ref, h_ref):
        yv = y_ref[...]
        r = lax.rsqrt(jnp.mean(yv * yv, axis=-1, keepdims=True) + EPS)
        xv = x_ref[...] + gate_ref[...] * (yv * r * gp_ref[...])
        o_ref[...] = xv
        r = lax.rsqrt(jnp.mean(xv * xv, axis=-1, keepdims=True) + EPS)
        h_ref[...] = (xv * r * (g_ref[...] * (1.0 + sc_ref[...])) + sh_ref[...]).astype(h_ref.dtype)

    blk = pl.BlockSpec((tm, d), lambda i: (i, 0))
    return pl.pallas_call(
        body, name=name, grid=(l // tm,),
        in_specs=[blk, blk] + [_row(d)] * 5, out_specs=[blk, blk],
        out_shape=[jax.ShapeDtypeStruct((l, d), f32), jax.ShapeDtypeStruct((l, d), bf16)],
        compiler_params=_cparams("parallel"),
    )(x, y, gate, g_post, g_pre, shift, scale)


def _post_bwd_rows(dxv, yv, r, gate, gv, dy_ref, dgate_ref, dg_ref):
    yn = yv * r
    dgate_ref[...] += jnp.sum(dxv * yn * gv, axis=0, keepdims=True)
    dyg = dxv * gate
    dg_ref[...] += jnp.sum(dyg * yn, axis=0, keepdims=True)
    dyn = dyg * gv
    dy_ref[...] = (r * (dyn - yn * jnp.mean(dyn * yn, axis=-1, keepdims=True))).astype(dy_ref.dtype)


def final_loss(x, y, gate, g, target):
    l, d = x.shape
    tm = _row_tile(l)

    def body(x_ref, y_ref, gate_ref, g_ref, t_ref, dx_ref, loss_ref, dy_ref, dgate_ref, dg_ref):
        @pl.when(pl.program_id(0) == 0)
        def _():
            loss_ref[...] = jnp.zeros_like(loss_ref)
            dgate_ref[...] = jnp.zeros_like(dgate_ref)
            dg_ref[...] = jnp.zeros_like(dg_ref)

        yv, gate, gv = y_ref[...], gate_ref[...], g_ref[...]
        r = lax.rsqrt(jnp.mean(yv * yv, axis=-1, keepdims=True) + EPS)
        diff = x_ref[...] + gate * (yv * r * gv) - t_ref[...]
        dxv = diff * (1.0 / d)
        dx_ref[...] = dxv
        loss_ref[...] += jnp.sum(diff * diff)
        _post_bwd_rows(dxv, yv, r, gate, gv, dy_ref, dgate_ref, dg_ref)

    blk = pl.BlockSpec((tm, d), lambda i: (i, 0))
    return pl.pallas_call(
        body, name="final_loss", grid=(l // tm,),
        in_specs=[blk, blk, _row(d), _row(d), blk],
        out_specs=[blk, pl.BlockSpec((SUBLANES, HEAD), lambda i: (0, 0)), blk, _row(d), _row(d)],
        out_shape=[jax.ShapeDtypeStruct((l, d), f32), jax.ShapeDtypeStruct((SUBLANES, HEAD), f32),
                   jax.ShapeDtypeStruct((l, d), bf16), jax.ShapeDtypeStruct((1, d), f32), jax.ShapeDtypeStruct((1, d), f32)],
        compiler_params=_cparams("arbitrary"),
    )(x, y, gate, g, target)


def post_bwd(dx, y, gate, g, name):
    l, d = dx.shape
    tm = _row_tile(l)

    def body(dx_ref, y_ref, gate_ref, g_ref, dy_ref, dgate_ref, dg_ref):
        @pl.when(pl.program_id(0) == 0)
        def _():
            dgate_ref[...] = jnp.zeros_like(dgate_ref)
            dg_ref[...] = jnp.zeros_like(dg_ref)

        yv = y_ref[...]
        r = lax.rsqrt(jnp.mean(yv * yv, axis=-1, keepdims=True) + EPS)
        _post_bwd_rows(dx_ref[...], yv, r, gate_ref[...], g_ref[...], dy_ref, dgate_ref, dg_ref)

    blk = pl.BlockSpec((tm, d), lambda i: (i, 0))
    return pl.pallas_call(
        body, name=name, grid=(l // tm,),
        in_specs=[blk, blk, _row(d), _row(d)], out_specs=[blk, _row(d), _row(d)],
        out_shape=[jax.ShapeDtypeStruct((l, d), bf16), jax.ShapeDtypeStruct((1, d), f32),
                   jax.ShapeDtypeStruct((1, d), f32)],
        compiler_params=_cparams("arbitrary"),
    )(dx, y, gate, g)


def prenorm_bwd(dh, x, dx_next, g, scale, name):
    l, d = x.shape
    tm = _row_tile(l)

    def body(dh_ref, x_ref, dxn_ref, g_ref, sc_ref, dx_ref, dsh_ref, dsc_ref, dg_ref):
        @pl.when(pl.program_id(0) == 0)
        def _():
            dsh_ref[...] = jnp.zeros_like(dsh_ref)
            dsc_ref[...] = jnp.zeros_like(dsc_ref)
            dg_ref[...] = jnp.zeros_like(dg_ref)

        xv, dhv, gv, sc1 = x_ref[...], dh_ref[...], g_ref[...], 1.0 + sc_ref[...]
        r = lax.rsqrt(jnp.mean(xv * xv, axis=-1, keepdims=True) + EPS)
        xn = xv * r
        dhx = dhv * xn
        dsh_ref[...] += jnp.sum(dhv, axis=0, keepdims=True)
        dsc_ref[...] += jnp.sum(dhx * gv, axis=0, keepdims=True)
        dg_ref[...] += jnp.sum(dhx * sc1, axis=0, keepdims=True)
        dxn = dhv * (gv * sc1)
        dx_ref[...] = dxn_ref[...] + r * (dxn - xn * jnp.mean(dxn * xn, axis=-1, keepdims=True))

    blk = pl.BlockSpec((tm, d), lambda i: (i, 0))
    return pl.pallas_call(
        body, name=name, grid=(l // tm,),
        in_specs=[blk, blk, blk, _row(d), _row(d)], out_specs=[blk, _row(d), _row(d), _row(d)],
        out_shape=[jax.ShapeDtypeStruct((l, d), f32)] + [jax.ShapeDtypeStruct((1, d), f32)] * 3,
        compiler_params=_cparams("arbitrary"),
    )(dh, x, dx_next, g, scale)


def _tril_mask():
    r = lax.broadcasted_iota(jnp.int32, (HEAD, HEAD), 0)
    c = lax.broadcasted_iota(jnp.int32, (HEAD, HEAD), 1)
    return r >= c


def sgu_fwd(proj, norm_g, w_s, b_s):
    l = proj.shape[0]
    nh = w_s.shape[0]
    wa = nh * HEAD

    def body(au_ref, av_ref, az_ref, ng_ref, w_ref, b_ref, o_ref):
        tril = _tril_mask()
        for h in range(nh):
            sl = slice(h * HEAD, (h + 1) * HEAD)
            gv = _gelu(av_ref[:, sl].astype(f32))
            r = lax.rsqrt(jnp.mean(gv * gv, axis=-1, keepdims=True) + EPS)
            vh = gv * r * ng_ref[:, sl]
            wm = jnp.where(tril, w_ref[h], 0.0).astype(bf16)
            s = _dot(wm, vh.astype(bf16)) + b_ref[h]
            o_ref[:, sl] = (_gelu(au_ref[:, sl].astype(f32)) * s * _silu(az_ref[:, sl].astype(f32))).astype(o_ref.dtype)

    def col(j):
        return pl.BlockSpec((HEAD, wa), lambda n: (n, j))

    return pl.pallas_call(
        body, name="sgu_fwd", grid=(l // HEAD,),
        in_specs=[col(0), col(1), col(2), _row(wa),
                  pl.BlockSpec((nh, HEAD, HEAD), lambda n: (0, 0, 0)), pl.BlockSpec((nh, HEAD, 1), lambda n: (0, 0, 0))],
        out_specs=pl.BlockSpec((HEAD, wa), lambda n: (n, 0)),
        out_shape=jax.ShapeDtypeStruct((l, 2 * wa), bf16),
        compiler_params=_cparams("parallel"),
    )(proj, proj, proj, norm_g, w_s, b_s)


def sgu_bwd(proj, dcat, norm_g, w_s, b_s):
    l = proj.shape[0]
    nh = w_s.shape[0]
    wa = nh * HEAD

    def body(au_ref, av_ref, az_ref, do_ref, ng_ref, w_ref, b_ref, da_ref, dw_ref, db_ref, dng_ref):
        @pl.when(pl.program_id(0) == 0)
        def _():
            dw_ref[...] = jnp.zeros_like(dw_ref)
            db_ref[...] = jnp.zeros_like(db_ref)
            dng_ref[...] = jnp.zeros_like(dng_ref)

        tril = _tril_mask()
        for h in range(nh):
            sl = slice(h * HEAD, (h + 1) * HEAD)
            au, av, az = au_ref[:, sl].astype(f32), av_ref[:, sl].astype(f32), az_ref[:, sl].astype(f32)
            ng = ng_ref[:, sl]
            gv = _gelu(av)
            r = lax.rsqrt(jnp.mean(gv * gv, axis=-1, keepdims=True) + EPS)
            gvn = gv * r
            vh = (gvn * ng).astype(bf16)
            wm = jnp.where(tril, w_ref[h], 0.0).astype(bf16)
            s = _dot(wm, vh) + b_ref[h]
            gu, sz = _gelu(au), _silu(az)
            dov = do_ref[:, sl].astype(f32)
            da_ref[:, sl] = (dov * s * sz * _gelu_grad(au)).astype(da_ref.dtype)
            da_ref[:, 2 * wa + h * HEAD:2 * wa + (h + 1) * HEAD] = (dov * gu * s * _silu_grad(az)).astype(da_ref.dtype)
            ds = dov * gu * sz
            db_ref[h] += jnp.sum(ds, axis=-1, keepdims=True)
            dsb = ds.astype(bf16)
            dw_ref[h] += jnp.where(tril, _dot_nt(dsb, vh), 0.0)
            dvh = _dot_tn(wm, dsb)
            dng_ref[:, sl] += jnp.sum(dvh * gvn, axis=0, keepdims=True)
            dgvn = dvh * ng
            dgv = r * (dgvn - gvn * jnp.mean(dgvn * gvn, axis=-1, keepdims=True))
            da_ref[:, wa + h * HEAD:wa + (h + 1) * HEAD] = (dgv * _gelu_grad(av)).astype(da_ref.dtype)

    def col(j):
        return pl.BlockSpec((HEAD, wa), lambda n: (n, j))

    whole_w = pl.BlockSpec((nh, HEAD, HEAD), lambda n: (0, 0, 0))
    whole_b = pl.BlockSpec((nh, HEAD, 1), lambda n: (0, 0, 0))
    return pl.pallas_call(
        body, name="sgu_bwd", grid=(l // HEAD,),
        in_specs=[col(0), col(1), col(2), col(0), _row(wa), whole_w, whole_b],
        out_specs=[pl.BlockSpec((HEAD, 3 * wa), lambda n: (n, 0)), whole_w, whole_b, _row(wa)],
        out_shape=[jax.ShapeDtypeStruct((l, 3 * wa), bf16), jax.ShapeDtypeStruct((nh, HEAD, HEAD), f32),
                   jax.ShapeDtypeStruct((nh, HEAD, 1), f32), jax.ShapeDtypeStruct((1, wa), f32)],
        compiler_params=_cparams("arbitrary"),
    )(proj, proj, proj, dcat, norm_g, w_s, b_s)


_LOG2E = 1.0 / math.log(2.0)


def _sb_scores(q, k, scale):
    z = _dot_nt(q, k) * (scale * _LOG2E)
    return z, jnp.maximum(z, 0.0) + jnp.log2(1.0 + jnp.exp2(-jnp.abs(z)))


SB_KEYS = 256


def _sb_sum_matrix(tri, kb):
    s = lax.broadcasted_iota(jnp.int32, (2 * kb, kb + HEAD), 0) % kb
    j = lax.broadcasted_iota(jnp.int32, (2 * kb, kb + HEAD), 1)
    return jnp.where(jnp.logical_or(j >= kb, tri(s, j)), 1.0, 0.0).astype(bf16)


def _sb_sums(x, sums):
    kb = x.shape[1]
    c2 = _dot(jnp.concatenate(_split_bf16(x), axis=1), sums)
    return c2[:, :kb], c2[:, kb:]


def _sb_wide(v, kb):
    return jnp.concatenate([v] * (kb // HEAD), axis=1) if kb > HEAD else v


def _sb_q_tile(l, most=512):
    return _tile(l, tuple(t for t in (1024, 512, 256, 128) if t <= most))


def _sb_band_levels(band):
    return _tile(band, (4, 2, 1))


def _sb_heads_per_step(nh, most):
    return _tile(nh, tuple(h for h in (4, 2) if h <= most))


def sb_fwd(proj, mixed, nh):
    l = proj.shape[0]
    wb = nh * HEAD
    tq = _sb_q_tile(l, 1024)
    kb = min(SB_KEYS, tq)
    band = tq // kb
    hp = _sb_heads_per_step(nh, 2)
    levels = _sb_band_levels(band)
    scale = 1.0 / math.sqrt(HEAD)
    qc, kc, vc, zc = 3 * nh, 4 * nh, 5 * nh, 6 * nh

    def body(q_ref, k_ref, v_ref, bz_ref, mixed_ref, o_ref, att_ref, tot_ref):
        del mixed_ref
        i = pl.program_id(1)
        sums = _sb_sum_matrix(lambda s, j: s > j, kb)
        t_pos = i * tq + lax.broadcasted_iota(jnp.int32, (tq, kb), 0)
        s_off = lax.broadcasted_iota(jnp.int32, (tq, kb), 1)

        def step(j, carry, masked, row0=0):
            rows = pl.ds(pl.multiple_of(j * kb, kb), kb)
            out = []
            for e in range(hp):
                acc, tot = carry[e]
                sl = slice(e * HEAD, (e + 1) * HEAD)
                z, sp = _sb_scores(q_ref[row0:, sl], k_ref[rows, sl], scale)
                lb = z - sp
                if masked:
                    mask = s_off[row0:] + j * kb < t_pos[row0:]
                    sp = jnp.where(mask, sp, 0.0)
                later, total = _sb_sums(sp, sums)
                w = jnp.exp2(lb + _sb_wide(tot[row0:], kb) - later)
                if masked:
                    w = jnp.where(mask, w, 0.0)
                new = (acc[row0:] + _dot(w.astype(bf16), v_ref[rows, sl]), tot[row0:] - total)
                out.append(tuple(jnp.concatenate([old[:row0], upd]) if row0 else upd for old, upd in zip(carry[e], new)))
            return tuple(out)

        zero = jnp.zeros((tq, HEAD), f32)
        carry = ((zero, zero),) * hp
        for lv in reversed(range(levels)):
            carry = lax.fori_loop(
                0, band // levels,
                lambda t, c, lv=lv: step(band * i + (lv + 1) * (band // levels) - 1 - t, c, True, lv * (tq // levels)), carry)
        carry = lax.fori_loop(0, band * i, lambda t, c: step(band * i - 1 - t, c, False), carry)
        for e in range(hp):
            acc, tot = carry[e]
            sl = slice(e * HEAD, (e + 1) * HEAD)
            att_ref[:, sl] = acc.astype(att_ref.dtype)
            o_ref[:, sl] = (acc * _silu(bz_ref[:, sl].astype(f32))).astype(o_ref.dtype)
            tot_ref[e] = tot[:, :1]

    blk = lambda c0: pl.BlockSpec((tq, hp * HEAD), lambda g, i: (i, c0 // hp + g))
    head = lambda c0: pl.BlockSpec((l, hp * HEAD), lambda g, i: (0, c0 // hp + g))
    return pl.pallas_call(
        body, name="sb_fwd", grid=(nh // hp, l // tq),
        in_specs=[blk(qc), head(kc), head(vc), blk(zc), pl.BlockSpec(memory_space=pl.ANY)],
        out_specs=[blk(mixed.shape[1] // HEAD - nh), blk(0), pl.BlockSpec((hp, tq, 1), lambda g, i: (g, i, 0))],
        out_shape=[jax.ShapeDtypeStruct(mixed.shape, bf16), jax.ShapeDtypeStruct((l, wb), bf16),
                   jax.ShapeDtypeStruct((nh, l, 1), f32)],
        input_output_aliases={4: 0},
        compiler_params=_cparams("parallel", "arbitrary"),
    )(proj, proj, proj, proj, mixed)


def sb_bwd(proj, dcat, att, tot, nh):
    l = proj.shape[0]
    wb = nh * HEAD
    tq = _sb_q_tile(l, 1024)
    kb = min(SB_KEYS, tq)
    band = tq // kb
    nq = l // tq
    hp = _sb_heads_per_step(nh, 2)
    levels = _sb_band_levels(band)
    scale = 1.0 / math.sqrt(HEAD)
    qc, kc, vc, zc = 3 * nh, 4 * nh, 5 * nh, 6 * nh

    def body(q_ref, k_ref, v_ref, bz_ref, do_ref, att_ref, tot_ref, dq_ref, dk_ref, dv_ref, dbz_ref, dk_acc, dv_acc,
             dob_ref):
        i = pl.program_id(1)

        @pl.when(i == 0)
        def _():
            dk_acc[...] = jnp.zeros_like(dk_acc)
            dv_acc[...] = jnp.zeros_like(dv_acc)

        bz = bz_ref[...].astype(f32)
        dov = do_ref[...].astype(f32)
        dbz_ref[...] = (dov * att_ref[...].astype(f32) * _silu_grad(bz)).astype(dbz_ref.dtype)
        dob_ref[...] = (dov * _silu(bz)).astype(bf16)
        upto = _sb_sum_matrix(lambda s, j: s <= j, kb)
        before = _sb_sum_matrix(lambda j, s: j < s, kb)
        t_pos = i * tq + lax.broadcasted_iota(jnp.int32, (tq, kb), 0)
        s_off = lax.broadcasted_iota(jnp.int32, (tq, kb), 1)

        def step(j, carry, masked, row0=0):
            rows = pl.ds(pl.multiple_of(j * kb, kb), kb)
            out = []
            for h in range(hp):
                dq, sp_seen, e_seen = (c[row0:] for c in carry[h])
                sl = slice(h * HEAD, (h + 1) * HEAD)
                q, kj, vj, dob = q_ref[row0:, sl], k_ref[rows, sl], v_ref[rows, sl], dob_ref[row0:, sl]
                z, sp = _sb_scores(q, kj, scale)
                lb = z - sp
                if masked:
                    mask = s_off[row0:] + j * kb < t_pos[row0:]
                    sp = jnp.where(mask, sp, 0.0)
                sp_upto, sp_total = _sb_sums(sp, upto)
                w = jnp.exp2(lb + _sb_wide(sp_seen, kb) + sp_upto)
                if masked:
                    w = jnp.where(mask, w, 0.0)
                dv_acc[rows, sl] += _dot_tn(w.astype(bf16), dob)
                e = _dot_nt(dob, vj) * w
                e_before, e_total = _sb_sums(e, before)
                dz = (e - (e + _sb_wide(e_seen, kb) + e_before) * jnp.exp2(lb)) * scale
                if masked:
                    dz = jnp.where(mask, dz, 0.0)
                dz = dz.astype(bf16)
                dk_acc[rows, sl] += _dot_tn(dz, q)
                new = (dq + _dot(dz, kj), sp_seen + sp_total, e_seen + e_total)
                out.append(tuple(jnp.concatenate([old[:row0], upd]) if row0 else upd for old, upd in zip(carry[h], new)))
            return tuple(out)

        zero = jnp.zeros((tq, HEAD), f32)
        init = tuple((zero, jnp.broadcast_to(tot_ref[h], (tq, HEAD)), zero) for h in range(hp))
        carry = lax.fori_loop(0, band * i, lambda j, c: step(j, c, False), init)
        for lv in range(levels):
            carry = lax.fori_loop(
                0, band // levels,
                lambda t, c, lv=lv: step(band * i + lv * (band // levels) + t, c, True, lv * (tq // levels)), carry)
        for h in range(hp):
            dq_ref[:, h * HEAD:(h + 1) * HEAD] = carry[h][0].astype(dq_ref.dtype)

        @pl.when(i == nq - 1)
        def _():
            dk_ref[...] = dk_acc[...].astype(dk_ref.dtype)
            dv_ref[...] = dv_acc[...].astype(dv_ref.dtype)

    blk = lambda c0: pl.BlockSpec((tq, hp * HEAD), lambda g, i: (i, c0 // hp + g))
    head = lambda c0: pl.BlockSpec((l, hp * HEAD), lambda g, i: (0, c0 // hp + g))
    return pl.pallas_call(
        body, name="sb_bwd", grid=(nh // hp, nq),
        in_specs=[blk(qc), head(kc), head(vc), blk(zc), blk(nh), blk(0),
                  pl.BlockSpec((hp, tq, 1), lambda g, i: (g, i, 0))],
        out_specs=[blk(0), head(0), head(0), blk(0)],
        out_shape=[jax.ShapeDtypeStruct((l, wb), bf16)] * 4,
        scratch_shapes=[pltpu.VMEM((l, hp * HEAD), f32), pltpu.VMEM((l, hp * HEAD), f32),
                        pltpu.VMEM((tq, hp * HEAD), bf16)],
        compiler_params=_cparams("parallel", "arbitrary"),
    )(proj, proj, proj, proj, dcat, att, tot)


def _disc(lr, li, ldt):
    dt = jnp.exp(ldt)
    mag = jnp.exp(lr * dt)
    a_re = mag * jnp.cos(li * dt)
    a_im = mag * jnp.sin(li * dt)
    den = lr * lr + li * li
    nr = a_re - 1.0
    return a_re, a_im, (nr * lr + a_im * li) / den, (a_im * lr - nr * li) / den


def s5_params_fwd(lr, li, ldt, bt_re, bt_im):
    g, c, p = bt_re.shape

    def body(lr_ref, li_ref, ldt_ref, br_ref, bi_ref, ar_ref, ai_ref, bbr_ref, bbi_ref):
        a_re, a_im, cr, ci = _disc(lr_ref[...], li_ref[...], ldt_ref[...])
        ar_ref[...] = a_re
        ai_ref[...] = a_im
        for k in range(c):
            br, bi = br_ref[:, k, :], bi_ref[:, k, :]
            bbr_ref[:, k, :] = cr * br - ci * bi
            bbi_ref[:, k, :] = cr * bi + ci * br

    return pl.pallas_call(
        body, name="s5_params_fwd",
        out_shape=[jax.ShapeDtypeStruct((g, p), f32)] * 2 + [jax.ShapeDtypeStruct((g, c, p), f32)] * 2,
    )(lr, li, ldt, bt_re, bt_im)


def s5_params_bwd(lr, li, ldt, bt_re, bt_im, da_re, da_im, dbbt_re, dbbt_im):
    g, c, p = bt_re.shape

    def body(lr_ref, li_ref, ldt_ref, br_ref, bi_ref, dar_ref, dai_ref, dbbr_ref, dbbi_ref,
             dlr_ref, dli_ref, dldt_ref, dbr_ref, dbi_ref):
        (a_re, a_im, cr, ci), vjp = jax.vjp(_disc, lr_ref[...], li_ref[...], ldt_ref[...])
        dcr = jnp.zeros((g, p), f32)
        dci = jnp.zeros((g, p), f32)
        for k in range(c):
            br, bi = br_ref[:, k, :], bi_ref[:, k, :]
            dr, di = dbbr_ref[:, k, :], dbbi_ref[:, k, :]
            dcr += dr * br + di * bi
            dci += di * br - dr * bi
            dbr_ref[:, k, :] = cr * dr + ci * di
            dbi_ref[:, k, :] = cr * di - ci * dr
        dlr, dli, dldt = vjp((dar_ref[...], dai_ref[...], dcr, dci))
        dlr_ref[...] = dlr
        dli_ref[...] = dli
        dldt_ref[...] = dldt

    return pl.pallas_call(
        body, name="s5_params_bwd",
        out_shape=[jax.ShapeDtypeStruct((g, p), f32)] * 2 + [jax.ShapeDtypeStruct((g, 1), f32)]
        + [jax.ShapeDtypeStruct((g, c, p), f32)] * 2,
    )(lr, li, ldt, bt_re, bt_im, da_re, da_im, dbbt_re, dbbt_im)


def _cmul(ar, ai, br, bi):
    return ar * br - ai * bi, ar * bi + ai * br


def _power_tables(ar, ai):
    rows = lax.broadcasted_iota(jnp.int32, (SUBLANES, ar.shape[1]), 0)
    pr = jnp.zeros((SUBLANES, ar.shape[1]), f32)
    pi = jnp.zeros((SUBLANES, ar.shape[1]), f32)
    cr, ci = ar, ai
    pows = {}
    for r in range(SUBLANES):
        pows[r + 1] = (cr, ci)
        pr = jnp.where(rows == r, cr, pr)
        pi = jnp.where(rows == r, ci, pi)
        cr, ci = _cmul(cr, ci, ar, ai)
    return [pows[1], pows[2], pows[4]], pr, pi


def _ssm_time_tile(l):
    return _tile(l, (512, 256, 128))


def ssm_fwd(u, bre3, bim3, cre3, cimn3, a_re, a_im, d_skip):
    l, w = u.shape[0], d_skip.shape[1]
    nj = w // HEAD
    ns = STATES_PER_LANE_BLOCK
    tt = _ssm_time_tile(l)

    def body(u_ref, bre_ref, bim_ref, cre_ref, cim_ref, ar_ref, ai_ref, d_ref, y_ref, hr_ref, hi_ref, cr_ref, ci_ref):
        @pl.when(pl.program_id(1) == 0)
        def _():
            cr_ref[...] = jnp.zeros_like(cr_ref)
            ci_ref[...] = jnp.zeros_like(ci_ref)

        uv = u_ref[...]
        hr_ref[...] = _dot(uv, bre_ref[...])
        hi_ref[...] = _dot(uv, bim_ref[...])
        steps, pr, pi = _power_tables(ar_ref[...], ai_ref[...])
        rows = lax.broadcasted_iota(jnp.int32, (SUBLANES, ns), 0)
        steps = [(jnp.where(rows >= d, sr_, 0.0), jnp.where(rows >= d, si_, 0.0)) for d, (sr_, si_) in zip((1, 2, 4), steps)]

        def blk(b, carry):
            cr, ci = carry
            sl = pl.ds(pl.multiple_of(b * SUBLANES, SUBLANES), SUBLANES)
            xr, xi = hr_ref[sl, :], hi_ref[sl, :]
            for d, (sr_, si_) in zip((1, 2, 4), steps):
                mr, mi = _cmul(sr_, si_, pltpu.roll(xr, d, axis=0), pltpu.roll(xi, d, axis=0))
                xr, xi = xr + mr, xi + mi
            mr, mi = _cmul(pr, pi, cr, ci)
            xr, xi = xr + mr, xi + mi
            hr_ref[sl, :] = xr
            hi_ref[sl, :] = xi
            return xr[SUBLANES - 1:, :], xi[SUBLANES - 1:, :]

        cr, ci = lax.fori_loop(0, tt // SUBLANES, blk, (cr_ref[...], ci_ref[...]))
        cr_ref[...] = cr
        ci_ref[...] = ci
        y = _dot(hr_ref[...].astype(bf16), cre_ref[...]) + _dot(hi_ref[...].astype(bf16), cim_ref[...])
        y_ref[...] = y + d_ref[...] * uv.astype(f32)

    lane = pl.BlockSpec((tt, HEAD), lambda j, i: (i, j))
    st = pl.BlockSpec((tt, ns), lambda j, i: (i, j))
    b3 = pl.BlockSpec((None, HEAD, ns), lambda j, i: (j, 0, 0))
    c3 = pl.BlockSpec((None, ns, HEAD), lambda j, i: (j, 0, 0))
    arow = pl.BlockSpec((1, ns), lambda j, i: (0, j))
    return pl.pallas_call(
        body, name="ssm_fwd", grid=(nj, l // tt),
        in_specs=[lane, b3, b3, c3, c3, arow, arow, pl.BlockSpec((1, HEAD), lambda j, i: (0, j))],
        out_specs=[lane, st, st],
        out_shape=[jax.ShapeDtypeStruct((l, w), f32), jax.ShapeDtypeStruct((l, nj * ns), f32),
                   jax.ShapeDtypeStruct((l, nj * ns), f32)],
        scratch_shapes=[pltpu.VMEM((1, ns), f32), pltpu.VMEM((1, ns), f32)],
        compiler_params=_cparams("parallel", "arbitrary"),
    )(u, bre3, bim3, cre3, cimn3, a_re, a_im, d_skip)


def ssm_bwd(dy, u, dproj, h_re, h_im, bre3, bim3, cre3, cimn3, a_re, a_im, d_skip):
    l, w = u.shape[0], d_skip.shape[1]
    nj = w // HEAD
    ns = STATES_PER_LANE_BLOCK
    tt = _ssm_time_tile(l)
    nt = l // tt

    def body(dy_ref, u_ref, dproj_ref, hr_ref, hi_ref, bre_ref, bim_ref, cre_ref, cim_ref, ar_ref, ai_ref, d_ref,
             du_ref, dd_ref, dar_ref, dai_ref, dbre_ref, dbim_ref, dcre_ref, dcim_ref, kr_ref, ki_ref, cr_ref, ci_ref,
             accr_ref, acci_ref):
        del dproj_ref
        i = pl.program_id(1)

        @pl.when(i == 0)
        def _():
            for ref in (cr_ref, ci_ref, accr_ref, acci_ref, dd_ref, dbre_ref, dbim_ref, dcre_ref, dcim_ref):
                ref[...] = jnp.zeros_like(ref)

        dyv = dy_ref[...]
        dyb = dyv.astype(bf16)
        uv = u_ref[...]
        kr_ref[...] = _dot_nt(dyb, cre_ref[...])
        ki_ref[...] = _dot_nt(dyb, cim_ref[...])
        steps, pr, pi = _power_tables(ar_ref[...], -ai_ref[...])
        rows = lax.broadcasted_iota(jnp.int32, (SUBLANES, ns), 0)
        qr = jnp.zeros((SUBLANES, ns), f32)
        qi = jnp.zeros((SUBLANES, ns), f32)
        for r in range(SUBLANES):
            qr = jnp.where(rows == r, pr[SUBLANES - 1 - r:SUBLANES - r, :], qr)
            qi = jnp.where(rows == r, pi[SUBLANES - 1 - r:SUBLANES - r, :], qi)
        nb = tt // SUBLANES
        steps = [(jnp.where(rows < SUBLANES - d, sr_, 0.0), jnp.where(rows < SUBLANES - d, si_, 0.0))
                 for d, (sr_, si_) in zip((1, 2, 4), steps)]

        def blk(t, carry):
            cr, ci, accr, acci = carry
            sl = pl.ds(pl.multiple_of((nb - 1 - t) * SUBLANES, SUBLANES), SUBLANES)
            xr, xi = kr_ref[sl, :], ki_ref[sl, :]
            for d, (sr_, si_) in zip((1, 2, 4), steps):
                mr, mi = _cmul(sr_, si_, pltpu.roll(xr, SUBLANES - d, axis=0), pltpu.roll(xi, SUBLANES - d, axis=0))
                xr, xi = xr + mr, xi + mi
            mr, mi = _cmul(qr, qi, cr, ci)
            xr, xi = xr + mr, xi + mi
            kr_ref[sl, :] = xr
            ki_ref[sl, :] = xi
            last = rows == SUBLANES - 1
            nr = jnp.where(last, cr, pltpu.roll(xr, SUBLANES - 1, axis=0))
            ni = jnp.where(last, ci, pltpu.roll(xi, SUBLANES - 1, axis=0))
            hr, hi = hr_ref[sl, :], hi_ref[sl, :]
            accr = accr + nr * hr + ni * hi
            acci = acci + ni * hr - nr * hi
            return xr[:1, :], xi[:1, :], accr, acci

        cr, ci, accr, acci = lax.fori_loop(0, nb, blk, (cr_ref[...], ci_ref[...], accr_ref[...], acci_ref[...]))
        cr_ref[...] = cr
        ci_ref[...] = ci
        accr_ref[...] = accr
        acci_ref[...] = acci
        kr, ki = kr_ref[...].astype(bf16), ki_ref[...].astype(bf16)
        du = _dot_nt(kr, bre_ref[...]) + _dot_nt(ki, bim_ref[...]) + d_ref[...] * dyv
        du_ref[...] = du.astype(du_ref.dtype)
        dd_ref[...] += jnp.sum(dyv * uv.astype(f32), axis=0, keepdims=True)
        dbre_ref[...] += _dot_tn(uv, kr)
        dbim_ref[...] += _dot_tn(uv, ki)
        dcre_ref[...] += _dot_tn(hr_ref[...].astype(bf16), dyb)
        dcim_ref[...] += _dot_tn(hi_ref[...].astype(bf16), dyb)

        @pl.when(i == nt - 1)
        def _():
            dar_ref[...] = jnp.sum(accr_ref[...], axis=0, keepdims=True)
            dai_ref[...] = jnp.sum(acci_ref[...], axis=0, keepdims=True)

    lane = pl.BlockSpec((tt, HEAD), lambda j, i: (nt - 1 - i, j))
    st = pl.BlockSpec((tt, ns), lambda j, i: (nt - 1 - i, j))
    b3 = pl.BlockSpec((None, HEAD, ns), lambda j, i: (j, 0, 0))
    c3 = pl.BlockSpec((None, ns, HEAD), lambda j, i: (j, 0, 0))
    arow = pl.BlockSpec((1, ns), lambda j, i: (0, j))
    drow = pl.BlockSpec((1, HEAD), lambda j, i: (0, j))
    return pl.pallas_call(
        body, name="ssm_bwd", grid=(nj, nt),
        in_specs=[lane, lane, pl.BlockSpec(memory_space=pl.ANY), st, st, b3, b3, c3, c3, arow, arow, drow],
        out_specs=[lane, drow, arow, arow, b3, b3, c3, c3], input_output_aliases={2: 0},
        out_shape=[jax.ShapeDtypeStruct(dproj.shape, bf16), jax.ShapeDtypeStruct((1, w), f32),
                   jax.ShapeDtypeStruct((1, nj * ns), f32), jax.ShapeDtypeStruct((1, nj * ns), f32),
                   jax.ShapeDtypeStruct((nj, HEAD, ns), f32), jax.ShapeDtypeStruct((nj, HEAD, ns), f32),
                   jax.ShapeDtypeStruct((nj, ns, HEAD), f32), jax.ShapeDtypeStruct((nj, ns, HEAD), f32)],
        scratch_shapes=[pltpu.VMEM((tt, ns), f32), pltpu.VMEM((tt, ns), f32), pltpu.VMEM((1, ns), f32),
                        pltpu.VMEM((1, ns), f32), pltpu.VMEM((SUBLANES, ns), f32), pltpu.VMEM((SUBLANES, ns), f32)],
        compiler_params=_cparams("parallel", "arbitrary"),
    )(dy, u, dproj, h_re, h_im, bre3, bim3, cre3, cimn3, a_re, a_im, d_skip)


def glu_fwd(y, z_src, w_glu, b_glu):
    l, w = y.shape
    tm = _row_tile(l)

    def body(y_ref, z_ref, w_ref, b_ref, g_ref, t_ref, o_ref):
        g = _gelu(y_ref[...])
        gb = g.astype(bf16)
        t = _dot(gb, w_ref[...]) + b_ref[...]
        g_ref[...] = gb
        t_ref[...] = t
        o_ref[...] = (g * jax.nn.sigmoid(t) * _silu(z_ref[...].astype(f32))).astype(o_ref.dtype)

    blk = pl.BlockSpec((tm, w), lambda i: (i, 0))
    return pl.pallas_call(
        body, name="glu_fwd", grid=(l // tm,),
        in_specs=[blk, pl.BlockSpec((tm, w), lambda i: (i, 1)), pl.BlockSpec((w, w), lambda i: (0, 0)), _row(w)],
        out_specs=[blk, blk, blk],
        out_shape=[jax.ShapeDtypeStruct((l, w), bf16), jax.ShapeDtypeStruct((l, w), f32),
                   jax.ShapeDtypeStruct((l, w), bf16)],
        compiler_params=_cparams("parallel"),
    )(y, z_src, w_glu, b_glu)


def glu_bwd(dout, y, t, z_src, w_glu):
    l, w = y.shape
    tm = _row_tile(l)

    def body(do_ref, y_ref, t_ref, z_ref, w_ref, dy_ref, dz_ref, dt_ref, db_ref):
        @pl.when(pl.program_id(0) == 0)
        def _():
            db_ref[...] = jnp.zeros_like(db_ref)

        yv, zv, dov = y_ref[...], z_ref[...].astype(f32), do_ref[...]
        g = _gelu(yv)
        sg = jax.nn.sigmoid(t_ref[...])
        dy2 = dov * _silu(zv)
        dz_ref[...] = (dov * g * sg * _silu_grad(zv)).astype(dz_ref.dtype)
        dt = dy2 * g * sg * (1.0 - sg)
        dtb = dt.astype(bf16)
        dt_ref[...] = dtb
        db_ref[...] += jnp.sum(dt, axis=0, keepdims=True)
        dg = dy2 * sg + _dot_nt(dtb, w_ref[...])
        dy_ref[...] = dg * _gelu_grad(yv)

    blk = pl.BlockSpec((tm, w), lambda i: (i, 0))
    return pl.pallas_call(
        body, name="glu_bwd", grid=(l // tm,),
        in_specs=[blk, blk, blk, pl.BlockSpec((tm, w), lambda i: (i, 1)), pl.BlockSpec((w, w), lambda i: (0, 0))],
        out_specs=[blk, pl.BlockSpec((tm, w), lambda i: (i, 1)), blk, _row(w)],
        out_shape=[jax.ShapeDtypeStruct((l, w), f32), jax.ShapeDtypeStruct((l, 2 * w), bf16),
                   jax.ShapeDtypeStruct((l, w), bf16), jax.ShapeDtypeStruct((1, w), f32)],
        compiler_params=_cparams("arbitrary"),
    )(dout, y, t, z_src, w_glu)


def _adamw(w, g, m, v):
    m = ADAM_B1 * m + (1.0 - ADAM_B1) * g
    v = ADAM_B2 * v + (1.0 - ADAM_B2) * (g * g)
    m_hat = m / (1.0 - ADAM_B1 ** ADAM_STEP)
    v_hat = v / (1.0 - ADAM_B2 ** ADAM_STEP)
    return -ADAM_LR * (m_hat / (jnp.sqrt(v_hat) + ADAM_EPS) + ADAM_WD * w), m, v


def adam_reduce(pieces, w, m, v, name):
    r, c = w.shape
    n = pieces.shape[0]
    tr = _tile(r, (256, 128, 64, 32, 16, 8))

    def body(p_ref, w_ref, m_ref, v_ref, g_ref, d_ref, nm_ref, nv_ref):
        g = p_ref[0].astype(f32)
        for s in range(1, n):
            g = g + p_ref[s].astype(f32)
        g_ref[...] = g
        d_ref[...], nm_ref[...], nv_ref[...] = _adamw(w_ref[...], g, m_ref[...], v_ref[...])

    blk = pl.BlockSpec((tr, c), lambda i: (i, 0))
    return pl.pallas_call(
        body, name=name, grid=(r // tr,),
        in_specs=[pl.BlockSpec((n, tr, c), lambda i: (0, i, 0)), blk, blk, blk],
        out_specs=[blk] * 4, out_shape=[jax.ShapeDtypeStruct((r, c), f32)] * 4,
        compiler_params=_cparams("parallel"),
    )(pieces, w, m, v)


def adam_w_mod(cond_t, dm, w, m, v):
    nl, d, cols = w.shape
    tr = _tile(d, (512, 256, 128))

    def body(c_ref, dm_ref, w_ref, m_ref, v_ref, g_ref, d_ref, nm_ref, nv_ref):
        g = jnp.dot(c_ref[...], dm_ref[...], preferred_element_type=f32, precision=lax.Precision.HIGHEST)
        g_ref[...] = g
        d_ref[...], nm_ref[...], nv_ref[...] = _adamw(w_ref[...], g, m_ref[...], v_ref[...])

    blk = pl.BlockSpec((None, tr, cols), lambda l, i: (l, i, 0))
    return pl.pallas_call(
        body, name="adam_w_mod", grid=(nl, d // tr),
        in_specs=[pl.BlockSpec((tr, N_DEV), lambda l, i: (i, 0)), pl.BlockSpec((None, N_DEV, cols), lambda l, i: (l, 0, 0)),
                  blk, blk, blk],
        out_specs=[blk] * 4, out_shape=[jax.ShapeDtypeStruct((nl, d, cols), f32)] * 4,
        compiler_params=_cparams("parallel", "parallel"),
    )(cond_t, dm, w, m, v)


def silu_rows(c_all):
    def body(c_ref, o_ref):
        o_ref[...] = _silu(c_ref[...])

    return pl.pallas_call(body, name="silu_rows", out_shape=jax.ShapeDtypeStruct(c_all.shape, f32))(c_all)


def _block_diag(x):
    g, a, b = x.shape
    nj = g // GROUPS_PER_LANE_BLOCK
    eye = jnp.eye(GROUPS_PER_LANE_BLOCK, dtype=x.dtype)
    x5 = x.reshape(nj, GROUPS_PER_LANE_BLOCK, a, b)
    return jnp.einsum("jgab,gh->jgahb", x5, eye).reshape(nj, GROUPS_PER_LANE_BLOCK * a, GROUPS_PER_LANE_BLOCK * b)


def _diag_blocks(x, a, b):
    nj = x.shape[0]
    x5 = x.reshape(nj, GROUPS_PER_LANE_BLOCK, a, GROUPS_PER_LANE_BLOCK, b)
    eye = jnp.eye(GROUPS_PER_LANE_BLOCK, dtype=x.dtype)
    return jnp.einsum("jgahb,gh->jgab", x5, eye).reshape(nj * GROUPS_PER_LANE_BLOCK, a, b)


PACK_ROW = SUBLANES * HEAD


def _pack(parts, row_multiple=SUBLANES):
    rows = []
    for p in parts:
        flat = p.reshape(-1)
        pad = (-flat.shape[0]) % PACK_ROW
        if pad:
            flat = jnp.concatenate([flat, jnp.zeros((pad,), flat.dtype)])
        rows.append(flat.reshape(-1, HEAD))
    pad = (-sum(r.shape[0] for r in rows)) % row_multiple
    if pad:
        rows.append(jnp.zeros((pad, HEAD), rows[0].dtype))
    return jnp.concatenate(rows, axis=0)


def _unpack(packed, shapes):
    out, r0 = [], 0
    for shp in shapes:
        n = math.prod(shp)
        nr = -(-n // PACK_ROW) * SUBLANES
        out.append(packed[r0:r0 + nr].reshape(-1)[:n].reshape(shp))
        r0 += nr
    return out


def adam_small(g, w, m, v):
    r, c = w.shape

    def body(g_ref, w_ref, m_ref, v_ref, d_ref, nm_ref, nv_ref):
        d_ref[...], nm_ref[...], nv_ref[...] = _adamw(w_ref[...], g_ref[...], m_ref[...], v_ref[...])

    tr = max(t for t in range(SUBLANES, 1024 + 1, SUBLANES) if r % t == 0)
    blk = pl.BlockSpec((tr, c), lambda i: (i, 0))
    return pl.pallas_call(
        body, name="adam_small", grid=(r // tr,),
        in_specs=[blk] * 4, out_specs=[blk] * 3, out_shape=[jax.ShapeDtypeStruct((r, c), f32)] * 3,
        compiler_params=_cparams("parallel"),
    )(g, w, m, v)


def kernel(x, c, ln_pre_g, ln_post_g, w_mod, b_mod, w_in_ab, w_out_ab, sgu_norm_g, sgu_w, sgu_b, w_in_ssm, w_out_ssm, lam_re, lam_im, b_re, b_im, c_re, c_im, d_skip, log_dt, w_glu, b_glu, loss_target, m_ln_pre_g, m_ln_post_g, m_w_mod, m_b_mod, m_w_in_ab, m_w_out_ab, m_sgu_norm_g, m_sgu_w, m_sgu_b, m_w_in_ssm, m_w_out_ssm, m_lam_re, m_lam_im, m_b_re, m_b_im, m_c_re, m_c_im, m_d_skip, m_log_dt, m_w_glu, m_b_glu, v_ln_pre_g, v_ln_post_g, v_w_mod, v_b_mod, v_w_in_ab, v_w_out_ab, v_sgu_norm_g, v_sgu_w, v_sgu_b, v_w_in_ssm, v_w_out_ssm, v_lam_re, v_lam_im, v_b_re, v_b_im, v_c_re, v_c_im, v_d_skip, v_log_dt, v_w_glu, v_b_glu):
    me = _my_index()
    x0 = x[0]
    l, d = x0.shape
    target = loss_target[0]
    nh = sgu_w.shape[1]
    wa = nh * HEAD
    n_grp, n_st = lam_re.shape[1], lam_re.shape[2]
    mod_cols = w_mod.shape[2]

    c_all, d_skip_all, b_glu_all = all_gather([c, d_skip, b_glu], "gather_c")
    c_all = c_all.reshape(N_DEV, d)
    d_skip_all = d_skip_all.reshape(1, -1)
    b_glu_all = b_glu_all.reshape(1, -1)

    b_cols = lax.dynamic_slice_in_dim(b_mod, me * mod_cols, mod_cols, axis=1)
    (mod_all,) = all_gather([mod_part(c_all, w_mod, b_cols)], "gather_mod")
    def after(a, first):
        return a + jnp.minimum(jnp.abs(first[(0,) * first.ndim].astype(f32)), 0.0).astype(a.dtype)

    (win_ab3,) = sequencer_exchange(GATHER, [after(w_in_ab[0], mod_all).astype(bf16)], "gather_w_in", 1)
    mod_mine = lax.dynamic_index_in_dim(mod_all, me, axis=2, keepdims=False)
    mod_rows = jnp.transpose(mod_mine, (1, 0, 2)).reshape(2, 3, 1, d)

    def rows(a, i):
        return a[i].reshape(1, d)

    shift0, scale0, gate0 = mod_rows[0, 0], mod_rows[0, 1], mod_rows[0, 2]
    h0 = prenorm_fwd(x0, rows(ln_pre_g, 0), shift0, scale0, "prenorm0")
    wout_ab3, win_ssm3, wout_ssm3, wglu = sequencer_exchange(
        GATHER, [after(w, win_ab3).astype(bf16) for w in (w_out_ab[0], w_in_ssm[0], w_out_ssm[0], w_glu[0])],
        "gather_w_rest", 2)
    proj0 = mm_nn(h0, win_ab3, bf16, "proj0")
    sgu_b3 = sgu_b[0].reshape(nh, HEAD, 1)
    cat, att, tot = sb_fwd(proj0, sgu_fwd(proj0, sgu_norm_g, sgu_w[0], sgu_b3), nh)
    wout_ab3 = wout_ab3.reshape(1, d, d)
    win_ssm3 = win_ssm3.reshape(1, d, d)
    wglu = wglu.reshape(w_glu.shape[2], w_glu.shape[2])
    y0 = mm_nn(cat, wout_ab3, f32, "out0")

    shift1, scale1, gate1 = mod_rows[1, 0], mod_rows[1, 1], mod_rows[1, 2]
    x1, h1 = post_prenorm_fwd(x0, y0, gate0, rows(ln_post_g, 0), rows(ln_pre_g, 1), shift1, scale1, "post0_prenorm1")
    proj1 = mm_nn(h1, win_ssm3, bf16, "proj1")
    w_ssm = proj1.shape[1] // 2
    ldt = log_dt[0].reshape(n_grp, 1)
    bt_re = jnp.transpose(b_re[0], (0, 2, 1))
    bt_im = jnp.transpose(b_im[0], (0, 2, 1))
    a_re, a_im, bbt_re, bbt_im = s5_params_fwd(lam_re[0], lam_im[0], ldt, bt_re, bt_im)
    bre3 = _block_diag(bbt_re).astype(bf16)
    bim3 = _block_diag(bbt_im).astype(bf16)
    cre3 = _block_diag(jnp.transpose(c_re[0], (0, 2, 1))).astype(bf16)
    cimn3 = _block_diag(-jnp.transpose(c_im[0], (0, 2, 1))).astype(bf16)
    a_re_row, a_im_row = a_re.reshape(1, -1), a_im.reshape(1, -1)
    y_ssm, hs_re, hs_im = ssm_fwd(proj1, bre3, bim3, cre3, cimn3, a_re_row, a_im_row, d_skip_all)
    g_act, t_glu, mix1 = glu_fwd(y_ssm, proj1, wglu, b_glu_all)
    y1 = mm_nn(mix1, wout_ssm3, f32, "out1")

    dx2, loss_tile, dy1, dgate1, dgpost1 = final_loss(x1, y1, gate1, rows(ln_post_g, 1), target)

    dmix1 = mm_nt(dy1, wout_ssm3, f32, "dmix1")
    gw_out_ssm = mm_tn(mix1, dy1, N_DEV, bf16, "gw_out_ssm")
    (p_out_ssm,) = sequencer_exchange(SCATTER, [gw_out_ssm], "scatter_g1", 3)
    dy_ssm, dproj1, dt_glu, db_glu = glu_bwd(dmix1, y_ssm, t_glu, proj1, wglu)
    gw_glu = mm_tn(g_act, dt_glu, 1, bf16, "gw_glu").reshape(N_DEV, -1, w_ssm)
    dproj1, dd_skip, da_re, da_im, dbre3, dbim3, dcre3, dcimn3 = ssm_bwd(
        dy_ssm, proj1, dproj1, hs_re, hs_im, bre3, bim3, cre3, cimn3, a_re_row, a_im_row, d_skip_all)
    gw_in_ssm = mm_tn(h1, dproj1, 1, bf16, "gw_in_ssm").reshape(N_DEV, -1, proj1.shape[1])
    p_in_ssm, p_glu = sequencer_exchange(SCATTER, [gw_in_ssm, gw_glu], "scatter_g2", 4)
    dh1 = mm_nt(dproj1, win_ssm3, f32, "dh1")
    dx1, dshift1, dscale1, dgpre1 = prenorm_bwd(dh1, x1, dx2, rows(ln_pre_g, 1), scale1, "prenorm1_bwd")
    dlr, dli, dldt, dbt_re, dbt_im = s5_params_bwd(
        lam_re[0], lam_im[0], ldt, bt_re, bt_im, da_re.reshape(n_grp, n_st), da_im.reshape(n_grp, n_st),
        _diag_blocks(dbre3, SSM_GROUP, n_st), _diag_blocks(dbim3, SSM_GROUP, n_st))
    g_b_re = jnp.transpose(dbt_re, (0, 2, 1))
    g_b_im = jnp.transpose(dbt_im, (0, 2, 1))
    g_c_re = jnp.transpose(_diag_blocks(dcre3, n_st, SSM_GROUP), (0, 2, 1))
    g_c_im = -jnp.transpose(_diag_blocks(dcimn3, n_st, SSM_GROUP), (0, 2, 1))

    dy0, dgate0, dgpost0 = post_bwd(dx1, y0, gate0, rows(ln_post_g, 0), "post0_bwd")
    dcat = mm_nt(dy0, wout_ab3, f32, "dcat")
    gw_out_ab = mm_tn(cat, dy0, 1, bf16, "gw_out_ab").reshape(N_DEV, -1, d)
    (p_out_ab,) = sequencer_exchange(SCATTER, [gw_out_ab], "scatter_g3", 5)
    da, dsgu_w, dsgu_b, dsgu_ng = sgu_bwd(proj0, dcat, sgu_norm_g, sgu_w[0], sgu_b3)
    dq, dk, dv, dbz = sb_bwd(proj0, dcat, att, tot, nh)
    dproj0 = jnp.concatenate([da, dq, dk, dv, dbz], axis=1)
    gw_in_ab = mm_nn(jnp.transpose(h0), dproj0[None], bf16, "gw_in_ab", split_cols=N_DEV)
    (p_in_ab,) = sequencer_exchange(SCATTER, [gw_in_ab], "scatter_g4", 6)
    dh0 = mm_nt(dproj0, win_ab3, f32, "dh0")
    dx0, dshift0, dscale0, dgpre0 = prenorm_bwd(dh0, x0, dx1, rows(ln_pre_g, 0), scale0, "prenorm0_bwd")

    small_names = ["ln_pre_g", "ln_post_g", "b_mod", "sgu_norm_g", "sgu_w", "sgu_b", "lam_re", "lam_im", "b_re", "b_im",
                   "c_re", "c_im", "log_dt"]
    small_w = [ln_pre_g, ln_post_g, b_mod, sgu_norm_g, sgu_w, sgu_b, lam_re, lam_im, b_re, b_im, c_re, c_im, log_dt]
    small_m = [m_ln_pre_g, m_ln_post_g, m_b_mod, m_sgu_norm_g, m_sgu_w, m_sgu_b, m_lam_re, m_lam_im, m_b_re, m_b_im,
               m_c_re, m_c_im, m_log_dt]
    small_v = [v_ln_pre_g, v_ln_post_g, v_b_mod, v_sgu_norm_g, v_sgu_w, v_sgu_b, v_lam_re, v_lam_im, v_b_re, v_b_im,
               v_c_re, v_c_im, v_log_dt]
    dmod = jnp.concatenate([dshift0, dscale0, dgate0, dshift1, dscale1, dgate1], axis=1)
    small_g = [jnp.concatenate([dgpre0, dgpre1]), jnp.concatenate([dgpost0, dgpost1]), dmod, dsgu_ng, dsgu_w, dsgu_b,
               dlr, dli, g_b_re, g_b_im, g_c_re, g_c_im, dldt]
    shapes = [w.shape for w in small_w]
    g_sum, dmod_all = all_reduce_rows(_pack(small_g + [dd_skip, db_glu, loss_tile], SUBLANES * N_DEV), dmod,
                                      "reduce_small_grads")
    n_rows_small = sum(-(-math.prod(s) // PACK_ROW) * SUBLANES for s in shapes)
    loss = g_sum[n_rows_small + 2 * (d_skip_all.shape[1] // HEAD), 0] * (0.5 / d)
    new_small = adam_small(g_sum, _pack(small_w), _pack(small_m), _pack(small_v))
    r_small = [_unpack(o, shapes) for o in [g_sum[:n_rows_small]] + list(new_small)]
    small = {n: [r_small[k][i] for k in range(4)] for i, n in enumerate(small_names)}
    vec_rows = d_skip_all.shape[1] // HEAD

    def my_columns(r0):
        whole = g_sum[r0:r0 + vec_rows].reshape(1, 1, -1)
        return lax.dynamic_slice_in_dim(whole, me * d_skip.shape[1], d_skip.shape[1], axis=2)

    def sharded(p, w, m, v, name):
        shp = w.shape
        w2, m2, v2 = (a.reshape(-1, shp[-1]) for a in (w, m, v))
        return [o.reshape(shp) for o in adam_reduce(p.reshape(p.shape[0], -1, shp[-1]), w2, m2, v2, name)]

    r_d_skip = sharded(my_columns(n_rows_small), d_skip, m_d_skip, v_d_skip, "adam_d_skip")
    r_b_glu = sharded(my_columns(n_rows_small + vec_rows), b_glu, m_b_glu, v_b_glu, "adam_b_glu")
    r_w_out_ssm = sharded(p_out_ssm, w_out_ssm, m_w_out_ssm, v_w_out_ssm, "adam_w_out_ssm")
    r_w_in_ssm = sharded(p_in_ssm, w_in_ssm, m_w_in_ssm, v_w_in_ssm, "adam_w_in_ssm")
    r_w_glu = sharded(p_glu, w_glu, m_w_glu, v_w_glu, "adam_w_glu")
    r_w_out_ab = sharded(p_out_ab, w_out_ab, m_w_out_ab, v_w_out_ab, "adam_w_out_ab")
    r_w_in_ab = sharded(p_in_ab, w_in_ab, m_w_in_ab, v_w_in_ab, "adam_w_in_ab")

    dm_cols = jnp.transpose(
        lax.dynamic_slice_in_dim(dmod_all.reshape(N_DEV, 2, 3 * d), me * mod_cols, mod_cols, axis=2), (1, 0, 2))
    cond_t = jnp.transpose(silu_rows(c_all))
    r_w_mod = adam_w_mod(cond_t, dm_cols, w_mod, m_w_mod, v_w_mod)

    res = dict(small)
    res.update(w_mod=r_w_mod, w_in_ab=r_w_in_ab, w_out_ab=r_w_out_ab, w_in_ssm=r_w_in_ssm, w_out_ssm=r_w_out_ssm,
               d_skip=r_d_skip, w_glu=r_w_glu, b_glu=r_b_glu)
    order = ["ln_pre_g", "ln_post_g", "w_mod", "b_mod", "w_in_ab", "w_out_ab", "sgu_norm_g", "sgu_w", "sgu_b", "w_in_ssm",
             "w_out_ssm", "lam_re", "lam_im", "b_re", "b_im", "c_re", "c_im", "d_skip", "log_dt", "w_glu", "b_glu"]
    outs = [loss, dx0.reshape(x.shape)]
    for k in range(4):
        outs += [res[n][k] for n in order]
    return tuple(outs)
```

```python
import functools
import math

import jax
import jax.numpy as jnp
from jax import lax
from jax.experimental import pallas as pl
from jax.experimental.pallas import tpu as pltpu
from jax.experimental.pallas import tpu_sc as plsc

f32 = jnp.float32
bf16 = jnp.bfloat16

N_DEV = 8
EPS = 1e-6
HEAD = 128
SUBLANES = 8
SSM_GROUP = 16
SSM_STATE = 64
GROUPS_PER_LANE_BLOCK = HEAD // SSM_GROUP
STATES_PER_LANE_BLOCK = GROUPS_PER_LANE_BLOCK * SSM_STATE
VMEM_LIMIT = 56 * 2 ** 20
ADAM_LR, ADAM_B1, ADAM_B2, ADAM_EPS, ADAM_WD, ADAM_STEP = 0.001, 0.9, 0.999, 1e-08, 0.01, 10
_GELU_C0 = math.sqrt(2.0 / math.pi)
_GELU_C1 = 0.044715
MESH = pl.DeviceIdType.MESH


def _cparams(*sem):
    return pltpu.CompilerParams(dimension_semantics=sem if sem else None, vmem_limit_bytes=VMEM_LIMIT)


def _gelu(x):
    return 0.5 * x * (1.0 + jnp.tanh(_GELU_C0 * (x + _GELU_C1 * x * x * x)))


def _gelu_grad(x):
    t = jnp.tanh(_GELU_C0 * (x + _GELU_C1 * x * x * x))
    return 0.5 * (1.0 + t) + 0.5 * x * (1.0 - t * t) * _GELU_C0 * (1.0 + 3.0 * _GELU_C1 * x * x)


def _silu(x):
    return x * jax.nn.sigmoid(x)


def _silu_grad(x):
    s = jax.nn.sigmoid(x)
    return s * (1.0 + x * (1.0 - s))


def _dot(a, b):
    return jnp.dot(a, b, preferred_element_type=f32)


def _dot_nt(a, b):
    return lax.dot_general(a, b, (((1,), (1,)), ((), ())), preferred_element_type=f32)


def _dot_tn(a, b):
    return lax.dot_general(a, b, (((0,), (0,)), ((), ())), preferred_element_type=f32)


def _split_bf16(v):
    hi = v.astype(bf16)
    lo = (v - hi.astype(f32)).astype(bf16)
    return hi, lo


def _row(d):
    return pl.BlockSpec((1, d), lambda *_: (0, 0))


def _my_index():
    return 4 * lax.axis_index("x") + 2 * lax.axis_index("y") + lax.axis_index("c")


def _peer(k):
    x, y, c = lax.axis_index("x"), lax.axis_index("y"), lax.axis_index("c")
    return (1 - x if k & 4 else x, 1 - y if k & 2 else y, 1 - c if k & 1 else c)


def all_gather(arrs, name):
    n = len(arrs)

    def body(*refs):
        ins, outs = refs[:n], refs[n:2 * n]
        send, recv, local = refs[2 * n:]
        me = _my_index()
        copies = []
        for a in range(n):
            cp = pltpu.make_async_copy(ins[a], outs[a].at[me], local.at[a])
            cp.start()
            copies.append(cp)
            for k in range(1, N_DEV):
                s = a * (N_DEV - 1) + k - 1
                cp = pltpu.make_async_remote_copy(src_ref=ins[a], dst_ref=outs[a].at[me], send_sem=send.at[s],
                                                  recv_sem=recv.at[s], device_id=_peer(k), device_id_type=MESH)
                cp.start()
                copies.append(cp)
        for cp in copies:
            cp.wait()

    any_spec = pl.BlockSpec(memory_space=pl.ANY)
    outs = pl.pallas_call(
        body, name=name,
        out_shape=[jax.ShapeDtypeStruct((N_DEV,) + a.shape, a.dtype) for a in arrs],
        in_specs=[any_spec] * n, out_specs=[any_spec] * n,
        scratch_shapes=[pltpu.SemaphoreType.DMA((n * (N_DEV - 1),)), pltpu.SemaphoreType.DMA((n * (N_DEV - 1),)),
                        pltpu.SemaphoreType.DMA((n,))],
        compiler_params=pltpu.CompilerParams(has_side_effects=True),
    )(*arrs)
    return list(outs)


def all_reduce_rows(pack, extra, name):
    r, c = pack.shape
    rs = r // N_DEV
    n_peer = N_DEV - 1

    def body(p_ref, x_ref, o_ref, xo_ref, land, red, send1, recv1, send2, recv2, sendx, recvx, local):
        me = _my_index()

        def rows(i):
            return pl.ds(pl.multiple_of(i * rs, SUBLANES), rs)

        own = [pltpu.make_async_copy(p_ref.at[rows(me)], land.at[me], local.at[0]),
               pltpu.make_async_copy(x_ref, xo_ref.at[me], local.at[1])]
        first = []
        for k in range(1, N_DEV):
            first.append(pltpu.make_async_remote_copy(
                src_ref=p_ref.at[rows(jnp.bitwise_xor(me, k))], dst_ref=land.at[me], send_sem=send1.at[k - 1],
                recv_sem=recv1.at[k - 1], device_id=_peer(k), device_id_type=MESH))
            first.append(pltpu.make_async_remote_copy(
                src_ref=x_ref, dst_ref=xo_ref.at[me], send_sem=sendx.at[k - 1], recv_sem=recvx.at[k - 1],
                device_id=_peer(k), device_id_type=MESH))
        for cp in own + first:
            cp.start()
        for cp in own + first:
            cp.wait()
        acc = land[0]
        for s in range(1, N_DEV):
            acc = acc + land[s]
        red[...] = acc
        mine = pltpu.make_async_copy(red, o_ref.at[rows(me)], local.at[2])
        second = [pltpu.make_async_remote_copy(
            src_ref=red, dst_ref=o_ref.at[rows(me)], send_sem=send2.at[k - 1], recv_sem=recv2.at[k - 1],
            device_id=_peer(k), device_id_type=MESH) for k in range(1, N_DEV)]
        for cp in [mine] + second:
            cp.start()
        for cp in [mine] + second:
            cp.wait()

    any_spec = pl.BlockSpec(memory_space=pl.ANY)
    return pl.pallas_call(
        body, name=name,
        out_shape=[jax.ShapeDtypeStruct((r, c), pack.dtype), jax.ShapeDtypeStruct((N_DEV,) + extra.shape, extra.dtype)],
        in_specs=[any_spec, any_spec], out_specs=[any_spec, any_spec],
        scratch_shapes=[pltpu.VMEM((N_DEV, rs, c), pack.dtype), pltpu.VMEM((rs, c), pack.dtype)]
        + [pltpu.SemaphoreType.DMA((n_peer,))] * 6 + [pltpu.SemaphoreType.DMA((3,))],
        compiler_params=pltpu.CompilerParams(has_side_effects=True),
    )(pack, extra)


GATHER, SCATTER = "gather", "scatter"


def _exchange_copies(srcs, lands, send, recv):
    me = _my_index()
    copies = []
    for a, (src, land) in enumerate(zip(srcs, lands)):
        for k in range(1, N_DEV):
            s = a * (N_DEV - 1) + k - 1
            copies.append(pltpu.make_async_remote_copy(
                src_ref=src.at[jnp.bitwise_xor(me, k)], dst_ref=land.at[me],
                send_sem=send.at[s], recv_sem=recv.at[s], device_id=_peer(k), device_id_type=MESH))
    return copies


def sequencer_exchange(kind, arrs, name, collective_id):
    n = len(arrs)
    n_sem = n * (N_DEV - 1)
    land_shapes = [((N_DEV,) + a.shape if kind == GATHER else a.shape) for a in arrs]
    srcs = [jax.new_ref(a, memory_space=pltpu.MemorySpace.HBM) for a in arrs]
    lands = [jax.empty_ref(jax.ShapeDtypeStruct(s, a.dtype), memory_space=pltpu.MemorySpace.HBM)
             for s, a in zip(land_shapes, arrs)]

    @pl.kernel(mesh=plsc.ScalarSubcoreMesh(axis_name="sequencer", num_cores=1), name=name,
               scratch_types=(pltpu.SemaphoreType.DMA((n_sem,)), pltpu.SemaphoreType.DMA((n_sem,)),
                              pltpu.SemaphoreType.DMA((n,))),
               compiler_params=pltpu.CompilerParams(collective_id=collective_id))
    def launch(send, recv, local):
        barrier = pltpu.get_barrier_semaphore()
        for k in range(1, N_DEV):
            pl.semaphore_signal(barrier, inc=1, device_id=_peer(k), device_id_type=MESH)
        pl.semaphore_wait(barrier, N_DEV - 1)
        me = _my_index()
        mine = [pltpu.make_async_copy(src if kind == GATHER else src.at[me], land.at[me], local.at[a])
                for a, (src, land) in enumerate(zip(srcs, lands))]
        if kind == SCATTER:
            copies = mine + _exchange_copies(srcs, lands, send, recv)
            for cp in copies:
                cp.start()
            for cp in copies:
                cp.wait()
            return

        def block_copy(a, slot, block, k, src=None):
            s = a * (N_DEV - 1) + slot
            return pltpu.make_async_remote_copy(
                src_ref=lands[a].at[block] if src is None else src, dst_ref=lands[a].at[block],
                send_sem=send.at[s], recv_sem=recv.at[s], device_id=_peer(k), device_id_type=MESH)

        chips = (2, 4, 6)
        sibling = jnp.bitwise_xor(me, 1)
        first = [block_copy(a, slot, me, k, src=srcs[a]) for a in range(n) for slot, k in enumerate((1,) + chips)]
        for cp in mine + first:
            cp.start()
        passed = []
        for a in range(n):
            for i, k in enumerate(chips):
                block = jnp.bitwise_xor(me, k)
                block_copy(a, 1 + i, block, k).wait_recv()
                passed.append(block_copy(a, 4 + i, block, 1))
                passed[-1].start()
        for a in range(n):
            block_copy(a, 0, sibling, 1).wait_recv()
            for i, k in enumerate(chips):
                block_copy(a, 4 + i, jnp.bitwise_xor(sibling, k), 1).wait_recv()
        for cp in mine:
            cp.wait()
        for cp in first + passed:
            cp.wait_send()

    launch()
    return [land[...] for land in lands]


def _tile(n, pref):
    for t in pref:
        if n % t == 0:
            return t
    return n


MM_WIDE = 1024
MM_WEIGHT_BLOCK = 8 * 2 ** 20


def _blocks_per_step(nb, fits):
    return max(g for g in range(1, nb + 1) if nb % g == 0 and fits(g))


def mm_nn(a, b3, out_dtype, name, split_cols=None):
    m, k = a.shape
    nb, _, bn = b3.shape
    tm = _tile(m, (512, 256, 128))
    tn = bn // split_cols if split_cols else _tile(bn, (1024, 896, 512, 256, 128))
    per = bn // tn
    gb = _blocks_per_step(nb, lambda g: g == 1 or (per == 1 and g * bn <= MM_WIDE))

    def body(a_ref, b_ref, o_ref):
        for g in range(gb):
            o_ref[:, g * tn:(g + 1) * tn] = _dot(a_ref[...], b_ref[g]).astype(o_ref.dtype)

    if split_cols:
        out_spec = pl.BlockSpec((None, tm, tn), lambda i, j, jj: (jj, i, 0))
        out_shape = jax.ShapeDtypeStruct((split_cols, m, tn), out_dtype)
    else:
        out_spec = pl.BlockSpec((tm, gb * tn), lambda i, j, jj: (i, j * per + jj))
        out_shape = jax.ShapeDtypeStruct((m, nb * bn), out_dtype)
    return pl.pallas_call(
        body, name=name, grid=(m // tm, nb // gb, per),
        in_specs=[pl.BlockSpec((tm, k), lambda i, j, jj: (i, 0)),
                  pl.BlockSpec((gb, k, tn), lambda i, j, jj: (j, 0, jj))],
        out_specs=out_spec, out_shape=out_shape,
        compiler_params=_cparams("parallel", "arbitrary", "arbitrary"),
    )(a, b3)


def mm_nt(a, w3, out_dtype, name):
    m, _ = a.shape
    nb, ko, bn = w3.shape
    tm = _tile(m, (512, 256, 128))
    tko = _tile(ko, (1024, 512, 256, 128))
    gb = _blocks_per_step(nb, lambda g: g * tko * bn * w3.dtype.itemsize <= MM_WEIGHT_BLOCK)
    ns = nb // gb

    def body(a_ref, w_ref, o_ref, acc_ref):
        j = pl.program_id(2)

        @pl.when(j == 0)
        def _():
            acc_ref[...] = jnp.zeros_like(acc_ref)

        part = _dot_nt(a_ref[:, :bn], w_ref[0])
        for g in range(1, gb):
            part += _dot_nt(a_ref[:, g * bn:(g + 1) * bn], w_ref[g])
        acc_ref[...] += part

        @pl.when(j == ns - 1)
        def _():
            o_ref[...] = acc_ref[...].astype(o_ref.dtype)

    return pl.pallas_call(
        body, name=name, grid=(m // tm, ko // tko, ns),
        in_specs=[pl.BlockSpec((tm, gb * bn), lambda i, o, j: (i, j)),
                  pl.BlockSpec((gb, tko, bn), lambda i, o, j: (j, o, 0))],
        out_specs=pl.BlockSpec((tm, tko), lambda i, o, j: (i, o)),
        out_shape=jax.ShapeDtypeStruct((m, ko), out_dtype),
        scratch_shapes=[pltpu.VMEM((tm, tko), f32)],
        compiler_params=_cparams("parallel", "arbitrary", "arbitrary"),
    )(a, w3)


def mm_tn(a, dy, ncb, out_dtype, name):
    l, ka = a.shape
    _, n = dy.shape
    bn = n // ncb
    tl = _tile(l, (1024, 512, 256, 128))
    tka = _tile(ka, (512, 256, 128))
    tn = _tile(bn, (1024, 896, 512, 256, 128))
    per = bn // tn
    gb = _blocks_per_step(ncb, lambda g: g == 1 or (per == 1 and g * bn <= MM_WIDE))
    nl = l // tl

    def body(a_ref, dy_ref, o_ref, acc_ref):
        s = pl.program_id(2)

        @pl.when(s == 0)
        def _():
            acc_ref[...] = jnp.zeros_like(acc_ref)

        acc_ref[...] += _dot_tn(a_ref[...], dy_ref[...])

        @pl.when(s == nl - 1)
        def _():
            for g in range(gb):
                o_ref[g] = acc_ref[:, g * tn:(g + 1) * tn].astype(o_ref.dtype)

    return pl.pallas_call(
        body, name=name, grid=(ka // tka, n // (gb * tn), nl),
        in_specs=[pl.BlockSpec((tl, tka), lambda i, j, s: (s, i)),
                  pl.BlockSpec((tl, gb * tn), lambda i, j, s: (s, j))],
        out_specs=pl.BlockSpec((gb, tka, tn), lambda i, j, s: (j // per, i, j % per)),
        out_shape=jax.ShapeDtypeStruct((ncb, ka, bn), out_dtype),
        scratch_shapes=[pltpu.VMEM((tka, gb * tn), f32)],
        compiler_params=_cparams("parallel", "parallel", "arbitrary"),
    )(a, dy)


def mod_part(c_all, w_mod, b_cols):
    nl, d, cols = w_mod.shape

    def body(c_ref, w_ref, b_ref, o_ref):
        cond = _silu(c_ref[...]).astype(bf16)
        o_ref[...] = _dot(cond, w_ref[...].astype(bf16)) + b_ref[...]

    return pl.pallas_call(
        body, name="mod_part", grid=(nl,),
        in_specs=[pl.BlockSpec((N_DEV, d), lambda l: (0, 0)),
                  pl.BlockSpec((None, d, cols), lambda l: (l, 0, 0)),
                  pl.BlockSpec((None, 1, cols), lambda l: (l, 0, 0))],
        out_specs=pl.BlockSpec((None, N_DEV, cols), lambda l: (l, 0, 0)),
        out_shape=jax.ShapeDtypeStruct((nl, N_DEV, cols), f32),
        compiler_params=_cparams("arbitrary"),
    )(c_all, w_mod, b_cols.reshape(nl, 1, cols))


def _row_tile(l):
    return _tile(l, (512, 256, 128))


def _entry_rows(xv, g_ref, sh_ref, sc_ref, h_ref, ht_ref):
    r = lax.rsqrt(jnp.mean(xv * xv, axis=-1, keepdims=True) + EPS)
    h = xv * r * (g_ref[...] * (1.0 + sc_ref[...])) + sh_ref[...]
    h_ref[...] = h.astype(h_ref.dtype)
    ht_ref[...] = jnp.transpose(h).astype(ht_ref.dtype)


def prenorm_fwd(x, g, shift, scale, name):
    l, d = x.shape
    tm = _row_tile(l)

    def body(x_ref, g_ref, sh_ref, sc_ref, h_ref, ht_ref):
        _entry_rows(x_ref[...], g_ref, sh_ref, sc_ref, h_ref, ht_ref)

    return pl.pallas_call(
        body, name=name, grid=(l // tm,),
        in_specs=[pl.BlockSpec((tm, d), lambda i: (i, 0)), _row(d), _row(d), _row(d)],
        out_specs=[pl.BlockSpec((tm, d), lambda i: (i, 0)), pl.BlockSpec((d, tm), lambda i: (0, i))],
        out_shape=[jax.ShapeDtypeStruct((l, d), bf16), jax.ShapeDtypeStruct((d, l), bf16)],
        compiler_params=_cparams("parallel"),
    )(x, g, shift, scale)


def post_prenorm_fwd(x, y, gate, g_post, g_pre, shift, scale, name):
    l, d = x.shape
    tm = _row_tile(l)

    def body(x_ref, y_ref, gate_ref, gp_ref, g_ref, sh_ref, sc_ref, o_ref, h_ref, ht_ref):
        yv = y_ref[...]
        r = lax.rsqrt(jnp.mean(yv * yv, axis=-1, keepdims=True) + EPS)
        xv = x_ref[...] + gate_ref[...] * (yv * r * gp_ref[...])
        o_ref[...] = xv
        _entry_rows(xv, g_ref, sh_ref, sc_ref, h_ref, ht_ref)

    blk = pl.BlockSpec((tm, d), lambda i: (i, 0))
    return pl.pallas_call(
        body, name=name, grid=(l // tm,),
        in_specs=[blk, blk] + [_row(d)] * 5, out_specs=[blk, blk, pl.BlockSpec((d, tm), lambda i: (0, i))],
        out_shape=[jax.ShapeDtypeStruct((l, d), f32), jax.ShapeDtypeStruct((l, d), bf16),
                   jax.ShapeDtypeStruct((d, l), bf16)],
        compiler_params=_cparams("parallel"),
    )(x, y, gate, g_post, g_pre, shift, scale)


def _post_bwd_rows(dxv, yv, r, gate, gv, dy_ref, dgate_ref, dg_ref):
    yn = yv * r
    dgate_ref[...] += jnp.sum(dxv * yn * gv, axis=0, keepdims=True)
    dyg = dxv * gate
    dg_ref[...] += jnp.sum(dyg * yn, axis=0, keepdims=True)
    dyn = dyg * gv
    dy_ref[...] = (r * (dyn - yn * jnp.mean(dyn * yn, axis=-1, keepdims=True))).astype(dy_ref.dtype)


def final_loss(x, y, gate, g, target):
    l, d = x.shape
    tm = _row_tile(l)

    def body(x_ref, y_ref, gate_ref, g_ref, t_ref, dx_ref, loss_ref, dy_ref, dgate_ref, dg_ref):
        @pl.when(pl.program_id(0) == 0)
        def _():
            loss_ref[...] = jnp.zeros_like(loss_ref)
            dgate_ref[...] = jnp.zeros_like(dgate_ref)
            dg_ref[...] = jnp.zeros_like(dg_ref)

        yv, gate, gv = y_ref[...], gate_ref[...], g_ref[...]
        r = lax.rsqrt(jnp.mean(yv * yv, axis=-1, keepdims=True) + EPS)
        diff = x_ref[...] + gate * (yv * r * gv) - t_ref[...]
        dxv = diff * (1.0 / d)
        dx_ref[...] = dxv
        loss_ref[...] += jnp.sum(diff * diff)
        _post_bwd_rows(dxv, yv, r, gate, gv, dy_ref, dgate_ref, dg_ref)

    blk = pl.BlockSpec((tm, d), lambda i: (i, 0))
    return pl.pallas_call(
        body, name="final_loss", grid=(l // tm,),
        in_specs=[blk, blk, _row(d), _row(d), blk],
        out_specs=[blk, pl.BlockSpec((SUBLANES, HEAD), lambda i: (0, 0)), blk, _row(d), _row(d)],
        out_shape=[jax.ShapeDtypeStruct((l, d), f32), jax.ShapeDtypeStruct((SUBLANES, HEAD), f32),
                   jax.ShapeDtypeStruct((l, d), bf16), jax.ShapeDtypeStruct((1, d), f32), jax.ShapeDtypeStruct((1, d), f32)],
        compiler_params=_cparams("arbitrary"),
    )(x, y, gate, g, target)


def post_bwd(dx, y, gate, g, name):
    l, d = dx.shape
    tm = _row_tile(l)

    def body(dx_ref, y_ref, gate_ref, g_ref, dy_ref, dgate_ref, dg_ref):
        @pl.when(pl.program_id(0) == 0)
        def _():
            dgate_ref[...] = jnp.zeros_like(dgate_ref)
            dg_ref[...] = jnp.zeros_like(dg_ref)

        yv = y_ref[...]
        r = lax.rsqrt(jnp.mean(yv * yv, axis=-1, keepdims=True) + EPS)
        _post_bwd_rows(dx_ref[...], yv, r, gate_ref[...], g_ref[...], dy_ref, dgate_ref, dg_ref)

    blk = pl.BlockSpec((tm, d), lambda i: (i, 0))
    return pl.pallas_call(
        body, name=name, grid=(l // tm,),
        in_specs=[blk, blk, _row(d), _row(d)], out_specs=[blk, _row(d), _row(d)],
        out_shape=[jax.ShapeDtypeStruct((l, d), bf16), jax.ShapeDtypeStruct((1, d), f32),
                   jax.ShapeDtypeStruct((1, d), f32)],
        compiler_params=_cparams("arbitrary"),
    )(dx, y, gate, g)


def prenorm_bwd(dh, x, dx_next, g, scale, name):
    l, d = x.shape
    tm = _row_tile(l)

    def body(dh_ref, x_ref, dxn_ref, g_ref, sc_ref, dx_ref, dsh_ref, dsc_ref, dg_ref):
        @pl.when(pl.program_id(0) == 0)
        def _():
            dsh_ref[...] = jnp.zeros_like(dsh_ref)
            dsc_ref[...] = jnp.zeros_like(dsc_ref)
            dg_ref[...] = jnp.zeros_like(dg_ref)

        xv, dhv, gv, sc1 = x_ref[...], dh_ref[...], g_ref[...], 1.0 + sc_ref[...]
        r = lax.rsqrt(jnp.mean(xv * xv, axis=-1, keepdims=True) + EPS)
        xn = xv * r
        dhx = dhv * xn
        dsh_ref[...] += jnp.sum(dhv, axis=0, keepdims=True)
        dsc_ref[...] += jnp.sum(dhx * gv, axis=0, keepdims=True)
        dg_ref[...] += jnp.sum(dhx * sc1, axis=0, keepdims=True)
        dxn = dhv * (gv * sc1)
        dx_ref[...] = dxn_ref[...] + r * (dxn - xn * jnp.mean(dxn * xn, axis=-1, keepdims=True))

    blk = pl.BlockSpec((tm, d), lambda i: (i, 0))
    return pl.pallas_call(
        body, name=name, grid=(l // tm,),
        in_specs=[blk, blk, blk, _row(d), _row(d)], out_specs=[blk, _row(d), _row(d), _row(d)],
        out_shape=[jax.ShapeDtypeStruct((l, d), f32)] + [jax.ShapeDtypeStruct((1, d), f32)] * 3,
        compiler_params=_cparams("arbitrary"),
    )(dh, x, dx_next, g, scale)


def _tril_mask():
    r = lax.broadcasted_iota(jnp.int32, (HEAD, HEAD), 0)
    c = lax.broadcasted_iota(jnp.int32, (HEAD, HEAD), 1)
    return r >= c


def sgu_fwd(proj, norm_g, w_s, b_s):
    l = proj.shape[0]
    nh = w_s.shape[0]
    wa = nh * HEAD

    def body(au_ref, av_ref, az_ref, ng_ref, w_ref, b_ref, o_ref):
        tril = _tril_mask()
        for h in range(nh):
            sl = slice(h * HEAD, (h + 1) * HEAD)
            gv = _gelu(av_ref[:, sl].astype(f32))
            r = lax.rsqrt(jnp.mean(gv * gv, axis=-1, keepdims=True) + EPS)
            vh = gv * r * ng_ref[:, sl]
            wm = jnp.where(tril, w_ref[h], 0.0).astype(bf16)
            s = _dot(wm, vh.astype(bf16)) + b_ref[h]
            o_ref[:, sl] = (_gelu(au_ref[:, sl].astype(f32)) * s * _silu(az_ref[:, sl].astype(f32))).astype(o_ref.dtype)

    def col(j):
        return pl.BlockSpec((HEAD, wa), lambda n: (n, j))

    return pl.pallas_call(
        body, name="sgu_fwd", grid=(l // HEAD,),
        in_specs=[col(0), col(1), col(2), _row(wa),
                  pl.BlockSpec((nh, HEAD, HEAD), lambda n: (0, 0, 0)), pl.BlockSpec((nh, HEAD, 1), lambda n: (0, 0, 0))],
        out_specs=pl.BlockSpec((HEAD, wa), lambda n: (n, 0)),
        out_shape=jax.ShapeDtypeStruct((l, 2 * wa), bf16),
        compiler_params=_cparams("parallel"),
    )(proj, proj, proj, norm_g, w_s, b_s)


def sgu_bwd(proj, dcat, norm_g, w_s, b_s):
    l = proj.shape[0]
    nh = w_s.shape[0]
    wa = nh * HEAD

    def body(au_ref, av_ref, az_ref, do_ref, ng_ref, w_ref, b_ref, da_ref, dw_ref, db_ref, dng_ref):
        @pl.when(pl.program_id(0) == 0)
        def _():
            dw_ref[...] = jnp.zeros_like(dw_ref)
            db_ref[...] = jnp.zeros_like(db_ref)
            dng_ref[...] = jnp.zeros_like(dng_ref)

        tril = _tril_mask()
        for h in range(nh):
            sl = slice(h * HEAD, (h + 1) * HEAD)
            au, av, az = au_ref[:, sl].astype(f32), av_ref[:, sl].astype(f32), az_ref[:, sl].astype(f32)
            ng = ng_ref[:, sl]
            gv = _gelu(av)
            r = lax.rsqrt(jnp.mean(gv * gv, axis=-1, keepdims=True) + EPS)
            gvn = gv * r
            vh = (gvn * ng).astype(bf16)
            wm = jnp.where(tril, w_ref[h], 0.0).astype(bf16)
            s = _dot(wm, vh) + b_ref[h]
            gu, sz = _gelu(au), _silu(az)
            dov = do_ref[:, sl].astype(f32)
            da_ref[:, sl] = (dov * s * sz * _gelu_grad(au)).astype(da_ref.dtype)
            da_ref[:, 2 * wa + h * HEAD:2 * wa + (h + 1) * HEAD] = (dov * gu * s * _silu_grad(az)).astype(da_ref.dtype)
            ds = dov * gu * sz
            db_ref[h] += jnp.sum(ds, axis=-1, keepdims=True)
            dsb = ds.astype(bf16)
            dw_ref[h] += jnp.where(tril, _dot_nt(dsb, vh), 0.0)
            dvh = _dot_tn(wm, dsb)
            dng_ref[:, sl] += jnp.sum(dvh * gvn, axis=0, keepdims=True)
            dgvn = dvh * ng
            dgv = r * (dgvn - gvn * jnp.mean(dgvn * gvn, axis=-1, keepdims=True))
            da_ref[:, wa + h * HEAD:wa + (h + 1) * HEAD] = (dgv * _gelu_grad(av)).astype(da_ref.dtype)

    def col(j):
        return pl.BlockSpec((HEAD, wa), lambda n: (n, j))

    whole_w = pl.BlockSpec((nh, HEAD, HEAD), lambda n: (0, 0, 0))
    whole_b = pl.BlockSpec((nh, HEAD, 1), lambda n: (0, 0, 0))
    return pl.pallas_call(
        body, name="sgu_bwd", grid=(l // HEAD,),
        in_specs=[col(0), col(1), col(2), col(0), _row(wa), whole_w, whole_b],
        out_specs=[pl.BlockSpec((HEAD, 3 * wa), lambda n: (n, 0)), whole_w, whole_b, _row(wa)],
        out_shape=[jax.ShapeDtypeStruct((l, 3 * wa), bf16), jax.ShapeDtypeStruct((nh, HEAD, HEAD), f32),
                   jax.ShapeDtypeStruct((nh, HEAD, 1), f32), jax.ShapeDtypeStruct((1, wa), f32)],
        compiler_params=_cparams("arbitrary"),
    )(proj, proj, proj, dcat, norm_g, w_s, b_s)


_LOG2E = 1.0 / math.log(2.0)


def _sb_scores(q, k, scale):
    z = _dot_nt(q, k) * (scale * _LOG2E)
    return z, jnp.maximum(z, 0.0) + jnp.log2(1.0 + jnp.exp2(-jnp.abs(z)))


SB_KEYS = 256


def _sb_sum_matrix(tri, kb):
    s = lax.broadcasted_iota(jnp.int32, (2 * kb, kb + HEAD), 0) % kb
    j = lax.broadcasted_iota(jnp.int32, (2 * kb, kb + HEAD), 1)
    return jnp.where(jnp.logical_or(j >= kb, tri(s, j)), 1.0, 0.0).astype(bf16)


def _sb_sums(x, sums):
    kb = x.shape[1]
    c2 = _dot(jnp.concatenate(_split_bf16(x), axis=1), sums)
    return c2[:, :kb], c2[:, kb:]


def _sb_wide(v, kb):
    return jnp.concatenate([v] * (kb // HEAD), axis=1) if kb > HEAD else v


def _sb_q_tile(l, most=512):
    return _tile(l, tuple(t for t in (1024, 512, 256, 128) if t <= most))


def _sb_band_levels(band):
    return _tile(band, (4, 2, 1))


def _sb_heads_per_step(nh, most):
    return _tile(nh, tuple(h for h in (4, 2) if h <= most))


def sb_fwd(proj, mixed, nh):
    l = proj.shape[0]
    wb = nh * HEAD
    tq = _sb_q_tile(l, 1024)
    kb = min(SB_KEYS, tq)
    band = tq // kb
    hp = _sb_heads_per_step(nh, 2)
    levels = _sb_band_levels(band)
    scale = 1.0 / math.sqrt(HEAD)
    qc, kc, vc, zc = 3 * nh, 4 * nh, 5 * nh, 6 * nh

    def body(q_ref, k_ref, v_ref, bz_ref, mixed_ref, o_ref, att_ref, tot_ref):
        del mixed_ref
        i = pl.program_id(1)
        sums = _sb_sum_matrix(lambda s, j: s > j, kb)
        t_pos = i * tq + lax.broadcasted_iota(jnp.int32, (tq, kb), 0)
        s_off = lax.broadcasted_iota(jnp.int32, (tq, kb), 1)

        def step(j, carry, masked, row0=0):
            rows = pl.ds(pl.multiple_of(j * kb, kb), kb)
            out = []
            for e in range(hp):
                acc, tot = carry[e]
                sl = slice(e * HEAD, (e + 1) * HEAD)
                z, sp = _sb_scores(q_ref[row0:, sl], k_ref[rows, sl], scale)
                lb = z - sp
                if masked:
                    mask = s_off[row0:] + j * kb < t_pos[row0:]
                    sp = jnp.where(mask, sp, 0.0)
                later, total = _sb_sums(sp, sums)
                w = jnp.exp2(lb + _sb_wide(tot[row0:], kb) - later)
                if masked:
                    w = jnp.where(mask, w, 0.0)
                new = (acc[row0:] + _dot(w.astype(bf16), v_ref[rows, sl]), tot[row0:] - total)
                out.append(tuple(jnp.concatenate([old[:row0], upd]) if row0 else upd for old, upd in zip(carry[e], new)))
            return tuple(out)

        zero = jnp.zeros((tq, HEAD), f32)
        carry = ((zero, zero),) * hp
        for lv in reversed(range(levels)):
            carry = lax.fori_loop(
                0, band // levels,
                lambda t, c, lv=lv: step(band * i + (lv + 1) * (band // levels) - 1 - t, c, True, lv * (tq // levels)), carry)
        carry = lax.fori_loop(0, band * i, lambda t, c: step(band * i - 1 - t, c, False), carry)
        for e in range(hp):
            acc, tot = carry[e]
            sl = slice(e * HEAD, (e + 1) * HEAD)
            att_ref[:, sl] = acc.astype(att_ref.dtype)
            o_ref[:, sl] = (acc * _silu(bz_ref[:, sl].astype(f32))).astype(o_ref.dtype)
            tot_ref[e] = tot[:, :1]

    blk = lambda c0: pl.BlockSpec((tq, hp * HEAD), lambda g, i: (i, c0 // hp + g))
    head = lambda c0: pl.BlockSpec((l, hp * HEAD), lambda g, i: (0, c0 // hp + g))
    return pl.pallas_call(
        body, name="sb_fwd", grid=(nh // hp, l // tq),
        in_specs=[blk(qc), head(kc), head(vc), blk(zc), pl.BlockSpec(memory_space=pl.ANY)],
        out_specs=[blk(mixed.shape[1] // HEAD - nh), blk(0), pl.BlockSpec((hp, tq, 1), lambda g, i: (g, i, 0))],
        out_shape=[jax.ShapeDtypeStruct(mixed.shape, bf16), jax.ShapeDtypeStruct((l, wb), bf16),
                   jax.ShapeDtypeStruct((nh, l, 1), f32)],
        input_output_aliases={4: 0},
        compiler_params=_cparams("parallel", "arbitrary"),
    )(proj, proj, proj, proj, mixed)


def sb_bwd(proj, dcat, att, tot, nh):
    l = proj.shape[0]
    wb = nh * HEAD
    tq = _sb_q_tile(l, 1024)
    kb = min(SB_KEYS, tq)
    band = tq // kb
    nq = l // tq
    hp = _sb_heads_per_step(nh, 2)
    levels = _sb_band_levels(band)
    scale = 1.0 / math.sqrt(HEAD)
    qc, kc, vc, zc = 3 * nh, 4 * nh, 5 * nh, 6 * nh

    def body(q_ref, k_ref, v_ref, bz_ref, do_ref, att_ref, tot_ref, dq_ref, dk_ref, dv_ref, dbz_ref, dk_acc, dv_acc,
             dob_ref):
        i = pl.program_id(1)

        @pl.when(i == 0)
        def _():
            dk_acc[...] = jnp.zeros_like(dk_acc)
            dv_acc[...] = jnp.zeros_like(dv_acc)

        bz = bz_ref[...].astype(f32)
        dov = do_ref[...].astype(f32)
        dbz_ref[...] = (dov * att_ref[...].astype(f32) * _silu_grad(bz)).astype(dbz_ref.dtype)
        dob_ref[...] = (dov * _silu(bz)).astype(bf16)
        upto = _sb_sum_matrix(lambda s, j: s <= j, kb)
        before = _sb_sum_matrix(lambda j, s: j < s, kb)
        t_pos = i * tq + lax.broadcasted_iota(jnp.int32, (tq, kb), 0)
        s_off = lax.broadcasted_iota(jnp.int32, (tq, kb), 1)

        def step(j, carry, masked, row0=0):
            rows = pl.ds(pl.multiple_of(j * kb, kb), kb)
            out = []
            for h in range(hp):
                dq, sp_seen, e_seen = (c[row0:] for c in carry[h])
                sl = slice(h * HEAD, (h + 1) * HEAD)
                q, kj, vj, dob = q_ref[row0:, sl], k_ref[rows, sl], v_ref[rows, sl], dob_ref[row0:, sl]
                z, sp = _sb_scores(q, kj, scale)
                lb = z - sp
                if masked:
                    mask = s_off[row0:] + j * kb < t_pos[row0:]
                    sp = jnp.where(mask, sp, 0.0)
                sp_upto, sp_total = _sb_sums(sp, upto)
                w = jnp.exp2(lb + _sb_wide(sp_seen, kb) + sp_upto)
                if masked:
                    w = jnp.where(mask, w, 0.0)
                dv_acc[rows, sl] += _dot_tn(w.astype(bf16), dob)
                e = _dot_nt(dob, vj) * w
                e_before, e_total = _sb_sums(e, before)
                dz = (e - (e + _sb_wide(e_seen, kb) + e_before) * jnp.exp2(lb)) * scale
                if masked:
                    dz = jnp.where(mask, dz, 0.0)
                dz = dz.astype(bf16)
                dk_acc[rows, sl] += _dot_tn(dz, q)
                new = (dq + _dot(dz, kj), sp_seen + sp_total, e_seen + e_total)
                out.append(tuple(jnp.concatenate([old[:row0], upd]) if row0 else upd for old, upd in zip(carry[h], new)))
            return tuple(out)

        zero = jnp.zeros((tq, HEAD), f32)
        init = tuple((zero, jnp.broadcast_to(tot_ref[h], (tq, HEAD)), zero) for h in range(hp))
        carry = lax.fori_loop(0, band * i, lambda j, c: step(j, c, False), init)
        for lv in range(levels):
            carry = lax.fori_loop(
                0, band // levels,
                lambda t, c, lv=lv: step(band * i + lv * (band // levels) + t, c, True, lv * (tq // levels)), carry)
        for h in range(hp):
            dq_ref[:, h * HEAD:(h + 1) * HEAD] = carry[h][0].astype(dq_ref.dtype)

        @pl.when(i == nq - 1)
        def _():
            dk_ref[...] = dk_acc[...].astype(dk_ref.dtype)
            dv_ref[...] = dv_acc[...].astype(dv_ref.dtype)

    blk = lambda c0: pl.BlockSpec((tq, hp * HEAD), lambda g, i: (i, c0 // hp + g))
    head = lambda c0: pl.BlockSpec((l, hp * HEAD), lambda g, i: (0, c0 // hp + g))
    return pl.pallas_call(
        body, name="sb_bwd", grid=(nh // hp, nq),
        in_specs=[blk(qc), head(kc), head(vc), blk(zc), blk(nh), blk(0),
                  pl.BlockSpec((hp, tq, 1), lambda g, i: (g, i, 0))],
        out_specs=[blk(0), head(0), head(0), blk(0)],
        out_shape=[jax.ShapeDtypeStruct((l, wb), bf16)] * 4,
        scratch_shapes=[pltpu.VMEM((l, hp * HEAD), f32), pltpu.VMEM((l, hp * HEAD), f32),
                        pltpu.VMEM((tq, hp * HEAD), bf16)],
        compiler_params=_cparams("parallel", "arbitrary"),
    )(proj, proj, proj, proj, dcat, att, tot)


def _disc(lr, li, ldt):
    dt = jnp.exp(ldt)
    mag = jnp.exp(lr * dt)
    a_re = mag * jnp.cos(li * dt)
    a_im = mag * jnp.sin(li * dt)
    den = lr * lr + li * li
    nr = a_re - 1.0
    return a_re, a_im, (nr * lr + a_im * li) / den, (a_im * lr - nr * li) / den


def s5_params_fwd(lr, li, ldt, bt_re, bt_im):
    g, c, p = bt_re.shape

    def body(lr_ref, li_ref, ldt_ref, br_ref, bi_ref, ar_ref, ai_ref, bbr_ref, bbi_ref):
        a_re, a_im, cr, ci = _disc(lr_ref[...], li_ref[...], ldt_ref[...])
        ar_ref[...] = a_re
        ai_ref[...] = a_im
        for k in range(c):
            br, bi = br_ref[:, k, :], bi_ref[:, k, :]
            bbr_ref[:, k, :] = cr * br - ci * bi
            bbi_ref[:, k, :] = cr * bi + ci * br

    return pl.pallas_call(
        body, name="s5_params_fwd",
        out_shape=[jax.ShapeDtypeStruct((g, p), f32)] * 2 + [jax.ShapeDtypeStruct((g, c, p), f32)] * 2,
    )(lr, li, ldt, bt_re, bt_im)


def s5_params_bwd(lr, li, ldt, bt_re, bt_im, da_re, da_im, dbbt_re, dbbt_im):
    g, c, p = bt_re.shape

    def body(lr_ref, li_ref, ldt_ref, br_ref, bi_ref, dar_ref, dai_ref, dbbr_ref, dbbi_ref,
             dlr_ref, dli_ref, dldt_ref, dbr_ref, dbi_ref):
        (a_re, a_im, cr, ci), vjp = jax.vjp(_disc, lr_ref[...], li_ref[...], ldt_ref[...])
        dcr = jnp.zeros((g, p), f32)
        dci = jnp.zeros((g, p), f32)
        for k in range(c):
            br, bi = br_ref[:, k, :], bi_ref[:, k, :]
            dr, di = dbbr_ref[:, k, :], dbbi_ref[:, k, :]
            dcr += dr * br + di * bi
            dci += di * br - dr * bi
            dbr_ref[:, k, :] = cr * dr + ci * di
            dbi_ref[:, k, :] = cr * di - ci * dr
        dlr, dli, dldt = vjp((dar_ref[...], dai_ref[...], dcr, dci))
        dlr_ref[...] = dlr
        dli_ref[...] = dli
        dldt_ref[...] = dldt

    return pl.pallas_call(
        body, name="s5_params_bwd",
        out_shape=[jax.ShapeDtypeStruct((g, p), f32)] * 2 + [jax.ShapeDtypeStruct((g, 1), f32)]
        + [jax.ShapeDtypeStruct((g, c, p), f32)] * 2,
    )(lr, li, ldt, bt_re, bt_im, da_re, da_im, dbbt_re, dbbt_im)


def _cmul(ar, ai, br, bi):
    return ar * br - ai * bi, ar * bi + ai * br


def _power_tables(ar, ai):
    rows = lax.broadcasted_iota(jnp.int32, (SUBLANES, ar.shape[1]), 0)
    pr = jnp.zeros((SUBLANES, ar.shape[1]), f32)
    pi = jnp.zeros((SUBLANES, ar.shape[1]), f32)
    cr, ci = ar, ai
    pows = {}
    for r in range(SUBLANES):
        pows[r + 1] = (cr, ci)
        pr = jnp.where(rows == r, cr, pr)
        pi = jnp.where(rows == r, ci, pi)
        cr, ci = _cmul(cr, ci, ar, ai)
    return [pows[1], pows[2], pows[4]], pr, pi


def _ssm_time_tile(l):
    return _tile(l, (512, 256, 128))


def ssm_fwd(u, bre3, bim3, cre3, cimn3, a_re, a_im, d_skip):
    l, w = u.shape[0], d_skip.shape[1]
    nj = w // HEAD
    ns = STATES_PER_LANE_BLOCK
    tt = _ssm_time_tile(l)

    def body(u_ref, bre_ref, bim_ref, cre_ref, cim_ref, ar_ref, ai_ref, d_ref, y_ref, hr_ref, hi_ref, cr_ref, ci_ref):
        @pl.when(pl.program_id(1) == 0)
        def _():
            cr_ref[...] = jnp.zeros_like(cr_ref)
            ci_ref[...] = jnp.zeros_like(ci_ref)

        uv = u_ref[...]
        hr_ref[...] = _dot(uv, bre_ref[...])
        hi_ref[...] = _dot(uv, bim_ref[...])
        steps, pr, pi = _power_tables(ar_ref[...], ai_ref[...])
        rows = lax.broadcasted_iota(jnp.int32, (SUBLANES, ns), 0)
        steps = [(jnp.where(rows >= d, sr_, 0.0), jnp.where(rows >= d, si_, 0.0)) for d, (sr_, si_) in zip((1, 2, 4), steps)]

        def blk(b, carry):
            cr, ci = carry
            sl = pl.ds(pl.multiple_of(b * SUBLANES, SUBLANES), SUBLANES)
            xr, xi = hr_ref[sl, :], hi_ref[sl, :]
            for d, (sr_, si_) in zip((1, 2, 4), steps):
                mr, mi = _cmul(sr_, si_, pltpu.roll(xr, d, axis=0), pltpu.roll(xi, d, axis=0))
                xr, xi = xr + mr, xi + mi
            mr, mi = _cmul(pr, pi, cr, ci)
            xr, xi = xr + mr, xi + mi
            hr_ref[sl, :] = xr
            hi_ref[sl, :] = xi
            return xr[SUBLANES - 1:, :], xi[SUBLANES - 1:, :]

        cr, ci = lax.fori_loop(0, tt // SUBLANES, blk, (cr_ref[...], ci_ref[...]))
        cr_ref[...] = cr
        ci_ref[...] = ci
        y = _dot(hr_ref[...].astype(bf16), cre_ref[...]) + _dot(hi_ref[...].astype(bf16), cim_ref[...])
        y_ref[...] = y + d_ref[...] * uv.astype(f32)

    lane = pl.BlockSpec((tt, HEAD), lambda j, i: (i, j))
    st = pl.BlockSpec((tt, ns), lambda j, i: (i, j))
    b3 = pl.BlockSpec((None, HEAD, ns), lambda j, i: (j, 0, 0))
    c3 = pl.BlockSpec((None, ns, HEAD), lambda j, i: (j, 0, 0))
    arow = pl.BlockSpec((1, ns), lambda j, i: (0, j))
    return pl.pallas_call(
        body, name="ssm_fwd", grid=(nj, l // tt),
        in_specs=[lane, b3, b3, c3, c3, arow, arow, pl.BlockSpec((1, HEAD), lambda j, i: (0, j))],
        out_specs=[lane, st, st],
        out_shape=[jax.ShapeDtypeStruct((l, w), f32), jax.ShapeDtypeStruct((l, nj * ns), f32),
                   jax.ShapeDtypeStruct((l, nj * ns), f32)],
        scratch_shapes=[pltpu.VMEM((1, ns), f32), pltpu.VMEM((1, ns), f32)],
        compiler_params=_cparams("parallel", "arbitrary"),
    )(u, bre3, bim3, cre3, cimn3, a_re, a_im, d_skip)


def ssm_bwd(dy, u, dproj, h_re, h_im, bre3, bim3, cre3, cimn3, a_re, a_im, d_skip):
    l, w = u.shape[0], d_skip.shape[1]
    nj = w // HEAD
    ns = STATES_PER_LANE_BLOCK
    tt = _ssm_time_tile(l)
    nt = l // tt

    def body(dy_ref, u_ref, dproj_ref, hr_ref, hi_ref, bre_ref, bim_ref, cre_ref, cim_ref, ar_ref, ai_ref, d_ref,
             du_ref, dd_ref, dar_ref, dai_ref, dbre_ref, dbim_ref, dcre_ref, dcim_ref, kr_ref, ki_ref, cr_ref, ci_ref,
             accr_ref, acci_ref):
        del dproj_ref
        i = pl.program_id(1)

        @pl.when(i == 0)
        def _():
            for ref in (cr_ref, ci_ref, accr_ref, acci_ref, dd_ref, dbre_ref, dbim_ref, dcre_ref, dcim_ref):
                ref[...] = jnp.zeros_like(ref)

        dyv = dy_ref[...]
        dyb = dyv.astype(bf16)
        uv = u_ref[...]
        kr_ref[...] = _dot_nt(dyb, cre_ref[...])
        ki_ref[...] = _dot_nt(dyb, cim_ref[...])
        steps, pr, pi = _power_tables(ar_ref[...], -ai_ref[...])
        rows = lax.broadcasted_iota(jnp.int32, (SUBLANES, ns), 0)
        qr = jnp.zeros((SUBLANES, ns), f32)
        qi = jnp.zeros((SUBLANES, ns), f32)
        for r in range(SUBLANES):
            qr = jnp.where(rows == r, pr[SUBLANES - 1 - r:SUBLANES - r, :], qr)
            qi = jnp.where(rows == r, pi[SUBLANES - 1 - r:SUBLANES - r, :], qi)
        nb = tt // SUBLANES
        steps = [(jnp.where(rows < SUBLANES - d, sr_, 0.0), jnp.where(rows < SUBLANES - d, si_, 0.0))
                 for d, (sr_, si_) in zip((1, 2, 4), steps)]

        def blk(t, carry):
            cr, ci, accr, acci = carry
            sl = pl.ds(pl.multiple_of((nb - 1 - t) * SUBLANES, SUBLANES), SUBLANES)
            xr, xi = kr_ref[sl, :], ki_ref[sl, :]
            for d, (sr_, si_) in zip((1, 2, 4), steps):
                mr, mi = _cmul(sr_, si_, pltpu.roll(xr, SUBLANES - d, axis=0), pltpu.roll(xi, SUBLANES - d, axis=0))
                xr, xi = xr + mr, xi + mi
            mr, mi = _cmul(qr, qi, cr, ci)
            xr, xi = xr + mr, xi + mi
            kr_ref[sl, :] = xr
            ki_ref[sl, :] = xi
            last = rows == SUBLANES - 1
            nr = jnp.where(last, cr, pltpu.roll(xr, SUBLANES - 1, axis=0))
            ni = jnp.where(last, ci, pltpu.roll(xi, SUBLANES - 1, axis=0))
            hr, hi = hr_ref[sl, :], hi_ref[sl, :]
            accr = accr + nr * hr + ni * hi
            acci = acci + ni * hr - nr * hi
            return xr[:1, :], xi[:1, :], accr, acci

        cr, ci, accr, acci = lax.fori_loop(0, nb, blk, (cr_ref[...], ci_ref[...], accr_ref[...], acci_ref[...]))
        cr_ref[...] = cr
        ci_ref[...] = ci
        accr_ref[...] = accr
        acci_ref[...] = acci
        kr, ki = kr_ref[...].astype(bf16), ki_ref[...].astype(bf16)
        du = _dot_nt(kr, bre_ref[...]) + _dot_nt(ki, bim_ref[...]) + d_ref[...] * dyv
        du_ref[...] = du.astype(du_ref.dtype)
        dd_ref[...] += jnp.sum(dyv * uv.astype(f32), axis=0, keepdims=True)
        dbre_ref[...] += _dot_tn(uv, kr)
        dbim_ref[...] += _dot_tn(uv, ki)
        dcre_ref[...] += _dot_tn(hr_ref[...].astype(bf16), dyb)
        dcim_ref[...] += _dot_tn(hi_ref[...].astype(bf16), dyb)

        @pl.when(i == nt - 1)
        def _():
            dar_ref[...] = jnp.sum(accr_ref[...], axis=0, keepdims=True)
            dai_ref[...] = jnp.sum(acci_ref[...], axis=0, keepdims=True)

    lane = pl.BlockSpec((tt, HEAD), lambda j, i: (nt - 1 - i, j))
    st = pl.BlockSpec((tt, ns), lambda j, i: (nt - 1 - i, j))
    b3 = pl.BlockSpec((None, HEAD, ns), lambda j, i: (j, 0, 0))
    c3 = pl.BlockSpec((None, ns, HEAD), lambda j, i: (j, 0, 0))
    arow = pl.BlockSpec((1, ns), lambda j, i: (0, j))
    drow = pl.BlockSpec((1, HEAD), lambda j, i: (0, j))
    return pl.pallas_call(
        body, name="ssm_bwd", grid=(nj, nt),
        in_specs=[lane, lane, pl.BlockSpec(memory_space=pl.ANY), st, st, b3, b3, c3, c3, arow, arow, drow],
        out_specs=[lane, drow, arow, arow, b3, b3, c3, c3], input_output_aliases={2: 0},
        out_shape=[jax.ShapeDtypeStruct(dproj.shape, bf16), jax.ShapeDtypeStruct((1, w), f32),
                   jax.ShapeDtypeStruct((1, nj * ns), f32), jax.ShapeDtypeStruct((1, nj * ns), f32),
                   jax.ShapeDtypeStruct((nj, HEAD, ns), f32), jax.ShapeDtypeStruct((nj, HEAD, ns), f32),
                   jax.ShapeDtypeStruct((nj, ns, HEAD), f32), jax.ShapeDtypeStruct((nj, ns, HEAD), f32)],
        scratch_shapes=[pltpu.VMEM((tt, ns), f32), pltpu.VMEM((tt, ns), f32), pltpu.VMEM((1, ns), f32),
                        pltpu.VMEM((1, ns), f32), pltpu.VMEM((SUBLANES, ns), f32), pltpu.VMEM((SUBLANES, ns), f32)],
        compiler_params=_cparams("parallel", "arbitrary"),
    )(dy, u, dproj, h_re, h_im, bre3, bim3, cre3, cimn3, a_re, a_im, d_skip)


def glu_fwd(y, z_src, w_glu, b_glu):
    l, w = y.shape
    tm = _row_tile(l)

    def body(y_ref, z_ref, w_ref, b_ref, g_ref, t_ref, o_ref):
        g = _gelu(y_ref[...])
        gb = g.astype(bf16)
        t = _dot(gb, w_ref[...]) + b_ref[...]
        g_ref[...] = gb
        t_ref[...] = t
        o_ref[...] = (g * jax.nn.sigmoid(t) * _silu(z_ref[...].astype(f32))).astype(o_ref.dtype)

    blk = pl.BlockSpec((tm, w), lambda i: (i, 0))
    return pl.pallas_call(
        body, name="glu_fwd", grid=(l // tm,),
        in_specs=[blk, pl.BlockSpec((tm, w), lambda i: (i, 1)), pl.BlockSpec((w, w), lambda i: (0, 0)), _row(w)],
        out_specs=[blk, blk, blk],
        out_shape=[jax.ShapeDtypeStruct((l, w), bf16), jax.ShapeDtypeStruct((l, w), f32),
                   jax.ShapeDtypeStruct((l, w), bf16)],
        compiler_params=_cparams("parallel"),
    )(y, z_src, w_glu, b_glu)


def glu_bwd(dout, y, t, z_src, w_glu):
    l, w = y.shape
    tm = _row_tile(l)

    def body(do_ref, y_ref, t_ref, z_ref, w_ref, dy_ref, dz_ref, dt_ref, db_ref):
        @pl.when(pl.program_id(0) == 0)
        def _():
            db_ref[...] = jnp.zeros_like(db_ref)

        yv, zv, dov = y_ref[...], z_ref[...].astype(f32), do_ref[...]
        g = _gelu(yv)
        sg = jax.nn.sigmoid(t_ref[...])
        dy2 = dov * _silu(zv)
        dz_ref[...] = (dov * g * sg * _silu_grad(zv)).astype(dz_ref.dtype)
        dt = dy2 * g * sg * (1.0 - sg)
        dtb = dt.astype(bf16)
        dt_ref[...] = dtb
        db_ref[...] += jnp.sum(dt, axis=0, keepdims=True)
        dg = dy2 * sg + _dot_nt(dtb, w_ref[...])
        dy_ref[...] = dg * _gelu_grad(yv)

    blk = pl.BlockSpec((tm, w), lambda i: (i, 0))
    return pl.pallas_call(
        body, name="glu_bwd", grid=(l // tm,),
        in_specs=[blk, blk, blk, pl.BlockSpec((tm, w), lambda i: (i, 1)), pl.BlockSpec((w, w), lambda i: (0, 0))],
        out_specs=[blk, pl.BlockSpec((tm, w), lambda i: (i, 1)), blk, _row(w)],
        out_shape=[jax.ShapeDtypeStruct((l, w), f32), jax.ShapeDtypeStruct((l, 2 * w), bf16),
                   jax.ShapeDtypeStruct((l, w), bf16), jax.ShapeDtypeStruct((1, w), f32)],
        compiler_params=_cparams("arbitrary"),
    )(dout, y, t, z_src, w_glu)


def _adamw(w, g, m, v):
    m = ADAM_B1 * m + (1.0 - ADAM_B1) * g
    v = ADAM_B2 * v + (1.0 - ADAM_B2) * (g * g)
    m_hat = m / (1.0 - ADAM_B1 ** ADAM_STEP)
    v_hat = v / (1.0 - ADAM_B2 ** ADAM_STEP)
    return -ADAM_LR * (m_hat / (jnp.sqrt(v_hat) + ADAM_EPS) + ADAM_WD * w), m, v


def adam_reduce(pieces, w, m, v, name):
    r, c = w.shape
    n = pieces.shape[0]
    tr = _tile(r, (256, 128, 64, 32, 16, 8))

    def body(p_ref, w_ref, m_ref, v_ref, g_ref, d_ref, nm_ref, nv_ref):
        g = p_ref[0].astype(f32)
        for s in range(1, n):
            g = g + p_ref[s].astype(f32)
        g_ref[...] = g
        d_ref[...], nm_ref[...], nv_ref[...] = _adamw(w_ref[...], g, m_ref[...], v_ref[...])

    blk = pl.BlockSpec((tr, c), lambda i: (i, 0))
    return pl.pallas_call(
        body, name=name, grid=(r // tr,),
        in_specs=[pl.BlockSpec((n, tr, c), lambda i: (0, i, 0)), blk, blk, blk],
        out_specs=[blk] * 4, out_shape=[jax.ShapeDtypeStruct((r, c), f32)] * 4,
        compiler_params=_cparams("parallel"),
    )(pieces, w, m, v)


def adam_w_mod(cond_t, dm, w, m, v):
    nl, d, cols = w.shape
    tr = _tile(d, (512, 256, 128))

    def body(c_ref, dm_ref, w_ref, m_ref, v_ref, g_ref, d_ref, nm_ref, nv_ref):
        g = jnp.dot(c_ref[...], dm_ref[...], preferred_element_type=f32, precision=lax.Precision.HIGHEST)
        g_ref[...] = g
        d_ref[...], nm_ref[...], nv_ref[...] = _adamw(w_ref[...], g, m_ref[...], v_ref[...])

    blk = pl.BlockSpec((None, tr, cols), lambda l, i: (l, i, 0))
    return pl.pallas_call(
        body, name="adam_w_mod", grid=(nl, d // tr),
        in_specs=[pl.BlockSpec((tr, N_DEV), lambda l, i: (i, 0)), pl.BlockSpec((None, N_DEV, cols), lambda l, i: (l, 0, 0)),
                  blk, blk, blk],
        out_specs=[blk] * 4, out_shape=[jax.ShapeDtypeStruct((nl, d, cols), f32)] * 4,
        compiler_params=_cparams("parallel", "parallel"),
    )(cond_t, dm, w, m, v)


def silu_rows(c_all):
    def body(c_ref, o_ref):
        o_ref[...] = _silu(c_ref[...])

    return pl.pallas_call(body, name="silu_rows", out_shape=jax.ShapeDtypeStruct(c_all.shape, f32))(c_all)


def _block_diag(x):
    g, a, b = x.shape
    nj = g // GROUPS_PER_LANE_BLOCK
    eye = jnp.eye(GROUPS_PER_LANE_BLOCK, dtype=x.dtype)
    x5 = x.reshape(nj, GROUPS_PER_LANE_BLOCK, a, b)
    return jnp.einsum("jgab,gh->jgahb", x5, eye).reshape(nj, GROUPS_PER_LANE_BLOCK * a, GROUPS_PER_LANE_BLOCK * b)


def _diag_blocks(x, a, b):
    nj = x.shape[0]
    x5 = x.reshape(nj, GROUPS_PER_LANE_BLOCK, a, GROUPS_PER_LANE_BLOCK, b)
    eye = jnp.eye(GROUPS_PER_LANE_BLOCK, dtype=x.dtype)
    return jnp.einsum("jgahb,gh->jgab", x5, eye).reshape(nj * GROUPS_PER_LANE_BLOCK, a, b)


PACK_ROW = SUBLANES * HEAD


def _pack(parts, row_multiple=SUBLANES):
    rows = []
    for p in parts:
        flat = p.reshape(-1)
        pad = (-flat.shape[0]) % PACK_ROW
        if pad:
            flat = jnp.concatenate([flat, jnp.zeros((pad,), flat.dtype)])
        rows.append(flat.reshape(-1, HEAD))
    pad = (-sum(r.shape[0] for r in rows)) % row_multiple
    if pad:
        rows.append(jnp.zeros((pad, HEAD), rows[0].dtype))
    return jnp.concatenate(rows, axis=0)


def _unpack(packed, shapes):
    out, r0 = [], 0
    for shp in shapes:
        n = math.prod(shp)
        nr = -(-n // PACK_ROW) * SUBLANES
        out.append(packed[r0:r0 + nr].reshape(-1)[:n].reshape(shp))
        r0 += nr
    return out


def adam_small(g, w, m, v):
    r, c = w.shape

    def body(g_ref, w_ref, m_ref, v_ref, d_ref, nm_ref, nv_ref):
        d_ref[...], nm_ref[...], nv_ref[...] = _adamw(w_ref[...], g_ref[...], m_ref[...], v_ref[...])

    tr = max(t for t in range(SUBLANES, 1024 + 1, SUBLANES) if r % t == 0)
    blk = pl.BlockSpec((tr, c), lambda i: (i, 0))
    return pl.pallas_call(
        body, name="adam_small", grid=(r // tr,),
        in_specs=[blk] * 4, out_specs=[blk] * 3, out_shape=[jax.ShapeDtypeStruct((r, c), f32)] * 3,
        compiler_params=_cparams("parallel"),
    )(g, w, m, v)


def kernel(x, c, ln_pre_g, ln_post_g, w_mod, b_mod, w_in_ab, w_out_ab, sgu_norm_g, sgu_w, sgu_b, w_in_ssm, w_out_ssm, lam_re, lam_im, b_re, b_im, c_re, c_im, d_skip, log_dt, w_glu, b_glu, loss_target, m_ln_pre_g, m_ln_post_g, m_w_mod, m_b_mod, m_w_in_ab, m_w_out_ab, m_sgu_norm_g, m_sgu_w, m_sgu_b, m_w_in_ssm, m_w_out_ssm, m_lam_re, m_lam_im, m_b_re, m_b_im, m_c_re, m_c_im, m_d_skip, m_log_dt, m_w_glu, m_b_glu, v_ln_pre_g, v_ln_post_g, v_w_mod, v_b_mod, v_w_in_ab, v_w_out_ab, v_sgu_norm_g, v_sgu_w, v_sgu_b, v_w_in_ssm, v_w_out_ssm, v_lam_re, v_lam_im, v_b_re, v_b_im, v_c_re, v_c_im, v_d_skip, v_log_dt, v_w_glu, v_b_glu):
    me = _my_index()
    x0 = x[0]
    l, d = x0.shape
    target = loss_target[0]
    nh = sgu_w.shape[1]
    wa = nh * HEAD
    n_grp, n_st = lam_re.shape[1], lam_re.shape[2]
    mod_cols = w_mod.shape[2]

    c_all, d_skip_all, b_glu_all = all_gather([c, d_skip, b_glu], "gather_c")
    c_all = c_all.reshape(N_DEV, d)
    d_skip_all = d_skip_all.reshape(1, -1)
    b_glu_all = b_glu_all.reshape(1, -1)

    b_cols = lax.dynamic_slice_in_dim(b_mod, me * mod_cols, mod_cols, axis=1)
    (mod_all,) = all_gather([mod_part(c_all, w_mod, b_cols)], "gather_mod")
    def after(a, first):
        return a + jnp.minimum(jnp.abs(first[(0,) * first.ndim].astype(f32)), 0.0).astype(a.dtype)

    (win_ab3,) = sequencer_exchange(GATHER, [after(w_in_ab[0], mod_all).astype(bf16)], "gather_w_in", 1)
    mod_mine = lax.dynamic_index_in_dim(mod_all, me, axis=2, keepdims=False)
    mod_rows = jnp.transpose(mod_mine, (1, 0, 2)).reshape(2, 3, 1, d)

    def rows(a, i):
        return a[i].reshape(1, d)

    shift0, scale0, gate0 = mod_rows[0, 0], mod_rows[0, 1], mod_rows[0, 2]
    h0, h0_t = prenorm_fwd(x0, rows(ln_pre_g, 0), shift0, scale0, "prenorm0")
    wout_ab3, win_ssm3, wout_ssm3, wglu = sequencer_exchange(
        GATHER, [after(w, win_ab3).astype(bf16) for w in (w_out_ab[0], w_in_ssm[0], w_out_ssm[0], w_glu[0])],
        "gather_w_rest", 2)
    proj0 = mm_nn(h0, win_ab3, bf16, "proj0")
    sgu_b3 = sgu_b[0].reshape(nh, HEAD, 1)
    cat, att, tot = sb_fwd(proj0, sgu_fwd(proj0, sgu_norm_g, sgu_w[0], sgu_b3), nh)
    wout_ab3 = wout_ab3.reshape(1, d, d)
    win_ssm3 = win_ssm3.reshape(1, d, d)
    wglu = wglu.reshape(w_glu.shape[2], w_glu.shape[2])
    y0 = mm_nn(cat, wout_ab3, f32, "out0")

    shift1, scale1, gate1 = mod_rows[1, 0], mod_rows[1, 1], mod_rows[1, 2]
    x1, h1, h1_t = post_prenorm_fwd(x0, y0, gate0, rows(ln_post_g, 0), rows(ln_pre_g, 1), shift1, scale1,
                                    "post0_prenorm1")
    proj1 = mm_nn(h1, win_ssm3, bf16, "proj1")
    w_ssm = proj1.shape[1] // 2
    ldt = log_dt[0].reshape(n_grp, 1)
    bt_re = jnp.transpose(b_re[0], (0, 2, 1))
    bt_im = jnp.transpose(b_im[0], (0, 2, 1))
    a_re, a_im, bbt_re, bbt_im = s5_params_fwd(lam_re[0], lam_im[0], ldt, bt_re, bt_im)
    bre3 = _block_diag(bbt_re).astype(bf16)
    bim3 = _block_diag(bbt_im).astype(bf16)
    cre3 = _block_diag(jnp.transpose(c_re[0], (0, 2, 1))).astype(bf16)
    cimn3 = _block_diag(-jnp.transpose(c_im[0], (0, 2, 1))).astype(bf16)
    a_re_row, a_im_row = a_re.reshape(1, -1), a_im.reshape(1, -1)
    y_ssm, hs_re, hs_im = ssm_fwd(proj1, bre3, bim3, cre3, cimn3, a_re_row, a_im_row, d_skip_all)
    g_act, t_glu, mix1 = glu_fwd(y_ssm, proj1, wglu, b_glu_all)
    y1 = mm_nn(mix1, wout_ssm3, f32, "out1")

    dx2, loss_tile, dy1, dgate1, dgpost1 = final_loss(x1, y1, gate1, rows(ln_post_g, 1), target)

    dmix1 = mm_nt(dy1, wout_ssm3, f32, "dmix1")
    gw_out_ssm = mm_tn(mix1, dy1, N_DEV, bf16, "gw_out_ssm")
    (p_out_ssm,) = sequencer_exchange(SCATTER, [gw_out_ssm], "scatter_g1", 3)
    dy_ssm, dproj1, dt_glu, db_glu = glu_bwd(dmix1, y_ssm, t_glu, proj1, wglu)
    gw_glu = mm_tn(g_act, dt_glu, 1, bf16, "gw_glu").reshape(N_DEV, -1, w_ssm)
    dproj1, dd_skip, da_re, da_im, dbre3, dbim3, dcre3, dcimn3 = ssm_bwd(
        dy_ssm, proj1, dproj1, hs_re, hs_im, bre3, bim3, cre3, cimn3, a_re_row, a_im_row, d_skip_all)
    gw_in_ssm = mm_nn(h1_t, dproj1[None], bf16, "gw_in_ssm").reshape(N_DEV, -1, proj1.shape[1])
    p_in_ssm, p_glu = sequencer_exchange(SCATTER, [gw_in_ssm, gw_glu], "scatter_g2", 4)
    dh1 = mm_nt(dproj1, win_ssm3, f32, "dh1")
    dx1, dshift1, dscale1, dgpre1 = prenorm_bwd(dh1, x1, dx2, rows(ln_pre_g, 1), scale1, "prenorm1_bwd")
    dlr, dli, dldt, dbt_re, dbt_im = s5_params_bwd(
        lam_re[0], lam_im[0], ldt, bt_re, bt_im, da_re.reshape(n_grp, n_st), da_im.reshape(n_grp, n_st),
        _diag_blocks(dbre3, SSM_GROUP, n_st), _diag_blocks(dbim3, SSM_GROUP, n_st))
    g_b_re = jnp.transpose(dbt_re, (0, 2, 1))
    g_b_im = jnp.transpose(dbt_im, (0, 2, 1))
    g_c_re = jnp.transpose(_diag_blocks(dcre3, n_st, SSM_GROUP), (0, 2, 1))
    g_c_im = -jnp.transpose(_diag_blocks(dcimn3, n_st, SSM_GROUP), (0, 2, 1))

    dy0, dgate0, dgpost0 = post_bwd(dx1, y0, gate0, rows(ln_post_g, 0), "post0_bwd")
    dcat = mm_nt(dy0, wout_ab3, f32, "dcat")
    gw_out_ab = mm_tn(cat, dy0, 1, bf16, "gw_out_ab").reshape(N_DEV, -1, d)
    (p_out_ab,) = sequencer_exchange(SCATTER, [gw_out_ab], "scatter_g3", 5)
    da, dsgu_w, dsgu_b, dsgu_ng = sgu_bwd(proj0, dcat, sgu_norm_g, sgu_w[0], sgu_b3)
    dq, dk, dv, dbz = sb_bwd(proj0, dcat, att, tot, nh)
    dproj0 = jnp.concatenate([da, dq, dk, dv, dbz], axis=1)
    gw_in_ab = mm_nn(h0_t, dproj0[None], bf16, "gw_in_ab", split_cols=N_DEV)
    (p_in_ab,) = sequencer_exchange(SCATTER, [gw_in_ab], "scatter_g4", 6)
    dh0 = mm_nt(dproj0, win_ab3, f32, "dh0")
    dx0, dshift0, dscale0, dgpre0 = prenorm_bwd(dh0, x0, dx1, rows(ln_pre_g, 0), scale0, "prenorm0_bwd")

    small_names = ["ln_pre_g", "ln_post_g", "b_mod", "sgu_norm_g", "sgu_w", "sgu_b", "lam_re", "lam_im", "b_re", "b_im",
                   "c_re", "c_im", "log_dt"]
    small_w = [ln_pre_g, ln_post_g, b_mod, sgu_norm_g, sgu_w, sgu_b, lam_re, lam_im, b_re, b_im, c_re, c_im, log_dt]
    small_m = [m_ln_pre_g, m_ln_post_g, m_b_mod, m_sgu_norm_g, m_sgu_w, m_sgu_b, m_lam_re, m_lam_im, m_b_re, m_b_im,
               m_c_re, m_c_im, m_log_dt]
    small_v = [v_ln_pre_g, v_ln_post_g, v_b_mod, v_sgu_norm_g, v_sgu_w, v_sgu_b, v_lam_re, v_lam_im, v_b_re, v_b_im,
               v_c_re, v_c_im, v_log_dt]
    dmod = jnp.concatenate([dshift0, dscale0, dgate0, dshift1, dscale1, dgate1], axis=1)
    small_g = [jnp.concatenate([dgpre0, dgpre1]), jnp.concatenate([dgpost0, dgpost1]), dmod, dsgu_ng, dsgu_w, dsgu_b,
               dlr, dli, g_b_re, g_b_im, g_c_re, g_c_im, dldt]
    shapes = [w.shape for w in small_w]
    g_sum, dmod_all = all_reduce_rows(_pack(small_g + [dd_skip, db_glu, loss_tile], SUBLANES * N_DEV), dmod,
                                      "reduce_small_grads")
    n_rows_small = sum(-(-math.prod(s) // PACK_ROW) * SUBLANES for s in shapes)
    loss = g_sum[n_rows_small + 2 * (d_skip_all.shape[1] // HEAD), 0] * (0.5 / d)
    new_small = adam_small(g_sum, _pack(small_w), _pack(small_m), _pack(small_v))
    r_small = [_unpack(o, shapes) for o in [g_sum[:n_rows_small]] + list(new_small)]
    small = {n: [r_small[k][i] for k in range(4)] for i, n in enumerate(small_names)}
    vec_rows = d_skip_all.shape[1] // HEAD

    def my_columns(r0):
        whole = g_sum[r0:r0 + vec_rows].reshape(1, 1, -1)
        return lax.dynamic_slice_in_dim(whole, me * d_skip.shape[1], d_skip.shape[1], axis=2)

    def sharded(p, w, m, v, name):
        shp = w.shape
        w2, m2, v2 = (a.reshape(-1, shp[-1]) for a in (w, m, v))
        return [o.reshape(shp) for o in adam_reduce(p.reshape(p.shape[0], -1, shp[-1]), w2, m2, v2, name)]

    r_d_skip = sharded(my_columns(n_rows_small), d_skip, m_d_skip, v_d_skip, "adam_d_skip")
    r_b_glu = sharded(my_columns(n_rows_small + vec_rows), b_glu, m_b_glu, v_b_glu, "adam_b_glu")
    r_w_out_ssm = sharded(p_out_ssm, w_out_ssm, m_w_out_ssm, v_w_out_ssm, "adam_w_out_ssm")
    r_w_in_ssm = sharded(p_in_ssm, w_in_ssm, m_w_in_ssm, v_w_in_ssm, "adam_w_in_ssm")
    r_w_glu = sharded(p_glu, w_glu, m_w_glu, v_w_glu, "adam_w_glu")
    r_w_out_ab = sharded(p_out_ab, w_out_ab, m_w_out_ab, v_w_out_ab, "adam_w_out_ab")
    r_w_in_ab = sharded(p_in_ab, w_in_ab, m_w_in_ab, v_w_in_ab, "adam_w_in_ab")

    dm_cols = jnp.transpose(
        lax.dynamic_slice_in_dim(dmod_all.reshape(N_DEV, 2, 3 * d), me * mod_cols, mod_cols, axis=2), (1, 0, 2))
    cond_t = jnp.transpose(silu_rows(c_all))
    r_w_mod = adam_w_mod(cond_t, dm_cols, w_mod, m_w_mod, v_w_mod)

    res = dict(small)
    res.update(w_mod=r_w_mod, w_in_ab=r_w_in_ab, w_out_ab=r_w_out_ab, w_in_ssm=r_w_in_ssm, w_out_ssm=r_w_out_ssm,
               d_skip=r_d_skip, w_glu=r_w_glu, b_glu=r_b_glu)
    order = ["ln_pre_g", "ln_post_g", "w_mod", "b_mod", "w_in_ab", "w_out_ab", "sgu_norm_g", "sgu_w", "sgu_b", "w_in_ssm",
             "w_out_ssm", "lam_re", "lam_im", "b_re", "b_im", "c_re", "c_im", "d_skip", "log_dt", "w_glu", "b_glu"]
    outs = [loss, dx0.reshape(x.shape)]
    for k in range(4):
        outs += [res[n][k] for n in order]
    return tuple(outs)
```

```python
import functools
import math

import jax
import jax.numpy as jnp
from jax import lax
from jax.experimental import pallas as pl
from jax.experimental.pallas import tpu as pltpu
from jax.experimental.pallas import tpu_sc as plsc

f32 = jnp.float32
bf16 = jnp.bfloat16

N_DEV = 8
EPS = 1e-6
HEAD = 128
SUBLANES = 8
SSM_GROUP = 16
SSM_STATE = 64
GROUPS_PER_LANE_BLOCK = HEAD // SSM_GROUP
STATES_PER_LANE_BLOCK = GROUPS_PER_LANE_BLOCK * SSM_STATE
VMEM_LIMIT = 56 * 2 ** 20
ADAM_LR, ADAM_B1, ADAM_B2, ADAM_EPS, ADAM_WD, ADAM_STEP = 0.001, 0.9, 0.999, 1e-08, 0.01, 10
_GELU_C0 = math.sqrt(2.0 / math.pi)
_GELU_C1 = 0.044715
MESH = pl.DeviceIdType.MESH


def _cparams(*sem):
    return pltpu.CompilerParams(dimension_semantics=sem if sem else None, vmem_limit_bytes=VMEM_LIMIT)


def _gelu(x):
    return 0.5 * x * (1.0 + jnp.tanh(_GELU_C0 * (x + _GELU_C1 * x * x * x)))


def _gelu_grad(x):
    t = jnp.tanh(_GELU_C0 * (x + _GELU_C1 * x * x * x))
    return 0.5 * (1.0 + t) + 0.5 * x * (1.0 - t * t) * _GELU_C0 * (1.0 + 3.0 * _GELU_C1 * x * x)


def _silu(x):
    return x * jax.nn.sigmoid(x)


def _silu_grad(x):
    s = jax.nn.sigmoid(x)
    return s * (1.0 + x * (1.0 - s))


def _dot(a, b):
    return jnp.dot(a, b, preferred_element_type=f32)


def _dot_nt(a, b):
    return lax.dot_general(a, b, (((1,), (1,)), ((), ())), preferred_element_type=f32)


def _dot_tn(a, b):
    return lax.dot_general(a, b, (((0,), (0,)), ((), ())), preferred_element_type=f32)


def _split_bf16(v):
    hi = v.astype(bf16)
    lo = (v - hi.astype(f32)).astype(bf16)
    return hi, lo


def _row(d):
    return pl.BlockSpec((1, d), lambda *_: (0, 0))


def _my_index():
    return 4 * lax.axis_index("x") + 2 * lax.axis_index("y") + lax.axis_index("c")


def _peer(k):
    x, y, c = lax.axis_index("x"), lax.axis_index("y"), lax.axis_index("c")
    return (1 - x if k & 4 else x, 1 - y if k & 2 else y, 1 - c if k & 1 else c)


def all_gather(arrs, name):
    n = len(arrs)

    def body(*refs):
        ins, outs = refs[:n], refs[n:2 * n]
        send, recv, local = refs[2 * n:]
        me = _my_index()
        copies = []
        for a in range(n):
            cp = pltpu.make_async_copy(ins[a], outs[a].at[me], local.at[a])
            cp.start()
            copies.append(cp)
            for k in range(1, N_DEV):
                s = a * (N_DEV - 1) + k - 1
                cp = pltpu.make_async_remote_copy(src_ref=ins[a], dst_ref=outs[a].at[me], send_sem=send.at[s],
                                                  recv_sem=recv.at[s], device_id=_peer(k), device_id_type=MESH)
                cp.start()
                copies.append(cp)
        for cp in copies:
            cp.wait()

    any_spec = pl.BlockSpec(memory_space=pl.ANY)
    outs = pl.pallas_call(
        body, name=name,
        out_shape=[jax.ShapeDtypeStruct((N_DEV,) + a.shape, a.dtype) for a in arrs],
        in_specs=[any_spec] * n, out_specs=[any_spec] * n,
        scratch_shapes=[pltpu.SemaphoreType.DMA((n * (N_DEV - 1),)), pltpu.SemaphoreType.DMA((n * (N_DEV - 1),)),
                        pltpu.SemaphoreType.DMA((n,))],
        compiler_params=pltpu.CompilerParams(has_side_effects=True),
    )(*arrs)
    return list(outs)


def all_reduce_rows(pack, extra, name):
    r, c = pack.shape
    rs = r // N_DEV
    n_peer = N_DEV - 1

    def body(p_ref, x_ref, o_ref, xo_ref, land, red, send1, recv1, send2, recv2, sendx, recvx, local):
        me = _my_index()

        def rows(i):
            return pl.ds(pl.multiple_of(i * rs, SUBLANES), rs)

        own = [pltpu.make_async_copy(p_ref.at[rows(me)], land.at[me], local.at[0]),
               pltpu.make_async_copy(x_ref, xo_ref.at[me], local.at[1])]
        first = []
        for k in range(1, N_DEV):
            first.append(pltpu.make_async_remote_copy(
                src_ref=p_ref.at[rows(jnp.bitwise_xor(me, k))], dst_ref=land.at[me], send_sem=send1.at[k - 1],
                recv_sem=recv1.at[k - 1], device_id=_peer(k), device_id_type=MESH))
            first.append(pltpu.make_async_remote_copy(
                src_ref=x_ref, dst_ref=xo_ref.at[me], send_sem=sendx.at[k - 1], recv_sem=recvx.at[k - 1],
                device_id=_peer(k), device_id_type=MESH))
        for cp in own + first:
            cp.start()
        for cp in own + first:
            cp.wait()
        acc = land[0]
        for s in range(1, N_DEV):
            acc = acc + land[s]
        red[...] = acc
        mine = pltpu.make_async_copy(red, o_ref.at[rows(me)], local.at[2])
        second = [pltpu.make_async_remote_copy(
            src_ref=red, dst_ref=o_ref.at[rows(me)], send_sem=send2.at[k - 1], recv_sem=recv2.at[k - 1],
            device_id=_peer(k), device_id_type=MESH) for k in range(1, N_DEV)]
        for cp in [mine] + second:
            cp.start()
        for cp in [mine] + second:
            cp.wait()

    any_spec = pl.BlockSpec(memory_space=pl.ANY)
    return pl.pallas_call(
        body, name=name,
        out_shape=[jax.ShapeDtypeStruct((r, c), pack.dtype), jax.ShapeDtypeStruct((N_DEV,) + extra.shape, extra.dtype)],
        in_specs=[any_spec, any_spec], out_specs=[any_spec, any_spec],
        scratch_shapes=[pltpu.VMEM((N_DEV, rs, c), pack.dtype), pltpu.VMEM((rs, c), pack.dtype)]
        + [pltpu.SemaphoreType.DMA((n_peer,))] * 6 + [pltpu.SemaphoreType.DMA((3,))],
        compiler_params=pltpu.CompilerParams(has_side_effects=True),
    )(pack, extra)


GATHER, SCATTER = "gather", "scatter"


def _exchange_copies(srcs, lands, send, recv):
    me = _my_index()
    copies = []
    for a, (src, land) in enumerate(zip(srcs, lands)):
        for k in range(1, N_DEV):
            s = a * (N_DEV - 1) + k - 1
            copies.append(pltpu.make_async_remote_copy(
                src_ref=src.at[jnp.bitwise_xor(me, k)], dst_ref=land.at[me],
                send_sem=send.at[s], recv_sem=recv.at[s], device_id=_peer(k), device_id_type=MESH))
    return copies


def sequencer_exchange(kind, arrs, name, collective_id):
    n = len(arrs)
    n_sem = n * (N_DEV - 1)
    land_shapes = [((N_DEV,) + a.shape if kind == GATHER else a.shape) for a in arrs]
    srcs = [jax.new_ref(a, memory_space=pltpu.MemorySpace.HBM) for a in arrs]
    lands = [jax.empty_ref(jax.ShapeDtypeStruct(s, a.dtype), memory_space=pltpu.MemorySpace.HBM)
             for s, a in zip(land_shapes, arrs)]

    @pl.kernel(mesh=plsc.ScalarSubcoreMesh(axis_name="sequencer", num_cores=1), name=name,
               scratch_types=(pltpu.SemaphoreType.DMA((n_sem,)), pltpu.SemaphoreType.DMA((n_sem,)),
                              pltpu.SemaphoreType.DMA((n,))),
               compiler_params=pltpu.CompilerParams(collective_id=collective_id))
    def launch(send, recv, local):
        barrier = pltpu.get_barrier_semaphore()
        for k in range(1, N_DEV):
            pl.semaphore_signal(barrier, inc=1, device_id=_peer(k), device_id_type=MESH)
        pl.semaphore_wait(barrier, N_DEV - 1)
        me = _my_index()
        mine = [pltpu.make_async_copy(src if kind == GATHER else src.at[me], land.at[me], local.at[a])
                for a, (src, land) in enumerate(zip(srcs, lands))]
        if kind == SCATTER:
            copies = mine + _exchange_copies(srcs, lands, send, recv)
            for cp in copies:
                cp.start()
            for cp in copies:
                cp.wait()
            return

        def block_copy(a, slot, block, k, src=None):
            s = a * (N_DEV - 1) + slot
            return pltpu.make_async_remote_copy(
                src_ref=lands[a].at[block] if src is None else src, dst_ref=lands[a].at[block],
                send_sem=send.at[s], recv_sem=recv.at[s], device_id=_peer(k), device_id_type=MESH)

        chips = (2, 4, 6)
        sibling = jnp.bitwise_xor(me, 1)
        first = [block_copy(a, slot, me, k, src=srcs[a]) for a in range(n) for slot, k in enumerate((1,) + chips)]
        for cp in mine + first:
            cp.start()
        passed = []
        for a in range(n):
            for i, k in enumerate(chips):
                block = jnp.bitwise_xor(me, k)
                block_copy(a, 1 + i, block, k).wait_recv()
                passed.append(block_copy(a, 4 + i, block, 1))
                passed[-1].start()
        for a in range(n):
            block_copy(a, 0, sibling, 1).wait_recv()
            for i, k in enumerate(chips):
                block_copy(a, 4 + i, jnp.bitwise_xor(sibling, k), 1).wait_recv()
        for cp in mine:
            cp.wait()
        for cp in first + passed:
            cp.wait_send()

    launch()
    return [land[...] for land in lands]


def _tile(n, pref):
    for t in pref:
        if n % t == 0:
            return t
    return n


MM_WIDE = 1024
MM_WEIGHT_BLOCK = 8 * 2 ** 20


def _blocks_per_step(nb, fits):
    return max(g for g in range(1, nb + 1) if nb % g == 0 and fits(g))


def mm_nn(a, b3, out_dtype, name, split_cols=None):
    m, k = a.shape
    nb, _, bn = b3.shape
    tm = _tile(m, (512, 256, 128))
    tn = bn // split_cols if split_cols else _tile(bn, (1024, 896, 512, 256, 128))
    per = bn // tn
    gb = _blocks_per_step(nb, lambda g: g == 1 or (per == 1 and g * bn <= MM_WIDE))

    def body(a_ref, b_ref, o_ref):
        for g in range(gb):
            o_ref[:, g * tn:(g + 1) * tn] = _dot(a_ref[...], b_ref[g]).astype(o_ref.dtype)

    if split_cols:
        out_spec = pl.BlockSpec((None, tm, tn), lambda i, j, jj: (jj, i, 0))
        out_shape = jax.ShapeDtypeStruct((split_cols, m, tn), out_dtype)
    else:
        out_spec = pl.BlockSpec((tm, gb * tn), lambda i, j, jj: (i, j * per + jj))
        out_shape = jax.ShapeDtypeStruct((m, nb * bn), out_dtype)
    return pl.pallas_call(
        body, name=name, grid=(m // tm, nb // gb, per),
        in_specs=[pl.BlockSpec((tm, k), lambda i, j, jj: (i, 0)),
                  pl.BlockSpec((gb, k, tn), lambda i, j, jj: (j, 0, jj))],
        out_specs=out_spec, out_shape=out_shape,
        compiler_params=_cparams("parallel", "arbitrary", "arbitrary"),
    )(a, b3)


def mm_nt(a, w3, out_dtype, name):
    m, _ = a.shape
    nb, ko, bn = w3.shape
    tm = _tile(m, (512, 256, 128))
    tko = _tile(ko, (1024, 512, 256, 128))
    gb = _blocks_per_step(nb, lambda g: g * tko * bn * w3.dtype.itemsize <= MM_WEIGHT_BLOCK)
    ns = nb // gb

    def body(a_ref, w_ref, o_ref, acc_ref):
        j = pl.program_id(2)

        @pl.when(j == 0)
        def _():
            acc_ref[...] = jnp.zeros_like(acc_ref)

        part = _dot_nt(a_ref[:, :bn], w_ref[0])
        for g in range(1, gb):
            part += _dot_nt(a_ref[:, g * bn:(g + 1) * bn], w_ref[g])
        acc_ref[...] += part

        @pl.when(j == ns - 1)
        def _():
            o_ref[...] = acc_ref[...].astype(o_ref.dtype)

    return pl.pallas_call(
        body, name=name, grid=(m // tm, ko // tko, ns),
        in_specs=[pl.BlockSpec((tm, gb * bn), lambda i, o, j: (i, j)),
                  pl.BlockSpec((gb, tko, bn), lambda i, o, j: (j, o, 0))],
        out_specs=pl.BlockSpec((tm, tko), lambda i, o, j: (i, o)),
        out_shape=jax.ShapeDtypeStruct((m, ko), out_dtype),
        scratch_shapes=[pltpu.VMEM((tm, tko), f32)],
        compiler_params=_cparams("parallel", "arbitrary", "arbitrary"),
    )(a, w3)


def mm_tn(a, dy, ncb, out_dtype, name):
    l, ka = a.shape
    _, n = dy.shape
    bn = n // ncb
    tl = _tile(l, (1024, 512, 256, 128))
    tka = _tile(ka, (512, 256, 128))
    tn = _tile(bn, (1024, 896, 512, 256, 128))
    per = bn // tn
    gb = _blocks_per_step(ncb, lambda g: g == 1 or (per == 1 and g * bn <= MM_WIDE))
    nl = l // tl

    def body(a_ref, dy_ref, o_ref, acc_ref):
        s = pl.program_id(2)

        @pl.when(s == 0)
        def _():
            acc_ref[...] = jnp.zeros_like(acc_ref)

        acc_ref[...] += _dot_tn(a_ref[...], dy_ref[...])

        @pl.when(s == nl - 1)
        def _():
            for g in range(gb):
                o_ref[g] = acc_ref[:, g * tn:(g + 1) * tn].astype(o_ref.dtype)

    return pl.pallas_call(
        body, name=name, grid=(ka // tka, n // (gb * tn), nl),
        in_specs=[pl.BlockSpec((tl, tka), lambda i, j, s: (s, i)),
                  pl.BlockSpec((tl, gb * tn), lambda i, j, s: (s, j))],
        out_specs=pl.BlockSpec((gb, tka, tn), lambda i, j, s: (j // per, i, j % per)),
        out_shape=jax.ShapeDtypeStruct((ncb, ka, bn), out_dtype),
        scratch_shapes=[pltpu.VMEM((tka, gb * tn), f32)],
        compiler_params=_cparams("parallel", "parallel", "arbitrary"),
    )(a, dy)


def mod_part(c_all, w_mod, b_cols):
    nl, d, cols = w_mod.shape

    def body(c_ref, w_ref, b_ref, o_ref):
        cond = _silu(c_ref[...]).astype(bf16)
        o_ref[...] = _dot(cond, w_ref[...].astype(bf16)) + b_ref[...]

    return pl.pallas_call(
        body, name="mod_part", grid=(nl,),
        in_specs=[pl.BlockSpec((N_DEV, d), lambda l: (0, 0)),
                  pl.BlockSpec((None, d, cols), lambda l: (l, 0, 0)),
                  pl.BlockSpec((None, 1, cols), lambda l: (l, 0, 0))],
        out_specs=pl.BlockSpec((None, N_DEV, cols), lambda l: (l, 0, 0)),
        out_shape=jax.ShapeDtypeStruct((nl, N_DEV, cols), f32),
        compiler_params=_cparams("arbitrary"),
    )(c_all, w_mod, b_cols.reshape(nl, 1, cols))


def _row_tile(l):
    return _tile(l, (512, 256, 128))


def _entry_rows(xv, g_ref, sh_ref, sc_ref, h_ref, ht_ref):
    r = lax.rsqrt(jnp.mean(xv * xv, axis=-1, keepdims=True) + EPS)
    h = xv * r * (g_ref[...] * (1.0 + sc_ref[...])) + sh_ref[...]
    h_ref[...] = h.astype(h_ref.dtype)
    ht_ref[...] = jnp.transpose(h).astype(ht_ref.dtype)


def prenorm_fwd(x, g, shift, scale, name):
    l, d = x.shape
    tm = _row_tile(l)

    def body(x_ref, g_ref, sh_ref, sc_ref, h_ref, ht_ref):
        _entry_rows(x_ref[...], g_ref, sh_ref, sc_ref, h_ref, ht_ref)

    return pl.pallas_call(
        body, name=name, grid=(l // tm,),
        in_specs=[pl.BlockSpec((tm, d), lambda i: (i, 0)), _row(d), _row(d), _row(d)],
        out_specs=[pl.BlockSpec((tm, d), lambda i: (i, 0)), pl.BlockSpec((d, tm), lambda i: (0, i))],
        out_shape=[jax.ShapeDtypeStruct((l, d), bf16), jax.ShapeDtypeStruct((d, l), bf16)],
        compiler_params=_cparams("parallel"),
    )(x, g, shift, scale)


def post_prenorm_fwd(x, y, gate, g_post, g_pre, shift, scale, name):
    l, d = x.shape
    tm = _row_tile(l)

    def body(x_ref, y_ref, gate_ref, gp_ref, g_ref, sh_ref, sc_ref, o_ref, h_ref, ht_ref):
        yv = y_ref[...]
        r = lax.rsqrt(jnp.mean(yv * yv, axis=-1, keepdims=True) + EPS)
        xv = x_ref[...] + gate_ref[...] * (yv * r * gp_ref[...])
        o_ref[...] = xv
        _entry_rows(xv, g_ref, sh_ref, sc_ref, h_ref, ht_ref)

    blk = pl.BlockSpec((tm, d), lambda i: (i, 0))
    return pl.pallas_call(
        body, name=name, grid=(l // tm,),
        in_specs=[blk, blk] + [_row(d)] * 5, out_specs=[blk, blk, pl.BlockSpec((d, tm), lambda i: (0, i))],
        out_shape=[jax.ShapeDtypeStruct((l, d), f32), jax.ShapeDtypeStruct((l, d), bf16),
                   jax.ShapeDtypeStruct((d, l), bf16)],
        compiler_params=_cparams("parallel"),
    )(x, y, gate, g_post, g_pre, shift, scale)


def _post_bwd_rows(dxv, yv, r, gate, gv, dy_ref, dgate_ref, dg_ref):
    yn = yv * r
    dgate_ref[...] += jnp.sum(dxv * yn * gv, axis=0, keepdims=True)
    dyg = dxv * gate
    dg_ref[...] += jnp.sum(dyg * yn, axis=0, keepdims=True)
    dyn = dyg * gv
    dy_ref[...] = (r * (dyn - yn * jnp.mean(dyn * yn, axis=-1, keepdims=True))).astype(dy_ref.dtype)


def final_loss(x, y, gate, g, target):
    l, d = x.shape
    tm = _row_tile(l)

    def body(x_ref, y_ref, gate_ref, g_ref, t_ref, dx_ref, loss_ref, dy_ref, dgate_ref, dg_ref):
        @pl.when(pl.program_id(0) == 0)
        def _():
            loss_ref[...] = jnp.zeros_like(loss_ref)
            dgate_ref[...] = jnp.zeros_like(dgate_ref)
            dg_ref[...] = jnp.zeros_like(dg_ref)

        yv, gate, gv = y_ref[...], gate_ref[...], g_ref[...]
        r = lax.rsqrt(jnp.mean(yv * yv, axis=-1, keepdims=True) + EPS)
        diff = x_ref[...] + gate * (yv * r * gv) - t_ref[...]
        dxv = diff * (1.0 / d)
        dx_ref[...] = dxv
        loss_ref[...] += jnp.sum(diff * diff)
        _post_bwd_rows(dxv, yv, r, gate, gv, dy_ref, dgate_ref, dg_ref)

    blk = pl.BlockSpec((tm, d), lambda i: (i, 0))
    return pl.pallas_call(
        body, name="final_loss", grid=(l // tm,),
        in_specs=[blk, blk, _row(d), _row(d), blk],
        out_specs=[blk, pl.BlockSpec((SUBLANES, HEAD), lambda i: (0, 0)), blk, _row(d), _row(d)],
        out_shape=[jax.ShapeDtypeStruct((l, d), f32), jax.ShapeDtypeStruct((SUBLANES, HEAD), f32),
                   jax.ShapeDtypeStruct((l, d), bf16), jax.ShapeDtypeStruct((1, d), f32), jax.ShapeDtypeStruct((1, d), f32)],
        compiler_params=_cparams("arbitrary"),
    )(x, y, gate, g, target)


def post_bwd(dx, y, gate, g, name):
    l, d = dx.shape
    tm = _row_tile(l)

    def body(dx_ref, y_ref, gate_ref, g_ref, dy_ref, dgate_ref, dg_ref):
        @pl.when(pl.program_id(0) == 0)
        def _():
            dgate_ref[...] = jnp.zeros_like(dgate_ref)
            dg_ref[...] = jnp.zeros_like(dg_ref)

        yv = y_ref[...]
        r = lax.rsqrt(jnp.mean(yv * yv, axis=-1, keepdims=True) + EPS)
        _post_bwd_rows(dx_ref[...], yv, r, gate_ref[...], g_ref[...], dy_ref, dgate_ref, dg_ref)

    blk = pl.BlockSpec((tm, d), lambda i: (i, 0))
    return pl.pallas_call(
        body, name=name, grid=(l // tm,),
        in_specs=[blk, blk, _row(d), _row(d)], out_specs=[blk, _row(d), _row(d)],
        out_shape=[jax.ShapeDtypeStruct((l, d), bf16), jax.ShapeDtypeStruct((1, d), f32),
                   jax.ShapeDtypeStruct((1, d), f32)],
        compiler_params=_cparams("arbitrary"),
    )(dx, y, gate, g)


def prenorm_bwd(dh, x, dx_next, g, scale, name):
    l, d = x.shape
    tm = _row_tile(l)

    def body(dh_ref, x_ref, dxn_ref, g_ref, sc_ref, dx_ref, dsh_ref, dsc_ref, dg_ref):
        @pl.when(pl.program_id(0) == 0)
        def _():
            dsh_ref[...] = jnp.zeros_like(dsh_ref)
            dsc_ref[...] = jnp.zeros_like(dsc_ref)
            dg_ref[...] = jnp.zeros_like(dg_ref)

        xv, dhv, gv, sc1 = x_ref[...], dh_ref[...], g_ref[...], 1.0 + sc_ref[...]
        r = lax.rsqrt(jnp.mean(xv * xv, axis=-1, keepdims=True) + EPS)
        xn = xv * r
        dhx = dhv * xn
        dsh_ref[...] += jnp.sum(dhv, axis=0, keepdims=True)
        dsc_ref[...] += jnp.sum(dhx * gv, axis=0, keepdims=True)
        dg_ref[...] += jnp.sum(dhx * sc1, axis=0, keepdims=True)
        dxn = dhv * (gv * sc1)
        dx_ref[...] = dxn_ref[...] + r * (dxn - xn * jnp.mean(dxn * xn, axis=-1, keepdims=True))

    blk = pl.BlockSpec((tm, d), lambda i: (i, 0))
    return pl.pallas_call(
        body, name=name, grid=(l // tm,),
        in_specs=[blk, blk, blk, _row(d), _row(d)], out_specs=[blk, _row(d), _row(d), _row(d)],
        out_shape=[jax.ShapeDtypeStruct((l, d), f32)] + [jax.ShapeDtypeStruct((1, d), f32)] * 3,
        compiler_params=_cparams("arbitrary"),
    )(dh, x, dx_next, g, scale)


def _tril_mask():
    r = lax.broadcasted_iota(jnp.int32, (HEAD, HEAD), 0)
    c = lax.broadcasted_iota(jnp.int32, (HEAD, HEAD), 1)
    return r >= c


def sgu_fwd(proj, norm_g, w_s, b_s):
    l = proj.shape[0]
    nh = w_s.shape[0]
    wa = nh * HEAD

    def body(au_ref, av_ref, az_ref, ng_ref, w_ref, b_ref, o_ref):
        tril = _tril_mask()
        for h in range(nh):
            sl = slice(h * HEAD, (h + 1) * HEAD)
            gv = _gelu(av_ref[:, sl].astype(f32))
            r = lax.rsqrt(jnp.mean(gv * gv, axis=-1, keepdims=True) + EPS)
            vh = gv * r * ng_ref[:, sl]
            wm = jnp.where(tril, w_ref[h], 0.0).astype(bf16)
            s = _dot(wm, vh.astype(bf16)) + b_ref[h]
            o_ref[:, sl] = (_gelu(au_ref[:, sl].astype(f32)) * s * _silu(az_ref[:, sl].astype(f32))).astype(o_ref.dtype)

    def col(j):
        return pl.BlockSpec((HEAD, wa), lambda n: (n, j))

    return pl.pallas_call(
        body, name="sgu_fwd", grid=(l // HEAD,),
        in_specs=[col(0), col(1), col(2), _row(wa),
                  pl.BlockSpec((nh, HEAD, HEAD), lambda n: (0, 0, 0)), pl.BlockSpec((nh, HEAD, 1), lambda n: (0, 0, 0))],
        out_specs=pl.BlockSpec((HEAD, wa), lambda n: (n, 0)),
        out_shape=jax.ShapeDtypeStruct((l, 2 * wa), bf16),
        compiler_params=_cparams("parallel"),
    )(proj, proj, proj, norm_g, w_s, b_s)


def sgu_bwd(proj, dcat, norm_g, w_s, b_s):
    l = proj.shape[0]
    nh = w_s.shape[0]
    wa = nh * HEAD

    def body(au_ref, av_ref, az_ref, do_ref, ng_ref, w_ref, b_ref, da_ref, dw_ref, db_ref, dng_ref):
        @pl.when(pl.program_id(0) == 0)
        def _():
            dw_ref[...] = jnp.zeros_like(dw_ref)
            db_ref[...] = jnp.zeros_like(db_ref)
            dng_ref[...] = jnp.zeros_like(dng_ref)

        tril = _tril_mask()
        for h in range(nh):
            sl = slice(h * HEAD, (h + 1) * HEAD)
            au, av, az = au_ref[:, sl].astype(f32), av_ref[:, sl].astype(f32), az_ref[:, sl].astype(f32)
            ng = ng_ref[:, sl]
            gv = _gelu(av)
            r = lax.rsqrt(jnp.mean(gv * gv, axis=-1, keepdims=True) + EPS)
            gvn = gv * r
            vh = (gvn * ng).astype(bf16)
            wm = jnp.where(tril, w_ref[h], 0.0).astype(bf16)
            s = _dot(wm, vh) + b_ref[h]
            gu, sz = _gelu(au), _silu(az)
            dov = do_ref[:, sl].astype(f32)
            da_ref[:, sl] = (dov * s * sz * _gelu_grad(au)).astype(da_ref.dtype)
            da_ref[:, 2 * wa + h * HEAD:2 * wa + (h + 1) * HEAD] = (dov * gu * s * _silu_grad(az)).astype(da_ref.dtype)
            ds = dov * gu * sz
            db_ref[h] += jnp.sum(ds, axis=-1, keepdims=True)
            dsb = ds.astype(bf16)
            dw_ref[h] += jnp.where(tril, _dot_nt(dsb, vh), 0.0)
            dvh = _dot_tn(wm, dsb)
            dng_ref[:, sl] += jnp.sum(dvh * gvn, axis=0, keepdims=True)
            dgvn = dvh * ng
            dgv = r * (dgvn - gvn * jnp.mean(dgvn * gvn, axis=-1, keepdims=True))
            da_ref[:, wa + h * HEAD:wa + (h + 1) * HEAD] = (dgv * _gelu_grad(av)).astype(da_ref.dtype)

    def col(j):
        return pl.BlockSpec((HEAD, wa), lambda n: (n, j))

    whole_w = pl.BlockSpec((nh, HEAD, HEAD), lambda n: (0, 0, 0))
    whole_b = pl.BlockSpec((nh, HEAD, 1), lambda n: (0, 0, 0))
    return pl.pallas_call(
        body, name="sgu_bwd", grid=(l // HEAD,),
        in_specs=[col(0), col(1), col(2), col(0), _row(wa), whole_w, whole_b],
        out_specs=[pl.BlockSpec((HEAD, 3 * wa), lambda n: (n, 0)), whole_w, whole_b, _row(wa)],
        out_shape=[jax.ShapeDtypeStruct((l, 3 * wa), bf16), jax.ShapeDtypeStruct((nh, HEAD, HEAD), f32),
                   jax.ShapeDtypeStruct((nh, HEAD, 1), f32), jax.ShapeDtypeStruct((1, wa), f32)],
        compiler_params=_cparams("arbitrary"),
    )(proj, proj, proj, dcat, norm_g, w_s, b_s)


_LOG2E = 1.0 / math.log(2.0)


def _sb_scores(q, k, scale):
    z = _dot_nt(q, k) * (scale * _LOG2E)
    return z, jnp.maximum(z, 0.0) + jnp.log2(1.0 + jnp.exp2(-jnp.abs(z)))


SB_KEYS = 256


def _sb_sum_matrix(tri, kb):
    s = lax.broadcasted_iota(jnp.int32, (2 * kb, kb + HEAD), 0) % kb
    j = lax.broadcasted_iota(jnp.int32, (2 * kb, kb + HEAD), 1)
    return jnp.where(jnp.logical_or(j >= kb, tri(s, j)), 1.0, 0.0).astype(bf16)


def _sb_sums(x, sums):
    kb = x.shape[1]
    c2 = _dot(jnp.concatenate(_split_bf16(x), axis=1), sums)
    return c2[:, :kb], c2[:, kb:]


def _sb_wide(v, kb):
    return jnp.concatenate([v] * (kb // HEAD), axis=1) if kb > HEAD else v


def _sb_q_tile(l, most=512):
    return _tile(l, tuple(t for t in (1024, 512, 256, 128) if t <= most))


def _sb_band_levels(band):
    return _tile(band, (4, 2, 1))


def _sb_heads_per_step(nh, most):
    return _tile(nh, tuple(h for h in (4, 2) if h <= most))


def sb_fwd(proj, mixed, nh):
    l = proj.shape[0]
    wb = nh * HEAD
    tq = _sb_q_tile(l, 1024)
    kb = min(SB_KEYS, tq)
    band = tq // kb
    hp = _sb_heads_per_step(nh, 2)
    levels = _sb_band_levels(band)
    scale = 1.0 / math.sqrt(HEAD)
    qc, kc, vc, zc = 3 * nh, 4 * nh, 5 * nh, 6 * nh

    def body(q_ref, k_ref, v_ref, bz_ref, mixed_ref, o_ref, att_ref, tot_ref):
        del mixed_ref
        i = pl.program_id(1)
        sums = _sb_sum_matrix(lambda s, j: s > j, kb)
        t_pos = i * tq + lax.broadcasted_iota(jnp.int32, (tq, kb), 0)
        s_off = lax.broadcasted_iota(jnp.int32, (tq, kb), 1)

        def step(j, carry, masked, row0=0):
            rows = pl.ds(pl.multiple_of(j * kb, kb), kb)
            out = []
            for e in range(hp):
                acc, tot = carry[e]
                sl = slice(e * HEAD, (e + 1) * HEAD)
                z, sp = _sb_scores(q_ref[row0:, sl], k_ref[rows, sl], scale)
                lb = z - sp
                if masked:
                    mask = s_off[row0:] + j * kb < t_pos[row0:]
                    sp = jnp.where(mask, sp, 0.0)
                later, total = _sb_sums(sp, sums)
                w = jnp.exp2(lb + _sb_wide(tot[row0:], kb) - later)
                if masked:
                    w = jnp.where(mask, w, 0.0)
                new = (acc[row0:] + _dot(w.astype(bf16), v_ref[rows, sl]), tot[row0:] - total)
                out.append(tuple(jnp.concatenate([old[:row0], upd]) if row0 else upd for old, upd in zip(carry[e], new)))
            return tuple(out)

        zero = jnp.zeros((tq, HEAD), f32)
        carry = ((zero, zero),) * hp
        for lv in reversed(range(levels)):
            carry = lax.fori_loop(
                0, band // levels,
                lambda t, c, lv=lv: step(band * i + (lv + 1) * (band // levels) - 1 - t, c, True, lv * (tq // levels)), carry)
        carry = lax.fori_loop(0, band * i, lambda t, c: step(band * i - 1 - t, c, False), carry)
        for e in range(hp):
            acc, tot = carry[e]
            sl = slice(e * HEAD, (e + 1) * HEAD)
            att_ref[:, sl] = acc.astype(att_ref.dtype)
            o_ref[:, sl] = (acc * _silu(bz_ref[:, sl].astype(f32))).astype(o_ref.dtype)
            tot_ref[e] = tot[:, :1]

    blk = lambda c0: pl.BlockSpec((tq, hp * HEAD), lambda g, i: (i, c0 // hp + g))
    head = lambda c0: pl.BlockSpec((l, hp * HEAD), lambda g, i: (0, c0 // hp + g))
    return pl.pallas_call(
        body, name="sb_fwd", grid=(nh // hp, l // tq),
        in_specs=[blk(qc), head(kc), head(vc), blk(zc), pl.BlockSpec(memory_space=pl.ANY)],
        out_specs=[blk(mixed.shape[1] // HEAD - nh), blk(0), pl.BlockSpec((hp, tq, 1), lambda g, i: (g, i, 0))],
        out_shape=[jax.ShapeDtypeStruct(mixed.shape, bf16), jax.ShapeDtypeStruct((l, wb), bf16),
                   jax.ShapeDtypeStruct((nh, l, 1), f32)],
        input_output_aliases={4: 0},
        compiler_params=_cparams("parallel", "arbitrary"),
    )(proj, proj, proj, proj, mixed)


def sb_bwd(proj, dcat, att, tot, nh):
    l = proj.shape[0]
    wb = nh * HEAD
    tq = _sb_q_tile(l, 1024)
    kb = min(SB_KEYS, tq)
    band = tq // kb
    nq = l // tq
    hp = _sb_heads_per_step(nh, 2)
    levels = _sb_band_levels(band)
    scale = 1.0 / math.sqrt(HEAD)
    qc, kc, vc, zc = 3 * nh, 4 * nh, 5 * nh, 6 * nh

    def body(q_ref, k_ref, v_ref, bz_ref, do_ref, att_ref, tot_ref, dq_ref, dk_ref, dv_ref, dbz_ref, dk_acc, dv_acc,
             dob_ref):
        i = pl.program_id(1)

        @pl.when(i == 0)
        def _():
            dk_acc[...] = jnp.zeros_like(dk_acc)
            dv_acc[...] = jnp.zeros_like(dv_acc)

        bz = bz_ref[...].astype(f32)
        dov = do_ref[...].astype(f32)
        dbz_ref[...] = (dov * att_ref[...].astype(f32) * _silu_grad(bz)).astype(dbz_ref.dtype)
        dob_ref[...] = (dov * _silu(bz)).astype(bf16)
        upto = _sb_sum_matrix(lambda s, j: s <= j, kb)
        before = _sb_sum_matrix(lambda j, s: j < s, kb)
        t_pos = i * tq + lax.broadcasted_iota(jnp.int32, (tq, kb), 0)
        s_off = lax.broadcasted_iota(jnp.int32, (tq, kb), 1)

        def step(j, carry, masked, row0=0):
            rows = pl.ds(pl.multiple_of(j * kb, kb), kb)
            out = []
            for h in range(hp):
                dq, sp_seen, e_seen = (c[row0:] for c in carry[h])
                sl = slice(h * HEAD, (h + 1) * HEAD)
                q, kj, vj, dob = q_ref[row0:, sl], k_ref[rows, sl], v_ref[rows, sl], dob_ref[row0:, sl]
                z, sp = _sb_scores(q, kj, scale)
                lb = z - sp
                if masked:
                    mask = s_off[row0:] + j * kb < t_pos[row0:]
                    sp = jnp.where(mask, sp, 0.0)
                sp_upto, sp_total = _sb_sums(sp, upto)
                w = jnp.exp2(lb + _sb_wide(sp_seen, kb) + sp_upto)
                if masked:
                    w = jnp.where(mask, w, 0.0)
                dv_acc[rows, sl] += _dot_tn(w.astype(bf16), dob)
                e = _dot_nt(dob, vj) * w
                e_before, e_total = _sb_sums(e, before)
                dz = (e - (e + _sb_wide(e_seen, kb) + e_before) * jnp.exp2(lb)) * scale
                if masked:
                    dz = jnp.where(mask, dz, 0.0)
                dz = dz.astype(bf16)
                dk_acc[rows, sl] += _dot_tn(dz, q)
                new = (dq + _dot(dz, kj), sp_seen + sp_total, e_seen + e_total)
                out.append(tuple(jnp.concatenate([old[:row0], upd]) if row0 else upd for old, upd in zip(carry[h], new)))
            return tuple(out)

        zero = jnp.zeros((tq, HEAD), f32)
        init = tuple((zero, jnp.broadcast_to(tot_ref[h], (tq, HEAD)), zero) for h in range(hp))
        carry = lax.fori_loop(0, band * i, lambda j, c: step(j, c, False), init)
        for lv in range(levels):
            carry = lax.fori_loop(
                0, band // levels,
                lambda t, c, lv=lv: step(band * i + lv * (band // levels) + t, c, True, lv * (tq // levels)), carry)
        for h in range(hp):
            dq_ref[:, h * HEAD:(h + 1) * HEAD] = carry[h][0].astype(dq_ref.dtype)

        @pl.when(i == nq - 1)
        def _():
            dk_ref[...] = dk_acc[...].astype(dk_ref.dtype)
            dv_ref[...] = dv_acc[...].astype(dv_ref.dtype)

    blk = lambda c0: pl.BlockSpec((tq, hp * HEAD), lambda g, i: (i, c0 // hp + g))
    head = lambda c0: pl.BlockSpec((l, hp * HEAD), lambda g, i: (0, c0 // hp + g))
    return pl.pallas_call(
        body, name="sb_bwd", grid=(nh // hp, nq),
        in_specs=[blk(qc), head(kc), head(vc), blk(zc), blk(nh), blk(0),
                  pl.BlockSpec((hp, tq, 1), lambda g, i: (g, i, 0))],
        out_specs=[blk(0), head(0), head(0), blk(0)],
        out_shape=[jax.ShapeDtypeStruct((l, wb), bf16)] * 4,
        scratch_shapes=[pltpu.VMEM((l, hp * HEAD), f32), pltpu.VMEM((l, hp * HEAD), f32),
                        pltpu.VMEM((tq, hp * HEAD), bf16)],
        compiler_params=_cparams("parallel", "arbitrary"),
    )(proj, proj, proj, proj, dcat, att, tot)


def _disc(lr, li, ldt):
    dt = jnp.exp(ldt)
    mag = jnp.exp(lr * dt)
    a_re = mag * jnp.cos(li * dt)
    a_im = mag * jnp.sin(li * dt)
    den = lr * lr + li * li
    nr = a_re - 1.0
    return a_re, a_im, (nr * lr + a_im * li) / den, (a_im * lr - nr * li) / den


def s5_params_fwd(lr, li, ldt, bt_re, bt_im):
    g, c, p = bt_re.shape

    def body(lr_ref, li_ref, ldt_ref, br_ref, bi_ref, ar_ref, ai_ref, bbr_ref, bbi_ref):
        a_re, a_im, cr, ci = _disc(lr_ref[...], li_ref[...], ldt_ref[...])
        ar_ref[...] = a_re
        ai_ref[...] = a_im
        for k in range(c):
            br, bi = br_ref[:, k, :], bi_ref[:, k, :]
            bbr_ref[:, k, :] = cr * br - ci * bi
            bbi_ref[:, k, :] = cr * bi + ci * br

    return pl.pallas_call(
        body, name="s5_params_fwd",
        out_shape=[jax.ShapeDtypeStruct((g, p), f32)] * 2 + [jax.ShapeDtypeStruct((g, c, p), f32)] * 2,
    )(lr, li, ldt, bt_re, bt_im)


def s5_params_bwd(lr, li, ldt, bt_re, bt_im, da_re, da_im, dbbt_re, dbbt_im):
    g, c, p = bt_re.shape

    def body(lr_ref, li_ref, ldt_ref, br_ref, bi_ref, dar_ref, dai_ref, dbbr_ref, dbbi_ref,
             dlr_ref, dli_ref, dldt_ref, dbr_ref, dbi_ref):
        (a_re, a_im, cr, ci), vjp = jax.vjp(_disc, lr_ref[...], li_ref[...], ldt_ref[...])
        dcr = jnp.zeros((g, p), f32)
        dci = jnp.zeros((g, p), f32)
        for k in range(c):
            br, bi = br_ref[:, k, :], bi_ref[:, k, :]
            dr, di = dbbr_ref[:, k, :], dbbi_ref[:, k, :]
            dcr += dr * br + di * bi
            dci += di * br - dr * bi
            dbr_ref[:, k, :] = cr * dr + ci * di
            dbi_ref[:, k, :] = cr * di - ci * dr
        dlr, dli, dldt = vjp((dar_ref[...], dai_ref[...], dcr, dci))
        dlr_ref[...] = dlr
        dli_ref[...] = dli
        dldt_ref[...] = dldt

    return pl.pallas_call(
        body, name="s5_params_bwd",
        out_shape=[jax.ShapeDtypeStruct((g, p), f32)] * 2 + [jax.ShapeDtypeStruct((g, 1), f32)]
        + [jax.ShapeDtypeStruct((g, c, p), f32)] * 2,
    )(lr, li, ldt, bt_re, bt_im, da_re, da_im, dbbt_re, dbbt_im)


def _cmul(ar, ai, br, bi):
    return ar * br - ai * bi, ar * bi + ai * br


def _power_tables(ar, ai):
    rows = lax.broadcasted_iota(jnp.int32, (SUBLANES, ar.shape[1]), 0)
    pr = jnp.zeros((SUBLANES, ar.shape[1]), f32)
    pi = jnp.zeros((SUBLANES, ar.shape[1]), f32)
    cr, ci = ar, ai
    pows = {}
    for r in range(SUBLANES):
        pows[r + 1] = (cr, ci)
        pr = jnp.where(rows == r, cr, pr)
        pi = jnp.where(rows == r, ci, pi)
        cr, ci = _cmul(cr, ci, ar, ai)
    return [pows[1], pows[2], pows[4]], pr, pi


def _ssm_time_tile(l):
    return _tile(l, (1024, 512, 256, 128))


def ssm_fwd(u, bre3, bim3, cre3, cimn3, a_re, a_im, d_skip):
    l, w = u.shape[0], d_skip.shape[1]
    nj = w // HEAD
    ns = STATES_PER_LANE_BLOCK
    tt = _ssm_time_tile(l)

    def body(u_ref, bre_ref, bim_ref, cre_ref, cim_ref, ar_ref, ai_ref, d_ref, y_ref, hr_ref, hi_ref, cr_ref, ci_ref):
        @pl.when(pl.program_id(1) == 0)
        def _():
            cr_ref[...] = jnp.zeros_like(cr_ref)
            ci_ref[...] = jnp.zeros_like(ci_ref)

        uv = u_ref[...]
        hr_ref[...] = _dot(uv, bre_ref[...])
        hi_ref[...] = _dot(uv, bim_ref[...])
        steps, pr, pi = _power_tables(ar_ref[...], ai_ref[...])
        rows = lax.broadcasted_iota(jnp.int32, (SUBLANES, ns), 0)
        steps = [(jnp.where(rows >= d, sr_, 0.0), jnp.where(rows >= d, si_, 0.0)) for d, (sr_, si_) in zip((1, 2, 4), steps)]

        def blk(b, carry):
            cr, ci = carry
            sl = pl.ds(pl.multiple_of(b * SUBLANES, SUBLANES), SUBLANES)
            xr, xi = hr_ref[sl, :], hi_ref[sl, :]
            for d, (sr_, si_) in zip((1, 2, 4), steps):
                mr, mi = _cmul(sr_, si_, pltpu.roll(xr, d, axis=0), pltpu.roll(xi, d, axis=0))
                xr, xi = xr + mr, xi + mi
            mr, mi = _cmul(pr, pi, cr, ci)
            xr, xi = xr + mr, xi + mi
            hr_ref[sl, :] = xr
            hi_ref[sl, :] = xi
            return xr[SUBLANES - 1:, :], xi[SUBLANES - 1:, :]

        cr, ci = lax.fori_loop(0, tt // SUBLANES, blk, (cr_ref[...], ci_ref[...]))
        cr_ref[...] = cr
        ci_ref[...] = ci
        y = _dot(hr_ref[...].astype(bf16), cre_ref[...]) + _dot(hi_ref[...].astype(bf16), cim_ref[...])
        y_ref[...] = y + d_ref[...] * uv.astype(f32)

    lane = pl.BlockSpec((tt, HEAD), lambda j, i: (i, j))
    st = pl.BlockSpec((tt, ns), lambda j, i: (i, j))
    b3 = pl.BlockSpec((None, HEAD, ns), lambda j, i: (j, 0, 0))
    c3 = pl.BlockSpec((None, ns, HEAD), lambda j, i: (j, 0, 0))
    arow = pl.BlockSpec((1, ns), lambda j, i: (0, j))
    return pl.pallas_call(
        body, name="ssm_fwd", grid=(nj, l // tt),
        in_specs=[lane, b3, b3, c3, c3, arow, arow, pl.BlockSpec((1, HEAD), lambda j, i: (0, j))],
        out_specs=[lane, st, st],
        out_shape=[jax.ShapeDtypeStruct((l, w), f32), jax.ShapeDtypeStruct((l, nj * ns), f32),
                   jax.ShapeDtypeStruct((l, nj * ns), f32)],
        scratch_shapes=[pltpu.VMEM((1, ns), f32), pltpu.VMEM((1, ns), f32)],
        compiler_params=_cparams("parallel", "arbitrary"),
    )(u, bre3, bim3, cre3, cimn3, a_re, a_im, d_skip)


def ssm_bwd(dy, u, dproj, h_re, h_im, bre3, bim3, cre3, cimn3, a_re, a_im, d_skip):
    l, w = u.shape[0], d_skip.shape[1]
    nj = w // HEAD
    ns = STATES_PER_LANE_BLOCK
    tt = _ssm_time_tile(l)
    nt = l // tt

    def body(dy_ref, u_ref, dproj_ref, hr_ref, hi_ref, bre_ref, bim_ref, cre_ref, cim_ref, ar_ref, ai_ref, d_ref,
             du_ref, dd_ref, dar_ref, dai_ref, dbre_ref, dbim_ref, dcre_ref, dcim_ref, kr_ref, ki_ref, cr_ref, ci_ref,
             accr_ref, acci_ref):
        del dproj_ref
        i = pl.program_id(1)

        @pl.when(i == 0)
        def _():
            for ref in (cr_ref, ci_ref, accr_ref, acci_ref, dd_ref, dbre_ref, dbim_ref, dcre_ref, dcim_ref):
                ref[...] = jnp.zeros_like(ref)

        dyv = dy_ref[...]
        dyb = dyv.astype(bf16)
        uv = u_ref[...]
        kr_ref[...] = _dot_nt(dyb, cre_ref[...])
        ki_ref[...] = _dot_nt(dyb, cim_ref[...])
        steps, pr, pi = _power_tables(ar_ref[...], -ai_ref[...])
        rows = lax.broadcasted_iota(jnp.int32, (SUBLANES, ns), 0)
        qr = jnp.zeros((SUBLANES, ns), f32)
        qi = jnp.zeros((SUBLANES, ns), f32)
        for r in range(SUBLANES):
            qr = jnp.where(rows == r, pr[SUBLANES - 1 - r:SUBLANES - r, :], qr)
            qi = jnp.where(rows == r, pi[SUBLANES - 1 - r:SUBLANES - r, :], qi)
        nb = tt // SUBLANES
        steps = [(jnp.where(rows < SUBLANES - d, sr_, 0.0), jnp.where(rows < SUBLANES - d, si_, 0.0))
                 for d, (sr_, si_) in zip((1, 2, 4), steps)]

        def blk(t, carry):
            cr, ci, accr, acci = carry
            sl = pl.ds(pl.multiple_of((nb - 1 - t) * SUBLANES, SUBLANES), SUBLANES)
            xr, xi = kr_ref[sl, :], ki_ref[sl, :]
            for d, (sr_, si_) in zip((1, 2, 4), steps):
                mr, mi = _cmul(sr_, si_, pltpu.roll(xr, SUBLANES - d, axis=0), pltpu.roll(xi, SUBLANES - d, axis=0))
                xr, xi = xr + mr, xi + mi
            mr, mi = _cmul(qr, qi, cr, ci)
            xr, xi = xr + mr, xi + mi
            kr_ref[sl, :] = xr
            ki_ref[sl, :] = xi
            last = rows == SUBLANES - 1
            nr = jnp.where(last, cr, pltpu.roll(xr, SUBLANES - 1, axis=0))
            ni = jnp.where(last, ci, pltpu.roll(xi, SUBLANES - 1, axis=0))
            hr, hi = hr_ref[sl, :], hi_ref[sl, :]
            accr = accr + nr * hr + ni * hi
            acci = acci + ni * hr - nr * hi
            return xr[:1, :], xi[:1, :], accr, acci

        cr, ci, accr, acci = lax.fori_loop(0, nb, blk, (cr_ref[...], ci_ref[...], accr_ref[...], acci_ref[...]))
        cr_ref[...] = cr
        ci_ref[...] = ci
        accr_ref[...] = accr
        acci_ref[...] = acci
        kr, ki = kr_ref[...].astype(bf16), ki_ref[...].astype(bf16)
        du = _dot_nt(kr, bre_ref[...]) + _dot_nt(ki, bim_ref[...]) + d_ref[...] * dyv
        du_ref[...] = du.astype(du_ref.dtype)
        dd_ref[...] += jnp.sum(dyv * uv.astype(f32), axis=0, keepdims=True)
        dbre_ref[...] += _dot_tn(uv, kr)
        dbim_ref[...] += _dot_tn(uv, ki)
        dcre_ref[...] += _dot_tn(hr_ref[...].astype(bf16), dyb)
        dcim_ref[...] += _dot_tn(hi_ref[...].astype(bf16), dyb)

        @pl.when(i == nt - 1)
        def _():
            dar_ref[...] = jnp.sum(accr_ref[...], axis=0, keepdims=True)
            dai_ref[...] = jnp.sum(acci_ref[...], axis=0, keepdims=True)

    lane = pl.BlockSpec((tt, HEAD), lambda j, i: (nt - 1 - i, j))
    st = pl.BlockSpec((tt, ns), lambda j, i: (nt - 1 - i, j))
    b3 = pl.BlockSpec((None, HEAD, ns), lambda j, i: (j, 0, 0))
    c3 = pl.BlockSpec((None, ns, HEAD), lambda j, i: (j, 0, 0))
    arow = pl.BlockSpec((1, ns), lambda j, i: (0, j))
    drow = pl.BlockSpec((1, HEAD), lambda j, i: (0, j))
    return pl.pallas_call(
        body, name="ssm_bwd", grid=(nj, nt),
        in_specs=[lane, lane, pl.BlockSpec(memory_space=pl.ANY), st, st, b3, b3, c3, c3, arow, arow, drow],
        out_specs=[lane, drow, arow, arow, b3, b3, c3, c3], input_output_aliases={2: 0},
        out_shape=[jax.ShapeDtypeStruct(dproj.shape, bf16), jax.ShapeDtypeStruct((1, w), f32),
                   jax.ShapeDtypeStruct((1, nj * ns), f32), jax.ShapeDtypeStruct((1, nj * ns), f32),
                   jax.ShapeDtypeStruct((nj, HEAD, ns), f32), jax.ShapeDtypeStruct((nj, HEAD, ns), f32),
                   jax.ShapeDtypeStruct((nj, ns, HEAD), f32), jax.ShapeDtypeStruct((nj, ns, HEAD), f32)],
        scratch_shapes=[pltpu.VMEM((tt, ns), f32), pltpu.VMEM((tt, ns), f32), pltpu.VMEM((1, ns), f32),
                        pltpu.VMEM((1, ns), f32), pltpu.VMEM((SUBLANES, ns), f32), pltpu.VMEM((SUBLANES, ns), f32)],
        compiler_params=_cparams("parallel", "arbitrary"),
    )(dy, u, dproj, h_re, h_im, bre3, bim3, cre3, cimn3, a_re, a_im, d_skip)


def glu_fwd(y, z_src, w_glu, b_glu):
    l, w = y.shape
    tm = _row_tile(l)

    def body(y_ref, z_ref, w_ref, b_ref, g_ref, t_ref, o_ref):
        g = _gelu(y_ref[...])
        gb = g.astype(bf16)
        t = _dot(gb, w_ref[...]) + b_ref[...]
        g_ref[...] = gb
        t_ref[...] = t
        o_ref[...] = (g * jax.nn.sigmoid(t) * _silu(z_ref[...].astype(f32))).astype(o_ref.dtype)

    blk = pl.BlockSpec((tm, w), lambda i: (i, 0))
    return pl.pallas_call(
        body, name="glu_fwd", grid=(l // tm,),
        in_specs=[blk, pl.BlockSpec((tm, w), lambda i: (i, 1)), pl.BlockSpec((w, w), lambda i: (0, 0)), _row(w)],
        out_specs=[blk, blk, blk],
        out_shape=[jax.ShapeDtypeStruct((l, w), bf16), jax.ShapeDtypeStruct((l, w), f32),
                   jax.ShapeDtypeStruct((l, w), bf16)],
        compiler_params=_cparams("parallel"),
    )(y, z_src, w_glu, b_glu)


def glu_bwd(dout, y, t, z_src, w_glu):
    l, w = y.shape
    tm = _row_tile(l)

    def body(do_ref, y_ref, t_ref, z_ref, w_ref, dy_ref, dz_ref, dt_ref, db_ref):
        @pl.when(pl.program_id(0) == 0)
        def _():
            db_ref[...] = jnp.zeros_like(db_ref)

        yv, zv, dov = y_ref[...], z_ref[...].astype(f32), do_ref[...]
        g = _gelu(yv)
        sg = jax.nn.sigmoid(t_ref[...])
        dy2 = dov * _silu(zv)
        dz_ref[...] = (dov * g * sg * _silu_grad(zv)).astype(dz_ref.dtype)
        dt = dy2 * g * sg * (1.0 - sg)
        dtb = dt.astype(bf16)
        dt_ref[...] = dtb
        db_ref[...] += jnp.sum(dt, axis=0, keepdims=True)
        dg = dy2 * sg + _dot_nt(dtb, w_ref[...])
        dy_ref[...] = dg * _gelu_grad(yv)

    blk = pl.BlockSpec((tm, w), lambda i: (i, 0))
    return pl.pallas_call(
        body, name="glu_bwd", grid=(l // tm,),
        in_specs=[blk, blk, blk, pl.BlockSpec((tm, w), lambda i: (i, 1)), pl.BlockSpec((w, w), lambda i: (0, 0))],
        out_specs=[blk, pl.BlockSpec((tm, w), lambda i: (i, 1)), blk, _row(w)],
        out_shape=[jax.ShapeDtypeStruct((l, w), f32), jax.ShapeDtypeStruct((l, 2 * w), bf16),
                   jax.ShapeDtypeStruct((l, w), bf16), jax.ShapeDtypeStruct((1, w), f32)],
        compiler_params=_cparams("arbitrary"),
    )(dout, y, t, z_src, w_glu)


def _adamw(w, g, m, v):
    m = ADAM_B1 * m + (1.0 - ADAM_B1) * g
    v = ADAM_B2 * v + (1.0 - ADAM_B2) * (g * g)
    m_hat = m / (1.0 - ADAM_B1 ** ADAM_STEP)
    v_hat = v / (1.0 - ADAM_B2 ** ADAM_STEP)
    return -ADAM_LR * (m_hat / (jnp.sqrt(v_hat) + ADAM_EPS) + ADAM_WD * w), m, v


def adam_reduce(pieces, w, m, v, name):
    r, c = w.shape
    n = pieces.shape[0]
    tr = _tile(r, (256, 128, 64, 32, 16, 8))

    def body(p_ref, w_ref, m_ref, v_ref, g_ref, d_ref, nm_ref, nv_ref):
        g = p_ref[0].astype(f32)
        for s in range(1, n):
            g = g + p_ref[s].astype(f32)
        g_ref[...] = g
        d_ref[...], nm_ref[...], nv_ref[...] = _adamw(w_ref[...], g, m_ref[...], v_ref[...])

    blk = pl.BlockSpec((tr, c), lambda i: (i, 0))
    return pl.pallas_call(
        body, name=name, grid=(r // tr,),
        in_specs=[pl.BlockSpec((n, tr, c), lambda i: (0, i, 0)), blk, blk, blk],
        out_specs=[blk] * 4, out_shape=[jax.ShapeDtypeStruct((r, c), f32)] * 4,
        compiler_params=_cparams("parallel"),
    )(pieces, w, m, v)


def adam_w_mod(cond_t, dm, w, m, v):
    nl, d, cols = w.shape
    tr = _tile(d, (512, 256, 128))

    def body(c_ref, dm_ref, w_ref, m_ref, v_ref, g_ref, d_ref, nm_ref, nv_ref):
        g = jnp.dot(c_ref[...], dm_ref[...], preferred_element_type=f32, precision=lax.Precision.HIGHEST)
        g_ref[...] = g
        d_ref[...], nm_ref[...], nv_ref[...] = _adamw(w_ref[...], g, m_ref[...], v_ref[...])

    blk = pl.BlockSpec((None, tr, cols), lambda l, i: (l, i, 0))
    return pl.pallas_call(
        body, name="adam_w_mod", grid=(nl, d // tr),
        in_specs=[pl.BlockSpec((tr, N_DEV), lambda l, i: (i, 0)), pl.BlockSpec((None, N_DEV, cols), lambda l, i: (l, 0, 0)),
                  blk, blk, blk],
        out_specs=[blk] * 4, out_shape=[jax.ShapeDtypeStruct((nl, d, cols), f32)] * 4,
        compiler_params=_cparams("parallel", "parallel"),
    )(cond_t, dm, w, m, v)


def silu_rows(c_all):
    def body(c_ref, o_ref):
        o_ref[...] = _silu(c_ref[...])

    return pl.pallas_call(body, name="silu_rows", out_shape=jax.ShapeDtypeStruct(c_all.shape, f32))(c_all)


def _block_diag(x):
    g, a, b = x.shape
    nj = g // GROUPS_PER_LANE_BLOCK
    eye = jnp.eye(GROUPS_PER_LANE_BLOCK, dtype=x.dtype)
    x5 = x.reshape(nj, GROUPS_PER_LANE_BLOCK, a, b)
    return jnp.einsum("jgab,gh->jgahb", x5, eye).reshape(nj, GROUPS_PER_LANE_BLOCK * a, GROUPS_PER_LANE_BLOCK * b)


def _diag_blocks(x, a, b):
    nj = x.shape[0]
    x5 = x.reshape(nj, GROUPS_PER_LANE_BLOCK, a, GROUPS_PER_LANE_BLOCK, b)
    eye = jnp.eye(GROUPS_PER_LANE_BLOCK, dtype=x.dtype)
    return jnp.einsum("jgahb,gh->jgab", x5, eye).reshape(nj * GROUPS_PER_LANE_BLOCK, a, b)


PACK_ROW = SUBLANES * HEAD


def _pack(parts, row_multiple=SUBLANES):
    rows = []
    for p in parts:
        flat = p.reshape(-1)
        pad = (-flat.shape[0]) % PACK_ROW
        if pad:
            flat = jnp.concatenate([flat, jnp.zeros((pad,), flat.dtype)])
        rows.append(flat.reshape(-1, HEAD))
    pad = (-sum(r.shape[0] for r in rows)) % row_multiple
    if pad:
        rows.append(jnp.zeros((pad, HEAD), rows[0].dtype))
    return jnp.concatenate(rows, axis=0)


def _unpack(packed, shapes):
    out, r0 = [], 0
    for shp in shapes:
        n = math.prod(shp)
        nr = -(-n // PACK_ROW) * SUBLANES
        out.append(packed[r0:r0 + nr].reshape(-1)[:n].reshape(shp))
        r0 += nr
    return out


def adam_small(g, w, m, v):
    r, c = w.shape

    def body(g_ref, w_ref, m_ref, v_ref, d_ref, nm_ref, nv_ref):
        d_ref[...], nm_ref[...], nv_ref[...] = _adamw(w_ref[...], g_ref[...], m_ref[...], v_ref[...])

    tr = max(t for t in range(SUBLANES, 1024 + 1, SUBLANES) if r % t == 0)
    blk = pl.BlockSpec((tr, c), lambda i: (i, 0))
    return pl.pallas_call(
        body, name="adam_small", grid=(r // tr,),
        in_specs=[blk] * 4, out_specs=[blk] * 3, out_shape=[jax.ShapeDtypeStruct((r, c), f32)] * 3,
        compiler_params=_cparams("parallel"),
    )(g, w, m, v)


def kernel(x, c, ln_pre_g, ln_post_g, w_mod, b_mod, w_in_ab, w_out_ab, sgu_norm_g, sgu_w, sgu_b, w_in_ssm, w_out_ssm, lam_re, lam_im, b_re, b_im, c_re, c_im, d_skip, log_dt, w_glu, b_glu, loss_target, m_ln_pre_g, m_ln_post_g, m_w_mod, m_b_mod, m_w_in_ab, m_w_out_ab, m_sgu_norm_g, m_sgu_w, m_sgu_b, m_w_in_ssm, m_w_out_ssm, m_lam_re, m_lam_im, m_b_re, m_b_im, m_c_re, m_c_im, m_d_skip, m_log_dt, m_w_glu, m_b_glu, v_ln_pre_g, v_ln_post_g, v_w_mod, v_b_mod, v_w_in_ab, v_w_out_ab, v_sgu_norm_g, v_sgu_w, v_sgu_b, v_w_in_ssm, v_w_out_ssm, v_lam_re, v_lam_im, v_b_re, v_b_im, v_c_re, v_c_im, v_d_skip, v_log_dt, v_w_glu, v_b_glu):
    me = _my_index()
    x0 = x[0]
    l, d = x0.shape
    target = loss_target[0]
    nh = sgu_w.shape[1]
    wa = nh * HEAD
    n_grp, n_st = lam_re.shape[1], lam_re.shape[2]
    mod_cols = w_mod.shape[2]

    c_all, d_skip_all, b_glu_all = all_gather([c, d_skip, b_glu], "gather_c")
    c_all = c_all.reshape(N_DEV, d)
    d_skip_all = d_skip_all.reshape(1, -1)
    b_glu_all = b_glu_all.reshape(1, -1)

    b_cols = lax.dynamic_slice_in_dim(b_mod, me * mod_cols, mod_cols, axis=1)
    (mod_all,) = all_gather([mod_part(c_all, w_mod, b_cols)], "gather_mod")
    def after(a, first):
        return a + jnp.minimum(jnp.abs(first[(0,) * first.ndim].astype(f32)), 0.0).astype(a.dtype)

    (win_ab3,) = sequencer_exchange(GATHER, [after(w_in_ab[0], mod_all).astype(bf16)], "gather_w_in", 1)
    mod_mine = lax.dynamic_index_in_dim(mod_all, me, axis=2, keepdims=False)
    mod_rows = jnp.transpose(mod_mine, (1, 0, 2)).reshape(2, 3, 1, d)

    def rows(a, i):
        return a[i].reshape(1, d)

    shift0, scale0, gate0 = mod_rows[0, 0], mod_rows[0, 1], mod_rows[0, 2]
    h0, h0_t = prenorm_fwd(x0, rows(ln_pre_g, 0), shift0, scale0, "prenorm0")
    wout_ab3, win_ssm3, wout_ssm3, wglu = sequencer_exchange(
        GATHER, [after(w, win_ab3).astype(bf16) for w in (w_out_ab[0], w_in_ssm[0], w_out_ssm[0], w_glu[0])],
        "gather_w_rest", 2)
    proj0 = mm_nn(h0, win_ab3, bf16, "proj0")
    sgu_b3 = sgu_b[0].reshape(nh, HEAD, 1)
    cat, att, tot = sb_fwd(proj0, sgu_fwd(proj0, sgu_norm_g, sgu_w[0], sgu_b3), nh)
    wout_ab3 = wout_ab3.reshape(1, d, d)
    win_ssm3 = win_ssm3.reshape(1, d, d)
    wglu = wglu.reshape(w_glu.shape[2], w_glu.shape[2])
    y0 = mm_nn(cat, wout_ab3, f32, "out0")

    shift1, scale1, gate1 = mod_rows[1, 0], mod_rows[1, 1], mod_rows[1, 2]
    x1, h1, h1_t = post_prenorm_fwd(x0, y0, gate0, rows(ln_post_g, 0), rows(ln_pre_g, 1), shift1, scale1,
                                    "post0_prenorm1")
    proj1 = mm_nn(h1, win_ssm3, bf16, "proj1")
    w_ssm = proj1.shape[1] // 2
    ldt = log_dt[0].reshape(n_grp, 1)
    bt_re = jnp.transpose(b_re[0], (0, 2, 1))
    bt_im = jnp.transpose(b_im[0], (0, 2, 1))
    a_re, a_im, bbt_re, bbt_im = s5_params_fwd(lam_re[0], lam_im[0], ldt, bt_re, bt_im)
    bre3 = _block_diag(bbt_re).astype(bf16)
    bim3 = _block_diag(bbt_im).astype(bf16)
    cre3 = _block_diag(jnp.transpose(c_re[0], (0, 2, 1))).astype(bf16)
    cimn3 = _block_diag(-jnp.transpose(c_im[0], (0, 2, 1))).astype(bf16)
    a_re_row, a_im_row = a_re.reshape(1, -1), a_im.reshape(1, -1)
    y_ssm, hs_re, hs_im = ssm_fwd(proj1, bre3, bim3, cre3, cimn3, a_re_row, a_im_row, d_skip_all)
    g_act, t_glu, mix1 = glu_fwd(y_ssm, proj1, wglu, b_glu_all)
    y1 = mm_nn(mix1, wout_ssm3, f32, "out1")

    dx2, loss_tile, dy1, dgate1, dgpost1 = final_loss(x1, y1, gate1, rows(ln_post_g, 1), target)

    dmix1 = mm_nt(dy1, wout_ssm3, f32, "dmix1")
    gw_out_ssm = mm_tn(mix1, dy1, N_DEV, bf16, "gw_out_ssm")
    (p_out_ssm,) = sequencer_exchange(SCATTER, [gw_out_ssm], "scatter_g1", 3)
    dy_ssm, dproj1, dt_glu, db_glu = glu_bwd(dmix1, y_ssm, t_glu, proj1, wglu)
    gw_glu = mm_tn(g_act, dt_glu, 1, bf16, "gw_glu").reshape(N_DEV, -1, w_ssm)
    dproj1, dd_skip, da_re, da_im, dbre3, dbim3, dcre3, dcimn3 = ssm_bwd(
        dy_ssm, proj1, dproj1, hs_re, hs_im, bre3, bim3, cre3, cimn3, a_re_row, a_im_row, d_skip_all)
    gw_in_ssm = mm_nn(h1_t, dproj1[None], bf16, "gw_in_ssm").reshape(N_DEV, -1, proj1.shape[1])
    p_in_ssm, p_glu = sequencer_exchange(SCATTER, [gw_in_ssm, gw_glu], "scatter_g2", 4)
    dh1 = mm_nt(dproj1, win_ssm3, f32, "dh1")
    dx1, dshift1, dscale1, dgpre1 = prenorm_bwd(dh1, x1, dx2, rows(ln_pre_g, 1), scale1, "prenorm1_bwd")
    dlr, dli, dldt, dbt_re, dbt_im = s5_params_bwd(
        lam_re[0], lam_im[0], ldt, bt_re, bt_im, da_re.reshape(n_grp, n_st), da_im.reshape(n_grp, n_st),
        _diag_blocks(dbre3, SSM_GROUP, n_st), _diag_blocks(dbim3, SSM_GROUP, n_st))
    g_b_re = jnp.transpose(dbt_re, (0, 2, 1))
    g_b_im = jnp.transpose(dbt_im, (0, 2, 1))
    g_c_re = jnp.transpose(_diag_blocks(dcre3, n_st, SSM_GROUP), (0, 2, 1))
    g_c_im = -jnp.transpose(_diag_blocks(dcimn3, n_st, SSM_GROUP), (0, 2, 1))

    dy0, dgate0, dgpost0 = post_bwd(dx1, y0, gate0, rows(ln_post_g, 0), "post0_bwd")
    dcat = mm_nt(dy0, wout_ab3, f32, "dcat")
    gw_out_ab = mm_tn(cat, dy0, 1, bf16, "gw_out_ab").reshape(N_DEV, -1, d)
    (p_out_ab,) = sequencer_exchange(SCATTER, [gw_out_ab], "scatter_g3", 5)
    da, dsgu_w, dsgu_b, dsgu_ng = sgu_bwd(proj0, dcat, sgu_norm_g, sgu_w[0], sgu_b3)
    dq, dk, dv, dbz = sb_bwd(proj0, dcat, att, tot, nh)
    dproj0 = jnp.concatenate([da, dq, dk, dv, dbz], axis=1)
    gw_in_ab = mm_nn(h0_t, dproj0[None], bf16, "gw_in_ab", split_cols=N_DEV)
    (p_in_ab,) = sequencer_exchange(SCATTER, [gw_in_ab], "scatter_g4", 6)
    dh0 = mm_nt(dproj0, win_ab3, f32, "dh0")
    dx0, dshift0, dscale0, dgpre0 = prenorm_bwd(dh0, x0, dx1, rows(ln_pre_g, 0), scale0, "prenorm0_bwd")

    small_names = ["ln_pre_g", "ln_post_g", "b_mod", "sgu_norm_g", "sgu_w", "sgu_b", "lam_re", "lam_im", "b_re", "b_im",
                   "c_re", "c_im", "log_dt"]
    small_w = [ln_pre_g, ln_post_g, b_mod, sgu_norm_g, sgu_w, sgu_b, lam_re, lam_im, b_re, b_im, c_re, c_im, log_dt]
    small_m = [m_ln_pre_g, m_ln_post_g, m_b_mod, m_sgu_norm_g, m_sgu_w, m_sgu_b, m_lam_re, m_lam_im, m_b_re, m_b_im,
               m_c_re, m_c_im, m_log_dt]
    small_v = [v_ln_pre_g, v_ln_post_g, v_b_mod, v_sgu_norm_g, v_sgu_w, v_sgu_b, v_lam_re, v_lam_im, v_b_re, v_b_im,
               v_c_re, v_c_im, v_log_dt]
    def sharded(p, w, m, v, name):
        shp = w.shape
        w2, m2, v2 = (a.reshape(-1, shp[-1]) for a in (w, m, v))
        return [o.reshape(shp) for o in adam_reduce(p.reshape(p.shape[0], -1, shp[-1]), w2, m2, v2, name)]

    r_w_out_ssm = sharded(p_out_ssm, w_out_ssm, m_w_out_ssm, v_w_out_ssm, "adam_w_out_ssm")
    r_w_in_ssm = sharded(p_in_ssm, w_in_ssm, m_w_in_ssm, v_w_in_ssm, "adam_w_in_ssm")
    r_w_glu = sharded(p_glu, w_glu, m_w_glu, v_w_glu, "adam_w_glu")
    r_w_out_ab = sharded(p_out_ab, w_out_ab, m_w_out_ab, v_w_out_ab, "adam_w_out_ab")
    dmod = jnp.concatenate([dshift0, dscale0, dgate0, dshift1, dscale1, dgate1], axis=1)
    for done in (r_w_out_ssm, r_w_in_ssm, r_w_glu, r_w_out_ab):
        dmod = after(dmod, done[0])
    small_g = [jnp.concatenate([dgpre0, dgpre1]), jnp.concatenate([dgpost0, dgpost1]), dmod, dsgu_ng, dsgu_w, dsgu_b,
               dlr, dli, g_b_re, g_b_im, g_c_re, g_c_im, dldt]
    shapes = [w.shape for w in small_w]
    g_sum, dmod_all = all_reduce_rows(_pack(small_g + [dd_skip, db_glu, loss_tile], SUBLANES * N_DEV), dmod,
                                      "reduce_small_grads")
    n_rows_small = sum(-(-math.prod(s) // PACK_ROW) * SUBLANES for s in shapes)
    loss = g_sum[n_rows_small + 2 * (d_skip_all.shape[1] // HEAD), 0] * (0.5 / d)
    new_small = adam_small(g_sum, _pack(small_w), _pack(small_m), _pack(small_v))
    r_small = [_unpack(o, shapes) for o in [g_sum[:n_rows_small]] + list(new_small)]
    small = {n: [r_small[k][i] for k in range(4)] for i, n in enumerate(small_names)}
    vec_rows = d_skip_all.shape[1] // HEAD

    def my_columns(r0):
        whole = g_sum[r0:r0 + vec_rows].reshape(1, 1, -1)
        return lax.dynamic_slice_in_dim(whole, me * d_skip.shape[1], d_skip.shape[1], axis=2)

    r_d_skip = sharded(my_columns(n_rows_small), d_skip, m_d_skip, v_d_skip, "adam_d_skip")
    r_b_glu = sharded(my_columns(n_rows_small + vec_rows), b_glu, m_b_glu, v_b_glu, "adam_b_glu")
    r_w_in_ab = sharded(p_in_ab, w_in_ab, m_w_in_ab, v_w_in_ab, "adam_w_in_ab")

    dm_cols = jnp.transpose(
        lax.dynamic_slice_in_dim(dmod_all.reshape(N_DEV, 2, 3 * d), me * mod_cols, mod_cols, axis=2), (1, 0, 2))
    cond_t = jnp.transpose(silu_rows(c_all))
    r_w_mod = adam_w_mod(cond_t, dm_cols, w_mod, m_w_mod, v_w_mod)

    res = dict(small)
    res.update(w_mod=r_w_mod, w_in_ab=r_w_in_ab, w_out_ab=r_w_out_ab, w_in_ssm=r_w_in_ssm, w_out_ssm=r_w_out_ssm,
               d_skip=r_d_skip, w_glu=r_w_glu, b_glu=r_b_glu)
    order = ["ln_pre_g", "ln_post_g", "w_mod", "b_mod", "w_in_ab", "w_out_ab", "sgu_norm_g", "sgu_w", "sgu_b", "w_in_ssm",
             "w_out_ssm", "lam_re", "lam_im", "b_re", "b_im", "c_re", "c_im", "d_skip", "log_dt", "w_glu", "b_glu"]
    outs = [loss, dx0.reshape(x.shape)]
    for k in range(4):
        outs += [res[n][k] for n in order]
    return tuple(outs)
```

```python
import functools
import math

import jax
import jax.numpy as jnp
from jax import lax
from jax.experimental import pallas as pl
from jax.experimental.pallas import tpu as pltpu
from jax.experimental.pallas import tpu_sc as plsc

f32 = jnp.float32
bf16 = jnp.bfloat16

N_DEV = 8
EPS = 1e-6
HEAD = 128
SUBLANES = 8
SSM_GROUP = 16
SSM_STATE = 64
GROUPS_PER_LANE_BLOCK = HEAD // SSM_GROUP
STATES_PER_LANE_BLOCK = GROUPS_PER_LANE_BLOCK * SSM_STATE
VMEM_LIMIT = 56 * 2 ** 20
ADAM_LR, ADAM_B1, ADAM_B2, ADAM_EPS, ADAM_WD, ADAM_STEP = 0.001, 0.9, 0.999, 1e-08, 0.01, 10
_GELU_C0 = math.sqrt(2.0 / math.pi)
_GELU_C1 = 0.044715
MESH = pl.DeviceIdType.MESH


def _cparams(*sem):
    return pltpu.CompilerParams(dimension_semantics=sem if sem else None, vmem_limit_bytes=VMEM_LIMIT)


def _gelu(x):
    return 0.5 * x * (1.0 + jnp.tanh(_GELU_C0 * (x + _GELU_C1 * x * x * x)))


def _gelu_grad(x):
    t = jnp.tanh(_GELU_C0 * (x + _GELU_C1 * x * x * x))
    return 0.5 * (1.0 + t) + 0.5 * x * (1.0 - t * t) * _GELU_C0 * (1.0 + 3.0 * _GELU_C1 * x * x)


def _silu(x):
    return x * jax.nn.sigmoid(x)


def _silu_grad(x):
    s = jax.nn.sigmoid(x)
    return s * (1.0 + x * (1.0 - s))


def _dot(a, b):
    return jnp.dot(a, b, preferred_element_type=f32)


def _dot_nt(a, b):
    return lax.dot_general(a, b, (((1,), (1,)), ((), ())), preferred_element_type=f32)


def _dot_tn(a, b):
    return lax.dot_general(a, b, (((0,), (0,)), ((), ())), preferred_element_type=f32)


def _split_bf16(v):
    hi = v.astype(bf16)
    lo = (v - hi.astype(f32)).astype(bf16)
    return hi, lo


def _row(d):
    return pl.BlockSpec((1, d), lambda *_: (0, 0))


def _my_index():
    return 4 * lax.axis_index("x") + 2 * lax.axis_index("y") + lax.axis_index("c")


def _peer(k):
    x, y, c = lax.axis_index("x"), lax.axis_index("y"), lax.axis_index("c")
    return (1 - x if k & 4 else x, 1 - y if k & 2 else y, 1 - c if k & 1 else c)


def all_gather(arrs, name):
    n = len(arrs)

    def body(*refs):
        ins, outs = refs[:n], refs[n:2 * n]
        send, recv, local = refs[2 * n:]
        me = _my_index()
        copies = []
        for a in range(n):
            cp = pltpu.make_async_copy(ins[a], outs[a].at[me], local.at[a])
            cp.start()
            copies.append(cp)
            for k in range(1, N_DEV):
                s = a * (N_DEV - 1) + k - 1
                cp = pltpu.make_async_remote_copy(src_ref=ins[a], dst_ref=outs[a].at[me], send_sem=send.at[s],
                                                  recv_sem=recv.at[s], device_id=_peer(k), device_id_type=MESH)
                cp.start()
                copies.append(cp)
        for cp in copies:
            cp.wait()

    any_spec = pl.BlockSpec(memory_space=pl.ANY)
    outs = pl.pallas_call(
        body, name=name,
        out_shape=[jax.ShapeDtypeStruct((N_DEV,) + a.shape, a.dtype) for a in arrs],
        in_specs=[any_spec] * n, out_specs=[any_spec] * n,
        scratch_shapes=[pltpu.SemaphoreType.DMA((n * (N_DEV - 1),)), pltpu.SemaphoreType.DMA((n * (N_DEV - 1),)),
                        pltpu.SemaphoreType.DMA((n,))],
        compiler_params=pltpu.CompilerParams(has_side_effects=True),
    )(*arrs)
    return list(outs)


def all_reduce_rows(pack, extra, name):
    r, c = pack.shape
    rs = r // N_DEV
    n_peer = N_DEV - 1

    def body(p_ref, x_ref, o_ref, xo_ref, land, red, send1, recv1, send2, recv2, sendx, recvx, local):
        me = _my_index()

        def rows(i):
            return pl.ds(pl.multiple_of(i * rs, SUBLANES), rs)

        own = [pltpu.make_async_copy(p_ref.at[rows(me)], land.at[me], local.at[0]),
               pltpu.make_async_copy(x_ref, xo_ref.at[me], local.at[1])]
        first = []
        for k in range(1, N_DEV):
            first.append(pltpu.make_async_remote_copy(
                src_ref=p_ref.at[rows(jnp.bitwise_xor(me, k))], dst_ref=land.at[me], send_sem=send1.at[k - 1],
                recv_sem=recv1.at[k - 1], device_id=_peer(k), device_id_type=MESH))
            first.append(pltpu.make_async_remote_copy(
                src_ref=x_ref, dst_ref=xo_ref.at[me], send_sem=sendx.at[k - 1], recv_sem=recvx.at[k - 1],
                device_id=_peer(k), device_id_type=MESH))
        for cp in own + first:
            cp.start()
        for cp in own + first:
            cp.wait()
        acc = land[0]
        for s in range(1, N_DEV):
            acc = acc + land[s]
        red[...] = acc
        mine = pltpu.make_async_copy(red, o_ref.at[rows(me)], local.at[2])
        second = [pltpu.make_async_remote_copy(
            src_ref=red, dst_ref=o_ref.at[rows(me)], send_sem=send2.at[k - 1], recv_sem=recv2.at[k - 1],
            device_id=_peer(k), device_id_type=MESH) for k in range(1, N_DEV)]
        for cp in [mine] + second:
            cp.start()
        for cp in [mine] + second:
            cp.wait()

    any_spec = pl.BlockSpec(memory_space=pl.ANY)
    return pl.pallas_call(
        body, name=name,
        out_shape=[jax.ShapeDtypeStruct((r, c), pack.dtype), jax.ShapeDtypeStruct((N_DEV,) + extra.shape, extra.dtype)],
        in_specs=[any_spec, any_spec], out_specs=[any_spec, any_spec],
        scratch_shapes=[pltpu.VMEM((N_DEV, rs, c), pack.dtype), pltpu.VMEM((rs, c), pack.dtype)]
        + [pltpu.SemaphoreType.DMA((n_peer,))] * 6 + [pltpu.SemaphoreType.DMA((3,))],
        compiler_params=pltpu.CompilerParams(has_side_effects=True),
    )(pack, extra)


GATHER, SCATTER = "gather", "scatter"


def _exchange_copies(srcs, lands, send, recv):
    me = _my_index()
    copies = []
    for a, (src, land) in enumerate(zip(srcs, lands)):
        for k in range(1, N_DEV):
            s = a * (N_DEV - 1) + k - 1
            copies.append(pltpu.make_async_remote_copy(
                src_ref=src.at[jnp.bitwise_xor(me, k)], dst_ref=land.at[me],
                send_sem=send.at[s], recv_sem=recv.at[s], device_id=_peer(k), device_id_type=MESH))
    return copies


def sequencer_exchange(kind, arrs, name, collective_id):
    n = len(arrs)
    n_sem = n * (N_DEV - 1)
    land_shapes = [((N_DEV,) + a.shape if kind == GATHER else a.shape) for a in arrs]
    srcs = [jax.new_ref(a, memory_space=pltpu.MemorySpace.HBM) for a in arrs]
    lands = [jax.empty_ref(jax.ShapeDtypeStruct(s, a.dtype), memory_space=pltpu.MemorySpace.HBM)
             for s, a in zip(land_shapes, arrs)]

    @pl.kernel(mesh=plsc.ScalarSubcoreMesh(axis_name="sequencer", num_cores=1), name=name,
               scratch_types=(pltpu.SemaphoreType.DMA((n_sem,)), pltpu.SemaphoreType.DMA((n_sem,)),
                              pltpu.SemaphoreType.DMA((n,))),
               compiler_params=pltpu.CompilerParams(collective_id=collective_id))
    def launch(send, recv, local):
        barrier = pltpu.get_barrier_semaphore()
        for k in range(1, N_DEV):
            pl.semaphore_signal(barrier, inc=1, device_id=_peer(k), device_id_type=MESH)
        pl.semaphore_wait(barrier, N_DEV - 1)
        me = _my_index()
        mine = [pltpu.make_async_copy(src if kind == GATHER else src.at[me], land.at[me], local.at[a])
                for a, (src, land) in enumerate(zip(srcs, lands))]
        if kind == SCATTER:
            copies = mine + _exchange_copies(srcs, lands, send, recv)
            for cp in copies:
                cp.start()
            for cp in copies:
                cp.wait()
            return

        def block_copy(a, slot, block, k, src=None):
            s = a * (N_DEV - 1) + slot
            return pltpu.make_async_remote_copy(
                src_ref=lands[a].at[block] if src is None else src, dst_ref=lands[a].at[block],
                send_sem=send.at[s], recv_sem=recv.at[s], device_id=_peer(k), device_id_type=MESH)

        chips = (2, 4, 6)
        sibling = jnp.bitwise_xor(me, 1)
        first = [block_copy(a, slot, me, k, src=srcs[a]) for a in range(n) for slot, k in enumerate((1,) + chips)]
        for cp in mine + first:
            cp.start()
        passed = []
        for a in range(n):
            for i, k in enumerate(chips):
                block = jnp.bitwise_xor(me, k)
                block_copy(a, 1 + i, block, k).wait_recv()
                passed.append(block_copy(a, 4 + i, block, 1))
                passed[-1].start()
        for a in range(n):
            block_copy(a, 0, sibling, 1).wait_recv()
            for i, k in enumerate(chips):
                block_copy(a, 4 + i, jnp.bitwise_xor(sibling, k), 1).wait_recv()
        for cp in mine:
            cp.wait()
        for cp in first + passed:
            cp.wait_send()

    launch()
    return [land[...] for land in lands]


def _tile(n, pref):
    for t in pref:
        if n % t == 0:
            return t
    return n


MM_WIDE = 1024
MM_WEIGHT_BLOCK = 8 * 2 ** 20


def _blocks_per_step(nb, fits):
    return max(g for g in range(1, nb + 1) if nb % g == 0 and fits(g))


def mm_nn(a, b3, out_dtype, name, split_cols=None):
    m, k = a.shape
    nb, _, bn = b3.shape
    tm = _tile(m, (512, 256, 128))
    tn = bn // split_cols if split_cols else _tile(bn, (1024, 896, 512, 256, 128))
    per = bn // tn
    gb = _blocks_per_step(nb, lambda g: g == 1 or (per == 1 and g * bn <= MM_WIDE))

    def body(a_ref, b_ref, o_ref):
        for g in range(gb):
            o_ref[:, g * tn:(g + 1) * tn] = _dot(a_ref[...], b_ref[g]).astype(o_ref.dtype)

    if split_cols:
        out_spec = pl.BlockSpec((None, tm, tn), lambda i, j, jj: (jj, i, 0))
        out_shape = jax.ShapeDtypeStruct((split_cols, m, tn), out_dtype)
    else:
        out_spec = pl.BlockSpec((tm, gb * tn), lambda i, j, jj: (i, j * per + jj))
        out_shape = jax.ShapeDtypeStruct((m, nb * bn), out_dtype)
    return pl.pallas_call(
        body, name=name, grid=(m // tm, nb // gb, per),
        in_specs=[pl.BlockSpec((tm, k), lambda i, j, jj: (i, 0)),
                  pl.BlockSpec((gb, k, tn), lambda i, j, jj: (j, 0, jj))],
        out_specs=out_spec, out_shape=out_shape,
        compiler_params=_cparams("parallel", "arbitrary", "arbitrary"),
    )(a, b3)


def mm_nt(a, w3, out_dtype, name):
    m, _ = a.shape
    nb, ko, bn = w3.shape
    tm = _tile(m, (512, 256, 128))
    tko = _tile(ko, (1024, 512, 256, 128))
    gb = _blocks_per_step(nb, lambda g: g * tko * bn * w3.dtype.itemsize <= MM_WEIGHT_BLOCK)
    ns = nb // gb

    def body(a_ref, w_ref, o_ref, acc_ref):
        j = pl.program_id(2)

        @pl.when(j == 0)
        def _():
            acc_ref[...] = jnp.zeros_like(acc_ref)

        part = _dot_nt(a_ref[:, :bn], w_ref[0])
        for g in range(1, gb):
            part += _dot_nt(a_ref[:, g * bn:(g + 1) * bn], w_ref[g])
        acc_ref[...] += part

        @pl.when(j == ns - 1)
        def _():
            o_ref[...] = acc_ref[...].astype(o_ref.dtype)

    return pl.pallas_call(
        body, name=name, grid=(m // tm, ko // tko, ns),
        in_specs=[pl.BlockSpec((tm, gb * bn), lambda i, o, j: (i, j)),
                  pl.BlockSpec((gb, tko, bn), lambda i, o, j: (j, o, 0))],
        out_specs=pl.BlockSpec((tm, tko), lambda i, o, j: (i, o)),
        out_shape=jax.ShapeDtypeStruct((m, ko), out_dtype),
        scratch_shapes=[pltpu.VMEM((tm, tko), f32)],
        compiler_params=_cparams("parallel", "arbitrary", "arbitrary"),
    )(a, w3)


def mm_tn(a, dy, ncb, out_dtype, name):
    l, ka = a.shape
    _, n = dy.shape
    bn = n // ncb
    tl = _tile(l, (1024, 512, 256, 128))
    tka = _tile(ka, (512, 256, 128))
    tn = _tile(bn, (1024, 896, 512, 256, 128))
    per = bn // tn
    gb = _blocks_per_step(ncb, lambda g: g == 1 or (per == 1 and g * bn <= MM_WIDE))
    nl = l // tl

    def body(a_ref, dy_ref, o_ref, acc_ref):
        s = pl.program_id(2)

        @pl.when(s == 0)
        def _():
            acc_ref[...] = jnp.zeros_like(acc_ref)

        acc_ref[...] += _dot_tn(a_ref[...], dy_ref[...])

        @pl.when(s == nl - 1)
        def _():
            for g in range(gb):
                o_ref[g] = acc_ref[:, g * tn:(g + 1) * tn].astype(o_ref.dtype)

    return pl.pallas_call(
        body, name=name, grid=(ka // tka, n // (gb * tn), nl),
        in_specs=[pl.BlockSpec((tl, tka), lambda i, j, s: (s, i)),
                  pl.BlockSpec((tl, gb * tn), lambda i, j, s: (s, j))],
        out_specs=pl.BlockSpec((gb, tka, tn), lambda i, j, s: (j // per, i, j % per)),
        out_shape=jax.ShapeDtypeStruct((ncb, ka, bn), out_dtype),
        scratch_shapes=[pltpu.VMEM((tka, gb * tn), f32)],
        compiler_params=_cparams("parallel", "parallel", "arbitrary"),
    )(a, dy)


def mod_part(c_all, w_mod, b_cols):
    nl, d, cols = w_mod.shape

    def body(c_ref, w_ref, b_ref, o_ref):
        cond = _silu(c_ref[...]).astype(bf16)
        o_ref[...] = _dot(cond, w_ref[...].astype(bf16)) + b_ref[...]

    return pl.pallas_call(
        body, name="mod_part", grid=(nl,),
        in_specs=[pl.BlockSpec((N_DEV, d), lambda l: (0, 0)),
                  pl.BlockSpec((None, d, cols), lambda l: (l, 0, 0)),
                  pl.BlockSpec((None, 1, cols), lambda l: (l, 0, 0))],
        out_specs=pl.BlockSpec((None, N_DEV, cols), lambda l: (l, 0, 0)),
        out_shape=jax.ShapeDtypeStruct((nl, N_DEV, cols), f32),
        compiler_params=_cparams("arbitrary"),
    )(c_all, w_mod, b_cols.reshape(nl, 1, cols))


def _row_tile(l):
    return _tile(l, (512, 256, 128))


def _entry_rows(xv, g_ref, sh_ref, sc_ref, h_ref, ht_ref):
    r = lax.rsqrt(jnp.mean(xv * xv, axis=-1, keepdims=True) + EPS)
    h = xv * r * (g_ref[...] * (1.0 + sc_ref[...])) + sh_ref[...]
    h_ref[...] = h.astype(h_ref.dtype)
    ht_ref[...] = jnp.transpose(h).astype(ht_ref.dtype)


def prenorm_fwd(x, g, shift, scale, name):
    l, d = x.shape
    tm = _row_tile(l)

    def body(x_ref, g_ref, sh_ref, sc_ref, h_ref, ht_ref):
        _entry_rows(x_ref[...], g_ref, sh_ref, sc_ref, h_ref, ht_ref)

    return pl.pallas_call(
        body, name=name, grid=(l // tm,),
        in_specs=[pl.BlockSpec((tm, d), lambda i: (i, 0)), _row(d), _row(d), _row(d)],
        out_specs=[pl.BlockSpec((tm, d), lambda i: (i, 0)), pl.BlockSpec((d, tm), lambda i: (0, i))],
        out_shape=[jax.ShapeDtypeStruct((l, d), bf16), jax.ShapeDtypeStruct((d, l), bf16)],
        compiler_params=_cparams("parallel"),
    )(x, g, shift, scale)


def post_prenorm_fwd(x, y, gate, g_post, g_pre, shift, scale, name):
    l, d = x.shape
    tm = _row_tile(l)

    def body(x_ref, y_ref, gate_ref, gp_ref, g_ref, sh_ref, sc_ref, o_ref, h_ref, ht_ref):
        yv = y_ref[...]
        r = lax.rsqrt(jnp.mean(yv * yv, axis=-1, keepdims=True) + EPS)
        xv = x_ref[...] + gate_ref[...] * (yv * r * gp_ref[...])
        o_ref[...] = xv
        _entry_rows(xv, g_ref, sh_ref, sc_ref, h_ref, ht_ref)

    blk = pl.BlockSpec((tm, d), lambda i: (i, 0))
    return pl.pallas_call(
        body, name=name, grid=(l // tm,),
        in_specs=[blk, blk] + [_row(d)] * 5, out_specs=[blk, blk, pl.BlockSpec((d, tm), lambda i: (0, i))],
        out_shape=[jax.ShapeDtypeStruct((l, d), f32), jax.ShapeDtypeStruct((l, d), bf16),
                   jax.ShapeDtypeStruct((d, l), bf16)],
        compiler_params=_cparams("parallel"),
    )(x, y, gate, g_post, g_pre, shift, scale)


def _post_bwd_rows(dxv, yv, r, gate, gv, dy_ref, dgate_ref, dg_ref):
    yn = yv * r
    dgate_ref[...] += jnp.sum(dxv * yn * gv, axis=0, keepdims=True)
    dyg = dxv * gate
    dg_ref[...] += jnp.sum(dyg * yn, axis=0, keepdims=True)
    dyn = dyg * gv
    dy_ref[...] = (r * (dyn - yn * jnp.mean(dyn * yn, axis=-1, keepdims=True))).astype(dy_ref.dtype)


def final_loss(x, y, gate, g, target):
    l, d = x.shape
    tm = _row_tile(l)

    def body(x_ref, y_ref, gate_ref, g_ref, t_ref, dx_ref, loss_ref, dy_ref, dgate_ref, dg_ref):
        @pl.when(pl.program_id(0) == 0)
        def _():
            loss_ref[...] = jnp.zeros_like(loss_ref)
            dgate_ref[...] = jnp.zeros_like(dgate_ref)
            dg_ref[...] = jnp.zeros_like(dg_ref)

        yv, gate, gv = y_ref[...], gate_ref[...], g_ref[...]
        r = lax.rsqrt(jnp.mean(yv * yv, axis=-1, keepdims=True) + EPS)
        diff = x_ref[...] + gate * (yv * r * gv) - t_ref[...]
        dxv = diff * (1.0 / d)
        dx_ref[...] = dxv
        loss_ref[...] += jnp.sum(diff * diff)
        _post_bwd_rows(dxv, yv, r, gate, gv, dy_ref, dgate_ref, dg_ref)

    blk = pl.BlockSpec((tm, d), lambda i: (i, 0))
    return pl.pallas_call(
        body, name="final_loss", grid=(l // tm,),
        in_specs=[blk, blk, _row(d), _row(d), blk],
        out_specs=[blk, pl.BlockSpec((SUBLANES, HEAD), lambda i: (0, 0)), blk, _row(d), _row(d)],
        out_shape=[jax.ShapeDtypeStruct((l, d), f32), jax.ShapeDtypeStruct((SUBLANES, HEAD), f32),
                   jax.ShapeDtypeStruct((l, d), bf16), jax.ShapeDtypeStruct((1, d), f32), jax.ShapeDtypeStruct((1, d), f32)],
        compiler_params=_cparams("arbitrary"),
    )(x, y, gate, g, target)


def post_bwd(dx, y, gate, g, name):
    l, d = dx.shape
    tm = _row_tile(l)

    def body(dx_ref, y_ref, gate_ref, g_ref, dy_ref, dgate_ref, dg_ref):
        @pl.when(pl.program_id(0) == 0)
        def _():
            dgate_ref[...] = jnp.zeros_like(dgate_ref)
            dg_ref[...] = jnp.zeros_like(dg_ref)

        yv = y_ref[...]
        r = lax.rsqrt(jnp.mean(yv * yv, axis=-1, keepdims=True) + EPS)
        _post_bwd_rows(dx_ref[...], yv, r, gate_ref[...], g_ref[...], dy_ref, dgate_ref, dg_ref)

    blk = pl.BlockSpec((tm, d), lambda i: (i, 0))
    return pl.pallas_call(
        body, name=name, grid=(l // tm,),
        in_specs=[blk, blk, _row(d), _row(d)], out_specs=[blk, _row(d), _row(d)],
        out_shape=[jax.ShapeDtypeStruct((l, d), bf16), jax.ShapeDtypeStruct((1, d), f32),
                   jax.ShapeDtypeStruct((1, d), f32)],
        compiler_params=_cparams("arbitrary"),
    )(dx, y, gate, g)


def prenorm_bwd(dh, x, dx_next, g, scale, name):
    l, d = x.shape
    tm = _row_tile(l)

    def body(dh_ref, x_ref, dxn_ref, g_ref, sc_ref, dx_ref, dsh_ref, dsc_ref, dg_ref):
        @pl.when(pl.program_id(0) == 0)
        def _():
            dsh_ref[...] = jnp.zeros_like(dsh_ref)
            dsc_ref[...] = jnp.zeros_like(dsc_ref)
            dg_ref[...] = jnp.zeros_like(dg_ref)

        xv, dhv, gv, sc1 = x_ref[...], dh_ref[...], g_ref[...], 1.0 + sc_ref[...]
        r = lax.rsqrt(jnp.mean(xv * xv, axis=-1, keepdims=True) + EPS)
        xn = xv * r
        dhx = dhv * xn
        dsh_ref[...] += jnp.sum(dhv, axis=0, keepdims=True)
        dsc_ref[...] += jnp.sum(dhx * gv, axis=0, keepdims=True)
        dg_ref[...] += jnp.sum(dhx * sc1, axis=0, keepdims=True)
        dxn = dhv * (gv * sc1)
        dx_ref[...] = dxn_ref[...] + r * (dxn - xn * jnp.mean(dxn * xn, axis=-1, keepdims=True))

    blk = pl.BlockSpec((tm, d), lambda i: (i, 0))
    return pl.pallas_call(
        body, name=name, grid=(l // tm,),
        in_specs=[blk, blk, blk, _row(d), _row(d)], out_specs=[blk, _row(d), _row(d), _row(d)],
        out_shape=[jax.ShapeDtypeStruct((l, d), f32)] + [jax.ShapeDtypeStruct((1, d), f32)] * 3,
        compiler_params=_cparams("arbitrary"),
    )(dh, x, dx_next, g, scale)


def _tril_mask():
    r = lax.broadcasted_iota(jnp.int32, (HEAD, HEAD), 0)
    c = lax.broadcasted_iota(jnp.int32, (HEAD, HEAD), 1)
    return r >= c


def sgu_fwd(proj, norm_g, w_s, b_s):
    l = proj.shape[0]
    nh = w_s.shape[0]
    wa = nh * HEAD

    def body(au_ref, av_ref, az_ref, ng_ref, w_ref, b_ref, o_ref):
        tril = _tril_mask()
        for h in range(nh):
            sl = slice(h * HEAD, (h + 1) * HEAD)
            gv = _gelu(av_ref[:, sl].astype(f32))
            r = lax.rsqrt(jnp.mean(gv * gv, axis=-1, keepdims=True) + EPS)
            vh = gv * r * ng_ref[:, sl]
            wm = jnp.where(tril, w_ref[h], 0.0).astype(bf16)
            s = _dot(wm, vh.astype(bf16)) + b_ref[h]
            o_ref[:, sl] = (_gelu(au_ref[:, sl].astype(f32)) * s * _silu(az_ref[:, sl].astype(f32))).astype(o_ref.dtype)

    def col(j):
        return pl.BlockSpec((HEAD, wa), lambda n: (n, j))

    return pl.pallas_call(
        body, name="sgu_fwd", grid=(l // HEAD,),
        in_specs=[col(0), col(1), col(2), _row(wa),
                  pl.BlockSpec((nh, HEAD, HEAD), lambda n: (0, 0, 0)), pl.BlockSpec((nh, HEAD, 1), lambda n: (0, 0, 0))],
        out_specs=pl.BlockSpec((HEAD, wa), lambda n: (n, 0)),
        out_shape=jax.ShapeDtypeStruct((l, 2 * wa), bf16),
        compiler_params=_cparams("parallel"),
    )(proj, proj, proj, norm_g, w_s, b_s)


def sgu_bwd(proj, dcat, norm_g, w_s, b_s):
    l = proj.shape[0]
    nh = w_s.shape[0]
    wa = nh * HEAD

    def body(au_ref, av_ref, az_ref, do_ref, ng_ref, w_ref, b_ref, da_ref, dw_ref, db_ref, dng_ref):
        @pl.when(pl.program_id(0) == 0)
        def _():
            dw_ref[...] = jnp.zeros_like(dw_ref)
            db_ref[...] = jnp.zeros_like(db_ref)
            dng_ref[...] = jnp.zeros_like(dng_ref)

        tril = _tril_mask()
        for h in range(nh):
            sl = slice(h * HEAD, (h + 1) * HEAD)
            au, av, az = au_ref[:, sl].astype(f32), av_ref[:, sl].astype(f32), az_ref[:, sl].astype(f32)
            ng = ng_ref[:, sl]
            gv = _gelu(av)
            r = lax.rsqrt(jnp.mean(gv * gv, axis=-1, keepdims=True) + EPS)
            gvn = gv * r
            vh = (gvn * ng).astype(bf16)
            wm = jnp.where(tril, w_ref[h], 0.0).astype(bf16)
            s = _dot(wm, vh) + b_ref[h]
            gu, sz = _gelu(au), _silu(az)
            dov = do_ref[:, sl].astype(f32)
            da_ref[:, sl] = (dov * s * sz * _gelu_grad(au)).astype(da_ref.dtype)
            da_ref[:, 2 * wa + h * HEAD:2 * wa + (h + 1) * HEAD] = (dov * gu * s * _silu_grad(az)).astype(da_ref.dtype)
            ds = dov * gu * sz
            db_ref[h] += jnp.sum(ds, axis=-1, keepdims=True)
            dsb = ds.astype(bf16)
            dw_ref[h] += jnp.where(tril, _dot_nt(dsb, vh), 0.0)
            dvh = _dot_tn(wm, dsb)
            dng_ref[:, sl] += jnp.sum(dvh * gvn, axis=0, keepdims=True)
            dgvn = dvh * ng
            dgv = r * (dgvn - gvn * jnp.mean(dgvn * gvn, axis=-1, keepdims=True))
            da_ref[:, wa + h * HEAD:wa + (h + 1) * HEAD] = (dgv * _gelu_grad(av)).astype(da_ref.dtype)

    def col(j):
        return pl.BlockSpec((HEAD, wa), lambda n: (n, j))

    whole_w = pl.BlockSpec((nh, HEAD, HEAD), lambda n: (0, 0, 0))
    whole_b = pl.BlockSpec((nh, HEAD, 1), lambda n: (0, 0, 0))
    return pl.pallas_call(
        body, name="sgu_bwd", grid=(l // HEAD,),
        in_specs=[col(0), col(1), col(2), col(0), _row(wa), whole_w, whole_b],
        out_specs=[pl.BlockSpec((HEAD, 3 * wa), lambda n: (n, 0)), whole_w, whole_b, _row(wa)],
        out_shape=[jax.ShapeDtypeStruct((l, 3 * wa), bf16), jax.ShapeDtypeStruct((nh, HEAD, HEAD), f32),
                   jax.ShapeDtypeStruct((nh, HEAD, 1), f32), jax.ShapeDtypeStruct((1, wa), f32)],
        compiler_params=_cparams("arbitrary"),
    )(proj, proj, proj, dcat, norm_g, w_s, b_s)


_LOG2E = 1.0 / math.log(2.0)


def _sb_scores(q, k, scale):
    z = _dot_nt(q, k) * (scale * _LOG2E)
    return z, jnp.maximum(z, 0.0) + jnp.log2(1.0 + jnp.exp2(-jnp.abs(z)))


SB_KEYS = 256


def _sb_sum_matrix(tri, kb):
    s = lax.broadcasted_iota(jnp.int32, (2 * kb, kb + HEAD), 0) % kb
    j = lax.broadcasted_iota(jnp.int32, (2 * kb, kb + HEAD), 1)
    return jnp.where(jnp.logical_or(j >= kb, tri(s, j)), 1.0, 0.0).astype(bf16)


def _sb_sums(x, sums):
    kb = x.shape[1]
    c2 = _dot(jnp.concatenate(_split_bf16(x), axis=1), sums)
    return c2[:, :kb], c2[:, kb:]


def _sb_wide(v, kb):
    return jnp.concatenate([v] * (kb // HEAD), axis=1) if kb > HEAD else v


def _sb_q_tile(l, most=512):
    return _tile(l, tuple(t for t in (1024, 512, 256, 128) if t <= most))


def _sb_band_levels(band):
    return _tile(band, (4, 2, 1))


def _sb_heads_per_step(nh, most):
    return _tile(nh, tuple(h for h in (4, 2) if h <= most))


def sb_fwd(proj, mixed, nh):
    l = proj.shape[0]
    wb = nh * HEAD
    tq = _sb_q_tile(l, 1024)
    kb = min(SB_KEYS, tq)
    band = tq // kb
    hp = _sb_heads_per_step(nh, 2)
    levels = _sb_band_levels(band)
    scale = 1.0 / math.sqrt(HEAD)
    qc, kc, vc, zc = 3 * nh, 4 * nh, 5 * nh, 6 * nh

    def body(q_ref, k_ref, v_ref, bz_ref, mixed_ref, o_ref, att_ref, tot_ref):
        del mixed_ref
        i = pl.program_id(1)
        sums = _sb_sum_matrix(lambda s, j: s > j, kb)
        t_pos = i * tq + lax.broadcasted_iota(jnp.int32, (tq, kb), 0)
        s_off = lax.broadcasted_iota(jnp.int32, (tq, kb), 1)

        def step(j, carry, masked, row0=0):
            rows = pl.ds(pl.multiple_of(j * kb, kb), kb)
            out = []
            for e in range(hp):
                acc, tot = carry[e]
                sl = slice(e * HEAD, (e + 1) * HEAD)
                z, sp = _sb_scores(q_ref[row0:, sl], k_ref[rows, sl], scale)
                lb = z - sp
                if masked:
                    mask = s_off[row0:] + j * kb < t_pos[row0:]
                    sp = jnp.where(mask, sp, 0.0)
                later, total = _sb_sums(sp, sums)
                w = jnp.exp2(lb + _sb_wide(tot[row0:], kb) - later)
                if masked:
                    w = jnp.where(mask, w, 0.0)
                new = (acc[row0:] + _dot(w.astype(bf16), v_ref[rows, sl]), tot[row0:] - total)
                out.append(tuple(jnp.concatenate([old[:row0], upd]) if row0 else upd for old, upd in zip(carry[e], new)))
            return tuple(out)

        zero = jnp.zeros((tq, HEAD), f32)
        carry = ((zero, zero),) * hp
        for lv in reversed(range(levels)):
            carry = lax.fori_loop(
                0, band // levels,
                lambda t, c, lv=lv: step(band * i + (lv + 1) * (band // levels) - 1 - t, c, True, lv * (tq // levels)), carry)
        carry = lax.fori_loop(0, band * i, lambda t, c: step(band * i - 1 - t, c, False), carry)
        for e in range(hp):
            acc, tot = carry[e]
            sl = slice(e * HEAD, (e + 1) * HEAD)
            att_ref[:, sl] = acc.astype(att_ref.dtype)
            o_ref[:, sl] = (acc * _silu(bz_ref[:, sl].astype(f32))).astype(o_ref.dtype)
            tot_ref[e] = tot[:, :1]

    blk = lambda c0: pl.BlockSpec((tq, hp * HEAD), lambda g, i: (i, c0 // hp + g))
    head = lambda c0: pl.BlockSpec((l, hp * HEAD), lambda g, i: (0, c0 // hp + g))
    return pl.pallas_call(
        body, name="sb_fwd", grid=(nh // hp, l // tq),
        in_specs=[blk(qc), head(kc), head(vc), blk(zc), pl.BlockSpec(memory_space=pl.ANY)],
        out_specs=[blk(mixed.shape[1] // HEAD - nh), blk(0), pl.BlockSpec((hp, tq, 1), lambda g, i: (g, i, 0))],
        out_shape=[jax.ShapeDtypeStruct(mixed.shape, bf16), jax.ShapeDtypeStruct((l, wb), bf16),
                   jax.ShapeDtypeStruct((nh, l, 1), f32)],
        input_output_aliases={4: 0},
        compiler_params=_cparams("parallel", "arbitrary"),
    )(proj, proj, proj, proj, mixed)


def sb_bwd(proj, dcat, att, tot, nh):
    l = proj.shape[0]
    wb = nh * HEAD
    tq = _sb_q_tile(l, 1024)
    kb = min(SB_KEYS, tq)
    band = tq // kb
    nq = l // tq
    hp = _sb_heads_per_step(nh, 2)
    levels = _sb_band_levels(band)
    scale = 1.0 / math.sqrt(HEAD)
    qc, kc, vc, zc = 3 * nh, 4 * nh, 5 * nh, 6 * nh

    def body(q_ref, k_ref, v_ref, bz_ref, do_ref, att_ref, tot_ref, dq_ref, dk_ref, dv_ref, dbz_ref, dk_acc, dv_acc,
             dob_ref):
        i = pl.program_id(1)

        @pl.when(i == 0)
        def _():
            dk_acc[...] = jnp.zeros_like(dk_acc)
            dv_acc[...] = jnp.zeros_like(dv_acc)

        bz = bz_ref[...].astype(f32)
        dov = do_ref[...].astype(f32)
        dbz_ref[...] = (dov * att_ref[...].astype(f32) * _silu_grad(bz)).astype(dbz_ref.dtype)
        dob_ref[...] = (dov * _silu(bz)).astype(bf16)
        upto = _sb_sum_matrix(lambda s, j: s <= j, kb)
        before = _sb_sum_matrix(lambda j, s: j < s, kb)
        t_pos = i * tq + lax.broadcasted_iota(jnp.int32, (tq, kb), 0)
        s_off = lax.broadcasted_iota(jnp.int32, (tq, kb), 1)

        def step(j, carry, masked, row0=0):
            rows = pl.ds(pl.multiple_of(j * kb, kb), kb)
            out = []
            for h in range(hp):
                dq, sp_seen, e_seen = (c[row0:] for c in carry[h])
                sl = slice(h * HEAD, (h + 1) * HEAD)
                q, kj, vj, dob = q_ref[row0:, sl], k_ref[rows, sl], v_ref[rows, sl], dob_ref[row0:, sl]
                z, sp = _sb_scores(q, kj, scale)
                lb = z - sp
                if masked:
                    mask = s_off[row0:] + j * kb < t_pos[row0:]
                    sp = jnp.where(mask, sp, 0.0)
                sp_upto, sp_total = _sb_sums(sp, upto)
                w = jnp.exp2(lb + _sb_wide(sp_seen, kb) + sp_upto)
                if masked:
                    w = jnp.where(mask, w, 0.0)
                dv_acc[rows, sl] += _dot_tn(w.astype(bf16), dob)
                e = _dot_nt(dob, vj) * w
                e_before, e_total = _sb_sums(e, before)
                dz = (e - (e + _sb_wide(e_seen, kb) + e_before) * jnp.exp2(lb)) * scale
                if masked:
                    dz = jnp.where(mask, dz, 0.0)
                dz = dz.astype(bf16)
                dk_acc[rows, sl] += _dot_tn(dz, q)
                new = (dq + _dot(dz, kj), sp_seen + sp_total, e_seen + e_total)
                out.append(tuple(jnp.concatenate([old[:row0], upd]) if row0 else upd for old, upd in zip(carry[h], new)))
            return tuple(out)

        zero = jnp.zeros((tq, HEAD), f32)
        init = tuple((zero, jnp.broadcast_to(tot_ref[h], (tq, HEAD)), zero) for h in range(hp))
        carry = lax.fori_loop(0, band * i, lambda j, c: step(j, c, False), init)
        for lv in range(levels):
            carry = lax.fori_loop(
                0, band // levels,
                lambda t, c, lv=lv: step(band * i + lv * (band // levels) + t, c, True, lv * (tq // levels)), carry)
        for h in range(hp):
            dq_ref[:, h * HEAD:(h + 1) * HEAD] = carry[h][0].astype(dq_ref.dtype)

        @pl.when(i == nq - 1)
        def _():
            dk_ref[...] = dk_acc[...].astype(dk_ref.dtype)
            dv_ref[...] = dv_acc[...].astype(dv_ref.dtype)

    blk = lambda c0: pl.BlockSpec((tq, hp * HEAD), lambda g, i: (i, c0 // hp + g))
    head = lambda c0: pl.BlockSpec((l, hp * HEAD), lambda g, i: (0, c0 // hp + g))
    return pl.pallas_call(
        body, name="sb_bwd", grid=(nh // hp, nq),
        in_specs=[blk(qc), head(kc), head(vc), blk(zc), blk(nh), blk(0),
                  pl.BlockSpec((hp, tq, 1), lambda g, i: (g, i, 0))],
        out_specs=[blk(0), head(0), head(0), blk(0)],
        out_shape=[jax.ShapeDtypeStruct((l, wb), bf16)] * 4,
        scratch_shapes=[pltpu.VMEM((l, hp * HEAD), f32), pltpu.VMEM((l, hp * HEAD), f32),
                        pltpu.VMEM((tq, hp * HEAD), bf16)],
        compiler_params=_cparams("parallel", "arbitrary"),
    )(proj, proj, proj, proj, dcat, att, tot)


def _disc(lr, li, ldt):
    dt = jnp.exp(ldt)
    mag = jnp.exp(lr * dt)
    a_re = mag * jnp.cos(li * dt)
    a_im = mag * jnp.sin(li * dt)
    den = lr * lr + li * li
    nr = a_re - 1.0
    return a_re, a_im, (nr * lr + a_im * li) / den, (a_im * lr - nr * li) / den


def s5_params_fwd(lr, li, ldt, bt_re, bt_im):
    g, c, p = bt_re.shape

    def body(lr_ref, li_ref, ldt_ref, br_ref, bi_ref, ar_ref, ai_ref, bbr_ref, bbi_ref):
        a_re, a_im, cr, ci = _disc(lr_ref[...], li_ref[...], ldt_ref[...])
        ar_ref[...] = a_re
        ai_ref[...] = a_im
        for k in range(c):
            br, bi = br_ref[:, k, :], bi_ref[:, k, :]
            bbr_ref[:, k, :] = cr * br - ci * bi
            bbi_ref[:, k, :] = cr * bi + ci * br

    return pl.pallas_call(
        body, name="s5_params_fwd",
        out_shape=[jax.ShapeDtypeStruct((g, p), f32)] * 2 + [jax.ShapeDtypeStruct((g, c, p), f32)] * 2,
    )(lr, li, ldt, bt_re, bt_im)


def s5_params_bwd(lr, li, ldt, bt_re, bt_im, da_re, da_im, dbbt_re, dbbt_im):
    g, c, p = bt_re.shape

    def body(lr_ref, li_ref, ldt_ref, br_ref, bi_ref, dar_ref, dai_ref, dbbr_ref, dbbi_ref,
             dlr_ref, dli_ref, dldt_ref, dbr_ref, dbi_ref):
        (a_re, a_im, cr, ci), vjp = jax.vjp(_disc, lr_ref[...], li_ref[...], ldt_ref[...])
        dcr = jnp.zeros((g, p), f32)
        dci = jnp.zeros((g, p), f32)
        for k in range(c):
            br, bi = br_ref[:, k, :], bi_ref[:, k, :]
            dr, di = dbbr_ref[:, k, :], dbbi_ref[:, k, :]
            dcr += dr * br + di * bi
            dci += di * br - dr * bi
            dbr_ref[:, k, :] = cr * dr + ci * di
            dbi_ref[:, k, :] = cr * di - ci * dr
        dlr, dli, dldt = vjp((dar_ref[...], dai_ref[...], dcr, dci))
        dlr_ref[...] = dlr
        dli_ref[...] = dli
        dldt_ref[...] = dldt

    return pl.pallas_call(
        body, name="s5_params_bwd",
        out_shape=[jax.ShapeDtypeStruct((g, p), f32)] * 2 + [jax.ShapeDtypeStruct((g, 1), f32)]
        + [jax.ShapeDtypeStruct((g, c, p), f32)] * 2,
    )(lr, li, ldt, bt_re, bt_im, da_re, da_im, dbbt_re, dbbt_im)


def _cmul(ar, ai, br, bi):
    return ar * br - ai * bi, ar * bi + ai * br


def _power_tables(ar, ai):
    rows = lax.broadcasted_iota(jnp.int32, (SUBLANES, ar.shape[1]), 0)
    pr = jnp.zeros((SUBLANES, ar.shape[1]), f32)
    pi = jnp.zeros((SUBLANES, ar.shape[1]), f32)
    cr, ci = ar, ai
    pows = {}
    for r in range(SUBLANES):
        pows[r + 1] = (cr, ci)
        pr = jnp.where(rows == r, cr, pr)
        pi = jnp.where(rows == r, ci, pi)
        cr, ci = _cmul(cr, ci, ar, ai)
    return [pows[1], pows[2], pows[4]], pr, pi


def _ssm_time_tile(l):
    return _tile(l, (2048, 1024, 512, 256, 128))


def ssm_fwd(u, bre3, bim3, cre3, cimn3, a_re, a_im, d_skip):
    l, w = u.shape[0], d_skip.shape[1]
    nj = w // HEAD
    ns = STATES_PER_LANE_BLOCK
    tt = _ssm_time_tile(l)

    def body(u_ref, bre_ref, bim_ref, cre_ref, cim_ref, ar_ref, ai_ref, d_ref, y_ref, hr_ref, hi_ref, cr_ref, ci_ref):
        @pl.when(pl.program_id(1) == 0)
        def _():
            cr_ref[...] = jnp.zeros_like(cr_ref)
            ci_ref[...] = jnp.zeros_like(ci_ref)

        uv = u_ref[...]
        hr_ref[...] = _dot(uv, bre_ref[...])
        hi_ref[...] = _dot(uv, bim_ref[...])
        steps, pr, pi = _power_tables(ar_ref[...], ai_ref[...])
        rows = lax.broadcasted_iota(jnp.int32, (SUBLANES, ns), 0)
        steps = [(jnp.where(rows >= d, sr_, 0.0), jnp.where(rows >= d, si_, 0.0)) for d, (sr_, si_) in zip((1, 2, 4), steps)]

        def blk(b, carry):
            cr, ci = carry
            sl = pl.ds(pl.multiple_of(b * SUBLANES, SUBLANES), SUBLANES)
            xr, xi = hr_ref[sl, :], hi_ref[sl, :]
            for d, (sr_, si_) in zip((1, 2, 4), steps):
                mr, mi = _cmul(sr_, si_, pltpu.roll(xr, d, axis=0), pltpu.roll(xi, d, axis=0))
                xr, xi = xr + mr, xi + mi
            mr, mi = _cmul(pr, pi, cr, ci)
            xr, xi = xr + mr, xi + mi
            hr_ref[sl, :] = xr
            hi_ref[sl, :] = xi
            return xr[SUBLANES - 1:, :], xi[SUBLANES - 1:, :]

        cr, ci = lax.fori_loop(0, tt // SUBLANES, blk, (cr_ref[...], ci_ref[...]))
        cr_ref[...] = cr
        ci_ref[...] = ci
        y = _dot(hr_ref[...].astype(bf16), cre_ref[...]) + _dot(hi_ref[...].astype(bf16), cim_ref[...])
        y_ref[...] = y + d_ref[...] * uv.astype(f32)

    lane = pl.BlockSpec((tt, HEAD), lambda j, i: (i, j))
    st = pl.BlockSpec((tt, ns), lambda j, i: (i, j))
    b3 = pl.BlockSpec((None, HEAD, ns), lambda j, i: (j, 0, 0))
    c3 = pl.BlockSpec((None, ns, HEAD), lambda j, i: (j, 0, 0))
    arow = pl.BlockSpec((1, ns), lambda j, i: (0, j))
    return pl.pallas_call(
        body, name="ssm_fwd", grid=(nj, l // tt),
        in_specs=[lane, b3, b3, c3, c3, arow, arow, pl.BlockSpec((1, HEAD), lambda j, i: (0, j))],
        out_specs=[lane, st, st],
        out_shape=[jax.ShapeDtypeStruct((l, w), f32), jax.ShapeDtypeStruct((l, nj * ns), f32),
                   jax.ShapeDtypeStruct((l, nj * ns), f32)],
        scratch_shapes=[pltpu.VMEM((1, ns), f32), pltpu.VMEM((1, ns), f32)],
        compiler_params=_cparams("parallel", "arbitrary"),
    )(u, bre3, bim3, cre3, cimn3, a_re, a_im, d_skip)


def ssm_bwd(dy, u, dproj, h_re, h_im, bre3, bim3, cre3, cimn3, a_re, a_im, d_skip):
    l, w = u.shape[0], d_skip.shape[1]
    nj = w // HEAD
    ns = STATES_PER_LANE_BLOCK
    tt = _ssm_time_tile(l)
    nt = l // tt

    def body(dy_ref, u_ref, dproj_ref, hr_ref, hi_ref, bre_ref, bim_ref, cre_ref, cim_ref, ar_ref, ai_ref, d_ref,
             du_ref, dd_ref, dar_ref, dai_ref, dbre_ref, dbim_ref, dcre_ref, dcim_ref, kr_ref, ki_ref, cr_ref, ci_ref,
             accr_ref, acci_ref):
        del dproj_ref
        i = pl.program_id(1)

        @pl.when(i == 0)
        def _():
            for ref in (cr_ref, ci_ref, accr_ref, acci_ref, dd_ref, dbre_ref, dbim_ref, dcre_ref, dcim_ref):
                ref[...] = jnp.zeros_like(ref)

        dyv = dy_ref[...]
        dyb = dyv.astype(bf16)
        uv = u_ref[...]
        kr_ref[...] = _dot_nt(dyb, cre_ref[...])
        ki_ref[...] = _dot_nt(dyb, cim_ref[...])
        steps, pr, pi = _power_tables(ar_ref[...], -ai_ref[...])
        rows = lax.broadcasted_iota(jnp.int32, (SUBLANES, ns), 0)
        qr = jnp.zeros((SUBLANES, ns), f32)
        qi = jnp.zeros((SUBLANES, ns), f32)
        for r in range(SUBLANES):
            qr = jnp.where(rows == r, pr[SUBLANES - 1 - r:SUBLANES - r, :], qr)
            qi = jnp.where(rows == r, pi[SUBLANES - 1 - r:SUBLANES - r, :], qi)
        nb = tt // SUBLANES
        steps = [(jnp.where(rows < SUBLANES - d, sr_, 0.0), jnp.where(rows < SUBLANES - d, si_, 0.0))
                 for d, (sr_, si_) in zip((1, 2, 4), steps)]

        def blk(t, carry):
            cr, ci, accr, acci = carry
            sl = pl.ds(pl.multiple_of((nb - 1 - t) * SUBLANES, SUBLANES), SUBLANES)
            xr, xi = kr_ref[sl, :], ki_ref[sl, :]
            for d, (sr_, si_) in zip((1, 2, 4), steps):
                mr, mi = _cmul(sr_, si_, pltpu.roll(xr, SUBLANES - d, axis=0), pltpu.roll(xi, SUBLANES - d, axis=0))
                xr, xi = xr + mr, xi + mi
            mr, mi = _cmul(qr, qi, cr, ci)
            xr, xi = xr + mr, xi + mi
            kr_ref[sl, :] = xr
            ki_ref[sl, :] = xi
            last = rows == SUBLANES - 1
            nr = jnp.where(last, cr, pltpu.roll(xr, SUBLANES - 1, axis=0))
            ni = jnp.where(last, ci, pltpu.roll(xi, SUBLANES - 1, axis=0))
            hr, hi = hr_ref[sl, :], hi_ref[sl, :]
            accr = accr + nr * hr + ni * hi
            acci = acci + ni * hr - nr * hi
            return xr[:1, :], xi[:1, :], accr, acci

        cr, ci, accr, acci = lax.fori_loop(0, nb, blk, (cr_ref[...], ci_ref[...], accr_ref[...], acci_ref[...]))
        cr_ref[...] = cr
        ci_ref[...] = ci
        accr_ref[...] = accr
        acci_ref[...] = acci
        kr, ki = kr_ref[...].astype(bf16), ki_ref[...].astype(bf16)
        du = _dot_nt(kr, bre_ref[...]) + _dot_nt(ki, bim_ref[...]) + d_ref[...] * dyv
        du_ref[...] = du.astype(du_ref.dtype)
        dd_ref[...] += jnp.sum(dyv * uv.astype(f32), axis=0, keepdims=True)
        dbre_ref[...] += _dot_tn(uv, kr)
        dbim_ref[...] += _dot_tn(uv, ki)
        dcre_ref[...] += _dot_tn(hr_ref[...].astype(bf16), dyb)
        dcim_ref[...] += _dot_tn(hi_ref[...].astype(bf16), dyb)

        @pl.when(i == nt - 1)
        def _():
            dar_ref[...] = jnp.sum(accr_ref[...], axis=0, keepdims=True)
            dai_ref[...] = jnp.sum(acci_ref[...], axis=0, keepdims=True)

    lane = pl.BlockSpec((tt, HEAD), lambda j, i: (nt - 1 - i, j))
    st = pl.BlockSpec((tt, ns), lambda j, i: (nt - 1 - i, j))
    b3 = pl.BlockSpec((None, HEAD, ns), lambda j, i: (j, 0, 0))
    c3 = pl.BlockSpec((None, ns, HEAD), lambda j, i: (j, 0, 0))
    arow = pl.BlockSpec((1, ns), lambda j, i: (0, j))
    drow = pl.BlockSpec((1, HEAD), lambda j, i: (0, j))
    return pl.pallas_call(
        body, name="ssm_bwd", grid=(nj, nt),
        in_specs=[lane, lane, pl.BlockSpec(memory_space=pl.ANY), st, st, b3, b3, c3, c3, arow, arow, drow],
        out_specs=[lane, drow, arow, arow, b3, b3, c3, c3], input_output_aliases={2: 0},
        out_shape=[jax.ShapeDtypeStruct(dproj.shape, bf16), jax.ShapeDtypeStruct((1, w), f32),
                   jax.ShapeDtypeStruct((1, nj * ns), f32), jax.ShapeDtypeStruct((1, nj * ns), f32),
                   jax.ShapeDtypeStruct((nj, HEAD, ns), f32), jax.ShapeDtypeStruct((nj, HEAD, ns), f32),
                   jax.ShapeDtypeStruct((nj, ns, HEAD), f32), jax.ShapeDtypeStruct((nj, ns, HEAD), f32)],
        scratch_shapes=[pltpu.VMEM((tt, ns), f32), pltpu.VMEM((tt, ns), f32), pltpu.VMEM((1, ns), f32),
                        pltpu.VMEM((1, ns), f32), pltpu.VMEM((SUBLANES, ns), f32), pltpu.VMEM((SUBLANES, ns), f32)],
        compiler_params=_cparams("parallel", "arbitrary"),
    )(dy, u, dproj, h_re, h_im, bre3, bim3, cre3, cimn3, a_re, a_im, d_skip)


def glu_fwd(y, z_src, w_glu, b_glu):
    l, w = y.shape
    tm = _row_tile(l)

    def body(y_ref, z_ref, w_ref, b_ref, g_ref, t_ref, o_ref):
        g = _gelu(y_ref[...])
        gb = g.astype(bf16)
        t = _dot(gb, w_ref[...]) + b_ref[...]
        g_ref[...] = gb
        t_ref[...] = t
        o_ref[...] = (g * jax.nn.sigmoid(t) * _silu(z_ref[...].astype(f32))).astype(o_ref.dtype)

    blk = pl.BlockSpec((tm, w), lambda i: (i, 0))
    return pl.pallas_call(
        body, name="glu_fwd", grid=(l // tm,),
        in_specs=[blk, pl.BlockSpec((tm, w), lambda i: (i, 1)), pl.BlockSpec((w, w), lambda i: (0, 0)), _row(w)],
        out_specs=[blk, blk, blk],
        out_shape=[jax.ShapeDtypeStruct((l, w), bf16), jax.ShapeDtypeStruct((l, w), f32),
                   jax.ShapeDtypeStruct((l, w), bf16)],
        compiler_params=_cparams("parallel"),
    )(y, z_src, w_glu, b_glu)


def glu_bwd(dout, y, t, z_src, w_glu):
    l, w = y.shape
    tm = _row_tile(l)

    def body(do_ref, y_ref, t_ref, z_ref, w_ref, dy_ref, dz_ref, dt_ref, db_ref):
        @pl.when(pl.program_id(0) == 0)
        def _():
            db_ref[...] = jnp.zeros_like(db_ref)

        yv, zv, dov = y_ref[...], z_ref[...].astype(f32), do_ref[...].astype(f32)
        g = _gelu(yv)
        sg = jax.nn.sigmoid(t_ref[...])
        dy2 = dov * _silu(zv)
        dz_ref[...] = (dov * g * sg * _silu_grad(zv)).astype(dz_ref.dtype)
        dt = dy2 * g * sg * (1.0 - sg)
        dtb = dt.astype(bf16)
        dt_ref[...] = dtb
        db_ref[...] += jnp.sum(dt, axis=0, keepdims=True)
        dg = dy2 * sg + _dot_nt(dtb, w_ref[...])
        dy_ref[...] = dg * _gelu_grad(yv)

    blk = pl.BlockSpec((tm, w), lambda i: (i, 0))
    return pl.pallas_call(
        body, name="glu_bwd", grid=(l // tm,),
        in_specs=[blk, blk, blk, pl.BlockSpec((tm, w), lambda i: (i, 1)), pl.BlockSpec((w, w), lambda i: (0, 0))],
        out_specs=[blk, pl.BlockSpec((tm, w), lambda i: (i, 1)), blk, _row(w)],
        out_shape=[jax.ShapeDtypeStruct((l, w), f32), jax.ShapeDtypeStruct((l, 2 * w), bf16),
                   jax.ShapeDtypeStruct((l, w), bf16), jax.ShapeDtypeStruct((1, w), f32)],
        compiler_params=_cparams("arbitrary"),
    )(dout, y, t, z_src, w_glu)


def _adamw(w, g, m, v):
    m = ADAM_B1 * m + (1.0 - ADAM_B1) * g
    v = ADAM_B2 * v + (1.0 - ADAM_B2) * (g * g)
    m_hat = m / (1.0 - ADAM_B1 ** ADAM_STEP)
    v_hat = v / (1.0 - ADAM_B2 ** ADAM_STEP)
    return -ADAM_LR * (m_hat / (jnp.sqrt(v_hat) + ADAM_EPS) + ADAM_WD * w), m, v


def adam_reduce(pieces, w, m, v, name):
    r, c = w.shape
    n = pieces.shape[0]
    tr = _tile(r, (256, 128, 64, 32, 16, 8))

    def body(p_ref, w_ref, m_ref, v_ref, g_ref, d_ref, nm_ref, nv_ref):
        g = p_ref[0].astype(f32)
        for s in range(1, n):
            g = g + p_ref[s].astype(f32)
        g_ref[...] = g
        d_ref[...], nm_ref[...], nv_ref[...] = _adamw(w_ref[...], g, m_ref[...], v_ref[...])

    blk = pl.BlockSpec((tr, c), lambda i: (i, 0))
    return pl.pallas_call(
        body, name=name, grid=(r // tr,),
        in_specs=[pl.BlockSpec((n, tr, c), lambda i: (0, i, 0)), blk, blk, blk],
        out_specs=[blk] * 4, out_shape=[jax.ShapeDtypeStruct((r, c), f32)] * 4,
        compiler_params=_cparams("parallel"),
    )(pieces, w, m, v)


def adam_w_mod(cond_t, dm, w, m, v):
    nl, d, cols = w.shape
    tr = _tile(d, (512, 256, 128))

    def body(c_ref, dm_ref, w_ref, m_ref, v_ref, g_ref, d_ref, nm_ref, nv_ref):
        g = jnp.dot(c_ref[...], dm_ref[...], preferred_element_type=f32, precision=lax.Precision.HIGHEST)
        g_ref[...] = g
        d_ref[...], nm_ref[...], nv_ref[...] = _adamw(w_ref[...], g, m_ref[...], v_ref[...])

    blk = pl.BlockSpec((None, tr, cols), lambda l, i: (l, i, 0))
    return pl.pallas_call(
        body, name="adam_w_mod", grid=(nl, d // tr),
        in_specs=[pl.BlockSpec((tr, N_DEV), lambda l, i: (i, 0)), pl.BlockSpec((None, N_DEV, cols), lambda l, i: (l, 0, 0)),
                  blk, blk, blk],
        out_specs=[blk] * 4, out_shape=[jax.ShapeDtypeStruct((nl, d, cols), f32)] * 4,
        compiler_params=_cparams("parallel", "parallel"),
    )(cond_t, dm, w, m, v)


def silu_rows(c_all):
    def body(c_ref, o_ref):
        o_ref[...] = _silu(c_ref[...])

    return pl.pallas_call(body, name="silu_rows", out_shape=jax.ShapeDtypeStruct(c_all.shape, f32))(c_all)


def _block_diag(x):
    g, a, b = x.shape
    nj = g // GROUPS_PER_LANE_BLOCK
    eye = jnp.eye(GROUPS_PER_LANE_BLOCK, dtype=x.dtype)
    x5 = x.reshape(nj, GROUPS_PER_LANE_BLOCK, a, b)
    return jnp.einsum("jgab,gh->jgahb", x5, eye).reshape(nj, GROUPS_PER_LANE_BLOCK * a, GROUPS_PER_LANE_BLOCK * b)


def _diag_blocks(x, a, b):
    nj = x.shape[0]
    x5 = x.reshape(nj, GROUPS_PER_LANE_BLOCK, a, GROUPS_PER_LANE_BLOCK, b)
    eye = jnp.eye(GROUPS_PER_LANE_BLOCK, dtype=x.dtype)
    return jnp.einsum("jgahb,gh->jgab", x5, eye).reshape(nj * GROUPS_PER_LANE_BLOCK, a, b)


PACK_ROW = SUBLANES * HEAD


def _pack(parts, row_multiple=SUBLANES):
    rows = []
    for p in parts:
        flat = p.reshape(-1)
        pad = (-flat.shape[0]) % PACK_ROW
        if pad:
            flat = jnp.concatenate([flat, jnp.zeros((pad,), flat.dtype)])
        rows.append(flat.reshape(-1, HEAD))
    pad = (-sum(r.shape[0] for r in rows)) % row_multiple
    if pad:
        rows.append(jnp.zeros((pad, HEAD), rows[0].dtype))
    return jnp.concatenate(rows, axis=0)


def _unpack(packed, shapes):
    out, r0 = [], 0
    for shp in shapes:
        n = math.prod(shp)
        nr = -(-n // PACK_ROW) * SUBLANES
        out.append(packed[r0:r0 + nr].reshape(-1)[:n].reshape(shp))
        r0 += nr
    return out


def adam_small(g, w, m, v):
    r, c = w.shape

    def body(g_ref, w_ref, m_ref, v_ref, d_ref, nm_ref, nv_ref):
        d_ref[...], nm_ref[...], nv_ref[...] = _adamw(w_ref[...], g_ref[...], m_ref[...], v_ref[...])

    tr = max(t for t in range(SUBLANES, 1024 + 1, SUBLANES) if r % t == 0)
    blk = pl.BlockSpec((tr, c), lambda i: (i, 0))
    return pl.pallas_call(
        body, name="adam_small", grid=(r // tr,),
        in_specs=[blk] * 4, out_specs=[blk] * 3, out_shape=[jax.ShapeDtypeStruct((r, c), f32)] * 3,
        compiler_params=_cparams("parallel"),
    )(g, w, m, v)


def kernel(x, c, ln_pre_g, ln_post_g, w_mod, b_mod, w_in_ab, w_out_ab, sgu_norm_g, sgu_w, sgu_b, w_in_ssm, w_out_ssm, lam_re, lam_im, b_re, b_im, c_re, c_im, d_skip, log_dt, w_glu, b_glu, loss_target, m_ln_pre_g, m_ln_post_g, m_w_mod, m_b_mod, m_w_in_ab, m_w_out_ab, m_sgu_norm_g, m_sgu_w, m_sgu_b, m_w_in_ssm, m_w_out_ssm, m_lam_re, m_lam_im, m_b_re, m_b_im, m_c_re, m_c_im, m_d_skip, m_log_dt, m_w_glu, m_b_glu, v_ln_pre_g, v_ln_post_g, v_w_mod, v_b_mod, v_w_in_ab, v_w_out_ab, v_sgu_norm_g, v_sgu_w, v_sgu_b, v_w_in_ssm, v_w_out_ssm, v_lam_re, v_lam_im, v_b_re, v_b_im, v_c_re, v_c_im, v_d_skip, v_log_dt, v_w_glu, v_b_glu):
    me = _my_index()
    x0 = x[0]
    l, d = x0.shape
    target = loss_target[0]
    nh = sgu_w.shape[1]
    wa = nh * HEAD
    n_grp, n_st = lam_re.shape[1], lam_re.shape[2]
    mod_cols = w_mod.shape[2]

    def after(a, first):
        return a + jnp.minimum(jnp.abs(first[(0,) * first.ndim].astype(f32)), 0.0).astype(a.dtype)

    (win_ab3,) = sequencer_exchange(GATHER, [w_in_ab[0].astype(bf16)], "gather_w_in", 1)
    c_all, d_skip_all, b_glu_all = all_gather([c, d_skip, b_glu], "gather_c")
    c_all = c_all.reshape(N_DEV, d)
    d_skip_all = d_skip_all.reshape(1, -1)
    b_glu_all = b_glu_all.reshape(1, -1)

    b_cols = lax.dynamic_slice_in_dim(b_mod, me * mod_cols, mod_cols, axis=1)
    (mod_all,) = all_gather([mod_part(c_all, w_mod, b_cols)], "gather_mod")
    mod_mine = lax.dynamic_index_in_dim(mod_all, me, axis=2, keepdims=False)
    mod_rows = jnp.transpose(mod_mine, (1, 0, 2)).reshape(2, 3, 1, d)

    def rows(a, i):
        return a[i].reshape(1, d)

    shift0, scale0, gate0 = mod_rows[0, 0], mod_rows[0, 1], mod_rows[0, 2]
    h0, h0_t = prenorm_fwd(x0, rows(ln_pre_g, 0), shift0, scale0, "prenorm0")
    wout_ab3, win_ssm3, wout_ssm3, wglu = sequencer_exchange(
        GATHER, [after(w, win_ab3).astype(bf16) for w in (w_out_ab[0], w_in_ssm[0], w_out_ssm[0], w_glu[0])],
        "gather_w_rest", 2)
    proj0 = mm_nn(h0, win_ab3, bf16, "proj0")
    sgu_b3 = sgu_b[0].reshape(nh, HEAD, 1)
    cat, att, tot = sb_fwd(proj0, sgu_fwd(proj0, sgu_norm_g, sgu_w[0], sgu_b3), nh)
    wout_ab3 = wout_ab3.reshape(1, d, d)
    win_ssm3 = win_ssm3.reshape(1, d, d)
    wglu = wglu.reshape(w_glu.shape[2], w_glu.shape[2])
    y0 = mm_nn(cat, wout_ab3, f32, "out0")

    shift1, scale1, gate1 = mod_rows[1, 0], mod_rows[1, 1], mod_rows[1, 2]
    x1, h1, h1_t = post_prenorm_fwd(x0, y0, gate0, rows(ln_post_g, 0), rows(ln_pre_g, 1), shift1, scale1,
                                    "post0_prenorm1")
    proj1 = mm_nn(h1, win_ssm3, bf16, "proj1")
    w_ssm = proj1.shape[1] // 2
    ldt = log_dt[0].reshape(n_grp, 1)
    bt_re = jnp.transpose(b_re[0], (0, 2, 1))
    bt_im = jnp.transpose(b_im[0], (0, 2, 1))
    a_re, a_im, bbt_re, bbt_im = s5_params_fwd(lam_re[0], lam_im[0], ldt, bt_re, bt_im)
    bre3 = _block_diag(bbt_re).astype(bf16)
    bim3 = _block_diag(bbt_im).astype(bf16)
    cre3 = _block_diag(jnp.transpose(c_re[0], (0, 2, 1))).astype(bf16)
    cimn3 = _block_diag(-jnp.transpose(c_im[0], (0, 2, 1))).astype(bf16)
    a_re_row, a_im_row = a_re.reshape(1, -1), a_im.reshape(1, -1)
    y_ssm, hs_re, hs_im = ssm_fwd(proj1, bre3, bim3, cre3, cimn3, a_re_row, a_im_row, d_skip_all)
    g_act, t_glu, mix1 = glu_fwd(y_ssm, proj1, wglu, b_glu_all)
    y1 = mm_nn(mix1, wout_ssm3, f32, "out1")

    dx2, loss_tile, dy1, dgate1, dgpost1 = final_loss(x1, y1, gate1, rows(ln_post_g, 1), target)

    dmix1 = mm_nt(dy1, wout_ssm3, bf16, "dmix1")
    gw_out_ssm = mm_tn(mix1, dy1, N_DEV, bf16, "gw_out_ssm")
    (p_out_ssm,) = sequencer_exchange(SCATTER, [gw_out_ssm], "scatter_g1", 3)
    dy_ssm, dproj1, dt_glu, db_glu = glu_bwd(dmix1, y_ssm, t_glu, proj1, wglu)
    gw_glu = mm_tn(g_act, dt_glu, 1, bf16, "gw_glu").reshape(N_DEV, -1, w_ssm)
    dproj1, dd_skip, da_re, da_im, dbre3, dbim3, dcre3, dcimn3 = ssm_bwd(
        dy_ssm, proj1, dproj1, hs_re, hs_im, bre3, bim3, cre3, cimn3, a_re_row, a_im_row, d_skip_all)
    gw_in_ssm = mm_nn(h1_t, dproj1[None], bf16, "gw_in_ssm").reshape(N_DEV, -1, proj1.shape[1])
    p_in_ssm, p_glu = sequencer_exchange(SCATTER, [gw_in_ssm, gw_glu], "scatter_g2", 4)
    dh1 = mm_nt(dproj1, win_ssm3, f32, "dh1")
    dx1, dshift1, dscale1, dgpre1 = prenorm_bwd(dh1, x1, dx2, rows(ln_pre_g, 1), scale1, "prenorm1_bwd")
    dlr, dli, dldt, dbt_re, dbt_im = s5_params_bwd(
        lam_re[0], lam_im[0], ldt, bt_re, bt_im, da_re.reshape(n_grp, n_st), da_im.reshape(n_grp, n_st),
        _diag_blocks(dbre3, SSM_GROUP, n_st), _diag_blocks(dbim3, SSM_GROUP, n_st))
    g_b_re = jnp.transpose(dbt_re, (0, 2, 1))
    g_b_im = jnp.transpose(dbt_im, (0, 2, 1))
    g_c_re = jnp.transpose(_diag_blocks(dcre3, n_st, SSM_GROUP), (0, 2, 1))
    g_c_im = -jnp.transpose(_diag_blocks(dcimn3, n_st, SSM_GROUP), (0, 2, 1))

    dy0, dgate0, dgpost0 = post_bwd(dx1, y0, gate0, rows(ln_post_g, 0), "post0_bwd")
    dcat = mm_nt(dy0, wout_ab3, bf16, "dcat")
    gw_out_ab = mm_tn(cat, dy0, 1, bf16, "gw_out_ab").reshape(N_DEV, -1, d)
    (p_out_ab,) = sequencer_exchange(SCATTER, [gw_out_ab], "scatter_g3", 5)
    da, dsgu_w, dsgu_b, dsgu_ng = sgu_bwd(proj0, dcat, sgu_norm_g, sgu_w[0], sgu_b3)
    dq, dk, dv, dbz = sb_bwd(proj0, dcat, att, tot, nh)
    dproj0 = jnp.concatenate([da, dq, dk, dv, dbz], axis=1)
    gw_in_ab = mm_nn(h0_t, dproj0[None], bf16, "gw_in_ab", split_cols=N_DEV)
    (p_in_ab,) = sequencer_exchange(SCATTER, [gw_in_ab], "scatter_g4", 6)
    dh0 = mm_nt(dproj0, win_ab3, f32, "dh0")
    dx0, dshift0, dscale0, dgpre0 = prenorm_bwd(dh0, x0, dx1, rows(ln_pre_g, 0), scale0, "prenorm0_bwd")

    small_names = ["ln_pre_g", "ln_post_g", "b_mod", "sgu_norm_g", "sgu_w", "sgu_b", "lam_re", "lam_im", "b_re", "b_im",
                   "c_re", "c_im", "log_dt"]
    small_w = [ln_pre_g, ln_post_g, b_mod, sgu_norm_g, sgu_w, sgu_b, lam_re, lam_im, b_re, b_im, c_re, c_im, log_dt]
    small_m = [m_ln_pre_g, m_ln_post_g, m_b_mod, m_sgu_norm_g, m_sgu_w, m_sgu_b, m_lam_re, m_lam_im, m_b_re, m_b_im,
               m_c_re, m_c_im, m_log_dt]
    small_v = [v_ln_pre_g, v_ln_post_g, v_b_mod, v_sgu_norm_g, v_sgu_w, v_sgu_b, v_lam_re, v_lam_im, v_b_re, v_b_im,
               v_c_re, v_c_im, v_log_dt]
    def sharded(p, w, m, v, name):
        shp = w.shape
        w2, m2, v2 = (a.reshape(-1, shp[-1]) for a in (w, m, v))
        return [o.reshape(shp) for o in adam_reduce(p.reshape(p.shape[0], -1, shp[-1]), w2, m2, v2, name)]

    r_w_out_ssm = sharded(p_out_ssm, w_out_ssm, m_w_out_ssm, v_w_out_ssm, "adam_w_out_ssm")
    r_w_in_ssm = sharded(p_in_ssm, w_in_ssm, m_w_in_ssm, v_w_in_ssm, "adam_w_in_ssm")
    r_w_glu = sharded(p_glu, w_glu, m_w_glu, v_w_glu, "adam_w_glu")
    r_w_out_ab = sharded(p_out_ab, w_out_ab, m_w_out_ab, v_w_out_ab, "adam_w_out_ab")
    dmod = jnp.concatenate([dshift0, dscale0, dgate0, dshift1, dscale1, dgate1], axis=1)
    for done in (r_w_out_ssm, r_w_in_ssm, r_w_glu, r_w_out_ab):
        dmod = after(dmod, done[0])
    small_g = [jnp.concatenate([dgpre0, dgpre1]), jnp.concatenate([dgpost0, dgpost1]), dmod, dsgu_ng, dsgu_w, dsgu_b,
               dlr, dli, g_b_re, g_b_im, g_c_re, g_c_im, dldt]
    shapes = [w.shape for w in small_w]
    g_sum, dmod_all = all_reduce_rows(_pack(small_g + [dd_skip, db_glu, loss_tile], SUBLANES * N_DEV), dmod,
                                      "reduce_small_grads")
    n_rows_small = sum(-(-math.prod(s) // PACK_ROW) * SUBLANES for s in shapes)
    loss = g_sum[n_rows_small + 2 * (d_skip_all.shape[1] // HEAD), 0] * (0.5 / d)
    new_small = adam_small(g_sum, _pack(small_w), _pack(small_m), _pack(small_v))
    r_small = [_unpack(o, shapes) for o in [g_sum[:n_rows_small]] + list(new_small)]
    small = {n: [r_small[k][i] for k in range(4)] for i, n in enumerate(small_names)}
    vec_rows = d_skip_all.shape[1] // HEAD

    def my_columns(r0):
        whole = g_sum[r0:r0 + vec_rows].reshape(1, 1, -1)
        return lax.dynamic_slice_in_dim(whole, me * d_skip.shape[1], d_skip.shape[1], axis=2)

    r_d_skip = sharded(my_columns(n_rows_small), d_skip, m_d_skip, v_d_skip, "adam_d_skip")
    r_b_glu = sharded(my_columns(n_rows_small + vec_rows), b_glu, m_b_glu, v_b_glu, "adam_b_glu")
    r_w_in_ab = sharded(p_in_ab, w_in_ab, m_w_in_ab, v_w_in_ab, "adam_w_in_ab")

    dm_cols = jnp.transpose(
        lax.dynamic_slice_in_dim(dmod_all.reshape(N_DEV, 2, 3 * d), me * mod_cols, mod_cols, axis=2), (1, 0, 2))
    cond_t = jnp.transpose(silu_rows(c_all))
    r_w_mod = adam_w_mod(cond_t, dm_cols, w_mod, m_w_mod, v_w_mod)

    res = dict(small)
    res.update(w_mod=r_w_mod, w_in_ab=r_w_in_ab, w_out_ab=r_w_out_ab, w_in_ssm=r_w_in_ssm, w_out_ssm=r_w_out_ssm,
               d_skip=r_d_skip, w_glu=r_w_glu, b_glu=r_b_glu)
    order = ["ln_pre_g", "ln_post_g", "w_mod", "b_mod", "w_in_ab", "w_out_ab", "sgu_norm_g", "sgu_w", "sgu_b", "w_in_ssm",
             "w_out_ssm", "lam_re", "lam_im", "b_re", "b_im", "c_re", "c_im", "d_skip", "log_dt", "w_glu", "b_glu"]
    outs = [loss, dx0.reshape(x.shape)]
    for k in range(4):
        outs += [res[n][k] for n in order]
    return tuple(outs)
```

```python
import functools
import math

import jax
import jax.numpy as jnp
from jax import lax
from jax.experimental import pallas as pl
from jax.experimental.pallas import tpu as pltpu
from jax.experimental.pallas import tpu_sc as plsc

f32 = jnp.float32
bf16 = jnp.bfloat16

N_DEV = 8
EPS = 1e-6
HEAD = 128
SUBLANES = 8
SSM_GROUP = 16
SSM_STATE = 64
GROUPS_PER_LANE_BLOCK = HEAD // SSM_GROUP
STATES_PER_LANE_BLOCK = GROUPS_PER_LANE_BLOCK * SSM_STATE
VMEM_LIMIT = 56 * 2 ** 20
ADAM_LR, ADAM_B1, ADAM_B2, ADAM_EPS, ADAM_WD, ADAM_STEP = 0.001, 0.9, 0.999, 1e-08, 0.01, 10
_GELU_C0 = math.sqrt(2.0 / math.pi)
_GELU_C1 = 0.044715
MESH = pl.DeviceIdType.MESH


def _cparams(*sem):
    return pltpu.CompilerParams(dimension_semantics=sem if sem else None, vmem_limit_bytes=VMEM_LIMIT)


def _gelu(x):
    return 0.5 * x * (1.0 + jnp.tanh(_GELU_C0 * (x + _GELU_C1 * x * x * x)))


def _gelu_grad(x):
    t = jnp.tanh(_GELU_C0 * (x + _GELU_C1 * x * x * x))
    return 0.5 * (1.0 + t) + 0.5 * x * (1.0 - t * t) * _GELU_C0 * (1.0 + 3.0 * _GELU_C1 * x * x)


def _silu(x):
    return x * jax.nn.sigmoid(x)


def _silu_grad(x):
    s = jax.nn.sigmoid(x)
    return s * (1.0 + x * (1.0 - s))


def _dot(a, b):
    return jnp.dot(a, b, preferred_element_type=f32)


def _dot_nt(a, b):
    return lax.dot_general(a, b, (((1,), (1,)), ((), ())), preferred_element_type=f32)


def _dot_tn(a, b):
    return lax.dot_general(a, b, (((0,), (0,)), ((), ())), preferred_element_type=f32)


def _split_bf16(v):
    hi = v.astype(bf16)
    lo = (v - hi.astype(f32)).astype(bf16)
    return hi, lo


def _row(d):
    return pl.BlockSpec((1, d), lambda *_: (0, 0))


def _my_index():
    return 4 * lax.axis_index("x") + 2 * lax.axis_index("y") + lax.axis_index("c")


def _peer(k):
    x, y, c = lax.axis_index("x"), lax.axis_index("y"), lax.axis_index("c")
    return (1 - x if k & 4 else x, 1 - y if k & 2 else y, 1 - c if k & 1 else c)


def all_gather(arrs, name):
    n = len(arrs)

    def body(*refs):
        ins, outs = refs[:n], refs[n:2 * n]
        send, recv, local = refs[2 * n:]
        me = _my_index()
        copies = []
        for a in range(n):
            cp = pltpu.make_async_copy(ins[a], outs[a].at[me], local.at[a])
            cp.start()
            copies.append(cp)
            for k in range(1, N_DEV):
                s = a * (N_DEV - 1) + k - 1
                cp = pltpu.make_async_remote_copy(src_ref=ins[a], dst_ref=outs[a].at[me], send_sem=send.at[s],
                                                  recv_sem=recv.at[s], device_id=_peer(k), device_id_type=MESH)
                cp.start()
                copies.append(cp)
        for cp in copies:
            cp.wait()

    any_spec = pl.BlockSpec(memory_space=pl.ANY)
    outs = pl.pallas_call(
        body, name=name,
        out_shape=[jax.ShapeDtypeStruct((N_DEV,) + a.shape, a.dtype) for a in arrs],
        in_specs=[any_spec] * n, out_specs=[any_spec] * n,
        scratch_shapes=[pltpu.SemaphoreType.DMA((n * (N_DEV - 1),)), pltpu.SemaphoreType.DMA((n * (N_DEV - 1),)),
                        pltpu.SemaphoreType.DMA((n,))],
        compiler_params=pltpu.CompilerParams(has_side_effects=True),
    )(*arrs)
    return list(outs)


def all_reduce_rows(pack, extra, name):
    r, c = pack.shape
    rs = r // N_DEV
    n_peer = N_DEV - 1

    def body(p_ref, x_ref, o_ref, xo_ref, land, red, send1, recv1, send2, recv2, sendx, recvx, local):
        me = _my_index()

        def rows(i):
            return pl.ds(pl.multiple_of(i * rs, SUBLANES), rs)

        own = [pltpu.make_async_copy(p_ref.at[rows(me)], land.at[me], local.at[0]),
               pltpu.make_async_copy(x_ref, xo_ref.at[me], local.at[1])]
        first = []
        for k in range(1, N_DEV):
            first.append(pltpu.make_async_remote_copy(
                src_ref=p_ref.at[rows(jnp.bitwise_xor(me, k))], dst_ref=land.at[me], send_sem=send1.at[k - 1],
                recv_sem=recv1.at[k - 1], device_id=_peer(k), device_id_type=MESH))
            first.append(pltpu.make_async_remote_copy(
                src_ref=x_ref, dst_ref=xo_ref.at[me], send_sem=sendx.at[k - 1], recv_sem=recvx.at[k - 1],
                device_id=_peer(k), device_id_type=MESH))
        for cp in own + first:
            cp.start()
        for cp in own + first:
            cp.wait()
        acc = land[0]
        for s in range(1, N_DEV):
            acc = acc + land[s]
        red[...] = acc
        mine = pltpu.make_async_copy(red, o_ref.at[rows(me)], local.at[2])
        second = [pltpu.make_async_remote_copy(
            src_ref=red, dst_ref=o_ref.at[rows(me)], send_sem=send2.at[k - 1], recv_sem=recv2.at[k - 1],
            device_id=_peer(k), device_id_type=MESH) for k in range(1, N_DEV)]
        for cp in [mine] + second:
            cp.start()
        for cp in [mine] + second:
            cp.wait()

    any_spec = pl.BlockSpec(memory_space=pl.ANY)
    return pl.pallas_call(
        body, name=name,
        out_shape=[jax.ShapeDtypeStruct((r, c), pack.dtype), jax.ShapeDtypeStruct((N_DEV,) + extra.shape, extra.dtype)],
        in_specs=[any_spec, any_spec], out_specs=[any_spec, any_spec],
        scratch_shapes=[pltpu.VMEM((N_DEV, rs, c), pack.dtype), pltpu.VMEM((rs, c), pack.dtype)]
        + [pltpu.SemaphoreType.DMA((n_peer,))] * 6 + [pltpu.SemaphoreType.DMA((3,))],
        compiler_params=pltpu.CompilerParams(has_side_effects=True),
    )(pack, extra)


GATHER, SCATTER = "gather", "scatter"


def _exchange_copies(srcs, lands, send, recv):
    me = _my_index()
    copies = []
    for a, (src, land) in enumerate(zip(srcs, lands)):
        for k in range(1, N_DEV):
            s = a * (N_DEV - 1) + k - 1
            copies.append(pltpu.make_async_remote_copy(
                src_ref=src.at[jnp.bitwise_xor(me, k)], dst_ref=land.at[me],
                send_sem=send.at[s], recv_sem=recv.at[s], device_id=_peer(k), device_id_type=MESH))
    return copies


def sequencer_exchange(kind, arrs, name, collective_id):
    n = len(arrs)
    n_sem = n * (N_DEV - 1)
    land_shapes = [((N_DEV,) + a.shape if kind == GATHER else a.shape) for a in arrs]
    srcs = [jax.new_ref(a, memory_space=pltpu.MemorySpace.HBM) for a in arrs]
    lands = [jax.empty_ref(jax.ShapeDtypeStruct(s, a.dtype), memory_space=pltpu.MemorySpace.HBM)
             for s, a in zip(land_shapes, arrs)]

    @pl.kernel(mesh=plsc.ScalarSubcoreMesh(axis_name="sequencer", num_cores=1), name=name,
               scratch_types=(pltpu.SemaphoreType.DMA((n_sem,)), pltpu.SemaphoreType.DMA((n_sem,)),
                              pltpu.SemaphoreType.DMA((n,))),
               compiler_params=pltpu.CompilerParams(collective_id=collective_id))
    def launch(send, recv, local):
        barrier = pltpu.get_barrier_semaphore()
        for k in range(1, N_DEV):
            pl.semaphore_signal(barrier, inc=1, device_id=_peer(k), device_id_type=MESH)
        pl.semaphore_wait(barrier, N_DEV - 1)
        me = _my_index()
        mine = [pltpu.make_async_copy(src if kind == GATHER else src.at[me], land.at[me], local.at[a])
                for a, (src, land) in enumerate(zip(srcs, lands))]
        if kind == SCATTER:
            copies = mine + _exchange_copies(srcs, lands, send, recv)
            for cp in copies:
                cp.start()
            for cp in copies:
                cp.wait()
            return

        def block_copy(a, slot, block, k, src=None):
            s = a * (N_DEV - 1) + slot
            return pltpu.make_async_remote_copy(
                src_ref=lands[a].at[block] if src is None else src, dst_ref=lands[a].at[block],
                send_sem=send.at[s], recv_sem=recv.at[s], device_id=_peer(k), device_id_type=MESH)

        chips = (2, 4, 6)
        sibling = jnp.bitwise_xor(me, 1)
        first = [block_copy(a, slot, me, k, src=srcs[a]) for a in range(n) for slot, k in enumerate((1,) + chips)]
        for cp in mine + first:
            cp.start()
        passed = []
        for a in range(n):
            for i, k in enumerate(chips):
                block = jnp.bitwise_xor(me, k)
                block_copy(a, 1 + i, block, k).wait_recv()
                passed.append(block_copy(a, 4 + i, block, 1))
                passed[-1].start()
        for a in range(n):
            block_copy(a, 0, sibling, 1).wait_recv()
            for i, k in enumerate(chips):
                block_copy(a, 4 + i, jnp.bitwise_xor(sibling, k), 1).wait_recv()
        for cp in mine:
            cp.wait()
        for cp in first + passed:
            cp.wait_send()

    launch()
    return [land[...] for land in lands]


def _tile(n, pref):
    for t in pref:
        if n % t == 0:
            return t
    return n


MM_WIDE = 1024
MM_WEIGHT_BLOCK = 8 * 2 ** 20


def _blocks_per_step(nb, fits):
    return max(g for g in range(1, nb + 1) if nb % g == 0 and fits(g))


def mm_nn(a, b3, out_dtype, name, split_cols=None):
    m, k = a.shape
    nb, _, bn = b3.shape
    tm = _tile(m, (512, 256, 128))
    tn = bn // split_cols if split_cols else _tile(bn, (1024, 896, 512, 256, 128))
    per = bn // tn
    gb = _blocks_per_step(nb, lambda g: g == 1 or (per == 1 and g * bn <= MM_WIDE))

    def body(a_ref, b_ref, o_ref):
        for g in range(gb):
            o_ref[:, g * tn:(g + 1) * tn] = _dot(a_ref[...], b_ref[g]).astype(o_ref.dtype)

    if split_cols:
        out_spec = pl.BlockSpec((None, tm, tn), lambda i, j, jj: (jj, i, 0))
        out_shape = jax.ShapeDtypeStruct((split_cols, m, tn), out_dtype)
    else:
        out_spec = pl.BlockSpec((tm, gb * tn), lambda i, j, jj: (i, j * per + jj))
        out_shape = jax.ShapeDtypeStruct((m, nb * bn), out_dtype)
    return pl.pallas_call(
        body, name=name, grid=(m // tm, nb // gb, per),
        in_specs=[pl.BlockSpec((tm, k), lambda i, j, jj: (i, 0)),
                  pl.BlockSpec((gb, k, tn), lambda i, j, jj: (j, 0, jj))],
        out_specs=out_spec, out_shape=out_shape,
        compiler_params=_cparams("parallel", "arbitrary", "arbitrary"),
    )(a, b3)


def mm_nt(a, w3, out_dtype, name):
    m, _ = a.shape
    nb, ko, bn = w3.shape
    tm = _tile(m, (512, 256, 128))
    tko = _tile(ko, (1024, 512, 256, 128))
    gb = _blocks_per_step(nb, lambda g: g * tko * bn * w3.dtype.itemsize <= MM_WEIGHT_BLOCK)
    ns = nb // gb

    def body(a_ref, w_ref, o_ref, acc_ref):
        j = pl.program_id(2)

        @pl.when(j == 0)
        def _():
            acc_ref[...] = jnp.zeros_like(acc_ref)

        part = _dot_nt(a_ref[:, :bn], w_ref[0])
        for g in range(1, gb):
            part += _dot_nt(a_ref[:, g * bn:(g + 1) * bn], w_ref[g])
        acc_ref[...] += part

        @pl.when(j == ns - 1)
        def _():
            o_ref[...] = acc_ref[...].astype(o_ref.dtype)

    return pl.pallas_call(
        body, name=name, grid=(m // tm, ko // tko, ns),
        in_specs=[pl.BlockSpec((tm, gb * bn), lambda i, o, j: (i, j)),
                  pl.BlockSpec((gb, tko, bn), lambda i, o, j: (j, o, 0))],
        out_specs=pl.BlockSpec((tm, tko), lambda i, o, j: (i, o)),
        out_shape=jax.ShapeDtypeStruct((m, ko), out_dtype),
        scratch_shapes=[pltpu.VMEM((tm, tko), f32)],
        compiler_params=_cparams("parallel", "arbitrary", "arbitrary"),
    )(a, w3)


def mm_tn(a, dy, ncb, out_dtype, name):
    l, ka = a.shape
    _, n = dy.shape
    bn = n // ncb
    tl = _tile(l, (1024, 512, 256, 128))
    tka = _tile(ka, (512, 256, 128))
    tn = _tile(bn, (1024, 896, 512, 256, 128))
    per = bn // tn
    gb = _blocks_per_step(ncb, lambda g: g == 1 or (per == 1 and g * bn <= MM_WIDE))
    nl = l // tl

    def body(a_ref, dy_ref, o_ref, acc_ref):
        s = pl.program_id(2)

        @pl.when(s == 0)
        def _():
            acc_ref[...] = jnp.zeros_like(acc_ref)

        acc_ref[...] += _dot_tn(a_ref[...], dy_ref[...])

        @pl.when(s == nl - 1)
        def _():
            for g in range(gb):
                o_ref[g] = acc_ref[:, g * tn:(g + 1) * tn].astype(o_ref.dtype)

    return pl.pallas_call(
        body, name=name, grid=(ka // tka, n // (gb * tn), nl),
        in_specs=[pl.BlockSpec((tl, tka), lambda i, j, s: (s, i)),
                  pl.BlockSpec((tl, gb * tn), lambda i, j, s: (s, j))],
        out_specs=pl.BlockSpec((gb, tka, tn), lambda i, j, s: (j // per, i, j % per)),
        out_shape=jax.ShapeDtypeStruct((ncb, ka, bn), out_dtype),
        scratch_shapes=[pltpu.VMEM((tka, gb * tn), f32)],
        compiler_params=_cparams("parallel", "parallel", "arbitrary"),
    )(a, dy)


def mod_part(c_all, w_mod, b_cols):
    nl, d, cols = w_mod.shape

    def body(c_ref, w_ref, b_ref, o_ref):
        cond = _silu(c_ref[...]).astype(bf16)
        o_ref[...] = _dot(cond, w_ref[...].astype(bf16)) + b_ref[...]

    return pl.pallas_call(
        body, name="mod_part", grid=(nl,),
        in_specs=[pl.BlockSpec((N_DEV, d), lambda l: (0, 0)),
                  pl.BlockSpec((None, d, cols), lambda l: (l, 0, 0)),
                  pl.BlockSpec((None, 1, cols), lambda l: (l, 0, 0))],
        out_specs=pl.BlockSpec((None, N_DEV, cols), lambda l: (l, 0, 0)),
        out_shape=jax.ShapeDtypeStruct((nl, N_DEV, cols), f32),
        compiler_params=_cparams("arbitrary"),
    )(c_all, w_mod, b_cols.reshape(nl, 1, cols))


def _row_tile(l):
    return _tile(l, (512, 256, 128))


def _entry_rows(xv, g_ref, sh_ref, sc_ref, h_ref, ht_ref):
    r = lax.rsqrt(jnp.mean(xv * xv, axis=-1, keepdims=True) + EPS)
    h = xv * r * (g_ref[...] * (1.0 + sc_ref[...])) + sh_ref[...]
    h_ref[...] = h.astype(h_ref.dtype)
    ht_ref[...] = jnp.transpose(h).astype(ht_ref.dtype)


def prenorm_fwd(x, g, shift, scale, name):
    l, d = x.shape
    tm = _row_tile(l)

    def body(x_ref, g_ref, sh_ref, sc_ref, h_ref, ht_ref):
        _entry_rows(x_ref[...], g_ref, sh_ref, sc_ref, h_ref, ht_ref)

    return pl.pallas_call(
        body, name=name, grid=(l // tm,),
        in_specs=[pl.BlockSpec((tm, d), lambda i: (i, 0)), _row(d), _row(d), _row(d)],
        out_specs=[pl.BlockSpec((tm, d), lambda i: (i, 0)), pl.BlockSpec((d, tm), lambda i: (0, i))],
        out_shape=[jax.ShapeDtypeStruct((l, d), bf16), jax.ShapeDtypeStruct((d, l), bf16)],
        compiler_params=_cparams("parallel"),
    )(x, g, shift, scale)


def post_prenorm_fwd(x, y, gate, g_post, g_pre, shift, scale, name):
    l, d = x.shape
    tm = _row_tile(l)

    def body(x_ref, y_ref, gate_ref, gp_ref, g_ref, sh_ref, sc_ref, o_ref, h_ref, ht_ref):
        yv = y_ref[...]
        r = lax.rsqrt(jnp.mean(yv * yv, axis=-1, keepdims=True) + EPS)
        xv = x_ref[...] + gate_ref[...] * (yv * r * gp_ref[...])
        o_ref[...] = xv
        _entry_rows(xv, g_ref, sh_ref, sc_ref, h_ref, ht_ref)

    blk = pl.BlockSpec((tm, d), lambda i: (i, 0))
    return pl.pallas_call(
        body, name=name, grid=(l // tm,),
        in_specs=[blk, blk] + [_row(d)] * 5, out_specs=[blk, blk, pl.BlockSpec((d, tm), lambda i: (0, i))],
        out_shape=[jax.ShapeDtypeStruct((l, d), f32), jax.ShapeDtypeStruct((l, d), bf16),
                   jax.ShapeDtypeStruct((d, l), bf16)],
        compiler_params=_cparams("parallel"),
    )(x, y, gate, g_post, g_pre, shift, scale)


def _post_bwd_rows(dxv, yv, r, gate, gv, dy_ref, dgate_ref, dg_ref):
    yn = yv * r
    dgate_ref[...] += jnp.sum(dxv * yn * gv, axis=0, keepdims=True)
    dyg = dxv * gate
    dg_ref[...] += jnp.sum(dyg * yn, axis=0, keepdims=True)
    dyn = dyg * gv
    dy_ref[...] = (r * (dyn - yn * jnp.mean(dyn * yn, axis=-1, keepdims=True))).astype(dy_ref.dtype)


def final_loss(x, y, gate, g, target):
    l, d = x.shape
    tm = _row_tile(l)

    def body(x_ref, y_ref, gate_ref, g_ref, t_ref, dx_ref, loss_ref, dy_ref, dgate_ref, dg_ref):
        @pl.when(pl.program_id(0) == 0)
        def _():
            loss_ref[...] = jnp.zeros_like(loss_ref)
            dgate_ref[...] = jnp.zeros_like(dgate_ref)
            dg_ref[...] = jnp.zeros_like(dg_ref)

        yv, gate, gv = y_ref[...], gate_ref[...], g_ref[...]
        r = lax.rsqrt(jnp.mean(yv * yv, axis=-1, keepdims=True) + EPS)
        diff = x_ref[...] + gate * (yv * r * gv) - t_ref[...]
        dxv = diff * (1.0 / d)
        dx_ref[...] = dxv
        loss_ref[...] += jnp.sum(diff * diff)
        _post_bwd_rows(dxv, yv, r, gate, gv, dy_ref, dgate_ref, dg_ref)

    blk = pl.BlockSpec((tm, d), lambda i: (i, 0))
    return pl.pallas_call(
        body, name="final_loss", grid=(l // tm,),
        in_specs=[blk, blk, _row(d), _row(d), blk],
        out_specs=[blk, pl.BlockSpec((SUBLANES, HEAD), lambda i: (0, 0)), blk, _row(d), _row(d)],
        out_shape=[jax.ShapeDtypeStruct((l, d), f32), jax.ShapeDtypeStruct((SUBLANES, HEAD), f32),
                   jax.ShapeDtypeStruct((l, d), bf16), jax.ShapeDtypeStruct((1, d), f32), jax.ShapeDtypeStruct((1, d), f32)],
        compiler_params=_cparams("arbitrary"),
    )(x, y, gate, g, target)


def post_bwd(dx, y, gate, g, name):
    l, d = dx.shape
    tm = _row_tile(l)

    def body(dx_ref, y_ref, gate_ref, g_ref, dy_ref, dgate_ref, dg_ref):
        @pl.when(pl.program_id(0) == 0)
        def _():
            dgate_ref[...] = jnp.zeros_like(dgate_ref)
            dg_ref[...] = jnp.zeros_like(dg_ref)

        yv = y_ref[...]
        r = lax.rsqrt(jnp.mean(yv * yv, axis=-1, keepdims=True) + EPS)
        _post_bwd_rows(dx_ref[...], yv, r, gate_ref[...], g_ref[...], dy_ref, dgate_ref, dg_ref)

    blk = pl.BlockSpec((tm, d), lambda i: (i, 0))
    return pl.pallas_call(
        body, name=name, grid=(l // tm,),
        in_specs=[blk, blk, _row(d), _row(d)], out_specs=[blk, _row(d), _row(d)],
        out_shape=[jax.ShapeDtypeStruct((l, d), bf16), jax.ShapeDtypeStruct((1, d), f32),
                   jax.ShapeDtypeStruct((1, d), f32)],
        compiler_params=_cparams("arbitrary"),
    )(dx, y, gate, g)


def prenorm_bwd(dh, x, dx_next, g, scale, name):
    l, d = x.shape
    tm = _row_tile(l)

    def body(dh_ref, x_ref, dxn_ref, g_ref, sc_ref, dx_ref, dsh_ref, dsc_ref, dg_ref):
        @pl.when(pl.program_id(0) == 0)
        def _():
            dsh_ref[...] = jnp.zeros_like(dsh_ref)
            dsc_ref[...] = jnp.zeros_like(dsc_ref)
            dg_ref[...] = jnp.zeros_like(dg_ref)

        xv, dhv, gv, sc1 = x_ref[...], dh_ref[...], g_ref[...], 1.0 + sc_ref[...]
        r = lax.rsqrt(jnp.mean(xv * xv, axis=-1, keepdims=True) + EPS)
        xn = xv * r
        dhx = dhv * xn
        dsh_ref[...] += jnp.sum(dhv, axis=0, keepdims=True)
        dsc_ref[...] += jnp.sum(dhx * gv, axis=0, keepdims=True)
        dg_ref[...] += jnp.sum(dhx * sc1, axis=0, keepdims=True)
        dxn = dhv * (gv * sc1)
        dx_ref[...] = dxn_ref[...] + r * (dxn - xn * jnp.mean(dxn * xn, axis=-1, keepdims=True))

    blk = pl.BlockSpec((tm, d), lambda i: (i, 0))
    return pl.pallas_call(
        body, name=name, grid=(l // tm,),
        in_specs=[blk, blk, blk, _row(d), _row(d)], out_specs=[blk, _row(d), _row(d), _row(d)],
        out_shape=[jax.ShapeDtypeStruct((l, d), f32)] + [jax.ShapeDtypeStruct((1, d), f32)] * 3,
        compiler_params=_cparams("arbitrary"),
    )(dh, x, dx_next, g, scale)


def _tril_mask():
    r = lax.broadcasted_iota(jnp.int32, (HEAD, HEAD), 0)
    c = lax.broadcasted_iota(jnp.int32, (HEAD, HEAD), 1)
    return r >= c


def sgu_fwd(proj, norm_g, w_s, b_s):
    l = proj.shape[0]
    nh = w_s.shape[0]
    wa = nh * HEAD

    def body(au_ref, av_ref, az_ref, ng_ref, w_ref, b_ref, o_ref):
        tril = _tril_mask()
        for h in range(nh):
            sl = slice(h * HEAD, (h + 1) * HEAD)
            gv = _gelu(av_ref[:, sl].astype(f32))
            r = lax.rsqrt(jnp.mean(gv * gv, axis=-1, keepdims=True) + EPS)
            vh = gv * r * ng_ref[:, sl]
            wm = jnp.where(tril, w_ref[h], 0.0).astype(bf16)
            s = _dot(wm, vh.astype(bf16)) + b_ref[h]
            o_ref[:, sl] = (_gelu(au_ref[:, sl].astype(f32)) * s * _silu(az_ref[:, sl].astype(f32))).astype(o_ref.dtype)

    def col(j):
        return pl.BlockSpec((HEAD, wa), lambda n: (n, j))

    return pl.pallas_call(
        body, name="sgu_fwd", grid=(l // HEAD,),
        in_specs=[col(0), col(1), col(2), _row(wa),
                  pl.BlockSpec((nh, HEAD, HEAD), lambda n: (0, 0, 0)), pl.BlockSpec((nh, HEAD, 1), lambda n: (0, 0, 0))],
        out_specs=pl.BlockSpec((HEAD, wa), lambda n: (n, 0)),
        out_shape=jax.ShapeDtypeStruct((l, 2 * wa), bf16),
        compiler_params=_cparams("parallel"),
    )(proj, proj, proj, norm_g, w_s, b_s)


def sgu_bwd(proj, dcat, norm_g, w_s, b_s):
    l = proj.shape[0]
    nh = w_s.shape[0]
    wa = nh * HEAD

    def body(au_ref, av_ref, az_ref, do_ref, ng_ref, w_ref, b_ref, da_ref, dw_ref, db_ref, dng_ref):
        @pl.when(pl.program_id(0) == 0)
        def _():
            dw_ref[...] = jnp.zeros_like(dw_ref)
            db_ref[...] = jnp.zeros_like(db_ref)
            dng_ref[...] = jnp.zeros_like(dng_ref)

        tril = _tril_mask()
        for h in range(nh):
            sl = slice(h * HEAD, (h + 1) * HEAD)
            au, av, az = au_ref[:, sl].astype(f32), av_ref[:, sl].astype(f32), az_ref[:, sl].astype(f32)
            ng = ng_ref[:, sl]
            gv = _gelu(av)
            r = lax.rsqrt(jnp.mean(gv * gv, axis=-1, keepdims=True) + EPS)
            gvn = gv * r
            vh = (gvn * ng).astype(bf16)
            wm = jnp.where(tril, w_ref[h], 0.0).astype(bf16)
            s = _dot(wm, vh) + b_ref[h]
            gu, sz = _gelu(au), _silu(az)
            dov = do_ref[:, sl].astype(f32)
            da_ref[:, sl] = (dov * s * sz * _gelu_grad(au)).astype(da_ref.dtype)
            da_ref[:, 2 * wa + h * HEAD:2 * wa + (h + 1) * HEAD] = (dov * gu * s * _silu_grad(az)).astype(da_ref.dtype)
            ds = dov * gu * sz
            db_ref[h] += jnp.sum(ds, axis=-1, keepdims=True)
            dsb = ds.astype(bf16)
            dw_ref[h] += jnp.where(tril, _dot_nt(dsb, vh), 0.0)
            dvh = _dot_tn(wm, dsb)
            dng_ref[:, sl] += jnp.sum(dvh * gvn, axis=0, keepdims=True)
            dgvn = dvh * ng
            dgv = r * (dgvn - gvn * jnp.mean(dgvn * gvn, axis=-1, keepdims=True))
            da_ref[:, wa + h * HEAD:wa + (h + 1) * HEAD] = (dgv * _gelu_grad(av)).astype(da_ref.dtype)

    def col(j):
        return pl.BlockSpec((HEAD, wa), lambda n: (n, j))

    whole_w = pl.BlockSpec((nh, HEAD, HEAD), lambda n: (0, 0, 0))
    whole_b = pl.BlockSpec((nh, HEAD, 1), lambda n: (0, 0, 0))
    return pl.pallas_call(
        body, name="sgu_bwd", grid=(l // HEAD,),
        in_specs=[col(0), col(1), col(2), col(0), _row(wa), whole_w, whole_b],
        out_specs=[pl.BlockSpec((HEAD, 3 * wa), lambda n: (n, 0)), whole_w, whole_b, _row(wa)],
        out_shape=[jax.ShapeDtypeStruct(proj.shape, bf16), jax.ShapeDtypeStruct((nh, HEAD, HEAD), f32),
                   jax.ShapeDtypeStruct((nh, HEAD, 1), f32), jax.ShapeDtypeStruct((1, wa), f32)],
        compiler_params=_cparams("arbitrary"),
    )(proj, proj, proj, dcat, norm_g, w_s, b_s)


_LOG2E = 1.0 / math.log(2.0)


def _sb_scores(q, k, scale):
    z = _dot_nt(q, k) * (scale * _LOG2E)
    return z, jnp.maximum(z, 0.0) + jnp.log2(1.0 + jnp.exp2(-jnp.abs(z)))


SB_KEYS = 256


def _sb_sum_matrix(tri, kb):
    s = lax.broadcasted_iota(jnp.int32, (2 * kb, kb + HEAD), 0) % kb
    j = lax.broadcasted_iota(jnp.int32, (2 * kb, kb + HEAD), 1)
    return jnp.where(jnp.logical_or(j >= kb, tri(s, j)), 1.0, 0.0).astype(bf16)


def _sb_sums(x, sums):
    kb = x.shape[1]
    c2 = _dot(jnp.concatenate(_split_bf16(x), axis=1), sums)
    return c2[:, :kb], c2[:, kb:]


def _sb_wide(v, kb):
    return jnp.concatenate([v] * (kb // HEAD), axis=1) if kb > HEAD else v


def _sb_q_tile(l, most=512):
    return _tile(l, tuple(t for t in (1024, 512, 256, 128) if t <= most))


def _sb_band_levels(band):
    return _tile(band, (4, 2, 1))


def _sb_heads_per_step(nh, most):
    return _tile(nh, tuple(h for h in (4, 2) if h <= most))


def sb_fwd(proj, mixed, nh):
    l = proj.shape[0]
    wb = nh * HEAD
    tq = _sb_q_tile(l, 1024)
    kb = min(SB_KEYS, tq)
    band = tq // kb
    hp = _sb_heads_per_step(nh, 2)
    levels = _sb_band_levels(band)
    scale = 1.0 / math.sqrt(HEAD)
    qc, kc, vc, zc = 3 * nh, 4 * nh, 5 * nh, 6 * nh

    def body(q_ref, k_ref, v_ref, bz_ref, mixed_ref, o_ref, att_ref, tot_ref):
        del mixed_ref
        i = pl.program_id(1)
        sums = _sb_sum_matrix(lambda s, j: s > j, kb)
        t_pos = i * tq + lax.broadcasted_iota(jnp.int32, (tq, kb), 0)
        s_off = lax.broadcasted_iota(jnp.int32, (tq, kb), 1)

        def step(j, carry, masked, row0=0):
            rows = pl.ds(pl.multiple_of(j * kb, kb), kb)
            out = []
            for e in range(hp):
                acc, tot = carry[e]
                sl = slice(e * HEAD, (e + 1) * HEAD)
                z, sp = _sb_scores(q_ref[row0:, sl], k_ref[rows, sl], scale)
                lb = z - sp
                if masked:
                    mask = s_off[row0:] + j * kb < t_pos[row0:]
                    sp = jnp.where(mask, sp, 0.0)
                later, total = _sb_sums(sp, sums)
                w = jnp.exp2(lb + _sb_wide(tot[row0:], kb) - later)
                if masked:
                    w = jnp.where(mask, w, 0.0)
                new = (acc[row0:] + _dot(w.astype(bf16), v_ref[rows, sl]), tot[row0:] - total)
                out.append(tuple(jnp.concatenate([old[:row0], upd]) if row0 else upd for old, upd in zip(carry[e], new)))
            return tuple(out)

        zero = jnp.zeros((tq, HEAD), f32)
        carry = ((zero, zero),) * hp
        for lv in reversed(range(levels)):
            carry = lax.fori_loop(
                0, band // levels,
                lambda t, c, lv=lv: step(band * i + (lv + 1) * (band // levels) - 1 - t, c, True, lv * (tq // levels)), carry)
        carry = lax.fori_loop(0, band * i, lambda t, c: step(band * i - 1 - t, c, False), carry)
        for e in range(hp):
            acc, tot = carry[e]
            sl = slice(e * HEAD, (e + 1) * HEAD)
            att_ref[:, sl] = acc.astype(att_ref.dtype)
            o_ref[:, sl] = (acc * _silu(bz_ref[:, sl].astype(f32))).astype(o_ref.dtype)
            tot_ref[e] = tot[:, :1]

    blk = lambda c0: pl.BlockSpec((tq, hp * HEAD), lambda g, i: (i, c0 // hp + g))
    head = lambda c0: pl.BlockSpec((l, hp * HEAD), lambda g, i: (0, c0 // hp + g))
    return pl.pallas_call(
        body, name="sb_fwd", grid=(nh // hp, l // tq),
        in_specs=[blk(qc), head(kc), head(vc), blk(zc), pl.BlockSpec(memory_space=pl.ANY)],
        out_specs=[blk(mixed.shape[1] // HEAD - nh), blk(0), pl.BlockSpec((hp, tq, 1), lambda g, i: (g, i, 0))],
        out_shape=[jax.ShapeDtypeStruct(mixed.shape, bf16), jax.ShapeDtypeStruct((l, wb), bf16),
                   jax.ShapeDtypeStruct((nh, l, 1), f32)],
        input_output_aliases={4: 0},
        compiler_params=_cparams("parallel", "arbitrary"),
    )(proj, proj, proj, proj, mixed)


def sb_bwd(proj, dcat, att, tot, dproj, nh):
    l = proj.shape[0]
    wb = nh * HEAD
    tq = _sb_q_tile(l, 1024)
    kb = min(SB_KEYS, tq)
    band = tq // kb
    nq = l // tq
    hp = _sb_heads_per_step(nh, 2)
    levels = _sb_band_levels(band)
    scale = 1.0 / math.sqrt(HEAD)
    qc, kc, vc, zc = 3 * nh, 4 * nh, 5 * nh, 6 * nh

    def body(q_ref, k_ref, v_ref, bz_ref, do_ref, att_ref, tot_ref, dproj_in, dproj_ref, dk_acc, dv_acc, dob_ref,
             tile_ref, head_ref, sems):
        del dproj_in
        g, i = pl.program_id(0), pl.program_id(1)

        def put(src, row0, c0, k):
            cols = pl.ds(pl.multiple_of((c0 + g * hp) * HEAD, HEAD), hp * HEAD)
            cp = pltpu.make_async_copy(src, dproj_ref.at[pl.ds(row0, src.shape[0]), cols], sems.at[k])
            cp.start()
            return cp

        @pl.when(i == 0)
        def _():
            dk_acc[...] = jnp.zeros_like(dk_acc)
            dv_acc[...] = jnp.zeros_like(dv_acc)

        my_rows = pl.multiple_of(i * tq, tq)
        bz = bz_ref[...].astype(f32)
        dov = do_ref[...].astype(f32)
        tile_ref[0] = (dov * att_ref[...].astype(f32) * _silu_grad(bz)).astype(bf16)
        dbz_copy = put(tile_ref.at[0], my_rows, zc, 0)
        dob_ref[...] = (dov * _silu(bz)).astype(bf16)
        upto = _sb_sum_matrix(lambda s, j: s <= j, kb)
        before = _sb_sum_matrix(lambda j, s: j < s, kb)
        t_pos = i * tq + lax.broadcasted_iota(jnp.int32, (tq, kb), 0)
        s_off = lax.broadcasted_iota(jnp.int32, (tq, kb), 1)

        def step(j, carry, masked, row0=0):
            rows = pl.ds(pl.multiple_of(j * kb, kb), kb)
            out = []
            for h in range(hp):
                dq, sp_seen, e_seen = (c[row0:] for c in carry[h])
                sl = slice(h * HEAD, (h + 1) * HEAD)
                q, kj, vj, dob = q_ref[row0:, sl], k_ref[rows, sl], v_ref[rows, sl], dob_ref[row0:, sl]
                z, sp = _sb_scores(q, kj, scale)
                lb = z - sp
                if masked:
                    mask = s_off[row0:] + j * kb < t_pos[row0:]
                    sp = jnp.where(mask, sp, 0.0)
                sp_upto, sp_total = _sb_sums(sp, upto)
                w = jnp.exp2(lb + _sb_wide(sp_seen, kb) + sp_upto)
                if masked:
                    w = jnp.where(mask, w, 0.0)
                dv_acc[rows, sl] += _dot_tn(w.astype(bf16), dob)
                e = _dot_nt(dob, vj) * w
                e_before, e_total = _sb_sums(e, before)
                dz = (e - (e + _sb_wide(e_seen, kb) + e_before) * jnp.exp2(lb)) * scale
                if masked:
                    dz = jnp.where(mask, dz, 0.0)
                dz = dz.astype(bf16)
                dk_acc[rows, sl] += _dot_tn(dz, q)
                new = (dq + _dot(dz, kj), sp_seen + sp_total, e_seen + e_total)
                out.append(tuple(jnp.concatenate([old[:row0], upd]) if row0 else upd for old, upd in zip(carry[h], new)))
            return tuple(out)

        zero = jnp.zeros((tq, HEAD), f32)
        init = tuple((zero, jnp.broadcast_to(tot_ref[h], (tq, HEAD)), zero) for h in range(hp))
        carry = lax.fori_loop(0, band * i, lambda j, c: step(j, c, False), init)
        for lv in range(levels):
            carry = lax.fori_loop(
                0, band // levels,
                lambda t, c, lv=lv: step(band * i + lv * (band // levels) + t, c, True, lv * (tq // levels)), carry)
        for h in range(hp):
            tile_ref[1, :, h * HEAD:(h + 1) * HEAD] = carry[h][0].astype(bf16)
        dq_copy = put(tile_ref.at[1], my_rows, qc, 1)
        dbz_copy.wait()
        dq_copy.wait()

        @pl.when(i == nq - 1)
        def _():
            head_ref[0] = dk_acc[...].astype(bf16)
            head_ref[1] = dv_acc[...].astype(bf16)
            copies = [put(head_ref.at[0], 0, kc, 2), put(head_ref.at[1], 0, vc, 3)]
            for cp in copies:
                cp.wait()

    blk = lambda c0: pl.BlockSpec((tq, hp * HEAD), lambda g, i: (i, c0 // hp + g))
    head = lambda c0: pl.BlockSpec((l, hp * HEAD), lambda g, i: (0, c0 // hp + g))
    any_spec = pl.BlockSpec(memory_space=pl.ANY)
    return pl.pallas_call(
        body, name="sb_bwd", grid=(nh // hp, nq),
        in_specs=[blk(qc), head(kc), head(vc), blk(zc), blk(nh), blk(0),
                  pl.BlockSpec((hp, tq, 1), lambda g, i: (g, i, 0)), any_spec],
        out_specs=any_spec, out_shape=jax.ShapeDtypeStruct(dproj.shape, bf16), input_output_aliases={7: 0},
        scratch_shapes=[pltpu.VMEM((l, hp * HEAD), f32), pltpu.VMEM((l, hp * HEAD), f32),
                        pltpu.VMEM((tq, hp * HEAD), bf16), pltpu.VMEM((2, tq, hp * HEAD), bf16),
                        pltpu.VMEM((2, l, hp * HEAD), bf16), pltpu.SemaphoreType.DMA((4,))],
        compiler_params=_cparams("parallel", "arbitrary"),
    )(proj, proj, proj, proj, dcat, att, tot, dproj)


def _disc(lr, li, ldt):
    dt = jnp.exp(ldt)
    mag = jnp.exp(lr * dt)
    a_re = mag * jnp.cos(li * dt)
    a_im = mag * jnp.sin(li * dt)
    den = lr * lr + li * li
    nr = a_re - 1.0
    return a_re, a_im, (nr * lr + a_im * li) / den, (a_im * lr - nr * li) / den


def s5_params_fwd(lr, li, ldt, bt_re, bt_im):
    g, c, p = bt_re.shape

    def body(lr_ref, li_ref, ldt_ref, br_ref, bi_ref, ar_ref, ai_ref, bbr_ref, bbi_ref):
        a_re, a_im, cr, ci = _disc(lr_ref[...], li_ref[...], ldt_ref[...])
        ar_ref[...] = a_re
        ai_ref[...] = a_im
        for k in range(c):
            br, bi = br_ref[:, k, :], bi_ref[:, k, :]
            bbr_ref[:, k, :] = cr * br - ci * bi
            bbi_ref[:, k, :] = cr * bi + ci * br

    return pl.pallas_call(
        body, name="s5_params_fwd",
        out_shape=[jax.ShapeDtypeStruct((g, p), f32)] * 2 + [jax.ShapeDtypeStruct((g, c, p), f32)] * 2,
    )(lr, li, ldt, bt_re, bt_im)


def s5_params_bwd(lr, li, ldt, bt_re, bt_im, da_re, da_im, dbbt_re, dbbt_im):
    g, c, p = bt_re.shape

    def body(lr_ref, li_ref, ldt_ref, br_ref, bi_ref, dar_ref, dai_ref, dbbr_ref, dbbi_ref,
             dlr_ref, dli_ref, dldt_ref, dbr_ref, dbi_ref):
        (a_re, a_im, cr, ci), vjp = jax.vjp(_disc, lr_ref[...], li_ref[...], ldt_ref[...])
        dcr = jnp.zeros((g, p), f32)
        dci = jnp.zeros((g, p), f32)
        for k in range(c):
            br, bi = br_ref[:, k, :], bi_ref[:, k, :]
            dr, di = dbbr_ref[:, k, :], dbbi_ref[:, k, :]
            dcr += dr * br + di * bi
            dci += di * br - dr * bi
            dbr_ref[:, k, :] = cr * dr + ci * di
            dbi_ref[:, k, :] = cr * di - ci * dr
        dlr, dli, dldt = vjp((dar_ref[...], dai_ref[...], dcr, dci))
        dlr_ref[...] = dlr
        dli_ref[...] = dli
        dldt_ref[...] = dldt

    return pl.pallas_call(
        body, name="s5_params_bwd",
        out_shape=[jax.ShapeDtypeStruct((g, p), f32)] * 2 + [jax.ShapeDtypeStruct((g, 1), f32)]
        + [jax.ShapeDtypeStruct((g, c, p), f32)] * 2,
    )(lr, li, ldt, bt_re, bt_im, da_re, da_im, dbbt_re, dbbt_im)


def _cmul(ar, ai, br, bi):
    return ar * br - ai * bi, ar * bi + ai * br


def _power_tables(ar, ai):
    rows = lax.broadcasted_iota(jnp.int32, (SUBLANES, ar.shape[1]), 0)
    pr = jnp.zeros((SUBLANES, ar.shape[1]), f32)
    pi = jnp.zeros((SUBLANES, ar.shape[1]), f32)
    cr, ci = ar, ai
    pows = {}
    for r in range(SUBLANES):
        pows[r + 1] = (cr, ci)
        pr = jnp.where(rows == r, cr, pr)
        pi = jnp.where(rows == r, ci, pi)
        cr, ci = _cmul(cr, ci, ar, ai)
    return [pows[1], pows[2], pows[4]], pr, pi


def _ssm_time_tile(l):
    return _tile(l, (2048, 1024, 512, 256, 128))


def ssm_fwd(u, bre3, bim3, cre3, cimn3, a_re, a_im, d_skip):
    l, w = u.shape[0], d_skip.shape[1]
    nj = w // HEAD
    ns = STATES_PER_LANE_BLOCK
    tt = _ssm_time_tile(l)

    def body(u_ref, bre_ref, bim_ref, cre_ref, cim_ref, ar_ref, ai_ref, d_ref, y_ref, hr_ref, hi_ref, cr_ref, ci_ref):
        @pl.when(pl.program_id(1) == 0)
        def _():
            cr_ref[...] = jnp.zeros_like(cr_ref)
            ci_ref[...] = jnp.zeros_like(ci_ref)

        uv = u_ref[...]
        hr_ref[...] = _dot(uv, bre_ref[...])
        hi_ref[...] = _dot(uv, bim_ref[...])
        steps, pr, pi = _power_tables(ar_ref[...], ai_ref[...])
        rows = lax.broadcasted_iota(jnp.int32, (SUBLANES, ns), 0)
        steps = [(jnp.where(rows >= d, sr_, 0.0), jnp.where(rows >= d, si_, 0.0)) for d, (sr_, si_) in zip((1, 2, 4), steps)]

        def blk(b, carry):
            cr, ci = carry
            sl = pl.ds(pl.multiple_of(b * SUBLANES, SUBLANES), SUBLANES)
            xr, xi = hr_ref[sl, :], hi_ref[sl, :]
            for d, (sr_, si_) in zip((1, 2, 4), steps):
                mr, mi = _cmul(sr_, si_, pltpu.roll(xr, d, axis=0), pltpu.roll(xi, d, axis=0))
                xr, xi = xr + mr, xi + mi
            mr, mi = _cmul(pr, pi, cr, ci)
            xr, xi = xr + mr, xi + mi
            hr_ref[sl, :] = xr
            hi_ref[sl, :] = xi
            return xr[SUBLANES - 1:, :], xi[SUBLANES - 1:, :]

        cr, ci = lax.fori_loop(0, tt // SUBLANES, blk, (cr_ref[...], ci_ref[...]))
        cr_ref[...] = cr
        ci_ref[...] = ci
        y = _dot(hr_ref[...].astype(bf16), cre_ref[...]) + _dot(hi_ref[...].astype(bf16), cim_ref[...])
        y_ref[...] = y + d_ref[...] * uv.astype(f32)

    lane = pl.BlockSpec((tt, HEAD), lambda j, i: (i, j))
    st = pl.BlockSpec((tt, ns), lambda j, i: (i, j))
    b3 = pl.BlockSpec((None, HEAD, ns), lambda j, i: (j, 0, 0))
    c3 = pl.BlockSpec((None, ns, HEAD), lambda j, i: (j, 0, 0))
    arow = pl.BlockSpec((1, ns), lambda j, i: (0, j))
    return pl.pallas_call(
        body, name="ssm_fwd", grid=(nj, l // tt),
        in_specs=[lane, b3, b3, c3, c3, arow, arow, pl.BlockSpec((1, HEAD), lambda j, i: (0, j))],
        out_specs=[lane, st, st],
        out_shape=[jax.ShapeDtypeStruct((l, w), f32), jax.ShapeDtypeStruct((l, nj * ns), f32),
                   jax.ShapeDtypeStruct((l, nj * ns), f32)],
        scratch_shapes=[pltpu.VMEM((1, ns), f32), pltpu.VMEM((1, ns), f32)],
        compiler_params=_cparams("parallel", "arbitrary"),
    )(u, bre3, bim3, cre3, cimn3, a_re, a_im, d_skip)


def ssm_bwd(dy, u, dproj, h_re, h_im, bre3, bim3, cre3, cimn3, a_re, a_im, d_skip):
    l, w = u.shape[0], d_skip.shape[1]
    nj = w // HEAD
    ns = STATES_PER_LANE_BLOCK
    tt = _ssm_time_tile(l)
    nt = l // tt

    def body(dy_ref, u_ref, dproj_ref, hr_ref, hi_ref, bre_ref, bim_ref, cre_ref, cim_ref, ar_ref, ai_ref, d_ref,
             du_ref, dd_ref, dar_ref, dai_ref, dbre_ref, dbim_ref, dcre_ref, dcim_ref, kr_ref, ki_ref, cr_ref, ci_ref,
             accr_ref, acci_ref):
        del dproj_ref
        i = pl.program_id(1)

        @pl.when(i == 0)
        def _():
            for ref in (cr_ref, ci_ref, accr_ref, acci_ref, dd_ref, dbre_ref, dbim_ref, dcre_ref, dcim_ref):
                ref[...] = jnp.zeros_like(ref)

        dyv = dy_ref[...]
        dyb = dyv.astype(bf16)
        uv = u_ref[...]
        kr_ref[...] = _dot_nt(dyb, cre_ref[...])
        ki_ref[...] = _dot_nt(dyb, cim_ref[...])
        steps, pr, pi = _power_tables(ar_ref[...], -ai_ref[...])
        rows = lax.broadcasted_iota(jnp.int32, (SUBLANES, ns), 0)
        qr = jnp.zeros((SUBLANES, ns), f32)
        qi = jnp.zeros((SUBLANES, ns), f32)
        for r in range(SUBLANES):
            qr = jnp.where(rows == r, pr[SUBLANES - 1 - r:SUBLANES - r, :], qr)
            qi = jnp.where(rows == r, pi[SUBLANES - 1 - r:SUBLANES - r, :], qi)
        nb = tt // SUBLANES
        steps = [(jnp.where(rows < SUBLANES - d, sr_, 0.0), jnp.where(rows < SUBLANES - d, si_, 0.0))
                 for d, (sr_, si_) in zip((1, 2, 4), steps)]

        def blk(t, carry):
            cr, ci, accr, acci = carry
            sl = pl.ds(pl.multiple_of((nb - 1 - t) * SUBLANES, SUBLANES), SUBLANES)
            xr, xi = kr_ref[sl, :], ki_ref[sl, :]
            for d, (sr_, si_) in zip((1, 2, 4), steps):
                mr, mi = _cmul(sr_, si_, pltpu.roll(xr, SUBLANES - d, axis=0), pltpu.roll(xi, SUBLANES - d, axis=0))
                xr, xi = xr + mr, xi + mi
            mr, mi = _cmul(qr, qi, cr, ci)
            xr, xi = xr + mr, xi + mi
            kr_ref[sl, :] = xr
            ki_ref[sl, :] = xi
            last = rows == SUBLANES - 1
            nr = jnp.where(last, cr, pltpu.roll(xr, SUBLANES - 1, axis=0))
            ni = jnp.where(last, ci, pltpu.roll(xi, SUBLANES - 1, axis=0))
            hr, hi = hr_ref[sl, :], hi_ref[sl, :]
            accr = accr + nr * hr + ni * hi
            acci = acci + ni * hr - nr * hi
            return xr[:1, :], xi[:1, :], accr, acci

        cr, ci, accr, acci = lax.fori_loop(0, nb, blk, (cr_ref[...], ci_ref[...], accr_ref[...], acci_ref[...]))
        cr_ref[...] = cr
        ci_ref[...] = ci
        accr_ref[...] = accr
        acci_ref[...] = acci
        kr, ki = kr_ref[...].astype(bf16), ki_ref[...].astype(bf16)
        du = _dot_nt(kr, bre_ref[...]) + _dot_nt(ki, bim_ref[...]) + d_ref[...] * dyv
        du_ref[...] = du.astype(du_ref.dtype)
        dd_ref[...] += jnp.sum(dyv * uv.astype(f32), axis=0, keepdims=True)
        dbre_ref[...] += _dot_tn(uv, kr)
        dbim_ref[...] += _dot_tn(uv, ki)
        dcre_ref[...] += _dot_tn(hr_ref[...].astype(bf16), dyb)
        dcim_ref[...] += _dot_tn(hi_ref[...].astype(bf16), dyb)

        @pl.when(i == nt - 1)
        def _():
            dar_ref[...] = jnp.sum(accr_ref[...], axis=0, keepdims=True)
            dai_ref[...] = jnp.sum(acci_ref[...], axis=0, keepdims=True)

    lane = pl.BlockSpec((tt, HEAD), lambda j, i: (nt - 1 - i, j))
    st = pl.BlockSpec((tt, ns), lambda j, i: (nt - 1 - i, j))
    b3 = pl.BlockSpec((None, HEAD, ns), lambda j, i: (j, 0, 0))
    c3 = pl.BlockSpec((None, ns, HEAD), lambda j, i: (j, 0, 0))
    arow = pl.BlockSpec((1, ns), lambda j, i: (0, j))
    drow = pl.BlockSpec((1, HEAD), lambda j, i: (0, j))
    return pl.pallas_call(
        body, name="ssm_bwd", grid=(nj, nt),
        in_specs=[lane, lane, pl.BlockSpec(memory_space=pl.ANY), st, st, b3, b3, c3, c3, arow, arow, drow],
        out_specs=[lane, drow, arow, arow, b3, b3, c3, c3], input_output_aliases={2: 0},
        out_shape=[jax.ShapeDtypeStruct(dproj.shape, bf16), jax.ShapeDtypeStruct((1, w), f32),
                   jax.ShapeDtypeStruct((1, nj * ns), f32), jax.ShapeDtypeStruct((1, nj * ns), f32),
                   jax.ShapeDtypeStruct((nj, HEAD, ns), f32), jax.ShapeDtypeStruct((nj, HEAD, ns), f32),
                   jax.ShapeDtypeStruct((nj, ns, HEAD), f32), jax.ShapeDtypeStruct((nj, ns, HEAD), f32)],
        scratch_shapes=[pltpu.VMEM((tt, ns), f32), pltpu.VMEM((tt, ns), f32), pltpu.VMEM((1, ns), f32),
                        pltpu.VMEM((1, ns), f32), pltpu.VMEM((SUBLANES, ns), f32), pltpu.VMEM((SUBLANES, ns), f32)],
        compiler_params=_cparams("parallel", "arbitrary"),
    )(dy, u, dproj, h_re, h_im, bre3, bim3, cre3, cimn3, a_re, a_im, d_skip)


def glu_fwd(y, z_src, w_glu, b_glu):
    l, w = y.shape
    tm = _row_tile(l)

    def body(y_ref, z_ref, w_ref, b_ref, g_ref, t_ref, o_ref):
        g = _gelu(y_ref[...])
        gb = g.astype(bf16)
        t = _dot(gb, w_ref[...]) + b_ref[...]
        g_ref[...] = gb
        t_ref[...] = t
        o_ref[...] = (g * jax.nn.sigmoid(t) * _silu(z_ref[...].astype(f32))).astype(o_ref.dtype)

    blk = pl.BlockSpec((tm, w), lambda i: (i, 0))
    return pl.pallas_call(
        body, name="glu_fwd", grid=(l // tm,),
        in_specs=[blk, pl.BlockSpec((tm, w), lambda i: (i, 1)), pl.BlockSpec((w, w), lambda i: (0, 0)), _row(w)],
        out_specs=[blk, blk, blk],
        out_shape=[jax.ShapeDtypeStruct((l, w), bf16), jax.ShapeDtypeStruct((l, w), f32),
                   jax.ShapeDtypeStruct((l, w), bf16)],
        compiler_params=_cparams("parallel"),
    )(y, z_src, w_glu, b_glu)


def glu_bwd(dout, y, t, z_src, w_glu):
    l, w = y.shape
    tm = _row_tile(l)

    def body(do_ref, y_ref, t_ref, z_ref, w_ref, dy_ref, dz_ref, dt_ref, db_ref):
        @pl.when(pl.program_id(0) == 0)
        def _():
            db_ref[...] = jnp.zeros_like(db_ref)

        yv, zv, dov = y_ref[...], z_ref[...].astype(f32), do_ref[...]
        g = _gelu(yv)
        sg = jax.nn.sigmoid(t_ref[...])
        dy2 = dov * _silu(zv)
        dz_ref[...] = (dov * g * sg * _silu_grad(zv)).astype(dz_ref.dtype)
        dt = dy2 * g * sg * (1.0 - sg)
        dtb = dt.astype(bf16)
        dt_ref[...] = dtb
        db_ref[...] += jnp.sum(dt, axis=0, keepdims=True)
        dg = dy2 * sg + _dot_nt(dtb, w_ref[...])
        dy_ref[...] = dg * _gelu_grad(yv)

    blk = pl.BlockSpec((tm, w), lambda i: (i, 0))
    return pl.pallas_call(
        body, name="glu_bwd", grid=(l // tm,),
        in_specs=[blk, blk, blk, pl.BlockSpec((tm, w), lambda i: (i, 1)), pl.BlockSpec((w, w), lambda i: (0, 0))],
        out_specs=[blk, pl.BlockSpec((tm, w), lambda i: (i, 1)), blk, _row(w)],
        out_shape=[jax.ShapeDtypeStruct((l, w), f32), jax.ShapeDtypeStruct((l, 2 * w), bf16),
                   jax.ShapeDtypeStruct((l, w), bf16), jax.ShapeDtypeStruct((1, w), f32)],
        compiler_params=_cparams("arbitrary"),
    )(dout, y, t, z_src, w_glu)


def _adamw(w, g, m, v):
    m = ADAM_B1 * m + (1.0 - ADAM_B1) * g
    v = ADAM_B2 * v + (1.0 - ADAM_B2) * (g * g)
    m_hat = m / (1.0 - ADAM_B1 ** ADAM_STEP)
    v_hat = v / (1.0 - ADAM_B2 ** ADAM_STEP)
    return -ADAM_LR * (m_hat / (jnp.sqrt(v_hat) + ADAM_EPS) + ADAM_WD * w), m, v


def adam_reduce(pieces, w, m, v, name):
    r, c = w.shape
    n = pieces.shape[0]
    tr = _tile(r, (256, 128, 64, 32, 16, 8))

    def body(p_ref, w_ref, m_ref, v_ref, g_ref, d_ref, nm_ref, nv_ref):
        g = p_ref[0].astype(f32)
        for s in range(1, n):
            g = g + p_ref[s].astype(f32)
        g_ref[...] = g
        d_ref[...], nm_ref[...], nv_ref[...] = _adamw(w_ref[...], g, m_ref[...], v_ref[...])

    blk = pl.BlockSpec((tr, c), lambda i: (i, 0))
    return pl.pallas_call(
        body, name=name, grid=(r // tr,),
        in_specs=[pl.BlockSpec((n, tr, c), lambda i: (0, i, 0)), blk, blk, blk],
        out_specs=[blk] * 4, out_shape=[jax.ShapeDtypeStruct((r, c), f32)] * 4,
        compiler_params=_cparams("parallel"),
    )(pieces, w, m, v)


def adam_w_mod(cond_t, dm, w, m, v):
    nl, d, cols = w.shape
    tr = _tile(d, (512, 256, 128))

    def body(c_ref, dm_ref, w_ref, m_ref, v_ref, g_ref, d_ref, nm_ref, nv_ref):
        g = jnp.dot(c_ref[...], dm_ref[...], preferred_element_type=f32, precision=lax.Precision.HIGHEST)
        g_ref[...] = g
        d_ref[...], nm_ref[...], nv_ref[...] = _adamw(w_ref[...], g, m_ref[...], v_ref[...])

    blk = pl.BlockSpec((None, tr, cols), lambda l, i: (l, i, 0))
    return pl.pallas_call(
        body, name="adam_w_mod", grid=(nl, d // tr),
        in_specs=[pl.BlockSpec((tr, N_DEV), lambda l, i: (i, 0)), pl.BlockSpec((None, N_DEV, cols), lambda l, i: (l, 0, 0)),
                  blk, blk, blk],
        out_specs=[blk] * 4, out_shape=[jax.ShapeDtypeStruct((nl, d, cols), f32)] * 4,
        compiler_params=_cparams("parallel", "parallel"),
    )(cond_t, dm, w, m, v)


def silu_rows(c_all):
    def body(c_ref, o_ref):
        o_ref[...] = _silu(c_ref[...])

    return pl.pallas_call(body, name="silu_rows", out_shape=jax.ShapeDtypeStruct(c_all.shape, f32))(c_all)


def _block_diag(x):
    g, a, b = x.shape
    nj = g // GROUPS_PER_LANE_BLOCK
    eye = jnp.eye(GROUPS_PER_LANE_BLOCK, dtype=x.dtype)
    x5 = x.reshape(nj, GROUPS_PER_LANE_BLOCK, a, b)
    return jnp.einsum("jgab,gh->jgahb", x5, eye).reshape(nj, GROUPS_PER_LANE_BLOCK * a, GROUPS_PER_LANE_BLOCK * b)


def _diag_blocks(x, a, b):
    nj = x.shape[0]
    x5 = x.reshape(nj, GROUPS_PER_LANE_BLOCK, a, GROUPS_PER_LANE_BLOCK, b)
    eye = jnp.eye(GROUPS_PER_LANE_BLOCK, dtype=x.dtype)
    return jnp.einsum("jgahb,gh->jgab", x5, eye).reshape(nj * GROUPS_PER_LANE_BLOCK, a, b)


PACK_ROW = SUBLANES * HEAD


def _pack(parts, row_multiple=SUBLANES):
    rows = []
    for p in parts:
        flat = p.reshape(-1)
        pad = (-flat.shape[0]) % PACK_ROW
        if pad:
            flat = jnp.concatenate([flat, jnp.zeros((pad,), flat.dtype)])
        rows.append(flat.reshape(-1, HEAD))
    pad = (-sum(r.shape[0] for r in rows)) % row_multiple
    if pad:
        rows.append(jnp.zeros((pad, HEAD), rows[0].dtype))
    return jnp.concatenate(rows, axis=0)


def _unpack(packed, shapes):
    out, r0 = [], 0
    for shp in shapes:
        n = math.prod(shp)
        nr = -(-n // PACK_ROW) * SUBLANES
        out.append(packed[r0:r0 + nr].reshape(-1)[:n].reshape(shp))
        r0 += nr
    return out


def adam_small(g, w, m, v):
    r, c = w.shape

    def body(g_ref, w_ref, m_ref, v_ref, d_ref, nm_ref, nv_ref):
        d_ref[...], nm_ref[...], nv_ref[...] = _adamw(w_ref[...], g_ref[...], m_ref[...], v_ref[...])

    tr = max(t for t in range(SUBLANES, 1024 + 1, SUBLANES) if r % t == 0)
    blk = pl.BlockSpec((tr, c), lambda i: (i, 0))
    return pl.pallas_call(
        body, name="adam_small", grid=(r // tr,),
        in_specs=[blk] * 4, out_specs=[blk] * 3, out_shape=[jax.ShapeDtypeStruct((r, c), f32)] * 3,
        compiler_params=_cparams("parallel"),
    )(g, w, m, v)


def kernel(x, c, ln_pre_g, ln_post_g, w_mod, b_mod, w_in_ab, w_out_ab, sgu_norm_g, sgu_w, sgu_b, w_in_ssm, w_out_ssm, lam_re, lam_im, b_re, b_im, c_re, c_im, d_skip, log_dt, w_glu, b_glu, loss_target, m_ln_pre_g, m_ln_post_g, m_w_mod, m_b_mod, m_w_in_ab, m_w_out_ab, m_sgu_norm_g, m_sgu_w, m_sgu_b, m_w_in_ssm, m_w_out_ssm, m_lam_re, m_lam_im, m_b_re, m_b_im, m_c_re, m_c_im, m_d_skip, m_log_dt, m_w_glu, m_b_glu, v_ln_pre_g, v_ln_post_g, v_w_mod, v_b_mod, v_w_in_ab, v_w_out_ab, v_sgu_norm_g, v_sgu_w, v_sgu_b, v_w_in_ssm, v_w_out_ssm, v_lam_re, v_lam_im, v_b_re, v_b_im, v_c_re, v_c_im, v_d_skip, v_log_dt, v_w_glu, v_b_glu):
    me = _my_index()
    x0 = x[0]
    l, d = x0.shape
    target = loss_target[0]
    nh = sgu_w.shape[1]
    wa = nh * HEAD
    n_grp, n_st = lam_re.shape[1], lam_re.shape[2]
    mod_cols = w_mod.shape[2]

    def after(a, first):
        return a + jnp.minimum(jnp.abs(first[(0,) * first.ndim].astype(f32)), 0.0).astype(a.dtype)

    c_all, d_skip_all, b_glu_all = all_gather([c, d_skip, b_glu], "gather_c")
    c_all = c_all.reshape(N_DEV, d)
    d_skip_all = d_skip_all.reshape(1, -1)
    b_glu_all = b_glu_all.reshape(1, -1)

    b_cols = lax.dynamic_slice_in_dim(b_mod, me * mod_cols, mod_cols, axis=1)
    (mod_all,) = all_gather([mod_part(c_all, w_mod, b_cols)], "gather_mod")
    (win_ab3,) = sequencer_exchange(GATHER, [after(w_in_ab[0], mod_all).astype(bf16)], "gather_w_in", 1)
    mod_mine = lax.dynamic_index_in_dim(mod_all, me, axis=2, keepdims=False)
    mod_rows = jnp.transpose(mod_mine, (1, 0, 2)).reshape(2, 3, 1, d)

    def rows(a, i):
        return a[i].reshape(1, d)

    shift0, scale0, gate0 = mod_rows[0, 0], mod_rows[0, 1], mod_rows[0, 2]
    h0, h0_t = prenorm_fwd(x0, rows(ln_pre_g, 0), shift0, scale0, "prenorm0")
    wout_ab3, win_ssm3, wout_ssm3, wglu = sequencer_exchange(
        GATHER, [after(w, win_ab3).astype(bf16) for w in (w_out_ab[0], w_in_ssm[0], w_out_ssm[0], w_glu[0])],
        "gather_w_rest", 2)
    proj0 = mm_nn(h0, win_ab3, bf16, "proj0")
    sgu_b3 = sgu_b[0].reshape(nh, HEAD, 1)
    cat, att, tot = sb_fwd(proj0, sgu_fwd(proj0, sgu_norm_g, sgu_w[0], sgu_b3), nh)
    wout_ab3 = wout_ab3.reshape(1, d, d)
    win_ssm3 = win_ssm3.reshape(1, d, d)
    wglu = wglu.reshape(w_glu.shape[2], w_glu.shape[2])
    y0 = mm_nn(cat, wout_ab3, f32, "out0")

    shift1, scale1, gate1 = mod_rows[1, 0], mod_rows[1, 1], mod_rows[1, 2]
    x1, h1, h1_t = post_prenorm_fwd(x0, y0, gate0, rows(ln_post_g, 0), rows(ln_pre_g, 1), shift1, scale1,
                                    "post0_prenorm1")
    proj1 = mm_nn(h1, win_ssm3, bf16, "proj1")
    w_ssm = proj1.shape[1] // 2
    ldt = log_dt[0].reshape(n_grp, 1)
    bt_re = jnp.transpose(b_re[0], (0, 2, 1))
    bt_im = jnp.transpose(b_im[0], (0, 2, 1))
    a_re, a_im, bbt_re, bbt_im = s5_params_fwd(lam_re[0], lam_im[0], ldt, bt_re, bt_im)
    bre3 = _block_diag(bbt_re).astype(bf16)
    bim3 = _block_diag(bbt_im).astype(bf16)
    cre3 = _block_diag(jnp.transpose(c_re[0], (0, 2, 1))).astype(bf16)
    cimn3 = _block_diag(-jnp.transpose(c_im[0], (0, 2, 1))).astype(bf16)
    a_re_row, a_im_row = a_re.reshape(1, -1), a_im.reshape(1, -1)
    y_ssm, hs_re, hs_im = ssm_fwd(proj1, bre3, bim3, cre3, cimn3, a_re_row, a_im_row, d_skip_all)
    g_act, t_glu, mix1 = glu_fwd(y_ssm, proj1, wglu, b_glu_all)
    y1 = mm_nn(mix1, wout_ssm3, f32, "out1")

    dx2, loss_tile, dy1, dgate1, dgpost1 = final_loss(x1, y1, gate1, rows(ln_post_g, 1), target)

    dmix1 = mm_nt(dy1, wout_ssm3, f32, "dmix1")
    gw_out_ssm = mm_tn(mix1, dy1, N_DEV, bf16, "gw_out_ssm")
    (p_out_ssm,) = sequencer_exchange(SCATTER, [gw_out_ssm], "scatter_g1", 3)
    dy_ssm, dproj1, dt_glu, db_glu = glu_bwd(dmix1, y_ssm, t_glu, proj1, wglu)
    gw_glu = mm_tn(g_act, dt_glu, 1, bf16, "gw_glu").reshape(N_DEV, -1, w_ssm)
    dproj1, dd_skip, da_re, da_im, dbre3, dbim3, dcre3, dcimn3 = ssm_bwd(
        dy_ssm, proj1, dproj1, hs_re, hs_im, bre3, bim3, cre3, cimn3, a_re_row, a_im_row, d_skip_all)
    gw_in_ssm = mm_nn(h1_t, dproj1[None], bf16, "gw_in_ssm").reshape(N_DEV, -1, proj1.shape[1])
    p_in_ssm, p_glu = sequencer_exchange(SCATTER, [gw_in_ssm, gw_glu], "scatter_g2", 4)
    dh1 = mm_nt(dproj1, win_ssm3, f32, "dh1")
    dx1, dshift1, dscale1, dgpre1 = prenorm_bwd(dh1, x1, dx2, rows(ln_pre_g, 1), scale1, "prenorm1_bwd")
    dlr, dli, dldt, dbt_re, dbt_im = s5_params_bwd(
        lam_re[0], lam_im[0], ldt, bt_re, bt_im, da_re.reshape(n_grp, n_st), da_im.reshape(n_grp, n_st),
        _diag_blocks(dbre3, SSM_GROUP, n_st), _diag_blocks(dbim3, SSM_GROUP, n_st))
    g_b_re = jnp.transpose(dbt_re, (0, 2, 1))
    g_b_im = jnp.transpose(dbt_im, (0, 2, 1))
    g_c_re = jnp.transpose(_diag_blocks(dcre3, n_st, SSM_GROUP), (0, 2, 1))
    g_c_im = -jnp.transpose(_diag_blocks(dcimn3, n_st, SSM_GROUP), (0, 2, 1))

    dy0, dgate0, dgpost0 = post_bwd(dx1, y0, gate0, rows(ln_post_g, 0), "post0_bwd")
    dcat = mm_nt(dy0, wout_ab3, f32, "dcat")
    gw_out_ab = mm_tn(cat, dy0, 1, bf16, "gw_out_ab").reshape(N_DEV, -1, d)
    (p_out_ab,) = sequencer_exchange(SCATTER, [gw_out_ab], "scatter_g3", 5)
    dproj0, dsgu_w, dsgu_b, dsgu_ng = sgu_bwd(proj0, dcat, sgu_norm_g, sgu_w[0], sgu_b3)
    dproj0 = sb_bwd(proj0, dcat, att, tot, dproj0, nh)
    gw_in_ab = mm_nn(h0_t, dproj0[None], bf16, "gw_in_ab", split_cols=N_DEV)
    (p_in_ab,) = sequencer_exchange(SCATTER, [gw_in_ab], "scatter_g4", 6)
    dh0 = mm_nt(dproj0, win_ab3, f32, "dh0")
    dx0, dshift0, dscale0, dgpre0 = prenorm_bwd(dh0, x0, dx1, rows(ln_pre_g, 0), scale0, "prenorm0_bwd")

    small_names = ["ln_pre_g", "ln_post_g", "b_mod", "sgu_norm_g", "sgu_w", "sgu_b", "lam_re", "lam_im", "b_re", "b_im",
                   "c_re", "c_im", "log_dt"]
    small_w = [ln_pre_g, ln_post_g, b_mod, sgu_norm_g, sgu_w, sgu_b, lam_re, lam_im, b_re, b_im, c_re, c_im, log_dt]
    small_m = [m_ln_pre_g, m_ln_post_g, m_b_mod, m_sgu_norm_g, m_sgu_w, m_sgu_b, m_lam_re, m_lam_im, m_b_re, m_b_im,
               m_c_re, m_c_im, m_log_dt]
    small_v = [v_ln_pre_g, v_ln_post_g, v_b_mod, v_sgu_norm_g, v_sgu_w, v_sgu_b, v_lam_re, v_lam_im, v_b_re, v_b_im,
               v_c_re, v_c_im, v_log_dt]
    def sharded(p, w, m, v, name):
        shp = w.shape
        w2, m2, v2 = (a.reshape(-1, shp[-1]) for a in (w, m, v))
        return [o.reshape(shp) for o in adam_reduce(p.reshape(p.shape[0], -1, shp[-1]), w2, m2, v2, name)]

    r_w_out_ssm = sharded(p_out_ssm, w_out_ssm, m_w_out_ssm, v_w_out_ssm, "adam_w_out_ssm")
    r_w_in_ssm = sharded(p_in_ssm, w_in_ssm, m_w_in_ssm, v_w_in_ssm, "adam_w_in_ssm")
    r_w_glu = sharded(p_glu, w_glu, m_w_glu, v_w_glu, "adam_w_glu")
    r_w_out_ab = sharded(p_out_ab, w_out_ab, m_w_out_ab, v_w_out_ab, "adam_w_out_ab")
    dmod = jnp.concatenate([dshift0, dscale0, dgate0, dshift1, dscale1, dgate1], axis=1)
    for done in (r_w_out_ssm, r_w_in_ssm, r_w_glu, r_w_out_ab):
        dmod = after(dmod, done[0])
    small_g = [jnp.concatenate([dgpre0, dgpre1]), jnp.concatenate([dgpost0, dgpost1]), dmod, dsgu_ng, dsgu_w, dsgu_b,
               dlr, dli, g_b_re, g_b_im, g_c_re, g_c_im, dldt]
    shapes = [w.shape for w in small_w]
    g_sum, dmod_all = all_reduce_rows(_pack(small_g + [dd_skip, db_glu, loss_tile], SUBLANES * N_DEV), dmod,
                                      "reduce_small_grads")
    n_rows_small = sum(-(-math.prod(s) // PACK_ROW) * SUBLANES for s in shapes)
    loss = g_sum[n_rows_small + 2 * (d_skip_all.shape[1] // HEAD), 0] * (0.5 / d)
    new_small = adam_small(g_sum, _pack(small_w), _pack(small_m), _pack(small_v))
    r_small = [_unpack(o, shapes) for o in [g_sum[:n_rows_small]] + list(new_small)]
    small = {n: [r_small[k][i] for k in range(4)] for i, n in enumerate(small_names)}
    vec_rows = d_skip_all.shape[1] // HEAD

    def my_columns(r0):
        whole = g_sum[r0:r0 + vec_rows].reshape(1, 1, -1)
        return lax.dynamic_slice_in_dim(whole, me * d_skip.shape[1], d_skip.shape[1], axis=2)

    r_d_skip = sharded(my_columns(n_rows_small), d_skip, m_d_skip, v_d_skip, "adam_d_skip")
    r_b_glu = sharded(my_columns(n_rows_small + vec_rows), b_glu, m_b_glu, v_b_glu, "adam_b_glu")
    r_w_in_ab = sharded(p_in_ab, w_in_ab, m_w_in_ab, v_w_in_ab, "adam_w_in_ab")

    dm_cols = jnp.transpose(
        lax.dynamic_slice_in_dim(dmod_all.reshape(N_DEV, 2, 3 * d), me * mod_cols, mod_cols, axis=2), (1, 0, 2))
    cond_t = jnp.transpose(silu_rows(c_all))
    r_w_mod = adam_w_mod(cond_t, dm_cols, w_mod, m_w_mod, v_w_mod)

    res = dict(small)
    res.update(w_mod=r_w_mod, w_in_ab=r_w_in_ab, w_out_ab=r_w_out_ab, w_in_ssm=r_w_in_ssm, w_out_ssm=r_w_out_ssm,
               d_skip=r_d_skip, w_glu=r_w_glu, b_glu=r_b_glu)
    order = ["ln_pre_g", "ln_post_g", "w_mod", "b_mod", "w_in_ab", "w_out_ab", "sgu_norm_g", "sgu_w", "sgu_b", "w_in_ssm",
             "w_out_ssm", "lam_re", "lam_im", "b_re", "b_im", "c_re", "c_im", "d_skip", "log_dt", "w_glu", "b_glu"]
    outs = [loss, dx0.reshape(x.shape)]
    for k in range(4):
        outs += [res[n][k] for n in order]
    return tuple(outs)
```

```python
import functools
import math

import jax
import jax.numpy as jnp
from jax import lax
from jax.experimental import pallas as pl
from jax.experimental.pallas import tpu as pltpu
from jax.experimental.pallas import tpu_sc as plsc

f32 = jnp.float32
bf16 = jnp.bfloat16

N_DEV = 8
EPS = 1e-6
HEAD = 128
SUBLANES = 8
SSM_GROUP = 16
SSM_STATE = 64
GROUPS_PER_LANE_BLOCK = HEAD // SSM_GROUP
STATES_PER_LANE_BLOCK = GROUPS_PER_LANE_BLOCK * SSM_STATE
VMEM_LIMIT = 56 * 2 ** 20
ADAM_LR, ADAM_B1, ADAM_B2, ADAM_EPS, ADAM_WD, ADAM_STEP = 0.001, 0.9, 0.999, 1e-08, 0.01, 10
_GELU_C0 = math.sqrt(2.0 / math.pi)
_GELU_C1 = 0.044715
MESH = pl.DeviceIdType.MESH


def _cparams(*sem):
    return pltpu.CompilerParams(dimension_semantics=sem if sem else None, vmem_limit_bytes=VMEM_LIMIT)


def _gelu(x):
    return 0.5 * x * (1.0 + jnp.tanh(_GELU_C0 * (x + _GELU_C1 * x * x * x)))


def _gelu_grad(x):
    t = jnp.tanh(_GELU_C0 * (x + _GELU_C1 * x * x * x))
    return 0.5 * (1.0 + t) + 0.5 * x * (1.0 - t * t) * _GELU_C0 * (1.0 + 3.0 * _GELU_C1 * x * x)


def _silu(x):
    return x * jax.nn.sigmoid(x)


def _silu_grad(x):
    s = jax.nn.sigmoid(x)
    return s * (1.0 + x * (1.0 - s))


def _dot(a, b):
    return jnp.dot(a, b, preferred_element_type=f32)


def _dot_nt(a, b):
    return lax.dot_general(a, b, (((1,), (1,)), ((), ())), preferred_element_type=f32)


def _dot_tn(a, b):
    return lax.dot_general(a, b, (((0,), (0,)), ((), ())), preferred_element_type=f32)


def _split_bf16(v):
    hi = v.astype(bf16)
    lo = (v - hi.astype(f32)).astype(bf16)
    return hi, lo


def _row(d):
    return pl.BlockSpec((1, d), lambda *_: (0, 0))


def _my_index():
    return 4 * lax.axis_index("x") + 2 * lax.axis_index("y") + lax.axis_index("c")


def _peer(k):
    x, y, c = lax.axis_index("x"), lax.axis_index("y"), lax.axis_index("c")
    return (1 - x if k & 4 else x, 1 - y if k & 2 else y, 1 - c if k & 1 else c)


def all_gather(arrs, name):
    n = len(arrs)

    def body(*refs):
        ins, outs = refs[:n], refs[n:2 * n]
        send, recv, local = refs[2 * n:]
        me = _my_index()
        copies = []
        for a in range(n):
            cp = pltpu.make_async_copy(ins[a], outs[a].at[me], local.at[a])
            cp.start()
            copies.append(cp)
            for k in range(1, N_DEV):
                s = a * (N_DEV - 1) + k - 1
                cp = pltpu.make_async_remote_copy(src_ref=ins[a], dst_ref=outs[a].at[me], send_sem=send.at[s],
                                                  recv_sem=recv.at[s], device_id=_peer(k), device_id_type=MESH)
                cp.start()
                copies.append(cp)
        for cp in copies:
            cp.wait()

    any_spec = pl.BlockSpec(memory_space=pl.ANY)
    outs = pl.pallas_call(
        body, name=name,
        out_shape=[jax.ShapeDtypeStruct((N_DEV,) + a.shape, a.dtype) for a in arrs],
        in_specs=[any_spec] * n, out_specs=[any_spec] * n,
        scratch_shapes=[pltpu.SemaphoreType.DMA((n * (N_DEV - 1),)), pltpu.SemaphoreType.DMA((n * (N_DEV - 1),)),
                        pltpu.SemaphoreType.DMA((n,))],
        compiler_params=pltpu.CompilerParams(has_side_effects=True),
    )(*arrs)
    return list(outs)


def all_reduce_rows(pack, extra, name):
    r, c = pack.shape
    rs = r // N_DEV
    n_peer = N_DEV - 1

    def body(p_ref, x_ref, o_ref, xo_ref, land, red, send1, recv1, send2, recv2, sendx, recvx, local):
        me = _my_index()

        def rows(i):
            return pl.ds(pl.multiple_of(i * rs, SUBLANES), rs)

        own = [pltpu.make_async_copy(p_ref.at[rows(me)], land.at[me], local.at[0]),
               pltpu.make_async_copy(x_ref, xo_ref.at[me], local.at[1])]
        first = []
        for k in range(1, N_DEV):
            first.append(pltpu.make_async_remote_copy(
                src_ref=p_ref.at[rows(jnp.bitwise_xor(me, k))], dst_ref=land.at[me], send_sem=send1.at[k - 1],
                recv_sem=recv1.at[k - 1], device_id=_peer(k), device_id_type=MESH))
            first.append(pltpu.make_async_remote_copy(
                src_ref=x_ref, dst_ref=xo_ref.at[me], send_sem=sendx.at[k - 1], recv_sem=recvx.at[k - 1],
                device_id=_peer(k), device_id_type=MESH))
        for cp in own + first:
            cp.start()
        for cp in own + first:
            cp.wait()
        acc = land[0]
        for s in range(1, N_DEV):
            acc = acc + land[s]
        red[...] = acc
        mine = pltpu.make_async_copy(red, o_ref.at[rows(me)], local.at[2])
        second = [pltpu.make_async_remote_copy(
            src_ref=red, dst_ref=o_ref.at[rows(me)], send_sem=send2.at[k - 1], recv_sem=recv2.at[k - 1],
            device_id=_peer(k), device_id_type=MESH) for k in range(1, N_DEV)]
        for cp in [mine] + second:
            cp.start()
        for cp in [mine] + second:
            cp.wait()

    any_spec = pl.BlockSpec(memory_space=pl.ANY)
    return pl.pallas_call(
        body, name=name,
        out_shape=[jax.ShapeDtypeStruct((r, c), pack.dtype), jax.ShapeDtypeStruct((N_DEV,) + extra.shape, extra.dtype)],
        in_specs=[any_spec, any_spec], out_specs=[any_spec, any_spec],
        scratch_shapes=[pltpu.VMEM((N_DEV, rs, c), pack.dtype), pltpu.VMEM((rs, c), pack.dtype)]
        + [pltpu.SemaphoreType.DMA((n_peer,))] * 6 + [pltpu.SemaphoreType.DMA((3,))],
        compiler_params=pltpu.CompilerParams(has_side_effects=True),
    )(pack, extra)


GATHER, SCATTER = "gather", "scatter"


def _exchange_copies(srcs, lands, send, recv):
    me = _my_index()
    copies = []
    for a, (src, land) in enumerate(zip(srcs, lands)):
        for k in range(1, N_DEV):
            s = a * (N_DEV - 1) + k - 1
            copies.append(pltpu.make_async_remote_copy(
                src_ref=src.at[jnp.bitwise_xor(me, k)], dst_ref=land.at[me],
                send_sem=send.at[s], recv_sem=recv.at[s], device_id=_peer(k), device_id_type=MESH))
    return copies


def sequencer_exchange(kind, arrs, name, collective_id):
    n = len(arrs)
    n_sem = n * (N_DEV - 1)
    land_shapes = [((N_DEV,) + a.shape if kind == GATHER else a.shape) for a in arrs]
    srcs = [jax.new_ref(a, memory_space=pltpu.MemorySpace.HBM) for a in arrs]
    lands = [jax.empty_ref(jax.ShapeDtypeStruct(s, a.dtype), memory_space=pltpu.MemorySpace.HBM)
             for s, a in zip(land_shapes, arrs)]

    @pl.kernel(mesh=plsc.ScalarSubcoreMesh(axis_name="sequencer", num_cores=1), name=name,
               scratch_types=(pltpu.SemaphoreType.DMA((n_sem,)), pltpu.SemaphoreType.DMA((n_sem,)),
                              pltpu.SemaphoreType.DMA((n,))),
               compiler_params=pltpu.CompilerParams(collective_id=collective_id))
    def launch(send, recv, local):
        barrier = pltpu.get_barrier_semaphore()
        for k in range(1, N_DEV):
            pl.semaphore_signal(barrier, inc=1, device_id=_peer(k), device_id_type=MESH)
        pl.semaphore_wait(barrier, N_DEV - 1)
        me = _my_index()
        mine = [pltpu.make_async_copy(src if kind == GATHER else src.at[me], land.at[me], local.at[a])
                for a, (src, land) in enumerate(zip(srcs, lands))]
        if kind == SCATTER:
            copies = mine + _exchange_copies(srcs, lands, send, recv)
            for cp in copies:
                cp.start()
            for cp in copies:
                cp.wait()
            return

        def block_copy(a, slot, block, k, src=None):
            s = a * (N_DEV - 1) + slot
            return pltpu.make_async_remote_copy(
                src_ref=lands[a].at[block] if src is None else src, dst_ref=lands[a].at[block],
                send_sem=send.at[s], recv_sem=recv.at[s], device_id=_peer(k), device_id_type=MESH)

        chips = (2, 4, 6)
        sibling = jnp.bitwise_xor(me, 1)
        first = [block_copy(a, slot, me, k, src=srcs[a]) for a in range(n) for slot, k in enumerate((1,) + chips)]
        for cp in mine + first:
            cp.start()
        passed = []
        for a in range(n):
            for i, k in enumerate(chips):
                block = jnp.bitwise_xor(me, k)
                block_copy(a, 1 + i, block, k).wait_recv()
                passed.append(block_copy(a, 4 + i, block, 1))
                passed[-1].start()
        for a in range(n):
            block_copy(a, 0, sibling, 1).wait_recv()
            for i, k in enumerate(chips):
                block_copy(a, 4 + i, jnp.bitwise_xor(sibling, k), 1).wait_recv()
        for cp in mine:
            cp.wait()
        for cp in first + passed:
            cp.wait_send()

    launch()
    return [land[...] for land in lands]


def _tile(n, pref):
    for t in pref:
        if n % t == 0:
            return t
    return n


MM_WIDE = 1024
MM_WEIGHT_BLOCK = 8 * 2 ** 20


def _blocks_per_step(nb, fits):
    return max(g for g in range(1, nb + 1) if nb % g == 0 and fits(g))


def mm_nn(a, b3, out_dtype, name, split_cols=None):
    m, k = a.shape
    nb, _, bn = b3.shape
    tm = _tile(m, (512, 256, 128))
    tn = bn // split_cols if split_cols else _tile(bn, (1024, 896, 512, 256, 128))
    per = bn // tn
    gb = _blocks_per_step(nb, lambda g: g == 1 or (per == 1 and g * bn <= MM_WIDE))

    def body(a_ref, b_ref, o_ref):
        for g in range(gb):
            o_ref[:, g * tn:(g + 1) * tn] = _dot(a_ref[...], b_ref[g]).astype(o_ref.dtype)

    if split_cols:
        out_spec = pl.BlockSpec((None, tm, tn), lambda i, j, jj: (jj, i, 0))
        out_shape = jax.ShapeDtypeStruct((split_cols, m, tn), out_dtype)
    else:
        out_spec = pl.BlockSpec((tm, gb * tn), lambda i, j, jj: (i, j * per + jj))
        out_shape = jax.ShapeDtypeStruct((m, nb * bn), out_dtype)
    return pl.pallas_call(
        body, name=name, grid=(m // tm, nb // gb, per),
        in_specs=[pl.BlockSpec((tm, k), lambda i, j, jj: (i, 0)),
                  pl.BlockSpec((gb, k, tn), lambda i, j, jj: (j, 0, jj))],
        out_specs=out_spec, out_shape=out_shape,
        compiler_params=_cparams("parallel", "arbitrary", "arbitrary"),
    )(a, b3)


def mm_nt(a, w3, out_dtype, name):
    m, _ = a.shape
    nb, ko, bn = w3.shape
    tm = _tile(m, (512, 256, 128))
    tko = _tile(ko, (1024, 512, 256, 128))
    gb = _blocks_per_step(nb, lambda g: g * tko * bn * w3.dtype.itemsize <= MM_WEIGHT_BLOCK)
    ns = nb // gb

    def body(a_ref, w_ref, o_ref, acc_ref):
        j = pl.program_id(2)

        @pl.when(j == 0)
        def _():
            acc_ref[...] = jnp.zeros_like(acc_ref)

        part = _dot_nt(a_ref[:, :bn], w_ref[0])
        for g in range(1, gb):
            part += _dot_nt(a_ref[:, g * bn:(g + 1) * bn], w_ref[g])
        acc_ref[...] += part

        @pl.when(j == ns - 1)
        def _():
            o_ref[...] = acc_ref[...].astype(o_ref.dtype)

    return pl.pallas_call(
        body, name=name, grid=(m // tm, ko // tko, ns),
        in_specs=[pl.BlockSpec((tm, gb * bn), lambda i, o, j: (i, j)),
                  pl.BlockSpec((gb, tko, bn), lambda i, o, j: (j, o, 0))],
        out_specs=pl.BlockSpec((tm, tko), lambda i, o, j: (i, o)),
        out_shape=jax.ShapeDtypeStruct((m, ko), out_dtype),
        scratch_shapes=[pltpu.VMEM((tm, tko), f32)],
        compiler_params=_cparams("parallel", "arbitrary", "arbitrary"),
    )(a, w3)


def mm_tn(a, dy, ncb, out_dtype, name):
    l, ka = a.shape
    _, n = dy.shape
    bn = n // ncb
    tl = _tile(l, (1024, 512, 256, 128))
    tka = _tile(ka, (512, 256, 128))
    tn = _tile(bn, (1024, 896, 512, 256, 128))
    per = bn // tn
    gb = _blocks_per_step(ncb, lambda g: g == 1 or (per == 1 and g * bn <= MM_WIDE))
    nl = l // tl

    def body(a_ref, dy_ref, o_ref, acc_ref):
        s = pl.program_id(2)

        @pl.when(s == 0)
        def _():
            acc_ref[...] = jnp.zeros_like(acc_ref)

        acc_ref[...] += _dot_tn(a_ref[...], dy_ref[...])

        @pl.when(s == nl - 1)
        def _():
            for g in range(gb):
                o_ref[g] = acc_ref[:, g * tn:(g + 1) * tn].astype(o_ref.dtype)

    return pl.pallas_call(
        body, name=name, grid=(ka // tka, n // (gb * tn), nl),
        in_specs=[pl.BlockSpec((tl, tka), lambda i, j, s: (s, i)),
                  pl.BlockSpec((tl, gb * tn), lambda i, j, s: (s, j))],
        out_specs=pl.BlockSpec((gb, tka, tn), lambda i, j, s: (j // per, i, j % per)),
        out_shape=jax.ShapeDtypeStruct((ncb, ka, bn), out_dtype),
        scratch_shapes=[pltpu.VMEM((tka, gb * tn), f32)],
        compiler_params=_cparams("parallel", "parallel", "arbitrary"),
    )(a, dy)


def mod_part(c_all, w_mod, b_cols):
    nl, d, cols = w_mod.shape

    def body(c_ref, w_ref, b_ref, o_ref):
        cond = _silu(c_ref[...]).astype(bf16)
        o_ref[...] = _dot(cond, w_ref[...].astype(bf16)) + b_ref[...]

    return pl.pallas_call(
        body, name="mod_part", grid=(nl,),
        in_specs=[pl.BlockSpec((N_DEV, d), lambda l: (0, 0)),
                  pl.BlockSpec((None, d, cols), lambda l: (l, 0, 0)),
                  pl.BlockSpec((None, 1, cols), lambda l: (l, 0, 0))],
        out_specs=pl.BlockSpec((None, N_DEV, cols), lambda l: (l, 0, 0)),
        out_shape=jax.ShapeDtypeStruct((nl, N_DEV, cols), f32),
        compiler_params=_cparams("arbitrary"),
    )(c_all, w_mod, b_cols.reshape(nl, 1, cols))


def _row_tile(l):
    return _tile(l, (512, 256, 128))


def _entry_rows(xv, g_ref, sh_ref, sc_ref, h_ref, ht_ref):
    r = lax.rsqrt(jnp.mean(xv * xv, axis=-1, keepdims=True) + EPS)
    h = xv * r * (g_ref[...] * (1.0 + sc_ref[...])) + sh_ref[...]
    h_ref[...] = h.astype(h_ref.dtype)
    ht_ref[...] = jnp.transpose(h).astype(ht_ref.dtype)


def prenorm_fwd(x, g, shift, scale, name):
    l, d = x.shape
    tm = _row_tile(l)

    def body(x_ref, g_ref, sh_ref, sc_ref, h_ref, ht_ref):
        _entry_rows(x_ref[...], g_ref, sh_ref, sc_ref, h_ref, ht_ref)

    return pl.pallas_call(
        body, name=name, grid=(l // tm,),
        in_specs=[pl.BlockSpec((tm, d), lambda i: (i, 0)), _row(d), _row(d), _row(d)],
        out_specs=[pl.BlockSpec((tm, d), lambda i: (i, 0)), pl.BlockSpec((d, tm), lambda i: (0, i))],
        out_shape=[jax.ShapeDtypeStruct((l, d), bf16), jax.ShapeDtypeStruct((d, l), bf16)],
        compiler_params=_cparams("parallel"),
    )(x, g, shift, scale)


def post_prenorm_fwd(x, y, gate, g_post, g_pre, shift, scale, name):
    l, d = x.shape
    tm = _row_tile(l)

    def body(x_ref, y_ref, gate_ref, gp_ref, g_ref, sh_ref, sc_ref, o_ref, h_ref, ht_ref):
        yv = y_ref[...]
        r = lax.rsqrt(jnp.mean(yv * yv, axis=-1, keepdims=True) + EPS)
        xv = x_ref[...] + gate_ref[...] * (yv * r * gp_ref[...])
        o_ref[...] = xv
        _entry_rows(xv, g_ref, sh_ref, sc_ref, h_ref, ht_ref)

    blk = pl.BlockSpec((tm, d), lambda i: (i, 0))
    return pl.pallas_call(
        body, name=name, grid=(l // tm,),
        in_specs=[blk, blk] + [_row(d)] * 5, out_specs=[blk, blk, pl.BlockSpec((d, tm), lambda i: (0, i))],
        out_shape=[jax.ShapeDtypeStruct((l, d), f32), jax.ShapeDtypeStruct((l, d), bf16),
                   jax.ShapeDtypeStruct((d, l), bf16)],
        compiler_params=_cparams("parallel"),
    )(x, y, gate, g_post, g_pre, shift, scale)


def _post_bwd_rows(dxv, yv, r, gate, gv, dy_ref, dgate_ref, dg_ref):
    yn = yv * r
    dgate_ref[...] += jnp.sum(dxv * yn * gv, axis=0, keepdims=True)
    dyg = dxv * gate
    dg_ref[...] += jnp.sum(dyg * yn, axis=0, keepdims=True)
    dyn = dyg * gv
    dy_ref[...] = (r * (dyn - yn * jnp.mean(dyn * yn, axis=-1, keepdims=True))).astype(dy_ref.dtype)


def final_loss(x, y, gate, g, target):
    l, d = x.shape
    tm = _row_tile(l)

    def body(x_ref, y_ref, gate_ref, g_ref, t_ref, dx_ref, loss_ref, dy_ref, dgate_ref, dg_ref):
        @pl.when(pl.program_id(0) == 0)
        def _():
            loss_ref[...] = jnp.zeros_like(loss_ref)
            dgate_ref[...] = jnp.zeros_like(dgate_ref)
            dg_ref[...] = jnp.zeros_like(dg_ref)

        yv, gate, gv = y_ref[...], gate_ref[...], g_ref[...]
        r = lax.rsqrt(jnp.mean(yv * yv, axis=-1, keepdims=True) + EPS)
        diff = x_ref[...] + gate * (yv * r * gv) - t_ref[...]
        dxv = diff * (1.0 / d)
        dx_ref[...] = dxv
        loss_ref[...] += jnp.sum(diff * diff)
        _post_bwd_rows(dxv, yv, r, gate, gv, dy_ref, dgate_ref, dg_ref)

    blk = pl.BlockSpec((tm, d), lambda i: (i, 0))
    return pl.pallas_call(
        body, name="final_loss", grid=(l // tm,),
        in_specs=[blk, blk, _row(d), _row(d), blk],
        out_specs=[blk, pl.BlockSpec((SUBLANES, HEAD), lambda i: (0, 0)), blk, _row(d), _row(d)],
        out_shape=[jax.ShapeDtypeStruct((l, d), f32), jax.ShapeDtypeStruct((SUBLANES, HEAD), f32),
                   jax.ShapeDtypeStruct((l, d), bf16), jax.ShapeDtypeStruct((1, d), f32), jax.ShapeDtypeStruct((1, d), f32)],
        compiler_params=_cparams("arbitrary"),
    )(x, y, gate, g, target)


def post_bwd(dx, y, gate, g, name):
    l, d = dx.shape
    tm = _row_tile(l)

    def body(dx_ref, y_ref, gate_ref, g_ref, dy_ref, dgate_ref, dg_ref):
        @pl.when(pl.program_id(0) == 0)
        def _():
            dgate_ref[...] = jnp.zeros_like(dgate_ref)
            dg_ref[...] = jnp.zeros_like(dg_ref)

        yv = y_ref[...]
        r = lax.rsqrt(jnp.mean(yv * yv, axis=-1, keepdims=True) + EPS)
        _post_bwd_rows(dx_ref[...], yv, r, gate_ref[...], g_ref[...], dy_ref, dgate_ref, dg_ref)

    blk = pl.BlockSpec((tm, d), lambda i: (i, 0))
    return pl.pallas_call(
        body, name=name, grid=(l // tm,),
        in_specs=[blk, blk, _row(d), _row(d)], out_specs=[blk, _row(d), _row(d)],
        out_shape=[jax.ShapeDtypeStruct((l, d), bf16), jax.ShapeDtypeStruct((1, d), f32),
                   jax.ShapeDtypeStruct((1, d), f32)],
        compiler_params=_cparams("arbitrary"),
    )(dx, y, gate, g)


def prenorm_bwd(dh, x, dx_next, g, scale, name):
    l, d = x.shape
    tm = _row_tile(l)

    def body(dh_ref, x_ref, dxn_ref, g_ref, sc_ref, dx_ref, dsh_ref, dsc_ref, dg_ref):
        @pl.when(pl.program_id(0) == 0)
        def _():
            dsh_ref[...] = jnp.zeros_like(dsh_ref)
            dsc_ref[...] = jnp.zeros_like(dsc_ref)
            dg_ref[...] = jnp.zeros_like(dg_ref)

        xv, dhv, gv, sc1 = x_ref[...], dh_ref[...], g_ref[...], 1.0 + sc_ref[...]
        r = lax.rsqrt(jnp.mean(xv * xv, axis=-1, keepdims=True) + EPS)
        xn = xv * r
        dhx = dhv * xn
        dsh_ref[...] += jnp.sum(dhv, axis=0, keepdims=True)
        dsc_ref[...] += jnp.sum(dhx * gv, axis=0, keepdims=True)
        dg_ref[...] += jnp.sum(dhx * sc1, axis=0, keepdims=True)
        dxn = dhv * (gv * sc1)
        dx_ref[...] = dxn_ref[...] + r * (dxn - xn * jnp.mean(dxn * xn, axis=-1, keepdims=True))

    blk = pl.BlockSpec((tm, d), lambda i: (i, 0))
    return pl.pallas_call(
        body, name=name, grid=(l // tm,),
        in_specs=[blk, blk, blk, _row(d), _row(d)], out_specs=[blk, _row(d), _row(d), _row(d)],
        out_shape=[jax.ShapeDtypeStruct((l, d), f32)] + [jax.ShapeDtypeStruct((1, d), f32)] * 3,
        compiler_params=_cparams("arbitrary"),
    )(dh, x, dx_next, g, scale)


def _tril_mask():
    r = lax.broadcasted_iota(jnp.int32, (HEAD, HEAD), 0)
    c = lax.broadcasted_iota(jnp.int32, (HEAD, HEAD), 1)
    return r >= c


def sgu_fwd(proj, norm_g, w_s, b_s):
    l = proj.shape[0]
    nh = w_s.shape[0]
    wa = nh * HEAD

    def body(au_ref, av_ref, az_ref, ng_ref, w_ref, b_ref, o_ref):
        tril = _tril_mask()
        for h in range(nh):
            sl = slice(h * HEAD, (h + 1) * HEAD)
            gv = _gelu(av_ref[:, sl].astype(f32))
            r = lax.rsqrt(jnp.mean(gv * gv, axis=-1, keepdims=True) + EPS)
            vh = gv * r * ng_ref[:, sl]
            wm = jnp.where(tril, w_ref[h], 0.0).astype(bf16)
            s = _dot(wm, vh.astype(bf16)) + b_ref[h]
            o_ref[:, sl] = (_gelu(au_ref[:, sl].astype(f32)) * s * _silu(az_ref[:, sl].astype(f32))).astype(o_ref.dtype)

    def col(j):
        return pl.BlockSpec((HEAD, wa), lambda n: (n, j))

    return pl.pallas_call(
        body, name="sgu_fwd", grid=(l // HEAD,),
        in_specs=[col(0), col(1), col(2), _row(wa),
                  pl.BlockSpec((nh, HEAD, HEAD), lambda n: (0, 0, 0)), pl.BlockSpec((nh, HEAD, 1), lambda n: (0, 0, 0))],
        out_specs=pl.BlockSpec((HEAD, wa), lambda n: (n, 0)),
        out_shape=jax.ShapeDtypeStruct((l, 2 * wa), bf16),
        compiler_params=_cparams("parallel"),
    )(proj, proj, proj, norm_g, w_s, b_s)


def sgu_bwd(proj, dcat, norm_g, w_s, b_s):
    l = proj.shape[0]
    nh = w_s.shape[0]
    wa = nh * HEAD

    def body(au_ref, av_ref, az_ref, do_ref, ng_ref, w_ref, b_ref, da_ref, dw_ref, db_ref, dng_ref):
        @pl.when(pl.program_id(0) == 0)
        def _():
            dw_ref[...] = jnp.zeros_like(dw_ref)
            db_ref[...] = jnp.zeros_like(db_ref)
            dng_ref[...] = jnp.zeros_like(dng_ref)

        tril = _tril_mask()
        for h in range(nh):
            sl = slice(h * HEAD, (h + 1) * HEAD)
            au, av, az = au_ref[:, sl].astype(f32), av_ref[:, sl].astype(f32), az_ref[:, sl].astype(f32)
            ng = ng_ref[:, sl]
            gv = _gelu(av)
            r = lax.rsqrt(jnp.mean(gv * gv, axis=-1, keepdims=True) + EPS)
            gvn = gv * r
            vh = (gvn * ng).astype(bf16)
            wm = jnp.where(tril, w_ref[h], 0.0).astype(bf16)
            s = _dot(wm, vh) + b_ref[h]
            gu, sz = _gelu(au), _silu(az)
            dov = do_ref[:, sl].astype(f32)
            da_ref[:, sl] = (dov * s * sz * _gelu_grad(au)).astype(da_ref.dtype)
            da_ref[:, 2 * wa + h * HEAD:2 * wa + (h + 1) * HEAD] = (dov * gu * s * _silu_grad(az)).astype(da_ref.dtype)
            ds = dov * gu * sz
            db_ref[h] += jnp.sum(ds, axis=-1, keepdims=True)
            dsb = ds.astype(bf16)
            dw_ref[h] += jnp.where(tril, _dot_nt(dsb, vh), 0.0)
            dvh = _dot_tn(wm, dsb)
            dng_ref[:, sl] += jnp.sum(dvh * gvn, axis=0, keepdims=True)
            dgvn = dvh * ng
            dgv = r * (dgvn - gvn * jnp.mean(dgvn * gvn, axis=-1, keepdims=True))
            da_ref[:, wa + h * HEAD:wa + (h + 1) * HEAD] = (dgv * _gelu_grad(av)).astype(da_ref.dtype)

    def col(j):
        return pl.BlockSpec((HEAD, wa), lambda n: (n, j))

    whole_w = pl.BlockSpec((nh, HEAD, HEAD), lambda n: (0, 0, 0))
    whole_b = pl.BlockSpec((nh, HEAD, 1), lambda n: (0, 0, 0))
    return pl.pallas_call(
        body, name="sgu_bwd", grid=(l // HEAD,),
        in_specs=[col(0), col(1), col(2), col(0), _row(wa), whole_w, whole_b],
        out_specs=[pl.BlockSpec((HEAD, 3 * wa), lambda n: (n, 0)), whole_w, whole_b, _row(wa)],
        out_shape=[jax.ShapeDtypeStruct(proj.shape, bf16), jax.ShapeDtypeStruct((nh, HEAD, HEAD), f32),
                   jax.ShapeDtypeStruct((nh, HEAD, 1), f32), jax.ShapeDtypeStruct((1, wa), f32)],
        compiler_params=_cparams("arbitrary"),
    )(proj, proj, proj, dcat, norm_g, w_s, b_s)


_LOG2E = 1.0 / math.log(2.0)
_SB_EXP_CLAMP = 120.0


def _sb_scores(q, k, scale):
    z = _dot_nt(q, k) * (scale * _LOG2E)
    return z, jnp.maximum(z, jnp.log2(1.0 + jnp.exp2(jnp.minimum(z, _SB_EXP_CLAMP))))


SB_KEYS = 256


def _sb_sum_matrix(tri, kb):
    s = lax.broadcasted_iota(jnp.int32, (2 * kb, kb + HEAD), 0) % kb
    j = lax.broadcasted_iota(jnp.int32, (2 * kb, kb + HEAD), 1)
    return jnp.where(jnp.logical_or(j >= kb, tri(s, j)), 1.0, 0.0).astype(bf16)


def _sb_sums(x, sums):
    kb = x.shape[1]
    c2 = _dot(jnp.concatenate(_split_bf16(x), axis=1), sums)
    return c2[:, :kb], c2[:, kb:]


def _sb_wide(v, kb):
    return jnp.concatenate([v] * (kb // HEAD), axis=1) if kb > HEAD else v


def _sb_q_tile(l, most=512):
    return _tile(l, tuple(t for t in (1024, 512, 256, 128) if t <= most))


def _sb_band_levels(band):
    return _tile(band, (4, 2, 1))


def _sb_heads_per_step(nh, most):
    return _tile(nh, tuple(h for h in (4, 2) if h <= most))


def sb_fwd(proj, mixed, nh):
    l = proj.shape[0]
    wb = nh * HEAD
    tq = _sb_q_tile(l, 1024)
    kb = min(SB_KEYS, tq)
    band = tq // kb
    hp = _sb_heads_per_step(nh, 2)
    levels = _sb_band_levels(band)
    scale = 1.0 / math.sqrt(HEAD)
    qc, kc, vc, zc = 3 * nh, 4 * nh, 5 * nh, 6 * nh

    def body(q_ref, k_ref, v_ref, bz_ref, mixed_ref, o_ref, att_ref, tot_ref):
        del mixed_ref
        i = pl.program_id(1)
        sums = _sb_sum_matrix(lambda s, j: s > j, kb)
        t_pos = i * tq + lax.broadcasted_iota(jnp.int32, (tq, kb), 0)
        s_off = lax.broadcasted_iota(jnp.int32, (tq, kb), 1)

        def step(j, carry, masked, row0=0):
            rows = pl.ds(pl.multiple_of(j * kb, kb), kb)
            out = []
            for e in range(hp):
                acc, tot = carry[e]
                sl = slice(e * HEAD, (e + 1) * HEAD)
                z, sp = _sb_scores(q_ref[row0:, sl], k_ref[rows, sl], scale)
                lb = z - sp
                if masked:
                    mask = s_off[row0:] + j * kb < t_pos[row0:]
                    sp = jnp.where(mask, sp, 0.0)
                later, total = _sb_sums(sp, sums)
                w = jnp.exp2(lb + _sb_wide(tot[row0:], kb) - later)
                if masked:
                    w = jnp.where(mask, w, 0.0)
                new = (acc[row0:] + _dot(w.astype(bf16), v_ref[rows, sl]), tot[row0:] - total)
                out.append(tuple(jnp.concatenate([old[:row0], upd]) if row0 else upd for old, upd in zip(carry[e], new)))
            return tuple(out)

        zero = jnp.zeros((tq, HEAD), f32)
        carry = ((zero, zero),) * hp
        for lv in reversed(range(levels)):
            carry = lax.fori_loop(
                0, band // levels,
                lambda t, c, lv=lv: step(band * i + (lv + 1) * (band // levels) - 1 - t, c, True, lv * (tq // levels)), carry)
        carry = lax.fori_loop(0, band * i, lambda t, c: step(band * i - 1 - t, c, False), carry)
        for e in range(hp):
            acc, tot = carry[e]
            sl = slice(e * HEAD, (e + 1) * HEAD)
            att_ref[:, sl] = acc.astype(att_ref.dtype)
            o_ref[:, sl] = (acc * _silu(bz_ref[:, sl].astype(f32))).astype(o_ref.dtype)
            tot_ref[e] = tot[:, :1]

    blk = lambda c0: pl.BlockSpec((tq, hp * HEAD), lambda g, i: (i, c0 // hp + g))
    head = lambda c0: pl.BlockSpec((l, hp * HEAD), lambda g, i: (0, c0 // hp + g))
    return pl.pallas_call(
        body, name="sb_fwd", grid=(nh // hp, l // tq),
        in_specs=[blk(qc), head(kc), head(vc), blk(zc), pl.BlockSpec(memory_space=pl.ANY)],
        out_specs=[blk(mixed.shape[1] // HEAD - nh), blk(0), pl.BlockSpec((hp, tq, 1), lambda g, i: (g, i, 0))],
        out_shape=[jax.ShapeDtypeStruct(mixed.shape, bf16), jax.ShapeDtypeStruct((l, wb), bf16),
                   jax.ShapeDtypeStruct((nh, l, 1), f32)],
        input_output_aliases={4: 0},
        compiler_params=_cparams("parallel", "arbitrary"),
    )(proj, proj, proj, proj, mixed)


def sb_bwd(proj, dcat, att, tot, dproj, nh):
    l = proj.shape[0]
    wb = nh * HEAD
    tq = _sb_q_tile(l, 1024)
    kb = min(SB_KEYS, tq)
    band = tq // kb
    nq = l // tq
    hp = _sb_heads_per_step(nh, 2)
    levels = _sb_band_levels(band)
    scale = 1.0 / math.sqrt(HEAD)
    qc, kc, vc, zc = 3 * nh, 4 * nh, 5 * nh, 6 * nh

    def body(q_ref, k_ref, v_ref, bz_ref, do_ref, att_ref, tot_ref, dproj_in, dproj_ref, dk_acc, dv_acc, dob_ref,
             tile_ref, head_ref, sems):
        del dproj_in
        g, i = pl.program_id(0), pl.program_id(1)

        def put(src, row0, c0, k):
            cols = pl.ds(pl.multiple_of((c0 + g * hp) * HEAD, HEAD), hp * HEAD)
            cp = pltpu.make_async_copy(src, dproj_ref.at[pl.ds(row0, src.shape[0]), cols], sems.at[k])
            cp.start()
            return cp

        @pl.when(i == 0)
        def _():
            dk_acc[...] = jnp.zeros_like(dk_acc)
            dv_acc[...] = jnp.zeros_like(dv_acc)

        my_rows = pl.multiple_of(i * tq, tq)
        bz = bz_ref[...].astype(f32)
        dov = do_ref[...].astype(f32)
        tile_ref[0] = (dov * att_ref[...].astype(f32) * _silu_grad(bz)).astype(bf16)
        dbz_copy = put(tile_ref.at[0], my_rows, zc, 0)
        dob_ref[...] = (dov * _silu(bz)).astype(bf16)
        upto = _sb_sum_matrix(lambda s, j: s <= j, kb)
        before = _sb_sum_matrix(lambda j, s: j < s, kb)
        t_pos = i * tq + lax.broadcasted_iota(jnp.int32, (tq, kb), 0)
        s_off = lax.broadcasted_iota(jnp.int32, (tq, kb), 1)

        def step(j, carry, masked, row0=0):
            rows = pl.ds(pl.multiple_of(j * kb, kb), kb)
            out = []
            for h in range(hp):
                dq, sp_seen, e_seen = (c[row0:] for c in carry[h])
                sl = slice(h * HEAD, (h + 1) * HEAD)
                q, kj, vj, dob = q_ref[row0:, sl], k_ref[rows, sl], v_ref[rows, sl], dob_ref[row0:, sl]
                z, sp = _sb_scores(q, kj, scale)
                lb = z - sp
                if masked:
                    mask = s_off[row0:] + j * kb < t_pos[row0:]
                    sp = jnp.where(mask, sp, 0.0)
                sp_upto, sp_total = _sb_sums(sp, upto)
                w = jnp.exp2(lb + _sb_wide(sp_seen, kb) + sp_upto)
                if masked:
                    w = jnp.where(mask, w, 0.0)
                dv_acc[rows, sl] += _dot_tn(w.astype(bf16), dob)
                e = _dot_nt(dob, vj) * w
                e_before, e_total = _sb_sums(e, before)
                dz = (e - (e + _sb_wide(e_seen, kb) + e_before) * jnp.exp2(lb)) * scale
                if masked:
                    dz = jnp.where(mask, dz, 0.0)
                dz = dz.astype(bf16)
                dk_acc[rows, sl] += _dot_tn(dz, q)
                new = (dq + _dot(dz, kj), sp_seen + sp_total, e_seen + e_total)
                out.append(tuple(jnp.concatenate([old[:row0], upd]) if row0 else upd for old, upd in zip(carry[h], new)))
            return tuple(out)

        zero = jnp.zeros((tq, HEAD), f32)
        init = tuple((zero, jnp.broadcast_to(tot_ref[h], (tq, HEAD)), zero) for h in range(hp))
        carry = lax.fori_loop(0, band * i, lambda j, c: step(j, c, False), init)
        for lv in range(levels):
            carry = lax.fori_loop(
                0, band // levels,
                lambda t, c, lv=lv: step(band * i + lv * (band // levels) + t, c, True, lv * (tq // levels)), carry)
        for h in range(hp):
            tile_ref[1, :, h * HEAD:(h + 1) * HEAD] = carry[h][0].astype(bf16)
        dq_copy = put(tile_ref.at[1], my_rows, qc, 1)
        dbz_copy.wait()
        dq_copy.wait()

        @pl.when(i == nq - 1)
        def _():
            head_ref[0] = dk_acc[...].astype(bf16)
            head_ref[1] = dv_acc[...].astype(bf16)
            copies = [put(head_ref.at[0], 0, kc, 2), put(head_ref.at[1], 0, vc, 3)]
            for cp in copies:
                cp.wait()

    blk = lambda c0: pl.BlockSpec((tq, hp * HEAD), lambda g, i: (i, c0 // hp + g))
    head = lambda c0: pl.BlockSpec((l, hp * HEAD), lambda g, i: (0, c0 // hp + g))
    any_spec = pl.BlockSpec(memory_space=pl.ANY)
    return pl.pallas_call(
        body, name="sb_bwd", grid=(nh // hp, nq),
        in_specs=[blk(qc), head(kc), head(vc), blk(zc), blk(nh), blk(0),
                  pl.BlockSpec((hp, tq, 1), lambda g, i: (g, i, 0)), any_spec],
        out_specs=any_spec, out_shape=jax.ShapeDtypeStruct(dproj.shape, bf16), input_output_aliases={7: 0},
        scratch_shapes=[pltpu.VMEM((l, hp * HEAD), f32), pltpu.VMEM((l, hp * HEAD), f32),
                        pltpu.VMEM((tq, hp * HEAD), bf16), pltpu.VMEM((2, tq, hp * HEAD), bf16),
                        pltpu.VMEM((2, l, hp * HEAD), bf16), pltpu.SemaphoreType.DMA((4,))],
        compiler_params=_cparams("parallel", "arbitrary"),
    )(proj, proj, proj, proj, dcat, att, tot, dproj)


def _disc(lr, li, ldt):
    dt = jnp.exp(ldt)
    mag = jnp.exp(lr * dt)
    a_re = mag * jnp.cos(li * dt)
    a_im = mag * jnp.sin(li * dt)
    den = lr * lr + li * li
    nr = a_re - 1.0
    return a_re, a_im, (nr * lr + a_im * li) / den, (a_im * lr - nr * li) / den


def s5_params_fwd(lr, li, ldt, bt_re, bt_im):
    g, c, p = bt_re.shape

    def body(lr_ref, li_ref, ldt_ref, br_ref, bi_ref, ar_ref, ai_ref, bbr_ref, bbi_ref):
        a_re, a_im, cr, ci = _disc(lr_ref[...], li_ref[...], ldt_ref[...])
        ar_ref[...] = a_re
        ai_ref[...] = a_im
        for k in range(c):
            br, bi = br_ref[:, k, :], bi_ref[:, k, :]
            bbr_ref[:, k, :] = cr * br - ci * bi
            bbi_ref[:, k, :] = cr * bi + ci * br

    return pl.pallas_call(
        body, name="s5_params_fwd",
        out_shape=[jax.ShapeDtypeStruct((g, p), f32)] * 2 + [jax.ShapeDtypeStruct((g, c, p), f32)] * 2,
    )(lr, li, ldt, bt_re, bt_im)


def s5_params_bwd(lr, li, ldt, bt_re, bt_im, da_re, da_im, dbbt_re, dbbt_im):
    g, c, p = bt_re.shape

    def body(lr_ref, li_ref, ldt_ref, br_ref, bi_ref, dar_ref, dai_ref, dbbr_ref, dbbi_ref,
             dlr_ref, dli_ref, dldt_ref, dbr_ref, dbi_ref):
        (a_re, a_im, cr, ci), vjp = jax.vjp(_disc, lr_ref[...], li_ref[...], ldt_ref[...])
        dcr = jnp.zeros((g, p), f32)
        dci = jnp.zeros((g, p), f32)
        for k in range(c):
            br, bi = br_ref[:, k, :], bi_ref[:, k, :]
            dr, di = dbbr_ref[:, k, :], dbbi_ref[:, k, :]
            dcr += dr * br + di * bi
            dci += di * br - dr * bi
            dbr_ref[:, k, :] = cr * dr + ci * di
            dbi_ref[:, k, :] = cr * di - ci * dr
        dlr, dli, dldt = vjp((dar_ref[...], dai_ref[...], dcr, dci))
        dlr_ref[...] = dlr
        dli_ref[...] = dli
        dldt_ref[...] = dldt

    return pl.pallas_call(
        body, name="s5_params_bwd",
        out_shape=[jax.ShapeDtypeStruct((g, p), f32)] * 2 + [jax.ShapeDtypeStruct((g, 1), f32)]
        + [jax.ShapeDtypeStruct((g, c, p), f32)] * 2,
    )(lr, li, ldt, bt_re, bt_im, da_re, da_im, dbbt_re, dbbt_im)


def _cmul(ar, ai, br, bi):
    return ar * br - ai * bi, ar * bi + ai * br


def _power_tables(ar, ai):
    rows = lax.broadcasted_iota(jnp.int32, (SUBLANES, ar.shape[1]), 0)
    pr = jnp.zeros((SUBLANES, ar.shape[1]), f32)
    pi = jnp.zeros((SUBLANES, ar.shape[1]), f32)
    cr, ci = ar, ai
    pows = {}
    for r in range(SUBLANES):
        pows[r + 1] = (cr, ci)
        pr = jnp.where(rows == r, cr, pr)
        pi = jnp.where(rows == r, ci, pi)
        cr, ci = _cmul(cr, ci, ar, ai)
    return [pows[1], pows[2], pows[4]], pr, pi


def _ssm_time_tile(l):
    return _tile(l, (2048, 1024, 512, 256, 128))


def ssm_fwd(u, bre3, bim3, cre3, cimn3, a_re, a_im, d_skip):
    l, w = u.shape[0], d_skip.shape[1]
    nj = w // HEAD
    ns = STATES_PER_LANE_BLOCK
    tt = _ssm_time_tile(l)

    def body(u_ref, bre_ref, bim_ref, cre_ref, cim_ref, ar_ref, ai_ref, d_ref, y_ref, hr_ref, hi_ref, cr_ref, ci_ref):
        @pl.when(pl.program_id(1) == 0)
        def _():
            cr_ref[...] = jnp.zeros_like(cr_ref)
            ci_ref[...] = jnp.zeros_like(ci_ref)

        uv = u_ref[...]
        hr_ref[...] = _dot(uv, bre_ref[...])
        hi_ref[...] = _dot(uv, bim_ref[...])
        steps, pr, pi = _power_tables(ar_ref[...], ai_ref[...])
        rows = lax.broadcasted_iota(jnp.int32, (SUBLANES, ns), 0)
        steps = [(jnp.where(rows >= d, sr_, 0.0), jnp.where(rows >= d, si_, 0.0)) for d, (sr_, si_) in zip((1, 2, 4), steps)]

        def blk(b, carry):
            cr, ci = carry
            sl = pl.ds(pl.multiple_of(b * SUBLANES, SUBLANES), SUBLANES)
            xr, xi = hr_ref[sl, :], hi_ref[sl, :]
            for d, (sr_, si_) in zip((1, 2, 4), steps):
                mr, mi = _cmul(sr_, si_, pltpu.roll(xr, d, axis=0), pltpu.roll(xi, d, axis=0))
                xr, xi = xr + mr, xi + mi
            mr, mi = _cmul(pr, pi, cr, ci)
            xr, xi = xr + mr, xi + mi
            hr_ref[sl, :] = xr
            hi_ref[sl, :] = xi
            return xr[SUBLANES - 1:, :], xi[SUBLANES - 1:, :]

        cr, ci = lax.fori_loop(0, tt // SUBLANES, blk, (cr_ref[...], ci_ref[...]))
        cr_ref[...] = cr
        ci_ref[...] = ci
        y = _dot(hr_ref[...].astype(bf16), cre_ref[...]) + _dot(hi_ref[...].astype(bf16), cim_ref[...])
        y_ref[...] = y + d_ref[...] * uv.astype(f32)

    lane = pl.BlockSpec((tt, HEAD), lambda j, i: (i, j))
    st = pl.BlockSpec((tt, ns), lambda j, i: (i, j))
    b3 = pl.BlockSpec((None, HEAD, ns), lambda j, i: (j, 0, 0))
    c3 = pl.BlockSpec((None, ns, HEAD), lambda j, i: (j, 0, 0))
    arow = pl.BlockSpec((1, ns), lambda j, i: (0, j))
    return pl.pallas_call(
        body, name="ssm_fwd", grid=(nj, l // tt),
        in_specs=[lane, b3, b3, c3, c3, arow, arow, pl.BlockSpec((1, HEAD), lambda j, i: (0, j))],
        out_specs=[lane, st, st],
        out_shape=[jax.ShapeDtypeStruct((l, w), f32), jax.ShapeDtypeStruct((l, nj * ns), f32),
                   jax.ShapeDtypeStruct((l, nj * ns), f32)],
        scratch_shapes=[pltpu.VMEM((1, ns), f32), pltpu.VMEM((1, ns), f32)],
        compiler_params=_cparams("parallel", "arbitrary"),
    )(u, bre3, bim3, cre3, cimn3, a_re, a_im, d_skip)


def ssm_bwd(dy, u, dproj, h_re, h_im, bre3, bim3, cre3, cimn3, a_re, a_im, d_skip):
    l, w = u.shape[0], d_skip.shape[1]
    nj = w // HEAD
    ns = STATES_PER_LANE_BLOCK
    tt = _ssm_time_tile(l)
    nt = l // tt

    def body(dy_ref, u_ref, dproj_ref, hr_ref, hi_ref, bre_ref, bim_ref, cre_ref, cim_ref, ar_ref, ai_ref, d_ref,
             du_ref, dd_ref, dar_ref, dai_ref, dbre_ref, dbim_ref, dcre_ref, dcim_ref, kr_ref, ki_ref, cr_ref, ci_ref,
             accr_ref, acci_ref):
        del dproj_ref
        i = pl.program_id(1)

        @pl.when(i == 0)
        def _():
            for ref in (cr_ref, ci_ref, accr_ref, acci_ref, dd_ref, dbre_ref, dbim_ref, dcre_ref, dcim_ref):
                ref[...] = jnp.zeros_like(ref)

        dyv = dy_ref[...]
        dyb = dyv.astype(bf16)
        uv = u_ref[...]
        kr_ref[...] = _dot_nt(dyb, cre_ref[...])
        ki_ref[...] = _dot_nt(dyb, cim_ref[...])
        steps, pr, pi = _power_tables(ar_ref[...], -ai_ref[...])
        rows = lax.broadcasted_iota(jnp.int32, (SUBLANES, ns), 0)
        qr = jnp.zeros((SUBLANES, ns), f32)
        qi = jnp.zeros((SUBLANES, ns), f32)
        for r in range(SUBLANES):
            qr = jnp.where(rows == r, pr[SUBLANES - 1 - r:SUBLANES - r, :], qr)
            qi = jnp.where(rows == r, pi[SUBLANES - 1 - r:SUBLANES - r, :], qi)
        nb = tt // SUBLANES
        steps = [(jnp.where(rows < SUBLANES - d, sr_, 0.0), jnp.where(rows < SUBLANES - d, si_, 0.0))
                 for d, (sr_, si_) in zip((1, 2, 4), steps)]

        def blk(t, carry):
            cr, ci, accr, acci = carry
            sl = pl.ds(pl.multiple_of((nb - 1 - t) * SUBLANES, SUBLANES), SUBLANES)
            xr, xi = kr_ref[sl, :], ki_ref[sl, :]
            for d, (sr_, si_) in zip((1, 2, 4), steps):
                mr, mi = _cmul(sr_, si_, pltpu.roll(xr, SUBLANES - d, axis=0), pltpu.roll(xi, SUBLANES - d, axis=0))
                xr, xi = xr + mr, xi + mi
            mr, mi = _cmul(qr, qi, cr, ci)
            xr, xi = xr + mr, xi + mi
            kr_ref[sl, :] = xr
            ki_ref[sl, :] = xi
            last = rows == SUBLANES - 1
            nr = jnp.where(last, cr, pltpu.roll(xr, SUBLANES - 1, axis=0))
            ni = jnp.where(last, ci, pltpu.roll(xi, SUBLANES - 1, axis=0))
            hr, hi = hr_ref[sl, :], hi_ref[sl, :]
            accr = accr + nr * hr + ni * hi
            acci = acci + ni * hr - nr * hi
            return xr[:1, :], xi[:1, :], accr, acci

        cr, ci, accr, acci = lax.fori_loop(0, nb, blk, (cr_ref[...], ci_ref[...], accr_ref[...], acci_ref[...]))
        cr_ref[...] = cr
        ci_ref[...] = ci
        accr_ref[...] = accr
        acci_ref[...] = acci
        kr, ki = kr_ref[...].astype(bf16), ki_ref[...].astype(bf16)
        du = _dot_nt(kr, bre_ref[...]) + _dot_nt(ki, bim_ref[...]) + d_ref[...] * dyv
        du_ref[...] = du.astype(du_ref.dtype)
        dd_ref[...] += jnp.sum(dyv * uv.astype(f32), axis=0, keepdims=True)
        dbre_ref[...] += _dot_tn(uv, kr)
        dbim_ref[...] += _dot_tn(uv, ki)
        dcre_ref[...] += _dot_tn(hr_ref[...].astype(bf16), dyb)
        dcim_ref[...] += _dot_tn(hi_ref[...].astype(bf16), dyb)

        @pl.when(i == nt - 1)
        def _():
            dar_ref[...] = jnp.sum(accr_ref[...], axis=0, keepdims=True)
            dai_ref[...] = jnp.sum(acci_ref[...], axis=0, keepdims=True)

    lane = pl.BlockSpec((tt, HEAD), lambda j, i: (nt - 1 - i, j))
    st = pl.BlockSpec((tt, ns), lambda j, i: (nt - 1 - i, j))
    b3 = pl.BlockSpec((None, HEAD, ns), lambda j, i: (j, 0, 0))
    c3 = pl.BlockSpec((None, ns, HEAD), lambda j, i: (j, 0, 0))
    arow = pl.BlockSpec((1, ns), lambda j, i: (0, j))
    drow = pl.BlockSpec((1, HEAD), lambda j, i: (0, j))
    return pl.pallas_call(
        body, name="ssm_bwd", grid=(nj, nt),
        in_specs=[lane, lane, pl.BlockSpec(memory_space=pl.ANY), st, st, b3, b3, c3, c3, arow, arow, drow],
        out_specs=[lane, drow, arow, arow, b3, b3, c3, c3], input_output_aliases={2: 0},
        out_shape=[jax.ShapeDtypeStruct(dproj.shape, bf16), jax.ShapeDtypeStruct((1, w), f32),
                   jax.ShapeDtypeStruct((1, nj * ns), f32), jax.ShapeDtypeStruct((1, nj * ns), f32),
                   jax.ShapeDtypeStruct((nj, HEAD, ns), f32), jax.ShapeDtypeStruct((nj, HEAD, ns), f32),
                   jax.ShapeDtypeStruct((nj, ns, HEAD), f32), jax.ShapeDtypeStruct((nj, ns, HEAD), f32)],
        scratch_shapes=[pltpu.VMEM((tt, ns), f32), pltpu.VMEM((tt, ns), f32), pltpu.VMEM((1, ns), f32),
                        pltpu.VMEM((1, ns), f32), pltpu.VMEM((SUBLANES, ns), f32), pltpu.VMEM((SUBLANES, ns), f32)],
        compiler_params=_cparams("parallel", "arbitrary"),
    )(dy, u, dproj, h_re, h_im, bre3, bim3, cre3, cimn3, a_re, a_im, d_skip)


def glu_fwd(y, z_src, w_glu, b_glu):
    l, w = y.shape
    tm = _row_tile(l)

    def body(y_ref, z_ref, w_ref, b_ref, g_ref, t_ref, o_ref):
        g = _gelu(y_ref[...])
        gb = g.astype(bf16)
        t = _dot(gb, w_ref[...]) + b_ref[...]
        g_ref[...] = gb
        t_ref[...] = t
        o_ref[...] = (g * jax.nn.sigmoid(t) * _silu(z_ref[...].astype(f32))).astype(o_ref.dtype)

    blk = pl.BlockSpec((tm, w), lambda i: (i, 0))
    return pl.pallas_call(
        body, name="glu_fwd", grid=(l // tm,),
        in_specs=[blk, pl.BlockSpec((tm, w), lambda i: (i, 1)), pl.BlockSpec((w, w), lambda i: (0, 0)), _row(w)],
        out_specs=[blk, blk, blk],
        out_shape=[jax.ShapeDtypeStruct((l, w), bf16), jax.ShapeDtypeStruct((l, w), f32),
                   jax.ShapeDtypeStruct((l, w), bf16)],
        compiler_params=_cparams("parallel"),
    )(y, z_src, w_glu, b_glu)


def glu_bwd(dout, y, t, z_src, w_glu):
    l, w = y.shape
    tm = _row_tile(l)

    def body(do_ref, y_ref, t_ref, z_ref, w_ref, dy_ref, dz_ref, dt_ref, db_ref):
        @pl.when(pl.program_id(0) == 0)
        def _():
            db_ref[...] = jnp.zeros_like(db_ref)

        yv, zv, dov = y_ref[...], z_ref[...].astype(f32), do_ref[...]
        g = _gelu(yv)
        sg = jax.nn.sigmoid(t_ref[...])
        dy2 = dov * _silu(zv)
        dz_ref[...] = (dov * g * sg * _silu_grad(zv)).astype(dz_ref.dtype)
        dt = dy2 * g * sg * (1.0 - sg)
        dtb = dt.astype(bf16)
        dt_ref[...] = dtb
        db_ref[...] += jnp.sum(dt, axis=0, keepdims=True)
        dg = dy2 * sg + _dot_nt(dtb, w_ref[...])
        dy_ref[...] = dg * _gelu_grad(yv)

    blk = pl.BlockSpec((tm, w), lambda i: (i, 0))
    return pl.pallas_call(
        body, name="glu_bwd", grid=(l // tm,),
        in_specs=[blk, blk, blk, pl.BlockSpec((tm, w), lambda i: (i, 1)), pl.BlockSpec((w, w), lambda i: (0, 0))],
        out_specs=[blk, pl.BlockSpec((tm, w), lambda i: (i, 1)), blk, _row(w)],
        out_shape=[jax.ShapeDtypeStruct((l, w), f32), jax.ShapeDtypeStruct((l, 2 * w), bf16),
                   jax.ShapeDtypeStruct((l, w), bf16), jax.ShapeDtypeStruct((1, w), f32)],
        compiler_params=_cparams("arbitrary"),
    )(dout, y, t, z_src, w_glu)


def _adamw(w, g, m, v):
    m = ADAM_B1 * m + (1.0 - ADAM_B1) * g
    v = ADAM_B2 * v + (1.0 - ADAM_B2) * (g * g)
    m_hat = m / (1.0 - ADAM_B1 ** ADAM_STEP)
    v_hat = v / (1.0 - ADAM_B2 ** ADAM_STEP)
    return -ADAM_LR * (m_hat / (jnp.sqrt(v_hat) + ADAM_EPS) + ADAM_WD * w), m, v


def adam_reduce(pieces, w, m, v, name):
    r, c = w.shape
    n = pieces.shape[0]
    tr = _tile(r, (256, 128, 64, 32, 16, 8))

    def body(p_ref, w_ref, m_ref, v_ref, g_ref, d_ref, nm_ref, nv_ref):
        g = p_ref[0].astype(f32)
        for s in range(1, n):
            g = g + p_ref[s].astype(f32)
        g_ref[...] = g
        d_ref[...], nm_ref[...], nv_ref[...] = _adamw(w_ref[...], g, m_ref[...], v_ref[...])

    blk = pl.BlockSpec((tr, c), lambda i: (i, 0))
    return pl.pallas_call(
        body, name=name, grid=(r // tr,),
        in_specs=[pl.BlockSpec((n, tr, c), lambda i: (0, i, 0)), blk, blk, blk],
        out_specs=[blk] * 4, out_shape=[jax.ShapeDtypeStruct((r, c), f32)] * 4,
        compiler_params=_cparams("parallel"),
    )(pieces, w, m, v)


def adam_w_mod(cond_t, dm, w, m, v):
    nl, d, cols = w.shape
    tr = _tile(d, (512, 256, 128))

    def body(c_ref, dm_ref, w_ref, m_ref, v_ref, g_ref, d_ref, nm_ref, nv_ref):
        g = jnp.dot(c_ref[...], dm_ref[...], preferred_element_type=f32, precision=lax.Precision.HIGHEST)
        g_ref[...] = g
        d_ref[...], nm_ref[...], nv_ref[...] = _adamw(w_ref[...], g, m_ref[...], v_ref[...])

    blk = pl.BlockSpec((None, tr, cols), lambda l, i: (l, i, 0))
    return pl.pallas_call(
        body, name="adam_w_mod", grid=(nl, d // tr),
        in_specs=[pl.BlockSpec((tr, N_DEV), lambda l, i: (i, 0)), pl.BlockSpec((None, N_DEV, cols), lambda l, i: (l, 0, 0)),
                  blk, blk, blk],
        out_specs=[blk] * 4, out_shape=[jax.ShapeDtypeStruct((nl, d, cols), f32)] * 4,
        compiler_params=_cparams("parallel", "parallel"),
    )(cond_t, dm, w, m, v)


def silu_rows(c_all):
    def body(c_ref, o_ref):
        o_ref[...] = _silu(c_ref[...])

    return pl.pallas_call(body, name="silu_rows", out_shape=jax.ShapeDtypeStruct(c_all.shape, f32))(c_all)


def _block_diag(x):
    g, a, b = x.shape
    nj = g // GROUPS_PER_LANE_BLOCK
    eye = jnp.eye(GROUPS_PER_LANE_BLOCK, dtype=x.dtype)
    x5 = x.reshape(nj, GROUPS_PER_LANE_BLOCK, a, b)
    return jnp.einsum("jgab,gh->jgahb", x5, eye).reshape(nj, GROUPS_PER_LANE_BLOCK * a, GROUPS_PER_LANE_BLOCK * b)


def _diag_blocks(x, a, b):
    nj = x.shape[0]
    x5 = x.reshape(nj, GROUPS_PER_LANE_BLOCK, a, GROUPS_PER_LANE_BLOCK, b)
    eye = jnp.eye(GROUPS_PER_LANE_BLOCK, dtype=x.dtype)
    return jnp.einsum("jgahb,gh->jgab", x5, eye).reshape(nj * GROUPS_PER_LANE_BLOCK, a, b)


PACK_ROW = SUBLANES * HEAD


def _pack(parts, row_multiple=SUBLANES):
    rows = []
    for p in parts:
        flat = p.reshape(-1)
        pad = (-flat.shape[0]) % PACK_ROW
        if pad:
            flat = jnp.concatenate([flat, jnp.zeros((pad,), flat.dtype)])
        rows.append(flat.reshape(-1, HEAD))
    pad = (-sum(r.shape[0] for r in rows)) % row_multiple
    if pad:
        rows.append(jnp.zeros((pad, HEAD), rows[0].dtype))
    return jnp.concatenate(rows, axis=0)


def _unpack(packed, shapes):
    out, r0 = [], 0
    for shp in shapes:
        n = math.prod(shp)
        nr = -(-n // PACK_ROW) * SUBLANES
        out.append(packed[r0:r0 + nr].reshape(-1)[:n].reshape(shp))
        r0 += nr
    return out


def adam_small(g, w, m, v):
    r, c = w.shape

    def body(g_ref, w_ref, m_ref, v_ref, d_ref, nm_ref, nv_ref):
        d_ref[...], nm_ref[...], nv_ref[...] = _adamw(w_ref[...], g_ref[...], m_ref[...], v_ref[...])

    tr = max(t for t in range(SUBLANES, 1024 + 1, SUBLANES) if r % t == 0)
    blk = pl.BlockSpec((tr, c), lambda i: (i, 0))
    return pl.pallas_call(
        body, name="adam_small", grid=(r // tr,),
        in_specs=[blk] * 4, out_specs=[blk] * 3, out_shape=[jax.ShapeDtypeStruct((r, c), f32)] * 3,
        compiler_params=_cparams("parallel"),
    )(g, w, m, v)


def kernel(x, c, ln_pre_g, ln_post_g, w_mod, b_mod, w_in_ab, w_out_ab, sgu_norm_g, sgu_w, sgu_b, w_in_ssm, w_out_ssm, lam_re, lam_im, b_re, b_im, c_re, c_im, d_skip, log_dt, w_glu, b_glu, loss_target, m_ln_pre_g, m_ln_post_g, m_w_mod, m_b_mod, m_w_in_ab, m_w_out_ab, m_sgu_norm_g, m_sgu_w, m_sgu_b, m_w_in_ssm, m_w_out_ssm, m_lam_re, m_lam_im, m_b_re, m_b_im, m_c_re, m_c_im, m_d_skip, m_log_dt, m_w_glu, m_b_glu, v_ln_pre_g, v_ln_post_g, v_w_mod, v_b_mod, v_w_in_ab, v_w_out_ab, v_sgu_norm_g, v_sgu_w, v_sgu_b, v_w_in_ssm, v_w_out_ssm, v_lam_re, v_lam_im, v_b_re, v_b_im, v_c_re, v_c_im, v_d_skip, v_log_dt, v_w_glu, v_b_glu):
    me = _my_index()
    x0 = x[0]
    l, d = x0.shape
    target = loss_target[0]
    nh = sgu_w.shape[1]
    wa = nh * HEAD
    n_grp, n_st = lam_re.shape[1], lam_re.shape[2]
    mod_cols = w_mod.shape[2]

    def after(a, first):
        return a + jnp.minimum(jnp.abs(first[(0,) * first.ndim].astype(f32)), 0.0).astype(a.dtype)

    c_all, d_skip_all, b_glu_all = all_gather([c, d_skip, b_glu], "gather_c")
    c_all = c_all.reshape(N_DEV, d)
    d_skip_all = d_skip_all.reshape(1, -1)
    b_glu_all = b_glu_all.reshape(1, -1)

    b_cols = lax.dynamic_slice_in_dim(b_mod, me * mod_cols, mod_cols, axis=1)
    (mod_all,) = all_gather([mod_part(c_all, w_mod, b_cols)], "gather_mod")
    (win_ab3,) = sequencer_exchange(GATHER, [after(w_in_ab[0], mod_all).astype(bf16)], "gather_w_in", 1)
    mod_mine = lax.dynamic_index_in_dim(mod_all, me, axis=2, keepdims=False)
    mod_rows = jnp.transpose(mod_mine, (1, 0, 2)).reshape(2, 3, 1, d)

    def rows(a, i):
        return a[i].reshape(1, d)

    shift0, scale0, gate0 = mod_rows[0, 0], mod_rows[0, 1], mod_rows[0, 2]
    h0, h0_t = prenorm_fwd(x0, rows(ln_pre_g, 0), shift0, scale0, "prenorm0")
    wout_ab3, win_ssm3, wout_ssm3, wglu = sequencer_exchange(
        GATHER, [after(w, win_ab3).astype(bf16) for w in (w_out_ab[0], w_in_ssm[0], w_out_ssm[0], w_glu[0])],
        "gather_w_rest", 2)
    proj0 = mm_nn(h0, win_ab3, bf16, "proj0")
    sgu_b3 = sgu_b[0].reshape(nh, HEAD, 1)
    cat, att, tot = sb_fwd(proj0, sgu_fwd(proj0, sgu_norm_g, sgu_w[0], sgu_b3), nh)
    wout_ab3 = wout_ab3.reshape(1, d, d)
    win_ssm3 = win_ssm3.reshape(1, d, d)
    wglu = wglu.reshape(w_glu.shape[2], w_glu.shape[2])
    y0 = mm_nn(cat, wout_ab3, f32, "out0")

    shift1, scale1, gate1 = mod_rows[1, 0], mod_rows[1, 1], mod_rows[1, 2]
    x1, h1, h1_t = post_prenorm_fwd(x0, y0, gate0, rows(ln_post_g, 0), rows(ln_pre_g, 1), shift1, scale1,
                                    "post0_prenorm1")
    proj1 = mm_nn(h1, win_ssm3, bf16, "proj1")
    w_ssm = proj1.shape[1] // 2
    ldt = log_dt[0].reshape(n_grp, 1)
    bt_re = jnp.transpose(b_re[0], (0, 2, 1))
    bt_im = jnp.transpose(b_im[0], (0, 2, 1))
    a_re, a_im, bbt_re, bbt_im = s5_params_fwd(lam_re[0], lam_im[0], ldt, bt_re, bt_im)
    bre3 = _block_diag(bbt_re).astype(bf16)
    bim3 = _block_diag(bbt_im).astype(bf16)
    cre3 = _block_diag(jnp.transpose(c_re[0], (0, 2, 1))).astype(bf16)
    cimn3 = _block_diag(-jnp.transpose(c_im[0], (0, 2, 1))).astype(bf16)
    a_re_row, a_im_row = a_re.reshape(1, -1), a_im.reshape(1, -1)
    y_ssm, hs_re, hs_im = ssm_fwd(proj1, bre3, bim3, cre3, cimn3, a_re_row, a_im_row, d_skip_all)
    g_act, t_glu, mix1 = glu_fwd(y_ssm, proj1, wglu, b_glu_all)
    y1 = mm_nn(mix1, wout_ssm3, f32, "out1")

    dx2, loss_tile, dy1, dgate1, dgpost1 = final_loss(x1, y1, gate1, rows(ln_post_g, 1), target)

    dmix1 = mm_nt(dy1, wout_ssm3, f32, "dmix1")
    gw_out_ssm = mm_tn(mix1, dy1, N_DEV, bf16, "gw_out_ssm")
    (p_out_ssm,) = sequencer_exchange(SCATTER, [gw_out_ssm], "scatter_g1", 3)
    dy_ssm, dproj1, dt_glu, db_glu = glu_bwd(dmix1, y_ssm, t_glu, proj1, wglu)
    gw_glu = mm_tn(g_act, dt_glu, 1, bf16, "gw_glu").reshape(N_DEV, -1, w_ssm)
    dproj1, dd_skip, da_re, da_im, dbre3, dbim3, dcre3, dcimn3 = ssm_bwd(
        dy_ssm, proj1, dproj1, hs_re, hs_im, bre3, bim3, cre3, cimn3, a_re_row, a_im_row, d_skip_all)
    gw_in_ssm = mm_nn(h1_t, dproj1[None], bf16, "gw_in_ssm").reshape(N_DEV, -1, proj1.shape[1])
    p_in_ssm, p_glu = sequencer_exchange(SCATTER, [gw_in_ssm, gw_glu], "scatter_g2", 4)
    dh1 = mm_nt(dproj1, win_ssm3, f32, "dh1")
    dx1, dshift1, dscale1, dgpre1 = prenorm_bwd(dh1, x1, dx2, rows(ln_pre_g, 1), scale1, "prenorm1_bwd")
    dlr, dli, dldt, dbt_re, dbt_im = s5_params_bwd(
        lam_re[0], lam_im[0], ldt, bt_re, bt_im, da_re.reshape(n_grp, n_st), da_im.reshape(n_grp, n_st),
        _diag_blocks(dbre3, SSM_GROUP, n_st), _diag_blocks(dbim3, SSM_GROUP, n_st))
    g_b_re = jnp.transpose(dbt_re, (0, 2, 1))
    g_b_im = jnp.transpose(dbt_im, (0, 2, 1))
    g_c_re = jnp.transpose(_diag_blocks(dcre3, n_st, SSM_GROUP), (0, 2, 1))
    g_c_im = -jnp.transpose(_diag_blocks(dcimn3, n_st, SSM_GROUP), (0, 2, 1))

    dy0, dgate0, dgpost0 = post_bwd(dx1, y0, gate0, rows(ln_post_g, 0), "post0_bwd")
    dcat = mm_nt(dy0, wout_ab3, f32, "dcat")
    gw_out_ab = mm_tn(cat, dy0, 1, bf16, "gw_out_ab").reshape(N_DEV, -1, d)
    (p_out_ab,) = sequencer_exchange(SCATTER, [gw_out_ab], "scatter_g3", 5)
    dproj0, dsgu_w, dsgu_b, dsgu_ng = sgu_bwd(proj0, dcat, sgu_norm_g, sgu_w[0], sgu_b3)
    dproj0 = sb_bwd(proj0, dcat, att, tot, dproj0, nh)
    gw_in_ab = mm_nn(h0_t, dproj0[None], bf16, "gw_in_ab", split_cols=N_DEV)
    (p_in_ab,) = sequencer_exchange(SCATTER, [gw_in_ab], "scatter_g4", 6)
    dh0 = mm_nt(dproj0, win_ab3, f32, "dh0")
    dx0, dshift0, dscale0, dgpre0 = prenorm_bwd(dh0, x0, dx1, rows(ln_pre_g, 0), scale0, "prenorm0_bwd")

    small_names = ["ln_pre_g", "ln_post_g", "b_mod", "sgu_norm_g", "sgu_w", "sgu_b", "lam_re", "lam_im", "b_re", "b_im",
                   "c_re", "c_im", "log_dt"]
    small_w = [ln_pre_g, ln_post_g, b_mod, sgu_norm_g, sgu_w, sgu_b, lam_re, lam_im, b_re, b_im, c_re, c_im, log_dt]
    small_m = [m_ln_pre_g, m_ln_post_g, m_b_mod, m_sgu_norm_g, m_sgu_w, m_sgu_b, m_lam_re, m_lam_im, m_b_re, m_b_im,
               m_c_re, m_c_im, m_log_dt]
    small_v = [v_ln_pre_g, v_ln_post_g, v_b_mod, v_sgu_norm_g, v_sgu_w, v_sgu_b, v_lam_re, v_lam_im, v_b_re, v_b_im,
               v_c_re, v_c_im, v_log_dt]
    def sharded(p, w, m, v, name):
        shp = w.shape
        w2, m2, v2 = (a.reshape(-1, shp[-1]) for a in (w, m, v))
        return [o.reshape(shp) for o in adam_reduce(p.reshape(p.shape[0], -1, shp[-1]), w2, m2, v2, name)]

    r_w_out_ssm = sharded(p_out_ssm, w_out_ssm, m_w_out_ssm, v_w_out_ssm, "adam_w_out_ssm")
    r_w_in_ssm = sharded(p_in_ssm, w_in_ssm, m_w_in_ssm, v_w_in_ssm, "adam_w_in_ssm")
    r_w_glu = sharded(p_glu, w_glu, m_w_glu, v_w_glu, "adam_w_glu")
    r_w_out_ab = sharded(p_out_ab, w_out_ab, m_w_out_ab, v_w_out_ab, "adam_w_out_ab")
    dmod = jnp.concatenate([dshift0, dscale0, dgate0, dshift1, dscale1, dgate1], axis=1)
    for done in (r_w_out_ssm, r_w_in_ssm, r_w_glu, r_w_out_ab):
        dmod = after(dmod, done[0])
    small_g = [jnp.concatenate([dgpre0, dgpre1]), jnp.concatenate([dgpost0, dgpost1]), dmod, dsgu_ng, dsgu_w, dsgu_b,
               dlr, dli, g_b_re, g_b_im, g_c_re, g_c_im, dldt]
    shapes = [w.shape for w in small_w]
    g_sum, dmod_all = all_reduce_rows(_pack(small_g + [dd_skip, db_glu, loss_tile], SUBLANES * N_DEV), dmod,
                                      "reduce_small_grads")
    n_rows_small = sum(-(-math.prod(s) // PACK_ROW) * SUBLANES for s in shapes)
    loss = g_sum[n_rows_small + 2 * (d_skip_all.shape[1] // HEAD), 0] * (0.5 / d)
    new_small = adam_small(g_sum, _pack(small_w), _pack(small_m), _pack(small_v))
    r_small = [_unpack(o, shapes) for o in [g_sum[:n_rows_small]] + list(new_small)]
    small = {n: [r_small[k][i] for k in range(4)] for i, n in enumerate(small_names)}
    vec_rows = d_skip_all.shape[1] // HEAD

    def my_columns(r0):
        whole = g_sum[r0:r0 + vec_rows].reshape(1, 1, -1)
        return lax.dynamic_slice_in_dim(whole, me * d_skip.shape[1], d_skip.shape[1], axis=2)

    r_d_skip = sharded(my_columns(n_rows_small), d_skip, m_d_skip, v_d_skip, "adam_d_skip")
    r_b_glu = sharded(my_columns(n_rows_small + vec_rows), b_glu, m_b_glu, v_b_glu, "adam_b_glu")
    r_w_in_ab = sharded(p_in_ab, w_in_ab, m_w_in_ab, v_w_in_ab, "adam_w_in_ab")

    dm_cols = jnp.transpose(
        lax.dynamic_slice_in_dim(dmod_all.reshape(N_DEV, 2, 3 * d), me * mod_cols, mod_cols, axis=2), (1, 0, 2))
    cond_t = jnp.transpose(silu_rows(c_all))
    r_w_mod = adam_w_mod(cond_t, dm_cols, w_mod, m_w_mod, v_w_mod)

    res = dict(small)
    res.update(w_mod=r_w_mod, w_in_ab=r_w_in_ab, w_out_ab=r_w_out_ab, w_in_ssm=r_w_in_ssm, w_out_ssm=r_w_out_ssm,
               d_skip=r_d_skip, w_glu=r_w_glu, b_glu=r_b_glu)
    order = ["ln_pre_g", "ln_post_g", "w_mod", "b_mod", "w_in_ab", "w_out_ab", "sgu_norm_g", "sgu_w", "sgu_b", "w_in_ssm",
             "w_out_ssm", "lam_re", "lam_im", "b_re", "b_im", "c_re", "c_im", "d_skip", "log_dt", "w_glu", "b_glu"]
    outs = [loss, dx0.reshape(x.shape)]
    for k in range(4):
        outs += [res[n][k] for n in order]
    return tuple(outs)
```

```python
import functools
import math

import jax
import jax.numpy as jnp
from jax import lax
from jax.experimental import pallas as pl
from jax.experimental.pallas import tpu as pltpu
from jax.experimental.pallas import tpu_sc as plsc

f32 = jnp.float32
bf16 = jnp.bfloat16

N_DEV = 8
EPS = 1e-6
HEAD = 128
SUBLANES = 8
SSM_GROUP = 16
SSM_STATE = 64
GROUPS_PER_LANE_BLOCK = HEAD // SSM_GROUP
STATES_PER_LANE_BLOCK = GROUPS_PER_LANE_BLOCK * SSM_STATE
VMEM_LIMIT = 56 * 2 ** 20
ADAM_LR, ADAM_B1, ADAM_B2, ADAM_EPS, ADAM_WD, ADAM_STEP = 0.001, 0.9, 0.999, 1e-08, 0.01, 10
_GELU_C0 = math.sqrt(2.0 / math.pi)
_GELU_C1 = 0.044715
MESH = pl.DeviceIdType.MESH


def _cparams(*sem):
    return pltpu.CompilerParams(dimension_semantics=sem if sem else None, vmem_limit_bytes=VMEM_LIMIT)


def _gelu(x):
    return 0.5 * x * (1.0 + jnp.tanh(_GELU_C0 * (x + _GELU_C1 * x * x * x)))


def _gelu_grad(x):
    t = jnp.tanh(_GELU_C0 * (x + _GELU_C1 * x * x * x))
    return 0.5 * (1.0 + t) + 0.5 * x * (1.0 - t * t) * _GELU_C0 * (1.0 + 3.0 * _GELU_C1 * x * x)


def _silu(x):
    return x * jax.nn.sigmoid(x)


def _silu_grad(x):
    s = jax.nn.sigmoid(x)
    return s * (1.0 + x * (1.0 - s))


def _dot(a, b):
    return jnp.dot(a, b, preferred_element_type=f32)


def _dot_nt(a, b):
    return lax.dot_general(a, b, (((1,), (1,)), ((), ())), preferred_element_type=f32)


def _dot_tn(a, b):
    return lax.dot_general(a, b, (((0,), (0,)), ((), ())), preferred_element_type=f32)


def _split_bf16(v):
    hi = v.astype(bf16)
    lo = (v - hi.astype(f32)).astype(bf16)
    return hi, lo


def _row(d):
    return pl.BlockSpec((1, d), lambda *_: (0, 0))


def _my_index():
    return 4 * lax.axis_index("x") + 2 * lax.axis_index("y") + lax.axis_index("c")


def _peer(k):
    x, y, c = lax.axis_index("x"), lax.axis_index("y"), lax.axis_index("c")
    return (1 - x if k & 4 else x, 1 - y if k & 2 else y, 1 - c if k & 1 else c)


def all_gather(arrs, name):
    n = len(arrs)

    def body(*refs):
        ins, outs = refs[:n], refs[n:2 * n]
        send, recv, local = refs[2 * n:]
        me = _my_index()
        copies = []
        for a in range(n):
            cp = pltpu.make_async_copy(ins[a], outs[a].at[me], local.at[a])
            cp.start()
            copies.append(cp)
            for k in range(1, N_DEV):
                s = a * (N_DEV - 1) + k - 1
                cp = pltpu.make_async_remote_copy(src_ref=ins[a], dst_ref=outs[a].at[me], send_sem=send.at[s],
                                                  recv_sem=recv.at[s], device_id=_peer(k), device_id_type=MESH)
                cp.start()
                copies.append(cp)
        for cp in copies:
            cp.wait()

    any_spec = pl.BlockSpec(memory_space=pl.ANY)
    outs = pl.pallas_call(
        body, name=name,
        out_shape=[jax.ShapeDtypeStruct((N_DEV,) + a.shape, a.dtype) for a in arrs],
        in_specs=[any_spec] * n, out_specs=[any_spec] * n,
        scratch_shapes=[pltpu.SemaphoreType.DMA((n * (N_DEV - 1),)), pltpu.SemaphoreType.DMA((n * (N_DEV - 1),)),
                        pltpu.SemaphoreType.DMA((n,))],
        compiler_params=pltpu.CompilerParams(has_side_effects=True),
    )(*arrs)
    return list(outs)


def all_reduce_rows(pack, extra, name):
    r, c = pack.shape
    rs = r // N_DEV
    n_peer = N_DEV - 1

    def body(p_ref, x_ref, o_ref, xo_ref, land, red, send1, recv1, send2, recv2, sendx, recvx, local):
        me = _my_index()

        def rows(i):
            return pl.ds(pl.multiple_of(i * rs, SUBLANES), rs)

        own = [pltpu.make_async_copy(p_ref.at[rows(me)], land.at[me], local.at[0]),
               pltpu.make_async_copy(x_ref, xo_ref.at[me], local.at[1])]
        first = []
        for k in range(1, N_DEV):
            first.append(pltpu.make_async_remote_copy(
                src_ref=p_ref.at[rows(jnp.bitwise_xor(me, k))], dst_ref=land.at[me], send_sem=send1.at[k - 1],
                recv_sem=recv1.at[k - 1], device_id=_peer(k), device_id_type=MESH))
            first.append(pltpu.make_async_remote_copy(
                src_ref=x_ref, dst_ref=xo_ref.at[me], send_sem=sendx.at[k - 1], recv_sem=recvx.at[k - 1],
                device_id=_peer(k), device_id_type=MESH))
        for cp in own + first:
            cp.start()
        for cp in own + first:
            cp.wait()
        acc = land[0]
        for s in range(1, N_DEV):
            acc = acc + land[s]
        red[...] = acc
        mine = pltpu.make_async_copy(red, o_ref.at[rows(me)], local.at[2])
        second = [pltpu.make_async_remote_copy(
            src_ref=red, dst_ref=o_ref.at[rows(me)], send_sem=send2.at[k - 1], recv_sem=recv2.at[k - 1],
            device_id=_peer(k), device_id_type=MESH) for k in range(1, N_DEV)]
        for cp in [mine] + second:
            cp.start()
        for cp in [mine] + second:
            cp.wait()

    any_spec = pl.BlockSpec(memory_space=pl.ANY)
    return pl.pallas_call(
        body, name=name,
        out_shape=[jax.ShapeDtypeStruct((r, c), pack.dtype), jax.ShapeDtypeStruct((N_DEV,) + extra.shape, extra.dtype)],
        in_specs=[any_spec, any_spec], out_specs=[any_spec, any_spec],
        scratch_shapes=[pltpu.VMEM((N_DEV, rs, c), pack.dtype), pltpu.VMEM((rs, c), pack.dtype)]
        + [pltpu.SemaphoreType.DMA((n_peer,))] * 6 + [pltpu.SemaphoreType.DMA((3,))],
        compiler_params=pltpu.CompilerParams(has_side_effects=True),
    )(pack, extra)


GATHER, SCATTER = "gather", "scatter"


def _exchange_copies(srcs, lands, send, recv):
    me = _my_index()
    copies = []
    for a, (src, land) in enumerate(zip(srcs, lands)):
        for k in range(1, N_DEV):
            s = a * (N_DEV - 1) + k - 1
            copies.append(pltpu.make_async_remote_copy(
                src_ref=src.at[jnp.bitwise_xor(me, k)], dst_ref=land.at[me],
                send_sem=send.at[s], recv_sem=recv.at[s], device_id=_peer(k), device_id_type=MESH))
    return copies


def sequencer_exchange(kind, arrs, name, collective_id):
    n = len(arrs)
    n_sem = n * (N_DEV - 1)
    land_shapes = [((N_DEV,) + a.shape if kind == GATHER else a.shape) for a in arrs]
    srcs = [jax.new_ref(a, memory_space=pltpu.MemorySpace.HBM) for a in arrs]
    lands = [jax.empty_ref(jax.ShapeDtypeStruct(s, a.dtype), memory_space=pltpu.MemorySpace.HBM)
             for s, a in zip(land_shapes, arrs)]

    @pl.kernel(mesh=plsc.ScalarSubcoreMesh(axis_name="sequencer", num_cores=1), name=name,
               scratch_types=(pltpu.SemaphoreType.DMA((n_sem,)), pltpu.SemaphoreType.DMA((n_sem,)),
                              pltpu.SemaphoreType.DMA((n,))),
               compiler_params=pltpu.CompilerParams(collective_id=collective_id))
    def launch(send, recv, local):
        barrier = pltpu.get_barrier_semaphore()
        for k in range(1, N_DEV):
            pl.semaphore_signal(barrier, inc=1, device_id=_peer(k), device_id_type=MESH)
        pl.semaphore_wait(barrier, N_DEV - 1)
        me = _my_index()
        mine = [pltpu.make_async_copy(src if kind == GATHER else src.at[me], land.at[me], local.at[a])
                for a, (src, land) in enumerate(zip(srcs, lands))]
        if kind == SCATTER:
            copies = mine + _exchange_copies(srcs, lands, send, recv)
            for cp in copies:
                cp.start()
            for cp in copies:
                cp.wait()
            return

        def block_copy(a, slot, block, k, src=None):
            s = a * (N_DEV - 1) + slot
            return pltpu.make_async_remote_copy(
                src_ref=lands[a].at[block] if src is None else src, dst_ref=lands[a].at[block],
                send_sem=send.at[s], recv_sem=recv.at[s], device_id=_peer(k), device_id_type=MESH)

        chips = (2, 4, 6)
        sibling = jnp.bitwise_xor(me, 1)
        first = [block_copy(a, slot, me, k, src=srcs[a]) for a in range(n) for slot, k in enumerate((1,) + chips)]
        for cp in mine + first:
            cp.start()
        passed = []
        for a in range(n):
            for i, k in enumerate(chips):
                block = jnp.bitwise_xor(me, k)
                block_copy(a, 1 + i, block, k).wait_recv()
                passed.append(block_copy(a, 4 + i, block, 1))
                passed[-1].start()
        for a in range(n):
            block_copy(a, 0, sibling, 1).wait_recv()
            for i, k in enumerate(chips):
                block_copy(a, 4 + i, jnp.bitwise_xor(sibling, k), 1).wait_recv()
        for cp in mine:
            cp.wait()
        for cp in first + passed:
            cp.wait_send()

    launch()
    return [land[...] for land in lands]


def _tile(n, pref):
    for t in pref:
        if n % t == 0:
            return t
    return n


MM_WIDE = 1024
MM_WEIGHT_BLOCK = 8 * 2 ** 20


def _blocks_per_step(nb, fits):
    return max(g for g in range(1, nb + 1) if nb % g == 0 and fits(g))


def mm_nn(a, b3, out_dtype, name, split_cols=None):
    m, k = a.shape
    nb, _, bn = b3.shape
    tm = _tile(m, (512, 256, 128))
    tn = bn // split_cols if split_cols else _tile(bn, (1024, 896, 512, 256, 128))
    per = bn // tn
    gb = _blocks_per_step(nb, lambda g: g == 1 or (per == 1 and g * bn <= MM_WIDE))

    def body(a_ref, b_ref, o_ref):
        for g in range(gb):
            o_ref[:, g * tn:(g + 1) * tn] = _dot(a_ref[...], b_ref[g]).astype(o_ref.dtype)

    if split_cols:
        out_spec = pl.BlockSpec((None, tm, tn), lambda i, j, jj: (jj, i, 0))
        out_shape = jax.ShapeDtypeStruct((split_cols, m, tn), out_dtype)
    else:
        out_spec = pl.BlockSpec((tm, gb * tn), lambda i, j, jj: (i, j * per + jj))
        out_shape = jax.ShapeDtypeStruct((m, nb * bn), out_dtype)
    return pl.pallas_call(
        body, name=name, grid=(m // tm, nb // gb, per),
        in_specs=[pl.BlockSpec((tm, k), lambda i, j, jj: (i, 0)),
                  pl.BlockSpec((gb, k, tn), lambda i, j, jj: (j, 0, jj))],
        out_specs=out_spec, out_shape=out_shape,
        compiler_params=_cparams("parallel", "arbitrary", "arbitrary"),
    )(a, b3)


def mm_nt(a, w3, out_dtype, name):
    m, _ = a.shape
    nb, ko, bn = w3.shape
    tm = _tile(m, (512, 256, 128))
    tko = _tile(ko, (1024, 512, 256, 128))
    gb = _blocks_per_step(nb, lambda g: g * tko * bn * w3.dtype.itemsize <= MM_WEIGHT_BLOCK)
    ns = nb // gb

    def body(a_ref, w_ref, o_ref, acc_ref):
        j = pl.program_id(2)

        @pl.when(j == 0)
        def _():
            acc_ref[...] = jnp.zeros_like(acc_ref)

        part = _dot_nt(a_ref[:, :bn], w_ref[0])
        for g in range(1, gb):
            part += _dot_nt(a_ref[:, g * bn:(g + 1) * bn], w_ref[g])
        acc_ref[...] += part

        @pl.when(j == ns - 1)
        def _():
            o_ref[...] = acc_ref[...].astype(o_ref.dtype)

    return pl.pallas_call(
        body, name=name, grid=(m // tm, ko // tko, ns),
        in_specs=[pl.BlockSpec((tm, gb * bn), lambda i, o, j: (i, j)),
                  pl.BlockSpec((gb, tko, bn), lambda i, o, j: (j, o, 0))],
        out_specs=pl.BlockSpec((tm, tko), lambda i, o, j: (i, o)),
        out_shape=jax.ShapeDtypeStruct((m, ko), out_dtype),
        scratch_shapes=[pltpu.VMEM((tm, tko), f32)],
        compiler_params=_cparams("parallel", "arbitrary", "arbitrary"),
    )(a, w3)


def mm_tn(a, dy, ncb, out_dtype, name):
    l, ka = a.shape
    _, n = dy.shape
    bn = n // ncb
    tl = _tile(l, (1024, 512, 256, 128))
    tka = _tile(ka, (512, 256, 128))
    tn = _tile(bn, (1024, 896, 512, 256, 128))
    per = bn // tn
    gb = _blocks_per_step(ncb, lambda g: g == 1 or (per == 1 and g * bn <= MM_WIDE))
    nl = l // tl

    def body(a_ref, dy_ref, o_ref, acc_ref):
        s = pl.program_id(2)

        @pl.when(s == 0)
        def _():
            acc_ref[...] = jnp.zeros_like(acc_ref)

        acc_ref[...] += _dot_tn(a_ref[...], dy_ref[...])

        @pl.when(s == nl - 1)
        def _():
            for g in range(gb):
                o_ref[g] = acc_ref[:, g * tn:(g + 1) * tn].astype(o_ref.dtype)

    return pl.pallas_call(
        body, name=name, grid=(ka // tka, n // (gb * tn), nl),
        in_specs=[pl.BlockSpec((tl, tka), lambda i, j, s: (s, i)),
                  pl.BlockSpec((tl, gb * tn), lambda i, j, s: (s, j))],
        out_specs=pl.BlockSpec((gb, tka, tn), lambda i, j, s: (j // per, i, j % per)),
        out_shape=jax.ShapeDtypeStruct((ncb, ka, bn), out_dtype),
        scratch_shapes=[pltpu.VMEM((tka, gb * tn), f32)],
        compiler_params=_cparams("parallel", "parallel", "arbitrary"),
    )(a, dy)


def mod_part(c_all, w_mod, b_cols):
    nl, d, cols = w_mod.shape

    def body(c_ref, w_ref, b_ref, o_ref):
        cond = _silu(c_ref[...]).astype(bf16)
        o_ref[...] = _dot(cond, w_ref[...].astype(bf16)) + b_ref[...]

    return pl.pallas_call(
        body, name="mod_part", grid=(nl,),
        in_specs=[pl.BlockSpec((N_DEV, d), lambda l: (0, 0)),
                  pl.BlockSpec((None, d, cols), lambda l: (l, 0, 0)),
                  pl.BlockSpec((None, 1, cols), lambda l: (l, 0, 0))],
        out_specs=pl.BlockSpec((None, N_DEV, cols), lambda l: (l, 0, 0)),
        out_shape=jax.ShapeDtypeStruct((nl, N_DEV, cols), f32),
        compiler_params=_cparams("arbitrary"),
    )(c_all, w_mod, b_cols.reshape(nl, 1, cols))


def _row_tile(l):
    return _tile(l, (512, 256, 128))


def _entry_rows(xv, g_ref, sh_ref, sc_ref, h_ref, ht_ref):
    r = lax.rsqrt(jnp.mean(xv * xv, axis=-1, keepdims=True) + EPS)
    h = xv * r * (g_ref[...] * (1.0 + sc_ref[...])) + sh_ref[...]
    h_ref[...] = h.astype(h_ref.dtype)
    ht_ref[...] = jnp.transpose(h).astype(ht_ref.dtype)


def prenorm_fwd(x, g, shift, scale, name):
    l, d = x.shape
    tm = _row_tile(l)

    def body(x_ref, g_ref, sh_ref, sc_ref, h_ref, ht_ref):
        _entry_rows(x_ref[...], g_ref, sh_ref, sc_ref, h_ref, ht_ref)

    return pl.pallas_call(
        body, name=name, grid=(l // tm,),
        in_specs=[pl.BlockSpec((tm, d), lambda i: (i, 0)), _row(d), _row(d), _row(d)],
        out_specs=[pl.BlockSpec((tm, d), lambda i: (i, 0)), pl.BlockSpec((d, tm), lambda i: (0, i))],
        out_shape=[jax.ShapeDtypeStruct((l, d), bf16), jax.ShapeDtypeStruct((d, l), bf16)],
        compiler_params=_cparams("parallel"),
    )(x, g, shift, scale)


def post_prenorm_fwd(x, y, gate, g_post, g_pre, shift, scale, name):
    l, d = x.shape
    tm = _row_tile(l)

    def body(x_ref, y_ref, gate_ref, gp_ref, g_ref, sh_ref, sc_ref, o_ref, h_ref, ht_ref):
        yv = y_ref[...]
        r = lax.rsqrt(jnp.mean(yv * yv, axis=-1, keepdims=True) + EPS)
        xv = x_ref[...] + gate_ref[...] * (yv * r * gp_ref[...])
        o_ref[...] = xv
        _entry_rows(xv, g_ref, sh_ref, sc_ref, h_ref, ht_ref)

    blk = pl.BlockSpec((tm, d), lambda i: (i, 0))
    return pl.pallas_call(
        body, name=name, grid=(l // tm,),
        in_specs=[blk, blk] + [_row(d)] * 5, out_specs=[blk, blk, pl.BlockSpec((d, tm), lambda i: (0, i))],
        out_shape=[jax.ShapeDtypeStruct((l, d), f32), jax.ShapeDtypeStruct((l, d), bf16),
                   jax.ShapeDtypeStruct((d, l), bf16)],
        compiler_params=_cparams("parallel"),
    )(x, y, gate, g_post, g_pre, shift, scale)


def _post_bwd_rows(dxv, yv, r, gate, gv, dy_ref, dgate_ref, dg_ref):
    yn = yv * r
    dgate_ref[...] += jnp.sum(dxv * yn * gv, axis=0, keepdims=True)
    dyg = dxv * gate
    dg_ref[...] += jnp.sum(dyg * yn, axis=0, keepdims=True)
    dyn = dyg * gv
    dy_ref[...] = (r * (dyn - yn * jnp.mean(dyn * yn, axis=-1, keepdims=True))).astype(dy_ref.dtype)


def final_loss(x, y, gate, g, target):
    l, d = x.shape
    tm = _row_tile(l)

    def body(x_ref, y_ref, gate_ref, g_ref, t_ref, dx_ref, loss_ref, dy_ref, dgate_ref, dg_ref):
        @pl.when(pl.program_id(0) == 0)
        def _():
            loss_ref[...] = jnp.zeros_like(loss_ref)
            dgate_ref[...] = jnp.zeros_like(dgate_ref)
            dg_ref[...] = jnp.zeros_like(dg_ref)

        yv, gate, gv = y_ref[...], gate_ref[...], g_ref[...]
        r = lax.rsqrt(jnp.mean(yv * yv, axis=-1, keepdims=True) + EPS)
        diff = x_ref[...] + gate * (yv * r * gv) - t_ref[...]
        dxv = diff * (1.0 / d)
        dx_ref[...] = dxv
        loss_ref[...] += jnp.sum(diff * diff)
        _post_bwd_rows(dxv, yv, r, gate, gv, dy_ref, dgate_ref, dg_ref)

    blk = pl.BlockSpec((tm, d), lambda i: (i, 0))
    return pl.pallas_call(
        body, name="final_loss", grid=(l // tm,),
        in_specs=[blk, blk, _row(d), _row(d), blk],
        out_specs=[blk, pl.BlockSpec((SUBLANES, HEAD), lambda i: (0, 0)), blk, _row(d), _row(d)],
        out_shape=[jax.ShapeDtypeStruct((l, d), f32), jax.ShapeDtypeStruct((SUBLANES, HEAD), f32),
                   jax.ShapeDtypeStruct((l, d), bf16), jax.ShapeDtypeStruct((1, d), f32), jax.ShapeDtypeStruct((1, d), f32)],
        compiler_params=_cparams("arbitrary"),
    )(x, y, gate, g, target)


def prenorm_bwd(dh, x, dx_next, g, scale, name):
    l, d = x.shape
    tm = _row_tile(l)

    def body(dh_ref, x_ref, dxn_ref, g_ref, sc_ref, dx_ref, dsh_ref, dsc_ref, dg_ref):
        @pl.when(pl.program_id(0) == 0)
        def _():
            dsh_ref[...] = jnp.zeros_like(dsh_ref)
            dsc_ref[...] = jnp.zeros_like(dsc_ref)
            dg_ref[...] = jnp.zeros_like(dg_ref)

        xv, dhv, gv, sc1 = x_ref[...], dh_ref[...], g_ref[...], 1.0 + sc_ref[...]
        r = lax.rsqrt(jnp.mean(xv * xv, axis=-1, keepdims=True) + EPS)
        xn = xv * r
        dhx = dhv * xn
        dsh_ref[...] += jnp.sum(dhv, axis=0, keepdims=True)
        dsc_ref[...] += jnp.sum(dhx * gv, axis=0, keepdims=True)
        dg_ref[...] += jnp.sum(dhx * sc1, axis=0, keepdims=True)
        dxn = dhv * (gv * sc1)
        dx_ref[...] = dxn_ref[...] + r * (dxn - xn * jnp.mean(dxn * xn, axis=-1, keepdims=True))

    blk = pl.BlockSpec((tm, d), lambda i: (i, 0))
    return pl.pallas_call(
        body, name=name, grid=(l // tm,),
        in_specs=[blk, blk, blk, _row(d), _row(d)], out_specs=[blk, _row(d), _row(d), _row(d)],
        out_shape=[jax.ShapeDtypeStruct((l, d), f32)] + [jax.ShapeDtypeStruct((1, d), f32)] * 3,
        compiler_params=_cparams("arbitrary"),
    )(dh, x, dx_next, g, scale)


def prenorm_post_bwd(dh, x, dx_next, g, scale, y, gate, g_post, name):
    l, d = x.shape
    tm = _tile(l, (256, 128))

    def body(dh_ref, x_ref, dxn_ref, g_ref, sc_ref, y_ref, gate_ref, gp_ref,
             dx_ref, dsh_ref, dsc_ref, dg_ref, dy_ref, dgate_ref, dgp_ref):
        @pl.when(pl.program_id(0) == 0)
        def _():
            for ref in (dsh_ref, dsc_ref, dg_ref, dgate_ref, dgp_ref):
                ref[...] = jnp.zeros_like(ref)

        xv, dhv, gv, sc1 = x_ref[...], dh_ref[...], g_ref[...], 1.0 + sc_ref[...]
        r = lax.rsqrt(jnp.mean(xv * xv, axis=-1, keepdims=True) + EPS)
        xn = xv * r
        dhx = dhv * xn
        dsh_ref[...] += jnp.sum(dhv, axis=0, keepdims=True)
        dsc_ref[...] += jnp.sum(dhx * gv, axis=0, keepdims=True)
        dg_ref[...] += jnp.sum(dhx * sc1, axis=0, keepdims=True)
        dxn = dhv * (gv * sc1)
        dxv = dxn_ref[...] + r * (dxn - xn * jnp.mean(dxn * xn, axis=-1, keepdims=True))
        dx_ref[...] = dxv
        yv = y_ref[...]
        ry = lax.rsqrt(jnp.mean(yv * yv, axis=-1, keepdims=True) + EPS)
        _post_bwd_rows(dxv, yv, ry, gate_ref[...], gp_ref[...], dy_ref, dgate_ref, dgp_ref)

    blk = pl.BlockSpec((tm, d), lambda i: (i, 0))
    row = jax.ShapeDtypeStruct((1, d), f32)
    return pl.pallas_call(
        body, name=name, grid=(l // tm,),
        in_specs=[blk, blk, blk, _row(d), _row(d), blk, _row(d), _row(d)],
        out_specs=[blk, _row(d), _row(d), _row(d), blk, _row(d), _row(d)],
        out_shape=[jax.ShapeDtypeStruct((l, d), f32), row, row, row, jax.ShapeDtypeStruct((l, d), bf16), row, row],
        compiler_params=_cparams("arbitrary"),
    )(dh, x, dx_next, g, scale, y, gate, g_post)


def _tril_mask():
    r = lax.broadcasted_iota(jnp.int32, (HEAD, HEAD), 0)
    c = lax.broadcasted_iota(jnp.int32, (HEAD, HEAD), 1)
    return r >= c


def sgu_fwd(proj, norm_g, w_s, b_s):
    l = proj.shape[0]
    nh = w_s.shape[0]
    wa = nh * HEAD

    def body(au_ref, av_ref, az_ref, ng_ref, w_ref, b_ref, o_ref):
        tril = _tril_mask()
        for h in range(nh):
            sl = slice(h * HEAD, (h + 1) * HEAD)
            gv = _gelu(av_ref[:, sl].astype(f32))
            r = lax.rsqrt(jnp.mean(gv * gv, axis=-1, keepdims=True) + EPS)
            vh = gv * r * ng_ref[:, sl]
            wm = jnp.where(tril, w_ref[h], 0.0).astype(bf16)
            s = _dot(wm, vh.astype(bf16)) + b_ref[h]
            o_ref[:, sl] = (_gelu(au_ref[:, sl].astype(f32)) * s * _silu(az_ref[:, sl].astype(f32))).astype(o_ref.dtype)

    def col(j):
        return pl.BlockSpec((HEAD, wa), lambda n: (n, j))

    return pl.pallas_call(
        body, name="sgu_fwd", grid=(l // HEAD,),
        in_specs=[col(0), col(1), col(2), _row(wa),
                  pl.BlockSpec((nh, HEAD, HEAD), lambda n: (0, 0, 0)), pl.BlockSpec((nh, HEAD, 1), lambda n: (0, 0, 0))],
        out_specs=pl.BlockSpec((HEAD, wa), lambda n: (n, 0)),
        out_shape=jax.ShapeDtypeStruct((l, 2 * wa), bf16),
        compiler_params=_cparams("parallel"),
    )(proj, proj, proj, norm_g, w_s, b_s)


def sgu_bwd(proj, dcat, norm_g, w_s, b_s):
    l = proj.shape[0]
    nh = w_s.shape[0]
    wa = nh * HEAD

    def body(au_ref, av_ref, az_ref, do_ref, ng_ref, w_ref, b_ref, da_ref, dw_ref, db_ref, dng_ref):
        @pl.when(pl.program_id(0) == 0)
        def _():
            dw_ref[...] = jnp.zeros_like(dw_ref)
            db_ref[...] = jnp.zeros_like(db_ref)
            dng_ref[...] = jnp.zeros_like(dng_ref)

        tril = _tril_mask()
        for h in range(nh):
            sl = slice(h * HEAD, (h + 1) * HEAD)
            au, av, az = au_ref[:, sl].astype(f32), av_ref[:, sl].astype(f32), az_ref[:, sl].astype(f32)
            ng = ng_ref[:, sl]
            gv = _gelu(av)
            r = lax.rsqrt(jnp.mean(gv * gv, axis=-1, keepdims=True) + EPS)
            gvn = gv * r
            vh = (gvn * ng).astype(bf16)
            wm = jnp.where(tril, w_ref[h], 0.0).astype(bf16)
            s = _dot(wm, vh) + b_ref[h]
            gu, sz = _gelu(au), _silu(az)
            dov = do_ref[:, sl].astype(f32)
            da_ref[:, sl] = (dov * s * sz * _gelu_grad(au)).astype(da_ref.dtype)
            da_ref[:, 2 * wa + h * HEAD:2 * wa + (h + 1) * HEAD] = (dov * gu * s * _silu_grad(az)).astype(da_ref.dtype)
            ds = dov * gu * sz
            db_ref[h] += jnp.sum(ds, axis=-1, keepdims=True)
            dsb = ds.astype(bf16)
            dw_ref[h] += jnp.where(tril, _dot_nt(dsb, vh), 0.0)
            dvh = _dot_tn(wm, dsb)
            dng_ref[:, sl] += jnp.sum(dvh * gvn, axis=0, keepdims=True)
            dgvn = dvh * ng
            dgv = r * (dgvn - gvn * jnp.mean(dgvn * gvn, axis=-1, keepdims=True))
            da_ref[:, wa + h * HEAD:wa + (h + 1) * HEAD] = (dgv * _gelu_grad(av)).astype(da_ref.dtype)

    def col(j):
        return pl.BlockSpec((HEAD, wa), lambda n: (n, j))

    whole_w = pl.BlockSpec((nh, HEAD, HEAD), lambda n: (0, 0, 0))
    whole_b = pl.BlockSpec((nh, HEAD, 1), lambda n: (0, 0, 0))
    return pl.pallas_call(
        body, name="sgu_bwd", grid=(l // HEAD,),
        in_specs=[col(0), col(1), col(2), col(0), _row(wa), whole_w, whole_b],
        out_specs=[pl.BlockSpec((HEAD, 3 * wa), lambda n: (n, 0)), whole_w, whole_b, _row(wa)],
        out_shape=[jax.ShapeDtypeStruct(proj.shape, bf16), jax.ShapeDtypeStruct((nh, HEAD, HEAD), f32),
                   jax.ShapeDtypeStruct((nh, HEAD, 1), f32), jax.ShapeDtypeStruct((1, wa), f32)],
        compiler_params=_cparams("arbitrary"),
    )(proj, proj, proj, dcat, norm_g, w_s, b_s)


_LOG2E = 1.0 / math.log(2.0)
_SB_EXP_CLAMP = 120.0


def _sb_scores(q, k, scale):
    z = _dot_nt(q, k) * (scale * _LOG2E)
    return z, jnp.maximum(z, jnp.log2(1.0 + jnp.exp2(jnp.minimum(z, _SB_EXP_CLAMP))))


SB_KEYS = 256


def _sb_sum_matrix(tri, kb):
    s = lax.broadcasted_iota(jnp.int32, (2 * kb, kb + HEAD), 0) % kb
    j = lax.broadcasted_iota(jnp.int32, (2 * kb, kb + HEAD), 1)
    return jnp.where(jnp.logical_or(j >= kb, tri(s, j)), 1.0, 0.0).astype(bf16)


def _sb_sums(x, sums):
    kb = x.shape[1]
    c2 = _dot(jnp.concatenate(_split_bf16(x), axis=1), sums)
    return c2[:, :kb], c2[:, kb:]


def _sb_wide(v, kb):
    return jnp.concatenate([v] * (kb // HEAD), axis=1) if kb > HEAD else v


def _sb_q_tile(l, most=512):
    return _tile(l, tuple(t for t in (1024, 512, 256, 128) if t <= most))


def _sb_band_levels(band):
    return _tile(band, (4, 2, 1))


def _sb_heads_per_step(nh, most):
    return _tile(nh, tuple(h for h in (4, 2) if h <= most))


def sb_fwd(proj, mixed, nh):
    l = proj.shape[0]
    wb = nh * HEAD
    tq = _sb_q_tile(l, 1024)
    kb = min(SB_KEYS, tq)
    band = tq // kb
    hp = _sb_heads_per_step(nh, 2)
    levels = _sb_band_levels(band)
    scale = 1.0 / math.sqrt(HEAD)
    qc, kc, vc, zc = 3 * nh, 4 * nh, 5 * nh, 6 * nh

    def body(q_ref, k_ref, v_ref, bz_ref, mixed_ref, o_ref, att_ref, tot_ref):
        del mixed_ref
        i = pl.program_id(1)
        sums = _sb_sum_matrix(lambda s, j: s > j, kb)
        t_pos = i * tq + lax.broadcasted_iota(jnp.int32, (tq, kb), 0)
        s_off = lax.broadcasted_iota(jnp.int32, (tq, kb), 1)

        def step(j, carry, masked, row0=0):
            rows = pl.ds(pl.multiple_of(j * kb, kb), kb)
            out = []
            for e in range(hp):
                acc, tot = carry[e]
                sl = slice(e * HEAD, (e + 1) * HEAD)
                z, sp = _sb_scores(q_ref[row0:, sl], k_ref[rows, sl], scale)
                lb = z - sp
                if masked:
                    mask = s_off[row0:] + j * kb < t_pos[row0:]
                    sp = jnp.where(mask, sp, 0.0)
                later, total = _sb_sums(sp, sums)
                w = jnp.exp2(lb + _sb_wide(tot[row0:], kb) - later)
                if masked:
                    w = jnp.where(mask, w, 0.0)
                new = (acc[row0:] + _dot(w.astype(bf16), v_ref[rows, sl]), tot[row0:] - total)
                out.append(tuple(jnp.concatenate([old[:row0], upd]) if row0 else upd for old, upd in zip(carry[e], new)))
            return tuple(out)

        zero = jnp.zeros((tq, HEAD), f32)
        carry = ((zero, zero),) * hp
        for lv in reversed(range(levels)):
            carry = lax.fori_loop(
                0, band // levels,
                lambda t, c, lv=lv: step(band * i + (lv + 1) * (band // levels) - 1 - t, c, True, lv * (tq // levels)), carry)
        carry = lax.fori_loop(0, band * i, lambda t, c: step(band * i - 1 - t, c, False), carry)
        for e in range(hp):
            acc, tot = carry[e]
            sl = slice(e * HEAD, (e + 1) * HEAD)
            att_ref[:, sl] = acc.astype(att_ref.dtype)
            o_ref[:, sl] = (acc * _silu(bz_ref[:, sl].astype(f32))).astype(o_ref.dtype)
            tot_ref[e] = tot[:, :1]

    blk = lambda c0: pl.BlockSpec((tq, hp * HEAD), lambda g, i: (i, c0 // hp + g))
    head = lambda c0: pl.BlockSpec((l, hp * HEAD), lambda g, i: (0, c0 // hp + g))
    return pl.pallas_call(
        body, name="sb_fwd", grid=(nh // hp, l // tq),
        in_specs=[blk(qc), head(kc), head(vc), blk(zc), pl.BlockSpec(memory_space=pl.ANY)],
        out_specs=[blk(mixed.shape[1] // HEAD - nh), blk(0), pl.BlockSpec((hp, tq, 1), lambda g, i: (g, i, 0))],
        out_shape=[jax.ShapeDtypeStruct(mixed.shape, bf16), jax.ShapeDtypeStruct((l, wb), bf16),
                   jax.ShapeDtypeStruct((nh, l, 1), f32)],
        input_output_aliases={4: 0},
        compiler_params=_cparams("parallel", "arbitrary"),
    )(proj, proj, proj, proj, mixed)


def sb_bwd(proj, dcat, att, tot, dproj, nh):
    l = proj.shape[0]
    wb = nh * HEAD
    tq = _sb_q_tile(l, 1024)
    kb = min(SB_KEYS, tq)
    band = tq // kb
    nq = l // tq
    hp = _sb_heads_per_step(nh, 2)
    levels = _sb_band_levels(band)
    scale = 1.0 / math.sqrt(HEAD)
    qc, kc, vc, zc = 3 * nh, 4 * nh, 5 * nh, 6 * nh

    def body(q_ref, k_ref, v_ref, bz_ref, do_ref, att_ref, tot_ref, dproj_in, dproj_ref, dk_acc, dv_acc, dob_ref,
             tile_ref, head_ref, sems):
        del dproj_in
        g, i = pl.program_id(0), pl.program_id(1)

        def put(src, row0, c0, k):
            cols = pl.ds(pl.multiple_of((c0 + g * hp) * HEAD, HEAD), hp * HEAD)
            cp = pltpu.make_async_copy(src, dproj_ref.at[pl.ds(row0, src.shape[0]), cols], sems.at[k])
            cp.start()
            return cp

        @pl.when(i == 0)
        def _():
            dk_acc[...] = jnp.zeros_like(dk_acc)
            dv_acc[...] = jnp.zeros_like(dv_acc)

        my_rows = pl.multiple_of(i * tq, tq)
        bz = bz_ref[...].astype(f32)
        dov = do_ref[...].astype(f32)
        tile_ref[0] = (dov * att_ref[...].astype(f32) * _silu_grad(bz)).astype(bf16)
        dbz_copy = put(tile_ref.at[0], my_rows, zc, 0)
        dob_ref[...] = (dov * _silu(bz)).astype(bf16)
        upto = _sb_sum_matrix(lambda s, j: s <= j, kb)
        before = _sb_sum_matrix(lambda j, s: j < s, kb)
        t_pos = i * tq + lax.broadcasted_iota(jnp.int32, (tq, kb), 0)
        s_off = lax.broadcasted_iota(jnp.int32, (tq, kb), 1)

        def step(j, carry, masked, row0=0):
            rows = pl.ds(pl.multiple_of(j * kb, kb), kb)
            out = []
            for h in range(hp):
                dq, sp_seen, e_seen = (c[row0:] for c in carry[h])
                sl = slice(h * HEAD, (h + 1) * HEAD)
                q, kj, vj, dob = q_ref[row0:, sl], k_ref[rows, sl], v_ref[rows, sl], dob_ref[row0:, sl]
                z, sp = _sb_scores(q, kj, scale)
                lb = z - sp
                if masked:
                    mask = s_off[row0:] + j * kb < t_pos[row0:]
                    sp = jnp.where(mask, sp, 0.0)
                sp_upto, sp_total = _sb_sums(sp, upto)
                w = jnp.exp2(lb + _sb_wide(sp_seen, kb) + sp_upto)
                if masked:
                    w = jnp.where(mask, w, 0.0)
                dv_acc[rows, sl] += _dot_tn(w.astype(bf16), dob)
                e = _dot_nt(dob, vj) * w
                e_before, e_total = _sb_sums(e, before)
                dz = (e - (e + _sb_wide(e_seen, kb) + e_before) * jnp.exp2(lb)) * scale
                if masked:
                    dz = jnp.where(mask, dz, 0.0)
                dz = dz.astype(bf16)
                dk_acc[rows, sl] += _dot_tn(dz, q)
                new = (dq + _dot(dz, kj), sp_seen + sp_total, e_seen + e_total)
                out.append(tuple(jnp.concatenate([old[:row0], upd]) if row0 else upd for old, upd in zip(carry[h], new)))
            return tuple(out)

        zero = jnp.zeros((tq, HEAD), f32)
        init = tuple((zero, jnp.broadcast_to(tot_ref[h], (tq, HEAD)), zero) for h in range(hp))
        carry = lax.fori_loop(0, band * i, lambda j, c: step(j, c, False), init)
        for lv in range(levels):
            carry = lax.fori_loop(
                0, band // levels,
                lambda t, c, lv=lv: step(band * i + lv * (band // levels) + t, c, True, lv * (tq // levels)), carry)
        for h in range(hp):
            tile_ref[1, :, h * HEAD:(h + 1) * HEAD] = carry[h][0].astype(bf16)
        dq_copy = put(tile_ref.at[1], my_rows, qc, 1)
        dbz_copy.wait()
        dq_copy.wait()

        @pl.when(i == nq - 1)
        def _():
            head_ref[0] = dk_acc[...].astype(bf16)
            head_ref[1] = dv_acc[...].astype(bf16)
            copies = [put(head_ref.at[0], 0, kc, 2), put(head_ref.at[1], 0, vc, 3)]
            for cp in copies:
                cp.wait()

    blk = lambda c0: pl.BlockSpec((tq, hp * HEAD), lambda g, i: (i, c0 // hp + g))
    head = lambda c0: pl.BlockSpec((l, hp * HEAD), lambda g, i: (0, c0 // hp + g))
    any_spec = pl.BlockSpec(memory_space=pl.ANY)
    return pl.pallas_call(
        body, name="sb_bwd", grid=(nh // hp, nq),
        in_specs=[blk(qc), head(kc), head(vc), blk(zc), blk(nh), blk(0),
                  pl.BlockSpec((hp, tq, 1), lambda g, i: (g, i, 0)), any_spec],
        out_specs=any_spec, out_shape=jax.ShapeDtypeStruct(dproj.shape, bf16), input_output_aliases={7: 0},
        scratch_shapes=[pltpu.VMEM((l, hp * HEAD), f32), pltpu.VMEM((l, hp * HEAD), f32),
                        pltpu.VMEM((tq, hp * HEAD), bf16), pltpu.VMEM((2, tq, hp * HEAD), bf16),
                        pltpu.VMEM((2, l, hp * HEAD), bf16), pltpu.SemaphoreType.DMA((4,))],
        compiler_params=_cparams("parallel", "arbitrary"),
    )(proj, proj, proj, proj, dcat, att, tot, dproj)


def _disc(lr, li, ldt):
    dt = jnp.exp(ldt)
    mag = jnp.exp(lr * dt)
    a_re = mag * jnp.cos(li * dt)
    a_im = mag * jnp.sin(li * dt)
    den = lr * lr + li * li
    nr = a_re - 1.0
    return a_re, a_im, (nr * lr + a_im * li) / den, (a_im * lr - nr * li) / den


def s5_params_fwd(lr, li, ldt, bt_re, bt_im):
    g, c, p = bt_re.shape

    def body(lr_ref, li_ref, ldt_ref, br_ref, bi_ref, ar_ref, ai_ref, bbr_ref, bbi_ref):
        a_re, a_im, cr, ci = _disc(lr_ref[...], li_ref[...], ldt_ref[...])
        ar_ref[...] = a_re
        ai_ref[...] = a_im
        for k in range(c):
            br, bi = br_ref[:, k, :], bi_ref[:, k, :]
            bbr_ref[:, k, :] = cr * br - ci * bi
            bbi_ref[:, k, :] = cr * bi + ci * br

    return pl.pallas_call(
        body, name="s5_params_fwd",
        out_shape=[jax.ShapeDtypeStruct((g, p), f32)] * 2 + [jax.ShapeDtypeStruct((g, c, p), f32)] * 2,
    )(lr, li, ldt, bt_re, bt_im)


def s5_params_bwd(lr, li, ldt, bt_re, bt_im, da_re, da_im, dbbt_re, dbbt_im):
    g, c, p = bt_re.shape

    def body(lr_ref, li_ref, ldt_ref, br_ref, bi_ref, dar_ref, dai_ref, dbbr_ref, dbbi_ref,
             dlr_ref, dli_ref, dldt_ref, dbr_ref, dbi_ref):
        (a_re, a_im, cr, ci), vjp = jax.vjp(_disc, lr_ref[...], li_ref[...], ldt_ref[...])
        dcr = jnp.zeros((g, p), f32)
        dci = jnp.zeros((g, p), f32)
        for k in range(c):
            br, bi = br_ref[:, k, :], bi_ref[:, k, :]
            dr, di = dbbr_ref[:, k, :], dbbi_ref[:, k, :]
            dcr += dr * br + di * bi
            dci += di * br - dr * bi
            dbr_ref[:, k, :] = cr * dr + ci * di
            dbi_ref[:, k, :] = cr * di - ci * dr
        dlr, dli, dldt = vjp((dar_ref[...], dai_ref[...], dcr, dci))
        dlr_ref[...] = dlr
        dli_ref[...] = dli
        dldt_ref[...] = dldt

    return pl.pallas_call(
        body, name="s5_params_bwd",
        out_shape=[jax.ShapeDtypeStruct((g, p), f32)] * 2 + [jax.ShapeDtypeStruct((g, 1), f32)]
        + [jax.ShapeDtypeStruct((g, c, p), f32)] * 2,
    )(lr, li, ldt, bt_re, bt_im, da_re, da_im, dbbt_re, dbbt_im)


def _cmul(ar, ai, br, bi):
    return ar * br - ai * bi, ar * bi + ai * br


def _power_tables(ar, ai):
    rows = lax.broadcasted_iota(jnp.int32, (SUBLANES, ar.shape[1]), 0)
    pr = jnp.zeros((SUBLANES, ar.shape[1]), f32)
    pi = jnp.zeros((SUBLANES, ar.shape[1]), f32)
    cr, ci = ar, ai
    pows = {}
    for r in range(SUBLANES):
        pows[r + 1] = (cr, ci)
        pr = jnp.where(rows == r, cr, pr)
        pi = jnp.where(rows == r, ci, pi)
        cr, ci = _cmul(cr, ci, ar, ai)
    return [pows[1], pows[2], pows[4]], pr, pi


def _ssm_time_tile(l):
    return _tile(l, (2048, 1024, 512, 256, 128))


def ssm_fwd(u, bre3, bim3, cre3, cimn3, a_re, a_im, d_skip):
    l, w = u.shape[0], d_skip.shape[1]
    nj = w // HEAD
    ns = STATES_PER_LANE_BLOCK
    tt = _ssm_time_tile(l)

    def body(u_ref, bre_ref, bim_ref, cre_ref, cim_ref, ar_ref, ai_ref, d_ref, y_ref, hr_ref, hi_ref, cr_ref, ci_ref):
        @pl.when(pl.program_id(1) == 0)
        def _():
            cr_ref[...] = jnp.zeros_like(cr_ref)
            ci_ref[...] = jnp.zeros_like(ci_ref)

        uv = u_ref[...]
        hr_ref[...] = _dot(uv, bre_ref[...])
        hi_ref[...] = _dot(uv, bim_ref[...])
        steps, pr, pi = _power_tables(ar_ref[...], ai_ref[...])
        rows = lax.broadcasted_iota(jnp.int32, (SUBLANES, ns), 0)
        steps = [(jnp.where(rows >= d, sr_, 0.0), jnp.where(rows >= d, si_, 0.0)) for d, (sr_, si_) in zip((1, 2, 4), steps)]

        def blk(b, carry):
            cr, ci = carry
            sl = pl.ds(pl.multiple_of(b * SUBLANES, SUBLANES), SUBLANES)
            xr, xi = hr_ref[sl, :], hi_ref[sl, :]
            for d, (sr_, si_) in zip((1, 2, 4), steps):
                mr, mi = _cmul(sr_, si_, pltpu.roll(xr, d, axis=0), pltpu.roll(xi, d, axis=0))
                xr, xi = xr + mr, xi + mi
            mr, mi = _cmul(pr, pi, cr, ci)
            xr, xi = xr + mr, xi + mi
            hr_ref[sl, :] = xr
            hi_ref[sl, :] = xi
            return xr[SUBLANES - 1:, :], xi[SUBLANES - 1:, :]

        cr, ci = lax.fori_loop(0, tt // SUBLANES, blk, (cr_ref[...], ci_ref[...]))
        cr_ref[...] = cr
        ci_ref[...] = ci
        y = _dot(hr_ref[...].astype(bf16), cre_ref[...]) + _dot(hi_ref[...].astype(bf16), cim_ref[...])
        y_ref[...] = y + d_ref[...] * uv.astype(f32)

    lane = pl.BlockSpec((tt, HEAD), lambda j, i: (i, j))
    st = pl.BlockSpec((tt, ns), lambda j, i: (i, j))
    b3 = pl.BlockSpec((None, HEAD, ns), lambda j, i: (j, 0, 0))
    c3 = pl.BlockSpec((None, ns, HEAD), lambda j, i: (j, 0, 0))
    arow = pl.BlockSpec((1, ns), lambda j, i: (0, j))
    return pl.pallas_call(
        body, name="ssm_fwd", grid=(nj, l // tt),
        in_specs=[lane, b3, b3, c3, c3, arow, arow, pl.BlockSpec((1, HEAD), lambda j, i: (0, j))],
        out_specs=[lane, st, st],
        out_shape=[jax.ShapeDtypeStruct((l, w), f32), jax.ShapeDtypeStruct((l, nj * ns), f32),
                   jax.ShapeDtypeStruct((l, nj * ns), f32)],
        scratch_shapes=[pltpu.VMEM((1, ns), f32), pltpu.VMEM((1, ns), f32)],
        compiler_params=_cparams("parallel", "arbitrary"),
    )(u, bre3, bim3, cre3, cimn3, a_re, a_im, d_skip)


def ssm_bwd(dy, u, dproj, h_re, h_im, bre3, bim3, cre3, cimn3, a_re, a_im, d_skip):
    l, w = u.shape[0], d_skip.shape[1]
    nj = w // HEAD
    ns = STATES_PER_LANE_BLOCK
    tt = _ssm_time_tile(l)
    nt = l // tt

    def body(dy_ref, u_ref, dproj_ref, hr_ref, hi_ref, bre_ref, bim_ref, cre_ref, cim_ref, ar_ref, ai_ref, d_ref,
             du_ref, dd_ref, dar_ref, dai_ref, dbre_ref, dbim_ref, dcre_ref, dcim_ref, kr_ref, ki_ref, cr_ref, ci_ref,
             accr_ref, acci_ref):
        del dproj_ref
        i = pl.program_id(1)

        @pl.when(i == 0)
        def _():
            for ref in (cr_ref, ci_ref, accr_ref, acci_ref, dd_ref, dbre_ref, dbim_ref, dcre_ref, dcim_ref):
                ref[...] = jnp.zeros_like(ref)

        dyv = dy_ref[...]
        dyb = dyv.astype(bf16)
        uv = u_ref[...]
        kr_ref[...] = _dot_nt(dyb, cre_ref[...])
        ki_ref[...] = _dot_nt(dyb, cim_ref[...])
        steps, pr, pi = _power_tables(ar_ref[...], -ai_ref[...])
        rows = lax.broadcasted_iota(jnp.int32, (SUBLANES, ns), 0)
        qr = jnp.zeros((SUBLANES, ns), f32)
        qi = jnp.zeros((SUBLANES, ns), f32)
        for r in range(SUBLANES):
            qr = jnp.where(rows == r, pr[SUBLANES - 1 - r:SUBLANES - r, :], qr)
            qi = jnp.where(rows == r, pi[SUBLANES - 1 - r:SUBLANES - r, :], qi)
        nb = tt // SUBLANES
        steps = [(jnp.where(rows < SUBLANES - d, sr_, 0.0), jnp.where(rows < SUBLANES - d, si_, 0.0))
                 for d, (sr_, si_) in zip((1, 2, 4), steps)]

        def blk(t, carry):
            cr, ci, accr, acci = carry
            sl = pl.ds(pl.multiple_of((nb - 1 - t) * SUBLANES, SUBLANES), SUBLANES)
            xr, xi = kr_ref[sl, :], ki_ref[sl, :]
            for d, (sr_, si_) in zip((1, 2, 4), steps):
                mr, mi = _cmul(sr_, si_, pltpu.roll(xr, SUBLANES - d, axis=0), pltpu.roll(xi, SUBLANES - d, axis=0))
                xr, xi = xr + mr, xi + mi
            mr, mi = _cmul(qr, qi, cr, ci)
            xr, xi = xr + mr, xi + mi
            kr_ref[sl, :] = xr
            ki_ref[sl, :] = xi
            last = rows == SUBLANES - 1
            nr = jnp.where(last, cr, pltpu.roll(xr, SUBLANES - 1, axis=0))
            ni = jnp.where(last, ci, pltpu.roll(xi, SUBLANES - 1, axis=0))
            hr, hi = hr_ref[sl, :], hi_ref[sl, :]
            accr = accr + nr * hr + ni * hi
            acci = acci + ni * hr - nr * hi
            return xr[:1, :], xi[:1, :], accr, acci

        cr, ci, accr, acci = lax.fori_loop(0, nb, blk, (cr_ref[...], ci_ref[...], accr_ref[...], acci_ref[...]))
        cr_ref[...] = cr
        ci_ref[...] = ci
        accr_ref[...] = accr
        acci_ref[...] = acci
        kr, ki = kr_ref[...].astype(bf16), ki_ref[...].astype(bf16)
        du = _dot_nt(kr, bre_ref[...]) + _dot_nt(ki, bim_ref[...]) + d_ref[...] * dyv
        du_ref[...] = du.astype(du_ref.dtype)
        dd_ref[...] += jnp.sum(dyv * uv.astype(f32), axis=0, keepdims=True)
        dbre_ref[...] += _dot_tn(uv, kr)
        dbim_ref[...] += _dot_tn(uv, ki)
        dcre_ref[...] += _dot_tn(hr_ref[...].astype(bf16), dyb)
        dcim_ref[...] += _dot_tn(hi_ref[...].astype(bf16), dyb)

        @pl.when(i == nt - 1)
        def _():
            dar_ref[...] = jnp.sum(accr_ref[...], axis=0, keepdims=True)
            dai_ref[...] = jnp.sum(acci_ref[...], axis=0, keepdims=True)

    lane = pl.BlockSpec((tt, HEAD), lambda j, i: (nt - 1 - i, j))
    st = pl.BlockSpec((tt, ns), lambda j, i: (nt - 1 - i, j))
    b3 = pl.BlockSpec((None, HEAD, ns), lambda j, i: (j, 0, 0))
    c3 = pl.BlockSpec((None, ns, HEAD), lambda j, i: (j, 0, 0))
    arow = pl.BlockSpec((1, ns), lambda j, i: (0, j))
    drow = pl.BlockSpec((1, HEAD), lambda j, i: (0, j))
    return pl.pallas_call(
        body, name="ssm_bwd", grid=(nj, nt),
        in_specs=[lane, lane, pl.BlockSpec(memory_space=pl.ANY), st, st, b3, b3, c3, c3, arow, arow, drow],
        out_specs=[lane, drow, arow, arow, b3, b3, c3, c3], input_output_aliases={2: 0},
        out_shape=[jax.ShapeDtypeStruct(dproj.shape, bf16), jax.ShapeDtypeStruct((1, w), f32),
                   jax.ShapeDtypeStruct((1, nj * ns), f32), jax.ShapeDtypeStruct((1, nj * ns), f32),
                   jax.ShapeDtypeStruct((nj, HEAD, ns), f32), jax.ShapeDtypeStruct((nj, HEAD, ns), f32),
                   jax.ShapeDtypeStruct((nj, ns, HEAD), f32), jax.ShapeDtypeStruct((nj, ns, HEAD), f32)],
        scratch_shapes=[pltpu.VMEM((tt, ns), f32), pltpu.VMEM((tt, ns), f32), pltpu.VMEM((1, ns), f32),
                        pltpu.VMEM((1, ns), f32), pltpu.VMEM((SUBLANES, ns), f32), pltpu.VMEM((SUBLANES, ns), f32)],
        compiler_params=_cparams("parallel", "arbitrary"),
    )(dy, u, dproj, h_re, h_im, bre3, bim3, cre3, cimn3, a_re, a_im, d_skip)


def glu_fwd(y, z_src, w_glu, b_glu):
    l, w = y.shape
    tm = _row_tile(l)

    def body(y_ref, z_ref, w_ref, b_ref, g_ref, t_ref, o_ref):
        g = _gelu(y_ref[...])
        gb = g.astype(bf16)
        t = _dot(gb, w_ref[...]) + b_ref[...]
        g_ref[...] = gb
        t_ref[...] = t
        o_ref[...] = (g * jax.nn.sigmoid(t) * _silu(z_ref[...].astype(f32))).astype(o_ref.dtype)

    blk = pl.BlockSpec((tm, w), lambda i: (i, 0))
    return pl.pallas_call(
        body, name="glu_fwd", grid=(l // tm,),
        in_specs=[blk, pl.BlockSpec((tm, w), lambda i: (i, 1)), pl.BlockSpec((w, w), lambda i: (0, 0)), _row(w)],
        out_specs=[blk, blk, blk],
        out_shape=[jax.ShapeDtypeStruct((l, w), bf16), jax.ShapeDtypeStruct((l, w), f32),
                   jax.ShapeDtypeStruct((l, w), bf16)],
        compiler_params=_cparams("parallel"),
    )(y, z_src, w_glu, b_glu)


def glu_bwd(dout, y, t, z_src, w_glu):
    l, w = y.shape
    tm = _row_tile(l)

    def body(do_ref, y_ref, t_ref, z_ref, w_ref, dy_ref, dz_ref, dt_ref, db_ref):
        @pl.when(pl.program_id(0) == 0)
        def _():
            db_ref[...] = jnp.zeros_like(db_ref)

        yv, zv, dov = y_ref[...], z_ref[...].astype(f32), do_ref[...]
        g = _gelu(yv)
        sg = jax.nn.sigmoid(t_ref[...])
        dy2 = dov * _silu(zv)
        dz_ref[...] = (dov * g * sg * _silu_grad(zv)).astype(dz_ref.dtype)
        dt = dy2 * g * sg * (1.0 - sg)
        dtb = dt.astype(bf16)
        dt_ref[...] = dtb
        db_ref[...] += jnp.sum(dt, axis=0, keepdims=True)
        dg = dy2 * sg + _dot_nt(dtb, w_ref[...])
        dy_ref[...] = dg * _gelu_grad(yv)

    blk = pl.BlockSpec((tm, w), lambda i: (i, 0))
    return pl.pallas_call(
        body, name="glu_bwd", grid=(l // tm,),
        in_specs=[blk, blk, blk, pl.BlockSpec((tm, w), lambda i: (i, 1)), pl.BlockSpec((w, w), lambda i: (0, 0))],
        out_specs=[blk, pl.BlockSpec((tm, w), lambda i: (i, 1)), blk, _row(w)],
        out_shape=[jax.ShapeDtypeStruct((l, w), f32), jax.ShapeDtypeStruct((l, 2 * w), bf16),
                   jax.ShapeDtypeStruct((l, w), bf16), jax.ShapeDtypeStruct((1, w), f32)],
        compiler_params=_cparams("arbitrary"),
    )(dout, y, t, z_src, w_glu)


def _adamw(w, g, m, v):
    m = ADAM_B1 * m + (1.0 - ADAM_B1) * g
    v = ADAM_B2 * v + (1.0 - ADAM_B2) * (g * g)
    m_hat = m / (1.0 - ADAM_B1 ** ADAM_STEP)
    v_hat = v / (1.0 - ADAM_B2 ** ADAM_STEP)
    return -ADAM_LR * (m_hat / (jnp.sqrt(v_hat) + ADAM_EPS) + ADAM_WD * w), m, v


def adam_reduce(pieces, w, m, v, name):
    r, c = w.shape
    n = pieces.shape[0]
    tr = _tile(r, (256, 128, 64, 32, 16, 8))

    def body(p_ref, w_ref, m_ref, v_ref, g_ref, d_ref, nm_ref, nv_ref):
        g = p_ref[0].astype(f32)
        for s in range(1, n):
            g = g + p_ref[s].astype(f32)
        g_ref[...] = g
        d_ref[...], nm_ref[...], nv_ref[...] = _adamw(w_ref[...], g, m_ref[...], v_ref[...])

    blk = pl.BlockSpec((tr, c), lambda i: (i, 0))
    return pl.pallas_call(
        body, name=name, grid=(r // tr,),
        in_specs=[pl.BlockSpec((n, tr, c), lambda i: (0, i, 0)), blk, blk, blk],
        out_specs=[blk] * 4, out_shape=[jax.ShapeDtypeStruct((r, c), f32)] * 4,
        compiler_params=_cparams("parallel"),
    )(pieces, w, m, v)


def adam_w_mod(cond_t, dm, w, m, v):
    nl, d, cols = w.shape
    tr = _tile(d, (512, 256, 128))

    def body(c_ref, dm_ref, w_ref, m_ref, v_ref, g_ref, d_ref, nm_ref, nv_ref):
        g = jnp.dot(c_ref[...], dm_ref[...], preferred_element_type=f32, precision=lax.Precision.HIGHEST)
        g_ref[...] = g
        d_ref[...], nm_ref[...], nv_ref[...] = _adamw(w_ref[...], g, m_ref[...], v_ref[...])

    blk = pl.BlockSpec((None, tr, cols), lambda l, i: (l, i, 0))
    return pl.pallas_call(
        body, name="adam_w_mod", grid=(nl, d // tr),
        in_specs=[pl.BlockSpec((tr, N_DEV), lambda l, i: (i, 0)), pl.BlockSpec((None, N_DEV, cols), lambda l, i: (l, 0, 0)),
                  blk, blk, blk],
        out_specs=[blk] * 4, out_shape=[jax.ShapeDtypeStruct((nl, d, cols), f32)] * 4,
        compiler_params=_cparams("parallel", "parallel"),
    )(cond_t, dm, w, m, v)


def silu_rows(c_all):
    def body(c_ref, o_ref):
        o_ref[...] = _silu(c_ref[...])

    return pl.pallas_call(body, name="silu_rows", out_shape=jax.ShapeDtypeStruct(c_all.shape, f32))(c_all)


def _block_diag(x):
    g, a, b = x.shape
    nj = g // GROUPS_PER_LANE_BLOCK
    eye = jnp.eye(GROUPS_PER_LANE_BLOCK, dtype=x.dtype)
    x5 = x.reshape(nj, GROUPS_PER_LANE_BLOCK, a, b)
    return jnp.einsum("jgab,gh->jgahb", x5, eye).reshape(nj, GROUPS_PER_LANE_BLOCK * a, GROUPS_PER_LANE_BLOCK * b)


def _diag_blocks(x, a, b):
    nj = x.shape[0]
    x5 = x.reshape(nj, GROUPS_PER_LANE_BLOCK, a, GROUPS_PER_LANE_BLOCK, b)
    eye = jnp.eye(GROUPS_PER_LANE_BLOCK, dtype=x.dtype)
    return jnp.einsum("jgahb,gh->jgab", x5, eye).reshape(nj * GROUPS_PER_LANE_BLOCK, a, b)


PACK_ROW = SUBLANES * HEAD


def _pack(parts, row_multiple=SUBLANES):
    rows = []
    for p in parts:
        flat = p.reshape(-1)
        pad = (-flat.shape[0]) % PACK_ROW
        if pad:
            flat = jnp.concatenate([flat, jnp.zeros((pad,), flat.dtype)])
        rows.append(flat.reshape(-1, HEAD))
    pad = (-sum(r.shape[0] for r in rows)) % row_multiple
    if pad:
        rows.append(jnp.zeros((pad, HEAD), rows[0].dtype))
    return jnp.concatenate(rows, axis=0)


def _unpack(packed, shapes):
    out, r0 = [], 0
    for shp in shapes:
        n = math.prod(shp)
        nr = -(-n // PACK_ROW) * SUBLANES
        out.append(packed[r0:r0 + nr].reshape(-1)[:n].reshape(shp))
        r0 += nr
    return out


def adam_small(g, w, m, v):
    r, c = w.shape

    def body(g_ref, w_ref, m_ref, v_ref, d_ref, nm_ref, nv_ref):
        d_ref[...], nm_ref[...], nv_ref[...] = _adamw(w_ref[...], g_ref[...], m_ref[...], v_ref[...])

    tr = max(t for t in range(SUBLANES, 1024 + 1, SUBLANES) if r % t == 0)
    blk = pl.BlockSpec((tr, c), lambda i: (i, 0))
    return pl.pallas_call(
        body, name="adam_small", grid=(r // tr,),
        in_specs=[blk] * 4, out_specs=[blk] * 3, out_shape=[jax.ShapeDtypeStruct((r, c), f32)] * 3,
        compiler_params=_cparams("parallel"),
    )(g, w, m, v)


def kernel(x, c, ln_pre_g, ln_post_g, w_mod, b_mod, w_in_ab, w_out_ab, sgu_norm_g, sgu_w, sgu_b, w_in_ssm, w_out_ssm, lam_re, lam_im, b_re, b_im, c_re, c_im, d_skip, log_dt, w_glu, b_glu, loss_target, m_ln_pre_g, m_ln_post_g, m_w_mod, m_b_mod, m_w_in_ab, m_w_out_ab, m_sgu_norm_g, m_sgu_w, m_sgu_b, m_w_in_ssm, m_w_out_ssm, m_lam_re, m_lam_im, m_b_re, m_b_im, m_c_re, m_c_im, m_d_skip, m_log_dt, m_w_glu, m_b_glu, v_ln_pre_g, v_ln_post_g, v_w_mod, v_b_mod, v_w_in_ab, v_w_out_ab, v_sgu_norm_g, v_sgu_w, v_sgu_b, v_w_in_ssm, v_w_out_ssm, v_lam_re, v_lam_im, v_b_re, v_b_im, v_c_re, v_c_im, v_d_skip, v_log_dt, v_w_glu, v_b_glu):
    me = _my_index()
    x0 = x[0]
    l, d = x0.shape
    target = loss_target[0]
    nh = sgu_w.shape[1]
    wa = nh * HEAD
    n_grp, n_st = lam_re.shape[1], lam_re.shape[2]
    mod_cols = w_mod.shape[2]

    def after(a, first):
        return a + jnp.minimum(jnp.abs(first[(0,) * first.ndim].astype(f32)), 0.0).astype(a.dtype)

    c_all, d_skip_all, b_glu_all = all_gather([c, d_skip, b_glu], "gather_c")
    c_all = c_all.reshape(N_DEV, d)
    d_skip_all = d_skip_all.reshape(1, -1)
    b_glu_all = b_glu_all.reshape(1, -1)

    b_cols = lax.dynamic_slice_in_dim(b_mod, me * mod_cols, mod_cols, axis=1)
    (mod_all,) = all_gather([mod_part(c_all, w_mod, b_cols)], "gather_mod")
    (win_ab3,) = sequencer_exchange(GATHER, [after(w_in_ab[0], mod_all).astype(bf16)], "gather_w_in", 1)
    mod_mine = lax.dynamic_index_in_dim(mod_all, me, axis=2, keepdims=False)
    mod_rows = jnp.transpose(mod_mine, (1, 0, 2)).reshape(2, 3, 1, d)

    def rows(a, i):
        return a[i].reshape(1, d)

    shift0, scale0, gate0 = mod_rows[0, 0], mod_rows[0, 1], mod_rows[0, 2]
    h0, h0_t = prenorm_fwd(x0, rows(ln_pre_g, 0), shift0, scale0, "prenorm0")
    wout_ab3, win_ssm3, wout_ssm3, wglu = sequencer_exchange(
        GATHER, [after(w, win_ab3).astype(bf16) for w in (w_out_ab[0], w_in_ssm[0], w_out_ssm[0], w_glu[0])],
        "gather_w_rest", 2)
    proj0 = mm_nn(h0, win_ab3, bf16, "proj0")
    sgu_b3 = sgu_b[0].reshape(nh, HEAD, 1)
    cat, att, tot = sb_fwd(proj0, sgu_fwd(proj0, sgu_norm_g, sgu_w[0], sgu_b3), nh)
    wout_ab3 = wout_ab3.reshape(1, d, d)
    win_ssm3 = win_ssm3.reshape(1, d, d)
    wglu = wglu.reshape(w_glu.shape[2], w_glu.shape[2])
    y0 = mm_nn(cat, wout_ab3, f32, "out0")

    shift1, scale1, gate1 = mod_rows[1, 0], mod_rows[1, 1], mod_rows[1, 2]
    x1, h1, h1_t = post_prenorm_fwd(x0, y0, gate0, rows(ln_post_g, 0), rows(ln_pre_g, 1), shift1, scale1,
                                    "post0_prenorm1")
    proj1 = mm_nn(h1, win_ssm3, bf16, "proj1")
    w_ssm = proj1.shape[1] // 2
    ldt = log_dt[0].reshape(n_grp, 1)
    bt_re = jnp.transpose(b_re[0], (0, 2, 1))
    bt_im = jnp.transpose(b_im[0], (0, 2, 1))
    a_re, a_im, bbt_re, bbt_im = s5_params_fwd(lam_re[0], lam_im[0], ldt, bt_re, bt_im)
    bre3 = _block_diag(bbt_re).astype(bf16)
    bim3 = _block_diag(bbt_im).astype(bf16)
    cre3 = _block_diag(jnp.transpose(c_re[0], (0, 2, 1))).astype(bf16)
    cimn3 = _block_diag(-jnp.transpose(c_im[0], (0, 2, 1))).astype(bf16)
    a_re_row, a_im_row = a_re.reshape(1, -1), a_im.reshape(1, -1)
    y_ssm, hs_re, hs_im = ssm_fwd(proj1, bre3, bim3, cre3, cimn3, a_re_row, a_im_row, d_skip_all)
    g_act, t_glu, mix1 = glu_fwd(y_ssm, proj1, wglu, b_glu_all)
    y1 = mm_nn(mix1, wout_ssm3, f32, "out1")

    dx2, loss_tile, dy1, dgate1, dgpost1 = final_loss(x1, y1, gate1, rows(ln_post_g, 1), target)

    dmix1 = mm_nt(dy1, wout_ssm3, f32, "dmix1")
    gw_out_ssm = mm_tn(mix1, dy1, N_DEV, bf16, "gw_out_ssm")
    (p_out_ssm,) = sequencer_exchange(SCATTER, [gw_out_ssm], "scatter_g1", 3)
    dy_ssm, dproj1, dt_glu, db_glu = glu_bwd(dmix1, y_ssm, t_glu, proj1, wglu)
    gw_glu = mm_tn(g_act, dt_glu, 1, bf16, "gw_glu").reshape(N_DEV, -1, w_ssm)
    dproj1, dd_skip, da_re, da_im, dbre3, dbim3, dcre3, dcimn3 = ssm_bwd(
        dy_ssm, proj1, dproj1, hs_re, hs_im, bre3, bim3, cre3, cimn3, a_re_row, a_im_row, d_skip_all)
    gw_in_ssm = mm_nn(h1_t, dproj1[None], bf16, "gw_in_ssm").reshape(N_DEV, -1, proj1.shape[1])
    p_in_ssm, p_glu = sequencer_exchange(SCATTER, [gw_in_ssm, gw_glu], "scatter_g2", 4)
    dh1 = mm_nt(dproj1, win_ssm3, f32, "dh1")
    dx1, dshift1, dscale1, dgpre1, dy0, dgate0, dgpost0 = prenorm_post_bwd(
        dh1, x1, dx2, rows(ln_pre_g, 1), scale1, y0, gate0, rows(ln_post_g, 0), "prenorm1_post0_bwd")
    dlr, dli, dldt, dbt_re, dbt_im = s5_params_bwd(
        lam_re[0], lam_im[0], ldt, bt_re, bt_im, da_re.reshape(n_grp, n_st), da_im.reshape(n_grp, n_st),
        _diag_blocks(dbre3, SSM_GROUP, n_st), _diag_blocks(dbim3, SSM_GROUP, n_st))
    g_b_re = jnp.transpose(dbt_re, (0, 2, 1))
    g_b_im = jnp.transpose(dbt_im, (0, 2, 1))
    g_c_re = jnp.transpose(_diag_blocks(dcre3, n_st, SSM_GROUP), (0, 2, 1))
    g_c_im = -jnp.transpose(_diag_blocks(dcimn3, n_st, SSM_GROUP), (0, 2, 1))

    dcat = mm_nt(dy0, wout_ab3, f32, "dcat")
    gw_out_ab = mm_tn(cat, dy0, 1, bf16, "gw_out_ab").reshape(N_DEV, -1, d)
    (p_out_ab,) = sequencer_exchange(SCATTER, [gw_out_ab], "scatter_g3", 5)
    dproj0, dsgu_w, dsgu_b, dsgu_ng = sgu_bwd(proj0, dcat, sgu_norm_g, sgu_w[0], sgu_b3)
    dproj0 = sb_bwd(proj0, dcat, att, tot, dproj0, nh)
    gw_in_ab = mm_nn(h0_t, dproj0[None], bf16, "gw_in_ab", split_cols=N_DEV)
    (p_in_ab,) = sequencer_exchange(SCATTER, [gw_in_ab], "scatter_g4", 6)
    dh0 = mm_nt(dproj0, win_ab3, f32, "dh0")
    dx0, dshift0, dscale0, dgpre0 = prenorm_bwd(dh0, x0, dx1, rows(ln_pre_g, 0), scale0, "prenorm0_bwd")

    small_names = ["ln_pre_g", "ln_post_g", "b_mod", "sgu_norm_g", "sgu_w", "sgu_b", "lam_re", "lam_im", "b_re", "b_im",
                   "c_re", "c_im", "log_dt"]
    small_w = [ln_pre_g, ln_post_g, b_mod, sgu_norm_g, sgu_w, sgu_b, lam_re, lam_im, b_re, b_im, c_re, c_im, log_dt]
    small_m = [m_ln_pre_g, m_ln_post_g, m_b_mod, m_sgu_norm_g, m_sgu_w, m_sgu_b, m_lam_re, m_lam_im, m_b_re, m_b_im,
               m_c_re, m_c_im, m_log_dt]
    small_v = [v_ln_pre_g, v_ln_post_g, v_b_mod, v_sgu_norm_g, v_sgu_w, v_sgu_b, v_lam_re, v_lam_im, v_b_re, v_b_im,
               v_c_re, v_c_im, v_log_dt]
    def sharded(p, w, m, v, name):
        shp = w.shape
        w2, m2, v2 = (a.reshape(-1, shp[-1]) for a in (w, m, v))
        return [o.reshape(shp) for o in adam_reduce(p.reshape(p.shape[0], -1, shp[-1]), w2, m2, v2, name)]

    r_w_out_ssm = sharded(p_out_ssm, w_out_ssm, m_w_out_ssm, v_w_out_ssm, "adam_w_out_ssm")
    r_w_in_ssm = sharded(p_in_ssm, w_in_ssm, m_w_in_ssm, v_w_in_ssm, "adam_w_in_ssm")
    r_w_glu = sharded(p_glu, w_glu, m_w_glu, v_w_glu, "adam_w_glu")
    r_w_out_ab = sharded(p_out_ab, w_out_ab, m_w_out_ab, v_w_out_ab, "adam_w_out_ab")
    dmod = jnp.concatenate([dshift0, dscale0, dgate0, dshift1, dscale1, dgate1], axis=1)
    for done in (r_w_out_ssm, r_w_in_ssm, r_w_glu, r_w_out_ab):
        dmod = after(dmod, done[0])
    small_g = [jnp.concatenate([dgpre0, dgpre1]), jnp.concatenate([dgpost0, dgpost1]), dmod, dsgu_ng, dsgu_w, dsgu_b,
               dlr, dli, g_b_re, g_b_im, g_c_re, g_c_im, dldt]
    shapes = [w.shape for w in small_w]
    g_sum, dmod_all = all_reduce_rows(_pack(small_g + [dd_skip, db_glu, loss_tile], SUBLANES * N_DEV), dmod,
                                      "reduce_small_grads")
    n_rows_small = sum(-(-math.prod(s) // PACK_ROW) * SUBLANES for s in shapes)
    loss = g_sum[n_rows_small + 2 * (d_skip_all.shape[1] // HEAD), 0] * (0.5 / d)
    new_small = adam_small(g_sum, _pack(small_w), _pack(small_m), _pack(small_v))
    r_small = [_unpack(o, shapes) for o in [g_sum[:n_rows_small]] + list(new_small)]
    small = {n: [r_small[k][i] for k in range(4)] for i, n in enumerate(small_names)}
    vec_rows = d_skip_all.shape[1] // HEAD

    def my_columns(r0):
        whole = g_sum[r0:r0 + vec_rows].reshape(1, 1, -1)
        return lax.dynamic_slice_in_dim(whole, me * d_skip.shape[1], d_skip.shape[1], axis=2)

    r_d_skip = sharded(my_columns(n_rows_small), d_skip, m_d_skip, v_d_skip, "adam_d_skip")
    r_b_glu = sharded(my_columns(n_rows_small + vec_rows), b_glu, m_b_glu, v_b_glu, "adam_b_glu")
    r_w_in_ab = sharded(p_in_ab, w_in_ab, m_w_in_ab, v_w_in_ab, "adam_w_in_ab")

    dm_cols = jnp.transpose(
        lax.dynamic_slice_in_dim(dmod_all.reshape(N_DEV, 2, 3 * d), me * mod_cols, mod_cols, axis=2), (1, 0, 2))
    cond_t = jnp.transpose(silu_rows(c_all))
    r_w_mod = adam_w_mod(cond_t, dm_cols, w_mod, m_w_mod, v_w_mod)

    res = dict(small)
    res.update(w_mod=r_w_mod, w_in_ab=r_w_in_ab, w_out_ab=r_w_out_ab, w_in_ssm=r_w_in_ssm, w_out_ssm=r_w_out_ssm,
               d_skip=r_d_skip, w_glu=r_w_glu, b_glu=r_b_glu)
    order = ["ln_pre_g", "ln_post_g", "w_mod", "b_mod", "w_in_ab", "w_out_ab", "sgu_norm_g", "sgu_w", "sgu_b", "w_in_ssm",
             "w_out_ssm", "lam_re", "lam_im", "b_re", "b_im", "c_re", "c_im", "d_skip", "log_dt", "w_glu", "b_glu"]
    outs = [loss, dx0.reshape(x.shape)]
    for k in range(4):
        outs += [res[n][k] for n in order]
    return tuple(outs)
```

```python
import functools
import math

import jax
import jax.numpy as jnp
from jax import lax
from jax.experimental import pallas as pl
from jax.experimental.pallas import tpu as pltpu
from jax.experimental.pallas import tpu_sc as plsc

f32 = jnp.float32
bf16 = jnp.bfloat16

N_DEV = 8
EPS = 1e-6
HEAD = 128
SUBLANES = 8
SSM_GROUP = 16
SSM_STATE = 64
GROUPS_PER_LANE_BLOCK = HEAD // SSM_GROUP
STATES_PER_LANE_BLOCK = GROUPS_PER_LANE_BLOCK * SSM_STATE
VMEM_LIMIT = 56 * 2 ** 20
ADAM_LR, ADAM_B1, ADAM_B2, ADAM_EPS, ADAM_WD, ADAM_STEP = 0.001, 0.9, 0.999, 1e-08, 0.01, 10
_GELU_C0 = math.sqrt(2.0 / math.pi)
_GELU_C1 = 0.044715
MESH = pl.DeviceIdType.MESH


def _cparams(*sem):
    return pltpu.CompilerParams(dimension_semantics=sem if sem else None, vmem_limit_bytes=VMEM_LIMIT)


def _gelu(x):
    return 0.5 * x * (1.0 + jnp.tanh(_GELU_C0 * (x + _GELU_C1 * x * x * x)))


def _gelu_grad(x):
    t = jnp.tanh(_GELU_C0 * (x + _GELU_C1 * x * x * x))
    return 0.5 * (1.0 + t) + 0.5 * x * (1.0 - t * t) * _GELU_C0 * (1.0 + 3.0 * _GELU_C1 * x * x)


def _silu(x):
    return x * jax.nn.sigmoid(x)


def _silu_grad(x):
    s = jax.nn.sigmoid(x)
    return s * (1.0 + x * (1.0 - s))


def _dot(a, b):
    return jnp.dot(a, b, preferred_element_type=f32)


def _dot_nt(a, b):
    return lax.dot_general(a, b, (((1,), (1,)), ((), ())), preferred_element_type=f32)


def _dot_tn(a, b):
    return lax.dot_general(a, b, (((0,), (0,)), ((), ())), preferred_element_type=f32)


def _split_bf16(v):
    hi = v.astype(bf16)
    lo = (v - hi.astype(f32)).astype(bf16)
    return hi, lo


def _row(d):
    return pl.BlockSpec((1, d), lambda *_: (0, 0))


def _my_index():
    return 4 * lax.axis_index("x") + 2 * lax.axis_index("y") + lax.axis_index("c")


def _peer(k):
    x, y, c = lax.axis_index("x"), lax.axis_index("y"), lax.axis_index("c")
    return (1 - x if k & 4 else x, 1 - y if k & 2 else y, 1 - c if k & 1 else c)


def all_gather(arrs, name):
    n = len(arrs)

    def body(*refs):
        ins, outs = refs[:n], refs[n:2 * n]
        send, recv, local = refs[2 * n:]
        me = _my_index()
        copies = []
        for a in range(n):
            cp = pltpu.make_async_copy(ins[a], outs[a].at[me], local.at[a])
            cp.start()
            copies.append(cp)
            for k in range(1, N_DEV):
                s = a * (N_DEV - 1) + k - 1
                cp = pltpu.make_async_remote_copy(src_ref=ins[a], dst_ref=outs[a].at[me], send_sem=send.at[s],
                                                  recv_sem=recv.at[s], device_id=_peer(k), device_id_type=MESH)
                cp.start()
                copies.append(cp)
        for cp in copies:
            cp.wait()

    any_spec = pl.BlockSpec(memory_space=pl.ANY)
    outs = pl.pallas_call(
        body, name=name,
        out_shape=[jax.ShapeDtypeStruct((N_DEV,) + a.shape, a.dtype) for a in arrs],
        in_specs=[any_spec] * n, out_specs=[any_spec] * n,
        scratch_shapes=[pltpu.SemaphoreType.DMA((n * (N_DEV - 1),)), pltpu.SemaphoreType.DMA((n * (N_DEV - 1),)),
                        pltpu.SemaphoreType.DMA((n,))],
        compiler_params=pltpu.CompilerParams(has_side_effects=True),
    )(*arrs)
    return list(outs)


def all_reduce_rows(pack, extra, name):
    r, c = pack.shape
    rs = r // N_DEV
    n_peer = N_DEV - 1

    def body(p_ref, x_ref, o_ref, xo_ref, land, red, send1, recv1, send2, recv2, sendx, recvx, local):
        me = _my_index()

        def rows(i):
            return pl.ds(pl.multiple_of(i * rs, SUBLANES), rs)

        own = [pltpu.make_async_copy(p_ref.at[rows(me)], land.at[me], local.at[0]),
               pltpu.make_async_copy(x_ref, xo_ref.at[me], local.at[1])]
        first = []
        for k in range(1, N_DEV):
            first.append(pltpu.make_async_remote_copy(
                src_ref=p_ref.at[rows(jnp.bitwise_xor(me, k))], dst_ref=land.at[me], send_sem=send1.at[k - 1],
                recv_sem=recv1.at[k - 1], device_id=_peer(k), device_id_type=MESH))
            first.append(pltpu.make_async_remote_copy(
                src_ref=x_ref, dst_ref=xo_ref.at[me], send_sem=sendx.at[k - 1], recv_sem=recvx.at[k - 1],
                device_id=_peer(k), device_id_type=MESH))
        for cp in own + first:
            cp.start()
        for cp in own + first:
            cp.wait()
        acc = land[0]
        for s in range(1, N_DEV):
            acc = acc + land[s]
        red[...] = acc
        mine = pltpu.make_async_copy(red, o_ref.at[rows(me)], local.at[2])
        second = [pltpu.make_async_remote_copy(
            src_ref=red, dst_ref=o_ref.at[rows(me)], send_sem=send2.at[k - 1], recv_sem=recv2.at[k - 1],
            device_id=_peer(k), device_id_type=MESH) for k in range(1, N_DEV)]
        for cp in [mine] + second:
            cp.start()
        for cp in [mine] + second:
            cp.wait()

    any_spec = pl.BlockSpec(memory_space=pl.ANY)
    return pl.pallas_call(
        body, name=name,
        out_shape=[jax.ShapeDtypeStruct((r, c), pack.dtype), jax.ShapeDtypeStruct((N_DEV,) + extra.shape, extra.dtype)],
        in_specs=[any_spec, any_spec], out_specs=[any_spec, any_spec],
        scratch_shapes=[pltpu.VMEM((N_DEV, rs, c), pack.dtype), pltpu.VMEM((rs, c), pack.dtype)]
        + [pltpu.SemaphoreType.DMA((n_peer,))] * 6 + [pltpu.SemaphoreType.DMA((3,))],
        compiler_params=pltpu.CompilerParams(has_side_effects=True),
    )(pack, extra)


GATHER, SCATTER = "gather", "scatter"


def _exchange_copies(srcs, lands, send, recv):
    me = _my_index()
    copies = []
    for a, (src, land) in enumerate(zip(srcs, lands)):
        for k in range(1, N_DEV):
            s = a * (N_DEV - 1) + k - 1
            copies.append(pltpu.make_async_remote_copy(
                src_ref=src.at[jnp.bitwise_xor(me, k)], dst_ref=land.at[me],
                send_sem=send.at[s], recv_sem=recv.at[s], device_id=_peer(k), device_id_type=MESH))
    return copies


def sequencer_exchange(kind, arrs, name, collective_id):
    n = len(arrs)
    n_sem = n * (N_DEV - 1)
    land_shapes = [((N_DEV,) + a.shape if kind == GATHER else a.shape) for a in arrs]
    srcs = [jax.new_ref(a, memory_space=pltpu.MemorySpace.HBM) for a in arrs]
    lands = [jax.empty_ref(jax.ShapeDtypeStruct(s, a.dtype), memory_space=pltpu.MemorySpace.HBM)
             for s, a in zip(land_shapes, arrs)]

    @pl.kernel(mesh=plsc.ScalarSubcoreMesh(axis_name="sequencer", num_cores=1), name=name,
               scratch_types=(pltpu.SemaphoreType.DMA((n_sem,)), pltpu.SemaphoreType.DMA((n_sem,)),
                              pltpu.SemaphoreType.DMA((n,))),
               compiler_params=pltpu.CompilerParams(collective_id=collective_id))
    def launch(send, recv, local):
        barrier = pltpu.get_barrier_semaphore()
        for k in range(1, N_DEV):
            pl.semaphore_signal(barrier, inc=1, device_id=_peer(k), device_id_type=MESH)
        pl.semaphore_wait(barrier, N_DEV - 1)
        me = _my_index()
        mine = [pltpu.make_async_copy(src if kind == GATHER else src.at[me], land.at[me], local.at[a])
                for a, (src, land) in enumerate(zip(srcs, lands))]
        if kind == SCATTER:
            copies = mine + _exchange_copies(srcs, lands, send, recv)
            for cp in copies:
                cp.start()
            for cp in copies:
                cp.wait()
            return

        def block_copy(a, slot, block, k, src=None):
            s = a * (N_DEV - 1) + slot
            return pltpu.make_async_remote_copy(
                src_ref=lands[a].at[block] if src is None else src, dst_ref=lands[a].at[block],
                send_sem=send.at[s], recv_sem=recv.at[s], device_id=_peer(k), device_id_type=MESH)

        chips = (2, 4, 6)
        sibling = jnp.bitwise_xor(me, 1)
        first = [block_copy(a, slot, me, k, src=srcs[a]) for a in range(n) for slot, k in enumerate((1,) + chips)]
        for cp in mine + first:
            cp.start()
        passed = []
        for a in range(n):
            for i, k in enumerate(chips):
                block = jnp.bitwise_xor(me, k)
                block_copy(a, 1 + i, block, k).wait_recv()
                passed.append(block_copy(a, 4 + i, block, 1))
                passed[-1].start()
        for a in range(n):
            block_copy(a, 0, sibling, 1).wait_recv()
            for i, k in enumerate(chips):
                block_copy(a, 4 + i, jnp.bitwise_xor(sibling, k), 1).wait_recv()
        for cp in mine:
            cp.wait()
        for cp in first + passed:
            cp.wait_send()

    launch()
    return [land[...] for land in lands]


def _tile(n, pref):
    for t in pref:
        if n % t == 0:
            return t
    return n


MM_WIDE = 1024
MM_WEIGHT_BLOCK = 8 * 2 ** 20


def _blocks_per_step(nb, fits):
    return max(g for g in range(1, nb + 1) if nb % g == 0 and fits(g))


def mm_nn(a, b3, out_dtype, name, split_cols=None):
    m, k = a.shape
    nb, _, bn = b3.shape
    tm = _tile(m, (512, 256, 128))
    tn = bn // split_cols if split_cols else _tile(bn, (1024, 896, 512, 256, 128))
    per = bn // tn
    gb = _blocks_per_step(nb, lambda g: g == 1 or (per == 1 and g * bn <= MM_WIDE))

    def body(a_ref, b_ref, o_ref):
        for g in range(gb):
            o_ref[:, g * tn:(g + 1) * tn] = _dot(a_ref[...], b_ref[g]).astype(o_ref.dtype)

    if split_cols:
        out_spec = pl.BlockSpec((None, tm, tn), lambda i, j, jj: (jj, i, 0))
        out_shape = jax.ShapeDtypeStruct((split_cols, m, tn), out_dtype)
    else:
        out_spec = pl.BlockSpec((tm, gb * tn), lambda i, j, jj: (i, j * per + jj))
        out_shape = jax.ShapeDtypeStruct((m, nb * bn), out_dtype)
    return pl.pallas_call(
        body, name=name, grid=(m // tm, nb // gb, per),
        in_specs=[pl.BlockSpec((tm, k), lambda i, j, jj: (i, 0)),
                  pl.BlockSpec((gb, k, tn), lambda i, j, jj: (j, 0, jj))],
        out_specs=out_spec, out_shape=out_shape,
        compiler_params=_cparams("parallel", "arbitrary", "arbitrary"),
    )(a, b3)


def mm_nt(a, w3, out_dtype, name):
    m, _ = a.shape
    nb, ko, bn = w3.shape
    tm = _tile(m, (512, 256, 128))
    tko = _tile(ko, (1024, 512, 256, 128))
    gb = _blocks_per_step(nb, lambda g: g * tko * bn * w3.dtype.itemsize <= MM_WEIGHT_BLOCK)
    ns = nb // gb

    def body(a_ref, w_ref, o_ref, acc_ref):
        j = pl.program_id(2)

        @pl.when(j == 0)
        def _():
            acc_ref[...] = jnp.zeros_like(acc_ref)

        part = _dot_nt(a_ref[:, :bn], w_ref[0])
        for g in range(1, gb):
            part += _dot_nt(a_ref[:, g * bn:(g + 1) * bn], w_ref[g])
        acc_ref[...] += part

        @pl.when(j == ns - 1)
        def _():
            o_ref[...] = acc_ref[...].astype(o_ref.dtype)

    return pl.pallas_call(
        body, name=name, grid=(m // tm, ko // tko, ns),
        in_specs=[pl.BlockSpec((tm, gb * bn), lambda i, o, j: (i, j)),
                  pl.BlockSpec((gb, tko, bn), lambda i, o, j: (j, o, 0))],
        out_specs=pl.BlockSpec((tm, tko), lambda i, o, j: (i, o)),
        out_shape=jax.ShapeDtypeStruct((m, ko), out_dtype),
        scratch_shapes=[pltpu.VMEM((tm, tko), f32)],
        compiler_params=_cparams("parallel", "arbitrary", "arbitrary"),
    )(a, w3)


def mm_tn(a, dy, ncb, out_dtype, name):
    l, ka = a.shape
    _, n = dy.shape
    bn = n // ncb
    tl = _tile(l, (1024, 512, 256, 128))
    tka = _tile(ka, (512, 256, 128))
    tn = _tile(bn, (1024, 896, 512, 256, 128))
    per = bn // tn
    gb = _blocks_per_step(ncb, lambda g: g == 1 or (per == 1 and g * bn <= MM_WIDE))
    nl = l // tl

    def body(a_ref, dy_ref, o_ref, acc_ref):
        s = pl.program_id(2)

        @pl.when(s == 0)
        def _():
            acc_ref[...] = jnp.zeros_like(acc_ref)

        acc_ref[...] += _dot_tn(a_ref[...], dy_ref[...])

        @pl.when(s == nl - 1)
        def _():
            for g in range(gb):
                o_ref[g] = acc_ref[:, g * tn:(g + 1) * tn].astype(o_ref.dtype)

    return pl.pallas_call(
        body, name=name, grid=(ka // tka, n // (gb * tn), nl),
        in_specs=[pl.BlockSpec((tl, tka), lambda i, j, s: (s, i)),
                  pl.BlockSpec((tl, gb * tn), lambda i, j, s: (s, j))],
        out_specs=pl.BlockSpec((gb, tka, tn), lambda i, j, s: (j // per, i, j % per)),
        out_shape=jax.ShapeDtypeStruct((ncb, ka, bn), out_dtype),
        scratch_shapes=[pltpu.VMEM((tka, gb * tn), f32)],
        compiler_params=_cparams("parallel", "parallel", "arbitrary"),
    )(a, dy)


def mod_part(c_all, w_mod, b_cols):
    nl, d, cols = w_mod.shape

    def body(c_ref, w_ref, b_ref, o_ref):
        cond = _silu(c_ref[...]).astype(bf16)
        o_ref[...] = _dot(cond, w_ref[...].astype(bf16)) + b_ref[...]

    return pl.pallas_call(
        body, name="mod_part", grid=(nl,),
        in_specs=[pl.BlockSpec((N_DEV, d), lambda l: (0, 0)),
                  pl.BlockSpec((None, d, cols), lambda l: (l, 0, 0)),
                  pl.BlockSpec((None, 1, cols), lambda l: (l, 0, 0))],
        out_specs=pl.BlockSpec((None, N_DEV, cols), lambda l: (l, 0, 0)),
        out_shape=jax.ShapeDtypeStruct((nl, N_DEV, cols), f32),
        compiler_params=_cparams("arbitrary"),
    )(c_all, w_mod, b_cols.reshape(nl, 1, cols))


def _row_tile(l):
    return _tile(l, (512, 256, 128))


def _entry_rows(xv, g_ref, sh_ref, sc_ref, h_ref, ht_ref):
    r = lax.rsqrt(jnp.mean(xv * xv, axis=-1, keepdims=True) + EPS)
    h = xv * r * (g_ref[...] * (1.0 + sc_ref[...])) + sh_ref[...]
    h_ref[...] = h.astype(h_ref.dtype)
    ht_ref[...] = jnp.transpose(h).astype(ht_ref.dtype)


def prenorm_fwd(x, g, shift, scale, name):
    l, d = x.shape
    tm = _row_tile(l)

    def body(x_ref, g_ref, sh_ref, sc_ref, h_ref, ht_ref):
        _entry_rows(x_ref[...], g_ref, sh_ref, sc_ref, h_ref, ht_ref)

    return pl.pallas_call(
        body, name=name, grid=(l // tm,),
        in_specs=[pl.BlockSpec((tm, d), lambda i: (i, 0)), _row(d), _row(d), _row(d)],
        out_specs=[pl.BlockSpec((tm, d), lambda i: (i, 0)), pl.BlockSpec((d, tm), lambda i: (0, i))],
        out_shape=[jax.ShapeDtypeStruct((l, d), bf16), jax.ShapeDtypeStruct((d, l), bf16)],
        compiler_params=_cparams("parallel"),
    )(x, g, shift, scale)


def post_prenorm_fwd(x, y, gate, g_post, g_pre, shift, scale, name):
    l, d = x.shape
    tm = _row_tile(l)

    def body(x_ref, y_ref, gate_ref, gp_ref, g_ref, sh_ref, sc_ref, o_ref, h_ref, ht_ref):
        yv = y_ref[...]
        r = lax.rsqrt(jnp.mean(yv * yv, axis=-1, keepdims=True) + EPS)
        xv = x_ref[...] + gate_ref[...] * (yv * r * gp_ref[...])
        o_ref[...] = xv
        _entry_rows(xv, g_ref, sh_ref, sc_ref, h_ref, ht_ref)

    blk = pl.BlockSpec((tm, d), lambda i: (i, 0))
    return pl.pallas_call(
        body, name=name, grid=(l // tm,),
        in_specs=[blk, blk] + [_row(d)] * 5, out_specs=[blk, blk, pl.BlockSpec((d, tm), lambda i: (0, i))],
        out_shape=[jax.ShapeDtypeStruct((l, d), f32), jax.ShapeDtypeStruct((l, d), bf16),
                   jax.ShapeDtypeStruct((d, l), bf16)],
        compiler_params=_cparams("parallel"),
    )(x, y, gate, g_post, g_pre, shift, scale)


def _post_bwd_rows(dxv, yv, r, gate, gv, dy_ref, dgate_ref, dg_ref):
    yn = yv * r
    dgate_ref[...] += jnp.sum(dxv * yn * gv, axis=0, keepdims=True)
    dyg = dxv * gate
    dg_ref[...] += jnp.sum(dyg * yn, axis=0, keepdims=True)
    dyn = dyg * gv
    dy_ref[...] = (r * (dyn - yn * jnp.mean(dyn * yn, axis=-1, keepdims=True))).astype(dy_ref.dtype)


def final_loss(x, y, gate, g, target):
    l, d = x.shape
    tm = _row_tile(l)

    def body(x_ref, y_ref, gate_ref, g_ref, t_ref, dx_ref, loss_ref, dy_ref, dgate_ref, dg_ref):
        @pl.when(pl.program_id(0) == 0)
        def _():
            loss_ref[...] = jnp.zeros_like(loss_ref)
            dgate_ref[...] = jnp.zeros_like(dgate_ref)
            dg_ref[...] = jnp.zeros_like(dg_ref)

        yv, gate, gv = y_ref[...], gate_ref[...], g_ref[...]
        r = lax.rsqrt(jnp.mean(yv * yv, axis=-1, keepdims=True) + EPS)
        diff = x_ref[...] + gate * (yv * r * gv) - t_ref[...]
        dxv = diff * (1.0 / d)
        dx_ref[...] = dxv
        loss_ref[...] += jnp.sum(diff * diff)
        _post_bwd_rows(dxv, yv, r, gate, gv, dy_ref, dgate_ref, dg_ref)

    blk = pl.BlockSpec((tm, d), lambda i: (i, 0))
    return pl.pallas_call(
        body, name="final_loss", grid=(l // tm,),
        in_specs=[blk, blk, _row(d), _row(d), blk],
        out_specs=[blk, pl.BlockSpec((SUBLANES, HEAD), lambda i: (0, 0)), blk, _row(d), _row(d)],
        out_shape=[jax.ShapeDtypeStruct((l, d), f32), jax.ShapeDtypeStruct((SUBLANES, HEAD), f32),
                   jax.ShapeDtypeStruct((l, d), bf16), jax.ShapeDtypeStruct((1, d), f32), jax.ShapeDtypeStruct((1, d), f32)],
        compiler_params=_cparams("arbitrary"),
    )(x, y, gate, g, target)


def prenorm_bwd(dh, x, dx_next, g, scale, name):
    l, d = x.shape
    tm = _row_tile(l)

    def body(dh_ref, x_ref, dxn_ref, g_ref, sc_ref, dx_ref, dsh_ref, dsc_ref, dg_ref):
        @pl.when(pl.program_id(0) == 0)
        def _():
            dsh_ref[...] = jnp.zeros_like(dsh_ref)
            dsc_ref[...] = jnp.zeros_like(dsc_ref)
            dg_ref[...] = jnp.zeros_like(dg_ref)

        xv, dhv, gv, sc1 = x_ref[...], dh_ref[...], g_ref[...], 1.0 + sc_ref[...]
        r = lax.rsqrt(jnp.mean(xv * xv, axis=-1, keepdims=True) + EPS)
        xn = xv * r
        dhx = dhv * xn
        dsh_ref[...] += jnp.sum(dhv, axis=0, keepdims=True)
        dsc_ref[...] += jnp.sum(dhx * gv, axis=0, keepdims=True)
        dg_ref[...] += jnp.sum(dhx * sc1, axis=0, keepdims=True)
        dxn = dhv * (gv * sc1)
        dx_ref[...] = dxn_ref[...] + r * (dxn - xn * jnp.mean(dxn * xn, axis=-1, keepdims=True))

    blk = pl.BlockSpec((tm, d), lambda i: (i, 0))
    return pl.pallas_call(
        body, name=name, grid=(l // tm,),
        in_specs=[blk, blk, blk, _row(d), _row(d)], out_specs=[blk, _row(d), _row(d), _row(d)],
        out_shape=[jax.ShapeDtypeStruct((l, d), f32)] + [jax.ShapeDtypeStruct((1, d), f32)] * 3,
        compiler_params=_cparams("arbitrary"),
    )(dh, x, dx_next, g, scale)


def prenorm_post_bwd(dh, x, dx_next, g, scale, y, gate, g_post, name):
    l, d = x.shape
    tm = _tile(l, (256, 128))

    def body(dh_ref, x_ref, dxn_ref, g_ref, sc_ref, y_ref, gate_ref, gp_ref,
             dx_ref, dsh_ref, dsc_ref, dg_ref, dy_ref, dgate_ref, dgp_ref):
        @pl.when(pl.program_id(0) == 0)
        def _():
            for ref in (dsh_ref, dsc_ref, dg_ref, dgate_ref, dgp_ref):
                ref[...] = jnp.zeros_like(ref)

        xv, dhv, gv, sc1 = x_ref[...], dh_ref[...], g_ref[...], 1.0 + sc_ref[...]
        r = lax.rsqrt(jnp.mean(xv * xv, axis=-1, keepdims=True) + EPS)
        xn = xv * r
        dhx = dhv * xn
        dsh_ref[...] += jnp.sum(dhv, axis=0, keepdims=True)
        dsc_ref[...] += jnp.sum(dhx * gv, axis=0, keepdims=True)
        dg_ref[...] += jnp.sum(dhx * sc1, axis=0, keepdims=True)
        dxn = dhv * (gv * sc1)
        dxv = dxn_ref[...] + r * (dxn - xn * jnp.mean(dxn * xn, axis=-1, keepdims=True))
        dx_ref[...] = dxv
        yv = y_ref[...]
        ry = lax.rsqrt(jnp.mean(yv * yv, axis=-1, keepdims=True) + EPS)
        _post_bwd_rows(dxv, yv, ry, gate_ref[...], gp_ref[...], dy_ref, dgate_ref, dgp_ref)

    blk = pl.BlockSpec((tm, d), lambda i: (i, 0))
    row = jax.ShapeDtypeStruct((1, d), f32)
    return pl.pallas_call(
        body, name=name, grid=(l // tm,),
        in_specs=[blk, blk, blk, _row(d), _row(d), blk, _row(d), _row(d)],
        out_specs=[blk, _row(d), _row(d), _row(d), blk, _row(d), _row(d)],
        out_shape=[jax.ShapeDtypeStruct((l, d), f32), row, row, row, jax.ShapeDtypeStruct((l, d), bf16), row, row],
        compiler_params=_cparams("arbitrary"),
    )(dh, x, dx_next, g, scale, y, gate, g_post)


def _tril_mask():
    r = lax.broadcasted_iota(jnp.int32, (HEAD, HEAD), 0)
    c = lax.broadcasted_iota(jnp.int32, (HEAD, HEAD), 1)
    return r >= c


def sgu_fwd(proj, norm_g, w_s, b_s):
    l = proj.shape[0]
    nh = w_s.shape[0]
    wa = nh * HEAD

    def body(au_ref, av_ref, az_ref, ng_ref, w_ref, b_ref, o_ref):
        tril = _tril_mask()
        for h in range(nh):
            sl = slice(h * HEAD, (h + 1) * HEAD)
            gv = _gelu(av_ref[:, sl].astype(f32))
            r = lax.rsqrt(jnp.mean(gv * gv, axis=-1, keepdims=True) + EPS)
            vh = gv * r * ng_ref[:, sl]
            wm = jnp.where(tril, w_ref[h], 0.0).astype(bf16)
            s = _dot(wm, vh.astype(bf16)) + b_ref[h]
            o_ref[:, sl] = (_gelu(au_ref[:, sl].astype(f32)) * s * _silu(az_ref[:, sl].astype(f32))).astype(o_ref.dtype)

    def col(j):
        return pl.BlockSpec((HEAD, wa), lambda n: (n, j))

    return pl.pallas_call(
        body, name="sgu_fwd", grid=(l // HEAD,),
        in_specs=[col(0), col(1), col(2), _row(wa),
                  pl.BlockSpec((nh, HEAD, HEAD), lambda n: (0, 0, 0)), pl.BlockSpec((nh, HEAD, 1), lambda n: (0, 0, 0))],
        out_specs=pl.BlockSpec((HEAD, wa), lambda n: (n, 0)),
        out_shape=jax.ShapeDtypeStruct((l, 2 * wa), bf16),
        compiler_params=_cparams("parallel"),
    )(proj, proj, proj, norm_g, w_s, b_s)


def sgu_bwd(proj, dcat, norm_g, w_s, b_s):
    l = proj.shape[0]
    nh = w_s.shape[0]
    wa = nh * HEAD

    def body(au_ref, av_ref, az_ref, do_ref, ng_ref, w_ref, b_ref, da_ref, dw_ref, db_ref, dng_ref):
        @pl.when(pl.program_id(0) == 0)
        def _():
            dw_ref[...] = jnp.zeros_like(dw_ref)
            db_ref[...] = jnp.zeros_like(db_ref)
            dng_ref[...] = jnp.zeros_like(dng_ref)

        tril = _tril_mask()
        for h in range(nh):
            sl = slice(h * HEAD, (h + 1) * HEAD)
            au, av, az = au_ref[:, sl].astype(f32), av_ref[:, sl].astype(f32), az_ref[:, sl].astype(f32)
            ng = ng_ref[:, sl]
            gv = _gelu(av)
            r = lax.rsqrt(jnp.mean(gv * gv, axis=-1, keepdims=True) + EPS)
            gvn = gv * r
            vh = (gvn * ng).astype(bf16)
            wm = jnp.where(tril, w_ref[h], 0.0).astype(bf16)
            s = _dot(wm, vh) + b_ref[h]
            gu, sz = _gelu(au), _silu(az)
            dov = do_ref[:, sl].astype(f32)
            da_ref[:, sl] = (dov * s * sz * _gelu_grad(au)).astype(da_ref.dtype)
            da_ref[:, 2 * wa + h * HEAD:2 * wa + (h + 1) * HEAD] = (dov * gu * s * _silu_grad(az)).astype(da_ref.dtype)
            ds = dov * gu * sz
            db_ref[h] += jnp.sum(ds, axis=-1, keepdims=True)
            dsb = ds.astype(bf16)
            dw_ref[h] += jnp.where(tril, _dot_nt(dsb, vh), 0.0)
            dvh = _dot_tn(wm, dsb)
            dng_ref[:, sl] += jnp.sum(dvh * gvn, axis=0, keepdims=True)
            dgvn = dvh * ng
            dgv = r * (dgvn - gvn * jnp.mean(dgvn * gvn, axis=-1, keepdims=True))
            da_ref[:, wa + h * HEAD:wa + (h + 1) * HEAD] = (dgv * _gelu_grad(av)).astype(da_ref.dtype)

    def col(j):
        return pl.BlockSpec((HEAD, wa), lambda n: (n, j))

    whole_w = pl.BlockSpec((nh, HEAD, HEAD), lambda n: (0, 0, 0))
    whole_b = pl.BlockSpec((nh, HEAD, 1), lambda n: (0, 0, 0))
    return pl.pallas_call(
        body, name="sgu_bwd", grid=(l // HEAD,),
        in_specs=[col(0), col(1), col(2), col(0), _row(wa), whole_w, whole_b],
        out_specs=[pl.BlockSpec((HEAD, 3 * wa), lambda n: (n, 0)), whole_w, whole_b, _row(wa)],
        out_shape=[jax.ShapeDtypeStruct(proj.shape, bf16), jax.ShapeDtypeStruct((nh, HEAD, HEAD), f32),
                   jax.ShapeDtypeStruct((nh, HEAD, 1), f32), jax.ShapeDtypeStruct((1, wa), f32)],
        compiler_params=_cparams("arbitrary"),
    )(proj, proj, proj, dcat, norm_g, w_s, b_s)


_LOG2E = 1.0 / math.log(2.0)
_SB_EXP_CLAMP = 120.0


def _sb_scores(q, k, scale):
    z = _dot_nt(q, k) * (scale * _LOG2E)
    return z, jnp.maximum(z, jnp.log2(1.0 + jnp.exp2(jnp.minimum(z, _SB_EXP_CLAMP))))


SB_KEYS = 256


def _sb_sum_matrix(tri, kb):
    s = lax.broadcasted_iota(jnp.int32, (2 * kb, kb + HEAD), 0) % kb
    j = lax.broadcasted_iota(jnp.int32, (2 * kb, kb + HEAD), 1)
    return jnp.where(jnp.logical_or(j >= kb, tri(s, j)), 1.0, 0.0).astype(bf16)


def _sb_sums(x, sums):
    kb = x.shape[1]
    c2 = _dot(jnp.concatenate(_split_bf16(x), axis=1), sums)
    return c2[:, :kb], c2[:, kb:]


def _sb_wide(v, kb):
    return jnp.concatenate([v] * (kb // HEAD), axis=1) if kb > HEAD else v


def _sb_q_tile(l, most=512):
    return _tile(l, tuple(t for t in (1024, 512, 256, 128) if t <= most))


def _sb_band_levels(band):
    return _tile(band, (4, 2, 1))


def _sb_heads_per_step(nh, most):
    return _tile(nh, tuple(h for h in (4, 2) if h <= most))


def sb_fwd(proj, mixed, nh):
    l = proj.shape[0]
    wb = nh * HEAD
    tq = _sb_q_tile(l, 1024)
    kb = min(SB_KEYS, tq)
    band = tq // kb
    hp = _sb_heads_per_step(nh, 4)
    levels = _sb_band_levels(band)
    scale = 1.0 / math.sqrt(HEAD)
    qc, kc, vc, zc = 3 * nh, 4 * nh, 5 * nh, 6 * nh

    def body(q_ref, k_ref, v_ref, bz_ref, mixed_ref, o_ref, att_ref, tot_ref):
        del mixed_ref
        i = pl.program_id(1)
        sums = _sb_sum_matrix(lambda s, j: s > j, kb)
        t_pos = i * tq + lax.broadcasted_iota(jnp.int32, (tq, kb), 0)
        s_off = lax.broadcasted_iota(jnp.int32, (tq, kb), 1)

        def step(j, carry, masked, row0=0):
            rows = pl.ds(pl.multiple_of(j * kb, kb), kb)
            out = []
            for e in range(hp):
                acc, tot = carry[e]
                sl = slice(e * HEAD, (e + 1) * HEAD)
                z, sp = _sb_scores(q_ref[row0:, sl], k_ref[rows, sl], scale)
                lb = z - sp
                if masked:
                    mask = s_off[row0:] + j * kb < t_pos[row0:]
                    sp = jnp.where(mask, sp, 0.0)
                later, total = _sb_sums(sp, sums)
                w = jnp.exp2(lb + _sb_wide(tot[row0:], kb) - later)
                if masked:
                    w = jnp.where(mask, w, 0.0)
                new = (acc[row0:] + _dot(w.astype(bf16), v_ref[rows, sl]), tot[row0:] - total)
                out.append(tuple(jnp.concatenate([old[:row0], upd]) if row0 else upd for old, upd in zip(carry[e], new)))
            return tuple(out)

        zero = jnp.zeros((tq, HEAD), f32)
        carry = ((zero, zero),) * hp
        for lv in reversed(range(levels)):
            carry = lax.fori_loop(
                0, band // levels,
                lambda t, c, lv=lv: step(band * i + (lv + 1) * (band // levels) - 1 - t, c, True, lv * (tq // levels)), carry)
        carry = lax.fori_loop(0, band * i, lambda t, c: step(band * i - 1 - t, c, False), carry)
        for e in range(hp):
            acc, tot = carry[e]
            sl = slice(e * HEAD, (e + 1) * HEAD)
            att_ref[:, sl] = acc.astype(att_ref.dtype)
            o_ref[:, sl] = (acc * _silu(bz_ref[:, sl].astype(f32))).astype(o_ref.dtype)
            tot_ref[e] = tot[:, :1]

    blk = lambda c0: pl.BlockSpec((tq, hp * HEAD), lambda g, i: (i, c0 // hp + g))
    head = lambda c0: pl.BlockSpec((l, hp * HEAD), lambda g, i: (0, c0 // hp + g))
    return pl.pallas_call(
        body, name="sb_fwd", grid=(nh // hp, l // tq),
        in_specs=[blk(qc), head(kc), head(vc), blk(zc), pl.BlockSpec(memory_space=pl.ANY)],
        out_specs=[blk(mixed.shape[1] // HEAD - nh), blk(0), pl.BlockSpec((hp, tq, 1), lambda g, i: (g, i, 0))],
        out_shape=[jax.ShapeDtypeStruct(mixed.shape, bf16), jax.ShapeDtypeStruct((l, wb), bf16),
                   jax.ShapeDtypeStruct((nh, l, 1), f32)],
        input_output_aliases={4: 0},
        compiler_params=_cparams("parallel", "arbitrary"),
    )(proj, proj, proj, proj, mixed)


def sb_bwd(proj, dcat, att, tot, dproj, nh):
    l = proj.shape[0]
    wb = nh * HEAD
    tq = _sb_q_tile(l, 1024)
    kb = min(SB_KEYS, tq)
    band = tq // kb
    nq = l // tq
    hp = _sb_heads_per_step(nh, 2)
    levels = _sb_band_levels(band)
    scale = 1.0 / math.sqrt(HEAD)
    qc, kc, vc, zc = 3 * nh, 4 * nh, 5 * nh, 6 * nh

    def body(q_ref, k_ref, v_ref, bz_ref, do_ref, att_ref, tot_ref, dproj_in, dproj_ref, dk_acc, dv_acc, dob_ref,
             tile_ref, head_ref, sems):
        del dproj_in
        g, i = pl.program_id(0), pl.program_id(1)

        def put(src, row0, c0, k):
            cols = pl.ds(pl.multiple_of((c0 + g * hp) * HEAD, HEAD), hp * HEAD)
            cp = pltpu.make_async_copy(src, dproj_ref.at[pl.ds(row0, src.shape[0]), cols], sems.at[k])
            cp.start()
            return cp

        @pl.when(i == 0)
        def _():
            dk_acc[...] = jnp.zeros_like(dk_acc)
            dv_acc[...] = jnp.zeros_like(dv_acc)

        my_rows = pl.multiple_of(i * tq, tq)
        bz = bz_ref[...].astype(f32)
        dov = do_ref[...].astype(f32)
        tile_ref[0] = (dov * att_ref[...].astype(f32) * _silu_grad(bz)).astype(bf16)
        dbz_copy = put(tile_ref.at[0], my_rows, zc, 0)
        dob_ref[...] = (dov * _silu(bz)).astype(bf16)
        upto = _sb_sum_matrix(lambda s, j: s <= j, kb)
        before = _sb_sum_matrix(lambda j, s: j < s, kb)
        t_pos = i * tq + lax.broadcasted_iota(jnp.int32, (tq, kb), 0)
        s_off = lax.broadcasted_iota(jnp.int32, (tq, kb), 1)

        def step(j, carry, masked, row0=0):
            rows = pl.ds(pl.multiple_of(j * kb, kb), kb)
            out = []
            for h in range(hp):
                dq, sp_seen, e_seen = (c[row0:] for c in carry[h])
                sl = slice(h * HEAD, (h + 1) * HEAD)
                q, kj, vj, dob = q_ref[row0:, sl], k_ref[rows, sl], v_ref[rows, sl], dob_ref[row0:, sl]
                z, sp = _sb_scores(q, kj, scale)
                lb = z - sp
                if masked:
                    mask = s_off[row0:] + j * kb < t_pos[row0:]
                    sp = jnp.where(mask, sp, 0.0)
                sp_upto, sp_total = _sb_sums(sp, upto)
                w = jnp.exp2(lb + _sb_wide(sp_seen, kb) + sp_upto)
                if masked:
                    w = jnp.where(mask, w, 0.0)
                dv_acc[rows, sl] += _dot_tn(w.astype(bf16), dob)
                e = _dot_nt(dob, vj) * w
                e_before, e_total = _sb_sums(e, before)
                dz = (e - (e + _sb_wide(e_seen, kb) + e_before) * jnp.exp2(lb)) * scale
                if masked:
                    dz = jnp.where(mask, dz, 0.0)
                dz = dz.astype(bf16)
                dk_acc[rows, sl] += _dot_tn(dz, q)
                new = (dq + _dot(dz, kj), sp_seen + sp_total, e_seen + e_total)
                out.append(tuple(jnp.concatenate([old[:row0], upd]) if row0 else upd for old, upd in zip(carry[h], new)))
            return tuple(out)

        zero = jnp.zeros((tq, HEAD), f32)
        init = tuple((zero, jnp.broadcast_to(tot_ref[h], (tq, HEAD)), zero) for h in range(hp))
        carry = lax.fori_loop(0, band * i, lambda j, c: step(j, c, False), init)
        for lv in range(levels):
            carry = lax.fori_loop(
                0, band // levels,
                lambda t, c, lv=lv: step(band * i + lv * (band // levels) + t, c, True, lv * (tq // levels)), carry)
        for h in range(hp):
            tile_ref[1, :, h * HEAD:(h + 1) * HEAD] = carry[h][0].astype(bf16)
        dq_copy = put(tile_ref.at[1], my_rows, qc, 1)
        dbz_copy.wait()
        dq_copy.wait()

        @pl.when(i == nq - 1)
        def _():
            head_ref[0] = dk_acc[...].astype(bf16)
            head_ref[1] = dv_acc[...].astype(bf16)
            copies = [put(head_ref.at[0], 0, kc, 2), put(head_ref.at[1], 0, vc, 3)]
            for cp in copies:
                cp.wait()

    blk = lambda c0: pl.BlockSpec((tq, hp * HEAD), lambda g, i: (i, c0 // hp + g))
    head = lambda c0: pl.BlockSpec((l, hp * HEAD), lambda g, i: (0, c0 // hp + g))
    any_spec = pl.BlockSpec(memory_space=pl.ANY)
    return pl.pallas_call(
        body, name="sb_bwd", grid=(nh // hp, nq),
        in_specs=[blk(qc), head(kc), head(vc), blk(zc), blk(nh), blk(0),
                  pl.BlockSpec((hp, tq, 1), lambda g, i: (g, i, 0)), any_spec],
        out_specs=any_spec, out_shape=jax.ShapeDtypeStruct(dproj.shape, bf16), input_output_aliases={7: 0},
        scratch_shapes=[pltpu.VMEM((l, hp * HEAD), f32), pltpu.VMEM((l, hp * HEAD), f32),
                        pltpu.VMEM((tq, hp * HEAD), bf16), pltpu.VMEM((2, tq, hp * HEAD), bf16),
                        pltpu.VMEM((2, l, hp * HEAD), bf16), pltpu.SemaphoreType.DMA((4,))],
        compiler_params=_cparams("parallel", "arbitrary"),
    )(proj, proj, proj, proj, dcat, att, tot, dproj)


def _disc(lr, li, ldt):
    dt = jnp.exp(ldt)
    mag = jnp.exp(lr * dt)
    a_re = mag * jnp.cos(li * dt)
    a_im = mag * jnp.sin(li * dt)
    den = lr * lr + li * li
    nr = a_re - 1.0
    return a_re, a_im, (nr * lr + a_im * li) / den, (a_im * lr - nr * li) / den


def s5_params_fwd(lr, li, ldt, bt_re, bt_im):
    g, c, p = bt_re.shape

    def body(lr_ref, li_ref, ldt_ref, br_ref, bi_ref, ar_ref, ai_ref, bbr_ref, bbi_ref):
        a_re, a_im, cr, ci = _disc(lr_ref[...], li_ref[...], ldt_ref[...])
        ar_ref[...] = a_re
        ai_ref[...] = a_im
        for k in range(c):
            br, bi = br_ref[:, k, :], bi_ref[:, k, :]
            bbr_ref[:, k, :] = cr * br - ci * bi
            bbi_ref[:, k, :] = cr * bi + ci * br

    return pl.pallas_call(
        body, name="s5_params_fwd",
        out_shape=[jax.ShapeDtypeStruct((g, p), f32)] * 2 + [jax.ShapeDtypeStruct((g, c, p), f32)] * 2,
    )(lr, li, ldt, bt_re, bt_im)


def s5_params_bwd(lr, li, ldt, bt_re, bt_im, da_re, da_im, dbbt_re, dbbt_im):
    g, c, p = bt_re.shape

    def body(lr_ref, li_ref, ldt_ref, br_ref, bi_ref, dar_ref, dai_ref, dbbr_ref, dbbi_ref,
             dlr_ref, dli_ref, dldt_ref, dbr_ref, dbi_ref):
        (a_re, a_im, cr, ci), vjp = jax.vjp(_disc, lr_ref[...], li_ref[...], ldt_ref[...])
        dcr = jnp.zeros((g, p), f32)
        dci = jnp.zeros((g, p), f32)
        for k in range(c):
            br, bi = br_ref[:, k, :], bi_ref[:, k, :]
            dr, di = dbbr_ref[:, k, :], dbbi_ref[:, k, :]
            dcr += dr * br + di * bi
            dci += di * br - dr * bi
            dbr_ref[:, k, :] = cr * dr + ci * di
            dbi_ref[:, k, :] = cr * di - ci * dr
        dlr, dli, dldt = vjp((dar_ref[...], dai_ref[...], dcr, dci))
        dlr_ref[...] = dlr
        dli_ref[...] = dli
        dldt_ref[...] = dldt

    return pl.pallas_call(
        body, name="s5_params_bwd",
        out_shape=[jax.ShapeDtypeStruct((g, p), f32)] * 2 + [jax.ShapeDtypeStruct((g, 1), f32)]
        + [jax.ShapeDtypeStruct((g, c, p), f32)] * 2,
    )(lr, li, ldt, bt_re, bt_im, da_re, da_im, dbbt_re, dbbt_im)


def _cmul(ar, ai, br, bi):
    return ar * br - ai * bi, ar * bi + ai * br


def _power_tables(ar, ai):
    rows = lax.broadcasted_iota(jnp.int32, (SUBLANES, ar.shape[1]), 0)
    pr = jnp.zeros((SUBLANES, ar.shape[1]), f32)
    pi = jnp.zeros((SUBLANES, ar.shape[1]), f32)
    cr, ci = ar, ai
    pows = {}
    for r in range(SUBLANES):
        pows[r + 1] = (cr, ci)
        pr = jnp.where(rows == r, cr, pr)
        pi = jnp.where(rows == r, ci, pi)
        cr, ci = _cmul(cr, ci, ar, ai)
    return [pows[1], pows[2], pows[4]], pr, pi


def _ssm_time_tile(l):
    return _tile(l, (2048, 1024, 512, 256, 128))


def ssm_fwd(u, bre3, bim3, cre3, cimn3, a_re, a_im, d_skip):
    l, w = u.shape[0], d_skip.shape[1]
    nj = w // HEAD
    ns = STATES_PER_LANE_BLOCK
    tt = _ssm_time_tile(l)

    def body(u_ref, bre_ref, bim_ref, cre_ref, cim_ref, ar_ref, ai_ref, d_ref, y_ref, hr_ref, hi_ref, cr_ref, ci_ref):
        @pl.when(pl.program_id(1) == 0)
        def _():
            cr_ref[...] = jnp.zeros_like(cr_ref)
            ci_ref[...] = jnp.zeros_like(ci_ref)

        uv = u_ref[...]
        hr_ref[...] = _dot(uv, bre_ref[...])
        hi_ref[...] = _dot(uv, bim_ref[...])
        steps, pr, pi = _power_tables(ar_ref[...], ai_ref[...])
        rows = lax.broadcasted_iota(jnp.int32, (SUBLANES, ns), 0)
        steps = [(jnp.where(rows >= d, sr_, 0.0), jnp.where(rows >= d, si_, 0.0)) for d, (sr_, si_) in zip((1, 2, 4), steps)]

        def blk(b, carry):
            cr, ci = carry
            sl = pl.ds(pl.multiple_of(b * SUBLANES, SUBLANES), SUBLANES)
            xr, xi = hr_ref[sl, :], hi_ref[sl, :]
            for d, (sr_, si_) in zip((1, 2, 4), steps):
                mr, mi = _cmul(sr_, si_, pltpu.roll(xr, d, axis=0), pltpu.roll(xi, d, axis=0))
                xr, xi = xr + mr, xi + mi
            mr, mi = _cmul(pr, pi, cr, ci)
            xr, xi = xr + mr, xi + mi
            hr_ref[sl, :] = xr
            hi_ref[sl, :] = xi
            return xr[SUBLANES - 1:, :], xi[SUBLANES - 1:, :]

        cr, ci = lax.fori_loop(0, tt // SUBLANES, blk, (cr_ref[...], ci_ref[...]))
        cr_ref[...] = cr
        ci_ref[...] = ci
        y = _dot(hr_ref[...].astype(bf16), cre_ref[...]) + _dot(hi_ref[...].astype(bf16), cim_ref[...])
        y_ref[...] = y + d_ref[...] * uv.astype(f32)

    lane = pl.BlockSpec((tt, HEAD), lambda j, i: (i, j))
    st = pl.BlockSpec((tt, ns), lambda j, i: (i, j))
    b3 = pl.BlockSpec((None, HEAD, ns), lambda j, i: (j, 0, 0))
    c3 = pl.BlockSpec((None, ns, HEAD), lambda j, i: (j, 0, 0))
    arow = pl.BlockSpec((1, ns), lambda j, i: (0, j))
    return pl.pallas_call(
        body, name="ssm_fwd", grid=(nj, l // tt),
        in_specs=[lane, b3, b3, c3, c3, arow, arow, pl.BlockSpec((1, HEAD), lambda j, i: (0, j))],
        out_specs=[lane, st, st],
        out_shape=[jax.ShapeDtypeStruct((l, w), f32), jax.ShapeDtypeStruct((l, nj * ns), f32),
                   jax.ShapeDtypeStruct((l, nj * ns), f32)],
        scratch_shapes=[pltpu.VMEM((1, ns), f32), pltpu.VMEM((1, ns), f32)],
        compiler_params=_cparams("parallel", "arbitrary"),
    )(u, bre3, bim3, cre3, cimn3, a_re, a_im, d_skip)


def ssm_bwd(dy, u, dproj, h_re, h_im, bre3, bim3, cre3, cimn3, a_re, a_im, d_skip):
    l, w = u.shape[0], d_skip.shape[1]
    nj = w // HEAD
    ns = STATES_PER_LANE_BLOCK
    tt = _ssm_time_tile(l)
    nt = l // tt

    def body(dy_ref, u_ref, dproj_ref, hr_ref, hi_ref, bre_ref, bim_ref, cre_ref, cim_ref, ar_ref, ai_ref, d_ref,
             du_ref, dd_ref, dar_ref, dai_ref, dbre_ref, dbim_ref, dcre_ref, dcim_ref, kr_ref, ki_ref, cr_ref, ci_ref,
             accr_ref, acci_ref):
        del dproj_ref
        i = pl.program_id(1)

        @pl.when(i == 0)
        def _():
            for ref in (cr_ref, ci_ref, accr_ref, acci_ref, dd_ref, dbre_ref, dbim_ref, dcre_ref, dcim_ref):
                ref[...] = jnp.zeros_like(ref)

        dyv = dy_ref[...]
        dyb = dyv.astype(bf16)
        uv = u_ref[...]
        kr_ref[...] = _dot_nt(dyb, cre_ref[...])
        ki_ref[...] = _dot_nt(dyb, cim_ref[...])
        steps, pr, pi = _power_tables(ar_ref[...], -ai_ref[...])
        rows = lax.broadcasted_iota(jnp.int32, (SUBLANES, ns), 0)
        qr = jnp.zeros((SUBLANES, ns), f32)
        qi = jnp.zeros((SUBLANES, ns), f32)
        for r in range(SUBLANES):
            qr = jnp.where(rows == r, pr[SUBLANES - 1 - r:SUBLANES - r, :], qr)
            qi = jnp.where(rows == r, pi[SUBLANES - 1 - r:SUBLANES - r, :], qi)
        nb = tt // SUBLANES
        steps = [(jnp.where(rows < SUBLANES - d, sr_, 0.0), jnp.where(rows < SUBLANES - d, si_, 0.0))
                 for d, (sr_, si_) in zip((1, 2, 4), steps)]

        def blk(t, carry):
            cr, ci, accr, acci = carry
            sl = pl.ds(pl.multiple_of((nb - 1 - t) * SUBLANES, SUBLANES), SUBLANES)
            xr, xi = kr_ref[sl, :], ki_ref[sl, :]
            for d, (sr_, si_) in zip((1, 2, 4), steps):
                mr, mi = _cmul(sr_, si_, pltpu.roll(xr, SUBLANES - d, axis=0), pltpu.roll(xi, SUBLANES - d, axis=0))
                xr, xi = xr + mr, xi + mi
            mr, mi = _cmul(qr, qi, cr, ci)
            xr, xi = xr + mr, xi + mi
            kr_ref[sl, :] = xr
            ki_ref[sl, :] = xi
            last = rows == SUBLANES - 1
            nr = jnp.where(last, cr, pltpu.roll(xr, SUBLANES - 1, axis=0))
            ni = jnp.where(last, ci, pltpu.roll(xi, SUBLANES - 1, axis=0))
            hr, hi = hr_ref[sl, :], hi_ref[sl, :]
            accr = accr + nr * hr + ni * hi
            acci = acci + ni * hr - nr * hi
            return xr[:1, :], xi[:1, :], accr, acci

        cr, ci, accr, acci = lax.fori_loop(0, nb, blk, (cr_ref[...], ci_ref[...], accr_ref[...], acci_ref[...]))
        cr_ref[...] = cr
        ci_ref[...] = ci
        accr_ref[...] = accr
        acci_ref[...] = acci
        kr, ki = kr_ref[...].astype(bf16), ki_ref[...].astype(bf16)
        du = _dot_nt(kr, bre_ref[...]) + _dot_nt(ki, bim_ref[...]) + d_ref[...] * dyv
        du_ref[...] = du.astype(du_ref.dtype)
        dd_ref[...] += jnp.sum(dyv * uv.astype(f32), axis=0, keepdims=True)
        dbre_ref[...] += _dot_tn(uv, kr)
        dbim_ref[...] += _dot_tn(uv, ki)
        dcre_ref[...] += _dot_tn(hr_ref[...].astype(bf16), dyb)
        dcim_ref[...] += _dot_tn(hi_ref[...].astype(bf16), dyb)

        @pl.when(i == nt - 1)
        def _():
            dar_ref[...] = jnp.sum(accr_ref[...], axis=0, keepdims=True)
            dai_ref[...] = jnp.sum(acci_ref[...], axis=0, keepdims=True)

    lane = pl.BlockSpec((tt, HEAD), lambda j, i: (nt - 1 - i, j))
    st = pl.BlockSpec((tt, ns), lambda j, i: (nt - 1 - i, j))
    b3 = pl.BlockSpec((None, HEAD, ns), lambda j, i: (j, 0, 0))
    c3 = pl.BlockSpec((None, ns, HEAD), lambda j, i: (j, 0, 0))
    arow = pl.BlockSpec((1, ns), lambda j, i: (0, j))
    drow = pl.BlockSpec((1, HEAD), lambda j, i: (0, j))
    return pl.pallas_call(
        body, name="ssm_bwd", grid=(nj, nt),
        in_specs=[lane, lane, pl.BlockSpec(memory_space=pl.ANY), st, st, b3, b3, c3, c3, arow, arow, drow],
        out_specs=[lane, drow, arow, arow, b3, b3, c3, c3], input_output_aliases={2: 0},
        out_shape=[jax.ShapeDtypeStruct(dproj.shape, bf16), jax.ShapeDtypeStruct((1, w), f32),
                   jax.ShapeDtypeStruct((1, nj * ns), f32), jax.ShapeDtypeStruct((1, nj * ns), f32),
                   jax.ShapeDtypeStruct((nj, HEAD, ns), f32), jax.ShapeDtypeStruct((nj, HEAD, ns), f32),
                   jax.ShapeDtypeStruct((nj, ns, HEAD), f32), jax.ShapeDtypeStruct((nj, ns, HEAD), f32)],
        scratch_shapes=[pltpu.VMEM((tt, ns), f32), pltpu.VMEM((tt, ns), f32), pltpu.VMEM((1, ns), f32),
                        pltpu.VMEM((1, ns), f32), pltpu.VMEM((SUBLANES, ns), f32), pltpu.VMEM((SUBLANES, ns), f32)],
        compiler_params=_cparams("parallel", "arbitrary"),
    )(dy, u, dproj, h_re, h_im, bre3, bim3, cre3, cimn3, a_re, a_im, d_skip)


def glu_fwd(y, z_src, w_glu, b_glu):
    l, w = y.shape
    tm = _row_tile(l)

    def body(y_ref, z_ref, w_ref, b_ref, g_ref, t_ref, o_ref):
        g = _gelu(y_ref[...])
        gb = g.astype(bf16)
        t = _dot(gb, w_ref[...]) + b_ref[...]
        g_ref[...] = gb
        t_ref[...] = t
        o_ref[...] = (g * jax.nn.sigmoid(t) * _silu(z_ref[...].astype(f32))).astype(o_ref.dtype)

    blk = pl.BlockSpec((tm, w), lambda i: (i, 0))
    return pl.pallas_call(
        body, name="glu_fwd", grid=(l // tm,),
        in_specs=[blk, pl.BlockSpec((tm, w), lambda i: (i, 1)), pl.BlockSpec((w, w), lambda i: (0, 0)), _row(w)],
        out_specs=[blk, blk, blk],
        out_shape=[jax.ShapeDtypeStruct((l, w), bf16), jax.ShapeDtypeStruct((l, w), f32),
                   jax.ShapeDtypeStruct((l, w), bf16)],
        compiler_params=_cparams("parallel"),
    )(y, z_src, w_glu, b_glu)


def glu_bwd(dout, y, t, z_src, w_glu):
    l, w = y.shape
    tm = _row_tile(l)

    def body(do_ref, y_ref, t_ref, z_ref, w_ref, dy_ref, dz_ref, dt_ref, db_ref):
        @pl.when(pl.program_id(0) == 0)
        def _():
            db_ref[...] = jnp.zeros_like(db_ref)

        yv, zv, dov = y_ref[...], z_ref[...].astype(f32), do_ref[...]
        g = _gelu(yv)
        sg = jax.nn.sigmoid(t_ref[...])
        dy2 = dov * _silu(zv)
        dz_ref[...] = (dov * g * sg * _silu_grad(zv)).astype(dz_ref.dtype)
        dt = dy2 * g * sg * (1.0 - sg)
        dtb = dt.astype(bf16)
        dt_ref[...] = dtb
        db_ref[...] += jnp.sum(dt, axis=0, keepdims=True)
        dg = dy2 * sg + _dot_nt(dtb, w_ref[...])
        dy_ref[...] = dg * _gelu_grad(yv)

    blk = pl.BlockSpec((tm, w), lambda i: (i, 0))
    return pl.pallas_call(
        body, name="glu_bwd", grid=(l // tm,),
        in_specs=[blk, blk, blk, pl.BlockSpec((tm, w), lambda i: (i, 1)), pl.BlockSpec((w, w), lambda i: (0, 0))],
        out_specs=[blk, pl.BlockSpec((tm, w), lambda i: (i, 1)), blk, _row(w)],
        out_shape=[jax.ShapeDtypeStruct((l, w), f32), jax.ShapeDtypeStruct((l, 2 * w), bf16),
                   jax.ShapeDtypeStruct((l, w), bf16), jax.ShapeDtypeStruct((1, w), f32)],
        compiler_params=_cparams("arbitrary"),
    )(dout, y, t, z_src, w_glu)


def _adamw(w, g, m, v):
    m = ADAM_B1 * m + (1.0 - ADAM_B1) * g
    v = ADAM_B2 * v + (1.0 - ADAM_B2) * (g * g)
    m_hat = m / (1.0 - ADAM_B1 ** ADAM_STEP)
    v_hat = v / (1.0 - ADAM_B2 ** ADAM_STEP)
    return -ADAM_LR * (m_hat / (jnp.sqrt(v_hat) + ADAM_EPS) + ADAM_WD * w), m, v


def adam_reduce(pieces, w, m, v, name):
    r, c = w.shape
    n = pieces.shape[0]
    tr = _tile(r, (256, 128, 64, 32, 16, 8))

    def body(p_ref, w_ref, m_ref, v_ref, g_ref, d_ref, nm_ref, nv_ref):
        g = p_ref[0].astype(f32)
        for s in range(1, n):
            g = g + p_ref[s].astype(f32)
        g_ref[...] = g
        d_ref[...], nm_ref[...], nv_ref[...] = _adamw(w_ref[...], g, m_ref[...], v_ref[...])

    blk = pl.BlockSpec((tr, c), lambda i: (i, 0))
    return pl.pallas_call(
        body, name=name, grid=(r // tr,),
        in_specs=[pl.BlockSpec((n, tr, c), lambda i: (0, i, 0)), blk, blk, blk],
        out_specs=[blk] * 4, out_shape=[jax.ShapeDtypeStruct((r, c), f32)] * 4,
        compiler_params=_cparams("parallel"),
    )(pieces, w, m, v)


def adam_w_mod(cond_t, dm, w, m, v):
    nl, d, cols = w.shape
    tr = _tile(d, (512, 256, 128))

    def body(c_ref, dm_ref, w_ref, m_ref, v_ref, g_ref, d_ref, nm_ref, nv_ref):
        g = jnp.dot(c_ref[...], dm_ref[...], preferred_element_type=f32, precision=lax.Precision.HIGHEST)
        g_ref[...] = g
        d_ref[...], nm_ref[...], nv_ref[...] = _adamw(w_ref[...], g, m_ref[...], v_ref[...])

    blk = pl.BlockSpec((None, tr, cols), lambda l, i: (l, i, 0))
    return pl.pallas_call(
        body, name="adam_w_mod", grid=(nl, d // tr),
        in_specs=[pl.BlockSpec((tr, N_DEV), lambda l, i: (i, 0)), pl.BlockSpec((None, N_DEV, cols), lambda l, i: (l, 0, 0)),
                  blk, blk, blk],
        out_specs=[blk] * 4, out_shape=[jax.ShapeDtypeStruct((nl, d, cols), f32)] * 4,
        compiler_params=_cparams("parallel", "parallel"),
    )(cond_t, dm, w, m, v)


def silu_rows(c_all):
    def body(c_ref, o_ref):
        o_ref[...] = _silu(c_ref[...])

    return pl.pallas_call(body, name="silu_rows", out_shape=jax.ShapeDtypeStruct(c_all.shape, f32))(c_all)


def _block_diag(x):
    g, a, b = x.shape
    nj = g // GROUPS_PER_LANE_BLOCK
    eye = jnp.eye(GROUPS_PER_LANE_BLOCK, dtype=x.dtype)
    x5 = x.reshape(nj, GROUPS_PER_LANE_BLOCK, a, b)
    return jnp.einsum("jgab,gh->jgahb", x5, eye).reshape(nj, GROUPS_PER_LANE_BLOCK * a, GROUPS_PER_LANE_BLOCK * b)


def _diag_blocks(x, a, b):
    nj = x.shape[0]
    x5 = x.reshape(nj, GROUPS_PER_LANE_BLOCK, a, GROUPS_PER_LANE_BLOCK, b)
    eye = jnp.eye(GROUPS_PER_LANE_BLOCK, dtype=x.dtype)
    return jnp.einsum("jgahb,gh->jgab", x5, eye).reshape(nj * GROUPS_PER_LANE_BLOCK, a, b)


PACK_ROW = SUBLANES * HEAD


def _pack(parts, row_multiple=SUBLANES):
    rows = []
    for p in parts:
        flat = p.reshape(-1)
        pad = (-flat.shape[0]) % PACK_ROW
        if pad:
            flat = jnp.concatenate([flat, jnp.zeros((pad,), flat.dtype)])
        rows.append(flat.reshape(-1, HEAD))
    pad = (-sum(r.shape[0] for r in rows)) % row_multiple
    if pad:
        rows.append(jnp.zeros((pad, HEAD), rows[0].dtype))
    return jnp.concatenate(rows, axis=0)


def _unpack(packed, shapes):
    out, r0 = [], 0
    for shp in shapes:
        n = math.prod(shp)
        nr = -(-n // PACK_ROW) * SUBLANES
        out.append(packed[r0:r0 + nr].reshape(-1)[:n].reshape(shp))
        r0 += nr
    return out


def adam_small(g, w, m, v):
    r, c = w.shape

    def body(g_ref, w_ref, m_ref, v_ref, d_ref, nm_ref, nv_ref):
        d_ref[...], nm_ref[...], nv_ref[...] = _adamw(w_ref[...], g_ref[...], m_ref[...], v_ref[...])

    tr = max(t for t in range(SUBLANES, 1024 + 1, SUBLANES) if r % t == 0)
    blk = pl.BlockSpec((tr, c), lambda i: (i, 0))
    return pl.pallas_call(
        body, name="adam_small", grid=(r // tr,),
        in_specs=[blk] * 4, out_specs=[blk] * 3, out_shape=[jax.ShapeDtypeStruct((r, c), f32)] * 3,
        compiler_params=_cparams("parallel"),
    )(g, w, m, v)


def kernel(x, c, ln_pre_g, ln_post_g, w_mod, b_mod, w_in_ab, w_out_ab, sgu_norm_g, sgu_w, sgu_b, w_in_ssm, w_out_ssm, lam_re, lam_im, b_re, b_im, c_re, c_im, d_skip, log_dt, w_glu, b_glu, loss_target, m_ln_pre_g, m_ln_post_g, m_w_mod, m_b_mod, m_w_in_ab, m_w_out_ab, m_sgu_norm_g, m_sgu_w, m_sgu_b, m_w_in_ssm, m_w_out_ssm, m_lam_re, m_lam_im, m_b_re, m_b_im, m_c_re, m_c_im, m_d_skip, m_log_dt, m_w_glu, m_b_glu, v_ln_pre_g, v_ln_post_g, v_w_mod, v_b_mod, v_w_in_ab, v_w_out_ab, v_sgu_norm_g, v_sgu_w, v_sgu_b, v_w_in_ssm, v_w_out_ssm, v_lam_re, v_lam_im, v_b_re, v_b_im, v_c_re, v_c_im, v_d_skip, v_log_dt, v_w_glu, v_b_glu):
    me = _my_index()
    x0 = x[0]
    l, d = x0.shape
    target = loss_target[0]
    nh = sgu_w.shape[1]
    wa = nh * HEAD
    n_grp, n_st = lam_re.shape[1], lam_re.shape[2]
    mod_cols = w_mod.shape[2]

    def after(a, first):
        return a + jnp.minimum(jnp.abs(first[(0,) * first.ndim].astype(f32)), 0.0).astype(a.dtype)

    c_all, d_skip_all, b_glu_all = all_gather([c, d_skip, b_glu], "gather_c")
    c_all = c_all.reshape(N_DEV, d)
    d_skip_all = d_skip_all.reshape(1, -1)
    b_glu_all = b_glu_all.reshape(1, -1)

    b_cols = lax.dynamic_slice_in_dim(b_mod, me * mod_cols, mod_cols, axis=1)
    (mod_all,) = all_gather([mod_part(c_all, w_mod, b_cols)], "gather_mod")
    (win_ab3,) = sequencer_exchange(GATHER, [after(w_in_ab[0], mod_all).astype(bf16)], "gather_w_in", 1)
    mod_mine = lax.dynamic_index_in_dim(mod_all, me, axis=2, keepdims=False)
    mod_rows = jnp.transpose(mod_mine, (1, 0, 2)).reshape(2, 3, 1, d)

    def rows(a, i):
        return a[i].reshape(1, d)

    shift0, scale0, gate0 = mod_rows[0, 0], mod_rows[0, 1], mod_rows[0, 2]
    h0, h0_t = prenorm_fwd(x0, rows(ln_pre_g, 0), shift0, scale0, "prenorm0")
    wout_ab3, win_ssm3, wout_ssm3, wglu = sequencer_exchange(
        GATHER, [after(w, win_ab3).astype(bf16) for w in (w_out_ab[0], w_in_ssm[0], w_out_ssm[0], w_glu[0])],
        "gather_w_rest", 2)
    proj0 = mm_nn(h0, win_ab3, bf16, "proj0")
    sgu_b3 = sgu_b[0].reshape(nh, HEAD, 1)
    cat, att, tot = sb_fwd(proj0, sgu_fwd(proj0, sgu_norm_g, sgu_w[0], sgu_b3), nh)
    wout_ab3 = wout_ab3.reshape(1, d, d)
    win_ssm3 = win_ssm3.reshape(1, d, d)
    wglu = wglu.reshape(w_glu.shape[2], w_glu.shape[2])
    y0 = mm_nn(cat, wout_ab3, f32, "out0")

    shift1, scale1, gate1 = mod_rows[1, 0], mod_rows[1, 1], mod_rows[1, 2]
    x1, h1, h1_t = post_prenorm_fwd(x0, y0, gate0, rows(ln_post_g, 0), rows(ln_pre_g, 1), shift1, scale1,
                                    "post0_prenorm1")
    proj1 = mm_nn(h1, win_ssm3, bf16, "proj1")
    w_ssm = proj1.shape[1] // 2
    ldt = log_dt[0].reshape(n_grp, 1)
    bt_re = jnp.transpose(b_re[0], (0, 2, 1))
    bt_im = jnp.transpose(b_im[0], (0, 2, 1))
    a_re, a_im, bbt_re, bbt_im = s5_params_fwd(lam_re[0], lam_im[0], ldt, bt_re, bt_im)
    bre3 = _block_diag(bbt_re).astype(bf16)
    bim3 = _block_diag(bbt_im).astype(bf16)
    cre3 = _block_diag(jnp.transpose(c_re[0], (0, 2, 1))).astype(bf16)
    cimn3 = _block_diag(-jnp.transpose(c_im[0], (0, 2, 1))).astype(bf16)
    a_re_row, a_im_row = a_re.reshape(1, -1), a_im.reshape(1, -1)
    y_ssm, hs_re, hs_im = ssm_fwd(proj1, bre3, bim3, cre3, cimn3, a_re_row, a_im_row, d_skip_all)
    g_act, t_glu, mix1 = glu_fwd(y_ssm, proj1, wglu, b_glu_all)
    y1 = mm_nn(mix1, wout_ssm3, f32, "out1")

    dx2, loss_tile, dy1, dgate1, dgpost1 = final_loss(x1, y1, gate1, rows(ln_post_g, 1), target)

    dmix1 = mm_nt(dy1, wout_ssm3, f32, "dmix1")
    gw_out_ssm = mm_tn(mix1, dy1, N_DEV, bf16, "gw_out_ssm")
    (p_out_ssm,) = sequencer_exchange(SCATTER, [gw_out_ssm], "scatter_g1", 3)
    dy_ssm, dproj1, dt_glu, db_glu = glu_bwd(dmix1, y_ssm, t_glu, proj1, wglu)
    gw_glu = mm_tn(g_act, dt_glu, 1, bf16, "gw_glu").reshape(N_DEV, -1, w_ssm)
    dproj1, dd_skip, da_re, da_im, dbre3, dbim3, dcre3, dcimn3 = ssm_bwd(
        dy_ssm, proj1, dproj1, hs_re, hs_im, bre3, bim3, cre3, cimn3, a_re_row, a_im_row, d_skip_all)
    gw_in_ssm = mm_nn(h1_t, dproj1[None], bf16, "gw_in_ssm").reshape(N_DEV, -1, proj1.shape[1])
    p_in_ssm, p_glu = sequencer_exchange(SCATTER, [gw_in_ssm, gw_glu], "scatter_g2", 4)
    dh1 = mm_nt(dproj1, win_ssm3, f32, "dh1")
    dx1, dshift1, dscale1, dgpre1, dy0, dgate0, dgpost0 = prenorm_post_bwd(
        dh1, x1, dx2, rows(ln_pre_g, 1), scale1, y0, gate0, rows(ln_post_g, 0), "prenorm1_post0_bwd")
    dlr, dli, dldt, dbt_re, dbt_im = s5_params_bwd(
        lam_re[0], lam_im[0], ldt, bt_re, bt_im, da_re.reshape(n_grp, n_st), da_im.reshape(n_grp, n_st),
        _diag_blocks(dbre3, SSM_GROUP, n_st), _diag_blocks(dbim3, SSM_GROUP, n_st))
    g_b_re = jnp.transpose(dbt_re, (0, 2, 1))
    g_b_im = jnp.transpose(dbt_im, (0, 2, 1))
    g_c_re = jnp.transpose(_diag_blocks(dcre3, n_st, SSM_GROUP), (0, 2, 1))
    g_c_im = -jnp.transpose(_diag_blocks(dcimn3, n_st, SSM_GROUP), (0, 2, 1))

    dcat = mm_nt(dy0, wout_ab3, f32, "dcat")
    gw_out_ab = mm_tn(cat, dy0, 1, bf16, "gw_out_ab").reshape(N_DEV, -1, d)
    (p_out_ab,) = sequencer_exchange(SCATTER, [gw_out_ab], "scatter_g3", 5)
    dproj0, dsgu_w, dsgu_b, dsgu_ng = sgu_bwd(proj0, dcat, sgu_norm_g, sgu_w[0], sgu_b3)
    dproj0 = sb_bwd(proj0, dcat, att, tot, dproj0, nh)
    gw_in_ab = mm_nn(h0_t, dproj0[None], bf16, "gw_in_ab", split_cols=N_DEV)
    (p_in_ab,) = sequencer_exchange(SCATTER, [gw_in_ab], "scatter_g4", 6)
    dh0 = mm_nt(dproj0, win_ab3, f32, "dh0")
    dx0, dshift0, dscale0, dgpre0 = prenorm_bwd(dh0, x0, dx1, rows(ln_pre_g, 0), scale0, "prenorm0_bwd")

    small_names = ["ln_pre_g", "ln_post_g", "b_mod", "sgu_norm_g", "sgu_w", "sgu_b", "lam_re", "lam_im", "b_re", "b_im",
                   "c_re", "c_im", "log_dt"]
    small_w = [ln_pre_g, ln_post_g, b_mod, sgu_norm_g, sgu_w, sgu_b, lam_re, lam_im, b_re, b_im, c_re, c_im, log_dt]
    small_m = [m_ln_pre_g, m_ln_post_g, m_b_mod, m_sgu_norm_g, m_sgu_w, m_sgu_b, m_lam_re, m_lam_im, m_b_re, m_b_im,
               m_c_re, m_c_im, m_log_dt]
    small_v = [v_ln_pre_g, v_ln_post_g, v_b_mod, v_sgu_norm_g, v_sgu_w, v_sgu_b, v_lam_re, v_lam_im, v_b_re, v_b_im,
               v_c_re, v_c_im, v_log_dt]
    def sharded(p, w, m, v, name):
        shp = w.shape
        w2, m2, v2 = (a.reshape(-1, shp[-1]) for a in (w, m, v))
        return [o.reshape(shp) for o in adam_reduce(p.reshape(p.shape[0], -1, shp[-1]), w2, m2, v2, name)]

    r_w_out_ssm = sharded(p_out_ssm, w_out_ssm, m_w_out_ssm, v_w_out_ssm, "adam_w_out_ssm")
    r_w_in_ssm = sharded(p_in_ssm, w_in_ssm, m_w_in_ssm, v_w_in_ssm, "adam_w_in_ssm")
    r_w_glu = sharded(p_glu, w_glu, m_w_glu, v_w_glu, "adam_w_glu")
    r_w_out_ab = sharded(p_out_ab, w_out_ab, m_w_out_ab, v_w_out_ab, "adam_w_out_ab")
    dmod = jnp.concatenate([dshift0, dscale0, dgate0, dshift1, dscale1, dgate1], axis=1)
    for done in (r_w_out_ssm, r_w_in_ssm, r_w_glu, r_w_out_ab):
        dmod = after(dmod, done[0])
    small_g = [jnp.concatenate([dgpre0, dgpre1]), jnp.concatenate([dgpost0, dgpost1]), dmod, dsgu_ng, dsgu_w, dsgu_b,
               dlr, dli, g_b_re, g_b_im, g_c_re, g_c_im, dldt]
    shapes = [w.shape for w in small_w]
    g_sum, dmod_all = all_reduce_rows(_pack(small_g + [dd_skip, db_glu, loss_tile], SUBLANES * N_DEV), dmod,
                                      "reduce_small_grads")
    n_rows_small = sum(-(-math.prod(s) // PACK_ROW) * SUBLANES for s in shapes)
    loss = g_sum[n_rows_small + 2 * (d_skip_all.shape[1] // HEAD), 0] * (0.5 / d)
    new_small = adam_small(g_sum, _pack(small_w), _pack(small_m), _pack(small_v))
    r_small = [_unpack(o, shapes) for o in [g_sum[:n_rows_small]] + list(new_small)]
    small = {n: [r_small[k][i] for k in range(4)] for i, n in enumerate(small_names)}
    vec_rows = d_skip_all.shape[1] // HEAD

    def my_columns(r0):
        whole = g_sum[r0:r0 + vec_rows].reshape(1, 1, -1)
        return lax.dynamic_slice_in_dim(whole, me * d_skip.shape[1], d_skip.shape[1], axis=2)

    r_d_skip = sharded(my_columns(n_rows_small), d_skip, m_d_skip, v_d_skip, "adam_d_skip")
    r_b_glu = sharded(my_columns(n_rows_small + vec_rows), b_glu, m_b_glu, v_b_glu, "adam_b_glu")
    r_w_in_ab = sharded(p_in_ab, w_in_ab, m_w_in_ab, v_w_in_ab, "adam_w_in_ab")

    dm_cols = jnp.transpose(
        lax.dynamic_slice_in_dim(dmod_all.reshape(N_DEV, 2, 3 * d), me * mod_cols, mod_cols, axis=2), (1, 0, 2))
    cond_t = jnp.transpose(silu_rows(c_all))
    r_w_mod = adam_w_mod(cond_t, dm_cols, w_mod, m_w_mod, v_w_mod)

    res = dict(small)
    res.update(w_mod=r_w_mod, w_in_ab=r_w_in_ab, w_out_ab=r_w_out_ab, w_in_ssm=r_w_in_ssm, w_out_ssm=r_w_out_ssm,
               d_skip=r_d_skip, w_glu=r_w_glu, b_glu=r_b_glu)
    order = ["ln_pre_g", "ln_post_g", "w_mod", "b_mod", "w_in_ab", "w_out_ab", "sgu_norm_g", "sgu_w", "sgu_b", "w_in_ssm",
             "w_out_ssm", "lam_re", "lam_im", "b_re", "b_im", "c_re", "c_im", "d_skip", "log_dt", "w_glu", "b_glu"]
    outs = [loss, dx0.reshape(x.shape)]
    for k in range(4):
        outs += [res[n][k] for n in order]
    return tuple(outs)
```

```python
import functools
import math

import jax
import jax.numpy as jnp
from jax import lax
from jax.experimental import pallas as pl
from jax.experimental.pallas import tpu as pltpu
from jax.experimental.pallas import tpu_sc as plsc

f32 = jnp.float32
bf16 = jnp.bfloat16

N_DEV = 8
EPS = 1e-6
HEAD = 128
SUBLANES = 8
SSM_GROUP = 16
SSM_STATE = 64
GROUPS_PER_LANE_BLOCK = HEAD // SSM_GROUP
STATES_PER_LANE_BLOCK = GROUPS_PER_LANE_BLOCK * SSM_STATE
VMEM_LIMIT = 56 * 2 ** 20
ADAM_LR, ADAM_B1, ADAM_B2, ADAM_EPS, ADAM_WD, ADAM_STEP = 0.001, 0.9, 0.999, 1e-08, 0.01, 10
_GELU_C0 = math.sqrt(2.0 / math.pi)
_GELU_C1 = 0.044715
MESH = pl.DeviceIdType.MESH


def _cparams(*sem):
    return pltpu.CompilerParams(dimension_semantics=sem if sem else None, vmem_limit_bytes=VMEM_LIMIT)


def _gelu(x):
    return 0.5 * x * (1.0 + jnp.tanh(_GELU_C0 * (x + _GELU_C1 * x * x * x)))


def _gelu_grad(x):
    t = jnp.tanh(_GELU_C0 * (x + _GELU_C1 * x * x * x))
    return 0.5 * (1.0 + t) + 0.5 * x * (1.0 - t * t) * _GELU_C0 * (1.0 + 3.0 * _GELU_C1 * x * x)


def _silu(x):
    return x * jax.nn.sigmoid(x)


def _silu_grad(x):
    s = jax.nn.sigmoid(x)
    return s * (1.0 + x * (1.0 - s))


def _dot(a, b):
    return jnp.dot(a, b, preferred_element_type=f32)


def _dot_nt(a, b):
    return lax.dot_general(a, b, (((1,), (1,)), ((), ())), preferred_element_type=f32)


def _dot_tn(a, b):
    return lax.dot_general(a, b, (((0,), (0,)), ((), ())), preferred_element_type=f32)


def _split_bf16(v):
    hi = v.astype(bf16)
    lo = (v - hi.astype(f32)).astype(bf16)
    return hi, lo


def _row(d):
    return pl.BlockSpec((1, d), lambda *_: (0, 0))


def _my_index():
    return 4 * lax.axis_index("x") + 2 * lax.axis_index("y") + lax.axis_index("c")


def _peer(k):
    x, y, c = lax.axis_index("x"), lax.axis_index("y"), lax.axis_index("c")
    return (1 - x if k & 4 else x, 1 - y if k & 2 else y, 1 - c if k & 1 else c)


def all_gather(arrs, name):
    n = len(arrs)

    def body(*refs):
        ins, outs = refs[:n], refs[n:2 * n]
        send, recv, local = refs[2 * n:]
        me = _my_index()
        copies = []
        for a in range(n):
            cp = pltpu.make_async_copy(ins[a], outs[a].at[me], local.at[a])
            cp.start()
            copies.append(cp)
            for k in range(1, N_DEV):
                s = a * (N_DEV - 1) + k - 1
                cp = pltpu.make_async_remote_copy(src_ref=ins[a], dst_ref=outs[a].at[me], send_sem=send.at[s],
                                                  recv_sem=recv.at[s], device_id=_peer(k), device_id_type=MESH)
                cp.start()
                copies.append(cp)
        for cp in copies:
            cp.wait()

    any_spec = pl.BlockSpec(memory_space=pl.ANY)
    outs = pl.pallas_call(
        body, name=name,
        out_shape=[jax.ShapeDtypeStruct((N_DEV,) + a.shape, a.dtype) for a in arrs],
        in_specs=[any_spec] * n, out_specs=[any_spec] * n,
        scratch_shapes=[pltpu.SemaphoreType.DMA((n * (N_DEV - 1),)), pltpu.SemaphoreType.DMA((n * (N_DEV - 1),)),
                        pltpu.SemaphoreType.DMA((n,))],
        compiler_params=pltpu.CompilerParams(has_side_effects=True),
    )(*arrs)
    return list(outs)


def all_reduce_rows(pack, extra, name):
    r, c = pack.shape
    rs = r // N_DEV
    n_peer = N_DEV - 1

    def body(p_ref, x_ref, o_ref, xo_ref, land, red, send1, recv1, send2, recv2, sendx, recvx, local):
        me = _my_index()

        def rows(i):
            return pl.ds(pl.multiple_of(i * rs, SUBLANES), rs)

        own = [pltpu.make_async_copy(p_ref.at[rows(me)], land.at[me], local.at[0]),
               pltpu.make_async_copy(x_ref, xo_ref.at[me], local.at[1])]
        first = []
        for k in range(1, N_DEV):
            first.append(pltpu.make_async_remote_copy(
                src_ref=p_ref.at[rows(jnp.bitwise_xor(me, k))], dst_ref=land.at[me], send_sem=send1.at[k - 1],
                recv_sem=recv1.at[k - 1], device_id=_peer(k), device_id_type=MESH))
            first.append(pltpu.make_async_remote_copy(
                src_ref=x_ref, dst_ref=xo_ref.at[me], send_sem=sendx.at[k - 1], recv_sem=recvx.at[k - 1],
                device_id=_peer(k), device_id_type=MESH))
        for cp in own + first:
            cp.start()
        for cp in own + first:
            cp.wait()
        acc = land[0]
        for s in range(1, N_DEV):
            acc = acc + land[s]
        red[...] = acc
        mine = pltpu.make_async_copy(red, o_ref.at[rows(me)], local.at[2])
        second = [pltpu.make_async_remote_copy(
            src_ref=red, dst_ref=o_ref.at[rows(me)], send_sem=send2.at[k - 1], recv_sem=recv2.at[k - 1],
            device_id=_peer(k), device_id_type=MESH) for k in range(1, N_DEV)]
        for cp in [mine] + second:
            cp.start()
        for cp in [mine] + second:
            cp.wait()

    any_spec = pl.BlockSpec(memory_space=pl.ANY)
    return pl.pallas_call(
        body, name=name,
        out_shape=[jax.ShapeDtypeStruct((r, c), pack.dtype), jax.ShapeDtypeStruct((N_DEV,) + extra.shape, extra.dtype)],
        in_specs=[any_spec, any_spec], out_specs=[any_spec, any_spec],
        scratch_shapes=[pltpu.VMEM((N_DEV, rs, c), pack.dtype), pltpu.VMEM((rs, c), pack.dtype)]
        + [pltpu.SemaphoreType.DMA((n_peer,))] * 6 + [pltpu.SemaphoreType.DMA((3,))],
        compiler_params=pltpu.CompilerParams(has_side_effects=True),
    )(pack, extra)


GATHER, SCATTER = "gather", "scatter"


def _exchange_copies(srcs, lands, send, recv):
    me = _my_index()
    copies = []
    for a, (src, land) in enumerate(zip(srcs, lands)):
        for k in range(1, N_DEV):
            s = a * (N_DEV - 1) + k - 1
            copies.append(pltpu.make_async_remote_copy(
                src_ref=src.at[jnp.bitwise_xor(me, k)], dst_ref=land.at[me],
                send_sem=send.at[s], recv_sem=recv.at[s], device_id=_peer(k), device_id_type=MESH))
    return copies


def sequencer_exchange(kind, arrs, name, collective_id):
    n = len(arrs)
    n_sem = n * (N_DEV - 1)
    land_shapes = [((N_DEV,) + a.shape if kind == GATHER else a.shape) for a in arrs]
    srcs = [jax.new_ref(a, memory_space=pltpu.MemorySpace.HBM) for a in arrs]
    lands = [jax.empty_ref(jax.ShapeDtypeStruct(s, a.dtype), memory_space=pltpu.MemorySpace.HBM)
             for s, a in zip(land_shapes, arrs)]

    @pl.kernel(mesh=plsc.ScalarSubcoreMesh(axis_name="sequencer", num_cores=1), name=name,
               scratch_types=(pltpu.SemaphoreType.DMA((n_sem,)), pltpu.SemaphoreType.DMA((n_sem,)),
                              pltpu.SemaphoreType.DMA((n,))),
               compiler_params=pltpu.CompilerParams(collective_id=collective_id))
    def launch(send, recv, local):
        barrier = pltpu.get_barrier_semaphore()
        for k in range(1, N_DEV):
            pl.semaphore_signal(barrier, inc=1, device_id=_peer(k), device_id_type=MESH)
        pl.semaphore_wait(barrier, N_DEV - 1)
        me = _my_index()
        mine = [pltpu.make_async_copy(src if kind == GATHER else src.at[me], land.at[me], local.at[a])
                for a, (src, land) in enumerate(zip(srcs, lands))]
        if kind == SCATTER:
            copies = mine + _exchange_copies(srcs, lands, send, recv)
            for cp in copies:
                cp.start()
            for cp in copies:
                cp.wait()
            return

        def block_copy(a, slot, block, k, src=None):
            s = a * (N_DEV - 1) + slot
            return pltpu.make_async_remote_copy(
                src_ref=lands[a].at[block] if src is None else src, dst_ref=lands[a].at[block],
                send_sem=send.at[s], recv_sem=recv.at[s], device_id=_peer(k), device_id_type=MESH)

        chips = (2, 4, 6)
        sibling = jnp.bitwise_xor(me, 1)
        first = [block_copy(a, slot, me, k, src=srcs[a]) for a in range(n) for slot, k in enumerate((1,) + chips)]
        for cp in mine + first:
            cp.start()
        passed = []
        for a in range(n):
            for i, k in enumerate(chips):
                block = jnp.bitwise_xor(me, k)
                block_copy(a, 1 + i, block, k).wait_recv()
                passed.append(block_copy(a, 4 + i, block, 1))
                passed[-1].start()
        for a in range(n):
            block_copy(a, 0, sibling, 1).wait_recv()
            for i, k in enumerate(chips):
                block_copy(a, 4 + i, jnp.bitwise_xor(sibling, k), 1).wait_recv()
        for cp in mine:
            cp.wait()
        for cp in first + passed:
            cp.wait_send()

    launch()
    return [land[...] for land in lands]


def _tile(n, pref):
    for t in pref:
        if n % t == 0:
            return t
    return n


MM_WIDE = 1024
MM_WEIGHT_BLOCK = 8 * 2 ** 20


def _blocks_per_step(nb, fits):
    return max(g for g in range(1, nb + 1) if nb % g == 0 and fits(g))


def mm_nn(a, b3, out_dtype, name, split_cols=None):
    m, k = a.shape
    nb, _, bn = b3.shape
    tm = _tile(m, (512, 256, 128))
    tn = bn // split_cols if split_cols else _tile(bn, (1024, 896, 512, 256, 128))
    per = bn // tn
    gb = _blocks_per_step(nb, lambda g: g == 1 or (per == 1 and g * bn <= MM_WIDE))

    def body(a_ref, b_ref, o_ref):
        for g in range(gb):
            o_ref[:, g * tn:(g + 1) * tn] = _dot(a_ref[...], b_ref[g]).astype(o_ref.dtype)

    if split_cols:
        out_spec = pl.BlockSpec((None, tm, tn), lambda i, j, jj: (jj, i, 0))
        out_shape = jax.ShapeDtypeStruct((split_cols, m, tn), out_dtype)
    else:
        out_spec = pl.BlockSpec((tm, gb * tn), lambda i, j, jj: (i, j * per + jj))
        out_shape = jax.ShapeDtypeStruct((m, nb * bn), out_dtype)
    return pl.pallas_call(
        body, name=name, grid=(m // tm, nb // gb, per),
        in_specs=[pl.BlockSpec((tm, k), lambda i, j, jj: (i, 0)),
                  pl.BlockSpec((gb, k, tn), lambda i, j, jj: (j, 0, jj))],
        out_specs=out_spec, out_shape=out_shape,
        compiler_params=_cparams("parallel", "arbitrary", "arbitrary"),
    )(a, b3)


def mm_nt(a, w3, out_dtype, name):
    m, _ = a.shape
    nb, ko, bn = w3.shape
    tm = _tile(m, (512, 256, 128))
    tko = _tile(ko, (1024, 512, 256, 128))
    gb = _blocks_per_step(nb, lambda g: g * tko * bn * w3.dtype.itemsize <= MM_WEIGHT_BLOCK)
    ns = nb // gb

    def body(a_ref, w_ref, o_ref, acc_ref):
        j = pl.program_id(2)
        part = _dot_nt(a_ref[:, :bn], w_ref[0])
        for g in range(1, gb):
            part += _dot_nt(a_ref[:, g * bn:(g + 1) * bn], w_ref[g])
        if ns == 1:
            o_ref[...] = part.astype(o_ref.dtype)
            return

        @pl.when(j == 0)
        def _():
            acc_ref[...] = jnp.zeros_like(acc_ref)

        acc_ref[...] += part

        @pl.when(j == ns - 1)
        def _():
            o_ref[...] = acc_ref[...].astype(o_ref.dtype)

    return pl.pallas_call(
        body, name=name, grid=(m // tm, ko // tko, ns),
        in_specs=[pl.BlockSpec((tm, gb * bn), lambda i, o, j: (i, j)),
                  pl.BlockSpec((gb, tko, bn), lambda i, o, j: (j, o, 0))],
        out_specs=pl.BlockSpec((tm, tko), lambda i, o, j: (i, o)),
        out_shape=jax.ShapeDtypeStruct((m, ko), out_dtype),
        scratch_shapes=[pltpu.VMEM((tm, tko), f32)],
        compiler_params=_cparams("parallel", "arbitrary", "arbitrary"),
    )(a, w3)


def mm_tn(a, dy, ncb, out_dtype, name):
    l, ka = a.shape
    _, n = dy.shape
    bn = n // ncb
    tl = _tile(l, (1024, 512, 256, 128))
    tka = _tile(ka, (512, 256, 128))
    tn = _tile(bn, (1024, 896, 512, 256, 128))
    per = bn // tn
    gb = _blocks_per_step(ncb, lambda g: g == 1 or (per == 1 and g * bn <= MM_WIDE))
    nl = l // tl

    def body(a_ref, dy_ref, o_ref, acc_ref):
        s = pl.program_id(2)

        @pl.when(s == 0)
        def _():
            acc_ref[...] = jnp.zeros_like(acc_ref)

        acc_ref[...] += _dot_tn(a_ref[...], dy_ref[...])

        @pl.when(s == nl - 1)
        def _():
            for g in range(gb):
                o_ref[g] = acc_ref[:, g * tn:(g + 1) * tn].astype(o_ref.dtype)

    return pl.pallas_call(
        body, name=name, grid=(ka // tka, n // (gb * tn), nl),
        in_specs=[pl.BlockSpec((tl, tka), lambda i, j, s: (s, i)),
                  pl.BlockSpec((tl, gb * tn), lambda i, j, s: (s, j))],
        out_specs=pl.BlockSpec((gb, tka, tn), lambda i, j, s: (j // per, i, j % per)),
        out_shape=jax.ShapeDtypeStruct((ncb, ka, bn), out_dtype),
        scratch_shapes=[pltpu.VMEM((tka, gb * tn), f32)],
        compiler_params=_cparams("parallel", "parallel", "arbitrary"),
    )(a, dy)


def mod_part(c_all, w_mod, b_cols):
    nl, d, cols = w_mod.shape

    def body(c_ref, w_ref, b_ref, o_ref):
        cond = _silu(c_ref[...]).astype(bf16)
        o_ref[...] = _dot(cond, w_ref[...].astype(bf16)) + b_ref[...]

    return pl.pallas_call(
        body, name="mod_part", grid=(nl,),
        in_specs=[pl.BlockSpec((N_DEV, d), lambda l: (0, 0)),
                  pl.BlockSpec((None, d, cols), lambda l: (l, 0, 0)),
                  pl.BlockSpec((None, 1, cols), lambda l: (l, 0, 0))],
        out_specs=pl.BlockSpec((None, N_DEV, cols), lambda l: (l, 0, 0)),
        out_shape=jax.ShapeDtypeStruct((nl, N_DEV, cols), f32),
        compiler_params=_cparams("arbitrary"),
    )(c_all, w_mod, b_cols.reshape(nl, 1, cols))


def _row_tile(l):
    return _tile(l, (512, 256, 128))


def _entry_rows(xv, g_ref, sh_ref, sc_ref, h_ref, ht_ref):
    r = lax.rsqrt(jnp.mean(xv * xv, axis=-1, keepdims=True) + EPS)
    h = xv * r * (g_ref[...] * (1.0 + sc_ref[...])) + sh_ref[...]
    h_ref[...] = h.astype(h_ref.dtype)
    ht_ref[...] = jnp.transpose(h).astype(ht_ref.dtype)


def prenorm_fwd(x, g, shift, scale, name):
    l, d = x.shape
    tm = _row_tile(l)

    def body(x_ref, g_ref, sh_ref, sc_ref, h_ref, ht_ref):
        _entry_rows(x_ref[...], g_ref, sh_ref, sc_ref, h_ref, ht_ref)

    return pl.pallas_call(
        body, name=name, grid=(l // tm,),
        in_specs=[pl.BlockSpec((tm, d), lambda i: (i, 0)), _row(d), _row(d), _row(d)],
        out_specs=[pl.BlockSpec((tm, d), lambda i: (i, 0)), pl.BlockSpec((d, tm), lambda i: (0, i))],
        out_shape=[jax.ShapeDtypeStruct((l, d), bf16), jax.ShapeDtypeStruct((d, l), bf16)],
        compiler_params=_cparams("parallel"),
    )(x, g, shift, scale)


def post_prenorm_fwd(x, y, gate, g_post, g_pre, shift, scale, name):
    l, d = x.shape
    tm = _row_tile(l)

    def body(x_ref, y_ref, gate_ref, gp_ref, g_ref, sh_ref, sc_ref, o_ref, h_ref, ht_ref):
        yv = y_ref[...]
        r = lax.rsqrt(jnp.mean(yv * yv, axis=-1, keepdims=True) + EPS)
        xv = x_ref[...] + gate_ref[...] * (yv * r * gp_ref[...])
        o_ref[...] = xv
        _entry_rows(xv, g_ref, sh_ref, sc_ref, h_ref, ht_ref)

    blk = pl.BlockSpec((tm, d), lambda i: (i, 0))
    return pl.pallas_call(
        body, name=name, grid=(l // tm,),
        in_specs=[blk, blk] + [_row(d)] * 5, out_specs=[blk, blk, pl.BlockSpec((d, tm), lambda i: (0, i))],
        out_shape=[jax.ShapeDtypeStruct((l, d), f32), jax.ShapeDtypeStruct((l, d), bf16),
                   jax.ShapeDtypeStruct((d, l), bf16)],
        compiler_params=_cparams("parallel"),
    )(x, y, gate, g_post, g_pre, shift, scale)


def _post_bwd_rows(dxv, yv, r, gate, gv, dy_ref, dgate_ref, dg_ref):
    yn = yv * r
    dgate_ref[...] += jnp.sum(dxv * yn * gv, axis=0, keepdims=True)
    dyg = dxv * gate
    dg_ref[...] += jnp.sum(dyg * yn, axis=0, keepdims=True)
    dyn = dyg * gv
    dy_ref[...] = (r * (dyn - yn * jnp.mean(dyn * yn, axis=-1, keepdims=True))).astype(dy_ref.dtype)


def final_loss(x, y, gate, g, target):
    l, d = x.shape
    tm = _row_tile(l)

    def body(x_ref, y_ref, gate_ref, g_ref, t_ref, dx_ref, loss_ref, dy_ref, dgate_ref, dg_ref):
        @pl.when(pl.program_id(0) == 0)
        def _():
            loss_ref[...] = jnp.zeros_like(loss_ref)
            dgate_ref[...] = jnp.zeros_like(dgate_ref)
            dg_ref[...] = jnp.zeros_like(dg_ref)

        yv, gate, gv = y_ref[...], gate_ref[...], g_ref[...]
        r = lax.rsqrt(jnp.mean(yv * yv, axis=-1, keepdims=True) + EPS)
        diff = x_ref[...] + gate * (yv * r * gv) - t_ref[...]
        dxv = diff * (1.0 / d)
        dx_ref[...] = dxv
        loss_ref[...] += jnp.sum(diff * diff)
        _post_bwd_rows(dxv, yv, r, gate, gv, dy_ref, dgate_ref, dg_ref)

    blk = pl.BlockSpec((tm, d), lambda i: (i, 0))
    return pl.pallas_call(
        body, name="final_loss", grid=(l // tm,),
        in_specs=[blk, blk, _row(d), _row(d), blk],
        out_specs=[blk, pl.BlockSpec((SUBLANES, HEAD), lambda i: (0, 0)), blk, _row(d), _row(d)],
        out_shape=[jax.ShapeDtypeStruct((l, d), f32), jax.ShapeDtypeStruct((SUBLANES, HEAD), f32),
                   jax.ShapeDtypeStruct((l, d), bf16), jax.ShapeDtypeStruct((1, d), f32), jax.ShapeDtypeStruct((1, d), f32)],
        compiler_params=_cparams("arbitrary"),
    )(x, y, gate, g, target)


def prenorm_bwd(dh, x, dx_next, g, scale, name):
    l, d = x.shape
    tm = _row_tile(l)

    def body(dh_ref, x_ref, dxn_ref, g_ref, sc_ref, dx_ref, dsh_ref, dsc_ref, dg_ref):
        @pl.when(pl.program_id(0) == 0)
        def _():
            dsh_ref[...] = jnp.zeros_like(dsh_ref)
            dsc_ref[...] = jnp.zeros_like(dsc_ref)
            dg_ref[...] = jnp.zeros_like(dg_ref)

        xv, dhv, gv, sc1 = x_ref[...], dh_ref[...], g_ref[...], 1.0 + sc_ref[...]
        r = lax.rsqrt(jnp.mean(xv * xv, axis=-1, keepdims=True) + EPS)
        xn = xv * r
        dhx = dhv * xn
        dsh_ref[...] += jnp.sum(dhv, axis=0, keepdims=True)
        dsc_ref[...] += jnp.sum(dhx * gv, axis=0, keepdims=True)
        dg_ref[...] += jnp.sum(dhx * sc1, axis=0, keepdims=True)
        dxn = dhv * (gv * sc1)
        dx_ref[...] = dxn_ref[...] + r * (dxn - xn * jnp.mean(dxn * xn, axis=-1, keepdims=True))

    blk = pl.BlockSpec((tm, d), lambda i: (i, 0))
    return pl.pallas_call(
        body, name=name, grid=(l // tm,),
        in_specs=[blk, blk, blk, _row(d), _row(d)], out_specs=[blk, _row(d), _row(d), _row(d)],
        out_shape=[jax.ShapeDtypeStruct((l, d), f32)] + [jax.ShapeDtypeStruct((1, d), f32)] * 3,
        compiler_params=_cparams("arbitrary"),
    )(dh, x, dx_next, g, scale)


def prenorm_post_bwd(dh, x, dx_next, g, scale, y, gate, g_post, name):
    l, d = x.shape
    tm = _tile(l, (256, 128))

    def body(dh_ref, x_ref, dxn_ref, g_ref, sc_ref, y_ref, gate_ref, gp_ref,
             dx_ref, dsh_ref, dsc_ref, dg_ref, dy_ref, dgate_ref, dgp_ref):
        @pl.when(pl.program_id(0) == 0)
        def _():
            for ref in (dsh_ref, dsc_ref, dg_ref, dgate_ref, dgp_ref):
                ref[...] = jnp.zeros_like(ref)

        xv, dhv, gv, sc1 = x_ref[...], dh_ref[...], g_ref[...], 1.0 + sc_ref[...]
        r = lax.rsqrt(jnp.mean(xv * xv, axis=-1, keepdims=True) + EPS)
        xn = xv * r
        dhx = dhv * xn
        dsh_ref[...] += jnp.sum(dhv, axis=0, keepdims=True)
        dsc_ref[...] += jnp.sum(dhx * gv, axis=0, keepdims=True)
        dg_ref[...] += jnp.sum(dhx * sc1, axis=0, keepdims=True)
        dxn = dhv * (gv * sc1)
        dxv = dxn_ref[...] + r * (dxn - xn * jnp.mean(dxn * xn, axis=-1, keepdims=True))
        dx_ref[...] = dxv
        yv = y_ref[...]
        ry = lax.rsqrt(jnp.mean(yv * yv, axis=-1, keepdims=True) + EPS)
        _post_bwd_rows(dxv, yv, ry, gate_ref[...], gp_ref[...], dy_ref, dgate_ref, dgp_ref)

    blk = pl.BlockSpec((tm, d), lambda i: (i, 0))
    row = jax.ShapeDtypeStruct((1, d), f32)
    return pl.pallas_call(
        body, name=name, grid=(l // tm,),
        in_specs=[blk, blk, blk, _row(d), _row(d), blk, _row(d), _row(d)],
        out_specs=[blk, _row(d), _row(d), _row(d), blk, _row(d), _row(d)],
        out_shape=[jax.ShapeDtypeStruct((l, d), f32), row, row, row, jax.ShapeDtypeStruct((l, d), bf16), row, row],
        compiler_params=_cparams("arbitrary"),
    )(dh, x, dx_next, g, scale, y, gate, g_post)


def _tril_mask():
    r = lax.broadcasted_iota(jnp.int32, (HEAD, HEAD), 0)
    c = lax.broadcasted_iota(jnp.int32, (HEAD, HEAD), 1)
    return r >= c


def sgu_fwd(proj, norm_g, w_s, b_s):
    l = proj.shape[0]
    nh = w_s.shape[0]
    wa = nh * HEAD

    def body(au_ref, av_ref, az_ref, ng_ref, w_ref, b_ref, o_ref):
        tril = _tril_mask()
        for h in range(nh):
            sl = slice(h * HEAD, (h + 1) * HEAD)
            gv = _gelu(av_ref[:, sl].astype(f32))
            r = lax.rsqrt(jnp.mean(gv * gv, axis=-1, keepdims=True) + EPS)
            vh = gv * r * ng_ref[:, sl]
            wm = jnp.where(tril, w_ref[h], 0.0).astype(bf16)
            s = _dot(wm, vh.astype(bf16)) + b_ref[h]
            o_ref[:, sl] = (_gelu(au_ref[:, sl].astype(f32)) * s * _silu(az_ref[:, sl].astype(f32))).astype(o_ref.dtype)

    def col(j):
        return pl.BlockSpec((HEAD, wa), lambda n: (n, j))

    return pl.pallas_call(
        body, name="sgu_fwd", grid=(l // HEAD,),
        in_specs=[col(0), col(1), col(2), _row(wa),
                  pl.BlockSpec((nh, HEAD, HEAD), lambda n: (0, 0, 0)), pl.BlockSpec((nh, HEAD, 1), lambda n: (0, 0, 0))],
        out_specs=pl.BlockSpec((HEAD, wa), lambda n: (n, 0)),
        out_shape=jax.ShapeDtypeStruct((l, 2 * wa), bf16),
        compiler_params=_cparams("parallel"),
    )(proj, proj, proj, norm_g, w_s, b_s)


def sgu_bwd(proj, dcat, norm_g, w_s, b_s):
    l = proj.shape[0]
    nh = w_s.shape[0]
    wa = nh * HEAD

    def body(au_ref, av_ref, az_ref, do_ref, ng_ref, w_ref, b_ref, da_ref, dw_ref, db_ref, dng_ref):
        @pl.when(pl.program_id(0) == 0)
        def _():
            dw_ref[...] = jnp.zeros_like(dw_ref)
            db_ref[...] = jnp.zeros_like(db_ref)
            dng_ref[...] = jnp.zeros_like(dng_ref)

        tril = _tril_mask()
        for h in range(nh):
            sl = slice(h * HEAD, (h + 1) * HEAD)
            au, av, az = au_ref[:, sl].astype(f32), av_ref[:, sl].astype(f32), az_ref[:, sl].astype(f32)
            ng = ng_ref[:, sl]
            gv = _gelu(av)
            r = lax.rsqrt(jnp.mean(gv * gv, axis=-1, keepdims=True) + EPS)
            gvn = gv * r
            vh = (gvn * ng).astype(bf16)
            wm = jnp.where(tril, w_ref[h], 0.0).astype(bf16)
            s = _dot(wm, vh) + b_ref[h]
            gu, sz = _gelu(au), _silu(az)
            dov = do_ref[:, sl].astype(f32)
            da_ref[:, sl] = (dov * s * sz * _gelu_grad(au)).astype(da_ref.dtype)
            da_ref[:, 2 * wa + h * HEAD:2 * wa + (h + 1) * HEAD] = (dov * gu * s * _silu_grad(az)).astype(da_ref.dtype)
            ds = dov * gu * sz
            db_ref[h] += jnp.sum(ds, axis=-1, keepdims=True)
            dsb = ds.astype(bf16)
            dw_ref[h] += jnp.where(tril, _dot_nt(dsb, vh), 0.0)
            dvh = _dot_tn(wm, dsb)
            dng_ref[:, sl] += jnp.sum(dvh * gvn, axis=0, keepdims=True)
            dgvn = dvh * ng
            dgv = r * (dgvn - gvn * jnp.mean(dgvn * gvn, axis=-1, keepdims=True))
            da_ref[:, wa + h * HEAD:wa + (h + 1) * HEAD] = (dgv * _gelu_grad(av)).astype(da_ref.dtype)

    def col(j):
        return pl.BlockSpec((HEAD, wa), lambda n: (n, j))

    whole_w = pl.BlockSpec((nh, HEAD, HEAD), lambda n: (0, 0, 0))
    whole_b = pl.BlockSpec((nh, HEAD, 1), lambda n: (0, 0, 0))
    return pl.pallas_call(
        body, name="sgu_bwd", grid=(l // HEAD,),
        in_specs=[col(0), col(1), col(2), col(0), _row(wa), whole_w, whole_b],
        out_specs=[pl.BlockSpec((HEAD, 3 * wa), lambda n: (n, 0)), whole_w, whole_b, _row(wa)],
        out_shape=[jax.ShapeDtypeStruct(proj.shape, bf16), jax.ShapeDtypeStruct((nh, HEAD, HEAD), f32),
                   jax.ShapeDtypeStruct((nh, HEAD, 1), f32), jax.ShapeDtypeStruct((1, wa), f32)],
        compiler_params=_cparams("arbitrary"),
    )(proj, proj, proj, dcat, norm_g, w_s, b_s)


_LOG2E = 1.0 / math.log(2.0)
_SB_EXP_CLAMP = 120.0


def _sb_scores(q, k, scale):
    z = _dot_nt(q, k) * (scale * _LOG2E)
    return z, jnp.maximum(z, jnp.log2(1.0 + jnp.exp2(jnp.minimum(z, _SB_EXP_CLAMP))))


SB_KEYS = 256


def _sb_sum_matrix(tri, kb):
    s = lax.broadcasted_iota(jnp.int32, (2 * kb, kb + HEAD), 0) % kb
    j = lax.broadcasted_iota(jnp.int32, (2 * kb, kb + HEAD), 1)
    return jnp.where(jnp.logical_or(j >= kb, tri(s, j)), 1.0, 0.0).astype(bf16)


def _sb_sums(x, sums):
    kb = x.shape[1]
    c2 = _dot(jnp.concatenate(_split_bf16(x), axis=1), sums)
    return c2[:, :kb], c2[:, kb:]


def _sb_wide(v, kb):
    return jnp.concatenate([v] * (kb // HEAD), axis=1) if kb > HEAD else v


def _sb_q_tile(l, most=512):
    return _tile(l, tuple(t for t in (1024, 512, 256, 128) if t <= most))


def _sb_band_levels(band):
    return _tile(band, (4, 2, 1))


def _sb_heads_per_step(nh, most):
    return _tile(nh, tuple(h for h in (4, 2) if h <= most))


def sb_fwd(proj, mixed, nh):
    l = proj.shape[0]
    wb = nh * HEAD
    tq = _sb_q_tile(l, 1024)
    kb = min(SB_KEYS, tq)
    band = tq // kb
    hp = _sb_heads_per_step(nh, 4)
    levels = _sb_band_levels(band)
    scale = 1.0 / math.sqrt(HEAD)
    qc, kc, vc, zc = 3 * nh, 4 * nh, 5 * nh, 6 * nh

    def body(q_ref, k_ref, v_ref, bz_ref, mixed_ref, o_ref, att_ref, tot_ref):
        del mixed_ref
        i = pl.program_id(1)
        sums = _sb_sum_matrix(lambda s, j: s > j, kb)
        t_pos = i * tq + lax.broadcasted_iota(jnp.int32, (tq, kb), 0)
        s_off = lax.broadcasted_iota(jnp.int32, (tq, kb), 1)

        def step(j, carry, masked, row0=0):
            rows = pl.ds(pl.multiple_of(j * kb, kb), kb)
            out = []
            for e in range(hp):
                acc, tot = carry[e]
                sl = slice(e * HEAD, (e + 1) * HEAD)
                z, sp = _sb_scores(q_ref[row0:, sl], k_ref[rows, sl], scale)
                lb = z - sp
                if masked:
                    mask = s_off[row0:] + j * kb < t_pos[row0:]
                    sp = jnp.where(mask, sp, 0.0)
                later, total = _sb_sums(sp, sums)
                w = jnp.exp2(lb + _sb_wide(tot[row0:], kb) - later)
                if masked:
                    w = jnp.where(mask, w, 0.0)
                new = (acc[row0:] + _dot(w.astype(bf16), v_ref[rows, sl]), tot[row0:] - total)
                out.append(tuple(jnp.concatenate([old[:row0], upd]) if row0 else upd for old, upd in zip(carry[e], new)))
            return tuple(out)

        zero = jnp.zeros((tq, HEAD), f32)
        carry = ((zero, zero),) * hp
        for lv in reversed(range(levels)):
            carry = lax.fori_loop(
                0, band // levels,
                lambda t, c, lv=lv: step(band * i + (lv + 1) * (band // levels) - 1 - t, c, True, lv * (tq // levels)), carry)
        carry = lax.fori_loop(0, band * i, lambda t, c: step(band * i - 1 - t, c, False), carry)
        for e in range(hp):
            acc, tot = carry[e]
            sl = slice(e * HEAD, (e + 1) * HEAD)
            att_ref[:, sl] = acc.astype(att_ref.dtype)
            o_ref[:, sl] = (acc * _silu(bz_ref[:, sl].astype(f32))).astype(o_ref.dtype)
            tot_ref[e] = tot[:, :1]

    blk = lambda c0: pl.BlockSpec((tq, hp * HEAD), lambda g, i: (i, c0 // hp + g))
    head = lambda c0: pl.BlockSpec((l, hp * HEAD), lambda g, i: (0, c0 // hp + g))
    return pl.pallas_call(
        body, name="sb_fwd", grid=(nh // hp, l // tq),
        in_specs=[blk(qc), head(kc), head(vc), blk(zc), pl.BlockSpec(memory_space=pl.ANY)],
        out_specs=[blk(mixed.shape[1] // HEAD - nh), blk(0), pl.BlockSpec((hp, tq, 1), lambda g, i: (g, i, 0))],
        out_shape=[jax.ShapeDtypeStruct(mixed.shape, bf16), jax.ShapeDtypeStruct((l, wb), bf16),
                   jax.ShapeDtypeStruct((nh, l, 1), f32)],
        input_output_aliases={4: 0},
        compiler_params=_cparams("parallel", "arbitrary"),
    )(proj, proj, proj, proj, mixed)


def sb_bwd(proj, dcat, att, tot, dproj, nh):
    l = proj.shape[0]
    wb = nh * HEAD
    tq = _sb_q_tile(l, 1024)
    kb = min(SB_KEYS, tq)
    band = tq // kb
    nq = l // tq
    hp = _sb_heads_per_step(nh, 2)
    levels = _sb_band_levels(band)
    scale = 1.0 / math.sqrt(HEAD)
    qc, kc, vc, zc = 3 * nh, 4 * nh, 5 * nh, 6 * nh

    def body(q_ref, k_ref, v_ref, bz_ref, do_ref, att_ref, tot_ref, dproj_in, dproj_ref, dk_acc, dv_acc, dob_ref,
             tile_ref, head_ref, sems):
        del dproj_in
        g, i = pl.program_id(0), pl.program_id(1)

        def put(src, row0, c0, k):
            cols = pl.ds(pl.multiple_of((c0 + g * hp) * HEAD, HEAD), hp * HEAD)
            cp = pltpu.make_async_copy(src, dproj_ref.at[pl.ds(row0, src.shape[0]), cols], sems.at[k])
            cp.start()
            return cp

        @pl.when(i == 0)
        def _():
            dk_acc[...] = jnp.zeros_like(dk_acc)
            dv_acc[...] = jnp.zeros_like(dv_acc)

        my_rows = pl.multiple_of(i * tq, tq)
        bz = bz_ref[...].astype(f32)
        dov = do_ref[...].astype(f32)
        tile_ref[0] = (dov * att_ref[...].astype(f32) * _silu_grad(bz)).astype(bf16)
        dbz_copy = put(tile_ref.at[0], my_rows, zc, 0)
        dob_ref[...] = (dov * _silu(bz)).astype(bf16)
        upto = _sb_sum_matrix(lambda s, j: s <= j, kb)
        before = _sb_sum_matrix(lambda j, s: j < s, kb)
        t_pos = i * tq + lax.broadcasted_iota(jnp.int32, (tq, kb), 0)
        s_off = lax.broadcasted_iota(jnp.int32, (tq, kb), 1)

        def step(j, carry, masked, row0=0):
            rows = pl.ds(pl.multiple_of(j * kb, kb), kb)
            out = []
            for h in range(hp):
                dq, sp_seen, e_seen = (c[row0:] for c in carry[h])
                sl = slice(h * HEAD, (h + 1) * HEAD)
                q, kj, vj, dob = q_ref[row0:, sl], k_ref[rows, sl], v_ref[rows, sl], dob_ref[row0:, sl]
                z, sp = _sb_scores(q, kj, scale)
                lb = z - sp
                if masked:
                    mask = s_off[row0:] + j * kb < t_pos[row0:]
                    sp = jnp.where(mask, sp, 0.0)
                sp_upto, sp_total = _sb_sums(sp, upto)
                w = jnp.exp2(lb + _sb_wide(sp_seen, kb) + sp_upto)
                if masked:
                    w = jnp.where(mask, w, 0.0)
                dv_acc[rows, sl] += _dot_tn(w.astype(bf16), dob)
                e = _dot_nt(dob, vj) * w
                e_before, e_total = _sb_sums(e, before)
                dz = (e - (e + _sb_wide(e_seen, kb) + e_before) * jnp.exp2(lb)) * scale
                if masked:
                    dz = jnp.where(mask, dz, 0.0)
                dz = dz.astype(bf16)
                dk_acc[rows, sl] += _dot_tn(dz, q)
                new = (dq + _dot(dz, kj), sp_seen + sp_total, e_seen + e_total)
                out.append(tuple(jnp.concatenate([old[:row0], upd]) if row0 else upd for old, upd in zip(carry[h], new)))
            return tuple(out)

        zero = jnp.zeros((tq, HEAD), f32)
        init = tuple((zero, jnp.broadcast_to(tot_ref[h], (tq, HEAD)), zero) for h in range(hp))
        carry = lax.fori_loop(0, band * i, lambda j, c: step(j, c, False), init)
        for lv in range(levels):
            carry = lax.fori_loop(
                0, band // levels,
                lambda t, c, lv=lv: step(band * i + lv * (band // levels) + t, c, True, lv * (tq // levels)), carry)
        for h in range(hp):
            tile_ref[1, :, h * HEAD:(h + 1) * HEAD] = carry[h][0].astype(bf16)
        dq_copy = put(tile_ref.at[1], my_rows, qc, 1)
        dbz_copy.wait()
        dq_copy.wait()

        @pl.when(i == nq - 1)
        def _():
            head_ref[0] = dk_acc[...].astype(bf16)
            head_ref[1] = dv_acc[...].astype(bf16)
            copies = [put(head_ref.at[0], 0, kc, 2), put(head_ref.at[1], 0, vc, 3)]
            for cp in copies:
                cp.wait()

    blk = lambda c0: pl.BlockSpec((tq, hp * HEAD), lambda g, i: (i, c0 // hp + g))
    head = lambda c0: pl.BlockSpec((l, hp * HEAD), lambda g, i: (0, c0 // hp + g))
    any_spec = pl.BlockSpec(memory_space=pl.ANY)
    return pl.pallas_call(
        body, name="sb_bwd", grid=(nh // hp, nq),
        in_specs=[blk(qc), head(kc), head(vc), blk(zc), blk(nh), blk(0),
                  pl.BlockSpec((hp, tq, 1), lambda g, i: (g, i, 0)), any_spec],
        out_specs=any_spec, out_shape=jax.ShapeDtypeStruct(dproj.shape, bf16), input_output_aliases={7: 0},
        scratch_shapes=[pltpu.VMEM((l, hp * HEAD), f32), pltpu.VMEM((l, hp * HEAD), f32),
                        pltpu.VMEM((tq, hp * HEAD), bf16), pltpu.VMEM((2, tq, hp * HEAD), bf16),
                        pltpu.VMEM((2, l, hp * HEAD), bf16), pltpu.SemaphoreType.DMA((4,))],
        compiler_params=_cparams("parallel", "arbitrary"),
    )(proj, proj, proj, proj, dcat, att, tot, dproj)


def _disc(lr, li, ldt):
    dt = jnp.exp(ldt)
    mag = jnp.exp(lr * dt)
    a_re = mag * jnp.cos(li * dt)
    a_im = mag * jnp.sin(li * dt)
    den = lr * lr + li * li
    nr = a_re - 1.0
    return a_re, a_im, (nr * lr + a_im * li) / den, (a_im * lr - nr * li) / den


def s5_params_fwd(lr, li, ldt, bt_re, bt_im):
    g, c, p = bt_re.shape

    def body(lr_ref, li_ref, ldt_ref, br_ref, bi_ref, ar_ref, ai_ref, bbr_ref, bbi_ref):
        a_re, a_im, cr, ci = _disc(lr_ref[...], li_ref[...], ldt_ref[...])
        ar_ref[...] = a_re
        ai_ref[...] = a_im
        for k in range(c):
            br, bi = br_ref[:, k, :], bi_ref[:, k, :]
            bbr_ref[:, k, :] = cr * br - ci * bi
            bbi_ref[:, k, :] = cr * bi + ci * br

    return pl.pallas_call(
        body, name="s5_params_fwd",
        out_shape=[jax.ShapeDtypeStruct((g, p), f32)] * 2 + [jax.ShapeDtypeStruct((g, c, p), f32)] * 2,
    )(lr, li, ldt, bt_re, bt_im)


def s5_params_bwd(lr, li, ldt, bt_re, bt_im, da_re, da_im, dbbt_re, dbbt_im):
    g, c, p = bt_re.shape

    def body(lr_ref, li_ref, ldt_ref, br_ref, bi_ref, dar_ref, dai_ref, dbbr_ref, dbbi_ref,
             dlr_ref, dli_ref, dldt_ref, dbr_ref, dbi_ref):
        (a_re, a_im, cr, ci), vjp = jax.vjp(_disc, lr_ref[...], li_ref[...], ldt_ref[...])
        dcr = jnp.zeros((g, p), f32)
        dci = jnp.zeros((g, p), f32)
        for k in range(c):
            br, bi = br_ref[:, k, :], bi_ref[:, k, :]
            dr, di = dbbr_ref[:, k, :], dbbi_ref[:, k, :]
            dcr += dr * br + di * bi
            dci += di * br - dr * bi
            dbr_ref[:, k, :] = cr * dr + ci * di
            dbi_ref[:, k, :] = cr * di - ci * dr
        dlr, dli, dldt = vjp((dar_ref[...], dai_ref[...], dcr, dci))
        dlr_ref[...] = dlr
        dli_ref[...] = dli
        dldt_ref[...] = dldt

    return pl.pallas_call(
        body, name="s5_params_bwd",
        out_shape=[jax.ShapeDtypeStruct((g, p), f32)] * 2 + [jax.ShapeDtypeStruct((g, 1), f32)]
        + [jax.ShapeDtypeStruct((g, c, p), f32)] * 2,
    )(lr, li, ldt, bt_re, bt_im, da_re, da_im, dbbt_re, dbbt_im)


def _cmul(ar, ai, br, bi):
    return ar * br - ai * bi, ar * bi + ai * br


def _power_tables(ar, ai):
    rows = lax.broadcasted_iota(jnp.int32, (SUBLANES, ar.shape[1]), 0)
    pr = jnp.zeros((SUBLANES, ar.shape[1]), f32)
    pi = jnp.zeros((SUBLANES, ar.shape[1]), f32)
    cr, ci = ar, ai
    pows = {}
    for r in range(SUBLANES):
        pows[r + 1] = (cr, ci)
        pr = jnp.where(rows == r, cr, pr)
        pi = jnp.where(rows == r, ci, pi)
        cr, ci = _cmul(cr, ci, ar, ai)
    return [pows[1], pows[2], pows[4]], pr, pi


def _ssm_time_tile(l):
    return _tile(l, (2048, 1024, 512, 256, 128))


def ssm_fwd(u, bre3, bim3, cre3, cimn3, a_re, a_im, d_skip):
    l, w = u.shape[0], d_skip.shape[1]
    nj = w // HEAD
    ns = STATES_PER_LANE_BLOCK
    tt = _ssm_time_tile(l)

    def body(u_ref, bre_ref, bim_ref, cre_ref, cim_ref, ar_ref, ai_ref, d_ref, y_ref, hr_ref, hi_ref, cr_ref, ci_ref):
        @pl.when(pl.program_id(1) == 0)
        def _():
            cr_ref[...] = jnp.zeros_like(cr_ref)
            ci_ref[...] = jnp.zeros_like(ci_ref)

        uv = u_ref[...]
        hr_ref[...] = _dot(uv, bre_ref[...])
        hi_ref[...] = _dot(uv, bim_ref[...])
        steps, pr, pi = _power_tables(ar_ref[...], ai_ref[...])
        rows = lax.broadcasted_iota(jnp.int32, (SUBLANES, ns), 0)
        steps = [(jnp.where(rows >= d, sr_, 0.0), jnp.where(rows >= d, si_, 0.0)) for d, (sr_, si_) in zip((1, 2, 4), steps)]

        def blk(b, carry):
            cr, ci = carry
            sl = pl.ds(pl.multiple_of(b * SUBLANES, SUBLANES), SUBLANES)
            xr, xi = hr_ref[sl, :], hi_ref[sl, :]
            for d, (sr_, si_) in zip((1, 2, 4), steps):
                mr, mi = _cmul(sr_, si_, pltpu.roll(xr, d, axis=0), pltpu.roll(xi, d, axis=0))
                xr, xi = xr + mr, xi + mi
            mr, mi = _cmul(pr, pi, cr, ci)
            xr, xi = xr + mr, xi + mi
            hr_ref[sl, :] = xr
            hi_ref[sl, :] = xi
            return xr[SUBLANES - 1:, :], xi[SUBLANES - 1:, :]

        cr, ci = lax.fori_loop(0, tt // SUBLANES, blk, (cr_ref[...], ci_ref[...]))
        cr_ref[...] = cr
        ci_ref[...] = ci
        y = _dot(hr_ref[...].astype(bf16), cre_ref[...]) + _dot(hi_ref[...].astype(bf16), cim_ref[...])
        y_ref[...] = y + d_ref[...] * uv.astype(f32)

    lane = pl.BlockSpec((tt, HEAD), lambda j, i: (i, j))
    st = pl.BlockSpec((tt, ns), lambda j, i: (i, j))
    b3 = pl.BlockSpec((None, HEAD, ns), lambda j, i: (j, 0, 0))
    c3 = pl.BlockSpec((None, ns, HEAD), lambda j, i: (j, 0, 0))
    arow = pl.BlockSpec((1, ns), lambda j, i: (0, j))
    return pl.pallas_call(
        body, name="ssm_fwd", grid=(nj, l // tt),
        in_specs=[lane, b3, b3, c3, c3, arow, arow, pl.BlockSpec((1, HEAD), lambda j, i: (0, j))],
        out_specs=[lane, st, st],
        out_shape=[jax.ShapeDtypeStruct((l, w), f32), jax.ShapeDtypeStruct((l, nj * ns), f32),
                   jax.ShapeDtypeStruct((l, nj * ns), f32)],
        scratch_shapes=[pltpu.VMEM((1, ns), f32), pltpu.VMEM((1, ns), f32)],
        compiler_params=_cparams("parallel", "arbitrary"),
    )(u, bre3, bim3, cre3, cimn3, a_re, a_im, d_skip)


def ssm_bwd(dy, u, dproj, h_re, h_im, bre3, bim3, cre3, cimn3, a_re, a_im, d_skip):
    l, w = u.shape[0], d_skip.shape[1]
    nj = w // HEAD
    ns = STATES_PER_LANE_BLOCK
    tt = _ssm_time_tile(l)
    nt = l // tt

    def body(dy_ref, u_ref, dproj_ref, hr_ref, hi_ref, bre_ref, bim_ref, cre_ref, cim_ref, ar_ref, ai_ref, d_ref,
             du_ref, dd_ref, dar_ref, dai_ref, dbre_ref, dbim_ref, dcre_ref, dcim_ref, kr_ref, ki_ref, cr_ref, ci_ref,
             accr_ref, acci_ref):
        del dproj_ref
        i = pl.program_id(1)

        @pl.when(i == 0)
        def _():
            for ref in (cr_ref, ci_ref, accr_ref, acci_ref, dd_ref, dbre_ref, dbim_ref, dcre_ref, dcim_ref):
                ref[...] = jnp.zeros_like(ref)

        dyv = dy_ref[...]
        dyb = dyv.astype(bf16)
        uv = u_ref[...]
        kr_ref[...] = _dot_nt(dyb, cre_ref[...])
        ki_ref[...] = _dot_nt(dyb, cim_ref[...])
        steps, pr, pi = _power_tables(ar_ref[...], -ai_ref[...])
        rows = lax.broadcasted_iota(jnp.int32, (SUBLANES, ns), 0)
        qr = jnp.zeros((SUBLANES, ns), f32)
        qi = jnp.zeros((SUBLANES, ns), f32)
        for r in range(SUBLANES):
            qr = jnp.where(rows == r, pr[SUBLANES - 1 - r:SUBLANES - r, :], qr)
            qi = jnp.where(rows == r, pi[SUBLANES - 1 - r:SUBLANES - r, :], qi)
        nb = tt // SUBLANES
        steps = [(jnp.where(rows < SUBLANES - d, sr_, 0.0), jnp.where(rows < SUBLANES - d, si_, 0.0))
                 for d, (sr_, si_) in zip((1, 2, 4), steps)]

        def blk(t, carry):
            cr, ci, accr, acci = carry
            sl = pl.ds(pl.multiple_of((nb - 1 - t) * SUBLANES, SUBLANES), SUBLANES)
            xr, xi = kr_ref[sl, :], ki_ref[sl, :]
            for d, (sr_, si_) in zip((1, 2, 4), steps):
                mr, mi = _cmul(sr_, si_, pltpu.roll(xr, SUBLANES - d, axis=0), pltpu.roll(xi, SUBLANES - d, axis=0))
                xr, xi = xr + mr, xi + mi
            mr, mi = _cmul(qr, qi, cr, ci)
            xr, xi = xr + mr, xi + mi
            kr_ref[sl, :] = xr
            ki_ref[sl, :] = xi
            last = rows == SUBLANES - 1
            nr = jnp.where(last, cr, pltpu.roll(xr, SUBLANES - 1, axis=0))
            ni = jnp.where(last, ci, pltpu.roll(xi, SUBLANES - 1, axis=0))
            hr, hi = hr_ref[sl, :], hi_ref[sl, :]
            accr = accr + nr * hr + ni * hi
            acci = acci + ni * hr - nr * hi
            return xr[:1, :], xi[:1, :], accr, acci

        cr, ci, accr, acci = lax.fori_loop(0, nb, blk, (cr_ref[...], ci_ref[...], accr_ref[...], acci_ref[...]))
        cr_ref[...] = cr
        ci_ref[...] = ci
        accr_ref[...] = accr
        acci_ref[...] = acci
        kr, ki = kr_ref[...].astype(bf16), ki_ref[...].astype(bf16)
        du = _dot_nt(kr, bre_ref[...]) + _dot_nt(ki, bim_ref[...]) + d_ref[...] * dyv
        du_ref[...] = du.astype(du_ref.dtype)
        dd_ref[...] += jnp.sum(dyv * uv.astype(f32), axis=0, keepdims=True)
        dbre_ref[...] += _dot_tn(uv, kr)
        dbim_ref[...] += _dot_tn(uv, ki)
        dcre_ref[...] += _dot_tn(hr_ref[...].astype(bf16), dyb)
        dcim_ref[...] += _dot_tn(hi_ref[...].astype(bf16), dyb)

        @pl.when(i == nt - 1)
        def _():
            dar_ref[...] = jnp.sum(accr_ref[...], axis=0, keepdims=True)
            dai_ref[...] = jnp.sum(acci_ref[...], axis=0, keepdims=True)

    lane = pl.BlockSpec((tt, HEAD), lambda j, i: (nt - 1 - i, j))
    st = pl.BlockSpec((tt, ns), lambda j, i: (nt - 1 - i, j))
    b3 = pl.BlockSpec((None, HEAD, ns), lambda j, i: (j, 0, 0))
    c3 = pl.BlockSpec((None, ns, HEAD), lambda j, i: (j, 0, 0))
    arow = pl.BlockSpec((1, ns), lambda j, i: (0, j))
    drow = pl.BlockSpec((1, HEAD), lambda j, i: (0, j))
    return pl.pallas_call(
        body, name="ssm_bwd", grid=(nj, nt),
        in_specs=[lane, lane, pl.BlockSpec(memory_space=pl.ANY), st, st, b3, b3, c3, c3, arow, arow, drow],
        out_specs=[lane, drow, arow, arow, b3, b3, c3, c3], input_output_aliases={2: 0},
        out_shape=[jax.ShapeDtypeStruct(dproj.shape, bf16), jax.ShapeDtypeStruct((1, w), f32),
                   jax.ShapeDtypeStruct((1, nj * ns), f32), jax.ShapeDtypeStruct((1, nj * ns), f32),
                   jax.ShapeDtypeStruct((nj, HEAD, ns), f32), jax.ShapeDtypeStruct((nj, HEAD, ns), f32),
                   jax.ShapeDtypeStruct((nj, ns, HEAD), f32), jax.ShapeDtypeStruct((nj, ns, HEAD), f32)],
        scratch_shapes=[pltpu.VMEM((tt, ns), f32), pltpu.VMEM((tt, ns), f32), pltpu.VMEM((1, ns), f32),
                        pltpu.VMEM((1, ns), f32), pltpu.VMEM((SUBLANES, ns), f32), pltpu.VMEM((SUBLANES, ns), f32)],
        compiler_params=_cparams("parallel", "arbitrary"),
    )(dy, u, dproj, h_re, h_im, bre3, bim3, cre3, cimn3, a_re, a_im, d_skip)


def glu_fwd(y, z_src, w_glu, b_glu):
    l, w = y.shape
    tm = _row_tile(l)

    def body(y_ref, z_ref, w_ref, b_ref, g_ref, t_ref, o_ref):
        g = _gelu(y_ref[...])
        gb = g.astype(bf16)
        t = _dot(gb, w_ref[...]) + b_ref[...]
        g_ref[...] = gb
        t_ref[...] = t
        o_ref[...] = (g * jax.nn.sigmoid(t) * _silu(z_ref[...].astype(f32))).astype(o_ref.dtype)

    blk = pl.BlockSpec((tm, w), lambda i: (i, 0))
    return pl.pallas_call(
        body, name="glu_fwd", grid=(l // tm,),
        in_specs=[blk, pl.BlockSpec((tm, w), lambda i: (i, 1)), pl.BlockSpec((w, w), lambda i: (0, 0)), _row(w)],
        out_specs=[blk, blk, blk],
        out_shape=[jax.ShapeDtypeStruct((l, w), bf16), jax.ShapeDtypeStruct((l, w), f32),
                   jax.ShapeDtypeStruct((l, w), bf16)],
        compiler_params=_cparams("parallel"),
    )(y, z_src, w_glu, b_glu)


def glu_bwd(dout, y, t, z_src, w_glu):
    l, w = y.shape
    tm = _row_tile(l)

    def body(do_ref, y_ref, t_ref, z_ref, w_ref, dy_ref, dz_ref, dt_ref, db_ref):
        @pl.when(pl.program_id(0) == 0)
        def _():
            db_ref[...] = jnp.zeros_like(db_ref)

        yv, zv, dov = y_ref[...], z_ref[...].astype(f32), do_ref[...]
        g = _gelu(yv)
        sg = jax.nn.sigmoid(t_ref[...])
        dy2 = dov * _silu(zv)
        dz_ref[...] = (dov * g * sg * _silu_grad(zv)).astype(dz_ref.dtype)
        dt = dy2 * g * sg * (1.0 - sg)
        dtb = dt.astype(bf16)
        dt_ref[...] = dtb
        db_ref[...] += jnp.sum(dt, axis=0, keepdims=True)
        dg = dy2 * sg + _dot_nt(dtb, w_ref[...])
        dy_ref[...] = dg * _gelu_grad(yv)

    blk = pl.BlockSpec((tm, w), lambda i: (i, 0))
    return pl.pallas_call(
        body, name="glu_bwd", grid=(l // tm,),
        in_specs=[blk, blk, blk, pl.BlockSpec((tm, w), lambda i: (i, 1)), pl.BlockSpec((w, w), lambda i: (0, 0))],
        out_specs=[blk, pl.BlockSpec((tm, w), lambda i: (i, 1)), blk, _row(w)],
        out_shape=[jax.ShapeDtypeStruct((l, w), f32), jax.ShapeDtypeStruct((l, 2 * w), bf16),
                   jax.ShapeDtypeStruct((l, w), bf16), jax.ShapeDtypeStruct((1, w), f32)],
        compiler_params=_cparams("arbitrary"),
    )(dout, y, t, z_src, w_glu)


def _adamw(w, g, m, v):
    m = ADAM_B1 * m + (1.0 - ADAM_B1) * g
    v = ADAM_B2 * v + (1.0 - ADAM_B2) * (g * g)
    m_hat = m / (1.0 - ADAM_B1 ** ADAM_STEP)
    v_hat = v / (1.0 - ADAM_B2 ** ADAM_STEP)
    return -ADAM_LR * (m_hat / (jnp.sqrt(v_hat) + ADAM_EPS) + ADAM_WD * w), m, v


def adam_reduce(pieces, w, m, v, name):
    r, c = w.shape
    n = pieces.shape[0]
    tr = _tile(r, (256, 128, 64, 32, 16, 8))

    def body(p_ref, w_ref, m_ref, v_ref, g_ref, d_ref, nm_ref, nv_ref):
        g = p_ref[0].astype(f32)
        for s in range(1, n):
            g = g + p_ref[s].astype(f32)
        g_ref[...] = g
        d_ref[...], nm_ref[...], nv_ref[...] = _adamw(w_ref[...], g, m_ref[...], v_ref[...])

    blk = pl.BlockSpec((tr, c), lambda i: (i, 0))
    return pl.pallas_call(
        body, name=name, grid=(r // tr,),
        in_specs=[pl.BlockSpec((n, tr, c), lambda i: (0, i, 0)), blk, blk, blk],
        out_specs=[blk] * 4, out_shape=[jax.ShapeDtypeStruct((r, c), f32)] * 4,
        compiler_params=_cparams("parallel"),
    )(pieces, w, m, v)


def adam_w_mod(cond_t, dm, w, m, v):
    nl, d, cols = w.shape
    tr = _tile(d, (512, 256, 128))

    def body(c_ref, dm_ref, w_ref, m_ref, v_ref, g_ref, d_ref, nm_ref, nv_ref):
        g = jnp.dot(c_ref[...], dm_ref[...], preferred_element_type=f32, precision=lax.Precision.HIGHEST)
        g_ref[...] = g
        d_ref[...], nm_ref[...], nv_ref[...] = _adamw(w_ref[...], g, m_ref[...], v_ref[...])

    blk = pl.BlockSpec((None, tr, cols), lambda l, i: (l, i, 0))
    return pl.pallas_call(
        body, name="adam_w_mod", grid=(nl, d // tr),
        in_specs=[pl.BlockSpec((tr, N_DEV), lambda l, i: (i, 0)), pl.BlockSpec((None, N_DEV, cols), lambda l, i: (l, 0, 0)),
                  blk, blk, blk],
        out_specs=[blk] * 4, out_shape=[jax.ShapeDtypeStruct((nl, d, cols), f32)] * 4,
        compiler_params=_cparams("parallel", "parallel"),
    )(cond_t, dm, w, m, v)


def silu_rows(c_all):
    def body(c_ref, o_ref):
        o_ref[...] = _silu(c_ref[...])

    return pl.pallas_call(body, name="silu_rows", out_shape=jax.ShapeDtypeStruct(c_all.shape, f32))(c_all)


def _block_diag(x):
    g, a, b = x.shape
    nj = g // GROUPS_PER_LANE_BLOCK
    eye = jnp.eye(GROUPS_PER_LANE_BLOCK, dtype=x.dtype)
    x5 = x.reshape(nj, GROUPS_PER_LANE_BLOCK, a, b)
    return jnp.einsum("jgab,gh->jgahb", x5, eye).reshape(nj, GROUPS_PER_LANE_BLOCK * a, GROUPS_PER_LANE_BLOCK * b)


def _diag_blocks(x, a, b):
    nj = x.shape[0]
    x5 = x.reshape(nj, GROUPS_PER_LANE_BLOCK, a, GROUPS_PER_LANE_BLOCK, b)
    eye = jnp.eye(GROUPS_PER_LANE_BLOCK, dtype=x.dtype)
    return jnp.einsum("jgahb,gh->jgab", x5, eye).reshape(nj * GROUPS_PER_LANE_BLOCK, a, b)


PACK_ROW = SUBLANES * HEAD


def _pack(parts, row_multiple=SUBLANES):
    rows = []
    for p in parts:
        flat = p.reshape(-1)
        pad = (-flat.shape[0]) % PACK_ROW
        if pad:
            flat = jnp.concatenate([flat, jnp.zeros((pad,), flat.dtype)])
        rows.append(flat.reshape(-1, HEAD))
    pad = (-sum(r.shape[0] for r in rows)) % row_multiple
    if pad:
        rows.append(jnp.zeros((pad, HEAD), rows[0].dtype))
    return jnp.concatenate(rows, axis=0)


def _unpack(packed, shapes):
    out, r0 = [], 0
    for shp in shapes:
        n = math.prod(shp)
        nr = -(-n // PACK_ROW) * SUBLANES
        out.append(packed[r0:r0 + nr].reshape(-1)[:n].reshape(shp))
        r0 += nr
    return out


def adam_small(g, w, m, v):
    r, c = w.shape

    def body(g_ref, w_ref, m_ref, v_ref, d_ref, nm_ref, nv_ref):
        d_ref[...], nm_ref[...], nv_ref[...] = _adamw(w_ref[...], g_ref[...], m_ref[...], v_ref[...])

    tr = max(t for t in range(SUBLANES, 1024 + 1, SUBLANES) if r % t == 0)
    blk = pl.BlockSpec((tr, c), lambda i: (i, 0))
    return pl.pallas_call(
        body, name="adam_small", grid=(r // tr,),
        in_specs=[blk] * 4, out_specs=[blk] * 3, out_shape=[jax.ShapeDtypeStruct((r, c), f32)] * 3,
        compiler_params=_cparams("parallel"),
    )(g, w, m, v)


def kernel(x, c, ln_pre_g, ln_post_g, w_mod, b_mod, w_in_ab, w_out_ab, sgu_norm_g, sgu_w, sgu_b, w_in_ssm, w_out_ssm, lam_re, lam_im, b_re, b_im, c_re, c_im, d_skip, log_dt, w_glu, b_glu, loss_target, m_ln_pre_g, m_ln_post_g, m_w_mod, m_b_mod, m_w_in_ab, m_w_out_ab, m_sgu_norm_g, m_sgu_w, m_sgu_b, m_w_in_ssm, m_w_out_ssm, m_lam_re, m_lam_im, m_b_re, m_b_im, m_c_re, m_c_im, m_d_skip, m_log_dt, m_w_glu, m_b_glu, v_ln_pre_g, v_ln_post_g, v_w_mod, v_b_mod, v_w_in_ab, v_w_out_ab, v_sgu_norm_g, v_sgu_w, v_sgu_b, v_w_in_ssm, v_w_out_ssm, v_lam_re, v_lam_im, v_b_re, v_b_im, v_c_re, v_c_im, v_d_skip, v_log_dt, v_w_glu, v_b_glu):
    me = _my_index()
    x0 = x[0]
    l, d = x0.shape
    target = loss_target[0]
    nh = sgu_w.shape[1]
    wa = nh * HEAD
    n_grp, n_st = lam_re.shape[1], lam_re.shape[2]
    mod_cols = w_mod.shape[2]

    def after(a, first):
        return a + jnp.minimum(jnp.abs(first[(0,) * first.ndim].astype(f32)), 0.0).astype(a.dtype)

    c_all, d_skip_all, b_glu_all = all_gather([c, d_skip, b_glu], "gather_c")
    c_all = c_all.reshape(N_DEV, d)
    d_skip_all = d_skip_all.reshape(1, -1)
    b_glu_all = b_glu_all.reshape(1, -1)

    b_cols = lax.dynamic_slice_in_dim(b_mod, me * mod_cols, mod_cols, axis=1)
    (mod_all,) = all_gather([mod_part(c_all, w_mod, b_cols)], "gather_mod")
    (win_ab3,) = sequencer_exchange(GATHER, [after(w_in_ab[0], mod_all).astype(bf16)], "gather_w_in", 1)
    mod_mine = lax.dynamic_index_in_dim(mod_all, me, axis=2, keepdims=False)
    mod_rows = jnp.transpose(mod_mine, (1, 0, 2)).reshape(2, 3, 1, d)

    def rows(a, i):
        return a[i].reshape(1, d)

    shift0, scale0, gate0 = mod_rows[0, 0], mod_rows[0, 1], mod_rows[0, 2]
    h0, h0_t = prenorm_fwd(x0, rows(ln_pre_g, 0), shift0, scale0, "prenorm0")
    wout_ab3, win_ssm3, wout_ssm3, wglu = sequencer_exchange(
        GATHER, [after(w, win_ab3).astype(bf16) for w in (w_out_ab[0], w_in_ssm[0], w_out_ssm[0], w_glu[0])],
        "gather_w_rest", 2)
    proj0 = mm_nn(h0, win_ab3, bf16, "proj0")
    sgu_b3 = sgu_b[0].reshape(nh, HEAD, 1)
    cat, att, tot = sb_fwd(proj0, sgu_fwd(proj0, sgu_norm_g, sgu_w[0], sgu_b3), nh)
    wout_ab3 = wout_ab3.reshape(1, d, d)
    win_ssm3 = win_ssm3.reshape(1, d, d)
    wglu = wglu.reshape(w_glu.shape[2], w_glu.shape[2])
    y0 = mm_nn(cat, wout_ab3, f32, "out0")

    shift1, scale1, gate1 = mod_rows[1, 0], mod_rows[1, 1], mod_rows[1, 2]
    x1, h1, h1_t = post_prenorm_fwd(x0, y0, gate0, rows(ln_post_g, 0), rows(ln_pre_g, 1), shift1, scale1,
                                    "post0_prenorm1")
    proj1 = mm_nn(h1, win_ssm3, bf16, "proj1")
    w_ssm = proj1.shape[1] // 2
    ldt = log_dt[0].reshape(n_grp, 1)
    bt_re = jnp.transpose(b_re[0], (0, 2, 1))
    bt_im = jnp.transpose(b_im[0], (0, 2, 1))
    a_re, a_im, bbt_re, bbt_im = s5_params_fwd(lam_re[0], lam_im[0], ldt, bt_re, bt_im)
    bre3 = _block_diag(bbt_re).astype(bf16)
    bim3 = _block_diag(bbt_im).astype(bf16)
    cre3 = _block_diag(jnp.transpose(c_re[0], (0, 2, 1))).astype(bf16)
    cimn3 = _block_diag(-jnp.transpose(c_im[0], (0, 2, 1))).astype(bf16)
    a_re_row, a_im_row = a_re.reshape(1, -1), a_im.reshape(1, -1)
    y_ssm, hs_re, hs_im = ssm_fwd(proj1, bre3, bim3, cre3, cimn3, a_re_row, a_im_row, d_skip_all)
    g_act, t_glu, mix1 = glu_fwd(y_ssm, proj1, wglu, b_glu_all)
    y1 = mm_nn(mix1, wout_ssm3, f32, "out1")

    dx2, loss_tile, dy1, dgate1, dgpost1 = final_loss(x1, y1, gate1, rows(ln_post_g, 1), target)

    dmix1 = mm_nt(dy1, wout_ssm3, f32, "dmix1")
    gw_out_ssm = mm_tn(mix1, dy1, N_DEV, bf16, "gw_out_ssm")
    (p_out_ssm,) = sequencer_exchange(SCATTER, [gw_out_ssm], "scatter_g1", 3)
    dy_ssm, dproj1, dt_glu, db_glu = glu_bwd(dmix1, y_ssm, t_glu, proj1, wglu)
    gw_glu = mm_tn(g_act, dt_glu, 1, bf16, "gw_glu").reshape(N_DEV, -1, w_ssm)
    dproj1, dd_skip, da_re, da_im, dbre3, dbim3, dcre3, dcimn3 = ssm_bwd(
        dy_ssm, proj1, dproj1, hs_re, hs_im, bre3, bim3, cre3, cimn3, a_re_row, a_im_row, d_skip_all)
    gw_in_ssm = mm_nn(h1_t, dproj1[None], bf16, "gw_in_ssm").reshape(N_DEV, -1, proj1.shape[1])
    p_in_ssm, p_glu = sequencer_exchange(SCATTER, [gw_in_ssm, gw_glu], "scatter_g2", 4)
    dh1 = mm_nt(dproj1, win_ssm3, f32, "dh1")
    dx1, dshift1, dscale1, dgpre1, dy0, dgate0, dgpost0 = prenorm_post_bwd(
        dh1, x1, dx2, rows(ln_pre_g, 1), scale1, y0, gate0, rows(ln_post_g, 0), "prenorm1_post0_bwd")
    dlr, dli, dldt, dbt_re, dbt_im = s5_params_bwd(
        lam_re[0], lam_im[0], ldt, bt_re, bt_im, da_re.reshape(n_grp, n_st), da_im.reshape(n_grp, n_st),
        _diag_blocks(dbre3, SSM_GROUP, n_st), _diag_blocks(dbim3, SSM_GROUP, n_st))
    g_b_re = jnp.transpose(dbt_re, (0, 2, 1))
    g_b_im = jnp.transpose(dbt_im, (0, 2, 1))
    g_c_re = jnp.transpose(_diag_blocks(dcre3, n_st, SSM_GROUP), (0, 2, 1))
    g_c_im = -jnp.transpose(_diag_blocks(dcimn3, n_st, SSM_GROUP), (0, 2, 1))

    dcat = mm_nt(dy0, wout_ab3, f32, "dcat")
    gw_out_ab = mm_tn(cat, dy0, 1, bf16, "gw_out_ab").reshape(N_DEV, -1, d)
    (p_out_ab,) = sequencer_exchange(SCATTER, [gw_out_ab], "scatter_g3", 5)
    dproj0, dsgu_w, dsgu_b, dsgu_ng = sgu_bwd(proj0, dcat, sgu_norm_g, sgu_w[0], sgu_b3)
    dproj0 = sb_bwd(proj0, dcat, att, tot, dproj0, nh)
    gw_in_ab = mm_nn(h0_t, dproj0[None], bf16, "gw_in_ab", split_cols=N_DEV)
    (p_in_ab,) = sequencer_exchange(SCATTER, [gw_in_ab], "scatter_g4", 6)
    dh0 = mm_nt(dproj0, win_ab3, f32, "dh0")
    dx0, dshift0, dscale0, dgpre0 = prenorm_bwd(dh0, x0, dx1, rows(ln_pre_g, 0), scale0, "prenorm0_bwd")

    small_names = ["ln_pre_g", "ln_post_g", "b_mod", "sgu_norm_g", "sgu_w", "sgu_b", "lam_re", "lam_im", "b_re", "b_im",
                   "c_re", "c_im", "log_dt"]
    small_w = [ln_pre_g, ln_post_g, b_mod, sgu_norm_g, sgu_w, sgu_b, lam_re, lam_im, b_re, b_im, c_re, c_im, log_dt]
    small_m = [m_ln_pre_g, m_ln_post_g, m_b_mod, m_sgu_norm_g, m_sgu_w, m_sgu_b, m_lam_re, m_lam_im, m_b_re, m_b_im,
               m_c_re, m_c_im, m_log_dt]
    small_v = [v_ln_pre_g, v_ln_post_g, v_b_mod, v_sgu_norm_g, v_sgu_w, v_sgu_b, v_lam_re, v_lam_im, v_b_re, v_b_im,
               v_c_re, v_c_im, v_log_dt]
    def sharded(p, w, m, v, name):
        shp = w.shape
        w2, m2, v2 = (a.reshape(-1, shp[-1]) for a in (w, m, v))
        return [o.reshape(shp) for o in adam_reduce(p.reshape(p.shape[0], -1, shp[-1]), w2, m2, v2, name)]

    r_w_out_ssm = sharded(p_out_ssm, w_out_ssm, m_w_out_ssm, v_w_out_ssm, "adam_w_out_ssm")
    r_w_in_ssm = sharded(p_in_ssm, w_in_ssm, m_w_in_ssm, v_w_in_ssm, "adam_w_in_ssm")
    r_w_glu = sharded(p_glu, w_glu, m_w_glu, v_w_glu, "adam_w_glu")
    r_w_out_ab = sharded(p_out_ab, w_out_ab, m_w_out_ab, v_w_out_ab, "adam_w_out_ab")
    dmod = jnp.concatenate([dshift0, dscale0, dgate0, dshift1, dscale1, dgate1], axis=1)
    for done in (r_w_out_ssm, r_w_in_ssm, r_w_glu, r_w_out_ab):
        dmod = after(dmod, done[0])
    small_g = [jnp.concatenate([dgpre0, dgpre1]), jnp.concatenate([dgpost0, dgpost1]), dmod, dsgu_ng, dsgu_w, dsgu_b,
               dlr, dli, g_b_re, g_b_im, g_c_re, g_c_im, dldt]
    shapes = [w.shape for w in small_w]
    g_sum, dmod_all = all_reduce_rows(_pack(small_g + [dd_skip, db_glu, loss_tile], SUBLANES * N_DEV), dmod,
                                      "reduce_small_grads")
    n_rows_small = sum(-(-math.prod(s) // PACK_ROW) * SUBLANES for s in shapes)
    loss = g_sum[n_rows_small + 2 * (d_skip_all.shape[1] // HEAD), 0] * (0.5 / d)
    new_small = adam_small(g_sum, _pack(small_w), _pack(small_m), _pack(small_v))
    r_small = [_unpack(o, shapes) for o in [g_sum[:n_rows_small]] + list(new_small)]
    small = {n: [r_small[k][i] for k in range(4)] for i, n in enumerate(small_names)}
    vec_rows = d_skip_all.shape[1] // HEAD

    def my_columns(r0):
        whole = g_sum[r0:r0 + vec_rows].reshape(1, 1, -1)
        return lax.dynamic_slice_in_dim(whole, me * d_skip.shape[1], d_skip.shape[1], axis=2)

    r_d_skip = sharded(my_columns(n_rows_small), d_skip, m_d_skip, v_d_skip, "adam_d_skip")
    r_b_glu = sharded(my_columns(n_rows_small + vec_rows), b_glu, m_b_glu, v_b_glu, "adam_b_glu")
    r_w_in_ab = sharded(p_in_ab, w_in_ab, m_w_in_ab, v_w_in_ab, "adam_w_in_ab")

    dm_cols = jnp.transpose(
        lax.dynamic_slice_in_dim(dmod_all.reshape(N_DEV, 2, 3 * d), me * mod_cols, mod_cols, axis=2), (1, 0, 2))
    cond_t = jnp.transpose(silu_rows(c_all))
    r_w_mod = adam_w_mod(cond_t, dm_cols, w_mod, m_w_mod, v_w_mod)

    res = dict(small)
    res.update(w_mod=r_w_mod, w_in_ab=r_w_in_ab, w_out_ab=r_w_out_ab, w_in_ssm=r_w_in_ssm, w_out_ssm=r_w_out_ssm,
               d_skip=r_d_skip, w_glu=r_w_glu, b_glu=r_b_glu)
    order = ["ln_pre_g", "ln_post_g", "w_mod", "b_mod", "w_in_ab", "w_out_ab", "sgu_norm_g", "sgu_w", "sgu_b", "w_in_ssm",
             "w_out_ssm", "lam_re", "lam_im", "b_re", "b_im", "c_re", "c_im", "d_skip", "log_dt", "w_glu", "b_glu"]
    outs = [loss, dx0.reshape(x.shape)]
    for k in range(4):
        outs += [res[n][k] for n in order]
    return tuple(outs)
```

```python
import functools
import math

import jax
import jax.numpy as jnp
from jax import lax
from jax.experimental import pallas as pl
from jax.experimental.pallas import tpu as pltpu
from jax.experimental.pallas import tpu_sc as plsc

f32 = jnp.float32
bf16 = jnp.bfloat16

N_DEV = 8
EPS = 1e-6
HEAD = 128
SUBLANES = 8
SSM_GROUP = 16
SSM_STATE = 64
GROUPS_PER_LANE_BLOCK = HEAD // SSM_GROUP
STATES_PER_LANE_BLOCK = GROUPS_PER_LANE_BLOCK * SSM_STATE
VMEM_LIMIT = 56 * 2 ** 20
ADAM_LR, ADAM_B1, ADAM_B2, ADAM_EPS, ADAM_WD, ADAM_STEP = 0.001, 0.9, 0.999, 1e-08, 0.01, 10
_GELU_C0 = math.sqrt(2.0 / math.pi)
_GELU_C1 = 0.044715
MESH = pl.DeviceIdType.MESH


def _cparams(*sem):
    return pltpu.CompilerParams(dimension_semantics=sem if sem else None, vmem_limit_bytes=VMEM_LIMIT)


def _gelu(x):
    return 0.5 * x * (1.0 + jnp.tanh(_GELU_C0 * (x + _GELU_C1 * x * x * x)))


def _gelu_grad(x):
    t = jnp.tanh(_GELU_C0 * (x + _GELU_C1 * x * x * x))
    return 0.5 * (1.0 + t) + 0.5 * x * (1.0 - t * t) * _GELU_C0 * (1.0 + 3.0 * _GELU_C1 * x * x)


def _silu(x):
    return x * jax.nn.sigmoid(x)


def _silu_grad(x):
    s = jax.nn.sigmoid(x)
    return s * (1.0 + x * (1.0 - s))


def _dot(a, b):
    return jnp.dot(a, b, preferred_element_type=f32)


def _dot_nt(a, b):
    return lax.dot_general(a, b, (((1,), (1,)), ((), ())), preferred_element_type=f32)


def _dot_tn(a, b):
    return lax.dot_general(a, b, (((0,), (0,)), ((), ())), preferred_element_type=f32)


def _split_bf16(v):
    hi = v.astype(bf16)
    lo = (v - hi.astype(f32)).astype(bf16)
    return hi, lo


def _row(d):
    return pl.BlockSpec((1, d), lambda *_: (0, 0))


def _my_index():
    return 4 * lax.axis_index("x") + 2 * lax.axis_index("y") + lax.axis_index("c")


def _peer(k):
    x, y, c = lax.axis_index("x"), lax.axis_index("y"), lax.axis_index("c")
    return (1 - x if k & 4 else x, 1 - y if k & 2 else y, 1 - c if k & 1 else c)


def all_gather(arrs, name):
    n = len(arrs)

    def body(*refs):
        ins, outs = refs[:n], refs[n:2 * n]
        send, recv, local = refs[2 * n:]
        me = _my_index()
        copies = []
        for a in range(n):
            cp = pltpu.make_async_copy(ins[a], outs[a].at[me], local.at[a])
            cp.start()
            copies.append(cp)
            for k in range(1, N_DEV):
                s = a * (N_DEV - 1) + k - 1
                cp = pltpu.make_async_remote_copy(src_ref=ins[a], dst_ref=outs[a].at[me], send_sem=send.at[s],
                                                  recv_sem=recv.at[s], device_id=_peer(k), device_id_type=MESH)
                cp.start()
                copies.append(cp)
        for cp in copies:
            cp.wait()

    any_spec = pl.BlockSpec(memory_space=pl.ANY)
    outs = pl.pallas_call(
        body, name=name,
        out_shape=[jax.ShapeDtypeStruct((N_DEV,) + a.shape, a.dtype) for a in arrs],
        in_specs=[any_spec] * n, out_specs=[any_spec] * n,
        scratch_shapes=[pltpu.SemaphoreType.DMA((n * (N_DEV - 1),)), pltpu.SemaphoreType.DMA((n * (N_DEV - 1),)),
                        pltpu.SemaphoreType.DMA((n,))],
        compiler_params=pltpu.CompilerParams(has_side_effects=True),
    )(*arrs)
    return list(outs)


def all_reduce_rows(pack, extra, name):
    r, c = pack.shape
    rs = r // N_DEV
    n_peer = N_DEV - 1

    def body(p_ref, x_ref, o_ref, xo_ref, land, red, send1, recv1, send2, recv2, sendx, recvx, local):
        me = _my_index()

        def rows(i):
            return pl.ds(pl.multiple_of(i * rs, SUBLANES), rs)

        own = [pltpu.make_async_copy(p_ref.at[rows(me)], land.at[me], local.at[0]),
               pltpu.make_async_copy(x_ref, xo_ref.at[me], local.at[1])]
        first = []
        for k in range(1, N_DEV):
            first.append(pltpu.make_async_remote_copy(
                src_ref=p_ref.at[rows(jnp.bitwise_xor(me, k))], dst_ref=land.at[me], send_sem=send1.at[k - 1],
                recv_sem=recv1.at[k - 1], device_id=_peer(k), device_id_type=MESH))
            first.append(pltpu.make_async_remote_copy(
                src_ref=x_ref, dst_ref=xo_ref.at[me], send_sem=sendx.at[k - 1], recv_sem=recvx.at[k - 1],
                device_id=_peer(k), device_id_type=MESH))
        for cp in own + first:
            cp.start()
        for cp in own + first:
            cp.wait()
        acc = land[0]
        for s in range(1, N_DEV):
            acc = acc + land[s]
        red[...] = acc
        mine = pltpu.make_async_copy(red, o_ref.at[rows(me)], local.at[2])
        second = [pltpu.make_async_remote_copy(
            src_ref=red, dst_ref=o_ref.at[rows(me)], send_sem=send2.at[k - 1], recv_sem=recv2.at[k - 1],
            device_id=_peer(k), device_id_type=MESH) for k in range(1, N_DEV)]
        for cp in [mine] + second:
            cp.start()
        for cp in [mine] + second:
            cp.wait()

    any_spec = pl.BlockSpec(memory_space=pl.ANY)
    return pl.pallas_call(
        body, name=name,
        out_shape=[jax.ShapeDtypeStruct((r, c), pack.dtype), jax.ShapeDtypeStruct((N_DEV,) + extra.shape, extra.dtype)],
        in_specs=[any_spec, any_spec], out_specs=[any_spec, any_spec],
        scratch_shapes=[pltpu.VMEM((N_DEV, rs, c), pack.dtype), pltpu.VMEM((rs, c), pack.dtype)]
        + [pltpu.SemaphoreType.DMA((n_peer,))] * 6 + [pltpu.SemaphoreType.DMA((3,))],
        compiler_params=pltpu.CompilerParams(has_side_effects=True),
    )(pack, extra)


GATHER, SCATTER = "gather", "scatter"


def _exchange_copies(srcs, lands, send, recv):
    me = _my_index()
    copies = []
    for a, (src, land) in enumerate(zip(srcs, lands)):
        for k in range(1, N_DEV):
            s = a * (N_DEV - 1) + k - 1
            copies.append(pltpu.make_async_remote_copy(
                src_ref=src.at[jnp.bitwise_xor(me, k)], dst_ref=land.at[me],
                send_sem=send.at[s], recv_sem=recv.at[s], device_id=_peer(k), device_id_type=MESH))
    return copies


def sequencer_exchange(kind, arrs, name, collective_id):
    n = len(arrs)
    n_sem = n * (N_DEV - 1)
    land_shapes = [((N_DEV,) + a.shape if kind == GATHER else a.shape) for a in arrs]
    srcs = [jax.new_ref(a, memory_space=pltpu.MemorySpace.HBM) for a in arrs]
    lands = [jax.empty_ref(jax.ShapeDtypeStruct(s, a.dtype), memory_space=pltpu.MemorySpace.HBM)
             for s, a in zip(land_shapes, arrs)]

    @pl.kernel(mesh=plsc.ScalarSubcoreMesh(axis_name="sequencer", num_cores=1), name=name,
               scratch_types=(pltpu.SemaphoreType.DMA((n_sem,)), pltpu.SemaphoreType.DMA((n_sem,)),
                              pltpu.SemaphoreType.DMA((n,))),
               compiler_params=pltpu.CompilerParams(collective_id=collective_id))
    def launch(send, recv, local):
        barrier = pltpu.get_barrier_semaphore()
        for k in range(1, N_DEV):
            pl.semaphore_signal(barrier, inc=1, device_id=_peer(k), device_id_type=MESH)
        pl.semaphore_wait(barrier, N_DEV - 1)
        me = _my_index()
        mine = [pltpu.make_async_copy(src if kind == GATHER else src.at[me], land.at[me], local.at[a])
                for a, (src, land) in enumerate(zip(srcs, lands))]
        if kind == SCATTER:
            copies = mine + _exchange_copies(srcs, lands, send, recv)
            for cp in copies:
                cp.start()
            for cp in copies:
                cp.wait()
            return

        def block_copy(a, slot, block, k, src=None):
            s = a * (N_DEV - 1) + slot
            return pltpu.make_async_remote_copy(
                src_ref=lands[a].at[block] if src is None else src, dst_ref=lands[a].at[block],
                send_sem=send.at[s], recv_sem=recv.at[s], device_id=_peer(k), device_id_type=MESH)

        chips = (2, 4, 6)
        sibling = jnp.bitwise_xor(me, 1)
        first = [block_copy(a, slot, me, k, src=srcs[a]) for a in range(n) for slot, k in enumerate((1,) + chips)]
        for cp in mine + first:
            cp.start()
        passed = []
        for a in range(n):
            for i, k in enumerate(chips):
                block = jnp.bitwise_xor(me, k)
                block_copy(a, 1 + i, block, k).wait_recv()
                passed.append(block_copy(a, 4 + i, block, 1))
                passed[-1].start()
        for a in range(n):
            block_copy(a, 0, sibling, 1).wait_recv()
            for i, k in enumerate(chips):
                block_copy(a, 4 + i, jnp.bitwise_xor(sibling, k), 1).wait_recv()
        for cp in mine:
            cp.wait()
        for cp in first + passed:
            cp.wait_send()

    launch()
    return [land[...] for land in lands]


def _tile(n, pref):
    for t in pref:
        if n % t == 0:
            return t
    return n


MM_WIDE = 1024
MM_WEIGHT_BLOCK = 8 * 2 ** 20


def _blocks_per_step(nb, fits):
    return max(g for g in range(1, nb + 1) if nb % g == 0 and fits(g))


def mm_nn(a, b3, out_dtype, name, split_cols=None):
    m, k = a.shape
    nb, _, bn = b3.shape
    tm = _tile(m, (1024, 512, 256, 128))
    tn = bn // split_cols if split_cols else _tile(bn, (1024, 896, 512, 256, 128))
    per = bn // tn
    gb = _blocks_per_step(nb, lambda g: g == 1 or (per == 1 and g * bn <= MM_WIDE))

    def body(a_ref, b_ref, o_ref):
        for g in range(gb):
            o_ref[:, g * tn:(g + 1) * tn] = _dot(a_ref[...], b_ref[g]).astype(o_ref.dtype)

    if split_cols:
        out_spec = pl.BlockSpec((None, tm, tn), lambda i, j, jj: (jj, i, 0))
        out_shape = jax.ShapeDtypeStruct((split_cols, m, tn), out_dtype)
    else:
        out_spec = pl.BlockSpec((tm, gb * tn), lambda i, j, jj: (i, j * per + jj))
        out_shape = jax.ShapeDtypeStruct((m, nb * bn), out_dtype)
    return pl.pallas_call(
        body, name=name, grid=(m // tm, nb // gb, per),
        in_specs=[pl.BlockSpec((tm, k), lambda i, j, jj: (i, 0)),
                  pl.BlockSpec((gb, k, tn), lambda i, j, jj: (j, 0, jj))],
        out_specs=out_spec, out_shape=out_shape,
        compiler_params=_cparams("parallel", "arbitrary", "arbitrary"),
    )(a, b3)


def mm_nt(a, w3, out_dtype, name):
    m, _ = a.shape
    nb, ko, bn = w3.shape
    tm = _tile(m, (512, 256, 128))
    tko = _tile(ko, (1024, 512, 256, 128))
    gb = _blocks_per_step(nb, lambda g: g * tko * bn * w3.dtype.itemsize <= MM_WEIGHT_BLOCK)
    ns = nb // gb

    def body(a_ref, w_ref, o_ref, acc_ref):
        j = pl.program_id(2)

        @pl.when(j == 0)
        def _():
            acc_ref[...] = jnp.zeros_like(acc_ref)

        part = _dot_nt(a_ref[:, :bn], w_ref[0])
        for g in range(1, gb):
            part += _dot_nt(a_ref[:, g * bn:(g + 1) * bn], w_ref[g])
        acc_ref[...] += part

        @pl.when(j == ns - 1)
        def _():
            o_ref[...] = acc_ref[...].astype(o_ref.dtype)

    return pl.pallas_call(
        body, name=name, grid=(m // tm, ko // tko, ns),
        in_specs=[pl.BlockSpec((tm, gb * bn), lambda i, o, j: (i, j)),
                  pl.BlockSpec((gb, tko, bn), lambda i, o, j: (j, o, 0))],
        out_specs=pl.BlockSpec((tm, tko), lambda i, o, j: (i, o)),
        out_shape=jax.ShapeDtypeStruct((m, ko), out_dtype),
        scratch_shapes=[pltpu.VMEM((tm, tko), f32)],
        compiler_params=_cparams("parallel", "arbitrary", "arbitrary"),
    )(a, w3)


def mm_tn(a, dy, ncb, out_dtype, name):
    l, ka = a.shape
    _, n = dy.shape
    bn = n // ncb
    tl = _tile(l, (1024, 512, 256, 128))
    tka = _tile(ka, (512, 256, 128))
    tn = _tile(bn, (1024, 896, 512, 256, 128))
    per = bn // tn
    gb = _blocks_per_step(ncb, lambda g: g == 1 or (per == 1 and g * bn <= MM_WIDE))
    nl = l // tl

    def body(a_ref, dy_ref, o_ref, acc_ref):
        s = pl.program_id(2)

        @pl.when(s == 0)
        def _():
            acc_ref[...] = jnp.zeros_like(acc_ref)

        acc_ref[...] += _dot_tn(a_ref[...], dy_ref[...])

        @pl.when(s == nl - 1)
        def _():
            for g in range(gb):
                o_ref[g] = acc_ref[:, g * tn:(g + 1) * tn].astype(o_ref.dtype)

    return pl.pallas_call(
        body, name=name, grid=(ka // tka, n // (gb * tn), nl),
        in_specs=[pl.BlockSpec((tl, tka), lambda i, j, s: (s, i)),
                  pl.BlockSpec((tl, gb * tn), lambda i, j, s: (s, j))],
        out_specs=pl.BlockSpec((gb, tka, tn), lambda i, j, s: (j // per, i, j % per)),
        out_shape=jax.ShapeDtypeStruct((ncb, ka, bn), out_dtype),
        scratch_shapes=[pltpu.VMEM((tka, gb * tn), f32)],
        compiler_params=_cparams("parallel", "parallel", "arbitrary"),
    )(a, dy)


def mod_part(c_all, w_mod, b_cols):
    nl, d, cols = w_mod.shape

    def body(c_ref, w_ref, b_ref, o_ref):
        cond = _silu(c_ref[...]).astype(bf16)
        o_ref[...] = _dot(cond, w_ref[...].astype(bf16)) + b_ref[...]

    return pl.pallas_call(
        body, name="mod_part", grid=(nl,),
        in_specs=[pl.BlockSpec((N_DEV, d), lambda l: (0, 0)),
                  pl.BlockSpec((None, d, cols), lambda l: (l, 0, 0)),
                  pl.BlockSpec((None, 1, cols), lambda l: (l, 0, 0))],
        out_specs=pl.BlockSpec((None, N_DEV, cols), lambda l: (l, 0, 0)),
        out_shape=jax.ShapeDtypeStruct((nl, N_DEV, cols), f32),
        compiler_params=_cparams("arbitrary"),
    )(c_all, w_mod, b_cols.reshape(nl, 1, cols))


def _row_tile(l):
    return _tile(l, (512, 256, 128))


def _entry_rows(xv, g_ref, sh_ref, sc_ref, h_ref, ht_ref):
    r = lax.rsqrt(jnp.mean(xv * xv, axis=-1, keepdims=True) + EPS)
    h = xv * r * (g_ref[...] * (1.0 + sc_ref[...])) + sh_ref[...]
    h_ref[...] = h.astype(h_ref.dtype)
    ht_ref[...] = jnp.transpose(h).astype(ht_ref.dtype)


def prenorm_fwd(x, g, shift, scale, name):
    l, d = x.shape
    tm = _row_tile(l)

    def body(x_ref, g_ref, sh_ref, sc_ref, h_ref, ht_ref):
        _entry_rows(x_ref[...], g_ref, sh_ref, sc_ref, h_ref, ht_ref)

    return pl.pallas_call(
        body, name=name, grid=(l // tm,),
        in_specs=[pl.BlockSpec((tm, d), lambda i: (i, 0)), _row(d), _row(d), _row(d)],
        out_specs=[pl.BlockSpec((tm, d), lambda i: (i, 0)), pl.BlockSpec((d, tm), lambda i: (0, i))],
        out_shape=[jax.ShapeDtypeStruct((l, d), bf16), jax.ShapeDtypeStruct((d, l), bf16)],
        compiler_params=_cparams("parallel"),
    )(x, g, shift, scale)


def post_prenorm_fwd(x, y, gate, g_post, g_pre, shift, scale, name):
    l, d = x.shape
    tm = _row_tile(l)

    def body(x_ref, y_ref, gate_ref, gp_ref, g_ref, sh_ref, sc_ref, o_ref, h_ref, ht_ref):
        yv = y_ref[...]
        r = lax.rsqrt(jnp.mean(yv * yv, axis=-1, keepdims=True) + EPS)
        xv = x_ref[...] + gate_ref[...] * (yv * r * gp_ref[...])
        o_ref[...] = xv
        _entry_rows(xv, g_ref, sh_ref, sc_ref, h_ref, ht_ref)

    blk = pl.BlockSpec((tm, d), lambda i: (i, 0))
    return pl.pallas_call(
        body, name=name, grid=(l // tm,),
        in_specs=[blk, blk] + [_row(d)] * 5, out_specs=[blk, blk, pl.BlockSpec((d, tm), lambda i: (0, i))],
        out_shape=[jax.ShapeDtypeStruct((l, d), f32), jax.ShapeDtypeStruct((l, d), bf16),
                   jax.ShapeDtypeStruct((d, l), bf16)],
        compiler_params=_cparams("parallel"),
    )(x, y, gate, g_post, g_pre, shift, scale)


def _post_bwd_rows(dxv, yv, r, gate, gv, dy_ref, dgate_ref, dg_ref):
    yn = yv * r
    dgate_ref[...] += jnp.sum(dxv * yn * gv, axis=0, keepdims=True)
    dyg = dxv * gate
    dg_ref[...] += jnp.sum(dyg * yn, axis=0, keepdims=True)
    dyn = dyg * gv
    dy_ref[...] = (r * (dyn - yn * jnp.mean(dyn * yn, axis=-1, keepdims=True))).astype(dy_ref.dtype)


def final_loss(x, y, gate, g, target):
    l, d = x.shape
    tm = _row_tile(l)

    def body(x_ref, y_ref, gate_ref, g_ref, t_ref, dx_ref, loss_ref, dy_ref, dgate_ref, dg_ref):
        @pl.when(pl.program_id(0) == 0)
        def _():
            loss_ref[...] = jnp.zeros_like(loss_ref)
            dgate_ref[...] = jnp.zeros_like(dgate_ref)
            dg_ref[...] = jnp.zeros_like(dg_ref)

        yv, gate, gv = y_ref[...], gate_ref[...], g_ref[...]
        r = lax.rsqrt(jnp.mean(yv * yv, axis=-1, keepdims=True) + EPS)
        diff = x_ref[...] + gate * (yv * r * gv) - t_ref[...]
        dxv = diff * (1.0 / d)
        dx_ref[...] = dxv
        loss_ref[...] += jnp.sum(diff * diff)
        _post_bwd_rows(dxv, yv, r, gate, gv, dy_ref, dgate_ref, dg_ref)

    blk = pl.BlockSpec((tm, d), lambda i: (i, 0))
    return pl.pallas_call(
        body, name="final_loss", grid=(l // tm,),
        in_specs=[blk, blk, _row(d), _row(d), blk],
        out_specs=[blk, pl.BlockSpec((SUBLANES, HEAD), lambda i: (0, 0)), blk, _row(d), _row(d)],
        out_shape=[jax.ShapeDtypeStruct((l, d), f32), jax.ShapeDtypeStruct((SUBLANES, HEAD), f32),
                   jax.ShapeDtypeStruct((l, d), bf16), jax.ShapeDtypeStruct((1, d), f32), jax.ShapeDtypeStruct((1, d), f32)],
        compiler_params=_cparams("arbitrary"),
    )(x, y, gate, g, target)


def prenorm_bwd(dh, x, dx_next, g, scale, name):
    l, d = x.shape
    tm = _row_tile(l)

    def body(dh_ref, x_ref, dxn_ref, g_ref, sc_ref, dx_ref, dsh_ref, dsc_ref, dg_ref):
        @pl.when(pl.program_id(0) == 0)
        def _():
            dsh_ref[...] = jnp.zeros_like(dsh_ref)
            dsc_ref[...] = jnp.zeros_like(dsc_ref)
            dg_ref[...] = jnp.zeros_like(dg_ref)

        xv, dhv, gv, sc1 = x_ref[...], dh_ref[...], g_ref[...], 1.0 + sc_ref[...]
        r = lax.rsqrt(jnp.mean(xv * xv, axis=-1, keepdims=True) + EPS)
        xn = xv * r
        dhx = dhv * xn
        dsh_ref[...] += jnp.sum(dhv, axis=0, keepdims=True)
        dsc_ref[...] += jnp.sum(dhx * gv, axis=0, keepdims=True)
        dg_ref[...] += jnp.sum(dhx * sc1, axis=0, keepdims=True)
        dxn = dhv * (gv * sc1)
        dx_ref[...] = dxn_ref[...] + r * (dxn - xn * jnp.mean(dxn * xn, axis=-1, keepdims=True))

    blk = pl.BlockSpec((tm, d), lambda i: (i, 0))
    return pl.pallas_call(
        body, name=name, grid=(l // tm,),
        in_specs=[blk, blk, blk, _row(d), _row(d)], out_specs=[blk, _row(d), _row(d), _row(d)],
        out_shape=[jax.ShapeDtypeStruct((l, d), f32)] + [jax.ShapeDtypeStruct((1, d), f32)] * 3,
        compiler_params=_cparams("arbitrary"),
    )(dh, x, dx_next, g, scale)


def prenorm_post_bwd(dh, x, dx_next, g, scale, y, gate, g_post, name):
    l, d = x.shape
    tm = _tile(l, (256, 128))

    def body(dh_ref, x_ref, dxn_ref, g_ref, sc_ref, y_ref, gate_ref, gp_ref,
             dx_ref, dsh_ref, dsc_ref, dg_ref, dy_ref, dgate_ref, dgp_ref):
        @pl.when(pl.program_id(0) == 0)
        def _():
            for ref in (dsh_ref, dsc_ref, dg_ref, dgate_ref, dgp_ref):
                ref[...] = jnp.zeros_like(ref)

        xv, dhv, gv, sc1 = x_ref[...], dh_ref[...], g_ref[...], 1.0 + sc_ref[...]
        r = lax.rsqrt(jnp.mean(xv * xv, axis=-1, keepdims=True) + EPS)
        xn = xv * r
        dhx = dhv * xn
        dsh_ref[...] += jnp.sum(dhv, axis=0, keepdims=True)
        dsc_ref[...] += jnp.sum(dhx * gv, axis=0, keepdims=True)
        dg_ref[...] += jnp.sum(dhx * sc1, axis=0, keepdims=True)
        dxn = dhv * (gv * sc1)
        dxv = dxn_ref[...] + r * (dxn - xn * jnp.mean(dxn * xn, axis=-1, keepdims=True))
        dx_ref[...] = dxv
        yv = y_ref[...]
        ry = lax.rsqrt(jnp.mean(yv * yv, axis=-1, keepdims=True) + EPS)
        _post_bwd_rows(dxv, yv, ry, gate_ref[...], gp_ref[...], dy_ref, dgate_ref, dgp_ref)

    blk = pl.BlockSpec((tm, d), lambda i: (i, 0))
    row = jax.ShapeDtypeStruct((1, d), f32)
    return pl.pallas_call(
        body, name=name, grid=(l // tm,),
        in_specs=[blk, blk, blk, _row(d), _row(d), blk, _row(d), _row(d)],
        out_specs=[blk, _row(d), _row(d), _row(d), blk, _row(d), _row(d)],
        out_shape=[jax.ShapeDtypeStruct((l, d), f32), row, row, row, jax.ShapeDtypeStruct((l, d), bf16), row, row],
        compiler_params=_cparams("arbitrary"),
    )(dh, x, dx_next, g, scale, y, gate, g_post)


def _tril_mask():
    r = lax.broadcasted_iota(jnp.int32, (HEAD, HEAD), 0)
    c = lax.broadcasted_iota(jnp.int32, (HEAD, HEAD), 1)
    return r >= c


def sgu_fwd(proj, norm_g, w_s, b_s):
    l = proj.shape[0]
    nh = w_s.shape[0]
    wa = nh * HEAD

    def body(au_ref, av_ref, az_ref, ng_ref, w_ref, b_ref, o_ref):
        tril = _tril_mask()
        for h in range(nh):
            sl = slice(h * HEAD, (h + 1) * HEAD)
            gv = _gelu(av_ref[:, sl].astype(f32))
            r = lax.rsqrt(jnp.mean(gv * gv, axis=-1, keepdims=True) + EPS)
            vh = gv * r * ng_ref[:, sl]
            wm = jnp.where(tril, w_ref[h], 0.0).astype(bf16)
            s = _dot(wm, vh.astype(bf16)) + b_ref[h]
            o_ref[:, sl] = (_gelu(au_ref[:, sl].astype(f32)) * s * _silu(az_ref[:, sl].astype(f32))).astype(o_ref.dtype)

    def col(j):
        return pl.BlockSpec((HEAD, wa), lambda n: (n, j))

    return pl.pallas_call(
        body, name="sgu_fwd", grid=(l // HEAD,),
        in_specs=[col(0), col(1), col(2), _row(wa),
                  pl.BlockSpec((nh, HEAD, HEAD), lambda n: (0, 0, 0)), pl.BlockSpec((nh, HEAD, 1), lambda n: (0, 0, 0))],
        out_specs=pl.BlockSpec((HEAD, wa), lambda n: (n, 0)),
        out_shape=jax.ShapeDtypeStruct((l, 2 * wa), bf16),
        compiler_params=_cparams("parallel"),
    )(proj, proj, proj, norm_g, w_s, b_s)


def sgu_bwd(proj, dcat, norm_g, w_s, b_s):
    l = proj.shape[0]
    nh = w_s.shape[0]
    wa = nh * HEAD

    def body(au_ref, av_ref, az_ref, do_ref, ng_ref, w_ref, b_ref, da_ref, dw_ref, db_ref, dng_ref):
        @pl.when(pl.program_id(0) == 0)
        def _():
            dw_ref[...] = jnp.zeros_like(dw_ref)
            db_ref[...] = jnp.zeros_like(db_ref)
            dng_ref[...] = jnp.zeros_like(dng_ref)

        tril = _tril_mask()
        for h in range(nh):
            sl = slice(h * HEAD, (h + 1) * HEAD)
            au, av, az = au_ref[:, sl].astype(f32), av_ref[:, sl].astype(f32), az_ref[:, sl].astype(f32)
            ng = ng_ref[:, sl]
            gv = _gelu(av)
            r = lax.rsqrt(jnp.mean(gv * gv, axis=-1, keepdims=True) + EPS)
            gvn = gv * r
            vh = (gvn * ng).astype(bf16)
            wm = jnp.where(tril, w_ref[h], 0.0).astype(bf16)
            s = _dot(wm, vh) + b_ref[h]
            gu, sz = _gelu(au), _silu(az)
            dov = do_ref[:, sl].astype(f32)
            da_ref[:, sl] = (dov * s * sz * _gelu_grad(au)).astype(da_ref.dtype)
            da_ref[:, 2 * wa + h * HEAD:2 * wa + (h + 1) * HEAD] = (dov * gu * s * _silu_grad(az)).astype(da_ref.dtype)
            ds = dov * gu * sz
            db_ref[h] += jnp.sum(ds, axis=-1, keepdims=True)
            dsb = ds.astype(bf16)
            dw_ref[h] += jnp.where(tril, _dot_nt(dsb, vh), 0.0)
            dvh = _dot_tn(wm, dsb)
            dng_ref[:, sl] += jnp.sum(dvh * gvn, axis=0, keepdims=True)
            dgvn = dvh * ng
            dgv = r * (dgvn - gvn * jnp.mean(dgvn * gvn, axis=-1, keepdims=True))
            da_ref[:, wa + h * HEAD:wa + (h + 1) * HEAD] = (dgv * _gelu_grad(av)).astype(da_ref.dtype)

    def col(j):
        return pl.BlockSpec((HEAD, wa), lambda n: (n, j))

    whole_w = pl.BlockSpec((nh, HEAD, HEAD), lambda n: (0, 0, 0))
    whole_b = pl.BlockSpec((nh, HEAD, 1), lambda n: (0, 0, 0))
    return pl.pallas_call(
        body, name="sgu_bwd", grid=(l // HEAD,),
        in_specs=[col(0), col(1), col(2), col(0), _row(wa), whole_w, whole_b],
        out_specs=[pl.BlockSpec((HEAD, 3 * wa), lambda n: (n, 0)), whole_w, whole_b, _row(wa)],
        out_shape=[jax.ShapeDtypeStruct(proj.shape, bf16), jax.ShapeDtypeStruct((nh, HEAD, HEAD), f32),
                   jax.ShapeDtypeStruct((nh, HEAD, 1), f32), jax.ShapeDtypeStruct((1, wa), f32)],
        compiler_params=_cparams("arbitrary"),
    )(proj, proj, proj, dcat, norm_g, w_s, b_s)


_LOG2E = 1.0 / math.log(2.0)
_SB_EXP_CLAMP = 120.0


def _sb_scores(q, k, scale):
    z = _dot_nt(q, k) * (scale * _LOG2E)
    return z, jnp.maximum(z, jnp.log2(1.0 + jnp.exp2(jnp.minimum(z, _SB_EXP_CLAMP))))


SB_KEYS = 256


def _sb_sum_matrix(tri, kb):
    s = lax.broadcasted_iota(jnp.int32, (2 * kb, kb + HEAD), 0) % kb
    j = lax.broadcasted_iota(jnp.int32, (2 * kb, kb + HEAD), 1)
    return jnp.where(jnp.logical_or(j >= kb, tri(s, j)), 1.0, 0.0).astype(bf16)


def _sb_sums(x, sums):
    kb = x.shape[1]
    c2 = _dot(jnp.concatenate(_split_bf16(x), axis=1), sums)
    return c2[:, :kb], c2[:, kb:]


def _sb_wide(v, kb):
    return jnp.concatenate([v] * (kb // HEAD), axis=1) if kb > HEAD else v


def _sb_q_tile(l, most=512):
    return _tile(l, tuple(t for t in (1024, 512, 256, 128) if t <= most))


def _sb_band_levels(band):
    return _tile(band, (4, 2, 1))


def _sb_heads_per_step(nh, most):
    return _tile(nh, tuple(h for h in (4, 2) if h <= most))


def sb_fwd(proj, mixed, nh):
    l = proj.shape[0]
    wb = nh * HEAD
    tq = _sb_q_tile(l, 1024)
    kb = min(SB_KEYS, tq)
    band = tq // kb
    hp = _sb_heads_per_step(nh, 4)
    levels = _sb_band_levels(band)
    scale = 1.0 / math.sqrt(HEAD)
    qc, kc, vc, zc = 3 * nh, 4 * nh, 5 * nh, 6 * nh

    def body(q_ref, k_ref, v_ref, bz_ref, mixed_ref, o_ref, att_ref, tot_ref):
        del mixed_ref
        i = pl.program_id(1)
        sums = _sb_sum_matrix(lambda s, j: s > j, kb)
        t_pos = i * tq + lax.broadcasted_iota(jnp.int32, (tq, kb), 0)
        s_off = lax.broadcasted_iota(jnp.int32, (tq, kb), 1)

        def step(j, carry, masked, row0=0):
            rows = pl.ds(pl.multiple_of(j * kb, kb), kb)
            out = []
            for e in range(hp):
                acc, tot = carry[e]
                sl = slice(e * HEAD, (e + 1) * HEAD)
                z, sp = _sb_scores(q_ref[row0:, sl], k_ref[rows, sl], scale)
                lb = z - sp
                if masked:
                    mask = s_off[row0:] + j * kb < t_pos[row0:]
                    sp = jnp.where(mask, sp, 0.0)
                later, total = _sb_sums(sp, sums)
                w = jnp.exp2(lb + _sb_wide(tot[row0:], kb) - later)
                if masked:
                    w = jnp.where(mask, w, 0.0)
                new = (acc[row0:] + _dot(w.astype(bf16), v_ref[rows, sl]), tot[row0:] - total)
                out.append(tuple(jnp.concatenate([old[:row0], upd]) if row0 else upd for old, upd in zip(carry[e], new)))
            return tuple(out)

        zero = jnp.zeros((tq, HEAD), f32)
        carry = ((zero, zero),) * hp
        for lv in reversed(range(levels)):
            carry = lax.fori_loop(
                0, band // levels,
                lambda t, c, lv=lv: step(band * i + (lv + 1) * (band // levels) - 1 - t, c, True, lv * (tq // levels)), carry)
        carry = lax.fori_loop(0, band * i, lambda t, c: step(band * i - 1 - t, c, False), carry)
        for e in range(hp):
            acc, tot = carry[e]
            sl = slice(e * HEAD, (e + 1) * HEAD)
            att_ref[:, sl] = acc.astype(att_ref.dtype)
            o_ref[:, sl] = (acc * _silu(bz_ref[:, sl].astype(f32))).astype(o_ref.dtype)
            tot_ref[e] = tot[:, :1]

    blk = lambda c0: pl.BlockSpec((tq, hp * HEAD), lambda g, i: (i, c0 // hp + g))
    head = lambda c0: pl.BlockSpec((l, hp * HEAD), lambda g, i: (0, c0 // hp + g))
    return pl.pallas_call(
        body, name="sb_fwd", grid=(nh // hp, l // tq),
        in_specs=[blk(qc), head(kc), head(vc), blk(zc), pl.BlockSpec(memory_space=pl.ANY)],
        out_specs=[blk(mixed.shape[1] // HEAD - nh), blk(0), pl.BlockSpec((hp, tq, 1), lambda g, i: (g, i, 0))],
        out_shape=[jax.ShapeDtypeStruct(mixed.shape, bf16), jax.ShapeDtypeStruct((l, wb), bf16),
                   jax.ShapeDtypeStruct((nh, l, 1), f32)],
        input_output_aliases={4: 0},
        compiler_params=_cparams("parallel", "arbitrary"),
    )(proj, proj, proj, proj, mixed)


def sb_bwd(proj, dcat, att, tot, dproj, nh):
    l = proj.shape[0]
    wb = nh * HEAD
    tq = _sb_q_tile(l, 1024)
    kb = min(SB_KEYS, tq)
    band = tq // kb
    nq = l // tq
    hp = _sb_heads_per_step(nh, 2)
    levels = _sb_band_levels(band)
    scale = 1.0 / math.sqrt(HEAD)
    qc, kc, vc, zc = 3 * nh, 4 * nh, 5 * nh, 6 * nh

    def body(q_ref, k_ref, v_ref, bz_ref, do_ref, att_ref, tot_ref, dproj_in, dproj_ref, dk_acc, dv_acc, dob_ref,
             tile_ref, head_ref, sems):
        del dproj_in
        g, i = pl.program_id(0), pl.program_id(1)

        def put(src, row0, c0, k):
            cols = pl.ds(pl.multiple_of((c0 + g * hp) * HEAD, HEAD), hp * HEAD)
            cp = pltpu.make_async_copy(src, dproj_ref.at[pl.ds(row0, src.shape[0]), cols], sems.at[k])
            cp.start()
            return cp

        @pl.when(i == 0)
        def _():
            dk_acc[...] = jnp.zeros_like(dk_acc)
            dv_acc[...] = jnp.zeros_like(dv_acc)

        my_rows = pl.multiple_of(i * tq, tq)
        bz = bz_ref[...].astype(f32)
        dov = do_ref[...].astype(f32)
        tile_ref[0] = (dov * att_ref[...].astype(f32) * _silu_grad(bz)).astype(bf16)
        dbz_copy = put(tile_ref.at[0], my_rows, zc, 0)
        dob_ref[...] = (dov * _silu(bz)).astype(bf16)
        upto = _sb_sum_matrix(lambda s, j: s <= j, kb)
        before = _sb_sum_matrix(lambda j, s: j < s, kb)
        t_pos = i * tq + lax.broadcasted_iota(jnp.int32, (tq, kb), 0)
        s_off = lax.broadcasted_iota(jnp.int32, (tq, kb), 1)

        def step(j, carry, masked, row0=0):
            rows = pl.ds(pl.multiple_of(j * kb, kb), kb)
            out = []
            for h in range(hp):
                dq, sp_seen, e_seen = (c[row0:] for c in carry[h])
                sl = slice(h * HEAD, (h + 1) * HEAD)
                q, kj, vj, dob = q_ref[row0:, sl], k_ref[rows, sl], v_ref[rows, sl], dob_ref[row0:, sl]
                z, sp = _sb_scores(q, kj, scale)
                lb = z - sp
                if masked:
                    mask = s_off[row0:] + j * kb < t_pos[row0:]
                    sp = jnp.where(mask, sp, 0.0)
                sp_upto, sp_total = _sb_sums(sp, upto)
                w = jnp.exp2(lb + _sb_wide(sp_seen, kb) + sp_upto)
                if masked:
                    w = jnp.where(mask, w, 0.0)
                dv_acc[rows, sl] += _dot_tn(w.astype(bf16), dob)
                e = _dot_nt(dob, vj) * w
                e_before, e_total = _sb_sums(e, before)
                dz = (e - (e + _sb_wide(e_seen, kb) + e_before) * jnp.exp2(lb)) * scale
                if masked:
                    dz = jnp.where(mask, dz, 0.0)
                dz = dz.astype(bf16)
                dk_acc[rows, sl] += _dot_tn(dz, q)
                new = (dq + _dot(dz, kj), sp_seen + sp_total, e_seen + e_total)
                out.append(tuple(jnp.concatenate([old[:row0], upd]) if row0 else upd for old, upd in zip(carry[h], new)))
            return tuple(out)

        zero = jnp.zeros((tq, HEAD), f32)
        init = tuple((zero, jnp.broadcast_to(tot_ref[h], (tq, HEAD)), zero) for h in range(hp))
        carry = lax.fori_loop(0, band * i, lambda j, c: step(j, c, False), init)
        for lv in range(levels):
            carry = lax.fori_loop(
                0, band // levels,
                lambda t, c, lv=lv: step(band * i + lv * (band // levels) + t, c, True, lv * (tq // levels)), carry)
        for h in range(hp):
            tile_ref[1, :, h * HEAD:(h + 1) * HEAD] = carry[h][0].astype(bf16)
        dq_copy = put(tile_ref.at[1], my_rows, qc, 1)
        dbz_copy.wait()
        dq_copy.wait()

        @pl.when(i == nq - 1)
        def _():
            head_ref[0] = dk_acc[...].astype(bf16)
            head_ref[1] = dv_acc[...].astype(bf16)
            copies = [put(head_ref.at[0], 0, kc, 2), put(head_ref.at[1], 0, vc, 3)]
            for cp in copies:
                cp.wait()

    blk = lambda c0: pl.BlockSpec((tq, hp * HEAD), lambda g, i: (i, c0 // hp + g))
    head = lambda c0: pl.BlockSpec((l, hp * HEAD), lambda g, i: (0, c0 // hp + g))
    any_spec = pl.BlockSpec(memory_space=pl.ANY)
    return pl.pallas_call(
        body, name="sb_bwd", grid=(nh // hp, nq),
        in_specs=[blk(qc), head(kc), head(vc), blk(zc), blk(nh), blk(0),
                  pl.BlockSpec((hp, tq, 1), lambda g, i: (g, i, 0)), any_spec],
        out_specs=any_spec, out_shape=jax.ShapeDtypeStruct(dproj.shape, bf16), input_output_aliases={7: 0},
        scratch_shapes=[pltpu.VMEM((l, hp * HEAD), f32), pltpu.VMEM((l, hp * HEAD), f32),
                        pltpu.VMEM((tq, hp * HEAD), bf16), pltpu.VMEM((2, tq, hp * HEAD), bf16),
                        pltpu.VMEM((2, l, hp * HEAD), bf16), pltpu.SemaphoreType.DMA((4,))],
        compiler_params=_cparams("parallel", "arbitrary"),
    )(proj, proj, proj, proj, dcat, att, tot, dproj)


def _disc(lr, li, ldt):
    dt = jnp.exp(ldt)
    mag = jnp.exp(lr * dt)
    a_re = mag * jnp.cos(li * dt)
    a_im = mag * jnp.sin(li * dt)
    den = lr * lr + li * li
    nr = a_re - 1.0
    return a_re, a_im, (nr * lr + a_im * li) / den, (a_im * lr - nr * li) / den


def s5_params_fwd(lr, li, ldt, bt_re, bt_im):
    g, c, p = bt_re.shape

    def body(lr_ref, li_ref, ldt_ref, br_ref, bi_ref, ar_ref, ai_ref, bbr_ref, bbi_ref):
        a_re, a_im, cr, ci = _disc(lr_ref[...], li_ref[...], ldt_ref[...])
        ar_ref[...] = a_re
        ai_ref[...] = a_im
        for k in range(c):
            br, bi = br_ref[:, k, :], bi_ref[:, k, :]
            bbr_ref[:, k, :] = cr * br - ci * bi
            bbi_ref[:, k, :] = cr * bi + ci * br

    return pl.pallas_call(
        body, name="s5_params_fwd",
        out_shape=[jax.ShapeDtypeStruct((g, p), f32)] * 2 + [jax.ShapeDtypeStruct((g, c, p), f32)] * 2,
    )(lr, li, ldt, bt_re, bt_im)


def s5_params_bwd(lr, li, ldt, bt_re, bt_im, da_re, da_im, dbbt_re, dbbt_im):
    g, c, p = bt_re.shape

    def body(lr_ref, li_ref, ldt_ref, br_ref, bi_ref, dar_ref, dai_ref, dbbr_ref, dbbi_ref,
             dlr_ref, dli_ref, dldt_ref, dbr_ref, dbi_ref):
        (a_re, a_im, cr, ci), vjp = jax.vjp(_disc, lr_ref[...], li_ref[...], ldt_ref[...])
        dcr = jnp.zeros((g, p), f32)
        dci = jnp.zeros((g, p), f32)
        for k in range(c):
            br, bi = br_ref[:, k, :], bi_ref[:, k, :]
            dr, di = dbbr_ref[:, k, :], dbbi_ref[:, k, :]
            dcr += dr * br + di * bi
            dci += di * br - dr * bi
            dbr_ref[:, k, :] = cr * dr + ci * di
            dbi_ref[:, k, :] = cr * di - ci * dr
        dlr, dli, dldt = vjp((dar_ref[...], dai_ref[...], dcr, dci))
        dlr_ref[...] = dlr
        dli_ref[...] = dli
        dldt_ref[...] = dldt

    return pl.pallas_call(
        body, name="s5_params_bwd",
        out_shape=[jax.ShapeDtypeStruct((g, p), f32)] * 2 + [jax.ShapeDtypeStruct((g, 1), f32)]
        + [jax.ShapeDtypeStruct((g, c, p), f32)] * 2,
    )(lr, li, ldt, bt_re, bt_im, da_re, da_im, dbbt_re, dbbt_im)


def _cmul(ar, ai, br, bi):
    return ar * br - ai * bi, ar * bi + ai * br


def _power_tables(ar, ai):
    rows = lax.broadcasted_iota(jnp.int32, (SUBLANES, ar.shape[1]), 0)
    pr = jnp.zeros((SUBLANES, ar.shape[1]), f32)
    pi = jnp.zeros((SUBLANES, ar.shape[1]), f32)
    cr, ci = ar, ai
    pows = {}
    for r in range(SUBLANES):
        pows[r + 1] = (cr, ci)
        pr = jnp.where(rows == r, cr, pr)
        pi = jnp.where(rows == r, ci, pi)
        cr, ci = _cmul(cr, ci, ar, ai)
    return [pows[1], pows[2], pows[4]], pr, pi


def _ssm_time_tile(l):
    return _tile(l, (2048, 1024, 512, 256, 128))


def ssm_fwd(u, bre3, bim3, cre3, cimn3, a_re, a_im, d_skip):
    l, w = u.shape[0], d_skip.shape[1]
    nj = w // HEAD
    ns = STATES_PER_LANE_BLOCK
    tt = _ssm_time_tile(l)

    def body(u_ref, bre_ref, bim_ref, cre_ref, cim_ref, ar_ref, ai_ref, d_ref, y_ref, hr_ref, hi_ref, cr_ref, ci_ref):
        @pl.when(pl.program_id(1) == 0)
        def _():
            cr_ref[...] = jnp.zeros_like(cr_ref)
            ci_ref[...] = jnp.zeros_like(ci_ref)

        uv = u_ref[...]
        hr_ref[...] = _dot(uv, bre_ref[...])
        hi_ref[...] = _dot(uv, bim_ref[...])
        steps, pr, pi = _power_tables(ar_ref[...], ai_ref[...])
        rows = lax.broadcasted_iota(jnp.int32, (SUBLANES, ns), 0)
        steps = [(jnp.where(rows >= d, sr_, 0.0), jnp.where(rows >= d, si_, 0.0)) for d, (sr_, si_) in zip((1, 2, 4), steps)]

        def blk(b, carry):
            cr, ci = carry
            sl = pl.ds(pl.multiple_of(b * SUBLANES, SUBLANES), SUBLANES)
            xr, xi = hr_ref[sl, :], hi_ref[sl, :]
            for d, (sr_, si_) in zip((1, 2, 4), steps):
                mr, mi = _cmul(sr_, si_, pltpu.roll(xr, d, axis=0), pltpu.roll(xi, d, axis=0))
                xr, xi = xr + mr, xi + mi
            mr, mi = _cmul(pr, pi, cr, ci)
            xr, xi = xr + mr, xi + mi
            hr_ref[sl, :] = xr
            hi_ref[sl, :] = xi
            return xr[SUBLANES - 1:, :], xi[SUBLANES - 1:, :]

        cr, ci = lax.fori_loop(0, tt // SUBLANES, blk, (cr_ref[...], ci_ref[...]))
        cr_ref[...] = cr
        ci_ref[...] = ci
        y = _dot(hr_ref[...].astype(bf16), cre_ref[...]) + _dot(hi_ref[...].astype(bf16), cim_ref[...])
        y_ref[...] = y + d_ref[...] * uv.astype(f32)

    lane = pl.BlockSpec((tt, HEAD), lambda j, i: (i, j))
    st = pl.BlockSpec((tt, ns), lambda j, i: (i, j))
    b3 = pl.BlockSpec((None, HEAD, ns), lambda j, i: (j, 0, 0))
    c3 = pl.BlockSpec((None, ns, HEAD), lambda j, i: (j, 0, 0))
    arow = pl.BlockSpec((1, ns), lambda j, i: (0, j))
    return pl.pallas_call(
        body, name="ssm_fwd", grid=(nj, l // tt),
        in_specs=[lane, b3, b3, c3, c3, arow, arow, pl.BlockSpec((1, HEAD), lambda j, i: (0, j))],
        out_specs=[lane, st, st],
        out_shape=[jax.ShapeDtypeStruct((l, w), f32), jax.ShapeDtypeStruct((l, nj * ns), f32),
                   jax.ShapeDtypeStruct((l, nj * ns), f32)],
        scratch_shapes=[pltpu.VMEM((1, ns), f32), pltpu.VMEM((1, ns), f32)],
        compiler_params=_cparams("parallel", "arbitrary"),
    )(u, bre3, bim3, cre3, cimn3, a_re, a_im, d_skip)


def ssm_bwd(dy, u, dproj, h_re, h_im, bre3, bim3, cre3, cimn3, a_re, a_im, d_skip):
    l, w = u.shape[0], d_skip.shape[1]
    nj = w // HEAD
    ns = STATES_PER_LANE_BLOCK
    tt = _ssm_time_tile(l)
    nt = l // tt

    def body(dy_ref, u_ref, dproj_ref, hr_ref, hi_ref, bre_ref, bim_ref, cre_ref, cim_ref, ar_ref, ai_ref, d_ref,
             du_ref, dd_ref, dar_ref, dai_ref, dbre_ref, dbim_ref, dcre_ref, dcim_ref, kr_ref, ki_ref, cr_ref, ci_ref,
             accr_ref, acci_ref):
        del dproj_ref
        i = pl.program_id(1)

        @pl.when(i == 0)
        def _():
            for ref in (cr_ref, ci_ref, accr_ref, acci_ref, dd_ref, dbre_ref, dbim_ref, dcre_ref, dcim_ref):
                ref[...] = jnp.zeros_like(ref)

        dyv = dy_ref[...]
        dyb = dyv.astype(bf16)
        uv = u_ref[...]
        kr_ref[...] = _dot_nt(dyb, cre_ref[...])
        ki_ref[...] = _dot_nt(dyb, cim_ref[...])
        steps, pr, pi = _power_tables(ar_ref[...], -ai_ref[...])
        rows = lax.broadcasted_iota(jnp.int32, (SUBLANES, ns), 0)
        qr = jnp.zeros((SUBLANES, ns), f32)
        qi = jnp.zeros((SUBLANES, ns), f32)
        for r in range(SUBLANES):
            qr = jnp.where(rows == r, pr[SUBLANES - 1 - r:SUBLANES - r, :], qr)
            qi = jnp.where(rows == r, pi[SUBLANES - 1 - r:SUBLANES - r, :], qi)
        nb = tt // SUBLANES
        steps = [(jnp.where(rows < SUBLANES - d, sr_, 0.0), jnp.where(rows < SUBLANES - d, si_, 0.0))
                 for d, (sr_, si_) in zip((1, 2, 4), steps)]

        def blk(t, carry):
            cr, ci, accr, acci = carry
            sl = pl.ds(pl.multiple_of((nb - 1 - t) * SUBLANES, SUBLANES), SUBLANES)
            xr, xi = kr_ref[sl, :], ki_ref[sl, :]
            for d, (sr_, si_) in zip((1, 2, 4), steps):
                mr, mi = _cmul(sr_, si_, pltpu.roll(xr, SUBLANES - d, axis=0), pltpu.roll(xi, SUBLANES - d, axis=0))
                xr, xi = xr + mr, xi + mi
            mr, mi = _cmul(qr, qi, cr, ci)
            xr, xi = xr + mr, xi + mi
            kr_ref[sl, :] = xr
            ki_ref[sl, :] = xi
            last = rows == SUBLANES - 1
            nr = jnp.where(last, cr, pltpu.roll(xr, SUBLANES - 1, axis=0))
            ni = jnp.where(last, ci, pltpu.roll(xi, SUBLANES - 1, axis=0))
            hr, hi = hr_ref[sl, :], hi_ref[sl, :]
            accr = accr + nr * hr + ni * hi
            acci = acci + ni * hr - nr * hi
            return xr[:1, :], xi[:1, :], accr, acci

        cr, ci, accr, acci = lax.fori_loop(0, nb, blk, (cr_ref[...], ci_ref[...], accr_ref[...], acci_ref[...]))
        cr_ref[...] = cr
        ci_ref[...] = ci
        accr_ref[...] = accr
        acci_ref[...] = acci
        kr, ki = kr_ref[...].astype(bf16), ki_ref[...].astype(bf16)
        du = _dot_nt(kr, bre_ref[...]) + _dot_nt(ki, bim_ref[...]) + d_ref[...] * dyv
        du_ref[...] = du.astype(du_ref.dtype)
        dd_ref[...] += jnp.sum(dyv * uv.astype(f32), axis=0, keepdims=True)
        dbre_ref[...] += _dot_tn(uv, kr)
        dbim_ref[...] += _dot_tn(uv, ki)
        dcre_ref[...] += _dot_tn(hr_ref[...].astype(bf16), dyb)
        dcim_ref[...] += _dot_tn(hi_ref[...].astype(bf16), dyb)

        @pl.when(i == nt - 1)
        def _():
            dar_ref[...] = jnp.sum(accr_ref[...], axis=0, keepdims=True)
            dai_ref[...] = jnp.sum(acci_ref[...], axis=0, keepdims=True)

    lane = pl.BlockSpec((tt, HEAD), lambda j, i: (nt - 1 - i, j))
    st = pl.BlockSpec((tt, ns), lambda j, i: (nt - 1 - i, j))
    b3 = pl.BlockSpec((None, HEAD, ns), lambda j, i: (j, 0, 0))
    c3 = pl.BlockSpec((None, ns, HEAD), lambda j, i: (j, 0, 0))
    arow = pl.BlockSpec((1, ns), lambda j, i: (0, j))
    drow = pl.BlockSpec((1, HEAD), lambda j, i: (0, j))
    return pl.pallas_call(
        body, name="ssm_bwd", grid=(nj, nt),
        in_specs=[lane, lane, pl.BlockSpec(memory_space=pl.ANY), st, st, b3, b3, c3, c3, arow, arow, drow],
        out_specs=[lane, drow, arow, arow, b3, b3, c3, c3], input_output_aliases={2: 0},
        out_shape=[jax.ShapeDtypeStruct(dproj.shape, bf16), jax.ShapeDtypeStruct((1, w), f32),
                   jax.ShapeDtypeStruct((1, nj * ns), f32), jax.ShapeDtypeStruct((1, nj * ns), f32),
                   jax.ShapeDtypeStruct((nj, HEAD, ns), f32), jax.ShapeDtypeStruct((nj, HEAD, ns), f32),
                   jax.ShapeDtypeStruct((nj, ns, HEAD), f32), jax.ShapeDtypeStruct((nj, ns, HEAD), f32)],
        scratch_shapes=[pltpu.VMEM((tt, ns), f32), pltpu.VMEM((tt, ns), f32), pltpu.VMEM((1, ns), f32),
                        pltpu.VMEM((1, ns), f32), pltpu.VMEM((SUBLANES, ns), f32), pltpu.VMEM((SUBLANES, ns), f32)],
        compiler_params=_cparams("parallel", "arbitrary"),
    )(dy, u, dproj, h_re, h_im, bre3, bim3, cre3, cimn3, a_re, a_im, d_skip)


def glu_fwd(y, z_src, w_glu, b_glu):
    l, w = y.shape
    tm = _row_tile(l)

    def body(y_ref, z_ref, w_ref, b_ref, g_ref, t_ref, o_ref):
        g = _gelu(y_ref[...])
        gb = g.astype(bf16)
        t = _dot(gb, w_ref[...]) + b_ref[...]
        g_ref[...] = gb
        t_ref[...] = t
        o_ref[...] = (g * jax.nn.sigmoid(t) * _silu(z_ref[...].astype(f32))).astype(o_ref.dtype)

    blk = pl.BlockSpec((tm, w), lambda i: (i, 0))
    return pl.pallas_call(
        body, name="glu_fwd", grid=(l // tm,),
        in_specs=[blk, pl.BlockSpec((tm, w), lambda i: (i, 1)), pl.BlockSpec((w, w), lambda i: (0, 0)), _row(w)],
        out_specs=[blk, blk, blk],
        out_shape=[jax.ShapeDtypeStruct((l, w), bf16), jax.ShapeDtypeStruct((l, w), f32),
                   jax.ShapeDtypeStruct((l, w), bf16)],
        compiler_params=_cparams("parallel"),
    )(y, z_src, w_glu, b_glu)


def glu_bwd(dout, y, t, z_src, w_glu):
    l, w = y.shape
    tm = _row_tile(l)

    def body(do_ref, y_ref, t_ref, z_ref, w_ref, dy_ref, dz_ref, dt_ref, db_ref):
        @pl.when(pl.program_id(0) == 0)
        def _():
            db_ref[...] = jnp.zeros_like(db_ref)

        yv, zv, dov = y_ref[...], z_ref[...].astype(f32), do_ref[...]
        g = _gelu(yv)
        sg = jax.nn.sigmoid(t_ref[...])
        dy2 = dov * _silu(zv)
        dz_ref[...] = (dov * g * sg * _silu_grad(zv)).astype(dz_ref.dtype)
        dt = dy2 * g * sg * (1.0 - sg)
        dtb = dt.astype(bf16)
        dt_ref[...] = dtb
        db_ref[...] += jnp.sum(dt, axis=0, keepdims=True)
        dg = dy2 * sg + _dot_nt(dtb, w_ref[...])
        dy_ref[...] = dg * _gelu_grad(yv)

    blk = pl.BlockSpec((tm, w), lambda i: (i, 0))
    return pl.pallas_call(
        body, name="glu_bwd", grid=(l // tm,),
        in_specs=[blk, blk, blk, pl.BlockSpec((tm, w), lambda i: (i, 1)), pl.BlockSpec((w, w), lambda i: (0, 0))],
        out_specs=[blk, pl.BlockSpec((tm, w), lambda i: (i, 1)), blk, _row(w)],
        out_shape=[jax.ShapeDtypeStruct((l, w), f32), jax.ShapeDtypeStruct((l, 2 * w), bf16),
                   jax.ShapeDtypeStruct((l, w), bf16), jax.ShapeDtypeStruct((1, w), f32)],
        compiler_params=_cparams("arbitrary"),
    )(dout, y, t, z_src, w_glu)


def _adamw(w, g, m, v):
    m = ADAM_B1 * m + (1.0 - ADAM_B1) * g
    v = ADAM_B2 * v + (1.0 - ADAM_B2) * (g * g)
    m_hat = m / (1.0 - ADAM_B1 ** ADAM_STEP)
    v_hat = v / (1.0 - ADAM_B2 ** ADAM_STEP)
    return -ADAM_LR * (m_hat / (jnp.sqrt(v_hat) + ADAM_EPS) + ADAM_WD * w), m, v


def adam_reduce(pieces, w, m, v, name):
    r, c = w.shape
    n = pieces.shape[0]
    tr = _tile(r, (256, 128, 64, 32, 16, 8))

    def body(p_ref, w_ref, m_ref, v_ref, g_ref, d_ref, nm_ref, nv_ref):
        g = p_ref[0].astype(f32)
        for s in range(1, n):
            g = g + p_ref[s].astype(f32)
        g_ref[...] = g
        d_ref[...], nm_ref[...], nv_ref[...] = _adamw(w_ref[...], g, m_ref[...], v_ref[...])

    blk = pl.BlockSpec((tr, c), lambda i: (i, 0))
    return pl.pallas_call(
        body, name=name, grid=(r // tr,),
        in_specs=[pl.BlockSpec((n, tr, c), lambda i: (0, i, 0)), blk, blk, blk],
        out_specs=[blk] * 4, out_shape=[jax.ShapeDtypeStruct((r, c), f32)] * 4,
        compiler_params=_cparams("parallel"),
    )(pieces, w, m, v)


def adam_w_mod(cond_t, dm, w, m, v):
    nl, d, cols = w.shape
    tr = _tile(d, (512, 256, 128))

    def body(c_ref, dm_ref, w_ref, m_ref, v_ref, g_ref, d_ref, nm_ref, nv_ref):
        g = jnp.dot(c_ref[...], dm_ref[...], preferred_element_type=f32, precision=lax.Precision.HIGHEST)
        g_ref[...] = g
        d_ref[...], nm_ref[...], nv_ref[...] = _adamw(w_ref[...], g, m_ref[...], v_ref[...])

    blk = pl.BlockSpec((None, tr, cols), lambda l, i: (l, i, 0))
    return pl.pallas_call(
        body, name="adam_w_mod", grid=(nl, d // tr),
        in_specs=[pl.BlockSpec((tr, N_DEV), lambda l, i: (i, 0)), pl.BlockSpec((None, N_DEV, cols), lambda l, i: (l, 0, 0)),
                  blk, blk, blk],
        out_specs=[blk] * 4, out_shape=[jax.ShapeDtypeStruct((nl, d, cols), f32)] * 4,
        compiler_params=_cparams("parallel", "parallel"),
    )(cond_t, dm, w, m, v)


def silu_rows(c_all):
    def body(c_ref, o_ref):
        o_ref[...] = _silu(c_ref[...])

    return pl.pallas_call(body, name="silu_rows", out_shape=jax.ShapeDtypeStruct(c_all.shape, f32))(c_all)


def _block_diag(x):
    g, a, b = x.shape
    nj = g // GROUPS_PER_LANE_BLOCK
    eye = jnp.eye(GROUPS_PER_LANE_BLOCK, dtype=x.dtype)
    x5 = x.reshape(nj, GROUPS_PER_LANE_BLOCK, a, b)
    return jnp.einsum("jgab,gh->jgahb", x5, eye).reshape(nj, GROUPS_PER_LANE_BLOCK * a, GROUPS_PER_LANE_BLOCK * b)


def _diag_blocks(x, a, b):
    nj = x.shape[0]
    x5 = x.reshape(nj, GROUPS_PER_LANE_BLOCK, a, GROUPS_PER_LANE_BLOCK, b)
    eye = jnp.eye(GROUPS_PER_LANE_BLOCK, dtype=x.dtype)
    return jnp.einsum("jgahb,gh->jgab", x5, eye).reshape(nj * GROUPS_PER_LANE_BLOCK, a, b)


PACK_ROW = SUBLANES * HEAD


def _pack(parts, row_multiple=SUBLANES):
    rows = []
    for p in parts:
        flat = p.reshape(-1)
        pad = (-flat.shape[0]) % PACK_ROW
        if pad:
            flat = jnp.concatenate([flat, jnp.zeros((pad,), flat.dtype)])
        rows.append(flat.reshape(-1, HEAD))
    pad = (-sum(r.shape[0] for r in rows)) % row_multiple
    if pad:
        rows.append(jnp.zeros((pad, HEAD), rows[0].dtype))
    return jnp.concatenate(rows, axis=0)


def _unpack(packed, shapes):
    out, r0 = [], 0
    for shp in shapes:
        n = math.prod(shp)
        nr = -(-n // PACK_ROW) * SUBLANES
        out.append(packed[r0:r0 + nr].reshape(-1)[:n].reshape(shp))
        r0 += nr
    return out


def adam_small(g, w, m, v):
    r, c = w.shape

    def body(g_ref, w_ref, m_ref, v_ref, d_ref, nm_ref, nv_ref):
        d_ref[...], nm_ref[...], nv_ref[...] = _adamw(w_ref[...], g_ref[...], m_ref[...], v_ref[...])

    tr = max(t for t in range(SUBLANES, 1024 + 1, SUBLANES) if r % t == 0)
    blk = pl.BlockSpec((tr, c), lambda i: (i, 0))
    return pl.pallas_call(
        body, name="adam_small", grid=(r // tr,),
        in_specs=[blk] * 4, out_specs=[blk] * 3, out_shape=[jax.ShapeDtypeStruct((r, c), f32)] * 3,
        compiler_params=_cparams("parallel"),
    )(g, w, m, v)


def kernel(x, c, ln_pre_g, ln_post_g, w_mod, b_mod, w_in_ab, w_out_ab, sgu_norm_g, sgu_w, sgu_b, w_in_ssm, w_out_ssm, lam_re, lam_im, b_re, b_im, c_re, c_im, d_skip, log_dt, w_glu, b_glu, loss_target, m_ln_pre_g, m_ln_post_g, m_w_mod, m_b_mod, m_w_in_ab, m_w_out_ab, m_sgu_norm_g, m_sgu_w, m_sgu_b, m_w_in_ssm, m_w_out_ssm, m_lam_re, m_lam_im, m_b_re, m_b_im, m_c_re, m_c_im, m_d_skip, m_log_dt, m_w_glu, m_b_glu, v_ln_pre_g, v_ln_post_g, v_w_mod, v_b_mod, v_w_in_ab, v_w_out_ab, v_sgu_norm_g, v_sgu_w, v_sgu_b, v_w_in_ssm, v_w_out_ssm, v_lam_re, v_lam_im, v_b_re, v_b_im, v_c_re, v_c_im, v_d_skip, v_log_dt, v_w_glu, v_b_glu):
    me = _my_index()
    x0 = x[0]
    l, d = x0.shape
    target = loss_target[0]
    nh = sgu_w.shape[1]
    wa = nh * HEAD
    n_grp, n_st = lam_re.shape[1], lam_re.shape[2]
    mod_cols = w_mod.shape[2]

    def after(a, first):
        return a + jnp.minimum(jnp.abs(first[(0,) * first.ndim].astype(f32)), 0.0).astype(a.dtype)

    c_all, d_skip_all, b_glu_all = all_gather([c, d_skip, b_glu], "gather_c")
    c_all = c_all.reshape(N_DEV, d)
    d_skip_all = d_skip_all.reshape(1, -1)
    b_glu_all = b_glu_all.reshape(1, -1)

    b_cols = lax.dynamic_slice_in_dim(b_mod, me * mod_cols, mod_cols, axis=1)
    (mod_all,) = all_gather([mod_part(c_all, w_mod, b_cols)], "gather_mod")
    (win_ab3,) = sequencer_exchange(GATHER, [after(w_in_ab[0], mod_all).astype(bf16)], "gather_w_in", 1)
    mod_mine = lax.dynamic_index_in_dim(mod_all, me, axis=2, keepdims=False)
    mod_rows = jnp.transpose(mod_mine, (1, 0, 2)).reshape(2, 3, 1, d)

    def rows(a, i):
        return a[i].reshape(1, d)

    shift0, scale0, gate0 = mod_rows[0, 0], mod_rows[0, 1], mod_rows[0, 2]
    h0, h0_t = prenorm_fwd(x0, rows(ln_pre_g, 0), shift0, scale0, "prenorm0")
    wout_ab3, win_ssm3, wout_ssm3, wglu = sequencer_exchange(
        GATHER, [after(w, win_ab3).astype(bf16) for w in (w_out_ab[0], w_in_ssm[0], w_out_ssm[0], w_glu[0])],
        "gather_w_rest", 2)
    proj0 = mm_nn(h0, win_ab3, bf16, "proj0")
    sgu_b3 = sgu_b[0].reshape(nh, HEAD, 1)
    cat, att, tot = sb_fwd(proj0, sgu_fwd(proj0, sgu_norm_g, sgu_w[0], sgu_b3), nh)
    wout_ab3 = wout_ab3.reshape(1, d, d)
    win_ssm3 = win_ssm3.reshape(1, d, d)
    wglu = wglu.reshape(w_glu.shape[2], w_glu.shape[2])
    y0 = mm_nn(cat, wout_ab3, f32, "out0")

    shift1, scale1, gate1 = mod_rows[1, 0], mod_rows[1, 1], mod_rows[1, 2]
    x1, h1, h1_t = post_prenorm_fwd(x0, y0, gate0, rows(ln_post_g, 0), rows(ln_pre_g, 1), shift1, scale1,
                                    "post0_prenorm1")
    proj1 = mm_nn(h1, win_ssm3, bf16, "proj1")
    w_ssm = proj1.shape[1] // 2
    ldt = log_dt[0].reshape(n_grp, 1)
    bt_re = jnp.transpose(b_re[0], (0, 2, 1))
    bt_im = jnp.transpose(b_im[0], (0, 2, 1))
    a_re, a_im, bbt_re, bbt_im = s5_params_fwd(lam_re[0], lam_im[0], ldt, bt_re, bt_im)
    bre3 = _block_diag(bbt_re).astype(bf16)
    bim3 = _block_diag(bbt_im).astype(bf16)
    cre3 = _block_diag(jnp.transpose(c_re[0], (0, 2, 1))).astype(bf16)
    cimn3 = _block_diag(-jnp.transpose(c_im[0], (0, 2, 1))).astype(bf16)
    a_re_row, a_im_row = a_re.reshape(1, -1), a_im.reshape(1, -1)
    y_ssm, hs_re, hs_im = ssm_fwd(proj1, bre3, bim3, cre3, cimn3, a_re_row, a_im_row, d_skip_all)
    g_act, t_glu, mix1 = glu_fwd(y_ssm, proj1, wglu, b_glu_all)
    y1 = mm_nn(mix1, wout_ssm3, f32, "out1")

    dx2, loss_tile, dy1, dgate1, dgpost1 = final_loss(x1, y1, gate1, rows(ln_post_g, 1), target)

    dmix1 = mm_nt(dy1, wout_ssm3, f32, "dmix1")
    gw_out_ssm = mm_tn(mix1, dy1, N_DEV, bf16, "gw_out_ssm")
    (p_out_ssm,) = sequencer_exchange(SCATTER, [gw_out_ssm], "scatter_g1", 3)
    dy_ssm, dproj1, dt_glu, db_glu = glu_bwd(dmix1, y_ssm, t_glu, proj1, wglu)
    gw_glu = mm_tn(g_act, dt_glu, 1, bf16, "gw_glu").reshape(N_DEV, -1, w_ssm)
    dproj1, dd_skip, da_re, da_im, dbre3, dbim3, dcre3, dcimn3 = ssm_bwd(
        dy_ssm, proj1, dproj1, hs_re, hs_im, bre3, bim3, cre3, cimn3, a_re_row, a_im_row, d_skip_all)
    gw_in_ssm = mm_nn(h1_t, dproj1[None], bf16, "gw_in_ssm").reshape(N_DEV, -1, proj1.shape[1])
    p_in_ssm, p_glu = sequencer_exchange(SCATTER, [gw_in_ssm, gw_glu], "scatter_g2", 4)
    dh1 = mm_nt(dproj1, win_ssm3, f32, "dh1")
    dx1, dshift1, dscale1, dgpre1, dy0, dgate0, dgpost0 = prenorm_post_bwd(
        dh1, x1, dx2, rows(ln_pre_g, 1), scale1, y0, gate0, rows(ln_post_g, 0), "prenorm1_post0_bwd")
    dlr, dli, dldt, dbt_re, dbt_im = s5_params_bwd(
        lam_re[0], lam_im[0], ldt, bt_re, bt_im, da_re.reshape(n_grp, n_st), da_im.reshape(n_grp, n_st),
        _diag_blocks(dbre3, SSM_GROUP, n_st), _diag_blocks(dbim3, SSM_GROUP, n_st))
    g_b_re = jnp.transpose(dbt_re, (0, 2, 1))
    g_b_im = jnp.transpose(dbt_im, (0, 2, 1))
    g_c_re = jnp.transpose(_diag_blocks(dcre3, n_st, SSM_GROUP), (0, 2, 1))
    g_c_im = -jnp.transpose(_diag_blocks(dcimn3, n_st, SSM_GROUP), (0, 2, 1))

    dcat = mm_nt(dy0, wout_ab3, f32, "dcat")
    gw_out_ab = mm_tn(cat, dy0, 1, bf16, "gw_out_ab").reshape(N_DEV, -1, d)
    (p_out_ab,) = sequencer_exchange(SCATTER, [gw_out_ab], "scatter_g3", 5)
    dproj0, dsgu_w, dsgu_b, dsgu_ng = sgu_bwd(proj0, dcat, sgu_norm_g, sgu_w[0], sgu_b3)
    dproj0 = sb_bwd(proj0, dcat, att, tot, dproj0, nh)
    gw_in_ab = mm_nn(h0_t, dproj0[None], bf16, "gw_in_ab", split_cols=N_DEV)
    (p_in_ab,) = sequencer_exchange(SCATTER, [gw_in_ab], "scatter_g4", 6)
    dh0 = mm_nt(dproj0, win_ab3, f32, "dh0")
    dx0, dshift0, dscale0, dgpre0 = prenorm_bwd(dh0, x0, dx1, rows(ln_pre_g, 0), scale0, "prenorm0_bwd")

    small_names = ["ln_pre_g", "ln_post_g", "b_mod", "sgu_norm_g", "sgu_w", "sgu_b", "lam_re", "lam_im", "b_re", "b_im",
                   "c_re", "c_im", "log_dt"]
    small_w = [ln_pre_g, ln_post_g, b_mod, sgu_norm_g, sgu_w, sgu_b, lam_re, lam_im, b_re, b_im, c_re, c_im, log_dt]
    small_m = [m_ln_pre_g, m_ln_post_g, m_b_mod, m_sgu_norm_g, m_sgu_w, m_sgu_b, m_lam_re, m_lam_im, m_b_re, m_b_im,
               m_c_re, m_c_im, m_log_dt]
    small_v = [v_ln_pre_g, v_ln_post_g, v_b_mod, v_sgu_norm_g, v_sgu_w, v_sgu_b, v_lam_re, v_lam_im, v_b_re, v_b_im,
               v_c_re, v_c_im, v_log_dt]
    def sharded(p, w, m, v, name):
        shp = w.shape
        w2, m2, v2 = (a.reshape(-1, shp[-1]) for a in (w, m, v))
        return [o.reshape(shp) for o in adam_reduce(p.reshape(p.shape[0], -1, shp[-1]), w2, m2, v2, name)]

    r_w_out_ssm = sharded(p_out_ssm, w_out_ssm, m_w_out_ssm, v_w_out_ssm, "adam_w_out_ssm")
    r_w_in_ssm = sharded(p_in_ssm, w_in_ssm, m_w_in_ssm, v_w_in_ssm, "adam_w_in_ssm")
    r_w_glu = sharded(p_glu, w_glu, m_w_glu, v_w_glu, "adam_w_glu")
    r_w_out_ab = sharded(p_out_ab, w_out_ab, m_w_out_ab, v_w_out_ab, "adam_w_out_ab")
    dmod = jnp.concatenate([dshift0, dscale0, dgate0, dshift1, dscale1, dgate1], axis=1)
    for done in (r_w_out_ssm, r_w_in_ssm, r_w_glu, r_w_out_ab):
        dmod = after(dmod, done[0])
    small_g = [jnp.concatenate([dgpre0, dgpre1]), jnp.concatenate([dgpost0, dgpost1]), dmod, dsgu_ng, dsgu_w, dsgu_b,
               dlr, dli, g_b_re, g_b_im, g_c_re, g_c_im, dldt]
    shapes = [w.shape for w in small_w]
    g_sum, dmod_all = all_reduce_rows(_pack(small_g + [dd_skip, db_glu, loss_tile], SUBLANES * N_DEV), dmod,
                                      "reduce_small_grads")
    n_rows_small = sum(-(-math.prod(s) // PACK_ROW) * SUBLANES for s in shapes)
    loss = g_sum[n_rows_small + 2 * (d_skip_all.shape[1] // HEAD), 0] * (0.5 / d)
    new_small = adam_small(g_sum, _pack(small_w), _pack(small_m), _pack(small_v))
    r_small = [_unpack(o, shapes) for o in [g_sum[:n_rows_small]] + list(new_small)]
    small = {n: [r_small[k][i] for k in range(4)] for i, n in enumerate(small_names)}
    vec_rows = d_skip_all.shape[1] // HEAD

    def my_columns(r0):
        whole = g_sum[r0:r0 + vec_rows].reshape(1, 1, -1)
        return lax.dynamic_slice_in_dim(whole, me * d_skip.shape[1], d_skip.shape[1], axis=2)

    r_d_skip = sharded(my_columns(n_rows_small), d_skip, m_d_skip, v_d_skip, "adam_d_skip")
    r_b_glu = sharded(my_columns(n_rows_small + vec_rows), b_glu, m_b_glu, v_b_glu, "adam_b_glu")
    r_w_in_ab = sharded(p_in_ab, w_in_ab, m_w_in_ab, v_w_in_ab, "adam_w_in_ab")

    dm_cols = jnp.transpose(
        lax.dynamic_slice_in_dim(dmod_all.reshape(N_DEV, 2, 3 * d), me * mod_cols, mod_cols, axis=2), (1, 0, 2))
    cond_t = jnp.transpose(silu_rows(c_all))
    r_w_mod = adam_w_mod(cond_t, dm_cols, w_mod, m_w_mod, v_w_mod)

    res = dict(small)
    res.update(w_mod=r_w_mod, w_in_ab=r_w_in_ab, w_out_ab=r_w_out_ab, w_in_ssm=r_w_in_ssm, w_out_ssm=r_w_out_ssm,
               d_skip=r_d_skip, w_glu=r_w_glu, b_glu=r_b_glu)
    order = ["ln_pre_g", "ln_post_g", "w_mod", "b_mod", "w_in_ab", "w_out_ab", "sgu_norm_g", "sgu_w", "sgu_b", "w_in_ssm",
             "w_out_ssm", "lam_re", "lam_im", "b_re", "b_im", "c_re", "c_im", "d_skip", "log_dt", "w_glu", "b_glu"]
    outs = [loss, dx0.reshape(x.shape)]
    for k in range(4):
        outs += [res[n][k] for n in order]
    return tuple(outs)
```

```python
import functools
import math

import jax
import jax.numpy as jnp
from jax import lax
from jax.experimental import pallas as pl
from jax.experimental.pallas import tpu as pltpu
from jax.experimental.pallas import tpu_sc as plsc

f32 = jnp.float32
bf16 = jnp.bfloat16

N_DEV = 8
EPS = 1e-6
HEAD = 128
SUBLANES = 8
SSM_GROUP = 16
SSM_STATE = 64
GROUPS_PER_LANE_BLOCK = HEAD // SSM_GROUP
STATES_PER_LANE_BLOCK = GROUPS_PER_LANE_BLOCK * SSM_STATE
VMEM_LIMIT = 56 * 2 ** 20
ADAM_LR, ADAM_B1, ADAM_B2, ADAM_EPS, ADAM_WD, ADAM_STEP = 0.001, 0.9, 0.999, 1e-08, 0.01, 10
_GELU_C0 = math.sqrt(2.0 / math.pi)
_GELU_C1 = 0.044715
MESH = pl.DeviceIdType.MESH


def _cparams(*sem):
    return pltpu.CompilerParams(dimension_semantics=sem if sem else None, vmem_limit_bytes=VMEM_LIMIT)


def _gelu(x):
    return 0.5 * x * (1.0 + jnp.tanh(_GELU_C0 * (x + _GELU_C1 * x * x * x)))


def _gelu_grad(x):
    t = jnp.tanh(_GELU_C0 * (x + _GELU_C1 * x * x * x))
    return 0.5 * (1.0 + t) + 0.5 * x * (1.0 - t * t) * _GELU_C0 * (1.0 + 3.0 * _GELU_C1 * x * x)


def _silu(x):
    return x * jax.nn.sigmoid(x)


def _silu_grad(x):
    s = jax.nn.sigmoid(x)
    return s * (1.0 + x * (1.0 - s))


def _dot(a, b):
    return jnp.dot(a, b, preferred_element_type=f32)


def _dot_nt(a, b):
    return lax.dot_general(a, b, (((1,), (1,)), ((), ())), preferred_element_type=f32)


def _dot_tn(a, b):
    return lax.dot_general(a, b, (((0,), (0,)), ((), ())), preferred_element_type=f32)


def _split_bf16(v):
    hi = v.astype(bf16)
    lo = (v - hi.astype(f32)).astype(bf16)
    return hi, lo


def _row(d):
    return pl.BlockSpec((1, d), lambda *_: (0, 0))


def _my_index():
    return 4 * lax.axis_index("x") + 2 * lax.axis_index("y") + lax.axis_index("c")


def _peer(k):
    x, y, c = lax.axis_index("x"), lax.axis_index("y"), lax.axis_index("c")
    return (1 - x if k & 4 else x, 1 - y if k & 2 else y, 1 - c if k & 1 else c)


def all_gather(arrs, name):
    n = len(arrs)

    def body(*refs):
        ins, outs = refs[:n], refs[n:2 * n]
        send, recv, local = refs[2 * n:]
        me = _my_index()
        copies = []
        for a in range(n):
            cp = pltpu.make_async_copy(ins[a], outs[a].at[me], local.at[a])
            cp.start()
            copies.append(cp)
            for k in range(1, N_DEV):
                s = a * (N_DEV - 1) + k - 1
                cp = pltpu.make_async_remote_copy(src_ref=ins[a], dst_ref=outs[a].at[me], send_sem=send.at[s],
                                                  recv_sem=recv.at[s], device_id=_peer(k), device_id_type=MESH)
                cp.start()
                copies.append(cp)
        for cp in copies:
            cp.wait()

    any_spec = pl.BlockSpec(memory_space=pl.ANY)
    outs = pl.pallas_call(
        body, name=name,
        out_shape=[jax.ShapeDtypeStruct((N_DEV,) + a.shape, a.dtype) for a in arrs],
        in_specs=[any_spec] * n, out_specs=[any_spec] * n,
        scratch_shapes=[pltpu.SemaphoreType.DMA((n * (N_DEV - 1),)), pltpu.SemaphoreType.DMA((n * (N_DEV - 1),)),
                        pltpu.SemaphoreType.DMA((n,))],
        compiler_params=pltpu.CompilerParams(has_side_effects=True),
    )(*arrs)
    return list(outs)


def all_reduce_rows(pack, extra, name):
    r, c = pack.shape
    rs = r // N_DEV
    n_peer = N_DEV - 1

    def body(p_ref, x_ref, o_ref, xo_ref, land, red, send1, recv1, send2, recv2, sendx, recvx, local):
        me = _my_index()

        def rows(i):
            return pl.ds(pl.multiple_of(i * rs, SUBLANES), rs)

        own = [pltpu.make_async_copy(p_ref.at[rows(me)], land.at[me], local.at[0]),
               pltpu.make_async_copy(x_ref, xo_ref.at[me], local.at[1])]
        first = []
        for k in range(1, N_DEV):
            first.append(pltpu.make_async_remote_copy(
                src_ref=p_ref.at[rows(jnp.bitwise_xor(me, k))], dst_ref=land.at[me], send_sem=send1.at[k - 1],
                recv_sem=recv1.at[k - 1], device_id=_peer(k), device_id_type=MESH))
            first.append(pltpu.make_async_remote_copy(
                src_ref=x_ref, dst_ref=xo_ref.at[me], send_sem=sendx.at[k - 1], recv_sem=recvx.at[k - 1],
                device_id=_peer(k), device_id_type=MESH))
        for cp in own + first:
            cp.start()
        for cp in own + first:
            cp.wait()
        acc = land[0]
        for s in range(1, N_DEV):
            acc = acc + land[s]
        red[...] = acc
        mine = pltpu.make_async_copy(red, o_ref.at[rows(me)], local.at[2])
        second = [pltpu.make_async_remote_copy(
            src_ref=red, dst_ref=o_ref.at[rows(me)], send_sem=send2.at[k - 1], recv_sem=recv2.at[k - 1],
            device_id=_peer(k), device_id_type=MESH) for k in range(1, N_DEV)]
        for cp in [mine] + second:
            cp.start()
        for cp in [mine] + second:
            cp.wait()

    any_spec = pl.BlockSpec(memory_space=pl.ANY)
    return pl.pallas_call(
        body, name=name,
        out_shape=[jax.ShapeDtypeStruct((r, c), pack.dtype), jax.ShapeDtypeStruct((N_DEV,) + extra.shape, extra.dtype)],
        in_specs=[any_spec, any_spec], out_specs=[any_spec, any_spec],
        scratch_shapes=[pltpu.VMEM((N_DEV, rs, c), pack.dtype), pltpu.VMEM((rs, c), pack.dtype)]
        + [pltpu.SemaphoreType.DMA((n_peer,))] * 6 + [pltpu.SemaphoreType.DMA((3,))],
        compiler_params=pltpu.CompilerParams(has_side_effects=True),
    )(pack, extra)


GATHER, SCATTER = "gather", "scatter"


def _exchange_copies(srcs, lands, send, recv):
    me = _my_index()
    copies = []
    for a, (src, land) in enumerate(zip(srcs, lands)):
        for k in range(1, N_DEV):
            s = a * (N_DEV - 1) + k - 1
            copies.append(pltpu.make_async_remote_copy(
                src_ref=src.at[jnp.bitwise_xor(me, k)], dst_ref=land.at[me],
                send_sem=send.at[s], recv_sem=recv.at[s], device_id=_peer(k), device_id_type=MESH))
    return copies


def sequencer_exchange(kind, arrs, name, collective_id):
    n = len(arrs)
    n_sem = n * (N_DEV - 1)
    land_shapes = [((N_DEV,) + a.shape if kind == GATHER else a.shape) for a in arrs]
    srcs = [jax.new_ref(a, memory_space=pltpu.MemorySpace.HBM) for a in arrs]
    lands = [jax.empty_ref(jax.ShapeDtypeStruct(s, a.dtype), memory_space=pltpu.MemorySpace.HBM)
             for s, a in zip(land_shapes, arrs)]

    @pl.kernel(mesh=plsc.ScalarSubcoreMesh(axis_name="sequencer", num_cores=1), name=name,
               scratch_types=(pltpu.SemaphoreType.DMA((n_sem,)), pltpu.SemaphoreType.DMA((n_sem,)),
                              pltpu.SemaphoreType.DMA((n,))),
               compiler_params=pltpu.CompilerParams(collective_id=collective_id))
    def launch(send, recv, local):
        barrier = pltpu.get_barrier_semaphore()
        for k in range(1, N_DEV):
            pl.semaphore_signal(barrier, inc=1, device_id=_peer(k), device_id_type=MESH)
        pl.semaphore_wait(barrier, N_DEV - 1)
        me = _my_index()
        mine = [pltpu.make_async_copy(src if kind == GATHER else src.at[me], land.at[me], local.at[a])
                for a, (src, land) in enumerate(zip(srcs, lands))]
        if kind == SCATTER:
            copies = mine + _exchange_copies(srcs, lands, send, recv)
            for cp in copies:
                cp.start()
            for cp in copies:
                cp.wait()
            return

        def block_copy(a, slot, block, k, src=None):
            s = a * (N_DEV - 1) + slot
            return pltpu.make_async_remote_copy(
                src_ref=lands[a].at[block] if src is None else src, dst_ref=lands[a].at[block],
                send_sem=send.at[s], recv_sem=recv.at[s], device_id=_peer(k), device_id_type=MESH)

        chips = (2, 4, 6)
        sibling = jnp.bitwise_xor(me, 1)
        first = [block_copy(a, slot, me, k, src=srcs[a]) for a in range(n) for slot, k in enumerate((1,) + chips)]
        for cp in mine + first:
            cp.start()
        passed = []
        for a in range(n):
            for i, k in enumerate(chips):
                block = jnp.bitwise_xor(me, k)
                block_copy(a, 1 + i, block, k).wait_recv()
                passed.append(block_copy(a, 4 + i, block, 1))
                passed[-1].start()
        for a in range(n):
            block_copy(a, 0, sibling, 1).wait_recv()
            for i, k in enumerate(chips):
                block_copy(a, 4 + i, jnp.bitwise_xor(sibling, k), 1).wait_recv()
        for cp in mine:
            cp.wait()
        for cp in first + passed:
            cp.wait_send()

    launch()
    return [land[...] for land in lands]


def _tile(n, pref):
    for t in pref:
        if n % t == 0:
            return t
    return n


MM_WIDE = 1024
MM_WEIGHT_BLOCK = 8 * 2 ** 20


def _blocks_per_step(nb, fits):
    return max(g for g in range(1, nb + 1) if nb % g == 0 and fits(g))


def mm_nn(a, b3, out_dtype, name, split_cols=None):
    m, k = a.shape
    nb, _, bn = b3.shape
    tm = _tile(m, (1024, 512, 256, 128))
    tn = bn // split_cols if split_cols else _tile(bn, (1024, 896, 512, 256, 128))
    per = bn // tn
    gb = _blocks_per_step(nb, lambda g: g == 1 or (per == 1 and g * bn <= MM_WIDE))

    def body(a_ref, b_ref, o_ref):
        for g in range(gb):
            o_ref[:, g * tn:(g + 1) * tn] = _dot(a_ref[...], b_ref[g]).astype(o_ref.dtype)

    if split_cols:
        out_spec = pl.BlockSpec((None, tm, tn), lambda i, j, jj: (jj, i, 0))
        out_shape = jax.ShapeDtypeStruct((split_cols, m, tn), out_dtype)
    else:
        out_spec = pl.BlockSpec((tm, gb * tn), lambda i, j, jj: (i, j * per + jj))
        out_shape = jax.ShapeDtypeStruct((m, nb * bn), out_dtype)
    return pl.pallas_call(
        body, name=name, grid=(m // tm, nb // gb, per),
        in_specs=[pl.BlockSpec((tm, k), lambda i, j, jj: (i, 0)),
                  pl.BlockSpec((gb, k, tn), lambda i, j, jj: (j, 0, jj))],
        out_specs=out_spec, out_shape=out_shape,
        compiler_params=_cparams("parallel", "arbitrary", "arbitrary"),
    )(a, b3)


def mm_nt(a, w3, out_dtype, name):
    m, _ = a.shape
    nb, ko, bn = w3.shape
    tm = _tile(m, (1024, 512, 256, 128))
    tko = _tile(ko, (1024, 512, 256, 128))
    gb = _blocks_per_step(nb, lambda g: g * tko * bn * w3.dtype.itemsize <= MM_WEIGHT_BLOCK)
    ns = nb // gb

    def body(a_ref, w_ref, o_ref, acc_ref):
        j = pl.program_id(2)

        @pl.when(j == 0)
        def _():
            acc_ref[...] = jnp.zeros_like(acc_ref)

        part = _dot_nt(a_ref[:, :bn], w_ref[0])
        for g in range(1, gb):
            part += _dot_nt(a_ref[:, g * bn:(g + 1) * bn], w_ref[g])
        acc_ref[...] += part

        @pl.when(j == ns - 1)
        def _():
            o_ref[...] = acc_ref[...].astype(o_ref.dtype)

    return pl.pallas_call(
        body, name=name, grid=(m // tm, ko // tko, ns),
        in_specs=[pl.BlockSpec((tm, gb * bn), lambda i, o, j: (i, j)),
                  pl.BlockSpec((gb, tko, bn), lambda i, o, j: (j, o, 0))],
        out_specs=pl.BlockSpec((tm, tko), lambda i, o, j: (i, o)),
        out_shape=jax.ShapeDtypeStruct((m, ko), out_dtype),
        scratch_shapes=[pltpu.VMEM((tm, tko), f32)],
        compiler_params=_cparams("parallel", "arbitrary", "arbitrary"),
    )(a, w3)


def mm_tn(a, dy, ncb, out_dtype, name):
    l, ka = a.shape
    _, n = dy.shape
    bn = n // ncb
    tl = _tile(l, (1024, 512, 256, 128))
    tka = _tile(ka, (512, 256, 128))
    tn = _tile(bn, (1024, 896, 512, 256, 128))
    per = bn // tn
    gb = _blocks_per_step(ncb, lambda g: g == 1 or (per == 1 and g * bn <= MM_WIDE))
    nl = l // tl

    def body(a_ref, dy_ref, o_ref, acc_ref):
        s = pl.program_id(2)

        @pl.when(s == 0)
        def _():
            acc_ref[...] = jnp.zeros_like(acc_ref)

        acc_ref[...] += _dot_tn(a_ref[...], dy_ref[...])

        @pl.when(s == nl - 1)
        def _():
            for g in range(gb):
                o_ref[g] = acc_ref[:, g * tn:(g + 1) * tn].astype(o_ref.dtype)

    return pl.pallas_call(
        body, name=name, grid=(ka // tka, n // (gb * tn), nl),
        in_specs=[pl.BlockSpec((tl, tka), lambda i, j, s: (s, i)),
                  pl.BlockSpec((tl, gb * tn), lambda i, j, s: (s, j))],
        out_specs=pl.BlockSpec((gb, tka, tn), lambda i, j, s: (j // per, i, j % per)),
        out_shape=jax.ShapeDtypeStruct((ncb, ka, bn), out_dtype),
        scratch_shapes=[pltpu.VMEM((tka, gb * tn), f32)],
        compiler_params=_cparams("parallel", "parallel", "arbitrary"),
    )(a, dy)


def mod_part(c_all, w_mod, b_cols):
    nl, d, cols = w_mod.shape

    def body(c_ref, w_ref, b_ref, o_ref):
        cond = _silu(c_ref[...]).astype(bf16)
        o_ref[...] = _dot(cond, w_ref[...].astype(bf16)) + b_ref[...]

    return pl.pallas_call(
        body, name="mod_part", grid=(nl,),
        in_specs=[pl.BlockSpec((N_DEV, d), lambda l: (0, 0)),
                  pl.BlockSpec((None, d, cols), lambda l: (l, 0, 0)),
                  pl.BlockSpec((None, 1, cols), lambda l: (l, 0, 0))],
        out_specs=pl.BlockSpec((None, N_DEV, cols), lambda l: (l, 0, 0)),
        out_shape=jax.ShapeDtypeStruct((nl, N_DEV, cols), f32),
        compiler_params=_cparams("arbitrary"),
    )(c_all, w_mod, b_cols.reshape(nl, 1, cols))


def _row_tile(l):
    return _tile(l, (512, 256, 128))


def _entry_rows(xv, g_ref, sh_ref, sc_ref, h_ref, ht_ref):
    r = lax.rsqrt(jnp.mean(xv * xv, axis=-1, keepdims=True) + EPS)
    h = xv * r * (g_ref[...] * (1.0 + sc_ref[...])) + sh_ref[...]
    h_ref[...] = h.astype(h_ref.dtype)
    ht_ref[...] = jnp.transpose(h).astype(ht_ref.dtype)


def prenorm_fwd(x, g, shift, scale, name):
    l, d = x.shape
    tm = _row_tile(l)

    def body(x_ref, g_ref, sh_ref, sc_ref, h_ref, ht_ref):
        _entry_rows(x_ref[...], g_ref, sh_ref, sc_ref, h_ref, ht_ref)

    return pl.pallas_call(
        body, name=name, grid=(l // tm,),
        in_specs=[pl.BlockSpec((tm, d), lambda i: (i, 0)), _row(d), _row(d), _row(d)],
        out_specs=[pl.BlockSpec((tm, d), lambda i: (i, 0)), pl.BlockSpec((d, tm), lambda i: (0, i))],
        out_shape=[jax.ShapeDtypeStruct((l, d), bf16), jax.ShapeDtypeStruct((d, l), bf16)],
        compiler_params=_cparams("parallel"),
    )(x, g, shift, scale)


def post_prenorm_fwd(x, y, gate, g_post, g_pre, shift, scale, name):
    l, d = x.shape
    tm = _row_tile(l)

    def body(x_ref, y_ref, gate_ref, gp_ref, g_ref, sh_ref, sc_ref, o_ref, h_ref, ht_ref):
        yv = y_ref[...]
        r = lax.rsqrt(jnp.mean(yv * yv, axis=-1, keepdims=True) + EPS)
        xv = x_ref[...] + gate_ref[...] * (yv * r * gp_ref[...])
        o_ref[...] = xv
        _entry_rows(xv, g_ref, sh_ref, sc_ref, h_ref, ht_ref)

    blk = pl.BlockSpec((tm, d), lambda i: (i, 0))
    return pl.pallas_call(
        body, name=name, grid=(l // tm,),
        in_specs=[blk, blk] + [_row(d)] * 5, out_specs=[blk, blk, pl.BlockSpec((d, tm), lambda i: (0, i))],
        out_shape=[jax.ShapeDtypeStruct((l, d), f32), jax.ShapeDtypeStruct((l, d), bf16),
                   jax.ShapeDtypeStruct((d, l), bf16)],
        compiler_params=_cparams("parallel"),
    )(x, y, gate, g_post, g_pre, shift, scale)


def _post_bwd_rows(dxv, yv, r, gate, gv, dy_ref, dgate_ref, dg_ref):
    yn = yv * r
    dgate_ref[...] += jnp.sum(dxv * yn * gv, axis=0, keepdims=True)
    dyg = dxv * gate
    dg_ref[...] += jnp.sum(dyg * yn, axis=0, keepdims=True)
    dyn = dyg * gv
    dy_ref[...] = (r * (dyn - yn * jnp.mean(dyn * yn, axis=-1, keepdims=True))).astype(dy_ref.dtype)


def final_loss(x, y, gate, g, target):
    l, d = x.shape
    tm = _row_tile(l)

    def body(x_ref, y_ref, gate_ref, g_ref, t_ref, dx_ref, loss_ref, dy_ref, dgate_ref, dg_ref):
        @pl.when(pl.program_id(0) == 0)
        def _():
            loss_ref[...] = jnp.zeros_like(loss_ref)
            dgate_ref[...] = jnp.zeros_like(dgate_ref)
            dg_ref[...] = jnp.zeros_like(dg_ref)

        yv, gate, gv = y_ref[...], gate_ref[...], g_ref[...]
        r = lax.rsqrt(jnp.mean(yv * yv, axis=-1, keepdims=True) + EPS)
        diff = x_ref[...] + gate * (yv * r * gv) - t_ref[...]
        dxv = diff * (1.0 / d)
        dx_ref[...] = dxv
        loss_ref[...] += jnp.sum(diff * diff)
        _post_bwd_rows(dxv, yv, r, gate, gv, dy_ref, dgate_ref, dg_ref)

    blk = pl.BlockSpec((tm, d), lambda i: (i, 0))
    return pl.pallas_call(
        body, name="final_loss", grid=(l // tm,),
        in_specs=[blk, blk, _row(d), _row(d), blk],
        out_specs=[blk, pl.BlockSpec((SUBLANES, HEAD), lambda i: (0, 0)), blk, _row(d), _row(d)],
        out_shape=[jax.ShapeDtypeStruct((l, d), f32), jax.ShapeDtypeStruct((SUBLANES, HEAD), f32),
                   jax.ShapeDtypeStruct((l, d), bf16), jax.ShapeDtypeStruct((1, d), f32), jax.ShapeDtypeStruct((1, d), f32)],
        compiler_params=_cparams("arbitrary"),
    )(x, y, gate, g, target)


def prenorm_bwd(dh, x, dx_next, g, scale, name):
    l, d = x.shape
    tm = _row_tile(l)

    def body(dh_ref, x_ref, dxn_ref, g_ref, sc_ref, dx_ref, dsh_ref, dsc_ref, dg_ref):
        @pl.when(pl.program_id(0) == 0)
        def _():
            dsh_ref[...] = jnp.zeros_like(dsh_ref)
            dsc_ref[...] = jnp.zeros_like(dsc_ref)
            dg_ref[...] = jnp.zeros_like(dg_ref)

        xv, dhv, gv, sc1 = x_ref[...], dh_ref[...], g_ref[...], 1.0 + sc_ref[...]
        r = lax.rsqrt(jnp.mean(xv * xv, axis=-1, keepdims=True) + EPS)
        xn = xv * r
        dhx = dhv * xn
        dsh_ref[...] += jnp.sum(dhv, axis=0, keepdims=True)
        dsc_ref[...] += jnp.sum(dhx * gv, axis=0, keepdims=True)
        dg_ref[...] += jnp.sum(dhx * sc1, axis=0, keepdims=True)
        dxn = dhv * (gv * sc1)
        dx_ref[...] = dxn_ref[...] + r * (dxn - xn * jnp.mean(dxn * xn, axis=-1, keepdims=True))

    blk = pl.BlockSpec((tm, d), lambda i: (i, 0))
    return pl.pallas_call(
        body, name=name, grid=(l // tm,),
        in_specs=[blk, blk, blk, _row(d), _row(d)], out_specs=[blk, _row(d), _row(d), _row(d)],
        out_shape=[jax.ShapeDtypeStruct((l, d), f32)] + [jax.ShapeDtypeStruct((1, d), f32)] * 3,
        compiler_params=_cparams("arbitrary"),
    )(dh, x, dx_next, g, scale)


def prenorm_post_bwd(dh, x, dx_next, g, scale, y, gate, g_post, name):
    l, d = x.shape
    tm = _tile(l, (256, 128))

    def body(dh_ref, x_ref, dxn_ref, g_ref, sc_ref, y_ref, gate_ref, gp_ref,
             dx_ref, dsh_ref, dsc_ref, dg_ref, dy_ref, dgate_ref, dgp_ref):
        @pl.when(pl.program_id(0) == 0)
        def _():
            for ref in (dsh_ref, dsc_ref, dg_ref, dgate_ref, dgp_ref):
                ref[...] = jnp.zeros_like(ref)

        xv, dhv, gv, sc1 = x_ref[...], dh_ref[...], g_ref[...], 1.0 + sc_ref[...]
        r = lax.rsqrt(jnp.mean(xv * xv, axis=-1, keepdims=True) + EPS)
        xn = xv * r
        dhx = dhv * xn
        dsh_ref[...] += jnp.sum(dhv, axis=0, keepdims=True)
        dsc_ref[...] += jnp.sum(dhx * gv, axis=0, keepdims=True)
        dg_ref[...] += jnp.sum(dhx * sc1, axis=0, keepdims=True)
        dxn = dhv * (gv * sc1)
        dxv = dxn_ref[...] + r * (dxn - xn * jnp.mean(dxn * xn, axis=-1, keepdims=True))
        dx_ref[...] = dxv
        yv = y_ref[...]
        ry = lax.rsqrt(jnp.mean(yv * yv, axis=-1, keepdims=True) + EPS)
        _post_bwd_rows(dxv, yv, ry, gate_ref[...], gp_ref[...], dy_ref, dgate_ref, dgp_ref)

    blk = pl.BlockSpec((tm, d), lambda i: (i, 0))
    row = jax.ShapeDtypeStruct((1, d), f32)
    return pl.pallas_call(
        body, name=name, grid=(l // tm,),
        in_specs=[blk, blk, blk, _row(d), _row(d), blk, _row(d), _row(d)],
        out_specs=[blk, _row(d), _row(d), _row(d), blk, _row(d), _row(d)],
        out_shape=[jax.ShapeDtypeStruct((l, d), f32), row, row, row, jax.ShapeDtypeStruct((l, d), bf16), row, row],
        compiler_params=_cparams("arbitrary"),
    )(dh, x, dx_next, g, scale, y, gate, g_post)


def _tril_mask():
    r = lax.broadcasted_iota(jnp.int32, (HEAD, HEAD), 0)
    c = lax.broadcasted_iota(jnp.int32, (HEAD, HEAD), 1)
    return r >= c


def sgu_fwd(proj, norm_g, w_s, b_s):
    l = proj.shape[0]
    nh = w_s.shape[0]
    wa = nh * HEAD

    def body(au_ref, av_ref, az_ref, ng_ref, w_ref, b_ref, o_ref):
        tril = _tril_mask()
        for h in range(nh):
            sl = slice(h * HEAD, (h + 1) * HEAD)
            gv = _gelu(av_ref[:, sl].astype(f32))
            r = lax.rsqrt(jnp.mean(gv * gv, axis=-1, keepdims=True) + EPS)
            vh = gv * r * ng_ref[:, sl]
            wm = jnp.where(tril, w_ref[h], 0.0).astype(bf16)
            s = _dot(wm, vh.astype(bf16)) + b_ref[h]
            o_ref[:, sl] = (_gelu(au_ref[:, sl].astype(f32)) * s * _silu(az_ref[:, sl].astype(f32))).astype(o_ref.dtype)

    def col(j):
        return pl.BlockSpec((HEAD, wa), lambda n: (n, j))

    return pl.pallas_call(
        body, name="sgu_fwd", grid=(l // HEAD,),
        in_specs=[col(0), col(1), col(2), _row(wa),
                  pl.BlockSpec((nh, HEAD, HEAD), lambda n: (0, 0, 0)), pl.BlockSpec((nh, HEAD, 1), lambda n: (0, 0, 0))],
        out_specs=pl.BlockSpec((HEAD, wa), lambda n: (n, 0)),
        out_shape=jax.ShapeDtypeStruct((l, 2 * wa), bf16),
        compiler_params=_cparams("parallel"),
    )(proj, proj, proj, norm_g, w_s, b_s)


def sgu_bwd(proj, dcat, norm_g, w_s, b_s):
    l = proj.shape[0]
    nh = w_s.shape[0]
    wa = nh * HEAD

    def body(au_ref, av_ref, az_ref, do_ref, ng_ref, w_ref, b_ref, da_ref, dw_ref, db_ref, dng_ref):
        @pl.when(pl.program_id(0) == 0)
        def _():
            dw_ref[...] = jnp.zeros_like(dw_ref)
            db_ref[...] = jnp.zeros_like(db_ref)
            dng_ref[...] = jnp.zeros_like(dng_ref)

        tril = _tril_mask()
        for h in range(nh):
            sl = slice(h * HEAD, (h + 1) * HEAD)
            au, av, az = au_ref[:, sl].astype(f32), av_ref[:, sl].astype(f32), az_ref[:, sl].astype(f32)
            ng = ng_ref[:, sl]
            gv = _gelu(av)
            r = lax.rsqrt(jnp.mean(gv * gv, axis=-1, keepdims=True) + EPS)
            gvn = gv * r
            vh = (gvn * ng).astype(bf16)
            wm = jnp.where(tril, w_ref[h], 0.0).astype(bf16)
            s = _dot(wm, vh) + b_ref[h]
            gu, sz = _gelu(au), _silu(az)
            dov = do_ref[:, sl].astype(f32)
            da_ref[:, sl] = (dov * s * sz * _gelu_grad(au)).astype(da_ref.dtype)
            da_ref[:, 2 * wa + h * HEAD:2 * wa + (h + 1) * HEAD] = (dov * gu * s * _silu_grad(az)).astype(da_ref.dtype)
            ds = dov * gu * sz
            db_ref[h] += jnp.sum(ds, axis=-1, keepdims=True)
            dsb = ds.astype(bf16)
            dw_ref[h] += jnp.where(tril, _dot_nt(dsb, vh), 0.0)
            dvh = _dot_tn(wm, dsb)
            dng_ref[:, sl] += jnp.sum(dvh * gvn, axis=0, keepdims=True)
            dgvn = dvh * ng
            dgv = r * (dgvn - gvn * jnp.mean(dgvn * gvn, axis=-1, keepdims=True))
            da_ref[:, wa + h * HEAD:wa + (h + 1) * HEAD] = (dgv * _gelu_grad(av)).astype(da_ref.dtype)

    def col(j):
        return pl.BlockSpec((HEAD, wa), lambda n: (n, j))

    whole_w = pl.BlockSpec((nh, HEAD, HEAD), lambda n: (0, 0, 0))
    whole_b = pl.BlockSpec((nh, HEAD, 1), lambda n: (0, 0, 0))
    return pl.pallas_call(
        body, name="sgu_bwd", grid=(l // HEAD,),
        in_specs=[col(0), col(1), col(2), col(0), _row(wa), whole_w, whole_b],
        out_specs=[pl.BlockSpec((HEAD, 3 * wa), lambda n: (n, 0)), whole_w, whole_b, _row(wa)],
        out_shape=[jax.ShapeDtypeStruct(proj.shape, bf16), jax.ShapeDtypeStruct((nh, HEAD, HEAD), f32),
                   jax.ShapeDtypeStruct((nh, HEAD, 1), f32), jax.ShapeDtypeStruct((1, wa), f32)],
        compiler_params=_cparams("arbitrary"),
    )(proj, proj, proj, dcat, norm_g, w_s, b_s)


_LOG2E = 1.0 / math.log(2.0)
_SB_EXP_CLAMP = 120.0


def _sb_scores(q, k, scale):
    z = _dot_nt(q, k) * (scale * _LOG2E)
    return z, jnp.maximum(z, jnp.log2(1.0 + jnp.exp2(jnp.minimum(z, _SB_EXP_CLAMP))))


SB_KEYS = 256


def _sb_sum_matrix(tri, kb):
    s = lax.broadcasted_iota(jnp.int32, (2 * kb, kb + HEAD), 0) % kb
    j = lax.broadcasted_iota(jnp.int32, (2 * kb, kb + HEAD), 1)
    return jnp.where(jnp.logical_or(j >= kb, tri(s, j)), 1.0, 0.0).astype(bf16)


def _sb_sums(x, sums):
    kb = x.shape[1]
    c2 = _dot(jnp.concatenate(_split_bf16(x), axis=1), sums)
    return c2[:, :kb], c2[:, kb:]


def _sb_wide(v, kb):
    return jnp.concatenate([v] * (kb // HEAD), axis=1) if kb > HEAD else v


def _sb_q_tile(l, most=512):
    return _tile(l, tuple(t for t in (1024, 512, 256, 128) if t <= most))


def _sb_band_levels(band):
    return _tile(band, (4, 2, 1))


def _sb_heads_per_step(nh, most):
    return _tile(nh, tuple(h for h in (4, 2) if h <= most))


def sb_fwd(proj, mixed, nh):
    l = proj.shape[0]
    wb = nh * HEAD
    tq = _sb_q_tile(l, 1024)
    kb = min(SB_KEYS, tq)
    band = tq // kb
    hp = _sb_heads_per_step(nh, 4)
    levels = _sb_band_levels(band)
    scale = 1.0 / math.sqrt(HEAD)
    qc, kc, vc, zc = 3 * nh, 4 * nh, 5 * nh, 6 * nh

    def body(q_ref, k_ref, v_ref, bz_ref, mixed_ref, o_ref, att_ref, tot_ref):
        del mixed_ref
        i = pl.program_id(1)
        sums = _sb_sum_matrix(lambda s, j: s > j, kb)
        t_pos = i * tq + lax.broadcasted_iota(jnp.int32, (tq, kb), 0)
        s_off = lax.broadcasted_iota(jnp.int32, (tq, kb), 1)

        def step(j, carry, masked, row0=0):
            rows = pl.ds(pl.multiple_of(j * kb, kb), kb)
            out = []
            for e in range(hp):
                acc, tot = carry[e]
                sl = slice(e * HEAD, (e + 1) * HEAD)
                z, sp = _sb_scores(q_ref[row0:, sl], k_ref[rows, sl], scale)
                lb = z - sp
                if masked:
                    mask = s_off[row0:] + j * kb < t_pos[row0:]
                    sp = jnp.where(mask, sp, 0.0)
                later, total = _sb_sums(sp, sums)
                w = jnp.exp2(lb + _sb_wide(tot[row0:], kb) - later)
                if masked:
                    w = jnp.where(mask, w, 0.0)
                new = (acc[row0:] + _dot(w.astype(bf16), v_ref[rows, sl]), tot[row0:] - total)
                out.append(tuple(jnp.concatenate([old[:row0], upd]) if row0 else upd for old, upd in zip(carry[e], new)))
            return tuple(out)

        zero = jnp.zeros((tq, HEAD), f32)
        carry = ((zero, zero),) * hp
        for lv in reversed(range(levels)):
            carry = lax.fori_loop(
                0, band // levels,
                lambda t, c, lv=lv: step(band * i + (lv + 1) * (band // levels) - 1 - t, c, True, lv * (tq // levels)), carry)
        carry = lax.fori_loop(0, band * i, lambda t, c: step(band * i - 1 - t, c, False), carry)
        for e in range(hp):
            acc, tot = carry[e]
            sl = slice(e * HEAD, (e + 1) * HEAD)
            att_ref[:, sl] = acc.astype(att_ref.dtype)
            o_ref[:, sl] = (acc * _silu(bz_ref[:, sl].astype(f32))).astype(o_ref.dtype)
            tot_ref[e] = tot[:, :1]

    blk = lambda c0: pl.BlockSpec((tq, hp * HEAD), lambda g, i: (i, c0 // hp + g))
    head = lambda c0: pl.BlockSpec((l, hp * HEAD), lambda g, i: (0, c0 // hp + g))
    return pl.pallas_call(
        body, name="sb_fwd", grid=(nh // hp, l // tq),
        in_specs=[blk(qc), head(kc), head(vc), blk(zc), pl.BlockSpec(memory_space=pl.ANY)],
        out_specs=[blk(mixed.shape[1] // HEAD - nh), blk(0), pl.BlockSpec((hp, tq, 1), lambda g, i: (g, i, 0))],
        out_shape=[jax.ShapeDtypeStruct(mixed.shape, bf16), jax.ShapeDtypeStruct((l, wb), bf16),
                   jax.ShapeDtypeStruct((nh, l, 1), f32)],
        input_output_aliases={4: 0},
        compiler_params=_cparams("parallel", "arbitrary"),
    )(proj, proj, proj, proj, mixed)


def sb_bwd(proj, dcat, att, tot, dproj, nh):
    l = proj.shape[0]
    wb = nh * HEAD
    tq = _sb_q_tile(l, 1024)
    kb = min(SB_KEYS, tq)
    band = tq // kb
    nq = l // tq
    hp = _sb_heads_per_step(nh, 2)
    levels = _sb_band_levels(band)
    scale = 1.0 / math.sqrt(HEAD)
    qc, kc, vc, zc = 3 * nh, 4 * nh, 5 * nh, 6 * nh

    def body(q_ref, k_ref, v_ref, bz_ref, do_ref, att_ref, tot_ref, dproj_in, dproj_ref, dk_acc, dv_acc, dob_ref,
             tile_ref, head_ref, sems):
        del dproj_in
        g, i = pl.program_id(0), pl.program_id(1)

        def put(src, row0, c0, k):
            cols = pl.ds(pl.multiple_of((c0 + g * hp) * HEAD, HEAD), hp * HEAD)
            cp = pltpu.make_async_copy(src, dproj_ref.at[pl.ds(row0, src.shape[0]), cols], sems.at[k])
            cp.start()
            return cp

        @pl.when(i == 0)
        def _():
            dk_acc[...] = jnp.zeros_like(dk_acc)
            dv_acc[...] = jnp.zeros_like(dv_acc)

        my_rows = pl.multiple_of(i * tq, tq)
        bz = bz_ref[...].astype(f32)
        dov = do_ref[...].astype(f32)
        tile_ref[0] = (dov * att_ref[...].astype(f32) * _silu_grad(bz)).astype(bf16)
        dbz_copy = put(tile_ref.at[0], my_rows, zc, 0)
        dob_ref[...] = (dov * _silu(bz)).astype(bf16)
        upto = _sb_sum_matrix(lambda s, j: s <= j, kb)
        before = _sb_sum_matrix(lambda j, s: j < s, kb)
        t_pos = i * tq + lax.broadcasted_iota(jnp.int32, (tq, kb), 0)
        s_off = lax.broadcasted_iota(jnp.int32, (tq, kb), 1)

        def step(j, carry, masked, row0=0):
            rows = pl.ds(pl.multiple_of(j * kb, kb), kb)
            out = []
            for h in range(hp):
                dq, sp_seen, e_seen = (c[row0:] for c in carry[h])
                sl = slice(h * HEAD, (h + 1) * HEAD)
                q, kj, vj, dob = q_ref[row0:, sl], k_ref[rows, sl], v_ref[rows, sl], dob_ref[row0:, sl]
                z, sp = _sb_scores(q, kj, scale)
                lb = z - sp
                if masked:
                    mask = s_off[row0:] + j * kb < t_pos[row0:]
                    sp = jnp.where(mask, sp, 0.0)
                sp_upto, sp_total = _sb_sums(sp, upto)
                w = jnp.exp2(lb + _sb_wide(sp_seen, kb) + sp_upto)
                if masked:
                    w = jnp.where(mask, w, 0.0)
                dv_acc[rows, sl] += _dot_tn(w.astype(bf16), dob)
                e = _dot_nt(dob, vj) * w
                e_before, e_total = _sb_sums(e, before)
                dz = (e - (e + _sb_wide(e_seen, kb) + e_before) * jnp.exp2(lb)) * scale
                if masked:
                    dz = jnp.where(mask, dz, 0.0)
                dz = dz.astype(bf16)
                dk_acc[rows, sl] += _dot_tn(dz, q)
                new = (dq + _dot(dz, kj), sp_seen + sp_total, e_seen + e_total)
                out.append(tuple(jnp.concatenate([old[:row0], upd]) if row0 else upd for old, upd in zip(carry[h], new)))
            return tuple(out)

        zero = jnp.zeros((tq, HEAD), f32)
        init = tuple((zero, jnp.broadcast_to(tot_ref[h], (tq, HEAD)), zero) for h in range(hp))
        carry = lax.fori_loop(0, band * i, lambda j, c: step(j, c, False), init)
        for lv in range(levels):
            carry = lax.fori_loop(
                0, band // levels,
                lambda t, c, lv=lv: step(band * i + lv * (band // levels) + t, c, True, lv * (tq // levels)), carry)
        for h in range(hp):
            tile_ref[1, :, h * HEAD:(h + 1) * HEAD] = carry[h][0].astype(bf16)
        dq_copy = put(tile_ref.at[1], my_rows, qc, 1)
        dbz_copy.wait()
        dq_copy.wait()

        @pl.when(i == nq - 1)
        def _():
            head_ref[0] = dk_acc[...].astype(bf16)
            head_ref[1] = dv_acc[...].astype(bf16)
            copies = [put(head_ref.at[0], 0, kc, 2), put(head_ref.at[1], 0, vc, 3)]
            for cp in copies:
                cp.wait()

    blk = lambda c0: pl.BlockSpec((tq, hp * HEAD), lambda g, i: (i, c0 // hp + g))
    head = lambda c0: pl.BlockSpec((l, hp * HEAD), lambda g, i: (0, c0 // hp + g))
    any_spec = pl.BlockSpec(memory_space=pl.ANY)
    return pl.pallas_call(
        body, name="sb_bwd", grid=(nh // hp, nq),
        in_specs=[blk(qc), head(kc), head(vc), blk(zc), blk(nh), blk(0),
                  pl.BlockSpec((hp, tq, 1), lambda g, i: (g, i, 0)), any_spec],
        out_specs=any_spec, out_shape=jax.ShapeDtypeStruct(dproj.shape, bf16), input_output_aliases={7: 0},
        scratch_shapes=[pltpu.VMEM((l, hp * HEAD), f32), pltpu.VMEM((l, hp * HEAD), f32),
                        pltpu.VMEM((tq, hp * HEAD), bf16), pltpu.VMEM((2, tq, hp * HEAD), bf16),
                        pltpu.VMEM((2, l, hp * HEAD), bf16), pltpu.SemaphoreType.DMA((4,))],
        compiler_params=_cparams("parallel", "arbitrary"),
    )(proj, proj, proj, proj, dcat, att, tot, dproj)


def _disc(lr, li, ldt):
    dt = jnp.exp(ldt)
    mag = jnp.exp(lr * dt)
    a_re = mag * jnp.cos(li * dt)
    a_im = mag * jnp.sin(li * dt)
    den = lr * lr + li * li
    nr = a_re - 1.0
    return a_re, a_im, (nr * lr + a_im * li) / den, (a_im * lr - nr * li) / den


def s5_params_fwd(lr, li, ldt, bt_re, bt_im):
    g, c, p = bt_re.shape

    def body(lr_ref, li_ref, ldt_ref, br_ref, bi_ref, ar_ref, ai_ref, bbr_ref, bbi_ref):
        a_re, a_im, cr, ci = _disc(lr_ref[...], li_ref[...], ldt_ref[...])
        ar_ref[...] = a_re
        ai_ref[...] = a_im
        for k in range(c):
            br, bi = br_ref[:, k, :], bi_ref[:, k, :]
            bbr_ref[:, k, :] = cr * br - ci * bi
            bbi_ref[:, k, :] = cr * bi + ci * br

    return pl.pallas_call(
        body, name="s5_params_fwd",
        out_shape=[jax.ShapeDtypeStruct((g, p), f32)] * 2 + [jax.ShapeDtypeStruct((g, c, p), f32)] * 2,
    )(lr, li, ldt, bt_re, bt_im)


def s5_params_bwd(lr, li, ldt, bt_re, bt_im, da_re, da_im, dbbt_re, dbbt_im):
    g, c, p = bt_re.shape

    def body(lr_ref, li_ref, ldt_ref, br_ref, bi_ref, dar_ref, dai_ref, dbbr_ref, dbbi_ref,
             dlr_ref, dli_ref, dldt_ref, dbr_ref, dbi_ref):
        (a_re, a_im, cr, ci), vjp = jax.vjp(_disc, lr_ref[...], li_ref[...], ldt_ref[...])
        dcr = jnp.zeros((g, p), f32)
        dci = jnp.zeros((g, p), f32)
        for k in range(c):
            br, bi = br_ref[:, k, :], bi_ref[:, k, :]
            dr, di = dbbr_ref[:, k, :], dbbi_ref[:, k, :]
            dcr += dr * br + di * bi
            dci += di * br - dr * bi
            dbr_ref[:, k, :] = cr * dr + ci * di
            dbi_ref[:, k, :] = cr * di - ci * dr
        dlr, dli, dldt = vjp((dar_ref[...], dai_ref[...], dcr, dci))
        dlr_ref[...] = dlr
        dli_ref[...] = dli
        dldt_ref[...] = dldt

    return pl.pallas_call(
        body, name="s5_params_bwd",
        out_shape=[jax.ShapeDtypeStruct((g, p), f32)] * 2 + [jax.ShapeDtypeStruct((g, 1), f32)]
        + [jax.ShapeDtypeStruct((g, c, p), f32)] * 2,
    )(lr, li, ldt, bt_re, bt_im, da_re, da_im, dbbt_re, dbbt_im)


def _cmul(ar, ai, br, bi):
    return ar * br - ai * bi, ar * bi + ai * br


def _power_tables(ar, ai):
    rows = lax.broadcasted_iota(jnp.int32, (SUBLANES, ar.shape[1]), 0)
    pr = jnp.zeros((SUBLANES, ar.shape[1]), f32)
    pi = jnp.zeros((SUBLANES, ar.shape[1]), f32)
    cr, ci = ar, ai
    pows = {}
    for r in range(SUBLANES):
        pows[r + 1] = (cr, ci)
        pr = jnp.where(rows == r, cr, pr)
        pi = jnp.where(rows == r, ci, pi)
        cr, ci = _cmul(cr, ci, ar, ai)
    return [pows[1], pows[2], pows[4]], pr, pi


def _ssm_time_tile(l):
    return _tile(l, (2048, 1024, 512, 256, 128))


def ssm_fwd(u, bre3, bim3, cre3, cimn3, a_re, a_im, d_skip):
    l, w = u.shape[0], d_skip.shape[1]
    nj = w // HEAD
    ns = STATES_PER_LANE_BLOCK
    tt = _ssm_time_tile(l)

    def body(u_ref, bre_ref, bim_ref, cre_ref, cim_ref, ar_ref, ai_ref, d_ref, y_ref, hr_ref, hi_ref, cr_ref, ci_ref):
        @pl.when(pl.program_id(1) == 0)
        def _():
            cr_ref[...] = jnp.zeros_like(cr_ref)
            ci_ref[...] = jnp.zeros_like(ci_ref)

        uv = u_ref[...]
        hr_ref[...] = _dot(uv, bre_ref[...])
        hi_ref[...] = _dot(uv, bim_ref[...])
        steps, pr, pi = _power_tables(ar_ref[...], ai_ref[...])
        rows = lax.broadcasted_iota(jnp.int32, (SUBLANES, ns), 0)
        steps = [(jnp.where(rows >= d, sr_, 0.0), jnp.where(rows >= d, si_, 0.0)) for d, (sr_, si_) in zip((1, 2, 4), steps)]

        def blk(b, carry):
            cr, ci = carry
            sl = pl.ds(pl.multiple_of(b * SUBLANES, SUBLANES), SUBLANES)
            xr, xi = hr_ref[sl, :], hi_ref[sl, :]
            for d, (sr_, si_) in zip((1, 2, 4), steps):
                mr, mi = _cmul(sr_, si_, pltpu.roll(xr, d, axis=0), pltpu.roll(xi, d, axis=0))
                xr, xi = xr + mr, xi + mi
            mr, mi = _cmul(pr, pi, cr, ci)
            xr, xi = xr + mr, xi + mi
            hr_ref[sl, :] = xr
            hi_ref[sl, :] = xi
            return xr[SUBLANES - 1:, :], xi[SUBLANES - 1:, :]

        cr, ci = lax.fori_loop(0, tt // SUBLANES, blk, (cr_ref[...], ci_ref[...]))
        cr_ref[...] = cr
        ci_ref[...] = ci
        y = _dot(hr_ref[...].astype(bf16), cre_ref[...]) + _dot(hi_ref[...].astype(bf16), cim_ref[...])
        y_ref[...] = y + d_ref[...] * uv.astype(f32)

    lane = pl.BlockSpec((tt, HEAD), lambda j, i: (i, j))
    st = pl.BlockSpec((tt, ns), lambda j, i: (i, j))
    b3 = pl.BlockSpec((None, HEAD, ns), lambda j, i: (j, 0, 0))
    c3 = pl.BlockSpec((None, ns, HEAD), lambda j, i: (j, 0, 0))
    arow = pl.BlockSpec((1, ns), lambda j, i: (0, j))
    return pl.pallas_call(
        body, name="ssm_fwd", grid=(nj, l // tt),
        in_specs=[lane, b3, b3, c3, c3, arow, arow, pl.BlockSpec((1, HEAD), lambda j, i: (0, j))],
        out_specs=[lane, st, st],
        out_shape=[jax.ShapeDtypeStruct((l, w), f32), jax.ShapeDtypeStruct((l, nj * ns), f32),
                   jax.ShapeDtypeStruct((l, nj * ns), f32)],
        scratch_shapes=[pltpu.VMEM((1, ns), f32), pltpu.VMEM((1, ns), f32)],
        compiler_params=_cparams("parallel", "arbitrary"),
    )(u, bre3, bim3, cre3, cimn3, a_re, a_im, d_skip)


def ssm_bwd(dy, u, dproj, h_re, h_im, bre3, bim3, cre3, cimn3, a_re, a_im, d_skip):
    l, w = u.shape[0], d_skip.shape[1]
    nj = w // HEAD
    ns = STATES_PER_LANE_BLOCK
    tt = _ssm_time_tile(l)
    nt = l // tt

    def body(dy_ref, u_ref, dproj_ref, hr_ref, hi_ref, bre_ref, bim_ref, cre_ref, cim_ref, ar_ref, ai_ref, d_ref,
             du_ref, dd_ref, dar_ref, dai_ref, dbre_ref, dbim_ref, dcre_ref, dcim_ref, kr_ref, ki_ref, cr_ref, ci_ref,
             accr_ref, acci_ref):
        del dproj_ref
        i = pl.program_id(1)

        @pl.when(i == 0)
        def _():
            for ref in (cr_ref, ci_ref, accr_ref, acci_ref, dd_ref, dbre_ref, dbim_ref, dcre_ref, dcim_ref):
                ref[...] = jnp.zeros_like(ref)

        dyv = dy_ref[...]
        dyb = dyv.astype(bf16)
        uv = u_ref[...]
        kr_ref[...] = _dot_nt(dyb, cre_ref[...])
        ki_ref[...] = _dot_nt(dyb, cim_ref[...])
        steps, pr, pi = _power_tables(ar_ref[...], -ai_ref[...])
        rows = lax.broadcasted_iota(jnp.int32, (SUBLANES, ns), 0)
        qr = jnp.zeros((SUBLANES, ns), f32)
        qi = jnp.zeros((SUBLANES, ns), f32)
        for r in range(SUBLANES):
            qr = jnp.where(rows == r, pr[SUBLANES - 1 - r:SUBLANES - r, :], qr)
            qi = jnp.where(rows == r, pi[SUBLANES - 1 - r:SUBLANES - r, :], qi)
        nb = tt // SUBLANES
        steps = [(jnp.where(rows < SUBLANES - d, sr_, 0.0), jnp.where(rows < SUBLANES - d, si_, 0.0))
                 for d, (sr_, si_) in zip((1, 2, 4), steps)]

        def blk(t, carry):
            cr, ci, accr, acci = carry
            sl = pl.ds(pl.multiple_of((nb - 1 - t) * SUBLANES, SUBLANES), SUBLANES)
            xr, xi = kr_ref[sl, :], ki_ref[sl, :]
            for d, (sr_, si_) in zip((1, 2, 4), steps):
                mr, mi = _cmul(sr_, si_, pltpu.roll(xr, SUBLANES - d, axis=0), pltpu.roll(xi, SUBLANES - d, axis=0))
                xr, xi = xr + mr, xi + mi
            mr, mi = _cmul(qr, qi, cr, ci)
            xr, xi = xr + mr, xi + mi
            kr_ref[sl, :] = xr
            ki_ref[sl, :] = xi
            last = rows == SUBLANES - 1
            nr = jnp.where(last, cr, pltpu.roll(xr, SUBLANES - 1, axis=0))
            ni = jnp.where(last, ci, pltpu.roll(xi, SUBLANES - 1, axis=0))
            hr, hi = hr_ref[sl, :], hi_ref[sl, :]
            accr = accr + nr * hr + ni * hi
            acci = acci + ni * hr - nr * hi
            return xr[:1, :], xi[:1, :], accr, acci

        cr, ci, accr, acci = lax.fori_loop(0, nb, blk, (cr_ref[...], ci_ref[...], accr_ref[...], acci_ref[...]))
        cr_ref[...] = cr
        ci_ref[...] = ci
        accr_ref[...] = accr
        acci_ref[...] = acci
        kr, ki = kr_ref[...].astype(bf16), ki_ref[...].astype(bf16)
        du = _dot_nt(kr, bre_ref[...]) + _dot_nt(ki, bim_ref[...]) + d_ref[...] * dyv
        du_ref[...] = du.astype(du_ref.dtype)
        dd_ref[...] += jnp.sum(dyv * uv.astype(f32), axis=0, keepdims=True)
        dbre_ref[...] += _dot_tn(uv, kr)
        dbim_ref[...] += _dot_tn(uv, ki)
        dcre_ref[...] += _dot_tn(hr_ref[...].astype(bf16), dyb)
        dcim_ref[...] += _dot_tn(hi_ref[...].astype(bf16), dyb)

        @pl.when(i == nt - 1)
        def _():
            dar_ref[...] = jnp.sum(accr_ref[...], axis=0, keepdims=True)
            dai_ref[...] = jnp.sum(acci_ref[...], axis=0, keepdims=True)

    lane = pl.BlockSpec((tt, HEAD), lambda j, i: (nt - 1 - i, j))
    st = pl.BlockSpec((tt, ns), lambda j, i: (nt - 1 - i, j))
    b3 = pl.BlockSpec((None, HEAD, ns), lambda j, i: (j, 0, 0))
    c3 = pl.BlockSpec((None, ns, HEAD), lambda j, i: (j, 0, 0))
    arow = pl.BlockSpec((1, ns), lambda j, i: (0, j))
    drow = pl.BlockSpec((1, HEAD), lambda j, i: (0, j))
    return pl.pallas_call(
        body, name="ssm_bwd", grid=(nj, nt),
        in_specs=[lane, lane, pl.BlockSpec(memory_space=pl.ANY), st, st, b3, b3, c3, c3, arow, arow, drow],
        out_specs=[lane, drow, arow, arow, b3, b3, c3, c3], input_output_aliases={2: 0},
        out_shape=[jax.ShapeDtypeStruct(dproj.shape, bf16), jax.ShapeDtypeStruct((1, w), f32),
                   jax.ShapeDtypeStruct((1, nj * ns), f32), jax.ShapeDtypeStruct((1, nj * ns), f32),
                   jax.ShapeDtypeStruct((nj, HEAD, ns), f32), jax.ShapeDtypeStruct((nj, HEAD, ns), f32),
                   jax.ShapeDtypeStruct((nj, ns, HEAD), f32), jax.ShapeDtypeStruct((nj, ns, HEAD), f32)],
        scratch_shapes=[pltpu.VMEM((tt, ns), f32), pltpu.VMEM((tt, ns), f32), pltpu.VMEM((1, ns), f32),
                        pltpu.VMEM((1, ns), f32), pltpu.VMEM((SUBLANES, ns), f32), pltpu.VMEM((SUBLANES, ns), f32)],
        compiler_params=_cparams("parallel", "arbitrary"),
    )(dy, u, dproj, h_re, h_im, bre3, bim3, cre3, cimn3, a_re, a_im, d_skip)


def glu_fwd(y, z_src, w_glu, b_glu):
    l, w = y.shape
    tm = _row_tile(l)

    def body(y_ref, z_ref, w_ref, b_ref, g_ref, t_ref, o_ref):
        g = _gelu(y_ref[...])
        gb = g.astype(bf16)
        t = _dot(gb, w_ref[...]) + b_ref[...]
        g_ref[...] = gb
        t_ref[...] = t
        o_ref[...] = (g * jax.nn.sigmoid(t) * _silu(z_ref[...].astype(f32))).astype(o_ref.dtype)

    blk = pl.BlockSpec((tm, w), lambda i: (i, 0))
    return pl.pallas_call(
        body, name="glu_fwd", grid=(l // tm,),
        in_specs=[blk, pl.BlockSpec((tm, w), lambda i: (i, 1)), pl.BlockSpec((w, w), lambda i: (0, 0)), _row(w)],
        out_specs=[blk, blk, blk],
        out_shape=[jax.ShapeDtypeStruct((l, w), bf16), jax.ShapeDtypeStruct((l, w), f32),
                   jax.ShapeDtypeStruct((l, w), bf16)],
        compiler_params=_cparams("parallel"),
    )(y, z_src, w_glu, b_glu)


def glu_bwd(dout, y, t, z_src, w_glu):
    l, w = y.shape
    tm = _row_tile(l)

    def body(do_ref, y_ref, t_ref, z_ref, w_ref, dy_ref, dz_ref, dt_ref, db_ref):
        @pl.when(pl.program_id(0) == 0)
        def _():
            db_ref[...] = jnp.zeros_like(db_ref)

        yv, zv, dov = y_ref[...], z_ref[...].astype(f32), do_ref[...]
        g = _gelu(yv)
        sg = jax.nn.sigmoid(t_ref[...])
        dy2 = dov * _silu(zv)
        dz_ref[...] = (dov * g * sg * _silu_grad(zv)).astype(dz_ref.dtype)
        dt = dy2 * g * sg * (1.0 - sg)
        dtb = dt.astype(bf16)
        dt_ref[...] = dtb
        db_ref[...] += jnp.sum(dt, axis=0, keepdims=True)
        dg = dy2 * sg + _dot_nt(dtb, w_ref[...])
        dy_ref[...] = dg * _gelu_grad(yv)

    blk = pl.BlockSpec((tm, w), lambda i: (i, 0))
    return pl.pallas_call(
        body, name="glu_bwd", grid=(l // tm,),
        in_specs=[blk, blk, blk, pl.BlockSpec((tm, w), lambda i: (i, 1)), pl.BlockSpec((w, w), lambda i: (0, 0))],
        out_specs=[blk, pl.BlockSpec((tm, w), lambda i: (i, 1)), blk, _row(w)],
        out_shape=[jax.ShapeDtypeStruct((l, w), f32), jax.ShapeDtypeStruct((l, 2 * w), bf16),
                   jax.ShapeDtypeStruct((l, w), bf16), jax.ShapeDtypeStruct((1, w), f32)],
        compiler_params=_cparams("arbitrary"),
    )(dout, y, t, z_src, w_glu)


def _adamw(w, g, m, v):
    m = ADAM_B1 * m + (1.0 - ADAM_B1) * g
    v = ADAM_B2 * v + (1.0 - ADAM_B2) * (g * g)
    m_hat = m / (1.0 - ADAM_B1 ** ADAM_STEP)
    v_hat = v / (1.0 - ADAM_B2 ** ADAM_STEP)
    return -ADAM_LR * (m_hat / (jnp.sqrt(v_hat) + ADAM_EPS) + ADAM_WD * w), m, v


def adam_reduce(pieces, w, m, v, name):
    r, c = w.shape
    n = pieces.shape[0]
    tr = _tile(r, (256, 128, 64, 32, 16, 8))

    def body(p_ref, w_ref, m_ref, v_ref, g_ref, d_ref, nm_ref, nv_ref):
        g = p_ref[0].astype(f32)
        for s in range(1, n):
            g = g + p_ref[s].astype(f32)
        g_ref[...] = g
        d_ref[...], nm_ref[...], nv_ref[...] = _adamw(w_ref[...], g, m_ref[...], v_ref[...])

    blk = pl.BlockSpec((tr, c), lambda i: (i, 0))
    return pl.pallas_call(
        body, name=name, grid=(r // tr,),
        in_specs=[pl.BlockSpec((n, tr, c), lambda i: (0, i, 0)), blk, blk, blk],
        out_specs=[blk] * 4, out_shape=[jax.ShapeDtypeStruct((r, c), f32)] * 4,
        compiler_params=_cparams("parallel"),
    )(pieces, w, m, v)


def adam_w_mod(cond_t, dm, w, m, v):
    nl, d, cols = w.shape
    tr = _tile(d, (512, 256, 128))

    def body(c_ref, dm_ref, w_ref, m_ref, v_ref, g_ref, d_ref, nm_ref, nv_ref):
        g = jnp.dot(c_ref[...], dm_ref[...], preferred_element_type=f32, precision=lax.Precision.HIGHEST)
        g_ref[...] = g
        d_ref[...], nm_ref[...], nv_ref[...] = _adamw(w_ref[...], g, m_ref[...], v_ref[...])

    blk = pl.BlockSpec((None, tr, cols), lambda l, i: (l, i, 0))
    return pl.pallas_call(
        body, name="adam_w_mod", grid=(nl, d // tr),
        in_specs=[pl.BlockSpec((tr, N_DEV), lambda l, i: (i, 0)), pl.BlockSpec((None, N_DEV, cols), lambda l, i: (l, 0, 0)),
                  blk, blk, blk],
        out_specs=[blk] * 4, out_shape=[jax.ShapeDtypeStruct((nl, d, cols), f32)] * 4,
        compiler_params=_cparams("parallel", "parallel"),
    )(cond_t, dm, w, m, v)


def silu_rows(c_all):
    def body(c_ref, o_ref):
        o_ref[...] = _silu(c_ref[...])

    return pl.pallas_call(body, name="silu_rows", out_shape=jax.ShapeDtypeStruct(c_all.shape, f32))(c_all)


def _block_diag(x):
    g, a, b = x.shape
    nj = g // GROUPS_PER_LANE_BLOCK
    eye = jnp.eye(GROUPS_PER_LANE_BLOCK, dtype=x.dtype)
    x5 = x.reshape(nj, GROUPS_PER_LANE_BLOCK, a, b)
    return jnp.einsum("jgab,gh->jgahb", x5, eye).reshape(nj, GROUPS_PER_LANE_BLOCK * a, GROUPS_PER_LANE_BLOCK * b)


def _diag_blocks(x, a, b):
    nj = x.shape[0]
    x5 = x.reshape(nj, GROUPS_PER_LANE_BLOCK, a, GROUPS_PER_LANE_BLOCK, b)
    eye = jnp.eye(GROUPS_PER_LANE_BLOCK, dtype=x.dtype)
    return jnp.einsum("jgahb,gh->jgab", x5, eye).reshape(nj * GROUPS_PER_LANE_BLOCK, a, b)


PACK_ROW = SUBLANES * HEAD


def _pack(parts, row_multiple=SUBLANES):
    rows = []
    for p in parts:
        flat = p.reshape(-1)
        pad = (-flat.shape[0]) % PACK_ROW
        if pad:
            flat = jnp.concatenate([flat, jnp.zeros((pad,), flat.dtype)])
        rows.append(flat.reshape(-1, HEAD))
    pad = (-sum(r.shape[0] for r in rows)) % row_multiple
    if pad:
        rows.append(jnp.zeros((pad, HEAD), rows[0].dtype))
    return jnp.concatenate(rows, axis=0)


def _unpack(packed, shapes):
    out, r0 = [], 0
    for shp in shapes:
        n = math.prod(shp)
        nr = -(-n // PACK_ROW) * SUBLANES
        out.append(packed[r0:r0 + nr].reshape(-1)[:n].reshape(shp))
        r0 += nr
    return out


def adam_small(g, w, m, v):
    r, c = w.shape

    def body(g_ref, w_ref, m_ref, v_ref, d_ref, nm_ref, nv_ref):
        d_ref[...], nm_ref[...], nv_ref[...] = _adamw(w_ref[...], g_ref[...], m_ref[...], v_ref[...])

    tr = max(t for t in range(SUBLANES, 1024 + 1, SUBLANES) if r % t == 0)
    blk = pl.BlockSpec((tr, c), lambda i: (i, 0))
    return pl.pallas_call(
        body, name="adam_small", grid=(r // tr,),
        in_specs=[blk] * 4, out_specs=[blk] * 3, out_shape=[jax.ShapeDtypeStruct((r, c), f32)] * 3,
        compiler_params=_cparams("parallel"),
    )(g, w, m, v)


def kernel(x, c, ln_pre_g, ln_post_g, w_mod, b_mod, w_in_ab, w_out_ab, sgu_norm_g, sgu_w, sgu_b, w_in_ssm, w_out_ssm, lam_re, lam_im, b_re, b_im, c_re, c_im, d_skip, log_dt, w_glu, b_glu, loss_target, m_ln_pre_g, m_ln_post_g, m_w_mod, m_b_mod, m_w_in_ab, m_w_out_ab, m_sgu_norm_g, m_sgu_w, m_sgu_b, m_w_in_ssm, m_w_out_ssm, m_lam_re, m_lam_im, m_b_re, m_b_im, m_c_re, m_c_im, m_d_skip, m_log_dt, m_w_glu, m_b_glu, v_ln_pre_g, v_ln_post_g, v_w_mod, v_b_mod, v_w_in_ab, v_w_out_ab, v_sgu_norm_g, v_sgu_w, v_sgu_b, v_w_in_ssm, v_w_out_ssm, v_lam_re, v_lam_im, v_b_re, v_b_im, v_c_re, v_c_im, v_d_skip, v_log_dt, v_w_glu, v_b_glu):
    me = _my_index()
    x0 = x[0]
    l, d = x0.shape
    target = loss_target[0]
    nh = sgu_w.shape[1]
    wa = nh * HEAD
    n_grp, n_st = lam_re.shape[1], lam_re.shape[2]
    mod_cols = w_mod.shape[2]

    def after(a, first):
        return a + jnp.minimum(jnp.abs(first[(0,) * first.ndim].astype(f32)), 0.0).astype(a.dtype)

    c_all, d_skip_all, b_glu_all = all_gather([c, d_skip, b_glu], "gather_c")
    c_all = c_all.reshape(N_DEV, d)
    d_skip_all = d_skip_all.reshape(1, -1)
    b_glu_all = b_glu_all.reshape(1, -1)

    b_cols = lax.dynamic_slice_in_dim(b_mod, me * mod_cols, mod_cols, axis=1)
    (mod_all,) = all_gather([mod_part(c_all, w_mod, b_cols)], "gather_mod")
    (win_ab3,) = sequencer_exchange(GATHER, [after(w_in_ab[0], mod_all).astype(bf16)], "gather_w_in", 1)
    mod_mine = lax.dynamic_index_in_dim(mod_all, me, axis=2, keepdims=False)
    mod_rows = jnp.transpose(mod_mine, (1, 0, 2)).reshape(2, 3, 1, d)

    def rows(a, i):
        return a[i].reshape(1, d)

    shift0, scale0, gate0 = mod_rows[0, 0], mod_rows[0, 1], mod_rows[0, 2]
    h0, h0_t = prenorm_fwd(x0, rows(ln_pre_g, 0), shift0, scale0, "prenorm0")
    wout_ab3, win_ssm3, wout_ssm3, wglu = sequencer_exchange(
        GATHER, [after(w, win_ab3).astype(bf16) for w in (w_out_ab[0], w_in_ssm[0], w_out_ssm[0], w_glu[0])],
        "gather_w_rest", 2)
    proj0 = mm_nn(h0, win_ab3, bf16, "proj0")
    sgu_b3 = sgu_b[0].reshape(nh, HEAD, 1)
    cat, att, tot = sb_fwd(proj0, sgu_fwd(proj0, sgu_norm_g, sgu_w[0], sgu_b3), nh)
    wout_ab3 = wout_ab3.reshape(1, d, d)
    win_ssm3 = win_ssm3.reshape(1, d, d)
    wglu = wglu.reshape(w_glu.shape[2], w_glu.shape[2])
    y0 = mm_nn(cat, wout_ab3, f32, "out0")

    shift1, scale1, gate1 = mod_rows[1, 0], mod_rows[1, 1], mod_rows[1, 2]
    x1, h1, h1_t = post_prenorm_fwd(x0, y0, gate0, rows(ln_post_g, 0), rows(ln_pre_g, 1), shift1, scale1,
                                    "post0_prenorm1")
    proj1 = mm_nn(h1, win_ssm3, bf16, "proj1")
    w_ssm = proj1.shape[1] // 2
    ldt = log_dt[0].reshape(n_grp, 1)
    bt_re = jnp.transpose(b_re[0], (0, 2, 1))
    bt_im = jnp.transpose(b_im[0], (0, 2, 1))
    a_re, a_im, bbt_re, bbt_im = s5_params_fwd(lam_re[0], lam_im[0], ldt, bt_re, bt_im)
    bre3 = _block_diag(bbt_re).astype(bf16)
    bim3 = _block_diag(bbt_im).astype(bf16)
    cre3 = _block_diag(jnp.transpose(c_re[0], (0, 2, 1))).astype(bf16)
    cimn3 = _block_diag(-jnp.transpose(c_im[0], (0, 2, 1))).astype(bf16)
    a_re_row, a_im_row = a_re.reshape(1, -1), a_im.reshape(1, -1)
    y_ssm, hs_re, hs_im = ssm_fwd(proj1, bre3, bim3, cre3, cimn3, a_re_row, a_im_row, d_skip_all)
    g_act, t_glu, mix1 = glu_fwd(y_ssm, proj1, wglu, b_glu_all)
    y1 = mm_nn(mix1, wout_ssm3, f32, "out1")

    dx2, loss_tile, dy1, dgate1, dgpost1 = final_loss(x1, y1, gate1, rows(ln_post_g, 1), target)

    dmix1 = mm_nt(dy1, wout_ssm3, f32, "dmix1")
    gw_out_ssm = mm_tn(mix1, dy1, N_DEV, bf16, "gw_out_ssm")
    (p_out_ssm,) = sequencer_exchange(SCATTER, [gw_out_ssm], "scatter_g1", 3)
    dy_ssm, dproj1, dt_glu, db_glu = glu_bwd(dmix1, y_ssm, t_glu, proj1, wglu)
    gw_glu = mm_tn(g_act, dt_glu, 1, bf16, "gw_glu").reshape(N_DEV, -1, w_ssm)
    dproj1, dd_skip, da_re, da_im, dbre3, dbim3, dcre3, dcimn3 = ssm_bwd(
        dy_ssm, proj1, dproj1, hs_re, hs_im, bre3, bim3, cre3, cimn3, a_re_row, a_im_row, d_skip_all)
    gw_in_ssm = mm_nn(h1_t, dproj1[None], bf16, "gw_in_ssm").reshape(N_DEV, -1, proj1.shape[1])
    p_in_ssm, p_glu = sequencer_exchange(SCATTER, [gw_in_ssm, gw_glu], "scatter_g2", 4)
    dh1 = mm_nt(dproj1, win_ssm3, f32, "dh1")
    dx1, dshift1, dscale1, dgpre1, dy0, dgate0, dgpost0 = prenorm_post_bwd(
        dh1, x1, dx2, rows(ln_pre_g, 1), scale1, y0, gate0, rows(ln_post_g, 0), "prenorm1_post0_bwd")
    dlr, dli, dldt, dbt_re, dbt_im = s5_params_bwd(
        lam_re[0], lam_im[0], ldt, bt_re, bt_im, da_re.reshape(n_grp, n_st), da_im.reshape(n_grp, n_st),
        _diag_blocks(dbre3, SSM_GROUP, n_st), _diag_blocks(dbim3, SSM_GROUP, n_st))
    g_b_re = jnp.transpose(dbt_re, (0, 2, 1))
    g_b_im = jnp.transpose(dbt_im, (0, 2, 1))
    g_c_re = jnp.transpose(_diag_blocks(dcre3, n_st, SSM_GROUP), (0, 2, 1))
    g_c_im = -jnp.transpose(_diag_blocks(dcimn3, n_st, SSM_GROUP), (0, 2, 1))

    dcat = mm_nt(dy0, wout_ab3, f32, "dcat")
    gw_out_ab = mm_tn(cat, dy0, 1, bf16, "gw_out_ab").reshape(N_DEV, -1, d)
    (p_out_ab,) = sequencer_exchange(SCATTER, [gw_out_ab], "scatter_g3", 5)
    dproj0, dsgu_w, dsgu_b, dsgu_ng = sgu_bwd(proj0, dcat, sgu_norm_g, sgu_w[0], sgu_b3)
    dproj0 = sb_bwd(proj0, dcat, att, tot, dproj0, nh)
    gw_in_ab = mm_nn(h0_t, dproj0[None], bf16, "gw_in_ab", split_cols=N_DEV)
    (p_in_ab,) = sequencer_exchange(SCATTER, [gw_in_ab], "scatter_g4", 6)
    dh0 = mm_nt(dproj0, win_ab3, f32, "dh0")
    dx0, dshift0, dscale0, dgpre0 = prenorm_bwd(dh0, x0, dx1, rows(ln_pre_g, 0), scale0, "prenorm0_bwd")

    small_names = ["ln_pre_g", "ln_post_g", "b_mod", "sgu_norm_g", "sgu_w", "sgu_b", "lam_re", "lam_im", "b_re", "b_im",
                   "c_re", "c_im", "log_dt"]
    small_w = [ln_pre_g, ln_post_g, b_mod, sgu_norm_g, sgu_w, sgu_b, lam_re, lam_im, b_re, b_im, c_re, c_im, log_dt]
    small_m = [m_ln_pre_g, m_ln_post_g, m_b_mod, m_sgu_norm_g, m_sgu_w, m_sgu_b, m_lam_re, m_lam_im, m_b_re, m_b_im,
               m_c_re, m_c_im, m_log_dt]
    small_v = [v_ln_pre_g, v_ln_post_g, v_b_mod, v_sgu_norm_g, v_sgu_w, v_sgu_b, v_lam_re, v_lam_im, v_b_re, v_b_im,
               v_c_re, v_c_im, v_log_dt]
    def sharded(p, w, m, v, name):
        shp = w.shape
        w2, m2, v2 = (a.reshape(-1, shp[-1]) for a in (w, m, v))
        return [o.reshape(shp) for o in adam_reduce(p.reshape(p.shape[0], -1, shp[-1]), w2, m2, v2, name)]

    r_w_out_ssm = sharded(p_out_ssm, w_out_ssm, m_w_out_ssm, v_w_out_ssm, "adam_w_out_ssm")
    r_w_in_ssm = sharded(p_in_ssm, w_in_ssm, m_w_in_ssm, v_w_in_ssm, "adam_w_in_ssm")
    r_w_glu = sharded(p_glu, w_glu, m_w_glu, v_w_glu, "adam_w_glu")
    r_w_out_ab = sharded(p_out_ab, w_out_ab, m_w_out_ab, v_w_out_ab, "adam_w_out_ab")
    dmod = jnp.concatenate([dshift0, dscale0, dgate0, dshift1, dscale1, dgate1], axis=1)
    for done in (r_w_out_ssm, r_w_in_ssm, r_w_glu, r_w_out_ab):
        dmod = after(dmod, done[0])
    small_g = [jnp.concatenate([dgpre0, dgpre1]), jnp.concatenate([dgpost0, dgpost1]), dmod, dsgu_ng, dsgu_w, dsgu_b,
               dlr, dli, g_b_re, g_b_im, g_c_re, g_c_im, dldt]
    shapes = [w.shape for w in small_w]
    g_sum, dmod_all = all_reduce_rows(_pack(small_g + [dd_skip, db_glu, loss_tile], SUBLANES * N_DEV), dmod,
                                      "reduce_small_grads")
    n_rows_small = sum(-(-math.prod(s) // PACK_ROW) * SUBLANES for s in shapes)
    loss = g_sum[n_rows_small + 2 * (d_skip_all.shape[1] // HEAD), 0] * (0.5 / d)
    new_small = adam_small(g_sum, _pack(small_w), _pack(small_m), _pack(small_v))
    r_small = [_unpack(o, shapes) for o in [g_sum[:n_rows_small]] + list(new_small)]
    small = {n: [r_small[k][i] for k in range(4)] for i, n in enumerate(small_names)}
    vec_rows = d_skip_all.shape[1] // HEAD

    def my_columns(r0):
        whole = g_sum[r0:r0 + vec_rows].reshape(1, 1, -1)
        return lax.dynamic_slice_in_dim(whole, me * d_skip.shape[1], d_skip.shape[1], axis=2)

    r_d_skip = sharded(my_columns(n_rows_small), d_skip, m_d_skip, v_d_skip, "adam_d_skip")
    r_b_glu = sharded(my_columns(n_rows_small + vec_rows), b_glu, m_b_glu, v_b_glu, "adam_b_glu")
    r_w_in_ab = sharded(p_in_ab, w_in_ab, m_w_in_ab, v_w_in_ab, "adam_w_in_ab")

    dm_cols = jnp.transpose(
        lax.dynamic_slice_in_dim(dmod_all.reshape(N_DEV, 2, 3 * d), me * mod_cols, mod_cols, axis=2), (1, 0, 2))
    cond_t = jnp.transpose(silu_rows(c_all))
    r_w_mod = adam_w_mod(cond_t, dm_cols, w_mod, m_w_mod, v_w_mod)

    res = dict(small)
    res.update(w_mod=r_w_mod, w_in_ab=r_w_in_ab, w_out_ab=r_w_out_ab, w_in_ssm=r_w_in_ssm, w_out_ssm=r_w_out_ssm,
               d_skip=r_d_skip, w_glu=r_w_glu, b_glu=r_b_glu)
    order = ["ln_pre_g", "ln_post_g", "w_mod", "b_mod", "w_in_ab", "w_out_ab", "sgu_norm_g", "sgu_w", "sgu_b", "w_in_ssm",
             "w_out_ssm", "lam_re", "lam_im", "b_re", "b_im", "c_re", "c_im", "d_skip", "log_dt", "w_glu", "b_glu"]
    outs = [loss, dx0.reshape(x.shape)]
    for k in range(4):
        outs += [res[n][k] for n in order]
    return tuple(outs)
```
